```python
import jax, jax.numpy as jnp
from jax import lax
import numpy as np

D_MODEL = 1024
BATCH = 8
SEQ = 4096
DEPTH = 2

QBLOCK = 128
WINDOW = 128
EPS = 1e-6
ROPE_THETA = 10000.0
MLA_HEADS = D_MODEL // 128
MLA_Q_RANK = D_MODEL // 4
MLA_KV_RANK = D_MODEL // 8
MLA_NOPE = 64
MLA_ROPE = 32
MLA_V = 64
SWA_HEADS = D_MODEL // 128
SWA_KV_HEADS = SWA_HEADS // 4
SWA_HEAD_DIM = 64
FOX_HEADS = D_MODEL // 64
FOX_HEAD_DIM = 64
FOX_WIDTH = FOX_HEADS * FOX_HEAD_DIM
FORGET_BIAS_INIT = 2.0

EVEN_WIDTH = MLA_HEADS * MLA_V + SWA_HEADS * SWA_HEAD_DIM
EVEN_SPLITS = (MLA_Q_RANK, MLA_KV_RANK, MLA_ROPE, SWA_HEADS * SWA_HEAD_DIM,
               SWA_KV_HEADS * SWA_HEAD_DIM, SWA_KV_HEADS * SWA_HEAD_DIM, EVEN_WIDTH)
ODD_SPLITS = (FOX_WIDTH, FOX_WIDTH, FOX_WIDTH, FOX_HEADS, FOX_WIDTH)
N_EVEN = (DEPTH + 1) // 2
N_ODD = DEPTH // 2

kernel_name = "hybrid_mla_swa_fox_gated"


def rms_norm(x, g):
    xf = x.astype(jnp.float32)
    y = xf * lax.rsqrt(jnp.mean(xf * xf, axis=-1, keepdims=True) + EPS)
    return (y * g.astype(jnp.float32)).astype(x.dtype)


def split_cols(z, sizes):
    idx = [int(v) for v in np.cumsum(sizes)[:-1]]
    return jnp.split(z, idx, axis=-1)


def rope_angles(positions, dim):
    inv_freq = 1.0 / (ROPE_THETA ** (jnp.arange(0, dim, 2, dtype=jnp.float32) / dim))
    return positions.astype(jnp.float32)[..., None] * inv_freq


def apply_rope(x, ang):
    cos, sin = jnp.cos(ang), jnp.sin(ang)
    xf = x.astype(jnp.float32)
    x1, x2 = jnp.split(xf, 2, axis=-1)
    return jnp.concatenate([x1 * cos - x2 * sin, x2 * cos + x1 * sin], axis=-1).astype(x.dtype)


def alibi_slopes(n):
    return 2.0 ** (-8.0 * (jnp.arange(n, dtype=jnp.float32) + 1.0) / n)


def causal_block_attention(q, k, v, log_cum=None):
    B, S, H, Dk = q.shape
    Dv = v.shape[-1]
    nb = S // QBLOCK
    scale = Dk ** -0.5
    kpos = jnp.arange(S)
    lc_t = None if log_cum is None else jnp.transpose(log_cum, (0, 2, 1))

    def one_block(i):
        start = i * QBLOCK
        qi = lax.dynamic_slice_in_dim(q, start, QBLOCK, axis=1)
        s = jnp.einsum('bqhd,bkhd->bhqk', qi, k, preferred_element_type=jnp.float32) * scale
        if lc_t is not None:
            ci = lax.dynamic_slice_in_dim(lc_t, start, QBLOCK, axis=2)
            s = s + ci[..., :, None] - lc_t[..., None, :]
        qpos = start + jnp.arange(QBLOCK)
        mask = kpos[None, :] <= qpos[:, None]
        s = jnp.where(mask, s, -jnp.inf)
        p = jax.nn.softmax(s, axis=-1)
        return jnp.einsum('bhqk,bkhd->bqhd', p.astype(v.dtype), v)

    out = lax.map(one_block, jnp.arange(nb))
    return jnp.transpose(out, (1, 0, 2, 3, 4)).reshape(B, S, H, Dv)


def sliding_window_sink_attention(q, k, v, sinks, slopes):
    B, S, H, D = q.shape
    KV = k.shape[2]
    G = H // KV
    W = WINDOW
    nb = S // W
    qb = q.reshape(B, nb, W, KV, G, D)
    pad = ((0, 0), (W, 0), (0, 0), (0, 0))
    kp = jnp.pad(k, pad).reshape(B, nb + 1, W, KV, D)
    vp = jnp.pad(v, pad).reshape(B, nb + 1, W, KV, D)
    kb = jnp.concatenate([kp[:, :-1], kp[:, 1:]], axis=2)
    vb = jnp.concatenate([vp[:, :-1], vp[:, 1:]], axis=2)
    s = jnp.einsum('bnqkgd,bnckd->bnkgqc', qb, kb, preferred_element_type=jnp.float32) * (D ** -0.5)
    a = jnp.arange(W)[:, None]
    c = jnp.arange(2 * W)[None, :]
    dist = (W + a - c).astype(jnp.float32)
    blk = jnp.arange(nb)[:, None, None]
    valid = (dist >= 0) & (dist < W) & (blk * W + c[None] - W >= 0)
    s = s - slopes.reshape(KV, G)[:, :, None, None] * dist
    s = jnp.where(valid[None, :, None, None], s, -jnp.inf)
    sink = sinks.astype(jnp.float32).reshape(KV, G)[:, :, None, None]
    m = jnp.maximum(jnp.max(s, axis=-1, keepdims=True), sink)
    p = jnp.exp(s - m)
    p = p / (jnp.sum(p, axis=-1, keepdims=True) + jnp.exp(sink - m))
    o = jnp.einsum('bnkgqc,bnckd->bnqkgd', p.astype(v.dtype), vb)
    return o.reshape(B, S, H, D)


def mla_swa_layer(x, positions, g_in, w_in, g_q_a, w_q_up, g_kv_a, w_kv_up, sinks, w_out):
    B, S, _ = x.shape
    h = rms_norm(x, g_in)
    z = h @ w_in
    cq, ckv, kpe, q_s, k_s, v_s, gate = split_cols(z, EVEN_SPLITS)
    q = (rms_norm(cq, g_q_a) @ w_q_up).reshape(B, S, MLA_HEADS, MLA_NOPE + MLA_ROPE)
    q_nope, q_pe = q[..., :MLA_NOPE], q[..., MLA_NOPE:]
    kv = (rms_norm(ckv, g_kv_a) @ w_kv_up).reshape(B, S, MLA_HEADS, MLA_NOPE + MLA_V)
    k_nope, v_m = kv[..., :MLA_NOPE], kv[..., MLA_NOPE:]
    ang = rope_angles(positions, MLA_ROPE)
    q_pe = apply_rope(q_pe, ang[:, :, None, :])
    k_pe = apply_rope(kpe, ang)[:, :, None, :]
    qm = jnp.concatenate([q_nope, q_pe], axis=-1)
    km = jnp.concatenate([k_nope, jnp.broadcast_to(k_pe, (B, S, MLA_HEADS, MLA_ROPE))], axis=-1)
    o_mla = causal_block_attention(qm, km, v_m)
    o_swa = sliding_window_sink_attention(
        q_s.reshape(B, S, SWA_HEADS, SWA_HEAD_DIM),
        k_s.reshape(B, S, SWA_KV_HEADS, SWA_HEAD_DIM),
        v_s.reshape(B, S, SWA_KV_HEADS, SWA_HEAD_DIM),
        sinks, alibi_slopes(SWA_HEADS))
    o = jnp.concatenate([o_mla.reshape(B, S, -1), o_swa.reshape(B, S, -1)], axis=-1)
    return x + (o * jax.nn.silu(gate)) @ w_out


def fox_layer(x, g_in, w_in, b_f, w_out):
    B, S, _ = x.shape
    h = rms_norm(x, g_in)
    z = h @ w_in
    q, k, v, f_logit, gate = split_cols(z, ODD_SPLITS)
    log_f = jax.nn.log_sigmoid(f_logit.astype(jnp.float32) + b_f.astype(jnp.float32))
    log_cum = jnp.cumsum(log_f, axis=1)
    o = causal_block_attention(q.reshape(B, S, FOX_HEADS, FOX_HEAD_DIM),
                               k.reshape(B, S, FOX_HEADS, FOX_HEAD_DIM),
                               v.reshape(B, S, FOX_HEADS, FOX_HEAD_DIM), log_cum=log_cum)
    return x + (o.reshape(B, S, FOX_WIDTH) * jax.nn.silu(gate)) @ w_out


def _fwd_setup_inputs(seed: int = 0) -> dict:
    key = jax.random.key(seed)
    ks = jax.random.split(key, 16)
    f32 = jnp.float32

    def w(k, shape, fan_in):
        return jax.random.normal(k, shape, f32) * (fan_in ** -0.5)

    def gain(k, shape):
        return 1.0 + 0.05 * jax.random.normal(k, shape, f32)

    x = jax.random.normal(ks[0], (BATCH, SEQ, D_MODEL), f32)
    positions = jnp.broadcast_to(jnp.arange(SEQ, dtype=jnp.int32), (BATCH, SEQ))
    return {
        "x": x,
        "positions": positions,
        "e_g_in": gain(ks[1], (N_EVEN, D_MODEL)),
        "e_w_in": w(ks[2], (N_EVEN, D_MODEL, sum(EVEN_SPLITS)), D_MODEL),
        "e_g_q_a": gain(ks[3], (N_EVEN, MLA_Q_RANK)),
        "e_w_q_up": w(ks[4], (N_EVEN, MLA_Q_RANK, MLA_HEADS * (MLA_NOPE + MLA_ROPE)), MLA_Q_RANK),
        "e_g_kv_a": gain(ks[5], (N_EVEN, MLA_KV_RANK)),
        "e_w_kv_up": w(ks[6], (N_EVEN, MLA_KV_RANK, MLA_HEADS * (MLA_NOPE + MLA_V)), MLA_KV_RANK),
        "e_sinks": jax.random.normal(ks[7], (N_EVEN, SWA_HEADS), f32),
        "e_w_out": w(ks[8], (N_EVEN, EVEN_WIDTH, D_MODEL), EVEN_WIDTH),
        "o_g_in": gain(ks[9], (N_ODD, D_MODEL)),
        "o_w_in": w(ks[10], (N_ODD, D_MODEL, sum(ODD_SPLITS)), D_MODEL),
        "o_b_f": FORGET_BIAS_INIT + 0.5 * jax.random.normal(ks[11], (N_ODD, FOX_HEADS), f32),
        "o_w_out": w(ks[12], (N_ODD, FOX_WIDTH, D_MODEL), FOX_WIDTH),
        "g_final": gain(ks[13], (D_MODEL,)),
    }


def _fwd_reference(x, positions, e_g_in, e_w_in, e_g_q_a, e_w_q_up, e_g_kv_a, e_w_kv_up, e_sinks,
              e_w_out, o_g_in, o_w_in, o_b_f, o_w_out, g_final):
    for layer in range(DEPTH):
        j = layer // 2
        if layer % 2 == 0:
            x = mla_swa_layer(x, positions, e_g_in[j], e_w_in[j], e_g_q_a[j], e_w_q_up[j],
                              e_g_kv_a[j], e_w_kv_up[j], e_sinks[j], e_w_out[j])
        else:
            x = fox_layer(x, o_g_in[j], o_w_in[j], o_b_f[j], o_w_out[j])
    return rms_norm(x, g_final)


import jax as _jax
import jax.numpy as _jnp

TWIN_FORMAT = 'train_step'
FWD_PARAMS = ['x', 'positions', 'e_g_in', 'e_w_in', 'e_g_q_a', 'e_w_q_up', 'e_g_kv_a', 'e_w_kv_up', 'e_sinks', 'e_w_out', 'o_g_in', 'o_w_in', 'o_b_f', 'o_w_out', 'g_final']
TWIN_WEIGHTS = ['e_g_in', 'e_w_in', 'e_g_q_a', 'e_w_q_up', 'e_g_kv_a', 'e_w_kv_up', 'e_sinks', 'e_w_out', 'o_g_in', 'o_w_in', 'o_b_f', 'o_w_out', 'g_final']
TWIN_DIFF_INPUT = 'x'
TWIN_INPUTS = ['x', 'positions', 'e_g_in', 'e_w_in', 'e_g_q_a', 'e_w_q_up', 'e_g_kv_a', 'e_w_kv_up', 'e_sinks', 'e_w_out', 'o_g_in', 'o_w_in', 'o_b_f', 'o_w_out', 'g_final', 'loss_target', 'm_e_g_in', 'm_e_w_in', 'm_e_g_q_a', 'm_e_w_q_up', 'm_e_g_kv_a', 'm_e_w_kv_up', 'm_e_sinks', 'm_e_w_out', 'm_o_g_in', 'm_o_w_in', 'm_o_b_f', 'm_o_w_out', 'm_g_final', 'v_e_g_in', 'v_e_w_in', 'v_e_g_q_a', 'v_e_w_q_up', 'v_e_g_kv_a', 'v_e_w_kv_up', 'v_e_sinks', 'v_e_w_out', 'v_o_g_in', 'v_o_w_in', 'v_o_b_f', 'v_o_w_out', 'v_g_final']
TWIN_OUTPUTS = ['loss', 'grad_x', 'grad_e_g_in', 'grad_e_w_in', 'grad_e_g_q_a', 'grad_e_w_q_up', 'grad_e_g_kv_a', 'grad_e_w_kv_up', 'grad_e_sinks', 'grad_e_w_out', 'grad_o_g_in', 'grad_o_w_in', 'grad_o_b_f', 'grad_o_w_out', 'grad_g_final', 'delta_e_g_in', 'delta_e_w_in', 'delta_e_g_q_a', 'delta_e_w_q_up', 'delta_e_g_kv_a', 'delta_e_w_kv_up', 'delta_e_sinks', 'delta_e_w_out', 'delta_o_g_in', 'delta_o_w_in', 'delta_o_b_f', 'delta_o_w_out', 'delta_g_final', 'new_m_e_g_in', 'new_m_e_w_in', 'new_m_e_g_q_a', 'new_m_e_w_q_up', 'new_m_e_g_kv_a', 'new_m_e_w_kv_up', 'new_m_e_sinks', 'new_m_e_w_out', 'new_m_o_g_in', 'new_m_o_w_in', 'new_m_o_b_f', 'new_m_o_w_out', 'new_m_g_final', 'new_v_e_g_in', 'new_v_e_w_in', 'new_v_e_g_q_a', 'new_v_e_w_q_up', 'new_v_e_g_kv_a', 'new_v_e_w_kv_up', 'new_v_e_sinks', 'new_v_e_w_out', 'new_v_o_g_in', 'new_v_o_w_in', 'new_v_o_b_f', 'new_v_o_w_out', 'new_v_g_final']
TWIN_LEAF_KINDS = {'loss': 'loss', 'grad_x': 'grad_x', 'grad_e_g_in': 'grad_w', 'grad_e_w_in': 'grad_w', 'grad_e_g_q_a': 'grad_w', 'grad_e_w_q_up': 'grad_w', 'grad_e_g_kv_a': 'grad_w', 'grad_e_w_kv_up': 'grad_w', 'grad_e_sinks': 'grad_w', 'grad_e_w_out': 'grad_w', 'grad_o_g_in': 'grad_w', 'grad_o_w_in': 'grad_w', 'grad_o_b_f': 'grad_w', 'grad_o_w_out': 'grad_w', 'grad_g_final': 'grad_w', 'delta_e_g_in': 'delta_w', 'delta_e_w_in': 'delta_w', 'delta_e_g_q_a': 'delta_w', 'delta_e_w_q_up': 'delta_w', 'delta_e_g_kv_a': 'delta_w', 'delta_e_w_kv_up': 'delta_w', 'delta_e_sinks': 'delta_w', 'delta_e_w_out': 'delta_w', 'delta_o_g_in': 'delta_w', 'delta_o_w_in': 'delta_w', 'delta_o_b_f': 'delta_w', 'delta_o_w_out': 'delta_w', 'delta_g_final': 'delta_w', 'new_m_e_g_in': 'new_m', 'new_m_e_w_in': 'new_m', 'new_m_e_g_q_a': 'new_m', 'new_m_e_w_q_up': 'new_m', 'new_m_e_g_kv_a': 'new_m', 'new_m_e_w_kv_up': 'new_m', 'new_m_e_sinks': 'new_m', 'new_m_e_w_out': 'new_m', 'new_m_o_g_in': 'new_m', 'new_m_o_w_in': 'new_m', 'new_m_o_b_f': 'new_m', 'new_m_o_w_out': 'new_m', 'new_m_g_final': 'new_m', 'new_v_e_g_in': 'new_v', 'new_v_e_w_in': 'new_v', 'new_v_e_g_q_a': 'new_v', 'new_v_e_w_q_up': 'new_v', 'new_v_e_g_kv_a': 'new_v', 'new_v_e_w_kv_up': 'new_v', 'new_v_e_sinks': 'new_v', 'new_v_e_w_out': 'new_v', 'new_v_o_g_in': 'new_v', 'new_v_o_w_in': 'new_v', 'new_v_o_b_f': 'new_v', 'new_v_o_w_out': 'new_v', 'new_v_g_final': 'new_v'}


def _forward(args):
    return _fwd_reference(*[args[k] for k in FWD_PARAMS])


def _output_shape():
    out = _jax.eval_shape(lambda: _forward(_fwd_setup_inputs(0)))
    return out.shape, out.dtype

N_MICROBATCH = 1
ADAM_LR = 0.001
ADAM_B1 = 0.9
ADAM_B2 = 0.999
ADAM_EPS = 1e-08
ADAM_WD = 0.01
ADAM_STEP = 10
PER_EXAMPLE_BATCH_AXIS = {'x': 0, 'positions': 0, 'loss_target': 0}
SHARED_INPUTS = []
_WEIGHT_DTYPES = {'e_g_in': _jnp.float32, 'e_w_in': _jnp.float32, 'e_g_q_a': _jnp.float32, 'e_w_q_up': _jnp.float32, 'e_g_kv_a': _jnp.float32, 'e_w_kv_up': _jnp.float32, 'e_sinks': _jnp.float32, 'e_w_out': _jnp.float32, 'o_g_in': _jnp.float32, 'o_w_in': _jnp.float32, 'o_b_f': _jnp.float32, 'o_w_out': _jnp.float32, 'g_final': _jnp.float32}
MOMENT_SCALE = {'e_g_in': 5.666114e-02, 'e_w_in': 3.766790e-02, 'e_g_q_a': 3.059433e-02, 'e_w_q_up': 1.707080e-02, 'e_g_kv_a': 6.573533e-02, 'e_w_kv_up': 2.167790e-02, 'e_sinks': 7.064196e-02, 'e_w_out': 3.083066e-02, 'o_g_in': 7.571011e-02, 'o_w_in': 3.812472e-02, 'o_b_f': 3.221238e-01, 'o_w_out': 4.247714e-02, 'g_final': 3.202406e+01}


def _to_microbatches(a, axis):
    t = _jnp.moveaxis(a, axis, 0)
    t = t.reshape((N_MICROBATCH, t.shape[0] // N_MICROBATCH) + t.shape[1:])
    return _jnp.moveaxis(t, 1, axis + 1)


def setup_inputs(seed: int = 0) -> dict:
    inp = _fwd_setup_inputs(seed)
    key = _jax.random.fold_in(_jax.random.key(seed), 7919)
    shape, _ = _output_shape()
    out = dict(inp)
    out["loss_target"] = _jax.random.normal(_jax.random.fold_in(key, 0), shape, _jnp.float32)
    for i, name in enumerate(TWIN_WEIGHTS):
        w = inp[name].astype(_jnp.float32)
        if MOMENT_SCALE is None:
            s = _jnp.sqrt(_jnp.mean(_jnp.square(w)) + 1e-30)
        else:
            s = MOMENT_SCALE[name]
        km, kv = _jax.random.split(_jax.random.fold_in(key, i + 1))
        out[name] = w
        out["m_" + name] = s * _jax.random.normal(km, w.shape, _jnp.float32)
        out["v_" + name] = (s * s) * _jax.random.uniform(kv, w.shape, _jnp.float32, 0.5, 1.5)
    if N_MICROBATCH > 1:
        for name, axis in PER_EXAMPLE_BATCH_AXIS.items():
            out[name] = _to_microbatches(out[name], axis)
    return {'x': out['x'], 'positions': out['positions'], 'e_g_in': out['e_g_in'], 'e_w_in': out['e_w_in'], 'e_g_q_a': out['e_g_q_a'], 'e_w_q_up': out['e_w_q_up'], 'e_g_kv_a': out['e_g_kv_a'], 'e_w_kv_up': out['e_w_kv_up'], 'e_sinks': out['e_sinks'], 'e_w_out': out['e_w_out'], 'o_g_in': out['o_g_in'], 'o_w_in': out['o_w_in'], 'o_b_f': out['o_b_f'], 'o_w_out': out['o_w_out'], 'g_final': out['g_final'], 'loss_target': out['loss_target'], 'm_e_g_in': out['m_e_g_in'], 'm_e_w_in': out['m_e_w_in'], 'm_e_g_q_a': out['m_e_g_q_a'], 'm_e_w_q_up': out['m_e_w_q_up'], 'm_e_g_kv_a': out['m_e_g_kv_a'], 'm_e_w_kv_up': out['m_e_w_kv_up'], 'm_e_sinks': out['m_e_sinks'], 'm_e_w_out': out['m_e_w_out'], 'm_o_g_in': out['m_o_g_in'], 'm_o_w_in': out['m_o_w_in'], 'm_o_b_f': out['m_o_b_f'], 'm_o_w_out': out['m_o_w_out'], 'm_g_final': out['m_g_final'], 'v_e_g_in': out['v_e_g_in'], 'v_e_w_in': out['v_e_w_in'], 'v_e_g_q_a': out['v_e_g_q_a'], 'v_e_w_q_up': out['v_e_w_q_up'], 'v_e_g_kv_a': out['v_e_g_kv_a'], 'v_e_w_kv_up': out['v_e_w_kv_up'], 'v_e_sinks': out['v_e_sinks'], 'v_e_w_out': out['v_e_w_out'], 'v_o_g_in': out['v_o_g_in'], 'v_o_w_in': out['v_o_w_in'], 'v_o_b_f': out['v_o_b_f'], 'v_o_w_out': out['v_o_w_out'], 'v_g_final': out['v_g_final']}


def _loss(weights, diff, rest, loss_target):
    with _jax.named_scope("forward"):
        args = {**rest, TWIN_DIFF_INPUT: diff, **{k: w.astype(_WEIGHT_DTYPES[k]) for k, w in weights.items()}}
        y = _forward(args)
    with _jax.named_scope("loss_head"):
        err = _jnp.square(y.astype(_jnp.float32) - loss_target)
        return 0.5 * _jnp.sum(_jnp.mean(err, axis=-1)) if err.ndim else 0.5 * err


def _adamw(w, g, m, v):
    m = ADAM_B1 * m + (1.0 - ADAM_B1) * g
    v = ADAM_B2 * v + (1.0 - ADAM_B2) * _jnp.square(g)
    m_hat = m / (1.0 - ADAM_B1 ** ADAM_STEP)
    v_hat = v / (1.0 - ADAM_B2 ** ADAM_STEP)
    delta = -ADAM_LR * (m_hat / (_jnp.sqrt(v_hat) + ADAM_EPS) + ADAM_WD * w)
    return delta, m, v


def reference(x, positions, e_g_in, e_w_in, e_g_q_a, e_w_q_up, e_g_kv_a, e_w_kv_up, e_sinks, e_w_out, o_g_in, o_w_in, o_b_f, o_w_out, g_final, loss_target, m_e_g_in, m_e_w_in, m_e_g_q_a, m_e_w_q_up, m_e_g_kv_a, m_e_w_kv_up, m_e_sinks, m_e_w_out, m_o_g_in, m_o_w_in, m_o_b_f, m_o_w_out, m_g_final, v_e_g_in, v_e_w_in, v_e_g_q_a, v_e_w_q_up, v_e_g_kv_a, v_e_w_kv_up, v_e_sinks, v_e_w_out, v_o_g_in, v_o_w_in, v_o_b_f, v_o_w_out, v_g_final):
    given = dict(x=x, positions=positions, e_g_in=e_g_in, e_w_in=e_w_in, e_g_q_a=e_g_q_a, e_w_q_up=e_w_q_up, e_g_kv_a=e_g_kv_a, e_w_kv_up=e_w_kv_up, e_sinks=e_sinks, e_w_out=e_w_out, o_g_in=o_g_in, o_w_in=o_w_in, o_b_f=o_b_f, o_w_out=o_w_out, g_final=g_final, loss_target=loss_target, m_e_g_in=m_e_g_in, m_e_w_in=m_e_w_in, m_e_g_q_a=m_e_g_q_a, m_e_w_q_up=m_e_w_q_up, m_e_g_kv_a=m_e_g_kv_a, m_e_w_kv_up=m_e_w_kv_up, m_e_sinks=m_e_sinks, m_e_w_out=m_e_w_out, m_o_g_in=m_o_g_in, m_o_w_in=m_o_w_in, m_o_b_f=m_o_b_f, m_o_w_out=m_o_w_out, m_g_final=m_g_final, v_e_g_in=v_e_g_in, v_e_w_in=v_e_w_in, v_e_g_q_a=v_e_g_q_a, v_e_w_q_up=v_e_w_q_up, v_e_g_kv_a=v_e_g_kv_a, v_e_w_kv_up=v_e_w_kv_up, v_e_sinks=v_e_sinks, v_e_w_out=v_e_w_out, v_o_g_in=v_o_g_in, v_o_w_in=v_o_w_in, v_o_b_f=v_o_b_f, v_o_w_out=v_o_w_out, v_g_final=v_g_final)
    weights = {n: given[n] for n in TWIN_WEIGHTS}
    shared = {n: given[n] for n in SHARED_INPUTS}
    per_example = {n: given[n] for n in ['x', 'positions']}
    grad_fn = _jax.value_and_grad(_loss, argnums=(0, 1))

    def one_microbatch(ex, loss_target):
        ex = dict(ex)
        diff = ex.pop(TWIN_DIFF_INPUT)
        return grad_fn(weights, diff, {**shared, **ex}, loss_target)

    if N_MICROBATCH == 1:
        loss, (grad_w, grad_x) = one_microbatch(per_example, given["loss_target"])
    else:
        def body(carry, xs):
            loss_sum, grad_sum = carry
            l_k, (gw_k, gx_k) = one_microbatch(xs[0], xs[1])
            with _jax.named_scope("update"):
                return (loss_sum + l_k, _jax.tree.map(_jnp.add, grad_sum, gw_k)), gx_k

        init = (_jnp.zeros((), _jnp.float32), _jax.tree.map(_jnp.zeros_like, weights))
        (loss, grad_w), grad_x = _jax.lax.scan(body, init, (per_example, given["loss_target"]))
    with _jax.named_scope("update"):
        delta_w, new_m, new_v = {}, {}, {}
        for n in TWIN_WEIGHTS:
            delta_w[n], new_m[n], new_v[n] = _adamw(weights[n], grad_w[n], given["m_" + n], given["v_" + n])
    return (loss, grad_x, *[grad_w[n] for n in TWIN_WEIGHTS], *[delta_w[n] for n in TWIN_WEIGHTS],
            *[new_m[n] for n in TWIN_WEIGHTS], *[new_v[n] for n in TWIN_WEIGHTS])
```

```python
import functools
import math

import numpy as np
import jax
import jax.numpy as jnp
from jax import lax
from jax.experimental import pallas as pl
from jax.experimental.pallas import tpu as pltpu

D = 1024
EPS = 1e-6
ROPE_THETA = 10000.0
N_MLA = 8
Q_RANK = 256
KV_RANK = 128
NOPE = 64
ROPE = 32
N_SWA = 8
WINDOW = 128
N_FOX = 16
HEAD = 64
E_SPLITS = (256, 128, 32, 512, 128, 128, 1024)
O_SPLITS = (1024, 1024, 1024, 16, 1024)
LR, B1, B2, AEPS, WD, STEP = 0.001, 0.9, 0.999, 1e-08, 0.01, 10

LANES = 128
HALF = 64
VMEM_LIMIT = 56 * 1024 * 1024
MXU = jnp.bfloat16
TOK = 256
ATT = 256
NEG = float("-inf")

PACK_COLS = 1024
PACK_ROWS = 2176
HALF_ROWS = PACK_ROWS // 2
MESH_ID = pl.DeviceIdType.MESH


def _pcall(body, *, name, vmem=VMEM_LIMIT, semantics=None, **kw):
    params = dict(vmem_limit_bytes=vmem)
    if semantics is not None:
        params["dimension_semantics"] = semantics
    return pl.pallas_call(body, name=name, compiler_params=pltpu.CompilerParams(**params), **kw)


def _mm(a, b):
    return jnp.dot(a.astype(MXU), b.astype(MXU), preferred_element_type=jnp.float32)


def _mm_nt(a, b):
    return lax.dot_general(a.astype(MXU), b.astype(MXU), (((1,), (1,)), ((), ())),
                           preferred_element_type=jnp.float32)


def _mm_tn(a, b):
    return lax.dot_general(a.astype(MXU), b.astype(MXU), (((0,), (0,)), ((), ())),
                           preferred_element_type=jnp.float32)


def _full(shape):
    n = len(shape)
    return pl.BlockSpec(shape, lambda *_: (0,) * n)


def _rows(tm, n):
    return pl.BlockSpec((tm, n), lambda i: (i, 0))


def _sds(shape, dtype):
    return jax.ShapeDtypeStruct(shape, dtype)


def _rms(x, g):
    r = lax.rsqrt(jnp.mean(x * x, axis=-1, keepdims=True) + EPS)
    return x * r * g


def _rms_bwd(x, g, dy):
    r = lax.rsqrt(jnp.mean(x * x, axis=-1, keepdims=True) + EPS)
    xh = x * r
    dxh = dy * g
    dx = r * (dxh - xh * jnp.mean(dxh * xh, axis=-1, keepdims=True))
    return dx, dy * xh


def _sigmoid(x):
    return 1.0 / (1.0 + jnp.exp(-x))


def _lane_masks(dtype=None):
    lane = lax.broadcasted_iota(jnp.int32, (1, LANES), 1)
    return lane < HALF


def _split_heads(a, lo):
    z = jnp.zeros_like(a)
    return [jnp.where(lo, a, z), jnp.where(lo, z, a)]


def _rope_consts():
    inv = np.zeros((8, LANES), np.float32)
    j = np.arange(ROPE // 2, dtype=np.float32)
    f = (1.0 / (ROPE_THETA ** (np.arange(0, ROPE, 2, dtype=np.float32) / ROPE))).astype(np.float32)
    inv[0, HALF:HALF + 16] = f
    inv[0, HALF + 16:HALF + 32] = f
    inv[1, HALF:HALF + 16] = -1.0
    inv[1, HALF + 16:HALF + 32] = 1.0
    del j
    return jnp.asarray(inv)


def _rope_tables(pos_f, consts):
    ang = pos_f * consts[0:1, :]
    sign = consts[1:2, :]
    c = jnp.where(sign != 0.0, jnp.cos(ang), 1.0)
    s = jnp.sin(ang) * sign
    return c, s


def _swap_halves(v, sign):
    lo = pltpu.roll(v, LANES - 16, axis=1)
    hi = pltpu.roll(v, 16, axis=1)
    return jnp.where(sign < 0.0, lo, jnp.where(sign > 0.0, hi, 0.0))


def _rope(x, c, s, sign):
    return x * c + _swap_halves(x, sign) * s


def _rope_t(dy, c, s, sign):
    return dy * c + _swap_halves(dy * s, sign)


def _layer0_in(x, pos, g_in, w_in, g_q, w_q, g_kv, w_kv):
    S = x.shape[0]
    consts = _rope_consts()

    def body(x_ref, pos_ref, c_ref, g_ref, w_ref, gq_ref, wq_ref, gkv_ref, wkv_ref,
             h_ref, cq_ref, ckv_ref, cqn_ref, ckvn_ref, qm_ref, km_ref, vm_ref,
             qs_ref, kd_ref, vd_ref, gate_ref, cos_ref, sin_ref):
        h = _rms(x_ref[...], g_ref[...])
        h_ref[...] = h.astype(h_ref.dtype)
        z = _mm(h, w_ref[...])
        cq = z[:, 0:256]
        ckv = z[:, 256:384]
        kpe = z[:, 384:512]
        cq_ref[...] = cq
        ckv_ref[...] = ckv
        qs_ref[...] = z[:, 512:1024].astype(qs_ref.dtype)
        kd_ref[...] = z[:, 1024:1536].astype(kd_ref.dtype)
        vd_ref[...] = z[:, 1536:2048].astype(vd_ref.dtype)
        gate_ref[...] = z[:, 2048:3072]
        cqn = _rms(cq, gq_ref[...])
        ckvn = _rms(ckv, gkv_ref[...])
        cqn_ref[...] = cqn.astype(cqn_ref.dtype)
        ckvn_ref[...] = ckvn.astype(ckvn_ref.dtype)
        q = _mm(cqn, wq_ref[...])
        kv = _mm(ckvn, wkv_ref[...])
        vm_ref[...] = kv[:, 1024:1536].astype(vm_ref.dtype)
        consts_v = c_ref[...]
        sign = consts_v[1:2, :]
        c, s = _rope_tables(pos_ref[...].astype(jnp.float32), consts_v)
        cos_ref[...] = c
        sin_ref[...] = s
        kpe_r = _rope(kpe, c, s, sign)
        for hd in range(N_MLA):
            sl = slice(LANES * hd, LANES * (hd + 1))
            qm_ref[:, sl] = _rope(q[:, sl], c, s, sign).astype(qm_ref.dtype)
            km_ref[:, sl] = (kv[:, sl] + kpe_r).astype(km_ref.dtype)

    outs = [
        ((S, D), MXU), ((S, 256), jnp.float32), ((S, 128), jnp.float32), ((S, 256), MXU), ((S, 128), MXU),
        ((S, 1024), MXU), ((S, 1024), MXU), ((S, 512), MXU), ((S, 512), MXU), ((S, 512), MXU), ((S, 512), MXU),
        ((S, 1024), jnp.float32), ((S, 128), jnp.float32), ((S, 128), jnp.float32),
    ]
    return _pcall(
        body, name="layer0_in", grid=(S // TOK,), semantics=("arbitrary",),
        in_specs=[_rows(TOK, D), _rows(TOK, 1), _full((8, LANES)), _full((1, D)), _full(w_in.shape), _full((1, 256)),
                  _full(w_q.shape), _full((1, 128)), _full(w_kv.shape)],
        out_specs=[_rows(TOK, s[1]) for s, _ in outs],
        out_shape=[_sds(s, d) for s, d in outs],
    )(x, pos, consts, g_in, w_in, g_q, w_q, g_kv, w_kv)


def _attn_fwd(q, k, v, scale, *, split, name, lcc=None, lcr=None):
    S = q.shape[0]
    npair = v.shape[1] // LANES
    W = 2 * LANES if split else LANES
    T = min(ATT, S)
    nq = S // T
    bias = lcc is not None

    def body(*refs):
        if bias:
            q_ref, k_ref, v_ref, lcc_ref, lcr_ref, o_ref, lse_ref = refs
        else:
            q_ref, k_ref, v_ref, o_ref, lse_ref = refs
        lo = _lane_masks()
        row = lax.broadcasted_iota(jnp.int32, (T, T), 0)
        col = lax.broadcasted_iota(jnp.int32, (T, T), 1)
        causal = row >= col

        def heads(blk):
            if split:
                return [blk[:, :LANES], blk[:, LANES:]]
            return _split_heads(blk, lo)

        def q_block(qi, carry0):
            q0 = pl.multiple_of(qi * T, T)
            qs = heads(q_ref[pl.ds(q0, T), :])
            lq = [lcc_ref[h, pl.ds(q0, T), :] for h in (0, 1)] if bias else None

            def step(ki, carry, diag):
                k0 = pl.multiple_of(ki * T, T)
                kblk = k_ref[pl.ds(k0, T), :]
                ks = [kblk[:, :LANES], kblk[:, LANES:]] if split else [kblk, kblk]
                vs = _split_heads(v_ref[pl.ds(k0, T), :], lo)
                out = []
                for h in (0, 1):
                    m, l, acc = carry[h]
                    s = _mm_nt(qs[h], ks[h]) * scale
                    if bias:
                        s = s + lq[h] - lcr_ref[0, h:h + 1, pl.ds(k0, T)]
                    if diag:
                        s = jnp.where(causal, s, NEG)
                    mn = jnp.maximum(m, jnp.max(s, axis=1, keepdims=True))
                    a = jnp.exp(m - mn)
                    p = jnp.exp(s - mn)
                    l = a * l + jnp.sum(p, axis=1, keepdims=True)
                    acc = a * acc + _mm(p, vs[h])
                    out.append((mn, l, acc))
                return tuple(out)

            init = tuple((jnp.full((T, 1), NEG, jnp.float32), jnp.zeros((T, 1), jnp.float32),
                          jnp.zeros((T, LANES), jnp.float32)) for _ in (0, 1))
            carry = lax.fori_loop(0, qi, lambda ki, c: step(ki, c, False), init)
            carry = step(qi, carry, True)
            o = None
            for h in (0, 1):
                m, l, acc = carry[h]
                oh = acc / l
                o = oh if o is None else o + oh
                lse_ref[h, pl.ds(q0, T), :] = m + jnp.log(l)
            o_ref[pl.ds(q0, T), :] = o
            return carry0

        lax.fori_loop(0, nq, q_block, 0)

    in_specs = [pl.BlockSpec((S, W), lambda j: (0, j)), pl.BlockSpec((S, W), lambda j: (0, j)),
                pl.BlockSpec((S, LANES), lambda j: (0, j))]
    args = [q, k, v]
    if bias:
        in_specs += [pl.BlockSpec((2, S, 1), lambda j: (j, 0, 0)), pl.BlockSpec((1, 2, S), lambda j: (j, 0, 0))]
        args += [lcc, lcr]
    return _pcall(
        body, name=name, grid=(npair,), semantics=("arbitrary",),
        in_specs=in_specs,
        out_specs=[pl.BlockSpec((S, LANES), lambda j: (0, j)), pl.BlockSpec((2, S, 1), lambda j: (j, 0, 0))],
        out_shape=[_sds((S, npair * LANES), jnp.float32), _sds((2 * npair, S, 1), jnp.float32)],
    )(*args)


def _attn_bwd(q, k, v, do, o, lse, scale, *, split, name, lcc=None, lcr=None):
    S = q.shape[0]
    npair = v.shape[1] // LANES
    W = 2 * LANES if split else LANES
    T = min(ATT, S)
    nq = S // T
    bias = lcc is not None

    def body(*refs):
        if bias:
            (q_ref, k_ref, v_ref, do_ref, o_ref, lse_ref, lcc_ref, lcr_ref,
             dq_ref, dk_ref, dv_ref, dlc_ref, drow_ref, dq_acc, dl_a, dl_b, dk_acc, dv_acc, col_acc) = refs
        else:
            (q_ref, k_ref, v_ref, do_ref, o_ref, lse_ref,
             dq_ref, dk_ref, dv_ref, dq_acc, dl_a, dl_b, dk_acc, dv_acc) = refs
        dls = (dl_a, dl_b)
        lo = _lane_masks()
        row = lax.broadcasted_iota(jnp.int32, (T, T), 0)
        col = lax.broadcasted_iota(jnp.int32, (T, T), 1)
        causal = row >= col

        def heads(blk):
            if split:
                return [blk[:, :LANES], blk[:, LANES:]]
            return _split_heads(blk, lo)

        dq_acc[...] = jnp.zeros_like(dq_acc)
        if bias:
            drow_ref[...] = jnp.zeros_like(drow_ref)

        def delta_block(qi, c):
            q0 = pl.multiple_of(qi * T, T)
            prod = do_ref[pl.ds(q0, T), :].astype(jnp.float32) * o_ref[pl.ds(q0, T), :]
            ph = _split_heads(prod, lo)
            for h in (0, 1):
                dls[h][pl.ds(q0, T), :] = jnp.sum(ph[h], axis=1, keepdims=True)
            return c

        lax.fori_loop(0, nq, delta_block, 0)

        def k_block(ki, c):
            k0 = pl.multiple_of(ki * T, T)
            kblk = k_ref[pl.ds(k0, T), :]
            ks = heads(kblk)
            vblk = v_ref[pl.ds(k0, T), :]
            dk_acc[...] = jnp.zeros_like(dk_acc)
            dv_acc[...] = jnp.zeros_like(dv_acc)
            if bias:
                col_acc[...] = jnp.zeros_like(col_acc)

            def step(qi, c2, diag):
                q0 = pl.multiple_of(qi * T, T)
                qs = heads(q_ref[pl.ds(q0, T), :])
                dos = _split_heads(do_ref[pl.ds(q0, T), :], lo)
                for h in (0, 1):
                    s = _mm_nt(qs[h], ks[h]) * scale
                    if bias:
                        s = s + lcc_ref[h, pl.ds(q0, T), :] - lcr_ref[0, h:h + 1, pl.ds(k0, T)]
                    if diag:
                        s = jnp.where(causal, s, NEG)
                    p = jnp.exp(s - lse_ref[h, pl.ds(q0, T), :])
                    dv_acc[...] += _mm_tn(p, dos[h])
                    dp = _mm_nt(dos[h], vblk)
                    ds = p * (dp - dls[h][pl.ds(q0, T), :])
                    if split:
                        sl = slice(LANES * h, LANES * (h + 1))
                        dq_acc[pl.ds(q0, T), sl] += _mm(ds, ks[h]) * scale
                        dk_acc[:, sl] += _mm_tn(ds, qs[h]) * scale
                    else:
                        dq_acc[pl.ds(q0, T), :] += _mm(ds, ks[h]) * scale
                        dk_acc[...] += _mm_tn(ds, qs[h]) * scale
                    if bias:
                        col_acc[h:h + 1, :] += jnp.sum(ds, axis=0, keepdims=True)
                        drow_ref[h, pl.ds(q0, T), :] += jnp.sum(ds, axis=1, keepdims=True)
                return c2

            step(ki, 0, True)
            lax.fori_loop(ki + 1, nq, lambda qi, c2: step(qi, c2, False), 0)
            dk_ref[pl.ds(k0, T), :] = dk_acc[...].astype(dk_ref.dtype)
            dv_ref[pl.ds(k0, T), :] = dv_acc[...].astype(dv_ref.dtype)
            if bias:
                dlc_ref[0, :, pl.ds(k0, T)] = -col_acc[0:2, :]
            return c

        lax.fori_loop(0, nq, k_block, 0)
        dq_ref[...] = dq_acc[...].astype(dq_ref.dtype)

    wide = pl.BlockSpec((S, W), lambda j: (0, j))
    slab = pl.BlockSpec((S, LANES), lambda j: (0, j))
    stat = pl.BlockSpec((2, S, 1), lambda j: (j, 0, 0))
    in_specs = [wide, wide, slab, slab, slab, stat]
    args = [q, k, v, do, o, lse]
    out_specs = [wide, wide, slab]
    out_shape = [_sds(q.shape, do.dtype if not split else jnp.float32),
                 _sds(k.shape, do.dtype if not split else jnp.float32), _sds(v.shape, do.dtype)]
    scratch = [pltpu.VMEM((S, W), jnp.float32), pltpu.VMEM((S, 1), jnp.float32), pltpu.VMEM((S, 1), jnp.float32),
               pltpu.VMEM((T, W), jnp.float32), pltpu.VMEM((T, LANES), jnp.float32)]
    if bias:
        in_specs += [stat, pl.BlockSpec((1, 2, S), lambda j: (j, 0, 0))]
        args += [lcc, lcr]
        out_specs.append(pl.BlockSpec((1, 2, S), lambda j: (j, 0, 0)))
        out_shape.append(_sds((npair, 2, S), jnp.float32))
        out_specs.append(stat)
        out_shape.append(_sds((2 * npair, S, 1), jnp.float32))
        scratch.append(pltpu.VMEM((8, T), jnp.float32))
    return _pcall(
        body, name=name, grid=(npair,), semantics=("arbitrary",),
        in_specs=in_specs, out_specs=out_specs, out_shape=out_shape, scratch_shapes=scratch,
    )(*args)


def _swa_scores(qh, kblk, slope, shift):
    s = _mm_nt(qh, kblk) * (HEAD ** -0.5)
    a = lax.broadcasted_iota(jnp.int32, (WINDOW, 2 * WINDOW), 0)
    c = lax.broadcasted_iota(jnp.int32, (WINDOW, 2 * WINDOW), 1)
    dist = a - c + shift
    s = s - slope * dist.astype(jnp.float32)
    return jnp.where((dist >= 0) & (dist < WINDOW), s, NEG)


def _swa_fwd(q, kd, vd, sinks, slopes):
    S = q.shape[0]
    npair = q.shape[1] // LANES
    nb = S // WINDOW

    def body(sink_ref, slope_ref, q_ref, k_ref, v_ref, o_ref, lse_ref):
        j = pl.program_id(0)
        lo = _lane_masks()

        def q_block(qi, c):
            q0 = pl.multiple_of(qi * WINDOW, WINDOW)
            k0 = pl.multiple_of(jnp.maximum(qi - 1, 0) * WINDOW, WINDOW)
            shift = q0 - k0
            qs = _split_heads(q_ref[pl.ds(q0, WINDOW), :], lo)
            kblk = k_ref[pl.ds(k0, 2 * WINDOW), :]
            vs = _split_heads(v_ref[pl.ds(k0, 2 * WINDOW), :], lo)
            o = None
            for h in (0, 1):
                sink = sink_ref[2 * j + h]
                s = _swa_scores(qs[h], kblk, slope_ref[2 * j + h], shift)
                m = jnp.maximum(jnp.max(s, axis=1, keepdims=True), sink)
                p = jnp.exp(s - m)
                den = jnp.sum(p, axis=1, keepdims=True) + jnp.exp(sink - m)
                oh = _mm(p / den, vs[h])
                o = oh if o is None else o + oh
                lse_ref[h, pl.ds(q0, WINDOW), :] = m + jnp.log(den)
            o_ref[pl.ds(q0, WINDOW), :] = o
            return c

        lax.fori_loop(0, nb, q_block, 0)

    smem = pl.BlockSpec(memory_space=pltpu.SMEM)
    slab = pl.BlockSpec((S, LANES), lambda j: (0, j))
    return _pcall(
        body, name="swa_fwd", grid=(npair,), semantics=("arbitrary",),
        in_specs=[smem, smem, slab, slab, slab],
        out_specs=[slab, pl.BlockSpec((2, S, 1), lambda j: (j, 0, 0))],
        out_shape=[_sds((S, npair * LANES), jnp.float32), _sds((2 * npair, S, 1), jnp.float32)],
    )(sinks, slopes, q, kd, vd)


def _swa_bwd(q, kd, vd, do, o, lse, sinks, slopes):
    S = q.shape[0]
    npair = q.shape[1] // LANES
    nb = S // WINDOW

    def body(sink_ref, slope_ref, q_ref, k_ref, v_ref, do_ref, o_ref, lse_ref,
             dq_ref, dk_ref, dv_ref, dsink_ref, dk_acc, dv_acc):
        j = pl.program_id(0)
        lo = _lane_masks()
        dk_acc[...] = jnp.zeros_like(dk_acc)
        dv_acc[...] = jnp.zeros_like(dv_acc)

        def q_block(qi, carry):
            q0 = pl.multiple_of(qi * WINDOW, WINDOW)
            k0 = pl.multiple_of(jnp.maximum(qi - 1, 0) * WINDOW, WINDOW)
            shift = q0 - k0
            qs = _split_heads(q_ref[pl.ds(q0, WINDOW), :], lo)
            dos = _split_heads(do_ref[pl.ds(q0, WINDOW), :], lo)
            oblk = o_ref[pl.ds(q0, WINDOW), :]
            kblk = k_ref[pl.ds(k0, 2 * WINDOW), :]
            vblk = v_ref[pl.ds(k0, 2 * WINDOW), :]
            ks = _split_heads(kblk, lo)
            dq = None
            out = []
            for h in (0, 1):
                sink = sink_ref[2 * j + h]
                lse_h = lse_ref[h, pl.ds(q0, WINDOW), :]
                s = _swa_scores(qs[h], kblk, slope_ref[2 * j + h], shift)
                p = jnp.exp(s - lse_h)
                delta = jnp.sum(dos[h].astype(jnp.float32) * oblk, axis=1, keepdims=True)
                dv_acc[pl.ds(k0, 2 * WINDOW), :] += _mm_tn(p, dos[h])
                dp = _mm_nt(dos[h], vblk)
                ds = p * (dp - delta)
                dqh = _mm(ds, ks[h]) * (HEAD ** -0.5)
                dq = dqh if dq is None else dq + dqh
                dk_acc[pl.ds(k0, 2 * WINDOW), :] += _mm_tn(ds, qs[h]) * (HEAD ** -0.5)
                dsk = jnp.sum(-jnp.exp(sink - lse_h) * delta, axis=0, keepdims=True)
                out.append(carry[h] + dsk)
            dq_ref[pl.ds(q0, WINDOW), :] = dq.astype(dq_ref.dtype)
            return tuple(out)

        zero = jnp.zeros((1, 1), jnp.float32)
        dsa, dsb = lax.fori_loop(0, nb, q_block, (zero, zero))
        dk_ref[...] = dk_acc[...].astype(dk_ref.dtype)
        dv_ref[...] = dv_acc[...].astype(dv_ref.dtype)
        r = lax.broadcasted_iota(jnp.int32, (8, LANES), 0)
        dsink_ref[0] = jnp.where(r == 0, dsa, jnp.where(r == 1, dsb, 0.0))

    smem = pl.BlockSpec(memory_space=pltpu.SMEM)
    slab = pl.BlockSpec((S, LANES), lambda j: (0, j))
    return _pcall(
        body, name="swa_bwd", grid=(npair,), semantics=("arbitrary",),
        in_specs=[smem, smem, slab, slab, slab, slab, slab, pl.BlockSpec((2, S, 1), lambda j: (j, 0, 0))],
        out_specs=[slab, slab, slab, pl.BlockSpec((1, 8, LANES), lambda j: (j, 0, 0))],
        out_shape=[_sds(q.shape, do.dtype), _sds(kd.shape, do.dtype), _sds(vd.shape, do.dtype),
                   _sds((npair, 8, LANES), jnp.float32)],
        scratch_shapes=[pltpu.VMEM((S, LANES), jnp.float32), pltpu.VMEM((S, LANES), jnp.float32)],
    )(sinks, slopes, q, kd, vd, do, o, lse)


def _log_steps(S):
    k, out = 1, []
    while k < S:
        out.append(k)
        k *= 2
    return out


def _forget_fwd(f_row, b_col):
    S = f_row.shape[1]

    def body(f_ref, b_ref, lc_ref):
        x = f_ref[...] + b_ref[...]
        lc = jnp.minimum(x, 0.0) - jnp.log(1.0 + jnp.exp(-jnp.abs(x)))
        idx = lax.broadcasted_iota(jnp.int32, lc.shape, 1)
        for k in _log_steps(S):
            lc = lc + jnp.where(idx >= k, pltpu.roll(lc, k, axis=1), 0.0)
        lc_ref[...] = lc

    return _pcall(body, name="forget_fwd", out_shape=_sds(f_row.shape, jnp.float32))(f_row, b_col)


def _forget_bwd(dcol_row, drow_row, f_row, b_col):
    S = f_row.shape[1]

    def body(d_ref, r_ref, f_ref, b_ref, df_ref, db_ref):
        g = d_ref[...] + r_ref[...]
        idx = lax.broadcasted_iota(jnp.int32, g.shape, 1)
        for k in _log_steps(S):
            g = g + jnp.where(idx < S - k, pltpu.roll(g, S - k, axis=1), 0.0)
        x = f_ref[...] + b_ref[...]
        df = g * _sigmoid(-x)
        df_ref[...] = df
        db_ref[...] = jnp.sum(df, axis=1, keepdims=True)

    return _pcall(body, name="forget_bwd",
                  out_shape=[_sds(f_row.shape, jnp.float32), _sds((f_row.shape[0], 1), jnp.float32)])(dcol_row, drow_row, f_row, b_col)


def _layer0_out_layer1_in(x, o_m, o_s, gate, w_out, g1, w_in1):
    S = x.shape[0]

    def body(x_ref, om_ref, os_ref, gate_ref, wo_ref, g_ref, w_ref,
             x1_ref, u_ref, h_ref, q_ref, k_ref, v_ref, g1_ref, f_ref):
        gt = gate_ref[...]
        sg = gt * _sigmoid(gt)
        um = om_ref[...] * sg[:, :512]
        us = os_ref[...] * sg[:, 512:]
        u_ref[:, :512] = um.astype(u_ref.dtype)
        u_ref[:, 512:] = us.astype(u_ref.dtype)
        x1 = x_ref[...] + _mm(um, wo_ref[0:512, :]) + _mm(us, wo_ref[512:1024, :])
        x1_ref[...] = x1
        h = _rms(x1, g_ref[...])
        h_ref[...] = h.astype(h_ref.dtype)
        z = _mm(h, w_ref[...])
        q_ref[...] = z[:, 0:1024].astype(q_ref.dtype)
        k_ref[...] = z[:, 1024:2048].astype(k_ref.dtype)
        v_ref[...] = z[:, 2048:3072].astype(v_ref.dtype)
        g1_ref[...] = z[:, 3072:4096]
        f_ref[...] = z[:, 4096:4224]

    outs = [((S, D), jnp.float32), ((S, D), MXU), ((S, D), MXU), ((S, D), MXU), ((S, D), MXU), ((S, D), MXU),
            ((S, D), jnp.float32), ((S, LANES), jnp.float32)]
    return _pcall(
        body, name="layer0_out_layer1_in", grid=(S // TOK,), semantics=("arbitrary",),
        in_specs=[_rows(TOK, D), _rows(TOK, 512), _rows(TOK, 512), _rows(TOK, D), _full((D, D)), _full((1, D)),
                  _full(w_in1.shape)],
        out_specs=[_rows(TOK, s[1]) for s, _ in outs],
        out_shape=[_sds(s, d) for s, d in outs],
    )(x, o_m, o_s, gate, w_out, g1, w_in1)


def _head(x1, o1, gate1, w_out1, g_f, target):
    S = x1.shape[0]

    def body(x1_ref, o_ref, gate_ref, wo_ref, g_ref, t_ref,
             loss_ref, dgf_ref, dx2_ref, u_ref, do_ref, dgate_ref):
        i = pl.program_id(0)
        gt = gate_ref[...]
        sig = _sigmoid(gt)
        sg = gt * sig
        o = o_ref[...]
        u = o * sg
        u_ref[...] = u.astype(u_ref.dtype)
        x2 = x1_ref[...] + _mm(u, wo_ref[...])
        g = g_ref[...]
        y = _rms(x2, g)
        err = y - t_ref[...]
        part = 0.5 * jnp.sum(jnp.mean(err * err, axis=-1, keepdims=True), axis=0, keepdims=True)
        dy = err * (1.0 / D)
        dx2, dg_rows = _rms_bwd(x2, g, dy)
        dx2_ref[...] = dx2
        du = _mm_nt(dx2, wo_ref[...])
        do_ref[...] = (du * sg).astype(do_ref.dtype)
        dgate_ref[...] = (du * o * (sig * (1.0 + gt * (1.0 - sig)))).astype(dgate_ref.dtype)

        @pl.when(i == 0)
        def _():
            loss_ref[...] = jnp.zeros_like(loss_ref)
            dgf_ref[...] = jnp.zeros_like(dgf_ref)

        loss_ref[...] += jnp.broadcast_to(part, loss_ref.shape)
        dgf_ref[...] += jnp.sum(dg_rows, axis=0, keepdims=True)

    outs = [((S, D), jnp.float32), ((S, D), MXU), ((S, D), MXU), ((S, D), MXU)]
    return _pcall(
        body, name="head", grid=(S // TOK,), semantics=("arbitrary",),
        in_specs=[_rows(TOK, D), _rows(TOK, D), _rows(TOK, D), _full((D, D)), _full((1, D)), _rows(TOK, D)],
        out_specs=[_full((8, LANES)), _full((1, D))] + [_rows(TOK, D) for _ in outs],
        out_shape=[_sds((8, LANES), jnp.float32), _sds((1, D), jnp.float32)] + [_sds(s, d) for s, d in outs],
    )(x1, o1, gate1, w_out1, g_f, target)


def _layer1_in_bwd(dq, dk, dv, dgate1, df, x1, dx2, g1, w_in1, gate0, o_m, o_s, w_out0):
    S = x1.shape[0]

    def body(dq_ref, dk_ref, dv_ref, dg1_ref, df_ref, x1_ref, dx2_ref, g_ref, w_ref, gate_ref, om_ref, os_ref,
             wo_ref, dz_ref, dx1_ref, dgn_ref, dom_ref, dos_ref, dgate_ref):
        i = pl.program_id(0)
        dz_ref[:, 0:1024] = dq_ref[...]
        dz_ref[:, 1024:2048] = dk_ref[...]
        dz_ref[:, 2048:3072] = dv_ref[...]
        dz_ref[:, 3072:4096] = dg1_ref[...]
        dz_ref[:, 4096:4224] = df_ref[...]
        dh = _mm_nt(dz_ref[...], w_ref[...])
        g = g_ref[...]
        dxn, dg_rows = _rms_bwd(x1_ref[...], g, dh)
        dx1 = dx2_ref[...] + dxn
        dx1_ref[...] = dx1
        du = _mm_nt(dx1, wo_ref[...])
        gt = gate_ref[...]
        sig = _sigmoid(gt)
        sg = gt * sig
        dsg = sig * (1.0 + gt * (1.0 - sig))
        dom_ref[...] = (du[:, :512] * sg[:, :512]).astype(dom_ref.dtype)
        dos_ref[...] = (du[:, 512:] * sg[:, 512:]).astype(dos_ref.dtype)
        dgate_ref[:, :512] = (du[:, :512] * om_ref[...] * dsg[:, :512]).astype(dgate_ref.dtype)
        dgate_ref[:, 512:] = (du[:, 512:] * os_ref[...] * dsg[:, 512:]).astype(dgate_ref.dtype)

        @pl.when(i == 0)
        def _():
            dgn_ref[...] = jnp.zeros_like(dgn_ref)

        dgn_ref[...] += jnp.sum(dg_rows, axis=0, keepdims=True)

    return _pcall(
        body, name="layer1_in_bwd", grid=(S // TOK,), semantics=("arbitrary",),
        in_specs=[_rows(TOK, D), _rows(TOK, D), _rows(TOK, D), _rows(TOK, D), _rows(TOK, LANES), _rows(TOK, D),
                  _rows(TOK, D), _full((1, D)), _full(w_in1.shape), _rows(TOK, D), _rows(TOK, 512), _rows(TOK, 512),
                  _full((D, D))],
        out_specs=[_rows(TOK, 4224), _rows(TOK, D), _full((1, D)), _rows(TOK, 512), _rows(TOK, 512), _rows(TOK, D)],
        out_shape=[_sds((S, 4224), MXU), _sds((S, D), jnp.float32), _sds((1, D), jnp.float32),
                   _sds((S, 512), MXU), _sds((S, 512), MXU), _sds((S, D), MXU)],
    )(dq, dk, dv, dgate1, df, x1, dx2, g1, w_in1, gate0, o_m, o_s, w_out0)


def _layer0_in_bwd(dqm, dkm, dvm, dqs, dkd, dvd, dgate0, cos, sin, cq, ckv, x, dx1, g_in, w_in, g_q, w_q, g_kv, w_kv):
    S = x.shape[0]
    consts = _rope_consts()

    def body(dqm_ref, dkm_ref, dvm_ref, dqs_ref, dkd_ref, dvd_ref, dgate_ref, cos_ref, sin_ref, c_ref, cq_ref, ckv_ref,
             x_ref, dx1_ref, g_ref, w_ref, gq_ref, wq_ref, gkv_ref, wkv_ref,
             dx_ref, dz_ref, dqu_ref, dkvu_ref, dgin_ref, dgq_ref, dgkv_ref):
        i = pl.program_id(0)
        lo = _lane_masks()
        sign = c_ref[...][1:2, :]
        c = cos_ref[...]
        s = sin_ref[...]
        dkpe = None
        for hd in range(N_MLA):
            sl = slice(LANES * hd, LANES * (hd + 1))
            dqu_ref[:, sl] = _rope_t(dqm_ref[:, sl], c, s, sign).astype(dqu_ref.dtype)
            dkh = dkm_ref[:, sl]
            dkvu_ref[:, sl] = jnp.where(lo, dkh, 0.0).astype(dkvu_ref.dtype)
            dkpe = dkh if dkpe is None else dkpe + dkh
        dkvu_ref[:, 1024:1536] = dvm_ref[...]
        dkpe = _rope_t(jnp.where(lo, 0.0, dkpe), c, s, sign)
        dcqn = _mm_nt(dqu_ref[...], wq_ref[...])
        dckvn = _mm_nt(dkvu_ref[...], wkv_ref[...])
        gq = gq_ref[...]
        gkv = gkv_ref[...]
        dcq, dgq_rows = _rms_bwd(cq_ref[...], gq, dcqn)
        dckv, dgkv_rows = _rms_bwd(ckv_ref[...], gkv, dckvn)
        dz_ref[:, 0:256] = dcq.astype(dz_ref.dtype)
        dz_ref[:, 256:384] = dckv.astype(dz_ref.dtype)
        dz_ref[:, 384:512] = dkpe.astype(dz_ref.dtype)
        dz_ref[:, 512:1024] = dqs_ref[...]
        dz_ref[:, 1024:1536] = dkd_ref[...]
        dz_ref[:, 1536:2048] = dvd_ref[...]
        dz_ref[:, 2048:3072] = dgate_ref[...]
        dh = _mm_nt(dz_ref[...], w_ref[...])
        g = g_ref[...]
        dxn, dg_rows = _rms_bwd(x_ref[...], g, dh)
        dx_ref[...] = dx1_ref[...] + dxn

        @pl.when(i == 0)
        def _():
            dgin_ref[...] = jnp.zeros_like(dgin_ref)
            dgq_ref[...] = jnp.zeros_like(dgq_ref)
            dgkv_ref[...] = jnp.zeros_like(dgkv_ref)

        dgin_ref[...] += jnp.sum(dg_rows, axis=0, keepdims=True)
        dgq_ref[...] += jnp.sum(dgq_rows, axis=0, keepdims=True)
        dgkv_ref[...] += jnp.sum(dgkv_rows, axis=0, keepdims=True)

    return _pcall(
        body, name="layer0_in_bwd", grid=(S // TOK,), semantics=("arbitrary",),
        in_specs=[_rows(TOK, 1024), _rows(TOK, 1024), _rows(TOK, 512), _rows(TOK, 512), _rows(TOK, 512), _rows(TOK, 512),
                  _rows(TOK, D), _rows(TOK, LANES), _rows(TOK, LANES), _full((8, LANES)), _rows(TOK, 256), _rows(TOK, 128),
                  _rows(TOK, D), _rows(TOK, D), _full((1, D)), _full(w_in.shape), _full((1, 256)), _full(w_q.shape),
                  _full((1, 128)), _full(w_kv.shape)],
        out_specs=[_rows(TOK, D), _rows(TOK, 3072), _rows(TOK, 1024), _rows(TOK, 1536), _full((1, D)), _full((1, 256)),
                   _full((1, 128))],
        out_shape=[_sds((S, D), jnp.float32), _sds((S, 3072), MXU), _sds((S, 1024), MXU), _sds((S, 1536), MXU),
                   _sds((1, D), jnp.float32), _sds((1, 256), jnp.float32), _sds((1, 128), jnp.float32)],
    )(dqm, dkm, dvm, dqs, dkd, dvd, dgate0, cos, sin, consts, cq, ckv, x, dx1, g_in, w_in, g_q, w_q, g_kv, w_kv)


def _wgrad(a, b, name):
    S, M = a.shape
    N = b.shape[1]
    tn = 512 if N % 512 == 0 else (384 if N % 384 == 0 else LANES)
    tk = min(512, S)

    def body(a_ref, b_ref, o_ref):
        @pl.when(pl.program_id(1) == 0)
        def _():
            o_ref[...] = jnp.zeros_like(o_ref)

        o_ref[...] += _mm_tn(a_ref[...], b_ref[...])

    return _pcall(
        body, name=name, grid=(N // tn, S // tk), semantics=("parallel", "arbitrary"),
        in_specs=[pl.BlockSpec((tk, M), lambda n, k: (k, 0)), pl.BlockSpec((tk, tn), lambda n, k: (k, n))],
        out_specs=pl.BlockSpec((M, tn), lambda n, k: (0, n)),
        out_shape=_sds((M, N), jnp.float32),
    )(a, b)


def _adamw(w, g, m, v, name):
    shape = w.shape
    R, C = (int(np.prod(shape[:-1])), shape[-1])
    w2, g2, m2, v2 = (t.reshape(R, C) for t in (w, g, m, v))
    tr = 256 if R % 256 == 0 else R

    def body(w_ref, g_ref, m_ref, v_ref, d_ref, nm_ref, nv_ref):
        gg = g_ref[...]
        nm = B1 * m_ref[...] + (1.0 - B1) * gg
        nv = B2 * v_ref[...] + (1.0 - B2) * (gg * gg)
        m_hat = nm / (1.0 - B1 ** STEP)
        v_hat = nv / (1.0 - B2 ** STEP)
        d_ref[...] = -LR * (m_hat / (jnp.sqrt(v_hat) + AEPS) + WD * w_ref[...])
        nm_ref[...] = nm
        nv_ref[...] = nv

    spec = _rows(tr, C)
    d, nm, nv = _pcall(
        body, name=name, grid=(R // tr,), semantics=("parallel",),
        in_specs=[spec] * 4, out_specs=[spec] * 3, out_shape=[_sds((R, C), jnp.float32)] * 3,
    )(w2, g2, m2, v2)
    return d.reshape(shape), nm.reshape(shape), nv.reshape(shape)


def _sum_leading(a, name):
    n, R, C = a.shape
    tr = 136 if R % 136 == 0 else R

    def body(a_ref, o_ref):
        acc = a_ref[0]
        for i in range(1, n):
            acc = acc + a_ref[i]
        o_ref[...] = acc

    return _pcall(
        body, name=name, grid=(R // tr,), semantics=("parallel",),
        in_specs=[pl.BlockSpec((n, tr, C), lambda i: (0, i, 0))], out_specs=_rows(tr, C),
        out_shape=_sds((R, C), a.dtype),
    )(a)


def _add_blocks(a, b, name):
    n, R, C = a.shape
    tr = 136 if R % 136 == 0 else R

    def body(a_ref, b_ref, o_ref):
        o_ref[...] = a_ref[...] + b_ref[...]

    spec = pl.BlockSpec((1, tr, C), lambda k, i: (k, i, 0))
    return _pcall(
        body, name=name, grid=(n, R // tr), semantics=("parallel", "parallel"),
        in_specs=[spec, spec], out_specs=spec, out_shape=_sds(a.shape, a.dtype),
    )(a, b)


def _place():
    return lax.axis_index("x"), lax.axis_index("y"), lax.axis_index("c")


def _all_gather8(block, name):
    R, C = block.shape

    def body(x_ref, out_ref, send_sems, recv_sems, local_sem):
        x, y, c = _place()
        me, sibling = (x, y, c), (x, y, 1 - c)
        chips = [(1 - x, y), (x, 1 - y), (1 - x, 1 - y)]

        def slot(px, py, pc):
            return out_ref.at[4 * px + 2 * py + pc]

        def copy(k, blk, to, src=None):
            return pltpu.make_async_remote_copy(
                src_ref=slot(*blk) if src is None else src, dst_ref=slot(*blk),
                send_sem=send_sems.at[k], recv_sem=recv_sems.at[k], device_id=to, device_id_type=MESH_ID)

        mine = pltpu.make_async_copy(x_ref, slot(*me), local_sem)
        mine.start()
        first = [copy(0, me, sibling, src=x_ref)]
        first += [copy(1 + j, me, (*chip, c), src=x_ref) for j, chip in enumerate(chips)]
        for cp in first:
            cp.start()
        passed = [copy(4 + j, (*chip, c), sibling) for j, chip in enumerate(chips)]
        for j, chip in enumerate(chips):
            copy(1 + j, (*chip, c), me).wait_recv()
            passed[j].start()
        copy(0, sibling, me).wait_recv()
        for j, chip in enumerate(chips):
            copy(4 + j, (*chip, 1 - c), me).wait_recv()
        for cp in first + passed:
            cp.wait_send()
        mine.wait()

    any_spec = pl.BlockSpec(memory_space=pl.ANY)
    return _pcall(
        body, name=name, in_specs=[any_spec], out_specs=any_spec, out_shape=_sds((8, R, C), block.dtype),
        scratch_shapes=[pltpu.SemaphoreType.DMA((7,)), pltpu.SemaphoreType.DMA((7,)), pltpu.SemaphoreType.DMA],
    )(block)


def _pair_swap(g2, name):
    shape = g2.shape[1:]

    def body(g_ref, out_ref, send_sem, recv_sem):
        x, y, c = _place()
        cp = pltpu.make_async_remote_copy(src_ref=g_ref.at[1 - c], dst_ref=out_ref, send_sem=send_sem,
                                          recv_sem=recv_sem, device_id=(x, y, 1 - c), device_id_type=MESH_ID)
        cp.start()
        cp.wait()

    any_spec = pl.BlockSpec(memory_space=pl.ANY)
    return _pcall(
        body, name=name, in_specs=[any_spec], out_specs=any_spec, out_shape=_sds(shape, g2.dtype),
        scratch_shapes=[pltpu.SemaphoreType.DMA, pltpu.SemaphoreType.DMA],
    )(g2)


def _chip_all_to_all(p, name):
    def body(p_ref, out_ref, send_sems, recv_sems, local_sem):
        x, y, c = _place()
        mychip = 2 * x + y
        chips = [(1 - x, y), (x, 1 - y), (1 - x, 1 - y)]
        mine = pltpu.make_async_copy(p_ref.at[mychip], out_ref.at[mychip], local_sem)
        mine.start()
        cps = [pltpu.make_async_remote_copy(
            src_ref=p_ref.at[2 * cx + cy], dst_ref=out_ref.at[mychip], send_sem=send_sems.at[j],
            recv_sem=recv_sems.at[j], device_id=(cx, cy, c), device_id_type=MESH_ID)
            for j, (cx, cy) in enumerate(chips)]
        for cp in cps:
            cp.start()
        for j, (cx, cy) in enumerate(chips):
            pltpu.make_async_remote_copy(
                src_ref=p_ref.at[mychip], dst_ref=out_ref.at[2 * cx + cy], send_sem=send_sems.at[j],
                recv_sem=recv_sems.at[j], device_id=(cx, cy, c), device_id_type=MESH_ID).wait_recv()
        for cp in cps:
            cp.wait_send()
        mine.wait()

    any_spec = pl.BlockSpec(memory_space=pl.ANY)
    return _pcall(
        body, name=name, in_specs=[any_spec], out_specs=any_spec, out_shape=_sds(p.shape, p.dtype),
        scratch_shapes=[pltpu.SemaphoreType.DMA((3,)), pltpu.SemaphoreType.DMA((3,)), pltpu.SemaphoreType.DMA],
    )(p)


def _pair_gather(t, name):
    def body(t_ref, out_ref, send_sem, recv_sem, local_sem):
        x, y, c = _place()
        mine = pltpu.make_async_copy(t_ref, out_ref.at[c], local_sem)
        mine.start()
        cp = pltpu.make_async_remote_copy(src_ref=t_ref, dst_ref=out_ref.at[c], send_sem=send_sem, recv_sem=recv_sem,
                                          device_id=(x, y, 1 - c), device_id_type=MESH_ID)
        cp.start()
        pltpu.make_async_remote_copy(src_ref=t_ref, dst_ref=out_ref.at[1 - c], send_sem=send_sem, recv_sem=recv_sem,
                                     device_id=(x, y, 1 - c), device_id_type=MESH_ID).wait_recv()
        cp.wait_send()
        mine.wait()

    any_spec = pl.BlockSpec(memory_space=pl.ANY)
    return _pcall(
        body, name=name, in_specs=[any_spec], out_specs=any_spec, out_shape=_sds((2,) + t.shape, t.dtype),
        scratch_shapes=[pltpu.SemaphoreType.DMA, pltpu.SemaphoreType.DMA, pltpu.SemaphoreType.DMA],
    )(t)


def _prep_w_in0(w):
    z = jnp.zeros((w.shape[0], 32), w.dtype)
    z64 = jnp.zeros((w.shape[0], 64), w.dtype)
    k0, k1 = w[:, 928:992], w[:, 992:1056]
    v0, v1 = w[:, 1056:1120], w[:, 1120:1184]
    return jnp.concatenate([w[:, 0:384], z64, w[:, 384:416], z, w[:, 416:928],
                            k0, k0, k0, k0, k1, k1, k1, k1, v0, v0, v0, v0, v1, v1, v1, v1, w[:, 1184:2208]], axis=1)


def _fold_w_in0(d):
    def fold(blk):
        b = blk.reshape(blk.shape[0], 8, 64)
        return jnp.concatenate([b[:, 0] + b[:, 1] + b[:, 2] + b[:, 3], b[:, 4] + b[:, 5] + b[:, 6] + b[:, 7]], axis=1)
    return jnp.concatenate([d[:, 0:384], d[:, 448:480], d[:, 512:1024], fold(d[:, 1024:1536]), fold(d[:, 1536:2048]),
                            d[:, 2048:3072]], axis=1)


def _prep_w_q(w):
    return jnp.pad(w.reshape(Q_RANK, N_MLA, 96), ((0, 0), (0, 0), (0, 32))).reshape(Q_RANK, 1024)


def _fold_w_q(d):
    return d.reshape(Q_RANK, N_MLA, 128)[:, :, :96].reshape(Q_RANK, 768)


def _prep_w_kv(w):
    w3 = w.reshape(KV_RANK, N_MLA, 128)
    kk = jnp.pad(w3[:, :, :64], ((0, 0), (0, 0), (0, 64))).reshape(KV_RANK, 1024)
    return jnp.concatenate([kk, w3[:, :, 64:].reshape(KV_RANK, 512)], axis=1)


def _fold_w_kv(d):
    kk = d[:, :1024].reshape(KV_RANK, N_MLA, 128)[:, :, :64]
    vv = d[:, 1024:].reshape(KV_RANK, N_MLA, 64)
    return jnp.concatenate([kk, vv], axis=2).reshape(KV_RANK, 1024)


def _prep_w_in1(w):
    return jnp.concatenate([w[:, 0:3072], w[:, 3088:4112], w[:, 3072:3088], jnp.zeros((w.shape[0], 112), w.dtype)], axis=1)


def _fold_w_in1(d):
    return jnp.concatenate([d[:, 0:3072], d[:, 4096:4112], d[:, 3072:4096]], axis=1)


def _local_step(x, pos, target, e_g_in, w_in0, e_g_q, w_q, e_g_kv, w_kv, sinks, w_out0, o_g_in, w_in1, b_f, w_out1, g_final):
    S = x.shape[0]
    w_in0p, w_qp, w_kvp, w_in1p = _prep_w_in0(w_in0), _prep_w_q(w_q), _prep_w_kv(w_kv), _prep_w_in1(w_in1)
    slopes = jnp.asarray(2.0 ** (-8.0 * (np.arange(N_SWA, dtype=np.float32) + 1.0) / N_SWA), jnp.float32)
    sinks1 = sinks.reshape(N_SWA)
    b_col = b_f.reshape(N_FOX, 1)

    (h0, cq, ckv, cqn, ckvn, qm, km, vm, qs, kd, vd, gate0, cos, sin) = _layer0_in(
        x, pos, e_g_in, w_in0p, e_g_q, w_qp, e_g_kv, w_kvp)
    o_m, lse_m = _attn_fwd(qm, km, vm, (NOPE + ROPE) ** -0.5, split=True, name="mla_fwd")
    o_s, lse_s = _swa_fwd(qs, kd, vd, sinks1, slopes)
    x1, u0, h1, q1, k1, v1, gate1, f_slab = _layer0_out_layer1_in(x, o_m, o_s, gate0, w_out0, o_g_in, w_in1p)
    f_row = f_slab[:, :N_FOX].T
    lc_row = _forget_fwd(f_row, b_col)
    lcc = lc_row.reshape(N_FOX, S, 1)
    lcr = lc_row.reshape(N_FOX // 2, 2, S)
    o1, lse1 = _attn_fwd(q1, k1, v1, HEAD ** -0.5, split=False, name="fox_fwd", lcc=lcc, lcr=lcr)
    loss8, dg_final, dx2, u1, do1, dgate1 = _head(x1, o1, gate1, w_out1, g_final, target)

    dq1, dk1, dv1, dlc, drow = _attn_bwd(q1, k1, v1, do1, o1, lse1, HEAD ** -0.5, split=False, name="fox_bwd", lcc=lcc, lcr=lcr)
    df_row, db_f = _forget_bwd(dlc.reshape(N_FOX, S), drow.reshape(N_FOX, S), f_row, b_col)
    df_slab = jnp.pad(df_row.T, ((0, 0), (0, LANES - N_FOX))).astype(MXU)
    dz1, dx1, dg_o_in, do_m, do_s, dgate0 = _layer1_in_bwd(
        dq1, dk1, dv1, dgate1, df_slab, x1, dx2, o_g_in, w_in1p, gate0, o_m, o_s, w_out0)
    dqs, dkd, dvd, dsink = _swa_bwd(qs, kd, vd, do_s, o_s, lse_s, sinks1, slopes)
    dqm, dkm, dvm = _attn_bwd(qm, km, vm, do_m, o_m, lse_m, (NOPE + ROPE) ** -0.5, split=True, name="mla_bwd")
    dx, dz0, dqu, dkvu, dg_in, dg_q, dg_kv = _layer0_in_bwd(
        dqm, dkm, dvm, dqs, dkd, dvd, dgate0, cos, sin, cq, ckv, x, dx1, e_g_in, w_in0p, e_g_q, w_qp, e_g_kv, w_kvp)

    grads = dict(
        e_g_in=dg_in,
        e_w_in=_fold_w_in0(_wgrad(h0, dz0, "wgrad_in0")),
        e_g_q_a=dg_q,
        e_w_q_up=_fold_w_q(_wgrad(cqn, dqu, "wgrad_q_up")),
        e_g_kv_a=dg_kv,
        e_w_kv_up=_fold_w_kv(_wgrad(ckvn, dkvu, "wgrad_kv_up")),
        e_sinks=dsink[:, 0:2, 0].reshape(1, N_SWA),
        e_w_out=_wgrad(u0, dx1, "wgrad_out0"),
        o_g_in=dg_o_in,
        o_w_in=_fold_w_in1(_wgrad(h1, dz1, "wgrad_in1")),
        o_b_f=db_f.reshape(1, N_FOX),
        o_w_out=_wgrad(u1, dx2, "wgrad_out1"),
        g_final=dg_final,
    )
    return loss8[0, 0], dx, grads


SHARDED = ("e_w_in", "e_w_q_up", "e_w_kv_up", "e_w_out", "o_g_in", "o_w_in", "o_w_out")
COL_SHARDED = ("e_w_in", "e_w_q_up", "e_w_kv_up", "o_g_in", "o_w_in")
REPLICATED = ("e_g_in", "e_g_q_a", "e_g_kv_a", "e_sinks", "o_b_f", "g_final")
FULL_SHAPES = dict(e_w_in=(1024, 2208), e_w_q_up=(256, 768), e_w_kv_up=(128, 1024), e_w_out=(1024, 1024),
                   o_g_in=(1, 1024), o_w_in=(1024, 4112), o_w_out=(1024, 1024))
PACK_ELEMS = PACK_ROWS * PACK_COLS


def _shard_shape(name):
    r, c = FULL_SHAPES[name]
    return (r, c // 4) if name in COL_SHARDED else (r // 4, c)


def _pack_flat(parts, dtype):
    flat = jnp.concatenate([p.reshape(-1).astype(dtype) for p in parts])
    return jnp.pad(flat, (0, PACK_ELEMS - flat.shape[0]))


def _unshard(name, per_chip):
    r, c = FULL_SHAPES[name]
    if name in COL_SHARDED:
        return jnp.transpose(per_chip, (1, 0, 2)).reshape(r, c)
    return per_chip.reshape(r, c)


def _to_shards(name, full):
    r, c = FULL_SHAPES[name]
    if name in COL_SHARDED:
        return jnp.transpose(full.reshape(r, 4, c // 4), (1, 0, 2)).reshape(4, -1)
    return full.reshape(4, -1)


def kernel(x, positions, e_g_in, e_w_in, e_g_q_a, e_w_q_up, e_g_kv_a, e_w_kv_up, e_sinks, e_w_out, o_g_in, o_w_in, o_b_f, o_w_out, g_final, loss_target, m_e_g_in, m_e_w_in, m_e_g_q_a, m_e_w_q_up, m_e_g_kv_a, m_e_w_kv_up, m_e_sinks, m_e_w_out, m_o_g_in, m_o_w_in, m_o_b_f, m_o_w_out, m_g_final, v_e_g_in, v_e_w_in, v_e_g_q_a, v_e_w_q_up, v_e_g_kv_a, v_e_w_kv_up, v_e_sinks, v_e_w_out, v_o_g_in, v_o_w_in, v_o_b_f, v_o_w_out, v_g_final):
    w = dict(e_g_in=e_g_in, e_w_in=e_w_in, e_g_q_a=e_g_q_a, e_w_q_up=e_w_q_up, e_g_kv_a=e_g_kv_a, e_w_kv_up=e_w_kv_up,
             e_sinks=e_sinks, e_w_out=e_w_out, o_g_in=o_g_in, o_w_in=o_w_in, o_b_f=o_b_f, o_w_out=o_w_out, g_final=g_final)
    m = dict(e_g_in=m_e_g_in, e_w_in=m_e_w_in, e_g_q_a=m_e_g_q_a, e_w_q_up=m_e_w_q_up, e_g_kv_a=m_e_g_kv_a,
             e_w_kv_up=m_e_w_kv_up, e_sinks=m_e_sinks, e_w_out=m_e_w_out, o_g_in=m_o_g_in, o_w_in=m_o_w_in, o_b_f=m_o_b_f,
             o_w_out=m_o_w_out, g_final=m_g_final)
    v = dict(e_g_in=v_e_g_in, e_w_in=v_e_w_in, e_g_q_a=v_e_g_q_a, e_w_q_up=v_e_w_q_up, e_g_kv_a=v_e_g_kv_a,
             e_w_kv_up=v_e_w_kv_up, e_sinks=v_e_sinks, e_w_out=v_e_w_out, o_g_in=v_o_g_in, o_w_in=v_o_w_in, o_b_f=v_o_b_f,
             o_w_out=v_o_w_out, g_final=v_g_final)
    order = ("e_g_in", "e_w_in", "e_g_q_a", "e_w_q_up", "e_g_kv_a", "e_w_kv_up", "e_sinks", "e_w_out", "o_g_in", "o_w_in",
             "o_b_f", "o_w_out", "g_final")
    c = lax.axis_index("c")
    chip = 2 * lax.axis_index("x") + lax.axis_index("y")

    parts = []
    for n in SHARDED:
        a = w[n][0] if w[n].ndim == 3 else w[n]
        if n == "o_g_in":
            parts.append(lax.bitcast_convert_type(a, jnp.bfloat16))
        else:
            parts.append(a)
    flat = _pack_flat(parts, jnp.bfloat16).reshape(2, HALF_ROWS, PACK_COLS)
    gathered = _all_gather8(lax.dynamic_index_in_dim(flat, c, 0, keepdims=False), "gather_weights")
    per_chip = gathered.reshape(4, PACK_ELEMS)
    full, off = {}, 0
    for n in SHARDED:
        r, cc = _shard_shape(n)
        cnt = r * cc * (2 if n == "o_g_in" else 1)
        seg = per_chip[:, off:off + cnt]
        off += cnt
        if n == "o_g_in":
            seg = lax.bitcast_convert_type(seg.reshape(4, r, cc, 2), jnp.float32)
            full[n] = _unshard(n, seg)
        else:
            full[n] = _unshard(n, seg.reshape(4, r, cc)).astype(MXU)

    loss_part, dx, grads = _local_step(
        x[0], positions.reshape(-1, 1), loss_target[0], e_g_in, full["e_w_in"], e_g_q_a, full["e_w_q_up"], e_g_kv_a,
        full["e_w_kv_up"], e_sinks, full["e_w_out"], full["o_g_in"], full["o_w_in"], o_b_f, full["o_w_out"],
        g_final.reshape(1, D))
    loss = lax.psum(loss_part, ("x", "y", "c"))

    shards = jnp.concatenate([_to_shards(n, grads[n]) for n in SHARDED], axis=1)
    shards = jnp.pad(shards, ((0, 0), (0, PACK_ELEMS - shards.shape[1])))
    g2 = jnp.transpose(shards.reshape(4, 2, HALF_ROWS, PACK_COLS), (1, 0, 2, 3))
    theirs = _pair_swap(g2, "grad_pair_swap")
    mine = lax.dynamic_index_in_dim(g2, c, 0, keepdims=False)
    chip_sum = _add_blocks(mine, theirs, "grad_pair_add")
    parts4 = _chip_all_to_all(chip_sum, "grad_chip_all_to_all")
    total_half = _sum_leading(parts4, "grad_chip_sum")
    total = _pair_gather(total_half, "grad_pair_gather").reshape(PACK_ELEMS)
    gsum, off = {}, 0
    for n in SHARDED:
        r, cc = _shard_shape(n)
        gsum[n] = total[off:off + r * cc].reshape(w[n].shape)
        off += r * cc

    small = jnp.concatenate([jnp.pad(grads[n].reshape(-1), (0, (-grads[n].size) % LANES)) for n in REPLICATED])
    rows = small.shape[0] // LANES
    small = jnp.pad(small.reshape(rows, LANES), ((0, (-rows) % 8), (0, 0)))
    ssum = _sum_leading(_all_gather8(small, "gather_small_grads"), "small_grad_sum").reshape(-1)
    off = 0
    for n in REPLICATED:
        cnt = w[n].size
        gsum[n] = ssum[off:off + cnt].reshape(w[n].shape)
        off += cnt + (-cnt) % LANES

    delta, new_m, new_v = {}, {}, {}
    for n in order:
        delta[n], new_m[n], new_v[n] = _adamw(w[n], gsum[n], m[n], v[n], "adamw_" + n)
    return (loss, dx[None], *[gsum[n] for n in order], *[delta[n] for n in order], *[new_m[n] for n in order],
            *[new_v[n] for n in order])
```

```python
import functools
import math

import numpy as np
import jax
import jax.numpy as jnp
from jax import lax
from jax.experimental import pallas as pl
from jax.experimental.pallas import tpu as pltpu

D = 1024
EPS = 1e-6
ROPE_THETA = 10000.0
N_MLA = 8
Q_RANK = 256
KV_RANK = 128
NOPE = 64
ROPE = 32
N_SWA = 8
WINDOW = 128
N_FOX = 16
HEAD = 64
E_SPLITS = (256, 128, 32, 512, 128, 128, 1024)
O_SPLITS = (1024, 1024, 1024, 16, 1024)
LR, B1, B2, AEPS, WD, STEP = 0.001, 0.9, 0.999, 1e-08, 0.01, 10

LANES = 128
HALF = 64
VMEM_LIMIT = 56 * 1024 * 1024
MXU = jnp.bfloat16
TOK = 256
ATT = 256
NEG = float("-inf")

PACK_COLS = 1664
PACK_ROWS = 1568
HALF_ROWS = PACK_ROWS // 2
SUM_ROWS = 112
MESH_ID = pl.DeviceIdType.MESH


def _pcall(body, *, name, vmem=VMEM_LIMIT, semantics=None, **kw):
    params = dict(vmem_limit_bytes=vmem)
    if semantics is not None:
        params["dimension_semantics"] = semantics
    return pl.pallas_call(body, name=name, compiler_params=pltpu.CompilerParams(**params), **kw)


def _mm(a, b):
    return jnp.dot(a.astype(MXU), b.astype(MXU), preferred_element_type=jnp.float32)


def _mm_nt(a, b):
    return lax.dot_general(a.astype(MXU), b.astype(MXU), (((1,), (1,)), ((), ())),
                           preferred_element_type=jnp.float32)


def _mm_tn(a, b):
    return lax.dot_general(a.astype(MXU), b.astype(MXU), (((0,), (0,)), ((), ())),
                           preferred_element_type=jnp.float32)


def _full(shape):
    n = len(shape)
    return pl.BlockSpec(shape, lambda *_: (0,) * n)


def _rows(tm, n):
    return pl.BlockSpec((tm, n), lambda i: (i, 0))


def _sds(shape, dtype):
    return jax.ShapeDtypeStruct(shape, dtype)


def _rms(x, g):
    r = lax.rsqrt(jnp.mean(x * x, axis=-1, keepdims=True) + EPS)
    return x * r * g


def _rms_bwd(x, g, dy):
    r = lax.rsqrt(jnp.mean(x * x, axis=-1, keepdims=True) + EPS)
    xh = x * r
    dxh = dy * g
    dx = r * (dxh - xh * jnp.mean(dxh * xh, axis=-1, keepdims=True))
    return dx, dy * xh


def _sigmoid(x):
    return 1.0 / (1.0 + jnp.exp(-x))


def _lane_masks(dtype=None):
    lane = lax.broadcasted_iota(jnp.int32, (1, LANES), 1)
    return lane < HALF


def _split_heads(a, lo):
    z = jnp.zeros_like(a)
    return [jnp.where(lo, a, z), jnp.where(lo, z, a)]


def _rope_consts():
    inv = np.zeros((8, LANES), np.float32)
    j = np.arange(ROPE // 2, dtype=np.float32)
    f = (1.0 / (ROPE_THETA ** (np.arange(0, ROPE, 2, dtype=np.float32) / ROPE))).astype(np.float32)
    inv[0, HALF:HALF + 16] = f
    inv[0, HALF + 16:HALF + 32] = f
    inv[1, HALF:HALF + 16] = -1.0
    inv[1, HALF + 16:HALF + 32] = 1.0
    del j
    return jnp.asarray(inv)


def _rope_tables(pos_f, consts):
    ang = pos_f * consts[0:1, :]
    sign = consts[1:2, :]
    c = jnp.where(sign != 0.0, jnp.cos(ang), 1.0)
    s = jnp.sin(ang) * sign
    return c, s


def _swap_halves(v, sign):
    lo = pltpu.roll(v, LANES - 16, axis=1)
    hi = pltpu.roll(v, 16, axis=1)
    return jnp.where(sign < 0.0, lo, jnp.where(sign > 0.0, hi, 0.0))


def _rope(x, c, s, sign):
    return x * c + _swap_halves(x, sign) * s


def _rope_t(dy, c, s, sign):
    return dy * c + _swap_halves(dy * s, sign)


def _layer0_in(x, pos, g_in, w_in, g_q, w_q, g_kv, w_kv):
    S = x.shape[0]
    consts = _rope_consts()

    def body(x_ref, pos_ref, c_ref, g_ref, w_ref, gq_ref, wq_ref, gkv_ref, wkv_ref,
             h_ref, cq_ref, ckv_ref, cqn_ref, ckvn_ref, qm_ref, km_ref, vm_ref,
             qs_ref, kd_ref, vd_ref, gate_ref, cos_ref, sin_ref):
        h = _rms(x_ref[...], g_ref[...])
        h_ref[...] = h.astype(h_ref.dtype)
        z = _mm(h, w_ref[...])
        cq = z[:, 0:256]
        ckv = z[:, 256:384]
        kpe = z[:, 384:512]
        cq_ref[...] = cq
        ckv_ref[...] = ckv
        qs_ref[...] = z[:, 512:1024].astype(qs_ref.dtype)
        kd_ref[...] = z[:, 1024:1536].astype(kd_ref.dtype)
        vd_ref[...] = z[:, 1536:2048].astype(vd_ref.dtype)
        gate_ref[...] = z[:, 2048:3072]
        cqn = _rms(cq, gq_ref[...])
        ckvn = _rms(ckv, gkv_ref[...])
        cqn_ref[...] = cqn.astype(cqn_ref.dtype)
        ckvn_ref[...] = ckvn.astype(ckvn_ref.dtype)
        q = _mm(cqn, wq_ref[...])
        kv = _mm(ckvn, wkv_ref[...])
        vm_ref[...] = kv[:, 1024:1536].astype(vm_ref.dtype)
        consts_v = c_ref[...]
        sign = consts_v[1:2, :]
        c, s = _rope_tables(pos_ref[...].astype(jnp.float32), consts_v)
        cos_ref[...] = c
        sin_ref[...] = s
        kpe_r = _rope(kpe, c, s, sign)
        for hd in range(N_MLA):
            sl = slice(LANES * hd, LANES * (hd + 1))
            qm_ref[:, sl] = _rope(q[:, sl], c, s, sign).astype(qm_ref.dtype)
            km_ref[:, sl] = (kv[:, sl] + kpe_r).astype(km_ref.dtype)

    outs = [
        ((S, D), MXU), ((S, 256), jnp.float32), ((S, 128), jnp.float32), ((S, 256), MXU), ((S, 128), MXU),
        ((S, 1024), MXU), ((S, 1024), MXU), ((S, 512), MXU), ((S, 512), MXU), ((S, 512), MXU), ((S, 512), MXU),
        ((S, 1024), jnp.float32), ((S, 128), jnp.float32), ((S, 128), jnp.float32),
    ]
    return _pcall(
        body, name="layer0_in", grid=(S // TOK,), semantics=("arbitrary",),
        in_specs=[_rows(TOK, D), _rows(TOK, 1), _full((8, LANES)), _full((1, D)), _full(w_in.shape), _full((1, 256)),
                  _full(w_q.shape), _full((1, 128)), _full(w_kv.shape)],
        out_specs=[_rows(TOK, s[1]) for s, _ in outs],
        out_shape=[_sds(s, d) for s, d in outs],
    )(x, pos, consts, g_in, w_in, g_q, w_q, g_kv, w_kv)


def _attn_fwd(q, k, v, scale, *, split, name, lcc=None, lcr=None):
    S = q.shape[0]
    npair = v.shape[1] // LANES
    W = 2 * LANES if split else LANES
    T = min(ATT, S)
    nq = S // T
    bias = lcc is not None

    def body(*refs):
        if bias:
            q_ref, k_ref, v_ref, lcc_ref, lcr_ref, o_ref, lse_ref = refs
        else:
            q_ref, k_ref, v_ref, o_ref, lse_ref = refs
        lo = _lane_masks()
        row = lax.broadcasted_iota(jnp.int32, (T, T), 0)
        col = lax.broadcasted_iota(jnp.int32, (T, T), 1)
        causal = row >= col

        def heads(blk):
            if split:
                return [blk[:, :LANES], blk[:, LANES:]]
            return _split_heads(blk, lo)

        def q_block(qi, carry0):
            q0 = pl.multiple_of(qi * T, T)
            qs = heads(q_ref[pl.ds(q0, T), :])
            lq = [lcc_ref[h, pl.ds(q0, T), :] for h in (0, 1)] if bias else None

            def step(ki, carry, diag):
                k0 = pl.multiple_of(ki * T, T)
                kblk = k_ref[pl.ds(k0, T), :]
                ks = [kblk[:, :LANES], kblk[:, LANES:]] if split else [kblk, kblk]
                vs = _split_heads(v_ref[pl.ds(k0, T), :], lo)
                out = []
                for h in (0, 1):
                    m, l, acc = carry[h]
                    s = _mm_nt(qs[h], ks[h]) * scale
                    if bias:
                        s = s + lq[h] - lcr_ref[0, h:h + 1, pl.ds(k0, T)]
                    if diag:
                        s = jnp.where(causal, s, NEG)
                    mn = jnp.maximum(m, jnp.max(s, axis=1, keepdims=True))
                    a = jnp.exp(m - mn)
                    p = jnp.exp(s - mn)
                    l = a * l + jnp.sum(p, axis=1, keepdims=True)
                    acc = a * acc + _mm(p, vs[h])
                    out.append((mn, l, acc))
                return tuple(out)

            init = tuple((jnp.full((T, 1), NEG, jnp.float32), jnp.zeros((T, 1), jnp.float32),
                          jnp.zeros((T, LANES), jnp.float32)) for _ in (0, 1))
            carry = lax.fori_loop(0, qi, lambda ki, c: step(ki, c, False), init)
            carry = step(qi, carry, True)
            o = None
            for h in (0, 1):
                m, l, acc = carry[h]
                oh = acc / l
                o = oh if o is None else o + oh
                lse_ref[h, pl.ds(q0, T), :] = m + jnp.log(l)
            o_ref[pl.ds(q0, T), :] = o
            return carry0

        lax.fori_loop(0, nq, q_block, 0)

    in_specs = [pl.BlockSpec((S, W), lambda j: (0, j)), pl.BlockSpec((S, W), lambda j: (0, j)),
                pl.BlockSpec((S, LANES), lambda j: (0, j))]
    args = [q, k, v]
    if bias:
        in_specs += [pl.BlockSpec((2, S, 1), lambda j: (j, 0, 0)), pl.BlockSpec((1, 2, S), lambda j: (j, 0, 0))]
        args += [lcc, lcr]
    return _pcall(
        body, name=name, grid=(npair,), semantics=("arbitrary",),
        in_specs=in_specs,
        out_specs=[pl.BlockSpec((S, LANES), lambda j: (0, j)), pl.BlockSpec((2, S, 1), lambda j: (j, 0, 0))],
        out_shape=[_sds((S, npair * LANES), jnp.float32), _sds((2 * npair, S, 1), jnp.float32)],
    )(*args)


def _attn_bwd(q, k, v, do, o, lse, scale, *, split, name, lcc=None, lcr=None):
    S = q.shape[0]
    npair = v.shape[1] // LANES
    W = 2 * LANES if split else LANES
    T = min(ATT, S)
    nq = S // T
    bias = lcc is not None

    def body(*refs):
        if bias:
            (q_ref, k_ref, v_ref, do_ref, o_ref, lse_ref, lcc_ref, lcr_ref,
             dq_ref, dk_ref, dv_ref, dlc_ref, drow_ref, dq_acc, dl_a, dl_b, dk_acc, dv_acc, col_acc) = refs
        else:
            (q_ref, k_ref, v_ref, do_ref, o_ref, lse_ref,
             dq_ref, dk_ref, dv_ref, dq_acc, dl_a, dl_b, dk_acc, dv_acc) = refs
        dls = (dl_a, dl_b)
        lo = _lane_masks()
        row = lax.broadcasted_iota(jnp.int32, (T, T), 0)
        col = lax.broadcasted_iota(jnp.int32, (T, T), 1)
        causal = row >= col

        def heads(blk):
            if split:
                return [blk[:, :LANES], blk[:, LANES:]]
            return _split_heads(blk, lo)

        dq_acc[...] = jnp.zeros_like(dq_acc)
        if bias:
            drow_ref[...] = jnp.zeros_like(drow_ref)

        def delta_block(qi, c):
            q0 = pl.multiple_of(qi * T, T)
            prod = do_ref[pl.ds(q0, T), :].astype(jnp.float32) * o_ref[pl.ds(q0, T), :]
            ph = _split_heads(prod, lo)
            for h in (0, 1):
                dls[h][pl.ds(q0, T), :] = jnp.sum(ph[h], axis=1, keepdims=True)
            return c

        lax.fori_loop(0, nq, delta_block, 0)

        def k_block(ki, c):
            k0 = pl.multiple_of(ki * T, T)
            kblk = k_ref[pl.ds(k0, T), :]
            ks = heads(kblk)
            vblk = v_ref[pl.ds(k0, T), :]
            dk_acc[...] = jnp.zeros_like(dk_acc)
            dv_acc[...] = jnp.zeros_like(dv_acc)
            if bias:
                col_acc[...] = jnp.zeros_like(col_acc)

            def step(qi, c2, diag):
                q0 = pl.multiple_of(qi * T, T)
                qs = heads(q_ref[pl.ds(q0, T), :])
                dos = _split_heads(do_ref[pl.ds(q0, T), :], lo)
                for h in (0, 1):
                    s = _mm_nt(qs[h], ks[h]) * scale
                    if bias:
                        s = s + lcc_ref[h, pl.ds(q0, T), :] - lcr_ref[0, h:h + 1, pl.ds(k0, T)]
                    if diag:
                        s = jnp.where(causal, s, NEG)
                    p = jnp.exp(s - lse_ref[h, pl.ds(q0, T), :])
                    dv_acc[...] += _mm_tn(p, dos[h])
                    dp = _mm_nt(dos[h], vblk)
                    ds = p * (dp - dls[h][pl.ds(q0, T), :])
                    if split:
                        sl = slice(LANES * h, LANES * (h + 1))
                        dq_acc[pl.ds(q0, T), sl] += _mm(ds, ks[h]) * scale
                        dk_acc[:, sl] += _mm_tn(ds, qs[h]) * scale
                    else:
                        dq_acc[pl.ds(q0, T), :] += _mm(ds, ks[h]) * scale
                        dk_acc[...] += _mm_tn(ds, qs[h]) * scale
                    if bias:
                        col_acc[h:h + 1, :] += jnp.sum(ds, axis=0, keepdims=True)
                        drow_ref[h, pl.ds(q0, T), :] += jnp.sum(ds, axis=1, keepdims=True)
                return c2

            step(ki, 0, True)
            lax.fori_loop(ki + 1, nq, lambda qi, c2: step(qi, c2, False), 0)
            dk_ref[pl.ds(k0, T), :] = dk_acc[...].astype(dk_ref.dtype)
            dv_ref[pl.ds(k0, T), :] = dv_acc[...].astype(dv_ref.dtype)
            if bias:
                dlc_ref[0, :, pl.ds(k0, T)] = -col_acc[0:2, :]
            return c

        lax.fori_loop(0, nq, k_block, 0)
        dq_ref[...] = dq_acc[...].astype(dq_ref.dtype)

    wide = pl.BlockSpec((S, W), lambda j: (0, j))
    slab = pl.BlockSpec((S, LANES), lambda j: (0, j))
    stat = pl.BlockSpec((2, S, 1), lambda j: (j, 0, 0))
    in_specs = [wide, wide, slab, slab, slab, stat]
    args = [q, k, v, do, o, lse]
    out_specs = [wide, wide, slab]
    out_shape = [_sds(q.shape, do.dtype if not split else jnp.float32),
                 _sds(k.shape, do.dtype if not split else jnp.float32), _sds(v.shape, do.dtype)]
    scratch = [pltpu.VMEM((S, W), jnp.float32), pltpu.VMEM((S, 1), jnp.float32), pltpu.VMEM((S, 1), jnp.float32),
               pltpu.VMEM((T, W), jnp.float32), pltpu.VMEM((T, LANES), jnp.float32)]
    if bias:
        in_specs += [stat, pl.BlockSpec((1, 2, S), lambda j: (j, 0, 0))]
        args += [lcc, lcr]
        out_specs.append(pl.BlockSpec((1, 2, S), lambda j: (j, 0, 0)))
        out_shape.append(_sds((npair, 2, S), jnp.float32))
        out_specs.append(stat)
        out_shape.append(_sds((2 * npair, S, 1), jnp.float32))
        scratch.append(pltpu.VMEM((8, T), jnp.float32))
    return _pcall(
        body, name=name, grid=(npair,), semantics=("arbitrary",),
        in_specs=in_specs, out_specs=out_specs, out_shape=out_shape, scratch_shapes=scratch,
    )(*args)


def _swa_scores(qh, kblk, slope, shift):
    s = _mm_nt(qh, kblk) * (HEAD ** -0.5)
    a = lax.broadcasted_iota(jnp.int32, (WINDOW, 2 * WINDOW), 0)
    c = lax.broadcasted_iota(jnp.int32, (WINDOW, 2 * WINDOW), 1)
    dist = a - c + shift
    s = s - slope * dist.astype(jnp.float32)
    return jnp.where((dist >= 0) & (dist < WINDOW), s, NEG)


def _swa_fwd(q, kd, vd, sinks, slopes):
    S = q.shape[0]
    npair = q.shape[1] // LANES
    nb = S // WINDOW

    def body(sink_ref, slope_ref, q_ref, k_ref, v_ref, o_ref, lse_ref):
        j = pl.program_id(0)
        lo = _lane_masks()

        def q_block(qi, c):
            q0 = pl.multiple_of(qi * WINDOW, WINDOW)
            k0 = pl.multiple_of(jnp.maximum(qi - 1, 0) * WINDOW, WINDOW)
            shift = q0 - k0
            qs = _split_heads(q_ref[pl.ds(q0, WINDOW), :], lo)
            kblk = k_ref[pl.ds(k0, 2 * WINDOW), :]
            vs = _split_heads(v_ref[pl.ds(k0, 2 * WINDOW), :], lo)
            o = None
            for h in (0, 1):
                sink = sink_ref[2 * j + h]
                s = _swa_scores(qs[h], kblk, slope_ref[2 * j + h], shift)
                m = jnp.maximum(jnp.max(s, axis=1, keepdims=True), sink)
                p = jnp.exp(s - m)
                den = jnp.sum(p, axis=1, keepdims=True) + jnp.exp(sink - m)
                oh = _mm(p / den, vs[h])
                o = oh if o is None else o + oh
                lse_ref[h, pl.ds(q0, WINDOW), :] = m + jnp.log(den)
            o_ref[pl.ds(q0, WINDOW), :] = o
            return c

        lax.fori_loop(0, nb, q_block, 0)

    smem = pl.BlockSpec(memory_space=pltpu.SMEM)
    slab = pl.BlockSpec((S, LANES), lambda j: (0, j))
    return _pcall(
        body, name="swa_fwd", grid=(npair,), semantics=("arbitrary",),
        in_specs=[smem, smem, slab, slab, slab],
        out_specs=[slab, pl.BlockSpec((2, S, 1), lambda j: (j, 0, 0))],
        out_shape=[_sds((S, npair * LANES), jnp.float32), _sds((2 * npair, S, 1), jnp.float32)],
    )(sinks, slopes, q, kd, vd)


def _swa_bwd(q, kd, vd, do, o, lse, sinks, slopes):
    S = q.shape[0]
    npair = q.shape[1] // LANES
    nb = S // WINDOW

    def body(sink_ref, slope_ref, q_ref, k_ref, v_ref, do_ref, o_ref, lse_ref,
             dq_ref, dk_ref, dv_ref, dsink_ref, dk_acc, dv_acc):
        j = pl.program_id(0)
        lo = _lane_masks()
        dk_acc[...] = jnp.zeros_like(dk_acc)
        dv_acc[...] = jnp.zeros_like(dv_acc)

        def q_block(qi, carry):
            q0 = pl.multiple_of(qi * WINDOW, WINDOW)
            k0 = pl.multiple_of(jnp.maximum(qi - 1, 0) * WINDOW, WINDOW)
            shift = q0 - k0
            qs = _split_heads(q_ref[pl.ds(q0, WINDOW), :], lo)
            dos = _split_heads(do_ref[pl.ds(q0, WINDOW), :], lo)
            oblk = o_ref[pl.ds(q0, WINDOW), :]
            kblk = k_ref[pl.ds(k0, 2 * WINDOW), :]
            vblk = v_ref[pl.ds(k0, 2 * WINDOW), :]
            ks = _split_heads(kblk, lo)
            dq = None
            out = []
            for h in (0, 1):
                sink = sink_ref[2 * j + h]
                lse_h = lse_ref[h, pl.ds(q0, WINDOW), :]
                s = _swa_scores(qs[h], kblk, slope_ref[2 * j + h], shift)
                p = jnp.exp(s - lse_h)
                delta = jnp.sum(dos[h].astype(jnp.float32) * oblk, axis=1, keepdims=True)
                dv_acc[pl.ds(k0, 2 * WINDOW), :] += _mm_tn(p, dos[h])
                dp = _mm_nt(dos[h], vblk)
                ds = p * (dp - delta)
                dqh = _mm(ds, ks[h]) * (HEAD ** -0.5)
                dq = dqh if dq is None else dq + dqh
                dk_acc[pl.ds(k0, 2 * WINDOW), :] += _mm_tn(ds, qs[h]) * (HEAD ** -0.5)
                dsk = jnp.sum(-jnp.exp(sink - lse_h) * delta, axis=0, keepdims=True)
                out.append(carry[h] + dsk)
            dq_ref[pl.ds(q0, WINDOW), :] = dq.astype(dq_ref.dtype)
            return tuple(out)

        zero = jnp.zeros((1, 1), jnp.float32)
        dsa, dsb = lax.fori_loop(0, nb, q_block, (zero, zero))
        dk_ref[...] = dk_acc[...].astype(dk_ref.dtype)
        dv_ref[...] = dv_acc[...].astype(dv_ref.dtype)
        r = lax.broadcasted_iota(jnp.int32, (8, LANES), 0)
        dsink_ref[0] = jnp.where(r == 0, dsa, jnp.where(r == 1, dsb, 0.0))

    smem = pl.BlockSpec(memory_space=pltpu.SMEM)
    slab = pl.BlockSpec((S, LANES), lambda j: (0, j))
    return _pcall(
        body, name="swa_bwd", grid=(npair,), semantics=("arbitrary",),
        in_specs=[smem, smem, slab, slab, slab, slab, slab, pl.BlockSpec((2, S, 1), lambda j: (j, 0, 0))],
        out_specs=[slab, slab, slab, pl.BlockSpec((1, 8, LANES), lambda j: (j, 0, 0))],
        out_shape=[_sds(q.shape, do.dtype), _sds(kd.shape, do.dtype), _sds(vd.shape, do.dtype),
                   _sds((npair, 8, LANES), jnp.float32)],
        scratch_shapes=[pltpu.VMEM((S, LANES), jnp.float32), pltpu.VMEM((S, LANES), jnp.float32)],
    )(sinks, slopes, q, kd, vd, do, o, lse)


def _log_steps(S):
    k, out = 1, []
    while k < S:
        out.append(k)
        k *= 2
    return out


def _forget_fwd(f_row, b_col):
    S = f_row.shape[1]

    def body(f_ref, b_ref, lc_ref):
        x = f_ref[...] + b_ref[...]
        lc = jnp.minimum(x, 0.0) - jnp.log(1.0 + jnp.exp(-jnp.abs(x)))
        idx = lax.broadcasted_iota(jnp.int32, lc.shape, 1)
        for k in _log_steps(S):
            lc = lc + jnp.where(idx >= k, pltpu.roll(lc, k, axis=1), 0.0)
        lc_ref[...] = lc

    return _pcall(body, name="forget_fwd", out_shape=_sds(f_row.shape, jnp.float32))(f_row, b_col)


def _forget_bwd(dcol_row, drow_row, f_row, b_col):
    S = f_row.shape[1]

    def body(d_ref, r_ref, f_ref, b_ref, df_ref, db_ref):
        g = d_ref[...] + r_ref[...]
        idx = lax.broadcasted_iota(jnp.int32, g.shape, 1)
        for k in _log_steps(S):
            g = g + jnp.where(idx < S - k, pltpu.roll(g, S - k, axis=1), 0.0)
        x = f_ref[...] + b_ref[...]
        df = g * _sigmoid(-x)
        df_ref[...] = df
        db_ref[...] = jnp.sum(df, axis=1, keepdims=True)

    return _pcall(body, name="forget_bwd",
                  out_shape=[_sds(f_row.shape, jnp.float32), _sds((f_row.shape[0], 1), jnp.float32)])(dcol_row, drow_row, f_row, b_col)


def _layer0_out_layer1_in(x, o_m, o_s, gate, w_out, g1, w_in1):
    S = x.shape[0]

    def body(x_ref, om_ref, os_ref, gate_ref, wo_ref, g_ref, w_ref,
             x1_ref, u_ref, h_ref, q_ref, k_ref, v_ref, g1_ref, f_ref):
        gt = gate_ref[...]
        sg = gt * _sigmoid(gt)
        um = om_ref[...] * sg[:, :512]
        us = os_ref[...] * sg[:, 512:]
        u_ref[:, :512] = um.astype(u_ref.dtype)
        u_ref[:, 512:] = us.astype(u_ref.dtype)
        x1 = x_ref[...] + _mm(um, wo_ref[0:512, :]) + _mm(us, wo_ref[512:1024, :])
        x1_ref[...] = x1
        h = _rms(x1, g_ref[...])
        h_ref[...] = h.astype(h_ref.dtype)
        z = _mm(h, w_ref[...])
        q_ref[...] = z[:, 0:1024].astype(q_ref.dtype)
        k_ref[...] = z[:, 1024:2048].astype(k_ref.dtype)
        v_ref[...] = z[:, 2048:3072].astype(v_ref.dtype)
        g1_ref[...] = z[:, 3072:4096]
        f_ref[...] = z[:, 4096:4224]

    outs = [((S, D), jnp.float32), ((S, D), MXU), ((S, D), MXU), ((S, D), MXU), ((S, D), MXU), ((S, D), MXU),
            ((S, D), jnp.float32), ((S, LANES), jnp.float32)]
    return _pcall(
        body, name="layer0_out_layer1_in", grid=(S // TOK,), semantics=("arbitrary",),
        in_specs=[_rows(TOK, D), _rows(TOK, 512), _rows(TOK, 512), _rows(TOK, D), _full((D, D)), _full((1, D)),
                  _full(w_in1.shape)],
        out_specs=[_rows(TOK, s[1]) for s, _ in outs],
        out_shape=[_sds(s, d) for s, d in outs],
    )(x, o_m, o_s, gate, w_out, g1, w_in1)


def _head(x1, o1, gate1, w_out1, g_f, target):
    S = x1.shape[0]

    def body(x1_ref, o_ref, gate_ref, wo_ref, g_ref, t_ref,
             loss_ref, dgf_ref, dx2_ref, u_ref, do_ref, dgate_ref):
        i = pl.program_id(0)
        gt = gate_ref[...]
        sig = _sigmoid(gt)
        sg = gt * sig
        o = o_ref[...]
        u = o * sg
        u_ref[...] = u.astype(u_ref.dtype)
        x2 = x1_ref[...] + _mm(u, wo_ref[...])
        g = g_ref[...]
        y = _rms(x2, g)
        err = y - t_ref[...]
        part = 0.5 * jnp.sum(jnp.mean(err * err, axis=-1, keepdims=True), axis=0, keepdims=True)
        dy = err * (1.0 / D)
        dx2, dg_rows = _rms_bwd(x2, g, dy)
        dx2_ref[...] = dx2
        du = _mm_nt(dx2, wo_ref[...])
        do_ref[...] = (du * sg).astype(do_ref.dtype)
        dgate_ref[...] = (du * o * (sig * (1.0 + gt * (1.0 - sig)))).astype(dgate_ref.dtype)

        @pl.when(i == 0)
        def _():
            loss_ref[...] = jnp.zeros_like(loss_ref)
            dgf_ref[...] = jnp.zeros_like(dgf_ref)

        loss_ref[...] += jnp.broadcast_to(part, loss_ref.shape)
        dgf_ref[...] += jnp.sum(dg_rows, axis=0, keepdims=True)

    outs = [((S, D), jnp.float32), ((S, D), MXU), ((S, D), MXU), ((S, D), MXU)]
    return _pcall(
        body, name="head", grid=(S // TOK,), semantics=("arbitrary",),
        in_specs=[_rows(TOK, D), _rows(TOK, D), _rows(TOK, D), _full((D, D)), _full((1, D)), _rows(TOK, D)],
        out_specs=[_full((8, LANES)), _full((1, D))] + [_rows(TOK, D) for _ in outs],
        out_shape=[_sds((8, LANES), jnp.float32), _sds((1, D), jnp.float32)] + [_sds(s, d) for s, d in outs],
    )(x1, o1, gate1, w_out1, g_f, target)


def _layer1_in_bwd(dq, dk, dv, dgate1, df, x1, dx2, g1, w_in1, gate0, o_m, o_s, w_out0):
    S = x1.shape[0]

    def body(dq_ref, dk_ref, dv_ref, dg1_ref, df_ref, x1_ref, dx2_ref, g_ref, w_ref, gate_ref, om_ref, os_ref,
             wo_ref, dz_ref, dx1_ref, dgn_ref, dom_ref, dos_ref, dgate_ref):
        i = pl.program_id(0)
        dz_ref[:, 0:1024] = dq_ref[...]
        dz_ref[:, 1024:2048] = dk_ref[...]
        dz_ref[:, 2048:3072] = dv_ref[...]
        dz_ref[:, 3072:4096] = dg1_ref[...]
        dz_ref[:, 4096:4224] = df_ref[...]
        dh = _mm_nt(dz_ref[...], w_ref[...])
        g = g_ref[...]
        dxn, dg_rows = _rms_bwd(x1_ref[...], g, dh)
        dx1 = dx2_ref[...] + dxn
        dx1_ref[...] = dx1
        du = _mm_nt(dx1, wo_ref[...])
        gt = gate_ref[...]
        sig = _sigmoid(gt)
        sg = gt * sig
        dsg = sig * (1.0 + gt * (1.0 - sig))
        dom_ref[...] = (du[:, :512] * sg[:, :512]).astype(dom_ref.dtype)
        dos_ref[...] = (du[:, 512:] * sg[:, 512:]).astype(dos_ref.dtype)
        dgate_ref[:, :512] = (du[:, :512] * om_ref[...] * dsg[:, :512]).astype(dgate_ref.dtype)
        dgate_ref[:, 512:] = (du[:, 512:] * os_ref[...] * dsg[:, 512:]).astype(dgate_ref.dtype)

        @pl.when(i == 0)
        def _():
            dgn_ref[...] = jnp.zeros_like(dgn_ref)

        dgn_ref[...] += jnp.sum(dg_rows, axis=0, keepdims=True)

    return _pcall(
        body, name="layer1_in_bwd", grid=(S // TOK,), semantics=("arbitrary",),
        in_specs=[_rows(TOK, D), _rows(TOK, D), _rows(TOK, D), _rows(TOK, D), _rows(TOK, LANES), _rows(TOK, D),
                  _rows(TOK, D), _full((1, D)), _full(w_in1.shape), _rows(TOK, D), _rows(TOK, 512), _rows(TOK, 512),
                  _full((D, D))],
        out_specs=[_rows(TOK, 4224), _rows(TOK, D), _full((1, D)), _rows(TOK, 512), _rows(TOK, 512), _rows(TOK, D)],
        out_shape=[_sds((S, 4224), MXU), _sds((S, D), jnp.float32), _sds((1, D), jnp.float32),
                   _sds((S, 512), MXU), _sds((S, 512), MXU), _sds((S, D), MXU)],
    )(dq, dk, dv, dgate1, df, x1, dx2, g1, w_in1, gate0, o_m, o_s, w_out0)


def _layer0_in_bwd(dqm, dkm, dvm, dqs, dkd, dvd, dgate0, cos, sin, cq, ckv, x, dx1, g_in, w_in, g_q, w_q, g_kv, w_kv):
    S = x.shape[0]
    consts = _rope_consts()

    def body(dqm_ref, dkm_ref, dvm_ref, dqs_ref, dkd_ref, dvd_ref, dgate_ref, cos_ref, sin_ref, c_ref, cq_ref, ckv_ref,
             x_ref, dx1_ref, g_ref, w_ref, gq_ref, wq_ref, gkv_ref, wkv_ref,
             dx_ref, dz_ref, dqu_ref, dkvu_ref, dgin_ref, dgq_ref, dgkv_ref):
        i = pl.program_id(0)
        lo = _lane_masks()
        sign = c_ref[...][1:2, :]
        c = cos_ref[...]
        s = sin_ref[...]
        dkpe = None
        for hd in range(N_MLA):
            sl = slice(LANES * hd, LANES * (hd + 1))
            dqu_ref[:, sl] = _rope_t(dqm_ref[:, sl], c, s, sign).astype(dqu_ref.dtype)
            dkh = dkm_ref[:, sl]
            dkvu_ref[:, sl] = jnp.where(lo, dkh, 0.0).astype(dkvu_ref.dtype)
            dkpe = dkh if dkpe is None else dkpe + dkh
        dkvu_ref[:, 1024:1536] = dvm_ref[...]
        dkpe = _rope_t(jnp.where(lo, 0.0, dkpe), c, s, sign)
        dcqn = _mm_nt(dqu_ref[...], wq_ref[...])
        dckvn = _mm_nt(dkvu_ref[...], wkv_ref[...])
        gq = gq_ref[...]
        gkv = gkv_ref[...]
        dcq, dgq_rows = _rms_bwd(cq_ref[...], gq, dcqn)
        dckv, dgkv_rows = _rms_bwd(ckv_ref[...], gkv, dckvn)
        dz_ref[:, 0:256] = dcq.astype(dz_ref.dtype)
        dz_ref[:, 256:384] = dckv.astype(dz_ref.dtype)
        dz_ref[:, 384:512] = dkpe.astype(dz_ref.dtype)
        dz_ref[:, 512:1024] = dqs_ref[...]
        dz_ref[:, 1024:1536] = dkd_ref[...]
        dz_ref[:, 1536:2048] = dvd_ref[...]
        dz_ref[:, 2048:3072] = dgate_ref[...]
        dh = _mm_nt(dz_ref[...], w_ref[...])
        g = g_ref[...]
        dxn, dg_rows = _rms_bwd(x_ref[...], g, dh)
        dx_ref[...] = dx1_ref[...] + dxn

        @pl.when(i == 0)
        def _():
            dgin_ref[...] = jnp.zeros_like(dgin_ref)
            dgq_ref[...] = jnp.zeros_like(dgq_ref)
            dgkv_ref[...] = jnp.zeros_like(dgkv_ref)

        dgin_ref[...] += jnp.sum(dg_rows, axis=0, keepdims=True)
        dgq_ref[...] += jnp.sum(dgq_rows, axis=0, keepdims=True)
        dgkv_ref[...] += jnp.sum(dgkv_rows, axis=0, keepdims=True)

    return _pcall(
        body, name="layer0_in_bwd", grid=(S // TOK,), semantics=("arbitrary",),
        in_specs=[_rows(TOK, 1024), _rows(TOK, 1024), _rows(TOK, 512), _rows(TOK, 512), _rows(TOK, 512), _rows(TOK, 512),
                  _rows(TOK, D), _rows(TOK, LANES), _rows(TOK, LANES), _full((8, LANES)), _rows(TOK, 256), _rows(TOK, 128),
                  _rows(TOK, D), _rows(TOK, D), _full((1, D)), _full(w_in.shape), _full((1, 256)), _full(w_q.shape),
                  _full((1, 128)), _full(w_kv.shape)],
        out_specs=[_rows(TOK, D), _rows(TOK, 3072), _rows(TOK, 1024), _rows(TOK, 1536), _full((1, D)), _full((1, 256)),
                   _full((1, 128))],
        out_shape=[_sds((S, D), jnp.float32), _sds((S, 3072), MXU), _sds((S, 1024), MXU), _sds((S, 1536), MXU),
                   _sds((1, D), jnp.float32), _sds((1, 256), jnp.float32), _sds((1, 128), jnp.float32)],
    )(dqm, dkm, dvm, dqs, dkd, dvd, dgate0, cos, sin, consts, cq, ckv, x, dx1, g_in, w_in, g_q, w_q, g_kv, w_kv)


def _wgrad(a, b, name):
    S, M = a.shape
    N = b.shape[1]
    tn = 512 if N % 512 == 0 else (384 if N % 384 == 0 else LANES)
    tk = min(512, S)

    def body(a_ref, b_ref, o_ref):
        @pl.when(pl.program_id(1) == 0)
        def _():
            o_ref[...] = jnp.zeros_like(o_ref)

        o_ref[...] += _mm_tn(a_ref[...], b_ref[...])

    return _pcall(
        body, name=name, grid=(N // tn, S // tk), semantics=("parallel", "arbitrary"),
        in_specs=[pl.BlockSpec((tk, M), lambda n, k: (k, 0)), pl.BlockSpec((tk, tn), lambda n, k: (k, n))],
        out_specs=pl.BlockSpec((M, tn), lambda n, k: (0, n)),
        out_shape=_sds((M, N), jnp.float32),
    )(a, b)


def _adamw(w, g, m, v, name):
    shape = w.shape
    R, C = (int(np.prod(shape[:-1])), shape[-1])
    w2, g2, m2, v2 = (t.reshape(R, C) for t in (w, g, m, v))
    tr = 256 if R % 256 == 0 else R

    def body(w_ref, g_ref, m_ref, v_ref, d_ref, nm_ref, nv_ref):
        gg = g_ref[...]
        nm = B1 * m_ref[...] + (1.0 - B1) * gg
        nv = B2 * v_ref[...] + (1.0 - B2) * (gg * gg)
        m_hat = nm / (1.0 - B1 ** STEP)
        v_hat = nv / (1.0 - B2 ** STEP)
        d_ref[...] = -LR * (m_hat / (jnp.sqrt(v_hat) + AEPS) + WD * w_ref[...])
        nm_ref[...] = nm
        nv_ref[...] = nv

    spec = _rows(tr, C)
    d, nm, nv = _pcall(
        body, name=name, grid=(R // tr,), semantics=("parallel",),
        in_specs=[spec] * 4, out_specs=[spec] * 3, out_shape=[_sds((R, C), jnp.float32)] * 3,
    )(w2, g2, m2, v2)
    return d.reshape(shape), nm.reshape(shape), nv.reshape(shape)


def _sum_leading(a, name):
    n, R, C = a.shape
    tr = SUM_ROWS if R % SUM_ROWS == 0 else R

    def body(a_ref, o_ref):
        acc = a_ref[0]
        for i in range(1, n):
            acc = acc + a_ref[i]
        o_ref[...] = acc

    return _pcall(
        body, name=name, grid=(R // tr,), semantics=("parallel",),
        in_specs=[pl.BlockSpec((n, tr, C), lambda i: (0, i, 0))], out_specs=_rows(tr, C),
        out_shape=_sds((R, C), a.dtype),
    )(a)


def _add_blocks(a, b, name):
    n, R, C = a.shape
    tr = SUM_ROWS if R % SUM_ROWS == 0 else R

    def body(a_ref, b_ref, o_ref):
        o_ref[...] = a_ref[...] + b_ref[...]

    spec = pl.BlockSpec((1, tr, C), lambda k, i: (k, i, 0))
    return _pcall(
        body, name=name, grid=(n, R // tr), semantics=("parallel", "parallel"),
        in_specs=[spec, spec], out_specs=spec, out_shape=_sds(a.shape, a.dtype),
    )(a, b)


def _place():
    return lax.axis_index("x"), lax.axis_index("y"), lax.axis_index("c")


def _all_gather8(block, name):
    R, C = block.shape

    def body(x_ref, out_ref, send_sems, recv_sems, local_sem):
        x, y, c = _place()
        me, sibling = (x, y, c), (x, y, 1 - c)
        chips = [(1 - x, y), (x, 1 - y), (1 - x, 1 - y)]

        def slot(px, py, pc):
            return out_ref.at[4 * px + 2 * py + pc]

        def copy(k, blk, to, src=None):
            return pltpu.make_async_remote_copy(
                src_ref=slot(*blk) if src is None else src, dst_ref=slot(*blk),
                send_sem=send_sems.at[k], recv_sem=recv_sems.at[k], device_id=to, device_id_type=MESH_ID)

        mine = pltpu.make_async_copy(x_ref, slot(*me), local_sem)
        mine.start()
        first = [copy(0, me, sibling, src=x_ref)]
        first += [copy(1 + j, me, (*chip, c), src=x_ref) for j, chip in enumerate(chips)]
        for cp in first:
            cp.start()
        passed = [copy(4 + j, (*chip, c), sibling) for j, chip in enumerate(chips)]
        for j, chip in enumerate(chips):
            copy(1 + j, (*chip, c), me).wait_recv()
            passed[j].start()
        copy(0, sibling, me).wait_recv()
        for j, chip in enumerate(chips):
            copy(4 + j, (*chip, 1 - c), me).wait_recv()
        for cp in first + passed:
            cp.wait_send()
        mine.wait()

    any_spec = pl.BlockSpec(memory_space=pl.ANY)
    return _pcall(
        body, name=name, in_specs=[any_spec], out_specs=any_spec, out_shape=_sds((8, R, C), block.dtype),
        scratch_shapes=[pltpu.SemaphoreType.DMA((7,)), pltpu.SemaphoreType.DMA((7,)), pltpu.SemaphoreType.DMA],
    )(block)


def _pair_swap(g, name):
    n = g.shape[0]

    def body(g_ref, out_ref, send_sems, recv_sems):
        x, y, c = _place()
        cps = [pltpu.make_async_remote_copy(src_ref=g_ref.at[k, 1 - c], dst_ref=out_ref.at[k], send_sem=send_sems.at[k],
                                            recv_sem=recv_sems.at[k], device_id=(x, y, 1 - c), device_id_type=MESH_ID)
               for k in range(n)]
        for cp in cps:
            cp.start()
        for cp in cps:
            cp.wait()

    any_spec = pl.BlockSpec(memory_space=pl.ANY)
    return _pcall(
        body, name=name, in_specs=[any_spec], out_specs=any_spec, out_shape=_sds((n,) + g.shape[2:], g.dtype),
        scratch_shapes=[pltpu.SemaphoreType.DMA((n,)), pltpu.SemaphoreType.DMA((n,))],
    )(g)


def _chip_all_to_all(p, name):
    def body(p_ref, out_ref, send_sems, recv_sems, local_sem):
        x, y, c = _place()
        mychip = 2 * x + y
        chips = [(1 - x, y), (x, 1 - y), (1 - x, 1 - y)]
        mine = pltpu.make_async_copy(p_ref.at[mychip], out_ref.at[mychip], local_sem)
        mine.start()
        cps = [pltpu.make_async_remote_copy(
            src_ref=p_ref.at[2 * cx + cy], dst_ref=out_ref.at[mychip], send_sem=send_sems.at[j],
            recv_sem=recv_sems.at[j], device_id=(cx, cy, c), device_id_type=MESH_ID)
            for j, (cx, cy) in enumerate(chips)]
        for cp in cps:
            cp.start()
        for j, (cx, cy) in enumerate(chips):
            pltpu.make_async_remote_copy(
                src_ref=p_ref.at[mychip], dst_ref=out_ref.at[2 * cx + cy], send_sem=send_sems.at[j],
                recv_sem=recv_sems.at[j], device_id=(cx, cy, c), device_id_type=MESH_ID).wait_recv()
        for cp in cps:
            cp.wait_send()
        mine.wait()

    any_spec = pl.BlockSpec(memory_space=pl.ANY)
    return _pcall(
        body, name=name, in_specs=[any_spec], out_specs=any_spec, out_shape=_sds(p.shape, p.dtype),
        scratch_shapes=[pltpu.SemaphoreType.DMA((3,)), pltpu.SemaphoreType.DMA((3,)), pltpu.SemaphoreType.DMA],
    )(p)


def _pair_gather(t, name):
    def body(t_ref, out_ref, send_sem, recv_sem, local_sem):
        x, y, c = _place()
        mine = pltpu.make_async_copy(t_ref, out_ref.at[c], local_sem)
        mine.start()
        cp = pltpu.make_async_remote_copy(src_ref=t_ref, dst_ref=out_ref.at[c], send_sem=send_sem, recv_sem=recv_sem,
                                          device_id=(x, y, 1 - c), device_id_type=MESH_ID)
        cp.start()
        pltpu.make_async_remote_copy(src_ref=t_ref, dst_ref=out_ref.at[1 - c], send_sem=send_sem, recv_sem=recv_sem,
                                     device_id=(x, y, 1 - c), device_id_type=MESH_ID).wait_recv()
        cp.wait_send()
        mine.wait()

    any_spec = pl.BlockSpec(memory_space=pl.ANY)
    return _pcall(
        body, name=name, in_specs=[any_spec], out_specs=any_spec, out_shape=_sds((2,) + t.shape, t.dtype),
        scratch_shapes=[pltpu.SemaphoreType.DMA, pltpu.SemaphoreType.DMA, pltpu.SemaphoreType.DMA],
    )(t)


def _prep_w_in0(w):
    z = jnp.zeros((w.shape[0], 32), w.dtype)
    z64 = jnp.zeros((w.shape[0], 64), w.dtype)
    k0, k1 = w[:, 928:992], w[:, 992:1056]
    v0, v1 = w[:, 1056:1120], w[:, 1120:1184]
    return jnp.concatenate([w[:, 0:384], z64, w[:, 384:416], z, w[:, 416:928],
                            k0, k0, k0, k0, k1, k1, k1, k1, v0, v0, v0, v0, v1, v1, v1, v1, w[:, 1184:2208]], axis=1)


def _fold_w_in0(d):
    def fold(blk):
        b = blk.reshape(blk.shape[0], 8, 64)
        return jnp.concatenate([b[:, 0] + b[:, 1] + b[:, 2] + b[:, 3], b[:, 4] + b[:, 5] + b[:, 6] + b[:, 7]], axis=1)
    return jnp.concatenate([d[:, 0:384], d[:, 448:480], d[:, 512:1024], fold(d[:, 1024:1536]), fold(d[:, 1536:2048]),
                            d[:, 2048:3072]], axis=1)


def _prep_w_q(w):
    return jnp.pad(w.reshape(Q_RANK, N_MLA, 96), ((0, 0), (0, 0), (0, 32))).reshape(Q_RANK, 1024)


def _fold_w_q(d):
    return d.reshape(Q_RANK, N_MLA, 128)[:, :, :96].reshape(Q_RANK, 768)


def _prep_w_kv(w):
    w3 = w.reshape(KV_RANK, N_MLA, 128)
    kk = jnp.pad(w3[:, :, :64], ((0, 0), (0, 0), (0, 64))).reshape(KV_RANK, 1024)
    return jnp.concatenate([kk, w3[:, :, 64:].reshape(KV_RANK, 512)], axis=1)


def _fold_w_kv(d):
    kk = d[:, :1024].reshape(KV_RANK, N_MLA, 128)[:, :, :64]
    vv = d[:, 1024:].reshape(KV_RANK, N_MLA, 64)
    return jnp.concatenate([kk, vv], axis=2).reshape(KV_RANK, 1024)


def _prep_w_in1(w):
    return jnp.concatenate([w[:, 0:3072], w[:, 3088:4112], w[:, 3072:3088], jnp.zeros((w.shape[0], 112), w.dtype)], axis=1)


def _fold_w_in1(d):
    return jnp.concatenate([d[:, 0:3072], d[:, 4096:4112], d[:, 3072:4096]], axis=1)


def _local_step(x, pos, target, e_g_in, w_in0, e_g_q, w_q, e_g_kv, w_kv, sinks, w_out0, o_g_in, w_in1, b_f, w_out1, g_final):
    S = x.shape[0]
    w_in0p, w_qp, w_kvp, w_in1p = _prep_w_in0(w_in0), _prep_w_q(w_q), _prep_w_kv(w_kv), _prep_w_in1(w_in1)
    slopes = jnp.asarray(2.0 ** (-8.0 * (np.arange(N_SWA, dtype=np.float32) + 1.0) / N_SWA), jnp.float32)
    sinks1 = sinks.reshape(N_SWA)
    b_col = b_f.reshape(N_FOX, 1)

    (h0, cq, ckv, cqn, ckvn, qm, km, vm, qs, kd, vd, gate0, cos, sin) = _layer0_in(
        x, pos, e_g_in, w_in0p, e_g_q, w_qp, e_g_kv, w_kvp)
    o_m, lse_m = _attn_fwd(qm, km, vm, (NOPE + ROPE) ** -0.5, split=True, name="mla_fwd")
    o_s, lse_s = _swa_fwd(qs, kd, vd, sinks1, slopes)
    x1, u0, h1, q1, k1, v1, gate1, f_slab = _layer0_out_layer1_in(x, o_m, o_s, gate0, w_out0, o_g_in, w_in1p)
    f_row = f_slab[:, :N_FOX].T
    lc_row = _forget_fwd(f_row, b_col)
    lcc = lc_row.reshape(N_FOX, S, 1)
    lcr = lc_row.reshape(N_FOX // 2, 2, S)
    o1, lse1 = _attn_fwd(q1, k1, v1, HEAD ** -0.5, split=False, name="fox_fwd", lcc=lcc, lcr=lcr)
    loss8, dg_final, dx2, u1, do1, dgate1 = _head(x1, o1, gate1, w_out1, g_final, target)

    dq1, dk1, dv1, dlc, drow = _attn_bwd(q1, k1, v1, do1, o1, lse1, HEAD ** -0.5, split=False, name="fox_bwd", lcc=lcc, lcr=lcr)
    df_row, db_f = _forget_bwd(dlc.reshape(N_FOX, S), drow.reshape(N_FOX, S), f_row, b_col)
    df_slab = jnp.pad(df_row.T, ((0, 0), (0, LANES - N_FOX))).astype(MXU)
    dz1, dx1, dg_o_in, do_m, do_s, dgate0 = _layer1_in_bwd(
        dq1, dk1, dv1, dgate1, df_slab, x1, dx2, o_g_in, w_in1p, gate0, o_m, o_s, w_out0)
    dqs, dkd, dvd, dsink = _swa_bwd(qs, kd, vd, do_s, o_s, lse_s, sinks1, slopes)
    dqm, dkm, dvm = _attn_bwd(qm, km, vm, do_m, o_m, lse_m, (NOPE + ROPE) ** -0.5, split=True, name="mla_bwd")
    dx, dz0, dqu, dkvu, dg_in, dg_q, dg_kv = _layer0_in_bwd(
        dqm, dkm, dvm, dqs, dkd, dvd, dgate0, cos, sin, cq, ckv, x, dx1, e_g_in, w_in0p, e_g_q, w_qp, e_g_kv, w_kvp)

    grads = dict(
        e_g_in=dg_in,
        e_w_in=_fold_w_in0(_wgrad(h0, dz0, "wgrad_in0")),
        e_g_q_a=dg_q,
        e_w_q_up=_fold_w_q(_wgrad(cqn, dqu, "wgrad_q_up")),
        e_g_kv_a=dg_kv,
        e_w_kv_up=_fold_w_kv(_wgrad(ckvn, dkvu, "wgrad_kv_up")),
        e_sinks=dsink[:, 0:2, 0].reshape(1, N_SWA),
        e_w_out=_wgrad(u0, dx1, "wgrad_out0"),
        o_g_in=dg_o_in,
        o_w_in=_fold_w_in1(_wgrad(h1, dz1, "wgrad_in1")),
        o_b_f=db_f.reshape(1, N_FOX),
        o_w_out=_wgrad(u1, dx2, "wgrad_out1"),
        g_final=dg_final,
    )
    return loss8[0, 0], dx, grads


SHARDED = ("e_w_in", "e_w_q_up", "e_w_kv_up", "e_w_out", "o_g_in", "o_w_in", "o_w_out")
COL_SHARDED = ("e_w_in", "e_w_q_up", "e_w_kv_up", "o_g_in", "o_w_in")
REPLICATED = ("e_g_in", "e_g_q_a", "e_g_kv_a", "e_sinks", "o_b_f", "g_final")
FULL_SHAPES = dict(e_w_in=(1024, 2208), e_w_q_up=(256, 768), e_w_kv_up=(128, 1024), e_w_out=(1024, 1024),
                   o_g_in=(1, 1024), o_w_in=(1024, 4112), o_w_out=(1024, 1024))
WINDOWS = dict(e_w_in=(0, 0), o_w_in=(0, 552), e_w_out=(1024, 0), e_w_q_up=(1024, 1024), e_w_kv_up=(1024, 1216),
               o_w_out=(1280, 0), o_g_in=(1536, 0))


def _shard_shape(name):
    r, c = FULL_SHAPES[name]
    return (r, c // 4) if name in COL_SHARDED else (r // 4, c)


def _pack_block(p):
    dt = p["e_w_in"].dtype

    def z(r, c):
        return jnp.zeros((r, c), dt)

    band_a = jnp.concatenate([p["e_w_in"], p["o_w_in"], z(1024, PACK_COLS - 1580)], axis=1)
    small = jnp.concatenate([p["e_w_kv_up"], z(128, 256)], axis=0)
    band_b = jnp.concatenate([p["e_w_out"], p["e_w_q_up"], small, z(256, PACK_COLS - 1472)], axis=1)
    band_c = jnp.concatenate([p["o_w_out"], z(256, PACK_COLS - 1024)], axis=1)
    g = p["o_g_in"]
    band_d = jnp.pad(g, ((0, PACK_ROWS - 1536 - g.shape[0]), (0, PACK_COLS - g.shape[1])))
    return jnp.concatenate([band_a, band_b, band_c, band_d], axis=0)


def _window(block, name, width=None):
    r0, c0 = WINDOWS[name]
    r, c = _shard_shape(name)
    return block[..., r0:r0 + r, c0:c0 + (c if width is None else width)]


def _chip_slice(name, full, k):
    r, c = _shard_shape(name)
    return full[:, c * k:c * (k + 1)] if name in COL_SHARDED else full[r * k:r * (k + 1), :]


def kernel(x, positions, e_g_in, e_w_in, e_g_q_a, e_w_q_up, e_g_kv_a, e_w_kv_up, e_sinks, e_w_out, o_g_in, o_w_in, o_b_f, o_w_out, g_final, loss_target, m_e_g_in, m_e_w_in, m_e_g_q_a, m_e_w_q_up, m_e_g_kv_a, m_e_w_kv_up, m_e_sinks, m_e_w_out, m_o_g_in, m_o_w_in, m_o_b_f, m_o_w_out, m_g_final, v_e_g_in, v_e_w_in, v_e_g_q_a, v_e_w_q_up, v_e_g_kv_a, v_e_w_kv_up, v_e_sinks, v_e_w_out, v_o_g_in, v_o_w_in, v_o_b_f, v_o_w_out, v_g_final):
    w = dict(e_g_in=e_g_in, e_w_in=e_w_in, e_g_q_a=e_g_q_a, e_w_q_up=e_w_q_up, e_g_kv_a=e_g_kv_a, e_w_kv_up=e_w_kv_up,
             e_sinks=e_sinks, e_w_out=e_w_out, o_g_in=o_g_in, o_w_in=o_w_in, o_b_f=o_b_f, o_w_out=o_w_out, g_final=g_final)
    m = dict(e_g_in=m_e_g_in, e_w_in=m_e_w_in, e_g_q_a=m_e_g_q_a, e_w_q_up=m_e_w_q_up, e_g_kv_a=m_e_g_kv_a,
             e_w_kv_up=m_e_w_kv_up, e_sinks=m_e_sinks, e_w_out=m_e_w_out, o_g_in=m_o_g_in, o_w_in=m_o_w_in, o_b_f=m_o_b_f,
             o_w_out=m_o_w_out, g_final=m_g_final)
    v = dict(e_g_in=v_e_g_in, e_w_in=v_e_w_in, e_g_q_a=v_e_g_q_a, e_w_q_up=v_e_w_q_up, e_g_kv_a=v_e_g_kv_a,
             e_w_kv_up=v_e_w_kv_up, e_sinks=v_e_sinks, e_w_out=v_e_w_out, o_g_in=v_o_g_in, o_w_in=v_o_w_in, o_b_f=v_o_b_f,
             o_w_out=v_o_w_out, g_final=v_g_final)
    order = ("e_g_in", "e_w_in", "e_g_q_a", "e_w_q_up", "e_g_kv_a", "e_w_kv_up", "e_sinks", "e_w_out", "o_g_in", "o_w_in",
             "o_b_f", "o_w_out", "g_final")
    c = lax.axis_index("c")
    chip = 2 * lax.axis_index("x") + lax.axis_index("y")

    parts = {}
    for n in SHARDED:
        a = w[n][0] if w[n].ndim == 3 else w[n]
        if n == "o_g_in":
            parts[n] = lax.bitcast_convert_type(a, jnp.bfloat16).reshape(1, -1)
        else:
            parts[n] = a.astype(jnp.bfloat16)
    mine_w = _pack_block(parts).reshape(2, HALF_ROWS, PACK_COLS)
    gathered = _all_gather8(lax.dynamic_index_in_dim(mine_w, c, 0, keepdims=False), "gather_weights")
    blocks = gathered.reshape(4, PACK_ROWS, PACK_COLS)
    full = {}
    for n in SHARDED:
        if n == "o_g_in":
            halves = _window(blocks, n, width=512).reshape(4, 1, 256, 2)
            full[n] = jnp.concatenate(list(lax.bitcast_convert_type(halves, jnp.float32)), axis=1)
        else:
            pieces = [_window(blocks[k], n) for k in range(4)]
            full[n] = jnp.concatenate(pieces, axis=1 if n in COL_SHARDED else 0).astype(MXU)

    loss_part, dx, grads = _local_step(
        x[0], positions.reshape(-1, 1), loss_target[0], e_g_in, full["e_w_in"], e_g_q_a, full["e_w_q_up"], e_g_kv_a,
        full["e_w_kv_up"], e_sinks, full["e_w_out"], full["o_g_in"], full["o_w_in"], o_b_f, full["o_w_out"],
        g_final.reshape(1, D))
    loss = lax.psum(loss_part, ("x", "y", "c"))

    per_chip = jnp.stack([_pack_block({n: _chip_slice(n, grads[n], k) for n in SHARDED}) for k in range(4)])
    g4 = per_chip.reshape(4, 2, HALF_ROWS, PACK_COLS)
    theirs = _pair_swap(g4, "grad_pair_swap")
    mine = lax.dynamic_index_in_dim(g4, c, 1, keepdims=False)
    chip_sum = _add_blocks(mine, theirs, "grad_pair_add")
    parts4 = _chip_all_to_all(chip_sum, "grad_chip_all_to_all")
    total_half = _sum_leading(parts4, "grad_chip_sum")
    total = _pair_gather(total_half, "grad_pair_gather").reshape(PACK_ROWS, PACK_COLS)
    gsum = {n: _window(total, n).reshape(w[n].shape) for n in SHARDED}

    small = jnp.concatenate([jnp.pad(grads[n].reshape(-1), (0, (-grads[n].size) % LANES)) for n in REPLICATED])
    rows = small.shape[0] // LANES
    small = jnp.pad(small.reshape(rows, LANES), ((0, (-rows) % 8), (0, 0)))
    ssum = _sum_leading(_all_gather8(small, "gather_small_grads"), "small_grad_sum").reshape(-1)
    off = 0
    for n in REPLICATED:
        cnt = w[n].size
        gsum[n] = ssum[off:off + cnt].reshape(w[n].shape)
        off += cnt + (-cnt) % LANES

    delta, new_m, new_v = {}, {}, {}
    for n in order:
        delta[n], new_m[n], new_v[n] = _adamw(w[n], gsum[n], m[n], v[n], "adamw_" + n)
    return (loss, dx[None], *[gsum[n] for n in order], *[delta[n] for n in order], *[new_m[n] for n in order],
            *[new_v[n] for n in order])
```

```python
import functools
import math

import numpy as np
import jax
import jax.numpy as jnp
from jax import lax
from jax.experimental import pallas as pl
from jax.experimental.pallas import tpu as pltpu

D = 1024
EPS = 1e-6
ROPE_THETA = 10000.0
N_MLA = 8
Q_RANK = 256
KV_RANK = 128
NOPE = 64
ROPE = 32
N_SWA = 8
WINDOW = 128
N_FOX = 16
HEAD = 64
E_SPLITS = (256, 128, 32, 512, 128, 128, 1024)
O_SPLITS = (1024, 1024, 1024, 16, 1024)
LR, B1, B2, AEPS, WD, STEP = 0.001, 0.9, 0.999, 1e-08, 0.01, 10

LANES = 128
HALF = 64
VMEM_LIMIT = 56 * 1024 * 1024
MXU = jnp.bfloat16
TOK = 256
ATT = 256
NEG = float("-inf")

PACK_COLS = 1664
PACK_ROWS = 1568
HALF_ROWS = PACK_ROWS // 2
SUM_ROWS = 112
MESH_ID = pl.DeviceIdType.MESH


def _pcall(body, *, name, vmem=VMEM_LIMIT, semantics=None, **kw):
    params = dict(vmem_limit_bytes=vmem)
    if semantics is not None:
        params["dimension_semantics"] = semantics
    return pl.pallas_call(body, name=name, compiler_params=pltpu.CompilerParams(**params), **kw)


def _mm(a, b):
    return jnp.dot(a.astype(MXU), b.astype(MXU), preferred_element_type=jnp.float32)


def _mm_nt(a, b):
    return lax.dot_general(a.astype(MXU), b.astype(MXU), (((1,), (1,)), ((), ())),
                           preferred_element_type=jnp.float32)


def _mm_tn(a, b):
    return lax.dot_general(a.astype(MXU), b.astype(MXU), (((0,), (0,)), ((), ())),
                           preferred_element_type=jnp.float32)


def _full(shape):
    n = len(shape)
    return pl.BlockSpec(shape, lambda *_: (0,) * n)


def _rows(tm, n):
    return pl.BlockSpec((tm, n), lambda i: (i, 0))


def _sds(shape, dtype):
    return jax.ShapeDtypeStruct(shape, dtype)


def _rms(x, g):
    r = lax.rsqrt(jnp.mean(x * x, axis=-1, keepdims=True) + EPS)
    return x * r * g


def _rms_bwd(x, g, dy):
    r = lax.rsqrt(jnp.mean(x * x, axis=-1, keepdims=True) + EPS)
    xh = x * r
    dxh = dy * g
    dx = r * (dxh - xh * jnp.mean(dxh * xh, axis=-1, keepdims=True))
    return dx, dy * xh


def _sigmoid(x):
    return 1.0 / (1.0 + jnp.exp(-x))


def _lane_masks(dtype=None):
    lane = lax.broadcasted_iota(jnp.int32, (1, LANES), 1)
    return lane < HALF


def _split_heads(a, lo):
    z = jnp.zeros_like(a)
    return [jnp.where(lo, a, z), jnp.where(lo, z, a)]


def _rope_consts():
    inv = np.zeros((8, LANES), np.float32)
    j = np.arange(ROPE // 2, dtype=np.float32)
    f = (1.0 / (ROPE_THETA ** (np.arange(0, ROPE, 2, dtype=np.float32) / ROPE))).astype(np.float32)
    inv[0, HALF:HALF + 16] = f
    inv[0, HALF + 16:HALF + 32] = f
    inv[1, HALF:HALF + 16] = -1.0
    inv[1, HALF + 16:HALF + 32] = 1.0
    del j
    return jnp.asarray(inv)


def _rope_tables(pos_f, consts):
    ang = pos_f * consts[0:1, :]
    sign = consts[1:2, :]
    c = jnp.where(sign != 0.0, jnp.cos(ang), 1.0)
    s = jnp.sin(ang) * sign
    return c, s


def _swap_halves(v, sign):
    lo = pltpu.roll(v, LANES - 16, axis=1)
    hi = pltpu.roll(v, 16, axis=1)
    return jnp.where(sign < 0.0, lo, jnp.where(sign > 0.0, hi, 0.0))


def _rope(x, c, s, sign):
    return x * c + _swap_halves(x, sign) * s


def _rope_t(dy, c, s, sign):
    return dy * c + _swap_halves(dy * s, sign)


def _layer0_in(x, pos, g_in, w_in, g_q, w_q, g_kv, w_kv):
    S = x.shape[0]
    consts = _rope_consts()

    def body(x_ref, pos_ref, c_ref, g_ref, w_ref, gq_ref, wq_ref, gkv_ref, wkv_ref,
             h_ref, cq_ref, ckv_ref, cqn_ref, ckvn_ref, qm_ref, km_ref, vm_ref,
             qs_ref, kd_ref, vd_ref, gate_ref, cos_ref, sin_ref):
        h = _rms(x_ref[...], g_ref[...])
        h_ref[...] = h.astype(h_ref.dtype)
        z = _mm(h, w_ref[...])
        cq = z[:, 0:256]
        ckv = z[:, 256:384]
        kpe = z[:, 384:512]
        cq_ref[...] = cq
        ckv_ref[...] = ckv
        qs_ref[...] = z[:, 512:1024].astype(qs_ref.dtype)
        kd_ref[...] = z[:, 1024:1536].astype(kd_ref.dtype)
        vd_ref[...] = z[:, 1536:2048].astype(vd_ref.dtype)
        gate_ref[...] = z[:, 2048:3072]
        cqn = _rms(cq, gq_ref[...])
        ckvn = _rms(ckv, gkv_ref[...])
        cqn_ref[...] = cqn.astype(cqn_ref.dtype)
        ckvn_ref[...] = ckvn.astype(ckvn_ref.dtype)
        q = _mm(cqn, wq_ref[...])
        kv = _mm(ckvn, wkv_ref[...])
        vm_ref[...] = kv[:, 1024:1536].astype(vm_ref.dtype)
        consts_v = c_ref[...]
        sign = consts_v[1:2, :]
        c, s = _rope_tables(pos_ref[...].astype(jnp.float32), consts_v)
        cos_ref[...] = c
        sin_ref[...] = s
        kpe_r = _rope(kpe, c, s, sign)
        for hd in range(N_MLA):
            sl = slice(LANES * hd, LANES * (hd + 1))
            qm_ref[:, sl] = _rope(q[:, sl], c, s, sign).astype(qm_ref.dtype)
            km_ref[:, sl] = (kv[:, sl] + kpe_r).astype(km_ref.dtype)

    outs = [
        ((S, D), MXU), ((S, 256), jnp.float32), ((S, 128), jnp.float32), ((S, 256), MXU), ((S, 128), MXU),
        ((S, 1024), MXU), ((S, 1024), MXU), ((S, 512), MXU), ((S, 512), MXU), ((S, 512), MXU), ((S, 512), MXU),
        ((S, 1024), jnp.float32), ((S, 128), jnp.float32), ((S, 128), jnp.float32),
    ]
    return _pcall(
        body, name="layer0_in", grid=(S // TOK,), semantics=("arbitrary",),
        in_specs=[_rows(TOK, D), _rows(TOK, 1), _full((8, LANES)), _full((1, D)), _full(w_in.shape), _full((1, 256)),
                  _full(w_q.shape), _full((1, 128)), _full(w_kv.shape)],
        out_specs=[_rows(TOK, s[1]) for s, _ in outs],
        out_shape=[_sds(s, d) for s, d in outs],
    )(x, pos, consts, g_in, w_in, g_q, w_q, g_kv, w_kv)


AUG = (HALF, 0)
ONE = (HALF + 8, 8)


def _data_lanes(idx, h):
    return (idx < HALF) if h == 0 else (idx >= HALF)


def _three_terms(x):
    hi = x.astype(MXU).astype(jnp.float32)
    mid = (x - hi).astype(MXU).astype(jnp.float32)
    lo = (x - hi - mid).astype(MXU).astype(jnp.float32)
    return hi, mid, lo


def _q_aug(qblk, lc, h, scale, lane):
    a = AUG[h]
    hi, mid, lo = _three_terms(lc)
    ones = ((lane >= a + 3) & (lane <= a + 5)).astype(jnp.float32)
    aug = jnp.where(lane == a, hi, jnp.where(lane == a + 1, mid, jnp.where(lane == a + 2, lo, ones)))
    return jnp.where(_data_lanes(lane, h), qblk * jnp.asarray(scale, qblk.dtype), aug.astype(qblk.dtype))


def _k_aug(kblk, lc, h, lane):
    a = AUG[h]
    hi, mid, lo = _three_terms(-lc)
    ones = ((lane >= a) & (lane <= a + 2)).astype(jnp.float32)
    aug = jnp.where(lane == a + 3, hi, jnp.where(lane == a + 4, mid, jnp.where(lane == a + 5, lo, ones)))
    return jnp.where(_data_lanes(lane, h), kblk, aug.astype(kblk.dtype))


def _attn_fwd_t(q, k, v, scale, *, split, name, lcc=None):
    S = q.shape[0]
    npair = v.shape[1] // LANES
    W = 2 * LANES if split else LANES
    T = ATT
    CH = 2 * T
    assert S % CH == 0
    nq = S // T

    def body(*refs):
        if split:
            q_ref, k_ref, v_ref, o_ref, lse_ref, vt, acc, m_sc = refs
        else:
            q_ref, k_ref, v_ref, lcc_ref, o_ref, lse_ref, kaug, vt, acc, m_sc = refs
        lane = lax.broadcasted_iota(jnp.int32, (1, LANES), 1)
        sub = lax.broadcasted_iota(jnp.int32, (LANES, 1), 0)
        key_minus_qry = lax.broadcasted_iota(jnp.int32, (CH, T), 0) - lax.broadcasted_iota(jnp.int32, (CH, T), 1)

        def prep(i, c):
            r0 = pl.multiple_of(i * T, T)
            vblk = v_ref[pl.ds(r0, T), :].astype(jnp.float32)
            for h in (0, 1):
                vh = jnp.where(_data_lanes(lane, h), vblk, (lane == ONE[h]).astype(jnp.float32))
                vt[h, :, pl.ds(r0, T)] = vh.T.astype(vt.dtype)
                if not split:
                    kaug[h, pl.ds(r0, T), :] = _k_aug(k_ref[pl.ds(r0, T), :], lcc_ref[h, pl.ds(r0, T), :], h, lane)
            return c

        lax.fori_loop(0, nq, prep, 0)

        def q_block(qi, c):
            q0 = pl.multiple_of(qi * T, T)
            qblk = q_ref[pl.ds(q0, T), :]
            if split:
                qs = [qblk[:, :LANES], qblk[:, LANES:]]
            else:
                qs = [_q_aug(qblk, lcc_ref[h, pl.ds(q0, T), :], h, scale, lane) for h in (0, 1)]
            acc[...] = jnp.zeros_like(acc)
            m_sc[...] = jnp.full(m_sc.shape, NEG, jnp.float32)

            def scores(c):
                k0 = pl.multiple_of(c * CH, CH)
                out = []
                for h in (0, 1):
                    if split:
                        out.append(_mm_nt(k_ref[pl.ds(k0, CH), LANES * h:LANES * (h + 1)], qs[h]) * scale)
                    else:
                        out.append(_mm_nt(kaug[h, pl.ds(k0, CH), :], qs[h]))
                return tuple(out)

            def absorb(c, sts, masked):
                k0 = pl.multiple_of(c * CH, CH)
                for h in (0, 1):
                    st = sts[h]
                    if masked:
                        st = jnp.where(key_minus_qry <= q0 - k0, st, NEG)
                    m_old = m_sc[h:h + 1, :]
                    m_new = jnp.maximum(m_old, jnp.max(st, axis=0, keepdims=True))
                    alpha = jnp.exp(m_old - m_new)
                    pt = jnp.exp(st - m_new)
                    acc[h] = alpha * acc[h] + _mm(vt[h, :, pl.ds(k0, CH)], pt)
                    m_sc[h:h + 1, :] = m_new

            last = qi // 2

            def pipelined(c, sts):
                nxt = scores(c + 1)
                absorb(c, sts, False)
                return nxt

            sts = lax.fori_loop(0, last, pipelined, scores(0))
            absorb(last, sts, True)
            ot = None
            for h in (0, 1):
                a = acc[h]
                l = a[ONE[h]:ONE[h] + 1, :]
                oh = jnp.where(_data_lanes(sub, h), a * (1.0 / l), 0.0)
                ot = oh if ot is None else ot + oh
                lse_ref[0, h:h + 1, pl.ds(q0, T)] = m_sc[h:h + 1, :] + jnp.log(l)
            o_ref[pl.ds(q0, T), :] = ot.T
            return c

        lax.fori_loop(0, nq, q_block, 0)

    wide = pl.BlockSpec((S, W), lambda j: (0, j))
    slab = pl.BlockSpec((S, LANES), lambda j: (0, j))
    rows = pl.BlockSpec((1, 2, S), lambda j: (j, 0, 0))
    in_specs = [wide, wide, slab]
    args = [q, k, v]
    scratch = []
    if not split:
        in_specs.append(pl.BlockSpec((2, S, 1), lambda j: (j, 0, 0)))
        args.append(lcc)
        scratch.append(pltpu.VMEM((2, S, LANES), MXU))
    scratch += [pltpu.VMEM((2, LANES, S), MXU), pltpu.VMEM((2, LANES, T), jnp.float32), pltpu.VMEM((8, T), jnp.float32)]
    return _pcall(
        body, name=name, grid=(npair,), semantics=("arbitrary",),
        in_specs=in_specs, out_specs=[slab, rows],
        out_shape=[_sds((S, npair * LANES), jnp.float32), _sds((npair, 2, S), jnp.float32)],
        scratch_shapes=scratch,
    )(*args)


def _attn_bwd_t(q, k, v, do, o, lse, scale, *, split, name, lcc=None):
    S = q.shape[0]
    npair = v.shape[1] // LANES
    W = 2 * LANES if split else LANES
    T = ATT
    CH = 2 * T
    assert S % CH == 0
    nq = S // T

    def body(*refs):
        if split:
            (q_ref, k_ref, v_ref, do_ref, o_ref, lse_ref, dq_ref, dk_ref, dv_ref, dqt, delta, dk_acc, dv_acc) = refs
        else:
            (q_ref, k_ref, v_ref, do_ref, o_ref, lse_ref, lcc_ref, dq_ref, dk_ref, dv_ref, dlc_ref,
             dqt, delta, dk_acc, dv_acc, qaug, csum) = refs
        lane = lax.broadcasted_iota(jnp.int32, (1, LANES), 1)
        sub = lax.broadcasted_iota(jnp.int32, (LANES, 1), 0)
        key_minus_qry = lax.broadcasted_iota(jnp.int32, (T, CH), 0) - lax.broadcasted_iota(jnp.int32, (T, CH), 1)

        def prep(i, c):
            r0 = pl.multiple_of(i * T, T)
            prod_t = (do_ref[pl.ds(r0, T), :].astype(jnp.float32) * o_ref[pl.ds(r0, T), :]).T
            for h in (0, 1):
                delta[h:h + 1, pl.ds(r0, T)] = jnp.sum(jnp.where(_data_lanes(sub, h), prod_t, 0.0), axis=0, keepdims=True)
                dqt[h, :, pl.ds(r0, T)] = jnp.zeros((LANES, T), jnp.float32)
                if not split:
                    qaug[h, pl.ds(r0, T), :] = _q_aug(q_ref[pl.ds(r0, T), :], lcc_ref[h, pl.ds(r0, T), :], h, scale, lane)
            return c

        lax.fori_loop(0, nq, prep, 0)

        def k_block(ki, c):
            k0 = pl.multiple_of(ki * T, T)
            kblk = k_ref[pl.ds(k0, T), :]
            vblk = v_ref[pl.ds(k0, T), :]
            if split:
                khs = [kblk[:, :LANES], kblk[:, LANES:]]
            else:
                khs = [_k_aug(kblk, lcc_ref[h, pl.ds(k0, T), :], h, lane) for h in (0, 1)]
            khts = [kh.astype(jnp.float32).T.astype(kh.dtype) for kh in khs]
            vhs = _split_heads(vblk, lane < HALF)
            dk_acc[...] = jnp.zeros_like(dk_acc)
            dv_acc[...] = jnp.zeros_like(dv_acc)

            def q_of(c, h):
                q0 = pl.multiple_of(c * CH, CH)
                if split:
                    return q_ref[pl.ds(q0, CH), LANES * h:LANES * (h + 1)]
                return qaug[h, pl.ds(q0, CH), :]

            def scores(c):
                q0 = pl.multiple_of(c * CH, CH)
                dos = _split_heads(do_ref[pl.ds(q0, CH), :], lane < HALF)
                out = []
                for h in (0, 1):
                    st = _mm_nt(khs[h], q_of(c, h))
                    out += [st * scale if split else st, _mm_nt(vhs[h], dos[h])]
                return tuple(out)

            def absorb(c, vals):
                q0 = pl.multiple_of(c * CH, CH)
                dos = _split_heads(do_ref[pl.ds(q0, CH), :], lane < HALF)
                visible = key_minus_qry <= q0 - k0
                for h in (0, 1):
                    st = jnp.where(visible, vals[2 * h], NEG)
                    pt = jnp.exp(st - lse_ref[0, h:h + 1, pl.ds(q0, CH)])
                    dv_acc[...] += _mm(pt, dos[h])
                    dst = pt * (vals[2 * h + 1] - delta[h:h + 1, pl.ds(q0, CH)])
                    dk_acc[h] += _mm(dst, q_of(c, h))
                    dqt[h, :, pl.ds(q0, CH)] += _mm(khts[h], dst)

            first = ki // 2

            def pipelined(c, vals):
                nxt = scores(c + 1)
                absorb(c, vals)
                return nxt

            vals = lax.fori_loop(first, S // CH - 1, pipelined, scores(first))
            absorb(S // CH - 1, vals)
            if split:
                dk_ref[pl.ds(k0, T), :LANES] = (dk_acc[0] * scale).astype(dk_ref.dtype)
                dk_ref[pl.ds(k0, T), LANES:] = (dk_acc[1] * scale).astype(dk_ref.dtype)
            else:
                dk_ref[pl.ds(k0, T), :] = jnp.where(lane < HALF, dk_acc[0], dk_acc[1]).astype(dk_ref.dtype)
                for h in (0, 1):
                    csum[h:h + 1, pl.ds(k0, T)] = dk_acc[h].T[AUG[h] + 3:AUG[h] + 4, :]
            dv_ref[pl.ds(k0, T), :] = dv_acc[...].astype(dv_ref.dtype)
            return c

        lax.fori_loop(0, nq, k_block, 0)

        def finish(i, c):
            r0 = pl.multiple_of(i * T, T)
            if split:
                for h in (0, 1):
                    dq_ref[pl.ds(r0, T), LANES * h:LANES * (h + 1)] = (dqt[h, :, pl.ds(r0, T)].T * scale).astype(dq_ref.dtype)
            else:
                d = jnp.where(sub < HALF, dqt[0, :, pl.ds(r0, T)], dqt[1, :, pl.ds(r0, T)])
                dq_ref[pl.ds(r0, T), :] = (d.T * scale).astype(dq_ref.dtype)
                for h in (0, 1):
                    dlc_ref[0, h:h + 1, pl.ds(r0, T)] = dqt[h, AUG[h]:AUG[h] + 1, pl.ds(r0, T)] - csum[h:h + 1, pl.ds(r0, T)]
            return c

        lax.fori_loop(0, nq, finish, 0)

    wide = pl.BlockSpec((S, W), lambda j: (0, j))
    slab = pl.BlockSpec((S, LANES), lambda j: (0, j))
    rows = pl.BlockSpec((1, 2, S), lambda j: (j, 0, 0))
    in_specs = [wide, wide, slab, slab, slab, rows]
    args = [q, k, v, do, o, lse]
    out_specs = [wide, wide, slab]
    out_shape = [_sds(q.shape, jnp.float32 if split else do.dtype), _sds(k.shape, jnp.float32 if split else do.dtype),
                 _sds(v.shape, do.dtype)]
    scratch = [pltpu.VMEM((2, LANES, S), jnp.float32), pltpu.VMEM((8, S), jnp.float32),
               pltpu.VMEM((2, T, LANES), jnp.float32), pltpu.VMEM((T, LANES), jnp.float32)]
    if not split:
        in_specs.append(pl.BlockSpec((2, S, 1), lambda j: (j, 0, 0)))
        args.append(lcc)
        out_specs.append(rows)
        out_shape.append(_sds((npair, 2, S), jnp.float32))
        scratch += [pltpu.VMEM((2, S, LANES), MXU), pltpu.VMEM((8, S), jnp.float32)]
    return _pcall(
        body, name=name, grid=(npair,), semantics=("arbitrary",),
        in_specs=in_specs, out_specs=out_specs, out_shape=out_shape, scratch_shapes=scratch,
    )(*args)


def _swa_scores(qh, kblk, slope, shift):
    s = _mm_nt(qh, kblk) * (HEAD ** -0.5)
    a = lax.broadcasted_iota(jnp.int32, (WINDOW, 2 * WINDOW), 0)
    c = lax.broadcasted_iota(jnp.int32, (WINDOW, 2 * WINDOW), 1)
    dist = a - c + shift
    s = s - slope * dist.astype(jnp.float32)
    return jnp.where((dist >= 0) & (dist < WINDOW), s, NEG)


def _swa_fwd(q, kd, vd, sinks, slopes):
    S = q.shape[0]
    npair = q.shape[1] // LANES
    nb = S // WINDOW

    def body(sink_ref, slope_ref, q_ref, k_ref, v_ref, o_ref, lse_ref):
        j = pl.program_id(0)
        lo = _lane_masks()

        def q_block(qi, c):
            q0 = pl.multiple_of(qi * WINDOW, WINDOW)
            k0 = pl.multiple_of(jnp.maximum(qi - 1, 0) * WINDOW, WINDOW)
            shift = q0 - k0
            qs = _split_heads(q_ref[pl.ds(q0, WINDOW), :], lo)
            kblk = k_ref[pl.ds(k0, 2 * WINDOW), :]
            vs = _split_heads(v_ref[pl.ds(k0, 2 * WINDOW), :], lo)
            o = None
            for h in (0, 1):
                sink = sink_ref[2 * j + h]
                s = _swa_scores(qs[h], kblk, slope_ref[2 * j + h], shift)
                m = jnp.maximum(jnp.max(s, axis=1, keepdims=True), sink)
                p = jnp.exp(s - m)
                den = jnp.sum(p, axis=1, keepdims=True) + jnp.exp(sink - m)
                oh = _mm(p / den, vs[h])
                o = oh if o is None else o + oh
                lse_ref[h, pl.ds(q0, WINDOW), :] = m + jnp.log(den)
            o_ref[pl.ds(q0, WINDOW), :] = o
            return c

        lax.fori_loop(0, nb, q_block, 0)

    smem = pl.BlockSpec(memory_space=pltpu.SMEM)
    slab = pl.BlockSpec((S, LANES), lambda j: (0, j))
    return _pcall(
        body, name="swa_fwd", grid=(npair,), semantics=("arbitrary",),
        in_specs=[smem, smem, slab, slab, slab],
        out_specs=[slab, pl.BlockSpec((2, S, 1), lambda j: (j, 0, 0))],
        out_shape=[_sds((S, npair * LANES), jnp.float32), _sds((2 * npair, S, 1), jnp.float32)],
    )(sinks, slopes, q, kd, vd)


def _swa_bwd(q, kd, vd, do, o, lse, sinks, slopes):
    S = q.shape[0]
    npair = q.shape[1] // LANES
    nb = S // WINDOW

    def body(sink_ref, slope_ref, q_ref, k_ref, v_ref, do_ref, o_ref, lse_ref,
             dq_ref, dk_ref, dv_ref, dsink_ref, dk_acc, dv_acc):
        j = pl.program_id(0)
        lo = _lane_masks()
        dk_acc[...] = jnp.zeros_like(dk_acc)
        dv_acc[...] = jnp.zeros_like(dv_acc)

        def q_block(qi, carry):
            q0 = pl.multiple_of(qi * WINDOW, WINDOW)
            k0 = pl.multiple_of(jnp.maximum(qi - 1, 0) * WINDOW, WINDOW)
            shift = q0 - k0
            qs = _split_heads(q_ref[pl.ds(q0, WINDOW), :], lo)
            dos = _split_heads(do_ref[pl.ds(q0, WINDOW), :], lo)
            oblk = o_ref[pl.ds(q0, WINDOW), :]
            kblk = k_ref[pl.ds(k0, 2 * WINDOW), :]
            vblk = v_ref[pl.ds(k0, 2 * WINDOW), :]
            ks = _split_heads(kblk, lo)
            dq = None
            out = []
            for h in (0, 1):
                sink = sink_ref[2 * j + h]
                lse_h = lse_ref[h, pl.ds(q0, WINDOW), :]
                s = _swa_scores(qs[h], kblk, slope_ref[2 * j + h], shift)
                p = jnp.exp(s - lse_h)
                delta = jnp.sum(dos[h].astype(jnp.float32) * oblk, axis=1, keepdims=True)
                dv_acc[pl.ds(k0, 2 * WINDOW), :] += _mm_tn(p, dos[h])
                dp = _mm_nt(dos[h], vblk)
                ds = p * (dp - delta)
                dqh = _mm(ds, ks[h]) * (HEAD ** -0.5)
                dq = dqh if dq is None else dq + dqh
                dk_acc[pl.ds(k0, 2 * WINDOW), :] += _mm_tn(ds, qs[h]) * (HEAD ** -0.5)
                dsk = jnp.sum(-jnp.exp(sink - lse_h) * delta, axis=0, keepdims=True)
                out.append(carry[h] + dsk)
            dq_ref[pl.ds(q0, WINDOW), :] = dq.astype(dq_ref.dtype)
            return tuple(out)

        zero = jnp.zeros((1, 1), jnp.float32)
        dsa, dsb = lax.fori_loop(0, nb, q_block, (zero, zero))
        dk_ref[...] = dk_acc[...].astype(dk_ref.dtype)
        dv_ref[...] = dv_acc[...].astype(dv_ref.dtype)
        r = lax.broadcasted_iota(jnp.int32, (8, LANES), 0)
        dsink_ref[0] = jnp.where(r == 0, dsa, jnp.where(r == 1, dsb, 0.0))

    smem = pl.BlockSpec(memory_space=pltpu.SMEM)
    slab = pl.BlockSpec((S, LANES), lambda j: (0, j))
    return _pcall(
        body, name="swa_bwd", grid=(npair,), semantics=("arbitrary",),
        in_specs=[smem, smem, slab, slab, slab, slab, slab, pl.BlockSpec((2, S, 1), lambda j: (j, 0, 0))],
        out_specs=[slab, slab, slab, pl.BlockSpec((1, 8, LANES), lambda j: (j, 0, 0))],
        out_shape=[_sds(q.shape, do.dtype), _sds(kd.shape, do.dtype), _sds(vd.shape, do.dtype),
                   _sds((npair, 8, LANES), jnp.float32)],
        scratch_shapes=[pltpu.VMEM((S, LANES), jnp.float32), pltpu.VMEM((S, LANES), jnp.float32)],
    )(sinks, slopes, q, kd, vd, do, o, lse)


def _log_steps(S):
    k, out = 1, []
    while k < S:
        out.append(k)
        k *= 2
    return out


def _forget_fwd(f_row, b_col):
    S = f_row.shape[1]

    def body(f_ref, b_ref, lc_ref):
        x = f_ref[...] + b_ref[...]
        lc = jnp.minimum(x, 0.0) - jnp.log(1.0 + jnp.exp(-jnp.abs(x)))
        idx = lax.broadcasted_iota(jnp.int32, lc.shape, 1)
        for k in _log_steps(S):
            lc = lc + jnp.where(idx >= k, pltpu.roll(lc, k, axis=1), 0.0)
        lc_ref[...] = lc

    return _pcall(body, name="forget_fwd", out_shape=_sds(f_row.shape, jnp.float32))(f_row, b_col)


def _forget_bwd(dlc_row, f_row, b_col):
    S = f_row.shape[1]

    def body(d_ref, f_ref, b_ref, df_ref, db_ref):
        g = d_ref[...]
        idx = lax.broadcasted_iota(jnp.int32, g.shape, 1)
        for k in _log_steps(S):
            g = g + jnp.where(idx < S - k, pltpu.roll(g, S - k, axis=1), 0.0)
        x = f_ref[...] + b_ref[...]
        df = g * _sigmoid(-x)
        df_ref[...] = df
        db_ref[...] = jnp.sum(df, axis=1, keepdims=True)

    return _pcall(body, name="forget_bwd",
                  out_shape=[_sds(f_row.shape, jnp.float32), _sds((f_row.shape[0], 1), jnp.float32)])(dlc_row, f_row, b_col)


def _layer0_out_layer1_in(x, o_m, o_s, gate, w_out, g1, w_in1):
    S = x.shape[0]

    def body(x_ref, om_ref, os_ref, gate_ref, wo_ref, g_ref, w_ref,
             x1_ref, u_ref, h_ref, q_ref, k_ref, v_ref, g1_ref, f_ref):
        gt = gate_ref[...]
        sg = gt * _sigmoid(gt)
        um = om_ref[...] * sg[:, :512]
        us = os_ref[...] * sg[:, 512:]
        u_ref[:, :512] = um.astype(u_ref.dtype)
        u_ref[:, 512:] = us.astype(u_ref.dtype)
        x1 = x_ref[...] + _mm(um, wo_ref[0:512, :]) + _mm(us, wo_ref[512:1024, :])
        x1_ref[...] = x1
        h = _rms(x1, g_ref[...])
        h_ref[...] = h.astype(h_ref.dtype)
        z = _mm(h, w_ref[...])
        q_ref[...] = z[:, 0:1024].astype(q_ref.dtype)
        k_ref[...] = z[:, 1024:2048].astype(k_ref.dtype)
        v_ref[...] = z[:, 2048:3072].astype(v_ref.dtype)
        g1_ref[...] = z[:, 3072:4096]
        f_ref[...] = z[:, 4096:4224]

    outs = [((S, D), jnp.float32), ((S, D), MXU), ((S, D), MXU), ((S, D), MXU), ((S, D), MXU), ((S, D), MXU),
            ((S, D), jnp.float32), ((S, LANES), jnp.float32)]
    return _pcall(
        body, name="layer0_out_layer1_in", grid=(S // TOK,), semantics=("arbitrary",),
        in_specs=[_rows(TOK, D), _rows(TOK, 512), _rows(TOK, 512), _rows(TOK, D), _full((D, D)), _full((1, D)),
                  _full(w_in1.shape)],
        out_specs=[_rows(TOK, s[1]) for s, _ in outs],
        out_shape=[_sds(s, d) for s, d in outs],
    )(x, o_m, o_s, gate, w_out, g1, w_in1)


def _head(x1, o1, gate1, w_out1, g_f, target):
    S = x1.shape[0]

    def body(x1_ref, o_ref, gate_ref, wo_ref, g_ref, t_ref,
             loss_ref, dgf_ref, dx2_ref, u_ref, do_ref, dgate_ref):
        i = pl.program_id(0)
        gt = gate_ref[...]
        sig = _sigmoid(gt)
        sg = gt * sig
        o = o_ref[...]
        u = o * sg
        u_ref[...] = u.astype(u_ref.dtype)
        x2 = x1_ref[...] + _mm(u, wo_ref[...])
        g = g_ref[...]
        y = _rms(x2, g)
        err = y - t_ref[...]
        part = 0.5 * jnp.sum(jnp.mean(err * err, axis=-1, keepdims=True), axis=0, keepdims=True)
        dy = err * (1.0 / D)
        dx2, dg_rows = _rms_bwd(x2, g, dy)
        dx2_ref[...] = dx2
        du = _mm_nt(dx2, wo_ref[...])
        do_ref[...] = (du * sg).astype(do_ref.dtype)
        dgate_ref[...] = (du * o * (sig * (1.0 + gt * (1.0 - sig)))).astype(dgate_ref.dtype)

        @pl.when(i == 0)
        def _():
            loss_ref[...] = jnp.zeros_like(loss_ref)
            dgf_ref[...] = jnp.zeros_like(dgf_ref)

        loss_ref[...] += jnp.broadcast_to(part, loss_ref.shape)
        dgf_ref[...] += jnp.sum(dg_rows, axis=0, keepdims=True)

    outs = [((S, D), jnp.float32), ((S, D), MXU), ((S, D), MXU), ((S, D), MXU)]
    return _pcall(
        body, name="head", grid=(S // TOK,), semantics=("arbitrary",),
        in_specs=[_rows(TOK, D), _rows(TOK, D), _rows(TOK, D), _full((D, D)), _full((1, D)), _rows(TOK, D)],
        out_specs=[_full((8, LANES)), _full((1, D))] + [_rows(TOK, D) for _ in outs],
        out_shape=[_sds((8, LANES), jnp.float32), _sds((1, D), jnp.float32)] + [_sds(s, d) for s, d in outs],
    )(x1, o1, gate1, w_out1, g_f, target)


def _layer1_in_bwd(dq, dk, dv, dgate1, df, x1, dx2, g1, w_in1, gate0, o_m, o_s, w_out0):
    S = x1.shape[0]

    def body(dq_ref, dk_ref, dv_ref, dg1_ref, df_ref, x1_ref, dx2_ref, g_ref, w_ref, gate_ref, om_ref, os_ref,
             wo_ref, dz_ref, dx1_ref, dgn_ref, dom_ref, dos_ref, dgate_ref):
        i = pl.program_id(0)
        dz_ref[:, 0:1024] = dq_ref[...]
        dz_ref[:, 1024:2048] = dk_ref[...]
        dz_ref[:, 2048:3072] = dv_ref[...]
        dz_ref[:, 3072:4096] = dg1_ref[...]
        dz_ref[:, 4096:4224] = df_ref[...]
        dh = _mm_nt(dz_ref[...], w_ref[...])
        g = g_ref[...]
        dxn, dg_rows = _rms_bwd(x1_ref[...], g, dh)
        dx1 = dx2_ref[...] + dxn
        dx1_ref[...] = dx1
        du = _mm_nt(dx1, wo_ref[...])
        gt = gate_ref[...]
        sig = _sigmoid(gt)
        sg = gt * sig
        dsg = sig * (1.0 + gt * (1.0 - sig))
        dom_ref[...] = (du[:, :512] * sg[:, :512]).astype(dom_ref.dtype)
        dos_ref[...] = (du[:, 512:] * sg[:, 512:]).astype(dos_ref.dtype)
        dgate_ref[:, :512] = (du[:, :512] * om_ref[...] * dsg[:, :512]).astype(dgate_ref.dtype)
        dgate_ref[:, 512:] = (du[:, 512:] * os_ref[...] * dsg[:, 512:]).astype(dgate_ref.dtype)

        @pl.when(i == 0)
        def _():
            dgn_ref[...] = jnp.zeros_like(dgn_ref)

        dgn_ref[...] += jnp.sum(dg_rows, axis=0, keepdims=True)

    return _pcall(
        body, name="layer1_in_bwd", grid=(S // TOK,), semantics=("arbitrary",),
        in_specs=[_rows(TOK, D), _rows(TOK, D), _rows(TOK, D), _rows(TOK, D), _rows(TOK, LANES), _rows(TOK, D),
                  _rows(TOK, D), _full((1, D)), _full(w_in1.shape), _rows(TOK, D), _rows(TOK, 512), _rows(TOK, 512),
                  _full((D, D))],
        out_specs=[_rows(TOK, 4224), _rows(TOK, D), _full((1, D)), _rows(TOK, 512), _rows(TOK, 512), _rows(TOK, D)],
        out_shape=[_sds((S, 4224), MXU), _sds((S, D), jnp.float32), _sds((1, D), jnp.float32),
                   _sds((S, 512), MXU), _sds((S, 512), MXU), _sds((S, D), MXU)],
    )(dq, dk, dv, dgate1, df, x1, dx2, g1, w_in1, gate0, o_m, o_s, w_out0)


def _layer0_in_bwd(dqm, dkm, dvm, dqs, dkd, dvd, dgate0, cos, sin, cq, ckv, x, dx1, g_in, w_in, g_q, w_q, g_kv, w_kv):
    S = x.shape[0]
    consts = _rope_consts()

    def body(dqm_ref, dkm_ref, dvm_ref, dqs_ref, dkd_ref, dvd_ref, dgate_ref, cos_ref, sin_ref, c_ref, cq_ref, ckv_ref,
             x_ref, dx1_ref, g_ref, w_ref, gq_ref, wq_ref, gkv_ref, wkv_ref,
             dx_ref, dz_ref, dqu_ref, dkvu_ref, dgin_ref, dgq_ref, dgkv_ref):
        i = pl.program_id(0)
        lo = _lane_masks()
        sign = c_ref[...][1:2, :]
        c = cos_ref[...]
        s = sin_ref[...]
        dkpe = None
        for hd in range(N_MLA):
            sl = slice(LANES * hd, LANES * (hd + 1))
            dqu_ref[:, sl] = _rope_t(dqm_ref[:, sl], c, s, sign).astype(dqu_ref.dtype)
            dkh = dkm_ref[:, sl]
            dkvu_ref[:, sl] = jnp.where(lo, dkh, 0.0).astype(dkvu_ref.dtype)
            dkpe = dkh if dkpe is None else dkpe + dkh
        dkvu_ref[:, 1024:1536] = dvm_ref[...]
        dkpe = _rope_t(jnp.where(lo, 0.0, dkpe), c, s, sign)
        dcqn = _mm_nt(dqu_ref[...], wq_ref[...])
        dckvn = _mm_nt(dkvu_ref[...], wkv_ref[...])
        gq = gq_ref[...]
        gkv = gkv_ref[...]
        dcq, dgq_rows = _rms_bwd(cq_ref[...], gq, dcqn)
        dckv, dgkv_rows = _rms_bwd(ckv_ref[...], gkv, dckvn)
        dz_ref[:, 0:256] = dcq.astype(dz_ref.dtype)
        dz_ref[:, 256:384] = dckv.astype(dz_ref.dtype)
        dz_ref[:, 384:512] = dkpe.astype(dz_ref.dtype)
        dz_ref[:, 512:1024] = dqs_ref[...]
        dz_ref[:, 1024:1536] = dkd_ref[...]
        dz_ref[:, 1536:2048] = dvd_ref[...]
        dz_ref[:, 2048:3072] = dgate_ref[...]
        dh = _mm_nt(dz_ref[...], w_ref[...])
        g = g_ref[...]
        dxn, dg_rows = _rms_bwd(x_ref[...], g, dh)
        dx_ref[...] = dx1_ref[...] + dxn

        @pl.when(i == 0)
        def _():
            dgin_ref[...] = jnp.zeros_like(dgin_ref)
            dgq_ref[...] = jnp.zeros_like(dgq_ref)
            dgkv_ref[...] = jnp.zeros_like(dgkv_ref)

        dgin_ref[...] += jnp.sum(dg_rows, axis=0, keepdims=True)
        dgq_ref[...] += jnp.sum(dgq_rows, axis=0, keepdims=True)
        dgkv_ref[...] += jnp.sum(dgkv_rows, axis=0, keepdims=True)

    return _pcall(
        body, name="layer0_in_bwd", grid=(S // TOK,), semantics=("arbitrary",),
        in_specs=[_rows(TOK, 1024), _rows(TOK, 1024), _rows(TOK, 512), _rows(TOK, 512), _rows(TOK, 512), _rows(TOK, 512),
                  _rows(TOK, D), _rows(TOK, LANES), _rows(TOK, LANES), _full((8, LANES)), _rows(TOK, 256), _rows(TOK, 128),
                  _rows(TOK, D), _rows(TOK, D), _full((1, D)), _full(w_in.shape), _full((1, 256)), _full(w_q.shape),
                  _full((1, 128)), _full(w_kv.shape)],
        out_specs=[_rows(TOK, D), _rows(TOK, 3072), _rows(TOK, 1024), _rows(TOK, 1536), _full((1, D)), _full((1, 256)),
                   _full((1, 128))],
        out_shape=[_sds((S, D), jnp.float32), _sds((S, 3072), MXU), _sds((S, 1024), MXU), _sds((S, 1536), MXU),
                   _sds((1, D), jnp.float32), _sds((1, 256), jnp.float32), _sds((1, 128), jnp.float32)],
    )(dqm, dkm, dvm, dqs, dkd, dvd, dgate0, cos, sin, consts, cq, ckv, x, dx1, g_in, w_in, g_q, w_q, g_kv, w_kv)


def _wgrad(a, b, name):
    S, M = a.shape
    N = b.shape[1]
    tn = 512 if N % 512 == 0 else (384 if N % 384 == 0 else LANES)
    tk = min(512, S)

    def body(a_ref, b_ref, o_ref):
        @pl.when(pl.program_id(1) == 0)
        def _():
            o_ref[...] = jnp.zeros_like(o_ref)

        o_ref[...] += _mm_tn(a_ref[...], b_ref[...])

    return _pcall(
        body, name=name, grid=(N // tn, S // tk), semantics=("parallel", "arbitrary"),
        in_specs=[pl.BlockSpec((tk, M), lambda n, k: (k, 0)), pl.BlockSpec((tk, tn), lambda n, k: (k, n))],
        out_specs=pl.BlockSpec((M, tn), lambda n, k: (0, n)),
        out_shape=_sds((M, N), jnp.float32),
    )(a, b)


def _adamw(w, g, m, v, name):
    shape = w.shape
    R, C = (int(np.prod(shape[:-1])), shape[-1])
    w2, g2, m2, v2 = (t.reshape(R, C) for t in (w, g, m, v))
    tr = 256 if R % 256 == 0 else R

    def body(w_ref, g_ref, m_ref, v_ref, d_ref, nm_ref, nv_ref):
        gg = g_ref[...]
        nm = B1 * m_ref[...] + (1.0 - B1) * gg
        nv = B2 * v_ref[...] + (1.0 - B2) * (gg * gg)
        m_hat = nm / (1.0 - B1 ** STEP)
        v_hat = nv / (1.0 - B2 ** STEP)
        d_ref[...] = -LR * (m_hat / (jnp.sqrt(v_hat) + AEPS) + WD * w_ref[...])
        nm_ref[...] = nm
        nv_ref[...] = nv

    spec = _rows(tr, C)
    d, nm, nv = _pcall(
        body, name=name, grid=(R // tr,), semantics=("parallel",),
        in_specs=[spec] * 4, out_specs=[spec] * 3, out_shape=[_sds((R, C), jnp.float32)] * 3,
    )(w2, g2, m2, v2)
    return d.reshape(shape), nm.reshape(shape), nv.reshape(shape)


def _sum_leading(a, name):
    n, R, C = a.shape
    tr = SUM_ROWS if R % SUM_ROWS == 0 else R

    def body(a_ref, o_ref):
        acc = a_ref[0]
        for i in range(1, n):
            acc = acc + a_ref[i]
        o_ref[...] = acc

    return _pcall(
        body, name=name, grid=(R // tr,), semantics=("parallel",),
        in_specs=[pl.BlockSpec((n, tr, C), lambda i: (0, i, 0))], out_specs=_rows(tr, C),
        out_shape=_sds((R, C), a.dtype),
    )(a)


def _add_blocks(a, b, name):
    n, R, C = a.shape
    tr = SUM_ROWS if R % SUM_ROWS == 0 else R

    def body(a_ref, b_ref, o_ref):
        o_ref[...] = a_ref[...] + b_ref[...]

    spec = pl.BlockSpec((1, tr, C), lambda k, i: (k, i, 0))
    return _pcall(
        body, name=name, grid=(n, R // tr), semantics=("parallel", "parallel"),
        in_specs=[spec, spec], out_specs=spec, out_shape=_sds(a.shape, a.dtype),
    )(a, b)


def _place():
    return lax.axis_index("x"), lax.axis_index("y"), lax.axis_index("c")


def _all_gather8(block, name):
    R, C = block.shape

    def body(x_ref, out_ref, send_sems, recv_sems, local_sem):
        x, y, c = _place()
        me, sibling = (x, y, c), (x, y, 1 - c)
        chips = [(1 - x, y), (x, 1 - y), (1 - x, 1 - y)]

        def slot(px, py, pc):
            return out_ref.at[4 * px + 2 * py + pc]

        def copy(k, blk, to, src=None):
            return pltpu.make_async_remote_copy(
                src_ref=slot(*blk) if src is None else src, dst_ref=slot(*blk),
                send_sem=send_sems.at[k], recv_sem=recv_sems.at[k], device_id=to, device_id_type=MESH_ID)

        mine = pltpu.make_async_copy(x_ref, slot(*me), local_sem)
        mine.start()
        first = [copy(0, me, sibling, src=x_ref)]
        first += [copy(1 + j, me, (*chip, c), src=x_ref) for j, chip in enumerate(chips)]
        for cp in first:
            cp.start()
        passed = [copy(4 + j, (*chip, c), sibling) for j, chip in enumerate(chips)]
        for j, chip in enumerate(chips):
            copy(1 + j, (*chip, c), me).wait_recv()
            passed[j].start()
        copy(0, sibling, me).wait_recv()
        for j, chip in enumerate(chips):
            copy(4 + j, (*chip, 1 - c), me).wait_recv()
        for cp in first + passed:
            cp.wait_send()
        mine.wait()

    any_spec = pl.BlockSpec(memory_space=pl.ANY)
    return _pcall(
        body, name=name, in_specs=[any_spec], out_specs=any_spec, out_shape=_sds((8, R, C), block.dtype),
        scratch_shapes=[pltpu.SemaphoreType.DMA((7,)), pltpu.SemaphoreType.DMA((7,)), pltpu.SemaphoreType.DMA],
    )(block)


def _pair_swap(g, name):
    n = g.shape[0]

    def body(g_ref, out_ref, send_sems, recv_sems):
        x, y, c = _place()
        cps = [pltpu.make_async_remote_copy(src_ref=g_ref.at[k, 1 - c], dst_ref=out_ref.at[k], send_sem=send_sems.at[k],
                                            recv_sem=recv_sems.at[k], device_id=(x, y, 1 - c), device_id_type=MESH_ID)
               for k in range(n)]
        for cp in cps:
            cp.start()
        for cp in cps:
            cp.wait()

    any_spec = pl.BlockSpec(memory_space=pl.ANY)
    return _pcall(
        body, name=name, in_specs=[any_spec], out_specs=any_spec, out_shape=_sds((n,) + g.shape[2:], g.dtype),
        scratch_shapes=[pltpu.SemaphoreType.DMA((n,)), pltpu.SemaphoreType.DMA((n,))],
    )(g)


def _chip_all_to_all(p, name):
    def body(p_ref, out_ref, send_sems, recv_sems, local_sem):
        x, y, c = _place()
        mychip = 2 * x + y
        chips = [(1 - x, y), (x, 1 - y), (1 - x, 1 - y)]
        mine = pltpu.make_async_copy(p_ref.at[mychip], out_ref.at[mychip], local_sem)
        mine.start()
        cps = [pltpu.make_async_remote_copy(
            src_ref=p_ref.at[2 * cx + cy], dst_ref=out_ref.at[mychip], send_sem=send_sems.at[j],
            recv_sem=recv_sems.at[j], device_id=(cx, cy, c), device_id_type=MESH_ID)
            for j, (cx, cy) in enumerate(chips)]
        for cp in cps:
            cp.start()
        for j, (cx, cy) in enumerate(chips):
            pltpu.make_async_remote_copy(
                src_ref=p_ref.at[mychip], dst_ref=out_ref.at[2 * cx + cy], send_sem=send_sems.at[j],
                recv_sem=recv_sems.at[j], device_id=(cx, cy, c), device_id_type=MESH_ID).wait_recv()
        for cp in cps:
            cp.wait_send()
        mine.wait()

    any_spec = pl.BlockSpec(memory_space=pl.ANY)
    return _pcall(
        body, name=name, in_specs=[any_spec], out_specs=any_spec, out_shape=_sds(p.shape, p.dtype),
        scratch_shapes=[pltpu.SemaphoreType.DMA((3,)), pltpu.SemaphoreType.DMA((3,)), pltpu.SemaphoreType.DMA],
    )(p)


def _pair_gather(t, name):
    def body(t_ref, out_ref, send_sem, recv_sem, local_sem):
        x, y, c = _place()
        mine = pltpu.make_async_copy(t_ref, out_ref.at[c], local_sem)
        mine.start()
        cp = pltpu.make_async_remote_copy(src_ref=t_ref, dst_ref=out_ref.at[c], send_sem=send_sem, recv_sem=recv_sem,
                                          device_id=(x, y, 1 - c), device_id_type=MESH_ID)
        cp.start()
        pltpu.make_async_remote_copy(src_ref=t_ref, dst_ref=out_ref.at[1 - c], send_sem=send_sem, recv_sem=recv_sem,
                                     device_id=(x, y, 1 - c), device_id_type=MESH_ID).wait_recv()
        cp.wait_send()
        mine.wait()

    any_spec = pl.BlockSpec(memory_space=pl.ANY)
    return _pcall(
        body, name=name, in_specs=[any_spec], out_specs=any_spec, out_shape=_sds((2,) + t.shape, t.dtype),
        scratch_shapes=[pltpu.SemaphoreType.DMA, pltpu.SemaphoreType.DMA, pltpu.SemaphoreType.DMA],
    )(t)


def _prep_w_in0(w):
    z = jnp.zeros((w.shape[0], 32), w.dtype)
    z64 = jnp.zeros((w.shape[0], 64), w.dtype)
    k0, k1 = w[:, 928:992], w[:, 992:1056]
    v0, v1 = w[:, 1056:1120], w[:, 1120:1184]
    return jnp.concatenate([w[:, 0:384], z64, w[:, 384:416], z, w[:, 416:928],
                            k0, k0, k0, k0, k1, k1, k1, k1, v0, v0, v0, v0, v1, v1, v1, v1, w[:, 1184:2208]], axis=1)


def _fold_w_in0(d):
    def fold(blk):
        b = blk.reshape(blk.shape[0], 8, 64)
        return jnp.concatenate([b[:, 0] + b[:, 1] + b[:, 2] + b[:, 3], b[:, 4] + b[:, 5] + b[:, 6] + b[:, 7]], axis=1)
    return jnp.concatenate([d[:, 0:384], d[:, 448:480], d[:, 512:1024], fold(d[:, 1024:1536]), fold(d[:, 1536:2048]),
                            d[:, 2048:3072]], axis=1)


def _prep_w_q(w):
    return jnp.pad(w.reshape(Q_RANK, N_MLA, 96), ((0, 0), (0, 0), (0, 32))).reshape(Q_RANK, 1024)


def _fold_w_q(d):
    return d.reshape(Q_RANK, N_MLA, 128)[:, :, :96].reshape(Q_RANK, 768)


def _prep_w_kv(w):
    w3 = w.reshape(KV_RANK, N_MLA, 128)
    kk = jnp.pad(w3[:, :, :64], ((0, 0), (0, 0), (0, 64))).reshape(KV_RANK, 1024)
    return jnp.concatenate([kk, w3[:, :, 64:].reshape(KV_RANK, 512)], axis=1)


def _fold_w_kv(d):
    kk = d[:, :1024].reshape(KV_RANK, N_MLA, 128)[:, :, :64]
    vv = d[:, 1024:].reshape(KV_RANK, N_MLA, 64)
    return jnp.concatenate([kk, vv], axis=2).reshape(KV_RANK, 1024)


def _prep_w_in1(w):
    return jnp.concatenate([w[:, 0:3072], w[:, 3088:4112], w[:, 3072:3088], jnp.zeros((w.shape[0], 112), w.dtype)], axis=1)


def _fold_w_in1(d):
    return jnp.concatenate([d[:, 0:3072], d[:, 4096:4112], d[:, 3072:4096]], axis=1)


def _local_step(x, pos, target, e_g_in, w_in0, e_g_q, w_q, e_g_kv, w_kv, sinks, w_out0, o_g_in, w_in1, b_f, w_out1, g_final):
    S = x.shape[0]
    w_in0p, w_qp, w_kvp, w_in1p = _prep_w_in0(w_in0), _prep_w_q(w_q), _prep_w_kv(w_kv), _prep_w_in1(w_in1)
    slopes = jnp.asarray(2.0 ** (-8.0 * (np.arange(N_SWA, dtype=np.float32) + 1.0) / N_SWA), jnp.float32)
    sinks1 = sinks.reshape(N_SWA)
    b_col = b_f.reshape(N_FOX, 1)

    (h0, cq, ckv, cqn, ckvn, qm, km, vm, qs, kd, vd, gate0, cos, sin) = _layer0_in(
        x, pos, e_g_in, w_in0p, e_g_q, w_qp, e_g_kv, w_kvp)
    o_m, lse_m = _attn_fwd_t(qm, km, vm, (NOPE + ROPE) ** -0.5, split=True, name="mla_fwd")
    o_s, lse_s = _swa_fwd(qs, kd, vd, sinks1, slopes)
    x1, u0, h1, q1, k1, v1, gate1, f_slab = _layer0_out_layer1_in(x, o_m, o_s, gate0, w_out0, o_g_in, w_in1p)
    f_row = f_slab[:, :N_FOX].T
    lc_row = _forget_fwd(f_row, b_col)
    lcc = lc_row.reshape(N_FOX, S, 1)
    o1, lse1 = _attn_fwd_t(q1, k1, v1, HEAD ** -0.5, split=False, name="fox_fwd", lcc=lcc)
    loss8, dg_final, dx2, u1, do1, dgate1 = _head(x1, o1, gate1, w_out1, g_final, target)

    dq1, dk1, dv1, dlc = _attn_bwd_t(q1, k1, v1, do1, o1, lse1, HEAD ** -0.5, split=False, name="fox_bwd", lcc=lcc)
    df_row, db_f = _forget_bwd(dlc.reshape(N_FOX, S), f_row, b_col)
    df_slab = jnp.pad(df_row.T, ((0, 0), (0, LANES - N_FOX))).astype(MXU)
    dz1, dx1, dg_o_in, do_m, do_s, dgate0 = _layer1_in_bwd(
        dq1, dk1, dv1, dgate1, df_slab, x1, dx2, o_g_in, w_in1p, gate0, o_m, o_s, w_out0)
    dqs, dkd, dvd, dsink = _swa_bwd(qs, kd, vd, do_s, o_s, lse_s, sinks1, slopes)
    dqm, dkm, dvm = _attn_bwd_t(qm, km, vm, do_m, o_m, lse_m, (NOPE + ROPE) ** -0.5, split=True, name="mla_bwd")
    dx, dz0, dqu, dkvu, dg_in, dg_q, dg_kv = _layer0_in_bwd(
        dqm, dkm, dvm, dqs, dkd, dvd, dgate0, cos, sin, cq, ckv, x, dx1, e_g_in, w_in0p, e_g_q, w_qp, e_g_kv, w_kvp)

    grads = dict(
        e_g_in=dg_in,
        e_w_in=_fold_w_in0(_wgrad(h0, dz0, "wgrad_in0")),
        e_g_q_a=dg_q,
        e_w_q_up=_fold_w_q(_wgrad(cqn, dqu, "wgrad_q_up")),
        e_g_kv_a=dg_kv,
        e_w_kv_up=_fold_w_kv(_wgrad(ckvn, dkvu, "wgrad_kv_up")),
        e_sinks=dsink[:, 0:2, 0].reshape(1, N_SWA),
        e_w_out=_wgrad(u0, dx1, "wgrad_out0"),
        o_g_in=dg_o_in,
        o_w_in=_fold_w_in1(_wgrad(h1, dz1, "wgrad_in1")),
        o_b_f=db_f.reshape(1, N_FOX),
        o_w_out=_wgrad(u1, dx2, "wgrad_out1"),
        g_final=dg_final,
    )
    return loss8[0, 0], dx, grads


SHARDED = ("e_w_in", "e_w_q_up", "e_w_kv_up", "e_w_out", "o_g_in", "o_w_in", "o_w_out")
COL_SHARDED = ("e_w_in", "e_w_q_up", "e_w_kv_up", "o_g_in", "o_w_in")
REPLICATED = ("e_g_in", "e_g_q_a", "e_g_kv_a", "e_sinks", "o_b_f", "g_final")
FULL_SHAPES = dict(e_w_in=(1024, 2208), e_w_q_up=(256, 768), e_w_kv_up=(128, 1024), e_w_out=(1024, 1024),
                   o_g_in=(1, 1024), o_w_in=(1024, 4112), o_w_out=(1024, 1024))
WINDOWS = dict(e_w_in=(0, 0), o_w_in=(0, 552), e_w_out=(1024, 0), e_w_q_up=(1024, 1024), e_w_kv_up=(1024, 1216),
               o_w_out=(1280, 0), o_g_in=(1536, 0))


def _shard_shape(name):
    r, c = FULL_SHAPES[name]
    return (r, c // 4) if name in COL_SHARDED else (r // 4, c)


def _pack_block(p):
    dt = p["e_w_in"].dtype

    def z(r, c):
        return jnp.zeros((r, c), dt)

    band_a = jnp.concatenate([p["e_w_in"], p["o_w_in"], z(1024, PACK_COLS - 1580)], axis=1)
    small = jnp.concatenate([p["e_w_kv_up"], z(128, 256)], axis=0)
    band_b = jnp.concatenate([p["e_w_out"], p["e_w_q_up"], small, z(256, PACK_COLS - 1472)], axis=1)
    band_c = jnp.concatenate([p["o_w_out"], z(256, PACK_COLS - 1024)], axis=1)
    g = p["o_g_in"]
    band_d = jnp.pad(g, ((0, PACK_ROWS - 1536 - g.shape[0]), (0, PACK_COLS - g.shape[1])))
    return jnp.concatenate([band_a, band_b, band_c, band_d], axis=0)


def _window(block, name, width=None):
    r0, c0 = WINDOWS[name]
    r, c = _shard_shape(name)
    return block[..., r0:r0 + r, c0:c0 + (c if width is None else width)]


def _chip_slice(name, full, k):
    r, c = _shard_shape(name)
    return full[:, c * k:c * (k + 1)] if name in COL_SHARDED else full[r * k:r * (k + 1), :]


def kernel(x, positions, e_g_in, e_w_in, e_g_q_a, e_w_q_up, e_g_kv_a, e_w_kv_up, e_sinks, e_w_out, o_g_in, o_w_in, o_b_f, o_w_out, g_final, loss_target, m_e_g_in, m_e_w_in, m_e_g_q_a, m_e_w_q_up, m_e_g_kv_a, m_e_w_kv_up, m_e_sinks, m_e_w_out, m_o_g_in, m_o_w_in, m_o_b_f, m_o_w_out, m_g_final, v_e_g_in, v_e_w_in, v_e_g_q_a, v_e_w_q_up, v_e_g_kv_a, v_e_w_kv_up, v_e_sinks, v_e_w_out, v_o_g_in, v_o_w_in, v_o_b_f, v_o_w_out, v_g_final):
    w = dict(e_g_in=e_g_in, e_w_in=e_w_in, e_g_q_a=e_g_q_a, e_w_q_up=e_w_q_up, e_g_kv_a=e_g_kv_a, e_w_kv_up=e_w_kv_up,
             e_sinks=e_sinks, e_w_out=e_w_out, o_g_in=o_g_in, o_w_in=o_w_in, o_b_f=o_b_f, o_w_out=o_w_out, g_final=g_final)
    m = dict(e_g_in=m_e_g_in, e_w_in=m_e_w_in, e_g_q_a=m_e_g_q_a, e_w_q_up=m_e_w_q_up, e_g_kv_a=m_e_g_kv_a,
             e_w_kv_up=m_e_w_kv_up, e_sinks=m_e_sinks, e_w_out=m_e_w_out, o_g_in=m_o_g_in, o_w_in=m_o_w_in, o_b_f=m_o_b_f,
             o_w_out=m_o_w_out, g_final=m_g_final)
    v = dict(e_g_in=v_e_g_in, e_w_in=v_e_w_in, e_g_q_a=v_e_g_q_a, e_w_q_up=v_e_w_q_up, e_g_kv_a=v_e_g_kv_a,
             e_w_kv_up=v_e_w_kv_up, e_sinks=v_e_sinks, e_w_out=v_e_w_out, o_g_in=v_o_g_in, o_w_in=v_o_w_in, o_b_f=v_o_b_f,
             o_w_out=v_o_w_out, g_final=v_g_final)
    order = ("e_g_in", "e_w_in", "e_g_q_a", "e_w_q_up", "e_g_kv_a", "e_w_kv_up", "e_sinks", "e_w_out", "o_g_in", "o_w_in",
             "o_b_f", "o_w_out", "g_final")
    c = lax.axis_index("c")
    chip = 2 * lax.axis_index("x") + lax.axis_index("y")

    parts = {}
    for n in SHARDED:
        a = w[n][0] if w[n].ndim == 3 else w[n]
        if n == "o_g_in":
            parts[n] = lax.bitcast_convert_type(a, jnp.bfloat16).reshape(1, -1)
        else:
            parts[n] = a.astype(jnp.bfloat16)
    mine_w = _pack_block(parts).reshape(2, HALF_ROWS, PACK_COLS)
    gathered = _all_gather8(lax.dynamic_index_in_dim(mine_w, c, 0, keepdims=False), "gather_weights")
    blocks = gathered.reshape(4, PACK_ROWS, PACK_COLS)
    full = {}
    for n in SHARDED:
        if n == "o_g_in":
            halves = _window(blocks, n, width=512).reshape(4, 1, 256, 2)
            full[n] = jnp.concatenate(list(lax.bitcast_convert_type(halves, jnp.float32)), axis=1)
        else:
            pieces = [_window(blocks[k], n) for k in range(4)]
            full[n] = jnp.concatenate(pieces, axis=1 if n in COL_SHARDED else 0).astype(MXU)

    loss_part, dx, grads = _local_step(
        x[0], positions.reshape(-1, 1), loss_target[0], e_g_in, full["e_w_in"], e_g_q_a, full["e_w_q_up"], e_g_kv_a,
        full["e_w_kv_up"], e_sinks, full["e_w_out"], full["o_g_in"], full["o_w_in"], o_b_f, full["o_w_out"],
        g_final.reshape(1, D))
    loss = lax.psum(loss_part, ("x", "y", "c"))

    per_chip = jnp.stack([_pack_block({n: _chip_slice(n, grads[n], k) for n in SHARDED}) for k in range(4)])
    g4 = per_chip.reshape(4, 2, HALF_ROWS, PACK_COLS)
    theirs = _pair_swap(g4, "grad_pair_swap")
    mine = lax.dynamic_index_in_dim(g4, c, 1, keepdims=False)
    chip_sum = _add_blocks(mine, theirs, "grad_pair_add")
    parts4 = _chip_all_to_all(chip_sum, "grad_chip_all_to_all")
    total_half = _sum_leading(parts4, "grad_chip_sum")
    total = _pair_gather(total_half, "grad_pair_gather").reshape(PACK_ROWS, PACK_COLS)
    gsum = {n: _window(total, n).reshape(w[n].shape) for n in SHARDED}

    small = jnp.concatenate([jnp.pad(grads[n].reshape(-1), (0, (-grads[n].size) % LANES)) for n in REPLICATED])
    rows = small.shape[0] // LANES
    small = jnp.pad(small.reshape(rows, LANES), ((0, (-rows) % 8), (0, 0)))
    ssum = _sum_leading(_all_gather8(small, "gather_small_grads"), "small_grad_sum").reshape(-1)
    off = 0
    for n in REPLICATED:
        cnt = w[n].size
        gsum[n] = ssum[off:off + cnt].reshape(w[n].shape)
        off += cnt + (-cnt) % LANES

    delta, new_m, new_v = {}, {}, {}
    for n in order:
        delta[n], new_m[n], new_v[n] = _adamw(w[n], gsum[n], m[n], v[n], "adamw_" + n)
    return (loss, dx[None], *[gsum[n] for n in order], *[delta[n] for n in order], *[new_m[n] for n in order],
            *[new_v[n] for n in order])
```

```python
import functools
import math

import numpy as np
import jax
import jax.numpy as jnp
from jax import lax
from jax.experimental import pallas as pl
from jax.experimental.pallas import tpu as pltpu

D = 1024
EPS = 1e-6
ROPE_THETA = 10000.0
N_MLA = 8
Q_RANK = 256
KV_RANK = 128
NOPE = 64
ROPE = 32
N_SWA = 8
WINDOW = 128
N_FOX = 16
HEAD = 64
E_SPLITS = (256, 128, 32, 512, 128, 128, 1024)
O_SPLITS = (1024, 1024, 1024, 16, 1024)
LR, B1, B2, AEPS, WD, STEP = 0.001, 0.9, 0.999, 1e-08, 0.01, 10

LANES = 128
HALF = 64
VMEM_LIMIT = 56 * 1024 * 1024
MXU = jnp.bfloat16
TOK = 256
ATT = 256
NEG = float("-inf")

PACK_COLS = 1664
PACK_ROWS = 1568
HALF_ROWS = PACK_ROWS // 2
SUM_ROWS = 112
MESH_ID = pl.DeviceIdType.MESH


def _pcall(body, *, name, vmem=VMEM_LIMIT, semantics=None, **kw):
    params = dict(vmem_limit_bytes=vmem)
    if semantics is not None:
        params["dimension_semantics"] = semantics
    return pl.pallas_call(body, name=name, compiler_params=pltpu.CompilerParams(**params), **kw)


def _mm(a, b):
    return jnp.dot(a.astype(MXU), b.astype(MXU), preferred_element_type=jnp.float32)


def _mm_nt(a, b):
    return lax.dot_general(a.astype(MXU), b.astype(MXU), (((1,), (1,)), ((), ())),
                           preferred_element_type=jnp.float32)


def _mm_tn(a, b):
    return lax.dot_general(a.astype(MXU), b.astype(MXU), (((0,), (0,)), ((), ())),
                           preferred_element_type=jnp.float32)


def _full(shape):
    n = len(shape)
    return pl.BlockSpec(shape, lambda *_: (0,) * n)


def _rows(tm, n):
    return pl.BlockSpec((tm, n), lambda i: (i, 0))


def _sds(shape, dtype):
    return jax.ShapeDtypeStruct(shape, dtype)


def _rms(x, g):
    r = lax.rsqrt(jnp.mean(x * x, axis=-1, keepdims=True) + EPS)
    return x * r * g


def _rms_bwd(x, g, dy):
    r = lax.rsqrt(jnp.mean(x * x, axis=-1, keepdims=True) + EPS)
    xh = x * r
    dxh = dy * g
    dx = r * (dxh - xh * jnp.mean(dxh * xh, axis=-1, keepdims=True))
    return dx, dy * xh


def _sigmoid(x):
    return 1.0 / (1.0 + jnp.exp(-x))


def _lane_masks(dtype=None):
    lane = lax.broadcasted_iota(jnp.int32, (1, LANES), 1)
    return lane < HALF


def _split_heads(a, lo):
    z = jnp.zeros_like(a)
    return [jnp.where(lo, a, z), jnp.where(lo, z, a)]


def _rope_consts():
    inv = np.zeros((8, LANES), np.float32)
    j = np.arange(ROPE // 2, dtype=np.float32)
    f = (1.0 / (ROPE_THETA ** (np.arange(0, ROPE, 2, dtype=np.float32) / ROPE))).astype(np.float32)
    inv[0, HALF:HALF + 16] = f
    inv[0, HALF + 16:HALF + 32] = f
    inv[1, HALF:HALF + 16] = -1.0
    inv[1, HALF + 16:HALF + 32] = 1.0
    del j
    return jnp.asarray(inv)


def _rope_tables(pos_f, consts):
    ang = pos_f * consts[0:1, :]
    sign = consts[1:2, :]
    c = jnp.where(sign != 0.0, jnp.cos(ang), 1.0)
    s = jnp.sin(ang) * sign
    return c, s


def _swap_halves(v, sign):
    lo = pltpu.roll(v, LANES - 16, axis=1)
    hi = pltpu.roll(v, 16, axis=1)
    return jnp.where(sign < 0.0, lo, jnp.where(sign > 0.0, hi, 0.0))


def _rope(x, c, s, sign):
    return x * c + _swap_halves(x, sign) * s


def _rope_t(dy, c, s, sign):
    return dy * c + _swap_halves(dy * s, sign)


def _layer0_in(x, pos, g_in, w_in, g_q, w_q, g_kv, w_kv):
    S = x.shape[0]
    consts = _rope_consts()

    def body(x_ref, pos_ref, c_ref, g_ref, w_ref, gq_ref, wq_ref, gkv_ref, wkv_ref,
             h_ref, cq_ref, ckv_ref, cqn_ref, ckvn_ref, qm_ref, km_ref, vm_ref,
             qs_ref, kd_ref, vd_ref, gate_ref, cos_ref, sin_ref):
        h = _rms(x_ref[...], g_ref[...])
        h_ref[...] = h.astype(h_ref.dtype)
        z = _mm(h, w_ref[...])
        cq = z[:, 0:256]
        ckv = z[:, 256:384]
        kpe = z[:, 384:512]
        cq_ref[...] = cq
        ckv_ref[...] = ckv
        qs_ref[...] = z[:, 512:1024].astype(qs_ref.dtype)
        kd_ref[...] = z[:, 1024:1536].astype(kd_ref.dtype)
        vd_ref[...] = z[:, 1536:2048].astype(vd_ref.dtype)
        gate_ref[...] = z[:, 2048:3072]
        cqn = _rms(cq, gq_ref[...])
        ckvn = _rms(ckv, gkv_ref[...])
        cqn_ref[...] = cqn.astype(cqn_ref.dtype)
        ckvn_ref[...] = ckvn.astype(ckvn_ref.dtype)
        q = _mm(cqn, wq_ref[...])
        kv = _mm(ckvn, wkv_ref[...])
        vm_ref[...] = kv[:, 1024:1536].astype(vm_ref.dtype)
        consts_v = c_ref[...]
        sign = consts_v[1:2, :]
        c, s = _rope_tables(pos_ref[...].astype(jnp.float32), consts_v)
        cos_ref[...] = c
        sin_ref[...] = s
        kpe_r = _rope(kpe, c, s, sign)
        for hd in range(N_MLA):
            sl = slice(LANES * hd, LANES * (hd + 1))
            qm_ref[:, sl] = _rope(q[:, sl], c, s, sign).astype(qm_ref.dtype)
            km_ref[:, sl] = (kv[:, sl] + kpe_r).astype(km_ref.dtype)

    outs = [
        ((S, D), MXU), ((S, 256), jnp.float32), ((S, 128), jnp.float32), ((S, 256), MXU), ((S, 128), MXU),
        ((S, 1024), MXU), ((S, 1024), MXU), ((S, 512), MXU), ((S, 512), MXU), ((S, 512), MXU), ((S, 512), MXU),
        ((S, 1024), jnp.float32), ((S, 128), jnp.float32), ((S, 128), jnp.float32),
    ]
    return _pcall(
        body, name="layer0_in", grid=(S // TOK,), semantics=("arbitrary",),
        in_specs=[_rows(TOK, D), _rows(TOK, 1), _full((8, LANES)), _full((1, D)), _full(w_in.shape), _full((1, 256)),
                  _full(w_q.shape), _full((1, 128)), _full(w_kv.shape)],
        out_specs=[_rows(TOK, s[1]) for s, _ in outs],
        out_shape=[_sds(s, d) for s, d in outs],
    )(x, pos, consts, g_in, w_in, g_q, w_q, g_kv, w_kv)


AUG = (HALF, 0)
ONE = (HALF + 8, 8)


def _data_lanes(idx, h):
    return (idx < HALF) if h == 0 else (idx >= HALF)


def _three_terms(x):
    hi = x.astype(MXU).astype(jnp.float32)
    mid = (x - hi).astype(MXU).astype(jnp.float32)
    lo = (x - hi - mid).astype(MXU).astype(jnp.float32)
    return hi, mid, lo


def _q_aug(qblk, lc, h, scale, lane):
    a = AUG[h]
    hi, mid, lo = _three_terms(lc)
    ones = ((lane >= a + 3) & (lane <= a + 5)).astype(jnp.float32)
    aug = jnp.where(lane == a, hi, jnp.where(lane == a + 1, mid, jnp.where(lane == a + 2, lo, ones)))
    return jnp.where(_data_lanes(lane, h), qblk * jnp.asarray(scale, qblk.dtype), aug.astype(qblk.dtype))


def _k_aug(kblk, lc, h, lane):
    a = AUG[h]
    hi, mid, lo = _three_terms(-lc)
    ones = ((lane >= a) & (lane <= a + 2)).astype(jnp.float32)
    aug = jnp.where(lane == a + 3, hi, jnp.where(lane == a + 4, mid, jnp.where(lane == a + 5, lo, ones)))
    return jnp.where(_data_lanes(lane, h), kblk, aug.astype(kblk.dtype))


def _attn_fwd_t(q, k, v, scale, *, split, name, lcc=None):
    S = q.shape[0]
    npair = v.shape[1] // LANES
    W = 2 * LANES if split else LANES
    T = ATT
    CH = 2 * T
    assert S % CH == 0
    nq = S // T

    def body(*refs):
        if split:
            q_ref, k_ref, v_ref, o_ref, lse_ref, vt, acc, m_sc = refs
        else:
            q_ref, k_ref, v_ref, lcc_ref, o_ref, lse_ref, kaug, vt, acc, m_sc = refs
        lane = lax.broadcasted_iota(jnp.int32, (1, LANES), 1)
        sub = lax.broadcasted_iota(jnp.int32, (LANES, 1), 0)
        key_minus_qry = lax.broadcasted_iota(jnp.int32, (CH, T), 0) - lax.broadcasted_iota(jnp.int32, (CH, T), 1)

        def prep(i, c):
            r0 = pl.multiple_of(i * T, T)
            vblk = v_ref[pl.ds(r0, T), :].astype(jnp.float32)
            for h in (0, 1):
                vh = jnp.where(_data_lanes(lane, h), vblk, (lane == ONE[h]).astype(jnp.float32))
                vt[h, :, pl.ds(r0, T)] = vh.T.astype(vt.dtype)
                if not split:
                    kaug[h, pl.ds(r0, T), :] = _k_aug(k_ref[pl.ds(r0, T), :], lcc_ref[h, pl.ds(r0, T), :], h, lane)
            return c

        lax.fori_loop(0, nq, prep, 0)

        def q_block(qi, c):
            q0 = pl.multiple_of(qi * T, T)
            qblk = q_ref[pl.ds(q0, T), :]
            if split:
                qs = [qblk[:, :LANES], qblk[:, LANES:]]
            else:
                qs = [_q_aug(qblk, lcc_ref[h, pl.ds(q0, T), :], h, scale, lane) for h in (0, 1)]
            acc[...] = jnp.zeros_like(acc)
            m_sc[...] = jnp.full(m_sc.shape, NEG, jnp.float32)

            def scores(c):
                k0 = pl.multiple_of(c * CH, CH)
                out = []
                for h in (0, 1):
                    if split:
                        out.append(_mm_nt(k_ref[pl.ds(k0, CH), LANES * h:LANES * (h + 1)], qs[h]) * scale)
                    else:
                        out.append(_mm_nt(kaug[h, pl.ds(k0, CH), :], qs[h]))
                return tuple(out)

            def absorb(c, sts, masked):
                k0 = pl.multiple_of(c * CH, CH)
                for h in (0, 1):
                    st = sts[h]
                    if masked:
                        st = jnp.where(key_minus_qry <= q0 - k0, st, NEG)
                    m_old = m_sc[h:h + 1, :]
                    m_new = jnp.maximum(m_old, jnp.max(st, axis=0, keepdims=True))
                    alpha = jnp.exp(m_old - m_new)
                    pt = jnp.exp(st - m_new)
                    acc[h] = alpha * acc[h] + _mm(vt[h, :, pl.ds(k0, CH)], pt)
                    m_sc[h:h + 1, :] = m_new

            last = qi // 2

            def pipelined(c, sts):
                nxt = scores(c + 1)
                absorb(c, sts, False)
                return nxt

            sts = lax.fori_loop(0, last, pipelined, scores(0))
            absorb(last, sts, True)
            ot = None
            for h in (0, 1):
                a = acc[h]
                l = a[ONE[h]:ONE[h] + 1, :]
                oh = jnp.where(_data_lanes(sub, h), a * (1.0 / l), 0.0)
                ot = oh if ot is None else ot + oh
                lse_ref[0, h:h + 1, pl.ds(q0, T)] = m_sc[h:h + 1, :] + jnp.log(l)
            o_ref[pl.ds(q0, T), :] = ot.T
            return c

        lax.fori_loop(0, nq, q_block, 0)

    wide = pl.BlockSpec((S, W), lambda j: (0, j))
    slab = pl.BlockSpec((S, LANES), lambda j: (0, j))
    rows = pl.BlockSpec((1, 2, S), lambda j: (j, 0, 0))
    in_specs = [wide, wide, slab]
    args = [q, k, v]
    scratch = []
    if not split:
        in_specs.append(pl.BlockSpec((2, S, 1), lambda j: (j, 0, 0)))
        args.append(lcc)
        scratch.append(pltpu.VMEM((2, S, LANES), MXU))
    scratch += [pltpu.VMEM((2, LANES, S), MXU), pltpu.VMEM((2, LANES, T), jnp.float32), pltpu.VMEM((8, T), jnp.float32)]
    return _pcall(
        body, name=name, grid=(npair,), semantics=("arbitrary",),
        in_specs=in_specs, out_specs=[slab, rows],
        out_shape=[_sds((S, npair * LANES), jnp.float32), _sds((npair, 2, S), jnp.float32)],
        scratch_shapes=scratch,
    )(*args)


def _attn_bwd_t(q, k, v, do, o, lse, scale, *, split, name, lcc=None):
    S = q.shape[0]
    npair = v.shape[1] // LANES
    W = 2 * LANES if split else LANES
    T = ATT
    CH = 2 * T
    assert S % CH == 0
    nq = S // T

    def body(*refs):
        if split:
            (q_ref, k_ref, v_ref, do_ref, o_ref, lse_ref, dq_ref, dk_ref, dv_ref, dqt, delta, dk_acc, dv_acc) = refs
        else:
            (q_ref, k_ref, v_ref, do_ref, o_ref, lse_ref, lcc_ref, dq_ref, dk_ref, dv_ref, dlc_ref,
             dqt, delta, dk_acc, dv_acc, qaug, csum) = refs
        lane = lax.broadcasted_iota(jnp.int32, (1, LANES), 1)
        sub = lax.broadcasted_iota(jnp.int32, (LANES, 1), 0)
        key_minus_qry = lax.broadcasted_iota(jnp.int32, (T, CH), 0) - lax.broadcasted_iota(jnp.int32, (T, CH), 1)

        def prep(i, c):
            r0 = pl.multiple_of(i * T, T)
            prod_t = (do_ref[pl.ds(r0, T), :].astype(jnp.float32) * o_ref[pl.ds(r0, T), :]).T
            for h in (0, 1):
                delta[h:h + 1, pl.ds(r0, T)] = jnp.sum(jnp.where(_data_lanes(sub, h), prod_t, 0.0), axis=0, keepdims=True)
                dqt[h, :, pl.ds(r0, T)] = jnp.zeros((LANES, T), jnp.float32)
                if not split:
                    qaug[h, pl.ds(r0, T), :] = _q_aug(q_ref[pl.ds(r0, T), :], lcc_ref[h, pl.ds(r0, T), :], h, scale, lane)
            return c

        lax.fori_loop(0, nq, prep, 0)

        def k_block(ki, c):
            k0 = pl.multiple_of(ki * T, T)
            kblk = k_ref[pl.ds(k0, T), :]
            vblk = v_ref[pl.ds(k0, T), :]
            if split:
                khs = [kblk[:, :LANES], kblk[:, LANES:]]
            else:
                khs = [_k_aug(kblk, lcc_ref[h, pl.ds(k0, T), :], h, lane) for h in (0, 1)]
            khts = [kh.astype(jnp.float32).T.astype(kh.dtype) for kh in khs]
            vhs = _split_heads(vblk, lane < HALF)
            dk_acc[...] = jnp.zeros_like(dk_acc)
            dv_acc[...] = jnp.zeros_like(dv_acc)

            def q_of(c, h):
                q0 = pl.multiple_of(c * CH, CH)
                if split:
                    return q_ref[pl.ds(q0, CH), LANES * h:LANES * (h + 1)]
                return qaug[h, pl.ds(q0, CH), :]

            def scores(c):
                out = []
                for h in (0, 1):
                    st = _mm_nt(khs[h], q_of(c, h))
                    out.append(st * scale if split else st)
                return tuple(out)

            def absorb(c, vals):
                q0 = pl.multiple_of(c * CH, CH)
                dos = _split_heads(do_ref[pl.ds(q0, CH), :], lane < HALF)
                visible = key_minus_qry <= q0 - k0
                for h in (0, 1):
                    dpt = _mm_nt(vhs[h], dos[h])
                    st = jnp.where(visible, vals[h], NEG)
                    pt = jnp.exp(st - lse_ref[0, h:h + 1, pl.ds(q0, CH)])
                    dv_acc[...] += _mm(pt, dos[h])
                    dst = pt * (dpt - delta[h:h + 1, pl.ds(q0, CH)])
                    dk_acc[h] += _mm(dst, q_of(c, h))
                    dqt[h, :, pl.ds(q0, CH)] += _mm(khts[h], dst)

            first = ki // 2

            def pipelined(c, vals):
                nxt = scores(c + 1)
                absorb(c, vals)
                return nxt

            vals = lax.fori_loop(first, S // CH - 1, pipelined, scores(first))
            absorb(S // CH - 1, vals)
            if split:
                dk_ref[pl.ds(k0, T), :LANES] = (dk_acc[0] * scale).astype(dk_ref.dtype)
                dk_ref[pl.ds(k0, T), LANES:] = (dk_acc[1] * scale).astype(dk_ref.dtype)
            else:
                dk_ref[pl.ds(k0, T), :] = jnp.where(lane < HALF, dk_acc[0], dk_acc[1]).astype(dk_ref.dtype)
                for h in (0, 1):
                    csum[h:h + 1, pl.ds(k0, T)] = dk_acc[h].T[AUG[h] + 3:AUG[h] + 4, :]
            dv_ref[pl.ds(k0, T), :] = dv_acc[...].astype(dv_ref.dtype)
            return c

        lax.fori_loop(0, nq, k_block, 0)

        def finish(i, c):
            r0 = pl.multiple_of(i * T, T)
            if split:
                for h in (0, 1):
                    dq_ref[pl.ds(r0, T), LANES * h:LANES * (h + 1)] = (dqt[h, :, pl.ds(r0, T)].T * scale).astype(dq_ref.dtype)
            else:
                d = jnp.where(sub < HALF, dqt[0, :, pl.ds(r0, T)], dqt[1, :, pl.ds(r0, T)])
                dq_ref[pl.ds(r0, T), :] = (d.T * scale).astype(dq_ref.dtype)
                for h in (0, 1):
                    dlc_ref[0, h:h + 1, pl.ds(r0, T)] = dqt[h, AUG[h]:AUG[h] + 1, pl.ds(r0, T)] - csum[h:h + 1, pl.ds(r0, T)]
            return c

        lax.fori_loop(0, nq, finish, 0)

    wide = pl.BlockSpec((S, W), lambda j: (0, j))
    slab = pl.BlockSpec((S, LANES), lambda j: (0, j))
    rows = pl.BlockSpec((1, 2, S), lambda j: (j, 0, 0))
    in_specs = [wide, wide, slab, slab, slab, rows]
    args = [q, k, v, do, o, lse]
    out_specs = [wide, wide, slab]
    out_shape = [_sds(q.shape, jnp.float32 if split else do.dtype), _sds(k.shape, jnp.float32 if split else do.dtype),
                 _sds(v.shape, do.dtype)]
    scratch = [pltpu.VMEM((2, LANES, S), jnp.float32), pltpu.VMEM((8, S), jnp.float32),
               pltpu.VMEM((2, T, LANES), jnp.float32), pltpu.VMEM((T, LANES), jnp.float32)]
    if not split:
        in_specs.append(pl.BlockSpec((2, S, 1), lambda j: (j, 0, 0)))
        args.append(lcc)
        out_specs.append(rows)
        out_shape.append(_sds((npair, 2, S), jnp.float32))
        scratch += [pltpu.VMEM((2, S, LANES), MXU), pltpu.VMEM((8, S), jnp.float32)]
    return _pcall(
        body, name=name, grid=(npair,), semantics=("arbitrary",),
        in_specs=in_specs, out_specs=out_specs, out_shape=out_shape, scratch_shapes=scratch,
    )(*args)


def _swa_scores(qh, kblk, slope, shift):
    s = _mm_nt(qh, kblk) * (HEAD ** -0.5)
    a = lax.broadcasted_iota(jnp.int32, (WINDOW, 2 * WINDOW), 0)
    c = lax.broadcasted_iota(jnp.int32, (WINDOW, 2 * WINDOW), 1)
    dist = a - c + shift
    s = s - slope * dist.astype(jnp.float32)
    return jnp.where((dist >= 0) & (dist < WINDOW), s, NEG)


def _swa_fwd(q, kd, vd, sinks, slopes):
    S = q.shape[0]
    npair = q.shape[1] // LANES
    nb = S // WINDOW

    def body(sink_ref, slope_ref, q_ref, k_ref, v_ref, o_ref, lse_ref):
        j = pl.program_id(0)
        lo = _lane_masks()

        def q_block(qi, c):
            q0 = pl.multiple_of(qi * WINDOW, WINDOW)
            k0 = pl.multiple_of(jnp.maximum(qi - 1, 0) * WINDOW, WINDOW)
            shift = q0 - k0
            qs = _split_heads(q_ref[pl.ds(q0, WINDOW), :], lo)
            kblk = k_ref[pl.ds(k0, 2 * WINDOW), :]
            vs = _split_heads(v_ref[pl.ds(k0, 2 * WINDOW), :], lo)
            o = None
            for h in (0, 1):
                sink = sink_ref[2 * j + h]
                s = _swa_scores(qs[h], kblk, slope_ref[2 * j + h], shift)
                m = jnp.maximum(jnp.max(s, axis=1, keepdims=True), sink)
                p = jnp.exp(s - m)
                den = jnp.sum(p, axis=1, keepdims=True) + jnp.exp(sink - m)
                oh = _mm(p / den, vs[h])
                o = oh if o is None else o + oh
                lse_ref[h, pl.ds(q0, WINDOW), :] = m + jnp.log(den)
            o_ref[pl.ds(q0, WINDOW), :] = o
            return c

        lax.fori_loop(0, nb, q_block, 0)

    smem = pl.BlockSpec(memory_space=pltpu.SMEM)
    slab = pl.BlockSpec((S, LANES), lambda j: (0, j))
    return _pcall(
        body, name="swa_fwd", grid=(npair,), semantics=("arbitrary",),
        in_specs=[smem, smem, slab, slab, slab],
        out_specs=[slab, pl.BlockSpec((2, S, 1), lambda j: (j, 0, 0))],
        out_shape=[_sds((S, npair * LANES), jnp.float32), _sds((2 * npair, S, 1), jnp.float32)],
    )(sinks, slopes, q, kd, vd)


def _swa_bwd(q, kd, vd, do, o, lse, sinks, slopes):
    S = q.shape[0]
    npair = q.shape[1] // LANES
    nb = S // WINDOW

    def body(sink_ref, slope_ref, q_ref, k_ref, v_ref, do_ref, o_ref, lse_ref,
             dq_ref, dk_ref, dv_ref, dsink_ref, dk_acc, dv_acc):
        j = pl.program_id(0)
        lo = _lane_masks()
        dk_acc[...] = jnp.zeros_like(dk_acc)
        dv_acc[...] = jnp.zeros_like(dv_acc)

        def q_block(qi, carry):
            q0 = pl.multiple_of(qi * WINDOW, WINDOW)
            k0 = pl.multiple_of(jnp.maximum(qi - 1, 0) * WINDOW, WINDOW)
            shift = q0 - k0
            qs = _split_heads(q_ref[pl.ds(q0, WINDOW), :], lo)
            dos = _split_heads(do_ref[pl.ds(q0, WINDOW), :], lo)
            oblk = o_ref[pl.ds(q0, WINDOW), :]
            kblk = k_ref[pl.ds(k0, 2 * WINDOW), :]
            vblk = v_ref[pl.ds(k0, 2 * WINDOW), :]
            ks = _split_heads(kblk, lo)
            dq = None
            out = []
            for h in (0, 1):
                sink = sink_ref[2 * j + h]
                lse_h = lse_ref[h, pl.ds(q0, WINDOW), :]
                s = _swa_scores(qs[h], kblk, slope_ref[2 * j + h], shift)
                p = jnp.exp(s - lse_h)
                delta = jnp.sum(dos[h].astype(jnp.float32) * oblk, axis=1, keepdims=True)
                dv_acc[pl.ds(k0, 2 * WINDOW), :] += _mm_tn(p, dos[h])
                dp = _mm_nt(dos[h], vblk)
                ds = p * (dp - delta)
                dqh = _mm(ds, ks[h]) * (HEAD ** -0.5)
                dq = dqh if dq is None else dq + dqh
                dk_acc[pl.ds(k0, 2 * WINDOW), :] += _mm_tn(ds, qs[h]) * (HEAD ** -0.5)
                dsk = jnp.sum(-jnp.exp(sink - lse_h) * delta, axis=0, keepdims=True)
                out.append(carry[h] + dsk)
            dq_ref[pl.ds(q0, WINDOW), :] = dq.astype(dq_ref.dtype)
            return tuple(out)

        zero = jnp.zeros((1, 1), jnp.float32)
        dsa, dsb = lax.fori_loop(0, nb, q_block, (zero, zero))
        dk_ref[...] = dk_acc[...].astype(dk_ref.dtype)
        dv_ref[...] = dv_acc[...].astype(dv_ref.dtype)
        r = lax.broadcasted_iota(jnp.int32, (8, LANES), 0)
        dsink_ref[0] = jnp.where(r == 0, dsa, jnp.where(r == 1, dsb, 0.0))

    smem = pl.BlockSpec(memory_space=pltpu.SMEM)
    slab = pl.BlockSpec((S, LANES), lambda j: (0, j))
    return _pcall(
        body, name="swa_bwd", grid=(npair,), semantics=("arbitrary",),
        in_specs=[smem, smem, slab, slab, slab, slab, slab, pl.BlockSpec((2, S, 1), lambda j: (j, 0, 0))],
        out_specs=[slab, slab, slab, pl.BlockSpec((1, 8, LANES), lambda j: (j, 0, 0))],
        out_shape=[_sds(q.shape, do.dtype), _sds(kd.shape, do.dtype), _sds(vd.shape, do.dtype),
                   _sds((npair, 8, LANES), jnp.float32)],
        scratch_shapes=[pltpu.VMEM((S, LANES), jnp.float32), pltpu.VMEM((S, LANES), jnp.float32)],
    )(sinks, slopes, q, kd, vd, do, o, lse)


def _log_steps(S):
    k, out = 1, []
    while k < S:
        out.append(k)
        k *= 2
    return out


def _forget_fwd(f_row, b_col):
    S = f_row.shape[1]

    def body(f_ref, b_ref, lc_ref):
        x = f_ref[...] + b_ref[...]
        lc = jnp.minimum(x, 0.0) - jnp.log(1.0 + jnp.exp(-jnp.abs(x)))
        idx = lax.broadcasted_iota(jnp.int32, lc.shape, 1)
        for k in _log_steps(S):
            lc = lc + jnp.where(idx >= k, pltpu.roll(lc, k, axis=1), 0.0)
        lc_ref[...] = lc

    return _pcall(body, name="forget_fwd", out_shape=_sds(f_row.shape, jnp.float32))(f_row, b_col)


def _forget_bwd(dlc_row, f_row, b_col):
    S = f_row.shape[1]

    def body(d_ref, f_ref, b_ref, df_ref, db_ref):
        g = d_ref[...]
        idx = lax.broadcasted_iota(jnp.int32, g.shape, 1)
        for k in _log_steps(S):
            g = g + jnp.where(idx < S - k, pltpu.roll(g, S - k, axis=1), 0.0)
        x = f_ref[...] + b_ref[...]
        df = g * _sigmoid(-x)
        df_ref[...] = df
        db_ref[...] = jnp.sum(df, axis=1, keepdims=True)

    return _pcall(body, name="forget_bwd",
                  out_shape=[_sds(f_row.shape, jnp.float32), _sds((f_row.shape[0], 1), jnp.float32)])(dlc_row, f_row, b_col)


def _layer0_out_layer1_in(x, o_m, o_s, gate, w_out, g1, w_in1):
    S = x.shape[0]

    def body(x_ref, om_ref, os_ref, gate_ref, wo_ref, g_ref, w_ref,
             x1_ref, u_ref, h_ref, q_ref, k_ref, v_ref, g1_ref, f_ref):
        gt = gate_ref[...]
        sg = gt * _sigmoid(gt)
        um = om_ref[...] * sg[:, :512]
        us = os_ref[...] * sg[:, 512:]
        u_ref[:, :512] = um.astype(u_ref.dtype)
        u_ref[:, 512:] = us.astype(u_ref.dtype)
        x1 = x_ref[...] + _mm(um, wo_ref[0:512, :]) + _mm(us, wo_ref[512:1024, :])
        x1_ref[...] = x1
        h = _rms(x1, g_ref[...])
        h_ref[...] = h.astype(h_ref.dtype)
        z = _mm(h, w_ref[...])
        q_ref[...] = z[:, 0:1024].astype(q_ref.dtype)
        k_ref[...] = z[:, 1024:2048].astype(k_ref.dtype)
        v_ref[...] = z[:, 2048:3072].astype(v_ref.dtype)
        g1_ref[...] = z[:, 3072:4096]
        f_ref[...] = z[:, 4096:4224]

    outs = [((S, D), jnp.float32), ((S, D), MXU), ((S, D), MXU), ((S, D), MXU), ((S, D), MXU), ((S, D), MXU),
            ((S, D), jnp.float32), ((S, LANES), jnp.float32)]
    return _pcall(
        body, name="layer0_out_layer1_in", grid=(S // TOK,), semantics=("arbitrary",),
        in_specs=[_rows(TOK, D), _rows(TOK, 512), _rows(TOK, 512), _rows(TOK, D), _full((D, D)), _full((1, D)),
                  _full(w_in1.shape)],
        out_specs=[_rows(TOK, s[1]) for s, _ in outs],
        out_shape=[_sds(s, d) for s, d in outs],
    )(x, o_m, o_s, gate, w_out, g1, w_in1)


def _head(x1, o1, gate1, w_out1, g_f, target):
    S = x1.shape[0]

    def body(x1_ref, o_ref, gate_ref, wo_ref, g_ref, t_ref,
             loss_ref, dgf_ref, dx2_ref, u_ref, do_ref, dgate_ref):
        i = pl.program_id(0)
        gt = gate_ref[...]
        sig = _sigmoid(gt)
        sg = gt * sig
        o = o_ref[...]
        u = o * sg
        u_ref[...] = u.astype(u_ref.dtype)
        x2 = x1_ref[...] + _mm(u, wo_ref[...])
        g = g_ref[...]
        y = _rms(x2, g)
        err = y - t_ref[...]
        part = 0.5 * jnp.sum(jnp.mean(err * err, axis=-1, keepdims=True), axis=0, keepdims=True)
        dy = err * (1.0 / D)
        dx2, dg_rows = _rms_bwd(x2, g, dy)
        dx2_ref[...] = dx2
        du = _mm_nt(dx2, wo_ref[...])
        do_ref[...] = (du * sg).astype(do_ref.dtype)
        dgate_ref[...] = (du * o * (sig * (1.0 + gt * (1.0 - sig)))).astype(dgate_ref.dtype)

        @pl.when(i == 0)
        def _():
            loss_ref[...] = jnp.zeros_like(loss_ref)
            dgf_ref[...] = jnp.zeros_like(dgf_ref)

        loss_ref[...] += jnp.broadcast_to(part, loss_ref.shape)
        dgf_ref[...] += jnp.sum(dg_rows, axis=0, keepdims=True)

    outs = [((S, D), jnp.float32), ((S, D), MXU), ((S, D), MXU), ((S, D), MXU)]
    return _pcall(
        body, name="head", grid=(S // TOK,), semantics=("arbitrary",),
        in_specs=[_rows(TOK, D), _rows(TOK, D), _rows(TOK, D), _full((D, D)), _full((1, D)), _rows(TOK, D)],
        out_specs=[_full((8, LANES)), _full((1, D))] + [_rows(TOK, D) for _ in outs],
        out_shape=[_sds((8, LANES), jnp.float32), _sds((1, D), jnp.float32)] + [_sds(s, d) for s, d in outs],
    )(x1, o1, gate1, w_out1, g_f, target)


def _layer1_in_bwd(dq, dk, dv, dgate1, df, x1, dx2, g1, w_in1, gate0, o_m, o_s, w_out0):
    S = x1.shape[0]

    def body(dq_ref, dk_ref, dv_ref, dg1_ref, df_ref, x1_ref, dx2_ref, g_ref, w_ref, gate_ref, om_ref, os_ref,
             wo_ref, dz_ref, dx1_ref, dgn_ref, dom_ref, dos_ref, dgate_ref):
        i = pl.program_id(0)
        dz_ref[:, 0:1024] = dq_ref[...]
        dz_ref[:, 1024:2048] = dk_ref[...]
        dz_ref[:, 2048:3072] = dv_ref[...]
        dz_ref[:, 3072:4096] = dg1_ref[...]
        dz_ref[:, 4096:4224] = df_ref[...]
        dh = _mm_nt(dz_ref[...], w_ref[...])
        g = g_ref[...]
        dxn, dg_rows = _rms_bwd(x1_ref[...], g, dh)
        dx1 = dx2_ref[...] + dxn
        dx1_ref[...] = dx1
        du = _mm_nt(dx1, wo_ref[...])
        gt = gate_ref[...]
        sig = _sigmoid(gt)
        sg = gt * sig
        dsg = sig * (1.0 + gt * (1.0 - sig))
        dom_ref[...] = (du[:, :512] * sg[:, :512]).astype(dom_ref.dtype)
        dos_ref[...] = (du[:, 512:] * sg[:, 512:]).astype(dos_ref.dtype)
        dgate_ref[:, :512] = (du[:, :512] * om_ref[...] * dsg[:, :512]).astype(dgate_ref.dtype)
        dgate_ref[:, 512:] = (du[:, 512:] * os_ref[...] * dsg[:, 512:]).astype(dgate_ref.dtype)

        @pl.when(i == 0)
        def _():
            dgn_ref[...] = jnp.zeros_like(dgn_ref)

        dgn_ref[...] += jnp.sum(dg_rows, axis=0, keepdims=True)

    return _pcall(
        body, name="layer1_in_bwd", grid=(S // TOK,), semantics=("arbitrary",),
        in_specs=[_rows(TOK, D), _rows(TOK, D), _rows(TOK, D), _rows(TOK, D), _rows(TOK, LANES), _rows(TOK, D),
                  _rows(TOK, D), _full((1, D)), _full(w_in1.shape), _rows(TOK, D), _rows(TOK, 512), _rows(TOK, 512),
                  _full((D, D))],
        out_specs=[_rows(TOK, 4224), _rows(TOK, D), _full((1, D)), _rows(TOK, 512), _rows(TOK, 512), _rows(TOK, D)],
        out_shape=[_sds((S, 4224), MXU), _sds((S, D), jnp.float32), _sds((1, D), jnp.float32),
                   _sds((S, 512), MXU), _sds((S, 512), MXU), _sds((S, D), MXU)],
    )(dq, dk, dv, dgate1, df, x1, dx2, g1, w_in1, gate0, o_m, o_s, w_out0)


def _layer0_in_bwd(dqm, dkm, dvm, dqs, dkd, dvd, dgate0, cos, sin, cq, ckv, x, dx1, g_in, w_in, g_q, w_q, g_kv, w_kv):
    S = x.shape[0]
    consts = _rope_consts()

    def body(dqm_ref, dkm_ref, dvm_ref, dqs_ref, dkd_ref, dvd_ref, dgate_ref, cos_ref, sin_ref, c_ref, cq_ref, ckv_ref,
             x_ref, dx1_ref, g_ref, w_ref, gq_ref, wq_ref, gkv_ref, wkv_ref,
             dx_ref, dz_ref, dqu_ref, dkvu_ref, dgin_ref, dgq_ref, dgkv_ref):
        i = pl.program_id(0)
        lo = _lane_masks()
        sign = c_ref[...][1:2, :]
        c = cos_ref[...]
        s = sin_ref[...]
        dkpe = None
        for hd in range(N_MLA):
            sl = slice(LANES * hd, LANES * (hd + 1))
            dqu_ref[:, sl] = _rope_t(dqm_ref[:, sl], c, s, sign).astype(dqu_ref.dtype)
            dkh = dkm_ref[:, sl]
            dkvu_ref[:, sl] = jnp.where(lo, dkh, 0.0).astype(dkvu_ref.dtype)
            dkpe = dkh if dkpe is None else dkpe + dkh
        dkvu_ref[:, 1024:1536] = dvm_ref[...]
        dkpe = _rope_t(jnp.where(lo, 0.0, dkpe), c, s, sign)
        dcqn = _mm_nt(dqu_ref[...], wq_ref[...])
        dckvn = _mm_nt(dkvu_ref[...], wkv_ref[...])
        gq = gq_ref[...]
        gkv = gkv_ref[...]
        dcq, dgq_rows = _rms_bwd(cq_ref[...], gq, dcqn)
        dckv, dgkv_rows = _rms_bwd(ckv_ref[...], gkv, dckvn)
        dz_ref[:, 0:256] = dcq.astype(dz_ref.dtype)
        dz_ref[:, 256:384] = dckv.astype(dz_ref.dtype)
        dz_ref[:, 384:512] = dkpe.astype(dz_ref.dtype)
        dz_ref[:, 512:1024] = dqs_ref[...]
        dz_ref[:, 1024:1536] = dkd_ref[...]
        dz_ref[:, 1536:2048] = dvd_ref[...]
        dz_ref[:, 2048:3072] = dgate_ref[...]
        dh = _mm_nt(dz_ref[...], w_ref[...])
        g = g_ref[...]
        dxn, dg_rows = _rms_bwd(x_ref[...], g, dh)
        dx_ref[...] = dx1_ref[...] + dxn

        @pl.when(i == 0)
        def _():
            dgin_ref[...] = jnp.zeros_like(dgin_ref)
            dgq_ref[...] = jnp.zeros_like(dgq_ref)
            dgkv_ref[...] = jnp.zeros_like(dgkv_ref)

        dgin_ref[...] += jnp.sum(dg_rows, axis=0, keepdims=True)
        dgq_ref[...] += jnp.sum(dgq_rows, axis=0, keepdims=True)
        dgkv_ref[...] += jnp.sum(dgkv_rows, axis=0, keepdims=True)

    return _pcall(
        body, name="layer0_in_bwd", grid=(S // TOK,), semantics=("arbitrary",),
        in_specs=[_rows(TOK, 1024), _rows(TOK, 1024), _rows(TOK, 512), _rows(TOK, 512), _rows(TOK, 512), _rows(TOK, 512),
                  _rows(TOK, D), _rows(TOK, LANES), _rows(TOK, LANES), _full((8, LANES)), _rows(TOK, 256), _rows(TOK, 128),
                  _rows(TOK, D), _rows(TOK, D), _full((1, D)), _full(w_in.shape), _full((1, 256)), _full(w_q.shape),
                  _full((1, 128)), _full(w_kv.shape)],
        out_specs=[_rows(TOK, D), _rows(TOK, 3072), _rows(TOK, 1024), _rows(TOK, 1536), _full((1, D)), _full((1, 256)),
                   _full((1, 128))],
        out_shape=[_sds((S, D), jnp.float32), _sds((S, 3072), MXU), _sds((S, 1024), MXU), _sds((S, 1536), MXU),
                   _sds((1, D), jnp.float32), _sds((1, 256), jnp.float32), _sds((1, 128), jnp.float32)],
    )(dqm, dkm, dvm, dqs, dkd, dvd, dgate0, cos, sin, consts, cq, ckv, x, dx1, g_in, w_in, g_q, w_q, g_kv, w_kv)


def _wgrad(a, b, name):
    S, M = a.shape
    N = b.shape[1]
    tn = 512 if N % 512 == 0 else (384 if N % 384 == 0 else LANES)
    tk = min(512, S)

    def body(a_ref, b_ref, o_ref):
        @pl.when(pl.program_id(1) == 0)
        def _():
            o_ref[...] = jnp.zeros_like(o_ref)

        o_ref[...] += _mm_tn(a_ref[...], b_ref[...])

    return _pcall(
        body, name=name, grid=(N // tn, S // tk), semantics=("parallel", "arbitrary"),
        in_specs=[pl.BlockSpec((tk, M), lambda n, k: (k, 0)), pl.BlockSpec((tk, tn), lambda n, k: (k, n))],
        out_specs=pl.BlockSpec((M, tn), lambda n, k: (0, n)),
        out_shape=_sds((M, N), jnp.float32),
    )(a, b)


def _adamw(w, g, m, v, name):
    shape = w.shape
    R, C = (int(np.prod(shape[:-1])), shape[-1])
    w2, g2, m2, v2 = (t.reshape(R, C) for t in (w, g, m, v))
    tr = 256 if R % 256 == 0 else R

    def body(w_ref, g_ref, m_ref, v_ref, d_ref, nm_ref, nv_ref):
        gg = g_ref[...]
        nm = B1 * m_ref[...] + (1.0 - B1) * gg
        nv = B2 * v_ref[...] + (1.0 - B2) * (gg * gg)
        m_hat = nm / (1.0 - B1 ** STEP)
        v_hat = nv / (1.0 - B2 ** STEP)
        d_ref[...] = -LR * (m_hat / (jnp.sqrt(v_hat) + AEPS) + WD * w_ref[...])
        nm_ref[...] = nm
        nv_ref[...] = nv

    spec = _rows(tr, C)
    d, nm, nv = _pcall(
        body, name=name, grid=(R // tr,), semantics=("parallel",),
        in_specs=[spec] * 4, out_specs=[spec] * 3, out_shape=[_sds((R, C), jnp.float32)] * 3,
    )(w2, g2, m2, v2)
    return d.reshape(shape), nm.reshape(shape), nv.reshape(shape)


def _sum_leading(a, name):
    n, R, C = a.shape
    tr = SUM_ROWS if R % SUM_ROWS == 0 else R

    def body(a_ref, o_ref):
        acc = a_ref[0]
        for i in range(1, n):
            acc = acc + a_ref[i]
        o_ref[...] = acc

    return _pcall(
        body, name=name, grid=(R // tr,), semantics=("parallel",),
        in_specs=[pl.BlockSpec((n, tr, C), lambda i: (0, i, 0))], out_specs=_rows(tr, C),
        out_shape=_sds((R, C), a.dtype),
    )(a)


def _add_blocks(a, b, name, out_dtype):
    n, R, C = a.shape
    tr = SUM_ROWS if R % SUM_ROWS == 0 else R

    def body(a_ref, b_ref, o_ref):
        o_ref[...] = (a_ref[...] + b_ref[...]).astype(o_ref.dtype)

    spec = pl.BlockSpec((1, tr, C), lambda k, i: (k, i, 0))
    return _pcall(
        body, name=name, grid=(n, R // tr), semantics=("parallel", "parallel"),
        in_specs=[spec, spec], out_specs=spec, out_shape=_sds(a.shape, out_dtype),
    )(a, b)


def _total_sum(mine, theirs, recv, name):
    R, C = mine.shape
    n = recv.shape[0]
    tr = SUM_ROWS if R % SUM_ROWS == 0 else R

    def body(a_ref, b_ref, r_ref, o_ref):
        acc = a_ref[...] + b_ref[...]
        for i in range(n):
            acc = acc + r_ref[i].astype(jnp.float32)
        o_ref[...] = acc

    return _pcall(
        body, name=name, grid=(R // tr,), semantics=("parallel",),
        in_specs=[_rows(tr, C), _rows(tr, C), pl.BlockSpec((n, tr, C), lambda i: (0, i, 0))], out_specs=_rows(tr, C),
        out_shape=_sds((R, C), jnp.float32),
    )(mine, theirs, recv)


def _place():
    return lax.axis_index("x"), lax.axis_index("y"), lax.axis_index("c")


def _all_gather8(block, name):
    R, C = block.shape

    def body(x_ref, out_ref, send_sems, recv_sems, local_sem):
        x, y, c = _place()
        me, sibling = (x, y, c), (x, y, 1 - c)
        chips = [(1 - x, y), (x, 1 - y), (1 - x, 1 - y)]

        def slot(px, py, pc):
            return out_ref.at[4 * px + 2 * py + pc]

        def copy(k, blk, to, src=None):
            return pltpu.make_async_remote_copy(
                src_ref=slot(*blk) if src is None else src, dst_ref=slot(*blk),
                send_sem=send_sems.at[k], recv_sem=recv_sems.at[k], device_id=to, device_id_type=MESH_ID)

        mine = pltpu.make_async_copy(x_ref, slot(*me), local_sem)
        mine.start()
        first = [copy(0, me, sibling, src=x_ref)]
        first += [copy(1 + j, me, (*chip, c), src=x_ref) for j, chip in enumerate(chips)]
        for cp in first:
            cp.start()
        passed = [copy(4 + j, (*chip, c), sibling) for j, chip in enumerate(chips)]
        for j, chip in enumerate(chips):
            copy(1 + j, (*chip, c), me).wait_recv()
            passed[j].start()
        copy(0, sibling, me).wait_recv()
        for j, chip in enumerate(chips):
            copy(4 + j, (*chip, 1 - c), me).wait_recv()
        for cp in first + passed:
            cp.wait_send()
        mine.wait()

    any_spec = pl.BlockSpec(memory_space=pl.ANY)
    return _pcall(
        body, name=name, in_specs=[any_spec], out_specs=any_spec, out_shape=_sds((8, R, C), block.dtype),
        scratch_shapes=[pltpu.SemaphoreType.DMA((7,)), pltpu.SemaphoreType.DMA((7,)), pltpu.SemaphoreType.DMA],
    )(block)


def _pair_swap(g, name):
    n = g.shape[0]

    def body(g_ref, out_ref, send_sems, recv_sems):
        x, y, c = _place()
        cps = [pltpu.make_async_remote_copy(src_ref=g_ref.at[k, 1 - c], dst_ref=out_ref.at[k], send_sem=send_sems.at[k],
                                            recv_sem=recv_sems.at[k], device_id=(x, y, 1 - c), device_id_type=MESH_ID)
               for k in range(n)]
        for cp in cps:
            cp.start()
        for cp in cps:
            cp.wait()

    any_spec = pl.BlockSpec(memory_space=pl.ANY)
    return _pcall(
        body, name=name, in_specs=[any_spec], out_specs=any_spec, out_shape=_sds((n,) + g.shape[2:], g.dtype),
        scratch_shapes=[pltpu.SemaphoreType.DMA((n,)), pltpu.SemaphoreType.DMA((n,))],
    )(g)


def _chip_exchange(p, name):
    def body(p_ref, out_ref, send_sems, recv_sems):
        x, y, c = _place()
        chips = [(1 - x, y), (x, 1 - y), (1 - x, 1 - y)]
        cps = [pltpu.make_async_remote_copy(
            src_ref=p_ref.at[2 * cx + cy], dst_ref=out_ref.at[j], send_sem=send_sems.at[j],
            recv_sem=recv_sems.at[j], device_id=(cx, cy, c), device_id_type=MESH_ID)
            for j, (cx, cy) in enumerate(chips)]
        for cp in cps:
            cp.start()
        for cp in cps:
            cp.wait()

    any_spec = pl.BlockSpec(memory_space=pl.ANY)
    return _pcall(
        body, name=name, in_specs=[any_spec], out_specs=any_spec, out_shape=_sds((3,) + p.shape[1:], p.dtype),
        scratch_shapes=[pltpu.SemaphoreType.DMA((3,)), pltpu.SemaphoreType.DMA((3,))],
    )(p)


def _pair_exchange(t, name):
    def body(t_ref, out_ref, send_sem, recv_sem):
        x, y, c = _place()
        cp = pltpu.make_async_remote_copy(src_ref=t_ref, dst_ref=out_ref, send_sem=send_sem, recv_sem=recv_sem,
                                          device_id=(x, y, 1 - c), device_id_type=MESH_ID)
        cp.start()
        cp.wait()

    any_spec = pl.BlockSpec(memory_space=pl.ANY)
    return _pcall(
        body, name=name, in_specs=[any_spec], out_specs=any_spec, out_shape=_sds(t.shape, t.dtype),
        scratch_shapes=[pltpu.SemaphoreType.DMA, pltpu.SemaphoreType.DMA],
    )(t)


def _prep_w_in0(w):
    z = jnp.zeros((w.shape[0], 32), w.dtype)
    z64 = jnp.zeros((w.shape[0], 64), w.dtype)
    k0, k1 = w[:, 928:992], w[:, 992:1056]
    v0, v1 = w[:, 1056:1120], w[:, 1120:1184]
    return jnp.concatenate([w[:, 0:384], z64, w[:, 384:416], z, w[:, 416:928],
                            k0, k0, k0, k0, k1, k1, k1, k1, v0, v0, v0, v0, v1, v1, v1, v1, w[:, 1184:2208]], axis=1)


def _fold_w_in0(d):
    def fold(blk):
        b = blk.reshape(blk.shape[0], 8, 64)
        return jnp.concatenate([b[:, 0] + b[:, 1] + b[:, 2] + b[:, 3], b[:, 4] + b[:, 5] + b[:, 6] + b[:, 7]], axis=1)
    return jnp.concatenate([d[:, 0:384], d[:, 448:480], d[:, 512:1024], fold(d[:, 1024:1536]), fold(d[:, 1536:2048]),
                            d[:, 2048:3072]], axis=1)


def _prep_w_q(w):
    return jnp.pad(w.reshape(Q_RANK, N_MLA, 96), ((0, 0), (0, 0), (0, 32))).reshape(Q_RANK, 1024)


def _fold_w_q(d):
    return d.reshape(Q_RANK, N_MLA, 128)[:, :, :96].reshape(Q_RANK, 768)


def _prep_w_kv(w):
    w3 = w.reshape(KV_RANK, N_MLA, 128)
    kk = jnp.pad(w3[:, :, :64], ((0, 0), (0, 0), (0, 64))).reshape(KV_RANK, 1024)
    return jnp.concatenate([kk, w3[:, :, 64:].reshape(KV_RANK, 512)], axis=1)


def _fold_w_kv(d):
    kk = d[:, :1024].reshape(KV_RANK, N_MLA, 128)[:, :, :64]
    vv = d[:, 1024:].reshape(KV_RANK, N_MLA, 64)
    return jnp.concatenate([kk, vv], axis=2).reshape(KV_RANK, 1024)


def _prep_w_in1(w):
    return jnp.concatenate([w[:, 0:3072], w[:, 3088:4112], w[:, 3072:3088], jnp.zeros((w.shape[0], 112), w.dtype)], axis=1)


def _fold_w_in1(d):
    return jnp.concatenate([d[:, 0:3072], d[:, 4096:4112], d[:, 3072:4096]], axis=1)


def _local_step(x, pos, target, e_g_in, w_in0, e_g_q, w_q, e_g_kv, w_kv, sinks, w_out0, o_g_in, w_in1, b_f, w_out1, g_final):
    S = x.shape[0]
    w_in0p, w_qp, w_kvp, w_in1p = _prep_w_in0(w_in0), _prep_w_q(w_q), _prep_w_kv(w_kv), _prep_w_in1(w_in1)
    slopes = jnp.asarray(2.0 ** (-8.0 * (np.arange(N_SWA, dtype=np.float32) + 1.0) / N_SWA), jnp.float32)
    sinks1 = sinks.reshape(N_SWA)
    b_col = b_f.reshape(N_FOX, 1)

    (h0, cq, ckv, cqn, ckvn, qm, km, vm, qs, kd, vd, gate0, cos, sin) = _layer0_in(
        x, pos, e_g_in, w_in0p, e_g_q, w_qp, e_g_kv, w_kvp)
    o_m, lse_m = _attn_fwd_t(qm, km, vm, (NOPE + ROPE) ** -0.5, split=True, name="mla_fwd")
    o_s, lse_s = _swa_fwd(qs, kd, vd, sinks1, slopes)
    x1, u0, h1, q1, k1, v1, gate1, f_slab = _layer0_out_layer1_in(x, o_m, o_s, gate0, w_out0, o_g_in, w_in1p)
    f_row = f_slab[:, :N_FOX].T
    lc_row = _forget_fwd(f_row, b_col)
    lcc = lc_row.reshape(N_FOX, S, 1)
    o1, lse1 = _attn_fwd_t(q1, k1, v1, HEAD ** -0.5, split=False, name="fox_fwd", lcc=lcc)
    loss8, dg_final, dx2, u1, do1, dgate1 = _head(x1, o1, gate1, w_out1, g_final, target)

    dq1, dk1, dv1, dlc = _attn_bwd_t(q1, k1, v1, do1, o1, lse1, HEAD ** -0.5, split=False, name="fox_bwd", lcc=lcc)
    df_row, db_f = _forget_bwd(dlc.reshape(N_FOX, S), f_row, b_col)
    df_slab = jnp.pad(df_row.T, ((0, 0), (0, LANES - N_FOX))).astype(MXU)
    dz1, dx1, dg_o_in, do_m, do_s, dgate0 = _layer1_in_bwd(
        dq1, dk1, dv1, dgate1, df_slab, x1, dx2, o_g_in, w_in1p, gate0, o_m, o_s, w_out0)
    dqs, dkd, dvd, dsink = _swa_bwd(qs, kd, vd, do_s, o_s, lse_s, sinks1, slopes)
    dqm, dkm, dvm = _attn_bwd_t(qm, km, vm, do_m, o_m, lse_m, (NOPE + ROPE) ** -0.5, split=True, name="mla_bwd")
    dx, dz0, dqu, dkvu, dg_in, dg_q, dg_kv = _layer0_in_bwd(
        dqm, dkm, dvm, dqs, dkd, dvd, dgate0, cos, sin, cq, ckv, x, dx1, e_g_in, w_in0p, e_g_q, w_qp, e_g_kv, w_kvp)

    grads = dict(
        e_g_in=dg_in,
        e_w_in=_fold_w_in0(_wgrad(h0, dz0, "wgrad_in0")),
        e_g_q_a=dg_q,
        e_w_q_up=_fold_w_q(_wgrad(cqn, dqu, "wgrad_q_up")),
        e_g_kv_a=dg_kv,
        e_w_kv_up=_fold_w_kv(_wgrad(ckvn, dkvu, "wgrad_kv_up")),
        e_sinks=dsink[:, 0:2, 0].reshape(1, N_SWA),
        e_w_out=_wgrad(u0, dx1, "wgrad_out0"),
        o_g_in=dg_o_in,
        o_w_in=_fold_w_in1(_wgrad(h1, dz1, "wgrad_in1")),
        o_b_f=db_f.reshape(1, N_FOX),
        o_w_out=_wgrad(u1, dx2, "wgrad_out1"),
        g_final=dg_final,
    )
    return loss8[0, 0], dx, grads


SHARDED = ("e_w_in", "e_w_q_up", "e_w_kv_up", "e_w_out", "o_g_in", "o_w_in", "o_w_out")
COL_SHARDED = ("e_w_in", "e_w_q_up", "e_w_kv_up", "o_g_in", "o_w_in")
REPLICATED = ("e_g_in", "e_g_q_a", "e_g_kv_a", "e_sinks", "o_b_f", "g_final")
FULL_SHAPES = dict(e_w_in=(1024, 2208), e_w_q_up=(256, 768), e_w_kv_up=(128, 1024), e_w_out=(1024, 1024),
                   o_g_in=(1, 1024), o_w_in=(1024, 4112), o_w_out=(1024, 1024))
WINDOWS = dict(e_w_in=(0, 0), o_w_in=(0, 552), e_w_out=(1024, 0), e_w_q_up=(1024, 1024), e_w_kv_up=(1024, 1216),
               o_w_out=(1280, 0), o_g_in=(1536, 0))


def _shard_shape(name):
    r, c = FULL_SHAPES[name]
    return (r, c // 4) if name in COL_SHARDED else (r // 4, c)


def _pack_block(p):
    dt = p["e_w_in"].dtype

    def z(r, c):
        return jnp.zeros((r, c), dt)

    band_a = jnp.concatenate([p["e_w_in"], p["o_w_in"], z(1024, PACK_COLS - 1580)], axis=1)
    small = jnp.concatenate([p["e_w_kv_up"], z(128, 256)], axis=0)
    band_b = jnp.concatenate([p["e_w_out"], p["e_w_q_up"], small, z(256, PACK_COLS - 1472)], axis=1)
    band_c = jnp.concatenate([p["o_w_out"], z(256, PACK_COLS - 1024)], axis=1)
    g = p["o_g_in"]
    band_d = jnp.pad(g, ((0, PACK_ROWS - 1536 - g.shape[0]), (0, PACK_COLS - g.shape[1])))
    return jnp.concatenate([band_a, band_b, band_c, band_d], axis=0)


def _window(block, name, width=None):
    r0, c0 = WINDOWS[name]
    r, c = _shard_shape(name)
    return block[..., r0:r0 + r, c0:c0 + (c if width is None else width)]


def _chip_slice(name, full, k):
    r, c = _shard_shape(name)
    return full[:, c * k:c * (k + 1)] if name in COL_SHARDED else full[r * k:r * (k + 1), :]


def kernel(x, positions, e_g_in, e_w_in, e_g_q_a, e_w_q_up, e_g_kv_a, e_w_kv_up, e_sinks, e_w_out, o_g_in, o_w_in, o_b_f, o_w_out, g_final, loss_target, m_e_g_in, m_e_w_in, m_e_g_q_a, m_e_w_q_up, m_e_g_kv_a, m_e_w_kv_up, m_e_sinks, m_e_w_out, m_o_g_in, m_o_w_in, m_o_b_f, m_o_w_out, m_g_final, v_e_g_in, v_e_w_in, v_e_g_q_a, v_e_w_q_up, v_e_g_kv_a, v_e_w_kv_up, v_e_sinks, v_e_w_out, v_o_g_in, v_o_w_in, v_o_b_f, v_o_w_out, v_g_final):
    w = dict(e_g_in=e_g_in, e_w_in=e_w_in, e_g_q_a=e_g_q_a, e_w_q_up=e_w_q_up, e_g_kv_a=e_g_kv_a, e_w_kv_up=e_w_kv_up,
             e_sinks=e_sinks, e_w_out=e_w_out, o_g_in=o_g_in, o_w_in=o_w_in, o_b_f=o_b_f, o_w_out=o_w_out, g_final=g_final)
    m = dict(e_g_in=m_e_g_in, e_w_in=m_e_w_in, e_g_q_a=m_e_g_q_a, e_w_q_up=m_e_w_q_up, e_g_kv_a=m_e_g_kv_a,
             e_w_kv_up=m_e_w_kv_up, e_sinks=m_e_sinks, e_w_out=m_e_w_out, o_g_in=m_o_g_in, o_w_in=m_o_w_in, o_b_f=m_o_b_f,
             o_w_out=m_o_w_out, g_final=m_g_final)
    v = dict(e_g_in=v_e_g_in, e_w_in=v_e_w_in, e_g_q_a=v_e_g_q_a, e_w_q_up=v_e_w_q_up, e_g_kv_a=v_e_g_kv_a,
             e_w_kv_up=v_e_w_kv_up, e_sinks=v_e_sinks, e_w_out=v_e_w_out, o_g_in=v_o_g_in, o_w_in=v_o_w_in, o_b_f=v_o_b_f,
             o_w_out=v_o_w_out, g_final=v_g_final)
    order = ("e_g_in", "e_w_in", "e_g_q_a", "e_w_q_up", "e_g_kv_a", "e_w_kv_up", "e_sinks", "e_w_out", "o_g_in", "o_w_in",
             "o_b_f", "o_w_out", "g_final")
    c = lax.axis_index("c")
    chip = 2 * lax.axis_index("x") + lax.axis_index("y")

    parts = {}
    for n in SHARDED:
        a = w[n][0] if w[n].ndim == 3 else w[n]
        if n == "o_g_in":
            parts[n] = lax.bitcast_convert_type(a, jnp.bfloat16).reshape(1, -1)
        else:
            parts[n] = a.astype(jnp.bfloat16)
    mine_w = _pack_block(parts).reshape(2, HALF_ROWS, PACK_COLS)
    gathered = _all_gather8(lax.dynamic_index_in_dim(mine_w, c, 0, keepdims=False), "gather_weights")
    blocks = gathered.reshape(4, PACK_ROWS, PACK_COLS)
    full = {}
    for n in SHARDED:
        if n == "o_g_in":
            halves = _window(blocks, n, width=512).reshape(4, 1, 256, 2)
            full[n] = jnp.concatenate(list(lax.bitcast_convert_type(halves, jnp.float32)), axis=1)
        else:
            pieces = [_window(blocks[k], n) for k in range(4)]
            full[n] = jnp.concatenate(pieces, axis=1 if n in COL_SHARDED else 0).astype(MXU)

    loss_part, dx, grads = _local_step(
        x[0], positions.reshape(-1, 1), loss_target[0], e_g_in, full["e_w_in"], e_g_q_a, full["e_w_q_up"], e_g_kv_a,
        full["e_w_kv_up"], e_sinks, full["e_w_out"], full["o_g_in"], full["o_w_in"], o_b_f, full["o_w_out"],
        g_final.reshape(1, D))
    loss = lax.psum(loss_part, ("x", "y", "c"))

    per_chip = jnp.stack([_pack_block({n: _chip_slice(n, grads[n], k) for n in SHARDED}) for k in range(4)])
    g4 = per_chip.reshape(4, 2, HALF_ROWS, PACK_COLS)
    theirs = _pair_swap(g4, "grad_pair_swap")
    mine = lax.dynamic_index_in_dim(g4, c, 1, keepdims=False)
    pair_sums = _add_blocks(mine, theirs, "grad_pair_add", jnp.bfloat16)
    received = _chip_exchange(pair_sums, "grad_chip_exchange")
    my_half = _total_sum(lax.dynamic_index_in_dim(mine, chip, 0, keepdims=False),
                         lax.dynamic_index_in_dim(theirs, chip, 0, keepdims=False), received, "grad_chip_sum")
    other_half = _pair_exchange(my_half, "grad_pair_exchange")
    total = jnp.concatenate([jnp.where(c == 0, my_half, other_half), jnp.where(c == 0, other_half, my_half)], axis=0)
    gsum = {n: _window(total, n).reshape(w[n].shape) for n in SHARDED}

    small = jnp.concatenate([jnp.pad(grads[n].reshape(-1), (0, (-grads[n].size) % LANES)) for n in REPLICATED])
    rows = small.shape[0] // LANES
    small = jnp.pad(small.reshape(rows, LANES), ((0, (-rows) % 8), (0, 0)))
    ssum = _sum_leading(_all_gather8(small, "gather_small_grads"), "small_grad_sum").reshape(-1)
    off = 0
    for n in REPLICATED:
        cnt = w[n].size
        gsum[n] = ssum[off:off + cnt].reshape(w[n].shape)
        off += cnt + (-cnt) % LANES

    delta, new_m, new_v = {}, {}, {}
    for n in order:
        delta[n], new_m[n], new_v[n] = _adamw(w[n], gsum[n], m[n], v[n], "adamw_" + n)
    return (loss, dx[None], *[gsum[n] for n in order], *[delta[n] for n in order], *[new_m[n] for n in order],
            *[new_v[n] for n in order])
```

```python
import functools
import math

import numpy as np
import jax
import jax.numpy as jnp
from jax import lax
from jax.experimental import pallas as pl
from jax.experimental.pallas import tpu as pltpu

D = 1024
EPS = 1e-6
ROPE_THETA = 10000.0
N_MLA = 8
Q_RANK = 256
KV_RANK = 128
NOPE = 64
ROPE = 32
N_SWA = 8
WINDOW = 128
N_FOX = 16
HEAD = 64
E_SPLITS = (256, 128, 32, 512, 128, 128, 1024)
O_SPLITS = (1024, 1024, 1024, 16, 1024)
LR, B1, B2, AEPS, WD, STEP = 0.001, 0.9, 0.999, 1e-08, 0.01, 10

LANES = 128
HALF = 64
VMEM_LIMIT = 56 * 1024 * 1024
MXU = jnp.bfloat16
TOK = 256
WG_TOK = 2048
ATT = 256
FWD_CHUNK = 2
BWD_CHUNK = 2
SWA_GROUP = 4
NEG = float("-inf")

PACK_COLS = 1664
PACK_ROWS = 1568
HALF_ROWS = PACK_ROWS // 2
SUM_ROWS = 112
MESH_ID = pl.DeviceIdType.MESH


def _pcall(body, *, name, vmem=VMEM_LIMIT, semantics=None, **kw):
    params = dict(vmem_limit_bytes=vmem)
    if semantics is not None:
        params["dimension_semantics"] = semantics
    return pl.pallas_call(body, name=name, compiler_params=pltpu.CompilerParams(**params), **kw)


def _mm(a, b):
    return jnp.dot(a.astype(MXU), b.astype(MXU), preferred_element_type=jnp.float32)


def _mm_nt(a, b):
    return lax.dot_general(a.astype(MXU), b.astype(MXU), (((1,), (1,)), ((), ())),
                           preferred_element_type=jnp.float32)


def _mm_tn(a, b):
    return lax.dot_general(a.astype(MXU), b.astype(MXU), (((0,), (0,)), ((), ())),
                           preferred_element_type=jnp.float32)


def _full(shape):
    n = len(shape)
    return pl.BlockSpec(shape, lambda *_: (0,) * n)


def _rows(tm, n):
    return pl.BlockSpec((tm, n), lambda i: (i, 0))


def _sds(shape, dtype):
    return jax.ShapeDtypeStruct(shape, dtype)


def _rms(x, g):
    r = lax.rsqrt(jnp.mean(x * x, axis=-1, keepdims=True) + EPS)
    return x * r * g


def _rms_bwd(x, g, dy):
    r = lax.rsqrt(jnp.mean(x * x, axis=-1, keepdims=True) + EPS)
    xh = x * r
    dxh = dy * g
    dx = r * (dxh - xh * jnp.mean(dxh * xh, axis=-1, keepdims=True))
    return dx, dy * xh


def _sigmoid(x):
    return 1.0 / (1.0 + jnp.exp(-x))


def _lane_masks(dtype=None):
    lane = lax.broadcasted_iota(jnp.int32, (1, LANES), 1)
    return lane < HALF


def _split_heads(a, lo):
    z = jnp.zeros_like(a)
    return [jnp.where(lo, a, z), jnp.where(lo, z, a)]


def _rope_consts():
    inv = np.zeros((8, LANES), np.float32)
    j = np.arange(ROPE // 2, dtype=np.float32)
    f = (1.0 / (ROPE_THETA ** (np.arange(0, ROPE, 2, dtype=np.float32) / ROPE))).astype(np.float32)
    inv[0, HALF:HALF + 16] = f
    inv[0, HALF + 16:HALF + 32] = f
    inv[1, HALF:HALF + 16] = -1.0
    inv[1, HALF + 16:HALF + 32] = 1.0
    del j
    return jnp.asarray(inv)


def _rope_tables(pos_f, consts):
    ang = pos_f * consts[0:1, :]
    sign = consts[1:2, :]
    c = jnp.where(sign != 0.0, jnp.cos(ang), 1.0)
    s = jnp.sin(ang) * sign
    return c, s


def _swap_halves(v, sign):
    lo = pltpu.roll(v, LANES - 16, axis=1)
    hi = pltpu.roll(v, 16, axis=1)
    return jnp.where(sign < 0.0, lo, jnp.where(sign > 0.0, hi, 0.0))


def _rope(x, c, s, sign):
    return x * c + _swap_halves(x, sign) * s


def _rope_t(dy, c, s, sign):
    return dy * c + _swap_halves(dy * s, sign)


def _layer0_in(x, pos, g_in, w_in, g_q, w_q, g_kv, w_kv):
    S = x.shape[0]
    consts = _rope_consts()

    def body(x_ref, pos_ref, c_ref, g_ref, w_ref, gq_ref, wq_ref, gkv_ref, wkv_ref,
             h_ref, cq_ref, ckv_ref, cqn_ref, ckvn_ref, qm_ref, km_ref, vm_ref,
             qs_ref, kd_ref, vd_ref, gate_ref, cos_ref, sin_ref):
        h = _rms(x_ref[...], g_ref[...])
        h_ref[...] = h.astype(h_ref.dtype)
        z = _mm(h, w_ref[...])
        cq = z[:, 0:256]
        ckv = z[:, 256:384]
        kpe = z[:, 384:512]
        cq_ref[...] = cq
        ckv_ref[...] = ckv
        qs_ref[...] = z[:, 512:1024].astype(qs_ref.dtype)
        kd_ref[...] = z[:, 1024:1536].astype(kd_ref.dtype)
        vd_ref[...] = z[:, 1536:2048].astype(vd_ref.dtype)
        gate_ref[...] = z[:, 2048:3072]
        cqn = _rms(cq, gq_ref[...])
        ckvn = _rms(ckv, gkv_ref[...])
        cqn_ref[...] = cqn.astype(cqn_ref.dtype)
        ckvn_ref[...] = ckvn.astype(ckvn_ref.dtype)
        q = _mm(cqn, wq_ref[...])
        kv = _mm(ckvn, wkv_ref[...])
        vm_ref[...] = kv[:, 1024:1536].astype(vm_ref.dtype)
        consts_v = c_ref[...]
        sign = consts_v[1:2, :]
        c, s = _rope_tables(pos_ref[...].astype(jnp.float32), consts_v)
        cos_ref[...] = c
        sin_ref[...] = s
        kpe_r = _rope(kpe, c, s, sign)
        for hd in range(N_MLA):
            sl = slice(LANES * hd, LANES * (hd + 1))
            qm_ref[:, sl] = _rope(q[:, sl], c, s, sign).astype(qm_ref.dtype)
            km_ref[:, sl] = (kv[:, sl] + kpe_r).astype(km_ref.dtype)

    outs = [
        ((S, D), MXU), ((S, 256), jnp.float32), ((S, 128), jnp.float32), ((S, 256), MXU), ((S, 128), MXU),
        ((S, 1024), MXU), ((S, 1024), MXU), ((S, 512), MXU), ((S, 512), MXU), ((S, 512), MXU), ((S, 512), MXU),
        ((S, 1024), jnp.float32), ((S, 128), jnp.float32), ((S, 128), jnp.float32),
    ]
    return _pcall(
        body, name="layer0_in", grid=(S // TOK,), semantics=("arbitrary",),
        in_specs=[_rows(TOK, D), _rows(TOK, 1), _full((8, LANES)), _full((1, D)), _full(w_in.shape), _full((1, 256)),
                  _full(w_q.shape), _full((1, 128)), _full(w_kv.shape)],
        out_specs=[_rows(TOK, s[1]) for s, _ in outs],
        out_shape=[_sds(s, d) for s, d in outs],
    )(x, pos, consts, g_in, w_in, g_q, w_q, g_kv, w_kv)


AUG = (HALF, 0)
ONE = (HALF + 8, 8)


def _data_lanes(idx, h):
    return (idx < HALF) if h == 0 else (idx >= HALF)


def _three_terms(x):
    hi = x.astype(MXU).astype(jnp.float32)
    mid = (x - hi).astype(MXU).astype(jnp.float32)
    lo = (x - hi - mid).astype(MXU).astype(jnp.float32)
    return hi, mid, lo


def _q_aug(qblk, lc, h, scale, lane):
    a = AUG[h]
    hi, mid, lo = _three_terms(lc)
    ones = ((lane >= a + 3) & (lane <= a + 5)).astype(jnp.float32)
    aug = jnp.where(lane == a, hi, jnp.where(lane == a + 1, mid, jnp.where(lane == a + 2, lo, ones)))
    return jnp.where(_data_lanes(lane, h), qblk * jnp.asarray(scale, qblk.dtype), aug.astype(qblk.dtype))


def _k_aug(kblk, lc, h, lane):
    a = AUG[h]
    hi, mid, lo = _three_terms(-lc)
    ones = ((lane >= a) & (lane <= a + 2)).astype(jnp.float32)
    aug = jnp.where(lane == a + 3, hi, jnp.where(lane == a + 4, mid, jnp.where(lane == a + 5, lo, ones)))
    return jnp.where(_data_lanes(lane, h), kblk, aug.astype(kblk.dtype))


def _attn_fwd_t(q, k, v, scale, *, split, name, lcc=None):
    S = q.shape[0]
    npair = v.shape[1] // LANES
    W = 2 * LANES if split else LANES
    T = ATT
    CH = FWD_CHUNK * T
    assert S % CH == 0
    nq = S // T

    def body(*refs):
        if split:
            q_ref, k_ref, v_ref, o_ref, lse_ref, vt, acc, m_sc = refs
        else:
            q_ref, k_ref, v_ref, lcc_ref, o_ref, lse_ref, kaug, vt, acc, m_sc = refs
        lane = lax.broadcasted_iota(jnp.int32, (1, LANES), 1)
        sub = lax.broadcasted_iota(jnp.int32, (LANES, 1), 0)
        key_minus_qry = lax.broadcasted_iota(jnp.int32, (CH, T), 0) - lax.broadcasted_iota(jnp.int32, (CH, T), 1)

        def prep(i, c):
            r0 = pl.multiple_of(i * T, T)
            vblk = v_ref[pl.ds(r0, T), :].astype(jnp.float32)
            for h in (0, 1):
                vh = jnp.where(_data_lanes(lane, h), vblk, (lane == ONE[h]).astype(jnp.float32))
                vt[h, :, pl.ds(r0, T)] = vh.T.astype(vt.dtype)
                if not split:
                    kaug[h, pl.ds(r0, T), :] = _k_aug(k_ref[pl.ds(r0, T), :], lcc_ref[h, pl.ds(r0, T), :], h, lane)
            return c

        lax.fori_loop(0, nq, prep, 0)

        def queries(qi):
            q0 = pl.multiple_of(qi * T, T)
            qblk = q_ref[pl.ds(q0, T), :]
            if split:
                return (qblk[:, :LANES], qblk[:, LANES:])
            return tuple(_q_aug(qblk, lcc_ref[h, pl.ds(q0, T), :], h, scale, lane) for h in (0, 1))

        def scores(qs, c):
            k0 = pl.multiple_of(c * CH, CH)
            out = []
            for h in (0, 1):
                if split:
                    out.append(_mm_nt(k_ref[pl.ds(k0, CH), LANES * h:LANES * (h + 1)], qs[h]) * scale)
                else:
                    out.append(_mm_nt(kaug[h, pl.ds(k0, CH), :], qs[h]))
            return tuple(out)

        def q_block(qi, carry):
            qs, first_scores = carry[:2], carry[2:]
            q0 = pl.multiple_of(qi * T, T)
            acc[...] = jnp.zeros_like(acc)
            m_sc[...] = jnp.full(m_sc.shape, NEG, jnp.float32)

            def absorb(c, sts, masked):
                k0 = pl.multiple_of(c * CH, CH)
                for h in (0, 1):
                    st = sts[h]
                    if masked:
                        st = jnp.where(key_minus_qry <= q0 - k0, st, NEG)
                    m_old = m_sc[h:h + 1, :]
                    m_new = jnp.maximum(m_old, jnp.max(st, axis=0, keepdims=True))
                    alpha = jnp.exp(m_old - m_new)
                    pt = jnp.exp(st - m_new)
                    acc[h] = alpha * acc[h] + _mm(vt[h, :, pl.ds(k0, CH)], pt)
                    m_sc[h:h + 1, :] = m_new

            last = qi // FWD_CHUNK

            def pipelined(c, sts):
                nxt = scores(qs, c + 1)
                absorb(c, sts, False)
                return nxt

            sts = lax.fori_loop(0, last, pipelined, first_scores)
            qs_next = queries(jnp.minimum(qi + 1, nq - 1))
            nxt = qs_next + scores(qs_next, 0)
            absorb(last, sts, True)
            ot = None
            for h in (0, 1):
                a = acc[h]
                l = a[ONE[h]:ONE[h] + 1, :]
                oh = jnp.where(_data_lanes(sub, h), a * (1.0 / l), 0.0)
                ot = oh if ot is None else ot + oh
                lse_ref[0, h:h + 1, pl.ds(q0, T)] = m_sc[h:h + 1, :] + jnp.log(l)
            o_ref[pl.ds(q0, T), :] = ot.T
            return nxt

        qs0 = queries(0)
        lax.fori_loop(0, nq, q_block, qs0 + scores(qs0, 0))

    wide = pl.BlockSpec((S, W), lambda j: (0, j))
    slab = pl.BlockSpec((S, LANES), lambda j: (0, j))
    rows = pl.BlockSpec((1, 2, S), lambda j: (j, 0, 0))
    in_specs = [wide, wide, slab]
    args = [q, k, v]
    scratch = []
    if not split:
        in_specs.append(pl.BlockSpec((2, S, 1), lambda j: (j, 0, 0)))
        args.append(lcc)
        scratch.append(pltpu.VMEM((2, S, LANES), MXU))
    scratch += [pltpu.VMEM((2, LANES, S), MXU), pltpu.VMEM((2, LANES, T), jnp.float32), pltpu.VMEM((8, T), jnp.float32)]
    return _pcall(
        body, name=name, grid=(npair,), semantics=("arbitrary",),
        in_specs=in_specs, out_specs=[slab, rows],
        out_shape=[_sds((S, npair * LANES), jnp.float32), _sds((npair, 2, S), jnp.float32)],
        scratch_shapes=scratch,
    )(*args)


def _attn_bwd_t(q, k, v, do, o, lse, scale, *, split, name, lcc=None):
    S = q.shape[0]
    npair = v.shape[1] // LANES
    W = 2 * LANES if split else LANES
    T = ATT
    CH = BWD_CHUNK * T
    assert S % CH == 0
    nq = S // T

    def body(*refs):
        if split:
            (q_ref, k_ref, v_ref, do_ref, o_ref, lse_ref, dq_ref, dk_ref, dv_ref, dqt, delta, dk_acc, dv_acc) = refs
        else:
            (q_ref, k_ref, v_ref, do_ref, o_ref, lse_ref, lcc_ref, dq_ref, dk_ref, dv_ref, dlc_ref,
             dqt, delta, dk_acc, dv_acc, qaug, csum) = refs
        lane = lax.broadcasted_iota(jnp.int32, (1, LANES), 1)
        sub = lax.broadcasted_iota(jnp.int32, (LANES, 1), 0)
        key_minus_qry = lax.broadcasted_iota(jnp.int32, (T, CH), 0) - lax.broadcasted_iota(jnp.int32, (T, CH), 1)

        def prep(i, c):
            r0 = pl.multiple_of(i * T, T)
            prod_t = (do_ref[pl.ds(r0, T), :].astype(jnp.float32) * o_ref[pl.ds(r0, T), :]).T
            for h in (0, 1):
                delta[h:h + 1, pl.ds(r0, T)] = jnp.sum(jnp.where(_data_lanes(sub, h), prod_t, 0.0), axis=0, keepdims=True)
                dqt[h, :, pl.ds(r0, T)] = jnp.zeros((LANES, T), jnp.float32)
                if not split:
                    qaug[h, pl.ds(r0, T), :] = _q_aug(q_ref[pl.ds(r0, T), :], lcc_ref[h, pl.ds(r0, T), :], h, scale, lane)
            return c

        lax.fori_loop(0, nq, prep, 0)

        def keys(ki):
            k0 = pl.multiple_of(ki * T, T)
            kblk = k_ref[pl.ds(k0, T), :]
            if split:
                return (kblk[:, :LANES], kblk[:, LANES:])
            return tuple(_k_aug(kblk, lcc_ref[h, pl.ds(k0, T), :], h, lane) for h in (0, 1))

        def q_of(c, h):
            q0 = pl.multiple_of(c * CH, CH)
            if split:
                return q_ref[pl.ds(q0, CH), LANES * h:LANES * (h + 1)]
            return qaug[h, pl.ds(q0, CH), :]

        def scores(khs, c):
            out = []
            for h in (0, 1):
                st = _mm_nt(khs[h], q_of(c, h))
                out.append(st * scale if split else st)
            return tuple(out)

        def k_block(ki, carry):
            khs, first_scores = carry[:2], carry[2:]
            k0 = pl.multiple_of(ki * T, T)
            khts = [kh.astype(jnp.float32).T.astype(kh.dtype) for kh in khs]
            vhs = _split_heads(v_ref[pl.ds(k0, T), :], lane < HALF)
            dk_acc[...] = jnp.zeros_like(dk_acc)
            dv_acc[...] = jnp.zeros_like(dv_acc)

            def absorb(c, vals):
                q0 = pl.multiple_of(c * CH, CH)
                dos = _split_heads(do_ref[pl.ds(q0, CH), :], lane < HALF)
                visible = key_minus_qry <= q0 - k0
                for h in (0, 1):
                    dpt = _mm_nt(vhs[h], dos[h])
                    st = jnp.where(visible, vals[h], NEG)
                    pt = jnp.exp(st - lse_ref[0, h:h + 1, pl.ds(q0, CH)])
                    dv_acc[...] += _mm(pt, dos[h])
                    dst = pt * (dpt - delta[h:h + 1, pl.ds(q0, CH)])
                    dk_acc[h] += _mm(dst, q_of(c, h))
                    dqt[h, :, pl.ds(q0, CH)] += _mm(khts[h], dst)

            first = ki // BWD_CHUNK

            def pipelined(c, vals):
                nxt = scores(khs, c + 1)
                absorb(c, vals)
                return nxt

            vals = lax.fori_loop(first, S // CH - 1, pipelined, first_scores)
            kn = jnp.minimum(ki + 1, nq - 1)
            khs_next = keys(kn)
            nxt = khs_next + scores(khs_next, kn // BWD_CHUNK)
            absorb(S // CH - 1, vals)
            if split:
                dk_ref[pl.ds(k0, T), :LANES] = (dk_acc[0] * scale).astype(dk_ref.dtype)
                dk_ref[pl.ds(k0, T), LANES:] = (dk_acc[1] * scale).astype(dk_ref.dtype)
            else:
                dk_ref[pl.ds(k0, T), :] = jnp.where(lane < HALF, dk_acc[0], dk_acc[1]).astype(dk_ref.dtype)
                for h in (0, 1):
                    csum[h:h + 1, pl.ds(k0, T)] = dk_acc[h].T[AUG[h] + 3:AUG[h] + 4, :]
            dv_ref[pl.ds(k0, T), :] = dv_acc[...].astype(dv_ref.dtype)
            return nxt

        khs0 = keys(0)
        lax.fori_loop(0, nq, k_block, khs0 + scores(khs0, 0))

        def finish(i, c):
            r0 = pl.multiple_of(i * T, T)
            if split:
                for h in (0, 1):
                    dq_ref[pl.ds(r0, T), LANES * h:LANES * (h + 1)] = (dqt[h, :, pl.ds(r0, T)].T * scale).astype(dq_ref.dtype)
            else:
                d = jnp.where(sub < HALF, dqt[0, :, pl.ds(r0, T)], dqt[1, :, pl.ds(r0, T)])
                dq_ref[pl.ds(r0, T), :] = (d.T * scale).astype(dq_ref.dtype)
                for h in (0, 1):
                    dlc_ref[0, h:h + 1, pl.ds(r0, T)] = dqt[h, AUG[h]:AUG[h] + 1, pl.ds(r0, T)] - csum[h:h + 1, pl.ds(r0, T)]
            return c

        lax.fori_loop(0, nq, finish, 0)

    wide = pl.BlockSpec((S, W), lambda j: (0, j))
    slab = pl.BlockSpec((S, LANES), lambda j: (0, j))
    rows = pl.BlockSpec((1, 2, S), lambda j: (j, 0, 0))
    in_specs = [wide, wide, slab, slab, slab, rows]
    args = [q, k, v, do, o, lse]
    out_specs = [wide, wide, slab]
    out_shape = [_sds(q.shape, jnp.float32 if split else do.dtype), _sds(k.shape, jnp.float32 if split else do.dtype),
                 _sds(v.shape, do.dtype)]
    scratch = [pltpu.VMEM((2, LANES, S), jnp.float32), pltpu.VMEM((8, S), jnp.float32),
               pltpu.VMEM((2, T, LANES), jnp.float32), pltpu.VMEM((T, LANES), jnp.float32)]
    if not split:
        in_specs.append(pl.BlockSpec((2, S, 1), lambda j: (j, 0, 0)))
        args.append(lcc)
        out_specs.append(rows)
        out_shape.append(_sds((npair, 2, S), jnp.float32))
        scratch += [pltpu.VMEM((2, S, LANES), MXU), pltpu.VMEM((8, S), jnp.float32)]
    return _pcall(
        body, name=name, grid=(npair,), semantics=("arbitrary",),
        in_specs=in_specs, out_specs=out_specs, out_shape=out_shape, scratch_shapes=scratch,
    )(*args)


def _swa_scores(qh, kblk, slope, shift):
    s = _mm_nt(qh, kblk) * (HEAD ** -0.5)
    a = lax.broadcasted_iota(jnp.int32, (WINDOW, 2 * WINDOW), 0)
    c = lax.broadcasted_iota(jnp.int32, (WINDOW, 2 * WINDOW), 1)
    dist = a - c + shift
    s = s - slope * dist.astype(jnp.float32)
    return jnp.where((dist >= 0) & (dist < WINDOW), s, NEG)


def _swa_fwd(q, kd, vd, sinks, slopes):
    S = q.shape[0]
    npair = q.shape[1] // LANES
    nb = S // WINDOW

    def body(sink_ref, slope_ref, q_ref, k_ref, v_ref, o_ref, lse_ref):
        j = pl.program_id(0)
        lo = _lane_masks()

        def q_block(qi, c):
            q0 = pl.multiple_of(qi * WINDOW, WINDOW)
            k0 = pl.multiple_of(jnp.maximum(qi - 1, 0) * WINDOW, WINDOW)
            shift = q0 - k0
            qs = _split_heads(q_ref[pl.ds(q0, WINDOW), :], lo)
            kblk = k_ref[pl.ds(k0, 2 * WINDOW), :]
            vs = _split_heads(v_ref[pl.ds(k0, 2 * WINDOW), :], lo)
            o = None
            for h in (0, 1):
                sink = sink_ref[2 * j + h]
                s = _swa_scores(qs[h], kblk, slope_ref[2 * j + h], shift)
                m = jnp.maximum(jnp.max(s, axis=1, keepdims=True), sink)
                p = jnp.exp(s - m)
                den = jnp.sum(p, axis=1, keepdims=True) + jnp.exp(sink - m)
                oh = _mm(p / den, vs[h])
                o = oh if o is None else o + oh
                lse_ref[h, pl.ds(q0, WINDOW), :] = m + jnp.log(den)
            o_ref[pl.ds(q0, WINDOW), :] = o
            return c

        def q_group(gi, c):
            for g in range(SWA_GROUP):
                q_block(gi * SWA_GROUP + g, c)
            return c

        lax.fori_loop(0, nb // SWA_GROUP, q_group, 0)

    smem = pl.BlockSpec(memory_space=pltpu.SMEM)
    slab = pl.BlockSpec((S, LANES), lambda j: (0, j))
    return _pcall(
        body, name="swa_fwd", grid=(npair,), semantics=("arbitrary",),
        in_specs=[smem, smem, slab, slab, slab],
        out_specs=[slab, pl.BlockSpec((2, S, 1), lambda j: (j, 0, 0))],
        out_shape=[_sds((S, npair * LANES), jnp.float32), _sds((2 * npair, S, 1), jnp.float32)],
    )(sinks, slopes, q, kd, vd)


def _swa_bwd(q, kd, vd, do, o, lse, sinks, slopes):
    S = q.shape[0]
    npair = q.shape[1] // LANES
    nb = S // WINDOW

    def body(sink_ref, slope_ref, q_ref, k_ref, v_ref, do_ref, o_ref, lse_ref,
             dq_ref, dk_ref, dv_ref, dsink_ref, dk_acc, dv_acc):
        j = pl.program_id(0)
        lo = _lane_masks()
        dk_acc[...] = jnp.zeros_like(dk_acc)
        dv_acc[...] = jnp.zeros_like(dv_acc)

        def q_block(qi, carry):
            q0 = pl.multiple_of(qi * WINDOW, WINDOW)
            k0 = pl.multiple_of(jnp.maximum(qi - 1, 0) * WINDOW, WINDOW)
            shift = q0 - k0
            qs = _split_heads(q_ref[pl.ds(q0, WINDOW), :], lo)
            dos = _split_heads(do_ref[pl.ds(q0, WINDOW), :], lo)
            oblk = o_ref[pl.ds(q0, WINDOW), :]
            kblk = k_ref[pl.ds(k0, 2 * WINDOW), :]
            vblk = v_ref[pl.ds(k0, 2 * WINDOW), :]
            ks = _split_heads(kblk, lo)
            dq = None
            out = []
            for h in (0, 1):
                sink = sink_ref[2 * j + h]
                lse_h = lse_ref[h, pl.ds(q0, WINDOW), :]
                s = _swa_scores(qs[h], kblk, slope_ref[2 * j + h], shift)
                p = jnp.exp(s - lse_h)
                delta = jnp.sum(dos[h].astype(jnp.float32) * oblk, axis=1, keepdims=True)
                dv_acc[pl.ds(k0, 2 * WINDOW), :] += _mm_tn(p, dos[h])
                dp = _mm_nt(dos[h], vblk)
                ds = p * (dp - delta)
                dqh = _mm(ds, ks[h]) * (HEAD ** -0.5)
                dq = dqh if dq is None else dq + dqh
                dk_acc[pl.ds(k0, 2 * WINDOW), :] += _mm_tn(ds, qs[h]) * (HEAD ** -0.5)
                dsk = jnp.sum(-jnp.exp(sink - lse_h) * delta, axis=0, keepdims=True)
                out.append(carry[h] + dsk)
            dq_ref[pl.ds(q0, WINDOW), :] = dq.astype(dq_ref.dtype)
            return tuple(out)

        def q_group(gi, carry):
            for g in range(SWA_GROUP):
                carry = q_block(gi * SWA_GROUP + g, carry)
            return carry

        zero = jnp.zeros((1, 1), jnp.float32)
        dsa, dsb = lax.fori_loop(0, nb // SWA_GROUP, q_group, (zero, zero))
        dk_ref[...] = dk_acc[...].astype(dk_ref.dtype)
        dv_ref[...] = dv_acc[...].astype(dv_ref.dtype)
        r = lax.broadcasted_iota(jnp.int32, (8, LANES), 0)
        dsink_ref[0] = jnp.where(r == 0, dsa, jnp.where(r == 1, dsb, 0.0))

    smem = pl.BlockSpec(memory_space=pltpu.SMEM)
    slab = pl.BlockSpec((S, LANES), lambda j: (0, j))
    return _pcall(
        body, name="swa_bwd", grid=(npair,), semantics=("arbitrary",),
        in_specs=[smem, smem, slab, slab, slab, slab, slab, pl.BlockSpec((2, S, 1), lambda j: (j, 0, 0))],
        out_specs=[slab, slab, slab, pl.BlockSpec((1, 8, LANES), lambda j: (j, 0, 0))],
        out_shape=[_sds(q.shape, do.dtype), _sds(kd.shape, do.dtype), _sds(vd.shape, do.dtype),
                   _sds((npair, 8, LANES), jnp.float32)],
        scratch_shapes=[pltpu.VMEM((S, LANES), jnp.float32), pltpu.VMEM((S, LANES), jnp.float32)],
    )(sinks, slopes, q, kd, vd, do, o, lse)


def _log_steps(S):
    k, out = 1, []
    while k < S:
        out.append(k)
        k *= 2
    return out


def _forget_fwd(f_row, b_col):
    S = f_row.shape[1]

    def body(f_ref, b_ref, lc_ref):
        x = f_ref[...] + b_ref[...]
        lc = jnp.minimum(x, 0.0) - jnp.log(1.0 + jnp.exp(-jnp.abs(x)))
        idx = lax.broadcasted_iota(jnp.int32, lc.shape, 1)
        for k in _log_steps(S):
            lc = lc + jnp.where(idx >= k, pltpu.roll(lc, k, axis=1), 0.0)
        lc_ref[...] = lc

    return _pcall(body, name="forget_fwd", out_shape=_sds(f_row.shape, jnp.float32))(f_row, b_col)


def _forget_bwd(dlc_row, f_row, b_col):
    S = f_row.shape[1]

    def body(d_ref, f_ref, b_ref, df_ref, db_ref):
        g = d_ref[...]
        idx = lax.broadcasted_iota(jnp.int32, g.shape, 1)
        for k in _log_steps(S):
            g = g + jnp.where(idx < S - k, pltpu.roll(g, S - k, axis=1), 0.0)
        x = f_ref[...] + b_ref[...]
        df = g * _sigmoid(-x)
        df_ref[...] = df
        db_ref[...] = jnp.sum(df, axis=1, keepdims=True)

    return _pcall(body, name="forget_bwd",
                  out_shape=[_sds(f_row.shape, jnp.float32), _sds((f_row.shape[0], 1), jnp.float32)])(dlc_row, f_row, b_col)


def _layer0_out_layer1_in(x, o_m, o_s, gate, w_out, g1, w_in1):
    S = x.shape[0]

    def body(x_ref, om_ref, os_ref, gate_ref, wo_ref, g_ref, w_ref,
             x1_ref, u_ref, h_ref, q_ref, k_ref, v_ref, g1_ref, f_ref):
        gt = gate_ref[...]
        sg = gt * _sigmoid(gt)
        um = om_ref[...] * sg[:, :512]
        us = os_ref[...] * sg[:, 512:]
        u_ref[:, :512] = um.astype(u_ref.dtype)
        u_ref[:, 512:] = us.astype(u_ref.dtype)
        x1 = x_ref[...] + _mm(um, wo_ref[0:512, :]) + _mm(us, wo_ref[512:1024, :])
        x1_ref[...] = x1
        h = _rms(x1, g_ref[...])
        h_ref[...] = h.astype(h_ref.dtype)
        z = _mm(h, w_ref[...])
        q_ref[...] = z[:, 0:1024].astype(q_ref.dtype)
        k_ref[...] = z[:, 1024:2048].astype(k_ref.dtype)
        v_ref[...] = z[:, 2048:3072].astype(v_ref.dtype)
        g1_ref[...] = z[:, 3072:4096]
        f_ref[...] = z[:, 4096:4224]

    outs = [((S, D), jnp.float32), ((S, D), MXU), ((S, D), MXU), ((S, D), MXU), ((S, D), MXU), ((S, D), MXU),
            ((S, D), jnp.float32), ((S, LANES), jnp.float32)]
    return _pcall(
        body, name="layer0_out_layer1_in", grid=(S // TOK,), semantics=("arbitrary",),
        in_specs=[_rows(TOK, D), _rows(TOK, 512), _rows(TOK, 512), _rows(TOK, D), _full((D, D)), _full((1, D)),
                  _full(w_in1.shape)],
        out_specs=[_rows(TOK, s[1]) for s, _ in outs],
        out_shape=[_sds(s, d) for s, d in outs],
    )(x, o_m, o_s, gate, w_out, g1, w_in1)


def _head(x1, o1, gate1, w_out1, g_f, target):
    S = x1.shape[0]

    def body(x1_ref, o_ref, gate_ref, wo_ref, g_ref, t_ref,
             loss_ref, dgf_ref, dx2_ref, u_ref, do_ref, dgate_ref):
        i = pl.program_id(0)
        gt = gate_ref[...]
        sig = _sigmoid(gt)
        sg = gt * sig
        o = o_ref[...]
        u = o * sg
        u_ref[...] = u.astype(u_ref.dtype)
        x2 = x1_ref[...] + _mm(u, wo_ref[...])
        g = g_ref[...]
        y = _rms(x2, g)
        err = y - t_ref[...]
        part = 0.5 * jnp.sum(jnp.mean(err * err, axis=-1, keepdims=True), axis=0, keepdims=True)
        dy = err * (1.0 / D)
        dx2, dg_rows = _rms_bwd(x2, g, dy)
        dx2_ref[...] = dx2
        du = _mm_nt(dx2, wo_ref[...])
        do_ref[...] = (du * sg).astype(do_ref.dtype)
        dgate_ref[...] = (du * o * (sig * (1.0 + gt * (1.0 - sig)))).astype(dgate_ref.dtype)

        @pl.when(i == 0)
        def _():
            loss_ref[...] = jnp.zeros_like(loss_ref)
            dgf_ref[...] = jnp.zeros_like(dgf_ref)

        loss_ref[...] += jnp.broadcast_to(part, loss_ref.shape)
        dgf_ref[...] += jnp.sum(dg_rows, axis=0, keepdims=True)

    outs = [((S, D), jnp.float32), ((S, D), MXU), ((S, D), MXU), ((S, D), MXU)]
    return _pcall(
        body, name="head", grid=(S // TOK,), semantics=("arbitrary",),
        in_specs=[_rows(TOK, D), _rows(TOK, D), _rows(TOK, D), _full((D, D)), _full((1, D)), _rows(TOK, D)],
        out_specs=[_full((8, LANES)), _full((1, D))] + [_rows(TOK, D) for _ in outs],
        out_shape=[_sds((8, LANES), jnp.float32), _sds((1, D), jnp.float32)] + [_sds(s, d) for s, d in outs],
    )(x1, o1, gate1, w_out1, g_f, target)


def _layer1_in_bwd(dq, dk, dv, dgate1, df, x1, dx2, g1, w_in1, gate0, o_m, o_s, w_out0):
    S = x1.shape[0]

    def body(dq_ref, dk_ref, dv_ref, dg1_ref, df_ref, x1_ref, dx2_ref, g_ref, w_ref, gate_ref, om_ref, os_ref,
             wo_ref, dz_ref, dx1_ref, dgn_ref, dom_ref, dos_ref, dgate_ref):
        i = pl.program_id(0)
        dz_ref[:, 0:1024] = dq_ref[...]
        dz_ref[:, 1024:2048] = dk_ref[...]
        dz_ref[:, 2048:3072] = dv_ref[...]
        dz_ref[:, 3072:4096] = dg1_ref[...]
        dz_ref[:, 4096:4224] = df_ref[...]
        dh = _mm_nt(dz_ref[...], w_ref[...])
        g = g_ref[...]
        dxn, dg_rows = _rms_bwd(x1_ref[...], g, dh)
        dx1 = dx2_ref[...] + dxn
        dx1_ref[...] = dx1
        du = _mm_nt(dx1, wo_ref[...])
        gt = gate_ref[...]
        sig = _sigmoid(gt)
        sg = gt * sig
        dsg = sig * (1.0 + gt * (1.0 - sig))
        dom_ref[...] = (du[:, :512] * sg[:, :512]).astype(dom_ref.dtype)
        dos_ref[...] = (du[:, 512:] * sg[:, 512:]).astype(dos_ref.dtype)
        dgate_ref[:, :512] = (du[:, :512] * om_ref[...] * dsg[:, :512]).astype(dgate_ref.dtype)
        dgate_ref[:, 512:] = (du[:, 512:] * os_ref[...] * dsg[:, 512:]).astype(dgate_ref.dtype)

        @pl.when(i == 0)
        def _():
            dgn_ref[...] = jnp.zeros_like(dgn_ref)

        dgn_ref[...] += jnp.sum(dg_rows, axis=0, keepdims=True)

    return _pcall(
        body, name="layer1_in_bwd", grid=(S // TOK,), semantics=("arbitrary",),
        in_specs=[_rows(TOK, D), _rows(TOK, D), _rows(TOK, D), _rows(TOK, D), _rows(TOK, LANES), _rows(TOK, D),
                  _rows(TOK, D), _full((1, D)), _full(w_in1.shape), _rows(TOK, D), _rows(TOK, 512), _rows(TOK, 512),
                  _full((D, D))],
        out_specs=[_rows(TOK, 4224), _rows(TOK, D), _full((1, D)), _rows(TOK, 512), _rows(TOK, 512), _rows(TOK, D)],
        out_shape=[_sds((S, 4224), MXU), _sds((S, D), jnp.float32), _sds((1, D), jnp.float32),
                   _sds((S, 512), MXU), _sds((S, 512), MXU), _sds((S, D), MXU)],
    )(dq, dk, dv, dgate1, df, x1, dx2, g1, w_in1, gate0, o_m, o_s, w_out0)


def _layer0_in_bwd(dqm, dkm, dvm, dqs, dkd, dvd, dgate0, cos, sin, cq, ckv, x, dx1, g_in, w_in, g_q, w_q, g_kv, w_kv):
    S = x.shape[0]
    consts = _rope_consts()

    def body(dqm_ref, dkm_ref, dvm_ref, dqs_ref, dkd_ref, dvd_ref, dgate_ref, cos_ref, sin_ref, c_ref, cq_ref, ckv_ref,
             x_ref, dx1_ref, g_ref, w_ref, gq_ref, wq_ref, gkv_ref, wkv_ref,
             dx_ref, dz_ref, dqu_ref, dkvu_ref, dgin_ref, dgq_ref, dgkv_ref):
        i = pl.program_id(0)
        lo = _lane_masks()
        sign = c_ref[...][1:2, :]
        c = cos_ref[...]
        s = sin_ref[...]
        dkpe = None
        for hd in range(N_MLA):
            sl = slice(LANES * hd, LANES * (hd + 1))
            dqu_ref[:, sl] = _rope_t(dqm_ref[:, sl], c, s, sign).astype(dqu_ref.dtype)
            dkh = dkm_ref[:, sl]
            dkvu_ref[:, sl] = jnp.where(lo, dkh, 0.0).astype(dkvu_ref.dtype)
            dkpe = dkh if dkpe is None else dkpe + dkh
        dkvu_ref[:, 1024:1536] = dvm_ref[...]
        dkpe = _rope_t(jnp.where(lo, 0.0, dkpe), c, s, sign)
        dcqn = _mm_nt(dqu_ref[...], wq_ref[...])
        dckvn = _mm_nt(dkvu_ref[...], wkv_ref[...])
        gq = gq_ref[...]
        gkv = gkv_ref[...]
        dcq, dgq_rows = _rms_bwd(cq_ref[...], gq, dcqn)
        dckv, dgkv_rows = _rms_bwd(ckv_ref[...], gkv, dckvn)
        dz_ref[:, 0:256] = dcq.astype(dz_ref.dtype)
        dz_ref[:, 256:384] = dckv.astype(dz_ref.dtype)
        dz_ref[:, 384:512] = dkpe.astype(dz_ref.dtype)
        dz_ref[:, 512:1024] = dqs_ref[...]
        dz_ref[:, 1024:1536] = dkd_ref[...]
        dz_ref[:, 1536:2048] = dvd_ref[...]
        dz_ref[:, 2048:3072] = dgate_ref[...]
        dh = _mm_nt(dz_ref[...], w_ref[...])
        g = g_ref[...]
        dxn, dg_rows = _rms_bwd(x_ref[...], g, dh)
        dx_ref[...] = dx1_ref[...] + dxn

        @pl.when(i == 0)
        def _():
            dgin_ref[...] = jnp.zeros_like(dgin_ref)
            dgq_ref[...] = jnp.zeros_like(dgq_ref)
            dgkv_ref[...] = jnp.zeros_like(dgkv_ref)

        dgin_ref[...] += jnp.sum(dg_rows, axis=0, keepdims=True)
        dgq_ref[...] += jnp.sum(dgq_rows, axis=0, keepdims=True)
        dgkv_ref[...] += jnp.sum(dgkv_rows, axis=0, keepdims=True)

    return _pcall(
        body, name="layer0_in_bwd", grid=(S // TOK,), semantics=("arbitrary",),
        in_specs=[_rows(TOK, 1024), _rows(TOK, 1024), _rows(TOK, 512), _rows(TOK, 512), _rows(TOK, 512), _rows(TOK, 512),
                  _rows(TOK, D), _rows(TOK, LANES), _rows(TOK, LANES), _full((8, LANES)), _rows(TOK, 256), _rows(TOK, 128),
                  _rows(TOK, D), _rows(TOK, D), _full((1, D)), _full(w_in.shape), _full((1, 256)), _full(w_q.shape),
                  _full((1, 128)), _full(w_kv.shape)],
        out_specs=[_rows(TOK, D), _rows(TOK, 3072), _rows(TOK, 1024), _rows(TOK, 1536), _full((1, D)), _full((1, 256)),
                   _full((1, 128))],
        out_shape=[_sds((S, D), jnp.float32), _sds((S, 3072), MXU), _sds((S, 1024), MXU), _sds((S, 1536), MXU),
                   _sds((1, D), jnp.float32), _sds((1, 256), jnp.float32), _sds((1, 128), jnp.float32)],
    )(dqm, dkm, dvm, dqs, dkd, dvd, dgate0, cos, sin, consts, cq, ckv, x, dx1, g_in, w_in, g_q, w_q, g_kv, w_kv)


def _wgrad(a, b, name):
    S, M = a.shape
    N = b.shape[1]
    tn = 512 if N % 512 == 0 else (384 if N % 384 == 0 else LANES)
    tk = min(WG_TOK, S)

    def body(a_ref, b_ref, o_ref):
        @pl.when(pl.program_id(1) == 0)
        def _():
            o_ref[...] = jnp.zeros_like(o_ref)

        o_ref[...] += _mm_tn(a_ref[...], b_ref[...])

    return _pcall(
        body, name=name, grid=(N // tn, S // tk), semantics=("parallel", "arbitrary"),
        in_specs=[pl.BlockSpec((tk, M), lambda n, k: (k, 0)), pl.BlockSpec((tk, tn), lambda n, k: (k, n))],
        out_specs=pl.BlockSpec((M, tn), lambda n, k: (0, n)),
        out_shape=_sds((M, N), jnp.float32),
    )(a, b)


def _adamw(w, g, m, v, name):
    shape = w.shape
    R, C = (int(np.prod(shape[:-1])), shape[-1])
    w2, g2, m2, v2 = (t.reshape(R, C) for t in (w, g, m, v))
    tr = 256 if R % 256 == 0 else R

    def body(w_ref, g_ref, m_ref, v_ref, d_ref, nm_ref, nv_ref):
        gg = g_ref[...]
        nm = B1 * m_ref[...] + (1.0 - B1) * gg
        nv = B2 * v_ref[...] + (1.0 - B2) * (gg * gg)
        m_hat = nm / (1.0 - B1 ** STEP)
        v_hat = nv / (1.0 - B2 ** STEP)
        d_ref[...] = -LR * (m_hat / (jnp.sqrt(v_hat) + AEPS) + WD * w_ref[...])
        nm_ref[...] = nm
        nv_ref[...] = nv

    spec = _rows(tr, C)
    d, nm, nv = _pcall(
        body, name=name, grid=(R // tr,), semantics=("parallel",),
        in_specs=[spec] * 4, out_specs=[spec] * 3, out_shape=[_sds((R, C), jnp.float32)] * 3,
    )(w2, g2, m2, v2)
    return d.reshape(shape), nm.reshape(shape), nv.reshape(shape)


def _sum_leading(a, name):
    n, R, C = a.shape
    tr = SUM_ROWS if R % SUM_ROWS == 0 else R

    def body(a_ref, o_ref):
        acc = a_ref[0]
        for i in range(1, n):
            acc = acc + a_ref[i]
        o_ref[...] = acc

    return _pcall(
        body, name=name, grid=(R // tr,), semantics=("parallel",),
        in_specs=[pl.BlockSpec((n, tr, C), lambda i: (0, i, 0))], out_specs=_rows(tr, C),
        out_shape=_sds((R, C), a.dtype),
    )(a)


def _add_blocks(a, b, name, out_dtype):
    n, R, C = a.shape
    tr = SUM_ROWS if R % SUM_ROWS == 0 else R

    def body(a_ref, b_ref, o_ref):
        o_ref[...] = (a_ref[...] + b_ref[...]).astype(o_ref.dtype)

    spec = pl.BlockSpec((1, tr, C), lambda k, i: (k, i, 0))
    return _pcall(
        body, name=name, grid=(n, R // tr), semantics=("parallel", "parallel"),
        in_specs=[spec, spec], out_specs=spec, out_shape=_sds(a.shape, out_dtype),
    )(a, b)


def _total_sum(mine, theirs, recv, name):
    R, C = mine.shape
    n = recv.shape[0]
    tr = SUM_ROWS if R % SUM_ROWS == 0 else R

    def body(a_ref, b_ref, r_ref, o_ref):
        acc = a_ref[...] + b_ref[...]
        for i in range(n):
            acc = acc + r_ref[i].astype(jnp.float32)
        o_ref[...] = acc

    return _pcall(
        body, name=name, grid=(R // tr,), semantics=("parallel",),
        in_specs=[_rows(tr, C), _rows(tr, C), pl.BlockSpec((n, tr, C), lambda i: (0, i, 0))], out_specs=_rows(tr, C),
        out_shape=_sds((R, C), jnp.float32),
    )(mine, theirs, recv)


def _place():
    return lax.axis_index("x"), lax.axis_index("y"), lax.axis_index("c")


def _all_gather8(block, name):
    R, C = block.shape

    def body(x_ref, out_ref, send_sems, recv_sems, local_sem):
        x, y, c = _place()
        me, sibling = (x, y, c), (x, y, 1 - c)
        chips = [(1 - x, y), (x, 1 - y), (1 - x, 1 - y)]

        def slot(px, py, pc):
            return out_ref.at[4 * px + 2 * py + pc]

        def copy(k, blk, to, src=None):
            return pltpu.make_async_remote_copy(
                src_ref=slot(*blk) if src is None else src, dst_ref=slot(*blk),
                send_sem=send_sems.at[k], recv_sem=recv_sems.at[k], device_id=to, device_id_type=MESH_ID)

        mine = pltpu.make_async_copy(x_ref, slot(*me), local_sem)
        mine.start()
        first = [copy(0, me, sibling, src=x_ref)]
        first += [copy(1 + j, me, (*chip, c), src=x_ref) for j, chip in enumerate(chips)]
        for cp in first:
            cp.start()
        passed = [copy(4 + j, (*chip, c), sibling) for j, chip in enumerate(chips)]
        for j, chip in enumerate(chips):
            copy(1 + j, (*chip, c), me).wait_recv()
            passed[j].start()
        copy(0, sibling, me).wait_recv()
        for j, chip in enumerate(chips):
            copy(4 + j, (*chip, 1 - c), me).wait_recv()
        for cp in first + passed:
            cp.wait_send()
        mine.wait()

    any_spec = pl.BlockSpec(memory_space=pl.ANY)
    return _pcall(
        body, name=name, in_specs=[any_spec], out_specs=any_spec, out_shape=_sds((8, R, C), block.dtype),
        scratch_shapes=[pltpu.SemaphoreType.DMA((7,)), pltpu.SemaphoreType.DMA((7,)), pltpu.SemaphoreType.DMA],
    )(block)


def _pair_swap(g, name):
    n = g.shape[0]

    def body(g_ref, out_ref, send_sems, recv_sems):
        x, y, c = _place()
        cps = [pltpu.make_async_remote_copy(src_ref=g_ref.at[k, 1 - c], dst_ref=out_ref.at[k], send_sem=send_sems.at[k],
                                            recv_sem=recv_sems.at[k], device_id=(x, y, 1 - c), device_id_type=MESH_ID)
               for k in range(n)]
        for cp in cps:
            cp.start()
        for cp in cps:
            cp.wait()

    any_spec = pl.BlockSpec(memory_space=pl.ANY)
    return _pcall(
        body, name=name, in_specs=[any_spec], out_specs=any_spec, out_shape=_sds((n,) + g.shape[2:], g.dtype),
        scratch_shapes=[pltpu.SemaphoreType.DMA((n,)), pltpu.SemaphoreType.DMA((n,))],
    )(g)


def _chip_exchange(p, name):
    def body(p_ref, out_ref, send_sems, recv_sems):
        x, y, c = _place()
        chips = [(1 - x, y), (x, 1 - y), (1 - x, 1 - y)]
        cps = [pltpu.make_async_remote_copy(
            src_ref=p_ref.at[2 * cx + cy], dst_ref=out_ref.at[j], send_sem=send_sems.at[j],
            recv_sem=recv_sems.at[j], device_id=(cx, cy, c), device_id_type=MESH_ID)
            for j, (cx, cy) in enumerate(chips)]
        for cp in cps:
            cp.start()
        for cp in cps:
            cp.wait()

    any_spec = pl.BlockSpec(memory_space=pl.ANY)
    return _pcall(
        body, name=name, in_specs=[any_spec], out_specs=any_spec, out_shape=_sds((3,) + p.shape[1:], p.dtype),
        scratch_shapes=[pltpu.SemaphoreType.DMA((3,)), pltpu.SemaphoreType.DMA((3,))],
    )(p)


def _pair_exchange(t, name):
    def body(t_ref, out_ref, send_sem, recv_sem):
        x, y, c = _place()
        cp = pltpu.make_async_remote_copy(src_ref=t_ref, dst_ref=out_ref, send_sem=send_sem, recv_sem=recv_sem,
                                          device_id=(x, y, 1 - c), device_id_type=MESH_ID)
        cp.start()
        cp.wait()

    any_spec = pl.BlockSpec(memory_space=pl.ANY)
    return _pcall(
        body, name=name, in_specs=[any_spec], out_specs=any_spec, out_shape=_sds(t.shape, t.dtype),
        scratch_shapes=[pltpu.SemaphoreType.DMA, pltpu.SemaphoreType.DMA],
    )(t)


def _prep_w_in0(w):
    z = jnp.zeros((w.shape[0], 32), w.dtype)
    z64 = jnp.zeros((w.shape[0], 64), w.dtype)
    k0, k1 = w[:, 928:992], w[:, 992:1056]
    v0, v1 = w[:, 1056:1120], w[:, 1120:1184]
    return jnp.concatenate([w[:, 0:384], z64, w[:, 384:416], z, w[:, 416:928],
                            k0, k0, k0, k0, k1, k1, k1, k1, v0, v0, v0, v0, v1, v1, v1, v1, w[:, 1184:2208]], axis=1)


def _fold_w_in0(d):
    def fold(blk):
        b = blk.reshape(blk.shape[0], 8, 64)
        return jnp.concatenate([b[:, 0] + b[:, 1] + b[:, 2] + b[:, 3], b[:, 4] + b[:, 5] + b[:, 6] + b[:, 7]], axis=1)
    return jnp.concatenate([d[:, 0:384], d[:, 448:480], d[:, 512:1024], fold(d[:, 1024:1536]), fold(d[:, 1536:2048]),
                            d[:, 2048:3072]], axis=1)


def _prep_w_q(w):
    return jnp.pad(w.reshape(Q_RANK, N_MLA, 96), ((0, 0), (0, 0), (0, 32))).reshape(Q_RANK, 1024)


def _fold_w_q(d):
    return d.reshape(Q_RANK, N_MLA, 128)[:, :, :96].reshape(Q_RANK, 768)


def _prep_w_kv(w):
    w3 = w.reshape(KV_RANK, N_MLA, 128)
    kk = jnp.pad(w3[:, :, :64], ((0, 0), (0, 0), (0, 64))).reshape(KV_RANK, 1024)
    return jnp.concatenate([kk, w3[:, :, 64:].reshape(KV_RANK, 512)], axis=1)


def _fold_w_kv(d):
    kk = d[:, :1024].reshape(KV_RANK, N_MLA, 128)[:, :, :64]
    vv = d[:, 1024:].reshape(KV_RANK, N_MLA, 64)
    return jnp.concatenate([kk, vv], axis=2).reshape(KV_RANK, 1024)


def _prep_w_in1(w):
    return jnp.concatenate([w[:, 0:3072], w[:, 3088:4112], w[:, 3072:3088], jnp.zeros((w.shape[0], 112), w.dtype)], axis=1)


def _fold_w_in1(d):
    return jnp.concatenate([d[:, 0:3072], d[:, 4096:4112], d[:, 3072:4096]], axis=1)


def _local_step(x, pos, target, e_g_in, w_in0, e_g_q, w_q, e_g_kv, w_kv, sinks, w_out0, o_g_in, w_in1, b_f, w_out1, g_final):
    S = x.shape[0]
    w_in0p, w_qp, w_kvp, w_in1p = _prep_w_in0(w_in0), _prep_w_q(w_q), _prep_w_kv(w_kv), _prep_w_in1(w_in1)
    slopes = jnp.asarray(2.0 ** (-8.0 * (np.arange(N_SWA, dtype=np.float32) + 1.0) / N_SWA), jnp.float32)
    sinks1 = sinks.reshape(N_SWA)
    b_col = b_f.reshape(N_FOX, 1)

    (h0, cq, ckv, cqn, ckvn, qm, km, vm, qs, kd, vd, gate0, cos, sin) = _layer0_in(
        x, pos, e_g_in, w_in0p, e_g_q, w_qp, e_g_kv, w_kvp)
    o_m, lse_m = _attn_fwd_t(qm, km, vm, (NOPE + ROPE) ** -0.5, split=True, name="mla_fwd")
    o_s, lse_s = _swa_fwd(qs, kd, vd, sinks1, slopes)
    x1, u0, h1, q1, k1, v1, gate1, f_slab = _layer0_out_layer1_in(x, o_m, o_s, gate0, w_out0, o_g_in, w_in1p)
    f_row = f_slab[:, :N_FOX].T
    lc_row = _forget_fwd(f_row, b_col)
    lcc = lc_row.reshape(N_FOX, S, 1)
    o1, lse1 = _attn_fwd_t(q1, k1, v1, HEAD ** -0.5, split=False, name="fox_fwd", lcc=lcc)
    loss8, dg_final, dx2, u1, do1, dgate1 = _head(x1, o1, gate1, w_out1, g_final, target)

    dq1, dk1, dv1, dlc = _attn_bwd_t(q1, k1, v1, do1, o1, lse1, HEAD ** -0.5, split=False, name="fox_bwd", lcc=lcc)
    df_row, db_f = _forget_bwd(dlc.reshape(N_FOX, S), f_row, b_col)
    df_slab = jnp.pad(df_row.T, ((0, 0), (0, LANES - N_FOX))).astype(MXU)
    dz1, dx1, dg_o_in, do_m, do_s, dgate0 = _layer1_in_bwd(
        dq1, dk1, dv1, dgate1, df_slab, x1, dx2, o_g_in, w_in1p, gate0, o_m, o_s, w_out0)
    dqs, dkd, dvd, dsink = _swa_bwd(qs, kd, vd, do_s, o_s, lse_s, sinks1, slopes)
    dqm, dkm, dvm = _attn_bwd_t(qm, km, vm, do_m, o_m, lse_m, (NOPE + ROPE) ** -0.5, split=True, name="mla_bwd")
    dx, dz0, dqu, dkvu, dg_in, dg_q, dg_kv = _layer0_in_bwd(
        dqm, dkm, dvm, dqs, dkd, dvd, dgate0, cos, sin, cq, ckv, x, dx1, e_g_in, w_in0p, e_g_q, w_qp, e_g_kv, w_kvp)

    grads = dict(
        e_g_in=dg_in,
        e_w_in=_fold_w_in0(_wgrad(h0, dz0, "wgrad_in0")),
        e_g_q_a=dg_q,
        e_w_q_up=_fold_w_q(_wgrad(cqn, dqu, "wgrad_q_up")),
        e_g_kv_a=dg_kv,
        e_w_kv_up=_fold_w_kv(_wgrad(ckvn, dkvu, "wgrad_kv_up")),
        e_sinks=dsink[:, 0:2, 0].reshape(1, N_SWA),
        e_w_out=_wgrad(u0, dx1, "wgrad_out0"),
        o_g_in=dg_o_in,
        o_w_in=_fold_w_in1(_wgrad(h1, dz1, "wgrad_in1")),
        o_b_f=db_f.reshape(1, N_FOX),
        o_w_out=_wgrad(u1, dx2, "wgrad_out1"),
        g_final=dg_final,
    )
    return loss8[0, 0], dx, grads


SHARDED = ("e_w_in", "e_w_q_up", "e_w_kv_up", "e_w_out", "o_g_in", "o_w_in", "o_w_out")
COL_SHARDED = ("e_w_in", "e_w_q_up", "e_w_kv_up", "o_g_in", "o_w_in")
REPLICATED = ("e_g_in", "e_g_q_a", "e_g_kv_a", "e_sinks", "o_b_f", "g_final")
FULL_SHAPES = dict(e_w_in=(1024, 2208), e_w_q_up=(256, 768), e_w_kv_up=(128, 1024), e_w_out=(1024, 1024),
                   o_g_in=(1, 1024), o_w_in=(1024, 4112), o_w_out=(1024, 1024))
WINDOWS = dict(e_w_in=(0, 0), o_w_in=(0, 552), e_w_out=(1024, 0), e_w_q_up=(1024, 1024), e_w_kv_up=(1024, 1216),
               o_w_out=(1280, 0), o_g_in=(1536, 0))


def _shard_shape(name):
    r, c = FULL_SHAPES[name]
    return (r, c // 4) if name in COL_SHARDED else (r // 4, c)


def _pack_block(p):
    dt = p["e_w_in"].dtype

    def z(r, c):
        return jnp.zeros((r, c), dt)

    band_a = jnp.concatenate([p["e_w_in"], p["o_w_in"], z(1024, PACK_COLS - 1580)], axis=1)
    small = jnp.concatenate([p["e_w_kv_up"], z(128, 256)], axis=0)
    band_b = jnp.concatenate([p["e_w_out"], p["e_w_q_up"], small, z(256, PACK_COLS - 1472)], axis=1)
    band_c = jnp.concatenate([p["o_w_out"], z(256, PACK_COLS - 1024)], axis=1)
    g = p["o_g_in"]
    band_d = jnp.pad(g, ((0, PACK_ROWS - 1536 - g.shape[0]), (0, PACK_COLS - g.shape[1])))
    return jnp.concatenate([band_a, band_b, band_c, band_d], axis=0)


def _window(block, name, width=None):
    r0, c0 = WINDOWS[name]
    r, c = _shard_shape(name)
    return block[..., r0:r0 + r, c0:c0 + (c if width is None else width)]


def _chip_slice(name, full, k):
    r, c = _shard_shape(name)
    return full[:, c * k:c * (k + 1)] if name in COL_SHARDED else full[r * k:r * (k + 1), :]


def kernel(x, positions, e_g_in, e_w_in, e_g_q_a, e_w_q_up, e_g_kv_a, e_w_kv_up, e_sinks, e_w_out, o_g_in, o_w_in, o_b_f, o_w_out, g_final, loss_target, m_e_g_in, m_e_w_in, m_e_g_q_a, m_e_w_q_up, m_e_g_kv_a, m_e_w_kv_up, m_e_sinks, m_e_w_out, m_o_g_in, m_o_w_in, m_o_b_f, m_o_w_out, m_g_final, v_e_g_in, v_e_w_in, v_e_g_q_a, v_e_w_q_up, v_e_g_kv_a, v_e_w_kv_up, v_e_sinks, v_e_w_out, v_o_g_in, v_o_w_in, v_o_b_f, v_o_w_out, v_g_final):
    w = dict(e_g_in=e_g_in, e_w_in=e_w_in, e_g_q_a=e_g_q_a, e_w_q_up=e_w_q_up, e_g_kv_a=e_g_kv_a, e_w_kv_up=e_w_kv_up,
             e_sinks=e_sinks, e_w_out=e_w_out, o_g_in=o_g_in, o_w_in=o_w_in, o_b_f=o_b_f, o_w_out=o_w_out, g_final=g_final)
    m = dict(e_g_in=m_e_g_in, e_w_in=m_e_w_in, e_g_q_a=m_e_g_q_a, e_w_q_up=m_e_w_q_up, e_g_kv_a=m_e_g_kv_a,
             e_w_kv_up=m_e_w_kv_up, e_sinks=m_e_sinks, e_w_out=m_e_w_out, o_g_in=m_o_g_in, o_w_in=m_o_w_in, o_b_f=m_o_b_f,
             o_w_out=m_o_w_out, g_final=m_g_final)
    v = dict(e_g_in=v_e_g_in, e_w_in=v_e_w_in, e_g_q_a=v_e_g_q_a, e_w_q_up=v_e_w_q_up, e_g_kv_a=v_e_g_kv_a,
             e_w_kv_up=v_e_w_kv_up, e_sinks=v_e_sinks, e_w_out=v_e_w_out, o_g_in=v_o_g_in, o_w_in=v_o_w_in, o_b_f=v_o_b_f,
             o_w_out=v_o_w_out, g_final=v_g_final)
    order = ("e_g_in", "e_w_in", "e_g_q_a", "e_w_q_up", "e_g_kv_a", "e_w_kv_up", "e_sinks", "e_w_out", "o_g_in", "o_w_in",
             "o_b_f", "o_w_out", "g_final")
    c = lax.axis_index("c")
    chip = 2 * lax.axis_index("x") + lax.axis_index("y")

    parts = {}
    for n in SHARDED:
        a = w[n][0] if w[n].ndim == 3 else w[n]
        if n == "o_g_in":
            parts[n] = lax.bitcast_convert_type(a, jnp.bfloat16).reshape(1, -1)
        else:
            parts[n] = a.astype(jnp.bfloat16)
    mine_w = _pack_block(parts).reshape(2, HALF_ROWS, PACK_COLS)
    gathered = _all_gather8(lax.dynamic_index_in_dim(mine_w, c, 0, keepdims=False), "gather_weights")
    blocks = gathered.reshape(4, PACK_ROWS, PACK_COLS)
    full = {}
    for n in SHARDED:
        if n == "o_g_in":
            halves = _window(blocks, n, width=512).reshape(4, 1, 256, 2)
            full[n] = jnp.concatenate(list(lax.bitcast_convert_type(halves, jnp.float32)), axis=1)
        else:
            pieces = [_window(blocks[k], n) for k in range(4)]
            full[n] = jnp.concatenate(pieces, axis=1 if n in COL_SHARDED else 0).astype(MXU)

    loss_part, dx, grads = _local_step(
        x[0], positions.reshape(-1, 1), loss_target[0], e_g_in, full["e_w_in"], e_g_q_a, full["e_w_q_up"], e_g_kv_a,
        full["e_w_kv_up"], e_sinks, full["e_w_out"], full["o_g_in"], full["o_w_in"], o_b_f, full["o_w_out"],
        g_final.reshape(1, D))
    loss = lax.psum(loss_part, ("x", "y", "c"))

    per_chip = jnp.stack([_pack_block({n: _chip_slice(n, grads[n], k) for n in SHARDED}) for k in range(4)])
    g4 = per_chip.reshape(4, 2, HALF_ROWS, PACK_COLS)
    theirs = _pair_swap(g4, "grad_pair_swap")
    mine = lax.dynamic_index_in_dim(g4, c, 1, keepdims=False)
    pair_sums = _add_blocks(mine, theirs, "grad_pair_add", jnp.bfloat16)
    received = _chip_exchange(pair_sums, "grad_chip_exchange")
    my_half = _total_sum(lax.dynamic_index_in_dim(mine, chip, 0, keepdims=False),
                         lax.dynamic_index_in_dim(theirs, chip, 0, keepdims=False), received, "grad_chip_sum")
    other_half = _pair_exchange(my_half, "grad_pair_exchange")
    total = jnp.concatenate([jnp.where(c == 0, my_half, other_half), jnp.where(c == 0, other_half, my_half)], axis=0)
    gsum = {n: _window(total, n).reshape(w[n].shape) for n in SHARDED}

    small = jnp.concatenate([jnp.pad(grads[n].reshape(-1), (0, (-grads[n].size) % LANES)) for n in REPLICATED])
    rows = small.shape[0] // LANES
    small = jnp.pad(small.reshape(rows, LANES), ((0, (-rows) % 8), (0, 0)))
    ssum = _sum_leading(_all_gather8(small, "gather_small_grads"), "small_grad_sum").reshape(-1)
    off = 0
    for n in REPLICATED:
        cnt = w[n].size
        gsum[n] = ssum[off:off + cnt].reshape(w[n].shape)
        off += cnt + (-cnt) % LANES

    delta, new_m, new_v = {}, {}, {}
    for n in order:
        delta[n], new_m[n], new_v[n] = _adamw(w[n], gsum[n], m[n], v[n], "adamw_" + n)
    return (loss, dx[None], *[gsum[n] for n in order], *[delta[n] for n in order], *[new_m[n] for n in order],
            *[new_v[n] for n in order])
```

```python
import functools
import math

import numpy as np
import jax
import jax.numpy as jnp
from jax import lax
from jax.experimental import pallas as pl
from jax.experimental.pallas import tpu as pltpu

D = 1024
EPS = 1e-6
ROPE_THETA = 10000.0
N_MLA = 8
Q_RANK = 256
KV_RANK = 128
NOPE = 64
ROPE = 32
N_SWA = 8
WINDOW = 128
N_FOX = 16
HEAD = 64
E_SPLITS = (256, 128, 32, 512, 128, 128, 1024)
O_SPLITS = (1024, 1024, 1024, 16, 1024)
LR, B1, B2, AEPS, WD, STEP = 0.001, 0.9, 0.999, 1e-08, 0.01, 10

LANES = 128
HALF = 64
VMEM_LIMIT = 56 * 1024 * 1024
MXU = jnp.bfloat16
TOK = 256
WG_TOK = 2048
WG_ROWS = 1536
ATT = 256
FWD_CHUNK = 2
BWD_CHUNK = 2
SWA_GROUP = 4
NEG = float("-inf")

PACK_COLS = 1024
PACK_ROWS = 2304
HALF_ROWS = PACK_ROWS // 2
SUM_ROWS = 144
MESH_ID = pl.DeviceIdType.MESH


def _pcall(body, *, name, vmem=VMEM_LIMIT, semantics=None, **kw):
    params = dict(vmem_limit_bytes=vmem)
    if semantics is not None:
        params["dimension_semantics"] = semantics
    return pl.pallas_call(body, name=name, compiler_params=pltpu.CompilerParams(**params), **kw)


def _mm(a, b):
    return jnp.dot(a.astype(MXU), b.astype(MXU), preferred_element_type=jnp.float32)


def _mm_nt(a, b):
    return lax.dot_general(a.astype(MXU), b.astype(MXU), (((1,), (1,)), ((), ())),
                           preferred_element_type=jnp.float32)


def _mm_tn(a, b):
    return lax.dot_general(a.astype(MXU), b.astype(MXU), (((0,), (0,)), ((), ())),
                           preferred_element_type=jnp.float32)


def _full(shape):
    n = len(shape)
    return pl.BlockSpec(shape, lambda *_: (0,) * n)


def _rows(tm, n):
    return pl.BlockSpec((tm, n), lambda i: (i, 0))


def _sds(shape, dtype):
    return jax.ShapeDtypeStruct(shape, dtype)


def _rms(x, g):
    r = lax.rsqrt(jnp.mean(x * x, axis=-1, keepdims=True) + EPS)
    return x * r * g


def _rms_bwd(x, g, dy):
    r = lax.rsqrt(jnp.mean(x * x, axis=-1, keepdims=True) + EPS)
    xh = x * r
    dxh = dy * g
    dx = r * (dxh - xh * jnp.mean(dxh * xh, axis=-1, keepdims=True))
    return dx, dy * xh


def _sigmoid(x):
    return 1.0 / (1.0 + jnp.exp(-x))


def _lane_masks(dtype=None):
    lane = lax.broadcasted_iota(jnp.int32, (1, LANES), 1)
    return lane < HALF


def _split_heads(a, lo):
    z = jnp.zeros_like(a)
    return [jnp.where(lo, a, z), jnp.where(lo, z, a)]


def _rope_consts():
    inv = np.zeros((8, LANES), np.float32)
    j = np.arange(ROPE // 2, dtype=np.float32)
    f = (1.0 / (ROPE_THETA ** (np.arange(0, ROPE, 2, dtype=np.float32) / ROPE))).astype(np.float32)
    inv[0, HALF:HALF + 16] = f
    inv[0, HALF + 16:HALF + 32] = f
    inv[1, HALF:HALF + 16] = -1.0
    inv[1, HALF + 16:HALF + 32] = 1.0
    del j
    return jnp.asarray(inv)


def _rope_tables(pos_f, consts):
    ang = pos_f * consts[0:1, :]
    sign = consts[1:2, :]
    c = jnp.where(sign != 0.0, jnp.cos(ang), 1.0)
    s = jnp.sin(ang) * sign
    return c, s


def _swap_halves(v, sign):
    lo = pltpu.roll(v, LANES - 16, axis=1)
    hi = pltpu.roll(v, 16, axis=1)
    return jnp.where(sign < 0.0, lo, jnp.where(sign > 0.0, hi, 0.0))


def _rope(x, c, s, sign):
    return x * c + _swap_halves(x, sign) * s


def _rope_t(dy, c, s, sign):
    return dy * c + _swap_halves(dy * s, sign)


def _layer0_in(x, pos, g_in, w_in, g_q, w_q, g_kv, w_kv):
    S = x.shape[0]
    consts = _rope_consts()

    def body(x_ref, pos_ref, c_ref, g_ref, w_ref, gq_ref, wq_ref, gkv_ref, wkv_ref,
             h_ref, cq_ref, ckv_ref, cqn_ref, ckvn_ref, qm_ref, km_ref, vm_ref,
             qs_ref, kd_ref, vd_ref, gate_ref, cos_ref, sin_ref):
        h = _rms(x_ref[...], g_ref[...])
        h_ref[...] = h.astype(h_ref.dtype)
        z = _mm_nt(h, w_ref[...])
        cq = z[:, 0:256]
        ckv = z[:, 256:384]
        kpe = z[:, 384:512]
        cq_ref[...] = cq
        ckv_ref[...] = ckv
        qs_ref[...] = z[:, 512:1024].astype(qs_ref.dtype)
        kd_ref[...] = z[:, 1024:1536].astype(kd_ref.dtype)
        vd_ref[...] = z[:, 1536:2048].astype(vd_ref.dtype)
        gate_ref[...] = z[:, 2048:3072]
        cqn = _rms(cq, gq_ref[...])
        ckvn = _rms(ckv, gkv_ref[...])
        cqn_ref[...] = cqn.astype(cqn_ref.dtype)
        ckvn_ref[...] = ckvn.astype(ckvn_ref.dtype)
        q = _mm_nt(cqn, wq_ref[...])
        kv = _mm(ckvn, wkv_ref[...])
        vm_ref[...] = kv[:, 1024:1536].astype(vm_ref.dtype)
        consts_v = c_ref[...]
        sign = consts_v[1:2, :]
        c, s = _rope_tables(pos_ref[...].astype(jnp.float32), consts_v)
        cos_ref[...] = c
        sin_ref[...] = s
        kpe_r = _rope(kpe, c, s, sign)
        for hd in range(N_MLA):
            sl = slice(LANES * hd, LANES * (hd + 1))
            qm_ref[:, sl] = _rope(q[:, sl], c, s, sign).astype(qm_ref.dtype)
            km_ref[:, sl] = (kv[:, sl] + kpe_r).astype(km_ref.dtype)

    outs = [
        ((S, D), MXU), ((S, 256), jnp.float32), ((S, 128), jnp.float32), ((S, 256), MXU), ((S, 128), MXU),
        ((S, 1024), MXU), ((S, 1024), MXU), ((S, 512), MXU), ((S, 512), MXU), ((S, 512), MXU), ((S, 512), MXU),
        ((S, 1024), jnp.float32), ((S, 128), jnp.float32), ((S, 128), jnp.float32),
    ]
    return _pcall(
        body, name="layer0_in", grid=(S // TOK,), semantics=("arbitrary",),
        in_specs=[_rows(TOK, D), _rows(TOK, 1), _full((8, LANES)), _full((1, D)), _full(w_in.shape), _full((1, 256)),
                  _full(w_q.shape), _full((1, 128)), _full(w_kv.shape)],
        out_specs=[_rows(TOK, s[1]) for s, _ in outs],
        out_shape=[_sds(s, d) for s, d in outs],
    )(x, pos, consts, g_in, w_in, g_q, w_q, g_kv, w_kv)


AUG = (HALF, 0)
ONE = (HALF + 8, 8)


def _data_lanes(idx, h):
    return (idx < HALF) if h == 0 else (idx >= HALF)


def _three_terms(x):
    hi = x.astype(MXU).astype(jnp.float32)
    mid = (x - hi).astype(MXU).astype(jnp.float32)
    lo = (x - hi - mid).astype(MXU).astype(jnp.float32)
    return hi, mid, lo


def _q_aug(qblk, lc, h, scale, lane):
    a = AUG[h]
    hi, mid, lo = _three_terms(lc)
    ones = ((lane >= a + 3) & (lane <= a + 5)).astype(jnp.float32)
    aug = jnp.where(lane == a, hi, jnp.where(lane == a + 1, mid, jnp.where(lane == a + 2, lo, ones)))
    return jnp.where(_data_lanes(lane, h), qblk * jnp.asarray(scale, qblk.dtype), aug.astype(qblk.dtype))


def _k_aug(kblk, lc, h, lane):
    a = AUG[h]
    hi, mid, lo = _three_terms(-lc)
    ones = ((lane >= a) & (lane <= a + 2)).astype(jnp.float32)
    aug = jnp.where(lane == a + 3, hi, jnp.where(lane == a + 4, mid, jnp.where(lane == a + 5, lo, ones)))
    return jnp.where(_data_lanes(lane, h), kblk, aug.astype(kblk.dtype))


def _attn_fwd_t(q, k, v, scale, *, split, name, lcc=None):
    S = q.shape[0]
    npair = v.shape[1] // LANES
    W = 2 * LANES if split else LANES
    T = ATT
    CH = FWD_CHUNK * T
    assert S % CH == 0
    nq = S // T

    def body(*refs):
        if split:
            q_ref, k_ref, v_ref, o_ref, lse_ref, vt, acc, m_sc = refs
        else:
            q_ref, k_ref, v_ref, lcc_ref, o_ref, lse_ref, kaug, vt, acc, m_sc = refs
        lane = lax.broadcasted_iota(jnp.int32, (1, LANES), 1)
        sub = lax.broadcasted_iota(jnp.int32, (LANES, 1), 0)
        key_minus_qry = lax.broadcasted_iota(jnp.int32, (CH, T), 0) - lax.broadcasted_iota(jnp.int32, (CH, T), 1)

        def prep(i, c):
            r0 = pl.multiple_of(i * T, T)
            vblk = v_ref[pl.ds(r0, T), :].astype(jnp.float32)
            for h in (0, 1):
                vh = jnp.where(_data_lanes(lane, h), vblk, (lane == ONE[h]).astype(jnp.float32))
                vt[h, :, pl.ds(r0, T)] = vh.T.astype(vt.dtype)
                if not split:
                    kaug[h, pl.ds(r0, T), :] = _k_aug(k_ref[pl.ds(r0, T), :], lcc_ref[h, pl.ds(r0, T), :], h, lane)
            return c

        lax.fori_loop(0, nq, prep, 0)

        def queries(qi):
            q0 = pl.multiple_of(qi * T, T)
            qblk = q_ref[pl.ds(q0, T), :]
            if split:
                return (qblk[:, :LANES], qblk[:, LANES:])
            return tuple(_q_aug(qblk, lcc_ref[h, pl.ds(q0, T), :], h, scale, lane) for h in (0, 1))

        def scores(qs, c):
            k0 = pl.multiple_of(c * CH, CH)
            out = []
            for h in (0, 1):
                if split:
                    out.append(_mm_nt(k_ref[pl.ds(k0, CH), LANES * h:LANES * (h + 1)], qs[h]) * scale)
                else:
                    out.append(_mm_nt(kaug[h, pl.ds(k0, CH), :], qs[h]))
            return tuple(out)

        def q_block(qi, carry):
            qs, first_scores = carry[:2], carry[2:]
            q0 = pl.multiple_of(qi * T, T)
            acc[...] = jnp.zeros_like(acc)
            m_sc[...] = jnp.full(m_sc.shape, NEG, jnp.float32)

            def absorb(c, sts, masked):
                k0 = pl.multiple_of(c * CH, CH)
                for h in (0, 1):
                    st = sts[h]
                    if masked:
                        st = jnp.where(key_minus_qry <= q0 - k0, st, NEG)
                    m_old = m_sc[h:h + 1, :]
                    m_new = jnp.maximum(m_old, jnp.max(st, axis=0, keepdims=True))
                    alpha = jnp.exp(m_old - m_new)
                    pt = jnp.exp(st - m_new)
                    acc[h] = alpha * acc[h] + _mm(vt[h, :, pl.ds(k0, CH)], pt)
                    m_sc[h:h + 1, :] = m_new

            last = qi // FWD_CHUNK

            def pipelined(c, sts):
                nxt = scores(qs, c + 1)
                absorb(c, sts, False)
                return nxt

            sts = lax.fori_loop(0, last, pipelined, first_scores)
            qs_next = queries(jnp.minimum(qi + 1, nq - 1))
            nxt = qs_next + scores(qs_next, 0)
            absorb(last, sts, True)
            ot = None
            for h in (0, 1):
                a = acc[h]
                l = a[ONE[h]:ONE[h] + 1, :]
                oh = jnp.where(_data_lanes(sub, h), a * (1.0 / l), 0.0)
                ot = oh if ot is None else ot + oh
                lse_ref[0, h:h + 1, pl.ds(q0, T)] = m_sc[h:h + 1, :] + jnp.log(l)
            o_ref[pl.ds(q0, T), :] = ot.T
            return nxt

        qs0 = queries(0)
        lax.fori_loop(0, nq, q_block, qs0 + scores(qs0, 0))

    wide = pl.BlockSpec((S, W), lambda j: (0, j))
    slab = pl.BlockSpec((S, LANES), lambda j: (0, j))
    rows = pl.BlockSpec((1, 2, S), lambda j: (j, 0, 0))
    in_specs = [wide, wide, slab]
    args = [q, k, v]
    scratch = []
    if not split:
        in_specs.append(pl.BlockSpec((2, S, 1), lambda j: (j, 0, 0)))
        args.append(lcc)
        scratch.append(pltpu.VMEM((2, S, LANES), MXU))
    scratch += [pltpu.VMEM((2, LANES, S), MXU), pltpu.VMEM((2, LANES, T), jnp.float32), pltpu.VMEM((8, T), jnp.float32)]
    return _pcall(
        body, name=name, grid=(npair,), semantics=("arbitrary",),
        in_specs=in_specs, out_specs=[slab, rows],
        out_shape=[_sds((S, npair * LANES), jnp.float32), _sds((npair, 2, S), jnp.float32)],
        scratch_shapes=scratch,
    )(*args)


def _attn_bwd_t(q, k, v, do, o, lse, scale, *, split, name, lcc=None):
    S = q.shape[0]
    npair = v.shape[1] // LANES
    W = 2 * LANES if split else LANES
    T = ATT
    CH = BWD_CHUNK * T
    assert S % CH == 0
    nq = S // T

    def body(*refs):
        if split:
            (q_ref, k_ref, v_ref, do_ref, o_ref, lse_ref, dq_ref, dk_ref, dv_ref, dqt, delta, dk_acc, dv_acc) = refs
        else:
            (q_ref, k_ref, v_ref, do_ref, o_ref, lse_ref, lcc_ref, dq_ref, dk_ref, dv_ref, dlc_ref,
             dqt, delta, dk_acc, dv_acc, qaug, csum) = refs
        lane = lax.broadcasted_iota(jnp.int32, (1, LANES), 1)
        sub = lax.broadcasted_iota(jnp.int32, (LANES, 1), 0)
        key_minus_qry = lax.broadcasted_iota(jnp.int32, (T, CH), 0) - lax.broadcasted_iota(jnp.int32, (T, CH), 1)

        def prep(i, c):
            r0 = pl.multiple_of(i * T, T)
            prod_t = (do_ref[pl.ds(r0, T), :].astype(jnp.float32) * o_ref[pl.ds(r0, T), :]).T
            for h in (0, 1):
                delta[h:h + 1, pl.ds(r0, T)] = jnp.sum(jnp.where(_data_lanes(sub, h), prod_t, 0.0), axis=0, keepdims=True)
                dqt[h, :, pl.ds(r0, T)] = jnp.zeros((LANES, T), jnp.float32)
                if not split:
                    qaug[h, pl.ds(r0, T), :] = _q_aug(q_ref[pl.ds(r0, T), :], lcc_ref[h, pl.ds(r0, T), :], h, scale, lane)
            return c

        lax.fori_loop(0, nq, prep, 0)

        def keys(ki):
            k0 = pl.multiple_of(ki * T, T)
            kblk = k_ref[pl.ds(k0, T), :]
            if split:
                return (kblk[:, :LANES], kblk[:, LANES:])
            return tuple(_k_aug(kblk, lcc_ref[h, pl.ds(k0, T), :], h, lane) for h in (0, 1))

        def q_of(c, h):
            q0 = pl.multiple_of(c * CH, CH)
            if split:
                return q_ref[pl.ds(q0, CH), LANES * h:LANES * (h + 1)]
            return qaug[h, pl.ds(q0, CH), :]

        def scores(khs, c):
            out = []
            for h in (0, 1):
                st = _mm_nt(khs[h], q_of(c, h))
                out.append(st * scale if split else st)
            return tuple(out)

        def k_block(ki, carry):
            khs, first_scores = carry[:2], carry[2:]
            k0 = pl.multiple_of(ki * T, T)
            khts = [kh.astype(jnp.float32).T.astype(kh.dtype) for kh in khs]
            vhs = _split_heads(v_ref[pl.ds(k0, T), :], lane < HALF)
            dk_acc[...] = jnp.zeros_like(dk_acc)
            dv_acc[...] = jnp.zeros_like(dv_acc)

            def absorb(c, vals):
                q0 = pl.multiple_of(c * CH, CH)
                dos = _split_heads(do_ref[pl.ds(q0, CH), :], lane < HALF)
                visible = key_minus_qry <= q0 - k0
                for h in (0, 1):
                    dpt = _mm_nt(vhs[h], dos[h])
                    st = jnp.where(visible, vals[h], NEG)
                    pt = jnp.exp(st - lse_ref[0, h:h + 1, pl.ds(q0, CH)])
                    dv_acc[...] += _mm(pt, dos[h])
                    dst = pt * (dpt - delta[h:h + 1, pl.ds(q0, CH)])
                    dk_acc[h] += _mm(dst, q_of(c, h))
                    dqt[h, :, pl.ds(q0, CH)] += _mm(khts[h], dst)

            first = ki // BWD_CHUNK

            def pipelined(c, vals):
                nxt = scores(khs, c + 1)
                absorb(c, vals)
                return nxt

            vals = lax.fori_loop(first, S // CH - 1, pipelined, first_scores)
            kn = jnp.minimum(ki + 1, nq - 1)
            khs_next = keys(kn)
            nxt = khs_next + scores(khs_next, kn // BWD_CHUNK)
            absorb(S // CH - 1, vals)
            if split:
                dk_ref[pl.ds(k0, T), :LANES] = (dk_acc[0] * scale).astype(dk_ref.dtype)
                dk_ref[pl.ds(k0, T), LANES:] = (dk_acc[1] * scale).astype(dk_ref.dtype)
            else:
                dk_ref[pl.ds(k0, T), :] = jnp.where(lane < HALF, dk_acc[0], dk_acc[1]).astype(dk_ref.dtype)
                for h in (0, 1):
                    csum[h:h + 1, pl.ds(k0, T)] = dk_acc[h].T[AUG[h] + 3:AUG[h] + 4, :]
            dv_ref[pl.ds(k0, T), :] = dv_acc[...].astype(dv_ref.dtype)
            return nxt

        khs0 = keys(0)
        lax.fori_loop(0, nq, k_block, khs0 + scores(khs0, 0))

        def finish(i, c):
            r0 = pl.multiple_of(i * T, T)
            if split:
                for h in (0, 1):
                    dq_ref[pl.ds(r0, T), LANES * h:LANES * (h + 1)] = (dqt[h, :, pl.ds(r0, T)].T * scale).astype(dq_ref.dtype)
            else:
                d = jnp.where(sub < HALF, dqt[0, :, pl.ds(r0, T)], dqt[1, :, pl.ds(r0, T)])
                dq_ref[pl.ds(r0, T), :] = (d.T * scale).astype(dq_ref.dtype)
                for h in (0, 1):
                    dlc_ref[0, h:h + 1, pl.ds(r0, T)] = dqt[h, AUG[h]:AUG[h] + 1, pl.ds(r0, T)] - csum[h:h + 1, pl.ds(r0, T)]
            return c

        lax.fori_loop(0, nq, finish, 0)

    wide = pl.BlockSpec((S, W), lambda j: (0, j))
    slab = pl.BlockSpec((S, LANES), lambda j: (0, j))
    rows = pl.BlockSpec((1, 2, S), lambda j: (j, 0, 0))
    in_specs = [wide, wide, slab, slab, slab, rows]
    args = [q, k, v, do, o, lse]
    out_specs = [wide, wide, slab]
    out_shape = [_sds(q.shape, jnp.float32 if split else do.dtype), _sds(k.shape, jnp.float32 if split else do.dtype),
                 _sds(v.shape, do.dtype)]
    scratch = [pltpu.VMEM((2, LANES, S), jnp.float32), pltpu.VMEM((8, S), jnp.float32),
               pltpu.VMEM((2, T, LANES), jnp.float32), pltpu.VMEM((T, LANES), jnp.float32)]
    if not split:
        in_specs.append(pl.BlockSpec((2, S, 1), lambda j: (j, 0, 0)))
        args.append(lcc)
        out_specs.append(rows)
        out_shape.append(_sds((npair, 2, S), jnp.float32))
        scratch += [pltpu.VMEM((2, S, LANES), MXU), pltpu.VMEM((8, S), jnp.float32)]
    return _pcall(
        body, name=name, grid=(npair,), semantics=("arbitrary",),
        in_specs=in_specs, out_specs=out_specs, out_shape=out_shape, scratch_shapes=scratch,
    )(*args)


def _swa_scores(qh, kblk, slope, shift):
    s = _mm_nt(qh, kblk) * (HEAD ** -0.5)
    a = lax.broadcasted_iota(jnp.int32, (WINDOW, 2 * WINDOW), 0)
    c = lax.broadcasted_iota(jnp.int32, (WINDOW, 2 * WINDOW), 1)
    dist = a - c + shift
    s = s - slope * dist.astype(jnp.float32)
    return jnp.where((dist >= 0) & (dist < WINDOW), s, NEG)


def _swa_fwd(q, kd, vd, sinks, slopes):
    S = q.shape[0]
    npair = q.shape[1] // LANES
    nb = S // WINDOW

    def body(sink_ref, slope_ref, q_ref, k_ref, v_ref, o_ref, lse_ref):
        j = pl.program_id(0)
        lo = _lane_masks()

        def q_block(qi, c):
            q0 = pl.multiple_of(qi * WINDOW, WINDOW)
            k0 = pl.multiple_of(jnp.maximum(qi - 1, 0) * WINDOW, WINDOW)
            shift = q0 - k0
            qs = _split_heads(q_ref[pl.ds(q0, WINDOW), :], lo)
            kblk = k_ref[pl.ds(k0, 2 * WINDOW), :]
            vs = _split_heads(v_ref[pl.ds(k0, 2 * WINDOW), :], lo)
            o = None
            for h in (0, 1):
                sink = sink_ref[2 * j + h]
                s = _swa_scores(qs[h], kblk, slope_ref[2 * j + h], shift)
                m = jnp.maximum(jnp.max(s, axis=1, keepdims=True), sink)
                p = jnp.exp(s - m)
                den = jnp.sum(p, axis=1, keepdims=True) + jnp.exp(sink - m)
                oh = _mm(p / den, vs[h])
                o = oh if o is None else o + oh
                lse_ref[h, pl.ds(q0, WINDOW), :] = m + jnp.log(den)
            o_ref[pl.ds(q0, WINDOW), :] = o
            return c

        def q_group(gi, c):
            for g in range(SWA_GROUP):
                q_block(gi * SWA_GROUP + g, c)
            return c

        lax.fori_loop(0, nb // SWA_GROUP, q_group, 0)

    smem = pl.BlockSpec(memory_space=pltpu.SMEM)
    slab = pl.BlockSpec((S, LANES), lambda j: (0, j))
    return _pcall(
        body, name="swa_fwd", grid=(npair,), semantics=("arbitrary",),
        in_specs=[smem, smem, slab, slab, slab],
        out_specs=[slab, pl.BlockSpec((2, S, 1), lambda j: (j, 0, 0))],
        out_shape=[_sds((S, npair * LANES), jnp.float32), _sds((2 * npair, S, 1), jnp.float32)],
    )(sinks, slopes, q, kd, vd)


def _swa_bwd(q, kd, vd, do, o, lse, sinks, slopes):
    S = q.shape[0]
    npair = q.shape[1] // LANES
    nb = S // WINDOW

    def body(sink_ref, slope_ref, q_ref, k_ref, v_ref, do_ref, o_ref, lse_ref,
             dq_ref, dk_ref, dv_ref, dsink_ref, dk_acc, dv_acc):
        j = pl.program_id(0)
        lo = _lane_masks()
        dk_acc[...] = jnp.zeros_like(dk_acc)
        dv_acc[...] = jnp.zeros_like(dv_acc)

        def q_block(qi, carry):
            q0 = pl.multiple_of(qi * WINDOW, WINDOW)
            k0 = pl.multiple_of(jnp.maximum(qi - 1, 0) * WINDOW, WINDOW)
            shift = q0 - k0
            qs = _split_heads(q_ref[pl.ds(q0, WINDOW), :], lo)
            dos = _split_heads(do_ref[pl.ds(q0, WINDOW), :], lo)
            oblk = o_ref[pl.ds(q0, WINDOW), :]
            kblk = k_ref[pl.ds(k0, 2 * WINDOW), :]
            vblk = v_ref[pl.ds(k0, 2 * WINDOW), :]
            ks = _split_heads(kblk, lo)
            dq = None
            out = []
            for h in (0, 1):
                sink = sink_ref[2 * j + h]
                lse_h = lse_ref[h, pl.ds(q0, WINDOW), :]
                s = _swa_scores(qs[h], kblk, slope_ref[2 * j + h], shift)
                p = jnp.exp(s - lse_h)
                delta = jnp.sum(dos[h].astype(jnp.float32) * oblk, axis=1, keepdims=True)
                dv_acc[pl.ds(k0, 2 * WINDOW), :] += _mm_tn(p, dos[h])
                dp = _mm_nt(dos[h], vblk)
                ds = p * (dp - delta)
                dqh = _mm(ds, ks[h]) * (HEAD ** -0.5)
                dq = dqh if dq is None else dq + dqh
                dk_acc[pl.ds(k0, 2 * WINDOW), :] += _mm_tn(ds, qs[h]) * (HEAD ** -0.5)
                dsk = jnp.sum(-jnp.exp(sink - lse_h) * delta, axis=0, keepdims=True)
                out.append(carry[h] + dsk)
            dq_ref[pl.ds(q0, WINDOW), :] = dq.astype(dq_ref.dtype)
            return tuple(out)

        def q_group(gi, carry):
            for g in range(SWA_GROUP):
                carry = q_block(gi * SWA_GROUP + g, carry)
            return carry

        zero = jnp.zeros((1, 1), jnp.float32)
        dsa, dsb = lax.fori_loop(0, nb // SWA_GROUP, q_group, (zero, zero))
        dk_ref[...] = dk_acc[...].astype(dk_ref.dtype)
        dv_ref[...] = dv_acc[...].astype(dv_ref.dtype)
        r = lax.broadcasted_iota(jnp.int32, (8, LANES), 0)
        dsink_ref[0] = jnp.where(r == 0, dsa, jnp.where(r == 1, dsb, 0.0))

    smem = pl.BlockSpec(memory_space=pltpu.SMEM)
    slab = pl.BlockSpec((S, LANES), lambda j: (0, j))
    return _pcall(
        body, name="swa_bwd", grid=(npair,), semantics=("arbitrary",),
        in_specs=[smem, smem, slab, slab, slab, slab, slab, pl.BlockSpec((2, S, 1), lambda j: (j, 0, 0))],
        out_specs=[slab, slab, slab, pl.BlockSpec((1, 8, LANES), lambda j: (j, 0, 0))],
        out_shape=[_sds(q.shape, do.dtype), _sds(kd.shape, do.dtype), _sds(vd.shape, do.dtype),
                   _sds((npair, 8, LANES), jnp.float32)],
        scratch_shapes=[pltpu.VMEM((S, LANES), jnp.float32), pltpu.VMEM((S, LANES), jnp.float32)],
    )(sinks, slopes, q, kd, vd, do, o, lse)


def _log_steps(S):
    k, out = 1, []
    while k < S:
        out.append(k)
        k *= 2
    return out


def _forget_fwd(f_row, b_col):
    S = f_row.shape[1]

    def body(f_ref, b_ref, lc_ref):
        x = f_ref[...] + b_ref[...]
        lc = jnp.minimum(x, 0.0) - jnp.log(1.0 + jnp.exp(-jnp.abs(x)))
        idx = lax.broadcasted_iota(jnp.int32, lc.shape, 1)
        for k in _log_steps(S):
            lc = lc + jnp.where(idx >= k, pltpu.roll(lc, k, axis=1), 0.0)
        lc_ref[...] = lc

    return _pcall(body, name="forget_fwd", out_shape=_sds(f_row.shape, jnp.float32))(f_row, b_col)


def _forget_bwd(dlc_row, f_row, b_col):
    S = f_row.shape[1]

    def body(d_ref, f_ref, b_ref, df_ref, db_ref):
        g = d_ref[...]
        idx = lax.broadcasted_iota(jnp.int32, g.shape, 1)
        for k in _log_steps(S):
            g = g + jnp.where(idx < S - k, pltpu.roll(g, S - k, axis=1), 0.0)
        x = f_ref[...] + b_ref[...]
        df = g * _sigmoid(-x)
        df_ref[...] = df
        db_ref[...] = jnp.sum(df, axis=1, keepdims=True)

    return _pcall(body, name="forget_bwd",
                  out_shape=[_sds(f_row.shape, jnp.float32), _sds((f_row.shape[0], 1), jnp.float32)])(dlc_row, f_row, b_col)


def _layer0_out_layer1_in(x, o_m, o_s, gate, w_out, g1, w_in1):
    S = x.shape[0]

    def body(x_ref, om_ref, os_ref, gate_ref, wo_ref, g_ref, w_ref,
             x1_ref, u_ref, h_ref, q_ref, k_ref, v_ref, g1_ref, f_ref):
        gt = gate_ref[...]
        sg = gt * _sigmoid(gt)
        um = om_ref[...] * sg[:, :512]
        us = os_ref[...] * sg[:, 512:]
        u_ref[:, :512] = um.astype(u_ref.dtype)
        u_ref[:, 512:] = us.astype(u_ref.dtype)
        x1 = x_ref[...] + _mm(um, wo_ref[0:512, :]) + _mm(us, wo_ref[512:1024, :])
        x1_ref[...] = x1
        h = _rms(x1, g_ref[...])
        h_ref[...] = h.astype(h_ref.dtype)
        z = _mm_nt(h, w_ref[...])
        q_ref[...] = z[:, 0:1024].astype(q_ref.dtype)
        k_ref[...] = z[:, 1024:2048].astype(k_ref.dtype)
        v_ref[...] = z[:, 2048:3072].astype(v_ref.dtype)
        g1_ref[...] = z[:, 3072:4096]
        f_ref[...] = z[:, 4096:4224]

    outs = [((S, D), jnp.float32), ((S, D), MXU), ((S, D), MXU), ((S, D), MXU), ((S, D), MXU), ((S, D), MXU),
            ((S, D), jnp.float32), ((S, LANES), jnp.float32)]
    return _pcall(
        body, name="layer0_out_layer1_in", grid=(S // TOK,), semantics=("arbitrary",),
        in_specs=[_rows(TOK, D), _rows(TOK, 512), _rows(TOK, 512), _rows(TOK, D), _full((D, D)), _full((1, D)),
                  _full(w_in1.shape)],
        out_specs=[_rows(TOK, s[1]) for s, _ in outs],
        out_shape=[_sds(s, d) for s, d in outs],
    )(x, o_m, o_s, gate, w_out, g1, w_in1)


def _head(x1, o1, gate1, w_out1, g_f, target):
    S = x1.shape[0]

    def body(x1_ref, o_ref, gate_ref, wo_ref, g_ref, t_ref,
             loss_ref, dgf_ref, dx2_ref, u_ref, do_ref, dgate_ref):
        i = pl.program_id(0)
        gt = gate_ref[...]
        sig = _sigmoid(gt)
        sg = gt * sig
        o = o_ref[...]
        u = o * sg
        u_ref[...] = u.astype(u_ref.dtype)
        x2 = x1_ref[...] + _mm(u, wo_ref[...])
        g = g_ref[...]
        y = _rms(x2, g)
        err = y - t_ref[...]
        part = 0.5 * jnp.sum(jnp.mean(err * err, axis=-1, keepdims=True), axis=0, keepdims=True)
        dy = err * (1.0 / D)
        dx2, dg_rows = _rms_bwd(x2, g, dy)
        dx2_ref[...] = dx2
        du = _mm_nt(dx2, wo_ref[...])
        do_ref[...] = (du * sg).astype(do_ref.dtype)
        dgate_ref[...] = (du * o * (sig * (1.0 + gt * (1.0 - sig)))).astype(dgate_ref.dtype)

        @pl.when(i == 0)
        def _():
            loss_ref[...] = jnp.zeros_like(loss_ref)
            dgf_ref[...] = jnp.zeros_like(dgf_ref)

        loss_ref[...] += jnp.broadcast_to(part, loss_ref.shape)
        dgf_ref[...] += jnp.sum(dg_rows, axis=0, keepdims=True)

    outs = [((S, D), jnp.float32), ((S, D), MXU), ((S, D), MXU), ((S, D), MXU)]
    return _pcall(
        body, name="head", grid=(S // TOK,), semantics=("arbitrary",),
        in_specs=[_rows(TOK, D), _rows(TOK, D), _rows(TOK, D), _full((D, D)), _full((1, D)), _rows(TOK, D)],
        out_specs=[_full((8, LANES)), _full((1, D))] + [_rows(TOK, D) for _ in outs],
        out_shape=[_sds((8, LANES), jnp.float32), _sds((1, D), jnp.float32)] + [_sds(s, d) for s, d in outs],
    )(x1, o1, gate1, w_out1, g_f, target)


def _layer1_in_bwd(dq, dk, dv, dgate1, df, x1, dx2, g1, w_in1, gate0, o_m, o_s, w_out0):
    S = x1.shape[0]

    def body(dq_ref, dk_ref, dv_ref, dg1_ref, df_ref, x1_ref, dx2_ref, g_ref, w_ref, gate_ref, om_ref, os_ref,
             wo_ref, dz_ref, dx1_ref, dgn_ref, dom_ref, dos_ref, dgate_ref):
        i = pl.program_id(0)
        dz_ref[:, 0:1024] = dq_ref[...]
        dz_ref[:, 1024:2048] = dk_ref[...]
        dz_ref[:, 2048:3072] = dv_ref[...]
        dz_ref[:, 3072:4096] = dg1_ref[...]
        dz_ref[:, 4096:4224] = df_ref[...]
        dh = _mm(dz_ref[...], w_ref[...])
        g = g_ref[...]
        dxn, dg_rows = _rms_bwd(x1_ref[...], g, dh)
        dx1 = dx2_ref[...] + dxn
        dx1_ref[...] = dx1
        du = _mm_nt(dx1, wo_ref[...])
        gt = gate_ref[...]
        sig = _sigmoid(gt)
        sg = gt * sig
        dsg = sig * (1.0 + gt * (1.0 - sig))
        dom_ref[...] = (du[:, :512] * sg[:, :512]).astype(dom_ref.dtype)
        dos_ref[...] = (du[:, 512:] * sg[:, 512:]).astype(dos_ref.dtype)
        dgate_ref[:, :512] = (du[:, :512] * om_ref[...] * dsg[:, :512]).astype(dgate_ref.dtype)
        dgate_ref[:, 512:] = (du[:, 512:] * os_ref[...] * dsg[:, 512:]).astype(dgate_ref.dtype)

        @pl.when(i == 0)
        def _():
            dgn_ref[...] = jnp.zeros_like(dgn_ref)

        dgn_ref[...] += jnp.sum(dg_rows, axis=0, keepdims=True)

    return _pcall(
        body, name="layer1_in_bwd", grid=(S // TOK,), semantics=("arbitrary",),
        in_specs=[_rows(TOK, D), _rows(TOK, D), _rows(TOK, D), _rows(TOK, D), _rows(TOK, LANES), _rows(TOK, D),
                  _rows(TOK, D), _full((1, D)), _full(w_in1.shape), _rows(TOK, D), _rows(TOK, 512), _rows(TOK, 512),
                  _full((D, D))],
        out_specs=[_rows(TOK, 4224), _rows(TOK, D), _full((1, D)), _rows(TOK, 512), _rows(TOK, 512), _rows(TOK, D)],
        out_shape=[_sds((S, 4224), MXU), _sds((S, D), jnp.float32), _sds((1, D), jnp.float32),
                   _sds((S, 512), MXU), _sds((S, 512), MXU), _sds((S, D), MXU)],
    )(dq, dk, dv, dgate1, df, x1, dx2, g1, w_in1, gate0, o_m, o_s, w_out0)


def _layer0_in_bwd(dqm, dkm, dvm, dqs, dkd, dvd, dgate0, cos, sin, cq, ckv, x, dx1, g_in, w_in, g_q, w_q, g_kv, w_kv):
    S = x.shape[0]
    consts = _rope_consts()

    def body(dqm_ref, dkm_ref, dvm_ref, dqs_ref, dkd_ref, dvd_ref, dgate_ref, cos_ref, sin_ref, c_ref, cq_ref, ckv_ref,
             x_ref, dx1_ref, g_ref, w_ref, gq_ref, wq_ref, gkv_ref, wkv_ref,
             dx_ref, dz_ref, dqu_ref, dkvu_ref, dgin_ref, dgq_ref, dgkv_ref):
        i = pl.program_id(0)
        lo = _lane_masks()
        sign = c_ref[...][1:2, :]
        c = cos_ref[...]
        s = sin_ref[...]
        dkpe = None
        for hd in range(N_MLA):
            sl = slice(LANES * hd, LANES * (hd + 1))
            dqu_ref[:, sl] = _rope_t(dqm_ref[:, sl], c, s, sign).astype(dqu_ref.dtype)
            dkh = dkm_ref[:, sl]
            dkvu_ref[:, sl] = jnp.where(lo, dkh, 0.0).astype(dkvu_ref.dtype)
            dkpe = dkh if dkpe is None else dkpe + dkh
        dkvu_ref[:, 1024:1536] = dvm_ref[...]
        dkpe = _rope_t(jnp.where(lo, 0.0, dkpe), c, s, sign)
        dcqn = _mm(dqu_ref[...], wq_ref[...])
        dckvn = _mm_nt(dkvu_ref[...], wkv_ref[...])
        gq = gq_ref[...]
        gkv = gkv_ref[...]
        dcq, dgq_rows = _rms_bwd(cq_ref[...], gq, dcqn)
        dckv, dgkv_rows = _rms_bwd(ckv_ref[...], gkv, dckvn)
        dz_ref[:, 0:256] = dcq.astype(dz_ref.dtype)
        dz_ref[:, 256:384] = dckv.astype(dz_ref.dtype)
        dz_ref[:, 384:512] = dkpe.astype(dz_ref.dtype)
        dz_ref[:, 512:1024] = dqs_ref[...]
        dz_ref[:, 1024:1536] = dkd_ref[...]
        dz_ref[:, 1536:2048] = dvd_ref[...]
        dz_ref[:, 2048:3072] = dgate_ref[...]
        dh = _mm(dz_ref[...], w_ref[...])
        g = g_ref[...]
        dxn, dg_rows = _rms_bwd(x_ref[...], g, dh)
        dx_ref[...] = dx1_ref[...] + dxn

        @pl.when(i == 0)
        def _():
            dgin_ref[...] = jnp.zeros_like(dgin_ref)
            dgq_ref[...] = jnp.zeros_like(dgq_ref)
            dgkv_ref[...] = jnp.zeros_like(dgkv_ref)

        dgin_ref[...] += jnp.sum(dg_rows, axis=0, keepdims=True)
        dgq_ref[...] += jnp.sum(dgq_rows, axis=0, keepdims=True)
        dgkv_ref[...] += jnp.sum(dgkv_rows, axis=0, keepdims=True)

    return _pcall(
        body, name="layer0_in_bwd", grid=(S // TOK,), semantics=("arbitrary",),
        in_specs=[_rows(TOK, 1024), _rows(TOK, 1024), _rows(TOK, 512), _rows(TOK, 512), _rows(TOK, 512), _rows(TOK, 512),
                  _rows(TOK, D), _rows(TOK, LANES), _rows(TOK, LANES), _full((8, LANES)), _rows(TOK, 256), _rows(TOK, 128),
                  _rows(TOK, D), _rows(TOK, D), _full((1, D)), _full(w_in.shape), _full((1, 256)), _full(w_q.shape),
                  _full((1, 128)), _full(w_kv.shape)],
        out_specs=[_rows(TOK, D), _rows(TOK, 3072), _rows(TOK, 1024), _rows(TOK, 1536), _full((1, D)), _full((1, 256)),
                   _full((1, 128))],
        out_shape=[_sds((S, D), jnp.float32), _sds((S, 3072), MXU), _sds((S, 1024), MXU), _sds((S, 1536), MXU),
                   _sds((1, D), jnp.float32), _sds((1, 256), jnp.float32), _sds((1, 128), jnp.float32)],
    )(dqm, dkm, dvm, dqs, dkd, dvd, dgate0, cos, sin, consts, cq, ckv, x, dx1, g_in, w_in, g_q, w_q, g_kv, w_kv)


def _wgrad(a, b, name):
    S, M = a.shape
    N = b.shape[1]
    tm = next(t for t in range(WG_ROWS, 0, -LANES) if M % t == 0)
    tn = N if N <= 1024 else 512
    tk = min(WG_TOK, S)

    def body(a_ref, b_ref, o_ref):
        @pl.when(pl.program_id(2) == 0)
        def _():
            o_ref[...] = jnp.zeros_like(o_ref)

        o_ref[...] += _mm_tn(a_ref[...], b_ref[...])

    return _pcall(
        body, name=name, grid=(M // tm, N // tn, S // tk), semantics=("parallel", "parallel", "arbitrary"),
        in_specs=[pl.BlockSpec((tk, tm), lambda m, n, k: (k, m)), pl.BlockSpec((tk, tn), lambda m, n, k: (k, n))],
        out_specs=pl.BlockSpec((tm, tn), lambda m, n, k: (m, n)),
        out_shape=_sds((M, N), jnp.float32),
    )(a, b)


def _adamw(w, g, m, v, name):
    shape = w.shape
    R, C = (int(np.prod(shape[:-1])), shape[-1])
    w2, g2, m2, v2 = (t.reshape(R, C) for t in (w, g, m, v))
    tr = 256 if R % 256 == 0 else R
    tc = 256 if (tr == R and R > 256 and C % 256 == 0) else C

    def body(w_ref, g_ref, m_ref, v_ref, d_ref, nm_ref, nv_ref):
        gg = g_ref[...]
        nm = B1 * m_ref[...] + (1.0 - B1) * gg
        nv = B2 * v_ref[...] + (1.0 - B2) * (gg * gg)
        m_hat = nm / (1.0 - B1 ** STEP)
        v_hat = nv / (1.0 - B2 ** STEP)
        d_ref[...] = -LR * (m_hat / (jnp.sqrt(v_hat) + AEPS) + WD * w_ref[...])
        nm_ref[...] = nm
        nv_ref[...] = nv

    spec = pl.BlockSpec((tr, tc), lambda i, j: (i, j))
    d, nm, nv = _pcall(
        body, name=name, grid=(R // tr, C // tc), semantics=("parallel", "parallel"),
        in_specs=[spec] * 4, out_specs=[spec] * 3, out_shape=[_sds((R, C), jnp.float32)] * 3,
    )(w2, g2, m2, v2)
    return d.reshape(shape), nm.reshape(shape), nv.reshape(shape)


def _sum_leading(a, name):
    n, R, C = a.shape
    tr = SUM_ROWS if R % SUM_ROWS == 0 else R

    def body(a_ref, o_ref):
        acc = a_ref[0]
        for i in range(1, n):
            acc = acc + a_ref[i]
        o_ref[...] = acc

    return _pcall(
        body, name=name, grid=(R // tr,), semantics=("parallel",),
        in_specs=[pl.BlockSpec((n, tr, C), lambda i: (0, i, 0))], out_specs=_rows(tr, C),
        out_shape=_sds((R, C), a.dtype),
    )(a)


def _add_blocks(a, b, name, out_dtype):
    n, R, C = a.shape
    tr = SUM_ROWS if R % SUM_ROWS == 0 else R

    def body(a_ref, b_ref, o_ref):
        o_ref[...] = (a_ref[...] + b_ref[...]).astype(o_ref.dtype)

    spec = pl.BlockSpec((1, tr, C), lambda k, i: (k, i, 0))
    return _pcall(
        body, name=name, grid=(n, R // tr), semantics=("parallel", "parallel"),
        in_specs=[spec, spec], out_specs=spec, out_shape=_sds(a.shape, out_dtype),
    )(a, b)


def _total_sum(mine, theirs, recv, name):
    R, C = mine.shape
    n = recv.shape[0]
    tr = SUM_ROWS if R % SUM_ROWS == 0 else R

    def body(a_ref, b_ref, r_ref, o_ref):
        acc = a_ref[...] + b_ref[...]
        for i in range(n):
            acc = acc + r_ref[i].astype(jnp.float32)
        o_ref[...] = acc

    return _pcall(
        body, name=name, grid=(R // tr,), semantics=("parallel",),
        in_specs=[_rows(tr, C), _rows(tr, C), pl.BlockSpec((n, tr, C), lambda i: (0, i, 0))], out_specs=_rows(tr, C),
        out_shape=_sds((R, C), jnp.float32),
    )(mine, theirs, recv)


def _place():
    return lax.axis_index("x"), lax.axis_index("y"), lax.axis_index("c")


def _all_gather8(block, name):
    R, C = block.shape

    def body(x_ref, out_ref, send_sems, recv_sems, local_sem):
        x, y, c = _place()
        me, sibling = (x, y, c), (x, y, 1 - c)
        chips = [(1 - x, y), (x, 1 - y), (1 - x, 1 - y)]

        def slot(px, py, pc):
            return out_ref.at[4 * px + 2 * py + pc]

        def copy(k, blk, to, src=None):
            return pltpu.make_async_remote_copy(
                src_ref=slot(*blk) if src is None else src, dst_ref=slot(*blk),
                send_sem=send_sems.at[k], recv_sem=recv_sems.at[k], device_id=to, device_id_type=MESH_ID)

        mine = pltpu.make_async_copy(x_ref, slot(*me), local_sem)
        mine.start()
        first = [copy(0, me, sibling, src=x_ref)]
        first += [copy(1 + j, me, (*chip, c), src=x_ref) for j, chip in enumerate(chips)]
        for cp in first:
            cp.start()
        passed = [copy(4 + j, (*chip, c), sibling) for j, chip in enumerate(chips)]
        for j, chip in enumerate(chips):
            copy(1 + j, (*chip, c), me).wait_recv()
            passed[j].start()
        copy(0, sibling, me).wait_recv()
        for j, chip in enumerate(chips):
            copy(4 + j, (*chip, 1 - c), me).wait_recv()
        for cp in first + passed:
            cp.wait_send()
        mine.wait()

    any_spec = pl.BlockSpec(memory_space=pl.ANY)
    return _pcall(
        body, name=name, in_specs=[any_spec], out_specs=any_spec, out_shape=_sds((8, R, C), block.dtype),
        scratch_shapes=[pltpu.SemaphoreType.DMA((7,)), pltpu.SemaphoreType.DMA((7,)), pltpu.SemaphoreType.DMA],
    )(block)


def _pair_swap(g, name):
    n = g.shape[0]

    def body(g_ref, out_ref, send_sems, recv_sems):
        x, y, c = _place()
        cps = [pltpu.make_async_remote_copy(src_ref=g_ref.at[k, 1 - c], dst_ref=out_ref.at[k], send_sem=send_sems.at[k],
                                            recv_sem=recv_sems.at[k], device_id=(x, y, 1 - c), device_id_type=MESH_ID)
               for k in range(n)]
        for cp in cps:
            cp.start()
        for cp in cps:
            cp.wait()

    any_spec = pl.BlockSpec(memory_space=pl.ANY)
    return _pcall(
        body, name=name, in_specs=[any_spec], out_specs=any_spec, out_shape=_sds((n,) + g.shape[2:], g.dtype),
        scratch_shapes=[pltpu.SemaphoreType.DMA((n,)), pltpu.SemaphoreType.DMA((n,))],
    )(g)


def _chip_exchange(p, name):
    def body(p_ref, out_ref, send_sems, recv_sems):
        x, y, c = _place()
        chips = [(1 - x, y), (x, 1 - y), (1 - x, 1 - y)]
        cps = [pltpu.make_async_remote_copy(
            src_ref=p_ref.at[2 * cx + cy], dst_ref=out_ref.at[j], send_sem=send_sems.at[j],
            recv_sem=recv_sems.at[j], device_id=(cx, cy, c), device_id_type=MESH_ID)
            for j, (cx, cy) in enumerate(chips)]
        for cp in cps:
            cp.start()
        for cp in cps:
            cp.wait()

    any_spec = pl.BlockSpec(memory_space=pl.ANY)
    return _pcall(
        body, name=name, in_specs=[any_spec], out_specs=any_spec, out_shape=_sds((3,) + p.shape[1:], p.dtype),
        scratch_shapes=[pltpu.SemaphoreType.DMA((3,)), pltpu.SemaphoreType.DMA((3,))],
    )(p)


def _pair_exchange(t, name):
    def body(t_ref, out_ref, send_sem, recv_sem):
        x, y, c = _place()
        cp = pltpu.make_async_remote_copy(src_ref=t_ref, dst_ref=out_ref, send_sem=send_sem, recv_sem=recv_sem,
                                          device_id=(x, y, 1 - c), device_id_type=MESH_ID)
        cp.start()
        cp.wait()

    any_spec = pl.BlockSpec(memory_space=pl.ANY)
    return _pcall(
        body, name=name, in_specs=[any_spec], out_specs=any_spec, out_shape=_sds(t.shape, t.dtype),
        scratch_shapes=[pltpu.SemaphoreType.DMA, pltpu.SemaphoreType.DMA],
    )(t)


def _prep_w_in0(wt):
    z32 = jnp.zeros((32, wt.shape[1]), wt.dtype)
    z64 = jnp.zeros((64, wt.shape[1]), wt.dtype)
    k0, k1 = wt[928:992], wt[992:1056]
    v0, v1 = wt[1056:1120], wt[1120:1184]
    return jnp.concatenate([wt[0:384], z64, wt[384:416], z32, wt[416:928],
                            k0, k0, k0, k0, k1, k1, k1, k1, v0, v0, v0, v0, v1, v1, v1, v1, wt[1184:2208]], axis=0)


def _fold_w_in0(d):
    def fold(blk):
        b = blk.reshape(8, 64, blk.shape[1])
        return jnp.concatenate([b[0] + b[1] + b[2] + b[3], b[4] + b[5] + b[6] + b[7]], axis=0)
    return jnp.concatenate([d[0:384], d[448:480], d[512:1024], fold(d[1024:1536]), fold(d[1536:2048]), d[2048:3072]], axis=0)


def _prep_w_q(wt):
    return jnp.pad(wt.reshape(N_MLA, 96, Q_RANK), ((0, 0), (0, 32), (0, 0))).reshape(1024, Q_RANK)


def _fold_w_q(d):
    return d.reshape(N_MLA, 128, Q_RANK)[:, :96].reshape(768, Q_RANK)


def _prep_w_kv(w):
    w3 = w.reshape(KV_RANK, N_MLA, 128)
    kk = jnp.pad(w3[:, :, :64], ((0, 0), (0, 0), (0, 64))).reshape(KV_RANK, 1024)
    return jnp.concatenate([kk, w3[:, :, 64:].reshape(KV_RANK, 512)], axis=1)


def _fold_w_kv(d):
    kk = d[:, :1024].reshape(KV_RANK, N_MLA, 128)[:, :, :64]
    vv = d[:, 1024:].reshape(KV_RANK, N_MLA, 64)
    return jnp.concatenate([kk, vv], axis=2).reshape(KV_RANK, 1024)


def _prep_w_in1(wt):
    return jnp.concatenate([wt[0:3072], wt[3088:4112], wt[3072:3088], jnp.zeros((112, wt.shape[1]), wt.dtype)], axis=0)


def _fold_w_in1(d):
    return jnp.concatenate([d[0:3072], d[4096:4112], d[3072:4096]], axis=0)


def _local_step(x, pos, target, e_g_in, w_in0, e_g_q, w_q, e_g_kv, w_kv, sinks, w_out0, o_g_in, w_in1, b_f, w_out1, g_final):
    S = x.shape[0]
    w_in0p, w_qp, w_kvp, w_in1p = _prep_w_in0(w_in0), _prep_w_q(w_q), _prep_w_kv(w_kv), _prep_w_in1(w_in1)
    slopes = jnp.asarray(2.0 ** (-8.0 * (np.arange(N_SWA, dtype=np.float32) + 1.0) / N_SWA), jnp.float32)
    sinks1 = sinks.reshape(N_SWA)
    b_col = b_f.reshape(N_FOX, 1)

    (h0, cq, ckv, cqn, ckvn, qm, km, vm, qs, kd, vd, gate0, cos, sin) = _layer0_in(
        x, pos, e_g_in, w_in0p, e_g_q, w_qp, e_g_kv, w_kvp)
    o_m, lse_m = _attn_fwd_t(qm, km, vm, (NOPE + ROPE) ** -0.5, split=True, name="mla_fwd")
    o_s, lse_s = _swa_fwd(qs, kd, vd, sinks1, slopes)
    x1, u0, h1, q1, k1, v1, gate1, f_slab = _layer0_out_layer1_in(x, o_m, o_s, gate0, w_out0, o_g_in, w_in1p)
    f_row = f_slab[:, :N_FOX].T
    lc_row = _forget_fwd(f_row, b_col)
    lcc = lc_row.reshape(N_FOX, S, 1)
    o1, lse1 = _attn_fwd_t(q1, k1, v1, HEAD ** -0.5, split=False, name="fox_fwd", lcc=lcc)
    loss8, dg_final, dx2, u1, do1, dgate1 = _head(x1, o1, gate1, w_out1, g_final, target)

    dq1, dk1, dv1, dlc = _attn_bwd_t(q1, k1, v1, do1, o1, lse1, HEAD ** -0.5, split=False, name="fox_bwd", lcc=lcc)
    df_row, db_f = _forget_bwd(dlc.reshape(N_FOX, S), f_row, b_col)
    df_slab = jnp.pad(df_row.T, ((0, 0), (0, LANES - N_FOX))).astype(MXU)
    dz1, dx1, dg_o_in, do_m, do_s, dgate0 = _layer1_in_bwd(
        dq1, dk1, dv1, dgate1, df_slab, x1, dx2, o_g_in, w_in1p, gate0, o_m, o_s, w_out0)
    dqs, dkd, dvd, dsink = _swa_bwd(qs, kd, vd, do_s, o_s, lse_s, sinks1, slopes)
    dqm, dkm, dvm = _attn_bwd_t(qm, km, vm, do_m, o_m, lse_m, (NOPE + ROPE) ** -0.5, split=True, name="mla_bwd")
    dx, dz0, dqu, dkvu, dg_in, dg_q, dg_kv = _layer0_in_bwd(
        dqm, dkm, dvm, dqs, dkd, dvd, dgate0, cos, sin, cq, ckv, x, dx1, e_g_in, w_in0p, e_g_q, w_qp, e_g_kv, w_kvp)

    grads = dict(
        e_g_in=dg_in,
        e_w_in=_fold_w_in0(_wgrad(dz0, h0, "wgrad_in0")),
        e_g_q_a=dg_q,
        e_w_q_up=_fold_w_q(_wgrad(dqu, cqn, "wgrad_q_up")),
        e_g_kv_a=dg_kv,
        e_w_kv_up=_fold_w_kv(_wgrad(ckvn, dkvu, "wgrad_kv_up")),
        e_sinks=dsink[:, 0:2, 0].reshape(1, N_SWA),
        e_w_out=_wgrad(u0, dx1, "wgrad_out0"),
        o_g_in=dg_o_in,
        o_w_in=_fold_w_in1(_wgrad(dz1, h1, "wgrad_in1")),
        o_b_f=db_f.reshape(1, N_FOX),
        o_w_out=_wgrad(u1, dx2, "wgrad_out1"),
        g_final=dg_final,
    )
    return loss8[0, 0], dx, grads


SHARDED = ("e_w_in", "e_w_q_up", "e_w_kv_up", "e_w_out", "o_g_in", "o_w_in", "o_w_out")
TRANSPOSED = ("e_w_in", "e_w_q_up", "o_w_in")
COL_SHARDED = ("e_w_kv_up", "o_g_in")
REPLICATED = ("e_g_in", "e_g_q_a", "e_g_kv_a", "e_sinks", "o_b_f", "g_final")
FULL_SHAPES = dict(e_w_in=(2208, 1024), e_w_q_up=(768, 256), e_w_kv_up=(128, 1024), e_w_out=(1024, 1024),
                   o_g_in=(1, 1024), o_w_in=(4112, 1024), o_w_out=(1024, 1024))
WINDOWS = dict(e_w_in=(0, 0), o_w_in=(560, 0), e_w_out=(1600, 0), o_w_out=(1856, 0), e_w_q_up=(2112, 0),
               e_w_kv_up=(2112, 256), o_g_in=(2112, 512))


def _shard_shape(name):
    r, c = FULL_SHAPES[name]
    return (r, c // 4) if name in COL_SHARDED else (r // 4, c)


def _as_handled(name, a):
    a = a[0] if a.ndim == 3 else a
    return a.T if name in TRANSPOSED else a


def _as_given(name, a, shape):
    return (a.T if name in TRANSPOSED else a).reshape(shape)


def _pack_block(p):
    def rows(a, n):
        return jnp.pad(a, ((0, n - a.shape[0]), (0, 0)))

    g = p["o_g_in"]
    band = jnp.concatenate([p["e_w_q_up"], rows(p["e_w_kv_up"], 192), jnp.pad(g, ((0, 192 - g.shape[0]), (0, 512 - g.shape[1])))], axis=1)
    return jnp.concatenate([rows(p["e_w_in"], 560), rows(p["o_w_in"], 1040), p["e_w_out"], p["o_w_out"], band], axis=0)


def _window(block, name, width=None):
    r0, c0 = WINDOWS[name]
    r, c = _shard_shape(name)
    return block[..., r0:r0 + r, c0:c0 + (c if width is None else width)]


def _chip_slice(name, full, k):
    r, c = _shard_shape(name)
    return full[:, c * k:c * (k + 1)] if name in COL_SHARDED else full[r * k:r * (k + 1), :]


def kernel(x, positions, e_g_in, e_w_in, e_g_q_a, e_w_q_up, e_g_kv_a, e_w_kv_up, e_sinks, e_w_out, o_g_in, o_w_in, o_b_f, o_w_out, g_final, loss_target, m_e_g_in, m_e_w_in, m_e_g_q_a, m_e_w_q_up, m_e_g_kv_a, m_e_w_kv_up, m_e_sinks, m_e_w_out, m_o_g_in, m_o_w_in, m_o_b_f, m_o_w_out, m_g_final, v_e_g_in, v_e_w_in, v_e_g_q_a, v_e_w_q_up, v_e_g_kv_a, v_e_w_kv_up, v_e_sinks, v_e_w_out, v_o_g_in, v_o_w_in, v_o_b_f, v_o_w_out, v_g_final):
    w = dict(e_g_in=e_g_in, e_w_in=e_w_in, e_g_q_a=e_g_q_a, e_w_q_up=e_w_q_up, e_g_kv_a=e_g_kv_a, e_w_kv_up=e_w_kv_up,
             e_sinks=e_sinks, e_w_out=e_w_out, o_g_in=o_g_in, o_w_in=o_w_in, o_b_f=o_b_f, o_w_out=o_w_out, g_final=g_final)
    m = dict(e_g_in=m_e_g_in, e_w_in=m_e_w_in, e_g_q_a=m_e_g_q_a, e_w_q_up=m_e_w_q_up, e_g_kv_a=m_e_g_kv_a,
             e_w_kv_up=m_e_w_kv_up, e_sinks=m_e_sinks, e_w_out=m_e_w_out, o_g_in=m_o_g_in, o_w_in=m_o_w_in, o_b_f=m_o_b_f,
             o_w_out=m_o_w_out, g_final=m_g_final)
    v = dict(e_g_in=v_e_g_in, e_w_in=v_e_w_in, e_g_q_a=v_e_g_q_a, e_w_q_up=v_e_w_q_up, e_g_kv_a=v_e_g_kv_a,
             e_w_kv_up=v_e_w_kv_up, e_sinks=v_e_sinks, e_w_out=v_e_w_out, o_g_in=v_o_g_in, o_w_in=v_o_w_in, o_b_f=v_o_b_f,
             o_w_out=v_o_w_out, g_final=v_g_final)
    order = ("e_g_in", "e_w_in", "e_g_q_a", "e_w_q_up", "e_g_kv_a", "e_w_kv_up", "e_sinks", "e_w_out", "o_g_in", "o_w_in",
             "o_b_f", "o_w_out", "g_final")
    c = lax.axis_index("c")
    chip = 2 * lax.axis_index("x") + lax.axis_index("y")

    parts = {}
    for n in SHARDED:
        a = _as_handled(n, w[n])
        if n == "o_g_in":
            parts[n] = lax.bitcast_convert_type(a, jnp.bfloat16).reshape(1, -1)
        else:
            parts[n] = a.astype(jnp.bfloat16)
    mine_w = _pack_block(parts).reshape(2, HALF_ROWS, PACK_COLS)
    gathered = _all_gather8(lax.dynamic_index_in_dim(mine_w, c, 0, keepdims=False), "gather_weights")
    blocks = gathered.reshape(4, PACK_ROWS, PACK_COLS)
    full = {}
    for n in SHARDED:
        if n == "o_g_in":
            halves = _window(blocks, n, width=512).reshape(4, 1, 256, 2)
            full[n] = jnp.concatenate(list(lax.bitcast_convert_type(halves, jnp.float32)), axis=1)
        else:
            pieces = [_window(blocks[k], n) for k in range(4)]
            full[n] = jnp.concatenate(pieces, axis=1 if n in COL_SHARDED else 0).astype(MXU)

    loss_part, dx, grads = _local_step(
        x[0], positions.reshape(-1, 1), loss_target[0], e_g_in, full["e_w_in"], e_g_q_a, full["e_w_q_up"], e_g_kv_a,
        full["e_w_kv_up"], e_sinks, full["e_w_out"], full["o_g_in"], full["o_w_in"], o_b_f, full["o_w_out"],
        g_final.reshape(1, D))
    loss = lax.psum(loss_part, ("x", "y", "c"))

    per_chip = jnp.stack([_pack_block({n: _chip_slice(n, grads[n], k) for n in SHARDED}) for k in range(4)])
    g4 = per_chip.reshape(4, 2, HALF_ROWS, PACK_COLS)
    theirs = _pair_swap(g4, "grad_pair_swap")
    mine = lax.dynamic_index_in_dim(g4, c, 1, keepdims=False)
    pair_sums = _add_blocks(mine, theirs, "grad_pair_add", jnp.bfloat16)
    received = _chip_exchange(pair_sums, "grad_chip_exchange")
    my_half = _total_sum(lax.dynamic_index_in_dim(mine, chip, 0, keepdims=False),
                         lax.dynamic_index_in_dim(theirs, chip, 0, keepdims=False), received, "grad_chip_sum")
    other_half = _pair_exchange(my_half, "grad_pair_exchange")
    total = jnp.concatenate([jnp.where(c == 0, my_half, other_half), jnp.where(c == 0, other_half, my_half)], axis=0)
    gsum = {n: _window(total, n) for n in SHARDED}

    small = jnp.concatenate([jnp.pad(grads[n].reshape(-1), (0, (-grads[n].size) % LANES)) for n in REPLICATED])
    rows = small.shape[0] // LANES
    small = jnp.pad(small.reshape(rows, LANES), ((0, (-rows) % 8), (0, 0)))
    ssum = _sum_leading(_all_gather8(small, "gather_small_grads"), "small_grad_sum").reshape(-1)
    off = 0
    for n in REPLICATED:
        cnt = w[n].size
        gsum[n] = ssum[off:off + cnt].reshape(w[n].shape)
        off += cnt + (-cnt) % LANES

    grad, delta, new_m, new_v = {}, {}, {}, {}
    for n in order:
        if n in SHARDED:
            outs = _adamw(_as_handled(n, w[n]), gsum[n], _as_handled(n, m[n]), _as_handled(n, v[n]), "adamw_" + n)
            grad[n], delta[n], new_m[n], new_v[n] = (_as_given(n, a, w[n].shape) for a in (gsum[n],) + outs)
        else:
            grad[n] = gsum[n]
            delta[n], new_m[n], new_v[n] = _adamw(w[n], gsum[n], m[n], v[n], "adamw_" + n)
    return (loss, dx[None], *[grad[n] for n in order], *[delta[n] for n in order], *[new_m[n] for n in order],
            *[new_v[n] for n in order])
```

```python
import functools
import math

import numpy as np
import jax
import jax.numpy as jnp
from jax import lax
from jax.experimental import pallas as pl
from jax.experimental.pallas import tpu as pltpu

D = 1024
EPS = 1e-6
ROPE_THETA = 10000.0
N_MLA = 8
Q_RANK = 256
KV_RANK = 128
NOPE = 64
ROPE = 32
N_SWA = 8
WINDOW = 128
N_FOX = 16
HEAD = 64
E_SPLITS = (256, 128, 32, 512, 128, 128, 1024)
O_SPLITS = (1024, 1024, 1024, 16, 1024)
LR, B1, B2, AEPS, WD, STEP = 0.001, 0.9, 0.999, 1e-08, 0.01, 10

LANES = 128
HALF = 64
VMEM_LIMIT = 56 * 1024 * 1024
MXU = jnp.bfloat16
TOK = 256
WG_TOK = 2048
WG_ROWS = 1536
ATT = 256
FWD_CHUNK = 2
BWD_CHUNK = 2
SWA_GROUP = 4
NEG = float("-inf")

PACK_COLS = 1024
SUM_ROWS = 256
MESH_ID = pl.DeviceIdType.MESH


def _pcall(body, *, name, vmem=VMEM_LIMIT, semantics=None, **kw):
    params = dict(vmem_limit_bytes=vmem)
    if semantics is not None:
        params["dimension_semantics"] = semantics
    return pl.pallas_call(body, name=name, compiler_params=pltpu.CompilerParams(**params), **kw)


def _mm(a, b):
    return jnp.dot(a.astype(MXU), b.astype(MXU), preferred_element_type=jnp.float32)


def _mm_nt(a, b):
    return lax.dot_general(a.astype(MXU), b.astype(MXU), (((1,), (1,)), ((), ())),
                           preferred_element_type=jnp.float32)


def _mm_tn(a, b):
    return lax.dot_general(a.astype(MXU), b.astype(MXU), (((0,), (0,)), ((), ())),
                           preferred_element_type=jnp.float32)


def _full(shape):
    n = len(shape)
    return pl.BlockSpec(shape, lambda *_: (0,) * n)


def _rows(tm, n):
    return pl.BlockSpec((tm, n), lambda i: (i, 0))


def _sds(shape, dtype):
    return jax.ShapeDtypeStruct(shape, dtype)


def _rms(x, g):
    r = lax.rsqrt(jnp.mean(x * x, axis=-1, keepdims=True) + EPS)
    return x * r * g


def _rms_bwd(x, g, dy):
    r = lax.rsqrt(jnp.mean(x * x, axis=-1, keepdims=True) + EPS)
    xh = x * r
    dxh = dy * g
    dx = r * (dxh - xh * jnp.mean(dxh * xh, axis=-1, keepdims=True))
    return dx, dy * xh


def _sigmoid(x):
    return 1.0 / (1.0 + jnp.exp(-x))


def _lane_masks(dtype=None):
    lane = lax.broadcasted_iota(jnp.int32, (1, LANES), 1)
    return lane < HALF


def _split_heads(a, lo):
    z = jnp.zeros_like(a)
    return [jnp.where(lo, a, z), jnp.where(lo, z, a)]


def _rope_consts():
    inv = np.zeros((8, LANES), np.float32)
    j = np.arange(ROPE // 2, dtype=np.float32)
    f = (1.0 / (ROPE_THETA ** (np.arange(0, ROPE, 2, dtype=np.float32) / ROPE))).astype(np.float32)
    inv[0, HALF:HALF + 16] = f
    inv[0, HALF + 16:HALF + 32] = f
    inv[1, HALF:HALF + 16] = -1.0
    inv[1, HALF + 16:HALF + 32] = 1.0
    del j
    return jnp.asarray(inv)


def _rope_tables(pos_f, consts):
    ang = pos_f * consts[0:1, :]
    sign = consts[1:2, :]
    c = jnp.where(sign != 0.0, jnp.cos(ang), 1.0)
    s = jnp.sin(ang) * sign
    return c, s


def _swap_halves(v, sign):
    lo = pltpu.roll(v, LANES - 16, axis=1)
    hi = pltpu.roll(v, 16, axis=1)
    return jnp.where(sign < 0.0, lo, jnp.where(sign > 0.0, hi, 0.0))


def _rope(x, c, s, sign):
    return x * c + _swap_halves(x, sign) * s


def _rope_t(dy, c, s, sign):
    return dy * c + _swap_halves(dy * s, sign)


def _layer0_in(x, pos, g_in, w_in, g_q, w_q, g_kv, w_kv):
    S = x.shape[0]
    consts = _rope_consts()

    def body(x_ref, pos_ref, c_ref, g_ref, w_ref, gq_ref, wq_ref, gkv_ref, wkv_ref,
             h_ref, cq_ref, ckv_ref, cqn_ref, ckvn_ref, qm_ref, km_ref, vm_ref,
             qs_ref, kd_ref, vd_ref, gate_ref, cos_ref, sin_ref):
        h = _rms(x_ref[...], g_ref[...])
        h_ref[...] = h.astype(h_ref.dtype)
        z = _mm_nt(h, w_ref[...])
        cq = z[:, 0:256]
        ckv = z[:, 256:384]
        kpe = z[:, 384:512]
        cq_ref[...] = cq
        ckv_ref[...] = ckv
        qs_ref[...] = z[:, 512:1024].astype(qs_ref.dtype)
        kd_ref[...] = z[:, 1024:1536].astype(kd_ref.dtype)
        vd_ref[...] = z[:, 1536:2048].astype(vd_ref.dtype)
        gate_ref[...] = z[:, 2048:3072]
        cqn = _rms(cq, gq_ref[...])
        ckvn = _rms(ckv, gkv_ref[...])
        cqn_ref[...] = cqn.astype(cqn_ref.dtype)
        ckvn_ref[...] = ckvn.astype(ckvn_ref.dtype)
        q = _mm_nt(cqn, wq_ref[...])
        kv = _mm(ckvn, wkv_ref[...])
        vm_ref[...] = kv[:, 1024:1536].astype(vm_ref.dtype)
        consts_v = c_ref[...]
        sign = consts_v[1:2, :]
        c, s = _rope_tables(pos_ref[...].astype(jnp.float32), consts_v)
        cos_ref[...] = c
        sin_ref[...] = s
        kpe_r = _rope(kpe, c, s, sign)
        for hd in range(N_MLA):
            sl = slice(LANES * hd, LANES * (hd + 1))
            qm_ref[:, sl] = _rope(q[:, sl], c, s, sign).astype(qm_ref.dtype)
            km_ref[:, sl] = (kv[:, sl] + kpe_r).astype(km_ref.dtype)

    outs = [
        ((S, D), MXU), ((S, 256), jnp.float32), ((S, 128), jnp.float32), ((S, 256), MXU), ((S, 128), MXU),
        ((S, 1024), MXU), ((S, 1024), MXU), ((S, 512), MXU), ((S, 512), MXU), ((S, 512), MXU), ((S, 512), MXU),
        ((S, 1024), jnp.float32), ((S, 128), jnp.float32), ((S, 128), jnp.float32),
    ]
    return _pcall(
        body, name="layer0_in", grid=(S // TOK,), semantics=("arbitrary",),
        in_specs=[_rows(TOK, D), _rows(TOK, 1), _full((8, LANES)), _full((1, D)), _full(w_in.shape), _full((1, 256)),
                  _full(w_q.shape), _full((1, 128)), _full(w_kv.shape)],
        out_specs=[_rows(TOK, s[1]) for s, _ in outs],
        out_shape=[_sds(s, d) for s, d in outs],
    )(x, pos, consts, g_in, w_in, g_q, w_q, g_kv, w_kv)


AUG = (HALF, 0)
ONE = (HALF + 8, 8)


def _data_lanes(idx, h):
    return (idx < HALF) if h == 0 else (idx >= HALF)


def _three_terms(x):
    hi = x.astype(MXU).astype(jnp.float32)
    mid = (x - hi).astype(MXU).astype(jnp.float32)
    lo = (x - hi - mid).astype(MXU).astype(jnp.float32)
    return hi, mid, lo


def _q_aug(qblk, lc, h, scale, lane):
    a = AUG[h]
    hi, mid, lo = _three_terms(lc)
    ones = ((lane >= a + 3) & (lane <= a + 5)).astype(jnp.float32)
    aug = jnp.where(lane == a, hi, jnp.where(lane == a + 1, mid, jnp.where(lane == a + 2, lo, ones)))
    return jnp.where(_data_lanes(lane, h), qblk * jnp.asarray(scale, qblk.dtype), aug.astype(qblk.dtype))


def _k_aug(kblk, lc, h, lane):
    a = AUG[h]
    hi, mid, lo = _three_terms(-lc)
    ones = ((lane >= a) & (lane <= a + 2)).astype(jnp.float32)
    aug = jnp.where(lane == a + 3, hi, jnp.where(lane == a + 4, mid, jnp.where(lane == a + 5, lo, ones)))
    return jnp.where(_data_lanes(lane, h), kblk, aug.astype(kblk.dtype))


def _attn_fwd_t(q, k, v, scale, *, split, name, lcc=None, plan=None):
    S = q.shape[0]
    npair = v.shape[1] // LANES
    W = 2 * LANES if split else LANES
    T = ATT
    CH = FWD_CHUNK * T
    assert S % CH == 0
    nq = S // T

    def body(*refs):
        if split:
            q_ref, k_ref, v_ref, o_ref, lse_ref, vt, acc, m_sc = refs
        else:
            q_ref, k_ref, v_ref, lcc_ref, o_ref, lse_ref, kaug, vt, acc, m_sc = refs
        lane = lax.broadcasted_iota(jnp.int32, (1, LANES), 1)
        sub = lax.broadcasted_iota(jnp.int32, (LANES, 1), 0)
        key_minus_qry = lax.broadcasted_iota(jnp.int32, (CH, T), 0) - lax.broadcasted_iota(jnp.int32, (CH, T), 1)

        def prep(i, c):
            r0 = pl.multiple_of(i * T, T)
            vblk = v_ref[pl.ds(r0, T), :].astype(jnp.float32)
            for h in (0, 1):
                vh = jnp.where(_data_lanes(lane, h), vblk, (lane == ONE[h]).astype(jnp.float32))
                vt[h, :, pl.ds(r0, T)] = vh.T.astype(vt.dtype)
                if not split:
                    kaug[h, pl.ds(r0, T), :] = _k_aug(k_ref[pl.ds(r0, T), :], lcc_ref[h, pl.ds(r0, T), :], h, lane)
            return c

        lax.fori_loop(0, nq, prep, 0)

        def queries(qi):
            q0 = pl.multiple_of(qi * T, T)
            qblk = q_ref[pl.ds(q0, T), :]
            if split:
                return (qblk[:, :LANES], qblk[:, LANES:])
            return tuple(_q_aug(qblk, lcc_ref[h, pl.ds(q0, T), :], h, scale, lane) for h in (0, 1))

        def scores(qs, c):
            k0 = pl.multiple_of(c * CH, CH)
            out = []
            for h in (0, 1):
                if split:
                    out.append(_mm_nt(k_ref[pl.ds(k0, CH), LANES * h:LANES * (h + 1)], qs[h]) * scale)
                else:
                    out.append(_mm_nt(kaug[h, pl.ds(k0, CH), :], qs[h]))
            return tuple(out)

        def q_block(qi, carry):
            qs, first_scores = carry[:2], carry[2:]
            q0 = pl.multiple_of(qi * T, T)
            acc[...] = jnp.zeros_like(acc)
            m_sc[...] = jnp.full(m_sc.shape, NEG, jnp.float32)

            def absorb(c, sts, masked):
                k0 = pl.multiple_of(c * CH, CH)
                for h in (0, 1):
                    st = sts[h]
                    if masked:
                        st = jnp.where(key_minus_qry <= q0 - k0, st, NEG)
                    m_old = m_sc[h:h + 1, :]
                    m_new = jnp.maximum(m_old, jnp.max(st, axis=0, keepdims=True))
                    alpha = jnp.exp(m_old - m_new)
                    pt = jnp.exp(st - m_new)
                    acc[h] = alpha * acc[h] + _mm(vt[h, :, pl.ds(k0, CH)], pt)
                    m_sc[h:h + 1, :] = m_new

            last = qi // FWD_CHUNK

            def pipelined(c, sts):
                nxt = scores(qs, c + 1)
                absorb(c, sts, False)
                return nxt

            sts = lax.fori_loop(0, last, pipelined, first_scores)
            qs_next = queries(jnp.minimum(qi + 1, nq - 1))
            nxt = qs_next + scores(qs_next, 0)
            absorb(last, sts, True)
            ot = None
            for h in (0, 1):
                a = acc[h]
                l = a[ONE[h]:ONE[h] + 1, :]
                oh = jnp.where(_data_lanes(sub, h), a * (1.0 / l), 0.0)
                ot = oh if ot is None else ot + oh
                lse_ref[0, h:h + 1, pl.ds(q0, T)] = m_sc[h:h + 1, :] + jnp.log(l)
            o_ref[pl.ds(q0, T), :] = ot.T
            return nxt

        qs0 = queries(0)
        lax.fori_loop(0, nq, q_block, qs0 + scores(qs0, 0))

    wide = pl.BlockSpec((S, W), lambda j: (0, j))
    slab = pl.BlockSpec((S, LANES), lambda j: (0, j))
    rows = pl.BlockSpec((1, 2, S), lambda j: (j, 0, 0))
    in_specs = [wide, wide, slab]
    args = [q, k, v]
    scratch = []
    if not split:
        in_specs.append(pl.BlockSpec((2, S, 1), lambda j: (j, 0, 0)))
        args.append(lcc)
        scratch.append(pltpu.VMEM((2, S, LANES), MXU))
    scratch += [pltpu.VMEM((2, LANES, S), MXU), pltpu.VMEM((2, LANES, T), jnp.float32), pltpu.VMEM((8, T), jnp.float32)]
    (o, lse), rode = _pcall_riding(
        body, plan, args, name=name, grid=(npair,), in_specs=in_specs, out_specs=[slab, rows],
        out_shape=[_sds((S, npair * LANES), jnp.float32), _sds((npair, 2, S), jnp.float32)], scratch_shapes=scratch)
    return o, lse, rode


def _attn_bwd_t(q, k, v, do, o, lse, scale, *, split, name, lcc=None, plan=None):
    S = q.shape[0]
    npair = v.shape[1] // LANES
    W = 2 * LANES if split else LANES
    T = ATT
    CH = BWD_CHUNK * T
    assert S % CH == 0
    nq = S // T

    def body(*refs):
        if split:
            (q_ref, k_ref, v_ref, do_ref, o_ref, lse_ref, dq_ref, dk_ref, dv_ref, dqt, delta, dk_acc, dv_acc) = refs
        else:
            (q_ref, k_ref, v_ref, do_ref, o_ref, lse_ref, lcc_ref, dq_ref, dk_ref, dv_ref, dlc_ref,
             dqt, delta, dk_acc, dv_acc, qaug, csum) = refs
        lane = lax.broadcasted_iota(jnp.int32, (1, LANES), 1)
        sub = lax.broadcasted_iota(jnp.int32, (LANES, 1), 0)
        key_minus_qry = lax.broadcasted_iota(jnp.int32, (T, CH), 0) - lax.broadcasted_iota(jnp.int32, (T, CH), 1)

        def prep(i, c):
            r0 = pl.multiple_of(i * T, T)
            prod_t = (do_ref[pl.ds(r0, T), :].astype(jnp.float32) * o_ref[pl.ds(r0, T), :]).T
            for h in (0, 1):
                delta[h:h + 1, pl.ds(r0, T)] = jnp.sum(jnp.where(_data_lanes(sub, h), prod_t, 0.0), axis=0, keepdims=True)
                dqt[h, :, pl.ds(r0, T)] = jnp.zeros((LANES, T), jnp.float32)
                if not split:
                    qaug[h, pl.ds(r0, T), :] = _q_aug(q_ref[pl.ds(r0, T), :], lcc_ref[h, pl.ds(r0, T), :], h, scale, lane)
            return c

        lax.fori_loop(0, nq, prep, 0)

        def keys(ki):
            k0 = pl.multiple_of(ki * T, T)
            kblk = k_ref[pl.ds(k0, T), :]
            if split:
                return (kblk[:, :LANES], kblk[:, LANES:])
            return tuple(_k_aug(kblk, lcc_ref[h, pl.ds(k0, T), :], h, lane) for h in (0, 1))

        def q_of(c, h):
            q0 = pl.multiple_of(c * CH, CH)
            if split:
                return q_ref[pl.ds(q0, CH), LANES * h:LANES * (h + 1)]
            return qaug[h, pl.ds(q0, CH), :]

        def scores(khs, c):
            out = []
            for h in (0, 1):
                st = _mm_nt(khs[h], q_of(c, h))
                out.append(st * scale if split else st)
            return tuple(out)

        def k_block(ki, carry):
            khs, first_scores = carry[:2], carry[2:]
            k0 = pl.multiple_of(ki * T, T)
            khts = [kh.astype(jnp.float32).T.astype(kh.dtype) for kh in khs]
            vhs = _split_heads(v_ref[pl.ds(k0, T), :], lane < HALF)
            dk_acc[...] = jnp.zeros_like(dk_acc)
            dv_acc[...] = jnp.zeros_like(dv_acc)

            def absorb(c, vals):
                q0 = pl.multiple_of(c * CH, CH)
                dos = _split_heads(do_ref[pl.ds(q0, CH), :], lane < HALF)
                visible = key_minus_qry <= q0 - k0
                for h in (0, 1):
                    dpt = _mm_nt(vhs[h], dos[h])
                    st = jnp.where(visible, vals[h], NEG)
                    pt = jnp.exp(st - lse_ref[0, h:h + 1, pl.ds(q0, CH)])
                    dv_acc[...] += _mm(pt, dos[h])
                    dst = pt * (dpt - delta[h:h + 1, pl.ds(q0, CH)])
                    dk_acc[h] += _mm(dst, q_of(c, h))
                    dqt[h, :, pl.ds(q0, CH)] += _mm(khts[h], dst)

            first = ki // BWD_CHUNK

            def pipelined(c, vals):
                nxt = scores(khs, c + 1)
                absorb(c, vals)
                return nxt

            vals = lax.fori_loop(first, S // CH - 1, pipelined, first_scores)
            kn = jnp.minimum(ki + 1, nq - 1)
            khs_next = keys(kn)
            nxt = khs_next + scores(khs_next, kn // BWD_CHUNK)
            absorb(S // CH - 1, vals)
            if split:
                dk_ref[pl.ds(k0, T), :LANES] = (dk_acc[0] * scale).astype(dk_ref.dtype)
                dk_ref[pl.ds(k0, T), LANES:] = (dk_acc[1] * scale).astype(dk_ref.dtype)
            else:
                dk_ref[pl.ds(k0, T), :] = jnp.where(lane < HALF, dk_acc[0], dk_acc[1]).astype(dk_ref.dtype)
                for h in (0, 1):
                    csum[h:h + 1, pl.ds(k0, T)] = dk_acc[h].T[AUG[h] + 3:AUG[h] + 4, :]
            dv_ref[pl.ds(k0, T), :] = dv_acc[...].astype(dv_ref.dtype)
            return nxt

        khs0 = keys(0)
        lax.fori_loop(0, nq, k_block, khs0 + scores(khs0, 0))

        def finish(i, c):
            r0 = pl.multiple_of(i * T, T)
            if split:
                for h in (0, 1):
                    dq_ref[pl.ds(r0, T), LANES * h:LANES * (h + 1)] = (dqt[h, :, pl.ds(r0, T)].T * scale).astype(dq_ref.dtype)
            else:
                d = jnp.where(sub < HALF, dqt[0, :, pl.ds(r0, T)], dqt[1, :, pl.ds(r0, T)])
                dq_ref[pl.ds(r0, T), :] = (d.T * scale).astype(dq_ref.dtype)
                for h in (0, 1):
                    dlc_ref[0, h:h + 1, pl.ds(r0, T)] = dqt[h, AUG[h]:AUG[h] + 1, pl.ds(r0, T)] - csum[h:h + 1, pl.ds(r0, T)]
            return c

        lax.fori_loop(0, nq, finish, 0)

    wide = pl.BlockSpec((S, W), lambda j: (0, j))
    slab = pl.BlockSpec((S, LANES), lambda j: (0, j))
    rows = pl.BlockSpec((1, 2, S), lambda j: (j, 0, 0))
    in_specs = [wide, wide, slab, slab, slab, rows]
    args = [q, k, v, do, o, lse]
    out_specs = [wide, wide, slab]
    out_shape = [_sds(q.shape, jnp.float32 if split else do.dtype), _sds(k.shape, jnp.float32 if split else do.dtype),
                 _sds(v.shape, do.dtype)]
    scratch = [pltpu.VMEM((2, LANES, S), jnp.float32), pltpu.VMEM((8, S), jnp.float32),
               pltpu.VMEM((2, T, LANES), jnp.float32), pltpu.VMEM((T, LANES), jnp.float32)]
    if not split:
        in_specs.append(pl.BlockSpec((2, S, 1), lambda j: (j, 0, 0)))
        args.append(lcc)
        out_specs.append(rows)
        out_shape.append(_sds((npair, 2, S), jnp.float32))
        scratch += [pltpu.VMEM((2, S, LANES), MXU), pltpu.VMEM((8, S), jnp.float32)]
    outs, rode = _pcall_riding(body, plan, args, name=name, grid=(npair,), in_specs=in_specs, out_specs=out_specs,
                               out_shape=out_shape, scratch_shapes=scratch)
    return (*outs, rode)


def _swa_scores(qh, kblk, slope, shift):
    s = _mm_nt(qh, kblk) * (HEAD ** -0.5)
    a = lax.broadcasted_iota(jnp.int32, (WINDOW, 2 * WINDOW), 0)
    c = lax.broadcasted_iota(jnp.int32, (WINDOW, 2 * WINDOW), 1)
    dist = a - c + shift
    s = s - slope * dist.astype(jnp.float32)
    return jnp.where((dist >= 0) & (dist < WINDOW), s, NEG)


def _swa_fwd(q, kd, vd, sinks, slopes):
    S = q.shape[0]
    npair = q.shape[1] // LANES
    nb = S // WINDOW

    def body(sink_ref, slope_ref, q_ref, k_ref, v_ref, o_ref, lse_ref):
        j = pl.program_id(0)
        lo = _lane_masks()

        def q_block(qi, c):
            q0 = pl.multiple_of(qi * WINDOW, WINDOW)
            k0 = pl.multiple_of(jnp.maximum(qi - 1, 0) * WINDOW, WINDOW)
            shift = q0 - k0
            qs = _split_heads(q_ref[pl.ds(q0, WINDOW), :], lo)
            kblk = k_ref[pl.ds(k0, 2 * WINDOW), :]
            vs = _split_heads(v_ref[pl.ds(k0, 2 * WINDOW), :], lo)
            o = None
            for h in (0, 1):
                sink = sink_ref[2 * j + h]
                s = _swa_scores(qs[h], kblk, slope_ref[2 * j + h], shift)
                m = jnp.maximum(jnp.max(s, axis=1, keepdims=True), sink)
                p = jnp.exp(s - m)
                den = jnp.sum(p, axis=1, keepdims=True) + jnp.exp(sink - m)
                oh = _mm(p / den, vs[h])
                o = oh if o is None else o + oh
                lse_ref[h, pl.ds(q0, WINDOW), :] = m + jnp.log(den)
            o_ref[pl.ds(q0, WINDOW), :] = o
            return c

        def q_group(gi, c):
            for g in range(SWA_GROUP):
                q_block(gi * SWA_GROUP + g, c)
            return c

        lax.fori_loop(0, nb // SWA_GROUP, q_group, 0)

    smem = pl.BlockSpec(memory_space=pltpu.SMEM)
    slab = pl.BlockSpec((S, LANES), lambda j: (0, j))
    return _pcall(
        body, name="swa_fwd", grid=(npair,), semantics=("arbitrary",),
        in_specs=[smem, smem, slab, slab, slab],
        out_specs=[slab, pl.BlockSpec((2, S, 1), lambda j: (j, 0, 0))],
        out_shape=[_sds((S, npair * LANES), jnp.float32), _sds((2 * npair, S, 1), jnp.float32)],
    )(sinks, slopes, q, kd, vd)


def _swa_bwd(q, kd, vd, do, o, lse, sinks, slopes, plan=None):
    S = q.shape[0]
    npair = q.shape[1] // LANES
    nb = S // WINDOW

    def body(sink_ref, slope_ref, q_ref, k_ref, v_ref, do_ref, o_ref, lse_ref,
             dq_ref, dk_ref, dv_ref, dsink_ref, dk_acc, dv_acc):
        j = pl.program_id(0)
        lo = _lane_masks()
        dk_acc[...] = jnp.zeros_like(dk_acc)
        dv_acc[...] = jnp.zeros_like(dv_acc)

        def q_block(qi, carry):
            q0 = pl.multiple_of(qi * WINDOW, WINDOW)
            k0 = pl.multiple_of(jnp.maximum(qi - 1, 0) * WINDOW, WINDOW)
            shift = q0 - k0
            qs = _split_heads(q_ref[pl.ds(q0, WINDOW), :], lo)
            dos = _split_heads(do_ref[pl.ds(q0, WINDOW), :], lo)
            oblk = o_ref[pl.ds(q0, WINDOW), :]
            kblk = k_ref[pl.ds(k0, 2 * WINDOW), :]
            vblk = v_ref[pl.ds(k0, 2 * WINDOW), :]
            ks = _split_heads(kblk, lo)
            dq = None
            out = []
            for h in (0, 1):
                sink = sink_ref[2 * j + h]
                lse_h = lse_ref[h, pl.ds(q0, WINDOW), :]
                s = _swa_scores(qs[h], kblk, slope_ref[2 * j + h], shift)
                p = jnp.exp(s - lse_h)
                delta = jnp.sum(dos[h].astype(jnp.float32) * oblk, axis=1, keepdims=True)
                dv_acc[pl.ds(k0, 2 * WINDOW), :] += _mm_tn(p, dos[h])
                dp = _mm_nt(dos[h], vblk)
                ds = p * (dp - delta)
                dqh = _mm(ds, ks[h]) * (HEAD ** -0.5)
                dq = dqh if dq is None else dq + dqh
                dk_acc[pl.ds(k0, 2 * WINDOW), :] += _mm_tn(ds, qs[h]) * (HEAD ** -0.5)
                dsk = jnp.sum(-jnp.exp(sink - lse_h) * delta, axis=0, keepdims=True)
                out.append(carry[h] + dsk)
            dq_ref[pl.ds(q0, WINDOW), :] = dq.astype(dq_ref.dtype)
            return tuple(out)

        def q_group(gi, carry):
            for g in range(SWA_GROUP):
                carry = q_block(gi * SWA_GROUP + g, carry)
            return carry

        zero = jnp.zeros((1, 1), jnp.float32)
        dsa, dsb = lax.fori_loop(0, nb // SWA_GROUP, q_group, (zero, zero))
        dk_ref[...] = dk_acc[...].astype(dk_ref.dtype)
        dv_ref[...] = dv_acc[...].astype(dv_ref.dtype)
        r = lax.broadcasted_iota(jnp.int32, (8, LANES), 0)
        dsink_ref[0] = jnp.where(r == 0, dsa, jnp.where(r == 1, dsb, 0.0))

    smem = pl.BlockSpec(memory_space=pltpu.SMEM)
    slab = pl.BlockSpec((S, LANES), lambda j: (0, j))
    outs, rode = _pcall_riding(
        body, plan, [sinks, slopes, q, kd, vd, do, o, lse], name="swa_bwd", grid=(npair,),
        in_specs=[smem, smem, slab, slab, slab, slab, slab, pl.BlockSpec((2, S, 1), lambda j: (j, 0, 0))],
        out_specs=[slab, slab, slab, pl.BlockSpec((1, 8, LANES), lambda j: (j, 0, 0))],
        out_shape=[_sds(q.shape, do.dtype), _sds(kd.shape, do.dtype), _sds(vd.shape, do.dtype),
                   _sds((npair, 8, LANES), jnp.float32)],
        scratch_shapes=[pltpu.VMEM((S, LANES), jnp.float32), pltpu.VMEM((S, LANES), jnp.float32)])
    return (*outs, rode)


def _log_steps(S):
    k, out = 1, []
    while k < S:
        out.append(k)
        k *= 2
    return out


def _forget_fwd(f_row, b_col):
    S = f_row.shape[1]

    def body(f_ref, b_ref, lc_ref):
        x = f_ref[...] + b_ref[...]
        lc = jnp.minimum(x, 0.0) - jnp.log(1.0 + jnp.exp(-jnp.abs(x)))
        idx = lax.broadcasted_iota(jnp.int32, lc.shape, 1)
        for k in _log_steps(S):
            lc = lc + jnp.where(idx >= k, pltpu.roll(lc, k, axis=1), 0.0)
        lc_ref[...] = lc

    return _pcall(body, name="forget_fwd", out_shape=_sds(f_row.shape, jnp.float32))(f_row, b_col)


def _forget_bwd(dlc_row, f_row, b_col):
    S = f_row.shape[1]

    def body(d_ref, f_ref, b_ref, df_ref, db_ref):
        g = d_ref[...]
        idx = lax.broadcasted_iota(jnp.int32, g.shape, 1)
        for k in _log_steps(S):
            g = g + jnp.where(idx < S - k, pltpu.roll(g, S - k, axis=1), 0.0)
        x = f_ref[...] + b_ref[...]
        df = g * _sigmoid(-x)
        df_ref[...] = df
        db_ref[...] = jnp.sum(df, axis=1, keepdims=True)

    return _pcall(body, name="forget_bwd",
                  out_shape=[_sds(f_row.shape, jnp.float32), _sds((f_row.shape[0], 1), jnp.float32)])(dlc_row, f_row, b_col)


def _layer0_out_layer1_in(x, o_m, o_s, gate, w_out, g1, w_in1):
    S = x.shape[0]

    def body(x_ref, om_ref, os_ref, gate_ref, wo_ref, g_ref, w_ref,
             x1_ref, u_ref, h_ref, q_ref, k_ref, v_ref, g1_ref, f_ref):
        gt = gate_ref[...]
        sg = gt * _sigmoid(gt)
        um = om_ref[...] * sg[:, :512]
        us = os_ref[...] * sg[:, 512:]
        u_ref[:, :512] = um.astype(u_ref.dtype)
        u_ref[:, 512:] = us.astype(u_ref.dtype)
        x1 = x_ref[...] + _mm(um, wo_ref[0:512, :]) + _mm(us, wo_ref[512:1024, :])
        x1_ref[...] = x1
        h = _rms(x1, g_ref[...])
        h_ref[...] = h.astype(h_ref.dtype)
        z = _mm_nt(h, w_ref[...])
        q_ref[...] = z[:, 0:1024].astype(q_ref.dtype)
        k_ref[...] = z[:, 1024:2048].astype(k_ref.dtype)
        v_ref[...] = z[:, 2048:3072].astype(v_ref.dtype)
        g1_ref[...] = z[:, 3072:4096]
        f_ref[...] = z[:, 4096:4224]

    outs = [((S, D), jnp.float32), ((S, D), MXU), ((S, D), MXU), ((S, D), MXU), ((S, D), MXU), ((S, D), MXU),
            ((S, D), jnp.float32), ((S, LANES), jnp.float32)]
    return _pcall(
        body, name="layer0_out_layer1_in", grid=(S // TOK,), semantics=("arbitrary",),
        in_specs=[_rows(TOK, D), _rows(TOK, 512), _rows(TOK, 512), _rows(TOK, D), _full((D, D)), _full((1, D)),
                  _full(w_in1.shape)],
        out_specs=[_rows(TOK, s[1]) for s, _ in outs],
        out_shape=[_sds(s, d) for s, d in outs],
    )(x, o_m, o_s, gate, w_out, g1, w_in1)


def _head(x1, o1, gate1, w_out1, g_f, target):
    S = x1.shape[0]

    def body(x1_ref, o_ref, gate_ref, wo_ref, g_ref, t_ref,
             loss_ref, dgf_ref, dx2_ref, u_ref, do_ref, dgate_ref):
        i = pl.program_id(0)
        gt = gate_ref[...]
        sig = _sigmoid(gt)
        sg = gt * sig
        o = o_ref[...]
        u = o * sg
        u_ref[...] = u.astype(u_ref.dtype)
        x2 = x1_ref[...] + _mm(u, wo_ref[...])
        g = g_ref[...]
        y = _rms(x2, g)
        err = y - t_ref[...]
        part = 0.5 * jnp.sum(jnp.mean(err * err, axis=-1, keepdims=True), axis=0, keepdims=True)
        dy = err * (1.0 / D)
        dx2, dg_rows = _rms_bwd(x2, g, dy)
        dx2_ref[...] = dx2
        du = _mm_nt(dx2, wo_ref[...])
        do_ref[...] = (du * sg).astype(do_ref.dtype)
        dgate_ref[...] = (du * o * (sig * (1.0 + gt * (1.0 - sig)))).astype(dgate_ref.dtype)

        @pl.when(i == 0)
        def _():
            loss_ref[...] = jnp.zeros_like(loss_ref)
            dgf_ref[...] = jnp.zeros_like(dgf_ref)

        loss_ref[...] += jnp.broadcast_to(part, loss_ref.shape)
        dgf_ref[...] += jnp.sum(dg_rows, axis=0, keepdims=True)

    outs = [((S, D), jnp.float32), ((S, D), MXU), ((S, D), MXU), ((S, D), MXU)]
    return _pcall(
        body, name="head", grid=(S // TOK,), semantics=("arbitrary",),
        in_specs=[_rows(TOK, D), _rows(TOK, D), _rows(TOK, D), _full((D, D)), _full((1, D)), _rows(TOK, D)],
        out_specs=[_full((8, LANES)), _full((1, D))] + [_rows(TOK, D) for _ in outs],
        out_shape=[_sds((8, LANES), jnp.float32), _sds((1, D), jnp.float32)] + [_sds(s, d) for s, d in outs],
    )(x1, o1, gate1, w_out1, g_f, target)


def _layer1_in_bwd(dq, dk, dv, dgate1, df, x1, dx2, g1, w_in1, gate0, o_m, o_s, w_out0):
    S = x1.shape[0]

    def body(dq_ref, dk_ref, dv_ref, dg1_ref, df_ref, x1_ref, dx2_ref, g_ref, w_ref, gate_ref, om_ref, os_ref,
             wo_ref, dz_ref, dx1_ref, dgn_ref, dom_ref, dos_ref, dgate_ref):
        i = pl.program_id(0)
        dz_ref[:, 0:1024] = dq_ref[...]
        dz_ref[:, 1024:2048] = dk_ref[...]
        dz_ref[:, 2048:3072] = dv_ref[...]
        dz_ref[:, 3072:4096] = dg1_ref[...]
        dz_ref[:, 4096:4224] = df_ref[...]
        dh = _mm(dz_ref[...], w_ref[...])
        g = g_ref[...]
        dxn, dg_rows = _rms_bwd(x1_ref[...], g, dh)
        dx1 = dx2_ref[...] + dxn
        dx1_ref[...] = dx1
        du = _mm_nt(dx1, wo_ref[...])
        gt = gate_ref[...]
        sig = _sigmoid(gt)
        sg = gt * sig
        dsg = sig * (1.0 + gt * (1.0 - sig))
        dom_ref[...] = (du[:, :512] * sg[:, :512]).astype(dom_ref.dtype)
        dos_ref[...] = (du[:, 512:] * sg[:, 512:]).astype(dos_ref.dtype)
        dgate_ref[:, :512] = (du[:, :512] * om_ref[...] * dsg[:, :512]).astype(dgate_ref.dtype)
        dgate_ref[:, 512:] = (du[:, 512:] * os_ref[...] * dsg[:, 512:]).astype(dgate_ref.dtype)

        @pl.when(i == 0)
        def _():
            dgn_ref[...] = jnp.zeros_like(dgn_ref)

        dgn_ref[...] += jnp.sum(dg_rows, axis=0, keepdims=True)

    return _pcall(
        body, name="layer1_in_bwd", grid=(S // TOK,), semantics=("arbitrary",),
        in_specs=[_rows(TOK, D), _rows(TOK, D), _rows(TOK, D), _rows(TOK, D), _rows(TOK, LANES), _rows(TOK, D),
                  _rows(TOK, D), _full((1, D)), _full(w_in1.shape), _rows(TOK, D), _rows(TOK, 512), _rows(TOK, 512),
                  _full((D, D))],
        out_specs=[_rows(TOK, 4224), _rows(TOK, D), _full((1, D)), _rows(TOK, 512), _rows(TOK, 512), _rows(TOK, D)],
        out_shape=[_sds((S, 4224), MXU), _sds((S, D), jnp.float32), _sds((1, D), jnp.float32),
                   _sds((S, 512), MXU), _sds((S, 512), MXU), _sds((S, D), MXU)],
    )(dq, dk, dv, dgate1, df, x1, dx2, g1, w_in1, gate0, o_m, o_s, w_out0)


def _layer0_in_bwd(dqm, dkm, dvm, dqs, dkd, dvd, dgate0, cos, sin, cq, ckv, x, dx1, g_in, w_in, g_q, w_q, g_kv, w_kv):
    S = x.shape[0]
    consts = _rope_consts()

    def body(dqm_ref, dkm_ref, dvm_ref, dqs_ref, dkd_ref, dvd_ref, dgate_ref, cos_ref, sin_ref, c_ref, cq_ref, ckv_ref,
             x_ref, dx1_ref, g_ref, w_ref, gq_ref, wq_ref, gkv_ref, wkv_ref,
             dx_ref, dz_ref, dqu_ref, dkvu_ref, dgin_ref, dgq_ref, dgkv_ref):
        i = pl.program_id(0)
        lo = _lane_masks()
        sign = c_ref[...][1:2, :]
        c = cos_ref[...]
        s = sin_ref[...]
        dkpe = None
        for hd in range(N_MLA):
            sl = slice(LANES * hd, LANES * (hd + 1))
            dqu_ref[:, sl] = _rope_t(dqm_ref[:, sl], c, s, sign).astype(dqu_ref.dtype)
            dkh = dkm_ref[:, sl]
            dkvu_ref[:, sl] = jnp.where(lo, dkh, 0.0).astype(dkvu_ref.dtype)
            dkpe = dkh if dkpe is None else dkpe + dkh
        dkvu_ref[:, 1024:1536] = dvm_ref[...]
        dkpe = _rope_t(jnp.where(lo, 0.0, dkpe), c, s, sign)
        dcqn = _mm(dqu_ref[...], wq_ref[...])
        dckvn = _mm_nt(dkvu_ref[...], wkv_ref[...])
        gq = gq_ref[...]
        gkv = gkv_ref[...]
        dcq, dgq_rows = _rms_bwd(cq_ref[...], gq, dcqn)
        dckv, dgkv_rows = _rms_bwd(ckv_ref[...], gkv, dckvn)
        dz_ref[:, 0:256] = dcq.astype(dz_ref.dtype)
        dz_ref[:, 256:384] = dckv.astype(dz_ref.dtype)
        dz_ref[:, 384:512] = dkpe.astype(dz_ref.dtype)
        dz_ref[:, 512:1024] = dqs_ref[...]
        dz_ref[:, 1024:1536] = dkd_ref[...]
        dz_ref[:, 1536:2048] = dvd_ref[...]
        dz_ref[:, 2048:3072] = dgate_ref[...]
        dh = _mm(dz_ref[...], w_ref[...])
        g = g_ref[...]
        dxn, dg_rows = _rms_bwd(x_ref[...], g, dh)
        dx_ref[...] = dx1_ref[...] + dxn

        @pl.when(i == 0)
        def _():
            dgin_ref[...] = jnp.zeros_like(dgin_ref)
            dgq_ref[...] = jnp.zeros_like(dgq_ref)
            dgkv_ref[...] = jnp.zeros_like(dgkv_ref)

        dgin_ref[...] += jnp.sum(dg_rows, axis=0, keepdims=True)
        dgq_ref[...] += jnp.sum(dgq_rows, axis=0, keepdims=True)
        dgkv_ref[...] += jnp.sum(dgkv_rows, axis=0, keepdims=True)

    return _pcall(
        body, name="layer0_in_bwd", grid=(S // TOK,), semantics=("arbitrary",),
        in_specs=[_rows(TOK, 1024), _rows(TOK, 1024), _rows(TOK, 512), _rows(TOK, 512), _rows(TOK, 512), _rows(TOK, 512),
                  _rows(TOK, D), _rows(TOK, LANES), _rows(TOK, LANES), _full((8, LANES)), _rows(TOK, 256), _rows(TOK, 128),
                  _rows(TOK, D), _rows(TOK, D), _full((1, D)), _full(w_in.shape), _full((1, 256)), _full(w_q.shape),
                  _full((1, 128)), _full(w_kv.shape)],
        out_specs=[_rows(TOK, D), _rows(TOK, 3072), _rows(TOK, 1024), _rows(TOK, 1536), _full((1, D)), _full((1, 256)),
                   _full((1, 128))],
        out_shape=[_sds((S, D), jnp.float32), _sds((S, 3072), MXU), _sds((S, 1024), MXU), _sds((S, 1536), MXU),
                   _sds((1, D), jnp.float32), _sds((1, 256), jnp.float32), _sds((1, 128), jnp.float32)],
    )(dqm, dkm, dvm, dqs, dkd, dvd, dgate0, cos, sin, consts, cq, ckv, x, dx1, g_in, w_in, g_q, w_q, g_kv, w_kv)


def _wgrad(a, b, name):
    S, M = a.shape
    N = b.shape[1]
    tm = next(t for t in range(WG_ROWS, 0, -LANES) if M % t == 0)
    tn = N if N <= 1024 else 512
    tk = min(WG_TOK, S)

    def body(a_ref, b_ref, o_ref):
        @pl.when(pl.program_id(2) == 0)
        def _():
            o_ref[...] = jnp.zeros_like(o_ref)

        o_ref[...] += _mm_tn(a_ref[...], b_ref[...])

    return _pcall(
        body, name=name, grid=(M // tm, N // tn, S // tk), semantics=("parallel", "parallel", "arbitrary"),
        in_specs=[pl.BlockSpec((tk, tm), lambda m, n, k: (k, m)), pl.BlockSpec((tk, tn), lambda m, n, k: (k, n))],
        out_specs=pl.BlockSpec((tm, tn), lambda m, n, k: (m, n)),
        out_shape=_sds((M, N), jnp.float32),
    )(a, b)


def _adamw(w, g, m, v, name):
    shape = w.shape
    R, C = (int(np.prod(shape[:-1])), shape[-1])
    w2, g2, m2, v2 = (t.reshape(R, C) for t in (w, g, m, v))
    tr = 256 if R % 256 == 0 else R
    tc = 256 if (tr == R and R > 256 and C % 256 == 0) else C

    def body(w_ref, g_ref, m_ref, v_ref, d_ref, nm_ref, nv_ref):
        gg = g_ref[...]
        nm = B1 * m_ref[...] + (1.0 - B1) * gg
        nv = B2 * v_ref[...] + (1.0 - B2) * (gg * gg)
        m_hat = nm / (1.0 - B1 ** STEP)
        v_hat = nv / (1.0 - B2 ** STEP)
        d_ref[...] = -LR * (m_hat / (jnp.sqrt(v_hat) + AEPS) + WD * w_ref[...])
        nm_ref[...] = nm
        nv_ref[...] = nv

    spec = pl.BlockSpec((tr, tc), lambda i, j: (i, j))
    d, nm, nv = _pcall(
        body, name=name, grid=(R // tr, C // tc), semantics=("parallel", "parallel"),
        in_specs=[spec] * 4, out_specs=[spec] * 3, out_shape=[_sds((R, C), jnp.float32)] * 3,
    )(w2, g2, m2, v2)
    return d.reshape(shape), nm.reshape(shape), nv.reshape(shape)


def _sum_leading(a, name):
    n, R, C = a.shape
    tr = SUM_ROWS if R % SUM_ROWS == 0 else R

    def body(a_ref, o_ref):
        acc = a_ref[0]
        for i in range(1, n):
            acc = acc + a_ref[i]
        o_ref[...] = acc

    return _pcall(
        body, name=name, grid=(R // tr,), semantics=("parallel",),
        in_specs=[pl.BlockSpec((n, tr, C), lambda i: (0, i, 0))], out_specs=_rows(tr, C),
        out_shape=_sds((R, C), a.dtype),
    )(a)


def _add_blocks(a, b, name, out_dtype):
    n, R, C = a.shape
    tr = SUM_ROWS if R % SUM_ROWS == 0 else R

    def body(a_ref, b_ref, o_ref):
        o_ref[...] = (a_ref[...] + b_ref[...]).astype(o_ref.dtype)

    spec = pl.BlockSpec((1, tr, C), lambda k, i: (k, i, 0))
    return _pcall(
        body, name=name, grid=(n, R // tr), semantics=("parallel", "parallel"),
        in_specs=[spec, spec], out_specs=spec, out_shape=_sds(a.shape, out_dtype),
    )(a, b)


def _total_sum(mine, theirs, recv, name):
    R, C = mine.shape
    n = recv.shape[0]
    tr = SUM_ROWS if R % SUM_ROWS == 0 else R

    def body(a_ref, b_ref, r_ref, o_ref):
        acc = a_ref[...] + b_ref[...]
        for i in range(n):
            acc = acc + r_ref[i].astype(jnp.float32)
        o_ref[...] = acc

    return _pcall(
        body, name=name, grid=(R // tr,), semantics=("parallel",),
        in_specs=[_rows(tr, C), _rows(tr, C), pl.BlockSpec((n, tr, C), lambda i: (0, i, 0))], out_specs=_rows(tr, C),
        out_shape=_sds((R, C), jnp.float32),
    )(mine, theirs, recv)


def _place():
    return lax.axis_index("x"), lax.axis_index("y"), lax.axis_index("c")


class _Plan:
    def __init__(self, arrays, out_shape, scratch, start, finish, middle=None):
        self.arrays, self.out_shape, self.scratch = list(arrays), list(out_shape), list(scratch)
        self.start, self.finish, self.middle = start, finish, middle


def _gather8_plan(block):
    R, C = block.shape

    def parts(ins, outs, sems):
        (x_ref,), (out_ref,), (send_sems, recv_sems) = ins, outs, sems
        x, y, c = _place()
        me, sibling = (x, y, c), (x, y, 1 - c)
        chips = [(1 - x, y), (x, 1 - y), (1 - x, 1 - y)]

        def copy(k, blk, to, src=None):
            slot = out_ref.at[4 * blk[0] + 2 * blk[1] + blk[2]]
            return pltpu.make_async_remote_copy(
                src_ref=slot if src is None else src, dst_ref=slot,
                send_sem=send_sems.at[k], recv_sem=recv_sems.at[k], device_id=to, device_id_type=MESH_ID)

        def first():
            return [copy(0, me, sibling, src=x_ref)] + [copy(1 + j, me, (*chip, c), src=x_ref) for j, chip in enumerate(chips)]

        def passed():
            return [copy(4 + j, (*chip, c), sibling) for j, chip in enumerate(chips)]

        def arrivals():
            return [copy(1 + j, (*chip, c), me) for j, chip in enumerate(chips)]

        def late():
            return [copy(0, sibling, me)] + [copy(4 + j, (*chip, 1 - c), me) for j, chip in enumerate(chips)]

        return first, passed, arrivals, late

    def start(ins, outs, sems):
        for cp in parts(ins, outs, sems)[0]():
            cp.start()

    def middle(ins, outs, sems):
        _, passed, arrivals, _ = parts(ins, outs, sems)
        for arrived, forward in zip(arrivals(), passed()):
            arrived.wait_recv()
            forward.start()

    def finish(ins, outs, sems):
        first, passed, _, late = parts(ins, outs, sems)
        for cp in late():
            cp.wait_recv()
        for cp in first() + passed():
            cp.wait_send()

    return _Plan([block], [_sds((8, R, C), block.dtype)], [pltpu.SemaphoreType.DMA((7,)), pltpu.SemaphoreType.DMA((7,))],
                 start, finish, middle)


def _fill_own_slot(gathered, block):
    x, y, c = _place()
    return lax.dynamic_update_index_in_dim(gathered, block, 4 * x + 2 * y + c, 0)


def _started_and_waited(arrays, out_shape, n, copies):
    def start(ins, outs, sems):
        for cp in copies(ins, outs, sems):
            cp.start()

    def finish(ins, outs, sems):
        for cp in copies(ins, outs, sems):
            cp.wait()

    return _Plan(arrays, out_shape, [pltpu.SemaphoreType.DMA((n,)), pltpu.SemaphoreType.DMA((n,))], start, finish)


def _pair_swap_plan(g):
    n = g.shape[0]

    def copies(ins, outs, sems):
        (g_ref,), (out_ref,), (send_sems, recv_sems) = ins, outs, sems
        x, y, c = _place()
        return [pltpu.make_async_remote_copy(src_ref=g_ref.at[k, 1 - c], dst_ref=out_ref.at[k], send_sem=send_sems.at[k],
                                             recv_sem=recv_sems.at[k], device_id=(x, y, 1 - c), device_id_type=MESH_ID)
                for k in range(n)]

    return _started_and_waited([g], [_sds((n,) + g.shape[2:], g.dtype)], n, copies)


def _chip_exchange_plan(p):
    def copies(ins, outs, sems):
        (p_ref,), (out_ref,), (send_sems, recv_sems) = ins, outs, sems
        x, y, c = _place()
        chips = [(1 - x, y), (x, 1 - y), (1 - x, 1 - y)]
        return [pltpu.make_async_remote_copy(
            src_ref=p_ref.at[2 * cx + cy], dst_ref=out_ref.at[j], send_sem=send_sems.at[j],
            recv_sem=recv_sems.at[j], device_id=(cx, cy, c), device_id_type=MESH_ID)
            for j, (cx, cy) in enumerate(chips)]

    return _started_and_waited([p], [_sds((3,) + p.shape[1:], p.dtype)], 3, copies)


def _pair_exchange_plan(t):
    def copies(ins, outs, sems):
        (t_ref,), (out_ref,), (send_sems, recv_sems) = ins, outs, sems
        x, y, c = _place()
        return [pltpu.make_async_remote_copy(src_ref=t_ref, dst_ref=out_ref, send_sem=send_sems.at[0], recv_sem=recv_sems.at[0],
                                             device_id=(x, y, 1 - c), device_id_type=MESH_ID)]

    return _started_and_waited([t], [_sds(t.shape, t.dtype)], 1, copies)


ANY_SPEC = pl.BlockSpec(memory_space=pl.ANY)


def _run_plan(plan, name):
    n_in, n_out = len(plan.arrays), len(plan.out_shape)

    def body(*refs):
        ins, outs, sems = refs[:n_in], refs[n_in:n_in + n_out], refs[n_in + n_out:]
        plan.start(ins, outs, sems)
        if plan.middle is not None:
            plan.middle(ins, outs, sems)
        plan.finish(ins, outs, sems)

    return _pcall(body, name=name, in_specs=[ANY_SPEC] * n_in, out_specs=[ANY_SPEC] * n_out, out_shape=plan.out_shape,
                  scratch_shapes=plan.scratch)(*plan.arrays)


def _pcall_riding(body, plan, args, *, name, grid, in_specs, out_specs, out_shape, scratch_shapes):
    if plan is None:
        outs = _pcall(body, name=name, grid=grid, semantics=("arbitrary",), in_specs=in_specs, out_specs=out_specs,
                      out_shape=out_shape, scratch_shapes=scratch_shapes)(*args)
        return list(outs), None
    n_in, n_out, n_s = len(args), len(out_shape), len(scratch_shapes)
    p_in, p_out = len(plan.arrays), len(plan.out_shape)
    steps = grid[0]

    def riding(*refs):
        ins, pins = refs[:n_in], refs[n_in:n_in + p_in]
        o0 = n_in + p_in
        outs, pouts = refs[o0:o0 + n_out], refs[o0 + n_out:o0 + n_out + p_out]
        s0 = o0 + n_out + p_out
        scr, sems = refs[s0:s0 + n_s], refs[s0 + n_s:]
        j = pl.program_id(0)

        @pl.when(j == 0)
        def _():
            plan.start(pins, pouts, sems)

        if plan.middle is not None:
            @pl.when(j == steps // 2)
            def _():
                plan.middle(pins, pouts, sems)

        body(*ins, *outs, *scr)

        @pl.when(j == steps - 1)
        def _():
            plan.finish(pins, pouts, sems)

    res = _pcall(riding, name=name, grid=grid, semantics=("arbitrary",), in_specs=list(in_specs) + [ANY_SPEC] * p_in,
                 out_specs=list(out_specs) + [ANY_SPEC] * p_out, out_shape=list(out_shape) + plan.out_shape,
                 scratch_shapes=list(scratch_shapes) + plan.scratch)(*args, *plan.arrays)
    return list(res[:n_out]), list(res[n_out:])


def _prep_w_in0(wt):
    z32 = jnp.zeros((32, wt.shape[1]), wt.dtype)
    z64 = jnp.zeros((64, wt.shape[1]), wt.dtype)
    k0, k1 = wt[928:992], wt[992:1056]
    v0, v1 = wt[1056:1120], wt[1120:1184]
    return jnp.concatenate([wt[0:384], z64, wt[384:416], z32, wt[416:928],
                            k0, k0, k0, k0, k1, k1, k1, k1, v0, v0, v0, v0, v1, v1, v1, v1, wt[1184:2208]], axis=0)


def _fold_w_in0(d):
    def fold(blk):
        b = blk.reshape(8, 64, blk.shape[1])
        return jnp.concatenate([b[0] + b[1] + b[2] + b[3], b[4] + b[5] + b[6] + b[7]], axis=0)
    return jnp.concatenate([d[0:384], d[448:480], d[512:1024], fold(d[1024:1536]), fold(d[1536:2048]), d[2048:3072]], axis=0)


def _prep_w_q(wt):
    return jnp.pad(wt.reshape(N_MLA, 96, Q_RANK), ((0, 0), (0, 32), (0, 0))).reshape(1024, Q_RANK)


def _fold_w_q(d):
    return d.reshape(N_MLA, 128, Q_RANK)[:, :96].reshape(768, Q_RANK)


def _prep_w_kv(w):
    w3 = w.reshape(KV_RANK, N_MLA, 128)
    kk = jnp.pad(w3[:, :, :64], ((0, 0), (0, 0), (0, 64))).reshape(KV_RANK, 1024)
    return jnp.concatenate([kk, w3[:, :, 64:].reshape(KV_RANK, 512)], axis=1)


def _fold_w_kv(d):
    kk = d[:, :1024].reshape(KV_RANK, N_MLA, 128)[:, :, :64]
    vv = d[:, 1024:].reshape(KV_RANK, N_MLA, 64)
    return jnp.concatenate([kk, vv], axis=2).reshape(KV_RANK, 1024)


def _prep_w_in1(wt):
    return jnp.concatenate([wt[0:3072], wt[3088:4112], wt[3072:3088], jnp.zeros((112, wt.shape[1]), wt.dtype)], axis=0)


def _fold_w_in1(d):
    return jnp.concatenate([d[0:3072], d[4096:4112], d[3072:4096]], axis=0)


class _Alone:
    def __init__(self, o_g_in, w_in1, w_out1):
        self.layer1 = (o_g_in, w_in1, w_out1)

    def gather_plan(self):
        return None

    def layer1_weights(self, rode):
        return self.layer1

    def swap_plan(self, grads1):
        return None

    def exchange_plan(self, rode):
        return None

    def finish(self, rode):
        pass


def _local_step(x, pos, target, e_g_in, w_in0, e_g_q, w_q, e_g_kv, w_kv, sinks, w_out0, b_f, g_final, layer1):
    S = x.shape[0]
    w_in0p, w_qp, w_kvp = _prep_w_in0(w_in0), _prep_w_q(w_q), _prep_w_kv(w_kv)
    slopes = jnp.asarray(2.0 ** (-8.0 * (np.arange(N_SWA, dtype=np.float32) + 1.0) / N_SWA), jnp.float32)
    sinks1 = sinks.reshape(N_SWA)
    b_col = b_f.reshape(N_FOX, 1)

    (h0, cq, ckv, cqn, ckvn, qm, km, vm, qs, kd, vd, gate0, cos, sin) = _layer0_in(
        x, pos, e_g_in, w_in0p, e_g_q, w_qp, e_g_kv, w_kvp)
    o_m, lse_m, rode = _attn_fwd_t(qm, km, vm, (NOPE + ROPE) ** -0.5, split=True, name="mla_fwd", plan=layer1.gather_plan())
    o_g_in, w_in1, w_out1 = layer1.layer1_weights(rode)
    w_in1p = _prep_w_in1(w_in1)
    o_s, lse_s = _swa_fwd(qs, kd, vd, sinks1, slopes)
    x1, u0, h1, q1, k1, v1, gate1, f_slab = _layer0_out_layer1_in(x, o_m, o_s, gate0, w_out0, o_g_in, w_in1p)
    f_row = f_slab[:, :N_FOX].T
    lc_row = _forget_fwd(f_row, b_col)
    lcc = lc_row.reshape(N_FOX, S, 1)
    o1, lse1, _ = _attn_fwd_t(q1, k1, v1, HEAD ** -0.5, split=False, name="fox_fwd", lcc=lcc)
    loss8, dg_final, dx2, u1, do1, dgate1 = _head(x1, o1, gate1, w_out1, g_final, target)

    dq1, dk1, dv1, dlc, _ = _attn_bwd_t(q1, k1, v1, do1, o1, lse1, HEAD ** -0.5, split=False, name="fox_bwd", lcc=lcc)
    df_row, db_f = _forget_bwd(dlc.reshape(N_FOX, S), f_row, b_col)
    df_slab = jnp.pad(df_row.T, ((0, 0), (0, LANES - N_FOX))).astype(MXU)
    dz1, dx1, dg_o_in, do_m, do_s, dgate0 = _layer1_in_bwd(
        dq1, dk1, dv1, dgate1, df_slab, x1, dx2, o_g_in, w_in1p, gate0, o_m, o_s, w_out0)
    grads1 = dict(o_g_in=dg_o_in, o_w_in=_fold_w_in1(_wgrad(dz1, h1, "wgrad_in1")), o_w_out=_wgrad(u1, dx2, "wgrad_out1"))
    dqs, dkd, dvd, dsink, rode = _swa_bwd(qs, kd, vd, do_s, o_s, lse_s, sinks1, slopes, plan=layer1.swap_plan(grads1))
    dqm, dkm, dvm, rode = _attn_bwd_t(qm, km, vm, do_m, o_m, lse_m, (NOPE + ROPE) ** -0.5, split=True, name="mla_bwd",
                                      plan=layer1.exchange_plan(rode))
    layer1.finish(rode)
    dx, dz0, dqu, dkvu, dg_in, dg_q, dg_kv = _layer0_in_bwd(
        dqm, dkm, dvm, dqs, dkd, dvd, dgate0, cos, sin, cq, ckv, x, dx1, e_g_in, w_in0p, e_g_q, w_qp, e_g_kv, w_kvp)

    grads = dict(
        e_g_in=dg_in,
        e_w_in=_fold_w_in0(_wgrad(dz0, h0, "wgrad_in0")),
        e_g_q_a=dg_q,
        e_w_q_up=_fold_w_q(_wgrad(dqu, cqn, "wgrad_q_up")),
        e_g_kv_a=dg_kv,
        e_w_kv_up=_fold_w_kv(_wgrad(ckvn, dkvu, "wgrad_kv_up")),
        e_sinks=dsink[:, 0:2, 0].reshape(1, N_SWA),
        e_w_out=_wgrad(u0, dx1, "wgrad_out0"),
        o_b_f=db_f.reshape(1, N_FOX),
        g_final=dg_final,
        **grads1,
    )
    return loss8[0, 0], dx, grads


SHARDED = ("e_w_in", "e_w_q_up", "e_w_kv_up", "e_w_out", "o_g_in", "o_w_in", "o_w_out")
TRANSPOSED = ("e_w_in", "e_w_q_up", "o_w_in")
COL_SHARDED = ("e_w_kv_up", "o_g_in")
REPLICATED = ("e_g_in", "e_g_q_a", "e_g_kv_a", "e_sinks", "o_b_f", "g_final")
FULL_SHAPES = dict(e_w_in=(2208, 1024), e_w_q_up=(768, 256), e_w_kv_up=(128, 1024), e_w_out=(1024, 1024),
                   o_g_in=(1, 1024), o_w_in=(4112, 1024), o_w_out=(1024, 1024))
GROUPS = dict(
    layer0=dict(rows=1024, windows=dict(e_w_in=(0, 0), e_w_out=(560, 0), e_w_q_up=(816, 0), e_w_kv_up=(816, 256))),
    layer1=dict(rows=1312, windows=dict(o_w_in=(0, 0), o_w_out=(1040, 0), o_g_in=(1296, 0))),
)


def _shard_shape(name):
    r, c = FULL_SHAPES[name]
    return (r, c // 4) if name in COL_SHARDED else (r // 4, c)


def _as_handled(name, a):
    a = a[0] if a.ndim == 3 else a
    return a.T if name in TRANSPOSED else a


def _as_given(name, a, shape):
    return (a.T if name in TRANSPOSED else a).reshape(shape)


def _pack_block(p, group):
    def rows(a, n):
        return jnp.pad(a, ((0, n - a.shape[0]), (0, 0)))

    if group == "layer0":
        band = jnp.concatenate([p["e_w_q_up"], rows(p["e_w_kv_up"], 192), jnp.zeros((192, 512), p["e_w_in"].dtype)], axis=1)
        return jnp.concatenate([rows(p["e_w_in"], 560), p["e_w_out"], rows(band, 208)], axis=0)
    g = p["o_g_in"]
    return jnp.concatenate([rows(p["o_w_in"], 1040), p["o_w_out"], jnp.pad(g, ((0, 16 - g.shape[0]), (0, PACK_COLS - g.shape[1])))], axis=0)


def _window(block, group, name, width=None):
    r0, c0 = GROUPS[group]["windows"][name]
    r, c = _shard_shape(name)
    return block[..., r0:r0 + r, c0:c0 + (c if width is None else width)]


def _chip_slice(name, full, k):
    r, c = _shard_shape(name)
    return full[:, c * k:c * (k + 1)] if name in COL_SHARDED else full[r * k:r * (k + 1), :]


def _packed_weights(w, group):
    parts = {}
    for n in GROUPS[group]["windows"]:
        a = _as_handled(n, w[n])
        parts[n] = lax.bitcast_convert_type(a, jnp.bfloat16).reshape(1, -1) if n == "o_g_in" else a.astype(jnp.bfloat16)
    halves = _pack_block(parts, group).reshape(2, GROUPS[group]["rows"] // 2, PACK_COLS)
    return lax.dynamic_index_in_dim(halves, lax.axis_index("c"), 0, keepdims=False)


def _unpacked_weights(gathered, half, group):
    blocks = _fill_own_slot(gathered, half).reshape(4, GROUPS[group]["rows"], PACK_COLS)
    full = {}
    for n in GROUPS[group]["windows"]:
        if n == "o_g_in":
            halves = _window(blocks, group, n, width=512).reshape(4, 1, 256, 2)
            full[n] = jnp.concatenate(list(lax.bitcast_convert_type(halves, jnp.float32)), axis=1)
        else:
            pieces = [_window(blocks[k], group, n) for k in range(4)]
            full[n] = jnp.concatenate(pieces, axis=1 if n in COL_SHARDED else 0).astype(MXU)
    return full


class _GroupReduce:
    def __init__(self, group):
        self.group = group
        self.c = lax.axis_index("c")
        self.chip = 2 * lax.axis_index("x") + lax.axis_index("y")

    def swap_plan(self, grads):
        names = GROUPS[self.group]["windows"]
        per_chip = jnp.stack([_pack_block({n: _chip_slice(n, grads[n], k) for n in names}, self.group) for k in range(4)])
        self.g4 = per_chip.reshape(4, 2, GROUPS[self.group]["rows"] // 2, PACK_COLS)
        return _pair_swap_plan(self.g4)

    def exchange_plan(self, rode):
        theirs = rode[0]
        mine = lax.dynamic_index_in_dim(self.g4, self.c, 1, keepdims=False)
        self.own = (lax.dynamic_index_in_dim(mine, self.chip, 0, keepdims=False),
                    lax.dynamic_index_in_dim(theirs, self.chip, 0, keepdims=False))
        return _chip_exchange_plan(_add_blocks(mine, theirs, "pair_add_" + self.group, jnp.bfloat16))

    def finish(self, rode):
        my_half = _total_sum(*self.own, rode[0], "chip_sum_" + self.group)
        other_half = _run_plan(_pair_exchange_plan(my_half), "pair_exchange_" + self.group)[0]
        total = jnp.concatenate([jnp.where(self.c == 0, my_half, other_half), jnp.where(self.c == 0, other_half, my_half)], axis=0)
        self.sums = {n: _window(total, self.group, n) for n in GROUPS[self.group]["windows"]}

    def run(self, grads):
        rode = _run_plan(self.swap_plan(grads), "pair_swap_" + self.group)
        self.finish(_run_plan(self.exchange_plan(rode), "chip_exchange_" + self.group))
        return self.sums


class _Layer1Exchange(_GroupReduce):
    def __init__(self, w):
        super().__init__("layer1")
        self.half = _packed_weights(w, "layer1")

    def gather_plan(self):
        return _gather8_plan(self.half)

    def layer1_weights(self, rode):
        full = _unpacked_weights(rode[0], self.half, "layer1")
        return full["o_g_in"], full["o_w_in"], full["o_w_out"]


def kernel(x, positions, e_g_in, e_w_in, e_g_q_a, e_w_q_up, e_g_kv_a, e_w_kv_up, e_sinks, e_w_out, o_g_in, o_w_in, o_b_f, o_w_out, g_final, loss_target, m_e_g_in, m_e_w_in, m_e_g_q_a, m_e_w_q_up, m_e_g_kv_a, m_e_w_kv_up, m_e_sinks, m_e_w_out, m_o_g_in, m_o_w_in, m_o_b_f, m_o_w_out, m_g_final, v_e_g_in, v_e_w_in, v_e_g_q_a, v_e_w_q_up, v_e_g_kv_a, v_e_w_kv_up, v_e_sinks, v_e_w_out, v_o_g_in, v_o_w_in, v_o_b_f, v_o_w_out, v_g_final):
    w = dict(e_g_in=e_g_in, e_w_in=e_w_in, e_g_q_a=e_g_q_a, e_w_q_up=e_w_q_up, e_g_kv_a=e_g_kv_a, e_w_kv_up=e_w_kv_up,
             e_sinks=e_sinks, e_w_out=e_w_out, o_g_in=o_g_in, o_w_in=o_w_in, o_b_f=o_b_f, o_w_out=o_w_out, g_final=g_final)
    m = dict(e_g_in=m_e_g_in, e_w_in=m_e_w_in, e_g_q_a=m_e_g_q_a, e_w_q_up=m_e_w_q_up, e_g_kv_a=m_e_g_kv_a,
             e_w_kv_up=m_e_w_kv_up, e_sinks=m_e_sinks, e_w_out=m_e_w_out, o_g_in=m_o_g_in, o_w_in=m_o_w_in, o_b_f=m_o_b_f,
             o_w_out=m_o_w_out, g_final=m_g_final)
    v = dict(e_g_in=v_e_g_in, e_w_in=v_e_w_in, e_g_q_a=v_e_g_q_a, e_w_q_up=v_e_w_q_up, e_g_kv_a=v_e_g_kv_a,
             e_w_kv_up=v_e_w_kv_up, e_sinks=v_e_sinks, e_w_out=v_e_w_out, o_g_in=v_o_g_in, o_w_in=v_o_w_in, o_b_f=v_o_b_f,
             o_w_out=v_o_w_out, g_final=v_g_final)
    order = ("e_g_in", "e_w_in", "e_g_q_a", "e_w_q_up", "e_g_kv_a", "e_w_kv_up", "e_sinks", "e_w_out", "o_g_in", "o_w_in",
             "o_b_f", "o_w_out", "g_final")
    half0 = _packed_weights(w, "layer0")
    full = _unpacked_weights(_run_plan(_gather8_plan(half0), "gather_weights_layer0")[0], half0, "layer0")
    layer1 = _Layer1Exchange(w)

    loss_part, dx, grads = _local_step(
        x[0], positions.reshape(-1, 1), loss_target[0], e_g_in, full["e_w_in"], e_g_q_a, full["e_w_q_up"], e_g_kv_a,
        full["e_w_kv_up"], e_sinks, full["e_w_out"], o_b_f, g_final.reshape(1, D), layer1)
    loss = lax.psum(loss_part, ("x", "y", "c"))

    gsum = {**layer1.sums, **_GroupReduce("layer0").run(grads)}

    small = jnp.concatenate([jnp.pad(grads[n].reshape(-1), (0, (-grads[n].size) % LANES)) for n in REPLICATED])
    rows = small.shape[0] // LANES
    small = jnp.pad(small.reshape(rows, LANES), ((0, (-rows) % 8), (0, 0)))
    gathered_small = _fill_own_slot(_run_plan(_gather8_plan(small), "gather_small_grads")[0], small)
    ssum = _sum_leading(gathered_small, "small_grad_sum").reshape(-1)
    off = 0
    for n in REPLICATED:
        cnt = w[n].size
        gsum[n] = ssum[off:off + cnt].reshape(w[n].shape)
        off += cnt + (-cnt) % LANES

    grad, delta, new_m, new_v = {}, {}, {}, {}
    for n in order:
        if n in SHARDED:
            outs = _adamw(_as_handled(n, w[n]), gsum[n], _as_handled(n, m[n]), _as_handled(n, v[n]), "adamw_" + n)
            grad[n], delta[n], new_m[n], new_v[n] = (_as_given(n, a, w[n].shape) for a in (gsum[n],) + outs)
        else:
            grad[n] = gsum[n]
            delta[n], new_m[n], new_v[n] = _adamw(w[n], gsum[n], m[n], v[n], "adamw_" + n)
    return (loss, dx[None], *[grad[n] for n in order], *[delta[n] for n in order], *[new_m[n] for n in order],
            *[new_v[n] for n in order])
```

```python
import functools
import math

import numpy as np
import jax
import jax.numpy as jnp
from jax import lax
from jax.experimental import pallas as pl
from jax.experimental.pallas import tpu as pltpu

D = 1024
EPS = 1e-6
ROPE_THETA = 10000.0
N_MLA = 8
Q_RANK = 256
KV_RANK = 128
NOPE = 64
ROPE = 32
N_SWA = 8
WINDOW = 128
N_FOX = 16
HEAD = 64
E_SPLITS = (256, 128, 32, 512, 128, 128, 1024)
O_SPLITS = (1024, 1024, 1024, 16, 1024)
LR, B1, B2, AEPS, WD, STEP = 0.001, 0.9, 0.999, 1e-08, 0.01, 10

LANES = 128
HALF = 64
VMEM_LIMIT = 56 * 1024 * 1024
MXU = jnp.bfloat16
TOK = 256
WG_TOK = 2048
WG_ROWS = 1536
ATT = 256
FWD_CHUNK = 2
BWD_CHUNK = 2
SWA_GROUP = 4
NEG = float("-inf")

PACK_COLS = 1024
SUM_ROWS = 256
MESH_ID = pl.DeviceIdType.MESH


def _pcall(body, *, name, vmem=VMEM_LIMIT, semantics=None, **kw):
    params = dict(vmem_limit_bytes=vmem)
    if semantics is not None:
        params["dimension_semantics"] = semantics
    return pl.pallas_call(body, name=name, compiler_params=pltpu.CompilerParams(**params), **kw)


def _mm(a, b):
    return jnp.dot(a.astype(MXU), b.astype(MXU), preferred_element_type=jnp.float32)


def _mm_nt(a, b):
    return lax.dot_general(a.astype(MXU), b.astype(MXU), (((1,), (1,)), ((), ())),
                           preferred_element_type=jnp.float32)


def _mm_tn(a, b):
    return lax.dot_general(a.astype(MXU), b.astype(MXU), (((0,), (0,)), ((), ())),
                           preferred_element_type=jnp.float32)


def _full(shape):
    n = len(shape)
    return pl.BlockSpec(shape, lambda *_: (0,) * n)


def _rows(tm, n):
    return pl.BlockSpec((tm, n), lambda i: (i, 0))


def _sds(shape, dtype):
    return jax.ShapeDtypeStruct(shape, dtype)


def _rms(x, g):
    r = lax.rsqrt(jnp.mean(x * x, axis=-1, keepdims=True) + EPS)
    return x * r * g


def _rms_bwd(x, g, dy):
    r = lax.rsqrt(jnp.mean(x * x, axis=-1, keepdims=True) + EPS)
    xh = x * r
    dxh = dy * g
    dx = r * (dxh - xh * jnp.mean(dxh * xh, axis=-1, keepdims=True))
    return dx, dy * xh


def _sigmoid(x):
    return 1.0 / (1.0 + jnp.exp(-x))


def _lane_masks(dtype=None):
    lane = lax.broadcasted_iota(jnp.int32, (1, LANES), 1)
    return lane < HALF


def _split_heads(a, lo):
    z = jnp.zeros_like(a)
    return [jnp.where(lo, a, z), jnp.where(lo, z, a)]


def _rope_consts():
    inv = np.zeros((8, LANES), np.float32)
    j = np.arange(ROPE // 2, dtype=np.float32)
    f = (1.0 / (ROPE_THETA ** (np.arange(0, ROPE, 2, dtype=np.float32) / ROPE))).astype(np.float32)
    inv[0, HALF:HALF + 16] = f
    inv[0, HALF + 16:HALF + 32] = f
    inv[1, HALF:HALF + 16] = -1.0
    inv[1, HALF + 16:HALF + 32] = 1.0
    del j
    return jnp.asarray(inv)


def _rope_tables(pos_f, consts):
    ang = pos_f * consts[0:1, :]
    sign = consts[1:2, :]
    c = jnp.where(sign != 0.0, jnp.cos(ang), 1.0)
    s = jnp.sin(ang) * sign
    return c, s


def _swap_halves(v, sign):
    lo = pltpu.roll(v, LANES - 16, axis=1)
    hi = pltpu.roll(v, 16, axis=1)
    return jnp.where(sign < 0.0, lo, jnp.where(sign > 0.0, hi, 0.0))


def _rope(x, c, s, sign):
    return x * c + _swap_halves(x, sign) * s


def _rope_t(dy, c, s, sign):
    return dy * c + _swap_halves(dy * s, sign)


def _layer0_in(x, pos, g_in, w_in, g_q, w_q, g_kv, w_kv):
    S = x.shape[0]
    consts = _rope_consts()

    def body(x_ref, pos_ref, c_ref, g_ref, w_ref, gq_ref, wq_ref, gkv_ref, wkv_ref,
             h_ref, cq_ref, ckv_ref, cqn_ref, ckvn_ref, qm_ref, km_ref, vm_ref,
             qs_ref, kd_ref, vd_ref, gate_ref, cos_ref, sin_ref):
        h = _rms(x_ref[...], g_ref[...])
        h_ref[...] = h.astype(h_ref.dtype)
        z = _mm_nt(h, w_ref[...])
        cq = z[:, 0:256]
        ckv = z[:, 256:384]
        kpe = z[:, 384:512]
        cq_ref[...] = cq
        ckv_ref[...] = ckv
        qs_ref[...] = z[:, 512:1024].astype(qs_ref.dtype)
        kd_ref[...] = z[:, 1024:1536].astype(kd_ref.dtype)
        vd_ref[...] = z[:, 1536:2048].astype(vd_ref.dtype)
        gate_ref[...] = z[:, 2048:3072]
        cqn = _rms(cq, gq_ref[...])
        ckvn = _rms(ckv, gkv_ref[...])
        cqn_ref[...] = cqn.astype(cqn_ref.dtype)
        ckvn_ref[...] = ckvn.astype(ckvn_ref.dtype)
        q = _mm_nt(cqn, wq_ref[...])
        kv = _mm(ckvn, wkv_ref[...])
        vm_ref[...] = kv[:, 1024:1536].astype(vm_ref.dtype)
        consts_v = c_ref[...]
        sign = consts_v[1:2, :]
        c, s = _rope_tables(pos_ref[...].astype(jnp.float32), consts_v)
        cos_ref[...] = c
        sin_ref[...] = s
        kpe_r = _rope(kpe, c, s, sign)
        for hd in range(N_MLA):
            sl = slice(LANES * hd, LANES * (hd + 1))
            qm_ref[:, sl] = _rope(q[:, sl], c, s, sign).astype(qm_ref.dtype)
            km_ref[:, sl] = (kv[:, sl] + kpe_r).astype(km_ref.dtype)

    outs = [
        ((S, D), MXU), ((S, 256), jnp.float32), ((S, 128), jnp.float32), ((S, 256), MXU), ((S, 128), MXU),
        ((S, 1024), MXU), ((S, 1024), MXU), ((S, 512), MXU), ((S, 512), MXU), ((S, 512), MXU), ((S, 512), MXU),
        ((S, 1024), jnp.float32), ((S, 128), jnp.float32), ((S, 128), jnp.float32),
    ]
    return _pcall(
        body, name="layer0_in", grid=(S // TOK,), semantics=("arbitrary",),
        in_specs=[_rows(TOK, D), _rows(TOK, 1), _full((8, LANES)), _full((1, D)), _full(w_in.shape), _full((1, 256)),
                  _full(w_q.shape), _full((1, 128)), _full(w_kv.shape)],
        out_specs=[_rows(TOK, s[1]) for s, _ in outs],
        out_shape=[_sds(s, d) for s, d in outs],
    )(x, pos, consts, g_in, w_in, g_q, w_q, g_kv, w_kv)


AUG = (HALF, 0)
ONE = (HALF + 8, 8)


def _data_lanes(idx, h):
    return (idx < HALF) if h == 0 else (idx >= HALF)


def _three_terms(x):
    hi = x.astype(MXU).astype(jnp.float32)
    mid = (x - hi).astype(MXU).astype(jnp.float32)
    lo = (x - hi - mid).astype(MXU).astype(jnp.float32)
    return hi, mid, lo


def _q_aug(qblk, lc, h, scale, lane):
    a = AUG[h]
    hi, mid, lo = _three_terms(lc)
    ones = ((lane >= a + 3) & (lane <= a + 5)).astype(jnp.float32)
    aug = jnp.where(lane == a, hi, jnp.where(lane == a + 1, mid, jnp.where(lane == a + 2, lo, ones)))
    return jnp.where(_data_lanes(lane, h), qblk * jnp.asarray(scale, qblk.dtype), aug.astype(qblk.dtype))


def _k_aug(kblk, lc, h, lane):
    a = AUG[h]
    hi, mid, lo = _three_terms(-lc)
    ones = ((lane >= a) & (lane <= a + 2)).astype(jnp.float32)
    aug = jnp.where(lane == a + 3, hi, jnp.where(lane == a + 4, mid, jnp.where(lane == a + 5, lo, ones)))
    return jnp.where(_data_lanes(lane, h), kblk, aug.astype(kblk.dtype))


def _attn_fwd_t(q, k, v, scale, *, split, name, lcc=None, plan=None):
    S = q.shape[0]
    npair = v.shape[1] // LANES
    W = 2 * LANES if split else LANES
    T = ATT
    CH = FWD_CHUNK * T
    assert S % CH == 0
    nq = S // T

    def body(*refs):
        if split:
            q_ref, k_ref, v_ref, o_ref, lse_ref, vt, acc, m_sc = refs
        else:
            q_ref, k_ref, v_ref, lcc_ref, o_ref, lse_ref, kaug, vt, acc, m_sc = refs
        lane = lax.broadcasted_iota(jnp.int32, (1, LANES), 1)
        sub = lax.broadcasted_iota(jnp.int32, (LANES, 1), 0)
        key_minus_qry = lax.broadcasted_iota(jnp.int32, (CH, T), 0) - lax.broadcasted_iota(jnp.int32, (CH, T), 1)

        def prep(i, c):
            r0 = pl.multiple_of(i * T, T)
            vblk = v_ref[pl.ds(r0, T), :].astype(jnp.float32)
            for h in (0, 1):
                vh = jnp.where(_data_lanes(lane, h), vblk, (lane == ONE[h]).astype(jnp.float32))
                vt[h, :, pl.ds(r0, T)] = vh.T.astype(vt.dtype)
                if not split:
                    kaug[h, pl.ds(r0, T), :] = _k_aug(k_ref[pl.ds(r0, T), :], lcc_ref[h, pl.ds(r0, T), :], h, lane)
            return c

        lax.fori_loop(0, nq, prep, 0)

        def queries(qi):
            q0 = pl.multiple_of(qi * T, T)
            qblk = q_ref[pl.ds(q0, T), :]
            if split:
                return (qblk[:, :LANES], qblk[:, LANES:])
            return tuple(_q_aug(qblk, lcc_ref[h, pl.ds(q0, T), :], h, scale, lane) for h in (0, 1))

        def scores(qs, c):
            k0 = pl.multiple_of(c * CH, CH)
            out = []
            for h in (0, 1):
                if split:
                    out.append(_mm_nt(k_ref[pl.ds(k0, CH), LANES * h:LANES * (h + 1)], qs[h]) * scale)
                else:
                    out.append(_mm_nt(kaug[h, pl.ds(k0, CH), :], qs[h]))
            return tuple(out)

        def q_block(qi, carry):
            qs, first_scores = carry[:2], carry[2:]
            q0 = pl.multiple_of(qi * T, T)
            acc[...] = jnp.zeros_like(acc)
            m_sc[...] = jnp.full(m_sc.shape, NEG, jnp.float32)

            def absorb(c, sts, masked):
                k0 = pl.multiple_of(c * CH, CH)
                for h in (0, 1):
                    st = sts[h]
                    if masked:
                        st = jnp.where(key_minus_qry <= q0 - k0, st, NEG)
                    m_old = m_sc[h:h + 1, :]
                    m_new = jnp.maximum(m_old, jnp.max(st, axis=0, keepdims=True))
                    alpha = jnp.exp(m_old - m_new)
                    pt = jnp.exp(st - m_new)
                    acc[h] = alpha * acc[h] + _mm(vt[h, :, pl.ds(k0, CH)], pt)
                    m_sc[h:h + 1, :] = m_new

            last = qi // FWD_CHUNK

            def pipelined(c, sts):
                nxt = scores(qs, c + 1)
                absorb(c, sts, False)
                return nxt

            sts = lax.fori_loop(0, last, pipelined, first_scores)
            qs_next = queries(jnp.minimum(qi + 1, nq - 1))
            nxt = qs_next + scores(qs_next, 0)
            absorb(last, sts, True)
            ot = None
            for h in (0, 1):
                a = acc[h]
                l = a[ONE[h]:ONE[h] + 1, :]
                oh = jnp.where(_data_lanes(sub, h), a * (1.0 / l), 0.0)
                ot = oh if ot is None else ot + oh
                lse_ref[0, h:h + 1, pl.ds(q0, T)] = m_sc[h:h + 1, :] + jnp.log(l)
            o_ref[pl.ds(q0, T), :] = ot.T
            return nxt

        qs0 = queries(0)
        lax.fori_loop(0, nq, q_block, qs0 + scores(qs0, 0))

    wide = pl.BlockSpec((S, W), lambda j: (0, j))
    slab = pl.BlockSpec((S, LANES), lambda j: (0, j))
    rows = pl.BlockSpec((1, 2, S), lambda j: (j, 0, 0))
    in_specs = [wide, wide, slab]
    args = [q, k, v]
    scratch = []
    if not split:
        in_specs.append(pl.BlockSpec((2, S, 1), lambda j: (j, 0, 0)))
        args.append(lcc)
        scratch.append(pltpu.VMEM((2, S, LANES), MXU))
    scratch += [pltpu.VMEM((2, LANES, S), MXU), pltpu.VMEM((2, LANES, T), jnp.float32), pltpu.VMEM((8, T), jnp.float32)]
    (o, lse), rode = _pcall_riding(
        body, plan, args, name=name, grid=(npair,), in_specs=in_specs, out_specs=[slab, rows],
        out_shape=[_sds((S, npair * LANES), jnp.float32), _sds((npair, 2, S), jnp.float32)], scratch_shapes=scratch)
    return o, lse, rode


def _attn_bwd_t(q, k, v, do, o, lse, scale, *, split, name, lcc=None, plan=None):
    S = q.shape[0]
    npair = v.shape[1] // LANES
    W = 2 * LANES if split else LANES
    T = ATT
    CH = BWD_CHUNK * T
    assert S % CH == 0
    nq = S // T

    def body(*refs):
        if split:
            (q_ref, k_ref, v_ref, do_ref, o_ref, lse_ref, dq_ref, dk_ref, dv_ref, dqt, delta, dk_acc, dv_acc) = refs
        else:
            (q_ref, k_ref, v_ref, do_ref, o_ref, lse_ref, lcc_ref, dq_ref, dk_ref, dv_ref, dlc_ref,
             dqt, delta, dk_acc, dv_acc, qaug, csum) = refs
        lane = lax.broadcasted_iota(jnp.int32, (1, LANES), 1)
        sub = lax.broadcasted_iota(jnp.int32, (LANES, 1), 0)
        key_minus_qry = lax.broadcasted_iota(jnp.int32, (T, CH), 0) - lax.broadcasted_iota(jnp.int32, (T, CH), 1)

        def prep(i, c):
            r0 = pl.multiple_of(i * T, T)
            prod_t = (do_ref[pl.ds(r0, T), :].astype(jnp.float32) * o_ref[pl.ds(r0, T), :]).T
            for h in (0, 1):
                delta[h:h + 1, pl.ds(r0, T)] = jnp.sum(jnp.where(_data_lanes(sub, h), prod_t, 0.0), axis=0, keepdims=True)
                dqt[h, :, pl.ds(r0, T)] = jnp.zeros((LANES, T), jnp.float32)
                if not split:
                    qaug[h, pl.ds(r0, T), :] = _q_aug(q_ref[pl.ds(r0, T), :], lcc_ref[h, pl.ds(r0, T), :], h, scale, lane)
            return c

        lax.fori_loop(0, nq, prep, 0)

        def keys(ki):
            k0 = pl.multiple_of(ki * T, T)
            kblk = k_ref[pl.ds(k0, T), :]
            if split:
                return (kblk[:, :LANES], kblk[:, LANES:])
            return tuple(_k_aug(kblk, lcc_ref[h, pl.ds(k0, T), :], h, lane) for h in (0, 1))

        def q_of(c, h):
            q0 = pl.multiple_of(c * CH, CH)
            if split:
                return q_ref[pl.ds(q0, CH), LANES * h:LANES * (h + 1)]
            return qaug[h, pl.ds(q0, CH), :]

        def scores(khs, c):
            out = []
            for h in (0, 1):
                st = _mm_nt(khs[h], q_of(c, h))
                out.append(st * scale if split else st)
            return tuple(out)

        def k_block(ki, carry):
            khs, first_scores = carry[:2], carry[2:]
            k0 = pl.multiple_of(ki * T, T)
            khts = [kh.astype(jnp.float32).T.astype(kh.dtype) for kh in khs]
            vhs = _split_heads(v_ref[pl.ds(k0, T), :], lane < HALF)
            dk_acc[...] = jnp.zeros_like(dk_acc)
            dv_acc[...] = jnp.zeros_like(dv_acc)

            def absorb(c, vals):
                q0 = pl.multiple_of(c * CH, CH)
                dos = _split_heads(do_ref[pl.ds(q0, CH), :], lane < HALF)
                visible = key_minus_qry <= q0 - k0
                for h in (0, 1):
                    dpt = _mm_nt(vhs[h], dos[h])
                    st = jnp.where(visible, vals[h], NEG)
                    pt = jnp.exp(st - lse_ref[0, h:h + 1, pl.ds(q0, CH)])
                    dv_acc[...] += _mm(pt, dos[h])
                    dst = pt * (dpt - delta[h:h + 1, pl.ds(q0, CH)])
                    dk_acc[h] += _mm(dst, q_of(c, h))
                    dqt[h, :, pl.ds(q0, CH)] += _mm(khts[h], dst)

            first = ki // BWD_CHUNK

            def pipelined(c, vals):
                nxt = scores(khs, c + 1)
                absorb(c, vals)
                return nxt

            vals = lax.fori_loop(first, S // CH - 1, pipelined, first_scores)
            kn = jnp.minimum(ki + 1, nq - 1)
            khs_next = keys(kn)
            nxt = khs_next + scores(khs_next, kn // BWD_CHUNK)
            absorb(S // CH - 1, vals)
            if split:
                dk_ref[pl.ds(k0, T), :LANES] = (dk_acc[0] * scale).astype(dk_ref.dtype)
                dk_ref[pl.ds(k0, T), LANES:] = (dk_acc[1] * scale).astype(dk_ref.dtype)
            else:
                dk_ref[pl.ds(k0, T), :] = jnp.where(lane < HALF, dk_acc[0], dk_acc[1]).astype(dk_ref.dtype)
                for h in (0, 1):
                    csum[h:h + 1, pl.ds(k0, T)] = dk_acc[h].T[AUG[h] + 3:AUG[h] + 4, :]
            dv_ref[pl.ds(k0, T), :] = dv_acc[...].astype(dv_ref.dtype)
            return nxt

        khs0 = keys(0)
        lax.fori_loop(0, nq, k_block, khs0 + scores(khs0, 0))

        def finish(i, c):
            r0 = pl.multiple_of(i * T, T)
            if split:
                for h in (0, 1):
                    dq_ref[pl.ds(r0, T), LANES * h:LANES * (h + 1)] = (dqt[h, :, pl.ds(r0, T)].T * scale).astype(dq_ref.dtype)
            else:
                d = jnp.where(sub < HALF, dqt[0, :, pl.ds(r0, T)], dqt[1, :, pl.ds(r0, T)])
                dq_ref[pl.ds(r0, T), :] = (d.T * scale).astype(dq_ref.dtype)
                for h in (0, 1):
                    dlc_ref[0, h:h + 1, pl.ds(r0, T)] = dqt[h, AUG[h]:AUG[h] + 1, pl.ds(r0, T)] - csum[h:h + 1, pl.ds(r0, T)]
            return c

        lax.fori_loop(0, nq, finish, 0)

    wide = pl.BlockSpec((S, W), lambda j: (0, j))
    slab = pl.BlockSpec((S, LANES), lambda j: (0, j))
    rows = pl.BlockSpec((1, 2, S), lambda j: (j, 0, 0))
    in_specs = [wide, wide, slab, slab, slab, rows]
    args = [q, k, v, do, o, lse]
    out_specs = [wide, wide, slab]
    out_shape = [_sds(q.shape, jnp.float32 if split else do.dtype), _sds(k.shape, jnp.float32 if split else do.dtype),
                 _sds(v.shape, do.dtype)]
    scratch = [pltpu.VMEM((2, LANES, S), jnp.float32), pltpu.VMEM((8, S), jnp.float32),
               pltpu.VMEM((2, T, LANES), jnp.float32), pltpu.VMEM((T, LANES), jnp.float32)]
    if not split:
        in_specs.append(pl.BlockSpec((2, S, 1), lambda j: (j, 0, 0)))
        args.append(lcc)
        out_specs.append(rows)
        out_shape.append(_sds((npair, 2, S), jnp.float32))
        scratch += [pltpu.VMEM((2, S, LANES), MXU), pltpu.VMEM((8, S), jnp.float32)]
    outs, rode = _pcall_riding(body, plan, args, name=name, grid=(npair,), in_specs=in_specs, out_specs=out_specs,
                               out_shape=out_shape, scratch_shapes=scratch)
    return (*outs, rode)


def _swa_bias(slope, shift):
    a = lax.broadcasted_iota(jnp.int32, (WINDOW, 2 * WINDOW), 0)
    c = lax.broadcasted_iota(jnp.int32, (WINDOW, 2 * WINDOW), 1)
    dist = a - c + shift
    return jnp.where((dist >= 0) & (dist < WINDOW), -slope * dist.astype(jnp.float32), NEG)


def _swa_scores(qh, kblk, bias):
    return _mm_nt(qh, kblk) * (HEAD ** -0.5) + bias


def _swa_fwd(q, kd, vd, sinks, slopes):
    S = q.shape[0]
    npair = q.shape[1] // LANES
    nb = S // WINDOW

    def body(sink_ref, slope_ref, q_ref, k_ref, v_ref, o_ref, lse_ref):
        j = pl.program_id(0)
        lo = _lane_masks()
        biases = [(_swa_bias(slope_ref[2 * j + h], 0), _swa_bias(slope_ref[2 * j + h], WINDOW)) for h in (0, 1)]

        def q_block(qi, c):
            q0 = pl.multiple_of(qi * WINDOW, WINDOW)
            k0 = pl.multiple_of(jnp.maximum(qi - 1, 0) * WINDOW, WINDOW)
            qs = _split_heads(q_ref[pl.ds(q0, WINDOW), :], lo)
            kblk = k_ref[pl.ds(k0, 2 * WINDOW), :]
            vs = _split_heads(v_ref[pl.ds(k0, 2 * WINDOW), :], lo)
            o = None
            for h in (0, 1):
                sink = sink_ref[2 * j + h]
                s = _swa_scores(qs[h], kblk, jnp.where(qi == 0, *biases[h]))
                m = jnp.maximum(jnp.max(s, axis=1, keepdims=True), sink)
                p = jnp.exp(s - m)
                den = jnp.sum(p, axis=1, keepdims=True) + jnp.exp(sink - m)
                oh = _mm(p / den, vs[h])
                o = oh if o is None else o + oh
                lse_ref[h, pl.ds(q0, WINDOW), :] = m + jnp.log(den)
            o_ref[pl.ds(q0, WINDOW), :] = o
            return c

        def q_group(gi, c):
            for g in range(SWA_GROUP):
                q_block(gi * SWA_GROUP + g, c)
            return c

        lax.fori_loop(0, nb // SWA_GROUP, q_group, 0)

    smem = pl.BlockSpec(memory_space=pltpu.SMEM)
    slab = pl.BlockSpec((S, LANES), lambda j: (0, j))
    return _pcall(
        body, name="swa_fwd", grid=(npair,), semantics=("arbitrary",),
        in_specs=[smem, smem, slab, slab, slab],
        out_specs=[slab, pl.BlockSpec((2, S, 1), lambda j: (j, 0, 0))],
        out_shape=[_sds((S, npair * LANES), jnp.float32), _sds((2 * npair, S, 1), jnp.float32)],
    )(sinks, slopes, q, kd, vd)


def _swa_bwd(q, kd, vd, do, o, lse, sinks, slopes, plan=None):
    S = q.shape[0]
    npair = q.shape[1] // LANES
    nb = S // WINDOW

    def body(sink_ref, slope_ref, q_ref, k_ref, v_ref, do_ref, o_ref, lse_ref,
             dq_ref, dk_ref, dv_ref, dsink_ref, dk_acc, dv_acc):
        j = pl.program_id(0)
        lo = _lane_masks()
        dk_acc[...] = jnp.zeros_like(dk_acc)
        dv_acc[...] = jnp.zeros_like(dv_acc)
        biases = [(_swa_bias(slope_ref[2 * j + h], 0), _swa_bias(slope_ref[2 * j + h], WINDOW)) for h in (0, 1)]

        def q_block(qi, carry):
            q0 = pl.multiple_of(qi * WINDOW, WINDOW)
            k0 = pl.multiple_of(jnp.maximum(qi - 1, 0) * WINDOW, WINDOW)
            qs = _split_heads(q_ref[pl.ds(q0, WINDOW), :], lo)
            dos = _split_heads(do_ref[pl.ds(q0, WINDOW), :], lo)
            oblk = o_ref[pl.ds(q0, WINDOW), :]
            kblk = k_ref[pl.ds(k0, 2 * WINDOW), :]
            vblk = v_ref[pl.ds(k0, 2 * WINDOW), :]
            ks = _split_heads(kblk, lo)
            dq = None
            out = []
            for h in (0, 1):
                sink = sink_ref[2 * j + h]
                lse_h = lse_ref[h, pl.ds(q0, WINDOW), :]
                s = _swa_scores(qs[h], kblk, jnp.where(qi == 0, *biases[h]))
                p = jnp.exp(s - lse_h)
                delta = jnp.sum(dos[h].astype(jnp.float32) * oblk, axis=1, keepdims=True)
                dv_acc[pl.ds(k0, 2 * WINDOW), :] += _mm_tn(p, dos[h])
                dp = _mm_nt(dos[h], vblk)
                ds = p * (dp - delta)
                dqh = _mm(ds, ks[h]) * (HEAD ** -0.5)
                dq = dqh if dq is None else dq + dqh
                dk_acc[pl.ds(k0, 2 * WINDOW), :] += _mm_tn(ds, qs[h]) * (HEAD ** -0.5)
                dsk = jnp.sum(-jnp.exp(sink - lse_h) * delta, axis=0, keepdims=True)
                out.append(carry[h] + dsk)
            dq_ref[pl.ds(q0, WINDOW), :] = dq.astype(dq_ref.dtype)
            return tuple(out)

        def q_group(gi, carry):
            for g in range(SWA_GROUP):
                carry = q_block(gi * SWA_GROUP + g, carry)
            return carry

        zero = jnp.zeros((1, 1), jnp.float32)
        dsa, dsb = lax.fori_loop(0, nb // SWA_GROUP, q_group, (zero, zero))
        dk_ref[...] = dk_acc[...].astype(dk_ref.dtype)
        dv_ref[...] = dv_acc[...].astype(dv_ref.dtype)
        r = lax.broadcasted_iota(jnp.int32, (8, LANES), 0)
        dsink_ref[0] = jnp.where(r == 0, dsa, jnp.where(r == 1, dsb, 0.0))

    smem = pl.BlockSpec(memory_space=pltpu.SMEM)
    slab = pl.BlockSpec((S, LANES), lambda j: (0, j))
    outs, rode = _pcall_riding(
        body, plan, [sinks, slopes, q, kd, vd, do, o, lse], name="swa_bwd", grid=(npair,),
        in_specs=[smem, smem, slab, slab, slab, slab, slab, pl.BlockSpec((2, S, 1), lambda j: (j, 0, 0))],
        out_specs=[slab, slab, slab, pl.BlockSpec((1, 8, LANES), lambda j: (j, 0, 0))],
        out_shape=[_sds(q.shape, do.dtype), _sds(kd.shape, do.dtype), _sds(vd.shape, do.dtype),
                   _sds((npair, 8, LANES), jnp.float32)],
        scratch_shapes=[pltpu.VMEM((S, LANES), jnp.float32), pltpu.VMEM((S, LANES), jnp.float32)])
    return (*outs, rode)


def _log_steps(S):
    k, out = 1, []
    while k < S:
        out.append(k)
        k *= 2
    return out


def _forget_fwd(f_row, b_col):
    S = f_row.shape[1]

    def body(f_ref, b_ref, lc_ref):
        x = f_ref[...] + b_ref[...]
        lc = jnp.minimum(x, 0.0) - jnp.log(1.0 + jnp.exp(-jnp.abs(x)))
        idx = lax.broadcasted_iota(jnp.int32, lc.shape, 1)
        for k in _log_steps(S):
            lc = lc + jnp.where(idx >= k, pltpu.roll(lc, k, axis=1), 0.0)
        lc_ref[...] = lc

    return _pcall(body, name="forget_fwd", out_shape=_sds(f_row.shape, jnp.float32))(f_row, b_col)


def _forget_bwd(dlc_row, f_row, b_col):
    S = f_row.shape[1]

    def body(d_ref, f_ref, b_ref, df_ref, db_ref):
        g = d_ref[...]
        idx = lax.broadcasted_iota(jnp.int32, g.shape, 1)
        for k in _log_steps(S):
            g = g + jnp.where(idx < S - k, pltpu.roll(g, S - k, axis=1), 0.0)
        x = f_ref[...] + b_ref[...]
        df = g * _sigmoid(-x)
        df_ref[...] = df
        db_ref[...] = jnp.sum(df, axis=1, keepdims=True)

    return _pcall(body, name="forget_bwd",
                  out_shape=[_sds(f_row.shape, jnp.float32), _sds((f_row.shape[0], 1), jnp.float32)])(dlc_row, f_row, b_col)


def _layer0_out_layer1_in(x, o_m, o_s, gate, w_out, g1, w_in1):
    S = x.shape[0]

    def body(x_ref, om_ref, os_ref, gate_ref, wo_ref, g_ref, w_ref,
             x1_ref, u_ref, h_ref, q_ref, k_ref, v_ref, g1_ref, f_ref):
        gt = gate_ref[...]
        sg = gt * _sigmoid(gt)
        um = om_ref[...] * sg[:, :512]
        us = os_ref[...] * sg[:, 512:]
        u_ref[:, :512] = um.astype(u_ref.dtype)
        u_ref[:, 512:] = us.astype(u_ref.dtype)
        x1 = x_ref[...] + _mm(um, wo_ref[0:512, :]) + _mm(us, wo_ref[512:1024, :])
        x1_ref[...] = x1
        h = _rms(x1, g_ref[...])
        h_ref[...] = h.astype(h_ref.dtype)
        z = _mm_nt(h, w_ref[...])
        q_ref[...] = z[:, 0:1024].astype(q_ref.dtype)
        k_ref[...] = z[:, 1024:2048].astype(k_ref.dtype)
        v_ref[...] = z[:, 2048:3072].astype(v_ref.dtype)
        g1_ref[...] = z[:, 3072:4096]
        f_ref[...] = z[:, 4096:4224]

    outs = [((S, D), jnp.float32), ((S, D), MXU), ((S, D), MXU), ((S, D), MXU), ((S, D), MXU), ((S, D), MXU),
            ((S, D), jnp.float32), ((S, LANES), jnp.float32)]
    return _pcall(
        body, name="layer0_out_layer1_in", grid=(S // TOK,), semantics=("arbitrary",),
        in_specs=[_rows(TOK, D), _rows(TOK, 512), _rows(TOK, 512), _rows(TOK, D), _full((D, D)), _full((1, D)),
                  _full(w_in1.shape)],
        out_specs=[_rows(TOK, s[1]) for s, _ in outs],
        out_shape=[_sds(s, d) for s, d in outs],
    )(x, o_m, o_s, gate, w_out, g1, w_in1)


def _head(x1, o1, gate1, w_out1, g_f, target):
    S = x1.shape[0]

    def body(x1_ref, o_ref, gate_ref, wo_ref, g_ref, t_ref,
             loss_ref, dgf_ref, dx2_ref, u_ref, do_ref, dgate_ref):
        i = pl.program_id(0)
        gt = gate_ref[...]
        sig = _sigmoid(gt)
        sg = gt * sig
        o = o_ref[...]
        u = o * sg
        u_ref[...] = u.astype(u_ref.dtype)
        x2 = x1_ref[...] + _mm(u, wo_ref[...])
        g = g_ref[...]
        y = _rms(x2, g)
        err = y - t_ref[...]
        part = 0.5 * jnp.sum(jnp.mean(err * err, axis=-1, keepdims=True), axis=0, keepdims=True)
        dy = err * (1.0 / D)
        dx2, dg_rows = _rms_bwd(x2, g, dy)
        dx2_ref[...] = dx2
        du = _mm_nt(dx2, wo_ref[...])
        do_ref[...] = (du * sg).astype(do_ref.dtype)
        dgate_ref[...] = (du * o * (sig * (1.0 + gt * (1.0 - sig)))).astype(dgate_ref.dtype)

        @pl.when(i == 0)
        def _():
            loss_ref[...] = jnp.zeros_like(loss_ref)
            dgf_ref[...] = jnp.zeros_like(dgf_ref)

        loss_ref[...] += jnp.broadcast_to(part, loss_ref.shape)
        dgf_ref[...] += jnp.sum(dg_rows, axis=0, keepdims=True)

    outs = [((S, D), jnp.float32), ((S, D), MXU), ((S, D), MXU), ((S, D), MXU)]
    return _pcall(
        body, name="head", grid=(S // TOK,), semantics=("arbitrary",),
        in_specs=[_rows(TOK, D), _rows(TOK, D), _rows(TOK, D), _full((D, D)), _full((1, D)), _rows(TOK, D)],
        out_specs=[_full((8, LANES)), _full((1, D))] + [_rows(TOK, D) for _ in outs],
        out_shape=[_sds((8, LANES), jnp.float32), _sds((1, D), jnp.float32)] + [_sds(s, d) for s, d in outs],
    )(x1, o1, gate1, w_out1, g_f, target)


def _layer1_in_bwd(dq, dk, dv, dgate1, df, x1, dx2, g1, w_in1, gate0, o_m, o_s, w_out0):
    S = x1.shape[0]

    def body(dq_ref, dk_ref, dv_ref, dg1_ref, df_ref, x1_ref, dx2_ref, g_ref, w_ref, gate_ref, om_ref, os_ref,
             wo_ref, dz_ref, dx1_ref, dgn_ref, dom_ref, dos_ref, dgate_ref):
        i = pl.program_id(0)
        dz_ref[:, 0:1024] = dq_ref[...]
        dz_ref[:, 1024:2048] = dk_ref[...]
        dz_ref[:, 2048:3072] = dv_ref[...]
        dz_ref[:, 3072:4096] = dg1_ref[...]
        dz_ref[:, 4096:4224] = df_ref[...]
        dh = _mm(dz_ref[...], w_ref[...])
        g = g_ref[...]
        dxn, dg_rows = _rms_bwd(x1_ref[...], g, dh)
        dx1 = dx2_ref[...] + dxn
        dx1_ref[...] = dx1
        du = _mm_nt(dx1, wo_ref[...])
        gt = gate_ref[...]
        sig = _sigmoid(gt)
        sg = gt * sig
        dsg = sig * (1.0 + gt * (1.0 - sig))
        dom_ref[...] = (du[:, :512] * sg[:, :512]).astype(dom_ref.dtype)
        dos_ref[...] = (du[:, 512:] * sg[:, 512:]).astype(dos_ref.dtype)
        dgate_ref[:, :512] = (du[:, :512] * om_ref[...] * dsg[:, :512]).astype(dgate_ref.dtype)
        dgate_ref[:, 512:] = (du[:, 512:] * os_ref[...] * dsg[:, 512:]).astype(dgate_ref.dtype)

        @pl.when(i == 0)
        def _():
            dgn_ref[...] = jnp.zeros_like(dgn_ref)

        dgn_ref[...] += jnp.sum(dg_rows, axis=0, keepdims=True)

    return _pcall(
        body, name="layer1_in_bwd", grid=(S // TOK,), semantics=("arbitrary",),
        in_specs=[_rows(TOK, D), _rows(TOK, D), _rows(TOK, D), _rows(TOK, D), _rows(TOK, LANES), _rows(TOK, D),
                  _rows(TOK, D), _full((1, D)), _full(w_in1.shape), _rows(TOK, D), _rows(TOK, 512), _rows(TOK, 512),
                  _full((D, D))],
        out_specs=[_rows(TOK, 4224), _rows(TOK, D), _full((1, D)), _rows(TOK, 512), _rows(TOK, 512), _rows(TOK, D)],
        out_shape=[_sds((S, 4224), MXU), _sds((S, D), jnp.float32), _sds((1, D), jnp.float32),
                   _sds((S, 512), MXU), _sds((S, 512), MXU), _sds((S, D), MXU)],
    )(dq, dk, dv, dgate1, df, x1, dx2, g1, w_in1, gate0, o_m, o_s, w_out0)


def _layer0_in_bwd(dqm, dkm, dvm, dqs, dkd, dvd, dgate0, cos, sin, cq, ckv, x, dx1, g_in, w_in, g_q, w_q, g_kv, w_kv):
    S = x.shape[0]
    consts = _rope_consts()

    def body(dqm_ref, dkm_ref, dvm_ref, dqs_ref, dkd_ref, dvd_ref, dgate_ref, cos_ref, sin_ref, c_ref, cq_ref, ckv_ref,
             x_ref, dx1_ref, g_ref, w_ref, gq_ref, wq_ref, gkv_ref, wkv_ref,
             dx_ref, dz_ref, dqu_ref, dkvu_ref, dgin_ref, dgq_ref, dgkv_ref):
        i = pl.program_id(0)
        lo = _lane_masks()
        sign = c_ref[...][1:2, :]
        c = cos_ref[...]
        s = sin_ref[...]
        dkpe = None
        for hd in range(N_MLA):
            sl = slice(LANES * hd, LANES * (hd + 1))
            dqu_ref[:, sl] = _rope_t(dqm_ref[:, sl], c, s, sign).astype(dqu_ref.dtype)
            dkh = dkm_ref[:, sl]
            dkvu_ref[:, sl] = jnp.where(lo, dkh, 0.0).astype(dkvu_ref.dtype)
            dkpe = dkh if dkpe is None else dkpe + dkh
        dkvu_ref[:, 1024:1536] = dvm_ref[...]
        dkpe = _rope_t(jnp.where(lo, 0.0, dkpe), c, s, sign)
        dcqn = _mm(dqu_ref[...], wq_ref[...])
        dckvn = _mm_nt(dkvu_ref[...], wkv_ref[...])
        gq = gq_ref[...]
        gkv = gkv_ref[...]
        dcq, dgq_rows = _rms_bwd(cq_ref[...], gq, dcqn)
        dckv, dgkv_rows = _rms_bwd(ckv_ref[...], gkv, dckvn)
        dz_ref[:, 0:256] = dcq.astype(dz_ref.dtype)
        dz_ref[:, 256:384] = dckv.astype(dz_ref.dtype)
        dz_ref[:, 384:512] = dkpe.astype(dz_ref.dtype)
        dz_ref[:, 512:1024] = dqs_ref[...]
        dz_ref[:, 1024:1536] = dkd_ref[...]
        dz_ref[:, 1536:2048] = dvd_ref[...]
        dz_ref[:, 2048:3072] = dgate_ref[...]
        dh = _mm(dz_ref[...], w_ref[...])
        g = g_ref[...]
        dxn, dg_rows = _rms_bwd(x_ref[...], g, dh)
        dx_ref[...] = dx1_ref[...] + dxn

        @pl.when(i == 0)
        def _():
            dgin_ref[...] = jnp.zeros_like(dgin_ref)
            dgq_ref[...] = jnp.zeros_like(dgq_ref)
            dgkv_ref[...] = jnp.zeros_like(dgkv_ref)

        dgin_ref[...] += jnp.sum(dg_rows, axis=0, keepdims=True)
        dgq_ref[...] += jnp.sum(dgq_rows, axis=0, keepdims=True)
        dgkv_ref[...] += jnp.sum(dgkv_rows, axis=0, keepdims=True)

    return _pcall(
        body, name="layer0_in_bwd", grid=(S // TOK,), semantics=("arbitrary",),
        in_specs=[_rows(TOK, 1024), _rows(TOK, 1024), _rows(TOK, 512), _rows(TOK, 512), _rows(TOK, 512), _rows(TOK, 512),
                  _rows(TOK, D), _rows(TOK, LANES), _rows(TOK, LANES), _full((8, LANES)), _rows(TOK, 256), _rows(TOK, 128),
                  _rows(TOK, D), _rows(TOK, D), _full((1, D)), _full(w_in.shape), _full((1, 256)), _full(w_q.shape),
                  _full((1, 128)), _full(w_kv.shape)],
        out_specs=[_rows(TOK, D), _rows(TOK, 3072), _rows(TOK, 1024), _rows(TOK, 1536), _full((1, D)), _full((1, 256)),
                   _full((1, 128))],
        out_shape=[_sds((S, D), jnp.float32), _sds((S, 3072), MXU), _sds((S, 1024), MXU), _sds((S, 1536), MXU),
                   _sds((1, D), jnp.float32), _sds((1, 256), jnp.float32), _sds((1, 128), jnp.float32)],
    )(dqm, dkm, dvm, dqs, dkd, dvd, dgate0, cos, sin, consts, cq, ckv, x, dx1, g_in, w_in, g_q, w_q, g_kv, w_kv)


def _wgrad(a, b, name):
    S, M = a.shape
    N = b.shape[1]
    tm = next(t for t in range(WG_ROWS, 0, -LANES) if M % t == 0)
    tn = N if N <= 1024 else 512
    tk = min(WG_TOK, S)

    def body(a_ref, b_ref, o_ref):
        @pl.when(pl.program_id(2) == 0)
        def _():
            o_ref[...] = jnp.zeros_like(o_ref)

        o_ref[...] += _mm_tn(a_ref[...], b_ref[...])

    return _pcall(
        body, name=name, grid=(M // tm, N // tn, S // tk), semantics=("parallel", "parallel", "arbitrary"),
        in_specs=[pl.BlockSpec((tk, tm), lambda m, n, k: (k, m)), pl.BlockSpec((tk, tn), lambda m, n, k: (k, n))],
        out_specs=pl.BlockSpec((tm, tn), lambda m, n, k: (m, n)),
        out_shape=_sds((M, N), jnp.float32),
    )(a, b)


def _adamw(w, g, m, v, name):
    shape = w.shape
    R, C = (int(np.prod(shape[:-1])), shape[-1])
    w2, g2, m2, v2 = (t.reshape(R, C) for t in (w, g, m, v))
    tr = 256 if R % 256 == 0 else R
    tc = 256 if (tr == R and R > 256 and C % 256 == 0) else C

    def body(w_ref, g_ref, m_ref, v_ref, d_ref, nm_ref, nv_ref):
        gg = g_ref[...]
        nm = B1 * m_ref[...] + (1.0 - B1) * gg
        nv = B2 * v_ref[...] + (1.0 - B2) * (gg * gg)
        m_hat = nm / (1.0 - B1 ** STEP)
        v_hat = nv / (1.0 - B2 ** STEP)
        d_ref[...] = -LR * (m_hat / (jnp.sqrt(v_hat) + AEPS) + WD * w_ref[...])
        nm_ref[...] = nm
        nv_ref[...] = nv

    spec = pl.BlockSpec((tr, tc), lambda i, j: (i, j))
    d, nm, nv = _pcall(
        body, name=name, grid=(R // tr, C // tc), semantics=("parallel", "parallel"),
        in_specs=[spec] * 4, out_specs=[spec] * 3, out_shape=[_sds((R, C), jnp.float32)] * 3,
    )(w2, g2, m2, v2)
    return d.reshape(shape), nm.reshape(shape), nv.reshape(shape)


def _sum_leading(a, name):
    n, R, C = a.shape
    tr = SUM_ROWS if R % SUM_ROWS == 0 else R

    def body(a_ref, o_ref):
        acc = a_ref[0]
        for i in range(1, n):
            acc = acc + a_ref[i]
        o_ref[...] = acc

    return _pcall(
        body, name=name, grid=(R // tr,), semantics=("parallel",),
        in_specs=[pl.BlockSpec((n, tr, C), lambda i: (0, i, 0))], out_specs=_rows(tr, C),
        out_shape=_sds((R, C), a.dtype),
    )(a)


def _add_blocks(a, b, name, out_dtype):
    n, R, C = a.shape
    tr = SUM_ROWS if R % SUM_ROWS == 0 else R

    def body(a_ref, b_ref, o_ref):
        o_ref[...] = (a_ref[...] + b_ref[...]).astype(o_ref.dtype)

    spec = pl.BlockSpec((1, tr, C), lambda k, i: (k, i, 0))
    return _pcall(
        body, name=name, grid=(n, R // tr), semantics=("parallel", "parallel"),
        in_specs=[spec, spec], out_specs=spec, out_shape=_sds(a.shape, out_dtype),
    )(a, b)


def _total_sum(mine, theirs, recv, name):
    R, C = mine.shape
    n = recv.shape[0]
    tr = SUM_ROWS if R % SUM_ROWS == 0 else R

    def body(a_ref, b_ref, r_ref, o_ref):
        acc = a_ref[...] + b_ref[...]
        for i in range(n):
            acc = acc + r_ref[i].astype(jnp.float32)
        o_ref[...] = acc

    return _pcall(
        body, name=name, grid=(R // tr,), semantics=("parallel",),
        in_specs=[_rows(tr, C), _rows(tr, C), pl.BlockSpec((n, tr, C), lambda i: (0, i, 0))], out_specs=_rows(tr, C),
        out_shape=_sds((R, C), jnp.float32),
    )(mine, theirs, recv)


def _place():
    return lax.axis_index("x"), lax.axis_index("y"), lax.axis_index("c")


class _Plan:
    def __init__(self, arrays, out_shape, scratch, start, finish, middle=None):
        self.arrays, self.out_shape, self.scratch = list(arrays), list(out_shape), list(scratch)
        self.start, self.finish, self.middle = start, finish, middle


def _gather8_plan(block):
    R, C = block.shape

    def parts(ins, outs, sems):
        (x_ref,), (out_ref,), (send_sems, recv_sems) = ins, outs, sems
        x, y, c = _place()
        me, sibling = (x, y, c), (x, y, 1 - c)
        chips = [(1 - x, y), (x, 1 - y), (1 - x, 1 - y)]

        def copy(k, blk, to, src=None):
            slot = out_ref.at[4 * blk[0] + 2 * blk[1] + blk[2]]
            return pltpu.make_async_remote_copy(
                src_ref=slot if src is None else src, dst_ref=slot,
                send_sem=send_sems.at[k], recv_sem=recv_sems.at[k], device_id=to, device_id_type=MESH_ID)

        def first():
            return [copy(0, me, sibling, src=x_ref)] + [copy(1 + j, me, (*chip, c), src=x_ref) for j, chip in enumerate(chips)]

        def passed():
            return [copy(4 + j, (*chip, c), sibling) for j, chip in enumerate(chips)]

        def arrivals():
            return [copy(1 + j, (*chip, c), me) for j, chip in enumerate(chips)]

        def late():
            return [copy(0, sibling, me)] + [copy(4 + j, (*chip, 1 - c), me) for j, chip in enumerate(chips)]

        return first, passed, arrivals, late

    def start(ins, outs, sems):
        for cp in parts(ins, outs, sems)[0]():
            cp.start()

    def middle(ins, outs, sems):
        _, passed, arrivals, _ = parts(ins, outs, sems)
        for arrived, forward in zip(arrivals(), passed()):
            arrived.wait_recv()
            forward.start()

    def finish(ins, outs, sems):
        first, passed, _, late = parts(ins, outs, sems)
        for cp in late():
            cp.wait_recv()
        for cp in first() + passed():
            cp.wait_send()

    return _Plan([block], [_sds((8, R, C), block.dtype)], [pltpu.SemaphoreType.DMA((7,)), pltpu.SemaphoreType.DMA((7,))],
                 start, finish, middle)


def _fill_own_slot(gathered, block):
    x, y, c = _place()
    return lax.dynamic_update_index_in_dim(gathered, block, 4 * x + 2 * y + c, 0)


def _started_and_waited(arrays, out_shape, n, copies):
    def start(ins, outs, sems):
        for cp in copies(ins, outs, sems):
            cp.start()

    def finish(ins, outs, sems):
        for cp in copies(ins, outs, sems):
            cp.wait()

    return _Plan(arrays, out_shape, [pltpu.SemaphoreType.DMA((n,)), pltpu.SemaphoreType.DMA((n,))], start, finish)


def _pair_swap_plan(g):
    n = g.shape[0]

    def copies(ins, outs, sems):
        (g_ref,), (out_ref,), (send_sems, recv_sems) = ins, outs, sems
        x, y, c = _place()
        return [pltpu.make_async_remote_copy(src_ref=g_ref.at[k, 1 - c], dst_ref=out_ref.at[k], send_sem=send_sems.at[k],
                                             recv_sem=recv_sems.at[k], device_id=(x, y, 1 - c), device_id_type=MESH_ID)
                for k in range(n)]

    return _started_and_waited([g], [_sds((n,) + g.shape[2:], g.dtype)], n, copies)


def _chip_exchange_plan(p):
    def copies(ins, outs, sems):
        (p_ref,), (out_ref,), (send_sems, recv_sems) = ins, outs, sems
        x, y, c = _place()
        chips = [(1 - x, y), (x, 1 - y), (1 - x, 1 - y)]
        return [pltpu.make_async_remote_copy(
            src_ref=p_ref.at[2 * cx + cy], dst_ref=out_ref.at[j], send_sem=send_sems.at[j],
            recv_sem=recv_sems.at[j], device_id=(cx, cy, c), device_id_type=MESH_ID)
            for j, (cx, cy) in enumerate(chips)]

    return _started_and_waited([p], [_sds((3,) + p.shape[1:], p.dtype)], 3, copies)


def _pair_exchange_plan(t):
    def copies(ins, outs, sems):
        (t_ref,), (out_ref,), (send_sems, recv_sems) = ins, outs, sems
        x, y, c = _place()
        return [pltpu.make_async_remote_copy(src_ref=t_ref, dst_ref=out_ref, send_sem=send_sems.at[0], recv_sem=recv_sems.at[0],
                                             device_id=(x, y, 1 - c), device_id_type=MESH_ID)]

    return _started_and_waited([t], [_sds(t.shape, t.dtype)], 1, copies)


ANY_SPEC = pl.BlockSpec(memory_space=pl.ANY)


def _run_plan(plan, name):
    n_in, n_out = len(plan.arrays), len(plan.out_shape)

    def body(*refs):
        ins, outs, sems = refs[:n_in], refs[n_in:n_in + n_out], refs[n_in + n_out:]
        plan.start(ins, outs, sems)
        if plan.middle is not None:
            plan.middle(ins, outs, sems)
        plan.finish(ins, outs, sems)

    return _pcall(body, name=name, in_specs=[ANY_SPEC] * n_in, out_specs=[ANY_SPEC] * n_out, out_shape=plan.out_shape,
                  scratch_shapes=plan.scratch)(*plan.arrays)


def _pcall_riding(body, plan, args, *, name, grid, in_specs, out_specs, out_shape, scratch_shapes):
    if plan is None:
        outs = _pcall(body, name=name, grid=grid, semantics=("arbitrary",), in_specs=in_specs, out_specs=out_specs,
                      out_shape=out_shape, scratch_shapes=scratch_shapes)(*args)
        return list(outs), None
    n_in, n_out, n_s = len(args), len(out_shape), len(scratch_shapes)
    p_in, p_out = len(plan.arrays), len(plan.out_shape)
    steps = grid[0]

    def riding(*refs):
        ins, pins = refs[:n_in], refs[n_in:n_in + p_in]
        o0 = n_in + p_in
        outs, pouts = refs[o0:o0 + n_out], refs[o0 + n_out:o0 + n_out + p_out]
        s0 = o0 + n_out + p_out
        scr, sems = refs[s0:s0 + n_s], refs[s0 + n_s:]
        j = pl.program_id(0)

        @pl.when(j == 0)
        def _():
            plan.start(pins, pouts, sems)

        if plan.middle is not None:
            @pl.when(j == steps // 2)
            def _():
                plan.middle(pins, pouts, sems)

        body(*ins, *outs, *scr)

        @pl.when(j == steps - 1)
        def _():
            plan.finish(pins, pouts, sems)

    res = _pcall(riding, name=name, grid=grid, semantics=("arbitrary",), in_specs=list(in_specs) + [ANY_SPEC] * p_in,
                 out_specs=list(out_specs) + [ANY_SPEC] * p_out, out_shape=list(out_shape) + plan.out_shape,
                 scratch_shapes=list(scratch_shapes) + plan.scratch)(*args, *plan.arrays)
    return list(res[:n_out]), list(res[n_out:])


def _prep_w_in0(wt):
    z32 = jnp.zeros((32, wt.shape[1]), wt.dtype)
    z64 = jnp.zeros((64, wt.shape[1]), wt.dtype)
    k0, k1 = wt[928:992], wt[992:1056]
    v0, v1 = wt[1056:1120], wt[1120:1184]
    return jnp.concatenate([wt[0:384], z64, wt[384:416], z32, wt[416:928],
                            k0, k0, k0, k0, k1, k1, k1, k1, v0, v0, v0, v0, v1, v1, v1, v1, wt[1184:2208]], axis=0)


def _fold_w_in0(d):
    def fold(blk):
        b = blk.reshape(8, 64, blk.shape[1])
        return jnp.concatenate([b[0] + b[1] + b[2] + b[3], b[4] + b[5] + b[6] + b[7]], axis=0)
    return jnp.concatenate([d[0:384], d[448:480], d[512:1024], fold(d[1024:1536]), fold(d[1536:2048]), d[2048:3072]], axis=0)


def _prep_w_q(wt):
    return jnp.pad(wt.reshape(N_MLA, 96, Q_RANK), ((0, 0), (0, 32), (0, 0))).reshape(1024, Q_RANK)


def _fold_w_q(d):
    return d.reshape(N_MLA, 128, Q_RANK)[:, :96].reshape(768, Q_RANK)


def _prep_w_kv(w):
    w3 = w.reshape(KV_RANK, N_MLA, 128)
    kk = jnp.pad(w3[:, :, :64], ((0, 0), (0, 0), (0, 64))).reshape(KV_RANK, 1024)
    return jnp.concatenate([kk, w3[:, :, 64:].reshape(KV_RANK, 512)], axis=1)


def _fold_w_kv(d):
    kk = d[:, :1024].reshape(KV_RANK, N_MLA, 128)[:, :, :64]
    vv = d[:, 1024:].reshape(KV_RANK, N_MLA, 64)
    return jnp.concatenate([kk, vv], axis=2).reshape(KV_RANK, 1024)


def _prep_w_in1(wt):
    return jnp.concatenate([wt[0:3072], wt[3088:4112], wt[3072:3088], jnp.zeros((112, wt.shape[1]), wt.dtype)], axis=0)


def _fold_w_in1(d):
    return jnp.concatenate([d[0:3072], d[4096:4112], d[3072:4096]], axis=0)


class _Alone:
    def __init__(self, w_out0, o_g_in, w_in1, w_out1):
        self.layer1 = (w_out0, o_g_in, w_in1, w_out1)

    def gather_plan(self):
        return None

    def layer1_weights(self, rode):
        return self.layer1

    def swap_plan(self, grads1):
        return None

    def exchange_plan(self, rode):
        return None

    def finish(self, rode):
        pass


def _local_step(x, pos, target, e_g_in, w_in0, e_g_q, w_q, e_g_kv, w_kv, sinks, b_f, g_final, layer1):
    S = x.shape[0]
    w_in0p, w_qp, w_kvp = _prep_w_in0(w_in0), _prep_w_q(w_q), _prep_w_kv(w_kv)
    slopes = jnp.asarray(2.0 ** (-8.0 * (np.arange(N_SWA, dtype=np.float32) + 1.0) / N_SWA), jnp.float32)
    sinks1 = sinks.reshape(N_SWA)
    b_col = b_f.reshape(N_FOX, 1)

    (h0, cq, ckv, cqn, ckvn, qm, km, vm, qs, kd, vd, gate0, cos, sin) = _layer0_in(
        x, pos, e_g_in, w_in0p, e_g_q, w_qp, e_g_kv, w_kvp)
    o_m, lse_m, rode = _attn_fwd_t(qm, km, vm, (NOPE + ROPE) ** -0.5, split=True, name="mla_fwd", plan=layer1.gather_plan())
    w_out0, o_g_in, w_in1, w_out1 = layer1.layer1_weights(rode)
    w_in1p = _prep_w_in1(w_in1)
    o_s, lse_s = _swa_fwd(qs, kd, vd, sinks1, slopes)
    x1, u0, h1, q1, k1, v1, gate1, f_slab = _layer0_out_layer1_in(x, o_m, o_s, gate0, w_out0, o_g_in, w_in1p)
    f_row = f_slab[:, :N_FOX].T
    lc_row = _forget_fwd(f_row, b_col)
    lcc = lc_row.reshape(N_FOX, S, 1)
    o1, lse1, _ = _attn_fwd_t(q1, k1, v1, HEAD ** -0.5, split=False, name="fox_fwd", lcc=lcc)
    loss8, dg_final, dx2, u1, do1, dgate1 = _head(x1, o1, gate1, w_out1, g_final, target)

    dq1, dk1, dv1, dlc, _ = _attn_bwd_t(q1, k1, v1, do1, o1, lse1, HEAD ** -0.5, split=False, name="fox_bwd", lcc=lcc)
    df_row, db_f = _forget_bwd(dlc.reshape(N_FOX, S), f_row, b_col)
    df_slab = jnp.pad(df_row.T, ((0, 0), (0, LANES - N_FOX))).astype(MXU)
    dz1, dx1, dg_o_in, do_m, do_s, dgate0 = _layer1_in_bwd(
        dq1, dk1, dv1, dgate1, df_slab, x1, dx2, o_g_in, w_in1p, gate0, o_m, o_s, w_out0)
    grads1 = dict(o_g_in=dg_o_in, o_w_in=_fold_w_in1(_wgrad(dz1, h1, "wgrad_in1")), o_w_out=_wgrad(u1, dx2, "wgrad_out1"),
                  e_w_out=_wgrad(u0, dx1, "wgrad_out0"))
    dqs, dkd, dvd, dsink, rode = _swa_bwd(qs, kd, vd, do_s, o_s, lse_s, sinks1, slopes, plan=layer1.swap_plan(grads1))
    dqm, dkm, dvm, rode = _attn_bwd_t(qm, km, vm, do_m, o_m, lse_m, (NOPE + ROPE) ** -0.5, split=True, name="mla_bwd",
                                      plan=layer1.exchange_plan(rode))
    layer1.finish(rode)
    dx, dz0, dqu, dkvu, dg_in, dg_q, dg_kv = _layer0_in_bwd(
        dqm, dkm, dvm, dqs, dkd, dvd, dgate0, cos, sin, cq, ckv, x, dx1, e_g_in, w_in0p, e_g_q, w_qp, e_g_kv, w_kvp)

    grads = dict(
        e_g_in=dg_in,
        e_w_in=_fold_w_in0(_wgrad(dz0, h0, "wgrad_in0")),
        e_g_q_a=dg_q,
        e_w_q_up=_fold_w_q(_wgrad(dqu, cqn, "wgrad_q_up")),
        e_g_kv_a=dg_kv,
        e_w_kv_up=_fold_w_kv(_wgrad(ckvn, dkvu, "wgrad_kv_up")),
        e_sinks=dsink[:, 0:2, 0].reshape(1, N_SWA),
        o_b_f=db_f.reshape(1, N_FOX),
        g_final=dg_final,
        **grads1,
    )
    return loss8[0, 0], dx, grads


SHARDED = ("e_w_in", "e_w_q_up", "e_w_kv_up", "e_w_out", "o_g_in", "o_w_in", "o_w_out")
TRANSPOSED = ("e_w_in", "e_w_q_up", "o_w_in")
COL_SHARDED = ("e_w_kv_up", "o_g_in")
REPLICATED = ("e_g_in", "e_g_q_a", "e_g_kv_a", "e_sinks", "o_b_f", "g_final")
FULL_SHAPES = dict(e_w_in=(2208, 1024), e_w_q_up=(768, 256), e_w_kv_up=(128, 1024), e_w_out=(1024, 1024),
                   o_g_in=(1, 1024), o_w_in=(4112, 1024), o_w_out=(1024, 1024))
GROUPS = dict(
    layer0=dict(rows=768, windows=dict(e_w_in=(0, 0), e_w_q_up=(560, 0), e_w_kv_up=(560, 256))),
    layer1=dict(rows=1568, windows=dict(o_w_in=(0, 0), o_w_out=(1040, 0), e_w_out=(1296, 0), o_g_in=(1552, 0))),
)


def _shard_shape(name):
    r, c = FULL_SHAPES[name]
    return (r, c // 4) if name in COL_SHARDED else (r // 4, c)


def _as_handled(name, a):
    a = a[0] if a.ndim == 3 else a
    return a.T if name in TRANSPOSED else a


def _as_given(name, a, shape):
    return (a.T if name in TRANSPOSED else a).reshape(shape)


def _pack_block(p, group):
    def rows(a, n):
        return jnp.pad(a, ((0, n - a.shape[0]), (0, 0)))

    if group == "layer0":
        band = jnp.concatenate([p["e_w_q_up"], rows(p["e_w_kv_up"], 192), jnp.zeros((192, 512), p["e_w_in"].dtype)], axis=1)
        return jnp.concatenate([rows(p["e_w_in"], 560), rows(band, 208)], axis=0)
    g = p["o_g_in"]
    band = jnp.pad(g, ((0, 16 - g.shape[0]), (0, PACK_COLS - g.shape[1])))
    return jnp.concatenate([rows(p["o_w_in"], 1040), p["o_w_out"], p["e_w_out"], band], axis=0)


def _window(block, group, name, width=None):
    r0, c0 = GROUPS[group]["windows"][name]
    r, c = _shard_shape(name)
    return block[..., r0:r0 + r, c0:c0 + (c if width is None else width)]


def _chip_slice(name, full, k):
    r, c = _shard_shape(name)
    return full[:, c * k:c * (k + 1)] if name in COL_SHARDED else full[r * k:r * (k + 1), :]


def _packed_weights(w, group):
    parts = {}
    for n in GROUPS[group]["windows"]:
        a = _as_handled(n, w[n])
        parts[n] = lax.bitcast_convert_type(a, jnp.bfloat16).reshape(1, -1) if n == "o_g_in" else a.astype(jnp.bfloat16)
    halves = _pack_block(parts, group).reshape(2, GROUPS[group]["rows"] // 2, PACK_COLS)
    return lax.dynamic_index_in_dim(halves, lax.axis_index("c"), 0, keepdims=False)


def _unpacked_weights(gathered, half, group):
    blocks = _fill_own_slot(gathered, half).reshape(4, GROUPS[group]["rows"], PACK_COLS)
    full = {}
    for n in GROUPS[group]["windows"]:
        if n == "o_g_in":
            halves = _window(blocks, group, n, width=512).reshape(4, 1, 256, 2)
            full[n] = jnp.concatenate(list(lax.bitcast_convert_type(halves, jnp.float32)), axis=1)
        else:
            pieces = [_window(blocks[k], group, n) for k in range(4)]
            full[n] = jnp.concatenate(pieces, axis=1 if n in COL_SHARDED else 0).astype(MXU)
    return full


class _GroupReduce:
    def __init__(self, group):
        self.group = group
        self.c = lax.axis_index("c")
        self.chip = 2 * lax.axis_index("x") + lax.axis_index("y")

    def swap_plan(self, grads):
        names = GROUPS[self.group]["windows"]
        per_chip = jnp.stack([_pack_block({n: _chip_slice(n, grads[n], k) for n in names}, self.group) for k in range(4)])
        self.g4 = per_chip.reshape(4, 2, GROUPS[self.group]["rows"] // 2, PACK_COLS)
        return _pair_swap_plan(self.g4)

    def exchange_plan(self, rode):
        theirs = rode[0]
        mine = lax.dynamic_index_in_dim(self.g4, self.c, 1, keepdims=False)
        self.own = (lax.dynamic_index_in_dim(mine, self.chip, 0, keepdims=False),
                    lax.dynamic_index_in_dim(theirs, self.chip, 0, keepdims=False))
        return _chip_exchange_plan(_add_blocks(mine, theirs, "pair_add_" + self.group, jnp.bfloat16))

    def finish(self, rode):
        my_half = _total_sum(*self.own, rode[0], "chip_sum_" + self.group)
        other_half = _run_plan(_pair_exchange_plan(my_half), "pair_exchange_" + self.group)[0]
        total = jnp.concatenate([jnp.where(self.c == 0, my_half, other_half), jnp.where(self.c == 0, other_half, my_half)], axis=0)
        self.sums = {n: _window(total, self.group, n) for n in GROUPS[self.group]["windows"]}

    def run(self, grads):
        rode = _run_plan(self.swap_plan(grads), "pair_swap_" + self.group)
        self.finish(_run_plan(self.exchange_plan(rode), "chip_exchange_" + self.group))
        return self.sums


class _Layer1Exchange(_GroupReduce):
    def __init__(self, w):
        super().__init__("layer1")
        self.half = _packed_weights(w, "layer1")

    def gather_plan(self):
        return _gather8_plan(self.half)

    def layer1_weights(self, rode):
        full = _unpacked_weights(rode[0], self.half, "layer1")
        return full["e_w_out"], full["o_g_in"], full["o_w_in"], full["o_w_out"]


def kernel(x, positions, e_g_in, e_w_in, e_g_q_a, e_w_q_up, e_g_kv_a, e_w_kv_up, e_sinks, e_w_out, o_g_in, o_w_in, o_b_f, o_w_out, g_final, loss_target, m_e_g_in, m_e_w_in, m_e_g_q_a, m_e_w_q_up, m_e_g_kv_a, m_e_w_kv_up, m_e_sinks, m_e_w_out, m_o_g_in, m_o_w_in, m_o_b_f, m_o_w_out, m_g_final, v_e_g_in, v_e_w_in, v_e_g_q_a, v_e_w_q_up, v_e_g_kv_a, v_e_w_kv_up, v_e_sinks, v_e_w_out, v_o_g_in, v_o_w_in, v_o_b_f, v_o_w_out, v_g_final):
    w = dict(e_g_in=e_g_in, e_w_in=e_w_in, e_g_q_a=e_g_q_a, e_w_q_up=e_w_q_up, e_g_kv_a=e_g_kv_a, e_w_kv_up=e_w_kv_up,
             e_sinks=e_sinks, e_w_out=e_w_out, o_g_in=o_g_in, o_w_in=o_w_in, o_b_f=o_b_f, o_w_out=o_w_out, g_final=g_final)
    m = dict(e_g_in=m_e_g_in, e_w_in=m_e_w_in, e_g_q_a=m_e_g_q_a, e_w_q_up=m_e_w_q_up, e_g_kv_a=m_e_g_kv_a,
             e_w_kv_up=m_e_w_kv_up, e_sinks=m_e_sinks, e_w_out=m_e_w_out, o_g_in=m_o_g_in, o_w_in=m_o_w_in, o_b_f=m_o_b_f,
             o_w_out=m_o_w_out, g_final=m_g_final)
    v = dict(e_g_in=v_e_g_in, e_w_in=v_e_w_in, e_g_q_a=v_e_g_q_a, e_w_q_up=v_e_w_q_up, e_g_kv_a=v_e_g_kv_a,
             e_w_kv_up=v_e_w_kv_up, e_sinks=v_e_sinks, e_w_out=v_e_w_out, o_g_in=v_o_g_in, o_w_in=v_o_w_in, o_b_f=v_o_b_f,
             o_w_out=v_o_w_out, g_final=v_g_final)
    order = ("e_g_in", "e_w_in", "e_g_q_a", "e_w_q_up", "e_g_kv_a", "e_w_kv_up", "e_sinks", "e_w_out", "o_g_in", "o_w_in",
             "o_b_f", "o_w_out", "g_final")
    half0 = _packed_weights(w, "layer0")
    full = _unpacked_weights(_run_plan(_gather8_plan(half0), "gather_weights_layer0")[0], half0, "layer0")
    layer1 = _Layer1Exchange(w)

    loss_part, dx, grads = _local_step(
        x[0], positions.reshape(-1, 1), loss_target[0], e_g_in, full["e_w_in"], e_g_q_a, full["e_w_q_up"], e_g_kv_a,
        full["e_w_kv_up"], e_sinks, o_b_f, g_final.reshape(1, D), layer1)
    loss = lax.psum(loss_part, ("x", "y", "c"))

    gsum = {**layer1.sums, **_GroupReduce("layer0").run(grads)}

    small = jnp.concatenate([jnp.pad(grads[n].reshape(-1), (0, (-grads[n].size) % LANES)) for n in REPLICATED])
    rows = small.shape[0] // LANES
    small = jnp.pad(small.reshape(rows, LANES), ((0, (-rows) % 8), (0, 0)))
    gathered_small = _fill_own_slot(_run_plan(_gather8_plan(small), "gather_small_grads")[0], small)
    ssum = _sum_leading(gathered_small, "small_grad_sum").reshape(-1)
    off = 0
    for n in REPLICATED:
        cnt = w[n].size
        gsum[n] = ssum[off:off + cnt].reshape(w[n].shape)
        off += cnt + (-cnt) % LANES

    grad, delta, new_m, new_v = {}, {}, {}, {}
    for n in order:
        if n in SHARDED:
            outs = _adamw(_as_handled(n, w[n]), gsum[n], _as_handled(n, m[n]), _as_handled(n, v[n]), "adamw_" + n)
            grad[n], delta[n], new_m[n], new_v[n] = (_as_given(n, a, w[n].shape) for a in (gsum[n],) + outs)
        else:
            grad[n] = gsum[n]
            delta[n], new_m[n], new_v[n] = _adamw(w[n], gsum[n], m[n], v[n], "adamw_" + n)
    return (loss, dx[None], *[grad[n] for n in order], *[delta[n] for n in order], *[new_m[n] for n in order],
            *[new_v[n] for n in order])
```

```python
import functools
import math

import numpy as np
import jax
import jax.numpy as jnp
from jax import lax
from jax.experimental import pallas as pl
from jax.experimental.pallas import tpu as pltpu

D = 1024
EPS = 1e-6
ROPE_THETA = 10000.0
N_MLA = 8
Q_RANK = 256
KV_RANK = 128
NOPE = 64
ROPE = 32
N_SWA = 8
WINDOW = 128
N_FOX = 16
HEAD = 64
E_SPLITS = (256, 128, 32, 512, 128, 128, 1024)
O_SPLITS = (1024, 1024, 1024, 16, 1024)
LR, B1, B2, AEPS, WD, STEP = 0.001, 0.9, 0.999, 1e-08, 0.01, 10

LANES = 128
HALF = 64
VMEM_LIMIT = 56 * 1024 * 1024
MXU = jnp.bfloat16
TOK = 256
WG_TOK = 2048
WG_ROWS = 1536
ATT = 256
FWD_CHUNK = 2
BWD_CHUNK = 2
SWA_GROUP = 4
NEG = float("-inf")

PACK_COLS = 1024
SUM_ROWS = 256
ADAM_TILE_BYTES = 2 << 20
MESH_ID = pl.DeviceIdType.MESH


def _pcall(body, *, name, vmem=VMEM_LIMIT, semantics=None, **kw):
    params = dict(vmem_limit_bytes=vmem)
    if semantics is not None:
        params["dimension_semantics"] = semantics
    return pl.pallas_call(body, name=name, compiler_params=pltpu.CompilerParams(**params), **kw)


def _mm(a, b):
    return jnp.dot(a.astype(MXU), b.astype(MXU), preferred_element_type=jnp.float32)


def _mm_nt(a, b):
    return lax.dot_general(a.astype(MXU), b.astype(MXU), (((1,), (1,)), ((), ())),
                           preferred_element_type=jnp.float32)


def _mm_tn(a, b):
    return lax.dot_general(a.astype(MXU), b.astype(MXU), (((0,), (0,)), ((), ())),
                           preferred_element_type=jnp.float32)


def _full(shape):
    n = len(shape)
    return pl.BlockSpec(shape, lambda *_: (0,) * n)


def _rows(tm, n):
    return pl.BlockSpec((tm, n), lambda i: (i, 0))


def _sds(shape, dtype):
    return jax.ShapeDtypeStruct(shape, dtype)


def _rms(x, g):
    r = lax.rsqrt(jnp.mean(x * x, axis=-1, keepdims=True) + EPS)
    return x * r * g


def _rms_bwd(x, g, dy):
    r = lax.rsqrt(jnp.mean(x * x, axis=-1, keepdims=True) + EPS)
    xh = x * r
    dxh = dy * g
    dx = r * (dxh - xh * jnp.mean(dxh * xh, axis=-1, keepdims=True))
    return dx, dy * xh


def _sigmoid(x):
    return 1.0 / (1.0 + jnp.exp(-x))


def _lane_masks(dtype=None):
    lane = lax.broadcasted_iota(jnp.int32, (1, LANES), 1)
    return lane < HALF


def _split_heads(a, lo):
    z = jnp.zeros_like(a)
    return [jnp.where(lo, a, z), jnp.where(lo, z, a)]


def _rope_consts():
    inv = np.zeros((8, LANES), np.float32)
    j = np.arange(ROPE // 2, dtype=np.float32)
    f = (1.0 / (ROPE_THETA ** (np.arange(0, ROPE, 2, dtype=np.float32) / ROPE))).astype(np.float32)
    inv[0, HALF:HALF + 16] = f
    inv[0, HALF + 16:HALF + 32] = f
    inv[1, HALF:HALF + 16] = -1.0
    inv[1, HALF + 16:HALF + 32] = 1.0
    del j
    return jnp.asarray(inv)


def _rope_tables(pos_f, consts):
    ang = pos_f * consts[0:1, :]
    sign = consts[1:2, :]
    c = jnp.where(sign != 0.0, jnp.cos(ang), 1.0)
    s = jnp.sin(ang) * sign
    return c, s


def _swap_halves(v, sign):
    lo = pltpu.roll(v, LANES - 16, axis=1)
    hi = pltpu.roll(v, 16, axis=1)
    return jnp.where(sign < 0.0, lo, jnp.where(sign > 0.0, hi, 0.0))


def _rope(x, c, s, sign):
    return x * c + _swap_halves(x, sign) * s


def _rope_t(dy, c, s, sign):
    return dy * c + _swap_halves(dy * s, sign)


def _layer0_in(x, pos, g_in, w_in, g_q, w_q, g_kv, w_kv):
    S = x.shape[0]
    consts = _rope_consts()

    def body(x_ref, pos_ref, c_ref, g_ref, w_ref, gq_ref, wq_ref, gkv_ref, wkv_ref,
             h_ref, cq_ref, ckv_ref, cqn_ref, ckvn_ref, qm_ref, km_ref, vm_ref,
             qs_ref, kd_ref, vd_ref, gate_ref, cos_ref, sin_ref):
        h = _rms(x_ref[...], g_ref[...])
        h_ref[...] = h.astype(h_ref.dtype)
        z = _mm_nt(h, w_ref[...])
        cq = z[:, 0:256]
        ckv = z[:, 256:384]
        kpe = z[:, 384:512]
        cq_ref[...] = cq
        ckv_ref[...] = ckv
        qs_ref[...] = z[:, 512:1024].astype(qs_ref.dtype)
        kd_ref[...] = z[:, 1024:1536].astype(kd_ref.dtype)
        vd_ref[...] = z[:, 1536:2048].astype(vd_ref.dtype)
        gate_ref[...] = z[:, 2048:3072]
        cqn = _rms(cq, gq_ref[...])
        ckvn = _rms(ckv, gkv_ref[...])
        cqn_ref[...] = cqn.astype(cqn_ref.dtype)
        ckvn_ref[...] = ckvn.astype(ckvn_ref.dtype)
        q = _mm_nt(cqn, wq_ref[...])
        kv = _mm(ckvn, wkv_ref[...])
        vm_ref[...] = kv[:, 1024:1536].astype(vm_ref.dtype)
        consts_v = c_ref[...]
        sign = consts_v[1:2, :]
        c, s = _rope_tables(pos_ref[...].astype(jnp.float32), consts_v)
        cos_ref[...] = c
        sin_ref[...] = s
        kpe_r = _rope(kpe, c, s, sign)
        for hd in range(N_MLA):
            sl = slice(LANES * hd, LANES * (hd + 1))
            qm_ref[:, sl] = _rope(q[:, sl], c, s, sign).astype(qm_ref.dtype)
            km_ref[:, sl] = (kv[:, sl] + kpe_r).astype(km_ref.dtype)

    outs = [
        ((S, D), MXU), ((S, 256), jnp.float32), ((S, 128), jnp.float32), ((S, 256), MXU), ((S, 128), MXU),
        ((S, 1024), MXU), ((S, 1024), MXU), ((S, 512), MXU), ((S, 512), MXU), ((S, 512), MXU), ((S, 512), MXU),
        ((S, 1024), jnp.float32), ((S, 128), jnp.float32), ((S, 128), jnp.float32),
    ]
    return _pcall(
        body, name="layer0_in", grid=(S // TOK,), semantics=("arbitrary",),
        in_specs=[_rows(TOK, D), _rows(TOK, 1), _full((8, LANES)), _full((1, D)), _full(w_in.shape), _full((1, 256)),
                  _full(w_q.shape), _full((1, 128)), _full(w_kv.shape)],
        out_specs=[_rows(TOK, s[1]) for s, _ in outs],
        out_shape=[_sds(s, d) for s, d in outs],
    )(x, pos, consts, g_in, w_in, g_q, w_q, g_kv, w_kv)


AUG = (HALF, 0)
ONE = (HALF + 8, 8)


def _data_lanes(idx, h):
    return (idx < HALF) if h == 0 else (idx >= HALF)


def _three_terms(x):
    hi = x.astype(MXU).astype(jnp.float32)
    mid = (x - hi).astype(MXU).astype(jnp.float32)
    lo = (x - hi - mid).astype(MXU).astype(jnp.float32)
    return hi, mid, lo


def _q_aug(qblk, lc, h, scale, lane):
    a = AUG[h]
    hi, mid, lo = _three_terms(lc)
    ones = ((lane >= a + 3) & (lane <= a + 5)).astype(jnp.float32)
    aug = jnp.where(lane == a, hi, jnp.where(lane == a + 1, mid, jnp.where(lane == a + 2, lo, ones)))
    return jnp.where(_data_lanes(lane, h), qblk * jnp.asarray(scale, qblk.dtype), aug.astype(qblk.dtype))


def _k_aug(kblk, lc, h, lane):
    a = AUG[h]
    hi, mid, lo = _three_terms(-lc)
    ones = ((lane >= a) & (lane <= a + 2)).astype(jnp.float32)
    aug = jnp.where(lane == a + 3, hi, jnp.where(lane == a + 4, mid, jnp.where(lane == a + 5, lo, ones)))
    return jnp.where(_data_lanes(lane, h), kblk, aug.astype(kblk.dtype))


def _attn_fwd_t(q, k, v, scale, *, split, name, lcc=None, plan=None):
    S = q.shape[0]
    npair = v.shape[1] // LANES
    W = 2 * LANES if split else LANES
    T = ATT
    CH = FWD_CHUNK * T
    assert S % CH == 0
    nq = S // T

    def body(*refs):
        if split:
            q_ref, k_ref, v_ref, o_ref, lse_ref, vt, acc, m_sc = refs
        else:
            q_ref, k_ref, v_ref, lcc_ref, o_ref, lse_ref, kaug, vt, acc, m_sc = refs
        lane = lax.broadcasted_iota(jnp.int32, (1, LANES), 1)
        sub = lax.broadcasted_iota(jnp.int32, (LANES, 1), 0)
        key_minus_qry = lax.broadcasted_iota(jnp.int32, (CH, T), 0) - lax.broadcasted_iota(jnp.int32, (CH, T), 1)

        def prep(i, c):
            r0 = pl.multiple_of(i * T, T)
            vblk = v_ref[pl.ds(r0, T), :].astype(jnp.float32)
            for h in (0, 1):
                vh = jnp.where(_data_lanes(lane, h), vblk, (lane == ONE[h]).astype(jnp.float32))
                vt[h, :, pl.ds(r0, T)] = vh.T.astype(vt.dtype)
                if not split:
                    kaug[h, pl.ds(r0, T), :] = _k_aug(k_ref[pl.ds(r0, T), :], lcc_ref[h, pl.ds(r0, T), :], h, lane)
            return c

        lax.fori_loop(0, nq, prep, 0)

        def queries(qi):
            q0 = pl.multiple_of(qi * T, T)
            qblk = q_ref[pl.ds(q0, T), :]
            if split:
                return (qblk[:, :LANES], qblk[:, LANES:])
            return tuple(_q_aug(qblk, lcc_ref[h, pl.ds(q0, T), :], h, scale, lane) for h in (0, 1))

        def scores(qs, c):
            k0 = pl.multiple_of(c * CH, CH)
            out = []
            for h in (0, 1):
                if split:
                    out.append(_mm_nt(k_ref[pl.ds(k0, CH), LANES * h:LANES * (h + 1)], qs[h]) * scale)
                else:
                    out.append(_mm_nt(kaug[h, pl.ds(k0, CH), :], qs[h]))
            return tuple(out)

        def q_block(qi, carry):
            qs, first_scores = carry[:2], carry[2:]
            q0 = pl.multiple_of(qi * T, T)
            acc[...] = jnp.zeros_like(acc)
            m_sc[...] = jnp.full(m_sc.shape, NEG, jnp.float32)

            def absorb(c, sts, masked):
                k0 = pl.multiple_of(c * CH, CH)
                for h in (0, 1):
                    st = sts[h]
                    if masked:
                        st = jnp.where(key_minus_qry <= q0 - k0, st, NEG)
                    m_old = m_sc[h:h + 1, :]
                    m_new = jnp.maximum(m_old, jnp.max(st, axis=0, keepdims=True))
                    alpha = jnp.exp(m_old - m_new)
                    pt = jnp.exp(st - m_new)
                    acc[h] = alpha * acc[h] + _mm(vt[h, :, pl.ds(k0, CH)], pt)
                    m_sc[h:h + 1, :] = m_new

            last = qi // FWD_CHUNK

            def pipelined(c, sts):
                nxt = scores(qs, c + 1)
                absorb(c, sts, False)
                return nxt

            sts = lax.fori_loop(0, last, pipelined, first_scores)
            qs_next = queries(jnp.minimum(qi + 1, nq - 1))
            nxt = qs_next + scores(qs_next, 0)
            absorb(last, sts, True)
            ot = None
            for h in (0, 1):
                a = acc[h]
                l = a[ONE[h]:ONE[h] + 1, :]
                oh = jnp.where(_data_lanes(sub, h), a * (1.0 / l), 0.0)
                ot = oh if ot is None else ot + oh
                lse_ref[0, h:h + 1, pl.ds(q0, T)] = m_sc[h:h + 1, :] + jnp.log(l)
            o_ref[pl.ds(q0, T), :] = ot.T
            return nxt

        qs0 = queries(0)
        lax.fori_loop(0, nq, q_block, qs0 + scores(qs0, 0))

    wide = pl.BlockSpec((S, W), lambda j: (0, j))
    slab = pl.BlockSpec((S, LANES), lambda j: (0, j))
    rows = pl.BlockSpec((1, 2, S), lambda j: (j, 0, 0))
    in_specs = [wide, wide, slab]
    args = [q, k, v]
    scratch = []
    if not split:
        in_specs.append(pl.BlockSpec((2, S, 1), lambda j: (j, 0, 0)))
        args.append(lcc)
        scratch.append(pltpu.VMEM((2, S, LANES), MXU))
    scratch += [pltpu.VMEM((2, LANES, S), MXU), pltpu.VMEM((2, LANES, T), jnp.float32), pltpu.VMEM((8, T), jnp.float32)]
    (o, lse), rode = _pcall_riding(
        body, plan, args, name=name, grid=(npair,), in_specs=in_specs, out_specs=[slab, rows],
        out_shape=[_sds((S, npair * LANES), jnp.float32), _sds((npair, 2, S), jnp.float32)], scratch_shapes=scratch)
    return o, lse, rode


def _attn_bwd_t(q, k, v, do, o, lse, scale, *, split, name, lcc=None, plan=None):
    S = q.shape[0]
    npair = v.shape[1] // LANES
    W = 2 * LANES if split else LANES
    T = ATT
    CH = BWD_CHUNK * T
    assert S % CH == 0
    nq = S // T

    def body(*refs):
        if split:
            (q_ref, k_ref, v_ref, do_ref, o_ref, lse_ref, dq_ref, dk_ref, dv_ref, dqt, delta, dk_acc, dv_acc) = refs
        else:
            (q_ref, k_ref, v_ref, do_ref, o_ref, lse_ref, lcc_ref, dq_ref, dk_ref, dv_ref, dlc_ref,
             dqt, delta, dk_acc, dv_acc, qaug, csum) = refs
        lane = lax.broadcasted_iota(jnp.int32, (1, LANES), 1)
        sub = lax.broadcasted_iota(jnp.int32, (LANES, 1), 0)
        key_minus_qry = lax.broadcasted_iota(jnp.int32, (T, CH), 0) - lax.broadcasted_iota(jnp.int32, (T, CH), 1)

        def prep(i, c):
            r0 = pl.multiple_of(i * T, T)
            prod_t = (do_ref[pl.ds(r0, T), :].astype(jnp.float32) * o_ref[pl.ds(r0, T), :]).T
            for h in (0, 1):
                delta[h:h + 1, pl.ds(r0, T)] = jnp.sum(jnp.where(_data_lanes(sub, h), prod_t, 0.0), axis=0, keepdims=True)
                dqt[h, :, pl.ds(r0, T)] = jnp.zeros((LANES, T), jnp.float32)
                if not split:
                    qaug[h, pl.ds(r0, T), :] = _q_aug(q_ref[pl.ds(r0, T), :], lcc_ref[h, pl.ds(r0, T), :], h, scale, lane)
            return c

        lax.fori_loop(0, nq, prep, 0)

        def keys(ki):
            k0 = pl.multiple_of(ki * T, T)
            kblk = k_ref[pl.ds(k0, T), :]
            if split:
                return (kblk[:, :LANES], kblk[:, LANES:])
            return tuple(_k_aug(kblk, lcc_ref[h, pl.ds(k0, T), :], h, lane) for h in (0, 1))

        def q_of(c, h):
            q0 = pl.multiple_of(c * CH, CH)
            if split:
                return q_ref[pl.ds(q0, CH), LANES * h:LANES * (h + 1)]
            return qaug[h, pl.ds(q0, CH), :]

        def scores(khs, c):
            out = []
            for h in (0, 1):
                st = _mm_nt(khs[h], q_of(c, h))
                out.append(st * scale if split else st)
            return tuple(out)

        def k_block(ki, carry):
            khs, first_scores = carry[:2], carry[2:]
            k0 = pl.multiple_of(ki * T, T)
            khts = [kh.astype(jnp.float32).T.astype(kh.dtype) for kh in khs]
            vhs = _split_heads(v_ref[pl.ds(k0, T), :], lane < HALF)
            dk_acc[...] = jnp.zeros_like(dk_acc)
            dv_acc[...] = jnp.zeros_like(dv_acc)

            def absorb(c, vals):
                q0 = pl.multiple_of(c * CH, CH)
                dos = _split_heads(do_ref[pl.ds(q0, CH), :], lane < HALF)
                visible = key_minus_qry <= q0 - k0
                for h in (0, 1):
                    dpt = _mm_nt(vhs[h], dos[h])
                    st = jnp.where(visible, vals[h], NEG)
                    pt = jnp.exp(st - lse_ref[0, h:h + 1, pl.ds(q0, CH)])
                    dv_acc[...] += _mm(pt, dos[h])
                    dst = pt * (dpt - delta[h:h + 1, pl.ds(q0, CH)])
                    dk_acc[h] += _mm(dst, q_of(c, h))
                    dqt[h, :, pl.ds(q0, CH)] += _mm(khts[h], dst)

            first = ki // BWD_CHUNK

            def pipelined(c, vals):
                nxt = scores(khs, c + 1)
                absorb(c, vals)
                return nxt

            vals = lax.fori_loop(first, S // CH - 1, pipelined, first_scores)
            kn = jnp.minimum(ki + 1, nq - 1)
            khs_next = keys(kn)
            nxt = khs_next + scores(khs_next, kn // BWD_CHUNK)
            absorb(S // CH - 1, vals)
            if split:
                dk_ref[pl.ds(k0, T), :LANES] = (dk_acc[0] * scale).astype(dk_ref.dtype)
                dk_ref[pl.ds(k0, T), LANES:] = (dk_acc[1] * scale).astype(dk_ref.dtype)
            else:
                dk_ref[pl.ds(k0, T), :] = jnp.where(lane < HALF, dk_acc[0], dk_acc[1]).astype(dk_ref.dtype)
                for h in (0, 1):
                    csum[h:h + 1, pl.ds(k0, T)] = dk_acc[h].T[AUG[h] + 3:AUG[h] + 4, :]
            dv_ref[pl.ds(k0, T), :] = dv_acc[...].astype(dv_ref.dtype)
            return nxt

        khs0 = keys(0)
        lax.fori_loop(0, nq, k_block, khs0 + scores(khs0, 0))

        def finish(i, c):
            r0 = pl.multiple_of(i * T, T)
            if split:
                for h in (0, 1):
                    dq_ref[pl.ds(r0, T), LANES * h:LANES * (h + 1)] = (dqt[h, :, pl.ds(r0, T)].T * scale).astype(dq_ref.dtype)
            else:
                d = jnp.where(sub < HALF, dqt[0, :, pl.ds(r0, T)], dqt[1, :, pl.ds(r0, T)])
                dq_ref[pl.ds(r0, T), :] = (d.T * scale).astype(dq_ref.dtype)
                for h in (0, 1):
                    dlc_ref[0, h:h + 1, pl.ds(r0, T)] = dqt[h, AUG[h]:AUG[h] + 1, pl.ds(r0, T)] - csum[h:h + 1, pl.ds(r0, T)]
            return c

        lax.fori_loop(0, nq, finish, 0)

    wide = pl.BlockSpec((S, W), lambda j: (0, j))
    slab = pl.BlockSpec((S, LANES), lambda j: (0, j))
    rows = pl.BlockSpec((1, 2, S), lambda j: (j, 0, 0))
    in_specs = [wide, wide, slab, slab, slab, rows]
    args = [q, k, v, do, o, lse]
    out_specs = [wide, wide, slab]
    out_shape = [_sds(q.shape, jnp.float32 if split else do.dtype), _sds(k.shape, jnp.float32 if split else do.dtype),
                 _sds(v.shape, do.dtype)]
    scratch = [pltpu.VMEM((2, LANES, S), jnp.float32), pltpu.VMEM((8, S), jnp.float32),
               pltpu.VMEM((2, T, LANES), jnp.float32), pltpu.VMEM((T, LANES), jnp.float32)]
    if not split:
        in_specs.append(pl.BlockSpec((2, S, 1), lambda j: (j, 0, 0)))
        args.append(lcc)
        out_specs.append(rows)
        out_shape.append(_sds((npair, 2, S), jnp.float32))
        scratch += [pltpu.VMEM((2, S, LANES), MXU), pltpu.VMEM((8, S), jnp.float32)]
    outs, rode = _pcall_riding(body, plan, args, name=name, grid=(npair,), in_specs=in_specs, out_specs=out_specs,
                               out_shape=out_shape, scratch_shapes=scratch)
    return (*outs, rode)


def _swa_bias(slope, shift):
    a = lax.broadcasted_iota(jnp.int32, (WINDOW, 2 * WINDOW), 0)
    c = lax.broadcasted_iota(jnp.int32, (WINDOW, 2 * WINDOW), 1)
    dist = a - c + shift
    return jnp.where((dist >= 0) & (dist < WINDOW), -slope * dist.astype(jnp.float32), NEG)


def _swa_scores(qh, kblk, bias):
    return _mm_nt(qh, kblk) * (HEAD ** -0.5) + bias


def _swa_fwd(q, kd, vd, sinks, slopes):
    S = q.shape[0]
    npair = q.shape[1] // LANES
    nb = S // WINDOW

    def body(sink_ref, slope_ref, q_ref, k_ref, v_ref, o_ref, lse_ref):
        j = pl.program_id(0)
        lo = _lane_masks()
        biases = [(_swa_bias(slope_ref[2 * j + h], 0), _swa_bias(slope_ref[2 * j + h], WINDOW)) for h in (0, 1)]

        def q_block(qi, c):
            q0 = pl.multiple_of(qi * WINDOW, WINDOW)
            k0 = pl.multiple_of(jnp.maximum(qi - 1, 0) * WINDOW, WINDOW)
            qs = _split_heads(q_ref[pl.ds(q0, WINDOW), :], lo)
            kblk = k_ref[pl.ds(k0, 2 * WINDOW), :]
            vs = _split_heads(v_ref[pl.ds(k0, 2 * WINDOW), :], lo)
            o = None
            for h in (0, 1):
                sink = sink_ref[2 * j + h]
                s = _swa_scores(qs[h], kblk, jnp.where(qi == 0, *biases[h]))
                m = jnp.maximum(jnp.max(s, axis=1, keepdims=True), sink)
                p = jnp.exp(s - m)
                den = jnp.sum(p, axis=1, keepdims=True) + jnp.exp(sink - m)
                oh = _mm(p / den, vs[h])
                o = oh if o is None else o + oh
                lse_ref[h, pl.ds(q0, WINDOW), :] = m + jnp.log(den)
            o_ref[pl.ds(q0, WINDOW), :] = o
            return c

        def q_group(gi, c):
            for g in range(SWA_GROUP):
                q_block(gi * SWA_GROUP + g, c)
            return c

        lax.fori_loop(0, nb // SWA_GROUP, q_group, 0)

    smem = pl.BlockSpec(memory_space=pltpu.SMEM)
    slab = pl.BlockSpec((S, LANES), lambda j: (0, j))
    return _pcall(
        body, name="swa_fwd", grid=(npair,), semantics=("arbitrary",),
        in_specs=[smem, smem, slab, slab, slab],
        out_specs=[slab, pl.BlockSpec((2, S, 1), lambda j: (j, 0, 0))],
        out_shape=[_sds((S, npair * LANES), jnp.float32), _sds((2 * npair, S, 1), jnp.float32)],
    )(sinks, slopes, q, kd, vd)


def _swa_bwd(q, kd, vd, do, o, lse, sinks, slopes, plan=None):
    S = q.shape[0]
    npair = q.shape[1] // LANES
    nb = S // WINDOW

    def body(sink_ref, slope_ref, q_ref, k_ref, v_ref, do_ref, o_ref, lse_ref,
             dq_ref, dk_ref, dv_ref, dsink_ref, dk_acc, dv_acc):
        j = pl.program_id(0)
        lo = _lane_masks()
        dk_acc[...] = jnp.zeros_like(dk_acc)
        dv_acc[...] = jnp.zeros_like(dv_acc)
        biases = [(_swa_bias(slope_ref[2 * j + h], 0), _swa_bias(slope_ref[2 * j + h], WINDOW)) for h in (0, 1)]

        def q_block(qi, carry):
            q0 = pl.multiple_of(qi * WINDOW, WINDOW)
            k0 = pl.multiple_of(jnp.maximum(qi - 1, 0) * WINDOW, WINDOW)
            qs = _split_heads(q_ref[pl.ds(q0, WINDOW), :], lo)
            dos = _split_heads(do_ref[pl.ds(q0, WINDOW), :], lo)
            oblk = o_ref[pl.ds(q0, WINDOW), :]
            kblk = k_ref[pl.ds(k0, 2 * WINDOW), :]
            vblk = v_ref[pl.ds(k0, 2 * WINDOW), :]
            ks = _split_heads(kblk, lo)
            dq = None
            out = []
            for h in (0, 1):
                sink = sink_ref[2 * j + h]
                lse_h = lse_ref[h, pl.ds(q0, WINDOW), :]
                s = _swa_scores(qs[h], kblk, jnp.where(qi == 0, *biases[h]))
                p = jnp.exp(s - lse_h)
                delta = jnp.sum(dos[h].astype(jnp.float32) * oblk, axis=1, keepdims=True)
                dv_acc[pl.ds(k0, 2 * WINDOW), :] += _mm_tn(p, dos[h])
                dp = _mm_nt(dos[h], vblk)
                ds = p * (dp - delta)
                dqh = _mm(ds, ks[h]) * (HEAD ** -0.5)
                dq = dqh if dq is None else dq + dqh
                dk_acc[pl.ds(k0, 2 * WINDOW), :] += _mm_tn(ds, qs[h]) * (HEAD ** -0.5)
                dsk = jnp.sum(-jnp.exp(sink - lse_h) * delta, axis=0, keepdims=True)
                out.append(carry[h] + dsk)
            dq_ref[pl.ds(q0, WINDOW), :] = dq.astype(dq_ref.dtype)
            return tuple(out)

        def q_group(gi, carry):
            for g in range(SWA_GROUP):
                carry = q_block(gi * SWA_GROUP + g, carry)
            return carry

        zero = jnp.zeros((1, 1), jnp.float32)
        dsa, dsb = lax.fori_loop(0, nb // SWA_GROUP, q_group, (zero, zero))
        dk_ref[...] = dk_acc[...].astype(dk_ref.dtype)
        dv_ref[...] = dv_acc[...].astype(dv_ref.dtype)
        r = lax.broadcasted_iota(jnp.int32, (8, LANES), 0)
        dsink_ref[0] = jnp.where(r == 0, dsa, jnp.where(r == 1, dsb, 0.0))

    smem = pl.BlockSpec(memory_space=pltpu.SMEM)
    slab = pl.BlockSpec((S, LANES), lambda j: (0, j))
    outs, rode = _pcall_riding(
        body, plan, [sinks, slopes, q, kd, vd, do, o, lse], name="swa_bwd", grid=(npair,),
        in_specs=[smem, smem, slab, slab, slab, slab, slab, pl.BlockSpec((2, S, 1), lambda j: (j, 0, 0))],
        out_specs=[slab, slab, slab, pl.BlockSpec((1, 8, LANES), lambda j: (j, 0, 0))],
        out_shape=[_sds(q.shape, do.dtype), _sds(kd.shape, do.dtype), _sds(vd.shape, do.dtype),
                   _sds((npair, 8, LANES), jnp.float32)],
        scratch_shapes=[pltpu.VMEM((S, LANES), jnp.float32), pltpu.VMEM((S, LANES), jnp.float32)])
    return (*outs, rode)


def _log_steps(S):
    k, out = 1, []
    while k < S:
        out.append(k)
        k *= 2
    return out


def _forget_fwd(f_row, b_col):
    S = f_row.shape[1]

    def body(f_ref, b_ref, lc_ref):
        x = f_ref[...] + b_ref[...]
        lc = jnp.minimum(x, 0.0) - jnp.log(1.0 + jnp.exp(-jnp.abs(x)))
        idx = lax.broadcasted_iota(jnp.int32, lc.shape, 1)
        for k in _log_steps(S):
            lc = lc + jnp.where(idx >= k, pltpu.roll(lc, k, axis=1), 0.0)
        lc_ref[...] = lc

    return _pcall(body, name="forget_fwd", out_shape=_sds(f_row.shape, jnp.float32))(f_row, b_col)


def _forget_bwd(dlc_row, f_row, b_col):
    S = f_row.shape[1]

    def body(d_ref, f_ref, b_ref, df_ref, db_ref):
        g = d_ref[...]
        idx = lax.broadcasted_iota(jnp.int32, g.shape, 1)
        for k in _log_steps(S):
            g = g + jnp.where(idx < S - k, pltpu.roll(g, S - k, axis=1), 0.0)
        x = f_ref[...] + b_ref[...]
        df = g * _sigmoid(-x)
        df_ref[...] = df
        db_ref[...] = jnp.sum(df, axis=1, keepdims=True)

    return _pcall(body, name="forget_bwd",
                  out_shape=[_sds(f_row.shape, jnp.float32), _sds((f_row.shape[0], 1), jnp.float32)])(dlc_row, f_row, b_col)


def _layer0_out_layer1_in(x, o_m, o_s, gate, w_out, g1, w_in1):
    S = x.shape[0]

    def body(x_ref, om_ref, os_ref, gate_ref, wo_ref, g_ref, w_ref,
             x1_ref, u_ref, h_ref, q_ref, k_ref, v_ref, g1_ref, f_ref):
        gt = gate_ref[...]
        sg = gt * _sigmoid(gt)
        um = om_ref[...] * sg[:, :512]
        us = os_ref[...] * sg[:, 512:]
        u_ref[:, :512] = um.astype(u_ref.dtype)
        u_ref[:, 512:] = us.astype(u_ref.dtype)
        x1 = x_ref[...] + _mm(um, wo_ref[0:512, :]) + _mm(us, wo_ref[512:1024, :])
        x1_ref[...] = x1
        h = _rms(x1, g_ref[...])
        h_ref[...] = h.astype(h_ref.dtype)
        z = _mm_nt(h, w_ref[...])
        q_ref[...] = z[:, 0:1024].astype(q_ref.dtype)
        k_ref[...] = z[:, 1024:2048].astype(k_ref.dtype)
        v_ref[...] = z[:, 2048:3072].astype(v_ref.dtype)
        g1_ref[...] = z[:, 3072:4096]
        f_ref[...] = z[:, 4096:4224]

    outs = [((S, D), jnp.float32), ((S, D), MXU), ((S, D), MXU), ((S, D), MXU), ((S, D), MXU), ((S, D), MXU),
            ((S, D), jnp.float32), ((S, LANES), jnp.float32)]
    return _pcall(
        body, name="layer0_out_layer1_in", grid=(S // TOK,), semantics=("arbitrary",),
        in_specs=[_rows(TOK, D), _rows(TOK, 512), _rows(TOK, 512), _rows(TOK, D), _full((D, D)), _full((1, D)),
                  _full(w_in1.shape)],
        out_specs=[_rows(TOK, s[1]) for s, _ in outs],
        out_shape=[_sds(s, d) for s, d in outs],
    )(x, o_m, o_s, gate, w_out, g1, w_in1)


def _head(x1, o1, gate1, w_out1, g_f, target):
    S = x1.shape[0]

    def body(x1_ref, o_ref, gate_ref, wo_ref, g_ref, t_ref,
             loss_ref, dgf_ref, dx2_ref, u_ref, do_ref, dgate_ref):
        i = pl.program_id(0)
        gt = gate_ref[...]
        sig = _sigmoid(gt)
        sg = gt * sig
        o = o_ref[...]
        u = o * sg
        u_ref[...] = u.astype(u_ref.dtype)
        x2 = x1_ref[...] + _mm(u, wo_ref[...])
        g = g_ref[...]
        y = _rms(x2, g)
        err = y - t_ref[...]
        part = 0.5 * jnp.sum(jnp.mean(err * err, axis=-1, keepdims=True), axis=0, keepdims=True)
        dy = err * (1.0 / D)
        dx2, dg_rows = _rms_bwd(x2, g, dy)
        dx2_ref[...] = dx2
        du = _mm_nt(dx2, wo_ref[...])
        do_ref[...] = (du * sg).astype(do_ref.dtype)
        dgate_ref[...] = (du * o * (sig * (1.0 + gt * (1.0 - sig)))).astype(dgate_ref.dtype)

        @pl.when(i == 0)
        def _():
            loss_ref[...] = jnp.zeros_like(loss_ref)
            dgf_ref[...] = jnp.zeros_like(dgf_ref)

        loss_ref[...] += jnp.broadcast_to(part, loss_ref.shape)
        dgf_ref[...] += jnp.sum(dg_rows, axis=0, keepdims=True)

    outs = [((S, D), jnp.float32), ((S, D), MXU), ((S, D), MXU), ((S, D), MXU)]
    return _pcall(
        body, name="head", grid=(S // TOK,), semantics=("arbitrary",),
        in_specs=[_rows(TOK, D), _rows(TOK, D), _rows(TOK, D), _full((D, D)), _full((1, D)), _rows(TOK, D)],
        out_specs=[_full((8, LANES)), _full((1, D))] + [_rows(TOK, D) for _ in outs],
        out_shape=[_sds((8, LANES), jnp.float32), _sds((1, D), jnp.float32)] + [_sds(s, d) for s, d in outs],
    )(x1, o1, gate1, w_out1, g_f, target)


def _layer1_in_bwd(dq, dk, dv, dgate1, df, x1, dx2, g1, w_in1, gate0, o_m, o_s, w_out0):
    S = x1.shape[0]

    def body(dq_ref, dk_ref, dv_ref, dg1_ref, df_ref, x1_ref, dx2_ref, g_ref, w_ref, gate_ref, om_ref, os_ref,
             wo_ref, dz_ref, dx1_ref, dgn_ref, dom_ref, dos_ref, dgate_ref):
        i = pl.program_id(0)
        dz_ref[:, 0:1024] = dq_ref[...]
        dz_ref[:, 1024:2048] = dk_ref[...]
        dz_ref[:, 2048:3072] = dv_ref[...]
        dz_ref[:, 3072:4096] = dg1_ref[...]
        dz_ref[:, 4096:4224] = df_ref[...]
        dh = _mm(dz_ref[...], w_ref[...])
        g = g_ref[...]
        dxn, dg_rows = _rms_bwd(x1_ref[...], g, dh)
        dx1 = dx2_ref[...] + dxn
        dx1_ref[...] = dx1
        du = _mm_nt(dx1, wo_ref[...])
        gt = gate_ref[...]
        sig = _sigmoid(gt)
        sg = gt * sig
        dsg = sig * (1.0 + gt * (1.0 - sig))
        dom_ref[...] = (du[:, :512] * sg[:, :512]).astype(dom_ref.dtype)
        dos_ref[...] = (du[:, 512:] * sg[:, 512:]).astype(dos_ref.dtype)
        dgate_ref[:, :512] = (du[:, :512] * om_ref[...] * dsg[:, :512]).astype(dgate_ref.dtype)
        dgate_ref[:, 512:] = (du[:, 512:] * os_ref[...] * dsg[:, 512:]).astype(dgate_ref.dtype)

        @pl.when(i == 0)
        def _():
            dgn_ref[...] = jnp.zeros_like(dgn_ref)

        dgn_ref[...] += jnp.sum(dg_rows, axis=0, keepdims=True)

    return _pcall(
        body, name="layer1_in_bwd", grid=(S // TOK,), semantics=("arbitrary",),
        in_specs=[_rows(TOK, D), _rows(TOK, D), _rows(TOK, D), _rows(TOK, D), _rows(TOK, LANES), _rows(TOK, D),
                  _rows(TOK, D), _full((1, D)), _full(w_in1.shape), _rows(TOK, D), _rows(TOK, 512), _rows(TOK, 512),
                  _full((D, D))],
        out_specs=[_rows(TOK, 4224), _rows(TOK, D), _full((1, D)), _rows(TOK, 512), _rows(TOK, 512), _rows(TOK, D)],
        out_shape=[_sds((S, 4224), MXU), _sds((S, D), jnp.float32), _sds((1, D), jnp.float32),
                   _sds((S, 512), MXU), _sds((S, 512), MXU), _sds((S, D), MXU)],
    )(dq, dk, dv, dgate1, df, x1, dx2, g1, w_in1, gate0, o_m, o_s, w_out0)


def _layer0_in_bwd(dqm, dkm, dvm, dqs, dkd, dvd, dgate0, cos, sin, cq, ckv, x, dx1, g_in, w_in, g_q, w_q, g_kv, w_kv):
    S = x.shape[0]
    consts = _rope_consts()

    def body(dqm_ref, dkm_ref, dvm_ref, dqs_ref, dkd_ref, dvd_ref, dgate_ref, cos_ref, sin_ref, c_ref, cq_ref, ckv_ref,
             x_ref, dx1_ref, g_ref, w_ref, gq_ref, wq_ref, gkv_ref, wkv_ref,
             dx_ref, dz_ref, dqu_ref, dkvu_ref, dgin_ref, dgq_ref, dgkv_ref):
        i = pl.program_id(0)
        lo = _lane_masks()
        sign = c_ref[...][1:2, :]
        c = cos_ref[...]
        s = sin_ref[...]
        dkpe = None
        for hd in range(N_MLA):
            sl = slice(LANES * hd, LANES * (hd + 1))
            dqu_ref[:, sl] = _rope_t(dqm_ref[:, sl], c, s, sign).astype(dqu_ref.dtype)
            dkh = dkm_ref[:, sl]
            dkvu_ref[:, sl] = jnp.where(lo, dkh, 0.0).astype(dkvu_ref.dtype)
            dkpe = dkh if dkpe is None else dkpe + dkh
        dkvu_ref[:, 1024:1536] = dvm_ref[...]
        dkpe = _rope_t(jnp.where(lo, 0.0, dkpe), c, s, sign)
        dcqn = _mm(dqu_ref[...], wq_ref[...])
        dckvn = _mm_nt(dkvu_ref[...], wkv_ref[...])
        gq = gq_ref[...]
        gkv = gkv_ref[...]
        dcq, dgq_rows = _rms_bwd(cq_ref[...], gq, dcqn)
        dckv, dgkv_rows = _rms_bwd(ckv_ref[...], gkv, dckvn)
        dz_ref[:, 0:256] = dcq.astype(dz_ref.dtype)
        dz_ref[:, 256:384] = dckv.astype(dz_ref.dtype)
        dz_ref[:, 384:512] = dkpe.astype(dz_ref.dtype)
        dz_ref[:, 512:1024] = dqs_ref[...]
        dz_ref[:, 1024:1536] = dkd_ref[...]
        dz_ref[:, 1536:2048] = dvd_ref[...]
        dz_ref[:, 2048:3072] = dgate_ref[...]
        dh = _mm(dz_ref[...], w_ref[...])
        g = g_ref[...]
        dxn, dg_rows = _rms_bwd(x_ref[...], g, dh)
        dx_ref[...] = dx1_ref[...] + dxn

        @pl.when(i == 0)
        def _():
            dgin_ref[...] = jnp.zeros_like(dgin_ref)
            dgq_ref[...] = jnp.zeros_like(dgq_ref)
            dgkv_ref[...] = jnp.zeros_like(dgkv_ref)

        dgin_ref[...] += jnp.sum(dg_rows, axis=0, keepdims=True)
        dgq_ref[...] += jnp.sum(dgq_rows, axis=0, keepdims=True)
        dgkv_ref[...] += jnp.sum(dgkv_rows, axis=0, keepdims=True)

    return _pcall(
        body, name="layer0_in_bwd", grid=(S // TOK,), semantics=("arbitrary",),
        in_specs=[_rows(TOK, 1024), _rows(TOK, 1024), _rows(TOK, 512), _rows(TOK, 512), _rows(TOK, 512), _rows(TOK, 512),
                  _rows(TOK, D), _rows(TOK, LANES), _rows(TOK, LANES), _full((8, LANES)), _rows(TOK, 256), _rows(TOK, 128),
                  _rows(TOK, D), _rows(TOK, D), _full((1, D)), _full(w_in.shape), _full((1, 256)), _full(w_q.shape),
                  _full((1, 128)), _full(w_kv.shape)],
        out_specs=[_rows(TOK, D), _rows(TOK, 3072), _rows(TOK, 1024), _rows(TOK, 1536), _full((1, D)), _full((1, 256)),
                   _full((1, 128))],
        out_shape=[_sds((S, D), jnp.float32), _sds((S, 3072), MXU), _sds((S, 1024), MXU), _sds((S, 1536), MXU),
                   _sds((1, D), jnp.float32), _sds((1, 256), jnp.float32), _sds((1, 128), jnp.float32)],
    )(dqm, dkm, dvm, dqs, dkd, dvd, dgate0, cos, sin, consts, cq, ckv, x, dx1, g_in, w_in, g_q, w_q, g_kv, w_kv)


def _wgrad(a, b, name):
    S, M = a.shape
    N = b.shape[1]
    tm = next(t for t in range(WG_ROWS, 0, -LANES) if M % t == 0)
    tn = N if N <= 1024 else 512
    tk = min(WG_TOK, S)

    def body(a_ref, b_ref, o_ref):
        @pl.when(pl.program_id(2) == 0)
        def _():
            o_ref[...] = jnp.zeros_like(o_ref)

        o_ref[...] += _mm_tn(a_ref[...], b_ref[...])

    return _pcall(
        body, name=name, grid=(M // tm, N // tn, S // tk), semantics=("parallel", "parallel", "arbitrary"),
        in_specs=[pl.BlockSpec((tk, tm), lambda m, n, k: (k, m)), pl.BlockSpec((tk, tn), lambda m, n, k: (k, n))],
        out_specs=pl.BlockSpec((tm, tn), lambda m, n, k: (m, n)),
        out_shape=_sds((M, N), jnp.float32),
    )(a, b)


def _adamw(w, g, m, v, name):
    shape = w.shape
    R, C = (int(np.prod(shape[:-1])), shape[-1])
    w2, g2, m2, v2 = (t.reshape(R, C) for t in (w, g, m, v))
    fits = [t for t in range(8, ADAM_TILE_BYTES // (4 * C) + 1, 8) if R % t == 0]
    tr = max(fits) if fits else R

    def body(w_ref, g_ref, m_ref, v_ref, d_ref, nm_ref, nv_ref):
        gg = g_ref[...]
        nm = B1 * m_ref[...] + (1.0 - B1) * gg
        nv = B2 * v_ref[...] + (1.0 - B2) * (gg * gg)
        m_hat = nm / (1.0 - B1 ** STEP)
        v_hat = nv / (1.0 - B2 ** STEP)
        d_ref[...] = -LR * (m_hat / (jnp.sqrt(v_hat) + AEPS) + WD * w_ref[...])
        nm_ref[...] = nm
        nv_ref[...] = nv

    spec = _rows(tr, C)
    d, nm, nv = _pcall(
        body, name=name, grid=(R // tr,), semantics=("parallel",),
        in_specs=[spec] * 4, out_specs=[spec] * 3, out_shape=[_sds((R, C), jnp.float32)] * 3,
    )(w2, g2, m2, v2)
    return d.reshape(shape), nm.reshape(shape), nv.reshape(shape)


def _sum_leading(a, name):
    n, R, C = a.shape
    tr = SUM_ROWS if R % SUM_ROWS == 0 else R

    def body(a_ref, o_ref):
        acc = a_ref[0]
        for i in range(1, n):
            acc = acc + a_ref[i]
        o_ref[...] = acc

    return _pcall(
        body, name=name, grid=(R // tr,), semantics=("parallel",),
        in_specs=[pl.BlockSpec((n, tr, C), lambda i: (0, i, 0))], out_specs=_rows(tr, C),
        out_shape=_sds((R, C), a.dtype),
    )(a)


def _add_blocks(a, b, name, out_dtype):
    n, R, C = a.shape
    tr = SUM_ROWS if R % SUM_ROWS == 0 else R

    def body(a_ref, b_ref, o_ref):
        o_ref[...] = (a_ref[...] + b_ref[...]).astype(o_ref.dtype)

    spec = pl.BlockSpec((1, tr, C), lambda k, i: (k, i, 0))
    return _pcall(
        body, name=name, grid=(n, R // tr), semantics=("parallel", "parallel"),
        in_specs=[spec, spec], out_specs=spec, out_shape=_sds(a.shape, out_dtype),
    )(a, b)


def _total_sum(mine, theirs, recv, name):
    R, C = mine.shape
    n = recv.shape[0]
    tr = SUM_ROWS if R % SUM_ROWS == 0 else R

    def body(a_ref, b_ref, r_ref, o_ref):
        acc = a_ref[...] + b_ref[...]
        for i in range(n):
            acc = acc + r_ref[i].astype(jnp.float32)
        o_ref[...] = acc

    return _pcall(
        body, name=name, grid=(R // tr,), semantics=("parallel",),
        in_specs=[_rows(tr, C), _rows(tr, C), pl.BlockSpec((n, tr, C), lambda i: (0, i, 0))], out_specs=_rows(tr, C),
        out_shape=_sds((R, C), jnp.float32),
    )(mine, theirs, recv)


def _place():
    return lax.axis_index("x"), lax.axis_index("y"), lax.axis_index("c")


class _Plan:
    def __init__(self, arrays, out_shape, scratch, start, finish, middle=None):
        self.arrays, self.out_shape, self.scratch = list(arrays), list(out_shape), list(scratch)
        self.start, self.finish, self.middle = start, finish, middle


def _gather8_plan(block):
    R, C = block.shape

    def parts(ins, outs, sems):
        (x_ref,), (out_ref,), (send_sems, recv_sems) = ins, outs, sems
        x, y, c = _place()
        me, sibling = (x, y, c), (x, y, 1 - c)
        chips = [(1 - x, y), (x, 1 - y), (1 - x, 1 - y)]

        def copy(k, blk, to, src=None):
            slot = out_ref.at[4 * blk[0] + 2 * blk[1] + blk[2]]
            return pltpu.make_async_remote_copy(
                src_ref=slot if src is None else src, dst_ref=slot,
                send_sem=send_sems.at[k], recv_sem=recv_sems.at[k], device_id=to, device_id_type=MESH_ID)

        def first():
            return [copy(0, me, sibling, src=x_ref)] + [copy(1 + j, me, (*chip, c), src=x_ref) for j, chip in enumerate(chips)]

        def passed():
            return [copy(4 + j, (*chip, c), sibling) for j, chip in enumerate(chips)]

        def arrivals():
            return [copy(1 + j, (*chip, c), me) for j, chip in enumerate(chips)]

        def late():
            return [copy(0, sibling, me)] + [copy(4 + j, (*chip, 1 - c), me) for j, chip in enumerate(chips)]

        return first, passed, arrivals, late

    def start(ins, outs, sems):
        for cp in parts(ins, outs, sems)[0]():
            cp.start()

    def middle(ins, outs, sems):
        _, passed, arrivals, _ = parts(ins, outs, sems)
        for arrived, forward in zip(arrivals(), passed()):
            arrived.wait_recv()
            forward.start()

    def finish(ins, outs, sems):
        first, passed, _, late = parts(ins, outs, sems)
        for cp in late():
            cp.wait_recv()
        for cp in first() + passed():
            cp.wait_send()

    return _Plan([block], [_sds((8, R, C), block.dtype)], [pltpu.SemaphoreType.DMA((7,)), pltpu.SemaphoreType.DMA((7,))],
                 start, finish, middle)


def _fill_own_slot(gathered, block):
    x, y, c = _place()
    return lax.dynamic_update_index_in_dim(gathered, block, 4 * x + 2 * y + c, 0)


def _started_and_waited(arrays, out_shape, n, copies):
    def start(ins, outs, sems):
        for cp in copies(ins, outs, sems):
            cp.start()

    def finish(ins, outs, sems):
        for cp in copies(ins, outs, sems):
            cp.wait()

    return _Plan(arrays, out_shape, [pltpu.SemaphoreType.DMA((n,)), pltpu.SemaphoreType.DMA((n,))], start, finish)


def _pair_swap_plan(g):
    n = g.shape[0]

    def copies(ins, outs, sems):
        (g_ref,), (out_ref,), (send_sems, recv_sems) = ins, outs, sems
        x, y, c = _place()
        return [pltpu.make_async_remote_copy(src_ref=g_ref.at[k, 1 - c], dst_ref=out_ref.at[k], send_sem=send_sems.at[k],
                                             recv_sem=recv_sems.at[k], device_id=(x, y, 1 - c), device_id_type=MESH_ID)
                for k in range(n)]

    return _started_and_waited([g], [_sds((n,) + g.shape[2:], g.dtype)], n, copies)


def _chip_exchange_plan(p):
    def copies(ins, outs, sems):
        (p_ref,), (out_ref,), (send_sems, recv_sems) = ins, outs, sems
        x, y, c = _place()
        chips = [(1 - x, y), (x, 1 - y), (1 - x, 1 - y)]
        return [pltpu.make_async_remote_copy(
            src_ref=p_ref.at[2 * cx + cy], dst_ref=out_ref.at[j], send_sem=send_sems.at[j],
            recv_sem=recv_sems.at[j], device_id=(cx, cy, c), device_id_type=MESH_ID)
            for j, (cx, cy) in enumerate(chips)]

    return _started_and_waited([p], [_sds((3,) + p.shape[1:], p.dtype)], 3, copies)


def _pair_exchange_plan(t):
    def copies(ins, outs, sems):
        (t_ref,), (out_ref,), (send_sems, recv_sems) = ins, outs, sems
        x, y, c = _place()
        return [pltpu.make_async_remote_copy(src_ref=t_ref, dst_ref=out_ref, send_sem=send_sems.at[0], recv_sem=recv_sems.at[0],
                                             device_id=(x, y, 1 - c), device_id_type=MESH_ID)]

    return _started_and_waited([t], [_sds(t.shape, t.dtype)], 1, copies)


ANY_SPEC = pl.BlockSpec(memory_space=pl.ANY)


def _run_plan(plan, name):
    n_in, n_out = len(plan.arrays), len(plan.out_shape)

    def body(*refs):
        ins, outs, sems = refs[:n_in], refs[n_in:n_in + n_out], refs[n_in + n_out:]
        plan.start(ins, outs, sems)
        if plan.middle is not None:
            plan.middle(ins, outs, sems)
        plan.finish(ins, outs, sems)

    return _pcall(body, name=name, in_specs=[ANY_SPEC] * n_in, out_specs=[ANY_SPEC] * n_out, out_shape=plan.out_shape,
                  scratch_shapes=plan.scratch)(*plan.arrays)


def _pcall_riding(body, plan, args, *, name, grid, in_specs, out_specs, out_shape, scratch_shapes):
    if plan is None:
        outs = _pcall(body, name=name, grid=grid, semantics=("arbitrary",), in_specs=in_specs, out_specs=out_specs,
                      out_shape=out_shape, scratch_shapes=scratch_shapes)(*args)
        return list(outs), None
    n_in, n_out, n_s = len(args), len(out_shape), len(scratch_shapes)
    p_in, p_out = len(plan.arrays), len(plan.out_shape)
    steps = grid[0]

    def riding(*refs):
        ins, pins = refs[:n_in], refs[n_in:n_in + p_in]
        o0 = n_in + p_in
        outs, pouts = refs[o0:o0 + n_out], refs[o0 + n_out:o0 + n_out + p_out]
        s0 = o0 + n_out + p_out
        scr, sems = refs[s0:s0 + n_s], refs[s0 + n_s:]
        j = pl.program_id(0)

        @pl.when(j == 0)
        def _():
            plan.start(pins, pouts, sems)

        if plan.middle is not None:
            @pl.when(j == steps // 2)
            def _():
                plan.middle(pins, pouts, sems)

        body(*ins, *outs, *scr)

        @pl.when(j == steps - 1)
        def _():
            plan.finish(pins, pouts, sems)

    res = _pcall(riding, name=name, grid=grid, semantics=("arbitrary",), in_specs=list(in_specs) + [ANY_SPEC] * p_in,
                 out_specs=list(out_specs) + [ANY_SPEC] * p_out, out_shape=list(out_shape) + plan.out_shape,
                 scratch_shapes=list(scratch_shapes) + plan.scratch)(*args, *plan.arrays)
    return list(res[:n_out]), list(res[n_out:])


def _prep_w_in0(wt):
    z32 = jnp.zeros((32, wt.shape[1]), wt.dtype)
    z64 = jnp.zeros((64, wt.shape[1]), wt.dtype)
    k0, k1 = wt[928:992], wt[992:1056]
    v0, v1 = wt[1056:1120], wt[1120:1184]
    return jnp.concatenate([wt[0:384], z64, wt[384:416], z32, wt[416:928],
                            k0, k0, k0, k0, k1, k1, k1, k1, v0, v0, v0, v0, v1, v1, v1, v1, wt[1184:2208]], axis=0)


def _fold_w_in0(d):
    def fold(blk):
        b = blk.reshape(8, 64, blk.shape[1])
        return jnp.concatenate([b[0] + b[1] + b[2] + b[3], b[4] + b[5] + b[6] + b[7]], axis=0)
    return jnp.concatenate([d[0:384], d[448:480], d[512:1024], fold(d[1024:1536]), fold(d[1536:2048]), d[2048:3072]], axis=0)


def _prep_w_q(wt):
    return jnp.pad(wt.reshape(N_MLA, 96, Q_RANK), ((0, 0), (0, 32), (0, 0))).reshape(1024, Q_RANK)


def _fold_w_q(d):
    return d.reshape(N_MLA, 128, Q_RANK)[:, :96].reshape(768, Q_RANK)


def _prep_w_kv(w):
    w3 = w.reshape(KV_RANK, N_MLA, 128)
    kk = jnp.pad(w3[:, :, :64], ((0, 0), (0, 0), (0, 64))).reshape(KV_RANK, 1024)
    return jnp.concatenate([kk, w3[:, :, 64:].reshape(KV_RANK, 512)], axis=1)


def _fold_w_kv(d):
    kk = d[:, :1024].reshape(KV_RANK, N_MLA, 128)[:, :, :64]
    vv = d[:, 1024:].reshape(KV_RANK, N_MLA, 64)
    return jnp.concatenate([kk, vv], axis=2).reshape(KV_RANK, 1024)


def _prep_w_in1(wt):
    return jnp.concatenate([wt[0:3072], wt[3088:4112], wt[3072:3088], jnp.zeros((112, wt.shape[1]), wt.dtype)], axis=0)


def _fold_w_in1(d):
    return jnp.concatenate([d[0:3072], d[4096:4112], d[3072:4096]], axis=0)


class _Alone:
    def __init__(self, w_out0, o_g_in, w_in1, w_out1):
        self.layer1 = (w_out0, o_g_in, w_in1, w_out1)

    def gather_plan(self):
        return None

    def layer1_weights(self, rode):
        return self.layer1

    def swap_plan(self, grads1):
        return None

    def exchange_plan(self, rode):
        return None

    def finish(self, rode):
        pass


def _local_step(x, pos, target, e_g_in, w_in0, e_g_q, w_q, e_g_kv, w_kv, sinks, b_f, g_final, layer1):
    S = x.shape[0]
    w_in0p, w_qp, w_kvp = _prep_w_in0(w_in0), _prep_w_q(w_q), _prep_w_kv(w_kv)
    slopes = jnp.asarray(2.0 ** (-8.0 * (np.arange(N_SWA, dtype=np.float32) + 1.0) / N_SWA), jnp.float32)
    sinks1 = sinks.reshape(N_SWA)
    b_col = b_f.reshape(N_FOX, 1)

    (h0, cq, ckv, cqn, ckvn, qm, km, vm, qs, kd, vd, gate0, cos, sin) = _layer0_in(
        x, pos, e_g_in, w_in0p, e_g_q, w_qp, e_g_kv, w_kvp)
    o_m, lse_m, rode = _attn_fwd_t(qm, km, vm, (NOPE + ROPE) ** -0.5, split=True, name="mla_fwd", plan=layer1.gather_plan())
    w_out0, o_g_in, w_in1, w_out1 = layer1.layer1_weights(rode)
    w_in1p = _prep_w_in1(w_in1)
    o_s, lse_s = _swa_fwd(qs, kd, vd, sinks1, slopes)
    x1, u0, h1, q1, k1, v1, gate1, f_slab = _layer0_out_layer1_in(x, o_m, o_s, gate0, w_out0, o_g_in, w_in1p)
    f_row = f_slab[:, :N_FOX].T
    lc_row = _forget_fwd(f_row, b_col)
    lcc = lc_row.reshape(N_FOX, S, 1)
    o1, lse1, _ = _attn_fwd_t(q1, k1, v1, HEAD ** -0.5, split=False, name="fox_fwd", lcc=lcc)
    loss8, dg_final, dx2, u1, do1, dgate1 = _head(x1, o1, gate1, w_out1, g_final, target)

    dq1, dk1, dv1, dlc, _ = _attn_bwd_t(q1, k1, v1, do1, o1, lse1, HEAD ** -0.5, split=False, name="fox_bwd", lcc=lcc)
    df_row, db_f = _forget_bwd(dlc.reshape(N_FOX, S), f_row, b_col)
    df_slab = jnp.pad(df_row.T, ((0, 0), (0, LANES - N_FOX))).astype(MXU)
    dz1, dx1, dg_o_in, do_m, do_s, dgate0 = _layer1_in_bwd(
        dq1, dk1, dv1, dgate1, df_slab, x1, dx2, o_g_in, w_in1p, gate0, o_m, o_s, w_out0)
    grads1 = dict(o_g_in=dg_o_in, o_w_in=_fold_w_in1(_wgrad(dz1, h1, "wgrad_in1")), o_w_out=_wgrad(u1, dx2, "wgrad_out1"),
                  e_w_out=_wgrad(u0, dx1, "wgrad_out0"))
    dqs, dkd, dvd, dsink, rode = _swa_bwd(qs, kd, vd, do_s, o_s, lse_s, sinks1, slopes, plan=layer1.swap_plan(grads1))
    dqm, dkm, dvm, rode = _attn_bwd_t(qm, km, vm, do_m, o_m, lse_m, (NOPE + ROPE) ** -0.5, split=True, name="mla_bwd",
                                      plan=layer1.exchange_plan(rode))
    layer1.finish(rode)
    dx, dz0, dqu, dkvu, dg_in, dg_q, dg_kv = _layer0_in_bwd(
        dqm, dkm, dvm, dqs, dkd, dvd, dgate0, cos, sin, cq, ckv, x, dx1, e_g_in, w_in0p, e_g_q, w_qp, e_g_kv, w_kvp)

    grads = dict(
        e_g_in=dg_in,
        e_w_in=_fold_w_in0(_wgrad(dz0, h0, "wgrad_in0")),
        e_g_q_a=dg_q,
        e_w_q_up=_fold_w_q(_wgrad(dqu, cqn, "wgrad_q_up")),
        e_g_kv_a=dg_kv,
        e_w_kv_up=_fold_w_kv(_wgrad(ckvn, dkvu, "wgrad_kv_up")),
        e_sinks=dsink[:, 0:2, 0].reshape(1, N_SWA),
        o_b_f=db_f.reshape(1, N_FOX),
        g_final=dg_final,
        **grads1,
    )
    return loss8[0, 0], dx, grads


SHARDED = ("e_w_in", "e_w_q_up", "e_w_kv_up", "e_w_out", "o_g_in", "o_w_in", "o_w_out")
TRANSPOSED = ("e_w_in", "e_w_q_up", "o_w_in")
COL_SHARDED = ("e_w_kv_up", "o_g_in")
REPLICATED = ("e_g_in", "e_g_q_a", "e_g_kv_a", "e_sinks", "o_b_f", "g_final")
FULL_SHAPES = dict(e_w_in=(2208, 1024), e_w_q_up=(768, 256), e_w_kv_up=(128, 1024), e_w_out=(1024, 1024),
                   o_g_in=(1, 1024), o_w_in=(4112, 1024), o_w_out=(1024, 1024))
GROUPS = dict(
    layer0=dict(rows=768, windows=dict(e_w_in=(0, 0), e_w_q_up=(560, 0), e_w_kv_up=(560, 256))),
    layer1=dict(rows=1568, windows=dict(o_w_in=(0, 0), o_w_out=(1040, 0), e_w_out=(1296, 0), o_g_in=(1552, 0))),
)


def _shard_shape(name):
    r, c = FULL_SHAPES[name]
    return (r, c // 4) if name in COL_SHARDED else (r // 4, c)


def _as_handled(name, a):
    a = a[0] if a.ndim == 3 else a
    return a.T if name in TRANSPOSED else a


def _as_given(name, a, shape):
    return (a.T if name in TRANSPOSED else a).reshape(shape)


def _pack_block(p, group):
    def rows(a, n):
        return jnp.pad(a, ((0, n - a.shape[0]), (0, 0)))

    if group == "layer0":
        band = jnp.concatenate([p["e_w_q_up"], rows(p["e_w_kv_up"], 192), jnp.zeros((192, 512), p["e_w_in"].dtype)], axis=1)
        return jnp.concatenate([rows(p["e_w_in"], 560), rows(band, 208)], axis=0)
    g = p["o_g_in"]
    band = jnp.pad(g, ((0, 16 - g.shape[0]), (0, PACK_COLS - g.shape[1])))
    return jnp.concatenate([rows(p["o_w_in"], 1040), p["o_w_out"], p["e_w_out"], band], axis=0)


def _window(block, group, name, width=None):
    r0, c0 = GROUPS[group]["windows"][name]
    r, c = _shard_shape(name)
    return block[..., r0:r0 + r, c0:c0 + (c if width is None else width)]


def _chip_slice(name, full, k):
    r, c = _shard_shape(name)
    return full[:, c * k:c * (k + 1)] if name in COL_SHARDED else full[r * k:r * (k + 1), :]


def _packed_weights(w, group):
    parts = {}
    for n in GROUPS[group]["windows"]:
        a = _as_handled(n, w[n])
        parts[n] = lax.bitcast_convert_type(a, jnp.bfloat16).reshape(1, -1) if n == "o_g_in" else a.astype(jnp.bfloat16)
    halves = _pack_block(parts, group).reshape(2, GROUPS[group]["rows"] // 2, PACK_COLS)
    return lax.dynamic_index_in_dim(halves, lax.axis_index("c"), 0, keepdims=False)


def _unpacked_weights(gathered, half, group):
    blocks = _fill_own_slot(gathered, half).reshape(4, GROUPS[group]["rows"], PACK_COLS)
    full = {}
    for n in GROUPS[group]["windows"]:
        if n == "o_g_in":
            halves = _window(blocks, group, n, width=512).reshape(4, 1, 256, 2)
            full[n] = jnp.concatenate(list(lax.bitcast_convert_type(halves, jnp.float32)), axis=1)
        else:
            pieces = [_window(blocks[k], group, n) for k in range(4)]
            full[n] = jnp.concatenate(pieces, axis=1 if n in COL_SHARDED else 0).astype(MXU)
    return full


class _GroupReduce:
    def __init__(self, group):
        self.group = group
        self.c = lax.axis_index("c")
        self.chip = 2 * lax.axis_index("x") + lax.axis_index("y")

    def swap_plan(self, grads):
        names = GROUPS[self.group]["windows"]
        per_chip = jnp.stack([_pack_block({n: _chip_slice(n, grads[n], k) for n in names}, self.group) for k in range(4)])
        self.g4 = per_chip.reshape(4, 2, GROUPS[self.group]["rows"] // 2, PACK_COLS)
        return _pair_swap_plan(self.g4)

    def exchange_plan(self, rode):
        theirs = rode[0]
        mine = lax.dynamic_index_in_dim(self.g4, self.c, 1, keepdims=False)
        self.own = (lax.dynamic_index_in_dim(mine, self.chip, 0, keepdims=False),
                    lax.dynamic_index_in_dim(theirs, self.chip, 0, keepdims=False))
        return _chip_exchange_plan(_add_blocks(mine, theirs, "pair_add_" + self.group, jnp.bfloat16))

    def finish(self, rode):
        my_half = _total_sum(*self.own, rode[0], "chip_sum_" + self.group)
        other_half = _run_plan(_pair_exchange_plan(my_half), "pair_exchange_" + self.group)[0]
        total = jnp.concatenate([jnp.where(self.c == 0, my_half, other_half), jnp.where(self.c == 0, other_half, my_half)], axis=0)
        self.sums = {n: _window(total, self.group, n) for n in GROUPS[self.group]["windows"]}

    def run(self, grads):
        rode = _run_plan(self.swap_plan(grads), "pair_swap_" + self.group)
        self.finish(_run_plan(self.exchange_plan(rode), "chip_exchange_" + self.group))
        return self.sums


class _Layer1Exchange(_GroupReduce):
    def __init__(self, w):
        super().__init__("layer1")
        self.half = _packed_weights(w, "layer1")

    def gather_plan(self):
        return _gather8_plan(self.half)

    def layer1_weights(self, rode):
        full = _unpacked_weights(rode[0], self.half, "layer1")
        return full["e_w_out"], full["o_g_in"], full["o_w_in"], full["o_w_out"]


def kernel(x, positions, e_g_in, e_w_in, e_g_q_a, e_w_q_up, e_g_kv_a, e_w_kv_up, e_sinks, e_w_out, o_g_in, o_w_in, o_b_f, o_w_out, g_final, loss_target, m_e_g_in, m_e_w_in, m_e_g_q_a, m_e_w_q_up, m_e_g_kv_a, m_e_w_kv_up, m_e_sinks, m_e_w_out, m_o_g_in, m_o_w_in, m_o_b_f, m_o_w_out, m_g_final, v_e_g_in, v_e_w_in, v_e_g_q_a, v_e_w_q_up, v_e_g_kv_a, v_e_w_kv_up, v_e_sinks, v_e_w_out, v_o_g_in, v_o_w_in, v_o_b_f, v_o_w_out, v_g_final):
    w = dict(e_g_in=e_g_in, e_w_in=e_w_in, e_g_q_a=e_g_q_a, e_w_q_up=e_w_q_up, e_g_kv_a=e_g_kv_a, e_w_kv_up=e_w_kv_up,
             e_sinks=e_sinks, e_w_out=e_w_out, o_g_in=o_g_in, o_w_in=o_w_in, o_b_f=o_b_f, o_w_out=o_w_out, g_final=g_final)
    m = dict(e_g_in=m_e_g_in, e_w_in=m_e_w_in, e_g_q_a=m_e_g_q_a, e_w_q_up=m_e_w_q_up, e_g_kv_a=m_e_g_kv_a,
             e_w_kv_up=m_e_w_kv_up, e_sinks=m_e_sinks, e_w_out=m_e_w_out, o_g_in=m_o_g_in, o_w_in=m_o_w_in, o_b_f=m_o_b_f,
             o_w_out=m_o_w_out, g_final=m_g_final)
    v = dict(e_g_in=v_e_g_in, e_w_in=v_e_w_in, e_g_q_a=v_e_g_q_a, e_w_q_up=v_e_w_q_up, e_g_kv_a=v_e_g_kv_a,
             e_w_kv_up=v_e_w_kv_up, e_sinks=v_e_sinks, e_w_out=v_e_w_out, o_g_in=v_o_g_in, o_w_in=v_o_w_in, o_b_f=v_o_b_f,
             o_w_out=v_o_w_out, g_final=v_g_final)
    order = ("e_g_in", "e_w_in", "e_g_q_a", "e_w_q_up", "e_g_kv_a", "e_w_kv_up", "e_sinks", "e_w_out", "o_g_in", "o_w_in",
             "o_b_f", "o_w_out", "g_final")
    half0 = _packed_weights(w, "layer0")
    full = _unpacked_weights(_run_plan(_gather8_plan(half0), "gather_weights_layer0")[0], half0, "layer0")
    layer1 = _Layer1Exchange(w)

    loss_part, dx, grads = _local_step(
        x[0], positions.reshape(-1, 1), loss_target[0], e_g_in, full["e_w_in"], e_g_q_a, full["e_w_q_up"], e_g_kv_a,
        full["e_w_kv_up"], e_sinks, o_b_f, g_final.reshape(1, D), layer1)
    loss = lax.psum(loss_part, ("x", "y", "c"))

    gsum = {**layer1.sums, **_GroupReduce("layer0").run(grads)}

    small = jnp.concatenate([jnp.pad(grads[n].reshape(-1), (0, (-grads[n].size) % LANES)) for n in REPLICATED])
    rows = small.shape[0] // LANES
    small = jnp.pad(small.reshape(rows, LANES), ((0, (-rows) % 8), (0, 0)))
    gathered_small = _fill_own_slot(_run_plan(_gather8_plan(small), "gather_small_grads")[0], small)
    ssum = _sum_leading(gathered_small, "small_grad_sum").reshape(-1)
    off = 0
    for n in REPLICATED:
        cnt = w[n].size
        gsum[n] = ssum[off:off + cnt].reshape(w[n].shape)
        off += cnt + (-cnt) % LANES

    grad, delta, new_m, new_v = {}, {}, {}, {}
    for n in order:
        if n == "o_w_in":
            def tiles(a):
                return jnp.transpose(a, (2, 0, 1)).reshape(-1, 8, LANES)

            outs = _adamw(tiles(w[n]), gsum[n].reshape(-1, 8, LANES), tiles(m[n]), tiles(v[n]), "adamw_" + n)
            grad[n] = _as_given(n, gsum[n], w[n].shape)
            delta[n], new_m[n], new_v[n] = (jnp.transpose(a.reshape(-1, 1, D), (1, 2, 0)) for a in outs)
        elif n in SHARDED:
            outs = _adamw(_as_handled(n, w[n]), gsum[n], _as_handled(n, m[n]), _as_handled(n, v[n]), "adamw_" + n)
            grad[n], delta[n], new_m[n], new_v[n] = (_as_given(n, a, w[n].shape) for a in (gsum[n],) + outs)
        else:
            grad[n] = gsum[n]
            delta[n], new_m[n], new_v[n] = _adamw(w[n], gsum[n], m[n], v[n], "adamw_" + n)
    return (loss, dx[None], *[grad[n] for n in order], *[delta[n] for n in order], *[new_m[n] for n in order],
            *[new_v[n] for n in order])
```

```python
import functools
import math

import numpy as np
import jax
import jax.numpy as jnp
from jax import lax
from jax.experimental import pallas as pl
from jax.experimental.pallas import tpu as pltpu

D = 1024
EPS = 1e-6
ROPE_THETA = 10000.0
N_MLA = 8
Q_RANK = 256
KV_RANK = 128
NOPE = 64
ROPE = 32
N_SWA = 8
WINDOW = 128
N_FOX = 16
HEAD = 64
E_SPLITS = (256, 128, 32, 512, 128, 128, 1024)
O_SPLITS = (1024, 1024, 1024, 16, 1024)
LR, B1, B2, AEPS, WD, STEP = 0.001, 0.9, 0.999, 1e-08, 0.01, 10

LANES = 128
HALF = 64
VMEM_LIMIT = 56 * 1024 * 1024
MXU = jnp.bfloat16
TOK = 256
WG_TOK = 2048
WG_ROWS = 1536
ATT = 256
FWD_CHUNK = 2
BWD_CHUNK = 2
SWA_GROUP = 4
NEG = float("-inf")

PACK_COLS = 1024
SUM_ROWS = 256
ADAM_TILE_BYTES = 2 << 20
MESH_ID = pl.DeviceIdType.MESH


def _pcall(body, *, name, vmem=VMEM_LIMIT, semantics=None, **kw):
    params = dict(vmem_limit_bytes=vmem)
    if semantics is not None:
        params["dimension_semantics"] = semantics
    return pl.pallas_call(body, name=name, compiler_params=pltpu.CompilerParams(**params), **kw)


def _mm(a, b):
    return jnp.dot(a.astype(MXU), b.astype(MXU), preferred_element_type=jnp.float32)


def _mm_nt(a, b):
    return lax.dot_general(a.astype(MXU), b.astype(MXU), (((1,), (1,)), ((), ())),
                           preferred_element_type=jnp.float32)


def _mm_tn(a, b):
    return lax.dot_general(a.astype(MXU), b.astype(MXU), (((0,), (0,)), ((), ())),
                           preferred_element_type=jnp.float32)


def _full(shape):
    n = len(shape)
    return pl.BlockSpec(shape, lambda *_: (0,) * n)


def _rows(tm, n):
    return pl.BlockSpec((tm, n), lambda i: (i, 0))


def _sds(shape, dtype):
    return jax.ShapeDtypeStruct(shape, dtype)


def _rms(x, g):
    r = lax.rsqrt(jnp.mean(x * x, axis=-1, keepdims=True) + EPS)
    return x * r * g


def _rms_bwd(x, g, dy):
    r = lax.rsqrt(jnp.mean(x * x, axis=-1, keepdims=True) + EPS)
    xh = x * r
    dxh = dy * g
    dx = r * (dxh - xh * jnp.mean(dxh * xh, axis=-1, keepdims=True))
    return dx, dy * xh


def _sigmoid(x):
    return 1.0 / (1.0 + jnp.exp(-x))


def _lane_masks(dtype=None):
    lane = lax.broadcasted_iota(jnp.int32, (1, LANES), 1)
    return lane < HALF


def _split_heads(a, lo):
    z = jnp.zeros_like(a)
    return [jnp.where(lo, a, z), jnp.where(lo, z, a)]


def _rope_consts():
    inv = np.zeros((8, LANES), np.float32)
    j = np.arange(ROPE // 2, dtype=np.float32)
    f = (1.0 / (ROPE_THETA ** (np.arange(0, ROPE, 2, dtype=np.float32) / ROPE))).astype(np.float32)
    inv[0, HALF:HALF + 16] = f
    inv[0, HALF + 16:HALF + 32] = f
    inv[1, HALF:HALF + 16] = -1.0
    inv[1, HALF + 16:HALF + 32] = 1.0
    del j
    return jnp.asarray(inv)


def _rope_tables(pos_f, consts):
    ang = pos_f * consts[0:1, :]
    sign = consts[1:2, :]
    c = jnp.where(sign != 0.0, jnp.cos(ang), 1.0)
    s = jnp.sin(ang) * sign
    return c, s


def _swap_halves(v, sign):
    lo = pltpu.roll(v, LANES - 16, axis=1)
    hi = pltpu.roll(v, 16, axis=1)
    return jnp.where(sign < 0.0, lo, jnp.where(sign > 0.0, hi, 0.0))


def _rope(x, c, s, sign):
    return x * c + _swap_halves(x, sign) * s


def _rope_t(dy, c, s, sign):
    return dy * c + _swap_halves(dy * s, sign)


def _layer0_in(x, pos, g_in, w_in, g_q, w_q, g_kv, w_kv):
    S = x.shape[0]
    consts = _rope_consts()

    def body(x_ref, pos_ref, c_ref, g_ref, w_ref, gq_ref, wq_ref, gkv_ref, wkv_ref,
             h_ref, cq_ref, ckv_ref, cqn_ref, ckvn_ref, qm_ref, km_ref, vm_ref,
             qs_ref, kd_ref, vd_ref, gate_ref, cos_ref, sin_ref):
        h = _rms(x_ref[...], g_ref[...])
        h_ref[...] = h.astype(h_ref.dtype)
        z = _mm_nt(h, w_ref[...])
        cq = z[:, 0:256]
        ckv = z[:, 256:384]
        kpe = z[:, 384:512]
        cq_ref[...] = cq
        ckv_ref[...] = ckv
        qs_ref[...] = z[:, 512:1024].astype(qs_ref.dtype)
        kd_ref[...] = z[:, 1024:1536].astype(kd_ref.dtype)
        vd_ref[...] = z[:, 1536:2048].astype(vd_ref.dtype)
        gate_ref[...] = z[:, 2048:3072]
        cqn = _rms(cq, gq_ref[...])
        ckvn = _rms(ckv, gkv_ref[...])
        cqn_ref[...] = cqn.astype(cqn_ref.dtype)
        ckvn_ref[...] = ckvn.astype(ckvn_ref.dtype)
        q = _mm_nt(cqn, wq_ref[...])
        kv = _mm(ckvn, wkv_ref[...])
        vm_ref[...] = kv[:, 1024:1536].astype(vm_ref.dtype)
        consts_v = c_ref[...]
        sign = consts_v[1:2, :]
        c, s = _rope_tables(pos_ref[...].astype(jnp.float32), consts_v)
        cos_ref[...] = c
        sin_ref[...] = s
        kpe_r = _rope(kpe, c, s, sign)
        for hd in range(N_MLA):
            sl = slice(LANES * hd, LANES * (hd + 1))
            qm_ref[:, sl] = _rope(q[:, sl], c, s, sign).astype(qm_ref.dtype)
            km_ref[:, sl] = (kv[:, sl] + kpe_r).astype(km_ref.dtype)

    outs = [
        ((S, D), MXU), ((S, 256), jnp.float32), ((S, 128), jnp.float32), ((S, 256), MXU), ((S, 128), MXU),
        ((S, 1024), MXU), ((S, 1024), MXU), ((S, 512), MXU), ((S, 512), MXU), ((S, 512), MXU), ((S, 512), MXU),
        ((S, 1024), jnp.float32), ((S, 128), jnp.float32), ((S, 128), jnp.float32),
    ]
    return _pcall(
        body, name="layer0_in", grid=(S // TOK,), semantics=("arbitrary",),
        in_specs=[_rows(TOK, D), _rows(TOK, 1), _full((8, LANES)), _full((1, D)), _full(w_in.shape), _full((1, 256)),
                  _full(w_q.shape), _full((1, 128)), _full(w_kv.shape)],
        out_specs=[_rows(TOK, s[1]) for s, _ in outs],
        out_shape=[_sds(s, d) for s, d in outs],
    )(x, pos, consts, g_in, w_in, g_q, w_q, g_kv, w_kv)


AUG = (HALF, 0)
ONE = (HALF + 8, 8)


def _data_lanes(idx, h):
    return (idx < HALF) if h == 0 else (idx >= HALF)


def _three_terms(x):
    hi = x.astype(MXU).astype(jnp.float32)
    mid = (x - hi).astype(MXU).astype(jnp.float32)
    lo = (x - hi - mid).astype(MXU).astype(jnp.float32)
    return hi, mid, lo


def _q_aug(qblk, lc, h, scale, lane):
    a = AUG[h]
    hi, mid, lo = _three_terms(lc)
    ones = ((lane >= a + 3) & (lane <= a + 5)).astype(jnp.float32)
    aug = jnp.where(lane == a, hi, jnp.where(lane == a + 1, mid, jnp.where(lane == a + 2, lo, ones)))
    return jnp.where(_data_lanes(lane, h), qblk * jnp.asarray(scale, qblk.dtype), aug.astype(qblk.dtype))


def _k_aug(kblk, lc, h, lane):
    a = AUG[h]
    hi, mid, lo = _three_terms(-lc)
    ones = ((lane >= a) & (lane <= a + 2)).astype(jnp.float32)
    aug = jnp.where(lane == a + 3, hi, jnp.where(lane == a + 4, mid, jnp.where(lane == a + 5, lo, ones)))
    return jnp.where(_data_lanes(lane, h), kblk, aug.astype(kblk.dtype))


def _attn_fwd_t(q, k, v, scale, *, split, name, lcc=None, plan=None):
    S = q.shape[0]
    npair = v.shape[1] // LANES
    W = 2 * LANES if split else LANES
    T = ATT
    CH = FWD_CHUNK * T
    assert S % CH == 0
    nq = S // T

    def body(*refs):
        if split:
            q_ref, k_ref, v_ref, o_ref, lse_ref, vt, acc, m_sc = refs
        else:
            q_ref, k_ref, v_ref, lcc_ref, o_ref, lse_ref, kaug, vt, acc, m_sc = refs
        lane = lax.broadcasted_iota(jnp.int32, (1, LANES), 1)
        sub = lax.broadcasted_iota(jnp.int32, (LANES, 1), 0)
        key_minus_qry = lax.broadcasted_iota(jnp.int32, (CH, T), 0) - lax.broadcasted_iota(jnp.int32, (CH, T), 1)

        def prep(i, c):
            r0 = pl.multiple_of(i * T, T)
            vblk = v_ref[pl.ds(r0, T), :].astype(jnp.float32)
            for h in (0, 1):
                vh = jnp.where(_data_lanes(lane, h), vblk, (lane == ONE[h]).astype(jnp.float32))
                vt[h, :, pl.ds(r0, T)] = vh.T.astype(vt.dtype)
                if not split:
                    kaug[h, pl.ds(r0, T), :] = _k_aug(k_ref[pl.ds(r0, T), :], lcc_ref[h, pl.ds(r0, T), :], h, lane)
            return c

        lax.fori_loop(0, nq, prep, 0)

        def queries(qi):
            q0 = pl.multiple_of(qi * T, T)
            qblk = q_ref[pl.ds(q0, T), :]
            if split:
                return (qblk[:, :LANES], qblk[:, LANES:])
            return tuple(_q_aug(qblk, lcc_ref[h, pl.ds(q0, T), :], h, scale, lane) for h in (0, 1))

        def scores(qs, c):
            k0 = pl.multiple_of(c * CH, CH)
            out = []
            for h in (0, 1):
                if split:
                    out.append(_mm_nt(k_ref[pl.ds(k0, CH), LANES * h:LANES * (h + 1)], qs[h]) * scale)
                else:
                    out.append(_mm_nt(kaug[h, pl.ds(k0, CH), :], qs[h]))
            return tuple(out)

        def q_block(qi, carry):
            qs, first_scores = carry[:2], carry[2:]
            q0 = pl.multiple_of(qi * T, T)
            acc[...] = jnp.zeros_like(acc)
            m_sc[...] = jnp.full(m_sc.shape, NEG, jnp.float32)

            def absorb(c, sts, masked):
                k0 = pl.multiple_of(c * CH, CH)
                for h in (0, 1):
                    st = sts[h]
                    if masked:
                        st = jnp.where(key_minus_qry <= q0 - k0, st, NEG)
                    m_old = m_sc[h:h + 1, :]
                    m_new = jnp.maximum(m_old, jnp.max(st, axis=0, keepdims=True))
                    alpha = jnp.exp(m_old - m_new)
                    pt = jnp.exp(st - m_new)
                    acc[h] = alpha * acc[h] + _mm(vt[h, :, pl.ds(k0, CH)], pt)
                    m_sc[h:h + 1, :] = m_new

            last = qi // FWD_CHUNK

            def pipelined(c, sts):
                nxt = scores(qs, c + 1)
                absorb(c, sts, False)
                return nxt

            sts = lax.fori_loop(0, last, pipelined, first_scores)
            qs_next = queries(jnp.minimum(qi + 1, nq - 1))
            nxt = qs_next + scores(qs_next, 0)
            absorb(last, sts, True)
            ot = None
            for h in (0, 1):
                a = acc[h]
                l = a[ONE[h]:ONE[h] + 1, :]
                oh = jnp.where(_data_lanes(sub, h), a * (1.0 / l), 0.0)
                ot = oh if ot is None else ot + oh
                lse_ref[0, h:h + 1, pl.ds(q0, T)] = m_sc[h:h + 1, :] + jnp.log(l)
            o_ref[pl.ds(q0, T), :] = ot.T
            return nxt

        qs0 = queries(0)
        lax.fori_loop(0, nq, q_block, qs0 + scores(qs0, 0))

    wide = pl.BlockSpec((S, W), lambda j: (0, j))
    slab = pl.BlockSpec((S, LANES), lambda j: (0, j))
    rows = pl.BlockSpec((1, 2, S), lambda j: (j, 0, 0))
    in_specs = [wide, wide, slab]
    args = [q, k, v]
    scratch = []
    if not split:
        in_specs.append(pl.BlockSpec((2, S, 1), lambda j: (j, 0, 0)))
        args.append(lcc)
        scratch.append(pltpu.VMEM((2, S, LANES), MXU))
    scratch += [pltpu.VMEM((2, LANES, S), MXU), pltpu.VMEM((2, LANES, T), jnp.float32), pltpu.VMEM((8, T), jnp.float32)]
    (o, lse), rode = _pcall_riding(
        body, plan, args, name=name, grid=(npair,), in_specs=in_specs, out_specs=[slab, rows],
        out_shape=[_sds((S, npair * LANES), jnp.float32), _sds((npair, 2, S), jnp.float32)], scratch_shapes=scratch)
    return o, lse, rode


def _attn_bwd_t(q, k, v, do, o, lse, scale, *, split, name, lcc=None, plan=None):
    S = q.shape[0]
    npair = v.shape[1] // LANES
    W = 2 * LANES if split else LANES
    T = ATT
    CH = BWD_CHUNK * T
    assert S % CH == 0
    nq = S // T

    def body(*refs):
        if split:
            (q_ref, k_ref, v_ref, do_ref, o_ref, lse_ref, dq_ref, dk_ref, dv_ref, dqt, delta, dk_acc, dv_acc) = refs
        else:
            (q_ref, k_ref, v_ref, do_ref, o_ref, lse_ref, lcc_ref, dq_ref, dk_ref, dv_ref, dlc_ref,
             dqt, delta, dk_acc, dv_acc, qaug, csum) = refs
        lane = lax.broadcasted_iota(jnp.int32, (1, LANES), 1)
        sub = lax.broadcasted_iota(jnp.int32, (LANES, 1), 0)
        key_minus_qry = lax.broadcasted_iota(jnp.int32, (T, CH), 0) - lax.broadcasted_iota(jnp.int32, (T, CH), 1)

        def prep(i, c):
            r0 = pl.multiple_of(i * T, T)
            prod_t = (do_ref[pl.ds(r0, T), :].astype(jnp.float32) * o_ref[pl.ds(r0, T), :]).T
            for h in (0, 1):
                delta[h:h + 1, pl.ds(r0, T)] = jnp.sum(jnp.where(_data_lanes(sub, h), prod_t, 0.0), axis=0, keepdims=True)
                dqt[h, :, pl.ds(r0, T)] = jnp.zeros((LANES, T), jnp.float32)
                if not split:
                    qaug[h, pl.ds(r0, T), :] = _q_aug(q_ref[pl.ds(r0, T), :], lcc_ref[h, pl.ds(r0, T), :], h, scale, lane)
            return c

        lax.fori_loop(0, nq, prep, 0)

        def keys(ki):
            k0 = pl.multiple_of(ki * T, T)
            kblk = k_ref[pl.ds(k0, T), :]
            if split:
                return (kblk[:, :LANES], kblk[:, LANES:])
            return tuple(_k_aug(kblk, lcc_ref[h, pl.ds(k0, T), :], h, lane) for h in (0, 1))

        def q_of(c, h):
            q0 = pl.multiple_of(c * CH, CH)
            if split:
                return q_ref[pl.ds(q0, CH), LANES * h:LANES * (h + 1)]
            return qaug[h, pl.ds(q0, CH), :]

        def scores(khs, c):
            out = []
            for h in (0, 1):
                st = _mm_nt(khs[h], q_of(c, h))
                out.append(st * scale if split else st)
            return tuple(out)

        def k_block(ki, carry):
            khs, first_scores = carry[:2], carry[2:]
            k0 = pl.multiple_of(ki * T, T)
            khts = [kh.astype(jnp.float32).T.astype(kh.dtype) for kh in khs]
            vhs = _split_heads(v_ref[pl.ds(k0, T), :], lane < HALF)
            dk_acc[...] = jnp.zeros_like(dk_acc)
            dv_acc[...] = jnp.zeros_like(dv_acc)

            def absorb(c, vals):
                q0 = pl.multiple_of(c * CH, CH)
                dos = _split_heads(do_ref[pl.ds(q0, CH), :], lane < HALF)
                visible = key_minus_qry <= q0 - k0
                for h in (0, 1):
                    dpt = _mm_nt(vhs[h], dos[h])
                    st = jnp.where(visible, vals[h], NEG)
                    pt = jnp.exp(st - lse_ref[0, h:h + 1, pl.ds(q0, CH)])
                    dv_acc[...] += _mm(pt, dos[h])
                    dst = pt * (dpt - delta[h:h + 1, pl.ds(q0, CH)])
                    dk_acc[h] += _mm(dst, q_of(c, h))
                    dqt[h, :, pl.ds(q0, CH)] += _mm(khts[h], dst)

            first = ki // BWD_CHUNK

            def pipelined(c, vals):
                nxt = scores(khs, c + 1)
                absorb(c, vals)
                return nxt

            vals = lax.fori_loop(first, S // CH - 1, pipelined, first_scores)
            kn = jnp.minimum(ki + 1, nq - 1)
            khs_next = keys(kn)
            nxt = khs_next + scores(khs_next, kn // BWD_CHUNK)
            absorb(S // CH - 1, vals)
            if split:
                dk_ref[pl.ds(k0, T), :LANES] = (dk_acc[0] * scale).astype(dk_ref.dtype)
                dk_ref[pl.ds(k0, T), LANES:] = (dk_acc[1] * scale).astype(dk_ref.dtype)
            else:
                dk_ref[pl.ds(k0, T), :] = jnp.where(lane < HALF, dk_acc[0], dk_acc[1]).astype(dk_ref.dtype)
                for h in (0, 1):
                    csum[h:h + 1, pl.ds(k0, T)] = dk_acc[h].T[AUG[h] + 3:AUG[h] + 4, :]
            dv_ref[pl.ds(k0, T), :] = dv_acc[...].astype(dv_ref.dtype)
            return nxt

        khs0 = keys(0)
        lax.fori_loop(0, nq, k_block, khs0 + scores(khs0, 0))

        def finish(i, c):
            r0 = pl.multiple_of(i * T, T)
            if split:
                for h in (0, 1):
                    dq_ref[pl.ds(r0, T), LANES * h:LANES * (h + 1)] = (dqt[h, :, pl.ds(r0, T)].T * scale).astype(dq_ref.dtype)
            else:
                d = jnp.where(sub < HALF, dqt[0, :, pl.ds(r0, T)], dqt[1, :, pl.ds(r0, T)])
                dq_ref[pl.ds(r0, T), :] = (d.T * scale).astype(dq_ref.dtype)
                for h in (0, 1):
                    dlc_ref[0, h:h + 1, pl.ds(r0, T)] = dqt[h, AUG[h]:AUG[h] + 1, pl.ds(r0, T)] - csum[h:h + 1, pl.ds(r0, T)]
            return c

        lax.fori_loop(0, nq, finish, 0)

    wide = pl.BlockSpec((S, W), lambda j: (0, j))
    slab = pl.BlockSpec((S, LANES), lambda j: (0, j))
    rows = pl.BlockSpec((1, 2, S), lambda j: (j, 0, 0))
    in_specs = [wide, wide, slab, slab, slab, rows]
    args = [q, k, v, do, o, lse]
    out_specs = [wide, wide, slab]
    out_shape = [_sds(q.shape, jnp.float32 if split else do.dtype), _sds(k.shape, jnp.float32 if split else do.dtype),
                 _sds(v.shape, do.dtype)]
    scratch = [pltpu.VMEM((2, LANES, S), jnp.float32), pltpu.VMEM((8, S), jnp.float32),
               pltpu.VMEM((2, T, LANES), jnp.float32), pltpu.VMEM((T, LANES), jnp.float32)]
    if not split:
        in_specs.append(pl.BlockSpec((2, S, 1), lambda j: (j, 0, 0)))
        args.append(lcc)
        out_specs.append(rows)
        out_shape.append(_sds((npair, 2, S), jnp.float32))
        scratch += [pltpu.VMEM((2, S, LANES), MXU), pltpu.VMEM((8, S), jnp.float32)]
    outs, rode = _pcall_riding(body, plan, args, name=name, grid=(npair,), in_specs=in_specs, out_specs=out_specs,
                               out_shape=out_shape, scratch_shapes=scratch)
    return (*outs, rode)


def _swa_bias(slope, shift):
    a = lax.broadcasted_iota(jnp.int32, (WINDOW, 2 * WINDOW), 0)
    c = lax.broadcasted_iota(jnp.int32, (WINDOW, 2 * WINDOW), 1)
    dist = a - c + shift
    return jnp.where((dist >= 0) & (dist < WINDOW), -slope * dist.astype(jnp.float32), NEG)


def _swa_scores(qh, kblk, bias):
    return _mm_nt(qh, kblk) * (HEAD ** -0.5) + bias


def _swa_fwd(q, kd, vd, sinks, slopes):
    S = q.shape[0]
    npair = q.shape[1] // LANES
    nb = S // WINDOW

    def body(sink_ref, slope_ref, q_ref, k_ref, v_ref, o_ref, lse_ref):
        j = pl.program_id(0)
        lo = _lane_masks()
        biases = [(_swa_bias(slope_ref[2 * j + h], 0), _swa_bias(slope_ref[2 * j + h], WINDOW)) for h in (0, 1)]

        def q_block(qi, c):
            q0 = pl.multiple_of(qi * WINDOW, WINDOW)
            k0 = pl.multiple_of(jnp.maximum(qi - 1, 0) * WINDOW, WINDOW)
            qs = _split_heads(q_ref[pl.ds(q0, WINDOW), :], lo)
            kblk = k_ref[pl.ds(k0, 2 * WINDOW), :]
            vs = _split_heads(v_ref[pl.ds(k0, 2 * WINDOW), :], lo)
            o = None
            for h in (0, 1):
                sink = sink_ref[2 * j + h]
                s = _swa_scores(qs[h], kblk, jnp.where(qi == 0, *biases[h]))
                m = jnp.maximum(jnp.max(s, axis=1, keepdims=True), sink)
                p = jnp.exp(s - m)
                den = jnp.sum(p, axis=1, keepdims=True) + jnp.exp(sink - m)
                oh = _mm(p / den, vs[h])
                o = oh if o is None else o + oh
                lse_ref[h, pl.ds(q0, WINDOW), :] = m + jnp.log(den)
            o_ref[pl.ds(q0, WINDOW), :] = o
            return c

        def q_group(gi, c):
            for g in range(SWA_GROUP):
                q_block(gi * SWA_GROUP + g, c)
            return c

        lax.fori_loop(0, nb // SWA_GROUP, q_group, 0)

    smem = pl.BlockSpec(memory_space=pltpu.SMEM)
    slab = pl.BlockSpec((S, LANES), lambda j: (0, j))
    return _pcall(
        body, name="swa_fwd", grid=(npair,), semantics=("arbitrary",),
        in_specs=[smem, smem, slab, slab, slab],
        out_specs=[slab, pl.BlockSpec((2, S, 1), lambda j: (j, 0, 0))],
        out_shape=[_sds((S, npair * LANES), jnp.float32), _sds((2 * npair, S, 1), jnp.float32)],
    )(sinks, slopes, q, kd, vd)


def _swa_bwd(q, kd, vd, do, o, lse, sinks, slopes, plan=None):
    S = q.shape[0]
    npair = q.shape[1] // LANES
    nb = S // WINDOW

    def body(sink_ref, slope_ref, q_ref, k_ref, v_ref, do_ref, o_ref, lse_ref,
             dq_ref, dk_ref, dv_ref, dsink_ref, dk_acc, dv_acc):
        j = pl.program_id(0)
        lo = _lane_masks()
        dk_acc[...] = jnp.zeros_like(dk_acc)
        dv_acc[...] = jnp.zeros_like(dv_acc)
        biases = [(_swa_bias(slope_ref[2 * j + h], 0), _swa_bias(slope_ref[2 * j + h], WINDOW)) for h in (0, 1)]

        def q_block(qi, carry):
            q0 = pl.multiple_of(qi * WINDOW, WINDOW)
            k0 = pl.multiple_of(jnp.maximum(qi - 1, 0) * WINDOW, WINDOW)
            qs = _split_heads(q_ref[pl.ds(q0, WINDOW), :], lo)
            dos = _split_heads(do_ref[pl.ds(q0, WINDOW), :], lo)
            oblk = o_ref[pl.ds(q0, WINDOW), :]
            kblk = k_ref[pl.ds(k0, 2 * WINDOW), :]
            vblk = v_ref[pl.ds(k0, 2 * WINDOW), :]
            ks = _split_heads(kblk, lo)
            dq = None
            out = []
            for h in (0, 1):
                sink = sink_ref[2 * j + h]
                lse_h = lse_ref[h, pl.ds(q0, WINDOW), :]
                s = _swa_scores(qs[h], kblk, jnp.where(qi == 0, *biases[h]))
                p = jnp.exp(s - lse_h)
                delta = jnp.sum(dos[h].astype(jnp.float32) * oblk, axis=1, keepdims=True)
                dv_acc[pl.ds(k0, 2 * WINDOW), :] += _mm_tn(p, dos[h])
                dp = _mm_nt(dos[h], vblk)
                ds = p * (dp - delta)
                dqh = _mm(ds, ks[h]) * (HEAD ** -0.5)
                dq = dqh if dq is None else dq + dqh
                dk_acc[pl.ds(k0, 2 * WINDOW), :] += _mm_tn(ds, qs[h]) * (HEAD ** -0.5)
                dsk = jnp.sum(-jnp.exp(sink - lse_h) * delta, axis=0, keepdims=True)
                out.append(carry[h] + dsk)
            dq_ref[pl.ds(q0, WINDOW), :] = dq.astype(dq_ref.dtype)
            return tuple(out)

        def q_group(gi, carry):
            for g in range(SWA_GROUP):
                carry = q_block(gi * SWA_GROUP + g, carry)
            return carry

        zero = jnp.zeros((1, 1), jnp.float32)
        dsa, dsb = lax.fori_loop(0, nb // SWA_GROUP, q_group, (zero, zero))
        dk_ref[...] = dk_acc[...].astype(dk_ref.dtype)
        dv_ref[...] = dv_acc[...].astype(dv_ref.dtype)
        r = lax.broadcasted_iota(jnp.int32, (8, LANES), 0)
        dsink_ref[0] = jnp.where(r == 0, dsa, jnp.where(r == 1, dsb, 0.0))

    smem = pl.BlockSpec(memory_space=pltpu.SMEM)
    slab = pl.BlockSpec((S, LANES), lambda j: (0, j))
    outs, rode = _pcall_riding(
        body, plan, [sinks, slopes, q, kd, vd, do, o, lse], name="swa_bwd", grid=(npair,),
        in_specs=[smem, smem, slab, slab, slab, slab, slab, pl.BlockSpec((2, S, 1), lambda j: (j, 0, 0))],
        out_specs=[slab, slab, slab, pl.BlockSpec((1, 8, LANES), lambda j: (j, 0, 0))],
        out_shape=[_sds(q.shape, do.dtype), _sds(kd.shape, do.dtype), _sds(vd.shape, do.dtype),
                   _sds((npair, 8, LANES), jnp.float32)],
        scratch_shapes=[pltpu.VMEM((S, LANES), jnp.float32), pltpu.VMEM((S, LANES), jnp.float32)])
    return (*outs, rode)


def _log_steps(S):
    k, out = 1, []
    while k < S:
        out.append(k)
        k *= 2
    return out


def _forget_fwd(f_row, b_col):
    S = f_row.shape[1]

    def body(f_ref, b_ref, lc_ref):
        x = f_ref[...] + b_ref[...]
        lc = jnp.minimum(x, 0.0) - jnp.log(1.0 + jnp.exp(-jnp.abs(x)))
        idx = lax.broadcasted_iota(jnp.int32, lc.shape, 1)
        for k in _log_steps(S):
            lc = lc + jnp.where(idx >= k, pltpu.roll(lc, k, axis=1), 0.0)
        lc_ref[...] = lc

    return _pcall(body, name="forget_fwd", out_shape=_sds(f_row.shape, jnp.float32))(f_row, b_col)


def _forget_bwd(dlc_row, f_row, b_col):
    S = f_row.shape[1]

    def body(d_ref, f_ref, b_ref, df_ref, db_ref):
        g = d_ref[...]
        idx = lax.broadcasted_iota(jnp.int32, g.shape, 1)
        for k in _log_steps(S):
            g = g + jnp.where(idx < S - k, pltpu.roll(g, S - k, axis=1), 0.0)
        x = f_ref[...] + b_ref[...]
        df = g * _sigmoid(-x)
        df_ref[...] = df
        db_ref[...] = jnp.sum(df, axis=1, keepdims=True)

    return _pcall(body, name="forget_bwd",
                  out_shape=[_sds(f_row.shape, jnp.float32), _sds((f_row.shape[0], 1), jnp.float32)])(dlc_row, f_row, b_col)


def _layer0_out_layer1_in(x, o_m, o_s, gate, w_out, g1, w_in1):
    S = x.shape[0]

    def body(x_ref, om_ref, os_ref, gate_ref, wo_ref, g_ref, w_ref,
             x1_ref, u_ref, h_ref, q_ref, k_ref, v_ref, g1_ref, f_ref):
        gt = gate_ref[...]
        sg = gt * _sigmoid(gt)
        um = om_ref[...] * sg[:, :512]
        us = os_ref[...] * sg[:, 512:]
        u_ref[:, :512] = um.astype(u_ref.dtype)
        u_ref[:, 512:] = us.astype(u_ref.dtype)
        x1 = x_ref[...] + _mm(um, wo_ref[0:512, :]) + _mm(us, wo_ref[512:1024, :])
        x1_ref[...] = x1
        h = _rms(x1, g_ref[...])
        h_ref[...] = h.astype(h_ref.dtype)
        z = _mm_nt(h, w_ref[...])
        q_ref[...] = z[:, 0:1024].astype(q_ref.dtype)
        k_ref[...] = z[:, 1024:2048].astype(k_ref.dtype)
        v_ref[...] = z[:, 2048:3072].astype(v_ref.dtype)
        g1_ref[...] = z[:, 3072:4096]
        f_ref[...] = z[:, 4096:4224]

    outs = [((S, D), jnp.float32), ((S, D), MXU), ((S, D), MXU), ((S, D), MXU), ((S, D), MXU), ((S, D), MXU),
            ((S, D), jnp.float32), ((S, LANES), jnp.float32)]
    return _pcall(
        body, name="layer0_out_layer1_in", grid=(S // TOK,), semantics=("arbitrary",),
        in_specs=[_rows(TOK, D), _rows(TOK, 512), _rows(TOK, 512), _rows(TOK, D), _full((D, D)), _full((1, D)),
                  _full(w_in1.shape)],
        out_specs=[_rows(TOK, s[1]) for s, _ in outs],
        out_shape=[_sds(s, d) for s, d in outs],
    )(x, o_m, o_s, gate, w_out, g1, w_in1)


def _head(x1, o1, gate1, w_out1, g_f, target):
    S = x1.shape[0]

    def body(x1_ref, o_ref, gate_ref, wo_ref, g_ref, t_ref,
             loss_ref, dgf_ref, dx2_ref, u_ref, do_ref, dgate_ref):
        i = pl.program_id(0)
        gt = gate_ref[...]
        sig = _sigmoid(gt)
        sg = gt * sig
        o = o_ref[...]
        u = o * sg
        u_ref[...] = u.astype(u_ref.dtype)
        x2 = x1_ref[...] + _mm(u, wo_ref[...])
        g = g_ref[...]
        y = _rms(x2, g)
        err = y - t_ref[...]
        part = 0.5 * jnp.sum(jnp.mean(err * err, axis=-1, keepdims=True), axis=0, keepdims=True)
        dy = err * (1.0 / D)
        dx2, dg_rows = _rms_bwd(x2, g, dy)
        dx2_ref[...] = dx2
        du = _mm_nt(dx2, wo_ref[...])
        do_ref[...] = (du * sg).astype(do_ref.dtype)
        dgate_ref[...] = (du * o * (sig * (1.0 + gt * (1.0 - sig)))).astype(dgate_ref.dtype)

        @pl.when(i == 0)
        def _():
            loss_ref[...] = jnp.zeros_like(loss_ref)
            dgf_ref[...] = jnp.zeros_like(dgf_ref)

        loss_ref[...] += jnp.broadcast_to(part, loss_ref.shape)
        dgf_ref[...] += jnp.sum(dg_rows, axis=0, keepdims=True)

    outs = [((S, D), jnp.float32), ((S, D), MXU), ((S, D), MXU), ((S, D), MXU)]
    return _pcall(
        body, name="head", grid=(S // TOK,), semantics=("arbitrary",),
        in_specs=[_rows(TOK, D), _rows(TOK, D), _rows(TOK, D), _full((D, D)), _full((1, D)), _rows(TOK, D)],
        out_specs=[_full((8, LANES)), _full((1, D))] + [_rows(TOK, D) for _ in outs],
        out_shape=[_sds((8, LANES), jnp.float32), _sds((1, D), jnp.float32)] + [_sds(s, d) for s, d in outs],
    )(x1, o1, gate1, w_out1, g_f, target)


def _layer1_in_bwd(dq, dk, dv, dgate1, df, x1, dx2, g1, w_in1, gate0, o_m, o_s, w_out0):
    S = x1.shape[0]

    def body(dq_ref, dk_ref, dv_ref, dg1_ref, df_ref, x1_ref, dx2_ref, g_ref, w_ref, gate_ref, om_ref, os_ref,
             wo_ref, dz_ref, dx1_ref, dgn_ref, dom_ref, dos_ref, dgate_ref):
        i = pl.program_id(0)
        dz_ref[:, 0:1024] = dq_ref[...]
        dz_ref[:, 1024:2048] = dk_ref[...]
        dz_ref[:, 2048:3072] = dv_ref[...]
        dz_ref[:, 3072:4096] = dg1_ref[...]
        dz_ref[:, 4096:4224] = df_ref[...]
        dh = _mm(dz_ref[...], w_ref[...])
        g = g_ref[...]
        dxn, dg_rows = _rms_bwd(x1_ref[...], g, dh)
        dx1 = dx2_ref[...] + dxn
        dx1_ref[...] = dx1
        du = _mm_nt(dx1, wo_ref[...])
        gt = gate_ref[...]
        sig = _sigmoid(gt)
        sg = gt * sig
        dsg = sig * (1.0 + gt * (1.0 - sig))
        dom_ref[...] = (du[:, :512] * sg[:, :512]).astype(dom_ref.dtype)
        dos_ref[...] = (du[:, 512:] * sg[:, 512:]).astype(dos_ref.dtype)
        dgate_ref[:, :512] = (du[:, :512] * om_ref[...] * dsg[:, :512]).astype(dgate_ref.dtype)
        dgate_ref[:, 512:] = (du[:, 512:] * os_ref[...] * dsg[:, 512:]).astype(dgate_ref.dtype)

        @pl.when(i == 0)
        def _():
            dgn_ref[...] = jnp.zeros_like(dgn_ref)

        dgn_ref[...] += jnp.sum(dg_rows, axis=0, keepdims=True)

    return _pcall(
        body, name="layer1_in_bwd", grid=(S // TOK,), semantics=("arbitrary",),
        in_specs=[_rows(TOK, D), _rows(TOK, D), _rows(TOK, D), _rows(TOK, D), _rows(TOK, LANES), _rows(TOK, D),
                  _rows(TOK, D), _full((1, D)), _full(w_in1.shape), _rows(TOK, D), _rows(TOK, 512), _rows(TOK, 512),
                  _full((D, D))],
        out_specs=[_rows(TOK, 4224), _rows(TOK, D), _full((1, D)), _rows(TOK, 512), _rows(TOK, 512), _rows(TOK, D)],
        out_shape=[_sds((S, 4224), MXU), _sds((S, D), jnp.float32), _sds((1, D), jnp.float32),
                   _sds((S, 512), MXU), _sds((S, 512), MXU), _sds((S, D), MXU)],
    )(dq, dk, dv, dgate1, df, x1, dx2, g1, w_in1, gate0, o_m, o_s, w_out0)


def _layer0_in_bwd(dqm, dkm, dvm, dqs, dkd, dvd, dgate0, cos, sin, cq, ckv, x, dx1, g_in, w_in, g_q, w_q, g_kv, w_kv):
    S = x.shape[0]
    consts = _rope_consts()

    def body(dqm_ref, dkm_ref, dvm_ref, dqs_ref, dkd_ref, dvd_ref, dgate_ref, cos_ref, sin_ref, c_ref, cq_ref, ckv_ref,
             x_ref, dx1_ref, g_ref, w_ref, gq_ref, wq_ref, gkv_ref, wkv_ref,
             dx_ref, dz_ref, dqu_ref, dkvu_ref, dgin_ref, dgq_ref, dgkv_ref):
        i = pl.program_id(0)
        lo = _lane_masks()
        sign = c_ref[...][1:2, :]
        c = cos_ref[...]
        s = sin_ref[...]
        dkpe = None
        for hd in range(N_MLA):
            sl = slice(LANES * hd, LANES * (hd + 1))
            dqu_ref[:, sl] = _rope_t(dqm_ref[:, sl], c, s, sign).astype(dqu_ref.dtype)
            dkh = dkm_ref[:, sl]
            dkvu_ref[:, sl] = jnp.where(lo, dkh, 0.0).astype(dkvu_ref.dtype)
            dkpe = dkh if dkpe is None else dkpe + dkh
        dkvu_ref[:, 1024:1536] = dvm_ref[...]
        dkpe = _rope_t(jnp.where(lo, 0.0, dkpe), c, s, sign)
        dcqn = _mm(dqu_ref[...], wq_ref[...])
        dckvn = _mm_nt(dkvu_ref[...], wkv_ref[...])
        gq = gq_ref[...]
        gkv = gkv_ref[...]
        dcq, dgq_rows = _rms_bwd(cq_ref[...], gq, dcqn)
        dckv, dgkv_rows = _rms_bwd(ckv_ref[...], gkv, dckvn)
        dz_ref[:, 0:256] = dcq.astype(dz_ref.dtype)
        dz_ref[:, 256:384] = dckv.astype(dz_ref.dtype)
        dz_ref[:, 384:512] = dkpe.astype(dz_ref.dtype)
        dz_ref[:, 512:1024] = dqs_ref[...]
        dz_ref[:, 1024:1536] = dkd_ref[...]
        dz_ref[:, 1536:2048] = dvd_ref[...]
        dz_ref[:, 2048:3072] = dgate_ref[...]
        dh = _mm(dz_ref[...], w_ref[...])
        g = g_ref[...]
        dxn, dg_rows = _rms_bwd(x_ref[...], g, dh)
        dx_ref[...] = dx1_ref[...] + dxn

        @pl.when(i == 0)
        def _():
            dgin_ref[...] = jnp.zeros_like(dgin_ref)
            dgq_ref[...] = jnp.zeros_like(dgq_ref)
            dgkv_ref[...] = jnp.zeros_like(dgkv_ref)

        dgin_ref[...] += jnp.sum(dg_rows, axis=0, keepdims=True)
        dgq_ref[...] += jnp.sum(dgq_rows, axis=0, keepdims=True)
        dgkv_ref[...] += jnp.sum(dgkv_rows, axis=0, keepdims=True)

    return _pcall(
        body, name="layer0_in_bwd", grid=(S // TOK,), semantics=("arbitrary",),
        in_specs=[_rows(TOK, 1024), _rows(TOK, 1024), _rows(TOK, 512), _rows(TOK, 512), _rows(TOK, 512), _rows(TOK, 512),
                  _rows(TOK, D), _rows(TOK, LANES), _rows(TOK, LANES), _full((8, LANES)), _rows(TOK, 256), _rows(TOK, 128),
                  _rows(TOK, D), _rows(TOK, D), _full((1, D)), _full(w_in.shape), _full((1, 256)), _full(w_q.shape),
                  _full((1, 128)), _full(w_kv.shape)],
        out_specs=[_rows(TOK, D), _rows(TOK, 3072), _rows(TOK, 1024), _rows(TOK, 1536), _full((1, D)), _full((1, 256)),
                   _full((1, 128))],
        out_shape=[_sds((S, D), jnp.float32), _sds((S, 3072), MXU), _sds((S, 1024), MXU), _sds((S, 1536), MXU),
                   _sds((1, D), jnp.float32), _sds((1, 256), jnp.float32), _sds((1, 128), jnp.float32)],
    )(dqm, dkm, dvm, dqs, dkd, dvd, dgate0, cos, sin, consts, cq, ckv, x, dx1, g_in, w_in, g_q, w_q, g_kv, w_kv)


def _wgrad(a, b, name):
    S, M = a.shape
    N = b.shape[1]
    tm = next(t for t in range(WG_ROWS, 0, -LANES) if M % t == 0)
    tn = N if N <= 1024 else 512
    tk = min(WG_TOK, S)

    def body(a_ref, b_ref, o_ref):
        @pl.when(pl.program_id(2) == 0)
        def _():
            o_ref[...] = jnp.zeros_like(o_ref)

        o_ref[...] += _mm_tn(a_ref[...], b_ref[...])

    return _pcall(
        body, name=name, grid=(M // tm, N // tn, S // tk), semantics=("parallel", "parallel", "arbitrary"),
        in_specs=[pl.BlockSpec((tk, tm), lambda m, n, k: (k, m)), pl.BlockSpec((tk, tn), lambda m, n, k: (k, n))],
        out_specs=pl.BlockSpec((tm, tn), lambda m, n, k: (m, n)),
        out_shape=_sds((M, N), jnp.float32),
    )(a, b)


def _adamw(w, g, m, v, name):
    shape = w.shape
    R, C = (int(np.prod(shape[:-1])), shape[-1])
    w2, g2, m2, v2 = (t.reshape(R, C) for t in (w, g, m, v))
    fits = [t for t in range(8, ADAM_TILE_BYTES // (4 * C) + 1, 8) if R % t == 0]
    tr = max(fits) if fits else R
    tc = C if (tr * C * 4 <= ADAM_TILE_BYTES or C % 256) else 256

    def body(w_ref, g_ref, m_ref, v_ref, d_ref, nm_ref, nv_ref):
        gg = g_ref[...]
        nm = B1 * m_ref[...] + (1.0 - B1) * gg
        nv = B2 * v_ref[...] + (1.0 - B2) * (gg * gg)
        m_hat = nm / (1.0 - B1 ** STEP)
        v_hat = nv / (1.0 - B2 ** STEP)
        d_ref[...] = -LR * (m_hat / (jnp.sqrt(v_hat) + AEPS) + WD * w_ref[...])
        nm_ref[...] = nm
        nv_ref[...] = nv

    spec = pl.BlockSpec((tr, tc), lambda i, j: (i, j))
    d, nm, nv = _pcall(
        body, name=name, grid=(R // tr, C // tc), semantics=("parallel", "parallel"),
        in_specs=[spec] * 4, out_specs=[spec] * 3, out_shape=[_sds((R, C), jnp.float32)] * 3,
    )(w2, g2, m2, v2)
    return d.reshape(shape), nm.reshape(shape), nv.reshape(shape)


def _sum_leading(a, name):
    n, R, C = a.shape
    tr = SUM_ROWS if R % SUM_ROWS == 0 else R

    def body(a_ref, o_ref):
        acc = a_ref[0]
        for i in range(1, n):
            acc = acc + a_ref[i]
        o_ref[...] = acc

    return _pcall(
        body, name=name, grid=(R // tr,), semantics=("parallel",),
        in_specs=[pl.BlockSpec((n, tr, C), lambda i: (0, i, 0))], out_specs=_rows(tr, C),
        out_shape=_sds((R, C), a.dtype),
    )(a)


def _add_blocks(a, b, name, out_dtype):
    n, R, C = a.shape
    tr = SUM_ROWS if R % SUM_ROWS == 0 else R

    def body(a_ref, b_ref, o_ref):
        o_ref[...] = (a_ref[...] + b_ref[...]).astype(o_ref.dtype)

    spec = pl.BlockSpec((1, tr, C), lambda k, i: (k, i, 0))
    return _pcall(
        body, name=name, grid=(n, R // tr), semantics=("parallel", "parallel"),
        in_specs=[spec, spec], out_specs=spec, out_shape=_sds(a.shape, out_dtype),
    )(a, b)


def _total_sum(mine, theirs, recv, name):
    R, C = mine.shape
    n = recv.shape[0]
    tr = SUM_ROWS if R % SUM_ROWS == 0 else R

    def body(a_ref, b_ref, r_ref, o_ref):
        acc = a_ref[...] + b_ref[...]
        for i in range(n):
            acc = acc + r_ref[i].astype(jnp.float32)
        o_ref[...] = acc

    return _pcall(
        body, name=name, grid=(R // tr,), semantics=("parallel",),
        in_specs=[_rows(tr, C), _rows(tr, C), pl.BlockSpec((n, tr, C), lambda i: (0, i, 0))], out_specs=_rows(tr, C),
        out_shape=_sds((R, C), jnp.float32),
    )(mine, theirs, recv)


def _place():
    return lax.axis_index("x"), lax.axis_index("y"), lax.axis_index("c")


class _Plan:
    def __init__(self, arrays, out_shape, scratch, start, finish, middle=None):
        self.arrays, self.out_shape, self.scratch = list(arrays), list(out_shape), list(scratch)
        self.start, self.finish, self.middle = start, finish, middle


def _gather8_plan(block):
    R, C = block.shape

    def parts(ins, outs, sems):
        (x_ref,), (out_ref,), (send_sems, recv_sems) = ins, outs, sems
        x, y, c = _place()
        me, sibling = (x, y, c), (x, y, 1 - c)
        chips = [(1 - x, y), (x, 1 - y), (1 - x, 1 - y)]

        def copy(k, blk, to, src=None):
            slot = out_ref.at[4 * blk[0] + 2 * blk[1] + blk[2]]
            return pltpu.make_async_remote_copy(
                src_ref=slot if src is None else src, dst_ref=slot,
                send_sem=send_sems.at[k], recv_sem=recv_sems.at[k], device_id=to, device_id_type=MESH_ID)

        def first():
            return [copy(0, me, sibling, src=x_ref)] + [copy(1 + j, me, (*chip, c), src=x_ref) for j, chip in enumerate(chips)]

        def passed():
            return [copy(4 + j, (*chip, c), sibling) for j, chip in enumerate(chips)]

        def arrivals():
            return [copy(1 + j, (*chip, c), me) for j, chip in enumerate(chips)]

        def late():
            return [copy(0, sibling, me)] + [copy(4 + j, (*chip, 1 - c), me) for j, chip in enumerate(chips)]

        return first, passed, arrivals, late

    def start(ins, outs, sems):
        for cp in parts(ins, outs, sems)[0]():
            cp.start()

    def middle(ins, outs, sems):
        _, passed, arrivals, _ = parts(ins, outs, sems)
        for arrived, forward in zip(arrivals(), passed()):
            arrived.wait_recv()
            forward.start()

    def finish(ins, outs, sems):
        first, passed, _, late = parts(ins, outs, sems)
        for cp in late():
            cp.wait_recv()
        for cp in first() + passed():
            cp.wait_send()

    return _Plan([block], [_sds((8, R, C), block.dtype)], [pltpu.SemaphoreType.DMA((7,)), pltpu.SemaphoreType.DMA((7,))],
                 start, finish, middle)


def _fill_own_slot(gathered, block):
    x, y, c = _place()
    return lax.dynamic_update_index_in_dim(gathered, block, 4 * x + 2 * y + c, 0)


def _started_and_waited(arrays, out_shape, n, copies):
    def start(ins, outs, sems):
        for cp in copies(ins, outs, sems):
            cp.start()

    def finish(ins, outs, sems):
        for cp in copies(ins, outs, sems):
            cp.wait()

    return _Plan(arrays, out_shape, [pltpu.SemaphoreType.DMA((n,)), pltpu.SemaphoreType.DMA((n,))], start, finish)


def _pair_swap_plan(g):
    n = g.shape[0]

    def copies(ins, outs, sems):
        (g_ref,), (out_ref,), (send_sems, recv_sems) = ins, outs, sems
        x, y, c = _place()
        return [pltpu.make_async_remote_copy(src_ref=g_ref.at[k, 1 - c], dst_ref=out_ref.at[k], send_sem=send_sems.at[k],
                                             recv_sem=recv_sems.at[k], device_id=(x, y, 1 - c), device_id_type=MESH_ID)
                for k in range(n)]

    return _started_and_waited([g], [_sds((n,) + g.shape[2:], g.dtype)], n, copies)


def _chip_exchange_plan(p):
    def copies(ins, outs, sems):
        (p_ref,), (out_ref,), (send_sems, recv_sems) = ins, outs, sems
        x, y, c = _place()
        chips = [(1 - x, y), (x, 1 - y), (1 - x, 1 - y)]
        return [pltpu.make_async_remote_copy(
            src_ref=p_ref.at[2 * cx + cy], dst_ref=out_ref.at[j], send_sem=send_sems.at[j],
            recv_sem=recv_sems.at[j], device_id=(cx, cy, c), device_id_type=MESH_ID)
            for j, (cx, cy) in enumerate(chips)]

    return _started_and_waited([p], [_sds((3,) + p.shape[1:], p.dtype)], 3, copies)


def _pair_exchange_plan(t):
    def copies(ins, outs, sems):
        (t_ref,), (out_ref,), (send_sems, recv_sems) = ins, outs, sems
        x, y, c = _place()
        return [pltpu.make_async_remote_copy(src_ref=t_ref, dst_ref=out_ref, send_sem=send_sems.at[0], recv_sem=recv_sems.at[0],
                                             device_id=(x, y, 1 - c), device_id_type=MESH_ID)]

    return _started_and_waited([t], [_sds(t.shape, t.dtype)], 1, copies)


ANY_SPEC = pl.BlockSpec(memory_space=pl.ANY)


def _run_plan(plan, name):
    n_in, n_out = len(plan.arrays), len(plan.out_shape)

    def body(*refs):
        ins, outs, sems = refs[:n_in], refs[n_in:n_in + n_out], refs[n_in + n_out:]
        plan.start(ins, outs, sems)
        if plan.middle is not None:
            plan.middle(ins, outs, sems)
        plan.finish(ins, outs, sems)

    return _pcall(body, name=name, in_specs=[ANY_SPEC] * n_in, out_specs=[ANY_SPEC] * n_out, out_shape=plan.out_shape,
                  scratch_shapes=plan.scratch)(*plan.arrays)


def _pcall_riding(body, plan, args, *, name, grid, in_specs, out_specs, out_shape, scratch_shapes):
    if plan is None:
        outs = _pcall(body, name=name, grid=grid, semantics=("arbitrary",), in_specs=in_specs, out_specs=out_specs,
                      out_shape=out_shape, scratch_shapes=scratch_shapes)(*args)
        return list(outs), None
    n_in, n_out, n_s = len(args), len(out_shape), len(scratch_shapes)
    p_in, p_out = len(plan.arrays), len(plan.out_shape)
    steps = grid[0]

    def riding(*refs):
        ins, pins = refs[:n_in], refs[n_in:n_in + p_in]
        o0 = n_in + p_in
        outs, pouts = refs[o0:o0 + n_out], refs[o0 + n_out:o0 + n_out + p_out]
        s0 = o0 + n_out + p_out
        scr, sems = refs[s0:s0 + n_s], refs[s0 + n_s:]
        j = pl.program_id(0)

        @pl.when(j == 0)
        def _():
            plan.start(pins, pouts, sems)

        if plan.middle is not None:
            @pl.when(j == steps // 2)
            def _():
                plan.middle(pins, pouts, sems)

        body(*ins, *outs, *scr)

        @pl.when(j == steps - 1)
        def _():
            plan.finish(pins, pouts, sems)

    res = _pcall(riding, name=name, grid=grid, semantics=("arbitrary",), in_specs=list(in_specs) + [ANY_SPEC] * p_in,
                 out_specs=list(out_specs) + [ANY_SPEC] * p_out, out_shape=list(out_shape) + plan.out_shape,
                 scratch_shapes=list(scratch_shapes) + plan.scratch)(*args, *plan.arrays)
    return list(res[:n_out]), list(res[n_out:])


class _RowSeq:
    def __init__(self, pieces):
        self.pieces = list(pieces)

    def rows(self, a, b):
        out, off = [], 0
        for p in self.pieces:
            lo, hi = max(a, off), min(b, off + p.shape[0])
            if lo < hi:
                out.append(p[lo - off:hi - off])
            off += p.shape[0]
        return out

    def array(self):
        return jnp.concatenate(self.pieces, axis=0)


def _row_seq(w):
    return w if isinstance(w, _RowSeq) else _RowSeq([w])


def _prep_w_in0(wt):
    wt = _row_seq(wt)
    one = wt.pieces[0]
    z32 = [jnp.zeros((32, one.shape[1]), one.dtype)]
    k0, k1 = wt.rows(928, 992), wt.rows(992, 1056)
    v0, v1 = wt.rows(1056, 1120), wt.rows(1120, 1184)
    return jnp.concatenate(wt.rows(0, 384) + z32 + z32 + wt.rows(384, 416) + z32 + wt.rows(416, 928)
                           + k0 * 4 + k1 * 4 + v0 * 4 + v1 * 4 + wt.rows(1184, 2208), axis=0)


def _fold_w_in0(d):
    def fold(blk):
        b = blk.reshape(8, 64, blk.shape[1])
        return jnp.concatenate([b[0] + b[1] + b[2] + b[3], b[4] + b[5] + b[6] + b[7]], axis=0)
    return _RowSeq([d[0:384], d[448:480], d[512:1024], fold(d[1024:1536]), fold(d[1536:2048]), d[2048:3072]])


def _prep_w_q(wt):
    return jnp.pad(wt.reshape(N_MLA, 96, Q_RANK), ((0, 0), (0, 32), (0, 0))).reshape(1024, Q_RANK)


def _fold_w_q(d):
    return d.reshape(N_MLA, 128, Q_RANK)[:, :96].reshape(768, Q_RANK)


def _prep_w_kv(w):
    w3 = w.reshape(KV_RANK, N_MLA, 128)
    kk = jnp.pad(w3[:, :, :64], ((0, 0), (0, 0), (0, 64))).reshape(KV_RANK, 1024)
    return jnp.concatenate([kk, w3[:, :, 64:].reshape(KV_RANK, 512)], axis=1)


def _fold_w_kv(d):
    kk = d[:, :1024].reshape(KV_RANK, N_MLA, 128)[:, :, :64]
    vv = d[:, 1024:].reshape(KV_RANK, N_MLA, 64)
    return jnp.concatenate([kk, vv], axis=2).reshape(KV_RANK, 1024)


def _prep_w_in1(wt):
    wt = _row_seq(wt)
    one = wt.pieces[0]
    return jnp.concatenate(wt.rows(0, 3072) + wt.rows(3088, 4112) + wt.rows(3072, 3088)
                           + [jnp.zeros((112, one.shape[1]), one.dtype)], axis=0)


def _fold_w_in1(d):
    return _RowSeq([d[0:3072], d[4096:4112], d[3072:4096]])


class _Alone:
    def __init__(self, w_out0, o_g_in, w_in1, w_out1):
        self.layer1 = (w_out0, o_g_in, w_in1, w_out1)

    def gather_plan(self):
        return None

    def layer1_weights(self, rode):
        return self.layer1

    def swap_plan(self, grads1):
        return None

    def exchange_plan(self, rode):
        return None

    def finish(self, rode):
        pass


def _local_step(x, pos, target, e_g_in, w_in0, e_g_q, w_q, e_g_kv, w_kv, sinks, b_f, g_final, layer1):
    S = x.shape[0]
    w_in0p, w_qp, w_kvp = _prep_w_in0(w_in0), _prep_w_q(w_q), _prep_w_kv(w_kv)
    slopes = jnp.asarray(2.0 ** (-8.0 * (np.arange(N_SWA, dtype=np.float32) + 1.0) / N_SWA), jnp.float32)
    sinks1 = sinks.reshape(N_SWA)
    b_col = b_f.reshape(N_FOX, 1)

    (h0, cq, ckv, cqn, ckvn, qm, km, vm, qs, kd, vd, gate0, cos, sin) = _layer0_in(
        x, pos, e_g_in, w_in0p, e_g_q, w_qp, e_g_kv, w_kvp)
    o_m, lse_m, rode = _attn_fwd_t(qm, km, vm, (NOPE + ROPE) ** -0.5, split=True, name="mla_fwd", plan=layer1.gather_plan())
    w_out0, o_g_in, w_in1, w_out1 = layer1.layer1_weights(rode)
    w_in1p = _prep_w_in1(w_in1)
    o_s, lse_s = _swa_fwd(qs, kd, vd, sinks1, slopes)
    x1, u0, h1, q1, k1, v1, gate1, f_slab = _layer0_out_layer1_in(x, o_m, o_s, gate0, w_out0, o_g_in, w_in1p)
    f_row = f_slab[:, :N_FOX].T
    lc_row = _forget_fwd(f_row, b_col)
    lcc = lc_row.reshape(N_FOX, S, 1)
    o1, lse1, _ = _attn_fwd_t(q1, k1, v1, HEAD ** -0.5, split=False, name="fox_fwd", lcc=lcc)
    loss8, dg_final, dx2, u1, do1, dgate1 = _head(x1, o1, gate1, w_out1, g_final, target)

    dq1, dk1, dv1, dlc, _ = _attn_bwd_t(q1, k1, v1, do1, o1, lse1, HEAD ** -0.5, split=False, name="fox_bwd", lcc=lcc)
    df_row, db_f = _forget_bwd(dlc.reshape(N_FOX, S), f_row, b_col)
    df_slab = jnp.pad(df_row.T, ((0, 0), (0, LANES - N_FOX))).astype(MXU)
    dz1, dx1, dg_o_in, do_m, do_s, dgate0 = _layer1_in_bwd(
        dq1, dk1, dv1, dgate1, df_slab, x1, dx2, o_g_in, w_in1p, gate0, o_m, o_s, w_out0)
    grads1 = dict(o_g_in=dg_o_in, o_w_in=_fold_w_in1(_wgrad(dz1, h1, "wgrad_in1")), o_w_out=_wgrad(u1, dx2, "wgrad_out1"),
                  e_w_out=_wgrad(u0, dx1, "wgrad_out0"))
    dqs, dkd, dvd, dsink, rode = _swa_bwd(qs, kd, vd, do_s, o_s, lse_s, sinks1, slopes, plan=layer1.swap_plan(grads1))
    dqm, dkm, dvm, rode = _attn_bwd_t(qm, km, vm, do_m, o_m, lse_m, (NOPE + ROPE) ** -0.5, split=True, name="mla_bwd",
                                      plan=layer1.exchange_plan(rode))
    layer1.finish(rode)
    dx, dz0, dqu, dkvu, dg_in, dg_q, dg_kv = _layer0_in_bwd(
        dqm, dkm, dvm, dqs, dkd, dvd, dgate0, cos, sin, cq, ckv, x, dx1, e_g_in, w_in0p, e_g_q, w_qp, e_g_kv, w_kvp)

    grads = dict(
        e_g_in=dg_in,
        e_w_in=_fold_w_in0(_wgrad(dz0, h0, "wgrad_in0")),
        e_g_q_a=dg_q,
        e_w_q_up=_fold_w_q(_wgrad(dqu, cqn, "wgrad_q_up")),
        e_g_kv_a=dg_kv,
        e_w_kv_up=_fold_w_kv(_wgrad(ckvn, dkvu, "wgrad_kv_up")),
        e_sinks=dsink[:, 0:2, 0].reshape(1, N_SWA),
        o_b_f=db_f.reshape(1, N_FOX),
        g_final=dg_final,
        **grads1,
    )
    return loss8[0, 0], dx, grads


SHARDED = ("e_w_in", "e_w_q_up", "e_w_kv_up", "e_w_out", "o_g_in", "o_w_in", "o_w_out")
TRANSPOSED = ("e_w_in", "e_w_q_up", "o_w_in")
COL_SHARDED = ("e_w_kv_up", "o_g_in")
REPLICATED = ("e_g_in", "e_g_q_a", "e_g_kv_a", "e_sinks", "o_b_f", "g_final")
FULL_SHAPES = dict(e_w_in=(2208, 1024), e_w_q_up=(768, 256), e_w_kv_up=(128, 1024), e_w_out=(1024, 1024),
                   o_g_in=(1, 1024), o_w_in=(4112, 1024), o_w_out=(1024, 1024))
GROUPS = dict(
    layer0=dict(rows=768, windows=dict(e_w_in=(0, 0), e_w_q_up=(560, 0), e_w_kv_up=(560, 256))),
    layer1=dict(rows=1568, windows=dict(o_w_in=(0, 0), o_w_out=(1040, 0), e_w_out=(1296, 0), o_g_in=(1552, 0))),
)


def _shard_shape(name):
    r, c = FULL_SHAPES[name]
    return (r, c // 4) if name in COL_SHARDED else (r // 4, c)


def _as_handled(name, a):
    a = a[0] if a.ndim == 3 else a
    return a.T if name in TRANSPOSED else a


def _as_given(name, a, shape):
    return (a.T if name in TRANSPOSED else a).reshape(shape)


def _pack_block(p, group):
    def rows(a, n):
        return jnp.pad(a, ((0, n - a.shape[0]), (0, 0)))

    if group == "layer0":
        band = jnp.concatenate([p["e_w_q_up"], rows(p["e_w_kv_up"], 192), jnp.zeros((192, 512), p["e_w_in"].dtype)], axis=1)
        return jnp.concatenate([rows(p["e_w_in"], 560), rows(band, 208)], axis=0)
    g = p["o_g_in"]
    band = jnp.pad(g, ((0, 16 - g.shape[0]), (0, PACK_COLS - g.shape[1])))
    return jnp.concatenate([rows(p["o_w_in"], 1040), p["o_w_out"], p["e_w_out"], band], axis=0)


def _window(block, group, name, width=None):
    r0, c0 = GROUPS[group]["windows"][name]
    r, c = _shard_shape(name)
    return block[..., r0:r0 + r, c0:c0 + (c if width is None else width)]


def _chip_slice(name, full, k):
    r, c = _shard_shape(name)
    if isinstance(full, _RowSeq):
        return jnp.concatenate(full.rows(r * k, r * (k + 1)), axis=0)
    return full[:, c * k:c * (k + 1)] if name in COL_SHARDED else full[r * k:r * (k + 1), :]


def _packed_weights(w, group):
    parts = {}
    for n in GROUPS[group]["windows"]:
        a = _as_handled(n, w[n])
        parts[n] = lax.bitcast_convert_type(a, jnp.bfloat16).reshape(1, -1) if n == "o_g_in" else a.astype(jnp.bfloat16)
    halves = _pack_block(parts, group).reshape(2, GROUPS[group]["rows"] // 2, PACK_COLS)
    return lax.dynamic_index_in_dim(halves, lax.axis_index("c"), 0, keepdims=False)


def _unpacked_weights(gathered, half, group):
    blocks = _fill_own_slot(gathered, half).reshape(4, GROUPS[group]["rows"], PACK_COLS)
    full = {}
    for n in GROUPS[group]["windows"]:
        if n == "o_g_in":
            halves = _window(blocks, group, n, width=512).reshape(4, 1, 256, 2)
            full[n] = jnp.concatenate(list(lax.bitcast_convert_type(halves, jnp.float32)), axis=1)
        else:
            pieces = [_window(blocks[k], group, n).astype(MXU) for k in range(4)]
            if n in ("e_w_in", "o_w_in"):
                full[n] = _RowSeq(pieces)
            else:
                full[n] = jnp.concatenate(pieces, axis=1 if n in COL_SHARDED else 0)
    return full


class _GroupReduce:
    def __init__(self, group):
        self.group = group
        self.c = lax.axis_index("c")
        self.chip = 2 * lax.axis_index("x") + lax.axis_index("y")

    def swap_plan(self, grads):
        names = GROUPS[self.group]["windows"]
        per_chip = jnp.stack([_pack_block({n: _chip_slice(n, grads[n], k) for n in names}, self.group) for k in range(4)])
        self.g4 = per_chip.reshape(4, 2, GROUPS[self.group]["rows"] // 2, PACK_COLS)
        return _pair_swap_plan(self.g4)

    def exchange_plan(self, rode):
        theirs = rode[0]
        mine = lax.dynamic_index_in_dim(self.g4, self.c, 1, keepdims=False)
        self.own = (lax.dynamic_index_in_dim(mine, self.chip, 0, keepdims=False),
                    lax.dynamic_index_in_dim(theirs, self.chip, 0, keepdims=False))
        return _chip_exchange_plan(_add_blocks(mine, theirs, "pair_add_" + self.group, jnp.bfloat16))

    def finish(self, rode):
        my_half = _total_sum(*self.own, rode[0], "chip_sum_" + self.group)
        other_half = _run_plan(_pair_exchange_plan(my_half), "pair_exchange_" + self.group)[0]
        total = jnp.concatenate([jnp.where(self.c == 0, my_half, other_half), jnp.where(self.c == 0, other_half, my_half)], axis=0)
        self.sums = {n: _window(total, self.group, n) for n in GROUPS[self.group]["windows"]}

    def run(self, grads):
        rode = _run_plan(self.swap_plan(grads), "pair_swap_" + self.group)
        self.finish(_run_plan(self.exchange_plan(rode), "chip_exchange_" + self.group))
        return self.sums


class _Layer1Exchange(_GroupReduce):
    def __init__(self, w):
        super().__init__("layer1")
        self.half = _packed_weights(w, "layer1")

    def gather_plan(self):
        return _gather8_plan(self.half)

    def layer1_weights(self, rode):
        full = _unpacked_weights(rode[0], self.half, "layer1")
        return full["e_w_out"], full["o_g_in"], full["o_w_in"], full["o_w_out"]


def kernel(x, positions, e_g_in, e_w_in, e_g_q_a, e_w_q_up, e_g_kv_a, e_w_kv_up, e_sinks, e_w_out, o_g_in, o_w_in, o_b_f, o_w_out, g_final, loss_target, m_e_g_in, m_e_w_in, m_e_g_q_a, m_e_w_q_up, m_e_g_kv_a, m_e_w_kv_up, m_e_sinks, m_e_w_out, m_o_g_in, m_o_w_in, m_o_b_f, m_o_w_out, m_g_final, v_e_g_in, v_e_w_in, v_e_g_q_a, v_e_w_q_up, v_e_g_kv_a, v_e_w_kv_up, v_e_sinks, v_e_w_out, v_o_g_in, v_o_w_in, v_o_b_f, v_o_w_out, v_g_final):
    w = dict(e_g_in=e_g_in, e_w_in=e_w_in, e_g_q_a=e_g_q_a, e_w_q_up=e_w_q_up, e_g_kv_a=e_g_kv_a, e_w_kv_up=e_w_kv_up,
             e_sinks=e_sinks, e_w_out=e_w_out, o_g_in=o_g_in, o_w_in=o_w_in, o_b_f=o_b_f, o_w_out=o_w_out, g_final=g_final)
    m = dict(e_g_in=m_e_g_in, e_w_in=m_e_w_in, e_g_q_a=m_e_g_q_a, e_w_q_up=m_e_w_q_up, e_g_kv_a=m_e_g_kv_a,
             e_w_kv_up=m_e_w_kv_up, e_sinks=m_e_sinks, e_w_out=m_e_w_out, o_g_in=m_o_g_in, o_w_in=m_o_w_in, o_b_f=m_o_b_f,
             o_w_out=m_o_w_out, g_final=m_g_final)
    v = dict(e_g_in=v_e_g_in, e_w_in=v_e_w_in, e_g_q_a=v_e_g_q_a, e_w_q_up=v_e_w_q_up, e_g_kv_a=v_e_g_kv_a,
             e_w_kv_up=v_e_w_kv_up, e_sinks=v_e_sinks, e_w_out=v_e_w_out, o_g_in=v_o_g_in, o_w_in=v_o_w_in, o_b_f=v_o_b_f,
             o_w_out=v_o_w_out, g_final=v_g_final)
    order = ("e_g_in", "e_w_in", "e_g_q_a", "e_w_q_up", "e_g_kv_a", "e_w_kv_up", "e_sinks", "e_w_out", "o_g_in", "o_w_in",
             "o_b_f", "o_w_out", "g_final")
    half0 = _packed_weights(w, "layer0")
    full = _unpacked_weights(_run_plan(_gather8_plan(half0), "gather_weights_layer0")[0], half0, "layer0")
    layer1 = _Layer1Exchange(w)

    loss_part, dx, grads = _local_step(
        x[0], positions.reshape(-1, 1), loss_target[0], e_g_in, full["e_w_in"], e_g_q_a, full["e_w_q_up"], e_g_kv_a,
        full["e_w_kv_up"], e_sinks, o_b_f, g_final.reshape(1, D), layer1)
    loss = lax.psum(loss_part, ("x", "y", "c"))

    gsum = {**layer1.sums, **_GroupReduce("layer0").run(grads)}

    small = jnp.concatenate([jnp.pad(grads[n].reshape(-1), (0, (-grads[n].size) % LANES)) for n in REPLICATED])
    rows = small.shape[0] // LANES
    small = jnp.pad(small.reshape(rows, LANES), ((0, (-rows) % 8), (0, 0)))
    gathered_small = _fill_own_slot(_run_plan(_gather8_plan(small), "gather_small_grads")[0], small)
    ssum = _sum_leading(gathered_small, "small_grad_sum").reshape(-1)
    off = 0
    for n in REPLICATED:
        cnt = w[n].size
        gsum[n] = ssum[off:off + cnt].reshape(w[n].shape)
        off += cnt + (-cnt) % LANES

    grad, delta, new_m, new_v = {}, {}, {}, {}
    for n in order:
        if n in SHARDED:
            outs = _adamw(_as_handled(n, w[n]), gsum[n], _as_handled(n, m[n]), _as_handled(n, v[n]), "adamw_" + n)
            grad[n], delta[n], new_m[n], new_v[n] = (_as_given(n, a, w[n].shape) for a in (gsum[n],) + outs)
        else:
            grad[n] = gsum[n]
            delta[n], new_m[n], new_v[n] = _adamw(w[n], gsum[n], m[n], v[n], "adamw_" + n)
    return (loss, dx[None], *[grad[n] for n in order], *[delta[n] for n in order], *[new_m[n] for n in order],
            *[new_v[n] for n in order])
```

```python
import functools
import math

import numpy as np
import jax
import jax.numpy as jnp
from jax import lax
from jax.experimental import pallas as pl
from jax.experimental.pallas import tpu as pltpu

D = 1024
EPS = 1e-6
ROPE_THETA = 10000.0
N_MLA = 8
Q_RANK = 256
KV_RANK = 128
NOPE = 64
ROPE = 32
N_SWA = 8
WINDOW = 128
N_FOX = 16
HEAD = 64
E_SPLITS = (256, 128, 32, 512, 128, 128, 1024)
O_SPLITS = (1024, 1024, 1024, 16, 1024)
LR, B1, B2, AEPS, WD, STEP = 0.001, 0.9, 0.999, 1e-08, 0.01, 10

LANES = 128
HALF = 64
VMEM_LIMIT = 56 * 1024 * 1024
MXU = jnp.bfloat16
TOK = 256
WG_TOK = 2048
WG_ROWS = 1536
ATT = 256
FWD_CHUNK = 2
BWD_CHUNK = 2
SWA_GROUP = 4
NEG = float("-inf")

PACK_COLS = 1024
SUM_ROWS = 256
ADAM_TILE_BYTES = 2 << 20
MESH_ID = pl.DeviceIdType.MESH


def _pcall(body, *, name, vmem=VMEM_LIMIT, semantics=None, **kw):
    params = dict(vmem_limit_bytes=vmem)
    if semantics is not None:
        params["dimension_semantics"] = semantics
    return pl.pallas_call(body, name=name, compiler_params=pltpu.CompilerParams(**params), **kw)


def _mm(a, b):
    return jnp.dot(a.astype(MXU), b.astype(MXU), preferred_element_type=jnp.float32)


def _mm_nt(a, b):
    return lax.dot_general(a.astype(MXU), b.astype(MXU), (((1,), (1,)), ((), ())),
                           preferred_element_type=jnp.float32)


def _mm_tn(a, b):
    return lax.dot_general(a.astype(MXU), b.astype(MXU), (((0,), (0,)), ((), ())),
                           preferred_element_type=jnp.float32)


def _full(shape):
    n = len(shape)
    return pl.BlockSpec(shape, lambda *_: (0,) * n)


def _rows(tm, n):
    return pl.BlockSpec((tm, n), lambda i: (i, 0))


def _sds(shape, dtype):
    return jax.ShapeDtypeStruct(shape, dtype)


def _rms(x, g):
    r = lax.rsqrt(jnp.mean(x * x, axis=-1, keepdims=True) + EPS)
    return x * r * g


def _rms_bwd(x, g, dy):
    r = lax.rsqrt(jnp.mean(x * x, axis=-1, keepdims=True) + EPS)
    xh = x * r
    dxh = dy * g
    dx = r * (dxh - xh * jnp.mean(dxh * xh, axis=-1, keepdims=True))
    return dx, dy * xh


def _sigmoid(x):
    return 1.0 / (1.0 + jnp.exp(-x))


def _lane_masks(dtype=None):
    lane = lax.broadcasted_iota(jnp.int32, (1, LANES), 1)
    return lane < HALF


def _split_heads(a, lo):
    z = jnp.zeros_like(a)
    return [jnp.where(lo, a, z), jnp.where(lo, z, a)]


def _rope_consts():
    inv = np.zeros((8, LANES), np.float32)
    j = np.arange(ROPE // 2, dtype=np.float32)
    f = (1.0 / (ROPE_THETA ** (np.arange(0, ROPE, 2, dtype=np.float32) / ROPE))).astype(np.float32)
    inv[0, HALF:HALF + 16] = f
    inv[0, HALF + 16:HALF + 32] = f
    inv[1, HALF:HALF + 16] = -1.0
    inv[1, HALF + 16:HALF + 32] = 1.0
    del j
    return jnp.asarray(inv)


def _rope_tables(pos_f, consts):
    ang = pos_f * consts[0:1, :]
    sign = consts[1:2, :]
    c = jnp.where(sign != 0.0, jnp.cos(ang), 1.0)
    s = jnp.sin(ang) * sign
    return c, s


def _swap_halves(v, sign):
    lo = pltpu.roll(v, LANES - 16, axis=1)
    hi = pltpu.roll(v, 16, axis=1)
    return jnp.where(sign < 0.0, lo, jnp.where(sign > 0.0, hi, 0.0))


def _rope(x, c, s, sign):
    return x * c + _swap_halves(x, sign) * s


def _rope_t(dy, c, s, sign):
    return dy * c + _swap_halves(dy * s, sign)


def _layer0_in(x, pos, g_in, w_in, g_q, w_q, g_kv, w_kv):
    S = x.shape[0]
    consts = _rope_consts()

    def body(x_ref, pos_ref, c_ref, g_ref, w_ref, gq_ref, wq_ref, gkv_ref, wkv_ref,
             h_ref, cq_ref, ckv_ref, cqn_ref, ckvn_ref, qm_ref, km_ref, vm_ref,
             qs_ref, kd_ref, vd_ref, gate_ref, cos_ref, sin_ref):
        h = _rms(x_ref[...], g_ref[...])
        h_ref[...] = h.astype(h_ref.dtype)
        z = _mm_nt(h, w_ref[...])
        cq = z[:, 0:256]
        ckv = z[:, 256:384]
        kpe = z[:, 384:512]
        cq_ref[...] = cq
        ckv_ref[...] = ckv
        qs_ref[...] = z[:, 512:1024].astype(qs_ref.dtype)
        kd_ref[...] = z[:, 1024:1536].astype(kd_ref.dtype)
        vd_ref[...] = z[:, 1536:2048].astype(vd_ref.dtype)
        gate_ref[...] = z[:, 2048:3072]
        cqn = _rms(cq, gq_ref[...])
        ckvn = _rms(ckv, gkv_ref[...])
        cqn_ref[...] = cqn.astype(cqn_ref.dtype)
        ckvn_ref[...] = ckvn.astype(ckvn_ref.dtype)
        q = _mm_nt(cqn, wq_ref[...])
        kv = _mm(ckvn, wkv_ref[...])
        vm_ref[...] = kv[:, 1024:1536].astype(vm_ref.dtype)
        consts_v = c_ref[...]
        sign = consts_v[1:2, :]
        c, s = _rope_tables(pos_ref[...].astype(jnp.float32), consts_v)
        cos_ref[...] = c
        sin_ref[...] = s
        kpe_r = _rope(kpe, c, s, sign)
        for hd in range(N_MLA):
            sl = slice(LANES * hd, LANES * (hd + 1))
            qm_ref[:, sl] = _rope(q[:, sl], c, s, sign).astype(qm_ref.dtype)
            km_ref[:, sl] = (kv[:, sl] + kpe_r).astype(km_ref.dtype)

    outs = [
        ((S, D), MXU), ((S, 256), jnp.float32), ((S, 128), jnp.float32), ((S, 256), MXU), ((S, 128), MXU),
        ((S, 1024), MXU), ((S, 1024), MXU), ((S, 512), MXU), ((S, 512), MXU), ((S, 512), MXU), ((S, 512), MXU),
        ((S, 1024), jnp.float32), ((S, 128), jnp.float32), ((S, 128), jnp.float32),
    ]
    return _pcall(
        body, name="layer0_in", grid=(S // TOK,), semantics=("arbitrary",),
        in_specs=[_rows(TOK, D), _rows(TOK, 1), _full((8, LANES)), _full((1, D)), _full(w_in.shape), _full((1, 256)),
                  _full(w_q.shape), _full((1, 128)), _full(w_kv.shape)],
        out_specs=[_rows(TOK, s[1]) for s, _ in outs],
        out_shape=[_sds(s, d) for s, d in outs],
    )(x, pos, consts, g_in, w_in, g_q, w_q, g_kv, w_kv)


AUG = (HALF, 0)
ONE = (HALF + 8, 8)


def _data_lanes(idx, h):
    return (idx < HALF) if h == 0 else (idx >= HALF)


def _three_terms(x):
    hi = x.astype(MXU).astype(jnp.float32)
    mid = (x - hi).astype(MXU).astype(jnp.float32)
    lo = (x - hi - mid).astype(MXU).astype(jnp.float32)
    return hi, mid, lo


def _q_aug(qblk, lc, h, scale, lane):
    a = AUG[h]
    hi, mid, lo = _three_terms(lc)
    ones = ((lane >= a + 3) & (lane <= a + 5)).astype(jnp.float32)
    aug = jnp.where(lane == a, hi, jnp.where(lane == a + 1, mid, jnp.where(lane == a + 2, lo, ones)))
    return jnp.where(_data_lanes(lane, h), qblk * jnp.asarray(scale, qblk.dtype), aug.astype(qblk.dtype))


def _k_aug(kblk, lc, h, lane):
    a = AUG[h]
    hi, mid, lo = _three_terms(-lc)
    ones = ((lane >= a) & (lane <= a + 2)).astype(jnp.float32)
    aug = jnp.where(lane == a + 3, hi, jnp.where(lane == a + 4, mid, jnp.where(lane == a + 5, lo, ones)))
    return jnp.where(_data_lanes(lane, h), kblk, aug.astype(kblk.dtype))


def _attn_fwd_t(q, k, v, scale, *, split, name, lcc=None, plan=None):
    S = q.shape[0]
    npair = v.shape[1] // LANES
    W = 2 * LANES if split else LANES
    T = ATT
    CH = FWD_CHUNK * T
    assert S % CH == 0
    nq = S // T

    def body(*refs):
        if split:
            q_ref, k_ref, v_ref, o_ref, lse_ref, vt, acc, m_sc = refs
        else:
            q_ref, k_ref, v_ref, lcc_ref, o_ref, lse_ref, kaug, vt, acc, m_sc = refs
        lane = lax.broadcasted_iota(jnp.int32, (1, LANES), 1)
        sub = lax.broadcasted_iota(jnp.int32, (LANES, 1), 0)
        key_minus_qry = lax.broadcasted_iota(jnp.int32, (CH, T), 0) - lax.broadcasted_iota(jnp.int32, (CH, T), 1)

        def prep(i, c):
            r0 = pl.multiple_of(i * T, T)
            vblk = v_ref[pl.ds(r0, T), :].astype(jnp.float32)
            for h in (0, 1):
                vh = jnp.where(_data_lanes(lane, h), vblk, (lane == ONE[h]).astype(jnp.float32))
                vt[h, :, pl.ds(r0, T)] = vh.T.astype(vt.dtype)
                if not split:
                    kaug[h, pl.ds(r0, T), :] = _k_aug(k_ref[pl.ds(r0, T), :], lcc_ref[h, pl.ds(r0, T), :], h, lane)
            return c

        lax.fori_loop(0, nq, prep, 0)

        def queries(qi):
            q0 = pl.multiple_of(qi * T, T)
            qblk = q_ref[pl.ds(q0, T), :]
            if split:
                return (qblk[:, :LANES], qblk[:, LANES:])
            return tuple(_q_aug(qblk, lcc_ref[h, pl.ds(q0, T), :], h, scale, lane) for h in (0, 1))

        def scores(qs, c):
            k0 = pl.multiple_of(c * CH, CH)
            out = []
            for h in (0, 1):
                if split:
                    out.append(_mm_nt(k_ref[pl.ds(k0, CH), LANES * h:LANES * (h + 1)], qs[h]) * scale)
                else:
                    out.append(_mm_nt(kaug[h, pl.ds(k0, CH), :], qs[h]))
            return tuple(out)

        def q_block(qi, carry):
            qs, first_scores = carry[:2], carry[2:]
            q0 = pl.multiple_of(qi * T, T)
            acc[...] = jnp.zeros_like(acc)
            m_sc[...] = jnp.full(m_sc.shape, NEG, jnp.float32)

            def absorb(c, sts, masked):
                k0 = pl.multiple_of(c * CH, CH)
                for h in (0, 1):
                    st = sts[h]
                    if masked:
                        st = jnp.where(key_minus_qry <= q0 - k0, st, NEG)
                    m_old = m_sc[h:h + 1, :]
                    m_new = jnp.maximum(m_old, jnp.max(st, axis=0, keepdims=True))
                    alpha = jnp.exp(m_old - m_new)
                    pt = jnp.exp(st - m_new)
                    acc[h] = alpha * acc[h] + _mm(vt[h, :, pl.ds(k0, CH)], pt)
                    m_sc[h:h + 1, :] = m_new

            last = qi // FWD_CHUNK

            def pipelined(c, sts):
                nxt = scores(qs, c + 1)
                absorb(c, sts, False)
                return nxt

            sts = lax.fori_loop(0, last, pipelined, first_scores)
            qs_next = queries(jnp.minimum(qi + 1, nq - 1))
            nxt = qs_next + scores(qs_next, 0)
            absorb(last, sts, True)
            ot = None
            for h in (0, 1):
                a = acc[h]
                l = a[ONE[h]:ONE[h] + 1, :]
                oh = jnp.where(_data_lanes(sub, h), a * (1.0 / l), 0.0)
                ot = oh if ot is None else ot + oh
                lse_ref[0, h:h + 1, pl.ds(q0, T)] = m_sc[h:h + 1, :] + jnp.log(l)
            o_ref[pl.ds(q0, T), :] = ot.T
            return nxt

        qs0 = queries(0)
        lax.fori_loop(0, nq, q_block, qs0 + scores(qs0, 0))

    wide = pl.BlockSpec((S, W), lambda j: (0, j))
    slab = pl.BlockSpec((S, LANES), lambda j: (0, j))
    rows = pl.BlockSpec((1, 2, S), lambda j: (j, 0, 0))
    in_specs = [wide, wide, slab]
    args = [q, k, v]
    scratch = []
    if not split:
        in_specs.append(pl.BlockSpec((2, S, 1), lambda j: (j, 0, 0)))
        args.append(lcc)
        scratch.append(pltpu.VMEM((2, S, LANES), MXU))
    scratch += [pltpu.VMEM((2, LANES, S), MXU), pltpu.VMEM((2, LANES, T), jnp.float32), pltpu.VMEM((8, T), jnp.float32)]
    (o, lse), rode = _pcall_riding(
        body, plan, args, name=name, grid=(npair,), in_specs=in_specs, out_specs=[slab, rows],
        out_shape=[_sds((S, npair * LANES), jnp.float32), _sds((npair, 2, S), jnp.float32)], scratch_shapes=scratch)
    return o, lse, rode


def _attn_bwd_t(q, k, v, do, o, lse, scale, *, split, name, lcc=None, plan=None):
    S = q.shape[0]
    npair = v.shape[1] // LANES
    W = 2 * LANES if split else LANES
    T = ATT
    CH = BWD_CHUNK * T
    assert S % CH == 0
    nq = S // T

    def body(*refs):
        if split:
            (q_ref, k_ref, v_ref, do_ref, o_ref, lse_ref, dq_ref, dk_ref, dv_ref, dqt, delta, dk_acc, dv_acc) = refs
        else:
            (q_ref, k_ref, v_ref, do_ref, o_ref, lse_ref, lcc_ref, dq_ref, dk_ref, dv_ref, dlc_ref,
             dqt, delta, dk_acc, dv_acc, qaug, csum) = refs
        lane = lax.broadcasted_iota(jnp.int32, (1, LANES), 1)
        sub = lax.broadcasted_iota(jnp.int32, (LANES, 1), 0)
        key_minus_qry = lax.broadcasted_iota(jnp.int32, (T, CH), 0) - lax.broadcasted_iota(jnp.int32, (T, CH), 1)

        def prep(i, c):
            r0 = pl.multiple_of(i * T, T)
            prod_t = (do_ref[pl.ds(r0, T), :].astype(jnp.float32) * o_ref[pl.ds(r0, T), :]).T
            for h in (0, 1):
                delta[h:h + 1, pl.ds(r0, T)] = jnp.sum(jnp.where(_data_lanes(sub, h), prod_t, 0.0), axis=0, keepdims=True)
                dqt[h, :, pl.ds(r0, T)] = jnp.zeros((LANES, T), jnp.float32)
                if not split:
                    qaug[h, pl.ds(r0, T), :] = _q_aug(q_ref[pl.ds(r0, T), :], lcc_ref[h, pl.ds(r0, T), :], h, scale, lane)
            return c

        lax.fori_loop(0, nq, prep, 0)

        def keys(ki):
            k0 = pl.multiple_of(ki * T, T)
            kblk = k_ref[pl.ds(k0, T), :]
            if split:
                return (kblk[:, :LANES], kblk[:, LANES:])
            return tuple(_k_aug(kblk, lcc_ref[h, pl.ds(k0, T), :], h, lane) for h in (0, 1))

        def q_of(c, h):
            q0 = pl.multiple_of(c * CH, CH)
            if split:
                return q_ref[pl.ds(q0, CH), LANES * h:LANES * (h + 1)]
            return qaug[h, pl.ds(q0, CH), :]

        def scores(khs, c):
            out = []
            for h in (0, 1):
                st = _mm_nt(khs[h], q_of(c, h))
                out.append(st * scale if split else st)
            return tuple(out)

        def k_block(ki, carry):
            khs, first_scores = carry[:2], carry[2:]
            k0 = pl.multiple_of(ki * T, T)
            khts = [kh.astype(jnp.float32).T.astype(kh.dtype) for kh in khs]
            vhs = _split_heads(v_ref[pl.ds(k0, T), :], lane < HALF)
            dk_acc[...] = jnp.zeros_like(dk_acc)
            dv_acc[...] = jnp.zeros_like(dv_acc)

            def absorb(c, vals):
                q0 = pl.multiple_of(c * CH, CH)
                dos = _split_heads(do_ref[pl.ds(q0, CH), :], lane < HALF)
                visible = key_minus_qry <= q0 - k0
                for h in (0, 1):
                    dpt = _mm_nt(vhs[h], dos[h])
                    st = jnp.where(visible, vals[h], NEG)
                    pt = jnp.exp(st - lse_ref[0, h:h + 1, pl.ds(q0, CH)])
                    dv_acc[...] += _mm(pt, dos[h])
                    dst = pt * (dpt - delta[h:h + 1, pl.ds(q0, CH)])
                    dk_acc[h] += _mm(dst, q_of(c, h))
                    dqt[h, :, pl.ds(q0, CH)] += _mm(khts[h], dst)

            first = ki // BWD_CHUNK

            def pipelined(c, vals):
                nxt = scores(khs, c + 1)
                absorb(c, vals)
                return nxt

            vals = lax.fori_loop(first, S // CH - 1, pipelined, first_scores)
            kn = jnp.minimum(ki + 1, nq - 1)
            khs_next = keys(kn)
            nxt = khs_next + scores(khs_next, kn // BWD_CHUNK)
            absorb(S // CH - 1, vals)
            if split:
                dk_ref[pl.ds(k0, T), :LANES] = (dk_acc[0] * scale).astype(dk_ref.dtype)
                dk_ref[pl.ds(k0, T), LANES:] = (dk_acc[1] * scale).astype(dk_ref.dtype)
            else:
                dk_ref[pl.ds(k0, T), :] = jnp.where(lane < HALF, dk_acc[0], dk_acc[1]).astype(dk_ref.dtype)
                for h in (0, 1):
                    csum[h:h + 1, pl.ds(k0, T)] = dk_acc[h].T[AUG[h] + 3:AUG[h] + 4, :]
            dv_ref[pl.ds(k0, T), :] = dv_acc[...].astype(dv_ref.dtype)
            return nxt

        khs0 = keys(0)
        lax.fori_loop(0, nq, k_block, khs0 + scores(khs0, 0))

        def finish(i, c):
            r0 = pl.multiple_of(i * T, T)
            if split:
                for h in (0, 1):
                    dq_ref[pl.ds(r0, T), LANES * h:LANES * (h + 1)] = (dqt[h, :, pl.ds(r0, T)].T * scale).astype(dq_ref.dtype)
            else:
                d = jnp.where(sub < HALF, dqt[0, :, pl.ds(r0, T)], dqt[1, :, pl.ds(r0, T)])
                dq_ref[pl.ds(r0, T), :] = (d.T * scale).astype(dq_ref.dtype)
                for h in (0, 1):
                    dlc_ref[0, h:h + 1, pl.ds(r0, T)] = dqt[h, AUG[h]:AUG[h] + 1, pl.ds(r0, T)] - csum[h:h + 1, pl.ds(r0, T)]
            return c

        lax.fori_loop(0, nq, finish, 0)

    wide = pl.BlockSpec((S, W), lambda j: (0, j))
    slab = pl.BlockSpec((S, LANES), lambda j: (0, j))
    rows = pl.BlockSpec((1, 2, S), lambda j: (j, 0, 0))
    in_specs = [wide, wide, slab, slab, slab, rows]
    args = [q, k, v, do, o, lse]
    out_specs = [wide, wide, slab]
    out_shape = [_sds(q.shape, jnp.float32 if split else do.dtype), _sds(k.shape, jnp.float32 if split else do.dtype),
                 _sds(v.shape, do.dtype)]
    scratch = [pltpu.VMEM((2, LANES, S), jnp.float32), pltpu.VMEM((8, S), jnp.float32),
               pltpu.VMEM((2, T, LANES), jnp.float32), pltpu.VMEM((T, LANES), jnp.float32)]
    if not split:
        in_specs.append(pl.BlockSpec((2, S, 1), lambda j: (j, 0, 0)))
        args.append(lcc)
        out_specs.append(rows)
        out_shape.append(_sds((npair, 2, S), jnp.float32))
        scratch += [pltpu.VMEM((2, S, LANES), MXU), pltpu.VMEM((8, S), jnp.float32)]
    outs, rode = _pcall_riding(body, plan, args, name=name, grid=(npair,), in_specs=in_specs, out_specs=out_specs,
                               out_shape=out_shape, scratch_shapes=scratch)
    return (*outs, rode)


def _swa_bias(slope, shift):
    a = lax.broadcasted_iota(jnp.int32, (WINDOW, 2 * WINDOW), 0)
    c = lax.broadcasted_iota(jnp.int32, (WINDOW, 2 * WINDOW), 1)
    dist = a - c + shift
    return jnp.where((dist >= 0) & (dist < WINDOW), -slope * dist.astype(jnp.float32), NEG)


def _swa_scores(qh, kblk, bias):
    return _mm_nt(qh, kblk) * (HEAD ** -0.5) + bias


def _swa_fwd(q, kd, vd, sinks, slopes):
    S = q.shape[0]
    npair = q.shape[1] // LANES
    nb = S // WINDOW

    def body(sink_ref, slope_ref, q_ref, k_ref, v_ref, o_ref, lse_ref):
        j = pl.program_id(0)
        lo = _lane_masks()
        biases = [(_swa_bias(slope_ref[2 * j + h], 0), _swa_bias(slope_ref[2 * j + h], WINDOW)) for h in (0, 1)]

        def q_block(qi, c):
            q0 = pl.multiple_of(qi * WINDOW, WINDOW)
            k0 = pl.multiple_of(jnp.maximum(qi - 1, 0) * WINDOW, WINDOW)
            qs = _split_heads(q_ref[pl.ds(q0, WINDOW), :], lo)
            kblk = k_ref[pl.ds(k0, 2 * WINDOW), :]
            vs = _split_heads(v_ref[pl.ds(k0, 2 * WINDOW), :], lo)
            o = None
            for h in (0, 1):
                sink = sink_ref[2 * j + h]
                s = _swa_scores(qs[h], kblk, jnp.where(qi == 0, *biases[h]))
                m = jnp.maximum(jnp.max(s, axis=1, keepdims=True), sink)
                p = jnp.exp(s - m)
                den = jnp.sum(p, axis=1, keepdims=True) + jnp.exp(sink - m)
                oh = _mm(p / den, vs[h])
                o = oh if o is None else o + oh
                lse_ref[h, pl.ds(q0, WINDOW), :] = m + jnp.log(den)
            o_ref[pl.ds(q0, WINDOW), :] = o
            return c

        def q_group(gi, c):
            for g in range(SWA_GROUP):
                q_block(gi * SWA_GROUP + g, c)
            return c

        lax.fori_loop(0, nb // SWA_GROUP, q_group, 0)

    smem = pl.BlockSpec(memory_space=pltpu.SMEM)
    slab = pl.BlockSpec((S, LANES), lambda j: (0, j))
    return _pcall(
        body, name="swa_fwd", grid=(npair,), semantics=("arbitrary",),
        in_specs=[smem, smem, slab, slab, slab],
        out_specs=[slab, pl.BlockSpec((2, S, 1), lambda j: (j, 0, 0))],
        out_shape=[_sds((S, npair * LANES), jnp.float32), _sds((2 * npair, S, 1), jnp.float32)],
    )(sinks, slopes, q, kd, vd)


def _swa_bwd(q, kd, vd, do, o, lse, sinks, slopes, plan=None):
    S = q.shape[0]
    npair = q.shape[1] // LANES
    nb = S // WINDOW

    def body(sink_ref, slope_ref, q_ref, k_ref, v_ref, do_ref, o_ref, lse_ref,
             dq_ref, dk_ref, dv_ref, dsink_ref, dk_acc, dv_acc):
        j = pl.program_id(0)
        lo = _lane_masks()
        dk_acc[...] = jnp.zeros_like(dk_acc)
        dv_acc[...] = jnp.zeros_like(dv_acc)
        biases = [(_swa_bias(slope_ref[2 * j + h], 0), _swa_bias(slope_ref[2 * j + h], WINDOW)) for h in (0, 1)]

        def q_block(qi, carry):
            q0 = pl.multiple_of(qi * WINDOW, WINDOW)
            k0 = pl.multiple_of(jnp.maximum(qi - 1, 0) * WINDOW, WINDOW)
            qs = _split_heads(q_ref[pl.ds(q0, WINDOW), :], lo)
            dos = _split_heads(do_ref[pl.ds(q0, WINDOW), :], lo)
            oblk = o_ref[pl.ds(q0, WINDOW), :]
            kblk = k_ref[pl.ds(k0, 2 * WINDOW), :]
            vblk = v_ref[pl.ds(k0, 2 * WINDOW), :]
            ks = _split_heads(kblk, lo)
            dq = None
            out = []
            for h in (0, 1):
                sink = sink_ref[2 * j + h]
                lse_h = lse_ref[h, pl.ds(q0, WINDOW), :]
                s = _swa_scores(qs[h], kblk, jnp.where(qi == 0, *biases[h]))
                p = jnp.exp(s - lse_h)
                delta = jnp.sum(dos[h].astype(jnp.float32) * oblk, axis=1, keepdims=True)
                dv_acc[pl.ds(k0, 2 * WINDOW), :] += _mm_tn(p, dos[h])
                dp = _mm_nt(dos[h], vblk)
                ds = p * (dp - delta)
                dqh = _mm(ds, ks[h]) * (HEAD ** -0.5)
                dq = dqh if dq is None else dq + dqh
                dk_acc[pl.ds(k0, 2 * WINDOW), :] += _mm_tn(ds, qs[h]) * (HEAD ** -0.5)
                dsk = jnp.sum(-jnp.exp(sink - lse_h) * delta, axis=0, keepdims=True)
                out.append(carry[h] + dsk)
            dq_ref[pl.ds(q0, WINDOW), :] = dq.astype(dq_ref.dtype)
            return tuple(out)

        def q_group(gi, carry):
            for g in range(SWA_GROUP):
                carry = q_block(gi * SWA_GROUP + g, carry)
            return carry

        zero = jnp.zeros((1, 1), jnp.float32)
        dsa, dsb = lax.fori_loop(0, nb // SWA_GROUP, q_group, (zero, zero))
        dk_ref[...] = dk_acc[...].astype(dk_ref.dtype)
        dv_ref[...] = dv_acc[...].astype(dv_ref.dtype)
        r = lax.broadcasted_iota(jnp.int32, (8, LANES), 0)
        dsink_ref[0] = jnp.where(r == 0, dsa, jnp.where(r == 1, dsb, 0.0))

    smem = pl.BlockSpec(memory_space=pltpu.SMEM)
    slab = pl.BlockSpec((S, LANES), lambda j: (0, j))
    outs, rode = _pcall_riding(
        body, plan, [sinks, slopes, q, kd, vd, do, o, lse], name="swa_bwd", grid=(npair,),
        in_specs=[smem, smem, slab, slab, slab, slab, slab, pl.BlockSpec((2, S, 1), lambda j: (j, 0, 0))],
        out_specs=[slab, slab, slab, pl.BlockSpec((1, 8, LANES), lambda j: (j, 0, 0))],
        out_shape=[_sds(q.shape, do.dtype), _sds(kd.shape, do.dtype), _sds(vd.shape, do.dtype),
                   _sds((npair, 8, LANES), jnp.float32)],
        scratch_shapes=[pltpu.VMEM((S, LANES), jnp.float32), pltpu.VMEM((S, LANES), jnp.float32)])
    return (*outs, rode)


def _log_steps(S):
    k, out = 1, []
    while k < S:
        out.append(k)
        k *= 2
    return out


def _forget_fwd(f_row, b_col):
    S = f_row.shape[1]

    def body(f_ref, b_ref, lc_ref):
        x = f_ref[...] + b_ref[...]
        lc = jnp.minimum(x, 0.0) - jnp.log(1.0 + jnp.exp(-jnp.abs(x)))
        idx = lax.broadcasted_iota(jnp.int32, lc.shape, 1)
        for k in _log_steps(S):
            lc = lc + jnp.where(idx >= k, pltpu.roll(lc, k, axis=1), 0.0)
        lc_ref[...] = lc

    return _pcall(body, name="forget_fwd", out_shape=_sds(f_row.shape, jnp.float32))(f_row, b_col)


def _forget_bwd(dlc_row, f_row, b_col):
    S = f_row.shape[1]

    def body(d_ref, f_ref, b_ref, df_ref, db_ref):
        g = d_ref[...]
        idx = lax.broadcasted_iota(jnp.int32, g.shape, 1)
        for k in _log_steps(S):
            g = g + jnp.where(idx < S - k, pltpu.roll(g, S - k, axis=1), 0.0)
        x = f_ref[...] + b_ref[...]
        df = g * _sigmoid(-x)
        df_ref[...] = df
        db_ref[...] = jnp.sum(df, axis=1, keepdims=True)

    return _pcall(body, name="forget_bwd",
                  out_shape=[_sds(f_row.shape, jnp.float32), _sds((f_row.shape[0], 1), jnp.float32)])(dlc_row, f_row, b_col)


def _layer0_out_layer1_in(x, o_m, o_s, gate, w_out, g1, w_in1):
    S = x.shape[0]

    def body(x_ref, om_ref, os_ref, gate_ref, wo_ref, g_ref, w_ref,
             x1_ref, u_ref, h_ref, q_ref, k_ref, v_ref, g1_ref, f_ref):
        gt = gate_ref[...]
        sg = gt * _sigmoid(gt)
        um = om_ref[...] * sg[:, :512]
        us = os_ref[...] * sg[:, 512:]
        u_ref[:, :512] = um.astype(u_ref.dtype)
        u_ref[:, 512:] = us.astype(u_ref.dtype)
        x1 = x_ref[...] + _mm(um, wo_ref[0:512, :]) + _mm(us, wo_ref[512:1024, :])
        x1_ref[...] = x1
        h = _rms(x1, g_ref[...])
        h_ref[...] = h.astype(h_ref.dtype)
        z = _mm_nt(h, w_ref[...])
        q_ref[...] = z[:, 0:1024].astype(q_ref.dtype)
        k_ref[...] = z[:, 1024:2048].astype(k_ref.dtype)
        v_ref[...] = z[:, 2048:3072].astype(v_ref.dtype)
        g1_ref[...] = z[:, 3072:4096]
        f_ref[...] = z[:, 4096:4224]

    outs = [((S, D), jnp.float32), ((S, D), MXU), ((S, D), MXU), ((S, D), MXU), ((S, D), MXU), ((S, D), MXU),
            ((S, D), jnp.float32), ((S, LANES), jnp.float32)]
    return _pcall(
        body, name="layer0_out_layer1_in", grid=(S // TOK,), semantics=("arbitrary",),
        in_specs=[_rows(TOK, D), _rows(TOK, 512), _rows(TOK, 512), _rows(TOK, D), _full((D, D)), _full((1, D)),
                  _full(w_in1.shape)],
        out_specs=[_rows(TOK, s[1]) for s, _ in outs],
        out_shape=[_sds(s, d) for s, d in outs],
    )(x, o_m, o_s, gate, w_out, g1, w_in1)


def _head(x1, o1, gate1, w_out1, g_f, target):
    S = x1.shape[0]

    def body(x1_ref, o_ref, gate_ref, wo_ref, g_ref, t_ref,
             loss_ref, dgf_ref, dx2_ref, u_ref, do_ref, dgate_ref):
        i = pl.program_id(0)
        gt = gate_ref[...]
        sig = _sigmoid(gt)
        sg = gt * sig
        o = o_ref[...]
        u = o * sg
        u_ref[...] = u.astype(u_ref.dtype)
        x2 = x1_ref[...] + _mm(u, wo_ref[...])
        g = g_ref[...]
        y = _rms(x2, g)
        err = y - t_ref[...]
        part = 0.5 * jnp.sum(jnp.mean(err * err, axis=-1, keepdims=True), axis=0, keepdims=True)
        dy = err * (1.0 / D)
        dx2, dg_rows = _rms_bwd(x2, g, dy)
        dx2_ref[...] = dx2
        du = _mm_nt(dx2, wo_ref[...])
        do_ref[...] = (du * sg).astype(do_ref.dtype)
        dgate_ref[...] = (du * o * (sig * (1.0 + gt * (1.0 - sig)))).astype(dgate_ref.dtype)

        @pl.when(i == 0)
        def _():
            loss_ref[...] = jnp.zeros_like(loss_ref)
            dgf_ref[...] = jnp.zeros_like(dgf_ref)

        loss_ref[...] += jnp.broadcast_to(part, loss_ref.shape)
        dgf_ref[...] += jnp.sum(dg_rows, axis=0, keepdims=True)

    outs = [((S, D), jnp.float32), ((S, D), MXU), ((S, D), MXU), ((S, D), MXU)]
    return _pcall(
        body, name="head", grid=(S // TOK,), semantics=("arbitrary",),
        in_specs=[_rows(TOK, D), _rows(TOK, D), _rows(TOK, D), _full((D, D)), _full((1, D)), _rows(TOK, D)],
        out_specs=[_full((8, LANES)), _full((1, D))] + [_rows(TOK, D) for _ in outs],
        out_shape=[_sds((8, LANES), jnp.float32), _sds((1, D), jnp.float32)] + [_sds(s, d) for s, d in outs],
    )(x1, o1, gate1, w_out1, g_f, target)


def _layer1_in_bwd(dq, dk, dv, dgate1, df, x1, dx2, g1, w_in1, gate0, o_m, o_s, w_out0):
    S = x1.shape[0]

    def body(dq_ref, dk_ref, dv_ref, dg1_ref, df_ref, x1_ref, dx2_ref, g_ref, w_ref, gate_ref, om_ref, os_ref,
             wo_ref, dz_ref, dx1_ref, dgn_ref, dom_ref, dos_ref, dgate_ref):
        i = pl.program_id(0)
        dz_ref[:, 0:1024] = dq_ref[...]
        dz_ref[:, 1024:2048] = dk_ref[...]
        dz_ref[:, 2048:3072] = dv_ref[...]
        dz_ref[:, 3072:4096] = dg1_ref[...]
        dz_ref[:, 4096:4224] = df_ref[...]
        dh = _mm(dz_ref[...], w_ref[...])
        g = g_ref[...]
        dxn, dg_rows = _rms_bwd(x1_ref[...], g, dh)
        dx1 = dx2_ref[...] + dxn
        dx1_ref[...] = dx1
        du = _mm_nt(dx1, wo_ref[...])
        gt = gate_ref[...]
        sig = _sigmoid(gt)
        sg = gt * sig
        dsg = sig * (1.0 + gt * (1.0 - sig))
        dom_ref[...] = (du[:, :512] * sg[:, :512]).astype(dom_ref.dtype)
        dos_ref[...] = (du[:, 512:] * sg[:, 512:]).astype(dos_ref.dtype)
        dgate_ref[:, :512] = (du[:, :512] * om_ref[...] * dsg[:, :512]).astype(dgate_ref.dtype)
        dgate_ref[:, 512:] = (du[:, 512:] * os_ref[...] * dsg[:, 512:]).astype(dgate_ref.dtype)

        @pl.when(i == 0)
        def _():
            dgn_ref[...] = jnp.zeros_like(dgn_ref)

        dgn_ref[...] += jnp.sum(dg_rows, axis=0, keepdims=True)

    return _pcall(
        body, name="layer1_in_bwd", grid=(S // TOK,), semantics=("arbitrary",),
        in_specs=[_rows(TOK, D), _rows(TOK, D), _rows(TOK, D), _rows(TOK, D), _rows(TOK, LANES), _rows(TOK, D),
                  _rows(TOK, D), _full((1, D)), _full(w_in1.shape), _rows(TOK, D), _rows(TOK, 512), _rows(TOK, 512),
                  _full((D, D))],
        out_specs=[_rows(TOK, 4224), _rows(TOK, D), _full((1, D)), _rows(TOK, 512), _rows(TOK, 512), _rows(TOK, D)],
        out_shape=[_sds((S, 4224), MXU), _sds((S, D), jnp.float32), _sds((1, D), jnp.float32),
                   _sds((S, 512), MXU), _sds((S, 512), MXU), _sds((S, D), MXU)],
    )(dq, dk, dv, dgate1, df, x1, dx2, g1, w_in1, gate0, o_m, o_s, w_out0)


def _layer0_in_bwd(dqm, dkm, dvm, dqs, dkd, dvd, dgate0, cos, sin, cq, ckv, x, dx1, g_in, w_in, g_q, w_q, g_kv, w_kv):
    S = x.shape[0]
    consts = _rope_consts()

    def body(dqm_ref, dkm_ref, dvm_ref, dqs_ref, dkd_ref, dvd_ref, dgate_ref, cos_ref, sin_ref, c_ref, cq_ref, ckv_ref,
             x_ref, dx1_ref, g_ref, w_ref, gq_ref, wq_ref, gkv_ref, wkv_ref,
             dx_ref, dz_ref, dqu_ref, dkvu_ref, dgin_ref, dgq_ref, dgkv_ref):
        i = pl.program_id(0)
        lo = _lane_masks()
        sign = c_ref[...][1:2, :]
        c = cos_ref[...]
        s = sin_ref[...]
        dkpe = None
        for hd in range(N_MLA):
            sl = slice(LANES * hd, LANES * (hd + 1))
            dqu_ref[:, sl] = _rope_t(dqm_ref[:, sl], c, s, sign).astype(dqu_ref.dtype)
            dkh = dkm_ref[:, sl]
            dkvu_ref[:, sl] = jnp.where(lo, dkh, 0.0).astype(dkvu_ref.dtype)
            dkpe = dkh if dkpe is None else dkpe + dkh
        dkvu_ref[:, 1024:1536] = dvm_ref[...]
        dkpe = _rope_t(jnp.where(lo, 0.0, dkpe), c, s, sign)
        dcqn = _mm(dqu_ref[...], wq_ref[...])
        dckvn = _mm_nt(dkvu_ref[...], wkv_ref[...])
        gq = gq_ref[...]
        gkv = gkv_ref[...]
        dcq, dgq_rows = _rms_bwd(cq_ref[...], gq, dcqn)
        dckv, dgkv_rows = _rms_bwd(ckv_ref[...], gkv, dckvn)
        dz_ref[:, 0:256] = dcq.astype(dz_ref.dtype)
        dz_ref[:, 256:384] = dckv.astype(dz_ref.dtype)
        dz_ref[:, 384:512] = dkpe.astype(dz_ref.dtype)
        dz_ref[:, 512:1024] = dqs_ref[...]
        dz_ref[:, 1024:1536] = dkd_ref[...]
        dz_ref[:, 1536:2048] = dvd_ref[...]
        dz_ref[:, 2048:3072] = dgate_ref[...]
        dh = _mm(dz_ref[...], w_ref[...])
        g = g_ref[...]
        dxn, dg_rows = _rms_bwd(x_ref[...], g, dh)
        dx_ref[...] = dx1_ref[...] + dxn

        @pl.when(i == 0)
        def _():
            dgin_ref[...] = jnp.zeros_like(dgin_ref)
            dgq_ref[...] = jnp.zeros_like(dgq_ref)
            dgkv_ref[...] = jnp.zeros_like(dgkv_ref)

        dgin_ref[...] += jnp.sum(dg_rows, axis=0, keepdims=True)
        dgq_ref[...] += jnp.sum(dgq_rows, axis=0, keepdims=True)
        dgkv_ref[...] += jnp.sum(dgkv_rows, axis=0, keepdims=True)

    return _pcall(
        body, name="layer0_in_bwd", grid=(S // TOK,), semantics=("arbitrary",),
        in_specs=[_rows(TOK, 1024), _rows(TOK, 1024), _rows(TOK, 512), _rows(TOK, 512), _rows(TOK, 512), _rows(TOK, 512),
                  _rows(TOK, D), _rows(TOK, LANES), _rows(TOK, LANES), _full((8, LANES)), _rows(TOK, 256), _rows(TOK, 128),
                  _rows(TOK, D), _rows(TOK, D), _full((1, D)), _full(w_in.shape), _full((1, 256)), _full(w_q.shape),
                  _full((1, 128)), _full(w_kv.shape)],
        out_specs=[_rows(TOK, D), _rows(TOK, 3072), _rows(TOK, 1024), _rows(TOK, 1536), _full((1, D)), _full((1, 256)),
                   _full((1, 128))],
        out_shape=[_sds((S, D), jnp.float32), _sds((S, 3072), MXU), _sds((S, 1024), MXU), _sds((S, 1536), MXU),
                   _sds((1, D), jnp.float32), _sds((1, 256), jnp.float32), _sds((1, 128), jnp.float32)],
    )(dqm, dkm, dvm, dqs, dkd, dvd, dgate0, cos, sin, consts, cq, ckv, x, dx1, g_in, w_in, g_q, w_q, g_kv, w_kv)


def _wgrad(a, b, name):
    S, M = a.shape
    N = b.shape[1]
    tm = next(t for t in range(WG_ROWS, 0, -LANES) if M % t == 0)
    tn = N if N <= 1024 else 512
    tk = min(WG_TOK, S)

    def body(a_ref, b_ref, o_ref):
        @pl.when(pl.program_id(2) == 0)
        def _():
            o_ref[...] = jnp.zeros_like(o_ref)

        o_ref[...] += _mm_tn(a_ref[...], b_ref[...])

    return _pcall(
        body, name=name, grid=(M // tm, N // tn, S // tk), semantics=("parallel", "parallel", "arbitrary"),
        in_specs=[pl.BlockSpec((tk, tm), lambda m, n, k: (k, m)), pl.BlockSpec((tk, tn), lambda m, n, k: (k, n))],
        out_specs=pl.BlockSpec((tm, tn), lambda m, n, k: (m, n)),
        out_shape=_sds((M, N), jnp.float32),
    )(a, b)


def _adamw(w, g, m, v, name):
    shape = w.shape
    R, C = (int(np.prod(shape[:-1])), shape[-1])
    w2, g2, m2, v2 = (t.reshape(R, C) for t in (w, g, m, v))
    fits = [t for t in range(8, ADAM_TILE_BYTES // (4 * C) + 1, 8) if R % t == 0]
    tr = max(fits) if fits else R
    tc = C if (tr * C * 4 <= ADAM_TILE_BYTES or C % 256) else 256

    def body(w_ref, g_ref, m_ref, v_ref, d_ref, nm_ref, nv_ref):
        gg = g_ref[...]
        nm = B1 * m_ref[...] + (1.0 - B1) * gg
        nv = B2 * v_ref[...] + (1.0 - B2) * (gg * gg)
        m_hat = nm / (1.0 - B1 ** STEP)
        v_hat = nv / (1.0 - B2 ** STEP)
        d_ref[...] = -LR * (m_hat / (jnp.sqrt(v_hat) + AEPS) + WD * w_ref[...])
        nm_ref[...] = nm
        nv_ref[...] = nv

    spec = pl.BlockSpec((tr, tc), lambda i, j: (i, j))
    d, nm, nv = _pcall(
        body, name=name, grid=(R // tr, C // tc), semantics=("parallel", "parallel"),
        in_specs=[spec] * 4, out_specs=[spec] * 3, out_shape=[_sds((R, C), jnp.float32)] * 3,
    )(w2, g2, m2, v2)
    return d.reshape(shape), nm.reshape(shape), nv.reshape(shape)


def _sum_leading(a, name):
    n, R, C = a.shape
    tr = SUM_ROWS if R % SUM_ROWS == 0 else R

    def body(a_ref, o_ref):
        acc = a_ref[0]
        for i in range(1, n):
            acc = acc + a_ref[i]
        o_ref[...] = acc

    return _pcall(
        body, name=name, grid=(R // tr,), semantics=("parallel",),
        in_specs=[pl.BlockSpec((n, tr, C), lambda i: (0, i, 0))], out_specs=_rows(tr, C),
        out_shape=_sds((R, C), a.dtype),
    )(a)


def _add_halves(g, c, b, name, out_dtype):
    n, _, R, C = g.shape
    tr = SUM_ROWS if R % SUM_ROWS == 0 else R

    def body(c_ref, a_ref, b_ref, o_ref):
        o_ref[...] = (a_ref[0] + b_ref[...]).astype(o_ref.dtype)

    spec = pl.BlockSpec((1, tr, C), lambda k, i, c_ref: (k, i, 0))
    grid_spec = pltpu.PrefetchScalarGridSpec(
        num_scalar_prefetch=1, grid=(n, R // tr),
        in_specs=[pl.BlockSpec((1, 1, tr, C), lambda k, i, c_ref: (k, c_ref[0], i, 0)), spec], out_specs=spec)
    return _pcall(body, name=name, semantics=("parallel", "parallel"), grid_spec=grid_spec,
                  out_shape=_sds(b.shape, out_dtype))(c.reshape(1).astype(jnp.int32), g, b)


def _total_sum(mine, theirs, recv, name):
    R, C = mine.shape
    n = recv.shape[0]
    tr = SUM_ROWS if R % SUM_ROWS == 0 else R

    def body(a_ref, b_ref, r_ref, o_ref):
        acc = a_ref[...] + b_ref[...]
        for i in range(n):
            acc = acc + r_ref[i].astype(jnp.float32)
        o_ref[...] = acc

    return _pcall(
        body, name=name, grid=(R // tr,), semantics=("parallel",),
        in_specs=[_rows(tr, C), _rows(tr, C), pl.BlockSpec((n, tr, C), lambda i: (0, i, 0))], out_specs=_rows(tr, C),
        out_shape=_sds((R, C), jnp.float32),
    )(mine, theirs, recv)


def _place():
    return lax.axis_index("x"), lax.axis_index("y"), lax.axis_index("c")


class _Plan:
    def __init__(self, arrays, out_shape, scratch, start, finish, middle=None):
        self.arrays, self.out_shape, self.scratch = list(arrays), list(out_shape), list(scratch)
        self.start, self.finish, self.middle = start, finish, middle


def _gather8_plan(block):
    R, C = block.shape

    def parts(ins, outs, sems):
        (x_ref,), (out_ref,), (send_sems, recv_sems) = ins, outs, sems
        x, y, c = _place()
        me, sibling = (x, y, c), (x, y, 1 - c)
        chips = [(1 - x, y), (x, 1 - y), (1 - x, 1 - y)]

        def copy(k, blk, to, src=None):
            slot = out_ref.at[4 * blk[0] + 2 * blk[1] + blk[2]]
            return pltpu.make_async_remote_copy(
                src_ref=slot if src is None else src, dst_ref=slot,
                send_sem=send_sems.at[k], recv_sem=recv_sems.at[k], device_id=to, device_id_type=MESH_ID)

        def first():
            return [copy(0, me, sibling, src=x_ref)] + [copy(1 + j, me, (*chip, c), src=x_ref) for j, chip in enumerate(chips)]

        def passed():
            return [copy(4 + j, (*chip, c), sibling) for j, chip in enumerate(chips)]

        def arrivals():
            return [copy(1 + j, (*chip, c), me) for j, chip in enumerate(chips)]

        def late():
            return [copy(0, sibling, me)] + [copy(4 + j, (*chip, 1 - c), me) for j, chip in enumerate(chips)]

        return first, passed, arrivals, late

    def start(ins, outs, sems):
        for cp in parts(ins, outs, sems)[0]():
            cp.start()

    def middle(ins, outs, sems):
        _, passed, arrivals, _ = parts(ins, outs, sems)
        for arrived, forward in zip(arrivals(), passed()):
            arrived.wait_recv()
            forward.start()

    def finish(ins, outs, sems):
        first, passed, _, late = parts(ins, outs, sems)
        for cp in late():
            cp.wait_recv()
        for cp in first() + passed():
            cp.wait_send()

    return _Plan([block], [_sds((8, R, C), block.dtype)], [pltpu.SemaphoreType.DMA((7,)), pltpu.SemaphoreType.DMA((7,))],
                 start, finish, middle)


def _fill_own_slot(gathered, block):
    x, y, c = _place()
    return lax.dynamic_update_index_in_dim(gathered, block, 4 * x + 2 * y + c, 0)


def _started_and_waited(arrays, out_shape, n, copies):
    def start(ins, outs, sems):
        for cp in copies(ins, outs, sems):
            cp.start()

    def finish(ins, outs, sems):
        for cp in copies(ins, outs, sems):
            cp.wait()

    return _Plan(arrays, out_shape, [pltpu.SemaphoreType.DMA((n,)), pltpu.SemaphoreType.DMA((n,))], start, finish)


def _pair_swap_plan(g):
    n = g.shape[0]

    def copies(ins, outs, sems):
        (g_ref,), (out_ref,), (send_sems, recv_sems) = ins, outs, sems
        x, y, c = _place()
        return [pltpu.make_async_remote_copy(src_ref=g_ref.at[k, 1 - c], dst_ref=out_ref.at[k], send_sem=send_sems.at[k],
                                             recv_sem=recv_sems.at[k], device_id=(x, y, 1 - c), device_id_type=MESH_ID)
                for k in range(n)]

    return _started_and_waited([g], [_sds((n,) + g.shape[2:], g.dtype)], n, copies)


def _chip_exchange_plan(p):
    def copies(ins, outs, sems):
        (p_ref,), (out_ref,), (send_sems, recv_sems) = ins, outs, sems
        x, y, c = _place()
        chips = [(1 - x, y), (x, 1 - y), (1 - x, 1 - y)]
        return [pltpu.make_async_remote_copy(
            src_ref=p_ref.at[2 * cx + cy], dst_ref=out_ref.at[j], send_sem=send_sems.at[j],
            recv_sem=recv_sems.at[j], device_id=(cx, cy, c), device_id_type=MESH_ID)
            for j, (cx, cy) in enumerate(chips)]

    return _started_and_waited([p], [_sds((3,) + p.shape[1:], p.dtype)], 3, copies)


def _pair_exchange_plan(t):
    def copies(ins, outs, sems):
        (t_ref,), (out_ref,), (send_sems, recv_sems) = ins, outs, sems
        x, y, c = _place()
        return [pltpu.make_async_remote_copy(src_ref=t_ref, dst_ref=out_ref, send_sem=send_sems.at[0], recv_sem=recv_sems.at[0],
                                             device_id=(x, y, 1 - c), device_id_type=MESH_ID)]

    return _started_and_waited([t], [_sds(t.shape, t.dtype)], 1, copies)


ANY_SPEC = pl.BlockSpec(memory_space=pl.ANY)


def _run_plan(plan, name):
    n_in, n_out = len(plan.arrays), len(plan.out_shape)

    def body(*refs):
        ins, outs, sems = refs[:n_in], refs[n_in:n_in + n_out], refs[n_in + n_out:]
        plan.start(ins, outs, sems)
        if plan.middle is not None:
            plan.middle(ins, outs, sems)
        plan.finish(ins, outs, sems)

    return _pcall(body, name=name, in_specs=[ANY_SPEC] * n_in, out_specs=[ANY_SPEC] * n_out, out_shape=plan.out_shape,
                  scratch_shapes=plan.scratch)(*plan.arrays)


def _pcall_riding(body, plan, args, *, name, grid, in_specs, out_specs, out_shape, scratch_shapes):
    if plan is None:
        outs = _pcall(body, name=name, grid=grid, semantics=("arbitrary",), in_specs=in_specs, out_specs=out_specs,
                      out_shape=out_shape, scratch_shapes=scratch_shapes)(*args)
        return list(outs), None
    n_in, n_out, n_s = len(args), len(out_shape), len(scratch_shapes)
    p_in, p_out = len(plan.arrays), len(plan.out_shape)
    steps = grid[0]

    def riding(*refs):
        ins, pins = refs[:n_in], refs[n_in:n_in + p_in]
        o0 = n_in + p_in
        outs, pouts = refs[o0:o0 + n_out], refs[o0 + n_out:o0 + n_out + p_out]
        s0 = o0 + n_out + p_out
        scr, sems = refs[s0:s0 + n_s], refs[s0 + n_s:]
        j = pl.program_id(0)

        @pl.when(j == 0)
        def _():
            plan.start(pins, pouts, sems)

        if plan.middle is not None:
            @pl.when(j == steps // 2)
            def _():
                plan.middle(pins, pouts, sems)

        body(*ins, *outs, *scr)

        @pl.when(j == steps - 1)
        def _():
            plan.finish(pins, pouts, sems)

    res = _pcall(riding, name=name, grid=grid, semantics=("arbitrary",), in_specs=list(in_specs) + [ANY_SPEC] * p_in,
                 out_specs=list(out_specs) + [ANY_SPEC] * p_out, out_shape=list(out_shape) + plan.out_shape,
                 scratch_shapes=list(scratch_shapes) + plan.scratch)(*args, *plan.arrays)
    return list(res[:n_out]), list(res[n_out:])


class _RowSeq:
    def __init__(self, pieces):
        self.pieces = list(pieces)

    def rows(self, a, b):
        out, off = [], 0
        for p in self.pieces:
            lo, hi = max(a, off), min(b, off + p.shape[0])
            if lo < hi:
                out.append(p[lo - off:hi - off])
            off += p.shape[0]
        return out

    def array(self):
        return jnp.concatenate(self.pieces, axis=0)


def _row_seq(w):
    return w if isinstance(w, _RowSeq) else _RowSeq([w])


def _prep_w_in0(wt):
    wt = _row_seq(wt)
    one = wt.pieces[0]
    z32 = [jnp.zeros((32, one.shape[1]), one.dtype)]
    k0, k1 = wt.rows(928, 992), wt.rows(992, 1056)
    v0, v1 = wt.rows(1056, 1120), wt.rows(1120, 1184)
    return jnp.concatenate(wt.rows(0, 384) + z32 + z32 + wt.rows(384, 416) + z32 + wt.rows(416, 928)
                           + k0 * 4 + k1 * 4 + v0 * 4 + v1 * 4 + wt.rows(1184, 2208), axis=0)


def _fold_w_in0(d):
    def fold(blk):
        b = blk.reshape(8, 64, blk.shape[1])
        return jnp.concatenate([b[0] + b[1] + b[2] + b[3], b[4] + b[5] + b[6] + b[7]], axis=0)
    return _RowSeq([d[0:384], d[448:480], d[512:1024], fold(d[1024:1536]), fold(d[1536:2048]), d[2048:3072]])


def _prep_w_q(wt):
    return jnp.pad(wt.reshape(N_MLA, 96, Q_RANK), ((0, 0), (0, 32), (0, 0))).reshape(1024, Q_RANK)


def _fold_w_q(d):
    return d.reshape(N_MLA, 128, Q_RANK)[:, :96].reshape(768, Q_RANK)


def _prep_w_kv(w):
    w3 = w.reshape(KV_RANK, N_MLA, 128)
    kk = jnp.pad(w3[:, :, :64], ((0, 0), (0, 0), (0, 64))).reshape(KV_RANK, 1024)
    return jnp.concatenate([kk, w3[:, :, 64:].reshape(KV_RANK, 512)], axis=1)


def _fold_w_kv(d):
    kk = d[:, :1024].reshape(KV_RANK, N_MLA, 128)[:, :, :64]
    vv = d[:, 1024:].reshape(KV_RANK, N_MLA, 64)
    return jnp.concatenate([kk, vv], axis=2).reshape(KV_RANK, 1024)


def _prep_w_in1(wt):
    wt = _row_seq(wt)
    one = wt.pieces[0]
    return jnp.concatenate(wt.rows(0, 3072) + wt.rows(3088, 4112) + wt.rows(3072, 3088)
                           + [jnp.zeros((112, one.shape[1]), one.dtype)], axis=0)


def _fold_w_in1(d):
    return _RowSeq([d[0:3072], d[4096:4112], d[3072:4096]])


class _Alone:
    def __init__(self, w_out0, o_g_in, w_in1, w_out1):
        self.layer1 = (w_out0, o_g_in, w_in1, w_out1)

    def gather_plan(self):
        return None

    def layer1_weights(self, rode):
        return self.layer1

    def swap_plan(self, grads1):
        return None

    def exchange_plan(self, rode):
        return None

    def finish(self, rode):
        pass


def _local_step(x, pos, target, e_g_in, w_in0, e_g_q, w_q, e_g_kv, w_kv, sinks, b_f, g_final, layer1):
    S = x.shape[0]
    w_in0p, w_qp, w_kvp = _prep_w_in0(w_in0), _prep_w_q(w_q), _prep_w_kv(w_kv)
    slopes = jnp.asarray(2.0 ** (-8.0 * (np.arange(N_SWA, dtype=np.float32) + 1.0) / N_SWA), jnp.float32)
    sinks1 = sinks.reshape(N_SWA)
    b_col = b_f.reshape(N_FOX, 1)

    (h0, cq, ckv, cqn, ckvn, qm, km, vm, qs, kd, vd, gate0, cos, sin) = _layer0_in(
        x, pos, e_g_in, w_in0p, e_g_q, w_qp, e_g_kv, w_kvp)
    o_m, lse_m, rode = _attn_fwd_t(qm, km, vm, (NOPE + ROPE) ** -0.5, split=True, name="mla_fwd", plan=layer1.gather_plan())
    w_out0, o_g_in, w_in1, w_out1 = layer1.layer1_weights(rode)
    w_in1p = _prep_w_in1(w_in1)
    o_s, lse_s = _swa_fwd(qs, kd, vd, sinks1, slopes)
    x1, u0, h1, q1, k1, v1, gate1, f_slab = _layer0_out_layer1_in(x, o_m, o_s, gate0, w_out0, o_g_in, w_in1p)
    f_row = f_slab[:, :N_FOX].T
    lc_row = _forget_fwd(f_row, b_col)
    lcc = lc_row.reshape(N_FOX, S, 1)
    o1, lse1, _ = _attn_fwd_t(q1, k1, v1, HEAD ** -0.5, split=False, name="fox_fwd", lcc=lcc)
    loss8, dg_final, dx2, u1, do1, dgate1 = _head(x1, o1, gate1, w_out1, g_final, target)

    dq1, dk1, dv1, dlc, _ = _attn_bwd_t(q1, k1, v1, do1, o1, lse1, HEAD ** -0.5, split=False, name="fox_bwd", lcc=lcc)
    df_row, db_f = _forget_bwd(dlc.reshape(N_FOX, S), f_row, b_col)
    df_slab = jnp.pad(df_row.T, ((0, 0), (0, LANES - N_FOX))).astype(MXU)
    dz1, dx1, dg_o_in, do_m, do_s, dgate0 = _layer1_in_bwd(
        dq1, dk1, dv1, dgate1, df_slab, x1, dx2, o_g_in, w_in1p, gate0, o_m, o_s, w_out0)
    grads1 = dict(o_g_in=dg_o_in, o_w_in=_fold_w_in1(_wgrad(dz1, h1, "wgrad_in1")), o_w_out=_wgrad(u1, dx2, "wgrad_out1"),
                  e_w_out=_wgrad(u0, dx1, "wgrad_out0"))
    dqs, dkd, dvd, dsink, rode = _swa_bwd(qs, kd, vd, do_s, o_s, lse_s, sinks1, slopes, plan=layer1.swap_plan(grads1))
    dqm, dkm, dvm, rode = _attn_bwd_t(qm, km, vm, do_m, o_m, lse_m, (NOPE + ROPE) ** -0.5, split=True, name="mla_bwd",
                                      plan=layer1.exchange_plan(rode))
    layer1.finish(rode)
    dx, dz0, dqu, dkvu, dg_in, dg_q, dg_kv = _layer0_in_bwd(
        dqm, dkm, dvm, dqs, dkd, dvd, dgate0, cos, sin, cq, ckv, x, dx1, e_g_in, w_in0p, e_g_q, w_qp, e_g_kv, w_kvp)

    grads = dict(
        e_g_in=dg_in,
        e_w_in=_fold_w_in0(_wgrad(dz0, h0, "wgrad_in0")),
        e_g_q_a=dg_q,
        e_w_q_up=_fold_w_q(_wgrad(dqu, cqn, "wgrad_q_up")),
        e_g_kv_a=dg_kv,
        e_w_kv_up=_fold_w_kv(_wgrad(ckvn, dkvu, "wgrad_kv_up")),
        e_sinks=dsink[:, 0:2, 0].reshape(1, N_SWA),
        o_b_f=db_f.reshape(1, N_FOX),
        g_final=dg_final,
        **grads1,
    )
    return loss8[0, 0], dx, grads


SHARDED = ("e_w_in", "e_w_q_up", "e_w_kv_up", "e_w_out", "o_g_in", "o_w_in", "o_w_out")
TRANSPOSED = ("e_w_in", "e_w_q_up", "o_w_in")
COL_SHARDED = ("e_w_kv_up", "o_g_in")
REPLICATED = ("e_g_in", "e_g_q_a", "e_g_kv_a", "e_sinks", "o_b_f", "g_final")
FULL_SHAPES = dict(e_w_in=(2208, 1024), e_w_q_up=(768, 256), e_w_kv_up=(128, 1024), e_w_out=(1024, 1024),
                   o_g_in=(1, 1024), o_w_in=(4112, 1024), o_w_out=(1024, 1024))
GROUPS = dict(
    layer0=dict(rows=768, windows=dict(e_w_in=(0, 0), e_w_q_up=(560, 0), e_w_kv_up=(560, 256))),
    layer1=dict(rows=1568, windows=dict(o_w_in=(0, 0), o_w_out=(1040, 0), e_w_out=(1296, 0), o_g_in=(1552, 0))),
)


def _shard_shape(name):
    r, c = FULL_SHAPES[name]
    return (r, c // 4) if name in COL_SHARDED else (r // 4, c)


def _as_handled(name, a):
    a = a[0] if a.ndim == 3 else a
    return a.T if name in TRANSPOSED else a


def _as_given(name, a, shape):
    return (a.T if name in TRANSPOSED else a).reshape(shape)


def _pack_block(p, group):
    def rows(a, n):
        return jnp.pad(a, ((0, n - a.shape[0]), (0, 0)))

    if group == "layer0":
        band = jnp.concatenate([p["e_w_q_up"], rows(p["e_w_kv_up"], 192), jnp.zeros((192, 512), p["e_w_in"].dtype)], axis=1)
        return jnp.concatenate([rows(p["e_w_in"], 560), rows(band, 208)], axis=0)
    g = p["o_g_in"]
    band = jnp.pad(g, ((0, 16 - g.shape[0]), (0, PACK_COLS - g.shape[1])))
    return jnp.concatenate([rows(p["o_w_in"], 1040), p["o_w_out"], p["e_w_out"], band], axis=0)


def _window(block, group, name, width=None):
    r0, c0 = GROUPS[group]["windows"][name]
    r, c = _shard_shape(name)
    return block[..., r0:r0 + r, c0:c0 + (c if width is None else width)]


def _chip_slice(name, full, k):
    r, c = _shard_shape(name)
    if isinstance(full, _RowSeq):
        return jnp.concatenate(full.rows(r * k, r * (k + 1)), axis=0)
    return full[:, c * k:c * (k + 1)] if name in COL_SHARDED else full[r * k:r * (k + 1), :]


def _packed_weights(w, group):
    parts = {}
    for n in GROUPS[group]["windows"]:
        a = _as_handled(n, w[n])
        parts[n] = lax.bitcast_convert_type(a, jnp.bfloat16).reshape(1, -1) if n == "o_g_in" else a.astype(jnp.bfloat16)
    halves = _pack_block(parts, group).reshape(2, GROUPS[group]["rows"] // 2, PACK_COLS)
    return lax.dynamic_index_in_dim(halves, lax.axis_index("c"), 0, keepdims=False)


def _unpacked_weights(gathered, half, group):
    blocks = _fill_own_slot(gathered, half).reshape(4, GROUPS[group]["rows"], PACK_COLS)
    full = {}
    for n in GROUPS[group]["windows"]:
        if n == "o_g_in":
            halves = _window(blocks, group, n, width=512).reshape(4, 1, 256, 2)
            full[n] = jnp.concatenate(list(lax.bitcast_convert_type(halves, jnp.float32)), axis=1)
        else:
            pieces = [_window(blocks[k], group, n).astype(MXU) for k in range(4)]
            if n in ("e_w_in", "o_w_in"):
                full[n] = _RowSeq(pieces)
            else:
                full[n] = jnp.concatenate(pieces, axis=1 if n in COL_SHARDED else 0)
    return full


class _GroupReduce:
    def __init__(self, group):
        self.group = group
        self.c = lax.axis_index("c")
        self.chip = 2 * lax.axis_index("x") + lax.axis_index("y")

    def swap_plan(self, grads):
        names = GROUPS[self.group]["windows"]
        per_chip = jnp.stack([_pack_block({n: _chip_slice(n, grads[n], k) for n in names}, self.group) for k in range(4)])
        self.g4 = per_chip.reshape(4, 2, GROUPS[self.group]["rows"] // 2, PACK_COLS)
        return _pair_swap_plan(self.g4)

    def exchange_plan(self, rode):
        theirs = rode[0]
        rows = self.g4.shape[2]
        self.own = (lax.dynamic_slice(self.g4, (self.chip, self.c, 0, 0), (1, 1, rows, PACK_COLS)).reshape(rows, PACK_COLS),
                    lax.dynamic_index_in_dim(theirs, self.chip, 0, keepdims=False))
        return _chip_exchange_plan(_add_halves(self.g4, self.c, theirs, "pair_add_" + self.group, jnp.bfloat16))

    def finish(self, rode):
        my_half = _total_sum(*self.own, rode[0], "chip_sum_" + self.group)
        other_half = _run_plan(_pair_exchange_plan(my_half), "pair_exchange_" + self.group)[0]
        total = jnp.concatenate([jnp.where(self.c == 0, my_half, other_half), jnp.where(self.c == 0, other_half, my_half)], axis=0)
        self.sums = {n: _window(total, self.group, n) for n in GROUPS[self.group]["windows"]}

    def run(self, grads):
        rode = _run_plan(self.swap_plan(grads), "pair_swap_" + self.group)
        self.finish(_run_plan(self.exchange_plan(rode), "chip_exchange_" + self.group))
        return self.sums


class _Layer1Exchange(_GroupReduce):
    def __init__(self, w):
        super().__init__("layer1")
        self.half = _packed_weights(w, "layer1")

    def gather_plan(self):
        return _gather8_plan(self.half)

    def layer1_weights(self, rode):
        full = _unpacked_weights(rode[0], self.half, "layer1")
        return full["e_w_out"], full["o_g_in"], full["o_w_in"], full["o_w_out"]


def kernel(x, positions, e_g_in, e_w_in, e_g_q_a, e_w_q_up, e_g_kv_a, e_w_kv_up, e_sinks, e_w_out, o_g_in, o_w_in, o_b_f, o_w_out, g_final, loss_target, m_e_g_in, m_e_w_in, m_e_g_q_a, m_e_w_q_up, m_e_g_kv_a, m_e_w_kv_up, m_e_sinks, m_e_w_out, m_o_g_in, m_o_w_in, m_o_b_f, m_o_w_out, m_g_final, v_e_g_in, v_e_w_in, v_e_g_q_a, v_e_w_q_up, v_e_g_kv_a, v_e_w_kv_up, v_e_sinks, v_e_w_out, v_o_g_in, v_o_w_in, v_o_b_f, v_o_w_out, v_g_final):
    w = dict(e_g_in=e_g_in, e_w_in=e_w_in, e_g_q_a=e_g_q_a, e_w_q_up=e_w_q_up, e_g_kv_a=e_g_kv_a, e_w_kv_up=e_w_kv_up,
             e_sinks=e_sinks, e_w_out=e_w_out, o_g_in=o_g_in, o_w_in=o_w_in, o_b_f=o_b_f, o_w_out=o_w_out, g_final=g_final)
    m = dict(e_g_in=m_e_g_in, e_w_in=m_e_w_in, e_g_q_a=m_e_g_q_a, e_w_q_up=m_e_w_q_up, e_g_kv_a=m_e_g_kv_a,
             e_w_kv_up=m_e_w_kv_up, e_sinks=m_e_sinks, e_w_out=m_e_w_out, o_g_in=m_o_g_in, o_w_in=m_o_w_in, o_b_f=m_o_b_f,
             o_w_out=m_o_w_out, g_final=m_g_final)
    v = dict(e_g_in=v_e_g_in, e_w_in=v_e_w_in, e_g_q_a=v_e_g_q_a, e_w_q_up=v_e_w_q_up, e_g_kv_a=v_e_g_kv_a,
             e_w_kv_up=v_e_w_kv_up, e_sinks=v_e_sinks, e_w_out=v_e_w_out, o_g_in=v_o_g_in, o_w_in=v_o_w_in, o_b_f=v_o_b_f,
             o_w_out=v_o_w_out, g_final=v_g_final)
    order = ("e_g_in", "e_w_in", "e_g_q_a", "e_w_q_up", "e_g_kv_a", "e_w_kv_up", "e_sinks", "e_w_out", "o_g_in", "o_w_in",
             "o_b_f", "o_w_out", "g_final")
    half0 = _packed_weights(w, "layer0")
    full = _unpacked_weights(_run_plan(_gather8_plan(half0), "gather_weights_layer0")[0], half0, "layer0")
    layer1 = _Layer1Exchange(w)

    loss_part, dx, grads = _local_step(
        x[0], positions.reshape(-1, 1), loss_target[0], e_g_in, full["e_w_in"], e_g_q_a, full["e_w_q_up"], e_g_kv_a,
        full["e_w_kv_up"], e_sinks, o_b_f, g_final.reshape(1, D), layer1)

    gsum = {**layer1.sums, **_GroupReduce("layer0").run(grads)}

    small = jnp.concatenate([jnp.pad(loss_part.reshape(1), (0, LANES - 1))]
                            + [jnp.pad(grads[n].reshape(-1), (0, (-grads[n].size) % LANES)) for n in REPLICATED])
    rows = small.shape[0] // LANES
    small = jnp.pad(small.reshape(rows, LANES), ((0, (-rows) % 8), (0, 0)))
    gathered_small = _fill_own_slot(_run_plan(_gather8_plan(small), "gather_small_grads")[0], small)
    ssum = _sum_leading(gathered_small, "small_grad_sum").reshape(-1)
    loss = ssum[0]
    off = LANES
    for n in REPLICATED:
        cnt = w[n].size
        gsum[n] = ssum[off:off + cnt].reshape(w[n].shape)
        off += cnt + (-cnt) % LANES

    grad, delta, new_m, new_v = {}, {}, {}, {}
    for n in order:
        if n in SHARDED:
            outs = _adamw(_as_handled(n, w[n]), gsum[n], _as_handled(n, m[n]), _as_handled(n, v[n]), "adamw_" + n)
            grad[n], delta[n], new_m[n], new_v[n] = (_as_given(n, a, w[n].shape) for a in (gsum[n],) + outs)
        else:
            grad[n] = gsum[n]
            delta[n], new_m[n], new_v[n] = _adamw(w[n], gsum[n], m[n], v[n], "adamw_" + n)
    return (loss, dx[None], *[grad[n] for n in order], *[delta[n] for n in order], *[new_m[n] for n in order],
            *[new_v[n] for n in order])
```

```python
import numpy as np
import jax
import jax.numpy as jnp
from jax import lax
from jax.experimental import pallas as pl
from jax.experimental.pallas import tpu as pltpu

D = 1024
EPS = 1e-6
ROPE_THETA = 10000.0
N_MLA = 8
Q_RANK = 256
KV_RANK = 128
NOPE = 64
ROPE = 32
N_SWA = 8
WINDOW = 128
N_FOX = 16
HEAD = 64
LR, B1, B2, AEPS, WD, STEP = 0.001, 0.9, 0.999, 1e-08, 0.01, 10

LANES = 128
HALF = 64
VMEM_LIMIT = 56 * 1024 * 1024
MXU = jnp.bfloat16
TOK = 256
WG_TOK = 2048
WG_ROWS = 1536
ATT = 256
FWD_CHUNK = 2
BWD_CHUNK = 2
SWA_GROUP = 4
NEG = float("-inf")

PACK_COLS = 1024
SUM_ROWS = 256
ADAM_TILE_BYTES = 2 << 20
MESH_ID = pl.DeviceIdType.MESH


def _pcall(body, *, name, vmem=VMEM_LIMIT, semantics=None, **kw):
    params = dict(vmem_limit_bytes=vmem)
    if semantics is not None:
        params["dimension_semantics"] = semantics
    return pl.pallas_call(body, name=name, compiler_params=pltpu.CompilerParams(**params), **kw)


def _mm(a, b):
    return jnp.dot(a.astype(MXU), b.astype(MXU), preferred_element_type=jnp.float32)


def _mm_nt(a, b):
    return lax.dot_general(a.astype(MXU), b.astype(MXU), (((1,), (1,)), ((), ())),
                           preferred_element_type=jnp.float32)


def _mm_tn(a, b):
    return lax.dot_general(a.astype(MXU), b.astype(MXU), (((0,), (0,)), ((), ())),
                           preferred_element_type=jnp.float32)


def _full(shape):
    n = len(shape)
    return pl.BlockSpec(shape, lambda *_: (0,) * n)


def _rows(tm, n):
    return pl.BlockSpec((tm, n), lambda i: (i, 0))


def _sds(shape, dtype):
    return jax.ShapeDtypeStruct(shape, dtype)


def _rms(x, g):
    r = lax.rsqrt(jnp.mean(x * x, axis=-1, keepdims=True) + EPS)
    return x * r * g


def _rms_bwd(x, g, dy):
    r = lax.rsqrt(jnp.mean(x * x, axis=-1, keepdims=True) + EPS)
    xh = x * r
    dxh = dy * g
    dx = r * (dxh - xh * jnp.mean(dxh * xh, axis=-1, keepdims=True))
    return dx, dy * xh


def _sigmoid(x):
    return 1.0 / (1.0 + jnp.exp(-x))


def _lane_masks():
    lane = lax.broadcasted_iota(jnp.int32, (1, LANES), 1)
    return lane < HALF


def _split_heads(a, lo):
    z = jnp.zeros_like(a)
    return [jnp.where(lo, a, z), jnp.where(lo, z, a)]


def _rope_consts():
    inv = np.zeros((8, LANES), np.float32)
    j = np.arange(ROPE // 2, dtype=np.float32)
    f = (1.0 / (ROPE_THETA ** (np.arange(0, ROPE, 2, dtype=np.float32) / ROPE))).astype(np.float32)
    inv[0, HALF:HALF + 16] = f
    inv[0, HALF + 16:HALF + 32] = f
    inv[1, HALF:HALF + 16] = -1.0
    inv[1, HALF + 16:HALF + 32] = 1.0
    del j
    return jnp.asarray(inv)


def _rope_tables(pos_f, consts):
    ang = pos_f * consts[0:1, :]
    sign = consts[1:2, :]
    c = jnp.where(sign != 0.0, jnp.cos(ang), 1.0)
    s = jnp.sin(ang) * sign
    return c, s


def _swap_halves(v, sign):
    lo = pltpu.roll(v, LANES - 16, axis=1)
    hi = pltpu.roll(v, 16, axis=1)
    return jnp.where(sign < 0.0, lo, jnp.where(sign > 0.0, hi, 0.0))


def _rope(x, c, s, sign):
    return x * c + _swap_halves(x, sign) * s


def _rope_t(dy, c, s, sign):
    return dy * c + _swap_halves(dy * s, sign)


def _layer0_in(x, pos, g_in, w_in, g_q, w_q, g_kv, w_kv):
    S = x.shape[0]
    consts = _rope_consts()

    def body(x_ref, pos_ref, c_ref, g_ref, w_ref, gq_ref, wq_ref, gkv_ref, wkv_ref,
             h_ref, cq_ref, ckv_ref, qm_ref, km_ref, vm_ref,
             qs_ref, kd_ref, vd_ref, gate_ref, cos_ref, sin_ref):
        h = _rms(x_ref[...], g_ref[...])
        h_ref[...] = h.astype(h_ref.dtype)
        z = _mm_nt(h, w_ref[...])
        cq = z[:, 0:256]
        ckv = z[:, 256:384]
        kpe = z[:, 384:512]
        cq_ref[...] = cq
        ckv_ref[...] = ckv
        qs_ref[...] = z[:, 512:1024].astype(qs_ref.dtype)
        kd_ref[...] = z[:, 1024:1536].astype(kd_ref.dtype)
        vd_ref[...] = z[:, 1536:2048].astype(vd_ref.dtype)
        gate_ref[...] = z[:, 2048:3072]
        cqn = _rms(cq, gq_ref[...])
        ckvn = _rms(ckv, gkv_ref[...])
        q = _mm_nt(cqn, wq_ref[...])
        kv = _mm(ckvn, wkv_ref[...])
        vm_ref[...] = kv[:, 1024:1536].astype(vm_ref.dtype)
        consts_v = c_ref[...]
        sign = consts_v[1:2, :]
        c, s = _rope_tables(pos_ref[...].astype(jnp.float32), consts_v)
        cos_ref[...] = c
        sin_ref[...] = s
        kpe_r = _rope(kpe, c, s, sign)
        for hd in range(N_MLA):
            sl = slice(LANES * hd, LANES * (hd + 1))
            qm_ref[:, sl] = _rope(q[:, sl], c, s, sign).astype(qm_ref.dtype)
            km_ref[:, sl] = (kv[:, sl] + kpe_r).astype(km_ref.dtype)

    outs = [
        ((S, D), MXU), ((S, 256), jnp.float32), ((S, 128), jnp.float32),
        ((S, 1024), MXU), ((S, 1024), MXU), ((S, 512), MXU), ((S, 512), MXU), ((S, 512), MXU), ((S, 512), MXU),
        ((S, 1024), jnp.float32), ((S, 128), jnp.float32), ((S, 128), jnp.float32),
    ]
    return _pcall(
        body, name="layer0_in", grid=(S // TOK,), semantics=("arbitrary",),
        in_specs=[_rows(TOK, D), _rows(TOK, 1), _full((8, LANES)), _full((1, D)), _full(w_in.shape), _full((1, 256)),
                  _full(w_q.shape), _full((1, 128)), _full(w_kv.shape)],
        out_specs=[_rows(TOK, s[1]) for s, _ in outs],
        out_shape=[_sds(s, d) for s, d in outs],
    )(x, pos, consts, g_in, w_in, g_q, w_q, g_kv, w_kv)


AUG = (HALF, 0)
ONE = (HALF + 8, 8)


def _data_lanes(idx, h):
    return (idx < HALF) if h == 0 else (idx >= HALF)


def _three_terms(x):
    hi = x.astype(MXU).astype(jnp.float32)
    mid = (x - hi).astype(MXU).astype(jnp.float32)
    lo = (x - hi - mid).astype(MXU).astype(jnp.float32)
    return hi, mid, lo


def _q_aug(qblk, lc, h, scale, lane):
    a = AUG[h]
    hi, mid, lo = _three_terms(lc)
    ones = ((lane >= a + 3) & (lane <= a + 5)).astype(jnp.float32)
    aug = jnp.where(lane == a, hi, jnp.where(lane == a + 1, mid, jnp.where(lane == a + 2, lo, ones)))
    return jnp.where(_data_lanes(lane, h), qblk * jnp.asarray(scale, qblk.dtype), aug.astype(qblk.dtype))


def _k_aug(kblk, lc, h, lane):
    a = AUG[h]
    hi, mid, lo = _three_terms(-lc)
    ones = ((lane >= a) & (lane <= a + 2)).astype(jnp.float32)
    aug = jnp.where(lane == a + 3, hi, jnp.where(lane == a + 4, mid, jnp.where(lane == a + 5, lo, ones)))
    return jnp.where(_data_lanes(lane, h), kblk, aug.astype(kblk.dtype))


def _attn_fwd_t(q, k, v, scale, *, split, name, lcc=None, plan=None):
    S = q.shape[0]
    npair = v.shape[1] // LANES
    W = 2 * LANES if split else LANES
    T = ATT
    CH = FWD_CHUNK * T
    assert S % CH == 0
    nq = S // T

    def body(*refs):
        if split:
            q_ref, k_ref, v_ref, o_ref, lse_ref, vt, acc, m_sc = refs
        else:
            q_ref, k_ref, v_ref, lcc_ref, o_ref, lse_ref, kaug, vt, acc, m_sc = refs
        lane = lax.broadcasted_iota(jnp.int32, (1, LANES), 1)
        sub = lax.broadcasted_iota(jnp.int32, (LANES, 1), 0)
        key_minus_qry = lax.broadcasted_iota(jnp.int32, (CH, T), 0) - lax.broadcasted_iota(jnp.int32, (CH, T), 1)

        def prep(i, c):
            r0 = pl.multiple_of(i * T, T)
            vblk = v_ref[pl.ds(r0, T), :].astype(jnp.float32)
            for h in (0, 1):
                vh = jnp.where(_data_lanes(lane, h), vblk, (lane == ONE[h]).astype(jnp.float32))
                vt[h, :, pl.ds(r0, T)] = vh.T.astype(vt.dtype)
                if not split:
                    kaug[h, pl.ds(r0, T), :] = _k_aug(k_ref[pl.ds(r0, T), :], lcc_ref[h, pl.ds(r0, T), :], h, lane)
            return c

        lax.fori_loop(0, nq, prep, 0)

        def queries(qi):
            q0 = pl.multiple_of(qi * T, T)
            qblk = q_ref[pl.ds(q0, T), :]
            if split:
                return (qblk[:, :LANES], qblk[:, LANES:])
            return tuple(_q_aug(qblk, lcc_ref[h, pl.ds(q0, T), :], h, scale, lane) for h in (0, 1))

        def scores(qs, c):
            k0 = pl.multiple_of(c * CH, CH)
            out = []
            for h in (0, 1):
                if split:
                    out.append(_mm_nt(k_ref[pl.ds(k0, CH), LANES * h:LANES * (h + 1)], qs[h]) * scale)
                else:
                    out.append(_mm_nt(kaug[h, pl.ds(k0, CH), :], qs[h]))
            return tuple(out)

        def q_block(qi, carry):
            qs, first_scores = carry[:2], carry[2:]
            q0 = pl.multiple_of(qi * T, T)
            acc[...] = jnp.zeros_like(acc)
            m_sc[...] = jnp.full(m_sc.shape, NEG, jnp.float32)

            def absorb(c, sts, masked):
                k0 = pl.multiple_of(c * CH, CH)
                for h in (0, 1):
                    st = sts[h]
                    if masked:
                        st = jnp.where(key_minus_qry <= q0 - k0, st, NEG)
                    m_old = m_sc[h:h + 1, :]
                    m_new = jnp.maximum(m_old, jnp.max(st, axis=0, keepdims=True))
                    alpha = jnp.exp(m_old - m_new)
                    pt = jnp.exp(st - m_new)
                    acc[h] = alpha * acc[h] + _mm(vt[h, :, pl.ds(k0, CH)], pt)
                    m_sc[h:h + 1, :] = m_new

            last = qi // FWD_CHUNK

            def pipelined(c, sts):
                nxt = scores(qs, c + 1)
                absorb(c, sts, False)
                return nxt

            sts = lax.fori_loop(0, last, pipelined, first_scores)
            qs_next = queries(jnp.minimum(qi + 1, nq - 1))
            nxt = qs_next + scores(qs_next, 0)
            absorb(last, sts, True)
            ot = None
            for h in (0, 1):
                a = acc[h]
                l = a[ONE[h]:ONE[h] + 1, :]
                oh = jnp.where(_data_lanes(sub, h), a * (1.0 / l), 0.0)
                ot = oh if ot is None else ot + oh
                lse_ref[0, h:h + 1, pl.ds(q0, T)] = m_sc[h:h + 1, :] + jnp.log(l)
            o_ref[pl.ds(q0, T), :] = ot.T
            return nxt

        qs0 = queries(0)
        lax.fori_loop(0, nq, q_block, qs0 + scores(qs0, 0))

    wide = pl.BlockSpec((S, W), lambda j: (0, j))
    slab = pl.BlockSpec((S, LANES), lambda j: (0, j))
    rows = pl.BlockSpec((1, 2, S), lambda j: (j, 0, 0))
    in_specs = [wide, wide, slab]
    args = [q, k, v]
    scratch = []
    if not split:
        in_specs.append(pl.BlockSpec((2, S, 1), lambda j: (j, 0, 0)))
        args.append(lcc)
        scratch.append(pltpu.VMEM((2, S, LANES), MXU))
    scratch += [pltpu.VMEM((2, LANES, S), MXU), pltpu.VMEM((2, LANES, T), jnp.float32), pltpu.VMEM((8, T), jnp.float32)]
    (o, lse), rode = _pcall_riding(
        body, plan, args, name=name, grid=(npair,), in_specs=in_specs, out_specs=[slab, rows],
        out_shape=[_sds((S, npair * LANES), jnp.float32), _sds((npair, 2, S), jnp.float32)], scratch_shapes=scratch)
    return o, lse, rode


def _attn_bwd_t(q, k, v, do, o, lse, scale, *, split, name, lcc=None, plan=None):
    S = q.shape[0]
    npair = v.shape[1] // LANES
    W = 2 * LANES if split else LANES
    T = ATT
    CH = BWD_CHUNK * T
    assert S % CH == 0
    nq = S // T

    def body(*refs):
        if split:
            (q_ref, k_ref, v_ref, do_ref, o_ref, lse_ref, dq_ref, dk_ref, dv_ref, dqt, delta, dk_acc, dv_acc) = refs
        else:
            (q_ref, k_ref, v_ref, do_ref, o_ref, lse_ref, lcc_ref, dq_ref, dk_ref, dv_ref, dlc_ref,
             dqt, delta, dk_acc, dv_acc, qaug, csum) = refs
        lane = lax.broadcasted_iota(jnp.int32, (1, LANES), 1)
        sub = lax.broadcasted_iota(jnp.int32, (LANES, 1), 0)
        key_minus_qry = lax.broadcasted_iota(jnp.int32, (T, CH), 0) - lax.broadcasted_iota(jnp.int32, (T, CH), 1)

        def prep(i, c):
            r0 = pl.multiple_of(i * T, T)
            prod_t = (do_ref[pl.ds(r0, T), :].astype(jnp.float32) * o_ref[pl.ds(r0, T), :]).T
            for h in (0, 1):
                delta[h:h + 1, pl.ds(r0, T)] = jnp.sum(jnp.where(_data_lanes(sub, h), prod_t, 0.0), axis=0, keepdims=True)
                dqt[h, :, pl.ds(r0, T)] = jnp.zeros((LANES, T), jnp.float32)
                if not split:
                    qaug[h, pl.ds(r0, T), :] = _q_aug(q_ref[pl.ds(r0, T), :], lcc_ref[h, pl.ds(r0, T), :], h, scale, lane)
            return c

        lax.fori_loop(0, nq, prep, 0)

        def keys(ki):
            k0 = pl.multiple_of(ki * T, T)
            kblk = k_ref[pl.ds(k0, T), :]
            if split:
                return (kblk[:, :LANES], kblk[:, LANES:])
            return tuple(_k_aug(kblk, lcc_ref[h, pl.ds(k0, T), :], h, lane) for h in (0, 1))

        def q_of(c, h):
            q0 = pl.multiple_of(c * CH, CH)
            if split:
                return q_ref[pl.ds(q0, CH), LANES * h:LANES * (h + 1)]
            return qaug[h, pl.ds(q0, CH), :]

        def scores(khs, c):
            out = []
            for h in (0, 1):
                st = _mm_nt(khs[h], q_of(c, h))
                out.append(st * scale if split else st)
            return tuple(out)

        def k_block(ki, carry):
            khs, first_scores = carry[:2], carry[2:]
            k0 = pl.multiple_of(ki * T, T)
            khts = [kh.astype(jnp.float32).T.astype(kh.dtype) for kh in khs]
            vhs = _split_heads(v_ref[pl.ds(k0, T), :], lane < HALF)
            dk_acc[...] = jnp.zeros_like(dk_acc)
            dv_acc[...] = jnp.zeros_like(dv_acc)

            def absorb(c, vals):
                q0 = pl.multiple_of(c * CH, CH)
                dos = _split_heads(do_ref[pl.ds(q0, CH), :], lane < HALF)
                visible = key_minus_qry <= q0 - k0
                for h in (0, 1):
                    dpt = _mm_nt(vhs[h], dos[h])
                    st = jnp.where(visible, vals[h], NEG)
                    pt = jnp.exp(st - lse_ref[0, h:h + 1, pl.ds(q0, CH)])
                    dv_acc[...] += _mm(pt, dos[h])
                    dst = pt * (dpt - delta[h:h + 1, pl.ds(q0, CH)])
                    dk_acc[h] += _mm(dst, q_of(c, h))
                    dqt[h, :, pl.ds(q0, CH)] += _mm(khts[h], dst)

            first = ki // BWD_CHUNK

            def pipelined(c, vals):
                nxt = scores(khs, c + 1)
                absorb(c, vals)
                return nxt

            vals = lax.fori_loop(first, S // CH - 1, pipelined, first_scores)
            kn = jnp.minimum(ki + 1, nq - 1)
            khs_next = keys(kn)
            nxt = khs_next + scores(khs_next, kn // BWD_CHUNK)
            absorb(S // CH - 1, vals)
            if split:
                dk_ref[pl.ds(k0, T), :LANES] = (dk_acc[0] * scale).astype(dk_ref.dtype)
                dk_ref[pl.ds(k0, T), LANES:] = (dk_acc[1] * scale).astype(dk_ref.dtype)
            else:
                dk_ref[pl.ds(k0, T), :] = jnp.where(lane < HALF, dk_acc[0], dk_acc[1]).astype(dk_ref.dtype)
                for h in (0, 1):
                    csum[h:h + 1, pl.ds(k0, T)] = dk_acc[h].T[AUG[h] + 3:AUG[h] + 4, :]
            dv_ref[pl.ds(k0, T), :] = dv_acc[...].astype(dv_ref.dtype)
            return nxt

        khs0 = keys(0)
        lax.fori_loop(0, nq, k_block, khs0 + scores(khs0, 0))

        def finish(i, c):
            r0 = pl.multiple_of(i * T, T)
            if split:
                for h in (0, 1):
                    dq_ref[pl.ds(r0, T), LANES * h:LANES * (h + 1)] = (dqt[h, :, pl.ds(r0, T)].T * scale).astype(dq_ref.dtype)
            else:
                d = jnp.where(sub < HALF, dqt[0, :, pl.ds(r0, T)], dqt[1, :, pl.ds(r0, T)])
                dq_ref[pl.ds(r0, T), :] = (d.T * scale).astype(dq_ref.dtype)
                for h in (0, 1):
                    dlc_ref[0, h:h + 1, pl.ds(r0, T)] = dqt[h, AUG[h]:AUG[h] + 1, pl.ds(r0, T)] - csum[h:h + 1, pl.ds(r0, T)]
            return c

        lax.fori_loop(0, nq, finish, 0)

    wide = pl.BlockSpec((S, W), lambda j: (0, j))
    slab = pl.BlockSpec((S, LANES), lambda j: (0, j))
    rows = pl.BlockSpec((1, 2, S), lambda j: (j, 0, 0))
    in_specs = [wide, wide, slab, slab, slab, rows]
    args = [q, k, v, do, o, lse]
    out_specs = [wide, wide, slab]
    out_shape = [_sds(q.shape, jnp.float32 if split else do.dtype), _sds(k.shape, jnp.float32 if split else do.dtype),
                 _sds(v.shape, do.dtype)]
    scratch = [pltpu.VMEM((2, LANES, S), jnp.float32), pltpu.VMEM((8, S), jnp.float32),
               pltpu.VMEM((2, T, LANES), jnp.float32), pltpu.VMEM((T, LANES), jnp.float32)]
    if not split:
        in_specs.append(pl.BlockSpec((2, S, 1), lambda j: (j, 0, 0)))
        args.append(lcc)
        out_specs.append(rows)
        out_shape.append(_sds((npair, 2, S), jnp.float32))
        scratch += [pltpu.VMEM((2, S, LANES), MXU), pltpu.VMEM((8, S), jnp.float32)]
    outs, rode = _pcall_riding(body, plan, args, name=name, grid=(npair,), in_specs=in_specs, out_specs=out_specs,
                               out_shape=out_shape, scratch_shapes=scratch)
    return (*outs, rode)


def _swa_bias(slope, shift):
    a = lax.broadcasted_iota(jnp.int32, (WINDOW, 2 * WINDOW), 0)
    c = lax.broadcasted_iota(jnp.int32, (WINDOW, 2 * WINDOW), 1)
    dist = a - c + shift
    return jnp.where((dist >= 0) & (dist < WINDOW), -slope * dist.astype(jnp.float32), NEG)


def _swa_scores(qh, kblk, bias):
    return _mm_nt(qh, kblk) * (HEAD ** -0.5) + bias


def _swa_fwd(q, kd, vd, sinks, slopes):
    S = q.shape[0]
    npair = q.shape[1] // LANES
    nb = S // WINDOW

    def body(sink_ref, slope_ref, q_ref, k_ref, v_ref, o_ref, lse_ref):
        j = pl.program_id(0)
        lo = _lane_masks()
        biases = [(_swa_bias(slope_ref[2 * j + h], 0), _swa_bias(slope_ref[2 * j + h], WINDOW)) for h in (0, 1)]

        def q_block(qi, c):
            q0 = pl.multiple_of(qi * WINDOW, WINDOW)
            k0 = pl.multiple_of(jnp.maximum(qi - 1, 0) * WINDOW, WINDOW)
            qs = _split_heads(q_ref[pl.ds(q0, WINDOW), :], lo)
            kblk = k_ref[pl.ds(k0, 2 * WINDOW), :]
            vs = _split_heads(v_ref[pl.ds(k0, 2 * WINDOW), :], lo)
            o = None
            for h in (0, 1):
                sink = sink_ref[2 * j + h]
                s = _swa_scores(qs[h], kblk, jnp.where(qi == 0, *biases[h]))
                m = jnp.maximum(jnp.max(s, axis=1, keepdims=True), sink)
                p = jnp.exp(s - m)
                den = jnp.sum(p, axis=1, keepdims=True) + jnp.exp(sink - m)
                oh = _mm(p / den, vs[h])
                o = oh if o is None else o + oh
                lse_ref[h, pl.ds(q0, WINDOW), :] = m + jnp.log(den)
            o_ref[pl.ds(q0, WINDOW), :] = o
            return c

        def q_group(gi, c):
            for g in range(SWA_GROUP):
                q_block(gi * SWA_GROUP + g, c)
            return c

        lax.fori_loop(0, nb // SWA_GROUP, q_group, 0)

    smem = pl.BlockSpec(memory_space=pltpu.SMEM)
    slab = pl.BlockSpec((S, LANES), lambda j: (0, j))
    return _pcall(
        body, name="swa_fwd", grid=(npair,), semantics=("arbitrary",),
        in_specs=[smem, smem, slab, slab, slab],
        out_specs=[slab, pl.BlockSpec((2, S, 1), lambda j: (j, 0, 0))],
        out_shape=[_sds((S, npair * LANES), jnp.float32), _sds((2 * npair, S, 1), jnp.float32)],
    )(sinks, slopes, q, kd, vd)


def _swa_bwd(q, kd, vd, do, o, lse, sinks, slopes, plan=None):
    S = q.shape[0]
    npair = q.shape[1] // LANES
    nb = S // WINDOW

    def body(sink_ref, slope_ref, q_ref, k_ref, v_ref, do_ref, o_ref, lse_ref,
             dq_ref, dk_ref, dv_ref, dsink_ref, dk_acc, dv_acc):
        j = pl.program_id(0)
        lo = _lane_masks()
        dk_acc[...] = jnp.zeros_like(dk_acc)
        dv_acc[...] = jnp.zeros_like(dv_acc)
        biases = [(_swa_bias(slope_ref[2 * j + h], 0), _swa_bias(slope_ref[2 * j + h], WINDOW)) for h in (0, 1)]

        def q_block(qi, carry):
            q0 = pl.multiple_of(qi * WINDOW, WINDOW)
            k0 = pl.multiple_of(jnp.maximum(qi - 1, 0) * WINDOW, WINDOW)
            qs = _split_heads(q_ref[pl.ds(q0, WINDOW), :], lo)
            dos = _split_heads(do_ref[pl.ds(q0, WINDOW), :], lo)
            oblk = o_ref[pl.ds(q0, WINDOW), :]
            kblk = k_ref[pl.ds(k0, 2 * WINDOW), :]
            vblk = v_ref[pl.ds(k0, 2 * WINDOW), :]
            ks = _split_heads(kblk, lo)
            dq = None
            out = []
            for h in (0, 1):
                sink = sink_ref[2 * j + h]
                lse_h = lse_ref[h, pl.ds(q0, WINDOW), :]
                s = _swa_scores(qs[h], kblk, jnp.where(qi == 0, *biases[h]))
                p = jnp.exp(s - lse_h)
                delta = jnp.sum(dos[h].astype(jnp.float32) * oblk, axis=1, keepdims=True)
                dv_acc[pl.ds(k0, 2 * WINDOW), :] += _mm_tn(p, dos[h])
                dp = _mm_nt(dos[h], vblk)
                ds = p * (dp - delta)
                dqh = _mm(ds, ks[h]) * (HEAD ** -0.5)
                dq = dqh if dq is None else dq + dqh
                dk_acc[pl.ds(k0, 2 * WINDOW), :] += _mm_tn(ds, qs[h]) * (HEAD ** -0.5)
                dsk = jnp.sum(-jnp.exp(sink - lse_h) * delta, axis=0, keepdims=True)
                out.append(carry[h] + dsk)
            dq_ref[pl.ds(q0, WINDOW), :] = dq.astype(dq_ref.dtype)
            return tuple(out)

        def q_group(gi, carry):
            for g in range(SWA_GROUP):
                carry = q_block(gi * SWA_GROUP + g, carry)
            return carry

        zero = jnp.zeros((1, 1), jnp.float32)
        dsa, dsb = lax.fori_loop(0, nb // SWA_GROUP, q_group, (zero, zero))
        dk_ref[...] = dk_acc[...].astype(dk_ref.dtype)
        dv_ref[...] = dv_acc[...].astype(dv_ref.dtype)
        r = lax.broadcasted_iota(jnp.int32, (8, LANES), 0)
        dsink_ref[0] = jnp.where(r == 0, dsa, jnp.where(r == 1, dsb, 0.0))

    smem = pl.BlockSpec(memory_space=pltpu.SMEM)
    slab = pl.BlockSpec((S, LANES), lambda j: (0, j))
    outs, rode = _pcall_riding(
        body, plan, [sinks, slopes, q, kd, vd, do, o, lse], name="swa_bwd", grid=(npair,),
        in_specs=[smem, smem, slab, slab, slab, slab, slab, pl.BlockSpec((2, S, 1), lambda j: (j, 0, 0))],
        out_specs=[slab, slab, slab, pl.BlockSpec((1, 8, LANES), lambda j: (j, 0, 0))],
        out_shape=[_sds(q.shape, do.dtype), _sds(kd.shape, do.dtype), _sds(vd.shape, do.dtype),
                   _sds((npair, 8, LANES), jnp.float32)],
        scratch_shapes=[pltpu.VMEM((S, LANES), jnp.float32), pltpu.VMEM((S, LANES), jnp.float32)])
    return (*outs, rode)


def _log_steps(S):
    k, out = 1, []
    while k < S:
        out.append(k)
        k *= 2
    return out


def _forget_fwd(f_row, b_col):
    S = f_row.shape[1]

    def body(f_ref, b_ref, lc_ref):
        x = f_ref[...] + b_ref[...]
        lc = jnp.minimum(x, 0.0) - jnp.log(1.0 + jnp.exp(-jnp.abs(x)))
        idx = lax.broadcasted_iota(jnp.int32, lc.shape, 1)
        for k in _log_steps(S):
            lc = lc + jnp.where(idx >= k, pltpu.roll(lc, k, axis=1), 0.0)
        lc_ref[...] = lc

    return _pcall(body, name="forget_fwd", out_shape=_sds(f_row.shape, jnp.float32))(f_row, b_col)


def _forget_bwd(dlc_row, f_row, b_col):
    S = f_row.shape[1]

    def body(d_ref, f_ref, b_ref, df_ref, db_ref):
        g = d_ref[...]
        idx = lax.broadcasted_iota(jnp.int32, g.shape, 1)
        for k in _log_steps(S):
            g = g + jnp.where(idx < S - k, pltpu.roll(g, S - k, axis=1), 0.0)
        x = f_ref[...] + b_ref[...]
        df = g * _sigmoid(-x)
        df_ref[...] = df
        db_ref[...] = jnp.sum(df, axis=1, keepdims=True)

    return _pcall(body, name="forget_bwd",
                  out_shape=[_sds(f_row.shape, jnp.float32), _sds((f_row.shape[0], 1), jnp.float32)])(dlc_row, f_row, b_col)


def _layer0_out_layer1_in(x, o_m, o_s, gate, w_out, g1, w_in1):
    S = x.shape[0]

    def body(x_ref, om_ref, os_ref, gate_ref, wo_ref, g_ref, w_ref,
             x1_ref, h_ref, q_ref, k_ref, v_ref, g1_ref, f_ref):
        gt = gate_ref[...]
        sg = gt * _sigmoid(gt)
        um = om_ref[...] * sg[:, :512]
        us = os_ref[...] * sg[:, 512:]
        x1 = x_ref[...] + _mm(um, wo_ref[0:512, :]) + _mm(us, wo_ref[512:1024, :])
        x1_ref[...] = x1
        h = _rms(x1, g_ref[...])
        h_ref[...] = h.astype(h_ref.dtype)
        z = _mm_nt(h, w_ref[...])
        q_ref[...] = z[:, 0:1024].astype(q_ref.dtype)
        k_ref[...] = z[:, 1024:2048].astype(k_ref.dtype)
        v_ref[...] = z[:, 2048:3072].astype(v_ref.dtype)
        g1_ref[...] = z[:, 3072:4096]
        f_ref[...] = z[:, 4096:4224]

    outs = [((S, D), jnp.float32), ((S, D), MXU), ((S, D), MXU), ((S, D), MXU), ((S, D), MXU),
            ((S, D), jnp.float32), ((S, LANES), jnp.float32)]
    return _pcall(
        body, name="layer0_out_layer1_in", grid=(S // TOK,), semantics=("arbitrary",),
        in_specs=[_rows(TOK, D), _rows(TOK, 512), _rows(TOK, 512), _rows(TOK, D), _full((D, D)), _full((1, D)),
                  _full(w_in1.shape)],
        out_specs=[_rows(TOK, s[1]) for s, _ in outs],
        out_shape=[_sds(s, d) for s, d in outs],
    )(x, o_m, o_s, gate, w_out, g1, w_in1)


def _head(x1, o1, gate1, w_out1, g_f, target):
    S = x1.shape[0]

    def body(x1_ref, o_ref, gate_ref, wo_ref, g_ref, t_ref,
             loss_ref, dgf_ref, dwo_ref, dx2_ref, do_ref, dgate_ref):
        i = pl.program_id(0)
        gt = gate_ref[...]
        sig = _sigmoid(gt)
        sg = gt * sig
        o = o_ref[...]
        u = o * sg
        x2 = x1_ref[...] + _mm(u, wo_ref[...])
        g = g_ref[...]
        y = _rms(x2, g)
        err = y - t_ref[...]
        part = 0.5 * jnp.sum(jnp.mean(err * err, axis=-1, keepdims=True), axis=0, keepdims=True)
        dy = err * (1.0 / D)
        dx2, dg_rows = _rms_bwd(x2, g, dy)
        dx2_ref[...] = dx2
        du = _mm_nt(dx2, wo_ref[...])
        do_ref[...] = (du * sg).astype(do_ref.dtype)
        dgate_ref[...] = (du * o * (sig * (1.0 + gt * (1.0 - sig)))).astype(dgate_ref.dtype)

        @pl.when(i == 0)
        def _():
            loss_ref[...] = jnp.zeros_like(loss_ref)
            dgf_ref[...] = jnp.zeros_like(dgf_ref)
            dwo_ref[...] = jnp.zeros_like(dwo_ref)

        loss_ref[...] += jnp.broadcast_to(part, loss_ref.shape)
        dgf_ref[...] += jnp.sum(dg_rows, axis=0, keepdims=True)
        dwo_ref[...] += _mm_tn(u, dx2)

    outs = [((S, D), jnp.float32), ((S, D), MXU), ((S, D), MXU)]
    return _pcall(
        body, name="head", grid=(S // TOK,), semantics=("arbitrary",),
        in_specs=[_rows(TOK, D), _rows(TOK, D), _rows(TOK, D), _full((D, D)), _full((1, D)), _rows(TOK, D)],
        out_specs=[_full((8, LANES)), _full((1, D)), _full((D, D))] + [_rows(TOK, D) for _ in outs],
        out_shape=[_sds((8, LANES), jnp.float32), _sds((1, D), jnp.float32), _sds((D, D), jnp.float32)]
        + [_sds(s, d) for s, d in outs],
    )(x1, o1, gate1, w_out1, g_f, target)


def _layer1_in_bwd(dq, dk, dv, dgate1, df, x1, dx2, g1, w_in1, gate0, o_m, o_s, w_out0):
    S = x1.shape[0]

    def body(dq_ref, dk_ref, dv_ref, dg1_ref, df_ref, x1_ref, dx2_ref, g_ref, w_ref, gate_ref, om_ref, os_ref,
             wo_ref, dz_ref, dx1_ref, dgn_ref, dwo_ref, dom_ref, dos_ref, dgate_ref):
        i = pl.program_id(0)
        dz_ref[:, 0:1024] = dq_ref[...]
        dz_ref[:, 1024:2048] = dk_ref[...]
        dz_ref[:, 2048:3072] = dv_ref[...]
        dz_ref[:, 3072:4096] = dg1_ref[...]
        dz_ref[:, 4096:4224] = df_ref[...]
        dh = _mm(dz_ref[...], w_ref[...])
        g = g_ref[...]
        dxn, dg_rows = _rms_bwd(x1_ref[...], g, dh)
        dx1 = dx2_ref[...] + dxn
        dx1_ref[...] = dx1
        du = _mm_nt(dx1, wo_ref[...])
        gt = gate_ref[...]
        sig = _sigmoid(gt)
        sg = gt * sig
        dsg = sig * (1.0 + gt * (1.0 - sig))
        dom_ref[...] = (du[:, :512] * sg[:, :512]).astype(dom_ref.dtype)
        dos_ref[...] = (du[:, 512:] * sg[:, 512:]).astype(dos_ref.dtype)
        dgate_ref[:, :512] = (du[:, :512] * om_ref[...] * dsg[:, :512]).astype(dgate_ref.dtype)
        dgate_ref[:, 512:] = (du[:, 512:] * os_ref[...] * dsg[:, 512:]).astype(dgate_ref.dtype)

        @pl.when(i == 0)
        def _():
            dgn_ref[...] = jnp.zeros_like(dgn_ref)
            dwo_ref[...] = jnp.zeros_like(dwo_ref)

        dgn_ref[...] += jnp.sum(dg_rows, axis=0, keepdims=True)
        dwo_ref[0:512, :] += _mm_tn(om_ref[...] * sg[:, :512], dx1)
        dwo_ref[512:1024, :] += _mm_tn(os_ref[...] * sg[:, 512:], dx1)

    return _pcall(
        body, name="layer1_in_bwd", grid=(S // TOK,), semantics=("arbitrary",),
        in_specs=[_rows(TOK, D), _rows(TOK, D), _rows(TOK, D), _rows(TOK, D), _rows(TOK, LANES), _rows(TOK, D),
                  _rows(TOK, D), _full((1, D)), _full(w_in1.shape), _rows(TOK, D), _rows(TOK, 512), _rows(TOK, 512),
                  _full((D, D))],
        out_specs=[_rows(TOK, 4224), _rows(TOK, D), _full((1, D)), _full((D, D)), _rows(TOK, 512), _rows(TOK, 512),
                   _rows(TOK, D)],
        out_shape=[_sds((S, 4224), MXU), _sds((S, D), jnp.float32), _sds((1, D), jnp.float32), _sds((D, D), jnp.float32),
                   _sds((S, 512), MXU), _sds((S, 512), MXU), _sds((S, D), MXU)],
    )(dq, dk, dv, dgate1, df, x1, dx2, g1, w_in1, gate0, o_m, o_s, w_out0)


def _layer0_in_bwd(dqm, dkm, dvm, dqs, dkd, dvd, dgate0, cos, sin, cq, ckv, x, dx1, g_in, w_in, g_q, w_q, g_kv, w_kv):
    S = x.shape[0]
    consts = _rope_consts()

    def body(dqm_ref, dkm_ref, dvm_ref, dqs_ref, dkd_ref, dvd_ref, dgate_ref, cos_ref, sin_ref, c_ref, cq_ref, ckv_ref,
             x_ref, dx1_ref, g_ref, w_ref, gq_ref, wq_ref, gkv_ref, wkv_ref,
             dx_ref, dz_ref, dgin_ref, dgq_ref, dgkv_ref, dwq_ref, dwkv_ref, dqu_ref, dkvu_ref):
        i = pl.program_id(0)
        lo = _lane_masks()
        sign = c_ref[...][1:2, :]
        c = cos_ref[...]
        s = sin_ref[...]
        dkpe = None
        for hd in range(N_MLA):
            sl = slice(LANES * hd, LANES * (hd + 1))
            dqu_ref[:, sl] = _rope_t(dqm_ref[:, sl], c, s, sign).astype(dqu_ref.dtype)
            dkh = dkm_ref[:, sl]
            dkvu_ref[:, sl] = jnp.where(lo, dkh, 0.0).astype(dkvu_ref.dtype)
            dkpe = dkh if dkpe is None else dkpe + dkh
        dkvu_ref[:, 1024:1536] = dvm_ref[...]
        dkpe = _rope_t(jnp.where(lo, 0.0, dkpe), c, s, sign)
        dcqn = _mm(dqu_ref[...], wq_ref[...])
        dckvn = _mm_nt(dkvu_ref[...], wkv_ref[...])
        gq = gq_ref[...]
        gkv = gkv_ref[...]
        dcq, dgq_rows = _rms_bwd(cq_ref[...], gq, dcqn)
        dckv, dgkv_rows = _rms_bwd(ckv_ref[...], gkv, dckvn)
        dz_ref[:, 0:256] = dcq.astype(dz_ref.dtype)
        dz_ref[:, 256:384] = dckv.astype(dz_ref.dtype)
        dz_ref[:, 384:512] = dkpe.astype(dz_ref.dtype)
        dz_ref[:, 512:1024] = dqs_ref[...]
        dz_ref[:, 1024:1536] = dkd_ref[...]
        dz_ref[:, 1536:2048] = dvd_ref[...]
        dz_ref[:, 2048:3072] = dgate_ref[...]
        dh = _mm(dz_ref[...], w_ref[...])
        g = g_ref[...]
        dxn, dg_rows = _rms_bwd(x_ref[...], g, dh)
        dx_ref[...] = dx1_ref[...] + dxn

        @pl.when(i == 0)
        def _():
            dgin_ref[...] = jnp.zeros_like(dgin_ref)
            dgq_ref[...] = jnp.zeros_like(dgq_ref)
            dgkv_ref[...] = jnp.zeros_like(dgkv_ref)
            dwq_ref[...] = jnp.zeros_like(dwq_ref)
            dwkv_ref[...] = jnp.zeros_like(dwkv_ref)

        dgin_ref[...] += jnp.sum(dg_rows, axis=0, keepdims=True)
        dgq_ref[...] += jnp.sum(dgq_rows, axis=0, keepdims=True)
        dgkv_ref[...] += jnp.sum(dgkv_rows, axis=0, keepdims=True)
        dwq_ref[...] += _mm_tn(dqu_ref[...], _rms(cq_ref[...], gq))
        dwkv_ref[...] += _mm_tn(_rms(ckv_ref[...], gkv), dkvu_ref[...])

    return _pcall(
        body, name="layer0_in_bwd", grid=(S // TOK,), semantics=("arbitrary",),
        in_specs=[_rows(TOK, 1024), _rows(TOK, 1024), _rows(TOK, 512), _rows(TOK, 512), _rows(TOK, 512), _rows(TOK, 512),
                  _rows(TOK, D), _rows(TOK, LANES), _rows(TOK, LANES), _full((8, LANES)), _rows(TOK, 256), _rows(TOK, 128),
                  _rows(TOK, D), _rows(TOK, D), _full((1, D)), _full(w_in.shape), _full((1, 256)), _full(w_q.shape),
                  _full((1, 128)), _full(w_kv.shape)],
        out_specs=[_rows(TOK, D), _rows(TOK, 3072), _full((1, D)), _full((1, 256)), _full((1, 128)), _full(w_q.shape),
                   _full(w_kv.shape)],
        out_shape=[_sds((S, D), jnp.float32), _sds((S, 3072), MXU), _sds((1, D), jnp.float32), _sds((1, 256), jnp.float32),
                   _sds((1, 128), jnp.float32), _sds(w_q.shape, jnp.float32), _sds(w_kv.shape, jnp.float32)],
        scratch_shapes=[pltpu.VMEM((TOK, 1024), MXU), pltpu.VMEM((TOK, 1536), MXU)],
    )(dqm, dkm, dvm, dqs, dkd, dvd, dgate0, cos, sin, consts, cq, ckv, x, dx1, g_in, w_in, g_q, w_q, g_kv, w_kv)


def _wgrad(a, b, name):
    S, M = a.shape
    N = b.shape[1]
    tm = next(t for t in range(WG_ROWS, 0, -LANES) if M % t == 0)
    tn = N if N <= 1024 else 512
    tk = min(WG_TOK, S)

    def body(a_ref, b_ref, o_ref):
        @pl.when(pl.program_id(2) == 0)
        def _():
            o_ref[...] = jnp.zeros_like(o_ref)

        o_ref[...] += _mm_tn(a_ref[...], b_ref[...])

    return _pcall(
        body, name=name, grid=(M // tm, N // tn, S // tk), semantics=("parallel", "parallel", "arbitrary"),
        in_specs=[pl.BlockSpec((tk, tm), lambda m, n, k: (k, m)), pl.BlockSpec((tk, tn), lambda m, n, k: (k, n))],
        out_specs=pl.BlockSpec((tm, tn), lambda m, n, k: (m, n)),
        out_shape=_sds((M, N), jnp.float32),
    )(a, b)


def _adamw(w, g, m, v, name):
    shape = w.shape
    R, C = (int(np.prod(shape[:-1])), shape[-1])
    w2, g2, m2, v2 = (t.reshape(R, C) for t in (w, g, m, v))
    fits = [t for t in range(8, ADAM_TILE_BYTES // (4 * C) + 1, 8) if R % t == 0]
    tr = max(fits) if fits else R
    tc = C if (tr * C * 4 <= ADAM_TILE_BYTES or C % 256) else 256

    def body(w_ref, g_ref, m_ref, v_ref, d_ref, nm_ref, nv_ref):
        gg = g_ref[...]
        nm = B1 * m_ref[...] + (1.0 - B1) * gg
        nv = B2 * v_ref[...] + (1.0 - B2) * (gg * gg)
        m_hat = nm / (1.0 - B1 ** STEP)
        v_hat = nv / (1.0 - B2 ** STEP)
        d_ref[...] = -LR * (m_hat / (jnp.sqrt(v_hat) + AEPS) + WD * w_ref[...])
        nm_ref[...] = nm
        nv_ref[...] = nv

    spec = pl.BlockSpec((tr, tc), lambda i, j: (i, j))
    d, nm, nv = _pcall(
        body, name=name, grid=(R // tr, C // tc), semantics=("parallel", "parallel"),
        in_specs=[spec] * 4, out_specs=[spec] * 3, out_shape=[_sds((R, C), jnp.float32)] * 3,
    )(w2, g2, m2, v2)
    return d.reshape(shape), nm.reshape(shape), nv.reshape(shape)


def _sum_leading(a, name):
    n, R, C = a.shape
    tr = SUM_ROWS if R % SUM_ROWS == 0 else R

    def body(a_ref, o_ref):
        acc = a_ref[0]
        for i in range(1, n):
            acc = acc + a_ref[i]
        o_ref[...] = acc

    return _pcall(
        body, name=name, grid=(R // tr,), semantics=("parallel",),
        in_specs=[pl.BlockSpec((n, tr, C), lambda i: (0, i, 0))], out_specs=_rows(tr, C),
        out_shape=_sds((R, C), a.dtype),
    )(a)


def _add_halves(g, c, b, name, out_dtype):
    n, _, R, C = g.shape
    tr = SUM_ROWS if R % SUM_ROWS == 0 else R

    def body(c_ref, a_ref, b_ref, o_ref):
        o_ref[...] = (a_ref[0] + b_ref[...]).astype(o_ref.dtype)

    spec = pl.BlockSpec((1, tr, C), lambda k, i, c_ref: (k, i, 0))
    grid_spec = pltpu.PrefetchScalarGridSpec(
        num_scalar_prefetch=1, grid=(n, R // tr),
        in_specs=[pl.BlockSpec((1, 1, tr, C), lambda k, i, c_ref: (k, c_ref[0], i, 0)), spec], out_specs=spec)
    return _pcall(body, name=name, semantics=("parallel", "parallel"), grid_spec=grid_spec,
                  out_shape=_sds(b.shape, out_dtype))(c.reshape(1).astype(jnp.int32), g, b)


def _total_sum(mine, theirs, recv, name):
    R, C = mine.shape
    n = recv.shape[0]
    tr = SUM_ROWS if R % SUM_ROWS == 0 else R

    def body(a_ref, b_ref, r_ref, o_ref):
        acc = a_ref[...] + b_ref[...]
        for i in range(n):
            acc = acc + r_ref[i].astype(jnp.float32)
        o_ref[...] = acc

    return _pcall(
        body, name=name, grid=(R // tr,), semantics=("parallel",),
        in_specs=[_rows(tr, C), _rows(tr, C), pl.BlockSpec((n, tr, C), lambda i: (0, i, 0))], out_specs=_rows(tr, C),
        out_shape=_sds((R, C), jnp.float32),
    )(mine, theirs, recv)


def _place():
    return lax.axis_index("x"), lax.axis_index("y"), lax.axis_index("c")


class _Plan:
    def __init__(self, arrays, out_shape, scratch, start, finish, middle=None):
        self.arrays, self.out_shape, self.scratch = list(arrays), list(out_shape), list(scratch)
        self.start, self.finish, self.middle = start, finish, middle


def _gather8_plan(block):
    R, C = block.shape

    def parts(ins, outs, sems):
        (x_ref,), (out_ref,), (send_sems, recv_sems) = ins, outs, sems
        x, y, c = _place()
        me, sibling = (x, y, c), (x, y, 1 - c)
        chips = [(1 - x, y), (x, 1 - y), (1 - x, 1 - y)]

        def copy(k, blk, to, src=None):
            slot = out_ref.at[4 * blk[0] + 2 * blk[1] + blk[2]]
            return pltpu.make_async_remote_copy(
                src_ref=slot if src is None else src, dst_ref=slot,
                send_sem=send_sems.at[k], recv_sem=recv_sems.at[k], device_id=to, device_id_type=MESH_ID)

        def first():
            return [copy(0, me, sibling, src=x_ref)] + [copy(1 + j, me, (*chip, c), src=x_ref) for j, chip in enumerate(chips)]

        def passed():
            return [copy(4 + j, (*chip, c), sibling) for j, chip in enumerate(chips)]

        def arrivals():
            return [copy(1 + j, (*chip, c), me) for j, chip in enumerate(chips)]

        def late():
            return [copy(0, sibling, me)] + [copy(4 + j, (*chip, 1 - c), me) for j, chip in enumerate(chips)]

        return first, passed, arrivals, late

    def start(ins, outs, sems):
        for cp in parts(ins, outs, sems)[0]():
            cp.start()

    def middle(ins, outs, sems):
        _, passed, arrivals, _ = parts(ins, outs, sems)
        for arrived, forward in zip(arrivals(), passed()):
            arrived.wait_recv()
            forward.start()

    def finish(ins, outs, sems):
        first, passed, _, late = parts(ins, outs, sems)
        for cp in late():
            cp.wait_recv()
        for cp in first() + passed():
            cp.wait_send()

    return _Plan([block], [_sds((8, R, C), block.dtype)], [pltpu.SemaphoreType.DMA((7,)), pltpu.SemaphoreType.DMA((7,))],
                 start, finish, middle)


def _fill_own_slot(gathered, block):
    x, y, c = _place()
    return lax.dynamic_update_index_in_dim(gathered, block, 4 * x + 2 * y + c, 0)


def _started_and_waited(arrays, out_shape, n, copies):
    def start(ins, outs, sems):
        for cp in copies(ins, outs, sems):
            cp.start()

    def finish(ins, outs, sems):
        for cp in copies(ins, outs, sems):
            cp.wait()

    return _Plan(arrays, out_shape, [pltpu.SemaphoreType.DMA((n,)), pltpu.SemaphoreType.DMA((n,))], start, finish)


def _pair_swap_plan(g):
    n = g.shape[0]

    def copies(ins, outs, sems):
        (g_ref,), (out_ref,), (send_sems, recv_sems) = ins, outs, sems
        x, y, c = _place()
        return [pltpu.make_async_remote_copy(src_ref=g_ref.at[k, 1 - c], dst_ref=out_ref.at[k], send_sem=send_sems.at[k],
                                             recv_sem=recv_sems.at[k], device_id=(x, y, 1 - c), device_id_type=MESH_ID)
                for k in range(n)]

    return _started_and_waited([g], [_sds((n,) + g.shape[2:], g.dtype)], n, copies)


def _chip_exchange_plan(p):
    def copies(ins, outs, sems):
        (p_ref,), (out_ref,), (send_sems, recv_sems) = ins, outs, sems
        x, y, c = _place()
        chips = [(1 - x, y), (x, 1 - y), (1 - x, 1 - y)]
        return [pltpu.make_async_remote_copy(
            src_ref=p_ref.at[2 * cx + cy], dst_ref=out_ref.at[j], send_sem=send_sems.at[j],
            recv_sem=recv_sems.at[j], device_id=(cx, cy, c), device_id_type=MESH_ID)
            for j, (cx, cy) in enumerate(chips)]

    return _started_and_waited([p], [_sds((3,) + p.shape[1:], p.dtype)], 3, copies)


def _pair_exchange_plan(t):
    def copies(ins, outs, sems):
        (t_ref,), (out_ref,), (send_sems, recv_sems) = ins, outs, sems
        x, y, c = _place()
        return [pltpu.make_async_remote_copy(src_ref=t_ref, dst_ref=out_ref, send_sem=send_sems.at[0], recv_sem=recv_sems.at[0],
                                             device_id=(x, y, 1 - c), device_id_type=MESH_ID)]

    return _started_and_waited([t], [_sds(t.shape, t.dtype)], 1, copies)


ANY_SPEC = pl.BlockSpec(memory_space=pl.ANY)


def _run_plan(plan, name):
    n_in, n_out = len(plan.arrays), len(plan.out_shape)

    def body(*refs):
        ins, outs, sems = refs[:n_in], refs[n_in:n_in + n_out], refs[n_in + n_out:]
        plan.start(ins, outs, sems)
        if plan.middle is not None:
            plan.middle(ins, outs, sems)
        plan.finish(ins, outs, sems)

    return _pcall(body, name=name, in_specs=[ANY_SPEC] * n_in, out_specs=[ANY_SPEC] * n_out, out_shape=plan.out_shape,
                  scratch_shapes=plan.scratch)(*plan.arrays)


def _pcall_riding(body, plan, args, *, name, grid, in_specs, out_specs, out_shape, scratch_shapes):
    if plan is None:
        outs = _pcall(body, name=name, grid=grid, semantics=("arbitrary",), in_specs=in_specs, out_specs=out_specs,
                      out_shape=out_shape, scratch_shapes=scratch_shapes)(*args)
        return list(outs), None
    n_in, n_out, n_s = len(args), len(out_shape), len(scratch_shapes)
    p_in, p_out = len(plan.arrays), len(plan.out_shape)
    steps = grid[0]

    def riding(*refs):
        ins, pins = refs[:n_in], refs[n_in:n_in + p_in]
        o0 = n_in + p_in
        outs, pouts = refs[o0:o0 + n_out], refs[o0 + n_out:o0 + n_out + p_out]
        s0 = o0 + n_out + p_out
        scr, sems = refs[s0:s0 + n_s], refs[s0 + n_s:]
        j = pl.program_id(0)

        @pl.when(j == 0)
        def _():
            plan.start(pins, pouts, sems)

        if plan.middle is not None:
            @pl.when(j == steps // 2)
            def _():
                plan.middle(pins, pouts, sems)

        body(*ins, *outs, *scr)

        @pl.when(j == steps - 1)
        def _():
            plan.finish(pins, pouts, sems)

    res = _pcall(riding, name=name, grid=grid, semantics=("arbitrary",), in_specs=list(in_specs) + [ANY_SPEC] * p_in,
                 out_specs=list(out_specs) + [ANY_SPEC] * p_out, out_shape=list(out_shape) + plan.out_shape,
                 scratch_shapes=list(scratch_shapes) + plan.scratch)(*args, *plan.arrays)
    return list(res[:n_out]), list(res[n_out:])


class _RowSeq:
    def __init__(self, pieces):
        self.pieces = list(pieces)

    def rows(self, a, b):
        out, off = [], 0
        for p in self.pieces:
            lo, hi = max(a, off), min(b, off + p.shape[0])
            if lo < hi:
                out.append(p[lo - off:hi - off])
            off += p.shape[0]
        return out

    def array(self):
        return jnp.concatenate(self.pieces, axis=0)


def _row_seq(w):
    return w if isinstance(w, _RowSeq) else _RowSeq([w])


def _prep_w_in0(wt):
    wt = _row_seq(wt)
    one = wt.pieces[0]
    z32 = [jnp.zeros((32, one.shape[1]), one.dtype)]
    k0, k1 = wt.rows(928, 992), wt.rows(992, 1056)
    v0, v1 = wt.rows(1056, 1120), wt.rows(1120, 1184)
    return jnp.concatenate(wt.rows(0, 384) + z32 + z32 + wt.rows(384, 416) + z32 + wt.rows(416, 928)
                           + k0 * 4 + k1 * 4 + v0 * 4 + v1 * 4 + wt.rows(1184, 2208), axis=0)


def _fold_w_in0(d):
    def fold(blk):
        b = blk.reshape(8, 64, blk.shape[1])
        return jnp.concatenate([b[0] + b[1] + b[2] + b[3], b[4] + b[5] + b[6] + b[7]], axis=0)
    return _RowSeq([d[0:384], d[448:480], d[512:1024], fold(d[1024:1536]), fold(d[1536:2048]), d[2048:3072]])


def _prep_w_q(wt):
    return jnp.pad(wt.reshape(N_MLA, 96, Q_RANK), ((0, 0), (0, 32), (0, 0))).reshape(1024, Q_RANK)


def _fold_w_q(d):
    return d.reshape(N_MLA, 128, Q_RANK)[:, :96].reshape(768, Q_RANK)


def _prep_w_kv(w):
    w3 = w.reshape(KV_RANK, N_MLA, 128)
    kk = jnp.pad(w3[:, :, :64], ((0, 0), (0, 0), (0, 64))).reshape(KV_RANK, 1024)
    return jnp.concatenate([kk, w3[:, :, 64:].reshape(KV_RANK, 512)], axis=1)


def _fold_w_kv(d):
    kk = d[:, :1024].reshape(KV_RANK, N_MLA, 128)[:, :, :64]
    vv = d[:, 1024:].reshape(KV_RANK, N_MLA, 64)
    return jnp.concatenate([kk, vv], axis=2).reshape(KV_RANK, 1024)


def _prep_w_in1(wt):
    wt = _row_seq(wt)
    one = wt.pieces[0]
    return jnp.concatenate(wt.rows(0, 3072) + wt.rows(3088, 4112) + wt.rows(3072, 3088)
                           + [jnp.zeros((112, one.shape[1]), one.dtype)], axis=0)


def _fold_w_in1(d):
    return _RowSeq([d[0:3072], d[4096:4112], d[3072:4096]])


class _Alone:
    def __init__(self, w_out0, o_g_in, w_in1, w_out1):
        self.layer1 = (w_out0, o_g_in, w_in1, w_out1)

    def gather_plan(self):
        return None

    def layer1_weights(self, rode):
        return self.layer1

    def swap_plan(self, grads1):
        return None

    def exchange_plan(self, rode):
        return None

    def finish(self, rode):
        pass


def _local_step(x, pos, target, e_g_in, w_in0, e_g_q, w_q, e_g_kv, w_kv, sinks, b_f, g_final, layer1):
    S = x.shape[0]
    w_in0p, w_qp, w_kvp = _prep_w_in0(w_in0), _prep_w_q(w_q), _prep_w_kv(w_kv)
    slopes = jnp.asarray(2.0 ** (-8.0 * (np.arange(N_SWA, dtype=np.float32) + 1.0) / N_SWA), jnp.float32)
    sinks1 = sinks.reshape(N_SWA)
    b_col = b_f.reshape(N_FOX, 1)

    (h0, cq, ckv, qm, km, vm, qs, kd, vd, gate0, cos, sin) = _layer0_in(
        x, pos, e_g_in, w_in0p, e_g_q, w_qp, e_g_kv, w_kvp)
    o_m, lse_m, rode = _attn_fwd_t(qm, km, vm, (NOPE + ROPE) ** -0.5, split=True, name="mla_fwd", plan=layer1.gather_plan())
    w_out0, o_g_in, w_in1, w_out1 = layer1.layer1_weights(rode)
    w_in1p = _prep_w_in1(w_in1)
    o_s, lse_s = _swa_fwd(qs, kd, vd, sinks1, slopes)
    x1, h1, q1, k1, v1, gate1, f_slab = _layer0_out_layer1_in(x, o_m, o_s, gate0, w_out0, o_g_in, w_in1p)
    f_row = f_slab[:, :N_FOX].T
    lc_row = _forget_fwd(f_row, b_col)
    lcc = lc_row.reshape(N_FOX, S, 1)
    o1, lse1, _ = _attn_fwd_t(q1, k1, v1, HEAD ** -0.5, split=False, name="fox_fwd", lcc=lcc)
    loss8, dg_final, dw_out1, dx2, do1, dgate1 = _head(x1, o1, gate1, w_out1, g_final, target)

    dq1, dk1, dv1, dlc, _ = _attn_bwd_t(q1, k1, v1, do1, o1, lse1, HEAD ** -0.5, split=False, name="fox_bwd", lcc=lcc)
    df_row, db_f = _forget_bwd(dlc.reshape(N_FOX, S), f_row, b_col)
    df_slab = jnp.pad(df_row.T, ((0, 0), (0, LANES - N_FOX))).astype(MXU)
    dz1, dx1, dg_o_in, dw_out0, do_m, do_s, dgate0 = _layer1_in_bwd(
        dq1, dk1, dv1, dgate1, df_slab, x1, dx2, o_g_in, w_in1p, gate0, o_m, o_s, w_out0)
    grads1 = dict(o_g_in=dg_o_in, o_w_in=_fold_w_in1(_wgrad(dz1, h1, "wgrad_in1")), o_w_out=dw_out1, e_w_out=dw_out0)
    dqs, dkd, dvd, dsink, rode = _swa_bwd(qs, kd, vd, do_s, o_s, lse_s, sinks1, slopes, plan=layer1.swap_plan(grads1))
    dqm, dkm, dvm, rode = _attn_bwd_t(qm, km, vm, do_m, o_m, lse_m, (NOPE + ROPE) ** -0.5, split=True, name="mla_bwd",
                                      plan=layer1.exchange_plan(rode))
    layer1.finish(rode)
    dx, dz0, dg_in, dg_q, dg_kv, dw_q, dw_kv = _layer0_in_bwd(
        dqm, dkm, dvm, dqs, dkd, dvd, dgate0, cos, sin, cq, ckv, x, dx1, e_g_in, w_in0p, e_g_q, w_qp, e_g_kv, w_kvp)

    grads = dict(
        e_g_in=dg_in,
        e_w_in=_fold_w_in0(_wgrad(dz0, h0, "wgrad_in0")),
        e_g_q_a=dg_q,
        e_w_q_up=_fold_w_q(dw_q),
        e_g_kv_a=dg_kv,
        e_w_kv_up=_fold_w_kv(dw_kv),
        e_sinks=dsink[:, 0:2, 0].reshape(1, N_SWA),
        o_b_f=db_f.reshape(1, N_FOX),
        g_final=dg_final,
        **grads1,
    )
    return loss8[0, 0], dx, grads


SHARDED = ("e_w_in", "e_w_q_up", "e_w_kv_up", "e_w_out", "o_g_in", "o_w_in", "o_w_out")
TRANSPOSED = ("e_w_in", "e_w_q_up", "o_w_in")
COL_SHARDED = ("e_w_kv_up", "o_g_in")
REPLICATED = ("e_g_in", "e_g_q_a", "e_g_kv_a", "e_sinks", "o_b_f", "g_final")
FULL_SHAPES = dict(e_w_in=(2208, 1024), e_w_q_up=(768, 256), e_w_kv_up=(128, 1024), e_w_out=(1024, 1024),
                   o_g_in=(1, 1024), o_w_in=(4112, 1024), o_w_out=(1024, 1024))
GROUPS = dict(
    layer0=dict(rows=768, windows=dict(e_w_in=(0, 0), e_w_q_up=(560, 0), e_w_kv_up=(560, 256))),
    layer1=dict(rows=1568, windows=dict(o_w_in=(0, 0), o_w_out=(1040, 0), e_w_out=(1296, 0), o_g_in=(1552, 0))),
)


def _shard_shape(name):
    r, c = FULL_SHAPES[name]
    return (r, c // 4) if name in COL_SHARDED else (r // 4, c)


def _as_handled(name, a):
    a = a[0] if a.ndim == 3 else a
    return a.T if name in TRANSPOSED else a


def _as_given(name, a, shape):
    return (a.T if name in TRANSPOSED else a).reshape(shape)


def _pack_block(p, group):
    def rows(a, n):
        return jnp.pad(a, ((0, n - a.shape[0]), (0, 0)))

    if group == "layer0":
        band = jnp.concatenate([p["e_w_q_up"], rows(p["e_w_kv_up"], 192), jnp.zeros((192, 512), p["e_w_in"].dtype)], axis=1)
        return jnp.concatenate([rows(p["e_w_in"], 560), rows(band, 208)], axis=0)
    g = p["o_g_in"]
    band = jnp.pad(g, ((0, 16 - g.shape[0]), (0, PACK_COLS - g.shape[1])))
    return jnp.concatenate([rows(p["o_w_in"], 1040), p["o_w_out"], p["e_w_out"], band], axis=0)


def _window(block, group, name, width=None):
    r0, c0 = GROUPS[group]["windows"][name]
    r, c = _shard_shape(name)
    return block[..., r0:r0 + r, c0:c0 + (c if width is None else width)]


def _chip_slice(name, full, k):
    r, c = _shard_shape(name)
    if isinstance(full, _RowSeq):
        return jnp.concatenate(full.rows(r * k, r * (k + 1)), axis=0)
    return full[:, c * k:c * (k + 1)] if name in COL_SHARDED else full[r * k:r * (k + 1), :]


def _packed_weights(w, group):
    parts = {}
    for n in GROUPS[group]["windows"]:
        a = _as_handled(n, w[n])
        parts[n] = lax.bitcast_convert_type(a, jnp.bfloat16).reshape(1, -1) if n == "o_g_in" else a.astype(jnp.bfloat16)
    halves = _pack_block(parts, group).reshape(2, GROUPS[group]["rows"] // 2, PACK_COLS)
    return lax.dynamic_index_in_dim(halves, lax.axis_index("c"), 0, keepdims=False)


def _unpacked_weights(gathered, half, group):
    blocks = _fill_own_slot(gathered, half).reshape(4, GROUPS[group]["rows"], PACK_COLS)
    full = {}
    for n in GROUPS[group]["windows"]:
        if n == "o_g_in":
            halves = _window(blocks, group, n, width=512).reshape(4, 1, 256, 2)
            full[n] = jnp.concatenate(list(lax.bitcast_convert_type(halves, jnp.float32)), axis=1)
        else:
            pieces = [_window(blocks[k], group, n).astype(MXU) for k in range(4)]
            if n in ("e_w_in", "o_w_in"):
                full[n] = _RowSeq(pieces)
            else:
                full[n] = jnp.concatenate(pieces, axis=1 if n in COL_SHARDED else 0)
    return full


class _GroupReduce:
    def __init__(self, group):
        self.group = group
        self.c = lax.axis_index("c")
        self.chip = 2 * lax.axis_index("x") + lax.axis_index("y")

    def swap_plan(self, grads):
        names = GROUPS[self.group]["windows"]
        per_chip = jnp.stack([_pack_block({n: _chip_slice(n, grads[n], k) for n in names}, self.group) for k in range(4)])
        self.g4 = per_chip.reshape(4, 2, GROUPS[self.group]["rows"] // 2, PACK_COLS)
        return _pair_swap_plan(self.g4)

    def exchange_plan(self, rode):
        theirs = rode[0]
        rows = self.g4.shape[2]
        self.own = (lax.dynamic_slice(self.g4, (self.chip, self.c, 0, 0), (1, 1, rows, PACK_COLS)).reshape(rows, PACK_COLS),
                    lax.dynamic_index_in_dim(theirs, self.chip, 0, keepdims=False))
        return _chip_exchange_plan(_add_halves(self.g4, self.c, theirs, "pair_add_" + self.group, jnp.bfloat16))

    def finish(self, rode):
        my_half = _total_sum(*self.own, rode[0], "chip_sum_" + self.group)
        other_half = _run_plan(_pair_exchange_plan(my_half), "pair_exchange_" + self.group)[0]
        total = jnp.concatenate([jnp.where(self.c == 0, my_half, other_half), jnp.where(self.c == 0, other_half, my_half)], axis=0)
        self.sums = {n: _window(total, self.group, n) for n in GROUPS[self.group]["windows"]}

    def run(self, grads):
        rode = _run_plan(self.swap_plan(grads), "pair_swap_" + self.group)
        self.finish(_run_plan(self.exchange_plan(rode), "chip_exchange_" + self.group))
        return self.sums


class _Layer1Exchange(_GroupReduce):
    def __init__(self, w):
        super().__init__("layer1")
        self.half = _packed_weights(w, "layer1")

    def gather_plan(self):
        return _gather8_plan(self.half)

    def layer1_weights(self, rode):
        full = _unpacked_weights(rode[0], self.half, "layer1")
        return full["e_w_out"], full["o_g_in"], full["o_w_in"], full["o_w_out"]


def kernel(x, positions, e_g_in, e_w_in, e_g_q_a, e_w_q_up, e_g_kv_a, e_w_kv_up, e_sinks, e_w_out, o_g_in, o_w_in, o_b_f, o_w_out, g_final, loss_target, m_e_g_in, m_e_w_in, m_e_g_q_a, m_e_w_q_up, m_e_g_kv_a, m_e_w_kv_up, m_e_sinks, m_e_w_out, m_o_g_in, m_o_w_in, m_o_b_f, m_o_w_out, m_g_final, v_e_g_in, v_e_w_in, v_e_g_q_a, v_e_w_q_up, v_e_g_kv_a, v_e_w_kv_up, v_e_sinks, v_e_w_out, v_o_g_in, v_o_w_in, v_o_b_f, v_o_w_out, v_g_final):
    w = dict(e_g_in=e_g_in, e_w_in=e_w_in, e_g_q_a=e_g_q_a, e_w_q_up=e_w_q_up, e_g_kv_a=e_g_kv_a, e_w_kv_up=e_w_kv_up,
             e_sinks=e_sinks, e_w_out=e_w_out, o_g_in=o_g_in, o_w_in=o_w_in, o_b_f=o_b_f, o_w_out=o_w_out, g_final=g_final)
    m = dict(e_g_in=m_e_g_in, e_w_in=m_e_w_in, e_g_q_a=m_e_g_q_a, e_w_q_up=m_e_w_q_up, e_g_kv_a=m_e_g_kv_a,
             e_w_kv_up=m_e_w_kv_up, e_sinks=m_e_sinks, e_w_out=m_e_w_out, o_g_in=m_o_g_in, o_w_in=m_o_w_in, o_b_f=m_o_b_f,
             o_w_out=m_o_w_out, g_final=m_g_final)
    v = dict(e_g_in=v_e_g_in, e_w_in=v_e_w_in, e_g_q_a=v_e_g_q_a, e_w_q_up=v_e_w_q_up, e_g_kv_a=v_e_g_kv_a,
             e_w_kv_up=v_e_w_kv_up, e_sinks=v_e_sinks, e_w_out=v_e_w_out, o_g_in=v_o_g_in, o_w_in=v_o_w_in, o_b_f=v_o_b_f,
             o_w_out=v_o_w_out, g_final=v_g_final)
    order = ("e_g_in", "e_w_in", "e_g_q_a", "e_w_q_up", "e_g_kv_a", "e_w_kv_up", "e_sinks", "e_w_out", "o_g_in", "o_w_in",
             "o_b_f", "o_w_out", "g_final")
    half0 = _packed_weights(w, "layer0")
    full = _unpacked_weights(_run_plan(_gather8_plan(half0), "gather_weights_layer0")[0], half0, "layer0")
    layer1 = _Layer1Exchange(w)

    loss_part, dx, grads = _local_step(
        x[0], positions.reshape(-1, 1), loss_target[0], e_g_in, full["e_w_in"], e_g_q_a, full["e_w_q_up"], e_g_kv_a,
        full["e_w_kv_up"], e_sinks, o_b_f, g_final.reshape(1, D), layer1)

    gsum = {**layer1.sums, **_GroupReduce("layer0").run(grads)}

    small = jnp.concatenate([jnp.pad(loss_part.reshape(1), (0, LANES - 1))]
                            + [jnp.pad(grads[n].reshape(-1), (0, (-grads[n].size) % LANES)) for n in REPLICATED])
    rows = small.shape[0] // LANES
    small = jnp.pad(small.reshape(rows, LANES), ((0, (-rows) % 8), (0, 0)))
    gathered_small = _fill_own_slot(_run_plan(_gather8_plan(small), "gather_small_grads")[0], small)
    ssum = _sum_leading(gathered_small, "small_grad_sum").reshape(-1)
    loss = ssum[0]
    off = LANES
    for n in REPLICATED:
        cnt = w[n].size
        gsum[n] = ssum[off:off + cnt].reshape(w[n].shape)
        off += cnt + (-cnt) % LANES

    grad, delta, new_m, new_v = {}, {}, {}, {}
    for n in order:
        if n in SHARDED:
            outs = _adamw(_as_handled(n, w[n]), gsum[n], _as_handled(n, m[n]), _as_handled(n, v[n]), "adamw_" + n)
            grad[n], delta[n], new_m[n], new_v[n] = (_as_given(n, a, w[n].shape) for a in (gsum[n],) + outs)
        else:
            grad[n] = gsum[n]
            delta[n], new_m[n], new_v[n] = _adamw(w[n], gsum[n], m[n], v[n], "adamw_" + n)
    return (loss, dx[None], *[grad[n] for n in order], *[delta[n] for n in order], *[new_m[n] for n in order],
            *[new_v[n] for n in order])
```

```python
import numpy as np
import jax
import jax.numpy as jnp
from jax import lax
from jax.experimental import pallas as pl
from jax.experimental.pallas import tpu as pltpu

D = 1024
EPS = 1e-6
ROPE_THETA = 10000.0
N_MLA = 8
Q_RANK = 256
KV_RANK = 128
NOPE = 64
ROPE = 32
N_SWA = 8
WINDOW = 128
N_FOX = 16
HEAD = 64
LR, B1, B2, AEPS, WD, STEP = 0.001, 0.9, 0.999, 1e-08, 0.01, 10

LANES = 128
HALF = 64
VMEM_LIMIT = 56 * 1024 * 1024
MXU = jnp.bfloat16
TOK = 256
WG_TOK = 2048
WG_ROWS = 1536
ATT = 256
FWD_CHUNK = 2
BWD_CHUNK = 2
SWA_GROUP = 4
NEG = float("-inf")

PACK_COLS = 1024
SUM_ROWS = 256
ADAM_TILE_BYTES = 2 << 20
MESH_ID = pl.DeviceIdType.MESH


def _pcall(body, *, name, vmem=VMEM_LIMIT, semantics=None, **kw):
    params = dict(vmem_limit_bytes=vmem)
    if semantics is not None:
        params["dimension_semantics"] = semantics
    return pl.pallas_call(body, name=name, compiler_params=pltpu.CompilerParams(**params), **kw)


def _mm(a, b):
    return jnp.dot(a.astype(MXU), b.astype(MXU), preferred_element_type=jnp.float32)


def _mm_nt(a, b):
    return lax.dot_general(a.astype(MXU), b.astype(MXU), (((1,), (1,)), ((), ())),
                           preferred_element_type=jnp.float32)


def _mm_tn(a, b):
    return lax.dot_general(a.astype(MXU), b.astype(MXU), (((0,), (0,)), ((), ())),
                           preferred_element_type=jnp.float32)


def _full(shape):
    n = len(shape)
    return pl.BlockSpec(shape, lambda *_: (0,) * n)


def _rows(tm, n):
    return pl.BlockSpec((tm, n), lambda i: (i, 0))


def _sds(shape, dtype):
    return jax.ShapeDtypeStruct(shape, dtype)


def _rms(x, g):
    r = lax.rsqrt(jnp.mean(x * x, axis=-1, keepdims=True) + EPS)
    return x * r * g


def _rms_bwd(x, g, dy):
    r = lax.rsqrt(jnp.mean(x * x, axis=-1, keepdims=True) + EPS)
    xh = x * r
    dxh = dy * g
    dx = r * (dxh - xh * jnp.mean(dxh * xh, axis=-1, keepdims=True))
    return dx, dy * xh


def _sigmoid(x):
    return 1.0 / (1.0 + jnp.exp(-x))


def _lane_masks():
    lane = lax.broadcasted_iota(jnp.int32, (1, LANES), 1)
    return lane < HALF


def _split_heads(a, lo):
    z = jnp.zeros_like(a)
    return [jnp.where(lo, a, z), jnp.where(lo, z, a)]


def _rope_consts():
    inv = np.zeros((8, LANES), np.float32)
    j = np.arange(ROPE // 2, dtype=np.float32)
    f = (1.0 / (ROPE_THETA ** (np.arange(0, ROPE, 2, dtype=np.float32) / ROPE))).astype(np.float32)
    inv[0, HALF:HALF + 16] = f
    inv[0, HALF + 16:HALF + 32] = f
    inv[1, HALF:HALF + 16] = -1.0
    inv[1, HALF + 16:HALF + 32] = 1.0
    del j
    return jnp.asarray(inv)


def _rope_tables(pos_f, consts):
    ang = pos_f * consts[0:1, :]
    sign = consts[1:2, :]
    c = jnp.where(sign != 0.0, jnp.cos(ang), 1.0)
    s = jnp.sin(ang) * sign
    return c, s


def _swap_halves(v, sign):
    lo = pltpu.roll(v, LANES - 16, axis=1)
    hi = pltpu.roll(v, 16, axis=1)
    return jnp.where(sign < 0.0, lo, jnp.where(sign > 0.0, hi, 0.0))


def _rope(x, c, s, sign):
    return x * c + _swap_halves(x, sign) * s


def _rope_t(dy, c, s, sign):
    return dy * c + _swap_halves(dy * s, sign)


def _layer0_in(x, pos, g_in, w_in, g_q, w_q, g_kv, w_kv):
    S = x.shape[0]
    consts = _rope_consts()

    def body(x_ref, pos_ref, c_ref, g_ref, w_ref, gq_ref, wq_ref, gkv_ref, wkv_ref,
             h_ref, cq_ref, ckv_ref, qm_ref, km_ref, vm_ref,
             qs_ref, kd_ref, vd_ref, gate_ref, cos_ref, sin_ref):
        h = _rms(x_ref[...], g_ref[...])
        h_ref[...] = h.astype(h_ref.dtype)
        z = _mm_nt(h, w_ref[...])
        cq = z[:, 0:256]
        ckv = z[:, 256:384]
        kpe = z[:, 384:512]
        cq_ref[...] = cq
        ckv_ref[...] = ckv
        qs_ref[...] = z[:, 512:1024].astype(qs_ref.dtype)
        kd_ref[...] = z[:, 1024:1536].astype(kd_ref.dtype)
        vd_ref[...] = z[:, 1536:2048].astype(vd_ref.dtype)
        gate_ref[...] = z[:, 2048:3072]
        cqn = _rms(cq, gq_ref[...])
        ckvn = _rms(ckv, gkv_ref[...])
        q = _mm_nt(cqn, wq_ref[...])
        kv = _mm(ckvn, wkv_ref[...])
        vm_ref[...] = kv[:, 1024:1536].astype(vm_ref.dtype)
        consts_v = c_ref[...]
        sign = consts_v[1:2, :]
        c, s = _rope_tables(pos_ref[...].astype(jnp.float32), consts_v)
        cos_ref[...] = c
        sin_ref[...] = s
        kpe_r = _rope(kpe, c, s, sign)
        for hd in range(N_MLA):
            sl = slice(LANES * hd, LANES * (hd + 1))
            qm_ref[:, sl] = _rope(q[:, sl], c, s, sign).astype(qm_ref.dtype)
            km_ref[:, sl] = (kv[:, sl] + kpe_r).astype(km_ref.dtype)

    outs = [
        ((S, D), MXU), ((S, 256), jnp.float32), ((S, 128), jnp.float32),
        ((S, 1024), MXU), ((S, 1024), MXU), ((S, 512), MXU), ((S, 512), MXU), ((S, 512), MXU), ((S, 512), MXU),
        ((S, 1024), jnp.float32), ((S, 128), jnp.float32), ((S, 128), jnp.float32),
    ]
    return _pcall(
        body, name="layer0_in", grid=(S // TOK,), semantics=("arbitrary",),
        in_specs=[_rows(TOK, D), _rows(TOK, 1), _full((8, LANES)), _full((1, D)), _full(w_in.shape), _full((1, 256)),
                  _full(w_q.shape), _full((1, 128)), _full(w_kv.shape)],
        out_specs=[_rows(TOK, s[1]) for s, _ in outs],
        out_shape=[_sds(s, d) for s, d in outs],
    )(x, pos, consts, g_in, w_in, g_q, w_q, g_kv, w_kv)


AUG = (HALF, 0)
ONE = (HALF + 8, 8)


def _data_lanes(idx, h):
    return (idx < HALF) if h == 0 else (idx >= HALF)


def _three_terms(x):
    hi = x.astype(MXU).astype(jnp.float32)
    mid = (x - hi).astype(MXU).astype(jnp.float32)
    lo = (x - hi - mid).astype(MXU).astype(jnp.float32)
    return hi, mid, lo


def _q_aug(qblk, lc, h, scale, lane):
    a = AUG[h]
    hi, mid, lo = _three_terms(lc)
    ones = ((lane >= a + 3) & (lane <= a + 5)).astype(jnp.float32)
    aug = jnp.where(lane == a, hi, jnp.where(lane == a + 1, mid, jnp.where(lane == a + 2, lo, ones)))
    return jnp.where(_data_lanes(lane, h), qblk * jnp.asarray(scale, qblk.dtype), aug.astype(qblk.dtype))


def _k_aug(kblk, lc, h, lane):
    a = AUG[h]
    hi, mid, lo = _three_terms(-lc)
    ones = ((lane >= a) & (lane <= a + 2)).astype(jnp.float32)
    aug = jnp.where(lane == a + 3, hi, jnp.where(lane == a + 4, mid, jnp.where(lane == a + 5, lo, ones)))
    return jnp.where(_data_lanes(lane, h), kblk, aug.astype(kblk.dtype))


def _lc_col(lc_ref, r0, rows, h):
    head = lax.broadcasted_iota(jnp.int32, (1, lc_ref.shape[1]), 1)
    return jnp.sum(jnp.where(head == 2 * pl.program_id(0) + h, lc_ref[pl.ds(r0, rows), :], 0.0), axis=1, keepdims=True)


def _attn_fwd_t(q, k, v, scale, *, split, name, lcc=None, plan=None):
    S = q.shape[0]
    npair = v.shape[1] // LANES
    W = 2 * LANES if split else LANES
    T = ATT
    CH = FWD_CHUNK * T
    assert S % CH == 0
    nq = S // T

    def body(*refs):
        if split:
            q_ref, k_ref, v_ref, o_ref, lse_ref, vt, acc, m_sc = refs
        else:
            q_ref, k_ref, v_ref, lcc_ref, o_ref, lse_ref, kaug, vt, acc, m_sc = refs
        lane = lax.broadcasted_iota(jnp.int32, (1, LANES), 1)
        sub = lax.broadcasted_iota(jnp.int32, (LANES, 1), 0)
        key_minus_qry = lax.broadcasted_iota(jnp.int32, (CH, T), 0) - lax.broadcasted_iota(jnp.int32, (CH, T), 1)

        def prep(i, c):
            r0 = pl.multiple_of(i * T, T)
            vblk = v_ref[pl.ds(r0, T), :].astype(jnp.float32)
            for h in (0, 1):
                vh = jnp.where(_data_lanes(lane, h), vblk, (lane == ONE[h]).astype(jnp.float32))
                vt[h, :, pl.ds(r0, T)] = vh.T.astype(vt.dtype)
                if not split:
                    kaug[h, pl.ds(r0, T), :] = _k_aug(k_ref[pl.ds(r0, T), :], _lc_col(lcc_ref, r0, T, h), h, lane)
            return c

        lax.fori_loop(0, nq, prep, 0)

        def queries(qi):
            q0 = pl.multiple_of(qi * T, T)
            qblk = q_ref[pl.ds(q0, T), :]
            if split:
                return (qblk[:, :LANES], qblk[:, LANES:])
            return tuple(_q_aug(qblk, _lc_col(lcc_ref, q0, T, h), h, scale, lane) for h in (0, 1))

        def scores(qs, c):
            k0 = pl.multiple_of(c * CH, CH)
            out = []
            for h in (0, 1):
                if split:
                    out.append(_mm_nt(k_ref[pl.ds(k0, CH), LANES * h:LANES * (h + 1)], qs[h]) * scale)
                else:
                    out.append(_mm_nt(kaug[h, pl.ds(k0, CH), :], qs[h]))
            return tuple(out)

        def q_block(qi, carry):
            qs, first_scores = carry[:2], carry[2:]
            q0 = pl.multiple_of(qi * T, T)
            acc[...] = jnp.zeros_like(acc)
            m_sc[...] = jnp.full(m_sc.shape, NEG, jnp.float32)

            def absorb(c, sts, masked):
                k0 = pl.multiple_of(c * CH, CH)
                for h in (0, 1):
                    st = sts[h]
                    if masked:
                        st = jnp.where(key_minus_qry <= q0 - k0, st, NEG)
                    m_old = m_sc[h:h + 1, :]
                    m_new = jnp.maximum(m_old, jnp.max(st, axis=0, keepdims=True))
                    alpha = jnp.exp(m_old - m_new)
                    pt = jnp.exp(st - m_new)
                    acc[h] = alpha * acc[h] + _mm(vt[h, :, pl.ds(k0, CH)], pt)
                    m_sc[h:h + 1, :] = m_new

            last = qi // FWD_CHUNK

            def pipelined(c, sts):
                nxt = scores(qs, c + 1)
                absorb(c, sts, False)
                return nxt

            sts = lax.fori_loop(0, last, pipelined, first_scores)
            qs_next = queries(jnp.minimum(qi + 1, nq - 1))
            nxt = qs_next + scores(qs_next, 0)
            absorb(last, sts, True)
            ot = None
            for h in (0, 1):
                a = acc[h]
                l = a[ONE[h]:ONE[h] + 1, :]
                oh = jnp.where(_data_lanes(sub, h), a * (1.0 / l), 0.0)
                ot = oh if ot is None else ot + oh
                lse_ref[0, h:h + 1, pl.ds(q0, T)] = m_sc[h:h + 1, :] + jnp.log(l)
            o_ref[pl.ds(q0, T), :] = ot.T
            return nxt

        qs0 = queries(0)
        lax.fori_loop(0, nq, q_block, qs0 + scores(qs0, 0))

    wide = pl.BlockSpec((S, W), lambda j: (0, j))
    slab = pl.BlockSpec((S, LANES), lambda j: (0, j))
    rows = pl.BlockSpec((1, 2, S), lambda j: (j, 0, 0))
    in_specs = [wide, wide, slab]
    args = [q, k, v]
    scratch = []
    if not split:
        in_specs.append(_full(lcc.shape))
        args.append(lcc)
        scratch.append(pltpu.VMEM((2, S, LANES), MXU))
    scratch += [pltpu.VMEM((2, LANES, S), MXU), pltpu.VMEM((2, LANES, T), jnp.float32), pltpu.VMEM((8, T), jnp.float32)]
    (o, lse), rode = _pcall_riding(
        body, plan, args, name=name, grid=(npair,), in_specs=in_specs, out_specs=[slab, rows],
        out_shape=[_sds((S, npair * LANES), jnp.float32), _sds((npair, 2, S), jnp.float32)], scratch_shapes=scratch)
    return o, lse, rode


def _attn_bwd_t(q, k, v, do, o, lse, scale, *, split, name, lcc=None, plan=None):
    S = q.shape[0]
    npair = v.shape[1] // LANES
    W = 2 * LANES if split else LANES
    T = ATT
    CH = BWD_CHUNK * T
    assert S % CH == 0
    nq = S // T

    def body(*refs):
        if split:
            (q_ref, k_ref, v_ref, do_ref, o_ref, lse_ref, dq_ref, dk_ref, dv_ref, dqt, delta, dk_acc, dv_acc) = refs
        else:
            (q_ref, k_ref, v_ref, do_ref, o_ref, lse_ref, lcc_ref, dq_ref, dk_ref, dv_ref, dlc_ref,
             dqt, delta, dk_acc, dv_acc, qaug, csum) = refs
        lane = lax.broadcasted_iota(jnp.int32, (1, LANES), 1)
        sub = lax.broadcasted_iota(jnp.int32, (LANES, 1), 0)
        key_minus_qry = lax.broadcasted_iota(jnp.int32, (T, CH), 0) - lax.broadcasted_iota(jnp.int32, (T, CH), 1)

        def prep(i, c):
            r0 = pl.multiple_of(i * T, T)
            prod_t = (do_ref[pl.ds(r0, T), :].astype(jnp.float32) * o_ref[pl.ds(r0, T), :]).T
            for h in (0, 1):
                delta[h:h + 1, pl.ds(r0, T)] = jnp.sum(jnp.where(_data_lanes(sub, h), prod_t, 0.0), axis=0, keepdims=True)
                dqt[h, :, pl.ds(r0, T)] = jnp.zeros((LANES, T), jnp.float32)
                if not split:
                    qaug[h, pl.ds(r0, T), :] = _q_aug(q_ref[pl.ds(r0, T), :], _lc_col(lcc_ref, r0, T, h), h, scale, lane)
            return c

        lax.fori_loop(0, nq, prep, 0)

        def keys(ki):
            k0 = pl.multiple_of(ki * T, T)
            kblk = k_ref[pl.ds(k0, T), :]
            if split:
                return (kblk[:, :LANES], kblk[:, LANES:])
            return tuple(_k_aug(kblk, _lc_col(lcc_ref, k0, T, h), h, lane) for h in (0, 1))

        def q_of(c, h):
            q0 = pl.multiple_of(c * CH, CH)
            if split:
                return q_ref[pl.ds(q0, CH), LANES * h:LANES * (h + 1)]
            return qaug[h, pl.ds(q0, CH), :]

        def scores(khs, c):
            out = []
            for h in (0, 1):
                st = _mm_nt(khs[h], q_of(c, h))
                out.append(st * scale if split else st)
            return tuple(out)

        def k_block(ki, carry):
            khs, first_scores = carry[:2], carry[2:]
            k0 = pl.multiple_of(ki * T, T)
            khts = [kh.astype(jnp.float32).T.astype(kh.dtype) for kh in khs]
            vhs = _split_heads(v_ref[pl.ds(k0, T), :], lane < HALF)
            dk_acc[...] = jnp.zeros_like(dk_acc)
            dv_acc[...] = jnp.zeros_like(dv_acc)

            def absorb(c, vals):
                q0 = pl.multiple_of(c * CH, CH)
                dos = _split_heads(do_ref[pl.ds(q0, CH), :], lane < HALF)
                visible = key_minus_qry <= q0 - k0
                for h in (0, 1):
                    dpt = _mm_nt(vhs[h], dos[h])
                    st = jnp.where(visible, vals[h], NEG)
                    pt = jnp.exp(st - lse_ref[0, h:h + 1, pl.ds(q0, CH)])
                    dv_acc[...] += _mm(pt, dos[h])
                    dst = pt * (dpt - delta[h:h + 1, pl.ds(q0, CH)])
                    dk_acc[h] += _mm(dst, q_of(c, h))
                    dqt[h, :, pl.ds(q0, CH)] += _mm(khts[h], dst)

            first = ki // BWD_CHUNK

            def pipelined(c, vals):
                nxt = scores(khs, c + 1)
                absorb(c, vals)
                return nxt

            vals = lax.fori_loop(first, S // CH - 1, pipelined, first_scores)
            kn = jnp.minimum(ki + 1, nq - 1)
            khs_next = keys(kn)
            nxt = khs_next + scores(khs_next, kn // BWD_CHUNK)
            absorb(S // CH - 1, vals)
            if split:
                dk_ref[pl.ds(k0, T), :LANES] = (dk_acc[0] * scale).astype(dk_ref.dtype)
                dk_ref[pl.ds(k0, T), LANES:] = (dk_acc[1] * scale).astype(dk_ref.dtype)
            else:
                dk_ref[pl.ds(k0, T), :] = jnp.where(lane < HALF, dk_acc[0], dk_acc[1]).astype(dk_ref.dtype)
                for h in (0, 1):
                    csum[h:h + 1, pl.ds(k0, T)] = dk_acc[h].T[AUG[h] + 3:AUG[h] + 4, :]
            dv_ref[pl.ds(k0, T), :] = dv_acc[...].astype(dv_ref.dtype)
            return nxt

        khs0 = keys(0)
        lax.fori_loop(0, nq, k_block, khs0 + scores(khs0, 0))

        def finish(i, c):
            r0 = pl.multiple_of(i * T, T)
            if split:
                for h in (0, 1):
                    dq_ref[pl.ds(r0, T), LANES * h:LANES * (h + 1)] = (dqt[h, :, pl.ds(r0, T)].T * scale).astype(dq_ref.dtype)
            else:
                d = jnp.where(sub < HALF, dqt[0, :, pl.ds(r0, T)], dqt[1, :, pl.ds(r0, T)])
                dq_ref[pl.ds(r0, T), :] = (d.T * scale).astype(dq_ref.dtype)
                for h in (0, 1):
                    dlc_ref[0, h:h + 1, pl.ds(r0, T)] = dqt[h, AUG[h]:AUG[h] + 1, pl.ds(r0, T)] - csum[h:h + 1, pl.ds(r0, T)]
            return c

        lax.fori_loop(0, nq, finish, 0)

    wide = pl.BlockSpec((S, W), lambda j: (0, j))
    slab = pl.BlockSpec((S, LANES), lambda j: (0, j))
    rows = pl.BlockSpec((1, 2, S), lambda j: (j, 0, 0))
    in_specs = [wide, wide, slab, slab, slab, rows]
    args = [q, k, v, do, o, lse]
    out_specs = [wide, wide, slab]
    out_shape = [_sds(q.shape, jnp.float32 if split else do.dtype), _sds(k.shape, jnp.float32 if split else do.dtype),
                 _sds(v.shape, do.dtype)]
    scratch = [pltpu.VMEM((2, LANES, S), jnp.float32), pltpu.VMEM((8, S), jnp.float32),
               pltpu.VMEM((2, T, LANES), jnp.float32), pltpu.VMEM((T, LANES), jnp.float32)]
    if not split:
        in_specs.append(_full(lcc.shape))
        args.append(lcc)
        out_specs.append(rows)
        out_shape.append(_sds((npair, 2, S), jnp.float32))
        scratch += [pltpu.VMEM((2, S, LANES), MXU), pltpu.VMEM((8, S), jnp.float32)]
    outs, rode = _pcall_riding(body, plan, args, name=name, grid=(npair,), in_specs=in_specs, out_specs=out_specs,
                               out_shape=out_shape, scratch_shapes=scratch)
    return (*outs, rode)


def _swa_bias(slope, shift):
    a = lax.broadcasted_iota(jnp.int32, (WINDOW, 2 * WINDOW), 0)
    c = lax.broadcasted_iota(jnp.int32, (WINDOW, 2 * WINDOW), 1)
    dist = a - c + shift
    return jnp.where((dist >= 0) & (dist < WINDOW), -slope * dist.astype(jnp.float32), NEG)


def _swa_scores(qh, kblk, bias):
    return _mm_nt(qh, kblk) * (HEAD ** -0.5) + bias


def _swa_fwd(q, kd, vd, sinks, slopes):
    S = q.shape[0]
    npair = q.shape[1] // LANES
    nb = S // WINDOW

    def body(sink_ref, slope_ref, q_ref, k_ref, v_ref, o_ref, lse_ref):
        j = pl.program_id(0)
        lo = _lane_masks()
        biases = [(_swa_bias(slope_ref[2 * j + h], 0), _swa_bias(slope_ref[2 * j + h], WINDOW)) for h in (0, 1)]

        def q_block(qi, c):
            q0 = pl.multiple_of(qi * WINDOW, WINDOW)
            k0 = pl.multiple_of(jnp.maximum(qi - 1, 0) * WINDOW, WINDOW)
            qs = _split_heads(q_ref[pl.ds(q0, WINDOW), :], lo)
            kblk = k_ref[pl.ds(k0, 2 * WINDOW), :]
            vs = _split_heads(v_ref[pl.ds(k0, 2 * WINDOW), :], lo)
            o = None
            for h in (0, 1):
                sink = sink_ref[2 * j + h]
                s = _swa_scores(qs[h], kblk, jnp.where(qi == 0, *biases[h]))
                m = jnp.maximum(jnp.max(s, axis=1, keepdims=True), sink)
                p = jnp.exp(s - m)
                den = jnp.sum(p, axis=1, keepdims=True) + jnp.exp(sink - m)
                oh = _mm(p / den, vs[h])
                o = oh if o is None else o + oh
                lse_ref[h, pl.ds(q0, WINDOW), :] = m + jnp.log(den)
            o_ref[pl.ds(q0, WINDOW), :] = o
            return c

        def q_group(gi, c):
            for g in range(SWA_GROUP):
                q_block(gi * SWA_GROUP + g, c)
            return c

        lax.fori_loop(0, nb // SWA_GROUP, q_group, 0)

    smem = pl.BlockSpec(memory_space=pltpu.SMEM)
    slab = pl.BlockSpec((S, LANES), lambda j: (0, j))
    return _pcall(
        body, name="swa_fwd", grid=(npair,), semantics=("arbitrary",),
        in_specs=[smem, smem, slab, slab, slab],
        out_specs=[slab, pl.BlockSpec((2, S, 1), lambda j: (j, 0, 0))],
        out_shape=[_sds((S, npair * LANES), jnp.float32), _sds((2 * npair, S, 1), jnp.float32)],
    )(sinks, slopes, q, kd, vd)


def _swa_bwd(q, kd, vd, do, o, lse, sinks, slopes, plan=None):
    S = q.shape[0]
    npair = q.shape[1] // LANES
    nb = S // WINDOW

    def body(sink_ref, slope_ref, q_ref, k_ref, v_ref, do_ref, o_ref, lse_ref,
             dq_ref, dk_ref, dv_ref, dsink_ref, dk_acc, dv_acc):
        j = pl.program_id(0)
        lo = _lane_masks()
        dk_acc[...] = jnp.zeros_like(dk_acc)
        dv_acc[...] = jnp.zeros_like(dv_acc)
        biases = [(_swa_bias(slope_ref[2 * j + h], 0), _swa_bias(slope_ref[2 * j + h], WINDOW)) for h in (0, 1)]

        def q_block(qi, carry):
            q0 = pl.multiple_of(qi * WINDOW, WINDOW)
            k0 = pl.multiple_of(jnp.maximum(qi - 1, 0) * WINDOW, WINDOW)
            qs = _split_heads(q_ref[pl.ds(q0, WINDOW), :], lo)
            dos = _split_heads(do_ref[pl.ds(q0, WINDOW), :], lo)
            oblk = o_ref[pl.ds(q0, WINDOW), :]
            kblk = k_ref[pl.ds(k0, 2 * WINDOW), :]
            vblk = v_ref[pl.ds(k0, 2 * WINDOW), :]
            ks = _split_heads(kblk, lo)
            dq = None
            out = []
            for h in (0, 1):
                sink = sink_ref[2 * j + h]
                lse_h = lse_ref[h, pl.ds(q0, WINDOW), :]
                s = _swa_scores(qs[h], kblk, jnp.where(qi == 0, *biases[h]))
                p = jnp.exp(s - lse_h)
                delta = jnp.sum(dos[h].astype(jnp.float32) * oblk, axis=1, keepdims=True)
                dv_acc[pl.ds(k0, 2 * WINDOW), :] += _mm_tn(p, dos[h])
                dp = _mm_nt(dos[h], vblk)
                ds = p * (dp - delta)
                dqh = _mm(ds, ks[h]) * (HEAD ** -0.5)
                dq = dqh if dq is None else dq + dqh
                dk_acc[pl.ds(k0, 2 * WINDOW), :] += _mm_tn(ds, qs[h]) * (HEAD ** -0.5)
                dsk = jnp.sum(-jnp.exp(sink - lse_h) * delta, axis=0, keepdims=True)
                out.append(carry[h] + dsk)
            dq_ref[pl.ds(q0, WINDOW), :] = dq.astype(dq_ref.dtype)
            return tuple(out)

        def q_group(gi, carry):
            for g in range(SWA_GROUP):
                carry = q_block(gi * SWA_GROUP + g, carry)
            return carry

        zero = jnp.zeros((1, 1), jnp.float32)
        dsa, dsb = lax.fori_loop(0, nb // SWA_GROUP, q_group, (zero, zero))
        dk_ref[...] = dk_acc[...].astype(dk_ref.dtype)
        dv_ref[...] = dv_acc[...].astype(dv_ref.dtype)
        r = lax.broadcasted_iota(jnp.int32, (8, LANES), 0)
        dsink_ref[0] = jnp.where(r == 0, dsa, jnp.where(r == 1, dsb, 0.0))

    smem = pl.BlockSpec(memory_space=pltpu.SMEM)
    slab = pl.BlockSpec((S, LANES), lambda j: (0, j))
    outs, rode = _pcall_riding(
        body, plan, [sinks, slopes, q, kd, vd, do, o, lse], name="swa_bwd", grid=(npair,),
        in_specs=[smem, smem, slab, slab, slab, slab, slab, pl.BlockSpec((2, S, 1), lambda j: (j, 0, 0))],
        out_specs=[slab, slab, slab, pl.BlockSpec((1, 8, LANES), lambda j: (j, 0, 0))],
        out_shape=[_sds(q.shape, do.dtype), _sds(kd.shape, do.dtype), _sds(vd.shape, do.dtype),
                   _sds((npair, 8, LANES), jnp.float32)],
        scratch_shapes=[pltpu.VMEM((S, LANES), jnp.float32), pltpu.VMEM((S, LANES), jnp.float32)])
    return (*outs, rode)


def _log_steps(S):
    k, out = 1, []
    while k < S:
        out.append(k)
        k *= 2
    return out


def _forget_fwd(f_row, b_col):
    S = f_row.shape[1]

    def body(f_ref, b_ref, lc_ref):
        x = f_ref[...] + b_ref[...]
        lc = jnp.minimum(x, 0.0) - jnp.log(1.0 + jnp.exp(-jnp.abs(x)))
        idx = lax.broadcasted_iota(jnp.int32, lc.shape, 1)
        for k in _log_steps(S):
            lc = lc + jnp.where(idx >= k, pltpu.roll(lc, k, axis=1), 0.0)
        lc_ref[...] = lc

    return _pcall(body, name="forget_fwd", out_shape=_sds(f_row.shape, jnp.float32))(f_row, b_col)


def _forget_bwd(dlc_row, f_row, b_col):
    S = f_row.shape[1]

    def body(d_ref, f_ref, b_ref, df_ref, db_ref):
        g = d_ref[...]
        idx = lax.broadcasted_iota(jnp.int32, g.shape, 1)
        for k in _log_steps(S):
            g = g + jnp.where(idx < S - k, pltpu.roll(g, S - k, axis=1), 0.0)
        x = f_ref[...] + b_ref[...]
        df = g * _sigmoid(-x)
        df_ref[...] = df
        db_ref[...] = jnp.sum(df, axis=1, keepdims=True)

    return _pcall(body, name="forget_bwd",
                  out_shape=[_sds(f_row.shape, jnp.float32), _sds((f_row.shape[0], 1), jnp.float32)])(dlc_row, f_row, b_col)


def _layer0_out_layer1_in(x, o_m, o_s, gate, w_out, g1, w_in1):
    S = x.shape[0]

    def body(x_ref, om_ref, os_ref, gate_ref, wo_ref, g_ref, w_ref,
             x1_ref, h_ref, q_ref, k_ref, v_ref, g1_ref, f_ref):
        gt = gate_ref[...]
        sg = gt * _sigmoid(gt)
        um = om_ref[...] * sg[:, :512]
        us = os_ref[...] * sg[:, 512:]
        x1 = x_ref[...] + _mm(um, wo_ref[0:512, :]) + _mm(us, wo_ref[512:1024, :])
        x1_ref[...] = x1
        h = _rms(x1, g_ref[...])
        h_ref[...] = h.astype(h_ref.dtype)
        z = _mm_nt(h, w_ref[...])
        q_ref[...] = z[:, 0:1024].astype(q_ref.dtype)
        k_ref[...] = z[:, 1024:2048].astype(k_ref.dtype)
        v_ref[...] = z[:, 2048:3072].astype(v_ref.dtype)
        g1_ref[...] = z[:, 3072:4096]
        f_ref[...] = z[:, 4096:4224]

    outs = [((S, D), jnp.float32), ((S, D), MXU), ((S, D), MXU), ((S, D), MXU), ((S, D), MXU),
            ((S, D), jnp.float32), ((S, LANES), jnp.float32)]
    return _pcall(
        body, name="layer0_out_layer1_in", grid=(S // TOK,), semantics=("arbitrary",),
        in_specs=[_rows(TOK, D), _rows(TOK, 512), _rows(TOK, 512), _rows(TOK, D), _full((D, D)), _full((1, D)),
                  _full(w_in1.shape)],
        out_specs=[_rows(TOK, s[1]) for s, _ in outs],
        out_shape=[_sds(s, d) for s, d in outs],
    )(x, o_m, o_s, gate, w_out, g1, w_in1)


def _head(x1, o1, gate1, w_out1, g_f, target):
    S = x1.shape[0]

    def body(x1_ref, o_ref, gate_ref, wo_ref, g_ref, t_ref,
             loss_ref, dgf_ref, dwo_ref, dx2_ref, do_ref, dgate_ref):
        i = pl.program_id(0)
        gt = gate_ref[...]
        sig = _sigmoid(gt)
        sg = gt * sig
        o = o_ref[...]
        u = o * sg
        x2 = x1_ref[...] + _mm(u, wo_ref[...])
        g = g_ref[...]
        y = _rms(x2, g)
        err = y - t_ref[...]
        part = 0.5 * jnp.sum(jnp.mean(err * err, axis=-1, keepdims=True), axis=0, keepdims=True)
        dy = err * (1.0 / D)
        dx2, dg_rows = _rms_bwd(x2, g, dy)
        dx2_ref[...] = dx2
        du = _mm_nt(dx2, wo_ref[...])
        do_ref[...] = (du * sg).astype(do_ref.dtype)
        dgate_ref[...] = (du * o * (sig * (1.0 + gt * (1.0 - sig)))).astype(dgate_ref.dtype)

        @pl.when(i == 0)
        def _():
            loss_ref[...] = jnp.zeros_like(loss_ref)
            dgf_ref[...] = jnp.zeros_like(dgf_ref)
            dwo_ref[...] = jnp.zeros_like(dwo_ref)

        loss_ref[...] += jnp.broadcast_to(part, loss_ref.shape)
        dgf_ref[...] += jnp.sum(dg_rows, axis=0, keepdims=True)
        dwo_ref[...] += _mm_tn(u, dx2)

    outs = [((S, D), jnp.float32), ((S, D), MXU), ((S, D), MXU)]
    return _pcall(
        body, name="head", grid=(S // TOK,), semantics=("arbitrary",),
        in_specs=[_rows(TOK, D), _rows(TOK, D), _rows(TOK, D), _full((D, D)), _full((1, D)), _rows(TOK, D)],
        out_specs=[_full((8, LANES)), _full((1, D)), _full((D, D))] + [_rows(TOK, D) for _ in outs],
        out_shape=[_sds((8, LANES), jnp.float32), _sds((1, D), jnp.float32), _sds((D, D), jnp.float32)]
        + [_sds(s, d) for s, d in outs],
    )(x1, o1, gate1, w_out1, g_f, target)


def _layer1_in_bwd(dq, dk, dv, dgate1, df, x1, dx2, g1, w_in1, gate0, o_m, o_s, w_out0):
    S = x1.shape[0]

    def body(dq_ref, dk_ref, dv_ref, dg1_ref, df_ref, x1_ref, dx2_ref, g_ref, w_ref, gate_ref, om_ref, os_ref,
             wo_ref, dz_ref, dx1_ref, dgn_ref, dwo_ref, dom_ref, dos_ref, dgate_ref):
        i = pl.program_id(0)
        dz_ref[:, 0:1024] = dq_ref[...]
        dz_ref[:, 1024:2048] = dk_ref[...]
        dz_ref[:, 2048:3072] = dv_ref[...]
        dz_ref[:, 3072:4096] = dg1_ref[...]
        dz_ref[:, 4096:4224] = df_ref[...]
        dh = _mm(dz_ref[...], w_ref[...])
        g = g_ref[...]
        dxn, dg_rows = _rms_bwd(x1_ref[...], g, dh)
        dx1 = dx2_ref[...] + dxn
        dx1_ref[...] = dx1
        du = _mm_nt(dx1, wo_ref[...])
        gt = gate_ref[...]
        sig = _sigmoid(gt)
        sg = gt * sig
        dsg = sig * (1.0 + gt * (1.0 - sig))
        dom_ref[...] = (du[:, :512] * sg[:, :512]).astype(dom_ref.dtype)
        dos_ref[...] = (du[:, 512:] * sg[:, 512:]).astype(dos_ref.dtype)
        dgate_ref[:, :512] = (du[:, :512] * om_ref[...] * dsg[:, :512]).astype(dgate_ref.dtype)
        dgate_ref[:, 512:] = (du[:, 512:] * os_ref[...] * dsg[:, 512:]).astype(dgate_ref.dtype)

        @pl.when(i == 0)
        def _():
            dgn_ref[...] = jnp.zeros_like(dgn_ref)
            dwo_ref[...] = jnp.zeros_like(dwo_ref)

        dgn_ref[...] += jnp.sum(dg_rows, axis=0, keepdims=True)
        dwo_ref[0:512, :] += _mm_tn(om_ref[...] * sg[:, :512], dx1)
        dwo_ref[512:1024, :] += _mm_tn(os_ref[...] * sg[:, 512:], dx1)

    return _pcall(
        body, name="layer1_in_bwd", grid=(S // TOK,), semantics=("arbitrary",),
        in_specs=[_rows(TOK, D), _rows(TOK, D), _rows(TOK, D), _rows(TOK, D), _rows(TOK, LANES), _rows(TOK, D),
                  _rows(TOK, D), _full((1, D)), _full(w_in1.shape), _rows(TOK, D), _rows(TOK, 512), _rows(TOK, 512),
                  _full((D, D))],
        out_specs=[_rows(TOK, 4224), _rows(TOK, D), _full((1, D)), _full((D, D)), _rows(TOK, 512), _rows(TOK, 512),
                   _rows(TOK, D)],
        out_shape=[_sds((S, 4224), MXU), _sds((S, D), jnp.float32), _sds((1, D), jnp.float32), _sds((D, D), jnp.float32),
                   _sds((S, 512), MXU), _sds((S, 512), MXU), _sds((S, D), MXU)],
    )(dq, dk, dv, dgate1, df, x1, dx2, g1, w_in1, gate0, o_m, o_s, w_out0)


def _layer0_in_bwd(dqm, dkm, dvm, dqs, dkd, dvd, dgate0, cos, sin, cq, ckv, x, dx1, g_in, w_in, g_q, w_q, g_kv, w_kv):
    S = x.shape[0]
    consts = _rope_consts()

    def body(dqm_ref, dkm_ref, dvm_ref, dqs_ref, dkd_ref, dvd_ref, dgate_ref, cos_ref, sin_ref, c_ref, cq_ref, ckv_ref,
             x_ref, dx1_ref, g_ref, w_ref, gq_ref, wq_ref, gkv_ref, wkv_ref,
             dx_ref, dz_ref, dgin_ref, dgq_ref, dgkv_ref, dwq_ref, dwkv_ref, dqu_ref, dkvu_ref):
        i = pl.program_id(0)
        lo = _lane_masks()
        sign = c_ref[...][1:2, :]
        c = cos_ref[...]
        s = sin_ref[...]
        dkpe = None
        for hd in range(N_MLA):
            sl = slice(LANES * hd, LANES * (hd + 1))
            dqu_ref[:, sl] = _rope_t(dqm_ref[:, sl], c, s, sign).astype(dqu_ref.dtype)
            dkh = dkm_ref[:, sl]
            dkvu_ref[:, sl] = jnp.where(lo, dkh, 0.0).astype(dkvu_ref.dtype)
            dkpe = dkh if dkpe is None else dkpe + dkh
        dkvu_ref[:, 1024:1536] = dvm_ref[...]
        dkpe = _rope_t(jnp.where(lo, 0.0, dkpe), c, s, sign)
        dcqn = _mm(dqu_ref[...], wq_ref[...])
        dckvn = _mm_nt(dkvu_ref[...], wkv_ref[...])
        gq = gq_ref[...]
        gkv = gkv_ref[...]
        dcq, dgq_rows = _rms_bwd(cq_ref[...], gq, dcqn)
        dckv, dgkv_rows = _rms_bwd(ckv_ref[...], gkv, dckvn)
        dz_ref[:, 0:256] = dcq.astype(dz_ref.dtype)
        dz_ref[:, 256:384] = dckv.astype(dz_ref.dtype)
        dz_ref[:, 384:512] = dkpe.astype(dz_ref.dtype)
        dz_ref[:, 512:1024] = dqs_ref[...]
        dz_ref[:, 1024:1536] = dkd_ref[...]
        dz_ref[:, 1536:2048] = dvd_ref[...]
        dz_ref[:, 2048:3072] = dgate_ref[...]
        dh = _mm(dz_ref[...], w_ref[...])
        g = g_ref[...]
        dxn, dg_rows = _rms_bwd(x_ref[...], g, dh)
        dx_ref[...] = dx1_ref[...] + dxn

        @pl.when(i == 0)
        def _():
            dgin_ref[...] = jnp.zeros_like(dgin_ref)
            dgq_ref[...] = jnp.zeros_like(dgq_ref)
            dgkv_ref[...] = jnp.zeros_like(dgkv_ref)
            dwq_ref[...] = jnp.zeros_like(dwq_ref)
            dwkv_ref[...] = jnp.zeros_like(dwkv_ref)

        dgin_ref[...] += jnp.sum(dg_rows, axis=0, keepdims=True)
        dgq_ref[...] += jnp.sum(dgq_rows, axis=0, keepdims=True)
        dgkv_ref[...] += jnp.sum(dgkv_rows, axis=0, keepdims=True)
        dwq_ref[...] += _mm_tn(dqu_ref[...], _rms(cq_ref[...], gq))
        dwkv_ref[...] += _mm_tn(_rms(ckv_ref[...], gkv), dkvu_ref[...])

    return _pcall(
        body, name="layer0_in_bwd", grid=(S // TOK,), semantics=("arbitrary",),
        in_specs=[_rows(TOK, 1024), _rows(TOK, 1024), _rows(TOK, 512), _rows(TOK, 512), _rows(TOK, 512), _rows(TOK, 512),
                  _rows(TOK, D), _rows(TOK, LANES), _rows(TOK, LANES), _full((8, LANES)), _rows(TOK, 256), _rows(TOK, 128),
                  _rows(TOK, D), _rows(TOK, D), _full((1, D)), _full(w_in.shape), _full((1, 256)), _full(w_q.shape),
                  _full((1, 128)), _full(w_kv.shape)],
        out_specs=[_rows(TOK, D), _rows(TOK, 3072), _full((1, D)), _full((1, 256)), _full((1, 128)), _full(w_q.shape),
                   _full(w_kv.shape)],
        out_shape=[_sds((S, D), jnp.float32), _sds((S, 3072), MXU), _sds((1, D), jnp.float32), _sds((1, 256), jnp.float32),
                   _sds((1, 128), jnp.float32), _sds(w_q.shape, jnp.float32), _sds(w_kv.shape, jnp.float32)],
        scratch_shapes=[pltpu.VMEM((TOK, 1024), MXU), pltpu.VMEM((TOK, 1536), MXU)],
    )(dqm, dkm, dvm, dqs, dkd, dvd, dgate0, cos, sin, consts, cq, ckv, x, dx1, g_in, w_in, g_q, w_q, g_kv, w_kv)


def _wgrad(a, b, name):
    S, M = a.shape
    N = b.shape[1]
    tm = next(t for t in range(WG_ROWS, 0, -LANES) if M % t == 0)
    tn = N if N <= 1024 else 512
    tk = min(WG_TOK, S)

    def body(a_ref, b_ref, o_ref):
        @pl.when(pl.program_id(2) == 0)
        def _():
            o_ref[...] = jnp.zeros_like(o_ref)

        o_ref[...] += _mm_tn(a_ref[...], b_ref[...])

    return _pcall(
        body, name=name, grid=(M // tm, N // tn, S // tk), semantics=("parallel", "parallel", "arbitrary"),
        in_specs=[pl.BlockSpec((tk, tm), lambda m, n, k: (k, m)), pl.BlockSpec((tk, tn), lambda m, n, k: (k, n))],
        out_specs=pl.BlockSpec((tm, tn), lambda m, n, k: (m, n)),
        out_shape=_sds((M, N), jnp.float32),
    )(a, b)


def _adamw(w, g, m, v, name):
    shape = w.shape
    R, C = (int(np.prod(shape[:-1])), shape[-1])
    w2, g2, m2, v2 = (t.reshape(R, C) for t in (w, g, m, v))
    fits = [t for t in range(8, ADAM_TILE_BYTES // (4 * C) + 1, 8) if R % t == 0]
    tr = max(fits) if fits else R
    tc = C if (tr * C * 4 <= ADAM_TILE_BYTES or C % 256) else 256

    def body(w_ref, g_ref, m_ref, v_ref, d_ref, nm_ref, nv_ref):
        gg = g_ref[...]
        nm = B1 * m_ref[...] + (1.0 - B1) * gg
        nv = B2 * v_ref[...] + (1.0 - B2) * (gg * gg)
        m_hat = nm / (1.0 - B1 ** STEP)
        v_hat = nv / (1.0 - B2 ** STEP)
        d_ref[...] = -LR * (m_hat / (jnp.sqrt(v_hat) + AEPS) + WD * w_ref[...])
        nm_ref[...] = nm
        nv_ref[...] = nv

    spec = pl.BlockSpec((tr, tc), lambda i, j: (i, j))
    d, nm, nv = _pcall(
        body, name=name, grid=(R // tr, C // tc), semantics=("parallel", "parallel"),
        in_specs=[spec] * 4, out_specs=[spec] * 3, out_shape=[_sds((R, C), jnp.float32)] * 3,
    )(w2, g2, m2, v2)
    return d.reshape(shape), nm.reshape(shape), nv.reshape(shape)


def _sum_leading(a, name):
    n, R, C = a.shape
    tr = SUM_ROWS if R % SUM_ROWS == 0 else R

    def body(a_ref, o_ref):
        acc = a_ref[0]
        for i in range(1, n):
            acc = acc + a_ref[i]
        o_ref[...] = acc

    return _pcall(
        body, name=name, grid=(R // tr,), semantics=("parallel",),
        in_specs=[pl.BlockSpec((n, tr, C), lambda i: (0, i, 0))], out_specs=_rows(tr, C),
        out_shape=_sds((R, C), a.dtype),
    )(a)


def _add_halves(g, c, b, name, out_dtype):
    n, _, R, C = g.shape
    tr = SUM_ROWS if R % SUM_ROWS == 0 else R

    def body(c_ref, a_ref, b_ref, o_ref):
        o_ref[...] = (a_ref[0] + b_ref[...]).astype(o_ref.dtype)

    spec = pl.BlockSpec((1, tr, C), lambda k, i, c_ref: (k, i, 0))
    grid_spec = pltpu.PrefetchScalarGridSpec(
        num_scalar_prefetch=1, grid=(n, R // tr),
        in_specs=[pl.BlockSpec((1, 1, tr, C), lambda k, i, c_ref: (k, c_ref[0], i, 0)), spec], out_specs=spec)
    return _pcall(body, name=name, semantics=("parallel", "parallel"), grid_spec=grid_spec,
                  out_shape=_sds(b.shape, out_dtype))(c.reshape(1).astype(jnp.int32), g, b)


def _total_sum(mine, theirs, recv, name):
    R, C = mine.shape
    n = recv.shape[0]
    tr = SUM_ROWS if R % SUM_ROWS == 0 else R

    def body(a_ref, b_ref, r_ref, o_ref):
        acc = a_ref[...] + b_ref[...]
        for i in range(n):
            acc = acc + r_ref[i].astype(jnp.float32)
        o_ref[...] = acc

    return _pcall(
        body, name=name, grid=(R // tr,), semantics=("parallel",),
        in_specs=[_rows(tr, C), _rows(tr, C), pl.BlockSpec((n, tr, C), lambda i: (0, i, 0))], out_specs=_rows(tr, C),
        out_shape=_sds((R, C), jnp.float32),
    )(mine, theirs, recv)


def _place():
    return lax.axis_index("x"), lax.axis_index("y"), lax.axis_index("c")


class _Plan:
    def __init__(self, arrays, out_shape, scratch, start, finish, middle=None):
        self.arrays, self.out_shape, self.scratch = list(arrays), list(out_shape), list(scratch)
        self.start, self.finish, self.middle = start, finish, middle


def _gather8_plan(block):
    R, C = block.shape

    def parts(ins, outs, sems):
        (x_ref,), (out_ref,), (send_sems, recv_sems) = ins, outs, sems
        x, y, c = _place()
        me, sibling = (x, y, c), (x, y, 1 - c)
        chips = [(1 - x, y), (x, 1 - y), (1 - x, 1 - y)]

        def copy(k, blk, to, src=None):
            slot = out_ref.at[4 * blk[0] + 2 * blk[1] + blk[2]]
            return pltpu.make_async_remote_copy(
                src_ref=slot if src is None else src, dst_ref=slot,
                send_sem=send_sems.at[k], recv_sem=recv_sems.at[k], device_id=to, device_id_type=MESH_ID)

        def first():
            return [copy(0, me, sibling, src=x_ref)] + [copy(1 + j, me, (*chip, c), src=x_ref) for j, chip in enumerate(chips)]

        def passed():
            return [copy(4 + j, (*chip, c), sibling) for j, chip in enumerate(chips)]

        def arrivals():
            return [copy(1 + j, (*chip, c), me) for j, chip in enumerate(chips)]

        def late():
            return [copy(0, sibling, me)] + [copy(4 + j, (*chip, 1 - c), me) for j, chip in enumerate(chips)]

        return first, passed, arrivals, late

    def start(ins, outs, sems):
        for cp in parts(ins, outs, sems)[0]():
            cp.start()

    def middle(ins, outs, sems):
        _, passed, arrivals, _ = parts(ins, outs, sems)
        for arrived, forward in zip(arrivals(), passed()):
            arrived.wait_recv()
            forward.start()

    def finish(ins, outs, sems):
        first, passed, _, late = parts(ins, outs, sems)
        for cp in late():
            cp.wait_recv()
        for cp in first() + passed():
            cp.wait_send()

    return _Plan([block], [_sds((8, R, C), block.dtype)], [pltpu.SemaphoreType.DMA((7,)), pltpu.SemaphoreType.DMA((7,))],
                 start, finish, middle)


def _fill_own_slot(gathered, block):
    x, y, c = _place()
    return lax.dynamic_update_index_in_dim(gathered, block, 4 * x + 2 * y + c, 0)


def _started_and_waited(arrays, out_shape, n, copies):
    def start(ins, outs, sems):
        for cp in copies(ins, outs, sems):
            cp.start()

    def finish(ins, outs, sems):
        for cp in copies(ins, outs, sems):
            cp.wait()

    return _Plan(arrays, out_shape, [pltpu.SemaphoreType.DMA((n,)), pltpu.SemaphoreType.DMA((n,))], start, finish)


def _pair_swap_plan(g):
    n = g.shape[0]

    def copies(ins, outs, sems):
        (g_ref,), (out_ref,), (send_sems, recv_sems) = ins, outs, sems
        x, y, c = _place()
        return [pltpu.make_async_remote_copy(src_ref=g_ref.at[k, 1 - c], dst_ref=out_ref.at[k], send_sem=send_sems.at[k],
                                             recv_sem=recv_sems.at[k], device_id=(x, y, 1 - c), device_id_type=MESH_ID)
                for k in range(n)]

    return _started_and_waited([g], [_sds((n,) + g.shape[2:], g.dtype)], n, copies)


def _chip_exchange_plan(p):
    def copies(ins, outs, sems):
        (p_ref,), (out_ref,), (send_sems, recv_sems) = ins, outs, sems
        x, y, c = _place()
        chips = [(1 - x, y), (x, 1 - y), (1 - x, 1 - y)]
        return [pltpu.make_async_remote_copy(
            src_ref=p_ref.at[2 * cx + cy], dst_ref=out_ref.at[j], send_sem=send_sems.at[j],
            recv_sem=recv_sems.at[j], device_id=(cx, cy, c), device_id_type=MESH_ID)
            for j, (cx, cy) in enumerate(chips)]

    return _started_and_waited([p], [_sds((3,) + p.shape[1:], p.dtype)], 3, copies)


def _pair_exchange_plan(t):
    def copies(ins, outs, sems):
        (t_ref,), (out_ref,), (send_sems, recv_sems) = ins, outs, sems
        x, y, c = _place()
        return [pltpu.make_async_remote_copy(src_ref=t_ref, dst_ref=out_ref, send_sem=send_sems.at[0], recv_sem=recv_sems.at[0],
                                             device_id=(x, y, 1 - c), device_id_type=MESH_ID)]

    return _started_and_waited([t], [_sds(t.shape, t.dtype)], 1, copies)


def _both_plans(a, b):
    na, ma, sa = len(a.arrays), len(a.out_shape), len(a.scratch)

    def phase(name):
        fa, fb = getattr(a, name), getattr(b, name)
        if fa is None and fb is None:
            return None

        def run(ins, outs, sems):
            if fa is not None:
                fa(ins[:na], outs[:ma], sems[:sa])
            if fb is not None:
                fb(ins[na:], outs[ma:], sems[sa:])
        return run

    return _Plan(a.arrays + b.arrays, a.out_shape + b.out_shape, a.scratch + b.scratch,
                 phase("start"), phase("finish"), phase("middle"))


ANY_SPEC = pl.BlockSpec(memory_space=pl.ANY)


def _run_plan(plan, name):
    n_in, n_out = len(plan.arrays), len(plan.out_shape)

    def body(*refs):
        ins, outs, sems = refs[:n_in], refs[n_in:n_in + n_out], refs[n_in + n_out:]
        plan.start(ins, outs, sems)
        if plan.middle is not None:
            plan.middle(ins, outs, sems)
        plan.finish(ins, outs, sems)

    return _pcall(body, name=name, in_specs=[ANY_SPEC] * n_in, out_specs=[ANY_SPEC] * n_out, out_shape=plan.out_shape,
                  scratch_shapes=plan.scratch)(*plan.arrays)


def _pcall_riding(body, plan, args, *, name, grid, in_specs, out_specs, out_shape, scratch_shapes):
    if plan is None:
        outs = _pcall(body, name=name, grid=grid, semantics=("arbitrary",), in_specs=in_specs, out_specs=out_specs,
                      out_shape=out_shape, scratch_shapes=scratch_shapes)(*args)
        return list(outs), None
    n_in, n_out, n_s = len(args), len(out_shape), len(scratch_shapes)
    p_in, p_out = len(plan.arrays), len(plan.out_shape)
    steps = grid[0]

    def riding(*refs):
        ins, pins = refs[:n_in], refs[n_in:n_in + p_in]
        o0 = n_in + p_in
        outs, pouts = refs[o0:o0 + n_out], refs[o0 + n_out:o0 + n_out + p_out]
        s0 = o0 + n_out + p_out
        scr, sems = refs[s0:s0 + n_s], refs[s0 + n_s:]
        j = pl.program_id(0)

        @pl.when(j == 0)
        def _():
            plan.start(pins, pouts, sems)

        if plan.middle is not None:
            @pl.when(j == steps // 2)
            def _():
                plan.middle(pins, pouts, sems)

        body(*ins, *outs, *scr)

        @pl.when(j == steps - 1)
        def _():
            plan.finish(pins, pouts, sems)

    res = _pcall(riding, name=name, grid=grid, semantics=("arbitrary",), in_specs=list(in_specs) + [ANY_SPEC] * p_in,
                 out_specs=list(out_specs) + [ANY_SPEC] * p_out, out_shape=list(out_shape) + plan.out_shape,
                 scratch_shapes=list(scratch_shapes) + plan.scratch)(*args, *plan.arrays)
    return list(res[:n_out]), list(res[n_out:])


class _RowSeq:
    def __init__(self, pieces):
        self.pieces = list(pieces)

    def rows(self, a, b):
        out, off = [], 0
        for p in self.pieces:
            lo, hi = max(a, off), min(b, off + p.shape[0])
            if lo < hi:
                out.append(p[lo - off:hi - off])
            off += p.shape[0]
        return out

    def array(self):
        return jnp.concatenate(self.pieces, axis=0)


def _row_seq(w):
    return w if isinstance(w, _RowSeq) else _RowSeq([w])


def _prep_w_in0(wt):
    wt = _row_seq(wt)
    one = wt.pieces[0]
    z32 = [jnp.zeros((32, one.shape[1]), one.dtype)]
    k0, k1 = wt.rows(928, 992), wt.rows(992, 1056)
    v0, v1 = wt.rows(1056, 1120), wt.rows(1120, 1184)
    return jnp.concatenate(wt.rows(0, 384) + z32 + z32 + wt.rows(384, 416) + z32 + wt.rows(416, 928)
                           + k0 * 4 + k1 * 4 + v0 * 4 + v1 * 4 + wt.rows(1184, 2208), axis=0)


def _fold_w_in0(d):
    def fold(blk):
        b = blk.reshape(8, 64, blk.shape[1])
        return jnp.concatenate([b[0] + b[1] + b[2] + b[3], b[4] + b[5] + b[6] + b[7]], axis=0)
    return _RowSeq([d[0:384], d[448:480], d[512:1024], fold(d[1024:1536]), fold(d[1536:2048]), d[2048:3072]])


def _prep_w_q(wt):
    return jnp.pad(wt.reshape(N_MLA, 96, Q_RANK), ((0, 0), (0, 32), (0, 0))).reshape(1024, Q_RANK)


def _fold_w_q(d):
    return d.reshape(N_MLA, 128, Q_RANK)[:, :96].reshape(768, Q_RANK)


def _prep_w_kv(w):
    w3 = w.reshape(KV_RANK, N_MLA, 128)
    kk = jnp.pad(w3[:, :, :64], ((0, 0), (0, 0), (0, 64))).reshape(KV_RANK, 1024)
    return jnp.concatenate([kk, w3[:, :, 64:].reshape(KV_RANK, 512)], axis=1)


def _fold_w_kv(d):
    kk = d[:, :1024].reshape(KV_RANK, N_MLA, 128)[:, :, :64]
    vv = d[:, 1024:].reshape(KV_RANK, N_MLA, 64)
    return jnp.concatenate([kk, vv], axis=2).reshape(KV_RANK, 1024)


def _prep_w_in1(wt):
    wt = _row_seq(wt)
    one = wt.pieces[0]
    return jnp.concatenate(wt.rows(0, 3072) + wt.rows(3088, 4112) + wt.rows(3072, 3088)
                           + [jnp.zeros((112, one.shape[1]), one.dtype)], axis=0)


def _fold_w_in1(d):
    return _RowSeq([d[0:3072], d[4096:4112], d[3072:4096]])


class _Alone:
    def __init__(self, w_out0, o_g_in, w_in1, w_out1):
        self.layer1 = (w_out0, o_g_in, w_in1, w_out1)

    def gather_plan(self):
        return None

    def layer1_weights(self, rode):
        return self.layer1

    def swap_plan(self, grads1):
        return None

    def exchange_plan(self, rode):
        return None

    def finish(self, rode):
        pass


def _local_step(x, pos, target, e_g_in, w_in0, e_g_q, w_q, e_g_kv, w_kv, sinks, b_f, g_final, layer1):
    S = x.shape[0]
    w_in0p, w_qp, w_kvp = _prep_w_in0(w_in0), _prep_w_q(w_q), _prep_w_kv(w_kv)
    slopes = jnp.asarray(2.0 ** (-8.0 * (np.arange(N_SWA, dtype=np.float32) + 1.0) / N_SWA), jnp.float32)
    sinks1 = sinks.reshape(N_SWA)
    b_col = b_f.reshape(N_FOX, 1)

    (h0, cq, ckv, qm, km, vm, qs, kd, vd, gate0, cos, sin) = _layer0_in(
        x, pos, e_g_in, w_in0p, e_g_q, w_qp, e_g_kv, w_kvp)
    o_m, lse_m, rode = _attn_fwd_t(qm, km, vm, (NOPE + ROPE) ** -0.5, split=True, name="mla_fwd", plan=layer1.gather_plan())
    w_out0, o_g_in, w_in1, w_out1 = layer1.layer1_weights(rode)
    w_in1p = _prep_w_in1(w_in1)
    o_s, lse_s = _swa_fwd(qs, kd, vd, sinks1, slopes)
    x1, h1, q1, k1, v1, gate1, f_slab = _layer0_out_layer1_in(x, o_m, o_s, gate0, w_out0, o_g_in, w_in1p)
    f_row = f_slab[:, :N_FOX].T
    lc_row = _forget_fwd(f_row, b_col)
    lcc = lc_row.T
    o1, lse1, _ = _attn_fwd_t(q1, k1, v1, HEAD ** -0.5, split=False, name="fox_fwd", lcc=lcc)
    loss8, dg_final, dw_out1, dx2, do1, dgate1 = _head(x1, o1, gate1, w_out1, g_final, target)

    dq1, dk1, dv1, dlc, _ = _attn_bwd_t(q1, k1, v1, do1, o1, lse1, HEAD ** -0.5, split=False, name="fox_bwd", lcc=lcc)
    df_row, db_f = _forget_bwd(dlc.reshape(N_FOX, S), f_row, b_col)
    df_slab = jnp.pad(df_row.T, ((0, 0), (0, LANES - N_FOX))).astype(MXU)
    dz1, dx1, dg_o_in, dw_out0, do_m, do_s, dgate0 = _layer1_in_bwd(
        dq1, dk1, dv1, dgate1, df_slab, x1, dx2, o_g_in, w_in1p, gate0, o_m, o_s, w_out0)
    grads1 = dict(o_g_in=dg_o_in, o_w_in=_fold_w_in1(_wgrad(dz1, h1, "wgrad_in1")), o_w_out=dw_out1, e_w_out=dw_out0)
    dqs, dkd, dvd, dsink, rode = _swa_bwd(qs, kd, vd, do_s, o_s, lse_s, sinks1, slopes, plan=layer1.swap_plan(grads1))
    dqm, dkm, dvm, rode = _attn_bwd_t(qm, km, vm, do_m, o_m, lse_m, (NOPE + ROPE) ** -0.5, split=True, name="mla_bwd",
                                      plan=layer1.exchange_plan(rode))
    layer1.finish(rode)
    dx, dz0, dg_in, dg_q, dg_kv, dw_q, dw_kv = _layer0_in_bwd(
        dqm, dkm, dvm, dqs, dkd, dvd, dgate0, cos, sin, cq, ckv, x, dx1, e_g_in, w_in0p, e_g_q, w_qp, e_g_kv, w_kvp)

    grads = dict(
        e_g_in=dg_in,
        e_w_in=_fold_w_in0(_wgrad(dz0, h0, "wgrad_in0")),
        e_g_q_a=dg_q,
        e_w_q_up=_fold_w_q(dw_q),
        e_g_kv_a=dg_kv,
        e_w_kv_up=_fold_w_kv(dw_kv),
        e_sinks=dsink[:, 0:2, 0].reshape(1, N_SWA),
        o_b_f=db_f.reshape(1, N_FOX),
        g_final=dg_final,
        **grads1,
    )
    return loss8[0, 0], dx, grads


SHARDED = ("e_w_in", "e_w_q_up", "e_w_kv_up", "e_w_out", "o_g_in", "o_w_in", "o_w_out")
TRANSPOSED = ("e_w_in", "e_w_q_up", "o_w_in")
COL_SHARDED = ("e_w_kv_up", "o_g_in")
REPLICATED = ("e_g_in", "e_g_q_a", "e_g_kv_a", "e_sinks", "o_b_f", "g_final")
FULL_SHAPES = dict(e_w_in=(2208, 1024), e_w_q_up=(768, 256), e_w_kv_up=(128, 1024), e_w_out=(1024, 1024),
                   o_g_in=(1, 1024), o_w_in=(4112, 1024), o_w_out=(1024, 1024))
GROUPS = dict(
    layer0=dict(rows=768, windows=dict(e_w_in=(0, 0), e_w_q_up=(560, 0), e_w_kv_up=(560, 256))),
    layer1=dict(rows=1568, windows=dict(o_w_in=(0, 0), o_w_out=(1040, 0), e_w_out=(1296, 0), o_g_in=(1552, 0))),
)


def _shard_shape(name):
    r, c = FULL_SHAPES[name]
    return (r, c // 4) if name in COL_SHARDED else (r // 4, c)


def _as_handled(name, a):
    a = a[0] if a.ndim == 3 else a
    return a.T if name in TRANSPOSED else a


def _as_given(name, a, shape):
    return (a.T if name in TRANSPOSED else a).reshape(shape)


def _pack_block(p, group):
    def rows(a, n):
        return jnp.pad(a, ((0, n - a.shape[0]), (0, 0)))

    if group == "layer0":
        band = jnp.concatenate([p["e_w_q_up"], rows(p["e_w_kv_up"], 192), jnp.zeros((192, 512), p["e_w_in"].dtype)], axis=1)
        return jnp.concatenate([rows(p["e_w_in"], 560), rows(band, 208)], axis=0)
    g = p["o_g_in"]
    band = jnp.pad(g, ((0, 16 - g.shape[0]), (0, PACK_COLS - g.shape[1])))
    return jnp.concatenate([rows(p["o_w_in"], 1040), p["o_w_out"], p["e_w_out"], band], axis=0)


def _window(block, group, name, width=None):
    r0, c0 = GROUPS[group]["windows"][name]
    r, c = _shard_shape(name)
    return block[..., r0:r0 + r, c0:c0 + (c if width is None else width)]


def _chip_slice(name, full, k):
    r, c = _shard_shape(name)
    if isinstance(full, _RowSeq):
        return jnp.concatenate(full.rows(r * k, r * (k + 1)), axis=0)
    return full[:, c * k:c * (k + 1)] if name in COL_SHARDED else full[r * k:r * (k + 1), :]


def _packed_weights(w, group):
    parts = {}
    for n in GROUPS[group]["windows"]:
        a = _as_handled(n, w[n])
        parts[n] = lax.bitcast_convert_type(a, jnp.bfloat16).reshape(1, -1) if n == "o_g_in" else a.astype(jnp.bfloat16)
    halves = _pack_block(parts, group).reshape(2, GROUPS[group]["rows"] // 2, PACK_COLS)
    return lax.dynamic_index_in_dim(halves, lax.axis_index("c"), 0, keepdims=False)


def _unpacked_weights(gathered, half, group):
    blocks = _fill_own_slot(gathered, half).reshape(4, GROUPS[group]["rows"], PACK_COLS)
    full = {}
    for n in GROUPS[group]["windows"]:
        if n == "o_g_in":
            halves = _window(blocks, group, n, width=512).reshape(4, 1, 256, 2)
            full[n] = jnp.concatenate(list(lax.bitcast_convert_type(halves, jnp.float32)), axis=1)
        else:
            pieces = [_window(blocks[k], group, n).astype(MXU) for k in range(4)]
            if n in ("e_w_in", "o_w_in"):
                full[n] = _RowSeq(pieces)
            else:
                full[n] = jnp.concatenate(pieces, axis=1 if n in COL_SHARDED else 0)
    return full


class _GroupReduce:
    def __init__(self, group):
        self.group = group
        self.c = lax.axis_index("c")
        self.chip = 2 * lax.axis_index("x") + lax.axis_index("y")

    def swap_plan(self, grads):
        names = GROUPS[self.group]["windows"]
        per_chip = jnp.stack([_pack_block({n: _chip_slice(n, grads[n], k) for n in names}, self.group) for k in range(4)])
        self.g4 = per_chip.reshape(4, 2, GROUPS[self.group]["rows"] // 2, PACK_COLS)
        return _pair_swap_plan(self.g4)

    def exchange_plan(self, rode):
        theirs = rode[0]
        rows = self.g4.shape[2]
        self.own = (lax.dynamic_slice(self.g4, (self.chip, self.c, 0, 0), (1, 1, rows, PACK_COLS)).reshape(rows, PACK_COLS),
                    lax.dynamic_index_in_dim(theirs, self.chip, 0, keepdims=False))
        return _chip_exchange_plan(_add_halves(self.g4, self.c, theirs, "pair_add_" + self.group, jnp.bfloat16))

    def finish(self, rode):
        my_half = _total_sum(*self.own, rode[0], "chip_sum_" + self.group)
        other_half = _run_plan(_pair_exchange_plan(my_half), "pair_exchange_" + self.group)[0]
        total = jnp.concatenate([jnp.where(self.c == 0, my_half, other_half), jnp.where(self.c == 0, other_half, my_half)], axis=0)
        self.sums = {n: _window(total, self.group, n) for n in GROUPS[self.group]["windows"]}

    def run(self, grads, beside):
        swap = self.swap_plan(grads)
        outs = _run_plan(_both_plans(swap, beside), "pair_swap_" + self.group)
        rode, others = outs[:len(swap.out_shape)], outs[len(swap.out_shape):]
        self.finish(_run_plan(self.exchange_plan(rode), "chip_exchange_" + self.group))
        return self.sums, others


class _Layer1Exchange(_GroupReduce):
    def __init__(self, w):
        super().__init__("layer1")
        self.half = _packed_weights(w, "layer1")

    def gather_plan(self):
        return _gather8_plan(self.half)

    def layer1_weights(self, rode):
        full = _unpacked_weights(rode[0], self.half, "layer1")
        return full["e_w_out"], full["o_g_in"], full["o_w_in"], full["o_w_out"]


def kernel(x, positions, e_g_in, e_w_in, e_g_q_a, e_w_q_up, e_g_kv_a, e_w_kv_up, e_sinks, e_w_out, o_g_in, o_w_in, o_b_f, o_w_out, g_final, loss_target, m_e_g_in, m_e_w_in, m_e_g_q_a, m_e_w_q_up, m_e_g_kv_a, m_e_w_kv_up, m_e_sinks, m_e_w_out, m_o_g_in, m_o_w_in, m_o_b_f, m_o_w_out, m_g_final, v_e_g_in, v_e_w_in, v_e_g_q_a, v_e_w_q_up, v_e_g_kv_a, v_e_w_kv_up, v_e_sinks, v_e_w_out, v_o_g_in, v_o_w_in, v_o_b_f, v_o_w_out, v_g_final):
    w = dict(e_g_in=e_g_in, e_w_in=e_w_in, e_g_q_a=e_g_q_a, e_w_q_up=e_w_q_up, e_g_kv_a=e_g_kv_a, e_w_kv_up=e_w_kv_up,
             e_sinks=e_sinks, e_w_out=e_w_out, o_g_in=o_g_in, o_w_in=o_w_in, o_b_f=o_b_f, o_w_out=o_w_out, g_final=g_final)
    m = dict(e_g_in=m_e_g_in, e_w_in=m_e_w_in, e_g_q_a=m_e_g_q_a, e_w_q_up=m_e_w_q_up, e_g_kv_a=m_e_g_kv_a,
             e_w_kv_up=m_e_w_kv_up, e_sinks=m_e_sinks, e_w_out=m_e_w_out, o_g_in=m_o_g_in, o_w_in=m_o_w_in, o_b_f=m_o_b_f,
             o_w_out=m_o_w_out, g_final=m_g_final)
    v = dict(e_g_in=v_e_g_in, e_w_in=v_e_w_in, e_g_q_a=v_e_g_q_a, e_w_q_up=v_e_w_q_up, e_g_kv_a=v_e_g_kv_a,
             e_w_kv_up=v_e_w_kv_up, e_sinks=v_e_sinks, e_w_out=v_e_w_out, o_g_in=v_o_g_in, o_w_in=v_o_w_in, o_b_f=v_o_b_f,
             o_w_out=v_o_w_out, g_final=v_g_final)
    order = ("e_g_in", "e_w_in", "e_g_q_a", "e_w_q_up", "e_g_kv_a", "e_w_kv_up", "e_sinks", "e_w_out", "o_g_in", "o_w_in",
             "o_b_f", "o_w_out", "g_final")
    half0 = _packed_weights(w, "layer0")
    full = _unpacked_weights(_run_plan(_gather8_plan(half0), "gather_weights_layer0")[0], half0, "layer0")
    layer1 = _Layer1Exchange(w)

    loss_part, dx, grads = _local_step(
        x[0], positions.reshape(-1, 1), loss_target[0], e_g_in, full["e_w_in"], e_g_q_a, full["e_w_q_up"], e_g_kv_a,
        full["e_w_kv_up"], e_sinks, o_b_f, g_final.reshape(1, D), layer1)

    small = jnp.concatenate([jnp.pad(loss_part.reshape(1), (0, LANES - 1))]
                            + [jnp.pad(grads[n].reshape(-1), (0, (-grads[n].size) % LANES)) for n in REPLICATED])
    rows = small.shape[0] // LANES
    small = jnp.pad(small.reshape(rows, LANES), ((0, (-rows) % 8), (0, 0)))
    sums0, (gathered_small,) = _GroupReduce("layer0").run(grads, _gather8_plan(small))
    gsum = {**layer1.sums, **sums0}
    ssum = _sum_leading(_fill_own_slot(gathered_small, small), "small_grad_sum").reshape(-1)
    loss = ssum[0]
    off = LANES
    for n in REPLICATED:
        cnt = w[n].size
        gsum[n] = ssum[off:off + cnt].reshape(w[n].shape)
        off += cnt + (-cnt) % LANES

    grad, delta, new_m, new_v = {}, {}, {}, {}
    for n in order:
        if n in SHARDED:
            outs = _adamw(_as_handled(n, w[n]), gsum[n], _as_handled(n, m[n]), _as_handled(n, v[n]), "adamw_" + n)
            grad[n], delta[n], new_m[n], new_v[n] = (_as_given(n, a, w[n].shape) for a in (gsum[n],) + outs)
        else:
            grad[n] = gsum[n]
            delta[n], new_m[n], new_v[n] = _adamw(w[n], gsum[n], m[n], v[n], "adamw_" + n)
    return (loss, dx[None], *[grad[n] for n in order], *[delta[n] for n in order], *[new_m[n] for n in order],
            *[new_v[n] for n in order])
```

```python
import numpy as np
import jax
import jax.numpy as jnp
from jax import lax
from jax.experimental import pallas as pl
from jax.experimental.pallas import tpu as pltpu

D = 1024
EPS = 1e-6
ROPE_THETA = 10000.0
N_MLA = 8
Q_RANK = 256
KV_RANK = 128
NOPE = 64
ROPE = 32
N_SWA = 8
WINDOW = 128
N_FOX = 16
HEAD = 64
LR, B1, B2, AEPS, WD, STEP = 0.001, 0.9, 0.999, 1e-08, 0.01, 10

LANES = 128
HALF = 64
VMEM_LIMIT = 56 * 1024 * 1024
MXU = jnp.bfloat16
TOK = 256
WG_TOK = 2048
WG_ROWS = 1536
ATT = 256
FWD_CHUNK = 2
BWD_CHUNK = 2
SWA_GROUP = 4
NEG = float("-inf")

PACK_COLS = 1024
SUM_ROWS = 256
ADAM_TILE_BYTES = 2 << 20
MESH_ID = pl.DeviceIdType.MESH


def _pcall(body, *, name, vmem=VMEM_LIMIT, semantics=None, **kw):
    params = dict(vmem_limit_bytes=vmem)
    if semantics is not None:
        params["dimension_semantics"] = semantics
    return pl.pallas_call(body, name=name, compiler_params=pltpu.CompilerParams(**params), **kw)


def _mm(a, b):
    return jnp.dot(a.astype(MXU), b.astype(MXU), preferred_element_type=jnp.float32)


def _mm_nt(a, b):
    return lax.dot_general(a.astype(MXU), b.astype(MXU), (((1,), (1,)), ((), ())),
                           preferred_element_type=jnp.float32)


def _mm_tn(a, b):
    return lax.dot_general(a.astype(MXU), b.astype(MXU), (((0,), (0,)), ((), ())),
                           preferred_element_type=jnp.float32)


def _full(shape):
    n = len(shape)
    return pl.BlockSpec(shape, lambda *_: (0,) * n)


def _rows(tm, n):
    return pl.BlockSpec((tm, n), lambda i: (i, 0))


def _sds(shape, dtype):
    return jax.ShapeDtypeStruct(shape, dtype)


def _rms(x, g):
    r = lax.rsqrt(jnp.mean(x * x, axis=-1, keepdims=True) + EPS)
    return x * r * g


def _rms_bwd(x, g, dy):
    r = lax.rsqrt(jnp.mean(x * x, axis=-1, keepdims=True) + EPS)
    xh = x * r
    dxh = dy * g
    dx = r * (dxh - xh * jnp.mean(dxh * xh, axis=-1, keepdims=True))
    return dx, dy * xh


def _sigmoid(x):
    return 1.0 / (1.0 + jnp.exp(-x))


def _lane_masks():
    lane = lax.broadcasted_iota(jnp.int32, (1, LANES), 1)
    return lane < HALF


def _split_heads(a, lo):
    z = jnp.zeros_like(a)
    return [jnp.where(lo, a, z), jnp.where(lo, z, a)]


def _rope_consts():
    inv = np.zeros((8, LANES), np.float32)
    j = np.arange(ROPE // 2, dtype=np.float32)
    f = (1.0 / (ROPE_THETA ** (np.arange(0, ROPE, 2, dtype=np.float32) / ROPE))).astype(np.float32)
    inv[0, HALF:HALF + 16] = f
    inv[0, HALF + 16:HALF + 32] = f
    inv[1, HALF:HALF + 16] = -1.0
    inv[1, HALF + 16:HALF + 32] = 1.0
    del j
    return jnp.asarray(inv)


def _rope_tables(pos_f, consts):
    ang = pos_f * consts[0:1, :]
    sign = consts[1:2, :]
    c = jnp.where(sign != 0.0, jnp.cos(ang), 1.0)
    s = jnp.sin(ang) * sign
    return c, s


def _swap_halves(v, sign):
    lo = pltpu.roll(v, LANES - 16, axis=1)
    hi = pltpu.roll(v, 16, axis=1)
    return jnp.where(sign < 0.0, lo, jnp.where(sign > 0.0, hi, 0.0))


def _rope(x, c, s, sign):
    return x * c + _swap_halves(x, sign) * s


def _rope_t(dy, c, s, sign):
    return dy * c + _swap_halves(dy * s, sign)


def _layer0_in(x, pos, g_in, w_in, g_q, w_q, g_kv, w_kv):
    S = x.shape[0]
    consts = _rope_consts()

    def body(x_ref, pos_ref, c_ref, g_ref, w_ref, gq_ref, wq_ref, gkv_ref, wkv_ref,
             h_ref, cq_ref, ckv_ref, qm_ref, km_ref, vm_ref,
             qs_ref, kd_ref, vd_ref, gate_ref, cos_ref, sin_ref):
        h = _rms(x_ref[...], g_ref[...])
        h_ref[...] = h.astype(h_ref.dtype)
        z = _mm_nt(h, w_ref[...])
        cq = z[:, 0:256]
        ckv = z[:, 256:384]
        kpe = z[:, 384:512]
        cq_ref[...] = cq
        ckv_ref[...] = ckv
        qs_ref[...] = z[:, 512:1024].astype(qs_ref.dtype)
        kd_ref[...] = z[:, 1024:1536].astype(kd_ref.dtype)
        vd_ref[...] = z[:, 1536:2048].astype(vd_ref.dtype)
        gate_ref[...] = z[:, 2048:3072]
        cqn = _rms(cq, gq_ref[...])
        ckvn = _rms(ckv, gkv_ref[...])
        q = _mm_nt(cqn, wq_ref[...])
        kv = _mm(ckvn, wkv_ref[...])
        vm_ref[...] = kv[:, 1024:1536].astype(vm_ref.dtype)
        consts_v = c_ref[...]
        sign = consts_v[1:2, :]
        c, s = _rope_tables(pos_ref[...].astype(jnp.float32), consts_v)
        cos_ref[...] = c
        sin_ref[...] = s
        kpe_r = _rope(kpe, c, s, sign)
        for hd in range(N_MLA):
            sl = slice(LANES * hd, LANES * (hd + 1))
            qm_ref[:, sl] = _rope(q[:, sl], c, s, sign).astype(qm_ref.dtype)
            km_ref[:, sl] = (kv[:, sl] + kpe_r).astype(km_ref.dtype)

    outs = [
        ((S, D), MXU), ((S, 256), jnp.float32), ((S, 128), jnp.float32),
        ((S, 1024), MXU), ((S, 1024), MXU), ((S, 512), MXU), ((S, 512), MXU), ((S, 512), MXU), ((S, 512), MXU),
        ((S, 1024), jnp.float32), ((S, 128), jnp.float32), ((S, 128), jnp.float32),
    ]
    return _pcall(
        body, name="layer0_in", grid=(S // TOK,), semantics=("arbitrary",),
        in_specs=[_rows(TOK, D), _rows(TOK, 1), _full((8, LANES)), _full((1, D)), _full(w_in.shape), _full((1, 256)),
                  _full(w_q.shape), _full((1, 128)), _full(w_kv.shape)],
        out_specs=[_rows(TOK, s[1]) for s, _ in outs],
        out_shape=[_sds(s, d) for s, d in outs],
    )(x, pos, consts, g_in, w_in, g_q, w_q, g_kv, w_kv)


AUG = (HALF, 0)
ONE = (HALF + 8, 8)


def _data_lanes(idx, h):
    return (idx < HALF) if h == 0 else (idx >= HALF)


def _three_terms(x):
    hi = x.astype(MXU).astype(jnp.float32)
    mid = (x - hi).astype(MXU).astype(jnp.float32)
    lo = (x - hi - mid).astype(MXU).astype(jnp.float32)
    return hi, mid, lo


def _q_aug(qblk, lc, h, scale, lane):
    a = AUG[h]
    hi, mid, lo = _three_terms(lc)
    ones = ((lane >= a + 3) & (lane <= a + 5)).astype(jnp.float32)
    aug = jnp.where(lane == a, hi, jnp.where(lane == a + 1, mid, jnp.where(lane == a + 2, lo, ones)))
    return jnp.where(_data_lanes(lane, h), qblk * jnp.asarray(scale, qblk.dtype), aug.astype(qblk.dtype))


def _k_aug(kblk, lc, h, lane):
    a = AUG[h]
    hi, mid, lo = _three_terms(-lc)
    ones = ((lane >= a) & (lane <= a + 2)).astype(jnp.float32)
    aug = jnp.where(lane == a + 3, hi, jnp.where(lane == a + 4, mid, jnp.where(lane == a + 5, lo, ones)))
    return jnp.where(_data_lanes(lane, h), kblk, aug.astype(kblk.dtype))


def _lc_col(lc_ref, r0, rows, h):
    head = lax.broadcasted_iota(jnp.int32, (1, lc_ref.shape[1]), 1)
    return jnp.sum(jnp.where(head == 2 * pl.program_id(0) + h, lc_ref[pl.ds(r0, rows), :], 0.0), axis=1, keepdims=True)


def _attn_fwd_t(q, k, v, scale, *, split, name, lcc=None, plan=None):
    S = q.shape[0]
    npair = v.shape[1] // LANES
    W = 2 * LANES if split else LANES
    T = ATT
    CH = FWD_CHUNK * T
    assert S % CH == 0
    nq = S // T

    def body(*refs):
        if split:
            q_ref, k_ref, v_ref, o_ref, lse_ref, vt, acc, m_sc = refs
        else:
            q_ref, k_ref, v_ref, lcc_ref, o_ref, lse_ref, kaug, vt, acc, m_sc = refs
        lane = lax.broadcasted_iota(jnp.int32, (1, LANES), 1)
        sub = lax.broadcasted_iota(jnp.int32, (LANES, 1), 0)
        key_minus_qry = lax.broadcasted_iota(jnp.int32, (CH, T), 0) - lax.broadcasted_iota(jnp.int32, (CH, T), 1)

        def prep(i, c):
            r0 = pl.multiple_of(i * T, T)
            vblk = v_ref[pl.ds(r0, T), :].astype(jnp.float32)
            for h in (0, 1):
                vh = jnp.where(_data_lanes(lane, h), vblk, (lane == ONE[h]).astype(jnp.float32))
                vt[h, :, pl.ds(r0, T)] = vh.T.astype(vt.dtype)
                if not split:
                    kaug[h, pl.ds(r0, T), :] = _k_aug(k_ref[pl.ds(r0, T), :], _lc_col(lcc_ref, r0, T, h), h, lane)
            return c

        lax.fori_loop(0, nq, prep, 0)

        def queries(qi):
            q0 = pl.multiple_of(qi * T, T)
            qblk = q_ref[pl.ds(q0, T), :]
            if split:
                return (qblk[:, :LANES], qblk[:, LANES:])
            return tuple(_q_aug(qblk, _lc_col(lcc_ref, q0, T, h), h, scale, lane) for h in (0, 1))

        def scores(qs, c):
            k0 = pl.multiple_of(c * CH, CH)
            out = []
            for h in (0, 1):
                if split:
                    out.append(_mm_nt(k_ref[pl.ds(k0, CH), LANES * h:LANES * (h + 1)], qs[h]) * scale)
                else:
                    out.append(_mm_nt(kaug[h, pl.ds(k0, CH), :], qs[h]))
            return tuple(out)

        def q_block(qi, carry):
            qs, first_scores = carry[:2], carry[2:]
            q0 = pl.multiple_of(qi * T, T)
            acc[...] = jnp.zeros_like(acc)
            m_sc[...] = jnp.full(m_sc.shape, NEG, jnp.float32)

            def absorb(c, sts, masked):
                k0 = pl.multiple_of(c * CH, CH)
                for h in (0, 1):
                    st = sts[h]
                    if masked:
                        st = jnp.where(key_minus_qry <= q0 - k0, st, NEG)
                    m_old = m_sc[h:h + 1, :]
                    m_new = jnp.maximum(m_old, jnp.max(st, axis=0, keepdims=True))
                    alpha = jnp.exp(m_old - m_new)
                    pt = jnp.exp(st - m_new)
                    acc[h] = alpha * acc[h] + _mm(vt[h, :, pl.ds(k0, CH)], pt)
                    m_sc[h:h + 1, :] = m_new

            last = qi // FWD_CHUNK

            def pipelined(c, sts):
                nxt = scores(qs, c + 1)
                absorb(c, sts, False)
                return nxt

            sts = lax.fori_loop(0, last, pipelined, first_scores)
            qs_next = queries(jnp.minimum(qi + 1, nq - 1))
            nxt = qs_next + scores(qs_next, 0)
            absorb(last, sts, True)
            ot = None
            for h in (0, 1):
                a = acc[h]
                l = a[ONE[h]:ONE[h] + 1, :]
                oh = jnp.where(_data_lanes(sub, h), a * (1.0 / l), 0.0)
                ot = oh if ot is None else ot + oh
                lse_ref[0, h:h + 1, pl.ds(q0, T)] = m_sc[h:h + 1, :] + jnp.log(l)
            o_ref[pl.ds(q0, T), :] = ot.T
            return nxt

        qs0 = queries(0)
        lax.fori_loop(0, nq, q_block, qs0 + scores(qs0, 0))

    wide = pl.BlockSpec((S, W), lambda j: (0, j))
    slab = pl.BlockSpec((S, LANES), lambda j: (0, j))
    rows = pl.BlockSpec((1, 2, S), lambda j: (j, 0, 0))
    in_specs = [wide, wide, slab]
    args = [q, k, v]
    scratch = []
    if not split:
        in_specs.append(_full(lcc.shape))
        args.append(lcc)
        scratch.append(pltpu.VMEM((2, S, LANES), MXU))
    scratch += [pltpu.VMEM((2, LANES, S), MXU), pltpu.VMEM((2, LANES, T), jnp.float32), pltpu.VMEM((8, T), jnp.float32)]
    (o, lse), rode = _pcall_riding(
        body, plan, args, name=name, grid=(npair,), in_specs=in_specs, out_specs=[slab, rows],
        out_shape=[_sds((S, npair * LANES), jnp.float32), _sds((npair, 2, S), jnp.float32)], scratch_shapes=scratch)
    return o, lse, rode


def _attn_bwd_t(q, k, v, do, o, lse, scale, *, split, name, lcc=None, plan=None):
    S = q.shape[0]
    npair = v.shape[1] // LANES
    W = 2 * LANES if split else LANES
    T = ATT
    CH = BWD_CHUNK * T
    assert S % CH == 0
    nq = S // T

    def body(*refs):
        if split:
            (q_ref, k_ref, v_ref, do_ref, o_ref, lse_ref, dq_ref, dk_ref, dv_ref, dqt, delta, dk_acc, dv_acc) = refs
        else:
            (q_ref, k_ref, v_ref, do_ref, o_ref, lse_ref, lcc_ref, dq_ref, dk_ref, dv_ref, dlc_ref,
             dqt, delta, dk_acc, dv_acc, qaug, csum) = refs
        lane = lax.broadcasted_iota(jnp.int32, (1, LANES), 1)
        sub = lax.broadcasted_iota(jnp.int32, (LANES, 1), 0)
        key_minus_qry = lax.broadcasted_iota(jnp.int32, (T, CH), 0) - lax.broadcasted_iota(jnp.int32, (T, CH), 1)

        def prep(i, c):
            r0 = pl.multiple_of(i * T, T)
            prod_t = (do_ref[pl.ds(r0, T), :].astype(jnp.float32) * o_ref[pl.ds(r0, T), :]).T
            for h in (0, 1):
                delta[h:h + 1, pl.ds(r0, T)] = jnp.sum(jnp.where(_data_lanes(sub, h), prod_t, 0.0), axis=0, keepdims=True)
                dqt[h, :, pl.ds(r0, T)] = jnp.zeros((LANES, T), jnp.float32)
                if not split:
                    qaug[h, pl.ds(r0, T), :] = _q_aug(q_ref[pl.ds(r0, T), :], _lc_col(lcc_ref, r0, T, h), h, scale, lane)
            return c

        lax.fori_loop(0, nq, prep, 0)

        def keys(ki):
            k0 = pl.multiple_of(ki * T, T)
            kblk = k_ref[pl.ds(k0, T), :]
            if split:
                return (kblk[:, :LANES], kblk[:, LANES:])
            return tuple(_k_aug(kblk, _lc_col(lcc_ref, k0, T, h), h, lane) for h in (0, 1))

        def q_of(c, h):
            q0 = pl.multiple_of(c * CH, CH)
            if split:
                return q_ref[pl.ds(q0, CH), LANES * h:LANES * (h + 1)]
            return qaug[h, pl.ds(q0, CH), :]

        def scores(khs, c):
            out = []
            for h in (0, 1):
                st = _mm_nt(khs[h], q_of(c, h))
                out.append(st * scale if split else st)
            return tuple(out)

        def k_block(ki, carry):
            khs, first_scores = carry[:2], carry[2:]
            k0 = pl.multiple_of(ki * T, T)
            khts = [kh.astype(jnp.float32).T.astype(kh.dtype) for kh in khs]
            vhs = _split_heads(v_ref[pl.ds(k0, T), :], lane < HALF)
            dk_acc[...] = jnp.zeros_like(dk_acc)
            dv_acc[...] = jnp.zeros_like(dv_acc)

            def absorb(c, vals):
                q0 = pl.multiple_of(c * CH, CH)
                dos = _split_heads(do_ref[pl.ds(q0, CH), :], lane < HALF)
                visible = key_minus_qry <= q0 - k0
                for h in (0, 1):
                    dpt = _mm_nt(vhs[h], dos[h])
                    st = jnp.where(visible, vals[h], NEG)
                    pt = jnp.exp(st - lse_ref[0, h:h + 1, pl.ds(q0, CH)])
                    dv_acc[...] += _mm(pt, dos[h])
                    dst = pt * (dpt - delta[h:h + 1, pl.ds(q0, CH)])
                    dk_acc[h] += _mm(dst, q_of(c, h))
                    dqt[h, :, pl.ds(q0, CH)] += _mm(khts[h], dst)

            first = ki // BWD_CHUNK

            def pipelined(c, vals):
                nxt = scores(khs, c + 1)
                absorb(c, vals)
                return nxt

            vals = lax.fori_loop(first, S // CH - 1, pipelined, first_scores)
            kn = jnp.minimum(ki + 1, nq - 1)
            khs_next = keys(kn)
            nxt = khs_next + scores(khs_next, kn // BWD_CHUNK)
            absorb(S // CH - 1, vals)
            if split:
                dk_ref[pl.ds(k0, T), :LANES] = (dk_acc[0] * scale).astype(dk_ref.dtype)
                dk_ref[pl.ds(k0, T), LANES:] = (dk_acc[1] * scale).astype(dk_ref.dtype)
            else:
                dk_ref[pl.ds(k0, T), :] = jnp.where(lane < HALF, dk_acc[0], dk_acc[1]).astype(dk_ref.dtype)
                for h in (0, 1):
                    csum[h:h + 1, pl.ds(k0, T)] = dk_acc[h].T[AUG[h] + 3:AUG[h] + 4, :]
            dv_ref[pl.ds(k0, T), :] = dv_acc[...].astype(dv_ref.dtype)
            return nxt

        khs0 = keys(0)
        lax.fori_loop(0, nq, k_block, khs0 + scores(khs0, 0))

        def finish(i, c):
            r0 = pl.multiple_of(i * T, T)
            if split:
                for h in (0, 1):
                    dq_ref[pl.ds(r0, T), LANES * h:LANES * (h + 1)] = (dqt[h, :, pl.ds(r0, T)].T * scale).astype(dq_ref.dtype)
            else:
                d = jnp.where(sub < HALF, dqt[0, :, pl.ds(r0, T)], dqt[1, :, pl.ds(r0, T)])
                dq_ref[pl.ds(r0, T), :] = (d.T * scale).astype(dq_ref.dtype)
                for h in (0, 1):
                    dlc_ref[0, h:h + 1, pl.ds(r0, T)] = dqt[h, AUG[h]:AUG[h] + 1, pl.ds(r0, T)] - csum[h:h + 1, pl.ds(r0, T)]
            return c

        lax.fori_loop(0, nq, finish, 0)

    wide = pl.BlockSpec((S, W), lambda j: (0, j))
    slab = pl.BlockSpec((S, LANES), lambda j: (0, j))
    rows = pl.BlockSpec((1, 2, S), lambda j: (j, 0, 0))
    in_specs = [wide, wide, slab, slab, slab, rows]
    args = [q, k, v, do, o, lse]
    out_specs = [wide, wide, slab]
    out_shape = [_sds(q.shape, jnp.float32 if split else do.dtype), _sds(k.shape, jnp.float32 if split else do.dtype),
                 _sds(v.shape, do.dtype)]
    scratch = [pltpu.VMEM((2, LANES, S), jnp.float32), pltpu.VMEM((8, S), jnp.float32),
               pltpu.VMEM((2, T, LANES), jnp.float32), pltpu.VMEM((T, LANES), jnp.float32)]
    if not split:
        in_specs.append(_full(lcc.shape))
        args.append(lcc)
        out_specs.append(rows)
        out_shape.append(_sds((npair, 2, S), jnp.float32))
        scratch += [pltpu.VMEM((2, S, LANES), MXU), pltpu.VMEM((8, S), jnp.float32)]
    outs, rode = _pcall_riding(body, plan, args, name=name, grid=(npair,), in_specs=in_specs, out_specs=out_specs,
                               out_shape=out_shape, scratch_shapes=scratch)
    return (*outs, rode)


def _swa_bias(slope, shift):
    a = lax.broadcasted_iota(jnp.int32, (WINDOW, 2 * WINDOW), 0)
    c = lax.broadcasted_iota(jnp.int32, (WINDOW, 2 * WINDOW), 1)
    dist = a - c + shift
    return jnp.where((dist >= 0) & (dist < WINDOW), -slope * dist.astype(jnp.float32), NEG)


def _swa_scores(qh, kblk, bias):
    return _mm_nt(qh, kblk) * (HEAD ** -0.5) + bias


def _swa_fwd(q, kd, vd, sinks, slopes):
    S = q.shape[0]
    npair = q.shape[1] // LANES
    nb = S // WINDOW

    def body(sink_ref, slope_ref, q_ref, k_ref, v_ref, o_ref, lse_ref):
        j = pl.program_id(0)
        lo = _lane_masks()
        first = lax.broadcasted_iota(jnp.int32, (2 * WINDOW, 1), 0) < WINDOW
        sink = jnp.where(first, sink_ref[2 * j], sink_ref[2 * j + 1])
        biases = [jnp.concatenate([_swa_bias(slope_ref[2 * j + h], shift) for h in (0, 1)], axis=0) for shift in (0, WINDOW)]

        def q_block(qi, c):
            q0 = pl.multiple_of(qi * WINDOW, WINDOW)
            k0 = pl.multiple_of(jnp.maximum(qi - 1, 0) * WINDOW, WINDOW)
            q2 = jnp.concatenate(_split_heads(q_ref[pl.ds(q0, WINDOW), :], lo), axis=0)
            s = _swa_scores(q2, k_ref[pl.ds(k0, 2 * WINDOW), :], jnp.where(qi == 0, *biases))
            m = jnp.maximum(jnp.max(s, axis=1, keepdims=True), sink)
            p = jnp.exp(s - m)
            den = jnp.sum(p, axis=1, keepdims=True) + jnp.exp(sink - m)
            o2 = _mm(p / den, v_ref[pl.ds(k0, 2 * WINDOW), :])
            o_ref[pl.ds(q0, WINDOW), :] = jnp.where(lo, o2[:WINDOW], o2[WINDOW:])
            lse = m + jnp.log(den)
            lse_ref[0, pl.ds(q0, WINDOW), :] = lse[:WINDOW]
            lse_ref[1, pl.ds(q0, WINDOW), :] = lse[WINDOW:]
            return c

        def q_group(gi, c):
            for g in range(SWA_GROUP):
                q_block(gi * SWA_GROUP + g, c)
            return c

        lax.fori_loop(0, nb // SWA_GROUP, q_group, 0)

    smem = pl.BlockSpec(memory_space=pltpu.SMEM)
    slab = pl.BlockSpec((S, LANES), lambda j: (0, j))
    return _pcall(
        body, name="swa_fwd", grid=(npair,), semantics=("arbitrary",),
        in_specs=[smem, smem, slab, slab, slab],
        out_specs=[slab, pl.BlockSpec((2, S, 1), lambda j: (j, 0, 0))],
        out_shape=[_sds((S, npair * LANES), jnp.float32), _sds((2 * npair, S, 1), jnp.float32)],
    )(sinks, slopes, q, kd, vd)


def _swa_bwd(q, kd, vd, do, o, lse, sinks, slopes, plan=None):
    S = q.shape[0]
    npair = q.shape[1] // LANES
    nb = S // WINDOW

    def body(sink_ref, slope_ref, q_ref, k_ref, v_ref, do_ref, o_ref, lse_ref,
             dq_ref, dk_ref, dv_ref, dsink_ref, dk_acc, dv_acc):
        j = pl.program_id(0)
        lo = _lane_masks()
        dk_acc[...] = jnp.zeros_like(dk_acc)
        dv_acc[...] = jnp.zeros_like(dv_acc)
        first = lax.broadcasted_iota(jnp.int32, (2 * WINDOW, 1), 0) < WINDOW
        sink = jnp.where(first, sink_ref[2 * j], sink_ref[2 * j + 1])
        biases = [jnp.concatenate([_swa_bias(slope_ref[2 * j + h], shift) for h in (0, 1)], axis=0) for shift in (0, WINDOW)]

        def q_block(qi, carry):
            q0 = pl.multiple_of(qi * WINDOW, WINDOW)
            k0 = pl.multiple_of(jnp.maximum(qi - 1, 0) * WINDOW, WINDOW)
            q2 = jnp.concatenate(_split_heads(q_ref[pl.ds(q0, WINDOW), :], lo), axis=0)
            do2 = jnp.concatenate(_split_heads(do_ref[pl.ds(q0, WINDOW), :], lo), axis=0)
            oblk = o_ref[pl.ds(q0, WINDOW), :]
            kblk = k_ref[pl.ds(k0, 2 * WINDOW), :]
            vblk = v_ref[pl.ds(k0, 2 * WINDOW), :]
            lse = jnp.concatenate([lse_ref[0, pl.ds(q0, WINDOW), :], lse_ref[1, pl.ds(q0, WINDOW), :]], axis=0)
            s = _swa_scores(q2, kblk, jnp.where(qi == 0, *biases))
            p = jnp.exp(s - lse)
            delta = jnp.sum(do2.astype(jnp.float32) * jnp.concatenate([oblk, oblk], axis=0), axis=1, keepdims=True)
            dv_acc[pl.ds(k0, 2 * WINDOW), :] += _mm_tn(p, do2)
            ds = p * (_mm_nt(do2, vblk) - delta)
            dq2 = _mm(ds, kblk) * (HEAD ** -0.5)
            dq_ref[pl.ds(q0, WINDOW), :] = jnp.where(lo, dq2[:WINDOW], dq2[WINDOW:]).astype(dq_ref.dtype)
            dk_acc[pl.ds(k0, 2 * WINDOW), :] += _mm_tn(ds, q2) * (HEAD ** -0.5)
            dsk = -jnp.exp(sink - lse) * delta
            return (carry[0] + jnp.sum(dsk[:WINDOW], axis=0, keepdims=True),
                    carry[1] + jnp.sum(dsk[WINDOW:], axis=0, keepdims=True))

        def q_group(gi, carry):
            for g in range(SWA_GROUP):
                carry = q_block(gi * SWA_GROUP + g, carry)
            return carry

        zero = jnp.zeros((1, 1), jnp.float32)
        dsa, dsb = lax.fori_loop(0, nb // SWA_GROUP, q_group, (zero, zero))
        dk_ref[...] = dk_acc[...].astype(dk_ref.dtype)
        dv_ref[...] = dv_acc[...].astype(dv_ref.dtype)
        r = lax.broadcasted_iota(jnp.int32, (8, LANES), 0)
        dsink_ref[0] = jnp.where(r == 0, dsa, jnp.where(r == 1, dsb, 0.0))

    smem = pl.BlockSpec(memory_space=pltpu.SMEM)
    slab = pl.BlockSpec((S, LANES), lambda j: (0, j))
    outs, rode = _pcall_riding(
        body, plan, [sinks, slopes, q, kd, vd, do, o, lse], name="swa_bwd", grid=(npair,),
        in_specs=[smem, smem, slab, slab, slab, slab, slab, pl.BlockSpec((2, S, 1), lambda j: (j, 0, 0))],
        out_specs=[slab, slab, slab, pl.BlockSpec((1, 8, LANES), lambda j: (j, 0, 0))],
        out_shape=[_sds(q.shape, do.dtype), _sds(kd.shape, do.dtype), _sds(vd.shape, do.dtype),
                   _sds((npair, 8, LANES), jnp.float32)],
        scratch_shapes=[pltpu.VMEM((S, LANES), jnp.float32), pltpu.VMEM((S, LANES), jnp.float32)])
    return (*outs, rode)


def _log_steps(S):
    k, out = 1, []
    while k < S:
        out.append(k)
        k *= 2
    return out


def _forget_fwd(f_row, b_col):
    S = f_row.shape[1]

    def body(f_ref, b_ref, lc_ref):
        x = f_ref[...] + b_ref[...]
        lc = jnp.minimum(x, 0.0) - jnp.log(1.0 + jnp.exp(-jnp.abs(x)))
        idx = lax.broadcasted_iota(jnp.int32, lc.shape, 1)
        for k in _log_steps(S):
            lc = lc + jnp.where(idx >= k, pltpu.roll(lc, k, axis=1), 0.0)
        lc_ref[...] = lc

    return _pcall(body, name="forget_fwd", out_shape=_sds(f_row.shape, jnp.float32))(f_row, b_col)


def _forget_bwd(dlc_row, f_row, b_col):
    S = f_row.shape[1]

    def body(d_ref, f_ref, b_ref, df_ref, db_ref):
        g = d_ref[...]
        idx = lax.broadcasted_iota(jnp.int32, g.shape, 1)
        for k in _log_steps(S):
            g = g + jnp.where(idx < S - k, pltpu.roll(g, S - k, axis=1), 0.0)
        x = f_ref[...] + b_ref[...]
        df = g * _sigmoid(-x)
        df_ref[...] = df
        db_ref[...] = jnp.sum(df, axis=1, keepdims=True)

    return _pcall(body, name="forget_bwd",
                  out_shape=[_sds(f_row.shape, jnp.float32), _sds((f_row.shape[0], 1), jnp.float32)])(dlc_row, f_row, b_col)


def _layer0_out_layer1_in(x, o_m, o_s, gate, w_out, g1, w_in1):
    S = x.shape[0]

    def body(x_ref, om_ref, os_ref, gate_ref, wo_ref, g_ref, w_ref,
             x1_ref, h_ref, q_ref, k_ref, v_ref, g1_ref, f_ref):
        gt = gate_ref[...]
        sg = gt * _sigmoid(gt)
        um = om_ref[...] * sg[:, :512]
        us = os_ref[...] * sg[:, 512:]
        x1 = x_ref[...] + _mm(um, wo_ref[0:512, :]) + _mm(us, wo_ref[512:1024, :])
        x1_ref[...] = x1
        h = _rms(x1, g_ref[...])
        h_ref[...] = h.astype(h_ref.dtype)
        z = _mm_nt(h, w_ref[...])
        q_ref[...] = z[:, 0:1024].astype(q_ref.dtype)
        k_ref[...] = z[:, 1024:2048].astype(k_ref.dtype)
        v_ref[...] = z[:, 2048:3072].astype(v_ref.dtype)
        g1_ref[...] = z[:, 3072:4096]
        f_ref[...] = z[:, 4096:4224]

    outs = [((S, D), jnp.float32), ((S, D), MXU), ((S, D), MXU), ((S, D), MXU), ((S, D), MXU),
            ((S, D), jnp.float32), ((S, LANES), jnp.float32)]
    return _pcall(
        body, name="layer0_out_layer1_in", grid=(S // TOK,), semantics=("arbitrary",),
        in_specs=[_rows(TOK, D), _rows(TOK, 512), _rows(TOK, 512), _rows(TOK, D), _full((D, D)), _full((1, D)),
                  _full(w_in1.shape)],
        out_specs=[_rows(TOK, s[1]) for s, _ in outs],
        out_shape=[_sds(s, d) for s, d in outs],
    )(x, o_m, o_s, gate, w_out, g1, w_in1)


def _head(x1, o1, gate1, w_out1, g_f, target):
    S = x1.shape[0]

    def body(x1_ref, o_ref, gate_ref, wo_ref, g_ref, t_ref,
             loss_ref, dgf_ref, dwo_ref, dx2_ref, do_ref, dgate_ref):
        i = pl.program_id(0)
        gt = gate_ref[...]
        sig = _sigmoid(gt)
        sg = gt * sig
        o = o_ref[...]
        u = o * sg
        x2 = x1_ref[...] + _mm(u, wo_ref[...])
        g = g_ref[...]
        y = _rms(x2, g)
        err = y - t_ref[...]
        part = 0.5 * jnp.sum(jnp.mean(err * err, axis=-1, keepdims=True), axis=0, keepdims=True)
        dy = err * (1.0 / D)
        dx2, dg_rows = _rms_bwd(x2, g, dy)
        dx2_ref[...] = dx2
        du = _mm_nt(dx2, wo_ref[...])
        do_ref[...] = (du * sg).astype(do_ref.dtype)
        dgate_ref[...] = (du * o * (sig * (1.0 + gt * (1.0 - sig)))).astype(dgate_ref.dtype)

        @pl.when(i == 0)
        def _():
            loss_ref[...] = jnp.zeros_like(loss_ref)
            dgf_ref[...] = jnp.zeros_like(dgf_ref)
            dwo_ref[...] = jnp.zeros_like(dwo_ref)

        loss_ref[...] += jnp.broadcast_to(part, loss_ref.shape)
        dgf_ref[...] += jnp.sum(dg_rows, axis=0, keepdims=True)
        dwo_ref[...] += _mm_tn(u, dx2)

    outs = [((S, D), jnp.float32), ((S, D), MXU), ((S, D), MXU)]
    return _pcall(
        body, name="head", grid=(S // TOK,), semantics=("arbitrary",),
        in_specs=[_rows(TOK, D), _rows(TOK, D), _rows(TOK, D), _full((D, D)), _full((1, D)), _rows(TOK, D)],
        out_specs=[_full((8, LANES)), _full((1, D)), _full((D, D))] + [_rows(TOK, D) for _ in outs],
        out_shape=[_sds((8, LANES), jnp.float32), _sds((1, D), jnp.float32), _sds((D, D), jnp.float32)]
        + [_sds(s, d) for s, d in outs],
    )(x1, o1, gate1, w_out1, g_f, target)


def _layer1_in_bwd(dq, dk, dv, dgate1, df, x1, dx2, g1, w_in1, gate0, o_m, o_s, w_out0):
    S = x1.shape[0]

    def body(dq_ref, dk_ref, dv_ref, dg1_ref, df_ref, x1_ref, dx2_ref, g_ref, w_ref, gate_ref, om_ref, os_ref,
             wo_ref, dz_ref, dx1_ref, dgn_ref, dwo_ref, dom_ref, dos_ref, dgate_ref):
        i = pl.program_id(0)
        dz_ref[:, 0:1024] = dq_ref[...]
        dz_ref[:, 1024:2048] = dk_ref[...]
        dz_ref[:, 2048:3072] = dv_ref[...]
        dz_ref[:, 3072:4096] = dg1_ref[...]
        dz_ref[:, 4096:4224] = df_ref[...]
        dh = _mm(dz_ref[...], w_ref[...])
        g = g_ref[...]
        dxn, dg_rows = _rms_bwd(x1_ref[...], g, dh)
        dx1 = dx2_ref[...] + dxn
        dx1_ref[...] = dx1
        du = _mm_nt(dx1, wo_ref[...])
        gt = gate_ref[...]
        sig = _sigmoid(gt)
        sg = gt * sig
        dsg = sig * (1.0 + gt * (1.0 - sig))
        dom_ref[...] = (du[:, :512] * sg[:, :512]).astype(dom_ref.dtype)
        dos_ref[...] = (du[:, 512:] * sg[:, 512:]).astype(dos_ref.dtype)
        dgate_ref[:, :512] = (du[:, :512] * om_ref[...] * dsg[:, :512]).astype(dgate_ref.dtype)
        dgate_ref[:, 512:] = (du[:, 512:] * os_ref[...] * dsg[:, 512:]).astype(dgate_ref.dtype)

        @pl.when(i == 0)
        def _():
            dgn_ref[...] = jnp.zeros_like(dgn_ref)
            dwo_ref[...] = jnp.zeros_like(dwo_ref)

        dgn_ref[...] += jnp.sum(dg_rows, axis=0, keepdims=True)
        dwo_ref[0:512, :] += _mm_tn(om_ref[...] * sg[:, :512], dx1)
        dwo_ref[512:1024, :] += _mm_tn(os_ref[...] * sg[:, 512:], dx1)

    return _pcall(
        body, name="layer1_in_bwd", grid=(S // TOK,), semantics=("arbitrary",),
        in_specs=[_rows(TOK, D), _rows(TOK, D), _rows(TOK, D), _rows(TOK, D), _rows(TOK, LANES), _rows(TOK, D),
                  _rows(TOK, D), _full((1, D)), _full(w_in1.shape), _rows(TOK, D), _rows(TOK, 512), _rows(TOK, 512),
                  _full((D, D))],
        out_specs=[_rows(TOK, 4224), _rows(TOK, D), _full((1, D)), _full((D, D)), _rows(TOK, 512), _rows(TOK, 512),
                   _rows(TOK, D)],
        out_shape=[_sds((S, 4224), MXU), _sds((S, D), jnp.float32), _sds((1, D), jnp.float32), _sds((D, D), jnp.float32),
                   _sds((S, 512), MXU), _sds((S, 512), MXU), _sds((S, D), MXU)],
    )(dq, dk, dv, dgate1, df, x1, dx2, g1, w_in1, gate0, o_m, o_s, w_out0)


def _layer0_in_bwd(dqm, dkm, dvm, dqs, dkd, dvd, dgate0, cos, sin, cq, ckv, x, dx1, g_in, w_in, g_q, w_q, g_kv, w_kv):
    S = x.shape[0]
    consts = _rope_consts()

    def body(dqm_ref, dkm_ref, dvm_ref, dqs_ref, dkd_ref, dvd_ref, dgate_ref, cos_ref, sin_ref, c_ref, cq_ref, ckv_ref,
             x_ref, dx1_ref, g_ref, w_ref, gq_ref, wq_ref, gkv_ref, wkv_ref,
             dx_ref, dz_ref, dgin_ref, dgq_ref, dgkv_ref, dwq_ref, dwkv_ref, dqu_ref, dkvu_ref):
        i = pl.program_id(0)
        lo = _lane_masks()
        sign = c_ref[...][1:2, :]
        c = cos_ref[...]
        s = sin_ref[...]
        dkpe = None
        for hd in range(N_MLA):
            sl = slice(LANES * hd, LANES * (hd + 1))
            dqu_ref[:, sl] = _rope_t(dqm_ref[:, sl], c, s, sign).astype(dqu_ref.dtype)
            dkh = dkm_ref[:, sl]
            dkvu_ref[:, sl] = jnp.where(lo, dkh, 0.0).astype(dkvu_ref.dtype)
            dkpe = dkh if dkpe is None else dkpe + dkh
        dkvu_ref[:, 1024:1536] = dvm_ref[...]
        dkpe = _rope_t(jnp.where(lo, 0.0, dkpe), c, s, sign)
        dcqn = _mm(dqu_ref[...], wq_ref[...])
        dckvn = _mm_nt(dkvu_ref[...], wkv_ref[...])
        gq = gq_ref[...]
        gkv = gkv_ref[...]
        dcq, dgq_rows = _rms_bwd(cq_ref[...], gq, dcqn)
        dckv, dgkv_rows = _rms_bwd(ckv_ref[...], gkv, dckvn)
        dz_ref[:, 0:256] = dcq.astype(dz_ref.dtype)
        dz_ref[:, 256:384] = dckv.astype(dz_ref.dtype)
        dz_ref[:, 384:512] = dkpe.astype(dz_ref.dtype)
        dz_ref[:, 512:1024] = dqs_ref[...]
        dz_ref[:, 1024:1536] = dkd_ref[...]
        dz_ref[:, 1536:2048] = dvd_ref[...]
        dz_ref[:, 2048:3072] = dgate_ref[...]
        dh = _mm(dz_ref[...], w_ref[...])
        g = g_ref[...]
        dxn, dg_rows = _rms_bwd(x_ref[...], g, dh)
        dx_ref[...] = dx1_ref[...] + dxn

        @pl.when(i == 0)
        def _():
            dgin_ref[...] = jnp.zeros_like(dgin_ref)
            dgq_ref[...] = jnp.zeros_like(dgq_ref)
            dgkv_ref[...] = jnp.zeros_like(dgkv_ref)
            dwq_ref[...] = jnp.zeros_like(dwq_ref)
            dwkv_ref[...] = jnp.zeros_like(dwkv_ref)

        dgin_ref[...] += jnp.sum(dg_rows, axis=0, keepdims=True)
        dgq_ref[...] += jnp.sum(dgq_rows, axis=0, keepdims=True)
        dgkv_ref[...] += jnp.sum(dgkv_rows, axis=0, keepdims=True)
        dwq_ref[...] += _mm_tn(dqu_ref[...], _rms(cq_ref[...], gq))
        dwkv_ref[...] += _mm_tn(_rms(ckv_ref[...], gkv), dkvu_ref[...])

    return _pcall(
        body, name="layer0_in_bwd", grid=(S // TOK,), semantics=("arbitrary",),
        in_specs=[_rows(TOK, 1024), _rows(TOK, 1024), _rows(TOK, 512), _rows(TOK, 512), _rows(TOK, 512), _rows(TOK, 512),
                  _rows(TOK, D), _rows(TOK, LANES), _rows(TOK, LANES), _full((8, LANES)), _rows(TOK, 256), _rows(TOK, 128),
                  _rows(TOK, D), _rows(TOK, D), _full((1, D)), _full(w_in.shape), _full((1, 256)), _full(w_q.shape),
                  _full((1, 128)), _full(w_kv.shape)],
        out_specs=[_rows(TOK, D), _rows(TOK, 3072), _full((1, D)), _full((1, 256)), _full((1, 128)), _full(w_q.shape),
                   _full(w_kv.shape)],
        out_shape=[_sds((S, D), jnp.float32), _sds((S, 3072), MXU), _sds((1, D), jnp.float32), _sds((1, 256), jnp.float32),
                   _sds((1, 128), jnp.float32), _sds(w_q.shape, jnp.float32), _sds(w_kv.shape, jnp.float32)],
        scratch_shapes=[pltpu.VMEM((TOK, 1024), MXU), pltpu.VMEM((TOK, 1536), MXU)],
    )(dqm, dkm, dvm, dqs, dkd, dvd, dgate0, cos, sin, consts, cq, ckv, x, dx1, g_in, w_in, g_q, w_q, g_kv, w_kv)


def _wgrad(a, b, name):
    S, M = a.shape
    N = b.shape[1]
    tm = next(t for t in range(WG_ROWS, 0, -LANES) if M % t == 0)
    tn = N if N <= 1024 else 512
    tk = min(WG_TOK, S)

    def body(a_ref, b_ref, o_ref):
        @pl.when(pl.program_id(2) == 0)
        def _():
            o_ref[...] = jnp.zeros_like(o_ref)

        o_ref[...] += _mm_tn(a_ref[...], b_ref[...])

    return _pcall(
        body, name=name, grid=(M // tm, N // tn, S // tk), semantics=("parallel", "parallel", "arbitrary"),
        in_specs=[pl.BlockSpec((tk, tm), lambda m, n, k: (k, m)), pl.BlockSpec((tk, tn), lambda m, n, k: (k, n))],
        out_specs=pl.BlockSpec((tm, tn), lambda m, n, k: (m, n)),
        out_shape=_sds((M, N), jnp.float32),
    )(a, b)


def _adamw(w, g, m, v, name):
    shape = w.shape
    R, C = (int(np.prod(shape[:-1])), shape[-1])
    w2, g2, m2, v2 = (t.reshape(R, C) for t in (w, g, m, v))
    fits = [t for t in range(8, ADAM_TILE_BYTES // (4 * C) + 1, 8) if R % t == 0]
    tr = max(fits) if fits else R
    tc = C if (tr * C * 4 <= ADAM_TILE_BYTES or C % 256) else 256

    def body(w_ref, g_ref, m_ref, v_ref, d_ref, nm_ref, nv_ref):
        gg = g_ref[...]
        nm = B1 * m_ref[...] + (1.0 - B1) * gg
        nv = B2 * v_ref[...] + (1.0 - B2) * (gg * gg)
        m_hat = nm / (1.0 - B1 ** STEP)
        v_hat = nv / (1.0 - B2 ** STEP)
        d_ref[...] = -LR * (m_hat / (jnp.sqrt(v_hat) + AEPS) + WD * w_ref[...])
        nm_ref[...] = nm
        nv_ref[...] = nv

    spec = pl.BlockSpec((tr, tc), lambda i, j: (i, j))
    d, nm, nv = _pcall(
        body, name=name, grid=(R // tr, C // tc), semantics=("parallel", "parallel"),
        in_specs=[spec] * 4, out_specs=[spec] * 3, out_shape=[_sds((R, C), jnp.float32)] * 3,
    )(w2, g2, m2, v2)
    return d.reshape(shape), nm.reshape(shape), nv.reshape(shape)


def _sum_leading(a, name):
    n, R, C = a.shape
    tr = SUM_ROWS if R % SUM_ROWS == 0 else R

    def body(a_ref, o_ref):
        acc = a_ref[0]
        for i in range(1, n):
            acc = acc + a_ref[i]
        o_ref[...] = acc

    return _pcall(
        body, name=name, grid=(R // tr,), semantics=("parallel",),
        in_specs=[pl.BlockSpec((n, tr, C), lambda i: (0, i, 0))], out_specs=_rows(tr, C),
        out_shape=_sds((R, C), a.dtype),
    )(a)


def _add_halves(g, c, b, name, out_dtype):
    n, _, R, C = g.shape
    tr = SUM_ROWS if R % SUM_ROWS == 0 else R

    def body(c_ref, a_ref, b_ref, o_ref):
        o_ref[...] = (a_ref[0] + b_ref[...]).astype(o_ref.dtype)

    spec = pl.BlockSpec((1, tr, C), lambda k, i, c_ref: (k, i, 0))
    grid_spec = pltpu.PrefetchScalarGridSpec(
        num_scalar_prefetch=1, grid=(n, R // tr),
        in_specs=[pl.BlockSpec((1, 1, tr, C), lambda k, i, c_ref: (k, c_ref[0], i, 0)), spec], out_specs=spec)
    return _pcall(body, name=name, semantics=("parallel", "parallel"), grid_spec=grid_spec,
                  out_shape=_sds(b.shape, out_dtype))(c.reshape(1).astype(jnp.int32), g, b)


def _total_sum(mine, theirs, recv, name):
    R, C = mine.shape
    n = recv.shape[0]
    tr = SUM_ROWS if R % SUM_ROWS == 0 else R

    def body(a_ref, b_ref, r_ref, o_ref):
        acc = a_ref[...] + b_ref[...]
        for i in range(n):
            acc = acc + r_ref[i].astype(jnp.float32)
        o_ref[...] = acc

    return _pcall(
        body, name=name, grid=(R // tr,), semantics=("parallel",),
        in_specs=[_rows(tr, C), _rows(tr, C), pl.BlockSpec((n, tr, C), lambda i: (0, i, 0))], out_specs=_rows(tr, C),
        out_shape=_sds((R, C), jnp.float32),
    )(mine, theirs, recv)


def _place():
    return lax.axis_index("x"), lax.axis_index("y"), lax.axis_index("c")


class _Plan:
    def __init__(self, arrays, out_shape, scratch, start, finish, middle=None):
        self.arrays, self.out_shape, self.scratch = list(arrays), list(out_shape), list(scratch)
        self.start, self.finish, self.middle = start, finish, middle


def _gather8_plan(block):
    R, C = block.shape

    def parts(ins, outs, sems):
        (x_ref,), (out_ref,), (send_sems, recv_sems) = ins, outs, sems
        x, y, c = _place()
        me, sibling = (x, y, c), (x, y, 1 - c)
        chips = [(1 - x, y), (x, 1 - y), (1 - x, 1 - y)]

        def copy(k, blk, to, src=None):
            slot = out_ref.at[4 * blk[0] + 2 * blk[1] + blk[2]]
            return pltpu.make_async_remote_copy(
                src_ref=slot if src is None else src, dst_ref=slot,
                send_sem=send_sems.at[k], recv_sem=recv_sems.at[k], device_id=to, device_id_type=MESH_ID)

        def first():
            return [copy(0, me, sibling, src=x_ref)] + [copy(1 + j, me, (*chip, c), src=x_ref) for j, chip in enumerate(chips)]

        def passed():
            return [copy(4 + j, (*chip, c), sibling) for j, chip in enumerate(chips)]

        def arrivals():
            return [copy(1 + j, (*chip, c), me) for j, chip in enumerate(chips)]

        def late():
            return [copy(0, sibling, me)] + [copy(4 + j, (*chip, 1 - c), me) for j, chip in enumerate(chips)]

        return first, passed, arrivals, late

    def start(ins, outs, sems):
        for cp in parts(ins, outs, sems)[0]():
            cp.start()

    def middle(ins, outs, sems):
        _, passed, arrivals, _ = parts(ins, outs, sems)
        for arrived, forward in zip(arrivals(), passed()):
            arrived.wait_recv()
            forward.start()

    def finish(ins, outs, sems):
        first, passed, _, late = parts(ins, outs, sems)
        for cp in late():
            cp.wait_recv()
        for cp in first() + passed():
            cp.wait_send()

    return _Plan([block], [_sds((8, R, C), block.dtype)], [pltpu.SemaphoreType.DMA((7,)), pltpu.SemaphoreType.DMA((7,))],
                 start, finish, middle)


def _fill_own_slot(gathered, block):
    x, y, c = _place()
    return lax.dynamic_update_index_in_dim(gathered, block, 4 * x + 2 * y + c, 0)


def _started_and_waited(arrays, out_shape, n, copies):
    def start(ins, outs, sems):
        for cp in copies(ins, outs, sems):
            cp.start()

    def finish(ins, outs, sems):
        for cp in copies(ins, outs, sems):
            cp.wait()

    return _Plan(arrays, out_shape, [pltpu.SemaphoreType.DMA((n,)), pltpu.SemaphoreType.DMA((n,))], start, finish)


def _pair_swap_plan(g):
    n = g.shape[0]

    def copies(ins, outs, sems):
        (g_ref,), (out_ref,), (send_sems, recv_sems) = ins, outs, sems
        x, y, c = _place()
        return [pltpu.make_async_remote_copy(src_ref=g_ref.at[k, 1 - c], dst_ref=out_ref.at[k], send_sem=send_sems.at[k],
                                             recv_sem=recv_sems.at[k], device_id=(x, y, 1 - c), device_id_type=MESH_ID)
                for k in range(n)]

    return _started_and_waited([g], [_sds((n,) + g.shape[2:], g.dtype)], n, copies)


def _chip_exchange_plan(p):
    def copies(ins, outs, sems):
        (p_ref,), (out_ref,), (send_sems, recv_sems) = ins, outs, sems
        x, y, c = _place()
        chips = [(1 - x, y), (x, 1 - y), (1 - x, 1 - y)]
        return [pltpu.make_async_remote_copy(
            src_ref=p_ref.at[2 * cx + cy], dst_ref=out_ref.at[j], send_sem=send_sems.at[j],
            recv_sem=recv_sems.at[j], device_id=(cx, cy, c), device_id_type=MESH_ID)
            for j, (cx, cy) in enumerate(chips)]

    return _started_and_waited([p], [_sds((3,) + p.shape[1:], p.dtype)], 3, copies)


def _pair_exchange_plan(t):
    def copies(ins, outs, sems):
        (t_ref,), (out_ref,), (send_sems, recv_sems) = ins, outs, sems
        x, y, c = _place()
        return [pltpu.make_async_remote_copy(src_ref=t_ref, dst_ref=out_ref, send_sem=send_sems.at[0], recv_sem=recv_sems.at[0],
                                             device_id=(x, y, 1 - c), device_id_type=MESH_ID)]

    return _started_and_waited([t], [_sds(t.shape, t.dtype)], 1, copies)


def _both_plans(a, b):
    na, ma, sa = len(a.arrays), len(a.out_shape), len(a.scratch)

    def phase(name):
        fa, fb = getattr(a, name), getattr(b, name)
        if fa is None and fb is None:
            return None

        def run(ins, outs, sems):
            if fa is not None:
                fa(ins[:na], outs[:ma], sems[:sa])
            if fb is not None:
                fb(ins[na:], outs[ma:], sems[sa:])
        return run

    return _Plan(a.arrays + b.arrays, a.out_shape + b.out_shape, a.scratch + b.scratch,
                 phase("start"), phase("finish"), phase("middle"))


ANY_SPEC = pl.BlockSpec(memory_space=pl.ANY)


def _run_plan(plan, name):
    n_in, n_out = len(plan.arrays), len(plan.out_shape)

    def body(*refs):
        ins, outs, sems = refs[:n_in], refs[n_in:n_in + n_out], refs[n_in + n_out:]
        plan.start(ins, outs, sems)
        if plan.middle is not None:
            plan.middle(ins, outs, sems)
        plan.finish(ins, outs, sems)

    return _pcall(body, name=name, in_specs=[ANY_SPEC] * n_in, out_specs=[ANY_SPEC] * n_out, out_shape=plan.out_shape,
                  scratch_shapes=plan.scratch)(*plan.arrays)


def _pcall_riding(body, plan, args, *, name, grid, in_specs, out_specs, out_shape, scratch_shapes):
    if plan is None:
        outs = _pcall(body, name=name, grid=grid, semantics=("arbitrary",), in_specs=in_specs, out_specs=out_specs,
                      out_shape=out_shape, scratch_shapes=scratch_shapes)(*args)
        return list(outs), None
    n_in, n_out, n_s = len(args), len(out_shape), len(scratch_shapes)
    p_in, p_out = len(plan.arrays), len(plan.out_shape)
    steps = grid[0]

    def riding(*refs):
        ins, pins = refs[:n_in], refs[n_in:n_in + p_in]
        o0 = n_in + p_in
        outs, pouts = refs[o0:o0 + n_out], refs[o0 + n_out:o0 + n_out + p_out]
        s0 = o0 + n_out + p_out
        scr, sems = refs[s0:s0 + n_s], refs[s0 + n_s:]
        j = pl.program_id(0)

        @pl.when(j == 0)
        def _():
            plan.start(pins, pouts, sems)

        if plan.middle is not None:
            @pl.when(j == steps // 2)
            def _():
                plan.middle(pins, pouts, sems)

        body(*ins, *outs, *scr)

        @pl.when(j == steps - 1)
        def _():
            plan.finish(pins, pouts, sems)

    res = _pcall(riding, name=name, grid=grid, semantics=("arbitrary",), in_specs=list(in_specs) + [ANY_SPEC] * p_in,
                 out_specs=list(out_specs) + [ANY_SPEC] * p_out, out_shape=list(out_shape) + plan.out_shape,
                 scratch_shapes=list(scratch_shapes) + plan.scratch)(*args, *plan.arrays)
    return list(res[:n_out]), list(res[n_out:])


class _RowSeq:
    def __init__(self, pieces):
        self.pieces = list(pieces)

    def rows(self, a, b):
        out, off = [], 0
        for p in self.pieces:
            lo, hi = max(a, off), min(b, off + p.shape[0])
            if lo < hi:
                out.append(p[lo - off:hi - off])
            off += p.shape[0]
        return out

    def array(self):
        return jnp.concatenate(self.pieces, axis=0)


def _row_seq(w):
    return w if isinstance(w, _RowSeq) else _RowSeq([w])


def _prep_w_in0(wt):
    wt = _row_seq(wt)
    one = wt.pieces[0]
    z32 = [jnp.zeros((32, one.shape[1]), one.dtype)]
    k0, k1 = wt.rows(928, 992), wt.rows(992, 1056)
    v0, v1 = wt.rows(1056, 1120), wt.rows(1120, 1184)
    return jnp.concatenate(wt.rows(0, 384) + z32 + z32 + wt.rows(384, 416) + z32 + wt.rows(416, 928)
                           + k0 * 4 + k1 * 4 + v0 * 4 + v1 * 4 + wt.rows(1184, 2208), axis=0)


def _fold_w_in0(d):
    def fold(blk):
        b = blk.reshape(8, 64, blk.shape[1])
        return jnp.concatenate([b[0] + b[1] + b[2] + b[3], b[4] + b[5] + b[6] + b[7]], axis=0)
    return _RowSeq([d[0:384], d[448:480], d[512:1024], fold(d[1024:1536]), fold(d[1536:2048]), d[2048:3072]])


def _prep_w_q(wt):
    return jnp.pad(wt.reshape(N_MLA, 96, Q_RANK), ((0, 0), (0, 32), (0, 0))).reshape(1024, Q_RANK)


def _fold_w_q(d):
    return d.reshape(N_MLA, 128, Q_RANK)[:, :96].reshape(768, Q_RANK)


def _prep_w_kv(w):
    w3 = w.reshape(KV_RANK, N_MLA, 128)
    kk = jnp.pad(w3[:, :, :64], ((0, 0), (0, 0), (0, 64))).reshape(KV_RANK, 1024)
    return jnp.concatenate([kk, w3[:, :, 64:].reshape(KV_RANK, 512)], axis=1)


def _fold_w_kv(d):
    kk = d[:, :1024].reshape(KV_RANK, N_MLA, 128)[:, :, :64]
    vv = d[:, 1024:].reshape(KV_RANK, N_MLA, 64)
    return jnp.concatenate([kk, vv], axis=2).reshape(KV_RANK, 1024)


def _prep_w_in1(wt):
    wt = _row_seq(wt)
    one = wt.pieces[0]
    return jnp.concatenate(wt.rows(0, 3072) + wt.rows(3088, 4112) + wt.rows(3072, 3088)
                           + [jnp.zeros((112, one.shape[1]), one.dtype)], axis=0)


def _fold_w_in1(d):
    return _RowSeq([d[0:3072], d[4096:4112], d[3072:4096]])


class _Alone:
    def __init__(self, w_out0, o_g_in, w_in1, w_out1):
        self.layer1 = (w_out0, o_g_in, w_in1, w_out1)

    def gather_plan(self):
        return None

    def layer1_weights(self, rode):
        return self.layer1

    def swap_plan(self, grads1):
        return None

    def exchange_plan(self, rode):
        return None

    def finish(self, rode):
        pass


def _local_step(x, pos, target, e_g_in, w_in0, e_g_q, w_q, e_g_kv, w_kv, sinks, b_f, g_final, layer1):
    S = x.shape[0]
    w_in0p, w_qp, w_kvp = _prep_w_in0(w_in0), _prep_w_q(w_q), _prep_w_kv(w_kv)
    slopes = jnp.asarray(2.0 ** (-8.0 * (np.arange(N_SWA, dtype=np.float32) + 1.0) / N_SWA), jnp.float32)
    sinks1 = sinks.reshape(N_SWA)
    b_col = b_f.reshape(N_FOX, 1)

    (h0, cq, ckv, qm, km, vm, qs, kd, vd, gate0, cos, sin) = _layer0_in(
        x, pos, e_g_in, w_in0p, e_g_q, w_qp, e_g_kv, w_kvp)
    o_m, lse_m, rode = _attn_fwd_t(qm, km, vm, (NOPE + ROPE) ** -0.5, split=True, name="mla_fwd", plan=layer1.gather_plan())
    w_out0, o_g_in, w_in1, w_out1 = layer1.layer1_weights(rode)
    w_in1p = _prep_w_in1(w_in1)
    o_s, lse_s = _swa_fwd(qs, kd, vd, sinks1, slopes)
    x1, h1, q1, k1, v1, gate1, f_slab = _layer0_out_layer1_in(x, o_m, o_s, gate0, w_out0, o_g_in, w_in1p)
    f_row = f_slab[:, :N_FOX].T
    lc_row = _forget_fwd(f_row, b_col)
    lcc = lc_row.T
    o1, lse1, _ = _attn_fwd_t(q1, k1, v1, HEAD ** -0.5, split=False, name="fox_fwd", lcc=lcc)
    loss8, dg_final, dw_out1, dx2, do1, dgate1 = _head(x1, o1, gate1, w_out1, g_final, target)

    dq1, dk1, dv1, dlc, _ = _attn_bwd_t(q1, k1, v1, do1, o1, lse1, HEAD ** -0.5, split=False, name="fox_bwd", lcc=lcc)
    df_row, db_f = _forget_bwd(dlc.reshape(N_FOX, S), f_row, b_col)
    df_slab = jnp.pad(df_row.T, ((0, 0), (0, LANES - N_FOX))).astype(MXU)
    dz1, dx1, dg_o_in, dw_out0, do_m, do_s, dgate0 = _layer1_in_bwd(
        dq1, dk1, dv1, dgate1, df_slab, x1, dx2, o_g_in, w_in1p, gate0, o_m, o_s, w_out0)
    grads1 = dict(o_g_in=dg_o_in, o_w_in=_fold_w_in1(_wgrad(dz1, h1, "wgrad_in1")), o_w_out=dw_out1, e_w_out=dw_out0)
    dqs, dkd, dvd, dsink, rode = _swa_bwd(qs, kd, vd, do_s, o_s, lse_s, sinks1, slopes, plan=layer1.swap_plan(grads1))
    dqm, dkm, dvm, rode = _attn_bwd_t(qm, km, vm, do_m, o_m, lse_m, (NOPE + ROPE) ** -0.5, split=True, name="mla_bwd",
                                      plan=layer1.exchange_plan(rode))
    layer1.finish(rode)
    dx, dz0, dg_in, dg_q, dg_kv, dw_q, dw_kv = _layer0_in_bwd(
        dqm, dkm, dvm, dqs, dkd, dvd, dgate0, cos, sin, cq, ckv, x, dx1, e_g_in, w_in0p, e_g_q, w_qp, e_g_kv, w_kvp)

    grads = dict(
        e_g_in=dg_in,
        e_w_in=_fold_w_in0(_wgrad(dz0, h0, "wgrad_in0")),
        e_g_q_a=dg_q,
        e_w_q_up=_fold_w_q(dw_q),
        e_g_kv_a=dg_kv,
        e_w_kv_up=_fold_w_kv(dw_kv),
        e_sinks=dsink[:, 0:2, 0].reshape(1, N_SWA),
        o_b_f=db_f.reshape(1, N_FOX),
        g_final=dg_final,
        **grads1,
    )
    return loss8[0, 0], dx, grads


SHARDED = ("e_w_in", "e_w_q_up", "e_w_kv_up", "e_w_out", "o_g_in", "o_w_in", "o_w_out")
TRANSPOSED = ("e_w_in", "e_w_q_up", "o_w_in")
COL_SHARDED = ("e_w_kv_up", "o_g_in")
REPLICATED = ("e_g_in", "e_g_q_a", "e_g_kv_a", "e_sinks", "o_b_f", "g_final")
FULL_SHAPES = dict(e_w_in=(2208, 1024), e_w_q_up=(768, 256), e_w_kv_up=(128, 1024), e_w_out=(1024, 1024),
                   o_g_in=(1, 1024), o_w_in=(4112, 1024), o_w_out=(1024, 1024))
GROUPS = dict(
    layer0=dict(rows=768, windows=dict(e_w_in=(0, 0), e_w_q_up=(560, 0), e_w_kv_up=(560, 256))),
    layer1=dict(rows=1568, windows=dict(o_w_in=(0, 0), o_w_out=(1040, 0), e_w_out=(1296, 0), o_g_in=(1552, 0))),
)


def _shard_shape(name):
    r, c = FULL_SHAPES[name]
    return (r, c // 4) if name in COL_SHARDED else (r // 4, c)


def _as_handled(name, a):
    a = a[0] if a.ndim == 3 else a
    return a.T if name in TRANSPOSED else a


def _as_given(name, a, shape):
    return (a.T if name in TRANSPOSED else a).reshape(shape)


def _pack_block(p, group):
    def rows(a, n):
        return jnp.pad(a, ((0, n - a.shape[0]), (0, 0)))

    if group == "layer0":
        band = jnp.concatenate([p["e_w_q_up"], rows(p["e_w_kv_up"], 192), jnp.zeros((192, 512), p["e_w_in"].dtype)], axis=1)
        return jnp.concatenate([rows(p["e_w_in"], 560), rows(band, 208)], axis=0)
    g = p["o_g_in"]
    band = jnp.pad(g, ((0, 16 - g.shape[0]), (0, PACK_COLS - g.shape[1])))
    return jnp.concatenate([rows(p["o_w_in"], 1040), p["o_w_out"], p["e_w_out"], band], axis=0)


def _window(block, group, name, width=None):
    r0, c0 = GROUPS[group]["windows"][name]
    r, c = _shard_shape(name)
    return block[..., r0:r0 + r, c0:c0 + (c if width is None else width)]


def _chip_slice(name, full, k):
    r, c = _shard_shape(name)
    if isinstance(full, _RowSeq):
        return jnp.concatenate(full.rows(r * k, r * (k + 1)), axis=0)
    return full[:, c * k:c * (k + 1)] if name in COL_SHARDED else full[r * k:r * (k + 1), :]


def _packed_weights(w, group):
    parts = {}
    for n in GROUPS[group]["windows"]:
        a = _as_handled(n, w[n])
        parts[n] = lax.bitcast_convert_type(a, jnp.bfloat16).reshape(1, -1) if n == "o_g_in" else a.astype(jnp.bfloat16)
    halves = _pack_block(parts, group).reshape(2, GROUPS[group]["rows"] // 2, PACK_COLS)
    return lax.dynamic_index_in_dim(halves, lax.axis_index("c"), 0, keepdims=False)


def _unpacked_weights(gathered, half, group):
    blocks = _fill_own_slot(gathered, half).reshape(4, GROUPS[group]["rows"], PACK_COLS)
    full = {}
    for n in GROUPS[group]["windows"]:
        if n == "o_g_in":
            halves = _window(blocks, group, n, width=512).reshape(4, 1, 256, 2)
            full[n] = jnp.concatenate(list(lax.bitcast_convert_type(halves, jnp.float32)), axis=1)
        else:
            pieces = [_window(blocks[k], group, n).astype(MXU) for k in range(4)]
            if n in ("e_w_in", "o_w_in"):
                full[n] = _RowSeq(pieces)
            else:
                full[n] = jnp.concatenate(pieces, axis=1 if n in COL_SHARDED else 0)
    return full


class _GroupReduce:
    def __init__(self, group):
        self.group = group
        self.c = lax.axis_index("c")
        self.chip = 2 * lax.axis_index("x") + lax.axis_index("y")

    def swap_plan(self, grads):
        names = GROUPS[self.group]["windows"]
        per_chip = jnp.stack([_pack_block({n: _chip_slice(n, grads[n], k) for n in names}, self.group) for k in range(4)])
        self.g4 = per_chip.reshape(4, 2, GROUPS[self.group]["rows"] // 2, PACK_COLS)
        return _pair_swap_plan(self.g4)

    def exchange_plan(self, rode):
        theirs = rode[0]
        rows = self.g4.shape[2]
        self.own = (lax.dynamic_slice(self.g4, (self.chip, self.c, 0, 0), (1, 1, rows, PACK_COLS)).reshape(rows, PACK_COLS),
                    lax.dynamic_index_in_dim(theirs, self.chip, 0, keepdims=False))
        return _chip_exchange_plan(_add_halves(self.g4, self.c, theirs, "pair_add_" + self.group, jnp.bfloat16))

    def finish(self, rode):
        my_half = _total_sum(*self.own, rode[0], "chip_sum_" + self.group)
        other_half = _run_plan(_pair_exchange_plan(my_half), "pair_exchange_" + self.group)[0]
        total = jnp.concatenate([jnp.where(self.c == 0, my_half, other_half), jnp.where(self.c == 0, other_half, my_half)], axis=0)
        self.sums = {n: _window(total, self.group, n) for n in GROUPS[self.group]["windows"]}

    def run(self, grads, beside):
        swap = self.swap_plan(grads)
        outs = _run_plan(_both_plans(swap, beside), "pair_swap_" + self.group)
        rode, others = outs[:len(swap.out_shape)], outs[len(swap.out_shape):]
        self.finish(_run_plan(self.exchange_plan(rode), "chip_exchange_" + self.group))
        return self.sums, others


class _Layer1Exchange(_GroupReduce):
    def __init__(self, w):
        super().__init__("layer1")
        self.half = _packed_weights(w, "layer1")

    def gather_plan(self):
        return _gather8_plan(self.half)

    def layer1_weights(self, rode):
        full = _unpacked_weights(rode[0], self.half, "layer1")
        return full["e_w_out"], full["o_g_in"], full["o_w_in"], full["o_w_out"]


def kernel(x, positions, e_g_in, e_w_in, e_g_q_a, e_w_q_up, e_g_kv_a, e_w_kv_up, e_sinks, e_w_out, o_g_in, o_w_in, o_b_f, o_w_out, g_final, loss_target, m_e_g_in, m_e_w_in, m_e_g_q_a, m_e_w_q_up, m_e_g_kv_a, m_e_w_kv_up, m_e_sinks, m_e_w_out, m_o_g_in, m_o_w_in, m_o_b_f, m_o_w_out, m_g_final, v_e_g_in, v_e_w_in, v_e_g_q_a, v_e_w_q_up, v_e_g_kv_a, v_e_w_kv_up, v_e_sinks, v_e_w_out, v_o_g_in, v_o_w_in, v_o_b_f, v_o_w_out, v_g_final):
    w = dict(e_g_in=e_g_in, e_w_in=e_w_in, e_g_q_a=e_g_q_a, e_w_q_up=e_w_q_up, e_g_kv_a=e_g_kv_a, e_w_kv_up=e_w_kv_up,
             e_sinks=e_sinks, e_w_out=e_w_out, o_g_in=o_g_in, o_w_in=o_w_in, o_b_f=o_b_f, o_w_out=o_w_out, g_final=g_final)
    m = dict(e_g_in=m_e_g_in, e_w_in=m_e_w_in, e_g_q_a=m_e_g_q_a, e_w_q_up=m_e_w_q_up, e_g_kv_a=m_e_g_kv_a,
             e_w_kv_up=m_e_w_kv_up, e_sinks=m_e_sinks, e_w_out=m_e_w_out, o_g_in=m_o_g_in, o_w_in=m_o_w_in, o_b_f=m_o_b_f,
             o_w_out=m_o_w_out, g_final=m_g_final)
    v = dict(e_g_in=v_e_g_in, e_w_in=v_e_w_in, e_g_q_a=v_e_g_q_a, e_w_q_up=v_e_w_q_up, e_g_kv_a=v_e_g_kv_a,
             e_w_kv_up=v_e_w_kv_up, e_sinks=v_e_sinks, e_w_out=v_e_w_out, o_g_in=v_o_g_in, o_w_in=v_o_w_in, o_b_f=v_o_b_f,
             o_w_out=v_o_w_out, g_final=v_g_final)
    order = ("e_g_in", "e_w_in", "e_g_q_a", "e_w_q_up", "e_g_kv_a", "e_w_kv_up", "e_sinks", "e_w_out", "o_g_in", "o_w_in",
             "o_b_f", "o_w_out", "g_final")
    half0 = _packed_weights(w, "layer0")
    full = _unpacked_weights(_run_plan(_gather8_plan(half0), "gather_weights_layer0")[0], half0, "layer0")
    layer1 = _Layer1Exchange(w)

    loss_part, dx, grads = _local_step(
        x[0], positions.reshape(-1, 1), loss_target[0], e_g_in, full["e_w_in"], e_g_q_a, full["e_w_q_up"], e_g_kv_a,
        full["e_w_kv_up"], e_sinks, o_b_f, g_final.reshape(1, D), layer1)

    small = jnp.concatenate([jnp.pad(loss_part.reshape(1), (0, LANES - 1))]
                            + [jnp.pad(grads[n].reshape(-1), (0, (-grads[n].size) % LANES)) for n in REPLICATED])
    rows = small.shape[0] // LANES
    small = jnp.pad(small.reshape(rows, LANES), ((0, (-rows) % 8), (0, 0)))
    sums0, (gathered_small,) = _GroupReduce("layer0").run(grads, _gather8_plan(small))
    gsum = {**layer1.sums, **sums0}
    ssum = _sum_leading(_fill_own_slot(gathered_small, small), "small_grad_sum").reshape(-1)
    loss = ssum[0]
    off = LANES
    for n in REPLICATED:
        cnt = w[n].size
        gsum[n] = ssum[off:off + cnt].reshape(w[n].shape)
        off += cnt + (-cnt) % LANES

    grad, delta, new_m, new_v = {}, {}, {}, {}
    for n in order:
        if n in SHARDED:
            outs = _adamw(_as_handled(n, w[n]), gsum[n], _as_handled(n, m[n]), _as_handled(n, v[n]), "adamw_" + n)
            grad[n], delta[n], new_m[n], new_v[n] = (_as_given(n, a, w[n].shape) for a in (gsum[n],) + outs)
        else:
            grad[n] = gsum[n]
            delta[n], new_m[n], new_v[n] = _adamw(w[n], gsum[n], m[n], v[n], "adamw_" + n)
    return (loss, dx[None], *[grad[n] for n in order], *[delta[n] for n in order], *[new_m[n] for n in order],
            *[new_v[n] for n in order])
```

```python
import math

import numpy as np
import jax
import jax.numpy as jnp
from jax import lax
from jax.experimental import pallas as pl
from jax.experimental.pallas import tpu as pltpu

D = 1024
EPS = 1e-6
ROPE_THETA = 10000.0
N_MLA = 8
Q_RANK = 256
KV_RANK = 128
NOPE = 64
ROPE = 32
N_SWA = 8
WINDOW = 128
N_FOX = 16
HEAD = 64
LR, B1, B2, AEPS, WD, STEP = 0.001, 0.9, 0.999, 1e-08, 0.01, 10

LANES = 128
HALF = 64
VMEM_LIMIT = 56 * 1024 * 1024
MXU = jnp.bfloat16
TOK = 256
WG_TOK = 2048
WG_ROWS = 1536
ATT = 256
FWD_CHUNK = 2
BWD_CHUNK = 2
SWA_GROUP = 8
NEG = float("-inf")

PACK_COLS = 1024
SUM_ROWS = 256
ADAM_TILE_BYTES = 2 << 20
MESH_ID = pl.DeviceIdType.MESH


def _pcall(body, *, name, vmem=VMEM_LIMIT, semantics=None, **kw):
    params = dict(vmem_limit_bytes=vmem)
    if semantics is not None:
        params["dimension_semantics"] = semantics
    return pl.pallas_call(body, name=name, compiler_params=pltpu.CompilerParams(**params), **kw)


def _mm(a, b):
    return jnp.dot(a.astype(MXU), b.astype(MXU), preferred_element_type=jnp.float32)


def _mm_nt(a, b):
    return lax.dot_general(a.astype(MXU), b.astype(MXU), (((1,), (1,)), ((), ())),
                           preferred_element_type=jnp.float32)


def _mm_tn(a, b):
    return lax.dot_general(a.astype(MXU), b.astype(MXU), (((0,), (0,)), ((), ())),
                           preferred_element_type=jnp.float32)


def _full(shape):
    n = len(shape)
    return pl.BlockSpec(shape, lambda *_: (0,) * n)


def _rows(tm, n):
    return pl.BlockSpec((tm, n), lambda i: (i, 0))


def _sds(shape, dtype):
    return jax.ShapeDtypeStruct(shape, dtype)


def _rms(x, g):
    r = lax.rsqrt(jnp.mean(x * x, axis=-1, keepdims=True) + EPS)
    return x * r * g


def _rms_bwd(x, g, dy):
    r = lax.rsqrt(jnp.mean(x * x, axis=-1, keepdims=True) + EPS)
    xh = x * r
    dxh = dy * g
    dx = r * (dxh - xh * jnp.mean(dxh * xh, axis=-1, keepdims=True))
    return dx, dy * xh


def _sigmoid(x):
    return 1.0 / (1.0 + jnp.exp(-x))


def _lane_masks():
    lane = lax.broadcasted_iota(jnp.int32, (1, LANES), 1)
    return lane < HALF


def _split_heads(a, lo):
    z = jnp.zeros_like(a)
    return [jnp.where(lo, a, z), jnp.where(lo, z, a)]


def _rope_consts():
    inv = np.zeros((8, LANES), np.float32)
    j = np.arange(ROPE // 2, dtype=np.float32)
    f = (1.0 / (ROPE_THETA ** (np.arange(0, ROPE, 2, dtype=np.float32) / ROPE))).astype(np.float32)
    inv[0, HALF:HALF + 16] = f
    inv[0, HALF + 16:HALF + 32] = f
    inv[1, HALF:HALF + 16] = -1.0
    inv[1, HALF + 16:HALF + 32] = 1.0
    del j
    return jnp.asarray(inv)


def _rope_tables(pos_f, consts):
    ang = pos_f * consts[0:1, :]
    sign = consts[1:2, :]
    c = jnp.where(sign != 0.0, jnp.cos(ang), 1.0)
    s = jnp.sin(ang) * sign
    return c, s


def _swap_halves(v, sign):
    lo = pltpu.roll(v, LANES - 16, axis=1)
    hi = pltpu.roll(v, 16, axis=1)
    return jnp.where(sign < 0.0, lo, jnp.where(sign > 0.0, hi, 0.0))


def _rope(x, c, s, sign):
    return x * c + _swap_halves(x, sign) * s


def _rope_t(dy, c, s, sign):
    return dy * c + _swap_halves(dy * s, sign)


def _layer0_in(x, pos, g_in, w_in, g_q, w_q, g_kv, w_kv):
    S = x.shape[0]
    consts = _rope_consts()

    def body(x_ref, pos_ref, c_ref, g_ref, w_ref, gq_ref, wq_ref, gkv_ref, wkv_ref,
             h_ref, cq_ref, ckv_ref, qm_ref, km_ref, vm_ref,
             qs_ref, kd_ref, vd_ref, gate_ref, cos_ref, sin_ref):
        h = _rms(x_ref[...], g_ref[...])
        h_ref[...] = h.astype(h_ref.dtype)
        z = _mm_nt(h, w_ref[...])
        cq = z[:, 0:256]
        ckv = z[:, 256:384]
        kpe = z[:, 384:512]
        cq_ref[...] = cq
        ckv_ref[...] = ckv
        qs_ref[...] = z[:, 512:1024].astype(qs_ref.dtype)
        kd_ref[...] = z[:, 1024:1536].astype(kd_ref.dtype)
        vd_ref[...] = z[:, 1536:2048].astype(vd_ref.dtype)
        gate_ref[...] = z[:, 2048:3072]
        cqn = _rms(cq, gq_ref[...])
        ckvn = _rms(ckv, gkv_ref[...])
        q = _mm_nt(cqn, wq_ref[...])
        kv = _mm(ckvn, wkv_ref[...])
        vm_ref[...] = kv[:, 1024:1536].astype(vm_ref.dtype)
        consts_v = c_ref[...]
        sign = consts_v[1:2, :]
        c, s = _rope_tables(pos_ref[...].astype(jnp.float32), consts_v)
        cos_ref[...] = c
        sin_ref[...] = s
        kpe_r = _rope(kpe, c, s, sign)
        for hd in range(N_MLA):
            sl = slice(LANES * hd, LANES * (hd + 1))
            qm_ref[:, sl] = _rope(q[:, sl], c, s, sign).astype(qm_ref.dtype)
            km_ref[:, sl] = (kv[:, sl] + kpe_r).astype(km_ref.dtype)

    outs = [
        ((S, D), MXU), ((S, 256), jnp.float32), ((S, 128), jnp.float32),
        ((S, 1024), MXU), ((S, 1024), MXU), ((S, 512), MXU), ((S, 512), MXU), ((S, 512), MXU), ((S, 512), MXU),
        ((S, 1024), jnp.float32), ((S, 128), jnp.float32), ((S, 128), jnp.float32),
    ]
    return _pcall(
        body, name="layer0_in", grid=(S // TOK,), semantics=("arbitrary",),
        in_specs=[_rows(TOK, D), _rows(TOK, 1), _full((8, LANES)), _full((1, D)), _full(w_in.shape), _full((1, 256)),
                  _full(w_q.shape), _full((1, 128)), _full(w_kv.shape)],
        out_specs=[_rows(TOK, s[1]) for s, _ in outs],
        out_shape=[_sds(s, d) for s, d in outs],
    )(x, pos, consts, g_in, w_in, g_q, w_q, g_kv, w_kv)


AUG = (HALF, 0)
ONE = (HALF + 8, 8)


def _data_lanes(idx, h):
    return (idx < HALF) if h == 0 else (idx >= HALF)


def _three_terms(x):
    hi = x.astype(MXU).astype(jnp.float32)
    mid = (x - hi).astype(MXU).astype(jnp.float32)
    lo = (x - hi - mid).astype(MXU).astype(jnp.float32)
    return hi, mid, lo


def _q_aug(qblk, lc, h, scale, lane):
    a = AUG[h]
    hi, mid, lo = _three_terms(lc)
    ones = ((lane >= a + 3) & (lane <= a + 5)).astype(jnp.float32)
    aug = jnp.where(lane == a, hi, jnp.where(lane == a + 1, mid, jnp.where(lane == a + 2, lo, ones)))
    return jnp.where(_data_lanes(lane, h), qblk * jnp.asarray(scale, qblk.dtype), aug.astype(qblk.dtype))


def _k_aug(kblk, lc, h, lane):
    a = AUG[h]
    hi, mid, lo = _three_terms(-lc)
    ones = ((lane >= a) & (lane <= a + 2)).astype(jnp.float32)
    aug = jnp.where(lane == a + 3, hi, jnp.where(lane == a + 4, mid, jnp.where(lane == a + 5, lo, ones)))
    return jnp.where(_data_lanes(lane, h), kblk, aug.astype(kblk.dtype))


def _lc_col(lc_ref, r0, rows, h):
    head = lax.broadcasted_iota(jnp.int32, (1, lc_ref.shape[1]), 1)
    return jnp.sum(jnp.where(head == 2 * pl.program_id(0) + h, lc_ref[pl.ds(r0, rows), :], 0.0), axis=1, keepdims=True)


def _attn_fwd_t(q, k, v, scale, *, split, name, lcc=None, plan=None):
    S = q.shape[0]
    npair = v.shape[1] // LANES
    W = 2 * LANES if split else LANES
    T = ATT
    CH = FWD_CHUNK * T
    assert S % CH == 0
    nq = S // T

    def body(*refs):
        if split:
            q_ref, k_ref, v_ref, o_ref, lse_ref, vt, acc, m_sc = refs
        else:
            q_ref, k_ref, v_ref, lcc_ref, o_ref, lse_ref, kaug, vt, acc, m_sc = refs
        lane = lax.broadcasted_iota(jnp.int32, (1, LANES), 1)
        sub = lax.broadcasted_iota(jnp.int32, (LANES, 1), 0)
        key_minus_qry = lax.broadcasted_iota(jnp.int32, (CH, T), 0) - lax.broadcasted_iota(jnp.int32, (CH, T), 1)

        def prep(i, c):
            r0 = pl.multiple_of(i * T, T)
            vblk = v_ref[pl.ds(r0, T), :].astype(jnp.float32)
            for h in (0, 1):
                vh = jnp.where(_data_lanes(lane, h), vblk, (lane == ONE[h]).astype(jnp.float32))
                vt[h, :, pl.ds(r0, T)] = vh.T.astype(vt.dtype)
                if not split:
                    kaug[h, pl.ds(r0, T), :] = _k_aug(k_ref[pl.ds(r0, T), :], _lc_col(lcc_ref, r0, T, h), h, lane)
            return c

        lax.fori_loop(0, nq, prep, 0)

        def queries(qi):
            q0 = pl.multiple_of(qi * T, T)
            qblk = q_ref[pl.ds(q0, T), :]
            if split:
                return (qblk[:, :LANES], qblk[:, LANES:])
            return tuple(_q_aug(qblk, _lc_col(lcc_ref, q0, T, h), h, scale, lane) for h in (0, 1))

        def scores(qs, c):
            k0 = pl.multiple_of(c * CH, CH)
            out = []
            for h in (0, 1):
                if split:
                    out.append(_mm_nt(k_ref[pl.ds(k0, CH), LANES * h:LANES * (h + 1)], qs[h]) * scale)
                else:
                    out.append(_mm_nt(kaug[h, pl.ds(k0, CH), :], qs[h]))
            return tuple(out)

        def q_block(qi, carry):
            qs, first_scores = carry[:2], carry[2:]
            q0 = pl.multiple_of(qi * T, T)
            acc[...] = jnp.zeros_like(acc)
            m_sc[...] = jnp.full(m_sc.shape, NEG, jnp.float32)

            def absorb(c, sts, masked):
                k0 = pl.multiple_of(c * CH, CH)
                for h in (0, 1):
                    st = sts[h]
                    if masked:
                        st = jnp.where(key_minus_qry <= q0 - k0, st, NEG)
                    m_old = m_sc[h:h + 1, :]
                    m_new = jnp.maximum(m_old, jnp.max(st, axis=0, keepdims=True))
                    alpha = jnp.exp(m_old - m_new)
                    pt = jnp.exp(st - m_new)
                    acc[h] = alpha * acc[h] + _mm(vt[h, :, pl.ds(k0, CH)], pt)
                    m_sc[h:h + 1, :] = m_new

            last = qi // FWD_CHUNK

            def pipelined(c, sts):
                nxt = scores(qs, c + 1)
                absorb(c, sts, False)
                return nxt

            sts = lax.fori_loop(0, last, pipelined, first_scores)
            qs_next = queries(jnp.minimum(qi + 1, nq - 1))
            nxt = qs_next + scores(qs_next, 0)
            absorb(last, sts, True)
            ot = None
            for h in (0, 1):
                a = acc[h]
                l = a[ONE[h]:ONE[h] + 1, :]
                oh = jnp.where(_data_lanes(sub, h), a * (1.0 / l), 0.0)
                ot = oh if ot is None else ot + oh
                lse_ref[0, h:h + 1, pl.ds(q0, T)] = m_sc[h:h + 1, :] + jnp.log(l)
            o_ref[pl.ds(q0, T), :] = ot.T
            return nxt

        qs0 = queries(0)
        lax.fori_loop(0, nq, q_block, qs0 + scores(qs0, 0))

    wide = pl.BlockSpec((S, W), lambda j: (0, j))
    slab = pl.BlockSpec((S, LANES), lambda j: (0, j))
    rows = pl.BlockSpec((1, 2, S), lambda j: (j, 0, 0))
    in_specs = [wide, wide, slab]
    args = [q, k, v]
    scratch = []
    if not split:
        in_specs.append(_full(lcc.shape))
        args.append(lcc)
        scratch.append(pltpu.VMEM((2, S, LANES), MXU))
    scratch += [pltpu.VMEM((2, LANES, S), MXU), pltpu.VMEM((2, LANES, T), jnp.float32), pltpu.VMEM((8, T), jnp.float32)]
    (o, lse), rode = _pcall_riding(
        body, plan, args, name=name, grid=(npair,), in_specs=in_specs, out_specs=[slab, rows],
        out_shape=[_sds((S, npair * LANES), jnp.float32), _sds((npair, 2, S), jnp.float32)], scratch_shapes=scratch)
    return o, lse, rode


def _attn_bwd_t(q, k, v, do, o, lse, scale, *, split, name, lcc=None, plan=None):
    S = q.shape[0]
    npair = v.shape[1] // LANES
    W = 2 * LANES if split else LANES
    T = ATT
    CH = BWD_CHUNK * T
    assert S % CH == 0
    nq = S // T

    def body(*refs):
        if split:
            (q_ref, k_ref, v_ref, do_ref, o_ref, lse_ref, dq_ref, dk_ref, dv_ref, dqt, delta, dk_acc, dv_acc) = refs
        else:
            (q_ref, k_ref, v_ref, do_ref, o_ref, lse_ref, lcc_ref, dq_ref, dk_ref, dv_ref, dlc_ref,
             dqt, delta, dk_acc, dv_acc, qaug, csum) = refs
        lane = lax.broadcasted_iota(jnp.int32, (1, LANES), 1)
        sub = lax.broadcasted_iota(jnp.int32, (LANES, 1), 0)
        key_minus_qry = lax.broadcasted_iota(jnp.int32, (T, CH), 0) - lax.broadcasted_iota(jnp.int32, (T, CH), 1)

        def prep(i, c):
            r0 = pl.multiple_of(i * T, T)
            prod_t = (do_ref[pl.ds(r0, T), :].astype(jnp.float32) * o_ref[pl.ds(r0, T), :]).T
            for h in (0, 1):
                delta[h:h + 1, pl.ds(r0, T)] = jnp.sum(jnp.where(_data_lanes(sub, h), prod_t, 0.0), axis=0, keepdims=True)
                dqt[h, :, pl.ds(r0, T)] = jnp.zeros((LANES, T), jnp.float32)
                if not split:
                    qaug[h, pl.ds(r0, T), :] = _q_aug(q_ref[pl.ds(r0, T), :], _lc_col(lcc_ref, r0, T, h), h, scale, lane)
            return c

        lax.fori_loop(0, nq, prep, 0)

        def keys(ki):
            k0 = pl.multiple_of(ki * T, T)
            kblk = k_ref[pl.ds(k0, T), :]
            if split:
                return (kblk[:, :LANES], kblk[:, LANES:])
            return tuple(_k_aug(kblk, _lc_col(lcc_ref, k0, T, h), h, lane) for h in (0, 1))

        def q_of(c, h):
            q0 = pl.multiple_of(c * CH, CH)
            if split:
                return q_ref[pl.ds(q0, CH), LANES * h:LANES * (h + 1)]
            return qaug[h, pl.ds(q0, CH), :]

        def scores(khs, c):
            out = []
            for h in (0, 1):
                st = _mm_nt(khs[h], q_of(c, h))
                out.append(st * scale if split else st)
            return tuple(out)

        def k_block(ki, carry):
            khs, first_scores = carry[:2], carry[2:]
            k0 = pl.multiple_of(ki * T, T)
            khts = [kh.astype(jnp.float32).T.astype(kh.dtype) for kh in khs]
            vhs = _split_heads(v_ref[pl.ds(k0, T), :], lane < HALF)
            dk_acc[...] = jnp.zeros_like(dk_acc)
            dv_acc[...] = jnp.zeros_like(dv_acc)

            def absorb(c, vals):
                q0 = pl.multiple_of(c * CH, CH)
                dos = _split_heads(do_ref[pl.ds(q0, CH), :], lane < HALF)
                visible = key_minus_qry <= q0 - k0
                for h in (0, 1):
                    dpt = _mm_nt(vhs[h], dos[h])
                    st = jnp.where(visible, vals[h], NEG)
                    pt = jnp.exp(st - lse_ref[0, h:h + 1, pl.ds(q0, CH)])
                    dv_acc[...] += _mm(pt, dos[h])
                    dst = pt * (dpt - delta[h:h + 1, pl.ds(q0, CH)])
                    dk_acc[h] += _mm(dst, q_of(c, h))
                    dqt[h, :, pl.ds(q0, CH)] += _mm(khts[h], dst)

            first = ki // BWD_CHUNK

            def pipelined(c, vals):
                nxt = scores(khs, c + 1)
                absorb(c, vals)
                return nxt

            vals = lax.fori_loop(first, S // CH - 1, pipelined, first_scores)
            kn = jnp.minimum(ki + 1, nq - 1)
            khs_next = keys(kn)
            nxt = khs_next + scores(khs_next, kn // BWD_CHUNK)
            absorb(S // CH - 1, vals)
            if split:
                dk_ref[pl.ds(k0, T), :LANES] = (dk_acc[0] * scale).astype(dk_ref.dtype)
                dk_ref[pl.ds(k0, T), LANES:] = (dk_acc[1] * scale).astype(dk_ref.dtype)
            else:
                dk_ref[pl.ds(k0, T), :] = jnp.where(lane < HALF, dk_acc[0], dk_acc[1]).astype(dk_ref.dtype)
                for h in (0, 1):
                    csum[h:h + 1, pl.ds(k0, T)] = dk_acc[h].T[AUG[h] + 3:AUG[h] + 4, :]
            dv_ref[pl.ds(k0, T), :] = dv_acc[...].astype(dv_ref.dtype)
            return nxt

        khs0 = keys(0)
        lax.fori_loop(0, nq, k_block, khs0 + scores(khs0, 0))

        def finish(i, c):
            r0 = pl.multiple_of(i * T, T)
            if split:
                for h in (0, 1):
                    dq_ref[pl.ds(r0, T), LANES * h:LANES * (h + 1)] = (dqt[h, :, pl.ds(r0, T)].T * scale).astype(dq_ref.dtype)
            else:
                d = jnp.where(sub < HALF, dqt[0, :, pl.ds(r0, T)], dqt[1, :, pl.ds(r0, T)])
                dq_ref[pl.ds(r0, T), :] = (d.T * scale).astype(dq_ref.dtype)
                for h in (0, 1):
                    dlc_ref[0, h:h + 1, pl.ds(r0, T)] = dqt[h, AUG[h]:AUG[h] + 1, pl.ds(r0, T)] - csum[h:h + 1, pl.ds(r0, T)]
            return c

        lax.fori_loop(0, nq, finish, 0)

    wide = pl.BlockSpec((S, W), lambda j: (0, j))
    slab = pl.BlockSpec((S, LANES), lambda j: (0, j))
    rows = pl.BlockSpec((1, 2, S), lambda j: (j, 0, 0))
    in_specs = [wide, wide, slab, slab, slab, rows]
    args = [q, k, v, do, o, lse]
    out_specs = [wide, wide, slab]
    out_shape = [_sds(q.shape, jnp.float32 if split else do.dtype), _sds(k.shape, jnp.float32 if split else do.dtype),
                 _sds(v.shape, do.dtype)]
    scratch = [pltpu.VMEM((2, LANES, S), jnp.float32), pltpu.VMEM((8, S), jnp.float32),
               pltpu.VMEM((2, T, LANES), jnp.float32), pltpu.VMEM((T, LANES), jnp.float32)]
    if not split:
        in_specs.append(_full(lcc.shape))
        args.append(lcc)
        out_specs.append(rows)
        out_shape.append(_sds((npair, 2, S), jnp.float32))
        scratch += [pltpu.VMEM((2, S, LANES), MXU), pltpu.VMEM((8, S), jnp.float32)]
    outs, rode = _pcall_riding(body, plan, args, name=name, grid=(npair,), in_specs=in_specs, out_specs=out_specs,
                               out_shape=out_shape, scratch_shapes=scratch)
    return (*outs, rode)


def _swa_bias(slope, shift):
    a = lax.broadcasted_iota(jnp.int32, (WINDOW, 2 * WINDOW), 0)
    c = lax.broadcasted_iota(jnp.int32, (WINDOW, 2 * WINDOW), 1)
    dist = a - c + shift
    return jnp.where((dist >= 0) & (dist < WINDOW), -slope * dist.astype(jnp.float32), NEG)


def _swa_scores(qh, kblk, bias):
    return _mm_nt(qh, kblk) * (HEAD ** -0.5) + bias


def _swa_stack(blk, lo):
    return jnp.concatenate(_split_heads(blk[:, :LANES], lo) + _split_heads(blk[:, LANES:], lo), axis=0)


def _swa_unstack(x, lo):
    r = x.shape[0] // 4
    return jnp.concatenate([jnp.where(lo, x[0:r], x[r:2 * r]), jnp.where(lo, x[2 * r:3 * r], x[3 * r:])], axis=1)


def _swa_per_head(ref, j, rows):
    quarter = lax.broadcasted_iota(jnp.int32, (4 * rows, 1), 0) // rows
    return jnp.where(quarter == 0, ref[4 * j], jnp.where(quarter == 1, ref[4 * j + 1],
                                                         jnp.where(quarter == 2, ref[4 * j + 2], ref[4 * j + 3])))


def _swa_fwd(q, kd, vd, sinks, slopes):
    S = q.shape[0]
    nkv = q.shape[1] // (2 * LANES)
    nb = S // WINDOW
    group = math.gcd(SWA_GROUP, nb)

    def body(sink_ref, slope_ref, q_ref, k_ref, v_ref, o_ref, lse_ref):
        j = pl.program_id(0)
        lo = _lane_masks()
        sink = _swa_per_head(sink_ref, j, WINDOW)
        biases = [jnp.concatenate([_swa_bias(slope_ref[4 * j + h], shift) for h in range(4)], axis=0)
                  for shift in (0, WINDOW)]

        def q_block(qi, c):
            q0 = pl.multiple_of(qi * WINDOW, WINDOW)
            k0 = pl.multiple_of(jnp.maximum(qi - 1, 0) * WINDOW, WINDOW)
            s = _swa_scores(_swa_stack(q_ref[pl.ds(q0, WINDOW), :], lo), k_ref[pl.ds(k0, 2 * WINDOW), :],
                            jnp.where(qi == 0, *biases))
            m = jnp.maximum(jnp.max(s, axis=1, keepdims=True), sink)
            p = jnp.exp(s - m)
            den = jnp.sum(p, axis=1, keepdims=True) + jnp.exp(sink - m)
            o_ref[pl.ds(q0, WINDOW), :] = _swa_unstack(_mm(p / den, v_ref[pl.ds(k0, 2 * WINDOW), :]), lo)
            lse = m + jnp.log(den)
            for h in range(4):
                lse_ref[h, pl.ds(q0, WINDOW), :] = lse[h * WINDOW:(h + 1) * WINDOW]
            return c

        def q_group(gi, c):
            for g in range(group):
                q_block(gi * group + g, c)
            return c

        lax.fori_loop(0, nb // group, q_group, 0)

    smem = pl.BlockSpec(memory_space=pltpu.SMEM)
    two = pl.BlockSpec((S, 2 * LANES), lambda j: (0, j))
    kv = pl.BlockSpec((S, LANES), lambda j: (0, 2 * j))
    return _pcall(
        body, name="swa_fwd", grid=(nkv,), semantics=("arbitrary",),
        in_specs=[smem, smem, two, kv, kv],
        out_specs=[two, pl.BlockSpec((4, S, 1), lambda j: (j, 0, 0))],
        out_shape=[_sds(q.shape, jnp.float32), _sds((4 * nkv, S, 1), jnp.float32)],
    )(sinks, slopes, q, kd, vd)


def _swa_bwd(q, kd, vd, do, o, lse, sinks, slopes, plan=None):
    S = q.shape[0]
    nkv = q.shape[1] // (2 * LANES)
    nb = S // WINDOW
    group = math.gcd(SWA_GROUP, nb)

    def body(sink_ref, slope_ref, q_ref, k_ref, v_ref, do_ref, o_ref, lse_ref,
             dq_ref, dk_ref, dv_ref, dsink_ref, dk_acc, dv_acc):
        j = pl.program_id(0)
        lo = _lane_masks()
        dk_acc[...] = jnp.zeros_like(dk_acc)
        dv_acc[...] = jnp.zeros_like(dv_acc)
        sink = _swa_per_head(sink_ref, j, WINDOW)
        biases = [jnp.concatenate([_swa_bias(slope_ref[4 * j + h], shift) for h in range(4)], axis=0)
                  for shift in (0, WINDOW)]

        def q_block(qi, carry):
            q0 = pl.multiple_of(qi * WINDOW, WINDOW)
            k0 = pl.multiple_of(jnp.maximum(qi - 1, 0) * WINDOW, WINDOW)
            q4 = _swa_stack(q_ref[pl.ds(q0, WINDOW), :], lo)
            do4 = _swa_stack(do_ref[pl.ds(q0, WINDOW), :], lo)
            oblk = o_ref[pl.ds(q0, WINDOW), :]
            o4 = jnp.concatenate([oblk[:, :LANES], oblk[:, :LANES], oblk[:, LANES:], oblk[:, LANES:]], axis=0)
            kblk = k_ref[pl.ds(k0, 2 * WINDOW), :]
            vblk = v_ref[pl.ds(k0, 2 * WINDOW), :]
            lse = jnp.concatenate([lse_ref[h, pl.ds(q0, WINDOW), :] for h in range(4)], axis=0)
            p = jnp.exp(_swa_scores(q4, kblk, jnp.where(qi == 0, *biases)) - lse)
            delta = jnp.sum(do4.astype(jnp.float32) * o4, axis=1, keepdims=True)
            dv_acc[pl.ds(k0, 2 * WINDOW), :] += _mm_tn(p, do4)
            ds = p * (_mm_nt(do4, vblk) - delta)
            dq_ref[pl.ds(q0, WINDOW), :] = _swa_unstack(_mm(ds, kblk) * (HEAD ** -0.5), lo).astype(dq_ref.dtype)
            dk_acc[pl.ds(k0, 2 * WINDOW), :] += _mm_tn(ds, q4) * (HEAD ** -0.5)
            dsk = -jnp.exp(sink - lse) * delta
            return tuple(carry[h] + jnp.sum(dsk[h * WINDOW:(h + 1) * WINDOW], axis=0, keepdims=True)
                         for h in range(4))

        def q_group(gi, carry):
            for g in range(group):
                carry = q_block(gi * group + g, carry)
            return carry

        zero = jnp.zeros((1, 1), jnp.float32)
        dsinks = lax.fori_loop(0, nb // group, q_group, (zero,) * 4)
        dk_ref[:, :LANES] = dk_acc[...].astype(dk_ref.dtype)
        dk_ref[:, LANES:] = jnp.zeros((S, LANES), dk_ref.dtype)
        dv_ref[:, :LANES] = dv_acc[...].astype(dv_ref.dtype)
        dv_ref[:, LANES:] = jnp.zeros((S, LANES), dv_ref.dtype)
        r = lax.broadcasted_iota(jnp.int32, (8, LANES), 0)
        dsink_ref[0] = jnp.where(r == 0, dsinks[0], jnp.where(r == 1, dsinks[1], jnp.where(r == 2, dsinks[2],
                                 jnp.where(r == 3, dsinks[3], 0.0))))

    smem = pl.BlockSpec(memory_space=pltpu.SMEM)
    two = pl.BlockSpec((S, 2 * LANES), lambda j: (0, j))
    kv = pl.BlockSpec((S, LANES), lambda j: (0, 2 * j))
    outs, rode = _pcall_riding(
        body, plan, [sinks, slopes, q, kd, vd, do, o, lse], name="swa_bwd", grid=(nkv,),
        in_specs=[smem, smem, two, kv, kv, two, two, pl.BlockSpec((4, S, 1), lambda j: (j, 0, 0))],
        out_specs=[two, two, two, pl.BlockSpec((1, 8, LANES), lambda j: (j, 0, 0))],
        out_shape=[_sds(q.shape, do.dtype), _sds(kd.shape, do.dtype), _sds(vd.shape, do.dtype),
                   _sds((nkv, 8, LANES), jnp.float32)],
        scratch_shapes=[pltpu.VMEM((S, LANES), jnp.float32), pltpu.VMEM((S, LANES), jnp.float32)])
    return (*outs, rode)


def _log_steps(S):
    k, out = 1, []
    while k < S:
        out.append(k)
        k *= 2
    return out


def _forget_fwd(f_row, b_col):
    S = f_row.shape[1]

    def body(f_ref, b_ref, lc_ref):
        x = f_ref[...] + b_ref[...]
        lc = jnp.minimum(x, 0.0) - jnp.log(1.0 + jnp.exp(-jnp.abs(x)))
        idx = lax.broadcasted_iota(jnp.int32, lc.shape, 1)
        for k in _log_steps(S):
            lc = lc + jnp.where(idx >= k, pltpu.roll(lc, k, axis=1), 0.0)
        lc_ref[...] = lc

    return _pcall(body, name="forget_fwd", out_shape=_sds(f_row.shape, jnp.float32))(f_row, b_col)


def _forget_bwd(dlc_row, f_row, b_col):
    S = f_row.shape[1]

    def body(d_ref, f_ref, b_ref, df_ref, db_ref):
        g = d_ref[...]
        idx = lax.broadcasted_iota(jnp.int32, g.shape, 1)
        for k in _log_steps(S):
            g = g + jnp.where(idx < S - k, pltpu.roll(g, S - k, axis=1), 0.0)
        x = f_ref[...] + b_ref[...]
        df = g * _sigmoid(-x)
        df_ref[...] = df
        db_ref[...] = jnp.sum(df, axis=1, keepdims=True)

    return _pcall(body, name="forget_bwd",
                  out_shape=[_sds(f_row.shape, jnp.float32), _sds((f_row.shape[0], 1), jnp.float32)])(dlc_row, f_row, b_col)


def _layer0_out_layer1_in(x, o_m, o_s, gate, w_out, g1, w_in1):
    S = x.shape[0]

    def body(x_ref, om_ref, os_ref, gate_ref, wo_ref, g_ref, w_ref,
             x1_ref, h_ref, q_ref, k_ref, v_ref, g1_ref, f_ref):
        gt = gate_ref[...]
        sg = gt * _sigmoid(gt)
        um = om_ref[...] * sg[:, :512]
        us = os_ref[...] * sg[:, 512:]
        x1 = x_ref[...] + _mm(um, wo_ref[0:512, :]) + _mm(us, wo_ref[512:1024, :])
        x1_ref[...] = x1
        h = _rms(x1, g_ref[...])
        h_ref[...] = h.astype(h_ref.dtype)
        z = _mm_nt(h, w_ref[...])
        q_ref[...] = z[:, 0:1024].astype(q_ref.dtype)
        k_ref[...] = z[:, 1024:2048].astype(k_ref.dtype)
        v_ref[...] = z[:, 2048:3072].astype(v_ref.dtype)
        g1_ref[...] = z[:, 3072:4096]
        f_ref[...] = z[:, 4096:4224]

    outs = [((S, D), jnp.float32), ((S, D), MXU), ((S, D), MXU), ((S, D), MXU), ((S, D), MXU),
            ((S, D), jnp.float32), ((S, LANES), jnp.float32)]
    return _pcall(
        body, name="layer0_out_layer1_in", grid=(S // TOK,), semantics=("arbitrary",),
        in_specs=[_rows(TOK, D), _rows(TOK, 512), _rows(TOK, 512), _rows(TOK, D), _full((D, D)), _full((1, D)),
                  _full(w_in1.shape)],
        out_specs=[_rows(TOK, s[1]) for s, _ in outs],
        out_shape=[_sds(s, d) for s, d in outs],
    )(x, o_m, o_s, gate, w_out, g1, w_in1)


def _head(x1, o1, gate1, w_out1, g_f, target):
    S = x1.shape[0]

    def body(x1_ref, o_ref, gate_ref, wo_ref, g_ref, t_ref,
             loss_ref, dgf_ref, dwo_ref, dx2_ref, do_ref, dgate_ref):
        i = pl.program_id(0)
        gt = gate_ref[...]
        sig = _sigmoid(gt)
        sg = gt * sig
        o = o_ref[...]
        u = o * sg
        x2 = x1_ref[...] + _mm(u, wo_ref[...])
        g = g_ref[...]
        y = _rms(x2, g)
        err = y - t_ref[...]
        part = 0.5 * jnp.sum(jnp.mean(err * err, axis=-1, keepdims=True), axis=0, keepdims=True)
        dy = err * (1.0 / D)
        dx2, dg_rows = _rms_bwd(x2, g, dy)
        dx2_ref[...] = dx2
        du = _mm_nt(dx2, wo_ref[...])
        do_ref[...] = (du * sg).astype(do_ref.dtype)
        dgate_ref[...] = (du * o * (sig * (1.0 + gt * (1.0 - sig)))).astype(dgate_ref.dtype)

        @pl.when(i == 0)
        def _():
            loss_ref[...] = jnp.zeros_like(loss_ref)
            dgf_ref[...] = jnp.zeros_like(dgf_ref)
            dwo_ref[...] = jnp.zeros_like(dwo_ref)

        loss_ref[...] += jnp.broadcast_to(part, loss_ref.shape)
        dgf_ref[...] += jnp.sum(dg_rows, axis=0, keepdims=True)
        dwo_ref[...] += _mm_tn(u, dx2)

    outs = [((S, D), jnp.float32), ((S, D), MXU), ((S, D), MXU)]
    return _pcall(
        body, name="head", grid=(S // TOK,), semantics=("arbitrary",),
        in_specs=[_rows(TOK, D), _rows(TOK, D), _rows(TOK, D), _full((D, D)), _full((1, D)), _rows(TOK, D)],
        out_specs=[_full((8, LANES)), _full((1, D)), _full((D, D))] + [_rows(TOK, D) for _ in outs],
        out_shape=[_sds((8, LANES), jnp.float32), _sds((1, D), jnp.float32), _sds((D, D), jnp.float32)]
        + [_sds(s, d) for s, d in outs],
    )(x1, o1, gate1, w_out1, g_f, target)


def _layer1_in_bwd(dq, dk, dv, dgate1, df, x1, dx2, g1, w_in1, gate0, o_m, o_s, w_out0):
    S = x1.shape[0]

    def body(dq_ref, dk_ref, dv_ref, dg1_ref, df_ref, x1_ref, dx2_ref, g_ref, w_ref, gate_ref, om_ref, os_ref,
             wo_ref, dz_ref, dx1_ref, dgn_ref, dwo_ref, dom_ref, dos_ref, dgate_ref):
        i = pl.program_id(0)
        dz_ref[:, 0:1024] = dq_ref[...]
        dz_ref[:, 1024:2048] = dk_ref[...]
        dz_ref[:, 2048:3072] = dv_ref[...]
        dz_ref[:, 3072:4096] = dg1_ref[...]
        dz_ref[:, 4096:4224] = df_ref[...]
        dh = _mm(dz_ref[...], w_ref[...])
        g = g_ref[...]
        dxn, dg_rows = _rms_bwd(x1_ref[...], g, dh)
        dx1 = dx2_ref[...] + dxn
        dx1_ref[...] = dx1
        du = _mm_nt(dx1, wo_ref[...])
        gt = gate_ref[...]
        sig = _sigmoid(gt)
        sg = gt * sig
        dsg = sig * (1.0 + gt * (1.0 - sig))
        dom_ref[...] = (du[:, :512] * sg[:, :512]).astype(dom_ref.dtype)
        dos_ref[...] = (du[:, 512:] * sg[:, 512:]).astype(dos_ref.dtype)
        dgate_ref[:, :512] = (du[:, :512] * om_ref[...] * dsg[:, :512]).astype(dgate_ref.dtype)
        dgate_ref[:, 512:] = (du[:, 512:] * os_ref[...] * dsg[:, 512:]).astype(dgate_ref.dtype)

        @pl.when(i == 0)
        def _():
            dgn_ref[...] = jnp.zeros_like(dgn_ref)
            dwo_ref[...] = jnp.zeros_like(dwo_ref)

        dgn_ref[...] += jnp.sum(dg_rows, axis=0, keepdims=True)
        dwo_ref[0:512, :] += _mm_tn(om_ref[...] * sg[:, :512], dx1)
        dwo_ref[512:1024, :] += _mm_tn(os_ref[...] * sg[:, 512:], dx1)

    return _pcall(
        body, name="layer1_in_bwd", grid=(S // TOK,), semantics=("arbitrary",),
        in_specs=[_rows(TOK, D), _rows(TOK, D), _rows(TOK, D), _rows(TOK, D), _rows(TOK, LANES), _rows(TOK, D),
                  _rows(TOK, D), _full((1, D)), _full(w_in1.shape), _rows(TOK, D), _rows(TOK, 512), _rows(TOK, 512),
                  _full((D, D))],
        out_specs=[_rows(TOK, 4224), _rows(TOK, D), _full((1, D)), _full((D, D)), _rows(TOK, 512), _rows(TOK, 512),
                   _rows(TOK, D)],
        out_shape=[_sds((S, 4224), MXU), _sds((S, D), jnp.float32), _sds((1, D), jnp.float32), _sds((D, D), jnp.float32),
                   _sds((S, 512), MXU), _sds((S, 512), MXU), _sds((S, D), MXU)],
    )(dq, dk, dv, dgate1, df, x1, dx2, g1, w_in1, gate0, o_m, o_s, w_out0)


def _layer0_in_bwd(dqm, dkm, dvm, dqs, dkd, dvd, dgate0, cos, sin, cq, ckv, x, dx1, g_in, w_in, g_q, w_q, g_kv, w_kv):
    S = x.shape[0]
    consts = _rope_consts()

    def body(dqm_ref, dkm_ref, dvm_ref, dqs_ref, dkd_ref, dvd_ref, dgate_ref, cos_ref, sin_ref, c_ref, cq_ref, ckv_ref,
             x_ref, dx1_ref, g_ref, w_ref, gq_ref, wq_ref, gkv_ref, wkv_ref,
             dx_ref, dz_ref, dgin_ref, dgq_ref, dgkv_ref, dwq_ref, dwkv_ref, dqu_ref, dkvu_ref):
        i = pl.program_id(0)
        lo = _lane_masks()
        sign = c_ref[...][1:2, :]
        c = cos_ref[...]
        s = sin_ref[...]
        dkpe = None
        for hd in range(N_MLA):
            sl = slice(LANES * hd, LANES * (hd + 1))
            dqu_ref[:, sl] = _rope_t(dqm_ref[:, sl], c, s, sign).astype(dqu_ref.dtype)
            dkh = dkm_ref[:, sl]
            dkvu_ref[:, sl] = jnp.where(lo, dkh, 0.0).astype(dkvu_ref.dtype)
            dkpe = dkh if dkpe is None else dkpe + dkh
        dkvu_ref[:, 1024:1536] = dvm_ref[...]
        dkpe = _rope_t(jnp.where(lo, 0.0, dkpe), c, s, sign)
        dcqn = _mm(dqu_ref[...], wq_ref[...])
        dckvn = _mm_nt(dkvu_ref[...], wkv_ref[...])
        gq = gq_ref[...]
        gkv = gkv_ref[...]
        dcq, dgq_rows = _rms_bwd(cq_ref[...], gq, dcqn)
        dckv, dgkv_rows = _rms_bwd(ckv_ref[...], gkv, dckvn)
        dz_ref[:, 0:256] = dcq.astype(dz_ref.dtype)
        dz_ref[:, 256:384] = dckv.astype(dz_ref.dtype)
        dz_ref[:, 384:512] = dkpe.astype(dz_ref.dtype)
        dz_ref[:, 512:1024] = dqs_ref[...]
        dz_ref[:, 1024:1536] = dkd_ref[...]
        dz_ref[:, 1536:2048] = dvd_ref[...]
        dz_ref[:, 2048:3072] = dgate_ref[...]
        dh = _mm(dz_ref[...], w_ref[...])
        g = g_ref[...]
        dxn, dg_rows = _rms_bwd(x_ref[...], g, dh)
        dx_ref[...] = dx1_ref[...] + dxn

        @pl.when(i == 0)
        def _():
            dgin_ref[...] = jnp.zeros_like(dgin_ref)
            dgq_ref[...] = jnp.zeros_like(dgq_ref)
            dgkv_ref[...] = jnp.zeros_like(dgkv_ref)
            dwq_ref[...] = jnp.zeros_like(dwq_ref)
            dwkv_ref[...] = jnp.zeros_like(dwkv_ref)

        dgin_ref[...] += jnp.sum(dg_rows, axis=0, keepdims=True)
        dgq_ref[...] += jnp.sum(dgq_rows, axis=0, keepdims=True)
        dgkv_ref[...] += jnp.sum(dgkv_rows, axis=0, keepdims=True)
        dwq_ref[...] += _mm_tn(dqu_ref[...], _rms(cq_ref[...], gq))
        dwkv_ref[...] += _mm_tn(_rms(ckv_ref[...], gkv), dkvu_ref[...])

    return _pcall(
        body, name="layer0_in_bwd", grid=(S // TOK,), semantics=("arbitrary",),
        in_specs=[_rows(TOK, 1024), _rows(TOK, 1024), _rows(TOK, 512), _rows(TOK, 512), _rows(TOK, 512), _rows(TOK, 512),
                  _rows(TOK, D), _rows(TOK, LANES), _rows(TOK, LANES), _full((8, LANES)), _rows(TOK, 256), _rows(TOK, 128),
                  _rows(TOK, D), _rows(TOK, D), _full((1, D)), _full(w_in.shape), _full((1, 256)), _full(w_q.shape),
                  _full((1, 128)), _full(w_kv.shape)],
        out_specs=[_rows(TOK, D), _rows(TOK, 3072), _full((1, D)), _full((1, 256)), _full((1, 128)), _full(w_q.shape),
                   _full(w_kv.shape)],
        out_shape=[_sds((S, D), jnp.float32), _sds((S, 3072), MXU), _sds((1, D), jnp.float32), _sds((1, 256), jnp.float32),
                   _sds((1, 128), jnp.float32), _sds(w_q.shape, jnp.float32), _sds(w_kv.shape, jnp.float32)],
        scratch_shapes=[pltpu.VMEM((TOK, 1024), MXU), pltpu.VMEM((TOK, 1536), MXU)],
    )(dqm, dkm, dvm, dqs, dkd, dvd, dgate0, cos, sin, consts, cq, ckv, x, dx1, g_in, w_in, g_q, w_q, g_kv, w_kv)


def _wgrad(a, b, name):
    S, M = a.shape
    N = b.shape[1]
    tm = next(t for t in range(WG_ROWS, 0, -LANES) if M % t == 0)
    tn = N if N <= 1024 else 512
    tk = min(WG_TOK, S)

    def body(a_ref, b_ref, o_ref):
        @pl.when(pl.program_id(2) == 0)
        def _():
            o_ref[...] = jnp.zeros_like(o_ref)

        o_ref[...] += _mm_tn(a_ref[...], b_ref[...])

    return _pcall(
        body, name=name, grid=(M // tm, N // tn, S // tk), semantics=("parallel", "parallel", "arbitrary"),
        in_specs=[pl.BlockSpec((tk, tm), lambda m, n, k: (k, m)), pl.BlockSpec((tk, tn), lambda m, n, k: (k, n))],
        out_specs=pl.BlockSpec((tm, tn), lambda m, n, k: (m, n)),
        out_shape=_sds((M, N), jnp.float32),
    )(a, b)


def _adamw(w, g, m, v, name):
    shape = w.shape
    R, C = (int(np.prod(shape[:-1])), shape[-1])
    w2, g2, m2, v2 = (t.reshape(R, C) for t in (w, g, m, v))
    fits = [t for t in range(8, ADAM_TILE_BYTES // (4 * C) + 1, 8) if R % t == 0]
    tr = max(fits) if fits else R
    tc = C if (tr * C * 4 <= ADAM_TILE_BYTES or C % 256) else 256

    def body(w_ref, g_ref, m_ref, v_ref, d_ref, nm_ref, nv_ref):
        gg = g_ref[...]
        nm = B1 * m_ref[...] + (1.0 - B1) * gg
        nv = B2 * v_ref[...] + (1.0 - B2) * (gg * gg)
        m_hat = nm / (1.0 - B1 ** STEP)
        v_hat = nv / (1.0 - B2 ** STEP)
        d_ref[...] = -LR * (m_hat / (jnp.sqrt(v_hat) + AEPS) + WD * w_ref[...])
        nm_ref[...] = nm
        nv_ref[...] = nv

    spec = pl.BlockSpec((tr, tc), lambda i, j: (i, j))
    d, nm, nv = _pcall(
        body, name=name, grid=(R // tr, C // tc), semantics=("parallel", "parallel"),
        in_specs=[spec] * 4, out_specs=[spec] * 3, out_shape=[_sds((R, C), jnp.float32)] * 3,
    )(w2, g2, m2, v2)
    return d.reshape(shape), nm.reshape(shape), nv.reshape(shape)


def _sum_leading(a, name):
    n, R, C = a.shape
    tr = SUM_ROWS if R % SUM_ROWS == 0 else R

    def body(a_ref, o_ref):
        acc = a_ref[0]
        for i in range(1, n):
            acc = acc + a_ref[i]
        o_ref[...] = acc

    return _pcall(
        body, name=name, grid=(R // tr,), semantics=("parallel",),
        in_specs=[pl.BlockSpec((n, tr, C), lambda i: (0, i, 0))], out_specs=_rows(tr, C),
        out_shape=_sds((R, C), a.dtype),
    )(a)


def _add_halves(g, c, b, name, out_dtype):
    n, _, R, C = g.shape
    tr = SUM_ROWS if R % SUM_ROWS == 0 else R

    def body(c_ref, a_ref, b_ref, o_ref):
        o_ref[...] = (a_ref[0] + b_ref[...]).astype(o_ref.dtype)

    spec = pl.BlockSpec((1, tr, C), lambda k, i, c_ref: (k, i, 0))
    grid_spec = pltpu.PrefetchScalarGridSpec(
        num_scalar_prefetch=1, grid=(n, R // tr),
        in_specs=[pl.BlockSpec((1, 1, tr, C), lambda k, i, c_ref: (k, c_ref[0], i, 0)), spec], out_specs=spec)
    return _pcall(body, name=name, semantics=("parallel", "parallel"), grid_spec=grid_spec,
                  out_shape=_sds(b.shape, out_dtype))(c.reshape(1).astype(jnp.int32), g, b)


def _total_sum(mine, theirs, recv, name):
    R, C = mine.shape
    n = recv.shape[0]
    tr = SUM_ROWS if R % SUM_ROWS == 0 else R

    def body(a_ref, b_ref, r_ref, o_ref):
        acc = a_ref[...] + b_ref[...]
        for i in range(n):
            acc = acc + r_ref[i].astype(jnp.float32)
        o_ref[...] = acc

    return _pcall(
        body, name=name, grid=(R // tr,), semantics=("parallel",),
        in_specs=[_rows(tr, C), _rows(tr, C), pl.BlockSpec((n, tr, C), lambda i: (0, i, 0))], out_specs=_rows(tr, C),
        out_shape=_sds((R, C), jnp.float32),
    )(mine, theirs, recv)


def _place():
    return lax.axis_index("x"), lax.axis_index("y"), lax.axis_index("c")


class _Plan:
    def __init__(self, arrays, out_shape, scratch, start, finish, middle=None):
        self.arrays, self.out_shape, self.scratch = list(arrays), list(out_shape), list(scratch)
        self.start, self.finish, self.middle = start, finish, middle


def _gather8_plan(block):
    R, C = block.shape

    def parts(ins, outs, sems):
        (x_ref,), (out_ref,), (send_sems, recv_sems) = ins, outs, sems
        x, y, c = _place()
        me, sibling = (x, y, c), (x, y, 1 - c)
        chips = [(1 - x, y), (x, 1 - y), (1 - x, 1 - y)]

        def copy(k, blk, to, src=None):
            slot = out_ref.at[4 * blk[0] + 2 * blk[1] + blk[2]]
            return pltpu.make_async_remote_copy(
                src_ref=slot if src is None else src, dst_ref=slot,
                send_sem=send_sems.at[k], recv_sem=recv_sems.at[k], device_id=to, device_id_type=MESH_ID)

        def first():
            return [copy(0, me, sibling, src=x_ref)] + [copy(1 + j, me, (*chip, c), src=x_ref) for j, chip in enumerate(chips)]

        def passed():
            return [copy(4 + j, (*chip, c), sibling) for j, chip in enumerate(chips)]

        def arrivals():
            return [copy(1 + j, (*chip, c), me) for j, chip in enumerate(chips)]

        def late():
            return [copy(0, sibling, me)] + [copy(4 + j, (*chip, 1 - c), me) for j, chip in enumerate(chips)]

        return first, passed, arrivals, late

    def start(ins, outs, sems):
        for cp in parts(ins, outs, sems)[0]():
            cp.start()

    def middle(ins, outs, sems):
        _, passed, arrivals, _ = parts(ins, outs, sems)
        for arrived, forward in zip(arrivals(), passed()):
            arrived.wait_recv()
            forward.start()

    def finish(ins, outs, sems):
        first, passed, _, late = parts(ins, outs, sems)
        for cp in late():
            cp.wait_recv()
        for cp in first() + passed():
            cp.wait_send()

    return _Plan([block], [_sds((8, R, C), block.dtype)], [pltpu.SemaphoreType.DMA((7,)), pltpu.SemaphoreType.DMA((7,))],
                 start, finish, middle)


def _fill_own_slot(gathered, block):
    x, y, c = _place()
    return lax.dynamic_update_index_in_dim(gathered, block, 4 * x + 2 * y + c, 0)


def _started_and_waited(arrays, out_shape, n, copies):
    def start(ins, outs, sems):
        for cp in copies(ins, outs, sems):
            cp.start()

    def finish(ins, outs, sems):
        for cp in copies(ins, outs, sems):
            cp.wait()

    return _Plan(arrays, out_shape, [pltpu.SemaphoreType.DMA((n,)), pltpu.SemaphoreType.DMA((n,))], start, finish)


def _pair_swap_plan(g):
    n = g.shape[0]

    def copies(ins, outs, sems):
        (g_ref,), (out_ref,), (send_sems, recv_sems) = ins, outs, sems
        x, y, c = _place()
        return [pltpu.make_async_remote_copy(src_ref=g_ref.at[k, 1 - c], dst_ref=out_ref.at[k], send_sem=send_sems.at[k],
                                             recv_sem=recv_sems.at[k], device_id=(x, y, 1 - c), device_id_type=MESH_ID)
                for k in range(n)]

    return _started_and_waited([g], [_sds((n,) + g.shape[2:], g.dtype)], n, copies)


def _chip_exchange_plan(p):
    def copies(ins, outs, sems):
        (p_ref,), (out_ref,), (send_sems, recv_sems) = ins, outs, sems
        x, y, c = _place()
        chips = [(1 - x, y), (x, 1 - y), (1 - x, 1 - y)]
        return [pltpu.make_async_remote_copy(
            src_ref=p_ref.at[2 * cx + cy], dst_ref=out_ref.at[j], send_sem=send_sems.at[j],
            recv_sem=recv_sems.at[j], device_id=(cx, cy, c), device_id_type=MESH_ID)
            for j, (cx, cy) in enumerate(chips)]

    return _started_and_waited([p], [_sds((3,) + p.shape[1:], p.dtype)], 3, copies)


def _pair_exchange_plan(t):
    def copies(ins, outs, sems):
        (t_ref,), (out_ref,), (send_sems, recv_sems) = ins, outs, sems
        x, y, c = _place()
        return [pltpu.make_async_remote_copy(src_ref=t_ref, dst_ref=out_ref, send_sem=send_sems.at[0], recv_sem=recv_sems.at[0],
                                             device_id=(x, y, 1 - c), device_id_type=MESH_ID)]

    return _started_and_waited([t], [_sds(t.shape, t.dtype)], 1, copies)


def _both_plans(a, b):
    na, ma, sa = len(a.arrays), len(a.out_shape), len(a.scratch)

    def phase(name):
        fa, fb = getattr(a, name), getattr(b, name)
        if fa is None and fb is None:
            return None

        def run(ins, outs, sems):
            if fa is not None:
                fa(ins[:na], outs[:ma], sems[:sa])
            if fb is not None:
                fb(ins[na:], outs[ma:], sems[sa:])
        return run

    return _Plan(a.arrays + b.arrays, a.out_shape + b.out_shape, a.scratch + b.scratch,
                 phase("start"), phase("finish"), phase("middle"))


ANY_SPEC = pl.BlockSpec(memory_space=pl.ANY)


def _run_plan(plan, name):
    n_in, n_out = len(plan.arrays), len(plan.out_shape)

    def body(*refs):
        ins, outs, sems = refs[:n_in], refs[n_in:n_in + n_out], refs[n_in + n_out:]
        plan.start(ins, outs, sems)
        if plan.middle is not None:
            plan.middle(ins, outs, sems)
        plan.finish(ins, outs, sems)

    return _pcall(body, name=name, in_specs=[ANY_SPEC] * n_in, out_specs=[ANY_SPEC] * n_out, out_shape=plan.out_shape,
                  scratch_shapes=plan.scratch)(*plan.arrays)


def _pcall_riding(body, plan, args, *, name, grid, in_specs, out_specs, out_shape, scratch_shapes):
    if plan is None:
        outs = _pcall(body, name=name, grid=grid, semantics=("arbitrary",), in_specs=in_specs, out_specs=out_specs,
                      out_shape=out_shape, scratch_shapes=scratch_shapes)(*args)
        return list(outs), None
    n_in, n_out, n_s = len(args), len(out_shape), len(scratch_shapes)
    p_in, p_out = len(plan.arrays), len(plan.out_shape)
    steps = grid[0]

    def riding(*refs):
        ins, pins = refs[:n_in], refs[n_in:n_in + p_in]
        o0 = n_in + p_in
        outs, pouts = refs[o0:o0 + n_out], refs[o0 + n_out:o0 + n_out + p_out]
        s0 = o0 + n_out + p_out
        scr, sems = refs[s0:s0 + n_s], refs[s0 + n_s:]
        j = pl.program_id(0)

        @pl.when(j == 0)
        def _():
            plan.start(pins, pouts, sems)

        if plan.middle is not None:
            @pl.when(j == steps // 2)
            def _():
                plan.middle(pins, pouts, sems)

        body(*ins, *outs, *scr)

        @pl.when(j == steps - 1)
        def _():
            plan.finish(pins, pouts, sems)

    res = _pcall(riding, name=name, grid=grid, semantics=("arbitrary",), in_specs=list(in_specs) + [ANY_SPEC] * p_in,
                 out_specs=list(out_specs) + [ANY_SPEC] * p_out, out_shape=list(out_shape) + plan.out_shape,
                 scratch_shapes=list(scratch_shapes) + plan.scratch)(*args, *plan.arrays)
    return list(res[:n_out]), list(res[n_out:])


class _RowSeq:
    def __init__(self, pieces):
        self.pieces = list(pieces)

    def rows(self, a, b):
        out, off = [], 0
        for p in self.pieces:
            lo, hi = max(a, off), min(b, off + p.shape[0])
            if lo < hi:
                out.append(p[lo - off:hi - off])
            off += p.shape[0]
        return out

    def array(self):
        return jnp.concatenate(self.pieces, axis=0)


def _row_seq(w):
    return w if isinstance(w, _RowSeq) else _RowSeq([w])


def _prep_w_in0(wt):
    wt = _row_seq(wt)
    one = wt.pieces[0]
    z32 = [jnp.zeros((32, one.shape[1]), one.dtype)]
    k0, k1 = wt.rows(928, 992), wt.rows(992, 1056)
    v0, v1 = wt.rows(1056, 1120), wt.rows(1120, 1184)
    return jnp.concatenate(wt.rows(0, 384) + z32 + z32 + wt.rows(384, 416) + z32 + wt.rows(416, 928)
                           + k0 * 4 + k1 * 4 + v0 * 4 + v1 * 4 + wt.rows(1184, 2208), axis=0)


def _fold_w_in0(d):
    def fold(blk):
        b = blk.reshape(8, 64, blk.shape[1])
        return jnp.concatenate([b[0] + b[1] + b[2] + b[3], b[4] + b[5] + b[6] + b[7]], axis=0)
    return _RowSeq([d[0:384], d[448:480], d[512:1024], fold(d[1024:1536]), fold(d[1536:2048]), d[2048:3072]])


def _prep_w_q(wt):
    return jnp.pad(wt.reshape(N_MLA, 96, Q_RANK), ((0, 0), (0, 32), (0, 0))).reshape(1024, Q_RANK)


def _fold_w_q(d):
    return d.reshape(N_MLA, 128, Q_RANK)[:, :96].reshape(768, Q_RANK)


def _prep_w_kv(w):
    w3 = w.reshape(KV_RANK, N_MLA, 128)
    kk = jnp.pad(w3[:, :, :64], ((0, 0), (0, 0), (0, 64))).reshape(KV_RANK, 1024)
    return jnp.concatenate([kk, w3[:, :, 64:].reshape(KV_RANK, 512)], axis=1)


def _fold_w_kv(d):
    kk = d[:, :1024].reshape(KV_RANK, N_MLA, 128)[:, :, :64]
    vv = d[:, 1024:].reshape(KV_RANK, N_MLA, 64)
    return jnp.concatenate([kk, vv], axis=2).reshape(KV_RANK, 1024)


def _prep_w_in1(wt):
    wt = _row_seq(wt)
    one = wt.pieces[0]
    return jnp.concatenate(wt.rows(0, 3072) + wt.rows(3088, 4112) + wt.rows(3072, 3088)
                           + [jnp.zeros((112, one.shape[1]), one.dtype)], axis=0)


def _fold_w_in1(d):
    return _RowSeq([d[0:3072], d[4096:4112], d[3072:4096]])


class _Alone:
    def __init__(self, w_out0, o_g_in, w_in1, w_out1):
        self.layer1 = (w_out0, o_g_in, w_in1, w_out1)

    def gather_plan(self):
        return None

    def layer1_weights(self, rode):
        return self.layer1

    def swap_plan(self, grads1):
        return None

    def exchange_plan(self, rode):
        return None

    def finish(self, rode):
        pass


def _local_step(x, pos, target, e_g_in, w_in0, e_g_q, w_q, e_g_kv, w_kv, sinks, b_f, g_final, layer1):
    S = x.shape[0]
    w_in0p, w_qp, w_kvp = _prep_w_in0(w_in0), _prep_w_q(w_q), _prep_w_kv(w_kv)
    slopes = jnp.asarray(2.0 ** (-8.0 * (np.arange(N_SWA, dtype=np.float32) + 1.0) / N_SWA), jnp.float32)
    sinks1 = sinks.reshape(N_SWA)
    b_col = b_f.reshape(N_FOX, 1)

    (h0, cq, ckv, qm, km, vm, qs, kd, vd, gate0, cos, sin) = _layer0_in(
        x, pos, e_g_in, w_in0p, e_g_q, w_qp, e_g_kv, w_kvp)
    o_m, lse_m, rode = _attn_fwd_t(qm, km, vm, (NOPE + ROPE) ** -0.5, split=True, name="mla_fwd", plan=layer1.gather_plan())
    w_out0, o_g_in, w_in1, w_out1 = layer1.layer1_weights(rode)
    w_in1p = _prep_w_in1(w_in1)
    o_s, lse_s = _swa_fwd(qs, kd, vd, sinks1, slopes)
    x1, h1, q1, k1, v1, gate1, f_slab = _layer0_out_layer1_in(x, o_m, o_s, gate0, w_out0, o_g_in, w_in1p)
    f_row = f_slab[:, :N_FOX].T
    lc_row = _forget_fwd(f_row, b_col)
    lcc = lc_row.T
    o1, lse1, _ = _attn_fwd_t(q1, k1, v1, HEAD ** -0.5, split=False, name="fox_fwd", lcc=lcc)
    loss8, dg_final, dw_out1, dx2, do1, dgate1 = _head(x1, o1, gate1, w_out1, g_final, target)

    dq1, dk1, dv1, dlc, _ = _attn_bwd_t(q1, k1, v1, do1, o1, lse1, HEAD ** -0.5, split=False, name="fox_bwd", lcc=lcc)
    df_row, db_f = _forget_bwd(dlc.reshape(N_FOX, S), f_row, b_col)
    df_slab = jnp.pad(df_row.T, ((0, 0), (0, LANES - N_FOX))).astype(MXU)
    dz1, dx1, dg_o_in, dw_out0, do_m, do_s, dgate0 = _layer1_in_bwd(
        dq1, dk1, dv1, dgate1, df_slab, x1, dx2, o_g_in, w_in1p, gate0, o_m, o_s, w_out0)
    grads1 = dict(o_g_in=dg_o_in, o_w_in=_fold_w_in1(_wgrad(dz1, h1, "wgrad_in1")), o_w_out=dw_out1, e_w_out=dw_out0)
    dqs, dkd, dvd, dsink, rode = _swa_bwd(qs, kd, vd, do_s, o_s, lse_s, sinks1, slopes, plan=layer1.swap_plan(grads1))
    dqm, dkm, dvm, rode = _attn_bwd_t(qm, km, vm, do_m, o_m, lse_m, (NOPE + ROPE) ** -0.5, split=True, name="mla_bwd",
                                      plan=layer1.exchange_plan(rode))
    layer1.finish(rode)
    dx, dz0, dg_in, dg_q, dg_kv, dw_q, dw_kv = _layer0_in_bwd(
        dqm, dkm, dvm, dqs, dkd, dvd, dgate0, cos, sin, cq, ckv, x, dx1, e_g_in, w_in0p, e_g_q, w_qp, e_g_kv, w_kvp)

    grads = dict(
        e_g_in=dg_in,
        e_w_in=_fold_w_in0(_wgrad(dz0, h0, "wgrad_in0")),
        e_g_q_a=dg_q,
        e_w_q_up=_fold_w_q(dw_q),
        e_g_kv_a=dg_kv,
        e_w_kv_up=_fold_w_kv(dw_kv),
        e_sinks=dsink[:, 0:4, 0].reshape(1, N_SWA),
        o_b_f=db_f.reshape(1, N_FOX),
        g_final=dg_final,
        **grads1,
    )
    return loss8[0, 0], dx, grads


SHARDED = ("e_w_in", "e_w_q_up", "e_w_kv_up", "e_w_out", "o_g_in", "o_w_in", "o_w_out")
TRANSPOSED = ("e_w_in", "e_w_q_up", "o_w_in")
COL_SHARDED = ("e_w_kv_up", "o_g_in")
REPLICATED = ("e_g_in", "e_g_q_a", "e_g_kv_a", "e_sinks", "o_b_f", "g_final")
FULL_SHAPES = dict(e_w_in=(2208, 1024), e_w_q_up=(768, 256), e_w_kv_up=(128, 1024), e_w_out=(1024, 1024),
                   o_g_in=(1, 1024), o_w_in=(4112, 1024), o_w_out=(1024, 1024))
GROUPS = dict(
    layer0=dict(rows=768, windows=dict(e_w_in=(0, 0), e_w_q_up=(560, 0), e_w_kv_up=(560, 256))),
    layer1=dict(rows=1568, windows=dict(o_w_in=(0, 0), o_w_out=(1040, 0), e_w_out=(1296, 0), o_g_in=(1552, 0))),
)


def _shard_shape(name):
    r, c = FULL_SHAPES[name]
    return (r, c // 4) if name in COL_SHARDED else (r // 4, c)


def _as_handled(name, a):
    a = a[0] if a.ndim == 3 else a
    return a.T if name in TRANSPOSED else a


def _as_given(name, a, shape):
    return (a.T if name in TRANSPOSED else a).reshape(shape)


def _pack_block(p, group):
    def rows(a, n):
        return jnp.pad(a, ((0, n - a.shape[0]), (0, 0)))

    if group == "layer0":
        band = jnp.concatenate([p["e_w_q_up"], rows(p["e_w_kv_up"], 192), jnp.zeros((192, 512), p["e_w_in"].dtype)], axis=1)
        return jnp.concatenate([rows(p["e_w_in"], 560), rows(band, 208)], axis=0)
    g = p["o_g_in"]
    band = jnp.pad(g, ((0, 16 - g.shape[0]), (0, PACK_COLS - g.shape[1])))
    return jnp.concatenate([rows(p["o_w_in"], 1040), p["o_w_out"], p["e_w_out"], band], axis=0)


def _window(block, group, name, width=None):
    r0, c0 = GROUPS[group]["windows"][name]
    r, c = _shard_shape(name)
    return block[..., r0:r0 + r, c0:c0 + (c if width is None else width)]


def _chip_slice(name, full, k):
    r, c = _shard_shape(name)
    if isinstance(full, _RowSeq):
        return jnp.concatenate(full.rows(r * k, r * (k + 1)), axis=0)
    return full[:, c * k:c * (k + 1)] if name in COL_SHARDED else full[r * k:r * (k + 1), :]


def _packed_weights(w, group):
    parts = {}
    for n in GROUPS[group]["windows"]:
        a = _as_handled(n, w[n])
        parts[n] = lax.bitcast_convert_type(a, jnp.bfloat16).reshape(1, -1) if n == "o_g_in" else a.astype(jnp.bfloat16)
    halves = _pack_block(parts, group).reshape(2, GROUPS[group]["rows"] // 2, PACK_COLS)
    return lax.dynamic_index_in_dim(halves, lax.axis_index("c"), 0, keepdims=False)


def _unpacked_weights(gathered, half, group):
    blocks = _fill_own_slot(gathered, half).reshape(4, GROUPS[group]["rows"], PACK_COLS)
    full = {}
    for n in GROUPS[group]["windows"]:
        if n == "o_g_in":
            halves = _window(blocks, group, n, width=512).reshape(4, 1, 256, 2)
            full[n] = jnp.concatenate(list(lax.bitcast_convert_type(halves, jnp.float32)), axis=1)
        else:
            pieces = [_window(blocks[k], group, n).astype(MXU) for k in range(4)]
            if n in ("e_w_in", "o_w_in"):
                full[n] = _RowSeq(pieces)
            else:
                full[n] = jnp.concatenate(pieces, axis=1 if n in COL_SHARDED else 0)
    return full


class _GroupReduce:
    def __init__(self, group):
        self.group = group
        self.c = lax.axis_index("c")
        self.chip = 2 * lax.axis_index("x") + lax.axis_index("y")

    def swap_plan(self, grads):
        names = GROUPS[self.group]["windows"]
        per_chip = jnp.stack([_pack_block({n: _chip_slice(n, grads[n], k) for n in names}, self.group) for k in range(4)])
        self.g4 = per_chip.reshape(4, 2, GROUPS[self.group]["rows"] // 2, PACK_COLS)
        return _pair_swap_plan(self.g4)

    def exchange_plan(self, rode):
        theirs = rode[0]
        rows = self.g4.shape[2]
        self.own = (lax.dynamic_slice(self.g4, (self.chip, self.c, 0, 0), (1, 1, rows, PACK_COLS)).reshape(rows, PACK_COLS),
                    lax.dynamic_index_in_dim(theirs, self.chip, 0, keepdims=False))
        return _chip_exchange_plan(_add_halves(self.g4, self.c, theirs, "pair_add_" + self.group, jnp.bfloat16))

    def finish(self, rode):
        my_half = _total_sum(*self.own, rode[0], "chip_sum_" + self.group)
        other_half = _run_plan(_pair_exchange_plan(my_half), "pair_exchange_" + self.group)[0]
        total = jnp.concatenate([jnp.where(self.c == 0, my_half, other_half), jnp.where(self.c == 0, other_half, my_half)], axis=0)
        self.sums = {n: _window(total, self.group, n) for n in GROUPS[self.group]["windows"]}

    def run(self, grads, beside):
        swap = self.swap_plan(grads)
        outs = _run_plan(_both_plans(swap, beside), "pair_swap_" + self.group)
        rode, others = outs[:len(swap.out_shape)], outs[len(swap.out_shape):]
        self.finish(_run_plan(self.exchange_plan(rode), "chip_exchange_" + self.group))
        return self.sums, others


class _Layer1Exchange(_GroupReduce):
    def __init__(self, w):
        super().__init__("layer1")
        self.half = _packed_weights(w, "layer1")

    def gather_plan(self):
        return _gather8_plan(self.half)

    def layer1_weights(self, rode):
        full = _unpacked_weights(rode[0], self.half, "layer1")
        return full["e_w_out"], full["o_g_in"], full["o_w_in"], full["o_w_out"]


def kernel(x, positions, e_g_in, e_w_in, e_g_q_a, e_w_q_up, e_g_kv_a, e_w_kv_up, e_sinks, e_w_out, o_g_in, o_w_in, o_b_f, o_w_out, g_final, loss_target, m_e_g_in, m_e_w_in, m_e_g_q_a, m_e_w_q_up, m_e_g_kv_a, m_e_w_kv_up, m_e_sinks, m_e_w_out, m_o_g_in, m_o_w_in, m_o_b_f, m_o_w_out, m_g_final, v_e_g_in, v_e_w_in, v_e_g_q_a, v_e_w_q_up, v_e_g_kv_a, v_e_w_kv_up, v_e_sinks, v_e_w_out, v_o_g_in, v_o_w_in, v_o_b_f, v_o_w_out, v_g_final):
    w = dict(e_g_in=e_g_in, e_w_in=e_w_in, e_g_q_a=e_g_q_a, e_w_q_up=e_w_q_up, e_g_kv_a=e_g_kv_a, e_w_kv_up=e_w_kv_up,
             e_sinks=e_sinks, e_w_out=e_w_out, o_g_in=o_g_in, o_w_in=o_w_in, o_b_f=o_b_f, o_w_out=o_w_out, g_final=g_final)
    m = dict(e_g_in=m_e_g_in, e_w_in=m_e_w_in, e_g_q_a=m_e_g_q_a, e_w_q_up=m_e_w_q_up, e_g_kv_a=m_e_g_kv_a,
             e_w_kv_up=m_e_w_kv_up, e_sinks=m_e_sinks, e_w_out=m_e_w_out, o_g_in=m_o_g_in, o_w_in=m_o_w_in, o_b_f=m_o_b_f,
             o_w_out=m_o_w_out, g_final=m_g_final)
    v = dict(e_g_in=v_e_g_in, e_w_in=v_e_w_in, e_g_q_a=v_e_g_q_a, e_w_q_up=v_e_w_q_up, e_g_kv_a=v_e_g_kv_a,
             e_w_kv_up=v_e_w_kv_up, e_sinks=v_e_sinks, e_w_out=v_e_w_out, o_g_in=v_o_g_in, o_w_in=v_o_w_in, o_b_f=v_o_b_f,
             o_w_out=v_o_w_out, g_final=v_g_final)
    order = ("e_g_in", "e_w_in", "e_g_q_a", "e_w_q_up", "e_g_kv_a", "e_w_kv_up", "e_sinks", "e_w_out", "o_g_in", "o_w_in",
             "o_b_f", "o_w_out", "g_final")
    half0 = _packed_weights(w, "layer0")
    full = _unpacked_weights(_run_plan(_gather8_plan(half0), "gather_weights_layer0")[0], half0, "layer0")
    layer1 = _Layer1Exchange(w)

    loss_part, dx, grads = _local_step(
        x[0], positions.reshape(-1, 1), loss_target[0], e_g_in, full["e_w_in"], e_g_q_a, full["e_w_q_up"], e_g_kv_a,
        full["e_w_kv_up"], e_sinks, o_b_f, g_final.reshape(1, D), layer1)

    small = jnp.concatenate([jnp.pad(loss_part.reshape(1), (0, LANES - 1))]
                            + [jnp.pad(grads[n].reshape(-1), (0, (-grads[n].size) % LANES)) for n in REPLICATED])
    rows = small.shape[0] // LANES
    small = jnp.pad(small.reshape(rows, LANES), ((0, (-rows) % 8), (0, 0)))
    sums0, (gathered_small,) = _GroupReduce("layer0").run(grads, _gather8_plan(small))
    gsum = {**layer1.sums, **sums0}
    ssum = _sum_leading(_fill_own_slot(gathered_small, small), "small_grad_sum").reshape(-1)
    loss = ssum[0]
    off = LANES
    for n in REPLICATED:
        cnt = w[n].size
        gsum[n] = ssum[off:off + cnt].reshape(w[n].shape)
        off += cnt + (-cnt) % LANES

    grad, delta, new_m, new_v = {}, {}, {}, {}
    for n in order:
        if n in SHARDED:
            outs = _adamw(_as_handled(n, w[n]), gsum[n], _as_handled(n, m[n]), _as_handled(n, v[n]), "adamw_" + n)
            grad[n], delta[n], new_m[n], new_v[n] = (_as_given(n, a, w[n].shape) for a in (gsum[n],) + outs)
        else:
            grad[n] = gsum[n]
            delta[n], new_m[n], new_v[n] = _adamw(w[n], gsum[n], m[n], v[n], "adamw_" + n)
    return (loss, dx[None], *[grad[n] for n in order], *[delta[n] for n in order], *[new_m[n] for n in order],
            *[new_v[n] for n in order])
```

```python
import math

import numpy as np
import jax
import jax.numpy as jnp
from jax import lax
from jax.experimental import pallas as pl
from jax.experimental.pallas import tpu as pltpu

D = 1024
EPS = 1e-6
ROPE_THETA = 10000.0
N_MLA = 8
Q_RANK = 256
KV_RANK = 128
NOPE = 64
ROPE = 32
N_SWA = 8
WINDOW = 128
N_FOX = 16
HEAD = 64
LR, B1, B2, AEPS, WD, STEP = 0.001, 0.9, 0.999, 1e-08, 0.01, 10

LANES = 128
HALF = 64
VMEM_LIMIT = 56 * 1024 * 1024
MXU = jnp.bfloat16
TOK = 256
WG_TOK = 2048
WG_ROWS = 1536
ATT = 256
FWD_CHUNK = 2
BWD_CHUNK = 2
SWA_GROUP = 8
NEG = float("-inf")

PACK_COLS = 1024
SUM_ROWS = 256
ADAM_TILE_BYTES = 2 << 20
MESH_ID = pl.DeviceIdType.MESH


def _pcall(body, *, name, vmem=VMEM_LIMIT, semantics=None, **kw):
    params = dict(vmem_limit_bytes=vmem)
    if semantics is not None:
        params["dimension_semantics"] = semantics
    return pl.pallas_call(body, name=name, compiler_params=pltpu.CompilerParams(**params), **kw)


def _mm(a, b):
    return jnp.dot(a.astype(MXU), b.astype(MXU), preferred_element_type=jnp.float32)


def _mm_nt(a, b):
    return lax.dot_general(a.astype(MXU), b.astype(MXU), (((1,), (1,)), ((), ())),
                           preferred_element_type=jnp.float32)


def _mm_tn(a, b):
    return lax.dot_general(a.astype(MXU), b.astype(MXU), (((0,), (0,)), ((), ())),
                           preferred_element_type=jnp.float32)


def _full(shape):
    n = len(shape)
    return pl.BlockSpec(shape, lambda *_: (0,) * n)


def _rows(tm, n):
    return pl.BlockSpec((tm, n), lambda i: (i, 0))


def _sds(shape, dtype):
    return jax.ShapeDtypeStruct(shape, dtype)


def _rms(x, g):
    r = lax.rsqrt(jnp.mean(x * x, axis=-1, keepdims=True) + EPS)
    return x * r * g


def _rms_bwd(x, g, dy):
    r = lax.rsqrt(jnp.mean(x * x, axis=-1, keepdims=True) + EPS)
    xh = x * r
    dxh = dy * g
    dx = r * (dxh - xh * jnp.mean(dxh * xh, axis=-1, keepdims=True))
    return dx, dy * xh


def _sigmoid(x):
    return 1.0 / (1.0 + jnp.exp(-x))


def _lane_masks():
    lane = lax.broadcasted_iota(jnp.int32, (1, LANES), 1)
    return lane < HALF


def _split_heads(a, lo):
    z = jnp.zeros_like(a)
    return [jnp.where(lo, a, z), jnp.where(lo, z, a)]


def _rope_consts():
    inv = np.zeros((8, LANES), np.float32)
    j = np.arange(ROPE // 2, dtype=np.float32)
    f = (1.0 / (ROPE_THETA ** (np.arange(0, ROPE, 2, dtype=np.float32) / ROPE))).astype(np.float32)
    inv[0, HALF:HALF + 16] = f
    inv[0, HALF + 16:HALF + 32] = f
    inv[1, HALF:HALF + 16] = -1.0
    inv[1, HALF + 16:HALF + 32] = 1.0
    del j
    return jnp.asarray(inv)


def _rope_tables(pos_f, consts):
    ang = pos_f * consts[0:1, :]
    sign = consts[1:2, :]
    c = jnp.where(sign != 0.0, jnp.cos(ang), 1.0)
    s = jnp.sin(ang) * sign
    return c, s


def _swap_halves(v, sign):
    lo = pltpu.roll(v, LANES - 16, axis=1)
    hi = pltpu.roll(v, 16, axis=1)
    return jnp.where(sign < 0.0, lo, jnp.where(sign > 0.0, hi, 0.0))


def _rope(x, c, s, sign):
    return x * c + _swap_halves(x, sign) * s


def _rope_t(dy, c, s, sign):
    return dy * c + _swap_halves(dy * s, sign)


def _layer0_in(x, pos, g_in, w_in, g_q, w_q, g_kv, w_kv):
    S = x.shape[0]
    consts = _rope_consts()

    def body(x_ref, pos_ref, c_ref, g_ref, w_ref, gq_ref, wq_ref, gkv_ref, wkv_ref,
             h_ref, cq_ref, ckv_ref, qm_ref, km_ref, vm_ref,
             qs_ref, kd_ref, vd_ref, gate_ref, cos_ref, sin_ref):
        h = _rms(x_ref[...], g_ref[...])
        h_ref[...] = h.astype(h_ref.dtype)
        z = _mm_nt(h, w_ref[...])
        cq = z[:, 0:256]
        ckv = z[:, 256:384]
        kpe = z[:, 384:512]
        cq_ref[...] = cq
        ckv_ref[...] = ckv
        qs_ref[...] = z[:, 512:1024].astype(qs_ref.dtype)
        kd_ref[...] = z[:, 1024:1536].astype(kd_ref.dtype)
        vd_ref[...] = z[:, 1536:2048].astype(vd_ref.dtype)
        gate_ref[...] = z[:, 2048:3072]
        cqn = _rms(cq, gq_ref[...])
        ckvn = _rms(ckv, gkv_ref[...])
        q = _mm_nt(cqn, wq_ref[...])
        kv = _mm(ckvn, wkv_ref[...])
        vm_ref[...] = kv[:, 1024:1536].astype(vm_ref.dtype)
        consts_v = c_ref[...]
        sign = consts_v[1:2, :]
        c, s = _rope_tables(pos_ref[...].astype(jnp.float32), consts_v)
        cos_ref[...] = c
        sin_ref[...] = s
        kpe_r = _rope(kpe, c, s, sign)
        for hd in range(N_MLA):
            sl = slice(LANES * hd, LANES * (hd + 1))
            qm_ref[:, sl] = _rope(q[:, sl], c, s, sign).astype(qm_ref.dtype)
            km_ref[:, sl] = (kv[:, sl] + kpe_r).astype(km_ref.dtype)

    outs = [
        ((S, D), MXU), ((S, 256), jnp.float32), ((S, 128), jnp.float32),
        ((S, 1024), MXU), ((S, 1024), MXU), ((S, 512), MXU), ((S, 512), MXU), ((S, 512), MXU), ((S, 512), MXU),
        ((S, 1024), jnp.float32), ((S, 128), jnp.float32), ((S, 128), jnp.float32),
    ]
    return _pcall(
        body, name="layer0_in", grid=(S // TOK,), semantics=("arbitrary",),
        in_specs=[_rows(TOK, D), _rows(TOK, 1), _full((8, LANES)), _full((1, D)), _full(w_in.shape), _full((1, 256)),
                  _full(w_q.shape), _full((1, 128)), _full(w_kv.shape)],
        out_specs=[_rows(TOK, s[1]) for s, _ in outs],
        out_shape=[_sds(s, d) for s, d in outs],
    )(x, pos, consts, g_in, w_in, g_q, w_q, g_kv, w_kv)


AUG = (HALF, 0)
ONE = (HALF + 8, 8)


def _data_lanes(idx, h):
    return (idx < HALF) if h == 0 else (idx >= HALF)


def _three_terms(x):
    hi = x.astype(MXU).astype(jnp.float32)
    mid = (x - hi).astype(MXU).astype(jnp.float32)
    lo = (x - hi - mid).astype(MXU).astype(jnp.float32)
    return hi, mid, lo


def _q_aug(qblk, lc, h, scale, lane):
    a = AUG[h]
    hi, mid, lo = _three_terms(lc)
    ones = ((lane >= a + 3) & (lane <= a + 5)).astype(jnp.float32)
    aug = jnp.where(lane == a, hi, jnp.where(lane == a + 1, mid, jnp.where(lane == a + 2, lo, ones)))
    return jnp.where(_data_lanes(lane, h), qblk * jnp.asarray(scale, qblk.dtype), aug.astype(qblk.dtype))


def _k_aug(kblk, lc, h, lane):
    a = AUG[h]
    hi, mid, lo = _three_terms(-lc)
    ones = ((lane >= a) & (lane <= a + 2)).astype(jnp.float32)
    aug = jnp.where(lane == a + 3, hi, jnp.where(lane == a + 4, mid, jnp.where(lane == a + 5, lo, ones)))
    return jnp.where(_data_lanes(lane, h), kblk, aug.astype(kblk.dtype))


def _lc_col(lc_ref, r0, rows, h):
    head = lax.broadcasted_iota(jnp.int32, (1, lc_ref.shape[1]), 1)
    return jnp.sum(jnp.where(head == 2 * pl.program_id(0) + h, lc_ref[pl.ds(r0, rows), :], 0.0), axis=1, keepdims=True)


def _attn_fwd_t(q, k, v, scale, *, split, name, lcc=None, plan=None):
    S = q.shape[0]
    npair = v.shape[1] // LANES
    W = 2 * LANES if split else LANES
    T = ATT
    CH = FWD_CHUNK * T
    assert S % CH == 0
    nq = S // T

    def body(*refs):
        if split:
            q_ref, k_ref, v_ref, o_ref, lse_ref, vt, acc, m_sc = refs
        else:
            q_ref, k_ref, v_ref, lcc_ref, o_ref, lse_ref, kaug, vt, acc, m_sc = refs
        lane = lax.broadcasted_iota(jnp.int32, (1, LANES), 1)
        sub = lax.broadcasted_iota(jnp.int32, (LANES, 1), 0)
        key_minus_qry = lax.broadcasted_iota(jnp.int32, (CH, T), 0) - lax.broadcasted_iota(jnp.int32, (CH, T), 1)

        def prep(i, c):
            r0 = pl.multiple_of(i * T, T)
            vblk = v_ref[pl.ds(r0, T), :].astype(jnp.float32)
            for h in (0, 1):
                vh = jnp.where(_data_lanes(lane, h), vblk, (lane == ONE[h]).astype(jnp.float32))
                vt[h, :, pl.ds(r0, T)] = vh.T.astype(vt.dtype)
                if not split:
                    kaug[h, pl.ds(r0, T), :] = _k_aug(k_ref[pl.ds(r0, T), :], _lc_col(lcc_ref, r0, T, h), h, lane)
            return c

        lax.fori_loop(0, nq, prep, 0)

        def queries(qi):
            q0 = pl.multiple_of(qi * T, T)
            qblk = q_ref[pl.ds(q0, T), :]
            if split:
                return (qblk[:, :LANES], qblk[:, LANES:])
            return tuple(_q_aug(qblk, _lc_col(lcc_ref, q0, T, h), h, scale, lane) for h in (0, 1))

        def scores(qs, c):
            k0 = pl.multiple_of(c * CH, CH)
            out = []
            for h in (0, 1):
                if split:
                    out.append(_mm_nt(k_ref[pl.ds(k0, CH), LANES * h:LANES * (h + 1)], qs[h]) * scale)
                else:
                    out.append(_mm_nt(kaug[h, pl.ds(k0, CH), :], qs[h]))
            return tuple(out)

        def q_block(qi, carry):
            qs, first_scores = carry[:2], carry[2:]
            q0 = pl.multiple_of(qi * T, T)
            acc[...] = jnp.zeros_like(acc)
            m_sc[...] = jnp.full(m_sc.shape, NEG, jnp.float32)

            def absorb(c, sts, masked):
                k0 = pl.multiple_of(c * CH, CH)
                for h in (0, 1):
                    st = sts[h]
                    if masked:
                        st = jnp.where(key_minus_qry <= q0 - k0, st, NEG)
                    m_old = m_sc[h:h + 1, :]
                    m_new = jnp.maximum(m_old, jnp.max(st, axis=0, keepdims=True))
                    alpha = jnp.exp(m_old - m_new)
                    pt = jnp.exp(st - m_new)
                    acc[h] = alpha * acc[h] + _mm(vt[h, :, pl.ds(k0, CH)], pt)
                    m_sc[h:h + 1, :] = m_new

            last = qi // FWD_CHUNK

            def pipelined(c, sts):
                nxt = scores(qs, c + 1)
                absorb(c, sts, False)
                return nxt

            sts = lax.fori_loop(0, last, pipelined, first_scores)
            qs_next = queries(jnp.minimum(qi + 1, nq - 1))
            nxt = qs_next + scores(qs_next, 0)
            absorb(last, sts, True)
            ot = None
            for h in (0, 1):
                a = acc[h]
                l = a[ONE[h]:ONE[h] + 1, :]
                oh = jnp.where(_data_lanes(sub, h), a * (1.0 / l), 0.0)
                ot = oh if ot is None else ot + oh
                lse_ref[0, h:h + 1, pl.ds(q0, T)] = m_sc[h:h + 1, :] + jnp.log(l)
            o_ref[pl.ds(q0, T), :] = ot.T
            return nxt

        qs0 = queries(0)
        lax.fori_loop(0, nq, q_block, qs0 + scores(qs0, 0))

    wide = pl.BlockSpec((S, W), lambda j: (0, j))
    slab = pl.BlockSpec((S, LANES), lambda j: (0, j))
    rows = pl.BlockSpec((1, 2, S), lambda j: (j, 0, 0))
    in_specs = [wide, wide, slab]
    args = [q, k, v]
    scratch = []
    if not split:
        in_specs.append(_full(lcc.shape))
        args.append(lcc)
        scratch.append(pltpu.VMEM((2, S, LANES), MXU))
    scratch += [pltpu.VMEM((2, LANES, S), MXU), pltpu.VMEM((2, LANES, T), jnp.float32), pltpu.VMEM((8, T), jnp.float32)]
    (o, lse), rode = _pcall_riding(
        body, plan, args, name=name, grid=(npair,), in_specs=in_specs, out_specs=[slab, rows],
        out_shape=[_sds((S, npair * LANES), jnp.float32), _sds((npair, 2, S), jnp.float32)], scratch_shapes=scratch)
    return o, lse, rode


def _attn_bwd_t(q, k, v, do, o, lse, scale, *, split, name, lcc=None, plan=None):
    S = q.shape[0]
    npair = v.shape[1] // LANES
    W = 2 * LANES if split else LANES
    T = ATT
    CH = BWD_CHUNK * T
    assert S % CH == 0
    nq = S // T

    def body(*refs):
        if split:
            (q_ref, k_ref, v_ref, do_ref, o_ref, lse_ref, dq_ref, dk_ref, dv_ref, dqt, delta, dk_acc, dv_acc) = refs
        else:
            (q_ref, k_ref, v_ref, do_ref, o_ref, lse_ref, lcc_ref, dq_ref, dk_ref, dv_ref, dlc_ref,
             dqt, delta, dk_acc, dv_acc, qaug, csum) = refs
        lane = lax.broadcasted_iota(jnp.int32, (1, LANES), 1)
        sub = lax.broadcasted_iota(jnp.int32, (LANES, 1), 0)
        key_minus_qry = lax.broadcasted_iota(jnp.int32, (T, CH), 0) - lax.broadcasted_iota(jnp.int32, (T, CH), 1)

        def prep(i, c):
            r0 = pl.multiple_of(i * T, T)
            prod_t = (do_ref[pl.ds(r0, T), :].astype(jnp.float32) * o_ref[pl.ds(r0, T), :]).T
            for h in (0, 1):
                delta[h:h + 1, pl.ds(r0, T)] = jnp.sum(jnp.where(_data_lanes(sub, h), prod_t, 0.0), axis=0, keepdims=True)
                dqt[h, :, pl.ds(r0, T)] = jnp.zeros((LANES, T), jnp.float32)
                if not split:
                    qaug[h, pl.ds(r0, T), :] = _q_aug(q_ref[pl.ds(r0, T), :], _lc_col(lcc_ref, r0, T, h), h, scale, lane)
            return c

        lax.fori_loop(0, nq, prep, 0)

        def keys(ki):
            k0 = pl.multiple_of(ki * T, T)
            kblk = k_ref[pl.ds(k0, T), :]
            if split:
                return (kblk[:, :LANES], kblk[:, LANES:])
            return tuple(_k_aug(kblk, _lc_col(lcc_ref, k0, T, h), h, lane) for h in (0, 1))

        def q_of(c, h):
            q0 = pl.multiple_of(c * CH, CH)
            if split:
                return q_ref[pl.ds(q0, CH), LANES * h:LANES * (h + 1)]
            return qaug[h, pl.ds(q0, CH), :]

        def scores(khs, c):
            out = []
            for h in (0, 1):
                st = _mm_nt(khs[h], q_of(c, h))
                out.append(st * scale if split else st)
            return tuple(out)

        def k_block(ki, carry):
            khs, first_scores = carry[:2], carry[2:]
            k0 = pl.multiple_of(ki * T, T)
            khts = [kh.astype(jnp.float32).T.astype(kh.dtype) for kh in khs]
            vhs = _split_heads(v_ref[pl.ds(k0, T), :], lane < HALF)
            dk_acc[...] = jnp.zeros_like(dk_acc)
            dv_acc[...] = jnp.zeros_like(dv_acc)

            def absorb(c, vals):
                q0 = pl.multiple_of(c * CH, CH)
                dos = _split_heads(do_ref[pl.ds(q0, CH), :], lane < HALF)
                visible = key_minus_qry <= q0 - k0
                for h in (0, 1):
                    dpt = _mm_nt(vhs[h], dos[h])
                    st = jnp.where(visible, vals[h], NEG)
                    pt = jnp.exp(st - lse_ref[0, h:h + 1, pl.ds(q0, CH)])
                    dv_acc[...] += _mm(pt, dos[h])
                    dst = pt * (dpt - delta[h:h + 1, pl.ds(q0, CH)])
                    dk_acc[h] += _mm(dst, q_of(c, h))
                    dqt[h, :, pl.ds(q0, CH)] += _mm(khts[h], dst)

            first = ki // BWD_CHUNK

            def pipelined(c, vals):
                nxt = scores(khs, c + 1)
                absorb(c, vals)
                return nxt

            vals = lax.fori_loop(first, S // CH - 1, pipelined, first_scores)
            kn = jnp.minimum(ki + 1, nq - 1)
            khs_next = keys(kn)
            nxt = khs_next + scores(khs_next, kn // BWD_CHUNK)
            absorb(S // CH - 1, vals)
            if split:
                dk_ref[pl.ds(k0, T), :LANES] = (dk_acc[0] * scale).astype(dk_ref.dtype)
                dk_ref[pl.ds(k0, T), LANES:] = (dk_acc[1] * scale).astype(dk_ref.dtype)
            else:
                dk_ref[pl.ds(k0, T), :] = jnp.where(lane < HALF, dk_acc[0], dk_acc[1]).astype(dk_ref.dtype)
                for h in (0, 1):
                    csum[h:h + 1, pl.ds(k0, T)] = dk_acc[h].T[AUG[h] + 3:AUG[h] + 4, :]
            dv_ref[pl.ds(k0, T), :] = dv_acc[...].astype(dv_ref.dtype)
            return nxt

        khs0 = keys(0)
        lax.fori_loop(0, nq, k_block, khs0 + scores(khs0, 0))

        def finish(i, c):
            r0 = pl.multiple_of(i * T, T)
            if split:
                for h in (0, 1):
                    dq_ref[pl.ds(r0, T), LANES * h:LANES * (h + 1)] = (dqt[h, :, pl.ds(r0, T)].T * scale).astype(dq_ref.dtype)
            else:
                d = jnp.where(sub < HALF, dqt[0, :, pl.ds(r0, T)], dqt[1, :, pl.ds(r0, T)])
                dq_ref[pl.ds(r0, T), :] = (d.T * scale).astype(dq_ref.dtype)
                for h in (0, 1):
                    dlc_ref[0, h:h + 1, pl.ds(r0, T)] = dqt[h, AUG[h]:AUG[h] + 1, pl.ds(r0, T)] - csum[h:h + 1, pl.ds(r0, T)]
            return c

        lax.fori_loop(0, nq, finish, 0)

    wide = pl.BlockSpec((S, W), lambda j: (0, j))
    slab = pl.BlockSpec((S, LANES), lambda j: (0, j))
    rows = pl.BlockSpec((1, 2, S), lambda j: (j, 0, 0))
    in_specs = [wide, wide, slab, slab, slab, rows]
    args = [q, k, v, do, o, lse]
    out_specs = [wide, wide, slab]
    out_shape = [_sds(q.shape, jnp.float32 if split else do.dtype), _sds(k.shape, jnp.float32 if split else do.dtype),
                 _sds(v.shape, do.dtype)]
    scratch = [pltpu.VMEM((2, LANES, S), jnp.float32), pltpu.VMEM((8, S), jnp.float32),
               pltpu.VMEM((2, T, LANES), jnp.float32), pltpu.VMEM((T, LANES), jnp.float32)]
    if not split:
        in_specs.append(_full(lcc.shape))
        args.append(lcc)
        out_specs.append(rows)
        out_shape.append(_sds((npair, 2, S), jnp.float32))
        scratch += [pltpu.VMEM((2, S, LANES), MXU), pltpu.VMEM((8, S), jnp.float32)]
    outs, rode = _pcall_riding(body, plan, args, name=name, grid=(npair,), in_specs=in_specs, out_specs=out_specs,
                               out_shape=out_shape, scratch_shapes=scratch)
    return (*outs, rode)


def _swa_bias(slope, shift):
    a = lax.broadcasted_iota(jnp.int32, (WINDOW, 2 * WINDOW), 0)
    c = lax.broadcasted_iota(jnp.int32, (WINDOW, 2 * WINDOW), 1)
    dist = a - c + shift
    return jnp.where((dist >= 0) & (dist < WINDOW), -slope * dist.astype(jnp.float32), NEG)


def _swa_scores(qh, kblk, bias):
    return _mm_nt(qh, kblk) * (HEAD ** -0.5) + bias


def _swa_stack(blk, lo):
    return jnp.concatenate(_split_heads(blk[:, :LANES], lo) + _split_heads(blk[:, LANES:], lo), axis=0)


def _swa_unstack(x, lo):
    r = x.shape[0] // 4
    return jnp.concatenate([jnp.where(lo, x[0:r], x[r:2 * r]), jnp.where(lo, x[2 * r:3 * r], x[3 * r:])], axis=1)


def _swa_per_head(ref, j, rows):
    quarter = lax.broadcasted_iota(jnp.int32, (4 * rows, 1), 0) // rows
    return jnp.where(quarter == 0, ref[4 * j], jnp.where(quarter == 1, ref[4 * j + 1],
                                                         jnp.where(quarter == 2, ref[4 * j + 2], ref[4 * j + 3])))


def _swa_fwd(q, kd, vd, sinks, slopes):
    S = q.shape[0]
    nkv = q.shape[1] // (2 * LANES)
    nb = S // WINDOW
    group = math.gcd(SWA_GROUP, nb)

    def body(sink_ref, slope_ref, q_ref, k_ref, v_ref, o_ref, lse_ref):
        j = pl.program_id(0)
        lo = _lane_masks()
        sink = _swa_per_head(sink_ref, j, WINDOW)
        biases = [jnp.concatenate([_swa_bias(slope_ref[4 * j + h], shift) for h in range(4)], axis=0)
                  for shift in (0, WINDOW)]

        def q_block(qi, c):
            q0 = pl.multiple_of(qi * WINDOW, WINDOW)
            k0 = pl.multiple_of(jnp.maximum(qi - 1, 0) * WINDOW, WINDOW)
            s = _swa_scores(_swa_stack(q_ref[pl.ds(q0, WINDOW), :], lo), k_ref[pl.ds(k0, 2 * WINDOW), :],
                            jnp.where(qi == 0, *biases))
            m = jnp.maximum(jnp.max(s, axis=1, keepdims=True), sink)
            p = jnp.exp(s - m)
            den = jnp.sum(p, axis=1, keepdims=True) + jnp.exp(sink - m)
            o_ref[pl.ds(q0, WINDOW), :] = _swa_unstack(_mm(p / den, v_ref[pl.ds(k0, 2 * WINDOW), :]), lo)
            lse = m + jnp.log(den)
            for h in range(4):
                lse_ref[h, pl.ds(q0, WINDOW), :] = lse[h * WINDOW:(h + 1) * WINDOW]
            return c

        def q_group(gi, c):
            for g in range(group):
                q_block(gi * group + g, c)
            return c

        lax.fori_loop(0, nb // group, q_group, 0)

    smem = pl.BlockSpec(memory_space=pltpu.SMEM)
    two = pl.BlockSpec((S, 2 * LANES), lambda j: (0, j))
    kv = pl.BlockSpec((S, LANES), lambda j: (0, 2 * j))
    return _pcall(
        body, name="swa_fwd", grid=(nkv,), semantics=("arbitrary",),
        in_specs=[smem, smem, two, kv, kv],
        out_specs=[two, pl.BlockSpec((4, S, 1), lambda j: (j, 0, 0))],
        out_shape=[_sds(q.shape, jnp.float32), _sds((4 * nkv, S, 1), jnp.float32)],
    )(sinks, slopes, q, kd, vd)


def _swa_bwd(q, kd, vd, do, o, lse, sinks, slopes, plan=None):
    S = q.shape[0]
    nkv = q.shape[1] // (2 * LANES)
    nb = S // WINDOW
    group = math.gcd(SWA_GROUP, nb)

    def body(sink_ref, slope_ref, q_ref, k_ref, v_ref, do_ref, o_ref, lse_ref,
             dq_ref, dk_ref, dv_ref, dsink_ref, dk_acc, dv_acc):
        j = pl.program_id(0)
        lo = _lane_masks()
        dk_acc[...] = jnp.zeros_like(dk_acc)
        dv_acc[...] = jnp.zeros_like(dv_acc)
        sink = _swa_per_head(sink_ref, j, WINDOW)
        biases = [jnp.concatenate([_swa_bias(slope_ref[4 * j + h], shift) for h in range(4)], axis=0)
                  for shift in (0, WINDOW)]

        def q_block(qi, carry):
            q0 = pl.multiple_of(qi * WINDOW, WINDOW)
            k0 = pl.multiple_of(jnp.maximum(qi - 1, 0) * WINDOW, WINDOW)
            q4 = _swa_stack(q_ref[pl.ds(q0, WINDOW), :], lo)
            do4 = _swa_stack(do_ref[pl.ds(q0, WINDOW), :], lo)
            oblk = o_ref[pl.ds(q0, WINDOW), :]
            o4 = jnp.concatenate([oblk[:, :LANES], oblk[:, :LANES], oblk[:, LANES:], oblk[:, LANES:]], axis=0)
            kblk = k_ref[pl.ds(k0, 2 * WINDOW), :]
            vblk = v_ref[pl.ds(k0, 2 * WINDOW), :]
            lse = jnp.concatenate([lse_ref[h, pl.ds(q0, WINDOW), :] for h in range(4)], axis=0)
            p = jnp.exp(_swa_scores(q4, kblk, jnp.where(qi == 0, *biases)) - lse)
            delta = jnp.sum(do4.astype(jnp.float32) * o4, axis=1, keepdims=True)
            dv_acc[pl.ds(k0, 2 * WINDOW), :] += _mm_tn(p, do4)
            ds = p * (_mm_nt(do4, vblk) - delta)
            dq_ref[pl.ds(q0, WINDOW), :] = _swa_unstack(_mm(ds, kblk) * (HEAD ** -0.5), lo).astype(dq_ref.dtype)
            dk_acc[pl.ds(k0, 2 * WINDOW), :] += _mm_tn(ds, q4) * (HEAD ** -0.5)
            dsk = -jnp.exp(sink - lse) * delta
            return tuple(carry[h] + jnp.sum(dsk[h * WINDOW:(h + 1) * WINDOW], axis=0, keepdims=True)
                         for h in range(4))

        def q_group(gi, carry):
            for g in range(group):
                carry = q_block(gi * group + g, carry)
            return carry

        zero = jnp.zeros((1, 1), jnp.float32)
        dsinks = lax.fori_loop(0, nb // group, q_group, (zero,) * 4)
        dk_ref[:, :LANES] = dk_acc[...].astype(dk_ref.dtype)
        dk_ref[:, LANES:] = jnp.zeros((S, LANES), dk_ref.dtype)
        dv_ref[:, :LANES] = dv_acc[...].astype(dv_ref.dtype)
        dv_ref[:, LANES:] = jnp.zeros((S, LANES), dv_ref.dtype)
        r = lax.broadcasted_iota(jnp.int32, (8, LANES), 0)
        dsink_ref[0] = jnp.where(r == 0, dsinks[0], jnp.where(r == 1, dsinks[1], jnp.where(r == 2, dsinks[2],
                                 jnp.where(r == 3, dsinks[3], 0.0))))

    smem = pl.BlockSpec(memory_space=pltpu.SMEM)
    two = pl.BlockSpec((S, 2 * LANES), lambda j: (0, j))
    kv = pl.BlockSpec((S, LANES), lambda j: (0, 2 * j))
    outs, rode = _pcall_riding(
        body, plan, [sinks, slopes, q, kd, vd, do, o, lse], name="swa_bwd", grid=(nkv,),
        in_specs=[smem, smem, two, kv, kv, two, two, pl.BlockSpec((4, S, 1), lambda j: (j, 0, 0))],
        out_specs=[two, two, two, pl.BlockSpec((1, 8, LANES), lambda j: (j, 0, 0))],
        out_shape=[_sds(q.shape, do.dtype), _sds(kd.shape, do.dtype), _sds(vd.shape, do.dtype),
                   _sds((nkv, 8, LANES), jnp.float32)],
        scratch_shapes=[pltpu.VMEM((S, LANES), jnp.float32), pltpu.VMEM((S, LANES), jnp.float32)])
    return (*outs, rode)


def _log_steps(S):
    k, out = 1, []
    while k < S:
        out.append(k)
        k *= 2
    return out


def _forget_fwd(f_row, b_col):
    S = f_row.shape[1]

    def body(f_ref, b_ref, lc_ref):
        x = f_ref[...] + b_ref[...]
        lc = jnp.minimum(x, 0.0) - jnp.log(1.0 + jnp.exp(-jnp.abs(x)))
        idx = lax.broadcasted_iota(jnp.int32, lc.shape, 1)
        for k in _log_steps(S):
            lc = lc + jnp.where(idx >= k, pltpu.roll(lc, k, axis=1), 0.0)
        lc_ref[...] = lc

    return _pcall(body, name="forget_fwd", out_shape=_sds(f_row.shape, jnp.float32))(f_row, b_col)


def _forget_bwd(dlc_row, f_row, b_col):
    S = f_row.shape[1]

    def body(d_ref, f_ref, b_ref, df_ref, db_ref):
        g = d_ref[...]
        idx = lax.broadcasted_iota(jnp.int32, g.shape, 1)
        for k in _log_steps(S):
            g = g + jnp.where(idx < S - k, pltpu.roll(g, S - k, axis=1), 0.0)
        x = f_ref[...] + b_ref[...]
        df = g * _sigmoid(-x)
        df_ref[...] = df
        db_ref[...] = jnp.sum(df, axis=1, keepdims=True)

    return _pcall(body, name="forget_bwd",
                  out_shape=[_sds(f_row.shape, jnp.float32), _sds((f_row.shape[0], 1), jnp.float32)])(dlc_row, f_row, b_col)


def _layer0_out_layer1_in(x, o_m, o_s, gate, w_out, g1, w_in1):
    S = x.shape[0]

    def body(x_ref, om_ref, os_ref, gate_ref, wo_ref, g_ref, w_ref,
             x1_ref, h_ref, q_ref, k_ref, v_ref, g1_ref, f_ref):
        gt = gate_ref[...]
        sg = gt * _sigmoid(gt)
        um = om_ref[...] * sg[:, :512]
        us = os_ref[...] * sg[:, 512:]
        x1 = x_ref[...] + _mm(um, wo_ref[0:512, :]) + _mm(us, wo_ref[512:1024, :])
        x1_ref[...] = x1
        h = _rms(x1, g_ref[...])
        h_ref[...] = h.astype(h_ref.dtype)
        z = _mm_nt(h, w_ref[...])
        q_ref[...] = z[:, 0:1024].astype(q_ref.dtype)
        k_ref[...] = z[:, 1024:2048].astype(k_ref.dtype)
        v_ref[...] = z[:, 2048:3072].astype(v_ref.dtype)
        g1_ref[...] = z[:, 3072:4096]
        f_ref[...] = z[:, 4096:4224]

    outs = [((S, D), jnp.float32), ((S, D), MXU), ((S, D), MXU), ((S, D), MXU), ((S, D), MXU),
            ((S, D), jnp.float32), ((S, LANES), jnp.float32)]
    return _pcall(
        body, name="layer0_out_layer1_in", grid=(S // TOK,), semantics=("arbitrary",),
        in_specs=[_rows(TOK, D), _rows(TOK, 512), _rows(TOK, 512), _rows(TOK, D), _full((D, D)), _full((1, D)),
                  _full(w_in1.shape)],
        out_specs=[_rows(TOK, s[1]) for s, _ in outs],
        out_shape=[_sds(s, d) for s, d in outs],
    )(x, o_m, o_s, gate, w_out, g1, w_in1)


def _head(x1, o1, gate1, w_out1, g_f, target):
    S = x1.shape[0]

    def body(x1_ref, o_ref, gate_ref, wo_ref, g_ref, t_ref,
             loss_ref, dgf_ref, dwo_ref, dx2_ref, do_ref, dgate_ref):
        i = pl.program_id(0)
        gt = gate_ref[...]
        sig = _sigmoid(gt)
        sg = gt * sig
        o = o_ref[...]
        u = o * sg
        x2 = x1_ref[...] + _mm(u, wo_ref[...])
        g = g_ref[...]
        y = _rms(x2, g)
        err = y - t_ref[...]
        part = 0.5 * jnp.sum(jnp.mean(err * err, axis=-1, keepdims=True), axis=0, keepdims=True)
        dy = err * (1.0 / D)
        dx2, dg_rows = _rms_bwd(x2, g, dy)
        dx2_ref[...] = dx2
        du = _mm_nt(dx2, wo_ref[...])
        do_ref[...] = (du * sg).astype(do_ref.dtype)
        dgate_ref[...] = (du * o * (sig * (1.0 + gt * (1.0 - sig)))).astype(dgate_ref.dtype)

        @pl.when(i == 0)
        def _():
            loss_ref[...] = jnp.zeros_like(loss_ref)
            dgf_ref[...] = jnp.zeros_like(dgf_ref)
            dwo_ref[...] = jnp.zeros_like(dwo_ref)

        loss_ref[...] += jnp.broadcast_to(part, loss_ref.shape)
        dgf_ref[...] += jnp.sum(dg_rows, axis=0, keepdims=True)
        dwo_ref[...] += _mm_tn(u, dx2)

    outs = [((S, D), jnp.float32), ((S, D), MXU), ((S, D), MXU)]
    return _pcall(
        body, name="head", grid=(S // TOK,), semantics=("arbitrary",),
        in_specs=[_rows(TOK, D), _rows(TOK, D), _rows(TOK, D), _full((D, D)), _full((1, D)), _rows(TOK, D)],
        out_specs=[_full((8, LANES)), _full((1, D)), _full((D, D))] + [_rows(TOK, D) for _ in outs],
        out_shape=[_sds((8, LANES), jnp.float32), _sds((1, D), jnp.float32), _sds((D, D), jnp.float32)]
        + [_sds(s, d) for s, d in outs],
    )(x1, o1, gate1, w_out1, g_f, target)


def _layer1_in_bwd(dq, dk, dv, dgate1, df, x1, dx2, g1, w_in1, gate0, o_m, o_s, w_out0):
    S = x1.shape[0]

    def body(dq_ref, dk_ref, dv_ref, dg1_ref, df_ref, x1_ref, dx2_ref, g_ref, w_ref, gate_ref, om_ref, os_ref,
             wo_ref, dz_ref, dx1_ref, dgn_ref, dwo_ref, dom_ref, dos_ref, dgate_ref):
        i = pl.program_id(0)
        dz_ref[:, 0:1024] = dq_ref[...]
        dz_ref[:, 1024:2048] = dk_ref[...]
        dz_ref[:, 2048:3072] = dv_ref[...]
        dz_ref[:, 3072:4096] = dg1_ref[...]
        dz_ref[:, 4096:4224] = df_ref[...]
        dh = _mm(dz_ref[...], w_ref[...])
        g = g_ref[...]
        dxn, dg_rows = _rms_bwd(x1_ref[...], g, dh)
        dx1 = dx2_ref[...] + dxn
        dx1_ref[...] = dx1
        du = _mm_nt(dx1, wo_ref[...])
        gt = gate_ref[...]
        sig = _sigmoid(gt)
        sg = gt * sig
        dsg = sig * (1.0 + gt * (1.0 - sig))
        dom_ref[...] = (du[:, :512] * sg[:, :512]).astype(dom_ref.dtype)
        dos_ref[...] = (du[:, 512:] * sg[:, 512:]).astype(dos_ref.dtype)
        dgate_ref[:, :512] = (du[:, :512] * om_ref[...] * dsg[:, :512]).astype(dgate_ref.dtype)
        dgate_ref[:, 512:] = (du[:, 512:] * os_ref[...] * dsg[:, 512:]).astype(dgate_ref.dtype)

        @pl.when(i == 0)
        def _():
            dgn_ref[...] = jnp.zeros_like(dgn_ref)
            dwo_ref[...] = jnp.zeros_like(dwo_ref)

        dgn_ref[...] += jnp.sum(dg_rows, axis=0, keepdims=True)
        dwo_ref[0:512, :] += _mm_tn(om_ref[...] * sg[:, :512], dx1)
        dwo_ref[512:1024, :] += _mm_tn(os_ref[...] * sg[:, 512:], dx1)

    return _pcall(
        body, name="layer1_in_bwd", grid=(S // TOK,), semantics=("arbitrary",),
        in_specs=[_rows(TOK, D), _rows(TOK, D), _rows(TOK, D), _rows(TOK, D), _rows(TOK, LANES), _rows(TOK, D),
                  _rows(TOK, D), _full((1, D)), _full(w_in1.shape), _rows(TOK, D), _rows(TOK, 512), _rows(TOK, 512),
                  _full((D, D))],
        out_specs=[_rows(TOK, 4224), _rows(TOK, D), _full((1, D)), _full((D, D)), _rows(TOK, 512), _rows(TOK, 512),
                   _rows(TOK, D)],
        out_shape=[_sds((S, 4224), MXU), _sds((S, D), jnp.float32), _sds((1, D), jnp.float32), _sds((D, D), jnp.float32),
                   _sds((S, 512), MXU), _sds((S, 512), MXU), _sds((S, D), MXU)],
    )(dq, dk, dv, dgate1, df, x1, dx2, g1, w_in1, gate0, o_m, o_s, w_out0)


def _layer0_in_bwd(dqm, dkm, dvm, dqs, dkd, dvd, dgate0, cos, sin, cq, ckv, x, dx1, g_in, w_in, g_q, w_q, g_kv, w_kv):
    S = x.shape[0]
    consts = _rope_consts()

    def body(dqm_ref, dkm_ref, dvm_ref, dqs_ref, dkd_ref, dvd_ref, dgate_ref, cos_ref, sin_ref, c_ref, cq_ref, ckv_ref,
             x_ref, dx1_ref, g_ref, w_ref, gq_ref, wq_ref, gkv_ref, wkv_ref,
             dx_ref, dz_ref, dgin_ref, dgq_ref, dgkv_ref, dwq_ref, dwkv_ref, dqu_ref, dkvu_ref):
        i = pl.program_id(0)
        lo = _lane_masks()
        sign = c_ref[...][1:2, :]
        c = cos_ref[...]
        s = sin_ref[...]
        dkpe = None
        for hd in range(N_MLA):
            sl = slice(LANES * hd, LANES * (hd + 1))
            dqu_ref[:, sl] = _rope_t(dqm_ref[:, sl], c, s, sign).astype(dqu_ref.dtype)
            dkh = dkm_ref[:, sl]
            dkvu_ref[:, sl] = jnp.where(lo, dkh, 0.0).astype(dkvu_ref.dtype)
            dkpe = dkh if dkpe is None else dkpe + dkh
        dkvu_ref[:, 1024:1536] = dvm_ref[...]
        dkpe = _rope_t(jnp.where(lo, 0.0, dkpe), c, s, sign)
        dcqn = _mm(dqu_ref[...], wq_ref[...])
        dckvn = _mm_nt(dkvu_ref[...], wkv_ref[...])
        gq = gq_ref[...]
        gkv = gkv_ref[...]
        dcq, dgq_rows = _rms_bwd(cq_ref[...], gq, dcqn)
        dckv, dgkv_rows = _rms_bwd(ckv_ref[...], gkv, dckvn)
        dz_ref[:, 0:256] = dcq.astype(dz_ref.dtype)
        dz_ref[:, 256:384] = dckv.astype(dz_ref.dtype)
        dz_ref[:, 384:512] = dkpe.astype(dz_ref.dtype)
        dz_ref[:, 512:1024] = dqs_ref[...]
        dz_ref[:, 1024:1536] = dkd_ref[...]
        dz_ref[:, 1536:2048] = dvd_ref[...]
        dz_ref[:, 2048:3072] = dgate_ref[...]
        dh = _mm(dz_ref[...], w_ref[...])
        g = g_ref[...]
        dxn, dg_rows = _rms_bwd(x_ref[...], g, dh)
        dx_ref[...] = dx1_ref[...] + dxn

        @pl.when(i == 0)
        def _():
            dgin_ref[...] = jnp.zeros_like(dgin_ref)
            dgq_ref[...] = jnp.zeros_like(dgq_ref)
            dgkv_ref[...] = jnp.zeros_like(dgkv_ref)
            dwq_ref[...] = jnp.zeros_like(dwq_ref)
            dwkv_ref[...] = jnp.zeros_like(dwkv_ref)

        dgin_ref[...] += jnp.sum(dg_rows, axis=0, keepdims=True)
        dgq_ref[...] += jnp.sum(dgq_rows, axis=0, keepdims=True)
        dgkv_ref[...] += jnp.sum(dgkv_rows, axis=0, keepdims=True)
        dwq_ref[...] += _mm_tn(dqu_ref[...], _rms(cq_ref[...], gq))
        dwkv_ref[...] += _mm_tn(_rms(ckv_ref[...], gkv), dkvu_ref[...])

    return _pcall(
        body, name="layer0_in_bwd", grid=(S // TOK,), semantics=("arbitrary",),
        in_specs=[_rows(TOK, 1024), _rows(TOK, 1024), _rows(TOK, 512), _rows(TOK, 512), _rows(TOK, 512), _rows(TOK, 512),
                  _rows(TOK, D), _rows(TOK, LANES), _rows(TOK, LANES), _full((8, LANES)), _rows(TOK, 256), _rows(TOK, 128),
                  _rows(TOK, D), _rows(TOK, D), _full((1, D)), _full(w_in.shape), _full((1, 256)), _full(w_q.shape),
                  _full((1, 128)), _full(w_kv.shape)],
        out_specs=[_rows(TOK, D), _rows(TOK, 3072), _full((1, D)), _full((1, 256)), _full((1, 128)), _full(w_q.shape),
                   _full(w_kv.shape)],
        out_shape=[_sds((S, D), jnp.float32), _sds((S, 3072), MXU), _sds((1, D), jnp.float32), _sds((1, 256), jnp.float32),
                   _sds((1, 128), jnp.float32), _sds(w_q.shape, jnp.float32), _sds(w_kv.shape, jnp.float32)],
        scratch_shapes=[pltpu.VMEM((TOK, 1024), MXU), pltpu.VMEM((TOK, 1536), MXU)],
    )(dqm, dkm, dvm, dqs, dkd, dvd, dgate0, cos, sin, consts, cq, ckv, x, dx1, g_in, w_in, g_q, w_q, g_kv, w_kv)


def _wgrad(a, b, name):
    S, M = a.shape
    N = b.shape[1]
    tm = next(t for t in range(WG_ROWS, 0, -LANES) if M % t == 0)
    tn = N if N <= 1024 else 512
    tk = min(WG_TOK, S)

    def body(a_ref, b_ref, o_ref):
        @pl.when(pl.program_id(2) == 0)
        def _():
            o_ref[...] = jnp.zeros_like(o_ref)

        o_ref[...] += _mm_tn(a_ref[...], b_ref[...])

    return _pcall(
        body, name=name, grid=(M // tm, N // tn, S // tk), semantics=("parallel", "parallel", "arbitrary"),
        in_specs=[pl.BlockSpec((tk, tm), lambda m, n, k: (k, m)), pl.BlockSpec((tk, tn), lambda m, n, k: (k, n))],
        out_specs=pl.BlockSpec((tm, tn), lambda m, n, k: (m, n)),
        out_shape=_sds((M, N), jnp.float32),
    )(a, b)


def _adamw(w, g, m, v, name):
    shape = w.shape
    R, C = (int(np.prod(shape[:-1])), shape[-1])
    w2, g2, m2, v2 = (t.reshape(R, C) for t in (w, g, m, v))
    fits = [t for t in range(8, ADAM_TILE_BYTES // (4 * C) + 1, 8) if R % t == 0]
    tr = max(fits) if fits else R
    tc = C if (tr * C * 4 <= ADAM_TILE_BYTES or C % 256) else 256

    def body(w_ref, g_ref, m_ref, v_ref, d_ref, nm_ref, nv_ref):
        gg = g_ref[...]
        nm = B1 * m_ref[...] + (1.0 - B1) * gg
        nv = B2 * v_ref[...] + (1.0 - B2) * (gg * gg)
        m_hat = nm / (1.0 - B1 ** STEP)
        v_hat = nv / (1.0 - B2 ** STEP)
        d_ref[...] = -LR * (m_hat / (jnp.sqrt(v_hat) + AEPS) + WD * w_ref[...])
        nm_ref[...] = nm
        nv_ref[...] = nv

    spec = pl.BlockSpec((tr, tc), lambda i, j: (i, j))
    d, nm, nv = _pcall(
        body, name=name, grid=(R // tr, C // tc), semantics=("parallel", "parallel"),
        in_specs=[spec] * 4, out_specs=[spec] * 3, out_shape=[_sds((R, C), jnp.float32)] * 3,
    )(w2, g2, m2, v2)
    return d.reshape(shape), nm.reshape(shape), nv.reshape(shape)


def _sum_leading(a, name):
    n, R, C = a.shape
    tr = SUM_ROWS if R % SUM_ROWS == 0 else R

    def body(a_ref, o_ref):
        acc = a_ref[0]
        for i in range(1, n):
            acc = acc + a_ref[i]
        o_ref[...] = acc

    return _pcall(
        body, name=name, grid=(R // tr,), semantics=("parallel",),
        in_specs=[pl.BlockSpec((n, tr, C), lambda i: (0, i, 0))], out_specs=_rows(tr, C),
        out_shape=_sds((R, C), a.dtype),
    )(a)


def _add_halves(g, c, b, name, out_dtype):
    n, _, R, C = g.shape
    tr = SUM_ROWS if R % SUM_ROWS == 0 else R

    def body(c_ref, a_ref, b_ref, o_ref):
        o_ref[...] = (a_ref[0] + b_ref[...]).astype(o_ref.dtype)

    spec = pl.BlockSpec((1, tr, C), lambda k, i, c_ref: (k, i, 0))
    grid_spec = pltpu.PrefetchScalarGridSpec(
        num_scalar_prefetch=1, grid=(n, R // tr),
        in_specs=[pl.BlockSpec((1, 1, tr, C), lambda k, i, c_ref: (k, c_ref[0], i, 0)), spec], out_specs=spec)
    return _pcall(body, name=name, semantics=("parallel", "parallel"), grid_spec=grid_spec,
                  out_shape=_sds(b.shape, out_dtype))(c.reshape(1).astype(jnp.int32), g, b)


def _total_sum(mine, theirs, recv, name):
    R, C = mine.shape
    n = recv.shape[0]
    tr = SUM_ROWS if R % SUM_ROWS == 0 else R

    def body(a_ref, b_ref, r_ref, o_ref):
        acc = a_ref[...] + b_ref[...]
        for i in range(n):
            acc = acc + r_ref[i].astype(jnp.float32)
        o_ref[...] = acc

    return _pcall(
        body, name=name, grid=(R // tr,), semantics=("parallel",),
        in_specs=[_rows(tr, C), _rows(tr, C), pl.BlockSpec((n, tr, C), lambda i: (0, i, 0))], out_specs=_rows(tr, C),
        out_shape=_sds((R, C), jnp.float32),
    )(mine, theirs, recv)


def _place():
    return lax.axis_index("x"), lax.axis_index("y"), lax.axis_index("c")


class _Plan:
    def __init__(self, arrays, out_shape, scratch, start, finish, middle=None):
        self.arrays, self.out_shape, self.scratch = list(arrays), list(out_shape), list(scratch)
        self.start, self.finish, self.middle = start, finish, middle


def _gather8_plan(block):
    R, C = block.shape

    def parts(ins, outs, sems):
        (x_ref,), (out_ref,), (send_sems, recv_sems) = ins, outs, sems
        x, y, c = _place()
        me, sibling = (x, y, c), (x, y, 1 - c)
        chips = [(1 - x, y), (x, 1 - y), (1 - x, 1 - y)]

        def copy(k, blk, to, src=None):
            slot = out_ref.at[4 * blk[0] + 2 * blk[1] + blk[2]]
            return pltpu.make_async_remote_copy(
                src_ref=slot if src is None else src, dst_ref=slot,
                send_sem=send_sems.at[k], recv_sem=recv_sems.at[k], device_id=to, device_id_type=MESH_ID)

        def first():
            return [copy(0, me, sibling, src=x_ref)] + [copy(1 + j, me, (*chip, c), src=x_ref) for j, chip in enumerate(chips)]

        def passed():
            return [copy(4 + j, (*chip, c), sibling) for j, chip in enumerate(chips)]

        def arrivals():
            return [copy(1 + j, (*chip, c), me) for j, chip in enumerate(chips)]

        def late():
            return [copy(0, sibling, me)] + [copy(4 + j, (*chip, 1 - c), me) for j, chip in enumerate(chips)]

        return first, passed, arrivals, late

    def start(ins, outs, sems):
        for cp in parts(ins, outs, sems)[0]():
            cp.start()

    def middle(ins, outs, sems):
        _, passed, arrivals, _ = parts(ins, outs, sems)
        for arrived, forward in zip(arrivals(), passed()):
            arrived.wait_recv()
            forward.start()

    def finish(ins, outs, sems):
        first, passed, _, late = parts(ins, outs, sems)
        for cp in late():
            cp.wait_recv()
        for cp in first() + passed():
            cp.wait_send()

    return _Plan([block], [_sds((8, R, C), block.dtype)], [pltpu.SemaphoreType.DMA((7,)), pltpu.SemaphoreType.DMA((7,))],
                 start, finish, middle)


def _fill_own_slot(gathered, block):
    x, y, c = _place()
    return lax.dynamic_update_index_in_dim(gathered, block, 4 * x + 2 * y + c, 0)


def _started_and_waited(arrays, out_shape, n, copies):
    def start(ins, outs, sems):
        for cp in copies(ins, outs, sems):
            cp.start()

    def finish(ins, outs, sems):
        for cp in copies(ins, outs, sems):
            cp.wait()

    return _Plan(arrays, out_shape, [pltpu.SemaphoreType.DMA((n,)), pltpu.SemaphoreType.DMA((n,))], start, finish)


def _pair_swap_plan(g):
    n = g.shape[0]

    def copies(ins, outs, sems):
        (g_ref,), (out_ref,), (send_sems, recv_sems) = ins, outs, sems
        x, y, c = _place()
        return [pltpu.make_async_remote_copy(src_ref=g_ref.at[k, 1 - c], dst_ref=out_ref.at[k], send_sem=send_sems.at[k],
                                             recv_sem=recv_sems.at[k], device_id=(x, y, 1 - c), device_id_type=MESH_ID)
                for k in range(n)]

    return _started_and_waited([g], [_sds((n,) + g.shape[2:], g.dtype)], n, copies)


def _chip_exchange_plan(p):
    def copies(ins, outs, sems):
        (p_ref,), (out_ref,), (send_sems, recv_sems) = ins, outs, sems
        x, y, c = _place()
        chips = [(1 - x, y), (x, 1 - y), (1 - x, 1 - y)]
        return [pltpu.make_async_remote_copy(
            src_ref=p_ref.at[2 * cx + cy], dst_ref=out_ref.at[j], send_sem=send_sems.at[j],
            recv_sem=recv_sems.at[j], device_id=(cx, cy, c), device_id_type=MESH_ID)
            for j, (cx, cy) in enumerate(chips)]

    return _started_and_waited([p], [_sds((3,) + p.shape[1:], p.dtype)], 3, copies)


def _pair_exchange_plan(t):
    def copies(ins, outs, sems):
        (t_ref,), (out_ref,), (send_sems, recv_sems) = ins, outs, sems
        x, y, c = _place()
        return [pltpu.make_async_remote_copy(src_ref=t_ref, dst_ref=out_ref, send_sem=send_sems.at[0], recv_sem=recv_sems.at[0],
                                             device_id=(x, y, 1 - c), device_id_type=MESH_ID)]

    return _started_and_waited([t], [_sds(t.shape, t.dtype)], 1, copies)


def _both_plans(a, b):
    na, ma, sa = len(a.arrays), len(a.out_shape), len(a.scratch)

    def phase(name):
        fa, fb = getattr(a, name), getattr(b, name)
        if fa is None and fb is None:
            return None

        def run(ins, outs, sems):
            if fa is not None:
                fa(ins[:na], outs[:ma], sems[:sa])
            if fb is not None:
                fb(ins[na:], outs[ma:], sems[sa:])
        return run

    return _Plan(a.arrays + b.arrays, a.out_shape + b.out_shape, a.scratch + b.scratch,
                 phase("start"), phase("finish"), phase("middle"))


ANY_SPEC = pl.BlockSpec(memory_space=pl.ANY)


def _run_plan(plan, name):
    n_in, n_out = len(plan.arrays), len(plan.out_shape)

    def body(*refs):
        ins, outs, sems = refs[:n_in], refs[n_in:n_in + n_out], refs[n_in + n_out:]
        plan.start(ins, outs, sems)
        if plan.middle is not None:
            plan.middle(ins, outs, sems)
        plan.finish(ins, outs, sems)

    return _pcall(body, name=name, in_specs=[ANY_SPEC] * n_in, out_specs=[ANY_SPEC] * n_out, out_shape=plan.out_shape,
                  scratch_shapes=plan.scratch)(*plan.arrays)


def _pcall_riding(body, plan, args, *, name, grid, in_specs, out_specs, out_shape, scratch_shapes):
    if plan is None:
        outs = _pcall(body, name=name, grid=grid, semantics=("arbitrary",), in_specs=in_specs, out_specs=out_specs,
                      out_shape=out_shape, scratch_shapes=scratch_shapes)(*args)
        return list(outs), None
    n_in, n_out, n_s = len(args), len(out_shape), len(scratch_shapes)
    p_in, p_out = len(plan.arrays), len(plan.out_shape)
    steps = grid[0]

    def riding(*refs):
        ins, pins = refs[:n_in], refs[n_in:n_in + p_in]
        o0 = n_in + p_in
        outs, pouts = refs[o0:o0 + n_out], refs[o0 + n_out:o0 + n_out + p_out]
        s0 = o0 + n_out + p_out
        scr, sems = refs[s0:s0 + n_s], refs[s0 + n_s:]
        j = pl.program_id(0)

        @pl.when(j == 0)
        def _():
            plan.start(pins, pouts, sems)

        if plan.middle is not None:
            @pl.when(j == steps // 2)
            def _():
                plan.middle(pins, pouts, sems)

        body(*ins, *outs, *scr)

        @pl.when(j == steps - 1)
        def _():
            plan.finish(pins, pouts, sems)

    res = _pcall(riding, name=name, grid=grid, semantics=("arbitrary",), in_specs=list(in_specs) + [ANY_SPEC] * p_in,
                 out_specs=list(out_specs) + [ANY_SPEC] * p_out, out_shape=list(out_shape) + plan.out_shape,
                 scratch_shapes=list(scratch_shapes) + plan.scratch)(*args, *plan.arrays)
    return list(res[:n_out]), list(res[n_out:])


class _RowSeq:
    def __init__(self, pieces):
        self.pieces = list(pieces)

    def rows(self, a, b):
        out, off = [], 0
        for p in self.pieces:
            lo, hi = max(a, off), min(b, off + p.shape[0])
            if lo < hi:
                out.append(p[lo - off:hi - off])
            off += p.shape[0]
        return out

    def array(self):
        return jnp.concatenate(self.pieces, axis=0)


def _row_seq(w):
    return w if isinstance(w, _RowSeq) else _RowSeq([w])


def _prep_w_in0(wt):
    wt = _row_seq(wt)
    one = wt.pieces[0]
    z32 = [jnp.zeros((32, one.shape[1]), one.dtype)]
    k0, k1 = wt.rows(928, 992), wt.rows(992, 1056)
    v0, v1 = wt.rows(1056, 1120), wt.rows(1120, 1184)
    return jnp.concatenate(wt.rows(0, 384) + z32 + z32 + wt.rows(384, 416) + z32 + wt.rows(416, 928)
                           + k0 * 4 + k1 * 4 + v0 * 4 + v1 * 4 + wt.rows(1184, 2208), axis=0)


def _fold_w_in0(d):
    def fold(blk):
        b = blk.reshape(8, 64, blk.shape[1])
        return jnp.concatenate([b[0] + b[1] + b[2] + b[3], b[4] + b[5] + b[6] + b[7]], axis=0)
    return _RowSeq([d[0:384], d[448:480], d[512:1024], fold(d[1024:1536]), fold(d[1536:2048]), d[2048:3072]])


def _prep_w_q(wt):
    return jnp.pad(wt.reshape(N_MLA, 96, Q_RANK), ((0, 0), (0, 32), (0, 0))).reshape(1024, Q_RANK)


def _fold_w_q(d):
    return d.reshape(N_MLA, 128, Q_RANK)[:, :96].reshape(768, Q_RANK)


def _prep_w_kv(w):
    w3 = w.reshape(KV_RANK, N_MLA, 128)
    kk = jnp.pad(w3[:, :, :64], ((0, 0), (0, 0), (0, 64))).reshape(KV_RANK, 1024)
    return jnp.concatenate([kk, w3[:, :, 64:].reshape(KV_RANK, 512)], axis=1)


def _fold_w_kv(d):
    kk = d[:, :1024].reshape(KV_RANK, N_MLA, 128)[:, :, :64]
    vv = d[:, 1024:].reshape(KV_RANK, N_MLA, 64)
    return jnp.concatenate([kk, vv], axis=2).reshape(KV_RANK, 1024)


def _prep_w_in1(wt):
    wt = _row_seq(wt)
    one = wt.pieces[0]
    return jnp.concatenate(wt.rows(0, 3072) + wt.rows(3088, 4112) + wt.rows(3072, 3088)
                           + [jnp.zeros((112, one.shape[1]), one.dtype)], axis=0)


def _fold_w_in1(d):
    return _RowSeq([d[0:3072], d[4096:4112], d[3072:4096]])


class _Alone:
    def __init__(self, w_out0, o_g_in, w_in1, w_out1):
        self.layer1 = (w_out0, o_g_in, w_in1, w_out1)

    def gather_plan(self):
        return None

    def layer1_weights(self, rode):
        return self.layer1

    def swap_plan(self, grads1):
        return None

    def exchange_plan(self, rode):
        return None

    def finish(self, rode):
        pass


def _local_step(x, pos, target, e_g_in, w_in0, e_g_q, w_q, e_g_kv, w_kv, sinks, b_f, g_final, layer1):
    S = x.shape[0]
    w_in0p, w_qp, w_kvp = _prep_w_in0(w_in0), _prep_w_q(w_q), _prep_w_kv(w_kv)
    slopes = jnp.asarray(2.0 ** (-8.0 * (np.arange(N_SWA, dtype=np.float32) + 1.0) / N_SWA), jnp.float32)
    sinks1 = sinks.reshape(N_SWA)
    b_col = b_f.reshape(N_FOX, 1)

    (h0, cq, ckv, qm, km, vm, qs, kd, vd, gate0, cos, sin) = _layer0_in(
        x, pos, e_g_in, w_in0p, e_g_q, w_qp, e_g_kv, w_kvp)
    o_m, lse_m, rode = _attn_fwd_t(qm, km, vm, (NOPE + ROPE) ** -0.5, split=True, name="mla_fwd", plan=layer1.gather_plan())
    w_out0, o_g_in, w_in1, w_out1 = layer1.layer1_weights(rode)
    w_in1p = _prep_w_in1(w_in1)
    o_s, lse_s = _swa_fwd(qs, kd, vd, sinks1, slopes)
    x1, h1, q1, k1, v1, gate1, f_slab = _layer0_out_layer1_in(x, o_m, o_s, gate0, w_out0, o_g_in, w_in1p)
    f_row = f_slab[:, :N_FOX].T
    lc_row = _forget_fwd(f_row, b_col)
    lcc = lc_row.T
    o1, lse1, _ = _attn_fwd_t(q1, k1, v1, HEAD ** -0.5, split=False, name="fox_fwd", lcc=lcc)
    loss8, dg_final, dw_out1, dx2, do1, dgate1 = _head(x1, o1, gate1, w_out1, g_final, target)

    dq1, dk1, dv1, dlc, _ = _attn_bwd_t(q1, k1, v1, do1, o1, lse1, HEAD ** -0.5, split=False, name="fox_bwd", lcc=lcc)
    df_row, db_f = _forget_bwd(dlc.reshape(N_FOX, S), f_row, b_col)
    df_slab = jnp.pad(df_row.T, ((0, 0), (0, LANES - N_FOX))).astype(MXU)
    dz1, dx1, dg_o_in, dw_out0, do_m, do_s, dgate0 = _layer1_in_bwd(
        dq1, dk1, dv1, dgate1, df_slab, x1, dx2, o_g_in, w_in1p, gate0, o_m, o_s, w_out0)
    grads1 = dict(o_g_in=dg_o_in, o_w_in=_fold_w_in1(_wgrad(dz1, h1, "wgrad_in1")), o_w_out=dw_out1, e_w_out=dw_out0)
    dqs, dkd, dvd, dsink, rode = _swa_bwd(qs, kd, vd, do_s, o_s, lse_s, sinks1, slopes, plan=layer1.swap_plan(grads1))
    dqm, dkm, dvm, rode = _attn_bwd_t(qm, km, vm, do_m, o_m, lse_m, (NOPE + ROPE) ** -0.5, split=True, name="mla_bwd",
                                      plan=layer1.exchange_plan(rode))
    layer1.finish(rode)
    dx, dz0, dg_in, dg_q, dg_kv, dw_q, dw_kv = _layer0_in_bwd(
        dqm, dkm, dvm, dqs, dkd, dvd, dgate0, cos, sin, cq, ckv, x, dx1, e_g_in, w_in0p, e_g_q, w_qp, e_g_kv, w_kvp)

    grads = dict(
        e_g_in=dg_in,
        e_w_in=_fold_w_in0(_wgrad(dz0, h0, "wgrad_in0")),
        e_g_q_a=dg_q,
        e_w_q_up=_fold_w_q(dw_q),
        e_g_kv_a=dg_kv,
        e_w_kv_up=_fold_w_kv(dw_kv),
        e_sinks=dsink[:, 0:4, 0].reshape(1, N_SWA),
        o_b_f=db_f.reshape(1, N_FOX),
        g_final=dg_final,
        **grads1,
    )
    return loss8[0, 0], dx, grads


SHARDED = ("e_w_in", "e_w_q_up", "e_w_kv_up", "e_w_out", "o_g_in", "o_w_in", "o_w_out")
TRANSPOSED = ("e_w_in", "e_w_q_up", "o_w_in")
COL_SHARDED = ("e_w_kv_up", "o_g_in")
REPLICATED = ("e_g_in", "e_g_q_a", "e_g_kv_a", "e_sinks", "o_b_f", "g_final")
FULL_SHAPES = dict(e_w_in=(2208, 1024), e_w_q_up=(768, 256), e_w_kv_up=(128, 1024), e_w_out=(1024, 1024),
                   o_g_in=(1, 1024), o_w_in=(4112, 1024), o_w_out=(1024, 1024))
GROUPS = dict(
    layer0=dict(rows=768, windows=dict(e_w_in=(0, 0), e_w_q_up=(560, 0), e_w_kv_up=(560, 256))),
    layer1=dict(rows=1568, windows=dict(o_w_in=(0, 0), o_w_out=(1040, 0), e_w_out=(1296, 0), o_g_in=(1552, 0))),
)


def _shard_shape(name):
    r, c = FULL_SHAPES[name]
    return (r, c // 4) if name in COL_SHARDED else (r // 4, c)


def _as_handled(name, a):
    a = a[0] if a.ndim == 3 else a
    return a.T if name in TRANSPOSED else a


def _as_given(name, a, shape):
    return (a.T if name in TRANSPOSED else a).reshape(shape)


def _pack_block(p, group):
    def rows(a, n):
        return jnp.pad(a, ((0, n - a.shape[0]), (0, 0)))

    if group == "layer0":
        band = jnp.concatenate([p["e_w_q_up"], rows(p["e_w_kv_up"], 192), jnp.zeros((192, 512), p["e_w_in"].dtype)], axis=1)
        return jnp.concatenate([rows(p["e_w_in"], 560), rows(band, 208)], axis=0)
    g = p["o_g_in"]
    band = jnp.pad(g, ((0, 16 - g.shape[0]), (0, PACK_COLS - g.shape[1])))
    return jnp.concatenate([rows(p["o_w_in"], 1040), p["o_w_out"], p["e_w_out"], band], axis=0)


def _window(block, group, name, width=None):
    r0, c0 = GROUPS[group]["windows"][name]
    r, c = _shard_shape(name)
    return block[..., r0:r0 + r, c0:c0 + (c if width is None else width)]


def _chip_slice(name, full, k):
    r, c = _shard_shape(name)
    if isinstance(full, _RowSeq):
        return jnp.concatenate(full.rows(r * k, r * (k + 1)), axis=0)
    return full[:, c * k:c * (k + 1)] if name in COL_SHARDED else full[r * k:r * (k + 1), :]


def _packed_weights(w, group):
    parts = {}
    for n in GROUPS[group]["windows"]:
        a = _as_handled(n, w[n])
        parts[n] = lax.bitcast_convert_type(a, jnp.bfloat16).reshape(1, -1) if n == "o_g_in" else a.astype(jnp.bfloat16)
    halves = _pack_block(parts, group).reshape(2, GROUPS[group]["rows"] // 2, PACK_COLS)
    return lax.dynamic_index_in_dim(halves, lax.axis_index("c"), 0, keepdims=False)


def _unpacked_weights(gathered, half, group):
    blocks = _fill_own_slot(gathered, half).reshape(4, GROUPS[group]["rows"], PACK_COLS)
    full = {}
    for n in GROUPS[group]["windows"]:
        if n == "o_g_in":
            halves = _window(blocks, group, n, width=512).reshape(4, 1, 256, 2)
            full[n] = jnp.concatenate(list(lax.bitcast_convert_type(halves, jnp.float32)), axis=1)
        else:
            pieces = [_window(blocks[k], group, n).astype(MXU) for k in range(4)]
            if n in ("e_w_in", "o_w_in"):
                full[n] = _RowSeq(pieces)
            else:
                full[n] = jnp.concatenate(pieces, axis=1 if n in COL_SHARDED else 0)
    return full


class _GroupReduce:
    def __init__(self, group):
        self.group = group
        self.c = lax.axis_index("c")
        self.chip = 2 * lax.axis_index("x") + lax.axis_index("y")

    def swap_plan(self, grads):
        names = GROUPS[self.group]["windows"]
        per_chip = jnp.stack([_pack_block({n: _chip_slice(n, grads[n], k) for n in names}, self.group) for k in range(4)])
        self.g4 = per_chip.reshape(4, 2, GROUPS[self.group]["rows"] // 2, PACK_COLS)
        return _pair_swap_plan(self.g4)

    def exchange_plan(self, rode):
        theirs = rode[0]
        rows = self.g4.shape[2]
        self.own = (lax.dynamic_slice(self.g4, (self.chip, self.c, 0, 0), (1, 1, rows, PACK_COLS)).reshape(rows, PACK_COLS),
                    lax.dynamic_index_in_dim(theirs, self.chip, 0, keepdims=False))
        return _chip_exchange_plan(_add_halves(self.g4, self.c, theirs, "pair_add_" + self.group, jnp.bfloat16))

    def finish(self, rode):
        my_half = _total_sum(*self.own, rode[0], "chip_sum_" + self.group)
        other_half = _run_plan(_pair_exchange_plan(my_half), "pair_exchange_" + self.group)[0]
        total = jnp.concatenate([jnp.where(self.c == 0, my_half, other_half), jnp.where(self.c == 0, other_half, my_half)], axis=0)
        self.sums = {n: _window(total, self.group, n) for n in GROUPS[self.group]["windows"]}

    def run(self, grads, beside):
        swap = self.swap_plan(grads)
        outs = _run_plan(_both_plans(swap, beside), "pair_swap_" + self.group)
        rode, others = outs[:len(swap.out_shape)], outs[len(swap.out_shape):]
        self.finish(_run_plan(self.exchange_plan(rode), "chip_exchange_" + self.group))
        return self.sums, others


class _Layer1Exchange(_GroupReduce):
    def __init__(self, w):
        super().__init__("layer1")
        self.half = _packed_weights(w, "layer1")

    def gather_plan(self):
        return _gather8_plan(self.half)

    def layer1_weights(self, rode):
        full = _unpacked_weights(rode[0], self.half, "layer1")
        return full["e_w_out"], full["o_g_in"], full["o_w_in"], full["o_w_out"]


def kernel(x, positions, e_g_in, e_w_in, e_g_q_a, e_w_q_up, e_g_kv_a, e_w_kv_up, e_sinks, e_w_out, o_g_in, o_w_in, o_b_f, o_w_out, g_final, loss_target, m_e_g_in, m_e_w_in, m_e_g_q_a, m_e_w_q_up, m_e_g_kv_a, m_e_w_kv_up, m_e_sinks, m_e_w_out, m_o_g_in, m_o_w_in, m_o_b_f, m_o_w_out, m_g_final, v_e_g_in, v_e_w_in, v_e_g_q_a, v_e_w_q_up, v_e_g_kv_a, v_e_w_kv_up, v_e_sinks, v_e_w_out, v_o_g_in, v_o_w_in, v_o_b_f, v_o_w_out, v_g_final):
    w = dict(e_g_in=e_g_in, e_w_in=e_w_in, e_g_q_a=e_g_q_a, e_w_q_up=e_w_q_up, e_g_kv_a=e_g_kv_a, e_w_kv_up=e_w_kv_up,
             e_sinks=e_sinks, e_w_out=e_w_out, o_g_in=o_g_in, o_w_in=o_w_in, o_b_f=o_b_f, o_w_out=o_w_out, g_final=g_final)
    m = dict(e_g_in=m_e_g_in, e_w_in=m_e_w_in, e_g_q_a=m_e_g_q_a, e_w_q_up=m_e_w_q_up, e_g_kv_a=m_e_g_kv_a,
             e_w_kv_up=m_e_w_kv_up, e_sinks=m_e_sinks, e_w_out=m_e_w_out, o_g_in=m_o_g_in, o_w_in=m_o_w_in, o_b_f=m_o_b_f,
             o_w_out=m_o_w_out, g_final=m_g_final)
    v = dict(e_g_in=v_e_g_in, e_w_in=v_e_w_in, e_g_q_a=v_e_g_q_a, e_w_q_up=v_e_w_q_up, e_g_kv_a=v_e_g_kv_a,
             e_w_kv_up=v_e_w_kv_up, e_sinks=v_e_sinks, e_w_out=v_e_w_out, o_g_in=v_o_g_in, o_w_in=v_o_w_in, o_b_f=v_o_b_f,
             o_w_out=v_o_w_out, g_final=v_g_final)
    order = ("e_g_in", "e_w_in", "e_g_q_a", "e_w_q_up", "e_g_kv_a", "e_w_kv_up", "e_sinks", "e_w_out", "o_g_in", "o_w_in",
             "o_b_f", "o_w_out", "g_final")
    half0 = _packed_weights(w, "layer0")
    full = _unpacked_weights(_run_plan(_gather8_plan(half0), "gather_weights_layer0")[0], half0, "layer0")
    layer1 = _Layer1Exchange(w)

    loss_part, dx, grads = _local_step(
        x[0], positions.reshape(-1, 1), loss_target[0], e_g_in, full["e_w_in"], e_g_q_a, full["e_w_q_up"], e_g_kv_a,
        full["e_w_kv_up"], e_sinks, o_b_f, g_final.reshape(1, D), layer1)

    small = jnp.concatenate([jnp.pad(loss_part.reshape(1), (0, LANES - 1))]
                            + [jnp.pad(grads[n].reshape(-1), (0, (-grads[n].size) % LANES)) for n in REPLICATED])
    rows = small.shape[0] // LANES
    small = jnp.pad(small.reshape(rows, LANES), ((0, (-rows) % 8), (0, 0)))
    sums0, (gathered_small,) = _GroupReduce("layer0").run(grads, _gather8_plan(small))
    gsum = {**layer1.sums, **sums0}
    ssum = _sum_leading(_fill_own_slot(gathered_small, small), "small_grad_sum").reshape(-1)
    loss = ssum[0]
    off = LANES
    for n in REPLICATED:
        cnt = w[n].size
        gsum[n] = ssum[off:off + cnt].reshape(w[n].shape)
        off += cnt + (-cnt) % LANES

    grad, delta, new_m, new_v = {}, {}, {}, {}
    for n in order:
        if n == "o_w_in":
            def tiles(a):
                return jnp.transpose(a, (2, 0, 1)).reshape(-1, LANES)

            def given(a):
                return jnp.transpose(a.reshape(-1, 8, LANES), (1, 2, 0)).reshape(w[n].shape)

            g_t = gsum[n].reshape(-1, LANES)
            outs = _adamw(tiles(w[n]), g_t, tiles(m[n]), tiles(v[n]), "adamw_" + n)
            grad[n], delta[n], new_m[n], new_v[n] = (given(a) for a in (g_t,) + outs)
        elif n in SHARDED:
            outs = _adamw(_as_handled(n, w[n]), gsum[n], _as_handled(n, m[n]), _as_handled(n, v[n]), "adamw_" + n)
            grad[n], delta[n], new_m[n], new_v[n] = (_as_given(n, a, w[n].shape) for a in (gsum[n],) + outs)
        else:
            grad[n] = gsum[n]
            delta[n], new_m[n], new_v[n] = _adamw(w[n], gsum[n], m[n], v[n], "adamw_" + n)
    return (loss, dx[None], *[grad[n] for n in order], *[delta[n] for n in order], *[new_m[n] for n in order],
            *[new_v[n] for n in order])
```

```python
import math

import numpy as np
import jax
import jax.numpy as jnp
from jax import lax
from jax.experimental import pallas as pl
from jax.experimental.pallas import tpu as pltpu

D = 1024
EPS = 1e-6
ROPE_THETA = 10000.0
N_MLA = 8
Q_RANK = 256
KV_RANK = 128
NOPE = 64
ROPE = 32
N_SWA = 8
WINDOW = 128
N_FOX = 16
HEAD = 64
LR, B1, B2, AEPS, WD, STEP = 0.001, 0.9, 0.999, 1e-08, 0.01, 10

LANES = 128
HALF = 64
VMEM_LIMIT = 56 * 1024 * 1024
MXU = jnp.bfloat16
TOK = 256
WG_TOK = 2048
WG_ROWS = 1536
ATT = 256
FWD_CHUNK = 2
BWD_CHUNK = 2
SWA_GROUP = 8
NEG = float("-inf")

PACK_COLS = 1024
SUM_ROWS = 256
ADAM_TILE_BYTES = 2 << 20
MESH_ID = pl.DeviceIdType.MESH


def _pcall(body, *, name, vmem=VMEM_LIMIT, semantics=None, **kw):
    params = dict(vmem_limit_bytes=vmem)
    if semantics is not None:
        params["dimension_semantics"] = semantics
    return pl.pallas_call(body, name=name, compiler_params=pltpu.CompilerParams(**params), **kw)


def _mm(a, b):
    return jnp.dot(a.astype(MXU), b.astype(MXU), preferred_element_type=jnp.float32)


def _mm_nt(a, b):
    return lax.dot_general(a.astype(MXU), b.astype(MXU), (((1,), (1,)), ((), ())),
                           preferred_element_type=jnp.float32)


def _mm_tn(a, b):
    return lax.dot_general(a.astype(MXU), b.astype(MXU), (((0,), (0,)), ((), ())),
                           preferred_element_type=jnp.float32)


def _full(shape):
    n = len(shape)
    return pl.BlockSpec(shape, lambda *_: (0,) * n)


def _rows(tm, n):
    return pl.BlockSpec((tm, n), lambda i: (i, 0))


def _sds(shape, dtype):
    return jax.ShapeDtypeStruct(shape, dtype)


def _rms(x, g):
    r = lax.rsqrt(jnp.mean(x * x, axis=-1, keepdims=True) + EPS)
    return x * r * g


def _rms_bwd(x, g, dy):
    r = lax.rsqrt(jnp.mean(x * x, axis=-1, keepdims=True) + EPS)
    xh = x * r
    dxh = dy * g
    dx = r * (dxh - xh * jnp.mean(dxh * xh, axis=-1, keepdims=True))
    return dx, dy * xh


def _sigmoid(x):
    return 1.0 / (1.0 + jnp.exp(-x))


def _lane_masks():
    lane = lax.broadcasted_iota(jnp.int32, (1, LANES), 1)
    return lane < HALF


def _split_heads(a, lo):
    z = jnp.zeros_like(a)
    return [jnp.where(lo, a, z), jnp.where(lo, z, a)]


def _rope_consts():
    inv = np.zeros((8, LANES), np.float32)
    j = np.arange(ROPE // 2, dtype=np.float32)
    f = (1.0 / (ROPE_THETA ** (np.arange(0, ROPE, 2, dtype=np.float32) / ROPE))).astype(np.float32)
    inv[0, HALF:HALF + 16] = f
    inv[0, HALF + 16:HALF + 32] = f
    inv[1, HALF:HALF + 16] = -1.0
    inv[1, HALF + 16:HALF + 32] = 1.0
    del j
    return jnp.asarray(inv)


def _rope_tables(pos_f, consts):
    ang = pos_f * consts[0:1, :]
    sign = consts[1:2, :]
    c = jnp.where(sign != 0.0, jnp.cos(ang), 1.0)
    s = jnp.sin(ang) * sign
    return c, s


def _swap_halves(v, sign):
    lo = pltpu.roll(v, LANES - 16, axis=1)
    hi = pltpu.roll(v, 16, axis=1)
    return jnp.where(sign < 0.0, lo, jnp.where(sign > 0.0, hi, 0.0))


def _rope(x, c, s, sign):
    return x * c + _swap_halves(x, sign) * s


def _rope_t(dy, c, s, sign):
    return dy * c + _swap_halves(dy * s, sign)


def _layer0_in(x, pos, g_in, w_in, g_q, w_q, g_kv, w_kv):
    S = x.shape[0]
    consts = _rope_consts()

    def body(x_ref, pos_ref, c_ref, g_ref, w_ref, gq_ref, wq_ref, gkv_ref, wkv_ref,
             h_ref, cq_ref, ckv_ref, qm_ref, km_ref, vm_ref,
             qs_ref, kd_ref, vd_ref, gate_ref, cos_ref, sin_ref):
        h = _rms(x_ref[...], g_ref[...])
        h_ref[...] = h.astype(h_ref.dtype)
        z = _mm_nt(h, w_ref[...])
        cq = z[:, 0:256]
        ckv = z[:, 256:384]
        kpe = z[:, 384:512]
        cq_ref[...] = cq
        ckv_ref[...] = ckv
        qs_ref[...] = z[:, 512:1024].astype(qs_ref.dtype)
        kd_ref[...] = z[:, 1024:1536].astype(kd_ref.dtype)
        vd_ref[...] = z[:, 1536:2048].astype(vd_ref.dtype)
        gate_ref[...] = z[:, 2048:3072]
        cqn = _rms(cq, gq_ref[...])
        ckvn = _rms(ckv, gkv_ref[...])
        q = _mm_nt(cqn, wq_ref[...])
        kv = _mm(ckvn, wkv_ref[...])
        vm_ref[...] = kv[:, 1024:1536].astype(vm_ref.dtype)
        consts_v = c_ref[...]
        sign = consts_v[1:2, :]
        c, s = _rope_tables(pos_ref[...].astype(jnp.float32), consts_v)
        cos_ref[...] = c
        sin_ref[...] = s
        kpe_r = _rope(kpe, c, s, sign)
        for hd in range(N_MLA):
            sl = slice(LANES * hd, LANES * (hd + 1))
            qm_ref[:, sl] = _rope(q[:, sl], c, s, sign).astype(qm_ref.dtype)
            km_ref[:, sl] = (kv[:, sl] + kpe_r).astype(km_ref.dtype)

    outs = [
        ((S, D), MXU), ((S, 256), jnp.float32), ((S, 128), jnp.float32),
        ((S, 1024), MXU), ((S, 1024), MXU), ((S, 512), MXU), ((S, 512), MXU), ((S, 512), MXU), ((S, 512), MXU),
        ((S, 1024), jnp.float32), ((S, 128), jnp.float32), ((S, 128), jnp.float32),
    ]
    return _pcall(
        body, name="layer0_in", grid=(S // TOK,), semantics=("arbitrary",),
        in_specs=[_rows(TOK, D), _rows(TOK, 1), _full((8, LANES)), _full((1, D)), _full(w_in.shape), _full((1, 256)),
                  _full(w_q.shape), _full((1, 128)), _full(w_kv.shape)],
        out_specs=[_rows(TOK, s[1]) for s, _ in outs],
        out_shape=[_sds(s, d) for s, d in outs],
    )(x, pos, consts, g_in, w_in, g_q, w_q, g_kv, w_kv)


AUG = (HALF, 0)
ONE = (HALF + 8, 8)


def _data_lanes(idx, h):
    return (idx < HALF) if h == 0 else (idx >= HALF)


def _three_terms(x):
    hi = x.astype(MXU).astype(jnp.float32)
    mid = (x - hi).astype(MXU).astype(jnp.float32)
    lo = (x - hi - mid).astype(MXU).astype(jnp.float32)
    return hi, mid, lo


def _q_aug(qblk, lc, h, scale, lane):
    a = AUG[h]
    hi, mid, lo = _three_terms(lc)
    ones = ((lane >= a + 3) & (lane <= a + 5)).astype(jnp.float32)
    aug = jnp.where(lane == a, hi, jnp.where(lane == a + 1, mid, jnp.where(lane == a + 2, lo, ones)))
    return jnp.where(_data_lanes(lane, h), qblk * jnp.asarray(scale, qblk.dtype), aug.astype(qblk.dtype))


def _k_aug(kblk, lc, h, lane):
    a = AUG[h]
    hi, mid, lo = _three_terms(-lc)
    ones = ((lane >= a) & (lane <= a + 2)).astype(jnp.float32)
    aug = jnp.where(lane == a + 3, hi, jnp.where(lane == a + 4, mid, jnp.where(lane == a + 5, lo, ones)))
    return jnp.where(_data_lanes(lane, h), kblk, aug.astype(kblk.dtype))


def _lc_col(lc_ref, r0, rows, h):
    head = lax.broadcasted_iota(jnp.int32, (1, lc_ref.shape[1]), 1)
    return jnp.sum(jnp.where(head == 2 * pl.program_id(0) + h, lc_ref[pl.ds(r0, rows), :], 0.0), axis=1, keepdims=True)


def _attn_fwd_t(q, k, v, scale, *, split, name, lcc=None, plan=None):
    S = q.shape[0]
    npair = v.shape[1] // LANES
    W = 2 * LANES if split else LANES
    T = ATT
    CH = FWD_CHUNK * T
    assert S % CH == 0
    nq = S // T

    def body(*refs):
        if split:
            q_ref, k_ref, v_ref, o_ref, lse_ref, vt, acc, m_sc = refs
        else:
            q_ref, k_ref, v_ref, lcc_ref, o_ref, lse_ref, kaug, vt, acc, m_sc = refs
        lane = lax.broadcasted_iota(jnp.int32, (1, LANES), 1)
        sub = lax.broadcasted_iota(jnp.int32, (LANES, 1), 0)
        key_minus_qry = lax.broadcasted_iota(jnp.int32, (CH, T), 0) - lax.broadcasted_iota(jnp.int32, (CH, T), 1)

        def prep(i, c):
            r0 = pl.multiple_of(i * T, T)
            vblk = v_ref[pl.ds(r0, T), :].astype(jnp.float32)
            for h in (0, 1):
                vh = jnp.where(_data_lanes(lane, h), vblk, (lane == ONE[h]).astype(jnp.float32))
                vt[h, :, pl.ds(r0, T)] = vh.T.astype(vt.dtype)
                if not split:
                    kaug[h, pl.ds(r0, T), :] = _k_aug(k_ref[pl.ds(r0, T), :], _lc_col(lcc_ref, r0, T, h), h, lane)
            return c

        lax.fori_loop(0, nq, prep, 0)

        def queries(qi):
            q0 = pl.multiple_of(qi * T, T)
            qblk = q_ref[pl.ds(q0, T), :]
            if split:
                return (qblk[:, :LANES], qblk[:, LANES:])
            return tuple(_q_aug(qblk, _lc_col(lcc_ref, q0, T, h), h, scale, lane) for h in (0, 1))

        def scores(qs, c):
            k0 = pl.multiple_of(c * CH, CH)
            out = []
            for h in (0, 1):
                if split:
                    out.append(_mm_nt(k_ref[pl.ds(k0, CH), LANES * h:LANES * (h + 1)], qs[h]) * scale)
                else:
                    out.append(_mm_nt(kaug[h, pl.ds(k0, CH), :], qs[h]))
            return tuple(out)

        def q_block(qi, carry):
            qs, first_scores = carry[:2], carry[2:]
            q0 = pl.multiple_of(qi * T, T)
            acc[...] = jnp.zeros_like(acc)
            m_sc[...] = jnp.full(m_sc.shape, NEG, jnp.float32)

            def absorb(c, sts, masked):
                k0 = pl.multiple_of(c * CH, CH)
                for h in (0, 1):
                    st = sts[h]
                    if masked:
                        st = jnp.where(key_minus_qry <= q0 - k0, st, NEG)
                    m_old = m_sc[h:h + 1, :]
                    m_new = jnp.maximum(m_old, jnp.max(st, axis=0, keepdims=True))
                    alpha = jnp.exp(m_old - m_new)
                    pt = jnp.exp(st - m_new)
                    acc[h] = alpha * acc[h] + _mm(vt[h, :, pl.ds(k0, CH)], pt)
                    m_sc[h:h + 1, :] = m_new

            last = qi // FWD_CHUNK

            def pipelined(c, sts):
                nxt = scores(qs, c + 1)
                absorb(c, sts, False)
                return nxt

            sts = lax.fori_loop(0, last, pipelined, first_scores)
            qs_next = queries(jnp.minimum(qi + 1, nq - 1))
            nxt = qs_next + scores(qs_next, 0)
            absorb(last, sts, True)
            ot = None
            for h in (0, 1):
                a = acc[h]
                l = a[ONE[h]:ONE[h] + 1, :]
                oh = jnp.where(_data_lanes(sub, h), a * (1.0 / l), 0.0)
                ot = oh if ot is None else ot + oh
                lse_ref[0, h:h + 1, pl.ds(q0, T)] = m_sc[h:h + 1, :] + jnp.log(l)
            o_ref[pl.ds(q0, T), :] = ot.T
            return nxt

        qs0 = queries(0)
        lax.fori_loop(0, nq, q_block, qs0 + scores(qs0, 0))

    wide = pl.BlockSpec((S, W), lambda j: (0, j))
    slab = pl.BlockSpec((S, LANES), lambda j: (0, j))
    rows = pl.BlockSpec((1, 2, S), lambda j: (j, 0, 0))
    in_specs = [wide, wide, slab]
    args = [q, k, v]
    scratch = []
    if not split:
        in_specs.append(_full(lcc.shape))
        args.append(lcc)
        scratch.append(pltpu.VMEM((2, S, LANES), MXU))
    scratch += [pltpu.VMEM((2, LANES, S), MXU), pltpu.VMEM((2, LANES, T), jnp.float32), pltpu.VMEM((8, T), jnp.float32)]
    (o, lse), rode = _pcall_riding(
        body, plan, args, name=name, grid=(npair,), in_specs=in_specs, out_specs=[slab, rows],
        out_shape=[_sds((S, npair * LANES), jnp.float32), _sds((npair, 2, S), jnp.float32)], scratch_shapes=scratch)
    return o, lse, rode


def _attn_bwd_t(q, k, v, do, o, lse, scale, *, split, name, lcc=None, plan=None):
    S = q.shape[0]
    npair = v.shape[1] // LANES
    W = 2 * LANES if split else LANES
    T = ATT
    CH = BWD_CHUNK * T
    assert S % CH == 0
    nq = S // T

    def body(*refs):
        if split:
            (q_ref, k_ref, v_ref, do_ref, o_ref, lse_ref, dq_ref, dk_ref, dv_ref, dqt, delta, dk_acc, dv_acc) = refs
        else:
            (q_ref, k_ref, v_ref, do_ref, o_ref, lse_ref, lcc_ref, dq_ref, dk_ref, dv_ref, dlc_ref,
             dqt, delta, dk_acc, dv_acc, qaug, csum) = refs
        lane = lax.broadcasted_iota(jnp.int32, (1, LANES), 1)
        sub = lax.broadcasted_iota(jnp.int32, (LANES, 1), 0)
        key_minus_qry = lax.broadcasted_iota(jnp.int32, (T, CH), 0) - lax.broadcasted_iota(jnp.int32, (T, CH), 1)

        def prep(i, c):
            r0 = pl.multiple_of(i * T, T)
            prod_t = (do_ref[pl.ds(r0, T), :].astype(jnp.float32) * o_ref[pl.ds(r0, T), :]).T
            for h in (0, 1):
                delta[h:h + 1, pl.ds(r0, T)] = jnp.sum(jnp.where(_data_lanes(sub, h), prod_t, 0.0), axis=0, keepdims=True)
                dqt[h, :, pl.ds(r0, T)] = jnp.zeros((LANES, T), jnp.float32)
                if not split:
                    qaug[h, pl.ds(r0, T), :] = _q_aug(q_ref[pl.ds(r0, T), :], _lc_col(lcc_ref, r0, T, h), h, scale, lane)
            return c

        lax.fori_loop(0, nq, prep, 0)

        def keys(ki):
            k0 = pl.multiple_of(ki * T, T)
            kblk = k_ref[pl.ds(k0, T), :]
            if split:
                return (kblk[:, :LANES], kblk[:, LANES:])
            return tuple(_k_aug(kblk, _lc_col(lcc_ref, k0, T, h), h, lane) for h in (0, 1))

        def q_of(c, h):
            q0 = pl.multiple_of(c * CH, CH)
            if split:
                return q_ref[pl.ds(q0, CH), LANES * h:LANES * (h + 1)]
            return qaug[h, pl.ds(q0, CH), :]

        def scores(khs, c):
            out = []
            for h in (0, 1):
                st = _mm_nt(khs[h], q_of(c, h))
                out.append(st * scale if split else st)
            return tuple(out)

        def k_block(ki, carry):
            khs, first_scores = carry[:2], carry[2:]
            k0 = pl.multiple_of(ki * T, T)
            khts = [kh.astype(jnp.float32).T.astype(kh.dtype) for kh in khs]
            vhs = _split_heads(v_ref[pl.ds(k0, T), :], lane < HALF)
            dk_acc[...] = jnp.zeros_like(dk_acc)
            dv_acc[...] = jnp.zeros_like(dv_acc)

            def absorb(c, vals):
                q0 = pl.multiple_of(c * CH, CH)
                dos = _split_heads(do_ref[pl.ds(q0, CH), :], lane < HALF)
                visible = key_minus_qry <= q0 - k0
                for h in (0, 1):
                    dpt = _mm_nt(vhs[h], dos[h])
                    st = jnp.where(visible, vals[h], NEG)
                    pt = jnp.exp(st - lse_ref[0, h:h + 1, pl.ds(q0, CH)])
                    dv_acc[...] += _mm(pt, dos[h])
                    dst = pt * (dpt - delta[h:h + 1, pl.ds(q0, CH)])
                    dk_acc[h] += _mm(dst, q_of(c, h))
                    dqt[h, :, pl.ds(q0, CH)] += _mm(khts[h], dst)

            first = ki // BWD_CHUNK

            def pipelined(c, vals):
                nxt = scores(khs, c + 1)
                absorb(c, vals)
                return nxt

            vals = lax.fori_loop(first, S // CH - 1, pipelined, first_scores)
            kn = jnp.minimum(ki + 1, nq - 1)
            khs_next = keys(kn)
            nxt = khs_next + scores(khs_next, kn // BWD_CHUNK)
            absorb(S // CH - 1, vals)
            if split:
                dk_ref[pl.ds(k0, T), :LANES] = (dk_acc[0] * scale).astype(dk_ref.dtype)
                dk_ref[pl.ds(k0, T), LANES:] = (dk_acc[1] * scale).astype(dk_ref.dtype)
            else:
                dk_ref[pl.ds(k0, T), :] = jnp.where(lane < HALF, dk_acc[0], dk_acc[1]).astype(dk_ref.dtype)
                for h in (0, 1):
                    csum[h:h + 1, pl.ds(k0, T)] = dk_acc[h].T[AUG[h] + 3:AUG[h] + 4, :]
            dv_ref[pl.ds(k0, T), :] = dv_acc[...].astype(dv_ref.dtype)
            return nxt

        khs0 = keys(0)
        lax.fori_loop(0, nq, k_block, khs0 + scores(khs0, 0))

        def finish(i, c):
            r0 = pl.multiple_of(i * T, T)
            if split:
                for h in (0, 1):
                    dq_ref[pl.ds(r0, T), LANES * h:LANES * (h + 1)] = (dqt[h, :, pl.ds(r0, T)].T * scale).astype(dq_ref.dtype)
            else:
                d = jnp.where(sub < HALF, dqt[0, :, pl.ds(r0, T)], dqt[1, :, pl.ds(r0, T)])
                dq_ref[pl.ds(r0, T), :] = (d.T * scale).astype(dq_ref.dtype)
                for h in (0, 1):
                    dlc_ref[0, h:h + 1, pl.ds(r0, T)] = dqt[h, AUG[h]:AUG[h] + 1, pl.ds(r0, T)] - csum[h:h + 1, pl.ds(r0, T)]
            return c

        lax.fori_loop(0, nq, finish, 0)

    wide = pl.BlockSpec((S, W), lambda j: (0, j))
    slab = pl.BlockSpec((S, LANES), lambda j: (0, j))
    rows = pl.BlockSpec((1, 2, S), lambda j: (j, 0, 0))
    in_specs = [wide, wide, slab, slab, slab, rows]
    args = [q, k, v, do, o, lse]
    out_specs = [wide, wide, slab]
    out_shape = [_sds(q.shape, jnp.float32 if split else do.dtype), _sds(k.shape, jnp.float32 if split else do.dtype),
                 _sds(v.shape, do.dtype)]
    scratch = [pltpu.VMEM((2, LANES, S), jnp.float32), pltpu.VMEM((8, S), jnp.float32),
               pltpu.VMEM((2, T, LANES), jnp.float32), pltpu.VMEM((T, LANES), jnp.float32)]
    if not split:
        in_specs.append(_full(lcc.shape))
        args.append(lcc)
        out_specs.append(rows)
        out_shape.append(_sds((npair, 2, S), jnp.float32))
        scratch += [pltpu.VMEM((2, S, LANES), MXU), pltpu.VMEM((8, S), jnp.float32)]
    outs, rode = _pcall_riding(body, plan, args, name=name, grid=(npair,), in_specs=in_specs, out_specs=out_specs,
                               out_shape=out_shape, scratch_shapes=scratch)
    return (*outs, rode)


def _swa_bias(slope, shift):
    a = lax.broadcasted_iota(jnp.int32, (WINDOW, 2 * WINDOW), 0)
    c = lax.broadcasted_iota(jnp.int32, (WINDOW, 2 * WINDOW), 1)
    dist = a - c + shift
    return jnp.where((dist >= 0) & (dist < WINDOW), -slope * dist.astype(jnp.float32), NEG)


def _swa_scores(qh, kblk, bias):
    return _mm_nt(qh, kblk) * (HEAD ** -0.5) + bias


def _swa_stack(blk, lo):
    return jnp.concatenate(_split_heads(blk[:, :LANES], lo) + _split_heads(blk[:, LANES:], lo), axis=0)


def _swa_unstack(x, lo):
    r = x.shape[0] // 4
    return jnp.concatenate([jnp.where(lo, x[0:r], x[r:2 * r]), jnp.where(lo, x[2 * r:3 * r], x[3 * r:])], axis=1)


def _swa_per_head(ref, j, rows):
    quarter = lax.broadcasted_iota(jnp.int32, (4 * rows, 1), 0) // rows
    return jnp.where(quarter == 0, ref[4 * j], jnp.where(quarter == 1, ref[4 * j + 1],
                                                         jnp.where(quarter == 2, ref[4 * j + 2], ref[4 * j + 3])))


def _swa_fwd(q, kd, vd, sinks, slopes):
    S = q.shape[0]
    nkv = q.shape[1] // (2 * LANES)
    nb = S // WINDOW
    group = math.gcd(SWA_GROUP, nb)

    def body(sink_ref, slope_ref, q_ref, k_ref, v_ref, o_ref, lse_ref):
        j = pl.program_id(0)
        lo = _lane_masks()
        sink = _swa_per_head(sink_ref, j, WINDOW)
        biases = [jnp.concatenate([_swa_bias(slope_ref[4 * j + h], shift) for h in range(4)], axis=0)
                  for shift in (0, WINDOW)]

        def q_block(qi, c):
            q0 = pl.multiple_of(qi * WINDOW, WINDOW)
            k0 = pl.multiple_of(jnp.maximum(qi - 1, 0) * WINDOW, WINDOW)
            s = _swa_scores(_swa_stack(q_ref[pl.ds(q0, WINDOW), :], lo), k_ref[pl.ds(k0, 2 * WINDOW), :],
                            jnp.where(qi == 0, *biases))
            m = jnp.maximum(jnp.max(s, axis=1, keepdims=True), sink)
            p = jnp.exp(s - m)
            den = jnp.sum(p, axis=1, keepdims=True) + jnp.exp(sink - m)
            o_ref[pl.ds(q0, WINDOW), :] = _swa_unstack(_mm(p / den, v_ref[pl.ds(k0, 2 * WINDOW), :]), lo)
            lse = m + jnp.log(den)
            for h in range(4):
                lse_ref[h, pl.ds(q0, WINDOW), :] = lse[h * WINDOW:(h + 1) * WINDOW]
            return c

        def q_group(gi, c):
            for g in range(group):
                q_block(gi * group + g, c)
            return c

        lax.fori_loop(0, nb // group, q_group, 0)

    smem = pl.BlockSpec(memory_space=pltpu.SMEM)
    two = pl.BlockSpec((S, 2 * LANES), lambda j: (0, j))
    kv = pl.BlockSpec((S, LANES), lambda j: (0, 2 * j))
    return _pcall(
        body, name="swa_fwd", grid=(nkv,), semantics=("arbitrary",),
        in_specs=[smem, smem, two, kv, kv],
        out_specs=[two, pl.BlockSpec((4, S, 1), lambda j: (j, 0, 0))],
        out_shape=[_sds(q.shape, jnp.float32), _sds((4 * nkv, S, 1), jnp.float32)],
    )(sinks, slopes, q, kd, vd)


def _swa_bwd(q, kd, vd, do, o, lse, sinks, slopes, plan=None):
    S = q.shape[0]
    nkv = q.shape[1] // (2 * LANES)
    nb = S // WINDOW
    group = math.gcd(SWA_GROUP, nb)

    def body(sink_ref, slope_ref, q_ref, k_ref, v_ref, do_ref, o_ref, lse_ref,
             dq_ref, dk_ref, dv_ref, dsink_ref, dk_acc, dv_acc):
        j = pl.program_id(0)
        lo = _lane_masks()
        dk_acc[...] = jnp.zeros_like(dk_acc)
        dv_acc[...] = jnp.zeros_like(dv_acc)
        sink = _swa_per_head(sink_ref, j, WINDOW)
        biases = [jnp.concatenate([_swa_bias(slope_ref[4 * j + h], shift) for h in range(4)], axis=0)
                  for shift in (0, WINDOW)]

        def q_block(qi, carry):
            q0 = pl.multiple_of(qi * WINDOW, WINDOW)
            k0 = pl.multiple_of(jnp.maximum(qi - 1, 0) * WINDOW, WINDOW)
            q4 = _swa_stack(q_ref[pl.ds(q0, WINDOW), :], lo)
            do4 = _swa_stack(do_ref[pl.ds(q0, WINDOW), :], lo)
            oblk = o_ref[pl.ds(q0, WINDOW), :]
            o4 = jnp.concatenate([oblk[:, :LANES], oblk[:, :LANES], oblk[:, LANES:], oblk[:, LANES:]], axis=0)
            kblk = k_ref[pl.ds(k0, 2 * WINDOW), :]
            vblk = v_ref[pl.ds(k0, 2 * WINDOW), :]
            lse = jnp.concatenate([lse_ref[h, pl.ds(q0, WINDOW), :] for h in range(4)], axis=0)
            p = jnp.exp(_swa_scores(q4, kblk, jnp.where(qi == 0, *biases)) - lse)
            delta = jnp.sum(do4.astype(jnp.float32) * o4, axis=1, keepdims=True)
            dv_acc[pl.ds(k0, 2 * WINDOW), :] += _mm_tn(p, do4)
            ds = p * (_mm_nt(do4, vblk) - delta)
            dq_ref[pl.ds(q0, WINDOW), :] = _swa_unstack(_mm(ds, kblk) * (HEAD ** -0.5), lo).astype(dq_ref.dtype)
            dk_acc[pl.ds(k0, 2 * WINDOW), :] += _mm_tn(ds, q4) * (HEAD ** -0.5)
            dsk = -jnp.exp(sink - lse) * delta
            return tuple(carry[h] + jnp.sum(dsk[h * WINDOW:(h + 1) * WINDOW], axis=0, keepdims=True)
                         for h in range(4))

        def q_group(gi, carry):
            for g in range(group):
                carry = q_block(gi * group + g, carry)
            return carry

        zero = jnp.zeros((1, 1), jnp.float32)
        dsinks = lax.fori_loop(0, nb // group, q_group, (zero,) * 4)
        dk_ref[:, :LANES] = dk_acc[...].astype(dk_ref.dtype)
        dk_ref[:, LANES:] = jnp.zeros((S, LANES), dk_ref.dtype)
        dv_ref[:, :LANES] = dv_acc[...].astype(dv_ref.dtype)
        dv_ref[:, LANES:] = jnp.zeros((S, LANES), dv_ref.dtype)
        r = lax.broadcasted_iota(jnp.int32, (8, LANES), 0)
        dsink_ref[0] = jnp.where(r == 0, dsinks[0], jnp.where(r == 1, dsinks[1], jnp.where(r == 2, dsinks[2],
                                 jnp.where(r == 3, dsinks[3], 0.0))))

    smem = pl.BlockSpec(memory_space=pltpu.SMEM)
    two = pl.BlockSpec((S, 2 * LANES), lambda j: (0, j))
    kv = pl.BlockSpec((S, LANES), lambda j: (0, 2 * j))
    outs, rode = _pcall_riding(
        body, plan, [sinks, slopes, q, kd, vd, do, o, lse], name="swa_bwd", grid=(nkv,),
        in_specs=[smem, smem, two, kv, kv, two, two, pl.BlockSpec((4, S, 1), lambda j: (j, 0, 0))],
        out_specs=[two, two, two, pl.BlockSpec((1, 8, LANES), lambda j: (j, 0, 0))],
        out_shape=[_sds(q.shape, do.dtype), _sds(kd.shape, do.dtype), _sds(vd.shape, do.dtype),
                   _sds((nkv, 8, LANES), jnp.float32)],
        scratch_shapes=[pltpu.VMEM((S, LANES), jnp.float32), pltpu.VMEM((S, LANES), jnp.float32)])
    return (*outs, rode)


def _log_steps(S):
    k, out = 1, []
    while k < S:
        out.append(k)
        k *= 2
    return out


def _forget_fwd(f_row, b_col):
    S = f_row.shape[1]

    def body(f_ref, b_ref, lc_ref):
        x = f_ref[...] + b_ref[...]
        lc = jnp.minimum(x, 0.0) - jnp.log(1.0 + jnp.exp(-jnp.abs(x)))
        idx = lax.broadcasted_iota(jnp.int32, lc.shape, 1)
        for k in _log_steps(S):
            lc = lc + jnp.where(idx >= k, pltpu.roll(lc, k, axis=1), 0.0)
        lc_ref[...] = lc

    return _pcall(body, name="forget_fwd", out_shape=_sds(f_row.shape, jnp.float32))(f_row, b_col)


def _forget_bwd(dlc_row, f_row, b_col):
    S = f_row.shape[1]

    def body(d_ref, f_ref, b_ref, df_ref, db_ref):
        g = d_ref[...]
        idx = lax.broadcasted_iota(jnp.int32, g.shape, 1)
        for k in _log_steps(S):
            g = g + jnp.where(idx < S - k, pltpu.roll(g, S - k, axis=1), 0.0)
        x = f_ref[...] + b_ref[...]
        df = g * _sigmoid(-x)
        df_ref[...] = df
        db_ref[...] = jnp.sum(df, axis=1, keepdims=True)

    return _pcall(body, name="forget_bwd",
                  out_shape=[_sds(f_row.shape, jnp.float32), _sds((f_row.shape[0], 1), jnp.float32)])(dlc_row, f_row, b_col)


def _layer0_out_layer1_in(x, o_m, o_s, gate, w_out, g1, w_in1):
    S = x.shape[0]

    def body(x_ref, om_ref, os_ref, gate_ref, wo_ref, g_ref, w_ref,
             x1_ref, h_ref, q_ref, k_ref, v_ref, g1_ref, f_ref):
        gt = gate_ref[...]
        sg = gt * _sigmoid(gt)
        um = om_ref[...] * sg[:, :512]
        us = os_ref[...] * sg[:, 512:]
        x1 = x_ref[...] + _mm(um, wo_ref[0:512, :]) + _mm(us, wo_ref[512:1024, :])
        x1_ref[...] = x1
        h = _rms(x1, g_ref[...])
        h_ref[...] = h.astype(h_ref.dtype)
        z = _mm_nt(h, w_ref[...])
        q_ref[...] = z[:, 0:1024].astype(q_ref.dtype)
        k_ref[...] = z[:, 1024:2048].astype(k_ref.dtype)
        v_ref[...] = z[:, 2048:3072].astype(v_ref.dtype)
        g1_ref[...] = z[:, 3072:4096]
        f_ref[...] = z[:, 4096:4224]

    outs = [((S, D), jnp.float32), ((S, D), MXU), ((S, D), MXU), ((S, D), MXU), ((S, D), MXU),
            ((S, D), jnp.float32), ((S, LANES), jnp.float32)]
    return _pcall(
        body, name="layer0_out_layer1_in", grid=(S // TOK,), semantics=("arbitrary",),
        in_specs=[_rows(TOK, D), _rows(TOK, 512), _rows(TOK, 512), _rows(TOK, D), _full((D, D)), _full((1, D)),
                  _full(w_in1.shape)],
        out_specs=[_rows(TOK, s[1]) for s, _ in outs],
        out_shape=[_sds(s, d) for s, d in outs],
    )(x, o_m, o_s, gate, w_out, g1, w_in1)


def _head(x1, o1, gate1, w_out1, g_f, target):
    S = x1.shape[0]

    def body(x1_ref, o_ref, gate_ref, wo_ref, g_ref, t_ref,
             loss_ref, dgf_ref, dwo_ref, dx2_ref, do_ref, dgate_ref):
        i = pl.program_id(0)
        gt = gate_ref[...]
        sig = _sigmoid(gt)
        sg = gt * sig
        o = o_ref[...]
        u = o * sg
        x2 = x1_ref[...] + _mm(u, wo_ref[...])
        g = g_ref[...]
        y = _rms(x2, g)
        err = y - t_ref[...]
        part = 0.5 * jnp.sum(jnp.mean(err * err, axis=-1, keepdims=True), axis=0, keepdims=True)
        dy = err * (1.0 / D)
        dx2, dg_rows = _rms_bwd(x2, g, dy)
        dx2_ref[...] = dx2
        du = _mm_nt(dx2, wo_ref[...])
        do_ref[...] = (du * sg).astype(do_ref.dtype)
        dgate_ref[...] = (du * o * (sig * (1.0 + gt * (1.0 - sig)))).astype(dgate_ref.dtype)

        @pl.when(i == 0)
        def _():
            loss_ref[...] = jnp.zeros_like(loss_ref)
            dgf_ref[...] = jnp.zeros_like(dgf_ref)
            dwo_ref[...] = jnp.zeros_like(dwo_ref)

        loss_ref[...] += jnp.broadcast_to(part, loss_ref.shape)
        dgf_ref[...] += jnp.sum(dg_rows, axis=0, keepdims=True)
        dwo_ref[...] += _mm_tn(u, dx2)

    outs = [((S, D), jnp.float32), ((S, D), MXU), ((S, D), MXU)]
    return _pcall(
        body, name="head", grid=(S // TOK,), semantics=("arbitrary",),
        in_specs=[_rows(TOK, D), _rows(TOK, D), _rows(TOK, D), _full((D, D)), _full((1, D)), _rows(TOK, D)],
        out_specs=[_full((8, LANES)), _full((1, D)), _full((D, D))] + [_rows(TOK, D) for _ in outs],
        out_shape=[_sds((8, LANES), jnp.float32), _sds((1, D), jnp.float32), _sds((D, D), jnp.float32)]
        + [_sds(s, d) for s, d in outs],
    )(x1, o1, gate1, w_out1, g_f, target)


def _layer1_in_bwd(dq, dk, dv, dgate1, df, x1, dx2, g1, w_in1, gate0, o_m, o_s, w_out0):
    S = x1.shape[0]

    def body(dq_ref, dk_ref, dv_ref, dg1_ref, df_ref, x1_ref, dx2_ref, g_ref, w_ref, gate_ref, om_ref, os_ref,
             wo_ref, dz_ref, dx1_ref, dgn_ref, dwo_ref, dom_ref, dos_ref, dgate_ref):
        i = pl.program_id(0)
        dz_ref[:, 0:1024] = dq_ref[...]
        dz_ref[:, 1024:2048] = dk_ref[...]
        dz_ref[:, 2048:3072] = dv_ref[...]
        dz_ref[:, 3072:4096] = dg1_ref[...]
        dz_ref[:, 4096:4224] = df_ref[...]
        dh = _mm(dz_ref[...], w_ref[...])
        g = g_ref[...]
        dxn, dg_rows = _rms_bwd(x1_ref[...], g, dh)
        dx1 = dx2_ref[...] + dxn
        dx1_ref[...] = dx1
        du = _mm_nt(dx1, wo_ref[...])
        gt = gate_ref[...]
        sig = _sigmoid(gt)
        sg = gt * sig
        dsg = sig * (1.0 + gt * (1.0 - sig))
        dom_ref[...] = (du[:, :512] * sg[:, :512]).astype(dom_ref.dtype)
        dos_ref[...] = (du[:, 512:] * sg[:, 512:]).astype(dos_ref.dtype)
        dgate_ref[:, :512] = (du[:, :512] * om_ref[...] * dsg[:, :512]).astype(dgate_ref.dtype)
        dgate_ref[:, 512:] = (du[:, 512:] * os_ref[...] * dsg[:, 512:]).astype(dgate_ref.dtype)

        @pl.when(i == 0)
        def _():
            dgn_ref[...] = jnp.zeros_like(dgn_ref)
            dwo_ref[...] = jnp.zeros_like(dwo_ref)

        dgn_ref[...] += jnp.sum(dg_rows, axis=0, keepdims=True)
        dwo_ref[0:512, :] += _mm_tn(om_ref[...] * sg[:, :512], dx1)
        dwo_ref[512:1024, :] += _mm_tn(os_ref[...] * sg[:, 512:], dx1)

    return _pcall(
        body, name="layer1_in_bwd", grid=(S // TOK,), semantics=("arbitrary",),
        in_specs=[_rows(TOK, D), _rows(TOK, D), _rows(TOK, D), _rows(TOK, D), _rows(TOK, LANES), _rows(TOK, D),
                  _rows(TOK, D), _full((1, D)), _full(w_in1.shape), _rows(TOK, D), _rows(TOK, 512), _rows(TOK, 512),
                  _full((D, D))],
        out_specs=[_rows(TOK, 4224), _rows(TOK, D), _full((1, D)), _full((D, D)), _rows(TOK, 512), _rows(TOK, 512),
                   _rows(TOK, D)],
        out_shape=[_sds((S, 4224), MXU), _sds((S, D), jnp.float32), _sds((1, D), jnp.float32), _sds((D, D), jnp.float32),
                   _sds((S, 512), MXU), _sds((S, 512), MXU), _sds((S, D), MXU)],
    )(dq, dk, dv, dgate1, df, x1, dx2, g1, w_in1, gate0, o_m, o_s, w_out0)


def _layer0_in_bwd(dqm, dkm, dvm, dqs, dkd, dvd, dgate0, cos, sin, cq, ckv, x, dx1, g_in, w_in, g_q, w_q, g_kv, w_kv):
    S = x.shape[0]
    consts = _rope_consts()

    def body(dqm_ref, dkm_ref, dvm_ref, dqs_ref, dkd_ref, dvd_ref, dgate_ref, cos_ref, sin_ref, c_ref, cq_ref, ckv_ref,
             x_ref, dx1_ref, g_ref, w_ref, gq_ref, wq_ref, gkv_ref, wkv_ref,
             dx_ref, dz_ref, dgin_ref, dgq_ref, dgkv_ref, dwq_ref, dwkv_ref, dqu_ref, dkvu_ref):
        i = pl.program_id(0)
        lo = _lane_masks()
        sign = c_ref[...][1:2, :]
        c = cos_ref[...]
        s = sin_ref[...]
        dkpe = None
        for hd in range(N_MLA):
            sl = slice(LANES * hd, LANES * (hd + 1))
            dqu_ref[:, sl] = _rope_t(dqm_ref[:, sl], c, s, sign).astype(dqu_ref.dtype)
            dkh = dkm_ref[:, sl]
            dkvu_ref[:, sl] = jnp.where(lo, dkh, 0.0).astype(dkvu_ref.dtype)
            dkpe = dkh if dkpe is None else dkpe + dkh
        dkvu_ref[:, 1024:1536] = dvm_ref[...]
        dkpe = _rope_t(jnp.where(lo, 0.0, dkpe), c, s, sign)
        dcqn = _mm(dqu_ref[...], wq_ref[...])
        dckvn = _mm_nt(dkvu_ref[...], wkv_ref[...])
        gq = gq_ref[...]
        gkv = gkv_ref[...]
        dcq, dgq_rows = _rms_bwd(cq_ref[...], gq, dcqn)
        dckv, dgkv_rows = _rms_bwd(ckv_ref[...], gkv, dckvn)
        dz_ref[:, 0:256] = dcq.astype(dz_ref.dtype)
        dz_ref[:, 256:384] = dckv.astype(dz_ref.dtype)
        dz_ref[:, 384:512] = dkpe.astype(dz_ref.dtype)
        dz_ref[:, 512:1024] = dqs_ref[...]
        dz_ref[:, 1024:1536] = dkd_ref[...]
        dz_ref[:, 1536:2048] = dvd_ref[...]
        dz_ref[:, 2048:3072] = dgate_ref[...]
        dh = _mm(dz_ref[...], w_ref[...])
        g = g_ref[...]
        dxn, dg_rows = _rms_bwd(x_ref[...], g, dh)
        dx_ref[...] = dx1_ref[...] + dxn

        @pl.when(i == 0)
        def _():
            dgin_ref[...] = jnp.zeros_like(dgin_ref)
            dgq_ref[...] = jnp.zeros_like(dgq_ref)
            dgkv_ref[...] = jnp.zeros_like(dgkv_ref)
            dwq_ref[...] = jnp.zeros_like(dwq_ref)
            dwkv_ref[...] = jnp.zeros_like(dwkv_ref)

        dgin_ref[...] += jnp.sum(dg_rows, axis=0, keepdims=True)
        dgq_ref[...] += jnp.sum(dgq_rows, axis=0, keepdims=True)
        dgkv_ref[...] += jnp.sum(dgkv_rows, axis=0, keepdims=True)
        dwq_ref[...] += _mm_tn(dqu_ref[...], _rms(cq_ref[...], gq))
        dwkv_ref[...] += _mm_tn(_rms(ckv_ref[...], gkv), dkvu_ref[...])

    return _pcall(
        body, name="layer0_in_bwd", grid=(S // TOK,), semantics=("arbitrary",),
        in_specs=[_rows(TOK, 1024), _rows(TOK, 1024), _rows(TOK, 512), _rows(TOK, 512), _rows(TOK, 512), _rows(TOK, 512),
                  _rows(TOK, D), _rows(TOK, LANES), _rows(TOK, LANES), _full((8, LANES)), _rows(TOK, 256), _rows(TOK, 128),
                  _rows(TOK, D), _rows(TOK, D), _full((1, D)), _full(w_in.shape), _full((1, 256)), _full(w_q.shape),
                  _full((1, 128)), _full(w_kv.shape)],
        out_specs=[_rows(TOK, D), _rows(TOK, 3072), _full((1, D)), _full((1, 256)), _full((1, 128)), _full(w_q.shape),
                   _full(w_kv.shape)],
        out_shape=[_sds((S, D), jnp.float32), _sds((S, 3072), MXU), _sds((1, D), jnp.float32), _sds((1, 256), jnp.float32),
                   _sds((1, 128), jnp.float32), _sds(w_q.shape, jnp.float32), _sds(w_kv.shape, jnp.float32)],
        scratch_shapes=[pltpu.VMEM((TOK, 1024), MXU), pltpu.VMEM((TOK, 1536), MXU)],
    )(dqm, dkm, dvm, dqs, dkd, dvd, dgate0, cos, sin, consts, cq, ckv, x, dx1, g_in, w_in, g_q, w_q, g_kv, w_kv)


def _wgrad(a, b, name):
    S, M = a.shape
    N = b.shape[1]
    tm = next(t for t in range(WG_ROWS, 0, -LANES) if M % t == 0)
    tn = N if N <= 1024 else 512
    tk = min(WG_TOK, S)

    def body(a_ref, b_ref, o_ref):
        @pl.when(pl.program_id(2) == 0)
        def _():
            o_ref[...] = jnp.zeros_like(o_ref)

        o_ref[...] += _mm_tn(a_ref[...], b_ref[...])

    return _pcall(
        body, name=name, grid=(M // tm, N // tn, S // tk), semantics=("parallel", "parallel", "arbitrary"),
        in_specs=[pl.BlockSpec((tk, tm), lambda m, n, k: (k, m)), pl.BlockSpec((tk, tn), lambda m, n, k: (k, n))],
        out_specs=pl.BlockSpec((tm, tn), lambda m, n, k: (m, n)),
        out_shape=_sds((M, N), jnp.float32),
    )(a, b)


def _adamw(w, g, m, v, name):
    shape = w.shape
    R, C = (int(np.prod(shape[:-1])), shape[-1])
    w2, g2, m2, v2 = (t.reshape(R, C) for t in (w, g, m, v))
    fits = [t for t in range(8, ADAM_TILE_BYTES // (4 * C) + 1, 8) if R % t == 0]
    tr = max(fits) if fits else R
    tc = C if (tr * C * 4 <= ADAM_TILE_BYTES or C % 256) else 256

    def body(w_ref, g_ref, m_ref, v_ref, d_ref, nm_ref, nv_ref):
        gg = g_ref[...]
        nm = B1 * m_ref[...] + (1.0 - B1) * gg
        nv = B2 * v_ref[...] + (1.0 - B2) * (gg * gg)
        m_hat = nm / (1.0 - B1 ** STEP)
        v_hat = nv / (1.0 - B2 ** STEP)
        d_ref[...] = -LR * (m_hat / (jnp.sqrt(v_hat) + AEPS) + WD * w_ref[...])
        nm_ref[...] = nm
        nv_ref[...] = nv

    spec = pl.BlockSpec((tr, tc), lambda i, j: (i, j))
    d, nm, nv = _pcall(
        body, name=name, grid=(R // tr, C // tc), semantics=("parallel", "parallel"),
        in_specs=[spec] * 4, out_specs=[spec] * 3, out_shape=[_sds((R, C), jnp.float32)] * 3,
    )(w2, g2, m2, v2)
    return d.reshape(shape), nm.reshape(shape), nv.reshape(shape)


def _sum_leading(a, name):
    n, R, C = a.shape
    tr = SUM_ROWS if R % SUM_ROWS == 0 else R

    def body(a_ref, o_ref):
        acc = a_ref[0]
        for i in range(1, n):
            acc = acc + a_ref[i]
        o_ref[...] = acc

    return _pcall(
        body, name=name, grid=(R // tr,), semantics=("parallel",),
        in_specs=[pl.BlockSpec((n, tr, C), lambda i: (0, i, 0))], out_specs=_rows(tr, C),
        out_shape=_sds((R, C), a.dtype),
    )(a)


def _add_halves(g, c, b, name, out_dtype):
    n, _, R, C = g.shape
    tr = SUM_ROWS if R % SUM_ROWS == 0 else R

    def body(c_ref, a_ref, b_ref, o_ref):
        o_ref[...] = (a_ref[0] + b_ref[...]).astype(o_ref.dtype)

    spec = pl.BlockSpec((1, tr, C), lambda k, i, c_ref: (k, i, 0))
    grid_spec = pltpu.PrefetchScalarGridSpec(
        num_scalar_prefetch=1, grid=(n, R // tr),
        in_specs=[pl.BlockSpec((1, 1, tr, C), lambda k, i, c_ref: (k, c_ref[0], i, 0)), spec], out_specs=spec)
    return _pcall(body, name=name, semantics=("parallel", "parallel"), grid_spec=grid_spec,
                  out_shape=_sds(b.shape, out_dtype))(c.reshape(1).astype(jnp.int32), g, b)


def _total_sum(mine, theirs, recv, name):
    R, C = mine.shape
    n = recv.shape[0]
    tr = SUM_ROWS if R % SUM_ROWS == 0 else R

    def body(a_ref, b_ref, r_ref, o_ref):
        acc = a_ref[...] + b_ref[...]
        for i in range(n):
            acc = acc + r_ref[i].astype(jnp.float32)
        o_ref[...] = acc

    return _pcall(
        body, name=name, grid=(R // tr,), semantics=("parallel",),
        in_specs=[_rows(tr, C), _rows(tr, C), pl.BlockSpec((n, tr, C), lambda i: (0, i, 0))], out_specs=_rows(tr, C),
        out_shape=_sds((R, C), jnp.float32),
    )(mine, theirs, recv)


def _place():
    return lax.axis_index("x"), lax.axis_index("y"), lax.axis_index("c")


class _Plan:
    def __init__(self, arrays, out_shape, scratch, start, finish, middle=None):
        self.arrays, self.out_shape, self.scratch = list(arrays), list(out_shape), list(scratch)
        self.start, self.finish, self.middle = start, finish, middle


def _gather8_plan(block):
    R, C = block.shape

    def parts(ins, outs, sems):
        (x_ref,), (out_ref,), (send_sems, recv_sems) = ins, outs, sems
        x, y, c = _place()
        me, sibling = (x, y, c), (x, y, 1 - c)
        chips = [(1 - x, y), (x, 1 - y), (1 - x, 1 - y)]

        def copy(k, blk, to, src=None):
            slot = out_ref.at[4 * blk[0] + 2 * blk[1] + blk[2]]
            return pltpu.make_async_remote_copy(
                src_ref=slot if src is None else src, dst_ref=slot,
                send_sem=send_sems.at[k], recv_sem=recv_sems.at[k], device_id=to, device_id_type=MESH_ID)

        def first():
            return [copy(0, me, sibling, src=x_ref)] + [copy(1 + j, me, (*chip, c), src=x_ref) for j, chip in enumerate(chips)]

        def passed():
            return [copy(4 + j, (*chip, c), sibling) for j, chip in enumerate(chips)]

        def arrivals():
            return [copy(1 + j, (*chip, c), me) for j, chip in enumerate(chips)]

        def late():
            return [copy(0, sibling, me)] + [copy(4 + j, (*chip, 1 - c), me) for j, chip in enumerate(chips)]

        return first, passed, arrivals, late

    def start(ins, outs, sems):
        for cp in parts(ins, outs, sems)[0]():
            cp.start()

    def middle(ins, outs, sems):
        _, passed, arrivals, _ = parts(ins, outs, sems)
        for arrived, forward in zip(arrivals(), passed()):
            arrived.wait_recv()
            forward.start()

    def finish(ins, outs, sems):
        first, passed, _, late = parts(ins, outs, sems)
        for cp in late():
            cp.wait_recv()
        for cp in first() + passed():
            cp.wait_send()

    return _Plan([block], [_sds((8, R, C), block.dtype)], [pltpu.SemaphoreType.DMA((7,)), pltpu.SemaphoreType.DMA((7,))],
                 start, finish, middle)


def _fill_own_slot(gathered, block):
    x, y, c = _place()
    return lax.dynamic_update_index_in_dim(gathered, block, 4 * x + 2 * y + c, 0)


def _started_and_waited(arrays, out_shape, n, copies):
    def start(ins, outs, sems):
        for cp in copies(ins, outs, sems):
            cp.start()

    def finish(ins, outs, sems):
        for cp in copies(ins, outs, sems):
            cp.wait()

    return _Plan(arrays, out_shape, [pltpu.SemaphoreType.DMA((n,)), pltpu.SemaphoreType.DMA((n,))], start, finish)


def _pair_swap_plan(g):
    n = g.shape[0]

    def copies(ins, outs, sems):
        (g_ref,), (out_ref,), (send_sems, recv_sems) = ins, outs, sems
        x, y, c = _place()
        return [pltpu.make_async_remote_copy(src_ref=g_ref.at[k, 1 - c], dst_ref=out_ref.at[k], send_sem=send_sems.at[k],
                                             recv_sem=recv_sems.at[k], device_id=(x, y, 1 - c), device_id_type=MESH_ID)
                for k in range(n)]

    return _started_and_waited([g], [_sds((n,) + g.shape[2:], g.dtype)], n, copies)


def _chip_exchange_plan(p):
    def copies(ins, outs, sems):
        (p_ref,), (out_ref,), (send_sems, recv_sems) = ins, outs, sems
        x, y, c = _place()
        chips = [(1 - x, y), (x, 1 - y), (1 - x, 1 - y)]
        return [pltpu.make_async_remote_copy(
            src_ref=p_ref.at[2 * cx + cy], dst_ref=out_ref.at[j], send_sem=send_sems.at[j],
            recv_sem=recv_sems.at[j], device_id=(cx, cy, c), device_id_type=MESH_ID)
            for j, (cx, cy) in enumerate(chips)]

    return _started_and_waited([p], [_sds((3,) + p.shape[1:], p.dtype)], 3, copies)


def _pair_exchange_plan(t):
    def copies(ins, outs, sems):
        (t_ref,), (out_ref,), (send_sems, recv_sems) = ins, outs, sems
        x, y, c = _place()
        return [pltpu.make_async_remote_copy(src_ref=t_ref, dst_ref=out_ref, send_sem=send_sems.at[0], recv_sem=recv_sems.at[0],
                                             device_id=(x, y, 1 - c), device_id_type=MESH_ID)]

    return _started_and_waited([t], [_sds(t.shape, t.dtype)], 1, copies)


def _both_plans(a, b):
    na, ma, sa = len(a.arrays), len(a.out_shape), len(a.scratch)

    def phase(name):
        fa, fb = getattr(a, name), getattr(b, name)
        if fa is None and fb is None:
            return None

        def run(ins, outs, sems):
            if fa is not None:
                fa(ins[:na], outs[:ma], sems[:sa])
            if fb is not None:
                fb(ins[na:], outs[ma:], sems[sa:])
        return run

    return _Plan(a.arrays + b.arrays, a.out_shape + b.out_shape, a.scratch + b.scratch,
                 phase("start"), phase("finish"), phase("middle"))


ANY_SPEC = pl.BlockSpec(memory_space=pl.ANY)


def _run_plan(plan, name):
    n_in, n_out = len(plan.arrays), len(plan.out_shape)

    def body(*refs):
        ins, outs, sems = refs[:n_in], refs[n_in:n_in + n_out], refs[n_in + n_out:]
        plan.start(ins, outs, sems)
        if plan.middle is not None:
            plan.middle(ins, outs, sems)
        plan.finish(ins, outs, sems)

    return _pcall(body, name=name, in_specs=[ANY_SPEC] * n_in, out_specs=[ANY_SPEC] * n_out, out_shape=plan.out_shape,
                  scratch_shapes=plan.scratch)(*plan.arrays)


def _pcall_riding(body, plan, args, *, name, grid, in_specs, out_specs, out_shape, scratch_shapes):
    if plan is None:
        outs = _pcall(body, name=name, grid=grid, semantics=("arbitrary",), in_specs=in_specs, out_specs=out_specs,
                      out_shape=out_shape, scratch_shapes=scratch_shapes)(*args)
        return list(outs), None
    n_in, n_out, n_s = len(args), len(out_shape), len(scratch_shapes)
    p_in, p_out = len(plan.arrays), len(plan.out_shape)
    steps = grid[0]

    def riding(*refs):
        ins, pins = refs[:n_in], refs[n_in:n_in + p_in]
        o0 = n_in + p_in
        outs, pouts = refs[o0:o0 + n_out], refs[o0 + n_out:o0 + n_out + p_out]
        s0 = o0 + n_out + p_out
        scr, sems = refs[s0:s0 + n_s], refs[s0 + n_s:]
        j = pl.program_id(0)

        @pl.when(j == 0)
        def _():
            plan.start(pins, pouts, sems)

        if plan.middle is not None:
            @pl.when(j == steps // 2)
            def _():
                plan.middle(pins, pouts, sems)

        body(*ins, *outs, *scr)

        @pl.when(j == steps - 1)
        def _():
            plan.finish(pins, pouts, sems)

    res = _pcall(riding, name=name, grid=grid, semantics=("arbitrary",), in_specs=list(in_specs) + [ANY_SPEC] * p_in,
                 out_specs=list(out_specs) + [ANY_SPEC] * p_out, out_shape=list(out_shape) + plan.out_shape,
                 scratch_shapes=list(scratch_shapes) + plan.scratch)(*args, *plan.arrays)
    return list(res[:n_out]), list(res[n_out:])


class _RowSeq:
    def __init__(self, pieces):
        self.pieces = list(pieces)

    def rows(self, a, b):
        out, off = [], 0
        for p in self.pieces:
            lo, hi = max(a, off), min(b, off + p.shape[0])
            if lo < hi:
                out.append(p[lo - off:hi - off])
            off += p.shape[0]
        return out

    def array(self):
        return jnp.concatenate(self.pieces, axis=0)


def _row_seq(w):
    return w if isinstance(w, _RowSeq) else _RowSeq([w])


def _prep_w_in0(wt):
    wt = _row_seq(wt)
    one = wt.pieces[0]
    z32 = [jnp.zeros((32, one.shape[1]), one.dtype)]
    k0, k1 = wt.rows(928, 992), wt.rows(992, 1056)
    v0, v1 = wt.rows(1056, 1120), wt.rows(1120, 1184)
    return jnp.concatenate(wt.rows(0, 384) + z32 + z32 + wt.rows(384, 416) + z32 + wt.rows(416, 928)
                           + k0 * 4 + k1 * 4 + v0 * 4 + v1 * 4 + wt.rows(1184, 2208), axis=0)


def _fold_w_in0(d):
    def fold(blk):
        b = blk.reshape(8, 64, blk.shape[1])
        return jnp.concatenate([b[0] + b[1] + b[2] + b[3], b[4] + b[5] + b[6] + b[7]], axis=0)
    return _RowSeq([d[0:384], d[448:480], d[512:1024], fold(d[1024:1536]), fold(d[1536:2048]), d[2048:3072]])


def _prep_w_q(wt):
    return jnp.pad(wt.reshape(N_MLA, 96, Q_RANK), ((0, 0), (0, 32), (0, 0))).reshape(1024, Q_RANK)


def _fold_w_q(d):
    return d.reshape(N_MLA, 128, Q_RANK)[:, :96].reshape(768, Q_RANK)


def _prep_w_kv(w):
    w3 = w.reshape(KV_RANK, N_MLA, 128)
    kk = jnp.pad(w3[:, :, :64], ((0, 0), (0, 0), (0, 64))).reshape(KV_RANK, 1024)
    return jnp.concatenate([kk, w3[:, :, 64:].reshape(KV_RANK, 512)], axis=1)


def _fold_w_kv(d):
    kk = d[:, :1024].reshape(KV_RANK, N_MLA, 128)[:, :, :64]
    vv = d[:, 1024:].reshape(KV_RANK, N_MLA, 64)
    return jnp.concatenate([kk, vv], axis=2).reshape(KV_RANK, 1024)


W_IN1_SHARD = 1028
W_IN1_STEP = W_IN1_SHARD % 16


class _ShiftedShards:
    def __init__(self, blocks):
        self.blocks = list(blocks)


def _shifted_shard(a, chip, rows):
    out = jnp.zeros((rows, a.shape[1]), a.dtype)
    for k in range(4):
        out = jnp.where(chip == k, jnp.pad(a, ((W_IN1_STEP * k, rows - W_IN1_STEP * k - a.shape[0]), (0, 0))), out)
    return out


def _prep_w_in1(wt):
    if not isinstance(wt, _ShiftedShards):
        return jnp.concatenate([wt[0:3072], wt[3088:4112], wt[3072:3088], jnp.zeros((112, wt.shape[1]), wt.dtype)], axis=0)
    b = wt.blocks
    row = lax.broadcasted_iota(jnp.int32, (16, 1), 0)

    def seam(k, first, second):
        return jnp.where(row < W_IN1_STEP * (k + 1), first, second)

    return jnp.concatenate([
        b[0][0:1024], seam(0, b[0][1024:1040], b[1][0:16]), b[1][16:1024], seam(1, b[1][1024:1040], b[2][0:16]),
        b[2][16:1024], b[3][16:1040], seam(2, b[2][1024:1040], b[3][0:16]), jnp.zeros((112, 1024), b[0].dtype)], axis=0)


def _fold_w_in1(d):
    return _RowSeq([d[0:3072], d[4096:4112], d[3072:4096]])


class _Alone:
    def __init__(self, w_out0, o_g_in, w_in1, w_out1):
        self.layer1 = (w_out0, o_g_in, w_in1, w_out1)

    def gather_plan(self):
        return None

    def layer1_weights(self, rode):
        return self.layer1

    def swap_plan(self, grads1):
        return None

    def exchange_plan(self, rode):
        return None

    def finish(self, rode):
        pass


def _local_step(x, pos, target, e_g_in, w_in0, e_g_q, w_q, e_g_kv, w_kv, sinks, b_f, g_final, layer1):
    S = x.shape[0]
    w_in0p, w_qp, w_kvp = _prep_w_in0(w_in0), _prep_w_q(w_q), _prep_w_kv(w_kv)
    slopes = jnp.asarray(2.0 ** (-8.0 * (np.arange(N_SWA, dtype=np.float32) + 1.0) / N_SWA), jnp.float32)
    sinks1 = sinks.reshape(N_SWA)
    b_col = b_f.reshape(N_FOX, 1)

    (h0, cq, ckv, qm, km, vm, qs, kd, vd, gate0, cos, sin) = _layer0_in(
        x, pos, e_g_in, w_in0p, e_g_q, w_qp, e_g_kv, w_kvp)
    o_m, lse_m, rode = _attn_fwd_t(qm, km, vm, (NOPE + ROPE) ** -0.5, split=True, name="mla_fwd", plan=layer1.gather_plan())
    w_out0, o_g_in, w_in1, w_out1 = layer1.layer1_weights(rode)
    w_in1p = _prep_w_in1(w_in1)
    o_s, lse_s = _swa_fwd(qs, kd, vd, sinks1, slopes)
    x1, h1, q1, k1, v1, gate1, f_slab = _layer0_out_layer1_in(x, o_m, o_s, gate0, w_out0, o_g_in, w_in1p)
    f_row = f_slab[:, :N_FOX].T
    lc_row = _forget_fwd(f_row, b_col)
    lcc = lc_row.T
    o1, lse1, _ = _attn_fwd_t(q1, k1, v1, HEAD ** -0.5, split=False, name="fox_fwd", lcc=lcc)
    loss8, dg_final, dw_out1, dx2, do1, dgate1 = _head(x1, o1, gate1, w_out1, g_final, target)

    dq1, dk1, dv1, dlc, _ = _attn_bwd_t(q1, k1, v1, do1, o1, lse1, HEAD ** -0.5, split=False, name="fox_bwd", lcc=lcc)
    df_row, db_f = _forget_bwd(dlc.reshape(N_FOX, S), f_row, b_col)
    df_slab = jnp.pad(df_row.T, ((0, 0), (0, LANES - N_FOX))).astype(MXU)
    dz1, dx1, dg_o_in, dw_out0, do_m, do_s, dgate0 = _layer1_in_bwd(
        dq1, dk1, dv1, dgate1, df_slab, x1, dx2, o_g_in, w_in1p, gate0, o_m, o_s, w_out0)
    grads1 = dict(o_g_in=dg_o_in, o_w_in=_fold_w_in1(_wgrad(dz1, h1, "wgrad_in1")), o_w_out=dw_out1, e_w_out=dw_out0)
    dqs, dkd, dvd, dsink, rode = _swa_bwd(qs, kd, vd, do_s, o_s, lse_s, sinks1, slopes, plan=layer1.swap_plan(grads1))
    dqm, dkm, dvm, rode = _attn_bwd_t(qm, km, vm, do_m, o_m, lse_m, (NOPE + ROPE) ** -0.5, split=True, name="mla_bwd",
                                      plan=layer1.exchange_plan(rode))
    layer1.finish(rode)
    dx, dz0, dg_in, dg_q, dg_kv, dw_q, dw_kv = _layer0_in_bwd(
        dqm, dkm, dvm, dqs, dkd, dvd, dgate0, cos, sin, cq, ckv, x, dx1, e_g_in, w_in0p, e_g_q, w_qp, e_g_kv, w_kvp)

    grads = dict(
        e_g_in=dg_in,
        e_w_in=_fold_w_in0(_wgrad(dz0, h0, "wgrad_in0")),
        e_g_q_a=dg_q,
        e_w_q_up=_fold_w_q(dw_q),
        e_g_kv_a=dg_kv,
        e_w_kv_up=_fold_w_kv(dw_kv),
        e_sinks=dsink[:, 0:4, 0].reshape(1, N_SWA),
        o_b_f=db_f.reshape(1, N_FOX),
        g_final=dg_final,
        **grads1,
    )
    return loss8[0, 0], dx, grads


SHARDED = ("e_w_in", "e_w_q_up", "e_w_kv_up", "e_w_out", "o_g_in", "o_w_in", "o_w_out")
TRANSPOSED = ("e_w_in", "e_w_q_up", "o_w_in")
COL_SHARDED = ("e_w_kv_up", "o_g_in")
REPLICATED = ("e_g_in", "e_g_q_a", "e_g_kv_a", "e_sinks", "o_b_f", "g_final")
FULL_SHAPES = dict(e_w_in=(2208, 1024), e_w_q_up=(768, 256), e_w_kv_up=(128, 1024), e_w_out=(1024, 1024),
                   o_g_in=(1, 1024), o_w_in=(4112, 1024), o_w_out=(1024, 1024))
GROUPS = dict(
    layer0=dict(rows=768, windows=dict(e_w_in=(0, 0), e_w_q_up=(560, 0), e_w_kv_up=(560, 256))),
    layer1=dict(rows=1568, windows=dict(o_w_in=(0, 0), o_w_out=(1040, 0), e_w_out=(1296, 0), o_g_in=(1552, 0))),
)


def _shard_shape(name):
    r, c = FULL_SHAPES[name]
    return (r, c // 4) if name in COL_SHARDED else (r // 4, c)


def _as_handled(name, a):
    a = a[0] if a.ndim == 3 else a
    return a.T if name in TRANSPOSED else a


def _as_given(name, a, shape):
    return (a.T if name in TRANSPOSED else a).reshape(shape)


def _pack_block(p, group, shifted_for=None):
    def rows(a, n):
        return jnp.pad(a, ((0, n - a.shape[0]), (0, 0)))

    if group == "layer0":
        band = jnp.concatenate([p["e_w_q_up"], rows(p["e_w_kv_up"], 192), jnp.zeros((192, 512), p["e_w_in"].dtype)], axis=1)
        return jnp.concatenate([rows(p["e_w_in"], 560), rows(band, 208)], axis=0)
    g = p["o_g_in"]
    band = jnp.pad(g, ((0, 16 - g.shape[0]), (0, PACK_COLS - g.shape[1])))
    w_in = rows(p["o_w_in"], 1040) if shifted_for is None else _shifted_shard(p["o_w_in"], shifted_for, 1040)
    return jnp.concatenate([w_in, p["o_w_out"], p["e_w_out"], band], axis=0)


def _window(block, group, name, width=None):
    r0, c0 = GROUPS[group]["windows"][name]
    r, c = _shard_shape(name)
    return block[..., r0:r0 + r, c0:c0 + (c if width is None else width)]


def _chip_slice(name, full, k):
    r, c = _shard_shape(name)
    if isinstance(full, _RowSeq):
        return jnp.concatenate(full.rows(r * k, r * (k + 1)), axis=0)
    return full[:, c * k:c * (k + 1)] if name in COL_SHARDED else full[r * k:r * (k + 1), :]


def _packed_weights(w, group):
    parts = {}
    for n in GROUPS[group]["windows"]:
        a = _as_handled(n, w[n])
        parts[n] = lax.bitcast_convert_type(a, jnp.bfloat16).reshape(1, -1) if n == "o_g_in" else a.astype(jnp.bfloat16)
    x, y, _ = _place()
    halves = _pack_block(parts, group, shifted_for=2 * x + y).reshape(2, GROUPS[group]["rows"] // 2, PACK_COLS)
    return lax.dynamic_index_in_dim(halves, lax.axis_index("c"), 0, keepdims=False)


def _unpacked_weights(gathered, half, group):
    blocks = _fill_own_slot(gathered, half).reshape(4, GROUPS[group]["rows"], PACK_COLS)
    full = {}
    for n in GROUPS[group]["windows"]:
        if n == "o_g_in":
            halves = _window(blocks, group, n, width=512).reshape(4, 1, 256, 2)
            full[n] = jnp.concatenate(list(lax.bitcast_convert_type(halves, jnp.float32)), axis=1)
        elif n == "o_w_in":
            full[n] = _ShiftedShards(blocks[k, 0:1040].astype(MXU) for k in range(4))
        else:
            pieces = [_window(blocks[k], group, n).astype(MXU) for k in range(4)]
            if n == "e_w_in":
                full[n] = _RowSeq(pieces)
            else:
                full[n] = jnp.concatenate(pieces, axis=1 if n in COL_SHARDED else 0)
    return full


class _GroupReduce:
    def __init__(self, group):
        self.group = group
        self.c = lax.axis_index("c")
        self.chip = 2 * lax.axis_index("x") + lax.axis_index("y")

    def swap_plan(self, grads):
        names = GROUPS[self.group]["windows"]
        per_chip = jnp.stack([_pack_block({n: _chip_slice(n, grads[n], k) for n in names}, self.group) for k in range(4)])
        self.g4 = per_chip.reshape(4, 2, GROUPS[self.group]["rows"] // 2, PACK_COLS)
        return _pair_swap_plan(self.g4)

    def exchange_plan(self, rode):
        theirs = rode[0]
        rows = self.g4.shape[2]
        self.own = (lax.dynamic_slice(self.g4, (self.chip, self.c, 0, 0), (1, 1, rows, PACK_COLS)).reshape(rows, PACK_COLS),
                    lax.dynamic_index_in_dim(theirs, self.chip, 0, keepdims=False))
        return _chip_exchange_plan(_add_halves(self.g4, self.c, theirs, "pair_add_" + self.group, jnp.bfloat16))

    def finish(self, rode):
        my_half = _total_sum(*self.own, rode[0], "chip_sum_" + self.group)
        other_half = _run_plan(_pair_exchange_plan(my_half), "pair_exchange_" + self.group)[0]
        total = jnp.concatenate([jnp.where(self.c == 0, my_half, other_half), jnp.where(self.c == 0, other_half, my_half)], axis=0)
        self.sums = {n: _window(total, self.group, n) for n in GROUPS[self.group]["windows"]}

    def run(self, grads, beside):
        swap = self.swap_plan(grads)
        outs = _run_plan(_both_plans(swap, beside), "pair_swap_" + self.group)
        rode, others = outs[:len(swap.out_shape)], outs[len(swap.out_shape):]
        self.finish(_run_plan(self.exchange_plan(rode), "chip_exchange_" + self.group))
        return self.sums, others


class _Layer1Exchange(_GroupReduce):
    def __init__(self, w):
        super().__init__("layer1")
        self.half = _packed_weights(w, "layer1")

    def gather_plan(self):
        return _gather8_plan(self.half)

    def layer1_weights(self, rode):
        full = _unpacked_weights(rode[0], self.half, "layer1")
        return full["e_w_out"], full["o_g_in"], full["o_w_in"], full["o_w_out"]


def kernel(x, positions, e_g_in, e_w_in, e_g_q_a, e_w_q_up, e_g_kv_a, e_w_kv_up, e_sinks, e_w_out, o_g_in, o_w_in, o_b_f, o_w_out, g_final, loss_target, m_e_g_in, m_e_w_in, m_e_g_q_a, m_e_w_q_up, m_e_g_kv_a, m_e_w_kv_up, m_e_sinks, m_e_w_out, m_o_g_in, m_o_w_in, m_o_b_f, m_o_w_out, m_g_final, v_e_g_in, v_e_w_in, v_e_g_q_a, v_e_w_q_up, v_e_g_kv_a, v_e_w_kv_up, v_e_sinks, v_e_w_out, v_o_g_in, v_o_w_in, v_o_b_f, v_o_w_out, v_g_final):
    w = dict(e_g_in=e_g_in, e_w_in=e_w_in, e_g_q_a=e_g_q_a, e_w_q_up=e_w_q_up, e_g_kv_a=e_g_kv_a, e_w_kv_up=e_w_kv_up,
             e_sinks=e_sinks, e_w_out=e_w_out, o_g_in=o_g_in, o_w_in=o_w_in, o_b_f=o_b_f, o_w_out=o_w_out, g_final=g_final)
    m = dict(e_g_in=m_e_g_in, e_w_in=m_e_w_in, e_g_q_a=m_e_g_q_a, e_w_q_up=m_e_w_q_up, e_g_kv_a=m_e_g_kv_a,
             e_w_kv_up=m_e_w_kv_up, e_sinks=m_e_sinks, e_w_out=m_e_w_out, o_g_in=m_o_g_in, o_w_in=m_o_w_in, o_b_f=m_o_b_f,
             o_w_out=m_o_w_out, g_final=m_g_final)
    v = dict(e_g_in=v_e_g_in, e_w_in=v_e_w_in, e_g_q_a=v_e_g_q_a, e_w_q_up=v_e_w_q_up, e_g_kv_a=v_e_g_kv_a,
             e_w_kv_up=v_e_w_kv_up, e_sinks=v_e_sinks, e_w_out=v_e_w_out, o_g_in=v_o_g_in, o_w_in=v_o_w_in, o_b_f=v_o_b_f,
             o_w_out=v_o_w_out, g_final=v_g_final)
    order = ("e_g_in", "e_w_in", "e_g_q_a", "e_w_q_up", "e_g_kv_a", "e_w_kv_up", "e_sinks", "e_w_out", "o_g_in", "o_w_in",
             "o_b_f", "o_w_out", "g_final")
    half0 = _packed_weights(w, "layer0")
    full = _unpacked_weights(_run_plan(_gather8_plan(half0), "gather_weights_layer0")[0], half0, "layer0")
    layer1 = _Layer1Exchange(w)

    loss_part, dx, grads = _local_step(
        x[0], positions.reshape(-1, 1), loss_target[0], e_g_in, full["e_w_in"], e_g_q_a, full["e_w_q_up"], e_g_kv_a,
        full["e_w_kv_up"], e_sinks, o_b_f, g_final.reshape(1, D), layer1)

    small = jnp.concatenate([jnp.pad(loss_part.reshape(1), (0, LANES - 1))]
                            + [jnp.pad(grads[n].reshape(-1), (0, (-grads[n].size) % LANES)) for n in REPLICATED])
    rows = small.shape[0] // LANES
    small = jnp.pad(small.reshape(rows, LANES), ((0, (-rows) % 8), (0, 0)))
    sums0, (gathered_small,) = _GroupReduce("layer0").run(grads, _gather8_plan(small))
    gsum = {**layer1.sums, **sums0}
    ssum = _sum_leading(_fill_own_slot(gathered_small, small), "small_grad_sum").reshape(-1)
    loss = ssum[0]
    off = LANES
    for n in REPLICATED:
        cnt = w[n].size
        gsum[n] = ssum[off:off + cnt].reshape(w[n].shape)
        off += cnt + (-cnt) % LANES

    grad, delta, new_m, new_v = {}, {}, {}, {}
    for n in order:
        if n == "o_w_in":
            def tiles(a):
                return jnp.transpose(a, (2, 0, 1)).reshape(-1, LANES)

            def given(a):
                return jnp.transpose(a.reshape(-1, 8, LANES), (1, 2, 0)).reshape(w[n].shape)

            g_t = gsum[n].reshape(-1, LANES)
            outs = _adamw(tiles(w[n]), g_t, tiles(m[n]), tiles(v[n]), "adamw_" + n)
            grad[n], delta[n], new_m[n], new_v[n] = (given(a) for a in (g_t,) + outs)
        elif n in SHARDED:
            outs = _adamw(_as_handled(n, w[n]), gsum[n], _as_handled(n, m[n]), _as_handled(n, v[n]), "adamw_" + n)
            grad[n], delta[n], new_m[n], new_v[n] = (_as_given(n, a, w[n].shape) for a in (gsum[n],) + outs)
        else:
            grad[n] = gsum[n]
            delta[n], new_m[n], new_v[n] = _adamw(w[n], gsum[n], m[n], v[n], "adamw_" + n)
    return (loss, dx[None], *[grad[n] for n in order], *[delta[n] for n in order], *[new_m[n] for n in order],
            *[new_v[n] for n in order])
```

```python
import math

import numpy as np
import jax
import jax.numpy as jnp
from jax import lax
from jax.experimental import pallas as pl
from jax.experimental.pallas import tpu as pltpu

D = 1024
EPS = 1e-6
ROPE_THETA = 10000.0
N_MLA = 8
Q_RANK = 256
KV_RANK = 128
NOPE = 64
ROPE = 32
N_SWA = 8
WINDOW = 128
N_FOX = 16
HEAD = 64
LR, B1, B2, AEPS, WD, STEP = 0.001, 0.9, 0.999, 1e-08, 0.01, 10

LANES = 128
HALF = 64
VMEM_LIMIT = 56 * 1024 * 1024
MXU = jnp.bfloat16
TOK = 256
WG_TOK = 2048
WG_ROWS = 1536
ATT = 256
FWD_CHUNK = 2
BWD_CHUNK = 2
SWA_GROUP = 8
NEG = float("-inf")

PACK_COLS = 1024
SUM_ROWS = 256
ADAM_TILE_BYTES = 2 << 20
MESH_ID = pl.DeviceIdType.MESH


def _pcall(body, *, name, vmem=VMEM_LIMIT, semantics=None, **kw):
    params = dict(vmem_limit_bytes=vmem)
    if semantics is not None:
        params["dimension_semantics"] = semantics
    return pl.pallas_call(body, name=name, compiler_params=pltpu.CompilerParams(**params), **kw)


def _mm(a, b):
    return jnp.dot(a.astype(MXU), b.astype(MXU), preferred_element_type=jnp.float32)


def _mm_nt(a, b):
    return lax.dot_general(a.astype(MXU), b.astype(MXU), (((1,), (1,)), ((), ())),
                           preferred_element_type=jnp.float32)


def _mm_tn(a, b):
    return lax.dot_general(a.astype(MXU), b.astype(MXU), (((0,), (0,)), ((), ())),
                           preferred_element_type=jnp.float32)


def _full(shape):
    n = len(shape)
    return pl.BlockSpec(shape, lambda *_: (0,) * n)


def _rows(tm, n):
    return pl.BlockSpec((tm, n), lambda i: (i, 0))


def _sds(shape, dtype):
    return jax.ShapeDtypeStruct(shape, dtype)


def _rms(x, g):
    r = lax.rsqrt(jnp.mean(x * x, axis=-1, keepdims=True) + EPS)
    return x * r * g


def _rms_bwd(x, g, dy):
    r = lax.rsqrt(jnp.mean(x * x, axis=-1, keepdims=True) + EPS)
    xh = x * r
    dxh = dy * g
    dx = r * (dxh - xh * jnp.mean(dxh * xh, axis=-1, keepdims=True))
    return dx, dy * xh


def _sigmoid(x):
    return 1.0 / (1.0 + jnp.exp(-x))


def _lane_masks():
    lane = lax.broadcasted_iota(jnp.int32, (1, LANES), 1)
    return lane < HALF


def _split_heads(a, lo):
    z = jnp.zeros_like(a)
    return [jnp.where(lo, a, z), jnp.where(lo, z, a)]


def _rope_consts():
    inv = np.zeros((8, LANES), np.float32)
    j = np.arange(ROPE // 2, dtype=np.float32)
    f = (1.0 / (ROPE_THETA ** (np.arange(0, ROPE, 2, dtype=np.float32) / ROPE))).astype(np.float32)
    inv[0, HALF:HALF + 16] = f
    inv[0, HALF + 16:HALF + 32] = f
    inv[1, HALF:HALF + 16] = -1.0
    inv[1, HALF + 16:HALF + 32] = 1.0
    del j
    return jnp.asarray(inv)


def _rope_tables(pos_f, consts):
    ang = pos_f * consts[0:1, :]
    sign = consts[1:2, :]
    c = jnp.where(sign != 0.0, jnp.cos(ang), 1.0)
    s = jnp.sin(ang) * sign
    return c, s


def _swap_halves(v, sign):
    lo = pltpu.roll(v, LANES - 16, axis=1)
    hi = pltpu.roll(v, 16, axis=1)
    return jnp.where(sign < 0.0, lo, jnp.where(sign > 0.0, hi, 0.0))


def _rope(x, c, s, sign):
    return x * c + _swap_halves(x, sign) * s


def _rope_t(dy, c, s, sign):
    return dy * c + _swap_halves(dy * s, sign)


def _layer0_in(x, pos, g_in, w_in, g_q, w_q, g_kv, w_kv):
    S = x.shape[0]
    consts = _rope_consts()

    def body(x_ref, pos_ref, c_ref, g_ref, w_ref, gq_ref, wq_ref, gkv_ref, wkv_ref,
             h_ref, cq_ref, ckv_ref, qm_ref, km_ref, vm_ref,
             qs_ref, kd_ref, vd_ref, gate_ref, cos_ref, sin_ref):
        h = _rms(x_ref[...], g_ref[...])
        h_ref[...] = h.astype(h_ref.dtype)
        z = _mm_nt(h, w_ref[...])
        cq = z[:, 0:256]
        ckv = z[:, 256:384]
        kpe = z[:, 384:512]
        cq_ref[...] = cq
        ckv_ref[...] = ckv
        qs_ref[...] = z[:, 512:1024].astype(qs_ref.dtype)
        kd_ref[...] = z[:, 1024:1536].astype(kd_ref.dtype)
        vd_ref[...] = z[:, 1536:2048].astype(vd_ref.dtype)
        gate_ref[...] = z[:, 2048:3072]
        cqn = _rms(cq, gq_ref[...])
        ckvn = _rms(ckv, gkv_ref[...])
        q = _mm_nt(cqn, wq_ref[...])
        kv = _mm(ckvn, wkv_ref[...])
        vm_ref[...] = kv[:, 1024:1536].astype(vm_ref.dtype)
        consts_v = c_ref[...]
        sign = consts_v[1:2, :]
        c, s = _rope_tables(pos_ref[...].astype(jnp.float32), consts_v)
        cos_ref[...] = c
        sin_ref[...] = s
        kpe_r = _rope(kpe, c, s, sign)
        for hd in range(N_MLA):
            sl = slice(LANES * hd, LANES * (hd + 1))
            qm_ref[:, sl] = _rope(q[:, sl], c, s, sign).astype(qm_ref.dtype)
            km_ref[:, sl] = (kv[:, sl] + kpe_r).astype(km_ref.dtype)

    outs = [
        ((S, D), MXU), ((S, 256), jnp.float32), ((S, 128), jnp.float32),
        ((S, 1024), MXU), ((S, 1024), MXU), ((S, 512), MXU), ((S, 512), MXU), ((S, 512), MXU), ((S, 512), MXU),
        ((S, 1024), jnp.float32), ((S, 128), jnp.float32), ((S, 128), jnp.float32),
    ]
    return _pcall(
        body, name="layer0_in", grid=(S // TOK,), semantics=("arbitrary",),
        in_specs=[_rows(TOK, D), _rows(TOK, 1), _full((8, LANES)), _full((1, D)), _full(w_in.shape), _full((1, 256)),
                  _full(w_q.shape), _full((1, 128)), _full(w_kv.shape)],
        out_specs=[_rows(TOK, s[1]) for s, _ in outs],
        out_shape=[_sds(s, d) for s, d in outs],
    )(x, pos, consts, g_in, w_in, g_q, w_q, g_kv, w_kv)


AUG = (HALF, 0)
ONE = (HALF + 8, 8)


def _data_lanes(idx, h):
    return (idx < HALF) if h == 0 else (idx >= HALF)


def _three_terms(x):
    hi = x.astype(MXU).astype(jnp.float32)
    mid = (x - hi).astype(MXU).astype(jnp.float32)
    lo = (x - hi - mid).astype(MXU).astype(jnp.float32)
    return hi, mid, lo


def _q_aug(qblk, lc, h, scale, lane):
    a = AUG[h]
    hi, mid, lo = _three_terms(lc)
    ones = ((lane >= a + 3) & (lane <= a + 5)).astype(jnp.float32)
    aug = jnp.where(lane == a, hi, jnp.where(lane == a + 1, mid, jnp.where(lane == a + 2, lo, ones)))
    return jnp.where(_data_lanes(lane, h), qblk * jnp.asarray(scale, qblk.dtype), aug.astype(qblk.dtype))


def _k_aug(kblk, lc, h, lane):
    a = AUG[h]
    hi, mid, lo = _three_terms(-lc)
    ones = ((lane >= a) & (lane <= a + 2)).astype(jnp.float32)
    aug = jnp.where(lane == a + 3, hi, jnp.where(lane == a + 4, mid, jnp.where(lane == a + 5, lo, ones)))
    return jnp.where(_data_lanes(lane, h), kblk, aug.astype(kblk.dtype))


def _lc_col(lc_ref, r0, rows, h):
    head = lax.broadcasted_iota(jnp.int32, (1, lc_ref.shape[1]), 1)
    return jnp.sum(jnp.where(head == 2 * pl.program_id(0) + h, lc_ref[pl.ds(r0, rows), :], 0.0), axis=1, keepdims=True)


def _attn_fwd_t(q, k, v, scale, *, split, name, lcc=None, plan=None):
    S = q.shape[0]
    npair = v.shape[1] // LANES
    W = 2 * LANES if split else LANES
    T = ATT
    CH = FWD_CHUNK * T
    assert S % CH == 0
    nq = S // T

    def body(*refs):
        if split:
            q_ref, k_ref, v_ref, o_ref, lse_ref, vt, acc, m_sc = refs
        else:
            q_ref, k_ref, v_ref, lcc_ref, o_ref, lse_ref, kaug, vt, acc, m_sc = refs
        lane = lax.broadcasted_iota(jnp.int32, (1, LANES), 1)
        sub = lax.broadcasted_iota(jnp.int32, (LANES, 1), 0)
        key_minus_qry = lax.broadcasted_iota(jnp.int32, (CH, T), 0) - lax.broadcasted_iota(jnp.int32, (CH, T), 1)

        def prep(i, c):
            r0 = pl.multiple_of(i * T, T)
            vblk = v_ref[pl.ds(r0, T), :].astype(jnp.float32)
            for h in (0, 1):
                vh = jnp.where(_data_lanes(lane, h), vblk, (lane == ONE[h]).astype(jnp.float32))
                vt[h, :, pl.ds(r0, T)] = vh.T.astype(vt.dtype)
                if not split:
                    kaug[h, pl.ds(r0, T), :] = _k_aug(k_ref[pl.ds(r0, T), :], _lc_col(lcc_ref, r0, T, h), h, lane)
            return c

        lax.fori_loop(0, nq, prep, 0)

        def queries(qi):
            q0 = pl.multiple_of(qi * T, T)
            qblk = q_ref[pl.ds(q0, T), :]
            if split:
                return (qblk[:, :LANES], qblk[:, LANES:])
            return tuple(_q_aug(qblk, _lc_col(lcc_ref, q0, T, h), h, scale, lane) for h in (0, 1))

        def scores(qs, c):
            k0 = pl.multiple_of(c * CH, CH)
            out = []
            for h in (0, 1):
                if split:
                    out.append(_mm_nt(k_ref[pl.ds(k0, CH), LANES * h:LANES * (h + 1)], qs[h]) * scale)
                else:
                    out.append(_mm_nt(kaug[h, pl.ds(k0, CH), :], qs[h]))
            return tuple(out)

        def q_block(qi, carry):
            qs, first_scores = carry[:2], carry[2:]
            q0 = pl.multiple_of(qi * T, T)
            acc[...] = jnp.zeros_like(acc)
            m_sc[...] = jnp.full(m_sc.shape, NEG, jnp.float32)

            def absorb(c, sts, masked):
                k0 = pl.multiple_of(c * CH, CH)
                for h in (0, 1):
                    st = sts[h]
                    if masked:
                        st = jnp.where(key_minus_qry <= q0 - k0, st, NEG)
                    m_old = m_sc[h:h + 1, :]
                    m_new = jnp.maximum(m_old, jnp.max(st, axis=0, keepdims=True))
                    alpha = jnp.exp(m_old - m_new)
                    pt = jnp.exp(st - m_new)
                    acc[h] = alpha * acc[h] + _mm(vt[h, :, pl.ds(k0, CH)], pt)
                    m_sc[h:h + 1, :] = m_new

            last = qi // FWD_CHUNK

            def pipelined(c, sts):
                nxt = scores(qs, c + 1)
                absorb(c, sts, False)
                return nxt

            sts = lax.fori_loop(0, last, pipelined, first_scores)
            qs_next = queries(jnp.minimum(qi + 1, nq - 1))
            nxt = qs_next + scores(qs_next, 0)
            absorb(last, sts, True)
            ot = None
            for h in (0, 1):
                a = acc[h]
                l = a[ONE[h]:ONE[h] + 1, :]
                oh = jnp.where(_data_lanes(sub, h), a * (1.0 / l), 0.0)
                ot = oh if ot is None else ot + oh
                lse_ref[0, h:h + 1, pl.ds(q0, T)] = m_sc[h:h + 1, :] + jnp.log(l)
            o_ref[pl.ds(q0, T), :] = ot.T
            return nxt

        qs0 = queries(0)
        lax.fori_loop(0, nq, q_block, qs0 + scores(qs0, 0))

    wide = pl.BlockSpec((S, W), lambda j: (0, j))
    slab = pl.BlockSpec((S, LANES), lambda j: (0, j))
    rows = pl.BlockSpec((1, 2, S), lambda j: (j, 0, 0))
    in_specs = [wide, wide, slab]
    args = [q, k, v]
    scratch = []
    if not split:
        in_specs.append(_full(lcc.shape))
        args.append(lcc)
        scratch.append(pltpu.VMEM((2, S, LANES), MXU))
    scratch += [pltpu.VMEM((2, LANES, S), MXU), pltpu.VMEM((2, LANES, T), jnp.float32), pltpu.VMEM((8, T), jnp.float32)]
    (o, lse), rode = _pcall_riding(
        body, plan, args, name=name, grid=(npair,), in_specs=in_specs, out_specs=[slab, rows],
        out_shape=[_sds((S, npair * LANES), jnp.float32), _sds((npair, 2, S), jnp.float32)], scratch_shapes=scratch)
    return o, lse, rode


def _attn_bwd_t(q, k, v, do, o, lse, scale, *, split, name, lcc=None, plan=None):
    S = q.shape[0]
    npair = v.shape[1] // LANES
    W = 2 * LANES if split else LANES
    T = ATT
    CH = BWD_CHUNK * T
    assert S % CH == 0
    nq = S // T

    def body(*refs):
        if split:
            (q_ref, k_ref, v_ref, do_ref, o_ref, lse_ref, dq_ref, dk_ref, dv_ref, dqt, delta, dk_acc, dv_acc) = refs
        else:
            (q_ref, k_ref, v_ref, do_ref, o_ref, lse_ref, lcc_ref, dq_ref, dk_ref, dv_ref, dlc_ref,
             dqt, delta, dk_acc, dv_acc, qaug, csum) = refs
        lane = lax.broadcasted_iota(jnp.int32, (1, LANES), 1)
        sub = lax.broadcasted_iota(jnp.int32, (LANES, 1), 0)
        key_minus_qry = lax.broadcasted_iota(jnp.int32, (T, CH), 0) - lax.broadcasted_iota(jnp.int32, (T, CH), 1)

        def prep(i, c):
            r0 = pl.multiple_of(i * T, T)
            prod_t = (do_ref[pl.ds(r0, T), :].astype(jnp.float32) * o_ref[pl.ds(r0, T), :]).T
            for h in (0, 1):
                delta[h:h + 1, pl.ds(r0, T)] = jnp.sum(jnp.where(_data_lanes(sub, h), prod_t, 0.0), axis=0, keepdims=True)
                dqt[h, :, pl.ds(r0, T)] = jnp.zeros((LANES, T), jnp.float32)
                if not split:
                    qaug[h, pl.ds(r0, T), :] = _q_aug(q_ref[pl.ds(r0, T), :], _lc_col(lcc_ref, r0, T, h), h, scale, lane)
            return c

        lax.fori_loop(0, nq, prep, 0)

        def keys(ki):
            k0 = pl.multiple_of(ki * T, T)
            kblk = k_ref[pl.ds(k0, T), :]
            if split:
                return (kblk[:, :LANES], kblk[:, LANES:])
            return tuple(_k_aug(kblk, _lc_col(lcc_ref, k0, T, h), h, lane) for h in (0, 1))

        def q_of(c, h):
            q0 = pl.multiple_of(c * CH, CH)
            if split:
                return q_ref[pl.ds(q0, CH), LANES * h:LANES * (h + 1)]
            return qaug[h, pl.ds(q0, CH), :]

        def scores(khs, c):
            out = []
            for h in (0, 1):
                st = _mm_nt(khs[h], q_of(c, h))
                out.append(st * scale if split else st)
            return tuple(out)

        def k_block(ki, carry):
            khs, first_scores = carry[:2], carry[2:]
            k0 = pl.multiple_of(ki * T, T)
            khts = [kh.astype(jnp.float32).T.astype(kh.dtype) for kh in khs]
            vhs = _split_heads(v_ref[pl.ds(k0, T), :], lane < HALF)
            dk_acc[...] = jnp.zeros_like(dk_acc)
            dv_acc[...] = jnp.zeros_like(dv_acc)

            def absorb(c, vals):
                q0 = pl.multiple_of(c * CH, CH)
                dos = _split_heads(do_ref[pl.ds(q0, CH), :], lane < HALF)
                visible = key_minus_qry <= q0 - k0
                for h in (0, 1):
                    dpt = _mm_nt(vhs[h], dos[h])
                    st = jnp.where(visible, vals[h], NEG)
                    pt = jnp.exp(st - lse_ref[0, h:h + 1, pl.ds(q0, CH)])
                    dv_acc[...] += _mm(pt, dos[h])
                    dst = pt * (dpt - delta[h:h + 1, pl.ds(q0, CH)])
                    dk_acc[h] += _mm(dst, q_of(c, h))
                    dqt[h, :, pl.ds(q0, CH)] += _mm(khts[h], dst)

            first = ki // BWD_CHUNK

            def pipelined(c, vals):
                nxt = scores(khs, c + 1)
                absorb(c, vals)
                return nxt

            vals = lax.fori_loop(first, S // CH - 1, pipelined, first_scores)
            kn = jnp.minimum(ki + 1, nq - 1)
            khs_next = keys(kn)
            nxt = khs_next + scores(khs_next, kn // BWD_CHUNK)
            absorb(S // CH - 1, vals)
            if split:
                dk_ref[pl.ds(k0, T), :LANES] = (dk_acc[0] * scale).astype(dk_ref.dtype)
                dk_ref[pl.ds(k0, T), LANES:] = (dk_acc[1] * scale).astype(dk_ref.dtype)
            else:
                dk_ref[pl.ds(k0, T), :] = jnp.where(lane < HALF, dk_acc[0], dk_acc[1]).astype(dk_ref.dtype)
                for h in (0, 1):
                    csum[h:h + 1, pl.ds(k0, T)] = dk_acc[h].T[AUG[h] + 3:AUG[h] + 4, :]
            dv_ref[pl.ds(k0, T), :] = dv_acc[...].astype(dv_ref.dtype)
            return nxt

        khs0 = keys(0)
        lax.fori_loop(0, nq, k_block, khs0 + scores(khs0, 0))

        def finish(i, c):
            r0 = pl.multiple_of(i * T, T)
            if split:
                for h in (0, 1):
                    dq_ref[pl.ds(r0, T), LANES * h:LANES * (h + 1)] = (dqt[h, :, pl.ds(r0, T)].T * scale).astype(dq_ref.dtype)
            else:
                d = jnp.where(sub < HALF, dqt[0, :, pl.ds(r0, T)], dqt[1, :, pl.ds(r0, T)])
                dq_ref[pl.ds(r0, T), :] = (d.T * scale).astype(dq_ref.dtype)
                for h in (0, 1):
                    dlc_ref[0, h:h + 1, pl.ds(r0, T)] = dqt[h, AUG[h]:AUG[h] + 1, pl.ds(r0, T)] - csum[h:h + 1, pl.ds(r0, T)]
            return c

        lax.fori_loop(0, nq, finish, 0)

    wide = pl.BlockSpec((S, W), lambda j: (0, j))
    slab = pl.BlockSpec((S, LANES), lambda j: (0, j))
    rows = pl.BlockSpec((1, 2, S), lambda j: (j, 0, 0))
    in_specs = [wide, wide, slab, slab, slab, rows]
    args = [q, k, v, do, o, lse]
    out_specs = [wide, wide, slab]
    out_shape = [_sds(q.shape, jnp.float32 if split else do.dtype), _sds(k.shape, jnp.float32 if split else do.dtype),
                 _sds(v.shape, do.dtype)]
    scratch = [pltpu.VMEM((2, LANES, S), jnp.float32), pltpu.VMEM((8, S), jnp.float32),
               pltpu.VMEM((2, T, LANES), jnp.float32), pltpu.VMEM((T, LANES), jnp.float32)]
    if not split:
        in_specs.append(_full(lcc.shape))
        args.append(lcc)
        out_specs.append(rows)
        out_shape.append(_sds((npair, 2, S), jnp.float32))
        scratch += [pltpu.VMEM((2, S, LANES), MXU), pltpu.VMEM((8, S), jnp.float32)]
    outs, rode = _pcall_riding(body, plan, args, name=name, grid=(npair,), in_specs=in_specs, out_specs=out_specs,
                               out_shape=out_shape, scratch_shapes=scratch)
    return (*outs, rode)


def _swa_bias(slope, shift):
    a = lax.broadcasted_iota(jnp.int32, (WINDOW, 2 * WINDOW), 0)
    c = lax.broadcasted_iota(jnp.int32, (WINDOW, 2 * WINDOW), 1)
    dist = a - c + shift
    return jnp.where((dist >= 0) & (dist < WINDOW), -slope * dist.astype(jnp.float32), NEG)


def _swa_scores(qh, kblk, bias):
    return _mm_nt(qh, kblk) * (HEAD ** -0.5) + bias


def _swa_stack(blk, lo):
    return jnp.concatenate(_split_heads(blk[:, :LANES], lo) + _split_heads(blk[:, LANES:], lo), axis=0)


def _swa_unstack(x, lo):
    r = x.shape[0] // 4
    return jnp.concatenate([jnp.where(lo, x[0:r], x[r:2 * r]), jnp.where(lo, x[2 * r:3 * r], x[3 * r:])], axis=1)


def _swa_per_head(ref, j, rows):
    quarter = lax.broadcasted_iota(jnp.int32, (4 * rows, 1), 0) // rows
    return jnp.where(quarter == 0, ref[4 * j], jnp.where(quarter == 1, ref[4 * j + 1],
                                                         jnp.where(quarter == 2, ref[4 * j + 2], ref[4 * j + 3])))


def _swa_fwd(q, kd, vd, sinks, slopes):
    S = q.shape[0]
    nkv = q.shape[1] // (2 * LANES)
    nb = S // WINDOW
    group = math.gcd(SWA_GROUP, nb)

    def body(sink_ref, slope_ref, q_ref, k_ref, v_ref, o_ref, lse_ref):
        j = pl.program_id(0)
        lo = _lane_masks()
        sink = _swa_per_head(sink_ref, j, WINDOW)
        biases = [jnp.concatenate([_swa_bias(slope_ref[4 * j + h], shift) for h in range(4)], axis=0)
                  for shift in (0, WINDOW)]

        def q_block(qi, c):
            q0 = pl.multiple_of(qi * WINDOW, WINDOW)
            k0 = pl.multiple_of(jnp.maximum(qi - 1, 0) * WINDOW, WINDOW)
            s = _swa_scores(_swa_stack(q_ref[pl.ds(q0, WINDOW), :], lo), k_ref[pl.ds(k0, 2 * WINDOW), :],
                            jnp.where(qi == 0, *biases))
            m = jnp.maximum(jnp.max(s, axis=1, keepdims=True), sink)
            p = jnp.exp(s - m)
            den = jnp.sum(p, axis=1, keepdims=True) + jnp.exp(sink - m)
            o_ref[pl.ds(q0, WINDOW), :] = _swa_unstack(_mm(p / den, v_ref[pl.ds(k0, 2 * WINDOW), :]), lo)
            lse = m + jnp.log(den)
            for h in range(4):
                lse_ref[h, pl.ds(q0, WINDOW), :] = lse[h * WINDOW:(h + 1) * WINDOW]
            return c

        def q_group(gi, c):
            for g in range(group):
                q_block(gi * group + g, c)
            return c

        lax.fori_loop(0, nb // group, q_group, 0)

    smem = pl.BlockSpec(memory_space=pltpu.SMEM)
    two = pl.BlockSpec((S, 2 * LANES), lambda j: (0, j))
    kv = pl.BlockSpec((S, LANES), lambda j: (0, 2 * j))
    return _pcall(
        body, name="swa_fwd", grid=(nkv,), semantics=("arbitrary",),
        in_specs=[smem, smem, two, kv, kv],
        out_specs=[two, pl.BlockSpec((4, S, 1), lambda j: (j, 0, 0))],
        out_shape=[_sds(q.shape, jnp.float32), _sds((4 * nkv, S, 1), jnp.float32)],
    )(sinks, slopes, q, kd, vd)


def _swa_bwd(q, kd, vd, do, o, lse, sinks, slopes, plan=None):
    S = q.shape[0]
    nkv = q.shape[1] // (2 * LANES)
    nb = S // WINDOW
    group = math.gcd(SWA_GROUP, nb)

    def body(sink_ref, slope_ref, q_ref, k_ref, v_ref, do_ref, o_ref, lse_ref,
             dq_ref, dk_ref, dv_ref, dsink_ref, dk_acc, dv_acc):
        j = pl.program_id(0)
        lo = _lane_masks()
        dk_acc[...] = jnp.zeros_like(dk_acc)
        dv_acc[...] = jnp.zeros_like(dv_acc)
        sink = _swa_per_head(sink_ref, j, WINDOW)
        biases = [jnp.concatenate([_swa_bias(slope_ref[4 * j + h], shift) for h in range(4)], axis=0)
                  for shift in (0, WINDOW)]

        def q_block(qi, carry):
            q0 = pl.multiple_of(qi * WINDOW, WINDOW)
            k0 = pl.multiple_of(jnp.maximum(qi - 1, 0) * WINDOW, WINDOW)
            q4 = _swa_stack(q_ref[pl.ds(q0, WINDOW), :], lo)
            do4 = _swa_stack(do_ref[pl.ds(q0, WINDOW), :], lo)
            oblk = o_ref[pl.ds(q0, WINDOW), :]
            o4 = jnp.concatenate([oblk[:, :LANES], oblk[:, :LANES], oblk[:, LANES:], oblk[:, LANES:]], axis=0)
            kblk = k_ref[pl.ds(k0, 2 * WINDOW), :]
            vblk = v_ref[pl.ds(k0, 2 * WINDOW), :]
            lse = jnp.concatenate([lse_ref[h, pl.ds(q0, WINDOW), :] for h in range(4)], axis=0)
            p = jnp.exp(_swa_scores(q4, kblk, jnp.where(qi == 0, *biases)) - lse)
            delta = jnp.sum(do4.astype(jnp.float32) * o4, axis=1, keepdims=True)
            dv_acc[pl.ds(k0, 2 * WINDOW), :] += _mm_tn(p, do4)
            ds = p * (_mm_nt(do4, vblk) - delta)
            dq_ref[pl.ds(q0, WINDOW), :] = _swa_unstack(_mm(ds, kblk) * (HEAD ** -0.5), lo).astype(dq_ref.dtype)
            dk_acc[pl.ds(k0, 2 * WINDOW), :] += _mm_tn(ds, q4) * (HEAD ** -0.5)
            dsk = -jnp.exp(sink - lse) * delta
            return tuple(carry[h] + jnp.sum(dsk[h * WINDOW:(h + 1) * WINDOW], axis=0, keepdims=True)
                         for h in range(4))

        def q_group(gi, carry):
            for g in range(group):
                carry = q_block(gi * group + g, carry)
            return carry

        zero = jnp.zeros((1, 1), jnp.float32)
        dsinks = lax.fori_loop(0, nb // group, q_group, (zero,) * 4)
        dk_ref[:, :LANES] = dk_acc[...].astype(dk_ref.dtype)
        dk_ref[:, LANES:] = jnp.zeros((S, LANES), dk_ref.dtype)
        dv_ref[:, :LANES] = dv_acc[...].astype(dv_ref.dtype)
        dv_ref[:, LANES:] = jnp.zeros((S, LANES), dv_ref.dtype)
        r = lax.broadcasted_iota(jnp.int32, (8, LANES), 0)
        dsink_ref[0] = jnp.where(r == 0, dsinks[0], jnp.where(r == 1, dsinks[1], jnp.where(r == 2, dsinks[2],
                                 jnp.where(r == 3, dsinks[3], 0.0))))

    smem = pl.BlockSpec(memory_space=pltpu.SMEM)
    two = pl.BlockSpec((S, 2 * LANES), lambda j: (0, j))
    kv = pl.BlockSpec((S, LANES), lambda j: (0, 2 * j))
    outs, rode = _pcall_riding(
        body, plan, [sinks, slopes, q, kd, vd, do, o, lse], name="swa_bwd", grid=(nkv,),
        in_specs=[smem, smem, two, kv, kv, two, two, pl.BlockSpec((4, S, 1), lambda j: (j, 0, 0))],
        out_specs=[two, two, two, pl.BlockSpec((1, 8, LANES), lambda j: (j, 0, 0))],
        out_shape=[_sds(q.shape, do.dtype), _sds(kd.shape, do.dtype), _sds(vd.shape, do.dtype),
                   _sds((nkv, 8, LANES), jnp.float32)],
        scratch_shapes=[pltpu.VMEM((S, LANES), jnp.float32), pltpu.VMEM((S, LANES), jnp.float32)])
    return (*outs, rode)


def _log_steps(S):
    k, out = 1, []
    while k < S:
        out.append(k)
        k *= 2
    return out


def _forget_fwd(f_row, b_col):
    S = f_row.shape[1]

    def body(f_ref, b_ref, lc_ref):
        x = f_ref[...] + b_ref[...]
        lc = jnp.minimum(x, 0.0) - jnp.log(1.0 + jnp.exp(-jnp.abs(x)))
        idx = lax.broadcasted_iota(jnp.int32, lc.shape, 1)
        for k in _log_steps(S):
            lc = lc + jnp.where(idx >= k, pltpu.roll(lc, k, axis=1), 0.0)
        lc_ref[...] = lc

    return _pcall(body, name="forget_fwd", out_shape=_sds(f_row.shape, jnp.float32))(f_row, b_col)


def _forget_bwd(dlc_row, f_row, b_col):
    S = f_row.shape[1]

    def body(d_ref, f_ref, b_ref, df_ref, db_ref):
        g = d_ref[...]
        idx = lax.broadcasted_iota(jnp.int32, g.shape, 1)
        for k in _log_steps(S):
            g = g + jnp.where(idx < S - k, pltpu.roll(g, S - k, axis=1), 0.0)
        x = f_ref[...] + b_ref[...]
        df = g * _sigmoid(-x)
        df_ref[...] = df
        db_ref[...] = jnp.sum(df, axis=1, keepdims=True)

    return _pcall(body, name="forget_bwd",
                  out_shape=[_sds(f_row.shape, jnp.float32), _sds((f_row.shape[0], 1), jnp.float32)])(dlc_row, f_row, b_col)


def _layer0_out_layer1_in(x, o_m, o_s, gate, w_out, g1, w_in1):
    S = x.shape[0]

    def body(x_ref, om_ref, os_ref, gate_ref, wo_ref, g_ref, w_ref,
             x1_ref, h_ref, q_ref, k_ref, v_ref, g1_ref, f_ref):
        gt = gate_ref[...]
        sg = gt * _sigmoid(gt)
        um = om_ref[...] * sg[:, :512]
        us = os_ref[...] * sg[:, 512:]
        x1 = x_ref[...] + _mm(um, wo_ref[0:512, :]) + _mm(us, wo_ref[512:1024, :])
        x1_ref[...] = x1
        h = _rms(x1, g_ref[...])
        h_ref[...] = h.astype(h_ref.dtype)
        z = _mm_nt(h, w_ref[...])
        q_ref[...] = z[:, 0:1024].astype(q_ref.dtype)
        k_ref[...] = z[:, 1024:2048].astype(k_ref.dtype)
        v_ref[...] = z[:, 2048:3072].astype(v_ref.dtype)
        g1_ref[...] = z[:, 3072:4096]
        f_ref[...] = z[:, 4096:4224]

    outs = [((S, D), jnp.float32), ((S, D), MXU), ((S, D), MXU), ((S, D), MXU), ((S, D), MXU),
            ((S, D), jnp.float32), ((S, LANES), jnp.float32)]
    return _pcall(
        body, name="layer0_out_layer1_in", grid=(S // TOK,), semantics=("arbitrary",),
        in_specs=[_rows(TOK, D), _rows(TOK, 512), _rows(TOK, 512), _rows(TOK, D), _full((D, D)), _full((1, D)),
                  _full(w_in1.shape)],
        out_specs=[_rows(TOK, s[1]) for s, _ in outs],
        out_shape=[_sds(s, d) for s, d in outs],
    )(x, o_m, o_s, gate, w_out, g1, w_in1)


def _head(x1, o1, gate1, w_out1, g_f, target):
    S = x1.shape[0]

    def body(x1_ref, o_ref, gate_ref, wo_ref, g_ref, t_ref,
             loss_ref, dgf_ref, dwo_ref, dx2_ref, do_ref, dgate_ref):
        i = pl.program_id(0)
        gt = gate_ref[...]
        sig = _sigmoid(gt)
        sg = gt * sig
        o = o_ref[...]
        u = o * sg
        x2 = x1_ref[...] + _mm(u, wo_ref[...])
        g = g_ref[...]
        y = _rms(x2, g)
        err = y - t_ref[...]
        part = 0.5 * jnp.sum(jnp.mean(err * err, axis=-1, keepdims=True), axis=0, keepdims=True)
        dy = err * (1.0 / D)
        dx2, dg_rows = _rms_bwd(x2, g, dy)
        dx2_ref[...] = dx2
        du = _mm_nt(dx2, wo_ref[...])
        do_ref[...] = (du * sg).astype(do_ref.dtype)
        dgate_ref[...] = (du * o * (sig * (1.0 + gt * (1.0 - sig)))).astype(dgate_ref.dtype)

        @pl.when(i == 0)
        def _():
            loss_ref[...] = jnp.zeros_like(loss_ref)
            dgf_ref[...] = jnp.zeros_like(dgf_ref)
            dwo_ref[...] = jnp.zeros_like(dwo_ref)

        loss_ref[...] += jnp.broadcast_to(part, loss_ref.shape)
        dgf_ref[...] += jnp.sum(dg_rows, axis=0, keepdims=True)
        dwo_ref[...] += _mm_tn(u, dx2)

    outs = [((S, D), jnp.float32), ((S, D), MXU), ((S, D), MXU)]
    return _pcall(
        body, name="head", grid=(S // TOK,), semantics=("arbitrary",),
        in_specs=[_rows(TOK, D), _rows(TOK, D), _rows(TOK, D), _full((D, D)), _full((1, D)), _rows(TOK, D)],
        out_specs=[_full((8, LANES)), _full((1, D)), _full((D, D))] + [_rows(TOK, D) for _ in outs],
        out_shape=[_sds((8, LANES), jnp.float32), _sds((1, D), jnp.float32), _sds((D, D), jnp.float32)]
        + [_sds(s, d) for s, d in outs],
    )(x1, o1, gate1, w_out1, g_f, target)


def _layer1_in_bwd(dq, dk, dv, dgate1, df, x1, dx2, g1, w_in1, gate0, o_m, o_s, w_out0):
    S = x1.shape[0]

    def body(dq_ref, dk_ref, dv_ref, dg1_ref, df_ref, x1_ref, dx2_ref, g_ref, w_ref, gate_ref, om_ref, os_ref,
             wo_ref, dz_ref, dx1_ref, dgn_ref, dwo_ref, dom_ref, dos_ref, dgate_ref):
        i = pl.program_id(0)
        dz_ref[:, 0:1024] = dq_ref[...]
        dz_ref[:, 1024:2048] = dk_ref[...]
        dz_ref[:, 2048:3072] = dv_ref[...]
        dz_ref[:, 3072:4096] = dg1_ref[...]
        dz_ref[:, 4096:4224] = df_ref[...]
        dh = _mm(dz_ref[...], w_ref[...])
        g = g_ref[...]
        dxn, dg_rows = _rms_bwd(x1_ref[...], g, dh)
        dx1 = dx2_ref[...] + dxn
        dx1_ref[...] = dx1
        du = _mm_nt(dx1, wo_ref[...])
        gt = gate_ref[...]
        sig = _sigmoid(gt)
        sg = gt * sig
        dsg = sig * (1.0 + gt * (1.0 - sig))
        dom_ref[...] = (du[:, :512] * sg[:, :512]).astype(dom_ref.dtype)
        dos_ref[...] = (du[:, 512:] * sg[:, 512:]).astype(dos_ref.dtype)
        dgate_ref[:, :512] = (du[:, :512] * om_ref[...] * dsg[:, :512]).astype(dgate_ref.dtype)
        dgate_ref[:, 512:] = (du[:, 512:] * os_ref[...] * dsg[:, 512:]).astype(dgate_ref.dtype)

        @pl.when(i == 0)
        def _():
            dgn_ref[...] = jnp.zeros_like(dgn_ref)
            dwo_ref[...] = jnp.zeros_like(dwo_ref)

        dgn_ref[...] += jnp.sum(dg_rows, axis=0, keepdims=True)
        dwo_ref[0:512, :] += _mm_tn(om_ref[...] * sg[:, :512], dx1)
        dwo_ref[512:1024, :] += _mm_tn(os_ref[...] * sg[:, 512:], dx1)

    return _pcall(
        body, name="layer1_in_bwd", grid=(S // TOK,), semantics=("arbitrary",),
        in_specs=[_rows(TOK, D), _rows(TOK, D), _rows(TOK, D), _rows(TOK, D), _rows(TOK, LANES), _rows(TOK, D),
                  _rows(TOK, D), _full((1, D)), _full(w_in1.shape), _rows(TOK, D), _rows(TOK, 512), _rows(TOK, 512),
                  _full((D, D))],
        out_specs=[_rows(TOK, 4224), _rows(TOK, D), _full((1, D)), _full((D, D)), _rows(TOK, 512), _rows(TOK, 512),
                   _rows(TOK, D)],
        out_shape=[_sds((S, 4224), MXU), _sds((S, D), jnp.float32), _sds((1, D), jnp.float32), _sds((D, D), jnp.float32),
                   _sds((S, 512), MXU), _sds((S, 512), MXU), _sds((S, D), MXU)],
    )(dq, dk, dv, dgate1, df, x1, dx2, g1, w_in1, gate0, o_m, o_s, w_out0)


def _layer0_in_bwd(dqm, dkm, dvm, dqs, dkd, dvd, dgate0, cos, sin, cq, ckv, x, dx1, g_in, w_in, g_q, w_q, g_kv, w_kv):
    S = x.shape[0]
    consts = _rope_consts()

    def body(dqm_ref, dkm_ref, dvm_ref, dqs_ref, dkd_ref, dvd_ref, dgate_ref, cos_ref, sin_ref, c_ref, cq_ref, ckv_ref,
             x_ref, dx1_ref, g_ref, w_ref, gq_ref, wq_ref, gkv_ref, wkv_ref,
             dx_ref, dz_ref, dgin_ref, dgq_ref, dgkv_ref, dwq_ref, dwkv_ref, dqu_ref, dkvu_ref):
        i = pl.program_id(0)
        lo = _lane_masks()
        sign = c_ref[...][1:2, :]
        c = cos_ref[...]
        s = sin_ref[...]
        dkpe = None
        for hd in range(N_MLA):
            sl = slice(LANES * hd, LANES * (hd + 1))
            dqu_ref[:, sl] = _rope_t(dqm_ref[:, sl], c, s, sign).astype(dqu_ref.dtype)
            dkh = dkm_ref[:, sl]
            dkvu_ref[:, sl] = jnp.where(lo, dkh, 0.0).astype(dkvu_ref.dtype)
            dkpe = dkh if dkpe is None else dkpe + dkh
        dkvu_ref[:, 1024:1536] = dvm_ref[...]
        dkpe = _rope_t(jnp.where(lo, 0.0, dkpe), c, s, sign)
        dcqn = _mm(dqu_ref[...], wq_ref[...])
        dckvn = _mm_nt(dkvu_ref[...], wkv_ref[...])
        gq = gq_ref[...]
        gkv = gkv_ref[...]
        dcq, dgq_rows = _rms_bwd(cq_ref[...], gq, dcqn)
        dckv, dgkv_rows = _rms_bwd(ckv_ref[...], gkv, dckvn)
        dz_ref[:, 0:256] = dcq.astype(dz_ref.dtype)
        dz_ref[:, 256:384] = dckv.astype(dz_ref.dtype)
        dz_ref[:, 384:512] = dkpe.astype(dz_ref.dtype)
        dz_ref[:, 512:1024] = dqs_ref[...]
        dz_ref[:, 1024:1536] = dkd_ref[...]
        dz_ref[:, 1536:2048] = dvd_ref[...]
        dz_ref[:, 2048:3072] = dgate_ref[...]
        dh = _mm(dz_ref[...], w_ref[...])
        g = g_ref[...]
        dxn, dg_rows = _rms_bwd(x_ref[...], g, dh)
        dx_ref[...] = dx1_ref[...] + dxn

        @pl.when(i == 0)
        def _():
            dgin_ref[...] = jnp.zeros_like(dgin_ref)
            dgq_ref[...] = jnp.zeros_like(dgq_ref)
            dgkv_ref[...] = jnp.zeros_like(dgkv_ref)
            dwq_ref[...] = jnp.zeros_like(dwq_ref)
            dwkv_ref[...] = jnp.zeros_like(dwkv_ref)

        dgin_ref[...] += jnp.sum(dg_rows, axis=0, keepdims=True)
        dgq_ref[...] += jnp.sum(dgq_rows, axis=0, keepdims=True)
        dgkv_ref[...] += jnp.sum(dgkv_rows, axis=0, keepdims=True)
        dwq_ref[...] += _mm_tn(dqu_ref[...], _rms(cq_ref[...], gq))
        dwkv_ref[...] += _mm_tn(_rms(ckv_ref[...], gkv), dkvu_ref[...])

    return _pcall(
        body, name="layer0_in_bwd", grid=(S // TOK,), semantics=("arbitrary",),
        in_specs=[_rows(TOK, 1024), _rows(TOK, 1024), _rows(TOK, 512), _rows(TOK, 512), _rows(TOK, 512), _rows(TOK, 512),
                  _rows(TOK, D), _rows(TOK, LANES), _rows(TOK, LANES), _full((8, LANES)), _rows(TOK, 256), _rows(TOK, 128),
                  _rows(TOK, D), _rows(TOK, D), _full((1, D)), _full(w_in.shape), _full((1, 256)), _full(w_q.shape),
                  _full((1, 128)), _full(w_kv.shape)],
        out_specs=[_rows(TOK, D), _rows(TOK, 3072), _full((1, D)), _full((1, 256)), _full((1, 128)), _full(w_q.shape),
                   _full(w_kv.shape)],
        out_shape=[_sds((S, D), jnp.float32), _sds((S, 3072), MXU), _sds((1, D), jnp.float32), _sds((1, 256), jnp.float32),
                   _sds((1, 128), jnp.float32), _sds(w_q.shape, jnp.float32), _sds(w_kv.shape, jnp.float32)],
        scratch_shapes=[pltpu.VMEM((TOK, 1024), MXU), pltpu.VMEM((TOK, 1536), MXU)],
    )(dqm, dkm, dvm, dqs, dkd, dvd, dgate0, cos, sin, consts, cq, ckv, x, dx1, g_in, w_in, g_q, w_q, g_kv, w_kv)


def _wgrad(a, b, name):
    S, M = a.shape
    N = b.shape[1]
    tm = next(t for t in range(WG_ROWS, 0, -LANES) if M % t == 0)
    tn = N if N <= 1024 else 512
    tk = min(WG_TOK, S)

    def body(a_ref, b_ref, o_ref):
        @pl.when(pl.program_id(2) == 0)
        def _():
            o_ref[...] = jnp.zeros_like(o_ref)

        o_ref[...] += _mm_tn(a_ref[...], b_ref[...])

    return _pcall(
        body, name=name, grid=(M // tm, N // tn, S // tk), semantics=("parallel", "parallel", "arbitrary"),
        in_specs=[pl.BlockSpec((tk, tm), lambda m, n, k: (k, m)), pl.BlockSpec((tk, tn), lambda m, n, k: (k, n))],
        out_specs=pl.BlockSpec((tm, tn), lambda m, n, k: (m, n)),
        out_shape=_sds((M, N), jnp.float32),
    )(a, b)


def _adamw(w, g, m, v, name):
    shape = w.shape
    R, C = (int(np.prod(shape[:-1])), shape[-1])
    w2, g2, m2, v2 = (t.reshape(R, C) for t in (w, g, m, v))
    fits = [t for t in range(8, ADAM_TILE_BYTES // (4 * C) + 1, 8) if R % t == 0]
    tr = max(fits) if fits else R
    tc = C if (tr * C * 4 <= ADAM_TILE_BYTES or C % 256) else 256

    def body(w_ref, g_ref, m_ref, v_ref, d_ref, nm_ref, nv_ref):
        gg = g_ref[...]
        nm = B1 * m_ref[...] + (1.0 - B1) * gg
        nv = B2 * v_ref[...] + (1.0 - B2) * (gg * gg)
        m_hat = nm / (1.0 - B1 ** STEP)
        v_hat = nv / (1.0 - B2 ** STEP)
        d_ref[...] = -LR * (m_hat / (jnp.sqrt(v_hat) + AEPS) + WD * w_ref[...])
        nm_ref[...] = nm
        nv_ref[...] = nv

    spec = pl.BlockSpec((tr, tc), lambda i, j: (i, j))
    d, nm, nv = _pcall(
        body, name=name, grid=(R // tr, C // tc), semantics=("parallel", "parallel"),
        in_specs=[spec] * 4, out_specs=[spec] * 3, out_shape=[_sds((R, C), jnp.float32)] * 3,
    )(w2, g2, m2, v2)
    return d.reshape(shape), nm.reshape(shape), nv.reshape(shape)


def _sum_leading(a, name):
    n, R, C = a.shape
    tr = SUM_ROWS if R % SUM_ROWS == 0 else R

    def body(a_ref, o_ref):
        acc = a_ref[0]
        for i in range(1, n):
            acc = acc + a_ref[i]
        o_ref[...] = acc

    return _pcall(
        body, name=name, grid=(R // tr,), semantics=("parallel",),
        in_specs=[pl.BlockSpec((n, tr, C), lambda i: (0, i, 0))], out_specs=_rows(tr, C),
        out_shape=_sds((R, C), a.dtype),
    )(a)


def _add_halves(g, c, b, name, out_dtype):
    n, _, R, C = g.shape
    tr = SUM_ROWS if R % SUM_ROWS == 0 else R

    def body(c_ref, a_ref, b_ref, o_ref):
        o_ref[...] = (a_ref[0] + b_ref[...]).astype(o_ref.dtype)

    spec = pl.BlockSpec((1, tr, C), lambda k, i, c_ref: (k, i, 0))
    grid_spec = pltpu.PrefetchScalarGridSpec(
        num_scalar_prefetch=1, grid=(n, R // tr),
        in_specs=[pl.BlockSpec((1, 1, tr, C), lambda k, i, c_ref: (k, c_ref[0], i, 0)), spec], out_specs=spec)
    return _pcall(body, name=name, semantics=("parallel", "parallel"), grid_spec=grid_spec,
                  out_shape=_sds(b.shape, out_dtype))(c.reshape(1).astype(jnp.int32), g, b)


def _total_sum(mine, theirs, recv, name):
    R, C = mine.shape
    n = recv.shape[0]
    tr = SUM_ROWS if R % SUM_ROWS == 0 else R

    def body(a_ref, b_ref, r_ref, o_ref):
        acc = a_ref[...] + b_ref[...]
        for i in range(n):
            acc = acc + r_ref[i].astype(jnp.float32)
        o_ref[...] = acc

    return _pcall(
        body, name=name, grid=(R // tr,), semantics=("parallel",),
        in_specs=[_rows(tr, C), _rows(tr, C), pl.BlockSpec((n, tr, C), lambda i: (0, i, 0))], out_specs=_rows(tr, C),
        out_shape=_sds((R, C), jnp.float32),
    )(mine, theirs, recv)


def _place():
    return lax.axis_index("x"), lax.axis_index("y"), lax.axis_index("c")


class _Plan:
    def __init__(self, arrays, out_shape, scratch, start, finish, middle=None):
        self.arrays, self.out_shape, self.scratch = list(arrays), list(out_shape), list(scratch)
        self.start, self.finish, self.middle = start, finish, middle


def _gather8_plan(block):
    R, C = block.shape

    def parts(ins, outs, sems):
        (x_ref,), (out_ref,), (send_sems, recv_sems) = ins, outs, sems
        x, y, c = _place()
        me, sibling = (x, y, c), (x, y, 1 - c)
        chips = [(1 - x, y), (x, 1 - y), (1 - x, 1 - y)]

        def copy(k, blk, to, src=None):
            slot = out_ref.at[4 * blk[0] + 2 * blk[1] + blk[2]]
            return pltpu.make_async_remote_copy(
                src_ref=slot if src is None else src, dst_ref=slot,
                send_sem=send_sems.at[k], recv_sem=recv_sems.at[k], device_id=to, device_id_type=MESH_ID)

        def first():
            return [copy(0, me, sibling, src=x_ref)] + [copy(1 + j, me, (*chip, c), src=x_ref) for j, chip in enumerate(chips)]

        def passed():
            return [copy(4 + j, (*chip, c), sibling) for j, chip in enumerate(chips)]

        def arrivals():
            return [copy(1 + j, (*chip, c), me) for j, chip in enumerate(chips)]

        def late():
            return [copy(0, sibling, me)] + [copy(4 + j, (*chip, 1 - c), me) for j, chip in enumerate(chips)]

        return first, passed, arrivals, late

    def start(ins, outs, sems):
        for cp in parts(ins, outs, sems)[0]():
            cp.start()

    def middle(ins, outs, sems):
        _, passed, arrivals, _ = parts(ins, outs, sems)
        for arrived, forward in zip(arrivals(), passed()):
            arrived.wait_recv()
            forward.start()

    def finish(ins, outs, sems):
        first, passed, _, late = parts(ins, outs, sems)
        for cp in late():
            cp.wait_recv()
        for cp in first() + passed():
            cp.wait_send()

    return _Plan([block], [_sds((8, R, C), block.dtype)], [pltpu.SemaphoreType.DMA((7,)), pltpu.SemaphoreType.DMA((7,))],
                 start, finish, middle)


def _fill_own_slot(gathered, block):
    x, y, c = _place()
    return lax.dynamic_update_index_in_dim(gathered, block, 4 * x + 2 * y + c, 0)


def _started_and_waited(arrays, out_shape, n, copies):
    def start(ins, outs, sems):
        for cp in copies(ins, outs, sems):
            cp.start()

    def finish(ins, outs, sems):
        for cp in copies(ins, outs, sems):
            cp.wait()

    return _Plan(arrays, out_shape, [pltpu.SemaphoreType.DMA((n,)), pltpu.SemaphoreType.DMA((n,))], start, finish)


def _pair_swap_plan(g):
    n = g.shape[0]

    def copies(ins, outs, sems):
        (g_ref,), (out_ref,), (send_sems, recv_sems) = ins, outs, sems
        x, y, c = _place()
        return [pltpu.make_async_remote_copy(src_ref=g_ref.at[k, 1 - c], dst_ref=out_ref.at[k], send_sem=send_sems.at[k],
                                             recv_sem=recv_sems.at[k], device_id=(x, y, 1 - c), device_id_type=MESH_ID)
                for k in range(n)]

    return _started_and_waited([g], [_sds((n,) + g.shape[2:], g.dtype)], n, copies)


def _chip_exchange_plan(p):
    def copies(ins, outs, sems):
        (p_ref,), (out_ref,), (send_sems, recv_sems) = ins, outs, sems
        x, y, c = _place()
        chips = [(1 - x, y), (x, 1 - y), (1 - x, 1 - y)]
        return [pltpu.make_async_remote_copy(
            src_ref=p_ref.at[2 * cx + cy], dst_ref=out_ref.at[j], send_sem=send_sems.at[j],
            recv_sem=recv_sems.at[j], device_id=(cx, cy, c), device_id_type=MESH_ID)
            for j, (cx, cy) in enumerate(chips)]

    return _started_and_waited([p], [_sds((3,) + p.shape[1:], p.dtype)], 3, copies)


def _pair_exchange_plan(t):
    def copies(ins, outs, sems):
        (t_ref,), (out_ref,), (send_sems, recv_sems) = ins, outs, sems
        x, y, c = _place()
        return [pltpu.make_async_remote_copy(src_ref=t_ref, dst_ref=out_ref, send_sem=send_sems.at[0], recv_sem=recv_sems.at[0],
                                             device_id=(x, y, 1 - c), device_id_type=MESH_ID)]

    return _started_and_waited([t], [_sds(t.shape, t.dtype)], 1, copies)


def _both_plans(a, b):
    na, ma, sa = len(a.arrays), len(a.out_shape), len(a.scratch)

    def phase(name):
        fa, fb = getattr(a, name), getattr(b, name)
        if fa is None and fb is None:
            return None

        def run(ins, outs, sems):
            if fa is not None:
                fa(ins[:na], outs[:ma], sems[:sa])
            if fb is not None:
                fb(ins[na:], outs[ma:], sems[sa:])
        return run

    return _Plan(a.arrays + b.arrays, a.out_shape + b.out_shape, a.scratch + b.scratch,
                 phase("start"), phase("finish"), phase("middle"))


ANY_SPEC = pl.BlockSpec(memory_space=pl.ANY)


def _run_plan(plan, name):
    n_in, n_out = len(plan.arrays), len(plan.out_shape)

    def body(*refs):
        ins, outs, sems = refs[:n_in], refs[n_in:n_in + n_out], refs[n_in + n_out:]
        plan.start(ins, outs, sems)
        if plan.middle is not None:
            plan.middle(ins, outs, sems)
        plan.finish(ins, outs, sems)

    return _pcall(body, name=name, in_specs=[ANY_SPEC] * n_in, out_specs=[ANY_SPEC] * n_out, out_shape=plan.out_shape,
                  scratch_shapes=plan.scratch)(*plan.arrays)


def _pcall_riding(body, plan, args, *, name, grid, in_specs, out_specs, out_shape, scratch_shapes):
    if plan is None:
        outs = _pcall(body, name=name, grid=grid, semantics=("arbitrary",), in_specs=in_specs, out_specs=out_specs,
                      out_shape=out_shape, scratch_shapes=scratch_shapes)(*args)
        return list(outs), None
    n_in, n_out, n_s = len(args), len(out_shape), len(scratch_shapes)
    p_in, p_out = len(plan.arrays), len(plan.out_shape)
    steps = grid[0]

    def riding(*refs):
        ins, pins = refs[:n_in], refs[n_in:n_in + p_in]
        o0 = n_in + p_in
        outs, pouts = refs[o0:o0 + n_out], refs[o0 + n_out:o0 + n_out + p_out]
        s0 = o0 + n_out + p_out
        scr, sems = refs[s0:s0 + n_s], refs[s0 + n_s:]
        j = pl.program_id(0)

        @pl.when(j == 0)
        def _():
            plan.start(pins, pouts, sems)

        if plan.middle is not None:
            @pl.when(j == steps // 2)
            def _():
                plan.middle(pins, pouts, sems)

        body(*ins, *outs, *scr)

        @pl.when(j == steps - 1)
        def _():
            plan.finish(pins, pouts, sems)

    res = _pcall(riding, name=name, grid=grid, semantics=("arbitrary",), in_specs=list(in_specs) + [ANY_SPEC] * p_in,
                 out_specs=list(out_specs) + [ANY_SPEC] * p_out, out_shape=list(out_shape) + plan.out_shape,
                 scratch_shapes=list(scratch_shapes) + plan.scratch)(*args, *plan.arrays)
    return list(res[:n_out]), list(res[n_out:])


class _RowSeq:
    def __init__(self, pieces):
        self.pieces = list(pieces)

    def rows(self, a, b):
        out, off = [], 0
        for p in self.pieces:
            lo, hi = max(a, off), min(b, off + p.shape[0])
            if lo < hi:
                out.append(p[lo - off:hi - off])
            off += p.shape[0]
        return out

    def array(self):
        return jnp.concatenate(self.pieces, axis=0)


def _row_seq(w):
    return w if isinstance(w, _RowSeq) else _RowSeq([w])


def _prep_w_in0(wt):
    wt = _row_seq(wt)
    one = wt.pieces[0]
    z32 = [jnp.zeros((32, one.shape[1]), one.dtype)]
    k0, k1 = wt.rows(928, 992), wt.rows(992, 1056)
    v0, v1 = wt.rows(1056, 1120), wt.rows(1120, 1184)
    return jnp.concatenate(wt.rows(0, 384) + z32 + z32 + wt.rows(384, 416) + z32 + wt.rows(416, 928)
                           + k0 * 4 + k1 * 4 + v0 * 4 + v1 * 4 + wt.rows(1184, 2208), axis=0)


def _fold_w_in0(d):
    def fold(blk):
        b = blk.reshape(8, 64, blk.shape[1])
        return jnp.concatenate([b[0] + b[1] + b[2] + b[3], b[4] + b[5] + b[6] + b[7]], axis=0)
    return _RowSeq([d[0:384], d[448:480], d[512:1024], fold(d[1024:1536]), fold(d[1536:2048]), d[2048:3072]])


def _prep_w_q(wt):
    return jnp.pad(wt.reshape(N_MLA, 96, Q_RANK), ((0, 0), (0, 32), (0, 0))).reshape(1024, Q_RANK)


def _fold_w_q(d):
    return d.reshape(N_MLA, 128, Q_RANK)[:, :96].reshape(768, Q_RANK)


def _prep_w_kv(w):
    w3 = w.reshape(KV_RANK, N_MLA, 128)
    kk = jnp.pad(w3[:, :, :64], ((0, 0), (0, 0), (0, 64))).reshape(KV_RANK, 1024)
    return jnp.concatenate([kk, w3[:, :, 64:].reshape(KV_RANK, 512)], axis=1)


def _fold_w_kv(d):
    kk = d[:, :1024].reshape(KV_RANK, N_MLA, 128)[:, :, :64]
    vv = d[:, 1024:].reshape(KV_RANK, N_MLA, 64)
    return jnp.concatenate([kk, vv], axis=2).reshape(KV_RANK, 1024)


W_IN1_SHARD = 1028
W_IN1_STEP = W_IN1_SHARD % 16


class _ShiftedShards:
    def __init__(self, blocks):
        self.blocks = list(blocks)


def _shifted_shard(a, chip, rows):
    out = jnp.zeros((rows, a.shape[1]), a.dtype)
    for k in range(4):
        out = jnp.where(chip == k, jnp.pad(a, ((W_IN1_STEP * k, rows - W_IN1_STEP * k - a.shape[0]), (0, 0))), out)
    return out


def _prep_w_in1(wt):
    if not isinstance(wt, _ShiftedShards):
        return jnp.concatenate([wt[0:3072], wt[3088:4112], wt[3072:3088], jnp.zeros((112, wt.shape[1]), wt.dtype)], axis=0)
    b = wt.blocks
    row = lax.broadcasted_iota(jnp.int32, (16, 1), 0)

    def seam(k, first, second):
        return jnp.where(row < W_IN1_STEP * (k + 1), first, second)

    return jnp.concatenate([
        b[0][0:1024], seam(0, b[0][1024:1040], b[1][0:16]), b[1][16:1024], seam(1, b[1][1024:1040], b[2][0:16]),
        b[2][16:1024], b[3][16:1040], seam(2, b[2][1024:1040], b[3][0:16]), jnp.zeros((112, 1024), b[0].dtype)], axis=0)


def _fold_w_in1(d):
    return _RowSeq([d[0:3072], d[4096:4112], d[3072:4096]])


class _Alone:
    def __init__(self, w_out0, o_g_in, w_in1, w_out1):
        self.layer1 = (w_out0, o_g_in, w_in1, w_out1)

    def gather_plan(self):
        return None

    def layer1_weights(self, rode):
        return self.layer1

    def swap_plan(self, grads1):
        return None

    def exchange_plan(self, rode):
        return None

    def finish(self, rode):
        pass


def _local_step(x, pos, target, e_g_in, w_in0, e_g_q, w_q, e_g_kv, w_kv, sinks, b_f, g_final, layer1):
    S = x.shape[0]
    w_in0p, w_qp, w_kvp = _prep_w_in0(w_in0), _prep_w_q(w_q), _prep_w_kv(w_kv)
    slopes = jnp.asarray(2.0 ** (-8.0 * (np.arange(N_SWA, dtype=np.float32) + 1.0) / N_SWA), jnp.float32)
    sinks1 = sinks.reshape(N_SWA)
    b_col = b_f.reshape(N_FOX, 1)

    (h0, cq, ckv, qm, km, vm, qs, kd, vd, gate0, cos, sin) = _layer0_in(
        x, pos, e_g_in, w_in0p, e_g_q, w_qp, e_g_kv, w_kvp)
    o_m, lse_m, rode = _attn_fwd_t(qm, km, vm, (NOPE + ROPE) ** -0.5, split=True, name="mla_fwd", plan=layer1.gather_plan())
    w_out0, o_g_in, w_in1, w_out1 = layer1.layer1_weights(rode)
    w_in1p = _prep_w_in1(w_in1)
    o_s, lse_s = _swa_fwd(qs, kd, vd, sinks1, slopes)
    x1, h1, q1, k1, v1, gate1, f_slab = _layer0_out_layer1_in(x, o_m, o_s, gate0, w_out0, o_g_in, w_in1p)
    f_row = f_slab[:, :N_FOX].T
    lc_row = _forget_fwd(f_row, b_col)
    lcc = lc_row.T
    o1, lse1, _ = _attn_fwd_t(q1, k1, v1, HEAD ** -0.5, split=False, name="fox_fwd", lcc=lcc)
    loss8, dg_final, dw_out1, dx2, do1, dgate1 = _head(x1, o1, gate1, w_out1, g_final, target)

    dq1, dk1, dv1, dlc, _ = _attn_bwd_t(q1, k1, v1, do1, o1, lse1, HEAD ** -0.5, split=False, name="fox_bwd", lcc=lcc)
    df_row, db_f = _forget_bwd(dlc.reshape(N_FOX, S), f_row, b_col)
    df_slab = jnp.pad(df_row.T, ((0, 0), (0, LANES - N_FOX))).astype(MXU)
    dz1, dx1, dg_o_in, dw_out0, do_m, do_s, dgate0 = _layer1_in_bwd(
        dq1, dk1, dv1, dgate1, df_slab, x1, dx2, o_g_in, w_in1p, gate0, o_m, o_s, w_out0)
    grads1 = dict(o_g_in=dg_o_in, o_w_in=_fold_w_in1(_wgrad(dz1, h1, "wgrad_in1")), o_w_out=dw_out1, e_w_out=dw_out0)
    dqs, dkd, dvd, dsink, rode = _swa_bwd(qs, kd, vd, do_s, o_s, lse_s, sinks1, slopes, plan=layer1.swap_plan(grads1))
    dqm, dkm, dvm, rode = _attn_bwd_t(qm, km, vm, do_m, o_m, lse_m, (NOPE + ROPE) ** -0.5, split=True, name="mla_bwd",
                                      plan=layer1.exchange_plan(rode))
    layer1.finish(rode)
    dx, dz0, dg_in, dg_q, dg_kv, dw_q, dw_kv = _layer0_in_bwd(
        dqm, dkm, dvm, dqs, dkd, dvd, dgate0, cos, sin, cq, ckv, x, dx1, e_g_in, w_in0p, e_g_q, w_qp, e_g_kv, w_kvp)

    grads = dict(
        e_g_in=dg_in,
        e_w_in=_fold_w_in0(_wgrad(dz0, h0, "wgrad_in0")),
        e_g_q_a=dg_q,
        e_w_q_up=_fold_w_q(dw_q),
        e_g_kv_a=dg_kv,
        e_w_kv_up=_fold_w_kv(dw_kv),
        e_sinks=dsink[:, 0:4, 0].reshape(1, N_SWA),
        o_b_f=db_f.reshape(1, N_FOX),
        g_final=dg_final,
        **grads1,
    )
    return loss8[0, 0], dx, grads


SHARDED = ("e_w_in", "e_w_q_up", "e_w_kv_up", "e_w_out", "o_g_in", "o_w_in", "o_w_out")
TRANSPOSED = ("e_w_in", "e_w_q_up", "o_w_in")
COL_SHARDED = ("e_w_kv_up", "o_g_in")
REPLICATED = ("e_g_in", "e_g_q_a", "e_g_kv_a", "e_sinks", "o_b_f", "g_final")
FULL_SHAPES = dict(e_w_in=(2208, 1024), e_w_q_up=(768, 256), e_w_kv_up=(128, 1024), e_w_out=(1024, 1024),
                   o_g_in=(1, 1024), o_w_in=(4112, 1024), o_w_out=(1024, 1024))
GROUPS = dict(
    layer0=dict(rows=768, windows=dict(e_w_in=(0, 0), e_w_q_up=(560, 0), e_w_kv_up=(560, 256))),
    layer1=dict(rows=1568, windows=dict(o_w_in=(0, 0), o_w_out=(1040, 0), e_w_out=(1296, 0), o_g_in=(1552, 0))),
)


def _shard_shape(name):
    r, c = FULL_SHAPES[name]
    return (r, c // 4) if name in COL_SHARDED else (r // 4, c)


def _as_handled(name, a):
    a = a[0] if a.ndim == 3 else a
    return a.T if name in TRANSPOSED else a


def _as_given(name, a, shape):
    return (a.T if name in TRANSPOSED else a).reshape(shape)


def _pack_block(p, group, shifted_for=None):
    def rows(a, n):
        return jnp.pad(a, ((0, n - a.shape[0]), (0, 0)))

    if group == "layer0":
        band = jnp.concatenate([p["e_w_q_up"], rows(p["e_w_kv_up"], 192), jnp.zeros((192, 512), p["e_w_in"].dtype)], axis=1)
        return jnp.concatenate([rows(p["e_w_in"], 560), rows(band, 208)], axis=0)
    g = p["o_g_in"]
    band = jnp.pad(g, ((0, 16 - g.shape[0]), (0, PACK_COLS - g.shape[1])))
    w_in = rows(p["o_w_in"], 1040) if shifted_for is None else _shifted_shard(p["o_w_in"], shifted_for, 1040)
    return jnp.concatenate([w_in, p["o_w_out"], p["e_w_out"], band], axis=0)


def _window(block, group, name, width=None):
    r0, c0 = GROUPS[group]["windows"][name]
    r, c = _shard_shape(name)
    return block[..., r0:r0 + r, c0:c0 + (c if width is None else width)]


def _chip_slice(name, full, k):
    r, c = _shard_shape(name)
    if isinstance(full, _RowSeq):
        return jnp.concatenate(full.rows(r * k, r * (k + 1)), axis=0)
    return full[:, c * k:c * (k + 1)] if name in COL_SHARDED else full[r * k:r * (k + 1), :]


def _packed_weights(w, group):
    parts = {}
    for n in GROUPS[group]["windows"]:
        a = _as_handled(n, w[n])
        parts[n] = lax.bitcast_convert_type(a, jnp.bfloat16).reshape(1, -1) if n == "o_g_in" else a.astype(jnp.bfloat16)
    x, y, _ = _place()
    halves = _pack_block(parts, group, shifted_for=2 * x + y).reshape(2, GROUPS[group]["rows"] // 2, PACK_COLS)
    return lax.dynamic_index_in_dim(halves, lax.axis_index("c"), 0, keepdims=False)


def _unpacked_weights(gathered, half, group):
    blocks = _fill_own_slot(gathered, half).reshape(4, GROUPS[group]["rows"], PACK_COLS)
    full = {}
    for n in GROUPS[group]["windows"]:
        if n == "o_g_in":
            halves = _window(blocks, group, n, width=512).reshape(4, 1, 256, 2)
            full[n] = jnp.concatenate(list(lax.bitcast_convert_type(halves, jnp.float32)), axis=1)
        elif n == "o_w_in":
            full[n] = _ShiftedShards(blocks[k, 0:1040].astype(MXU) for k in range(4))
        else:
            pieces = [_window(blocks[k], group, n).astype(MXU) for k in range(4)]
            if n == "e_w_in":
                full[n] = _RowSeq(pieces)
            else:
                full[n] = jnp.concatenate(pieces, axis=1 if n in COL_SHARDED else 0)
    return full


class _GroupReduce:
    def __init__(self, group):
        self.group = group
        self.c = lax.axis_index("c")
        self.chip = 2 * lax.axis_index("x") + lax.axis_index("y")

    def swap_plan(self, grads):
        names = GROUPS[self.group]["windows"]
        per_chip = jnp.concatenate([_pack_block({n: _chip_slice(n, grads[n], k) for n in names}, self.group)
                                    for k in range(4)], axis=0)
        self.g4 = per_chip.reshape(4, 2, GROUPS[self.group]["rows"] // 2, PACK_COLS)
        return _pair_swap_plan(self.g4)

    def exchange_plan(self, rode):
        theirs = rode[0]
        rows = self.g4.shape[2]
        self.own = (lax.dynamic_slice(self.g4, (self.chip, self.c, 0, 0), (1, 1, rows, PACK_COLS)).reshape(rows, PACK_COLS),
                    lax.dynamic_index_in_dim(theirs, self.chip, 0, keepdims=False))
        return _chip_exchange_plan(_add_halves(self.g4, self.c, theirs, "pair_add_" + self.group, jnp.bfloat16))

    def finish(self, rode):
        my_half = _total_sum(*self.own, rode[0], "chip_sum_" + self.group)
        other_half = _run_plan(_pair_exchange_plan(my_half), "pair_exchange_" + self.group)[0]
        total = jnp.concatenate([jnp.where(self.c == 0, my_half, other_half), jnp.where(self.c == 0, other_half, my_half)], axis=0)
        self.sums = {n: _window(total, self.group, n) for n in GROUPS[self.group]["windows"]}

    def run(self, grads, beside):
        swap = self.swap_plan(grads)
        outs = _run_plan(_both_plans(swap, beside), "pair_swap_" + self.group)
        rode, others = outs[:len(swap.out_shape)], outs[len(swap.out_shape):]
        self.finish(_run_plan(self.exchange_plan(rode), "chip_exchange_" + self.group))
        return self.sums, others


class _Layer1Exchange(_GroupReduce):
    def __init__(self, w):
        super().__init__("layer1")
        self.half = _packed_weights(w, "layer1")

    def gather_plan(self):
        return _gather8_plan(self.half)

    def layer1_weights(self, rode):
        full = _unpacked_weights(rode[0], self.half, "layer1")
        return full["e_w_out"], full["o_g_in"], full["o_w_in"], full["o_w_out"]


def kernel(x, positions, e_g_in, e_w_in, e_g_q_a, e_w_q_up, e_g_kv_a, e_w_kv_up, e_sinks, e_w_out, o_g_in, o_w_in, o_b_f, o_w_out, g_final, loss_target, m_e_g_in, m_e_w_in, m_e_g_q_a, m_e_w_q_up, m_e_g_kv_a, m_e_w_kv_up, m_e_sinks, m_e_w_out, m_o_g_in, m_o_w_in, m_o_b_f, m_o_w_out, m_g_final, v_e_g_in, v_e_w_in, v_e_g_q_a, v_e_w_q_up, v_e_g_kv_a, v_e_w_kv_up, v_e_sinks, v_e_w_out, v_o_g_in, v_o_w_in, v_o_b_f, v_o_w_out, v_g_final):
    w = dict(e_g_in=e_g_in, e_w_in=e_w_in, e_g_q_a=e_g_q_a, e_w_q_up=e_w_q_up, e_g_kv_a=e_g_kv_a, e_w_kv_up=e_w_kv_up,
             e_sinks=e_sinks, e_w_out=e_w_out, o_g_in=o_g_in, o_w_in=o_w_in, o_b_f=o_b_f, o_w_out=o_w_out, g_final=g_final)
    m = dict(e_g_in=m_e_g_in, e_w_in=m_e_w_in, e_g_q_a=m_e_g_q_a, e_w_q_up=m_e_w_q_up, e_g_kv_a=m_e_g_kv_a,
             e_w_kv_up=m_e_w_kv_up, e_sinks=m_e_sinks, e_w_out=m_e_w_out, o_g_in=m_o_g_in, o_w_in=m_o_w_in, o_b_f=m_o_b_f,
             o_w_out=m_o_w_out, g_final=m_g_final)
    v = dict(e_g_in=v_e_g_in, e_w_in=v_e_w_in, e_g_q_a=v_e_g_q_a, e_w_q_up=v_e_w_q_up, e_g_kv_a=v_e_g_kv_a,
             e_w_kv_up=v_e_w_kv_up, e_sinks=v_e_sinks, e_w_out=v_e_w_out, o_g_in=v_o_g_in, o_w_in=v_o_w_in, o_b_f=v_o_b_f,
             o_w_out=v_o_w_out, g_final=v_g_final)
    order = ("e_g_in", "e_w_in", "e_g_q_a", "e_w_q_up", "e_g_kv_a", "e_w_kv_up", "e_sinks", "e_w_out", "o_g_in", "o_w_in",
             "o_b_f", "o_w_out", "g_final")
    half0 = _packed_weights(w, "layer0")
    full = _unpacked_weights(_run_plan(_gather8_plan(half0), "gather_weights_layer0")[0], half0, "layer0")
    layer1 = _Layer1Exchange(w)

    loss_part, dx, grads = _local_step(
        x[0], positions.reshape(-1, 1), loss_target[0], e_g_in, full["e_w_in"], e_g_q_a, full["e_w_q_up"], e_g_kv_a,
        full["e_w_kv_up"], e_sinks, o_b_f, g_final.reshape(1, D), layer1)

    small = jnp.concatenate([jnp.pad(loss_part.reshape(1), (0, LANES - 1))]
                            + [jnp.pad(grads[n].reshape(-1), (0, (-grads[n].size) % LANES)) for n in REPLICATED])
    rows = small.shape[0] // LANES
    small = jnp.pad(small.reshape(rows, LANES), ((0, (-rows) % 8), (0, 0)))
    sums0, (gathered_small,) = _GroupReduce("layer0").run(grads, _gather8_plan(small))
    gsum = {**layer1.sums, **sums0}
    ssum = _sum_leading(_fill_own_slot(gathered_small, small), "small_grad_sum").reshape(-1)
    loss = ssum[0]
    off = LANES
    for n in REPLICATED:
        cnt = w[n].size
        gsum[n] = ssum[off:off + cnt].reshape(w[n].shape)
        off += cnt + (-cnt) % LANES

    grad, delta, new_m, new_v = {}, {}, {}, {}
    for n in order:
        if n == "o_w_in":
            def tiles(a):
                return jnp.transpose(a, (2, 0, 1)).reshape(-1, LANES)

            def given(a):
                return jnp.transpose(a.reshape(-1, 8, LANES), (1, 2, 0)).reshape(w[n].shape)

            g_t = gsum[n].reshape(-1, LANES)
            outs = _adamw(tiles(w[n]), g_t, tiles(m[n]), tiles(v[n]), "adamw_" + n)
            grad[n], delta[n], new_m[n], new_v[n] = (given(a) for a in (g_t,) + outs)
        elif n in SHARDED:
            outs = _adamw(_as_handled(n, w[n]), gsum[n], _as_handled(n, m[n]), _as_handled(n, v[n]), "adamw_" + n)
            grad[n], delta[n], new_m[n], new_v[n] = (_as_given(n, a, w[n].shape) for a in (gsum[n],) + outs)
        else:
            grad[n] = gsum[n]
            delta[n], new_m[n], new_v[n] = _adamw(w[n], gsum[n], m[n], v[n], "adamw_" + n)
    return (loss, dx[None], *[grad[n] for n in order], *[delta[n] for n in order], *[new_m[n] for n in order],
            *[new_v[n] for n in order])
```

```python
import math

import numpy as np
import jax
import jax.numpy as jnp
from jax import lax
from jax.experimental import pallas as pl
from jax.experimental.pallas import tpu as pltpu

D = 1024
EPS = 1e-6
ROPE_THETA = 10000.0
N_MLA = 8
Q_RANK = 256
KV_RANK = 128
NOPE = 64
ROPE = 32
N_SWA = 8
WINDOW = 128
N_FOX = 16
HEAD = 64
LR, B1, B2, AEPS, WD, STEP = 0.001, 0.9, 0.999, 1e-08, 0.01, 10

LANES = 128
HALF = 64
VMEM_LIMIT = 56 * 1024 * 1024
MXU = jnp.bfloat16
TOK = 256
WG_TOK = 2048
WG_ROWS = 1536
ATT = 256
FWD_CHUNK = 2
BWD_CHUNK = 2
SWA_GROUP = 8
NEG = float("-inf")

PACK_COLS = 1024
SUM_ROWS = 256
ADAM_TILE_BYTES = 2 << 20
MESH_ID = pl.DeviceIdType.MESH


def _pcall(body, *, name, vmem=VMEM_LIMIT, semantics=None, **kw):
    params = dict(vmem_limit_bytes=vmem)
    if semantics is not None:
        params["dimension_semantics"] = semantics
    return pl.pallas_call(body, name=name, compiler_params=pltpu.CompilerParams(**params), **kw)


def _mm(a, b):
    return jnp.dot(a.astype(MXU), b.astype(MXU), preferred_element_type=jnp.float32)


def _mm_nt(a, b):
    return lax.dot_general(a.astype(MXU), b.astype(MXU), (((1,), (1,)), ((), ())),
                           preferred_element_type=jnp.float32)


def _mm_tn(a, b):
    return lax.dot_general(a.astype(MXU), b.astype(MXU), (((0,), (0,)), ((), ())),
                           preferred_element_type=jnp.float32)


def _full(shape):
    n = len(shape)
    return pl.BlockSpec(shape, lambda *_: (0,) * n)


def _rows(tm, n):
    return pl.BlockSpec((tm, n), lambda i: (i, 0))


def _sds(shape, dtype):
    return jax.ShapeDtypeStruct(shape, dtype)


def _rms(x, g):
    r = lax.rsqrt(jnp.mean(x * x, axis=-1, keepdims=True) + EPS)
    return x * r * g


def _rms_bwd(x, g, dy):
    r = lax.rsqrt(jnp.mean(x * x, axis=-1, keepdims=True) + EPS)
    xh = x * r
    dxh = dy * g
    dx = r * (dxh - xh * jnp.mean(dxh * xh, axis=-1, keepdims=True))
    return dx, dy * xh


def _sigmoid(x):
    return 1.0 / (1.0 + jnp.exp(-x))


def _lane_masks():
    lane = lax.broadcasted_iota(jnp.int32, (1, LANES), 1)
    return lane < HALF


def _split_heads(a, lo):
    z = jnp.zeros_like(a)
    return [jnp.where(lo, a, z), jnp.where(lo, z, a)]


def _rope_consts():
    inv = np.zeros((8, LANES), np.float32)
    j = np.arange(ROPE // 2, dtype=np.float32)
    f = (1.0 / (ROPE_THETA ** (np.arange(0, ROPE, 2, dtype=np.float32) / ROPE))).astype(np.float32)
    inv[0, HALF:HALF + 16] = f
    inv[0, HALF + 16:HALF + 32] = f
    inv[1, HALF:HALF + 16] = -1.0
    inv[1, HALF + 16:HALF + 32] = 1.0
    del j
    return jnp.asarray(inv)


def _rope_tables(pos_f, consts):
    ang = pos_f * consts[0:1, :]
    sign = consts[1:2, :]
    c = jnp.where(sign != 0.0, jnp.cos(ang), 1.0)
    s = jnp.sin(ang) * sign
    return c, s


def _swap_halves(v, sign):
    lo = pltpu.roll(v, LANES - 16, axis=1)
    hi = pltpu.roll(v, 16, axis=1)
    return jnp.where(sign < 0.0, lo, jnp.where(sign > 0.0, hi, 0.0))


def _rope(x, c, s, sign):
    return x * c + _swap_halves(x, sign) * s


def _rope_t(dy, c, s, sign):
    return dy * c + _swap_halves(dy * s, sign)


def _layer0_in(x, pos, g_in, w_in, g_q, w_q, g_kv, w_kv):
    S = x.shape[0]
    consts = _rope_consts()

    def body(x_ref, pos_ref, c_ref, g_ref, w_ref, gq_ref, wq_ref, gkv_ref, wkv_ref,
             h_ref, cq_ref, ckv_ref, qm_ref, km_ref, vm_ref,
             qs_ref, kd_ref, vd_ref, gate_ref, cos_ref, sin_ref):
        h = _rms(x_ref[...], g_ref[...])
        h_ref[...] = h.astype(h_ref.dtype)
        z = _mm_nt(h, w_ref[...])
        cq = z[:, 0:256]
        ckv = z[:, 256:384]
        kpe = z[:, 384:512]
        cq_ref[...] = cq
        ckv_ref[...] = ckv
        qs_ref[...] = z[:, 512:1024].astype(qs_ref.dtype)
        kd_ref[...] = z[:, 1024:1536].astype(kd_ref.dtype)
        vd_ref[...] = z[:, 1536:2048].astype(vd_ref.dtype)
        gate_ref[...] = z[:, 2048:3072]
        cqn = _rms(cq, gq_ref[...])
        ckvn = _rms(ckv, gkv_ref[...])
        q = _mm_nt(cqn, wq_ref[...])
        kv = _mm(ckvn, wkv_ref[...])
        vm_ref[...] = kv[:, 1024:1536].astype(vm_ref.dtype)
        consts_v = c_ref[...]
        sign = consts_v[1:2, :]
        c, s = _rope_tables(pos_ref[...].astype(jnp.float32), consts_v)
        cos_ref[...] = c
        sin_ref[...] = s
        kpe_r = _rope(kpe, c, s, sign)
        for hd in range(N_MLA):
            sl = slice(LANES * hd, LANES * (hd + 1))
            qm_ref[:, sl] = _rope(q[:, sl], c, s, sign).astype(qm_ref.dtype)
            km_ref[:, sl] = (kv[:, sl] + kpe_r).astype(km_ref.dtype)

    outs = [
        ((S, D), MXU), ((S, 256), jnp.float32), ((S, 128), jnp.float32),
        ((S, 1024), MXU), ((S, 1024), MXU), ((S, 512), MXU), ((S, 512), MXU), ((S, 512), MXU), ((S, 512), MXU),
        ((S, 1024), jnp.float32), ((S, 128), jnp.float32), ((S, 128), jnp.float32),
    ]
    return _pcall(
        body, name="layer0_in", grid=(S // TOK,), semantics=("arbitrary",),
        in_specs=[_rows(TOK, D), _rows(TOK, 1), _full((8, LANES)), _full((1, D)), _full(w_in.shape), _full((1, 256)),
                  _full(w_q.shape), _full((1, 128)), _full(w_kv.shape)],
        out_specs=[_rows(TOK, s[1]) for s, _ in outs],
        out_shape=[_sds(s, d) for s, d in outs],
    )(x, pos, consts, g_in, w_in, g_q, w_q, g_kv, w_kv)


AUG = (HALF, 0)
ONE = (HALF + 8, 8)


def _data_lanes(idx, h):
    return (idx < HALF) if h == 0 else (idx >= HALF)


def _three_terms(x):
    hi = x.astype(MXU).astype(jnp.float32)
    mid = (x - hi).astype(MXU).astype(jnp.float32)
    lo = (x - hi - mid).astype(MXU).astype(jnp.float32)
    return hi, mid, lo


def _q_aug(qblk, lc, h, scale, lane):
    a = AUG[h]
    hi, mid, lo = _three_terms(lc)
    ones = ((lane >= a + 3) & (lane <= a + 5)).astype(jnp.float32)
    aug = jnp.where(lane == a, hi, jnp.where(lane == a + 1, mid, jnp.where(lane == a + 2, lo, ones)))
    return jnp.where(_data_lanes(lane, h), qblk * jnp.asarray(scale, qblk.dtype), aug.astype(qblk.dtype))


def _k_aug(kblk, lc, h, lane):
    a = AUG[h]
    hi, mid, lo = _three_terms(-lc)
    ones = ((lane >= a) & (lane <= a + 2)).astype(jnp.float32)
    aug = jnp.where(lane == a + 3, hi, jnp.where(lane == a + 4, mid, jnp.where(lane == a + 5, lo, ones)))
    return jnp.where(_data_lanes(lane, h), kblk, aug.astype(kblk.dtype))


def _lc_col(lc_ref, r0, rows, h):
    head = lax.broadcasted_iota(jnp.int32, (1, lc_ref.shape[1]), 1)
    return jnp.sum(jnp.where(head == 2 * pl.program_id(0) + h, lc_ref[pl.ds(r0, rows), :], 0.0), axis=1, keepdims=True)


def _attn_fwd_t(q, k, v, scale, *, split, name, lcc=None, plan=None):
    S = q.shape[0]
    npair = v.shape[1] // LANES
    W = 2 * LANES if split else LANES
    T = ATT
    CH = FWD_CHUNK * T
    assert S % CH == 0
    nq = S // T

    def body(*refs):
        if split:
            q_ref, k_ref, v_ref, o_ref, lse_ref, vt, acc, m_sc = refs
        else:
            q_ref, k_ref, v_ref, lcc_ref, o_ref, lse_ref, kaug, vt, acc, m_sc = refs
        lane = lax.broadcasted_iota(jnp.int32, (1, LANES), 1)
        sub = lax.broadcasted_iota(jnp.int32, (LANES, 1), 0)
        key_minus_qry = lax.broadcasted_iota(jnp.int32, (CH, T), 0) - lax.broadcasted_iota(jnp.int32, (CH, T), 1)

        def prep(i, c):
            r0 = pl.multiple_of(i * T, T)
            vblk = v_ref[pl.ds(r0, T), :].astype(jnp.float32)
            for h in (0, 1):
                vh = jnp.where(_data_lanes(lane, h), vblk, (lane == ONE[h]).astype(jnp.float32))
                vt[h, :, pl.ds(r0, T)] = vh.T.astype(vt.dtype)
                if not split:
                    kaug[h, pl.ds(r0, T), :] = _k_aug(k_ref[pl.ds(r0, T), :], _lc_col(lcc_ref, r0, T, h), h, lane)
            return c

        lax.fori_loop(0, nq, prep, 0)

        def queries(qi):
            q0 = pl.multiple_of(qi * T, T)
            qblk = q_ref[pl.ds(q0, T), :]
            if split:
                return (qblk[:, :LANES], qblk[:, LANES:])
            return tuple(_q_aug(qblk, _lc_col(lcc_ref, q0, T, h), h, scale, lane) for h in (0, 1))

        def scores(qs, c):
            k0 = pl.multiple_of(c * CH, CH)
            out = []
            for h in (0, 1):
                if split:
                    out.append(_mm_nt(k_ref[pl.ds(k0, CH), LANES * h:LANES * (h + 1)], qs[h]) * scale)
                else:
                    out.append(_mm_nt(kaug[h, pl.ds(k0, CH), :], qs[h]))
            return tuple(out)

        def q_block(qi, carry):
            qs, first_scores = carry[:2], carry[2:]
            q0 = pl.multiple_of(qi * T, T)
            acc[...] = jnp.zeros_like(acc)
            m_sc[...] = jnp.full(m_sc.shape, NEG, jnp.float32)

            def absorb(c, sts, masked):
                k0 = pl.multiple_of(c * CH, CH)
                for h in (0, 1):
                    st = sts[h]
                    if masked:
                        st = jnp.where(key_minus_qry <= q0 - k0, st, NEG)
                    m_old = m_sc[h:h + 1, :]
                    m_new = jnp.maximum(m_old, jnp.max(st, axis=0, keepdims=True))
                    alpha = jnp.exp(m_old - m_new)
                    pt = jnp.exp(st - m_new)
                    acc[h] = alpha * acc[h] + _mm(vt[h, :, pl.ds(k0, CH)], pt)
                    m_sc[h:h + 1, :] = m_new

            last = qi // FWD_CHUNK

            def pipelined(c, sts):
                nxt = scores(qs, c + 1)
                absorb(c, sts, False)
                return nxt

            sts = lax.fori_loop(0, last, pipelined, first_scores)
            qs_next = queries(jnp.minimum(qi + 1, nq - 1))
            nxt = qs_next + scores(qs_next, 0)
            absorb(last, sts, True)
            ot = None
            for h in (0, 1):
                a = acc[h]
                l = a[ONE[h]:ONE[h] + 1, :]
                oh = jnp.where(_data_lanes(sub, h), a * (1.0 / l), 0.0)
                ot = oh if ot is None else ot + oh
                lse_ref[0, h:h + 1, pl.ds(q0, T)] = m_sc[h:h + 1, :] + jnp.log(l)
            o_ref[pl.ds(q0, T), :] = ot.T
            return nxt

        qs0 = queries(0)
        lax.fori_loop(0, nq, q_block, qs0 + scores(qs0, 0))

    wide = pl.BlockSpec((S, W), lambda j: (0, j))
    slab = pl.BlockSpec((S, LANES), lambda j: (0, j))
    rows = pl.BlockSpec((1, 2, S), lambda j: (j, 0, 0))
    in_specs = [wide, wide, slab]
    args = [q, k, v]
    scratch = []
    if not split:
        in_specs.append(_full(lcc.shape))
        args.append(lcc)
        scratch.append(pltpu.VMEM((2, S, LANES), MXU))
    scratch += [pltpu.VMEM((2, LANES, S), MXU), pltpu.VMEM((2, LANES, T), jnp.float32), pltpu.VMEM((8, T), jnp.float32)]
    (o, lse), rode = _pcall_riding(
        body, plan, args, name=name, grid=(npair,), in_specs=in_specs, out_specs=[slab, rows],
        out_shape=[_sds((S, npair * LANES), jnp.float32), _sds((npair, 2, S), jnp.float32)], scratch_shapes=scratch)
    return o, lse, rode


def _attn_bwd_t(q, k, v, do, o, lse, scale, *, split, name, lcc=None, plan=None):
    S = q.shape[0]
    npair = v.shape[1] // LANES
    W = 2 * LANES if split else LANES
    T = ATT
    CH = BWD_CHUNK * T
    assert S % CH == 0
    nq = S // T

    def body(*refs):
        if split:
            (q_ref, k_ref, v_ref, do_ref, o_ref, lse_ref, dq_ref, dk_ref, dv_ref, dqt, delta, dk_acc, dv_acc) = refs
        else:
            (q_ref, k_ref, v_ref, do_ref, o_ref, lse_ref, lcc_ref, dq_ref, dk_ref, dv_ref, dlc_ref,
             dqt, delta, dk_acc, dv_acc, qaug, csum) = refs
        lane = lax.broadcasted_iota(jnp.int32, (1, LANES), 1)
        sub = lax.broadcasted_iota(jnp.int32, (LANES, 1), 0)
        key_minus_qry = lax.broadcasted_iota(jnp.int32, (T, CH), 0) - lax.broadcasted_iota(jnp.int32, (T, CH), 1)

        def prep(i, c):
            r0 = pl.multiple_of(i * T, T)
            prod_t = (do_ref[pl.ds(r0, T), :].astype(jnp.float32) * o_ref[pl.ds(r0, T), :]).T
            for h in (0, 1):
                delta[h:h + 1, pl.ds(r0, T)] = jnp.sum(jnp.where(_data_lanes(sub, h), prod_t, 0.0), axis=0, keepdims=True)
                dqt[h, :, pl.ds(r0, T)] = jnp.zeros((LANES, T), jnp.float32)
                if not split:
                    qaug[h, pl.ds(r0, T), :] = _q_aug(q_ref[pl.ds(r0, T), :], _lc_col(lcc_ref, r0, T, h), h, scale, lane)
            return c

        lax.fori_loop(0, nq, prep, 0)

        def keys(ki):
            k0 = pl.multiple_of(ki * T, T)
            kblk = k_ref[pl.ds(k0, T), :]
            if split:
                return (kblk[:, :LANES], kblk[:, LANES:])
            return tuple(_k_aug(kblk, _lc_col(lcc_ref, k0, T, h), h, lane) for h in (0, 1))

        def q_of(c, h):
            q0 = pl.multiple_of(c * CH, CH)
            if split:
                return q_ref[pl.ds(q0, CH), LANES * h:LANES * (h + 1)]
            return qaug[h, pl.ds(q0, CH), :]

        def scores(khs, c):
            out = []
            for h in (0, 1):
                st = _mm_nt(khs[h], q_of(c, h))
                out.append(st * scale if split else st)
            return tuple(out)

        def k_block(ki, carry):
            khs, first_scores = carry[:2], carry[2:]
            k0 = pl.multiple_of(ki * T, T)
            khts = [kh.astype(jnp.float32).T.astype(kh.dtype) for kh in khs]
            vhs = _split_heads(v_ref[pl.ds(k0, T), :], lane < HALF)
            dk_acc[...] = jnp.zeros_like(dk_acc)
            dv_acc[...] = jnp.zeros_like(dv_acc)

            def absorb(c, vals):
                q0 = pl.multiple_of(c * CH, CH)
                dos = _split_heads(do_ref[pl.ds(q0, CH), :], lane < HALF)
                visible = key_minus_qry <= q0 - k0
                for h in (0, 1):
                    dpt = _mm_nt(vhs[h], dos[h])
                    st = jnp.where(visible, vals[h], NEG)
                    pt = jnp.exp(st - lse_ref[0, h:h + 1, pl.ds(q0, CH)])
                    dv_acc[...] += _mm(pt, dos[h])
                    dst = pt * (dpt - delta[h:h + 1, pl.ds(q0, CH)])
                    dk_acc[h] += _mm(dst, q_of(c, h))
                    dqt[h, :, pl.ds(q0, CH)] += _mm(khts[h], dst)

            first = ki // BWD_CHUNK

            def pipelined(c, vals):
                nxt = scores(khs, c + 1)
                absorb(c, vals)
                return nxt

            vals = lax.fori_loop(first, S // CH - 1, pipelined, first_scores)
            kn = jnp.minimum(ki + 1, nq - 1)
            khs_next = keys(kn)
            nxt = khs_next + scores(khs_next, kn // BWD_CHUNK)
            absorb(S // CH - 1, vals)
            if split:
                dk_ref[pl.ds(k0, T), :LANES] = (dk_acc[0] * scale).astype(dk_ref.dtype)
                dk_ref[pl.ds(k0, T), LANES:] = (dk_acc[1] * scale).astype(dk_ref.dtype)
            else:
                dk_ref[pl.ds(k0, T), :] = jnp.where(lane < HALF, dk_acc[0], dk_acc[1]).astype(dk_ref.dtype)
                for h in (0, 1):
                    csum[h:h + 1, pl.ds(k0, T)] = dk_acc[h].T[AUG[h] + 3:AUG[h] + 4, :]
            dv_ref[pl.ds(k0, T), :] = dv_acc[...].astype(dv_ref.dtype)
            return nxt

        khs0 = keys(0)
        lax.fori_loop(0, nq, k_block, khs0 + scores(khs0, 0))

        def finish(i, c):
            r0 = pl.multiple_of(i * T, T)
            if split:
                for h in (0, 1):
                    dq_ref[pl.ds(r0, T), LANES * h:LANES * (h + 1)] = (dqt[h, :, pl.ds(r0, T)].T * scale).astype(dq_ref.dtype)
            else:
                d = jnp.where(sub < HALF, dqt[0, :, pl.ds(r0, T)], dqt[1, :, pl.ds(r0, T)])
                dq_ref[pl.ds(r0, T), :] = (d.T * scale).astype(dq_ref.dtype)
                for h in (0, 1):
                    dlc_ref[0, h:h + 1, pl.ds(r0, T)] = dqt[h, AUG[h]:AUG[h] + 1, pl.ds(r0, T)] - csum[h:h + 1, pl.ds(r0, T)]
            return c

        lax.fori_loop(0, nq, finish, 0)

    wide = pl.BlockSpec((S, W), lambda j: (0, j))
    slab = pl.BlockSpec((S, LANES), lambda j: (0, j))
    rows = pl.BlockSpec((1, 2, S), lambda j: (j, 0, 0))
    in_specs = [wide, wide, slab, slab, slab, rows]
    args = [q, k, v, do, o, lse]
    out_specs = [wide, wide, slab]
    out_shape = [_sds(q.shape, jnp.float32 if split else do.dtype), _sds(k.shape, jnp.float32 if split else do.dtype),
                 _sds(v.shape, do.dtype)]
    scratch = [pltpu.VMEM((2, LANES, S), jnp.float32), pltpu.VMEM((8, S), jnp.float32),
               pltpu.VMEM((2, T, LANES), jnp.float32), pltpu.VMEM((T, LANES), jnp.float32)]
    if not split:
        in_specs.append(_full(lcc.shape))
        args.append(lcc)
        out_specs.append(rows)
        out_shape.append(_sds((npair, 2, S), jnp.float32))
        scratch += [pltpu.VMEM((2, S, LANES), MXU), pltpu.VMEM((8, S), jnp.float32)]
    outs, rode = _pcall_riding(body, plan, args, name=name, grid=(npair,), in_specs=in_specs, out_specs=out_specs,
                               out_shape=out_shape, scratch_shapes=scratch)
    return (*outs, rode)


def _swa_bias(slope, shift):
    a = lax.broadcasted_iota(jnp.int32, (WINDOW, 2 * WINDOW), 0)
    c = lax.broadcasted_iota(jnp.int32, (WINDOW, 2 * WINDOW), 1)
    dist = a - c + shift
    return jnp.where((dist >= 0) & (dist < WINDOW), -slope * dist.astype(jnp.float32), NEG)


def _swa_scores(qh, kblk, bias):
    return _mm_nt(qh, kblk) * (HEAD ** -0.5) + bias


def _swa_stack(blk, lo):
    return jnp.concatenate(_split_heads(blk[:, :LANES], lo) + _split_heads(blk[:, LANES:], lo), axis=0)


def _swa_unstack(x, lo):
    r = x.shape[0] // 4
    return jnp.concatenate([jnp.where(lo, x[0:r], x[r:2 * r]), jnp.where(lo, x[2 * r:3 * r], x[3 * r:])], axis=1)


def _swa_per_head(ref, j, rows):
    quarter = lax.broadcasted_iota(jnp.int32, (4 * rows, 1), 0) // rows
    return jnp.where(quarter == 0, ref[4 * j], jnp.where(quarter == 1, ref[4 * j + 1],
                                                         jnp.where(quarter == 2, ref[4 * j + 2], ref[4 * j + 3])))


def _swa_fwd(q, kd, vd, sinks, slopes):
    S = q.shape[0]
    nkv = q.shape[1] // (2 * LANES)
    nb = S // WINDOW
    group = math.gcd(SWA_GROUP, nb)

    def body(sink_ref, slope_ref, q_ref, k_ref, v_ref, o_ref, lse_ref):
        j = pl.program_id(0)
        lo = _lane_masks()
        sink = _swa_per_head(sink_ref, j, WINDOW)
        biases = [jnp.concatenate([_swa_bias(slope_ref[4 * j + h], shift) for h in range(4)], axis=0)
                  for shift in (0, WINDOW)]

        def q_block(qi, c):
            q0 = pl.multiple_of(qi * WINDOW, WINDOW)
            k0 = pl.multiple_of(jnp.maximum(qi - 1, 0) * WINDOW, WINDOW)
            s = _swa_scores(_swa_stack(q_ref[pl.ds(q0, WINDOW), :], lo), k_ref[pl.ds(k0, 2 * WINDOW), :],
                            jnp.where(qi == 0, *biases))
            m = jnp.maximum(jnp.max(s, axis=1, keepdims=True), sink)
            p = jnp.exp(s - m)
            den = jnp.sum(p, axis=1, keepdims=True) + jnp.exp(sink - m)
            o_ref[pl.ds(q0, WINDOW), :] = _swa_unstack(_mm(p / den, v_ref[pl.ds(k0, 2 * WINDOW), :]), lo)
            lse = m + jnp.log(den)
            for h in range(4):
                lse_ref[h, pl.ds(q0, WINDOW), :] = lse[h * WINDOW:(h + 1) * WINDOW]
            return c

        def q_group(gi, c):
            for g in range(group):
                q_block(gi * group + g, c)
            return c

        lax.fori_loop(0, nb // group, q_group, 0)

    smem = pl.BlockSpec(memory_space=pltpu.SMEM)
    two = pl.BlockSpec((S, 2 * LANES), lambda j: (0, j))
    kv = pl.BlockSpec((S, LANES), lambda j: (0, 2 * j))
    return _pcall(
        body, name="swa_fwd", grid=(nkv,), semantics=("arbitrary",),
        in_specs=[smem, smem, two, kv, kv],
        out_specs=[two, pl.BlockSpec((4, S, 1), lambda j: (j, 0, 0))],
        out_shape=[_sds(q.shape, jnp.float32), _sds((4 * nkv, S, 1), jnp.float32)],
    )(sinks, slopes, q, kd, vd)


def _swa_bwd(q, kd, vd, do, o, lse, sinks, slopes, plan=None):
    S = q.shape[0]
    nkv = q.shape[1] // (2 * LANES)
    nb = S // WINDOW
    group = math.gcd(SWA_GROUP, nb)

    def body(sink_ref, slope_ref, q_ref, k_ref, v_ref, do_ref, o_ref, lse_ref,
             dq_ref, dk_ref, dv_ref, dsink_ref, dk_acc, dv_acc):
        j = pl.program_id(0)
        lo = _lane_masks()
        dk_acc[...] = jnp.zeros_like(dk_acc)
        dv_acc[...] = jnp.zeros_like(dv_acc)
        sink = _swa_per_head(sink_ref, j, WINDOW)
        biases = [jnp.concatenate([_swa_bias(slope_ref[4 * j + h], shift) for h in range(4)], axis=0)
                  for shift in (0, WINDOW)]

        def q_block(qi, carry):
            q0 = pl.multiple_of(qi * WINDOW, WINDOW)
            k0 = pl.multiple_of(jnp.maximum(qi - 1, 0) * WINDOW, WINDOW)
            q4 = _swa_stack(q_ref[pl.ds(q0, WINDOW), :], lo)
            do4 = _swa_stack(do_ref[pl.ds(q0, WINDOW), :], lo)
            oblk = o_ref[pl.ds(q0, WINDOW), :]
            o4 = jnp.concatenate([oblk[:, :LANES], oblk[:, :LANES], oblk[:, LANES:], oblk[:, LANES:]], axis=0)
            kblk = k_ref[pl.ds(k0, 2 * WINDOW), :]
            vblk = v_ref[pl.ds(k0, 2 * WINDOW), :]
            lse = jnp.concatenate([lse_ref[h, pl.ds(q0, WINDOW), :] for h in range(4)], axis=0)
            p = jnp.exp(_swa_scores(q4, kblk, jnp.where(qi == 0, *biases)) - lse)
            delta = jnp.sum(do4.astype(jnp.float32) * o4, axis=1, keepdims=True)
            dv_acc[pl.ds(k0, 2 * WINDOW), :] += _mm_tn(p, do4)
            ds = p * (_mm_nt(do4, vblk) - delta)
            dq_ref[pl.ds(q0, WINDOW), :] = _swa_unstack(_mm(ds, kblk) * (HEAD ** -0.5), lo).astype(dq_ref.dtype)
            dk_acc[pl.ds(k0, 2 * WINDOW), :] += _mm_tn(ds, q4) * (HEAD ** -0.5)
            dsk = -jnp.exp(sink - lse) * delta
            return tuple(carry[h] + jnp.sum(dsk[h * WINDOW:(h + 1) * WINDOW], axis=0, keepdims=True)
                         for h in range(4))

        def q_group(gi, carry):
            for g in range(group):
                carry = q_block(gi * group + g, carry)
            return carry

        zero = jnp.zeros((1, 1), jnp.float32)
        dsinks = lax.fori_loop(0, nb // group, q_group, (zero,) * 4)
        dk_ref[:, :LANES] = dk_acc[...].astype(dk_ref.dtype)
        dk_ref[:, LANES:] = jnp.zeros((S, LANES), dk_ref.dtype)
        dv_ref[:, :LANES] = dv_acc[...].astype(dv_ref.dtype)
        dv_ref[:, LANES:] = jnp.zeros((S, LANES), dv_ref.dtype)
        r = lax.broadcasted_iota(jnp.int32, (8, LANES), 0)
        dsink_ref[0] = jnp.where(r == 0, dsinks[0], jnp.where(r == 1, dsinks[1], jnp.where(r == 2, dsinks[2],
                                 jnp.where(r == 3, dsinks[3], 0.0))))

    smem = pl.BlockSpec(memory_space=pltpu.SMEM)
    two = pl.BlockSpec((S, 2 * LANES), lambda j: (0, j))
    kv = pl.BlockSpec((S, LANES), lambda j: (0, 2 * j))
    outs, rode = _pcall_riding(
        body, plan, [sinks, slopes, q, kd, vd, do, o, lse], name="swa_bwd", grid=(nkv,),
        in_specs=[smem, smem, two, kv, kv, two, two, pl.BlockSpec((4, S, 1), lambda j: (j, 0, 0))],
        out_specs=[two, two, two, pl.BlockSpec((1, 8, LANES), lambda j: (j, 0, 0))],
        out_shape=[_sds(q.shape, do.dtype), _sds(kd.shape, do.dtype), _sds(vd.shape, do.dtype),
                   _sds((nkv, 8, LANES), jnp.float32)],
        scratch_shapes=[pltpu.VMEM((S, LANES), jnp.float32), pltpu.VMEM((S, LANES), jnp.float32)])
    return (*outs, rode)


def _log_steps(S):
    k, out = 1, []
    while k < S:
        out.append(k)
        k *= 2
    return out


def _forget_fwd(f_row, b_col):
    S = f_row.shape[1]

    def body(f_ref, b_ref, lc_ref):
        x = f_ref[...] + b_ref[...]
        lc = jnp.minimum(x, 0.0) - jnp.log(1.0 + jnp.exp(-jnp.abs(x)))
        idx = lax.broadcasted_iota(jnp.int32, lc.shape, 1)
        for k in _log_steps(S):
            lc = lc + jnp.where(idx >= k, pltpu.roll(lc, k, axis=1), 0.0)
        lc_ref[...] = lc

    return _pcall(body, name="forget_fwd", out_shape=_sds(f_row.shape, jnp.float32))(f_row, b_col)


def _forget_bwd(dlc_row, f_row, b_col):
    S = f_row.shape[1]

    def body(d_ref, f_ref, b_ref, df_ref, db_ref):
        g = d_ref[...]
        idx = lax.broadcasted_iota(jnp.int32, g.shape, 1)
        for k in _log_steps(S):
            g = g + jnp.where(idx < S - k, pltpu.roll(g, S - k, axis=1), 0.0)
        x = f_ref[...] + b_ref[...]
        df = g * _sigmoid(-x)
        df_ref[...] = df
        db_ref[...] = jnp.sum(df, axis=1, keepdims=True)

    return _pcall(body, name="forget_bwd",
                  out_shape=[_sds(f_row.shape, jnp.float32), _sds((f_row.shape[0], 1), jnp.float32)])(dlc_row, f_row, b_col)


def _layer0_out_layer1_in(x, o_m, o_s, gate, w_out, g1, w_in1):
    S = x.shape[0]

    def body(x_ref, om_ref, os_ref, gate_ref, wo_ref, g_ref, w_ref,
             x1_ref, h_ref, q_ref, k_ref, v_ref, g1_ref, f_ref):
        gt = gate_ref[...]
        sg = gt * _sigmoid(gt)
        um = om_ref[...] * sg[:, :512]
        us = os_ref[...] * sg[:, 512:]
        x1 = x_ref[...] + _mm(um, wo_ref[0:512, :]) + _mm(us, wo_ref[512:1024, :])
        x1_ref[...] = x1
        h = _rms(x1, g_ref[...])
        h_ref[...] = h.astype(h_ref.dtype)
        z = _mm_nt(h, w_ref[...])
        q_ref[...] = z[:, 0:1024].astype(q_ref.dtype)
        k_ref[...] = z[:, 1024:2048].astype(k_ref.dtype)
        v_ref[...] = z[:, 2048:3072].astype(v_ref.dtype)
        g1_ref[...] = z[:, 3072:4096]
        f_ref[...] = z[:, 4096:4224]

    outs = [((S, D), jnp.float32), ((S, D), MXU), ((S, D), MXU), ((S, D), MXU), ((S, D), MXU),
            ((S, D), jnp.float32), ((S, LANES), jnp.float32)]
    return _pcall(
        body, name="layer0_out_layer1_in", grid=(S // TOK,), semantics=("arbitrary",),
        in_specs=[_rows(TOK, D), _rows(TOK, 512), _rows(TOK, 512), _rows(TOK, D), _full((D, D)), _full((1, D)),
                  _full(w_in1.shape)],
        out_specs=[_rows(TOK, s[1]) for s, _ in outs],
        out_shape=[_sds(s, d) for s, d in outs],
    )(x, o_m, o_s, gate, w_out, g1, w_in1)


def _head(x1, o1, gate1, w_out1, g_f, target):
    S = x1.shape[0]

    def body(x1_ref, o_ref, gate_ref, wo_ref, g_ref, t_ref,
             loss_ref, dgf_ref, dwo_ref, dx2_ref, do_ref, dgate_ref):
        i = pl.program_id(0)
        gt = gate_ref[...]
        sig = _sigmoid(gt)
        sg = gt * sig
        o = o_ref[...]
        u = o * sg
        x2 = x1_ref[...] + _mm(u, wo_ref[...])
        g = g_ref[...]
        y = _rms(x2, g)
        err = y - t_ref[...]
        part = 0.5 * jnp.sum(jnp.mean(err * err, axis=-1, keepdims=True), axis=0, keepdims=True)
        dy = err * (1.0 / D)
        dx2, dg_rows = _rms_bwd(x2, g, dy)
        dx2_ref[...] = dx2
        du = _mm_nt(dx2, wo_ref[...])
        do_ref[...] = (du * sg).astype(do_ref.dtype)
        dgate_ref[...] = (du * o * (sig * (1.0 + gt * (1.0 - sig)))).astype(dgate_ref.dtype)

        @pl.when(i == 0)
        def _():
            loss_ref[...] = jnp.zeros_like(loss_ref)
            dgf_ref[...] = jnp.zeros_like(dgf_ref)
            dwo_ref[...] = jnp.zeros_like(dwo_ref)

        loss_ref[...] += jnp.broadcast_to(part, loss_ref.shape)
        dgf_ref[...] += jnp.sum(dg_rows, axis=0, keepdims=True)
        dwo_ref[...] += _mm_tn(u, dx2)

    outs = [((S, D), jnp.float32), ((S, D), MXU), ((S, D), MXU)]
    return _pcall(
        body, name="head", grid=(S // TOK,), semantics=("arbitrary",),
        in_specs=[_rows(TOK, D), _rows(TOK, D), _rows(TOK, D), _full((D, D)), _full((1, D)), _rows(TOK, D)],
        out_specs=[_full((8, LANES)), _full((1, D)), _full((D, D))] + [_rows(TOK, D) for _ in outs],
        out_shape=[_sds((8, LANES), jnp.float32), _sds((1, D), jnp.float32), _sds((D, D), jnp.float32)]
        + [_sds(s, d) for s, d in outs],
    )(x1, o1, gate1, w_out1, g_f, target)


def _layer1_in_bwd(dq, dk, dv, dgate1, df, x1, dx2, g1, w_in1, gate0, o_m, o_s, w_out0):
    S = x1.shape[0]

    def body(dq_ref, dk_ref, dv_ref, dg1_ref, df_ref, x1_ref, dx2_ref, g_ref, w_ref, gate_ref, om_ref, os_ref,
             wo_ref, dz_ref, dx1_ref, dgn_ref, dwo_ref, dom_ref, dos_ref, dgate_ref):
        i = pl.program_id(0)
        dz_ref[:, 0:1024] = dq_ref[...]
        dz_ref[:, 1024:2048] = dk_ref[...]
        dz_ref[:, 2048:3072] = dv_ref[...]
        dz_ref[:, 3072:4096] = dg1_ref[...]
        dz_ref[:, 4096:4224] = df_ref[...]
        dh = _mm(dz_ref[...], w_ref[...])
        g = g_ref[...]
        dxn, dg_rows = _rms_bwd(x1_ref[...], g, dh)
        dx1 = dx2_ref[...] + dxn
        dx1_ref[...] = dx1
        du = _mm_nt(dx1, wo_ref[...])
        gt = gate_ref[...]
        sig = _sigmoid(gt)
        sg = gt * sig
        dsg = sig * (1.0 + gt * (1.0 - sig))
        dom_ref[...] = (du[:, :512] * sg[:, :512]).astype(dom_ref.dtype)
        dos_ref[...] = (du[:, 512:] * sg[:, 512:]).astype(dos_ref.dtype)
        dgate_ref[:, :512] = (du[:, :512] * om_ref[...] * dsg[:, :512]).astype(dgate_ref.dtype)
        dgate_ref[:, 512:] = (du[:, 512:] * os_ref[...] * dsg[:, 512:]).astype(dgate_ref.dtype)

        @pl.when(i == 0)
        def _():
            dgn_ref[...] = jnp.zeros_like(dgn_ref)
            dwo_ref[...] = jnp.zeros_like(dwo_ref)

        dgn_ref[...] += jnp.sum(dg_rows, axis=0, keepdims=True)
        dwo_ref[0:512, :] += _mm_tn(om_ref[...] * sg[:, :512], dx1)
        dwo_ref[512:1024, :] += _mm_tn(os_ref[...] * sg[:, 512:], dx1)

    return _pcall(
        body, name="layer1_in_bwd", grid=(S // TOK,), semantics=("arbitrary",),
        in_specs=[_rows(TOK, D), _rows(TOK, D), _rows(TOK, D), _rows(TOK, D), _rows(TOK, LANES), _rows(TOK, D),
                  _rows(TOK, D), _full((1, D)), _full(w_in1.shape), _rows(TOK, D), _rows(TOK, 512), _rows(TOK, 512),
                  _full((D, D))],
        out_specs=[_rows(TOK, 4224), _rows(TOK, D), _full((1, D)), _full((D, D)), _rows(TOK, 512), _rows(TOK, 512),
                   _rows(TOK, D)],
        out_shape=[_sds((S, 4224), MXU), _sds((S, D), jnp.float32), _sds((1, D), jnp.float32), _sds((D, D), jnp.float32),
                   _sds((S, 512), MXU), _sds((S, 512), MXU), _sds((S, D), MXU)],
    )(dq, dk, dv, dgate1, df, x1, dx2, g1, w_in1, gate0, o_m, o_s, w_out0)


def _layer0_in_bwd(dqm, dkm, dvm, dqs, dkd, dvd, dgate0, cos, sin, cq, ckv, x, dx1, g_in, w_in, g_q, w_q, g_kv, w_kv):
    S = x.shape[0]
    consts = _rope_consts()

    def body(dqm_ref, dkm_ref, dvm_ref, dqs_ref, dkd_ref, dvd_ref, dgate_ref, cos_ref, sin_ref, c_ref, cq_ref, ckv_ref,
             x_ref, dx1_ref, g_ref, w_ref, gq_ref, wq_ref, gkv_ref, wkv_ref,
             dx_ref, dz_ref, dgin_ref, dgq_ref, dgkv_ref, dwq_ref, dwkv_ref, dqu_ref, dkvu_ref):
        i = pl.program_id(0)
        lo = _lane_masks()
        sign = c_ref[...][1:2, :]
        c = cos_ref[...]
        s = sin_ref[...]
        dkpe = None
        for hd in range(N_MLA):
            sl = slice(LANES * hd, LANES * (hd + 1))
            dqu_ref[:, sl] = _rope_t(dqm_ref[:, sl], c, s, sign).astype(dqu_ref.dtype)
            dkh = dkm_ref[:, sl]
            dkvu_ref[:, sl] = jnp.where(lo, dkh, 0.0).astype(dkvu_ref.dtype)
            dkpe = dkh if dkpe is None else dkpe + dkh
        dkvu_ref[:, 1024:1536] = dvm_ref[...]
        dkpe = _rope_t(jnp.where(lo, 0.0, dkpe), c, s, sign)
        dcqn = _mm(dqu_ref[...], wq_ref[...])
        dckvn = _mm_nt(dkvu_ref[...], wkv_ref[...])
        gq = gq_ref[...]
        gkv = gkv_ref[...]
        dcq, dgq_rows = _rms_bwd(cq_ref[...], gq, dcqn)
        dckv, dgkv_rows = _rms_bwd(ckv_ref[...], gkv, dckvn)
        dz_ref[:, 0:256] = dcq.astype(dz_ref.dtype)
        dz_ref[:, 256:384] = dckv.astype(dz_ref.dtype)
        dz_ref[:, 384:512] = dkpe.astype(dz_ref.dtype)
        dz_ref[:, 512:1024] = dqs_ref[...]
        dz_ref[:, 1024:1536] = dkd_ref[...]
        dz_ref[:, 1536:2048] = dvd_ref[...]
        dz_ref[:, 2048:3072] = dgate_ref[...]
        dh = _mm(dz_ref[...], w_ref[...])
        g = g_ref[...]
        dxn, dg_rows = _rms_bwd(x_ref[...], g, dh)
        dx_ref[...] = dx1_ref[...] + dxn

        @pl.when(i == 0)
        def _():
            dgin_ref[...] = jnp.zeros_like(dgin_ref)
            dgq_ref[...] = jnp.zeros_like(dgq_ref)
            dgkv_ref[...] = jnp.zeros_like(dgkv_ref)
            dwq_ref[...] = jnp.zeros_like(dwq_ref)
            dwkv_ref[...] = jnp.zeros_like(dwkv_ref)

        dgin_ref[...] += jnp.sum(dg_rows, axis=0, keepdims=True)
        dgq_ref[...] += jnp.sum(dgq_rows, axis=0, keepdims=True)
        dgkv_ref[...] += jnp.sum(dgkv_rows, axis=0, keepdims=True)
        dwq_ref[...] += _mm_tn(dqu_ref[...], _rms(cq_ref[...], gq))
        dwkv_ref[...] += _mm_tn(_rms(ckv_ref[...], gkv), dkvu_ref[...])

    return _pcall(
        body, name="layer0_in_bwd", grid=(S // TOK,), semantics=("arbitrary",),
        in_specs=[_rows(TOK, 1024), _rows(TOK, 1024), _rows(TOK, 512), _rows(TOK, 512), _rows(TOK, 512), _rows(TOK, 512),
                  _rows(TOK, D), _rows(TOK, LANES), _rows(TOK, LANES), _full((8, LANES)), _rows(TOK, 256), _rows(TOK, 128),
                  _rows(TOK, D), _rows(TOK, D), _full((1, D)), _full(w_in.shape), _full((1, 256)), _full(w_q.shape),
                  _full((1, 128)), _full(w_kv.shape)],
        out_specs=[_rows(TOK, D), _rows(TOK, 3072), _full((1, D)), _full((1, 256)), _full((1, 128)), _full(w_q.shape),
                   _full(w_kv.shape)],
        out_shape=[_sds((S, D), jnp.float32), _sds((S, 3072), MXU), _sds((1, D), jnp.float32), _sds((1, 256), jnp.float32),
                   _sds((1, 128), jnp.float32), _sds(w_q.shape, jnp.float32), _sds(w_kv.shape, jnp.float32)],
        scratch_shapes=[pltpu.VMEM((TOK, 1024), MXU), pltpu.VMEM((TOK, 1536), MXU)],
    )(dqm, dkm, dvm, dqs, dkd, dvd, dgate0, cos, sin, consts, cq, ckv, x, dx1, g_in, w_in, g_q, w_q, g_kv, w_kv)


def _wgrad(a, b, name):
    S, M = a.shape
    N = b.shape[1]
    tm = next(t for t in range(WG_ROWS, 0, -LANES) if M % t == 0)
    tn = N if N <= 1024 else 512
    tk = min(WG_TOK, S)

    def body(a_ref, b_ref, o_ref):
        @pl.when(pl.program_id(2) == 0)
        def _():
            o_ref[...] = jnp.zeros_like(o_ref)

        o_ref[...] += _mm_tn(a_ref[...], b_ref[...])

    return _pcall(
        body, name=name, grid=(M // tm, N // tn, S // tk), semantics=("parallel", "parallel", "arbitrary"),
        in_specs=[pl.BlockSpec((tk, tm), lambda m, n, k: (k, m)), pl.BlockSpec((tk, tn), lambda m, n, k: (k, n))],
        out_specs=pl.BlockSpec((tm, tn), lambda m, n, k: (m, n)),
        out_shape=_sds((M, N), jnp.float32),
    )(a, b)


def _adamw(w, g, m, v, name):
    shape = w.shape
    R, C = (int(np.prod(shape[:-1])), shape[-1])
    w2, g2, m2, v2 = (t.reshape(R, C) for t in (w, g, m, v))
    fits = [t for t in range(8, ADAM_TILE_BYTES // (4 * C) + 1, 8) if R % t == 0]
    tr = max(fits) if fits else R
    tc = C if (tr * C * 4 <= ADAM_TILE_BYTES or C % 256) else 256

    def body(w_ref, g_ref, m_ref, v_ref, d_ref, nm_ref, nv_ref):
        gg = g_ref[...]
        nm = B1 * m_ref[...] + (1.0 - B1) * gg
        nv = B2 * v_ref[...] + (1.0 - B2) * (gg * gg)
        m_hat = nm / (1.0 - B1 ** STEP)
        v_hat = nv / (1.0 - B2 ** STEP)
        d_ref[...] = -LR * (m_hat / (jnp.sqrt(v_hat) + AEPS) + WD * w_ref[...])
        nm_ref[...] = nm
        nv_ref[...] = nv

    spec = pl.BlockSpec((tr, tc), lambda i, j: (i, j))
    d, nm, nv = _pcall(
        body, name=name, grid=(R // tr, C // tc), semantics=("parallel", "parallel"),
        in_specs=[spec] * 4, out_specs=[spec] * 3, out_shape=[_sds((R, C), jnp.float32)] * 3,
    )(w2, g2, m2, v2)
    return d.reshape(shape), nm.reshape(shape), nv.reshape(shape)


def _sum_leading(a, name):
    n, R, C = a.shape
    tr = SUM_ROWS if R % SUM_ROWS == 0 else R

    def body(a_ref, o_ref):
        acc = a_ref[0]
        for i in range(1, n):
            acc = acc + a_ref[i]
        o_ref[...] = acc

    return _pcall(
        body, name=name, grid=(R // tr,), semantics=("parallel",),
        in_specs=[pl.BlockSpec((n, tr, C), lambda i: (0, i, 0))], out_specs=_rows(tr, C),
        out_shape=_sds((R, C), a.dtype),
    )(a)


def _add_halves(g, c, b, name, out_dtype):
    n, _, R, C = g.shape
    tr = SUM_ROWS if R % SUM_ROWS == 0 else R

    def body(c_ref, a_ref, b_ref, o_ref):
        o_ref[...] = (a_ref[0] + b_ref[...]).astype(o_ref.dtype)

    spec = pl.BlockSpec((1, tr, C), lambda k, i, c_ref: (k, i, 0))
    grid_spec = pltpu.PrefetchScalarGridSpec(
        num_scalar_prefetch=1, grid=(n, R // tr),
        in_specs=[pl.BlockSpec((1, 1, tr, C), lambda k, i, c_ref: (k, c_ref[0], i, 0)), spec], out_specs=spec)
    return _pcall(body, name=name, semantics=("parallel", "parallel"), grid_spec=grid_spec,
                  out_shape=_sds(b.shape, out_dtype))(c.reshape(1).astype(jnp.int32), g, b)


def _total_sum(g, theirs, chip, c, recv, name):
    _, _, R, C = g.shape
    n = recv.shape[0]
    tr = SUM_ROWS if R % SUM_ROWS == 0 else R

    def body(at_ref, a_ref, b_ref, r_ref, o_ref):
        acc = a_ref[0, 0] + b_ref[0]
        for i in range(n):
            acc = acc + r_ref[i].astype(jnp.float32)
        o_ref[...] = acc

    grid_spec = pltpu.PrefetchScalarGridSpec(
        num_scalar_prefetch=1, grid=(R // tr,),
        in_specs=[pl.BlockSpec((1, 1, tr, C), lambda i, at_ref: (at_ref[0], at_ref[1], i, 0)),
                  pl.BlockSpec((1, tr, C), lambda i, at_ref: (at_ref[0], i, 0)),
                  pl.BlockSpec((n, tr, C), lambda i, at_ref: (0, i, 0))],
        out_specs=pl.BlockSpec((tr, C), lambda i, at_ref: (i, 0)))
    return _pcall(body, name=name, semantics=("parallel",), grid_spec=grid_spec,
                  out_shape=_sds((R, C), jnp.float32))(jnp.stack([chip, c]).astype(jnp.int32), g, theirs, recv)


def _place():
    return lax.axis_index("x"), lax.axis_index("y"), lax.axis_index("c")


class _Plan:
    def __init__(self, arrays, out_shape, scratch, start, finish, middle=None):
        self.arrays, self.out_shape, self.scratch = list(arrays), list(out_shape), list(scratch)
        self.start, self.finish, self.middle = start, finish, middle


def _gather8_plan(block):
    R, C = block.shape

    def parts(ins, outs, sems):
        (x_ref,), (out_ref,), (send_sems, recv_sems) = ins, outs, sems
        x, y, c = _place()
        me, sibling = (x, y, c), (x, y, 1 - c)
        chips = [(1 - x, y), (x, 1 - y), (1 - x, 1 - y)]

        def copy(k, blk, to, src=None):
            slot = out_ref.at[4 * blk[0] + 2 * blk[1] + blk[2]]
            return pltpu.make_async_remote_copy(
                src_ref=slot if src is None else src, dst_ref=slot,
                send_sem=send_sems.at[k], recv_sem=recv_sems.at[k], device_id=to, device_id_type=MESH_ID)

        def first():
            return [copy(0, me, sibling, src=x_ref)] + [copy(1 + j, me, (*chip, c), src=x_ref) for j, chip in enumerate(chips)]

        def passed():
            return [copy(4 + j, (*chip, c), sibling) for j, chip in enumerate(chips)]

        def arrivals():
            return [copy(1 + j, (*chip, c), me) for j, chip in enumerate(chips)]

        def late():
            return [copy(0, sibling, me)] + [copy(4 + j, (*chip, 1 - c), me) for j, chip in enumerate(chips)]

        return first, passed, arrivals, late

    def start(ins, outs, sems):
        for cp in parts(ins, outs, sems)[0]():
            cp.start()

    def middle(ins, outs, sems):
        _, passed, arrivals, _ = parts(ins, outs, sems)
        for arrived, forward in zip(arrivals(), passed()):
            arrived.wait_recv()
            forward.start()

    def finish(ins, outs, sems):
        first, passed, _, late = parts(ins, outs, sems)
        for cp in late():
            cp.wait_recv()
        for cp in first() + passed():
            cp.wait_send()

    return _Plan([block], [_sds((8, R, C), block.dtype)], [pltpu.SemaphoreType.DMA((7,)), pltpu.SemaphoreType.DMA((7,))],
                 start, finish, middle)


def _fill_own_slot(gathered, block):
    x, y, c = _place()
    return lax.dynamic_update_index_in_dim(gathered, block, 4 * x + 2 * y + c, 0)


def _started_and_waited(arrays, out_shape, n, copies):
    def start(ins, outs, sems):
        for cp in copies(ins, outs, sems):
            cp.start()

    def finish(ins, outs, sems):
        for cp in copies(ins, outs, sems):
            cp.wait()

    return _Plan(arrays, out_shape, [pltpu.SemaphoreType.DMA((n,)), pltpu.SemaphoreType.DMA((n,))], start, finish)


def _pair_swap_plan(g):
    n = g.shape[0]

    def copies(ins, outs, sems):
        (g_ref,), (out_ref,), (send_sems, recv_sems) = ins, outs, sems
        x, y, c = _place()
        return [pltpu.make_async_remote_copy(src_ref=g_ref.at[k, 1 - c], dst_ref=out_ref.at[k], send_sem=send_sems.at[k],
                                             recv_sem=recv_sems.at[k], device_id=(x, y, 1 - c), device_id_type=MESH_ID)
                for k in range(n)]

    return _started_and_waited([g], [_sds((n,) + g.shape[2:], g.dtype)], n, copies)


def _chip_exchange_plan(p):
    def copies(ins, outs, sems):
        (p_ref,), (out_ref,), (send_sems, recv_sems) = ins, outs, sems
        x, y, c = _place()
        chips = [(1 - x, y), (x, 1 - y), (1 - x, 1 - y)]
        return [pltpu.make_async_remote_copy(
            src_ref=p_ref.at[2 * cx + cy], dst_ref=out_ref.at[j], send_sem=send_sems.at[j],
            recv_sem=recv_sems.at[j], device_id=(cx, cy, c), device_id_type=MESH_ID)
            for j, (cx, cy) in enumerate(chips)]

    return _started_and_waited([p], [_sds((3,) + p.shape[1:], p.dtype)], 3, copies)


def _pair_exchange_plan(t):
    def copies(ins, outs, sems):
        (t_ref,), (out_ref,), (send_sems, recv_sems) = ins, outs, sems
        x, y, c = _place()
        return [pltpu.make_async_remote_copy(src_ref=t_ref, dst_ref=out_ref, send_sem=send_sems.at[0], recv_sem=recv_sems.at[0],
                                             device_id=(x, y, 1 - c), device_id_type=MESH_ID)]

    return _started_and_waited([t], [_sds(t.shape, t.dtype)], 1, copies)


def _both_plans(a, b):
    na, ma, sa = len(a.arrays), len(a.out_shape), len(a.scratch)

    def phase(name):
        fa, fb = getattr(a, name), getattr(b, name)
        if fa is None and fb is None:
            return None

        def run(ins, outs, sems):
            if fa is not None:
                fa(ins[:na], outs[:ma], sems[:sa])
            if fb is not None:
                fb(ins[na:], outs[ma:], sems[sa:])
        return run

    return _Plan(a.arrays + b.arrays, a.out_shape + b.out_shape, a.scratch + b.scratch,
                 phase("start"), phase("finish"), phase("middle"))


ANY_SPEC = pl.BlockSpec(memory_space=pl.ANY)


def _run_plan(plan, name):
    n_in, n_out = len(plan.arrays), len(plan.out_shape)

    def body(*refs):
        ins, outs, sems = refs[:n_in], refs[n_in:n_in + n_out], refs[n_in + n_out:]
        plan.start(ins, outs, sems)
        if plan.middle is not None:
            plan.middle(ins, outs, sems)
        plan.finish(ins, outs, sems)

    return _pcall(body, name=name, in_specs=[ANY_SPEC] * n_in, out_specs=[ANY_SPEC] * n_out, out_shape=plan.out_shape,
                  scratch_shapes=plan.scratch)(*plan.arrays)


def _pcall_riding(body, plan, args, *, name, grid, in_specs, out_specs, out_shape, scratch_shapes):
    if plan is None:
        outs = _pcall(body, name=name, grid=grid, semantics=("arbitrary",), in_specs=in_specs, out_specs=out_specs,
                      out_shape=out_shape, scratch_shapes=scratch_shapes)(*args)
        return list(outs), None
    n_in, n_out, n_s = len(args), len(out_shape), len(scratch_shapes)
    p_in, p_out = len(plan.arrays), len(plan.out_shape)
    steps = grid[0]

    def riding(*refs):
        ins, pins = refs[:n_in], refs[n_in:n_in + p_in]
        o0 = n_in + p_in
        outs, pouts = refs[o0:o0 + n_out], refs[o0 + n_out:o0 + n_out + p_out]
        s0 = o0 + n_out + p_out
        scr, sems = refs[s0:s0 + n_s], refs[s0 + n_s:]
        j = pl.program_id(0)

        @pl.when(j == 0)
        def _():
            plan.start(pins, pouts, sems)

        if plan.middle is not None:
            @pl.when(j == steps // 2)
            def _():
                plan.middle(pins, pouts, sems)

        body(*ins, *outs, *scr)

        @pl.when(j == steps - 1)
        def _():
            plan.finish(pins, pouts, sems)

    res = _pcall(riding, name=name, grid=grid, semantics=("arbitrary",), in_specs=list(in_specs) + [ANY_SPEC] * p_in,
                 out_specs=list(out_specs) + [ANY_SPEC] * p_out, out_shape=list(out_shape) + plan.out_shape,
                 scratch_shapes=list(scratch_shapes) + plan.scratch)(*args, *plan.arrays)
    return list(res[:n_out]), list(res[n_out:])


class _RowSeq:
    def __init__(self, pieces):
        self.pieces = list(pieces)

    def rows(self, a, b):
        out, off = [], 0
        for p in self.pieces:
            lo, hi = max(a, off), min(b, off + p.shape[0])
            if lo < hi:
                out.append(p[lo - off:hi - off])
            off += p.shape[0]
        return out

    def array(self):
        return jnp.concatenate(self.pieces, axis=0)


def _row_seq(w):
    return w if isinstance(w, _RowSeq) else _RowSeq([w])


def _prep_w_in0(wt):
    wt = _row_seq(wt)
    one = wt.pieces[0]
    z32 = [jnp.zeros((32, one.shape[1]), one.dtype)]
    k0, k1 = wt.rows(928, 992), wt.rows(992, 1056)
    v0, v1 = wt.rows(1056, 1120), wt.rows(1120, 1184)
    return jnp.concatenate(wt.rows(0, 384) + z32 + z32 + wt.rows(384, 416) + z32 + wt.rows(416, 928)
                           + k0 * 4 + k1 * 4 + v0 * 4 + v1 * 4 + wt.rows(1184, 2208), axis=0)


def _fold_w_in0(d):
    def fold(blk):
        b = blk.reshape(8, 64, blk.shape[1])
        return jnp.concatenate([b[0] + b[1] + b[2] + b[3], b[4] + b[5] + b[6] + b[7]], axis=0)
    return _RowSeq([d[0:384], d[448:480], d[512:1024], fold(d[1024:1536]), fold(d[1536:2048]), d[2048:3072]])


def _prep_w_q(wt):
    return jnp.pad(wt.reshape(N_MLA, 96, Q_RANK), ((0, 0), (0, 32), (0, 0))).reshape(1024, Q_RANK)


def _fold_w_q(d):
    return d.reshape(N_MLA, 128, Q_RANK)[:, :96].reshape(768, Q_RANK)


def _prep_w_kv(w):
    w3 = w.reshape(KV_RANK, N_MLA, 128)
    kk = jnp.pad(w3[:, :, :64], ((0, 0), (0, 0), (0, 64))).reshape(KV_RANK, 1024)
    return jnp.concatenate([kk, w3[:, :, 64:].reshape(KV_RANK, 512)], axis=1)


def _fold_w_kv(d):
    kk = d[:, :1024].reshape(KV_RANK, N_MLA, 128)[:, :, :64]
    vv = d[:, 1024:].reshape(KV_RANK, N_MLA, 64)
    return jnp.concatenate([kk, vv], axis=2).reshape(KV_RANK, 1024)


W_IN1_SHARD = 1028
W_IN1_STEP = W_IN1_SHARD % 16


class _ShiftedShards:
    def __init__(self, blocks):
        self.blocks = list(blocks)


def _shifted_shard(a, chip, rows):
    out = jnp.zeros((rows, a.shape[1]), a.dtype)
    for k in range(4):
        out = jnp.where(chip == k, jnp.pad(a, ((W_IN1_STEP * k, rows - W_IN1_STEP * k - a.shape[0]), (0, 0))), out)
    return out


def _prep_w_in1(wt):
    if not isinstance(wt, _ShiftedShards):
        return jnp.concatenate([wt[0:3072], wt[3088:4112], wt[3072:3088], jnp.zeros((112, wt.shape[1]), wt.dtype)], axis=0)
    b = wt.blocks
    row = lax.broadcasted_iota(jnp.int32, (16, 1), 0)

    def seam(k, first, second):
        return jnp.where(row < W_IN1_STEP * (k + 1), first, second)

    return jnp.concatenate([
        b[0][0:1024], seam(0, b[0][1024:1040], b[1][0:16]), b[1][16:1024], seam(1, b[1][1024:1040], b[2][0:16]),
        b[2][16:1024], b[3][16:1040], seam(2, b[2][1024:1040], b[3][0:16]), jnp.zeros((112, 1024), b[0].dtype)], axis=0)


def _fold_w_in1(d):
    return _RowSeq([d[0:3072], d[4096:4112], d[3072:4096]])


class _Alone:
    def __init__(self, w_out0, o_g_in, w_in1, w_out1):
        self.layer1 = (w_out0, o_g_in, w_in1, w_out1)

    def gather_plan(self):
        return None

    def layer1_weights(self, rode):
        return self.layer1

    def swap_plan(self, grads1):
        return None

    def exchange_plan(self, rode):
        return None

    def finish(self, rode):
        pass


def _local_step(x, pos, target, e_g_in, w_in0, e_g_q, w_q, e_g_kv, w_kv, sinks, b_f, g_final, layer1):
    S = x.shape[0]
    w_in0p, w_qp, w_kvp = _prep_w_in0(w_in0), _prep_w_q(w_q), _prep_w_kv(w_kv)
    slopes = jnp.asarray(2.0 ** (-8.0 * (np.arange(N_SWA, dtype=np.float32) + 1.0) / N_SWA), jnp.float32)
    sinks1 = sinks.reshape(N_SWA)
    b_col = b_f.reshape(N_FOX, 1)

    (h0, cq, ckv, qm, km, vm, qs, kd, vd, gate0, cos, sin) = _layer0_in(
        x, pos, e_g_in, w_in0p, e_g_q, w_qp, e_g_kv, w_kvp)
    o_m, lse_m, rode = _attn_fwd_t(qm, km, vm, (NOPE + ROPE) ** -0.5, split=True, name="mla_fwd", plan=layer1.gather_plan())
    w_out0, o_g_in, w_in1, w_out1 = layer1.layer1_weights(rode)
    w_in1p = _prep_w_in1(w_in1)
    o_s, lse_s = _swa_fwd(qs, kd, vd, sinks1, slopes)
    x1, h1, q1, k1, v1, gate1, f_slab = _layer0_out_layer1_in(x, o_m, o_s, gate0, w_out0, o_g_in, w_in1p)
    f_row = f_slab[:, :N_FOX].T
    lc_row = _forget_fwd(f_row, b_col)
    lcc = lc_row.T
    o1, lse1, _ = _attn_fwd_t(q1, k1, v1, HEAD ** -0.5, split=False, name="fox_fwd", lcc=lcc)
    loss8, dg_final, dw_out1, dx2, do1, dgate1 = _head(x1, o1, gate1, w_out1, g_final, target)

    dq1, dk1, dv1, dlc, _ = _attn_bwd_t(q1, k1, v1, do1, o1, lse1, HEAD ** -0.5, split=False, name="fox_bwd", lcc=lcc)
    df_row, db_f = _forget_bwd(dlc.reshape(N_FOX, S), f_row, b_col)
    df_slab = jnp.pad(df_row.T, ((0, 0), (0, LANES - N_FOX))).astype(MXU)
    dz1, dx1, dg_o_in, dw_out0, do_m, do_s, dgate0 = _layer1_in_bwd(
        dq1, dk1, dv1, dgate1, df_slab, x1, dx2, o_g_in, w_in1p, gate0, o_m, o_s, w_out0)
    grads1 = dict(o_g_in=dg_o_in, o_w_in=_fold_w_in1(_wgrad(dz1, h1, "wgrad_in1")), o_w_out=dw_out1, e_w_out=dw_out0)
    dqs, dkd, dvd, dsink, rode = _swa_bwd(qs, kd, vd, do_s, o_s, lse_s, sinks1, slopes, plan=layer1.swap_plan(grads1))
    dqm, dkm, dvm, rode = _attn_bwd_t(qm, km, vm, do_m, o_m, lse_m, (NOPE + ROPE) ** -0.5, split=True, name="mla_bwd",
                                      plan=layer1.exchange_plan(rode))
    layer1.finish(rode)
    dx, dz0, dg_in, dg_q, dg_kv, dw_q, dw_kv = _layer0_in_bwd(
        dqm, dkm, dvm, dqs, dkd, dvd, dgate0, cos, sin, cq, ckv, x, dx1, e_g_in, w_in0p, e_g_q, w_qp, e_g_kv, w_kvp)

    grads = dict(
        e_g_in=dg_in,
        e_w_in=_fold_w_in0(_wgrad(dz0, h0, "wgrad_in0")),
        e_g_q_a=dg_q,
        e_w_q_up=_fold_w_q(dw_q),
        e_g_kv_a=dg_kv,
        e_w_kv_up=_fold_w_kv(dw_kv),
        e_sinks=dsink[:, 0:4, 0].reshape(1, N_SWA),
        o_b_f=db_f.reshape(1, N_FOX),
        g_final=dg_final,
        **grads1,
    )
    return loss8[0, 0], dx, grads


SHARDED = ("e_w_in", "e_w_q_up", "e_w_kv_up", "e_w_out", "o_g_in", "o_w_in", "o_w_out")
TRANSPOSED = ("e_w_in", "e_w_q_up", "o_w_in")
COL_SHARDED = ("e_w_kv_up", "o_g_in")
REPLICATED = ("e_g_in", "e_g_q_a", "e_g_kv_a", "e_sinks", "o_b_f", "g_final")
FULL_SHAPES = dict(e_w_in=(2208, 1024), e_w_q_up=(768, 256), e_w_kv_up=(128, 1024), e_w_out=(1024, 1024),
                   o_g_in=(1, 1024), o_w_in=(4112, 1024), o_w_out=(1024, 1024))
GROUPS = dict(
    layer0=dict(rows=768, windows=dict(e_w_in=(0, 0), e_w_q_up=(560, 0), e_w_kv_up=(560, 256))),
    layer1=dict(rows=1568, windows=dict(o_w_in=(0, 0), o_w_out=(1040, 0), e_w_out=(1296, 0), o_g_in=(1552, 0))),
)


def _shard_shape(name):
    r, c = FULL_SHAPES[name]
    return (r, c // 4) if name in COL_SHARDED else (r // 4, c)


def _as_handled(name, a):
    a = a[0] if a.ndim == 3 else a
    return a.T if name in TRANSPOSED else a


def _as_given(name, a, shape):
    return (a.T if name in TRANSPOSED else a).reshape(shape)


def _pack_block(p, group, shifted_for=None):
    def rows(a, n):
        return jnp.pad(a, ((0, n - a.shape[0]), (0, 0)))

    if group == "layer0":
        band = jnp.concatenate([p["e_w_q_up"], rows(p["e_w_kv_up"], 192), jnp.zeros((192, 512), p["e_w_in"].dtype)], axis=1)
        return jnp.concatenate([rows(p["e_w_in"], 560), rows(band, 208)], axis=0)
    g = p["o_g_in"]
    band = jnp.pad(g, ((0, 16 - g.shape[0]), (0, PACK_COLS - g.shape[1])))
    w_in = rows(p["o_w_in"], 1040) if shifted_for is None else _shifted_shard(p["o_w_in"], shifted_for, 1040)
    return jnp.concatenate([w_in, p["o_w_out"], p["e_w_out"], band], axis=0)


def _window(block, group, name, width=None):
    r0, c0 = GROUPS[group]["windows"][name]
    r, c = _shard_shape(name)
    return block[..., r0:r0 + r, c0:c0 + (c if width is None else width)]


def _chip_slice(name, full, k):
    r, c = _shard_shape(name)
    if isinstance(full, _RowSeq):
        return jnp.concatenate(full.rows(r * k, r * (k + 1)), axis=0)
    return full[:, c * k:c * (k + 1)] if name in COL_SHARDED else full[r * k:r * (k + 1), :]


def _packed_weights(w, group):
    parts = {}
    for n in GROUPS[group]["windows"]:
        a = _as_handled(n, w[n])
        parts[n] = lax.bitcast_convert_type(a, jnp.bfloat16).reshape(1, -1) if n == "o_g_in" else a.astype(jnp.bfloat16)
    x, y, _ = _place()
    halves = _pack_block(parts, group, shifted_for=2 * x + y).reshape(2, GROUPS[group]["rows"] // 2, PACK_COLS)
    return lax.dynamic_index_in_dim(halves, lax.axis_index("c"), 0, keepdims=False)


def _unpacked_weights(gathered, half, group):
    blocks = _fill_own_slot(gathered, half).reshape(4, GROUPS[group]["rows"], PACK_COLS)
    full = {}
    for n in GROUPS[group]["windows"]:
        if n == "o_g_in":
            halves = _window(blocks, group, n, width=512).reshape(4, 1, 256, 2)
            full[n] = jnp.concatenate(list(lax.bitcast_convert_type(halves, jnp.float32)), axis=1)
        elif n == "o_w_in":
            full[n] = _ShiftedShards(blocks[k, 0:1040].astype(MXU) for k in range(4))
        else:
            pieces = [_window(blocks[k], group, n).astype(MXU) for k in range(4)]
            if n == "e_w_in":
                full[n] = _RowSeq(pieces)
            else:
                full[n] = jnp.concatenate(pieces, axis=1 if n in COL_SHARDED else 0)
    return full


class _GroupReduce:
    def __init__(self, group):
        self.group = group
        self.c = lax.axis_index("c")
        self.chip = 2 * lax.axis_index("x") + lax.axis_index("y")

    def swap_plan(self, grads):
        names = GROUPS[self.group]["windows"]
        per_chip = jnp.concatenate([_pack_block({n: _chip_slice(n, grads[n], k) for n in names}, self.group)
                                    for k in range(4)], axis=0)
        self.g4 = per_chip.reshape(4, 2, GROUPS[self.group]["rows"] // 2, PACK_COLS)
        return _pair_swap_plan(self.g4)

    def exchange_plan(self, rode):
        self.theirs = rode[0]
        return _chip_exchange_plan(_add_halves(self.g4, self.c, self.theirs, "pair_add_" + self.group, jnp.bfloat16))

    def finish(self, rode):
        my_half = _total_sum(self.g4, self.theirs, self.chip, self.c, rode[0], "chip_sum_" + self.group)
        other_half = _run_plan(_pair_exchange_plan(my_half), "pair_exchange_" + self.group)[0]
        total = jnp.concatenate([jnp.where(self.c == 0, my_half, other_half), jnp.where(self.c == 0, other_half, my_half)], axis=0)
        self.sums = {n: _window(total, self.group, n) for n in GROUPS[self.group]["windows"]}

    def run(self, grads, beside):
        swap = self.swap_plan(grads)
        outs = _run_plan(_both_plans(swap, beside), "pair_swap_" + self.group)
        rode, others = outs[:len(swap.out_shape)], outs[len(swap.out_shape):]
        self.finish(_run_plan(self.exchange_plan(rode), "chip_exchange_" + self.group))
        return self.sums, others


class _Layer1Exchange(_GroupReduce):
    def __init__(self, w):
        super().__init__("layer1")
        self.half = _packed_weights(w, "layer1")

    def gather_plan(self):
        return _gather8_plan(self.half)

    def layer1_weights(self, rode):
        full = _unpacked_weights(rode[0], self.half, "layer1")
        return full["e_w_out"], full["o_g_in"], full["o_w_in"], full["o_w_out"]


def kernel(x, positions, e_g_in, e_w_in, e_g_q_a, e_w_q_up, e_g_kv_a, e_w_kv_up, e_sinks, e_w_out, o_g_in, o_w_in, o_b_f, o_w_out, g_final, loss_target, m_e_g_in, m_e_w_in, m_e_g_q_a, m_e_w_q_up, m_e_g_kv_a, m_e_w_kv_up, m_e_sinks, m_e_w_out, m_o_g_in, m_o_w_in, m_o_b_f, m_o_w_out, m_g_final, v_e_g_in, v_e_w_in, v_e_g_q_a, v_e_w_q_up, v_e_g_kv_a, v_e_w_kv_up, v_e_sinks, v_e_w_out, v_o_g_in, v_o_w_in, v_o_b_f, v_o_w_out, v_g_final):
    w = dict(e_g_in=e_g_in, e_w_in=e_w_in, e_g_q_a=e_g_q_a, e_w_q_up=e_w_q_up, e_g_kv_a=e_g_kv_a, e_w_kv_up=e_w_kv_up,
             e_sinks=e_sinks, e_w_out=e_w_out, o_g_in=o_g_in, o_w_in=o_w_in, o_b_f=o_b_f, o_w_out=o_w_out, g_final=g_final)
    m = dict(e_g_in=m_e_g_in, e_w_in=m_e_w_in, e_g_q_a=m_e_g_q_a, e_w_q_up=m_e_w_q_up, e_g_kv_a=m_e_g_kv_a,
             e_w_kv_up=m_e_w_kv_up, e_sinks=m_e_sinks, e_w_out=m_e_w_out, o_g_in=m_o_g_in, o_w_in=m_o_w_in, o_b_f=m_o_b_f,
             o_w_out=m_o_w_out, g_final=m_g_final)
    v = dict(e_g_in=v_e_g_in, e_w_in=v_e_w_in, e_g_q_a=v_e_g_q_a, e_w_q_up=v_e_w_q_up, e_g_kv_a=v_e_g_kv_a,
             e_w_kv_up=v_e_w_kv_up, e_sinks=v_e_sinks, e_w_out=v_e_w_out, o_g_in=v_o_g_in, o_w_in=v_o_w_in, o_b_f=v_o_b_f,
             o_w_out=v_o_w_out, g_final=v_g_final)
    order = ("e_g_in", "e_w_in", "e_g_q_a", "e_w_q_up", "e_g_kv_a", "e_w_kv_up", "e_sinks", "e_w_out", "o_g_in", "o_w_in",
             "o_b_f", "o_w_out", "g_final")
    half0 = _packed_weights(w, "layer0")
    full = _unpacked_weights(_run_plan(_gather8_plan(half0), "gather_weights_layer0")[0], half0, "layer0")
    layer1 = _Layer1Exchange(w)

    loss_part, dx, grads = _local_step(
        x[0], positions.reshape(-1, 1), loss_target[0], e_g_in, full["e_w_in"], e_g_q_a, full["e_w_q_up"], e_g_kv_a,
        full["e_w_kv_up"], e_sinks, o_b_f, g_final.reshape(1, D), layer1)

    small = jnp.concatenate([jnp.pad(loss_part.reshape(1), (0, LANES - 1))]
                            + [jnp.pad(grads[n].reshape(-1), (0, (-grads[n].size) % LANES)) for n in REPLICATED])
    rows = small.shape[0] // LANES
    small = jnp.pad(small.reshape(rows, LANES), ((0, (-rows) % 8), (0, 0)))
    sums0, (gathered_small,) = _GroupReduce("layer0").run(grads, _gather8_plan(small))
    gsum = {**layer1.sums, **sums0}
    ssum = _sum_leading(_fill_own_slot(gathered_small, small), "small_grad_sum").reshape(-1)
    loss = ssum[0]
    off = LANES
    for n in REPLICATED:
        cnt = w[n].size
        gsum[n] = ssum[off:off + cnt].reshape(w[n].shape)
        off += cnt + (-cnt) % LANES

    grad, delta, new_m, new_v = {}, {}, {}, {}
    for n in order:
        if n == "o_w_in":
            def tiles(a):
                return jnp.transpose(a, (2, 0, 1)).reshape(-1, LANES)

            def given(a):
                return jnp.transpose(a.reshape(-1, 8, LANES), (1, 2, 0)).reshape(w[n].shape)

            g_t = gsum[n].reshape(-1, LANES)
            outs = _adamw(tiles(w[n]), g_t, tiles(m[n]), tiles(v[n]), "adamw_" + n)
            grad[n], delta[n], new_m[n], new_v[n] = (given(a) for a in (g_t,) + outs)
        elif n in SHARDED:
            outs = _adamw(_as_handled(n, w[n]), gsum[n], _as_handled(n, m[n]), _as_handled(n, v[n]), "adamw_" + n)
            grad[n], delta[n], new_m[n], new_v[n] = (_as_given(n, a, w[n].shape) for a in (gsum[n],) + outs)
        else:
            grad[n] = gsum[n]
            delta[n], new_m[n], new_v[n] = _adamw(w[n], gsum[n], m[n], v[n], "adamw_" + n)
    return (loss, dx[None], *[grad[n] for n in order], *[delta[n] for n in order], *[new_m[n] for n in order],
            *[new_v[n] for n in order])
```

```python
import math

import numpy as np
import jax
import jax.numpy as jnp
from jax import lax
from jax.experimental import pallas as pl
from jax.experimental.pallas import tpu as pltpu

D = 1024
EPS = 1e-6
ROPE_THETA = 10000.0
N_MLA = 8
Q_RANK = 256
KV_RANK = 128
NOPE = 64
ROPE = 32
N_SWA = 8
WINDOW = 128
N_FOX = 16
HEAD = 64
LR, B1, B2, AEPS, WD, STEP = 0.001, 0.9, 0.999, 1e-08, 0.01, 10

LANES = 128
HALF = 64
VMEM_LIMIT = 56 * 1024 * 1024
MXU = jnp.bfloat16
TOK = 256
WG_TOK = 2048
WG_ROWS = 1536
ATT = 256
FWD_CHUNK = 2
BWD_CHUNK = 2
SWA_GROUP = 8
NEG = float("-inf")

PACK_COLS = 1024
SUM_ROWS = 256
ADAM_TILE_BYTES = 2 << 20
MESH_ID = pl.DeviceIdType.MESH


def _pcall(body, *, name, vmem=VMEM_LIMIT, semantics=None, **kw):
    params = dict(vmem_limit_bytes=vmem)
    if semantics is not None:
        params["dimension_semantics"] = semantics
    return pl.pallas_call(body, name=name, compiler_params=pltpu.CompilerParams(**params), **kw)


def _mm(a, b):
    return jnp.dot(a.astype(MXU), b.astype(MXU), preferred_element_type=jnp.float32)


def _mm_nt(a, b):
    return lax.dot_general(a.astype(MXU), b.astype(MXU), (((1,), (1,)), ((), ())),
                           preferred_element_type=jnp.float32)


def _mm_tn(a, b):
    return lax.dot_general(a.astype(MXU), b.astype(MXU), (((0,), (0,)), ((), ())),
                           preferred_element_type=jnp.float32)


def _full(shape):
    n = len(shape)
    return pl.BlockSpec(shape, lambda *_: (0,) * n)


def _rows(tm, n):
    return pl.BlockSpec((tm, n), lambda i: (i, 0))


def _sds(shape, dtype):
    return jax.ShapeDtypeStruct(shape, dtype)


def _rms(x, g):
    r = lax.rsqrt(jnp.mean(x * x, axis=-1, keepdims=True) + EPS)
    return x * r * g


def _rms_bwd(x, g, dy):
    r = lax.rsqrt(jnp.mean(x * x, axis=-1, keepdims=True) + EPS)
    xh = x * r
    dxh = dy * g
    dx = r * (dxh - xh * jnp.mean(dxh * xh, axis=-1, keepdims=True))
    return dx, dy * xh


def _sigmoid(x):
    return 1.0 / (1.0 + jnp.exp(-x))


def _lane_masks():
    lane = lax.broadcasted_iota(jnp.int32, (1, LANES), 1)
    return lane < HALF


def _split_heads(a, lo):
    z = jnp.zeros_like(a)
    return [jnp.where(lo, a, z), jnp.where(lo, z, a)]


def _rope_consts():
    inv = np.zeros((8, LANES), np.float32)
    j = np.arange(ROPE // 2, dtype=np.float32)
    f = (1.0 / (ROPE_THETA ** (np.arange(0, ROPE, 2, dtype=np.float32) / ROPE))).astype(np.float32)
    inv[0, HALF:HALF + 16] = f
    inv[0, HALF + 16:HALF + 32] = f
    inv[1, HALF:HALF + 16] = -1.0
    inv[1, HALF + 16:HALF + 32] = 1.0
    del j
    return jnp.asarray(inv)


def _rope_tables(pos_f, consts):
    ang = pos_f * consts[0:1, :]
    sign = consts[1:2, :]
    c = jnp.where(sign != 0.0, jnp.cos(ang), 1.0)
    s = jnp.sin(ang) * sign
    return c, s


def _swap_halves(v, sign):
    lo = pltpu.roll(v, LANES - 16, axis=1)
    hi = pltpu.roll(v, 16, axis=1)
    return jnp.where(sign < 0.0, lo, jnp.where(sign > 0.0, hi, 0.0))


def _rope(x, c, s, sign):
    return x * c + _swap_halves(x, sign) * s


def _rope_t(dy, c, s, sign):
    return dy * c + _swap_halves(dy * s, sign)


def _layer0_in(x, pos, g_in, w_in, g_q, w_q, g_kv, w_kv):
    S = x.shape[0]
    consts = _rope_consts()

    def body(x_ref, pos_ref, c_ref, g_ref, w_ref, gq_ref, wq_ref, gkv_ref, wkv_ref,
             h_ref, cq_ref, ckv_ref, qm_ref, km_ref, vm_ref,
             qs_ref, kd_ref, vd_ref, gate_ref, cos_ref, sin_ref):
        h = _rms(x_ref[...], g_ref[...])
        h_ref[...] = h.astype(h_ref.dtype)
        z = _mm_nt(h, w_ref[...])
        cq = z[:, 0:256]
        ckv = z[:, 256:384]
        kpe = z[:, 384:512]
        cq_ref[...] = cq
        ckv_ref[...] = ckv
        qs_ref[...] = z[:, 512:1024].astype(qs_ref.dtype)
        kd_ref[...] = z[:, 1024:1536].astype(kd_ref.dtype)
        vd_ref[...] = z[:, 1536:2048].astype(vd_ref.dtype)
        gate_ref[...] = z[:, 2048:3072]
        cqn = _rms(cq, gq_ref[...])
        ckvn = _rms(ckv, gkv_ref[...])
        q = _mm_nt(cqn, wq_ref[...])
        kv = _mm(ckvn, wkv_ref[...])
        vm_ref[...] = kv[:, 1024:1536].astype(vm_ref.dtype)
        consts_v = c_ref[...]
        sign = consts_v[1:2, :]
        c, s = _rope_tables(pos_ref[...].astype(jnp.float32), consts_v)
        cos_ref[...] = c
        sin_ref[...] = s
        kpe_r = _rope(kpe, c, s, sign)
        for hd in range(N_MLA):
            sl = slice(LANES * hd, LANES * (hd + 1))
            qm_ref[:, sl] = _rope(q[:, sl], c, s, sign).astype(qm_ref.dtype)
            km_ref[:, sl] = (kv[:, sl] + kpe_r).astype(km_ref.dtype)

    outs = [
        ((S, D), MXU), ((S, 256), jnp.float32), ((S, 128), jnp.float32),
        ((S, 1024), MXU), ((S, 1024), MXU), ((S, 512), MXU), ((S, 512), MXU), ((S, 512), MXU), ((S, 512), MXU),
        ((S, 1024), jnp.float32), ((S, 128), jnp.float32), ((S, 128), jnp.float32),
    ]
    return _pcall(
        body, name="layer0_in", grid=(S // TOK,), semantics=("arbitrary",),
        in_specs=[_rows(TOK, D), _rows(TOK, 1), _full((8, LANES)), _full((1, D)), _full(w_in.shape), _full((1, 256)),
                  _full(w_q.shape), _full((1, 128)), _full(w_kv.shape)],
        out_specs=[_rows(TOK, s[1]) for s, _ in outs],
        out_shape=[_sds(s, d) for s, d in outs],
    )(x, pos, consts, g_in, w_in, g_q, w_q, g_kv, w_kv)


AUG = (HALF, 0)
ONE = (HALF + 8, 8)


def _data_lanes(idx, h):
    return (idx < HALF) if h == 0 else (idx >= HALF)


def _three_terms(x):
    hi = x.astype(MXU).astype(jnp.float32)
    mid = (x - hi).astype(MXU).astype(jnp.float32)
    lo = (x - hi - mid).astype(MXU).astype(jnp.float32)
    return hi, mid, lo


def _q_aug(qblk, lc, h, scale, lane):
    a = AUG[h]
    hi, mid, lo = _three_terms(lc)
    ones = ((lane >= a + 3) & (lane <= a + 5)).astype(jnp.float32)
    aug = jnp.where(lane == a, hi, jnp.where(lane == a + 1, mid, jnp.where(lane == a + 2, lo, ones)))
    return jnp.where(_data_lanes(lane, h), qblk * jnp.asarray(scale, qblk.dtype), aug.astype(qblk.dtype))


def _k_aug(kblk, lc, h, lane):
    a = AUG[h]
    hi, mid, lo = _three_terms(-lc)
    ones = ((lane >= a) & (lane <= a + 2)).astype(jnp.float32)
    aug = jnp.where(lane == a + 3, hi, jnp.where(lane == a + 4, mid, jnp.where(lane == a + 5, lo, ones)))
    return jnp.where(_data_lanes(lane, h), kblk, aug.astype(kblk.dtype))


def _lc_col(lc_ref, r0, rows, h):
    head = lax.broadcasted_iota(jnp.int32, (1, lc_ref.shape[1]), 1)
    return jnp.sum(jnp.where(head == 2 * pl.program_id(0) + h, lc_ref[pl.ds(r0, rows), :], 0.0), axis=1, keepdims=True)


def _attn_fwd_t(q, k, v, scale, *, split, name, lcc=None, plan=None):
    S = q.shape[0]
    npair = v.shape[1] // LANES
    W = 2 * LANES if split else LANES
    T = ATT
    CH = FWD_CHUNK * T
    assert S % CH == 0
    nq = S // T

    def body(*refs):
        if split:
            q_ref, k_ref, v_ref, o_ref, lse_ref, vt, acc, m_sc = refs
        else:
            q_ref, k_ref, v_ref, lcc_ref, o_ref, lse_ref, kaug, vt, acc, m_sc = refs
        lane = lax.broadcasted_iota(jnp.int32, (1, LANES), 1)
        sub = lax.broadcasted_iota(jnp.int32, (LANES, 1), 0)
        key_minus_qry = lax.broadcasted_iota(jnp.int32, (CH, T), 0) - lax.broadcasted_iota(jnp.int32, (CH, T), 1)

        def prep(i, c):
            r0 = pl.multiple_of(i * T, T)
            vblk = v_ref[pl.ds(r0, T), :].astype(jnp.float32)
            for h in (0, 1):
                vh = jnp.where(_data_lanes(lane, h), vblk, (lane == ONE[h]).astype(jnp.float32))
                vt[h, :, pl.ds(r0, T)] = vh.T.astype(vt.dtype)
                if not split:
                    kaug[h, pl.ds(r0, T), :] = _k_aug(k_ref[pl.ds(r0, T), :], _lc_col(lcc_ref, r0, T, h), h, lane)
            return c

        lax.fori_loop(0, nq, prep, 0)

        def queries(qi):
            q0 = pl.multiple_of(qi * T, T)
            qblk = q_ref[pl.ds(q0, T), :]
            if split:
                return (qblk[:, :LANES], qblk[:, LANES:])
            return tuple(_q_aug(qblk, _lc_col(lcc_ref, q0, T, h), h, scale, lane) for h in (0, 1))

        def scores(qs, c):
            k0 = pl.multiple_of(c * CH, CH)
            out = []
            for h in (0, 1):
                if split:
                    out.append(_mm_nt(k_ref[pl.ds(k0, CH), LANES * h:LANES * (h + 1)], qs[h]) * scale)
                else:
                    out.append(_mm_nt(kaug[h, pl.ds(k0, CH), :], qs[h]))
            return tuple(out)

        def q_block(qi, carry):
            qs, first_scores = carry[:2], carry[2:]
            q0 = pl.multiple_of(qi * T, T)
            acc[...] = jnp.zeros_like(acc)
            m_sc[...] = jnp.full(m_sc.shape, NEG, jnp.float32)

            def absorb(c, sts, masked):
                k0 = pl.multiple_of(c * CH, CH)
                for h in (0, 1):
                    st = sts[h]
                    if masked:
                        st = jnp.where(key_minus_qry <= q0 - k0, st, NEG)
                    m_old = m_sc[h:h + 1, :]
                    m_new = jnp.maximum(m_old, jnp.max(st, axis=0, keepdims=True))
                    alpha = jnp.exp(m_old - m_new)
                    pt = jnp.exp(st - m_new)
                    acc[h] = alpha * acc[h] + _mm(vt[h, :, pl.ds(k0, CH)], pt)
                    m_sc[h:h + 1, :] = m_new

            last = qi // FWD_CHUNK

            def pipelined(c, sts):
                nxt = scores(qs, c + 1)
                absorb(c, sts, False)
                return nxt

            sts = lax.fori_loop(0, last, pipelined, first_scores)
            qs_next = queries(jnp.minimum(qi + 1, nq - 1))
            nxt = qs_next + scores(qs_next, 0)
            absorb(last, sts, True)
            ot = None
            for h in (0, 1):
                a = acc[h]
                l = a[ONE[h]:ONE[h] + 1, :]
                oh = jnp.where(_data_lanes(sub, h), a * (1.0 / l), 0.0)
                ot = oh if ot is None else ot + oh
                lse_ref[0, h:h + 1, pl.ds(q0, T)] = m_sc[h:h + 1, :] + jnp.log(l)
            o_ref[pl.ds(q0, T), :] = ot.T
            return nxt

        qs0 = queries(0)
        lax.fori_loop(0, nq, q_block, qs0 + scores(qs0, 0))

    wide = pl.BlockSpec((S, W), lambda j: (0, j))
    slab = pl.BlockSpec((S, LANES), lambda j: (0, j))
    rows = pl.BlockSpec((1, 2, S), lambda j: (j, 0, 0))
    in_specs = [wide, wide, slab]
    args = [q, k, v]
    scratch = []
    if not split:
        in_specs.append(_full(lcc.shape))
        args.append(lcc)
        scratch.append(pltpu.VMEM((2, S, LANES), MXU))
    scratch += [pltpu.VMEM((2, LANES, S), MXU), pltpu.VMEM((2, LANES, T), jnp.float32), pltpu.VMEM((8, T), jnp.float32)]
    (o, lse), rode = _pcall_riding(
        body, plan, args, name=name, grid=(npair,), in_specs=in_specs, out_specs=[slab, rows],
        out_shape=[_sds((S, npair * LANES), jnp.float32), _sds((npair, 2, S), jnp.float32)], scratch_shapes=scratch)
    return o, lse, rode


def _attn_bwd_t(q, k, v, do, o, lse, scale, *, split, name, lcc=None, plan=None):
    S = q.shape[0]
    npair = v.shape[1] // LANES
    W = 2 * LANES if split else LANES
    T = ATT
    CH = BWD_CHUNK * T
    assert S % CH == 0
    nq = S // T

    def body(*refs):
        if split:
            (q_ref, k_ref, v_ref, do_ref, o_ref, lse_ref, dq_ref, dk_ref, dv_ref, dqt, delta, dk_acc, dv_acc) = refs
        else:
            (q_ref, k_ref, v_ref, do_ref, o_ref, lse_ref, lcc_ref, dq_ref, dk_ref, dv_ref, dlc_ref,
             dqt, delta, dk_acc, dv_acc, qaug, csum) = refs
        lane = lax.broadcasted_iota(jnp.int32, (1, LANES), 1)
        sub = lax.broadcasted_iota(jnp.int32, (LANES, 1), 0)
        key_minus_qry = lax.broadcasted_iota(jnp.int32, (T, CH), 0) - lax.broadcasted_iota(jnp.int32, (T, CH), 1)

        def prep(i, c):
            r0 = pl.multiple_of(i * T, T)
            prod_t = (do_ref[pl.ds(r0, T), :].astype(jnp.float32) * o_ref[pl.ds(r0, T), :]).T
            for h in (0, 1):
                delta[h:h + 1, pl.ds(r0, T)] = jnp.sum(jnp.where(_data_lanes(sub, h), prod_t, 0.0), axis=0, keepdims=True)
                dqt[h, :, pl.ds(r0, T)] = jnp.zeros((LANES, T), jnp.float32)
                if not split:
                    qaug[h, pl.ds(r0, T), :] = _q_aug(q_ref[pl.ds(r0, T), :], _lc_col(lcc_ref, r0, T, h), h, scale, lane)
            return c

        lax.fori_loop(0, nq, prep, 0)

        def keys(ki):
            k0 = pl.multiple_of(ki * T, T)
            kblk = k_ref[pl.ds(k0, T), :]
            if split:
                return (kblk[:, :LANES], kblk[:, LANES:])
            return tuple(_k_aug(kblk, _lc_col(lcc_ref, k0, T, h), h, lane) for h in (0, 1))

        def q_of(c, h):
            q0 = pl.multiple_of(c * CH, CH)
            if split:
                return q_ref[pl.ds(q0, CH), LANES * h:LANES * (h + 1)]
            return qaug[h, pl.ds(q0, CH), :]

        def scores(khs, c):
            out = []
            for h in (0, 1):
                st = _mm_nt(khs[h], q_of(c, h))
                out.append(st * scale if split else st)
            return tuple(out)

        def k_block(ki, carry):
            khs, first_scores = carry[:2], carry[2:]
            k0 = pl.multiple_of(ki * T, T)
            khts = [kh.astype(jnp.float32).T.astype(kh.dtype) for kh in khs]
            vhs = _split_heads(v_ref[pl.ds(k0, T), :], lane < HALF)
            dk_acc[...] = jnp.zeros_like(dk_acc)
            dv_acc[...] = jnp.zeros_like(dv_acc)

            def absorb(c, vals):
                q0 = pl.multiple_of(c * CH, CH)
                dos = _split_heads(do_ref[pl.ds(q0, CH), :], lane < HALF)
                visible = key_minus_qry <= q0 - k0
                for h in (0, 1):
                    dpt = _mm_nt(vhs[h], dos[h])
                    st = jnp.where(visible, vals[h], NEG)
                    pt = jnp.exp(st - lse_ref[0, h:h + 1, pl.ds(q0, CH)])
                    dv_acc[...] += _mm(pt, dos[h])
                    dst = pt * (dpt - delta[h:h + 1, pl.ds(q0, CH)])
                    dk_acc[h] += _mm(dst, q_of(c, h))
                    dqt[h, :, pl.ds(q0, CH)] += _mm(khts[h], dst)

            first = ki // BWD_CHUNK

            def pipelined(c, vals):
                nxt = scores(khs, c + 1)
                absorb(c, vals)
                return nxt

            vals = lax.fori_loop(first, S // CH - 1, pipelined, first_scores)
            kn = jnp.minimum(ki + 1, nq - 1)
            khs_next = keys(kn)
            nxt = khs_next + scores(khs_next, kn // BWD_CHUNK)
            absorb(S // CH - 1, vals)
            if split:
                dk_ref[pl.ds(k0, T), :LANES] = (dk_acc[0] * scale).astype(dk_ref.dtype)
                dk_ref[pl.ds(k0, T), LANES:] = (dk_acc[1] * scale).astype(dk_ref.dtype)
            else:
                dk_ref[pl.ds(k0, T), :] = jnp.where(lane < HALF, dk_acc[0], dk_acc[1]).astype(dk_ref.dtype)
                for h in (0, 1):
                    csum[h:h + 1, pl.ds(k0, T)] = dk_acc[h].T[AUG[h] + 3:AUG[h] + 4, :]
            dv_ref[pl.ds(k0, T), :] = dv_acc[...].astype(dv_ref.dtype)
            return nxt

        khs0 = keys(0)
        lax.fori_loop(0, nq, k_block, khs0 + scores(khs0, 0))

        def finish(i, c):
            r0 = pl.multiple_of(i * T, T)
            if split:
                for h in (0, 1):
                    dq_ref[pl.ds(r0, T), LANES * h:LANES * (h + 1)] = (dqt[h, :, pl.ds(r0, T)].T * scale).astype(dq_ref.dtype)
            else:
                d = jnp.where(sub < HALF, dqt[0, :, pl.ds(r0, T)], dqt[1, :, pl.ds(r0, T)])
                dq_ref[pl.ds(r0, T), :] = (d.T * scale).astype(dq_ref.dtype)
                for h in (0, 1):
                    dlc_ref[0, h:h + 1, pl.ds(r0, T)] = dqt[h, AUG[h]:AUG[h] + 1, pl.ds(r0, T)] - csum[h:h + 1, pl.ds(r0, T)]
            return c

        lax.fori_loop(0, nq, finish, 0)

    wide = pl.BlockSpec((S, W), lambda j: (0, j))
    slab = pl.BlockSpec((S, LANES), lambda j: (0, j))
    rows = pl.BlockSpec((1, 2, S), lambda j: (j, 0, 0))
    in_specs = [wide, wide, slab, slab, slab, rows]
    args = [q, k, v, do, o, lse]
    out_specs = [wide, wide, slab]
    out_shape = [_sds(q.shape, jnp.float32 if split else do.dtype), _sds(k.shape, jnp.float32 if split else do.dtype),
                 _sds(v.shape, do.dtype)]
    scratch = [pltpu.VMEM((2, LANES, S), jnp.float32), pltpu.VMEM((8, S), jnp.float32),
               pltpu.VMEM((2, T, LANES), jnp.float32), pltpu.VMEM((T, LANES), jnp.float32)]
    if not split:
        in_specs.append(_full(lcc.shape))
        args.append(lcc)
        out_specs.append(rows)
        out_shape.append(_sds((npair, 2, S), jnp.float32))
        scratch += [pltpu.VMEM((2, S, LANES), MXU), pltpu.VMEM((8, S), jnp.float32)]
    outs, rode = _pcall_riding(body, plan, args, name=name, grid=(npair,), in_specs=in_specs, out_specs=out_specs,
                               out_shape=out_shape, scratch_shapes=scratch)
    return (*outs, rode)


def _swa_bias(slope, shift):
    a = lax.broadcasted_iota(jnp.int32, (WINDOW, 2 * WINDOW), 0)
    c = lax.broadcasted_iota(jnp.int32, (WINDOW, 2 * WINDOW), 1)
    dist = a - c + shift
    return jnp.where((dist >= 0) & (dist < WINDOW), -slope * dist.astype(jnp.float32), NEG)


def _swa_scores(qh, kblk, bias):
    return _mm_nt(qh, kblk) * (HEAD ** -0.5) + bias


def _swa_stack(blk, lo):
    return jnp.concatenate(_split_heads(blk[:, :LANES], lo) + _split_heads(blk[:, LANES:], lo), axis=0)


def _swa_unstack(x, lo):
    r = x.shape[0] // 4
    return jnp.concatenate([jnp.where(lo, x[0:r], x[r:2 * r]), jnp.where(lo, x[2 * r:3 * r], x[3 * r:])], axis=1)


def _swa_per_head(ref, j, rows):
    quarter = lax.broadcasted_iota(jnp.int32, (4 * rows, 1), 0) // rows
    return jnp.where(quarter == 0, ref[4 * j], jnp.where(quarter == 1, ref[4 * j + 1],
                                                         jnp.where(quarter == 2, ref[4 * j + 2], ref[4 * j + 3])))


def _swa_fwd(q, kd, vd, sinks, slopes):
    S = q.shape[0]
    nkv = q.shape[1] // (2 * LANES)
    nb = S // WINDOW
    group = math.gcd(SWA_GROUP, nb)

    def body(sink_ref, slope_ref, q_ref, k_ref, v_ref, o_ref, lse_ref):
        j = pl.program_id(0)
        lo = _lane_masks()
        sink = _swa_per_head(sink_ref, j, WINDOW)
        biases = [jnp.concatenate([_swa_bias(slope_ref[4 * j + h], shift) for h in range(4)], axis=0)
                  for shift in (0, WINDOW)]

        def q_block(qi, c):
            q0 = pl.multiple_of(qi * WINDOW, WINDOW)
            k0 = pl.multiple_of(jnp.maximum(qi - 1, 0) * WINDOW, WINDOW)
            s = _swa_scores(_swa_stack(q_ref[pl.ds(q0, WINDOW), :], lo), k_ref[pl.ds(k0, 2 * WINDOW), :],
                            jnp.where(qi == 0, *biases))
            m = jnp.maximum(jnp.max(s, axis=1, keepdims=True), sink)
            p = jnp.exp(s - m)
            den = jnp.sum(p, axis=1, keepdims=True) + jnp.exp(sink - m)
            o_ref[pl.ds(q0, WINDOW), :] = _swa_unstack(_mm(p / den, v_ref[pl.ds(k0, 2 * WINDOW), :]), lo)
            lse = m + jnp.log(den)
            for h in range(4):
                lse_ref[h, pl.ds(q0, WINDOW), :] = lse[h * WINDOW:(h + 1) * WINDOW]
            return c

        def q_group(gi, c):
            for g in range(group):
                q_block(gi * group + g, c)
            return c

        lax.fori_loop(0, nb // group, q_group, 0)

    smem = pl.BlockSpec(memory_space=pltpu.SMEM)
    two = pl.BlockSpec((S, 2 * LANES), lambda j: (0, j))
    kv = pl.BlockSpec((S, LANES), lambda j: (0, 2 * j))
    return _pcall(
        body, name="swa_fwd", grid=(nkv,), semantics=("arbitrary",),
        in_specs=[smem, smem, two, kv, kv],
        out_specs=[two, pl.BlockSpec((4, S, 1), lambda j: (j, 0, 0))],
        out_shape=[_sds(q.shape, jnp.float32), _sds((4 * nkv, S, 1), jnp.float32)],
    )(sinks, slopes, q, kd, vd)


def _swa_bwd(q, kd, vd, do, o, lse, sinks, slopes, plan=None):
    S = q.shape[0]
    nkv = q.shape[1] // (2 * LANES)
    nb = S // WINDOW
    group = math.gcd(SWA_GROUP, nb)

    def body(sink_ref, slope_ref, q_ref, k_ref, v_ref, do_ref, o_ref, lse_ref,
             dq_ref, dk_ref, dv_ref, dsink_ref, dk_acc, dv_acc):
        j = pl.program_id(0)
        lo = _lane_masks()
        dk_acc[...] = jnp.zeros_like(dk_acc)
        dv_acc[...] = jnp.zeros_like(dv_acc)
        sink = _swa_per_head(sink_ref, j, WINDOW)
        biases = [jnp.concatenate([_swa_bias(slope_ref[4 * j + h], shift) for h in range(4)], axis=0)
                  for shift in (0, WINDOW)]

        def q_block(qi, carry):
            q0 = pl.multiple_of(qi * WINDOW, WINDOW)
            k0 = pl.multiple_of(jnp.maximum(qi - 1, 0) * WINDOW, WINDOW)
            q4 = _swa_stack(q_ref[pl.ds(q0, WINDOW), :], lo)
            do4 = _swa_stack(do_ref[pl.ds(q0, WINDOW), :], lo)
            oblk = o_ref[pl.ds(q0, WINDOW), :]
            o4 = jnp.concatenate([oblk[:, :LANES], oblk[:, :LANES], oblk[:, LANES:], oblk[:, LANES:]], axis=0)
            kblk = k_ref[pl.ds(k0, 2 * WINDOW), :]
            vblk = v_ref[pl.ds(k0, 2 * WINDOW), :]
            lse = jnp.concatenate([lse_ref[h, pl.ds(q0, WINDOW), :] for h in range(4)], axis=0)
            p = jnp.exp(_swa_scores(q4, kblk, jnp.where(qi == 0, *biases)) - lse)
            delta = jnp.sum(do4.astype(jnp.float32) * o4, axis=1, keepdims=True)
            dv_acc[pl.ds(k0, 2 * WINDOW), :] += _mm_tn(p, do4)
            ds = p * (_mm_nt(do4, vblk) - delta)
            dq_ref[pl.ds(q0, WINDOW), :] = _swa_unstack(_mm(ds, kblk) * (HEAD ** -0.5), lo).astype(dq_ref.dtype)
            dk_acc[pl.ds(k0, 2 * WINDOW), :] += _mm_tn(ds, q4) * (HEAD ** -0.5)
            dsk = -jnp.exp(sink - lse) * delta
            return tuple(carry[h] + jnp.sum(dsk[h * WINDOW:(h + 1) * WINDOW], axis=0, keepdims=True)
                         for h in range(4))

        def q_group(gi, carry):
            for g in range(group):
                carry = q_block(gi * group + g, carry)
            return carry

        zero = jnp.zeros((1, 1), jnp.float32)
        dsinks = lax.fori_loop(0, nb // group, q_group, (zero,) * 4)
        dk_ref[:, :LANES] = dk_acc[...].astype(dk_ref.dtype)
        dk_ref[:, LANES:] = jnp.zeros((S, LANES), dk_ref.dtype)
        dv_ref[:, :LANES] = dv_acc[...].astype(dv_ref.dtype)
        dv_ref[:, LANES:] = jnp.zeros((S, LANES), dv_ref.dtype)
        r = lax.broadcasted_iota(jnp.int32, (8, LANES), 0)
        dsink_ref[0] = jnp.where(r == 0, dsinks[0], jnp.where(r == 1, dsinks[1], jnp.where(r == 2, dsinks[2],
                                 jnp.where(r == 3, dsinks[3], 0.0))))

    smem = pl.BlockSpec(memory_space=pltpu.SMEM)
    two = pl.BlockSpec((S, 2 * LANES), lambda j: (0, j))
    kv = pl.BlockSpec((S, LANES), lambda j: (0, 2 * j))
    outs, rode = _pcall_riding(
        body, plan, [sinks, slopes, q, kd, vd, do, o, lse], name="swa_bwd", grid=(nkv,),
        in_specs=[smem, smem, two, kv, kv, two, two, pl.BlockSpec((4, S, 1), lambda j: (j, 0, 0))],
        out_specs=[two, two, two, pl.BlockSpec((1, 8, LANES), lambda j: (j, 0, 0))],
        out_shape=[_sds(q.shape, do.dtype), _sds(kd.shape, do.dtype), _sds(vd.shape, do.dtype),
                   _sds((nkv, 8, LANES), jnp.float32)],
        scratch_shapes=[pltpu.VMEM((S, LANES), jnp.float32), pltpu.VMEM((S, LANES), jnp.float32)])
    return (*outs, rode)


def _log_steps(S):
    k, out = 1, []
    while k < S:
        out.append(k)
        k *= 2
    return out


def _forget_fwd(f_row, b_col):
    S = f_row.shape[1]

    def body(f_ref, b_ref, lc_ref):
        x = f_ref[...] + b_ref[...]
        lc = jnp.minimum(x, 0.0) - jnp.log(1.0 + jnp.exp(-jnp.abs(x)))
        idx = lax.broadcasted_iota(jnp.int32, lc.shape, 1)
        for k in _log_steps(S):
            lc = lc + jnp.where(idx >= k, pltpu.roll(lc, k, axis=1), 0.0)
        lc_ref[...] = lc

    return _pcall(body, name="forget_fwd", out_shape=_sds(f_row.shape, jnp.float32))(f_row, b_col)


def _forget_bwd(dlc_row, f_row, b_col):
    S = f_row.shape[1]

    def body(d_ref, f_ref, b_ref, df_ref, db_ref):
        g = d_ref[...]
        idx = lax.broadcasted_iota(jnp.int32, g.shape, 1)
        for k in _log_steps(S):
            g = g + jnp.where(idx < S - k, pltpu.roll(g, S - k, axis=1), 0.0)
        x = f_ref[...] + b_ref[...]
        df = g * _sigmoid(-x)
        df_ref[...] = df
        db_ref[...] = jnp.sum(df, axis=1, keepdims=True)

    return _pcall(body, name="forget_bwd",
                  out_shape=[_sds(f_row.shape, jnp.float32), _sds((f_row.shape[0], 1), jnp.float32)])(dlc_row, f_row, b_col)


def _layer0_out_layer1_in(x, o_m, o_s, gate, w_out, g1, w_in1):
    S = x.shape[0]

    def body(x_ref, om_ref, os_ref, gate_ref, wo_ref, g_ref, w_ref,
             x1_ref, h_ref, q_ref, k_ref, v_ref, g1_ref, f_ref):
        gt = gate_ref[...]
        sg = gt * _sigmoid(gt)
        um = om_ref[...] * sg[:, :512]
        us = os_ref[...] * sg[:, 512:]
        x1 = x_ref[...] + _mm(um, wo_ref[0:512, :]) + _mm(us, wo_ref[512:1024, :])
        x1_ref[...] = x1
        h = _rms(x1, g_ref[...])
        h_ref[...] = h.astype(h_ref.dtype)
        z = _mm_nt(h, w_ref[...])
        q_ref[...] = z[:, 0:1024].astype(q_ref.dtype)
        k_ref[...] = z[:, 1024:2048].astype(k_ref.dtype)
        v_ref[...] = z[:, 2048:3072].astype(v_ref.dtype)
        g1_ref[...] = z[:, 3072:4096]
        f_ref[...] = z[:, 4096:4224]

    outs = [((S, D), jnp.float32), ((S, D), MXU), ((S, D), MXU), ((S, D), MXU), ((S, D), MXU),
            ((S, D), jnp.float32), ((S, LANES), jnp.float32)]
    return _pcall(
        body, name="layer0_out_layer1_in", grid=(S // TOK,), semantics=("arbitrary",),
        in_specs=[_rows(TOK, D), _rows(TOK, 512), _rows(TOK, 512), _rows(TOK, D), _full((D, D)), _full((1, D)),
                  _full(w_in1.shape)],
        out_specs=[_rows(TOK, s[1]) for s, _ in outs],
        out_shape=[_sds(s, d) for s, d in outs],
    )(x, o_m, o_s, gate, w_out, g1, w_in1)


def _head(x1, o1, gate1, w_out1, g_f, target):
    S = x1.shape[0]

    def body(x1_ref, o_ref, gate_ref, wo_ref, g_ref, t_ref,
             loss_ref, dgf_ref, dwo_ref, dx2_ref, do_ref, dgate_ref):
        i = pl.program_id(0)
        gt = gate_ref[...]
        sig = _sigmoid(gt)
        sg = gt * sig
        o = o_ref[...]
        u = o * sg
        x2 = x1_ref[...] + _mm(u, wo_ref[...])
        g = g_ref[...]
        y = _rms(x2, g)
        err = y - t_ref[...]
        part = 0.5 * jnp.sum(jnp.mean(err * err, axis=-1, keepdims=True), axis=0, keepdims=True)
        dy = err * (1.0 / D)
        dx2, dg_rows = _rms_bwd(x2, g, dy)
        dx2_ref[...] = dx2
        du = _mm_nt(dx2, wo_ref[...])
        do_ref[...] = (du * sg).astype(do_ref.dtype)
        dgate_ref[...] = (du * o * (sig * (1.0 + gt * (1.0 - sig)))).astype(dgate_ref.dtype)

        @pl.when(i == 0)
        def _():
            loss_ref[...] = jnp.zeros_like(loss_ref)
            dgf_ref[...] = jnp.zeros_like(dgf_ref)
            dwo_ref[...] = jnp.zeros_like(dwo_ref)

        loss_ref[...] += jnp.broadcast_to(part, loss_ref.shape)
        dgf_ref[...] += jnp.sum(dg_rows, axis=0, keepdims=True)
        dwo_ref[...] += _mm_tn(u, dx2)

    outs = [((S, D), jnp.float32), ((S, D), MXU), ((S, D), MXU)]
    return _pcall(
        body, name="head", grid=(S // TOK,), semantics=("arbitrary",),
        in_specs=[_rows(TOK, D), _rows(TOK, D), _rows(TOK, D), _full((D, D)), _full((1, D)), _rows(TOK, D)],
        out_specs=[_full((8, LANES)), _full((1, D)), _full((D, D))] + [_rows(TOK, D) for _ in outs],
        out_shape=[_sds((8, LANES), jnp.float32), _sds((1, D), jnp.float32), _sds((D, D), jnp.float32)]
        + [_sds(s, d) for s, d in outs],
    )(x1, o1, gate1, w_out1, g_f, target)


def _layer1_in_bwd(dq, dk, dv, dgate1, df, x1, dx2, g1, w_in1, gate0, o_m, o_s, w_out0):
    S = x1.shape[0]

    def body(dq_ref, dk_ref, dv_ref, dg1_ref, df_ref, x1_ref, dx2_ref, g_ref, w_ref, gate_ref, om_ref, os_ref,
             wo_ref, dz_ref, dx1_ref, dgn_ref, dwo_ref, dom_ref, dos_ref, dgate_ref):
        i = pl.program_id(0)
        dz_ref[:, 0:1024] = dq_ref[...]
        dz_ref[:, 1024:2048] = dk_ref[...]
        dz_ref[:, 2048:3072] = dv_ref[...]
        dz_ref[:, 3072:4096] = dg1_ref[...]
        dz_ref[:, 4096:4224] = df_ref[...]
        dh = _mm(dz_ref[...], w_ref[...])
        g = g_ref[...]
        dxn, dg_rows = _rms_bwd(x1_ref[...], g, dh)
        dx1 = dx2_ref[...] + dxn
        dx1_ref[...] = dx1
        du = _mm_nt(dx1, wo_ref[...])
        gt = gate_ref[...]
        sig = _sigmoid(gt)
        sg = gt * sig
        dsg = sig * (1.0 + gt * (1.0 - sig))
        dom_ref[...] = (du[:, :512] * sg[:, :512]).astype(dom_ref.dtype)
        dos_ref[...] = (du[:, 512:] * sg[:, 512:]).astype(dos_ref.dtype)
        dgate_ref[:, :512] = (du[:, :512] * om_ref[...] * dsg[:, :512]).astype(dgate_ref.dtype)
        dgate_ref[:, 512:] = (du[:, 512:] * os_ref[...] * dsg[:, 512:]).astype(dgate_ref.dtype)

        @pl.when(i == 0)
        def _():
            dgn_ref[...] = jnp.zeros_like(dgn_ref)
            dwo_ref[...] = jnp.zeros_like(dwo_ref)

        dgn_ref[...] += jnp.sum(dg_rows, axis=0, keepdims=True)
        dwo_ref[0:512, :] += _mm_tn(om_ref[...] * sg[:, :512], dx1)
        dwo_ref[512:1024, :] += _mm_tn(os_ref[...] * sg[:, 512:], dx1)

    return _pcall(
        body, name="layer1_in_bwd", grid=(S // TOK,), semantics=("arbitrary",),
        in_specs=[_rows(TOK, D), _rows(TOK, D), _rows(TOK, D), _rows(TOK, D), _rows(TOK, LANES), _rows(TOK, D),
                  _rows(TOK, D), _full((1, D)), _full(w_in1.shape), _rows(TOK, D), _rows(TOK, 512), _rows(TOK, 512),
                  _full((D, D))],
        out_specs=[_rows(TOK, 4224), _rows(TOK, D), _full((1, D)), _full((D, D)), _rows(TOK, 512), _rows(TOK, 512),
                   _rows(TOK, D)],
        out_shape=[_sds((S, 4224), MXU), _sds((S, D), jnp.float32), _sds((1, D), jnp.float32), _sds((D, D), jnp.float32),
                   _sds((S, 512), MXU), _sds((S, 512), MXU), _sds((S, D), MXU)],
    )(dq, dk, dv, dgate1, df, x1, dx2, g1, w_in1, gate0, o_m, o_s, w_out0)


def _layer0_in_bwd(dqm, dkm, dvm, dqs, dkd, dvd, dgate0, cos, sin, cq, ckv, x, dx1, g_in, w_in, g_q, w_q, g_kv, w_kv,
                   plan=None):
    S = x.shape[0]
    consts = _rope_consts()

    def body(dqm_ref, dkm_ref, dvm_ref, dqs_ref, dkd_ref, dvd_ref, dgate_ref, cos_ref, sin_ref, c_ref, cq_ref, ckv_ref,
             x_ref, dx1_ref, g_ref, w_ref, gq_ref, wq_ref, gkv_ref, wkv_ref,
             dx_ref, dz_ref, dgin_ref, dgq_ref, dgkv_ref, dwq_ref, dwkv_ref, dqu_ref, dkvu_ref):
        i = pl.program_id(0)
        lo = _lane_masks()
        sign = c_ref[...][1:2, :]
        c = cos_ref[...]
        s = sin_ref[...]
        dkpe = None
        for hd in range(N_MLA):
            sl = slice(LANES * hd, LANES * (hd + 1))
            dqu_ref[:, sl] = _rope_t(dqm_ref[:, sl], c, s, sign).astype(dqu_ref.dtype)
            dkh = dkm_ref[:, sl]
            dkvu_ref[:, sl] = jnp.where(lo, dkh, 0.0).astype(dkvu_ref.dtype)
            dkpe = dkh if dkpe is None else dkpe + dkh
        dkvu_ref[:, 1024:1536] = dvm_ref[...]
        dkpe = _rope_t(jnp.where(lo, 0.0, dkpe), c, s, sign)
        dcqn = _mm(dqu_ref[...], wq_ref[...])
        dckvn = _mm_nt(dkvu_ref[...], wkv_ref[...])
        gq = gq_ref[...]
        gkv = gkv_ref[...]
        dcq, dgq_rows = _rms_bwd(cq_ref[...], gq, dcqn)
        dckv, dgkv_rows = _rms_bwd(ckv_ref[...], gkv, dckvn)
        dz_ref[:, 0:256] = dcq.astype(dz_ref.dtype)
        dz_ref[:, 256:384] = dckv.astype(dz_ref.dtype)
        dz_ref[:, 384:512] = dkpe.astype(dz_ref.dtype)
        dz_ref[:, 512:1024] = dqs_ref[...]
        dz_ref[:, 1024:1536] = dkd_ref[...]
        dz_ref[:, 1536:2048] = dvd_ref[...]
        dz_ref[:, 2048:3072] = dgate_ref[...]
        dh = _mm(dz_ref[...], w_ref[...])
        g = g_ref[...]
        dxn, dg_rows = _rms_bwd(x_ref[...], g, dh)
        dx_ref[...] = dx1_ref[...] + dxn

        @pl.when(i == 0)
        def _():
            dgin_ref[...] = jnp.zeros_like(dgin_ref)
            dgq_ref[...] = jnp.zeros_like(dgq_ref)
            dgkv_ref[...] = jnp.zeros_like(dgkv_ref)
            dwq_ref[...] = jnp.zeros_like(dwq_ref)
            dwkv_ref[...] = jnp.zeros_like(dwkv_ref)

        dgin_ref[...] += jnp.sum(dg_rows, axis=0, keepdims=True)
        dgq_ref[...] += jnp.sum(dgq_rows, axis=0, keepdims=True)
        dgkv_ref[...] += jnp.sum(dgkv_rows, axis=0, keepdims=True)
        dwq_ref[...] += _mm_tn(dqu_ref[...], _rms(cq_ref[...], gq))
        dwkv_ref[...] += _mm_tn(_rms(ckv_ref[...], gkv), dkvu_ref[...])

    outs, rode = _pcall_riding(
        body, plan, [dqm, dkm, dvm, dqs, dkd, dvd, dgate0, cos, sin, consts, cq, ckv, x, dx1, g_in, w_in, g_q, w_q, g_kv, w_kv],
        name="layer0_in_bwd", grid=(S // TOK,),
        in_specs=[_rows(TOK, 1024), _rows(TOK, 1024), _rows(TOK, 512), _rows(TOK, 512), _rows(TOK, 512), _rows(TOK, 512),
                  _rows(TOK, D), _rows(TOK, LANES), _rows(TOK, LANES), _full((8, LANES)), _rows(TOK, 256), _rows(TOK, 128),
                  _rows(TOK, D), _rows(TOK, D), _full((1, D)), _full(w_in.shape), _full((1, 256)), _full(w_q.shape),
                  _full((1, 128)), _full(w_kv.shape)],
        out_specs=[_rows(TOK, D), _rows(TOK, 3072), _full((1, D)), _full((1, 256)), _full((1, 128)), _full(w_q.shape),
                   _full(w_kv.shape)],
        out_shape=[_sds((S, D), jnp.float32), _sds((S, 3072), MXU), _sds((1, D), jnp.float32), _sds((1, 256), jnp.float32),
                   _sds((1, 128), jnp.float32), _sds(w_q.shape, jnp.float32), _sds(w_kv.shape, jnp.float32)],
        scratch_shapes=[pltpu.VMEM((TOK, 1024), MXU), pltpu.VMEM((TOK, 1536), MXU)])
    return (*outs, rode)


def _wgrad(a, b, name):
    S, M = a.shape
    N = b.shape[1]
    tm = next(t for t in range(WG_ROWS, 0, -LANES) if M % t == 0)
    tn = N if N <= 1024 else 512
    tk = min(WG_TOK, S)

    def body(a_ref, b_ref, o_ref):
        @pl.when(pl.program_id(2) == 0)
        def _():
            o_ref[...] = jnp.zeros_like(o_ref)

        o_ref[...] += _mm_tn(a_ref[...], b_ref[...])

    return _pcall(
        body, name=name, grid=(M // tm, N // tn, S // tk), semantics=("parallel", "parallel", "arbitrary"),
        in_specs=[pl.BlockSpec((tk, tm), lambda m, n, k: (k, m)), pl.BlockSpec((tk, tn), lambda m, n, k: (k, n))],
        out_specs=pl.BlockSpec((tm, tn), lambda m, n, k: (m, n)),
        out_shape=_sds((M, N), jnp.float32),
    )(a, b)


def _adamw(w, g, m, v, name):
    shape = w.shape
    R, C = (int(np.prod(shape[:-1])), shape[-1])
    w2, g2, m2, v2 = (t.reshape(R, C) for t in (w, g, m, v))
    fits = [t for t in range(8, ADAM_TILE_BYTES // (4 * C) + 1, 8) if R % t == 0]
    tr = max(fits) if fits else R
    tc = C if (tr * C * 4 <= ADAM_TILE_BYTES or C % 256) else 256

    def body(w_ref, g_ref, m_ref, v_ref, d_ref, nm_ref, nv_ref):
        gg = g_ref[...]
        nm = B1 * m_ref[...] + (1.0 - B1) * gg
        nv = B2 * v_ref[...] + (1.0 - B2) * (gg * gg)
        m_hat = nm / (1.0 - B1 ** STEP)
        v_hat = nv / (1.0 - B2 ** STEP)
        d_ref[...] = -LR * (m_hat / (jnp.sqrt(v_hat) + AEPS) + WD * w_ref[...])
        nm_ref[...] = nm
        nv_ref[...] = nv

    spec = pl.BlockSpec((tr, tc), lambda i, j: (i, j))
    d, nm, nv = _pcall(
        body, name=name, grid=(R // tr, C // tc), semantics=("parallel", "parallel"),
        in_specs=[spec] * 4, out_specs=[spec] * 3, out_shape=[_sds((R, C), jnp.float32)] * 3,
    )(w2, g2, m2, v2)
    return d.reshape(shape), nm.reshape(shape), nv.reshape(shape)


def _sum_leading(a, name):
    n, R, C = a.shape
    tr = SUM_ROWS if R % SUM_ROWS == 0 else R

    def body(a_ref, o_ref):
        acc = a_ref[0]
        for i in range(1, n):
            acc = acc + a_ref[i]
        o_ref[...] = acc

    return _pcall(
        body, name=name, grid=(R // tr,), semantics=("parallel",),
        in_specs=[pl.BlockSpec((n, tr, C), lambda i: (0, i, 0))], out_specs=_rows(tr, C),
        out_shape=_sds((R, C), a.dtype),
    )(a)


def _add_halves(g, c, b, name, out_dtype):
    n, _, R, C = g.shape
    tr = SUM_ROWS if R % SUM_ROWS == 0 else R

    def body(c_ref, a_ref, b_ref, o_ref):
        o_ref[...] = (a_ref[0] + b_ref[...]).astype(o_ref.dtype)

    spec = pl.BlockSpec((1, tr, C), lambda k, i, c_ref: (k, i, 0))
    grid_spec = pltpu.PrefetchScalarGridSpec(
        num_scalar_prefetch=1, grid=(n, R // tr),
        in_specs=[pl.BlockSpec((1, 1, tr, C), lambda k, i, c_ref: (k, c_ref[0], i, 0)), spec], out_specs=spec)
    return _pcall(body, name=name, semantics=("parallel", "parallel"), grid_spec=grid_spec,
                  out_shape=_sds(b.shape, out_dtype))(c.reshape(1).astype(jnp.int32), g, b)


def _total_sum(g, theirs, chip, c, recv, name):
    _, _, R, C = g.shape
    n = recv.shape[0]
    tr = SUM_ROWS if R % SUM_ROWS == 0 else R

    def body(at_ref, a_ref, b_ref, r_ref, o_ref):
        acc = a_ref[0, 0] + b_ref[0]
        for i in range(n):
            acc = acc + r_ref[i].astype(jnp.float32)
        o_ref[...] = acc

    grid_spec = pltpu.PrefetchScalarGridSpec(
        num_scalar_prefetch=1, grid=(R // tr,),
        in_specs=[pl.BlockSpec((1, 1, tr, C), lambda i, at_ref: (at_ref[0], at_ref[1], i, 0)),
                  pl.BlockSpec((1, tr, C), lambda i, at_ref: (at_ref[0], i, 0)),
                  pl.BlockSpec((n, tr, C), lambda i, at_ref: (0, i, 0))],
        out_specs=pl.BlockSpec((tr, C), lambda i, at_ref: (i, 0)))
    return _pcall(body, name=name, semantics=("parallel",), grid_spec=grid_spec,
                  out_shape=_sds((R, C), jnp.float32))(jnp.stack([chip, c]).astype(jnp.int32), g, theirs, recv)


def _place():
    return lax.axis_index("x"), lax.axis_index("y"), lax.axis_index("c")


class _Plan:
    def __init__(self, arrays, out_shape, scratch, start, finish, middle=None):
        self.arrays, self.out_shape, self.scratch = list(arrays), list(out_shape), list(scratch)
        self.start, self.finish, self.middle = start, finish, middle


def _gather8_plan(block):
    R, C = block.shape

    def parts(ins, outs, sems):
        (x_ref,), (out_ref,), (send_sems, recv_sems) = ins, outs, sems
        x, y, c = _place()
        me, sibling = (x, y, c), (x, y, 1 - c)
        chips = [(1 - x, y), (x, 1 - y), (1 - x, 1 - y)]

        def copy(k, blk, to, src=None):
            slot = out_ref.at[4 * blk[0] + 2 * blk[1] + blk[2]]
            return pltpu.make_async_remote_copy(
                src_ref=slot if src is None else src, dst_ref=slot,
                send_sem=send_sems.at[k], recv_sem=recv_sems.at[k], device_id=to, device_id_type=MESH_ID)

        def first():
            return [copy(0, me, sibling, src=x_ref)] + [copy(1 + j, me, (*chip, c), src=x_ref) for j, chip in enumerate(chips)]

        def passed():
            return [copy(4 + j, (*chip, c), sibling) for j, chip in enumerate(chips)]

        def arrivals():
            return [copy(1 + j, (*chip, c), me) for j, chip in enumerate(chips)]

        def late():
            return [copy(0, sibling, me)] + [copy(4 + j, (*chip, 1 - c), me) for j, chip in enumerate(chips)]

        return first, passed, arrivals, late

    def start(ins, outs, sems):
        for cp in parts(ins, outs, sems)[0]():
            cp.start()

    def middle(ins, outs, sems):
        _, passed, arrivals, _ = parts(ins, outs, sems)
        for arrived, forward in zip(arrivals(), passed()):
            arrived.wait_recv()
            forward.start()

    def finish(ins, outs, sems):
        first, passed, _, late = parts(ins, outs, sems)
        for cp in late():
            cp.wait_recv()
        for cp in first() + passed():
            cp.wait_send()

    return _Plan([block], [_sds((8, R, C), block.dtype)], [pltpu.SemaphoreType.DMA((7,)), pltpu.SemaphoreType.DMA((7,))],
                 start, finish, middle)


def _fill_own_slot(gathered, block):
    x, y, c = _place()
    return lax.dynamic_update_index_in_dim(gathered, block, 4 * x + 2 * y + c, 0)


def _started_and_waited(arrays, out_shape, n, copies):
    def start(ins, outs, sems):
        for cp in copies(ins, outs, sems):
            cp.start()

    def finish(ins, outs, sems):
        for cp in copies(ins, outs, sems):
            cp.wait()

    return _Plan(arrays, out_shape, [pltpu.SemaphoreType.DMA((n,)), pltpu.SemaphoreType.DMA((n,))], start, finish)


def _pair_swap_plan(g):
    n = g.shape[0]

    def copies(ins, outs, sems):
        (g_ref,), (out_ref,), (send_sems, recv_sems) = ins, outs, sems
        x, y, c = _place()
        return [pltpu.make_async_remote_copy(src_ref=g_ref.at[k, 1 - c], dst_ref=out_ref.at[k], send_sem=send_sems.at[k],
                                             recv_sem=recv_sems.at[k], device_id=(x, y, 1 - c), device_id_type=MESH_ID)
                for k in range(n)]

    return _started_and_waited([g], [_sds((n,) + g.shape[2:], g.dtype)], n, copies)


def _chip_exchange_plan(p):
    def copies(ins, outs, sems):
        (p_ref,), (out_ref,), (send_sems, recv_sems) = ins, outs, sems
        x, y, c = _place()
        chips = [(1 - x, y), (x, 1 - y), (1 - x, 1 - y)]
        return [pltpu.make_async_remote_copy(
            src_ref=p_ref.at[2 * cx + cy], dst_ref=out_ref.at[j], send_sem=send_sems.at[j],
            recv_sem=recv_sems.at[j], device_id=(cx, cy, c), device_id_type=MESH_ID)
            for j, (cx, cy) in enumerate(chips)]

    return _started_and_waited([p], [_sds((3,) + p.shape[1:], p.dtype)], 3, copies)


def _pair_exchange_plan(t):
    def copies(ins, outs, sems):
        (t_ref,), (out_ref,), (send_sems, recv_sems) = ins, outs, sems
        x, y, c = _place()
        return [pltpu.make_async_remote_copy(src_ref=t_ref, dst_ref=out_ref, send_sem=send_sems.at[0], recv_sem=recv_sems.at[0],
                                             device_id=(x, y, 1 - c), device_id_type=MESH_ID)]

    return _started_and_waited([t], [_sds(t.shape, t.dtype)], 1, copies)


def _both_plans(a, b):
    na, ma, sa = len(a.arrays), len(a.out_shape), len(a.scratch)

    def phase(name):
        fa, fb = getattr(a, name), getattr(b, name)
        if fa is None and fb is None:
            return None

        def run(ins, outs, sems):
            if fa is not None:
                fa(ins[:na], outs[:ma], sems[:sa])
            if fb is not None:
                fb(ins[na:], outs[ma:], sems[sa:])
        return run

    return _Plan(a.arrays + b.arrays, a.out_shape + b.out_shape, a.scratch + b.scratch,
                 phase("start"), phase("finish"), phase("middle"))


ANY_SPEC = pl.BlockSpec(memory_space=pl.ANY)


def _run_plan(plan, name):
    n_in, n_out = len(plan.arrays), len(plan.out_shape)

    def body(*refs):
        ins, outs, sems = refs[:n_in], refs[n_in:n_in + n_out], refs[n_in + n_out:]
        plan.start(ins, outs, sems)
        if plan.middle is not None:
            plan.middle(ins, outs, sems)
        plan.finish(ins, outs, sems)

    return _pcall(body, name=name, in_specs=[ANY_SPEC] * n_in, out_specs=[ANY_SPEC] * n_out, out_shape=plan.out_shape,
                  scratch_shapes=plan.scratch)(*plan.arrays)


def _pcall_riding(body, plan, args, *, name, grid, in_specs, out_specs, out_shape, scratch_shapes):
    if plan is None:
        outs = _pcall(body, name=name, grid=grid, semantics=("arbitrary",), in_specs=in_specs, out_specs=out_specs,
                      out_shape=out_shape, scratch_shapes=scratch_shapes)(*args)
        return list(outs), None
    n_in, n_out, n_s = len(args), len(out_shape), len(scratch_shapes)
    p_in, p_out = len(plan.arrays), len(plan.out_shape)
    steps = grid[0]

    def riding(*refs):
        ins, pins = refs[:n_in], refs[n_in:n_in + p_in]
        o0 = n_in + p_in
        outs, pouts = refs[o0:o0 + n_out], refs[o0 + n_out:o0 + n_out + p_out]
        s0 = o0 + n_out + p_out
        scr, sems = refs[s0:s0 + n_s], refs[s0 + n_s:]
        j = pl.program_id(0)

        @pl.when(j == 0)
        def _():
            plan.start(pins, pouts, sems)

        if plan.middle is not None:
            @pl.when(j == steps // 2)
            def _():
                plan.middle(pins, pouts, sems)

        body(*ins, *outs, *scr)

        @pl.when(j == steps - 1)
        def _():
            plan.finish(pins, pouts, sems)

    res = _pcall(riding, name=name, grid=grid, semantics=("arbitrary",), in_specs=list(in_specs) + [ANY_SPEC] * p_in,
                 out_specs=list(out_specs) + [ANY_SPEC] * p_out, out_shape=list(out_shape) + plan.out_shape,
                 scratch_shapes=list(scratch_shapes) + plan.scratch)(*args, *plan.arrays)
    return list(res[:n_out]), list(res[n_out:])


class _RowSeq:
    def __init__(self, pieces):
        self.pieces = list(pieces)

    def rows(self, a, b):
        out, off = [], 0
        for p in self.pieces:
            lo, hi = max(a, off), min(b, off + p.shape[0])
            if lo < hi:
                out.append(p[lo - off:hi - off])
            off += p.shape[0]
        return out

    def array(self):
        return jnp.concatenate(self.pieces, axis=0)


def _row_seq(w):
    return w if isinstance(w, _RowSeq) else _RowSeq([w])


def _prep_w_in0(wt):
    wt = _row_seq(wt)
    one = wt.pieces[0]
    z32 = [jnp.zeros((32, one.shape[1]), one.dtype)]
    k0, k1 = wt.rows(928, 992), wt.rows(992, 1056)
    v0, v1 = wt.rows(1056, 1120), wt.rows(1120, 1184)
    return jnp.concatenate(wt.rows(0, 384) + z32 + z32 + wt.rows(384, 416) + z32 + wt.rows(416, 928)
                           + k0 * 4 + k1 * 4 + v0 * 4 + v1 * 4 + wt.rows(1184, 2208), axis=0)


def _fold_w_in0(d):
    def fold(blk):
        b = blk.reshape(8, 64, blk.shape[1])
        return jnp.concatenate([b[0] + b[1] + b[2] + b[3], b[4] + b[5] + b[6] + b[7]], axis=0)
    return _RowSeq([d[0:384], d[448:480], d[512:1024], fold(d[1024:1536]), fold(d[1536:2048]), d[2048:3072]])


def _prep_w_q(wt):
    return jnp.pad(wt.reshape(N_MLA, 96, Q_RANK), ((0, 0), (0, 32), (0, 0))).reshape(1024, Q_RANK)


def _fold_w_q(d):
    return d.reshape(N_MLA, 128, Q_RANK)[:, :96].reshape(768, Q_RANK)


def _prep_w_kv(w):
    w3 = w.reshape(KV_RANK, N_MLA, 128)
    kk = jnp.pad(w3[:, :, :64], ((0, 0), (0, 0), (0, 64))).reshape(KV_RANK, 1024)
    return jnp.concatenate([kk, w3[:, :, 64:].reshape(KV_RANK, 512)], axis=1)


def _fold_w_kv(d):
    kk = d[:, :1024].reshape(KV_RANK, N_MLA, 128)[:, :, :64]
    vv = d[:, 1024:].reshape(KV_RANK, N_MLA, 64)
    return jnp.concatenate([kk, vv], axis=2).reshape(KV_RANK, 1024)


W_IN1_SHARD = 1028
W_IN1_STEP = W_IN1_SHARD % 16


class _ShiftedShards:
    def __init__(self, blocks):
        self.blocks = list(blocks)


def _shifted_shard(a, chip, rows):
    out = jnp.zeros((rows, a.shape[1]), a.dtype)
    for k in range(4):
        out = jnp.where(chip == k, jnp.pad(a, ((W_IN1_STEP * k, rows - W_IN1_STEP * k - a.shape[0]), (0, 0))), out)
    return out


def _prep_w_in1(wt):
    if not isinstance(wt, _ShiftedShards):
        return jnp.concatenate([wt[0:3072], wt[3088:4112], wt[3072:3088], jnp.zeros((112, wt.shape[1]), wt.dtype)], axis=0)
    b = wt.blocks
    row = lax.broadcasted_iota(jnp.int32, (16, 1), 0)

    def seam(k, first, second):
        return jnp.where(row < W_IN1_STEP * (k + 1), first, second)

    return jnp.concatenate([
        b[0][0:1024], seam(0, b[0][1024:1040], b[1][0:16]), b[1][16:1024], seam(1, b[1][1024:1040], b[2][0:16]),
        b[2][16:1024], b[3][16:1040], seam(2, b[2][1024:1040], b[3][0:16]), jnp.zeros((112, 1024), b[0].dtype)], axis=0)


def _fold_w_in1(d):
    return _RowSeq([d[0:3072], d[4096:4112], d[3072:4096]])


class _Alone:
    def __init__(self, w_out0, o_g_in, w_in1, w_out1):
        self.layer1 = (w_out0, o_g_in, w_in1, w_out1)

    def gather_plan(self):
        return None

    def layer1_weights(self, rode):
        return self.layer1

    def swap_plan(self, grads1):
        return None

    def exchange_plan(self, rode):
        return None

    def halves_plan(self, rode):
        return None

    def finish(self, rode):
        pass


def _local_step(x, pos, target, e_g_in, w_in0, e_g_q, w_q, e_g_kv, w_kv, sinks, b_f, g_final, layer1):
    S = x.shape[0]
    w_in0p, w_qp, w_kvp = _prep_w_in0(w_in0), _prep_w_q(w_q), _prep_w_kv(w_kv)
    slopes = jnp.asarray(2.0 ** (-8.0 * (np.arange(N_SWA, dtype=np.float32) + 1.0) / N_SWA), jnp.float32)
    sinks1 = sinks.reshape(N_SWA)
    b_col = b_f.reshape(N_FOX, 1)

    (h0, cq, ckv, qm, km, vm, qs, kd, vd, gate0, cos, sin) = _layer0_in(
        x, pos, e_g_in, w_in0p, e_g_q, w_qp, e_g_kv, w_kvp)
    o_m, lse_m, rode = _attn_fwd_t(qm, km, vm, (NOPE + ROPE) ** -0.5, split=True, name="mla_fwd", plan=layer1.gather_plan())
    w_out0, o_g_in, w_in1, w_out1 = layer1.layer1_weights(rode)
    w_in1p = _prep_w_in1(w_in1)
    o_s, lse_s = _swa_fwd(qs, kd, vd, sinks1, slopes)
    x1, h1, q1, k1, v1, gate1, f_slab = _layer0_out_layer1_in(x, o_m, o_s, gate0, w_out0, o_g_in, w_in1p)
    f_row = f_slab[:, :N_FOX].T
    lc_row = _forget_fwd(f_row, b_col)
    lcc = lc_row.T
    o1, lse1, _ = _attn_fwd_t(q1, k1, v1, HEAD ** -0.5, split=False, name="fox_fwd", lcc=lcc)
    loss8, dg_final, dw_out1, dx2, do1, dgate1 = _head(x1, o1, gate1, w_out1, g_final, target)

    dq1, dk1, dv1, dlc, _ = _attn_bwd_t(q1, k1, v1, do1, o1, lse1, HEAD ** -0.5, split=False, name="fox_bwd", lcc=lcc)
    df_row, db_f = _forget_bwd(dlc.reshape(N_FOX, S), f_row, b_col)
    df_slab = jnp.pad(df_row.T, ((0, 0), (0, LANES - N_FOX))).astype(MXU)
    dz1, dx1, dg_o_in, dw_out0, do_m, do_s, dgate0 = _layer1_in_bwd(
        dq1, dk1, dv1, dgate1, df_slab, x1, dx2, o_g_in, w_in1p, gate0, o_m, o_s, w_out0)
    grads1 = dict(o_g_in=dg_o_in, o_w_in=_fold_w_in1(_wgrad(dz1, h1, "wgrad_in1")), o_w_out=dw_out1, e_w_out=dw_out0)
    dqs, dkd, dvd, dsink, rode = _swa_bwd(qs, kd, vd, do_s, o_s, lse_s, sinks1, slopes, plan=layer1.swap_plan(grads1))
    dqm, dkm, dvm, rode = _attn_bwd_t(qm, km, vm, do_m, o_m, lse_m, (NOPE + ROPE) ** -0.5, split=True, name="mla_bwd",
                                      plan=layer1.exchange_plan(rode))
    dx, dz0, dg_in, dg_q, dg_kv, dw_q, dw_kv, rode = _layer0_in_bwd(
        dqm, dkm, dvm, dqs, dkd, dvd, dgate0, cos, sin, cq, ckv, x, dx1, e_g_in, w_in0p, e_g_q, w_qp, e_g_kv, w_kvp,
        plan=layer1.halves_plan(rode))
    layer1.finish(rode)

    grads = dict(
        e_g_in=dg_in,
        e_w_in=_fold_w_in0(_wgrad(dz0, h0, "wgrad_in0")),
        e_g_q_a=dg_q,
        e_w_q_up=_fold_w_q(dw_q),
        e_g_kv_a=dg_kv,
        e_w_kv_up=_fold_w_kv(dw_kv),
        e_sinks=dsink[:, 0:4, 0].reshape(1, N_SWA),
        o_b_f=db_f.reshape(1, N_FOX),
        g_final=dg_final,
        **grads1,
    )
    return loss8[0, 0], dx, grads


SHARDED = ("e_w_in", "e_w_q_up", "e_w_kv_up", "e_w_out", "o_g_in", "o_w_in", "o_w_out")
TRANSPOSED = ("e_w_in", "e_w_q_up", "o_w_in")
COL_SHARDED = ("e_w_kv_up", "o_g_in")
REPLICATED = ("e_g_in", "e_g_q_a", "e_g_kv_a", "e_sinks", "o_b_f", "g_final")
FULL_SHAPES = dict(e_w_in=(2208, 1024), e_w_q_up=(768, 256), e_w_kv_up=(128, 1024), e_w_out=(1024, 1024),
                   o_g_in=(1, 1024), o_w_in=(4112, 1024), o_w_out=(1024, 1024))
GROUPS = dict(
    layer0=dict(rows=768, windows=dict(e_w_in=(0, 0), e_w_q_up=(560, 0), e_w_kv_up=(560, 256))),
    layer1=dict(rows=1568, windows=dict(o_w_in=(0, 0), o_w_out=(1040, 0), e_w_out=(1296, 0), o_g_in=(1552, 0))),
)


def _shard_shape(name):
    r, c = FULL_SHAPES[name]
    return (r, c // 4) if name in COL_SHARDED else (r // 4, c)


def _as_handled(name, a):
    a = a[0] if a.ndim == 3 else a
    return a.T if name in TRANSPOSED else a


def _as_given(name, a, shape):
    return (a.T if name in TRANSPOSED else a).reshape(shape)


def _pack_block(p, group, shifted_for=None):
    def rows(a, n):
        return jnp.pad(a, ((0, n - a.shape[0]), (0, 0)))

    if group == "layer0":
        band = jnp.concatenate([p["e_w_q_up"], rows(p["e_w_kv_up"], 192), jnp.zeros((192, 512), p["e_w_in"].dtype)], axis=1)
        return jnp.concatenate([rows(p["e_w_in"], 560), rows(band, 208)], axis=0)
    g = p["o_g_in"]
    band = jnp.pad(g, ((0, 16 - g.shape[0]), (0, PACK_COLS - g.shape[1])))
    w_in = rows(p["o_w_in"], 1040) if shifted_for is None else _shifted_shard(p["o_w_in"], shifted_for, 1040)
    return jnp.concatenate([w_in, p["o_w_out"], p["e_w_out"], band], axis=0)


def _window(block, group, name, width=None):
    r0, c0 = GROUPS[group]["windows"][name]
    r, c = _shard_shape(name)
    return block[..., r0:r0 + r, c0:c0 + (c if width is None else width)]


def _chip_slice(name, full, k):
    r, c = _shard_shape(name)
    if isinstance(full, _RowSeq):
        return jnp.concatenate(full.rows(r * k, r * (k + 1)), axis=0)
    return full[:, c * k:c * (k + 1)] if name in COL_SHARDED else full[r * k:r * (k + 1), :]


def _packed_weights(w, group):
    parts = {}
    for n in GROUPS[group]["windows"]:
        a = _as_handled(n, w[n])
        parts[n] = lax.bitcast_convert_type(a, jnp.bfloat16).reshape(1, -1) if n == "o_g_in" else a.astype(jnp.bfloat16)
    x, y, _ = _place()
    halves = _pack_block(parts, group, shifted_for=2 * x + y).reshape(2, GROUPS[group]["rows"] // 2, PACK_COLS)
    return lax.dynamic_index_in_dim(halves, lax.axis_index("c"), 0, keepdims=False)


def _unpacked_weights(gathered, half, group):
    blocks = _fill_own_slot(gathered, half).reshape(4, GROUPS[group]["rows"], PACK_COLS)
    full = {}
    for n in GROUPS[group]["windows"]:
        if n == "o_g_in":
            halves = _window(blocks, group, n, width=512).reshape(4, 1, 256, 2)
            full[n] = jnp.concatenate(list(lax.bitcast_convert_type(halves, jnp.float32)), axis=1)
        elif n == "o_w_in":
            full[n] = _ShiftedShards(blocks[k, 0:1040].astype(MXU) for k in range(4))
        else:
            pieces = [_window(blocks[k], group, n).astype(MXU) for k in range(4)]
            if n == "e_w_in":
                full[n] = _RowSeq(pieces)
            else:
                full[n] = jnp.concatenate(pieces, axis=1 if n in COL_SHARDED else 0)
    return full


class _GroupReduce:
    def __init__(self, group):
        self.group = group
        self.c = lax.axis_index("c")
        self.chip = 2 * lax.axis_index("x") + lax.axis_index("y")

    def swap_plan(self, grads):
        names = GROUPS[self.group]["windows"]
        per_chip = jnp.concatenate([_pack_block({n: _chip_slice(n, grads[n], k) for n in names}, self.group)
                                    for k in range(4)], axis=0)
        self.g4 = per_chip.reshape(4, 2, GROUPS[self.group]["rows"] // 2, PACK_COLS)
        return _pair_swap_plan(self.g4)

    def exchange_plan(self, rode):
        self.theirs = rode[0]
        return _chip_exchange_plan(_add_halves(self.g4, self.c, self.theirs, "pair_add_" + self.group, jnp.bfloat16))

    def halves_plan(self, rode):
        self.my_half = _total_sum(self.g4, self.theirs, self.chip, self.c, rode[0], "chip_sum_" + self.group)
        return _pair_exchange_plan(self.my_half)

    def finish(self, rode):
        my_half, other_half = self.my_half, rode[0]
        total = jnp.concatenate([jnp.where(self.c == 0, my_half, other_half), jnp.where(self.c == 0, other_half, my_half)], axis=0)
        self.sums = {n: _window(total, self.group, n) for n in GROUPS[self.group]["windows"]}

    def run(self, grads, beside):
        swap = self.swap_plan(grads)
        outs = _run_plan(_both_plans(swap, beside), "pair_swap_" + self.group)
        rode, others = outs[:len(swap.out_shape)], outs[len(swap.out_shape):]
        halves = self.halves_plan(_run_plan(self.exchange_plan(rode), "chip_exchange_" + self.group))
        self.finish(_run_plan(halves, "pair_exchange_" + self.group))
        return self.sums, others


class _Layer1Exchange(_GroupReduce):
    def __init__(self, w):
        super().__init__("layer1")
        self.half = _packed_weights(w, "layer1")

    def gather_plan(self):
        return _gather8_plan(self.half)

    def layer1_weights(self, rode):
        full = _unpacked_weights(rode[0], self.half, "layer1")
        return full["e_w_out"], full["o_g_in"], full["o_w_in"], full["o_w_out"]


def kernel(x, positions, e_g_in, e_w_in, e_g_q_a, e_w_q_up, e_g_kv_a, e_w_kv_up, e_sinks, e_w_out, o_g_in, o_w_in, o_b_f, o_w_out, g_final, loss_target, m_e_g_in, m_e_w_in, m_e_g_q_a, m_e_w_q_up, m_e_g_kv_a, m_e_w_kv_up, m_e_sinks, m_e_w_out, m_o_g_in, m_o_w_in, m_o_b_f, m_o_w_out, m_g_final, v_e_g_in, v_e_w_in, v_e_g_q_a, v_e_w_q_up, v_e_g_kv_a, v_e_w_kv_up, v_e_sinks, v_e_w_out, v_o_g_in, v_o_w_in, v_o_b_f, v_o_w_out, v_g_final):
    w = dict(e_g_in=e_g_in, e_w_in=e_w_in, e_g_q_a=e_g_q_a, e_w_q_up=e_w_q_up, e_g_kv_a=e_g_kv_a, e_w_kv_up=e_w_kv_up,
             e_sinks=e_sinks, e_w_out=e_w_out, o_g_in=o_g_in, o_w_in=o_w_in, o_b_f=o_b_f, o_w_out=o_w_out, g_final=g_final)
    m = dict(e_g_in=m_e_g_in, e_w_in=m_e_w_in, e_g_q_a=m_e_g_q_a, e_w_q_up=m_e_w_q_up, e_g_kv_a=m_e_g_kv_a,
             e_w_kv_up=m_e_w_kv_up, e_sinks=m_e_sinks, e_w_out=m_e_w_out, o_g_in=m_o_g_in, o_w_in=m_o_w_in, o_b_f=m_o_b_f,
             o_w_out=m_o_w_out, g_final=m_g_final)
    v = dict(e_g_in=v_e_g_in, e_w_in=v_e_w_in, e_g_q_a=v_e_g_q_a, e_w_q_up=v_e_w_q_up, e_g_kv_a=v_e_g_kv_a,
             e_w_kv_up=v_e_w_kv_up, e_sinks=v_e_sinks, e_w_out=v_e_w_out, o_g_in=v_o_g_in, o_w_in=v_o_w_in, o_b_f=v_o_b_f,
             o_w_out=v_o_w_out, g_final=v_g_final)
    order = ("e_g_in", "e_w_in", "e_g_q_a", "e_w_q_up", "e_g_kv_a", "e_w_kv_up", "e_sinks", "e_w_out", "o_g_in", "o_w_in",
             "o_b_f", "o_w_out", "g_final")
    half0 = _packed_weights(w, "layer0")
    full = _unpacked_weights(_run_plan(_gather8_plan(half0), "gather_weights_layer0")[0], half0, "layer0")
    layer1 = _Layer1Exchange(w)

    loss_part, dx, grads = _local_step(
        x[0], positions.reshape(-1, 1), loss_target[0], e_g_in, full["e_w_in"], e_g_q_a, full["e_w_q_up"], e_g_kv_a,
        full["e_w_kv_up"], e_sinks, o_b_f, g_final.reshape(1, D), layer1)

    small = jnp.concatenate([jnp.pad(loss_part.reshape(1), (0, LANES - 1))]
                            + [jnp.pad(grads[n].reshape(-1), (0, (-grads[n].size) % LANES)) for n in REPLICATED])
    rows = small.shape[0] // LANES
    small = jnp.pad(small.reshape(rows, LANES), ((0, (-rows) % 8), (0, 0)))
    sums0, (gathered_small,) = _GroupReduce("layer0").run(grads, _gather8_plan(small))
    gsum = {**layer1.sums, **sums0}
    ssum = _sum_leading(_fill_own_slot(gathered_small, small), "small_grad_sum").reshape(-1)
    loss = ssum[0]
    off = LANES
    for n in REPLICATED:
        cnt = w[n].size
        gsum[n] = ssum[off:off + cnt].reshape(w[n].shape)
        off += cnt + (-cnt) % LANES

    grad, delta, new_m, new_v = {}, {}, {}, {}
    for n in order:
        if n == "o_w_in":
            def tiles(a):
                return jnp.transpose(a, (2, 0, 1)).reshape(-1, LANES)

            def given(a):
                return jnp.transpose(a.reshape(-1, 8, LANES), (1, 2, 0)).reshape(w[n].shape)

            g_t = gsum[n].reshape(-1, LANES)
            outs = _adamw(tiles(w[n]), g_t, tiles(m[n]), tiles(v[n]), "adamw_" + n)
            grad[n], delta[n], new_m[n], new_v[n] = (given(a) for a in (g_t,) + outs)
        elif n in SHARDED:
            outs = _adamw(_as_handled(n, w[n]), gsum[n], _as_handled(n, m[n]), _as_handled(n, v[n]), "adamw_" + n)
            grad[n], delta[n], new_m[n], new_v[n] = (_as_given(n, a, w[n].shape) for a in (gsum[n],) + outs)
        else:
            grad[n] = gsum[n]
            delta[n], new_m[n], new_v[n] = _adamw(w[n], gsum[n], m[n], v[n], "adamw_" + n)
    return (loss, dx[None], *[grad[n] for n in order], *[delta[n] for n in order], *[new_m[n] for n in order],
            *[new_v[n] for n in order])
```

```python
import math

import numpy as np
import jax
import jax.numpy as jnp
from jax import lax
from jax.experimental import pallas as pl
from jax.experimental.pallas import tpu as pltpu

D = 1024
EPS = 1e-6
ROPE_THETA = 10000.0
N_MLA = 8
Q_RANK = 256
KV_RANK = 128
NOPE = 64
ROPE = 32
N_SWA = 8
WINDOW = 128
N_FOX = 16
HEAD = 64
LR, B1, B2, AEPS, WD, STEP = 0.001, 0.9, 0.999, 1e-08, 0.01, 10

LANES = 128
HALF = 64
VMEM_LIMIT = 56 * 1024 * 1024
MXU = jnp.bfloat16
TOK = 256
HEAD_TOK = 512
WG_TOK = 2048
WG_ROWS = 1536
ATT = 256
FWD_CHUNK = 2
BWD_CHUNK = 2
SWA_GROUP = 8
NEG = float("-inf")

PACK_COLS = 1024
SUM_ROWS = 256
ADAM_TILE_BYTES = 2 << 20
MESH_ID = pl.DeviceIdType.MESH


def _pcall(body, *, name, vmem=VMEM_LIMIT, semantics=None, **kw):
    params = dict(vmem_limit_bytes=vmem)
    if semantics is not None:
        params["dimension_semantics"] = semantics
    return pl.pallas_call(body, name=name, compiler_params=pltpu.CompilerParams(**params), **kw)


def _mm(a, b):
    return jnp.dot(a.astype(MXU), b.astype(MXU), preferred_element_type=jnp.float32)


def _mm_nt(a, b):
    return lax.dot_general(a.astype(MXU), b.astype(MXU), (((1,), (1,)), ((), ())),
                           preferred_element_type=jnp.float32)


def _mm_tn(a, b):
    return lax.dot_general(a.astype(MXU), b.astype(MXU), (((0,), (0,)), ((), ())),
                           preferred_element_type=jnp.float32)


def _full(shape):
    n = len(shape)
    return pl.BlockSpec(shape, lambda *_: (0,) * n)


def _rows(tm, n):
    return pl.BlockSpec((tm, n), lambda i: (i, 0))


def _sds(shape, dtype):
    return jax.ShapeDtypeStruct(shape, dtype)


def _rms(x, g):
    r = lax.rsqrt(jnp.mean(x * x, axis=-1, keepdims=True) + EPS)
    return x * r * g


def _rms_bwd(x, g, dy):
    r = lax.rsqrt(jnp.mean(x * x, axis=-1, keepdims=True) + EPS)
    xh = x * r
    dxh = dy * g
    dx = r * (dxh - xh * jnp.mean(dxh * xh, axis=-1, keepdims=True))
    return dx, dy * xh


def _sigmoid(x):
    return 1.0 / (1.0 + jnp.exp(-x))


def _lane_masks():
    lane = lax.broadcasted_iota(jnp.int32, (1, LANES), 1)
    return lane < HALF


def _split_heads(a, lo):
    z = jnp.zeros_like(a)
    return [jnp.where(lo, a, z), jnp.where(lo, z, a)]


def _rope_consts():
    inv = np.zeros((8, LANES), np.float32)
    j = np.arange(ROPE // 2, dtype=np.float32)
    f = (1.0 / (ROPE_THETA ** (np.arange(0, ROPE, 2, dtype=np.float32) / ROPE))).astype(np.float32)
    inv[0, HALF:HALF + 16] = f
    inv[0, HALF + 16:HALF + 32] = f
    inv[1, HALF:HALF + 16] = -1.0
    inv[1, HALF + 16:HALF + 32] = 1.0
    del j
    return jnp.asarray(inv)


def _rope_tables(pos_f, consts):
    ang = pos_f * consts[0:1, :]
    sign = consts[1:2, :]
    c = jnp.where(sign != 0.0, jnp.cos(ang), 1.0)
    s = jnp.sin(ang) * sign
    return c, s


def _swap_halves(v, sign):
    lo = pltpu.roll(v, LANES - 16, axis=1)
    hi = pltpu.roll(v, 16, axis=1)
    return jnp.where(sign < 0.0, lo, jnp.where(sign > 0.0, hi, 0.0))


def _rope(x, c, s, sign):
    return x * c + _swap_halves(x, sign) * s


def _rope_t(dy, c, s, sign):
    return dy * c + _swap_halves(dy * s, sign)


def _layer0_in(x, pos, g_in, w_in, g_q, w_q, g_kv, w_kv):
    S = x.shape[0]
    T = math.gcd(HEAD_TOK, S)
    consts = _rope_consts()

    def body(x_ref, pos_ref, c_ref, g_ref, w_ref, gq_ref, wq_ref, gkv_ref, wkv_ref,
             h_ref, cq_ref, ckv_ref, qm_ref, km_ref, vm_ref,
             qs_ref, kd_ref, vd_ref, gate_ref, cos_ref, sin_ref):
        h = _rms(x_ref[...], g_ref[...])
        h_ref[...] = h.astype(h_ref.dtype)
        z = _mm_nt(h, w_ref[...])
        cq = z[:, 0:256]
        ckv = z[:, 256:384]
        kpe = z[:, 384:512]
        cq_ref[...] = cq
        ckv_ref[...] = ckv
        qs_ref[...] = z[:, 512:1024].astype(qs_ref.dtype)
        kd_ref[...] = z[:, 1024:1536].astype(kd_ref.dtype)
        vd_ref[...] = z[:, 1536:2048].astype(vd_ref.dtype)
        gate_ref[...] = z[:, 2048:3072]
        cqn = _rms(cq, gq_ref[...])
        ckvn = _rms(ckv, gkv_ref[...])
        q = _mm_nt(cqn, wq_ref[...])
        kv = _mm(ckvn, wkv_ref[...])
        vm_ref[...] = kv[:, 1024:1536].astype(vm_ref.dtype)
        consts_v = c_ref[...]
        sign = consts_v[1:2, :]
        c, s = _rope_tables(pos_ref[...].astype(jnp.float32), consts_v)
        cos_ref[...] = c
        sin_ref[...] = s
        kpe_r = _rope(kpe, c, s, sign)
        for hd in range(N_MLA):
            sl = slice(LANES * hd, LANES * (hd + 1))
            qm_ref[:, sl] = _rope(q[:, sl], c, s, sign).astype(qm_ref.dtype)
            km_ref[:, sl] = (kv[:, sl] + kpe_r).astype(km_ref.dtype)

    outs = [
        ((S, D), MXU), ((S, 256), jnp.float32), ((S, 128), jnp.float32),
        ((S, 1024), MXU), ((S, 1024), MXU), ((S, 512), MXU), ((S, 512), MXU), ((S, 512), MXU), ((S, 512), MXU),
        ((S, 1024), jnp.float32), ((S, 128), jnp.float32), ((S, 128), jnp.float32),
    ]
    return _pcall(
        body, name="layer0_in", grid=(S // T,), semantics=("arbitrary",),
        in_specs=[_rows(T, D), _rows(T, 1), _full((8, LANES)), _full((1, D)), _full(w_in.shape), _full((1, 256)),
                  _full(w_q.shape), _full((1, 128)), _full(w_kv.shape)],
        out_specs=[_rows(T, s[1]) for s, _ in outs],
        out_shape=[_sds(s, d) for s, d in outs],
    )(x, pos, consts, g_in, w_in, g_q, w_q, g_kv, w_kv)


AUG = (HALF, 0)
ONE = (HALF + 8, 8)


def _data_lanes(idx, h):
    return (idx < HALF) if h == 0 else (idx >= HALF)


def _three_terms(x):
    hi = x.astype(MXU).astype(jnp.float32)
    mid = (x - hi).astype(MXU).astype(jnp.float32)
    lo = (x - hi - mid).astype(MXU).astype(jnp.float32)
    return hi, mid, lo


def _q_aug(qblk, lc, h, scale, lane):
    a = AUG[h]
    hi, mid, lo = _three_terms(lc)
    ones = ((lane >= a + 3) & (lane <= a + 5)).astype(jnp.float32)
    aug = jnp.where(lane == a, hi, jnp.where(lane == a + 1, mid, jnp.where(lane == a + 2, lo, ones)))
    return jnp.where(_data_lanes(lane, h), qblk * jnp.asarray(scale, qblk.dtype), aug.astype(qblk.dtype))


def _k_aug(kblk, lc, h, lane):
    a = AUG[h]
    hi, mid, lo = _three_terms(-lc)
    ones = ((lane >= a) & (lane <= a + 2)).astype(jnp.float32)
    aug = jnp.where(lane == a + 3, hi, jnp.where(lane == a + 4, mid, jnp.where(lane == a + 5, lo, ones)))
    return jnp.where(_data_lanes(lane, h), kblk, aug.astype(kblk.dtype))


def _lc_col(lc_ref, r0, rows, h):
    head = lax.broadcasted_iota(jnp.int32, (1, lc_ref.shape[1]), 1)
    return jnp.sum(jnp.where(head == 2 * pl.program_id(0) + h, lc_ref[pl.ds(r0, rows), :], 0.0), axis=1, keepdims=True)


def _attn_fwd_t(q, k, v, scale, *, split, name, lcc=None, plan=None):
    S = q.shape[0]
    npair = v.shape[1] // LANES
    W = 2 * LANES if split else LANES
    T = ATT
    CH = FWD_CHUNK * T
    assert S % CH == 0
    nq = S // T

    def body(*refs):
        if split:
            q_ref, k_ref, v_ref, o_ref, lse_ref, vt, acc, m_sc = refs
        else:
            q_ref, k_ref, v_ref, lcc_ref, o_ref, lse_ref, kaug, vt, acc, m_sc = refs
        lane = lax.broadcasted_iota(jnp.int32, (1, LANES), 1)
        sub = lax.broadcasted_iota(jnp.int32, (LANES, 1), 0)
        key_minus_qry = lax.broadcasted_iota(jnp.int32, (CH, T), 0) - lax.broadcasted_iota(jnp.int32, (CH, T), 1)

        def prep(i, c):
            r0 = pl.multiple_of(i * T, T)
            vblk = v_ref[pl.ds(r0, T), :].astype(jnp.float32)
            for h in (0, 1):
                vh = jnp.where(_data_lanes(lane, h), vblk, (lane == ONE[h]).astype(jnp.float32))
                vt[h, :, pl.ds(r0, T)] = vh.T.astype(vt.dtype)
                if not split:
                    kaug[h, pl.ds(r0, T), :] = _k_aug(k_ref[pl.ds(r0, T), :], _lc_col(lcc_ref, r0, T, h), h, lane)
            return c

        lax.fori_loop(0, nq, prep, 0)

        def queries(qi):
            q0 = pl.multiple_of(qi * T, T)
            qblk = q_ref[pl.ds(q0, T), :]
            if split:
                return (qblk[:, :LANES], qblk[:, LANES:])
            return tuple(_q_aug(qblk, _lc_col(lcc_ref, q0, T, h), h, scale, lane) for h in (0, 1))

        def scores(qs, c):
            k0 = pl.multiple_of(c * CH, CH)
            out = []
            for h in (0, 1):
                if split:
                    out.append(_mm_nt(k_ref[pl.ds(k0, CH), LANES * h:LANES * (h + 1)], qs[h]) * scale)
                else:
                    out.append(_mm_nt(kaug[h, pl.ds(k0, CH), :], qs[h]))
            return tuple(out)

        def q_block(qi, carry):
            qs, first_scores = carry[:2], carry[2:]
            q0 = pl.multiple_of(qi * T, T)
            acc[...] = jnp.zeros_like(acc)
            m_sc[...] = jnp.full(m_sc.shape, NEG, jnp.float32)

            def absorb(c, sts, masked):
                k0 = pl.multiple_of(c * CH, CH)
                for h in (0, 1):
                    st = sts[h]
                    if masked:
                        st = jnp.where(key_minus_qry <= q0 - k0, st, NEG)
                    m_old = m_sc[h:h + 1, :]
                    m_new = jnp.maximum(m_old, jnp.max(st, axis=0, keepdims=True))
                    alpha = jnp.exp(m_old - m_new)
                    pt = jnp.exp(st - m_new)
                    acc[h] = alpha * acc[h] + _mm(vt[h, :, pl.ds(k0, CH)], pt)
                    m_sc[h:h + 1, :] = m_new

            last = qi // FWD_CHUNK

            def pipelined(c, sts):
                nxt = scores(qs, c + 1)
                absorb(c, sts, False)
                return nxt

            sts = lax.fori_loop(0, last, pipelined, first_scores)
            qs_next = queries(jnp.minimum(qi + 1, nq - 1))
            nxt = qs_next + scores(qs_next, 0)
            absorb(last, sts, True)
            ot = None
            for h in (0, 1):
                a = acc[h]
                l = a[ONE[h]:ONE[h] + 1, :]
                oh = jnp.where(_data_lanes(sub, h), a * (1.0 / l), 0.0)
                ot = oh if ot is None else ot + oh
                lse_ref[0, h:h + 1, pl.ds(q0, T)] = m_sc[h:h + 1, :] + jnp.log(l)
            o_ref[pl.ds(q0, T), :] = ot.T
            return nxt

        qs0 = queries(0)
        lax.fori_loop(0, nq, q_block, qs0 + scores(qs0, 0))

    wide = pl.BlockSpec((S, W), lambda j: (0, j))
    slab = pl.BlockSpec((S, LANES), lambda j: (0, j))
    rows = pl.BlockSpec((1, 2, S), lambda j: (j, 0, 0))
    in_specs = [wide, wide, slab]
    args = [q, k, v]
    scratch = []
    if not split:
        in_specs.append(_full(lcc.shape))
        args.append(lcc)
        scratch.append(pltpu.VMEM((2, S, LANES), MXU))
    scratch += [pltpu.VMEM((2, LANES, S), MXU), pltpu.VMEM((2, LANES, T), jnp.float32), pltpu.VMEM((8, T), jnp.float32)]
    (o, lse), rode = _pcall_riding(
        body, plan, args, name=name, grid=(npair,), in_specs=in_specs, out_specs=[slab, rows],
        out_shape=[_sds((S, npair * LANES), jnp.float32), _sds((npair, 2, S), jnp.float32)], scratch_shapes=scratch)
    return o, lse, rode


def _attn_bwd_t(q, k, v, do, o, lse, scale, *, split, name, lcc=None, plan=None):
    S = q.shape[0]
    npair = v.shape[1] // LANES
    W = 2 * LANES if split else LANES
    T = ATT
    CH = BWD_CHUNK * T
    assert S % CH == 0
    nq = S // T

    def body(*refs):
        if split:
            (q_ref, k_ref, v_ref, do_ref, o_ref, lse_ref, dq_ref, dk_ref, dv_ref, dqt, delta, dk_acc, dv_acc) = refs
        else:
            (q_ref, k_ref, v_ref, do_ref, o_ref, lse_ref, lcc_ref, dq_ref, dk_ref, dv_ref, dlc_ref,
             dqt, delta, dk_acc, dv_acc, qaug, csum) = refs
        lane = lax.broadcasted_iota(jnp.int32, (1, LANES), 1)
        sub = lax.broadcasted_iota(jnp.int32, (LANES, 1), 0)
        key_minus_qry = lax.broadcasted_iota(jnp.int32, (T, CH), 0) - lax.broadcasted_iota(jnp.int32, (T, CH), 1)

        def prep(i, c):
            r0 = pl.multiple_of(i * T, T)
            prod_t = (do_ref[pl.ds(r0, T), :].astype(jnp.float32) * o_ref[pl.ds(r0, T), :]).T
            for h in (0, 1):
                delta[h:h + 1, pl.ds(r0, T)] = jnp.sum(jnp.where(_data_lanes(sub, h), prod_t, 0.0), axis=0, keepdims=True)
                dqt[h, :, pl.ds(r0, T)] = jnp.zeros((LANES, T), jnp.float32)
                if not split:
                    qaug[h, pl.ds(r0, T), :] = _q_aug(q_ref[pl.ds(r0, T), :], _lc_col(lcc_ref, r0, T, h), h, scale, lane)
            return c

        lax.fori_loop(0, nq, prep, 0)

        def keys(ki):
            k0 = pl.multiple_of(ki * T, T)
            kblk = k_ref[pl.ds(k0, T), :]
            if split:
                return (kblk[:, :LANES], kblk[:, LANES:])
            return tuple(_k_aug(kblk, _lc_col(lcc_ref, k0, T, h), h, lane) for h in (0, 1))

        def q_of(c, h):
            q0 = pl.multiple_of(c * CH, CH)
            if split:
                return q_ref[pl.ds(q0, CH), LANES * h:LANES * (h + 1)]
            return qaug[h, pl.ds(q0, CH), :]

        def scores(khs, c):
            out = []
            for h in (0, 1):
                st = _mm_nt(khs[h], q_of(c, h))
                out.append(st * scale if split else st)
            return tuple(out)

        def k_block(ki, carry):
            khs, first_scores = carry[:2], carry[2:]
            k0 = pl.multiple_of(ki * T, T)
            khts = [kh.astype(jnp.float32).T.astype(kh.dtype) for kh in khs]
            vhs = _split_heads(v_ref[pl.ds(k0, T), :], lane < HALF)
            dk_acc[...] = jnp.zeros_like(dk_acc)
            dv_acc[...] = jnp.zeros_like(dv_acc)

            def absorb(c, vals):
                q0 = pl.multiple_of(c * CH, CH)
                dos = _split_heads(do_ref[pl.ds(q0, CH), :], lane < HALF)
                visible = key_minus_qry <= q0 - k0
                for h in (0, 1):
                    dpt = _mm_nt(vhs[h], dos[h])
                    st = jnp.where(visible, vals[h], NEG)
                    pt = jnp.exp(st - lse_ref[0, h:h + 1, pl.ds(q0, CH)])
                    dv_acc[...] += _mm(pt, dos[h])
                    dst = pt * (dpt - delta[h:h + 1, pl.ds(q0, CH)])
                    dk_acc[h] += _mm(dst, q_of(c, h))
                    dqt[h, :, pl.ds(q0, CH)] += _mm(khts[h], dst)

            first = ki // BWD_CHUNK

            def pipelined(c, vals):
                nxt = scores(khs, c + 1)
                absorb(c, vals)
                return nxt

            vals = lax.fori_loop(first, S // CH - 1, pipelined, first_scores)
            kn = jnp.minimum(ki + 1, nq - 1)
            khs_next = keys(kn)
            nxt = khs_next + scores(khs_next, kn // BWD_CHUNK)
            absorb(S // CH - 1, vals)
            if split:
                dk_ref[pl.ds(k0, T), :LANES] = (dk_acc[0] * scale).astype(dk_ref.dtype)
                dk_ref[pl.ds(k0, T), LANES:] = (dk_acc[1] * scale).astype(dk_ref.dtype)
            else:
                dk_ref[pl.ds(k0, T), :] = jnp.where(lane < HALF, dk_acc[0], dk_acc[1]).astype(dk_ref.dtype)
                for h in (0, 1):
                    csum[h:h + 1, pl.ds(k0, T)] = dk_acc[h].T[AUG[h] + 3:AUG[h] + 4, :]
            dv_ref[pl.ds(k0, T), :] = dv_acc[...].astype(dv_ref.dtype)
            return nxt

        khs0 = keys(0)
        lax.fori_loop(0, nq, k_block, khs0 + scores(khs0, 0))

        def finish(i, c):
            r0 = pl.multiple_of(i * T, T)
            if split:
                for h in (0, 1):
                    dq_ref[pl.ds(r0, T), LANES * h:LANES * (h + 1)] = (dqt[h, :, pl.ds(r0, T)].T * scale).astype(dq_ref.dtype)
            else:
                d = jnp.where(sub < HALF, dqt[0, :, pl.ds(r0, T)], dqt[1, :, pl.ds(r0, T)])
                dq_ref[pl.ds(r0, T), :] = (d.T * scale).astype(dq_ref.dtype)
                for h in (0, 1):
                    dlc_ref[0, h:h + 1, pl.ds(r0, T)] = dqt[h, AUG[h]:AUG[h] + 1, pl.ds(r0, T)] - csum[h:h + 1, pl.ds(r0, T)]
            return c

        lax.fori_loop(0, nq, finish, 0)

    wide = pl.BlockSpec((S, W), lambda j: (0, j))
    slab = pl.BlockSpec((S, LANES), lambda j: (0, j))
    rows = pl.BlockSpec((1, 2, S), lambda j: (j, 0, 0))
    in_specs = [wide, wide, slab, slab, slab, rows]
    args = [q, k, v, do, o, lse]
    out_specs = [wide, wide, slab]
    out_shape = [_sds(q.shape, jnp.float32 if split else do.dtype), _sds(k.shape, jnp.float32 if split else do.dtype),
                 _sds(v.shape, do.dtype)]
    scratch = [pltpu.VMEM((2, LANES, S), jnp.float32), pltpu.VMEM((8, S), jnp.float32),
               pltpu.VMEM((2, T, LANES), jnp.float32), pltpu.VMEM((T, LANES), jnp.float32)]
    if not split:
        in_specs.append(_full(lcc.shape))
        args.append(lcc)
        out_specs.append(rows)
        out_shape.append(_sds((npair, 2, S), jnp.float32))
        scratch += [pltpu.VMEM((2, S, LANES), MXU), pltpu.VMEM((8, S), jnp.float32)]
    outs, rode = _pcall_riding(body, plan, args, name=name, grid=(npair,), in_specs=in_specs, out_specs=out_specs,
                               out_shape=out_shape, scratch_shapes=scratch)
    return (*outs, rode)


def _swa_bias(slope, shift):
    a = lax.broadcasted_iota(jnp.int32, (WINDOW, 2 * WINDOW), 0)
    c = lax.broadcasted_iota(jnp.int32, (WINDOW, 2 * WINDOW), 1)
    dist = a - c + shift
    return jnp.where((dist >= 0) & (dist < WINDOW), -slope * dist.astype(jnp.float32), NEG)


def _swa_scores(qh, kblk, bias):
    return _mm_nt(qh, kblk) * (HEAD ** -0.5) + bias


def _swa_stack(blk, lo):
    return jnp.concatenate(_split_heads(blk[:, :LANES], lo) + _split_heads(blk[:, LANES:], lo), axis=0)


def _swa_unstack(x, lo):
    r = x.shape[0] // 4
    return jnp.concatenate([jnp.where(lo, x[0:r], x[r:2 * r]), jnp.where(lo, x[2 * r:3 * r], x[3 * r:])], axis=1)


def _swa_per_head(ref, j, rows):
    quarter = lax.broadcasted_iota(jnp.int32, (4 * rows, 1), 0) // rows
    return jnp.where(quarter == 0, ref[4 * j], jnp.where(quarter == 1, ref[4 * j + 1],
                                                         jnp.where(quarter == 2, ref[4 * j + 2], ref[4 * j + 3])))


def _swa_fwd(q, kd, vd, sinks, slopes):
    S = q.shape[0]
    nkv = q.shape[1] // (2 * LANES)
    nb = S // WINDOW
    group = math.gcd(SWA_GROUP, nb)

    def body(sink_ref, slope_ref, q_ref, k_ref, v_ref, o_ref, lse_ref):
        j = pl.program_id(0)
        lo = _lane_masks()
        sink = _swa_per_head(sink_ref, j, WINDOW)
        biases = [jnp.concatenate([_swa_bias(slope_ref[4 * j + h], shift) for h in range(4)], axis=0)
                  for shift in (0, WINDOW)]

        def q_block(qi, c):
            q0 = pl.multiple_of(qi * WINDOW, WINDOW)
            k0 = pl.multiple_of(jnp.maximum(qi - 1, 0) * WINDOW, WINDOW)
            s = _swa_scores(_swa_stack(q_ref[pl.ds(q0, WINDOW), :], lo), k_ref[pl.ds(k0, 2 * WINDOW), :],
                            jnp.where(qi == 0, *biases))
            m = jnp.maximum(jnp.max(s, axis=1, keepdims=True), sink)
            p = jnp.exp(s - m)
            den = jnp.sum(p, axis=1, keepdims=True) + jnp.exp(sink - m)
            o_ref[pl.ds(q0, WINDOW), :] = _swa_unstack(_mm(p / den, v_ref[pl.ds(k0, 2 * WINDOW), :]), lo)
            lse = m + jnp.log(den)
            for h in range(4):
                lse_ref[h, pl.ds(q0, WINDOW), :] = lse[h * WINDOW:(h + 1) * WINDOW]
            return c

        def q_group(gi, c):
            for g in range(group):
                q_block(gi * group + g, c)
            return c

        lax.fori_loop(0, nb // group, q_group, 0)

    smem = pl.BlockSpec(memory_space=pltpu.SMEM)
    two = pl.BlockSpec((S, 2 * LANES), lambda j: (0, j))
    kv = pl.BlockSpec((S, LANES), lambda j: (0, 2 * j))
    return _pcall(
        body, name="swa_fwd", grid=(nkv,), semantics=("arbitrary",),
        in_specs=[smem, smem, two, kv, kv],
        out_specs=[two, pl.BlockSpec((4, S, 1), lambda j: (j, 0, 0))],
        out_shape=[_sds(q.shape, jnp.float32), _sds((4 * nkv, S, 1), jnp.float32)],
    )(sinks, slopes, q, kd, vd)


def _swa_bwd(q, kd, vd, do, o, lse, sinks, slopes, plan=None):
    S = q.shape[0]
    nkv = q.shape[1] // (2 * LANES)
    nb = S // WINDOW
    group = math.gcd(SWA_GROUP, nb)

    def body(sink_ref, slope_ref, q_ref, k_ref, v_ref, do_ref, o_ref, lse_ref,
             dq_ref, dk_ref, dv_ref, dsink_ref, dk_acc, dv_acc):
        j = pl.program_id(0)
        lo = _lane_masks()
        dk_acc[...] = jnp.zeros_like(dk_acc)
        dv_acc[...] = jnp.zeros_like(dv_acc)
        sink = _swa_per_head(sink_ref, j, WINDOW)
        biases = [jnp.concatenate([_swa_bias(slope_ref[4 * j + h], shift) for h in range(4)], axis=0)
                  for shift in (0, WINDOW)]

        def q_block(qi, carry):
            q0 = pl.multiple_of(qi * WINDOW, WINDOW)
            k0 = pl.multiple_of(jnp.maximum(qi - 1, 0) * WINDOW, WINDOW)
            q4 = _swa_stack(q_ref[pl.ds(q0, WINDOW), :], lo)
            do4 = _swa_stack(do_ref[pl.ds(q0, WINDOW), :], lo)
            oblk = o_ref[pl.ds(q0, WINDOW), :]
            o4 = jnp.concatenate([oblk[:, :LANES], oblk[:, :LANES], oblk[:, LANES:], oblk[:, LANES:]], axis=0)
            kblk = k_ref[pl.ds(k0, 2 * WINDOW), :]
            vblk = v_ref[pl.ds(k0, 2 * WINDOW), :]
            lse = jnp.concatenate([lse_ref[h, pl.ds(q0, WINDOW), :] for h in range(4)], axis=0)
            p = jnp.exp(_swa_scores(q4, kblk, jnp.where(qi == 0, *biases)) - lse)
            delta = jnp.sum(do4.astype(jnp.float32) * o4, axis=1, keepdims=True)
            dv_acc[pl.ds(k0, 2 * WINDOW), :] += _mm_tn(p, do4)
            ds = p * (_mm_nt(do4, vblk) - delta)
            dq_ref[pl.ds(q0, WINDOW), :] = _swa_unstack(_mm(ds, kblk) * (HEAD ** -0.5), lo).astype(dq_ref.dtype)
            dk_acc[pl.ds(k0, 2 * WINDOW), :] += _mm_tn(ds, q4) * (HEAD ** -0.5)
            dsk = -jnp.exp(sink - lse) * delta
            return tuple(carry[h] + jnp.sum(dsk[h * WINDOW:(h + 1) * WINDOW], axis=0, keepdims=True)
                         for h in range(4))

        def q_group(gi, carry):
            for g in range(group):
                carry = q_block(gi * group + g, carry)
            return carry

        zero = jnp.zeros((1, 1), jnp.float32)
        dsinks = lax.fori_loop(0, nb // group, q_group, (zero,) * 4)
        dk_ref[:, :LANES] = dk_acc[...].astype(dk_ref.dtype)
        dk_ref[:, LANES:] = jnp.zeros((S, LANES), dk_ref.dtype)
        dv_ref[:, :LANES] = dv_acc[...].astype(dv_ref.dtype)
        dv_ref[:, LANES:] = jnp.zeros((S, LANES), dv_ref.dtype)
        r = lax.broadcasted_iota(jnp.int32, (8, LANES), 0)
        dsink_ref[0] = jnp.where(r == 0, dsinks[0], jnp.where(r == 1, dsinks[1], jnp.where(r == 2, dsinks[2],
                                 jnp.where(r == 3, dsinks[3], 0.0))))

    smem = pl.BlockSpec(memory_space=pltpu.SMEM)
    two = pl.BlockSpec((S, 2 * LANES), lambda j: (0, j))
    kv = pl.BlockSpec((S, LANES), lambda j: (0, 2 * j))
    outs, rode = _pcall_riding(
        body, plan, [sinks, slopes, q, kd, vd, do, o, lse], name="swa_bwd", grid=(nkv,),
        in_specs=[smem, smem, two, kv, kv, two, two, pl.BlockSpec((4, S, 1), lambda j: (j, 0, 0))],
        out_specs=[two, two, two, pl.BlockSpec((1, 8, LANES), lambda j: (j, 0, 0))],
        out_shape=[_sds(q.shape, do.dtype), _sds(kd.shape, do.dtype), _sds(vd.shape, do.dtype),
                   _sds((nkv, 8, LANES), jnp.float32)],
        scratch_shapes=[pltpu.VMEM((S, LANES), jnp.float32), pltpu.VMEM((S, LANES), jnp.float32)])
    return (*outs, rode)


def _log_steps(S):
    k, out = 1, []
    while k < S:
        out.append(k)
        k *= 2
    return out


def _forget_fwd(f_row, b_col):
    S = f_row.shape[1]

    def body(f_ref, b_ref, lc_ref):
        x = f_ref[...] + b_ref[...]
        lc = jnp.minimum(x, 0.0) - jnp.log(1.0 + jnp.exp(-jnp.abs(x)))
        idx = lax.broadcasted_iota(jnp.int32, lc.shape, 1)
        for k in _log_steps(S):
            lc = lc + jnp.where(idx >= k, pltpu.roll(lc, k, axis=1), 0.0)
        lc_ref[...] = lc

    return _pcall(body, name="forget_fwd", out_shape=_sds(f_row.shape, jnp.float32))(f_row, b_col)


def _forget_bwd(dlc_row, f_row, b_col):
    S = f_row.shape[1]

    def body(d_ref, f_ref, b_ref, df_ref, db_ref):
        g = d_ref[...]
        idx = lax.broadcasted_iota(jnp.int32, g.shape, 1)
        for k in _log_steps(S):
            g = g + jnp.where(idx < S - k, pltpu.roll(g, S - k, axis=1), 0.0)
        x = f_ref[...] + b_ref[...]
        df = g * _sigmoid(-x)
        df_ref[...] = df
        db_ref[...] = jnp.sum(df, axis=1, keepdims=True)

    return _pcall(body, name="forget_bwd",
                  out_shape=[_sds(f_row.shape, jnp.float32), _sds((f_row.shape[0], 1), jnp.float32)])(dlc_row, f_row, b_col)


def _layer0_out_layer1_in(x, o_m, o_s, gate, w_out, g1, w_in1):
    S = x.shape[0]

    def body(x_ref, om_ref, os_ref, gate_ref, wo_ref, g_ref, w_ref,
             x1_ref, h_ref, q_ref, k_ref, v_ref, g1_ref, f_ref):
        gt = gate_ref[...]
        sg = gt * _sigmoid(gt)
        um = om_ref[...] * sg[:, :512]
        us = os_ref[...] * sg[:, 512:]
        x1 = x_ref[...] + _mm(um, wo_ref[0:512, :]) + _mm(us, wo_ref[512:1024, :])
        x1_ref[...] = x1
        h = _rms(x1, g_ref[...])
        h_ref[...] = h.astype(h_ref.dtype)
        z = _mm_nt(h, w_ref[...])
        q_ref[...] = z[:, 0:1024].astype(q_ref.dtype)
        k_ref[...] = z[:, 1024:2048].astype(k_ref.dtype)
        v_ref[...] = z[:, 2048:3072].astype(v_ref.dtype)
        g1_ref[...] = z[:, 3072:4096]
        f_ref[...] = z[:, 4096:4224]

    outs = [((S, D), jnp.float32), ((S, D), MXU), ((S, D), MXU), ((S, D), MXU), ((S, D), MXU),
            ((S, D), jnp.float32), ((S, LANES), jnp.float32)]
    return _pcall(
        body, name="layer0_out_layer1_in", grid=(S // TOK,), semantics=("arbitrary",),
        in_specs=[_rows(TOK, D), _rows(TOK, 512), _rows(TOK, 512), _rows(TOK, D), _full((D, D)), _full((1, D)),
                  _full(w_in1.shape)],
        out_specs=[_rows(TOK, s[1]) for s, _ in outs],
        out_shape=[_sds(s, d) for s, d in outs],
    )(x, o_m, o_s, gate, w_out, g1, w_in1)


def _head(x1, o1, gate1, w_out1, g_f, target):
    S = x1.shape[0]
    T = math.gcd(HEAD_TOK, S)

    def body(x1_ref, o_ref, gate_ref, wo_ref, g_ref, t_ref,
             loss_ref, dgf_ref, dwo_ref, dx2_ref, do_ref, dgate_ref):
        i = pl.program_id(0)
        gt = gate_ref[...]
        sig = _sigmoid(gt)
        sg = gt * sig
        o = o_ref[...]
        u = o * sg
        x2 = x1_ref[...] + _mm(u, wo_ref[...])
        g = g_ref[...]
        y = _rms(x2, g)
        err = y - t_ref[...]
        part = 0.5 * jnp.sum(jnp.mean(err * err, axis=-1, keepdims=True), axis=0, keepdims=True)
        dy = err * (1.0 / D)
        dx2, dg_rows = _rms_bwd(x2, g, dy)
        dx2_ref[...] = dx2
        du = _mm_nt(dx2, wo_ref[...])
        do_ref[...] = (du * sg).astype(do_ref.dtype)
        dgate_ref[...] = (du * o * (sig * (1.0 + gt * (1.0 - sig)))).astype(dgate_ref.dtype)

        @pl.when(i == 0)
        def _():
            loss_ref[...] = jnp.zeros_like(loss_ref)
            dgf_ref[...] = jnp.zeros_like(dgf_ref)
            dwo_ref[...] = jnp.zeros_like(dwo_ref)

        loss_ref[...] += jnp.broadcast_to(part, loss_ref.shape)
        dgf_ref[...] += jnp.sum(dg_rows, axis=0, keepdims=True)
        dwo_ref[...] += _mm_tn(u, dx2)

    outs = [((S, D), jnp.float32), ((S, D), MXU), ((S, D), MXU)]
    return _pcall(
        body, name="head", grid=(S // T,), semantics=("arbitrary",),
        in_specs=[_rows(T, D), _rows(T, D), _rows(T, D), _full((D, D)), _full((1, D)), _rows(T, D)],
        out_specs=[_full((8, LANES)), _full((1, D)), _full((D, D))] + [_rows(T, D) for _ in outs],
        out_shape=[_sds((8, LANES), jnp.float32), _sds((1, D), jnp.float32), _sds((D, D), jnp.float32)]
        + [_sds(s, d) for s, d in outs],
    )(x1, o1, gate1, w_out1, g_f, target)


def _layer1_in_bwd(dq, dk, dv, dgate1, df, x1, dx2, g1, w_in1, gate0, o_m, o_s, w_out0):
    S = x1.shape[0]

    def body(dq_ref, dk_ref, dv_ref, dg1_ref, df_ref, x1_ref, dx2_ref, g_ref, w_ref, gate_ref, om_ref, os_ref,
             wo_ref, dz_ref, dx1_ref, dgn_ref, dwo_ref, dom_ref, dos_ref, dgate_ref):
        i = pl.program_id(0)
        dz_ref[:, 0:1024] = dq_ref[...]
        dz_ref[:, 1024:2048] = dk_ref[...]
        dz_ref[:, 2048:3072] = dv_ref[...]
        dz_ref[:, 3072:4096] = dg1_ref[...]
        dz_ref[:, 4096:4224] = df_ref[...]
        dh = _mm(dz_ref[...], w_ref[...])
        g = g_ref[...]
        dxn, dg_rows = _rms_bwd(x1_ref[...], g, dh)
        dx1 = dx2_ref[...] + dxn
        dx1_ref[...] = dx1
        du = _mm_nt(dx1, wo_ref[...])
        gt = gate_ref[...]
        sig = _sigmoid(gt)
        sg = gt * sig
        dsg = sig * (1.0 + gt * (1.0 - sig))
        dom_ref[...] = (du[:, :512] * sg[:, :512]).astype(dom_ref.dtype)
        dos_ref[...] = (du[:, 512:] * sg[:, 512:]).astype(dos_ref.dtype)
        dgate_ref[:, :512] = (du[:, :512] * om_ref[...] * dsg[:, :512]).astype(dgate_ref.dtype)
        dgate_ref[:, 512:] = (du[:, 512:] * os_ref[...] * dsg[:, 512:]).astype(dgate_ref.dtype)

        @pl.when(i == 0)
        def _():
            dgn_ref[...] = jnp.zeros_like(dgn_ref)
            dwo_ref[...] = jnp.zeros_like(dwo_ref)

        dgn_ref[...] += jnp.sum(dg_rows, axis=0, keepdims=True)
        dwo_ref[0:512, :] += _mm_tn(om_ref[...] * sg[:, :512], dx1)
        dwo_ref[512:1024, :] += _mm_tn(os_ref[...] * sg[:, 512:], dx1)

    return _pcall(
        body, name="layer1_in_bwd", grid=(S // TOK,), semantics=("arbitrary",),
        in_specs=[_rows(TOK, D), _rows(TOK, D), _rows(TOK, D), _rows(TOK, D), _rows(TOK, LANES), _rows(TOK, D),
                  _rows(TOK, D), _full((1, D)), _full(w_in1.shape), _rows(TOK, D), _rows(TOK, 512), _rows(TOK, 512),
                  _full((D, D))],
        out_specs=[_rows(TOK, 4224), _rows(TOK, D), _full((1, D)), _full((D, D)), _rows(TOK, 512), _rows(TOK, 512),
                   _rows(TOK, D)],
        out_shape=[_sds((S, 4224), MXU), _sds((S, D), jnp.float32), _sds((1, D), jnp.float32), _sds((D, D), jnp.float32),
                   _sds((S, 512), MXU), _sds((S, 512), MXU), _sds((S, D), MXU)],
    )(dq, dk, dv, dgate1, df, x1, dx2, g1, w_in1, gate0, o_m, o_s, w_out0)


def _layer0_in_bwd(dqm, dkm, dvm, dqs, dkd, dvd, dgate0, cos, sin, cq, ckv, x, dx1, g_in, w_in, g_q, w_q, g_kv, w_kv,
                   plan=None):
    S = x.shape[0]
    consts = _rope_consts()

    def body(dqm_ref, dkm_ref, dvm_ref, dqs_ref, dkd_ref, dvd_ref, dgate_ref, cos_ref, sin_ref, c_ref, cq_ref, ckv_ref,
             x_ref, dx1_ref, g_ref, w_ref, gq_ref, wq_ref, gkv_ref, wkv_ref,
             dx_ref, dz_ref, dgin_ref, dgq_ref, dgkv_ref, dwq_ref, dwkv_ref, dqu_ref, dkvu_ref):
        i = pl.program_id(0)
        lo = _lane_masks()
        sign = c_ref[...][1:2, :]
        c = cos_ref[...]
        s = sin_ref[...]
        dkpe = None
        for hd in range(N_MLA):
            sl = slice(LANES * hd, LANES * (hd + 1))
            dqu_ref[:, sl] = _rope_t(dqm_ref[:, sl], c, s, sign).astype(dqu_ref.dtype)
            dkh = dkm_ref[:, sl]
            dkvu_ref[:, sl] = jnp.where(lo, dkh, 0.0).astype(dkvu_ref.dtype)
            dkpe = dkh if dkpe is None else dkpe + dkh
        dkvu_ref[:, 1024:1536] = dvm_ref[...]
        dkpe = _rope_t(jnp.where(lo, 0.0, dkpe), c, s, sign)
        dcqn = _mm(dqu_ref[...], wq_ref[...])
        dckvn = _mm_nt(dkvu_ref[...], wkv_ref[...])
        gq = gq_ref[...]
        gkv = gkv_ref[...]
        dcq, dgq_rows = _rms_bwd(cq_ref[...], gq, dcqn)
        dckv, dgkv_rows = _rms_bwd(ckv_ref[...], gkv, dckvn)
        dz_ref[:, 0:256] = dcq.astype(dz_ref.dtype)
        dz_ref[:, 256:384] = dckv.astype(dz_ref.dtype)
        dz_ref[:, 384:512] = dkpe.astype(dz_ref.dtype)
        dz_ref[:, 512:1024] = dqs_ref[...]
        dz_ref[:, 1024:1536] = dkd_ref[...]
        dz_ref[:, 1536:2048] = dvd_ref[...]
        dz_ref[:, 2048:3072] = dgate_ref[...]
        dh = _mm(dz_ref[...], w_ref[...])
        g = g_ref[...]
        dxn, dg_rows = _rms_bwd(x_ref[...], g, dh)
        dx_ref[...] = dx1_ref[...] + dxn

        @pl.when(i == 0)
        def _():
            dgin_ref[...] = jnp.zeros_like(dgin_ref)
            dgq_ref[...] = jnp.zeros_like(dgq_ref)
            dgkv_ref[...] = jnp.zeros_like(dgkv_ref)
            dwq_ref[...] = jnp.zeros_like(dwq_ref)
            dwkv_ref[...] = jnp.zeros_like(dwkv_ref)

        dgin_ref[...] += jnp.sum(dg_rows, axis=0, keepdims=True)
        dgq_ref[...] += jnp.sum(dgq_rows, axis=0, keepdims=True)
        dgkv_ref[...] += jnp.sum(dgkv_rows, axis=0, keepdims=True)
        dwq_ref[...] += _mm_tn(dqu_ref[...], _rms(cq_ref[...], gq))
        dwkv_ref[...] += _mm_tn(_rms(ckv_ref[...], gkv), dkvu_ref[...])

    outs, rode = _pcall_riding(
        body, plan, [dqm, dkm, dvm, dqs, dkd, dvd, dgate0, cos, sin, consts, cq, ckv, x, dx1, g_in, w_in, g_q, w_q, g_kv, w_kv],
        name="layer0_in_bwd", grid=(S // TOK,),
        in_specs=[_rows(TOK, 1024), _rows(TOK, 1024), _rows(TOK, 512), _rows(TOK, 512), _rows(TOK, 512), _rows(TOK, 512),
                  _rows(TOK, D), _rows(TOK, LANES), _rows(TOK, LANES), _full((8, LANES)), _rows(TOK, 256), _rows(TOK, 128),
                  _rows(TOK, D), _rows(TOK, D), _full((1, D)), _full(w_in.shape), _full((1, 256)), _full(w_q.shape),
                  _full((1, 128)), _full(w_kv.shape)],
        out_specs=[_rows(TOK, D), _rows(TOK, 3072), _full((1, D)), _full((1, 256)), _full((1, 128)), _full(w_q.shape),
                   _full(w_kv.shape)],
        out_shape=[_sds((S, D), jnp.float32), _sds((S, 3072), MXU), _sds((1, D), jnp.float32), _sds((1, 256), jnp.float32),
                   _sds((1, 128), jnp.float32), _sds(w_q.shape, jnp.float32), _sds(w_kv.shape, jnp.float32)],
        scratch_shapes=[pltpu.VMEM((TOK, 1024), MXU), pltpu.VMEM((TOK, 1536), MXU)])
    return (*outs, rode)


def _wgrad(a, b, name):
    S, M = a.shape
    N = b.shape[1]
    tm = next(t for t in range(WG_ROWS, 0, -LANES) if M % t == 0)
    tn = N if N <= 1024 else 512
    tk = min(WG_TOK, S)

    def body(a_ref, b_ref, o_ref):
        @pl.when(pl.program_id(2) == 0)
        def _():
            o_ref[...] = jnp.zeros_like(o_ref)

        o_ref[...] += _mm_tn(a_ref[...], b_ref[...])

    return _pcall(
        body, name=name, grid=(M // tm, N // tn, S // tk), semantics=("parallel", "parallel", "arbitrary"),
        in_specs=[pl.BlockSpec((tk, tm), lambda m, n, k: (k, m)), pl.BlockSpec((tk, tn), lambda m, n, k: (k, n))],
        out_specs=pl.BlockSpec((tm, tn), lambda m, n, k: (m, n)),
        out_shape=_sds((M, N), jnp.float32),
    )(a, b)


def _adamw(w, g, m, v, name):
    shape = w.shape
    R, C = (int(np.prod(shape[:-1])), shape[-1])
    w2, g2, m2, v2 = (t.reshape(R, C) for t in (w, g, m, v))
    fits = [t for t in range(8, ADAM_TILE_BYTES // (4 * C) + 1, 8) if R % t == 0]
    tr = max(fits) if fits else R
    tc = C if (tr * C * 4 <= ADAM_TILE_BYTES or C % 256) else 256

    def body(w_ref, g_ref, m_ref, v_ref, d_ref, nm_ref, nv_ref):
        gg = g_ref[...]
        nm = B1 * m_ref[...] + (1.0 - B1) * gg
        nv = B2 * v_ref[...] + (1.0 - B2) * (gg * gg)
        m_hat = nm / (1.0 - B1 ** STEP)
        v_hat = nv / (1.0 - B2 ** STEP)
        d_ref[...] = -LR * (m_hat / (jnp.sqrt(v_hat) + AEPS) + WD * w_ref[...])
        nm_ref[...] = nm
        nv_ref[...] = nv

    spec = pl.BlockSpec((tr, tc), lambda i, j: (i, j))
    d, nm, nv = _pcall(
        body, name=name, grid=(R // tr, C // tc), semantics=("parallel", "parallel"),
        in_specs=[spec] * 4, out_specs=[spec] * 3, out_shape=[_sds((R, C), jnp.float32)] * 3,
    )(w2, g2, m2, v2)
    return d.reshape(shape), nm.reshape(shape), nv.reshape(shape)


def _sum_leading(a, name):
    n, R, C = a.shape
    tr = SUM_ROWS if R % SUM_ROWS == 0 else R

    def body(a_ref, o_ref):
        acc = a_ref[0]
        for i in range(1, n):
            acc = acc + a_ref[i]
        o_ref[...] = acc

    return _pcall(
        body, name=name, grid=(R // tr,), semantics=("parallel",),
        in_specs=[pl.BlockSpec((n, tr, C), lambda i: (0, i, 0))], out_specs=_rows(tr, C),
        out_shape=_sds((R, C), a.dtype),
    )(a)


def _add_halves(g, c, b, name, out_dtype):
    n, _, R, C = g.shape
    tr = SUM_ROWS if R % SUM_ROWS == 0 else R

    def body(c_ref, a_ref, b_ref, o_ref):
        o_ref[...] = (a_ref[0] + b_ref[...]).astype(o_ref.dtype)

    spec = pl.BlockSpec((1, tr, C), lambda k, i, c_ref: (k, i, 0))
    grid_spec = pltpu.PrefetchScalarGridSpec(
        num_scalar_prefetch=1, grid=(n, R // tr),
        in_specs=[pl.BlockSpec((1, 1, tr, C), lambda k, i, c_ref: (k, c_ref[0], i, 0)), spec], out_specs=spec)
    return _pcall(body, name=name, semantics=("parallel", "parallel"), grid_spec=grid_spec,
                  out_shape=_sds(b.shape, out_dtype))(c.reshape(1).astype(jnp.int32), g, b)


def _total_sum(g, theirs, chip, c, recv, name):
    _, _, R, C = g.shape
    n = recv.shape[0]
    tr = SUM_ROWS if R % SUM_ROWS == 0 else R

    def body(at_ref, a_ref, b_ref, r_ref, o_ref):
        acc = a_ref[0, 0] + b_ref[0]
        for i in range(n):
            acc = acc + r_ref[i].astype(jnp.float32)
        o_ref[...] = acc

    grid_spec = pltpu.PrefetchScalarGridSpec(
        num_scalar_prefetch=1, grid=(R // tr,),
        in_specs=[pl.BlockSpec((1, 1, tr, C), lambda i, at_ref: (at_ref[0], at_ref[1], i, 0)),
                  pl.BlockSpec((1, tr, C), lambda i, at_ref: (at_ref[0], i, 0)),
                  pl.BlockSpec((n, tr, C), lambda i, at_ref: (0, i, 0))],
        out_specs=pl.BlockSpec((tr, C), lambda i, at_ref: (i, 0)))
    return _pcall(body, name=name, semantics=("parallel",), grid_spec=grid_spec,
                  out_shape=_sds((R, C), jnp.float32))(jnp.stack([chip, c]).astype(jnp.int32), g, theirs, recv)


def _place():
    return lax.axis_index("x"), lax.axis_index("y"), lax.axis_index("c")


class _Plan:
    def __init__(self, arrays, out_shape, scratch, start, finish, middle=None):
        self.arrays, self.out_shape, self.scratch = list(arrays), list(out_shape), list(scratch)
        self.start, self.finish, self.middle = start, finish, middle


def _gather8_plan(block):
    R, C = block.shape

    def parts(ins, outs, sems):
        (x_ref,), (out_ref,), (send_sems, recv_sems) = ins, outs, sems
        x, y, c = _place()
        me, sibling = (x, y, c), (x, y, 1 - c)
        chips = [(1 - x, y), (x, 1 - y), (1 - x, 1 - y)]

        def copy(k, blk, to, src=None):
            slot = out_ref.at[4 * blk[0] + 2 * blk[1] + blk[2]]
            return pltpu.make_async_remote_copy(
                src_ref=slot if src is None else src, dst_ref=slot,
                send_sem=send_sems.at[k], recv_sem=recv_sems.at[k], device_id=to, device_id_type=MESH_ID)

        def first():
            return [copy(0, me, sibling, src=x_ref)] + [copy(1 + j, me, (*chip, c), src=x_ref) for j, chip in enumerate(chips)]

        def passed():
            return [copy(4 + j, (*chip, c), sibling) for j, chip in enumerate(chips)]

        def arrivals():
            return [copy(1 + j, (*chip, c), me) for j, chip in enumerate(chips)]

        def late():
            return [copy(0, sibling, me)] + [copy(4 + j, (*chip, 1 - c), me) for j, chip in enumerate(chips)]

        return first, passed, arrivals, late

    def start(ins, outs, sems):
        for cp in parts(ins, outs, sems)[0]():
            cp.start()

    def middle(ins, outs, sems):
        _, passed, arrivals, _ = parts(ins, outs, sems)
        for arrived, forward in zip(arrivals(), passed()):
            arrived.wait_recv()
            forward.start()

    def finish(ins, outs, sems):
        first, passed, _, late = parts(ins, outs, sems)
        for cp in late():
            cp.wait_recv()
        for cp in first() + passed():
            cp.wait_send()

    return _Plan([block], [_sds((8, R, C), block.dtype)], [pltpu.SemaphoreType.DMA((7,)), pltpu.SemaphoreType.DMA((7,))],
                 start, finish, middle)


def _fill_own_slot(gathered, block):
    x, y, c = _place()
    return lax.dynamic_update_index_in_dim(gathered, block, 4 * x + 2 * y + c, 0)


def _started_and_waited(arrays, out_shape, n, copies):
    def start(ins, outs, sems):
        for cp in copies(ins, outs, sems):
            cp.start()

    def finish(ins, outs, sems):
        for cp in copies(ins, outs, sems):
            cp.wait()

    return _Plan(arrays, out_shape, [pltpu.SemaphoreType.DMA((n,)), pltpu.SemaphoreType.DMA((n,))], start, finish)


def _pair_swap_plan(g):
    n = g.shape[0]

    def copies(ins, outs, sems):
        (g_ref,), (out_ref,), (send_sems, recv_sems) = ins, outs, sems
        x, y, c = _place()
        return [pltpu.make_async_remote_copy(src_ref=g_ref.at[k, 1 - c], dst_ref=out_ref.at[k], send_sem=send_sems.at[k],
                                             recv_sem=recv_sems.at[k], device_id=(x, y, 1 - c), device_id_type=MESH_ID)
                for k in range(n)]

    return _started_and_waited([g], [_sds((n,) + g.shape[2:], g.dtype)], n, copies)


def _chip_exchange_plan(p):
    def copies(ins, outs, sems):
        (p_ref,), (out_ref,), (send_sems, recv_sems) = ins, outs, sems
        x, y, c = _place()
        chips = [(1 - x, y), (x, 1 - y), (1 - x, 1 - y)]
        return [pltpu.make_async_remote_copy(
            src_ref=p_ref.at[2 * cx + cy], dst_ref=out_ref.at[j], send_sem=send_sems.at[j],
            recv_sem=recv_sems.at[j], device_id=(cx, cy, c), device_id_type=MESH_ID)
            for j, (cx, cy) in enumerate(chips)]

    return _started_and_waited([p], [_sds((3,) + p.shape[1:], p.dtype)], 3, copies)


def _pair_exchange_plan(t):
    def copies(ins, outs, sems):
        (t_ref,), (out_ref,), (send_sems, recv_sems) = ins, outs, sems
        x, y, c = _place()
        return [pltpu.make_async_remote_copy(src_ref=t_ref, dst_ref=out_ref, send_sem=send_sems.at[0], recv_sem=recv_sems.at[0],
                                             device_id=(x, y, 1 - c), device_id_type=MESH_ID)]

    return _started_and_waited([t], [_sds(t.shape, t.dtype)], 1, copies)


def _both_plans(a, b):
    na, ma, sa = len(a.arrays), len(a.out_shape), len(a.scratch)

    def phase(name):
        fa, fb = getattr(a, name), getattr(b, name)
        if fa is None and fb is None:
            return None

        def run(ins, outs, sems):
            if fa is not None:
                fa(ins[:na], outs[:ma], sems[:sa])
            if fb is not None:
                fb(ins[na:], outs[ma:], sems[sa:])
        return run

    return _Plan(a.arrays + b.arrays, a.out_shape + b.out_shape, a.scratch + b.scratch,
                 phase("start"), phase("finish"), phase("middle"))


ANY_SPEC = pl.BlockSpec(memory_space=pl.ANY)


def _run_plan(plan, name):
    n_in, n_out = len(plan.arrays), len(plan.out_shape)

    def body(*refs):
        ins, outs, sems = refs[:n_in], refs[n_in:n_in + n_out], refs[n_in + n_out:]
        plan.start(ins, outs, sems)
        if plan.middle is not None:
            plan.middle(ins, outs, sems)
        plan.finish(ins, outs, sems)

    return _pcall(body, name=name, in_specs=[ANY_SPEC] * n_in, out_specs=[ANY_SPEC] * n_out, out_shape=plan.out_shape,
                  scratch_shapes=plan.scratch)(*plan.arrays)


def _pcall_riding(body, plan, args, *, name, grid, in_specs, out_specs, out_shape, scratch_shapes):
    if plan is None:
        outs = _pcall(body, name=name, grid=grid, semantics=("arbitrary",), in_specs=in_specs, out_specs=out_specs,
                      out_shape=out_shape, scratch_shapes=scratch_shapes)(*args)
        return list(outs), None
    n_in, n_out, n_s = len(args), len(out_shape), len(scratch_shapes)
    p_in, p_out = len(plan.arrays), len(plan.out_shape)
    steps = grid[0]

    def riding(*refs):
        ins, pins = refs[:n_in], refs[n_in:n_in + p_in]
        o0 = n_in + p_in
        outs, pouts = refs[o0:o0 + n_out], refs[o0 + n_out:o0 + n_out + p_out]
        s0 = o0 + n_out + p_out
        scr, sems = refs[s0:s0 + n_s], refs[s0 + n_s:]
        j = pl.program_id(0)

        @pl.when(j == 0)
        def _():
            plan.start(pins, pouts, sems)

        if plan.middle is not None:
            @pl.when(j == steps // 2)
            def _():
                plan.middle(pins, pouts, sems)

        body(*ins, *outs, *scr)

        @pl.when(j == steps - 1)
        def _():
            plan.finish(pins, pouts, sems)

    res = _pcall(riding, name=name, grid=grid, semantics=("arbitrary",), in_specs=list(in_specs) + [ANY_SPEC] * p_in,
                 out_specs=list(out_specs) + [ANY_SPEC] * p_out, out_shape=list(out_shape) + plan.out_shape,
                 scratch_shapes=list(scratch_shapes) + plan.scratch)(*args, *plan.arrays)
    return list(res[:n_out]), list(res[n_out:])


class _RowSeq:
    def __init__(self, pieces):
        self.pieces = list(pieces)

    def rows(self, a, b):
        out, off = [], 0
        for p in self.pieces:
            lo, hi = max(a, off), min(b, off + p.shape[0])
            if lo < hi:
                out.append(p[lo - off:hi - off])
            off += p.shape[0]
        return out

    def array(self):
        return jnp.concatenate(self.pieces, axis=0)


def _row_seq(w):
    return w if isinstance(w, _RowSeq) else _RowSeq([w])


def _prep_w_in0(wt):
    wt = _row_seq(wt)
    one = wt.pieces[0]
    z32 = [jnp.zeros((32, one.shape[1]), one.dtype)]
    k0, k1 = wt.rows(928, 992), wt.rows(992, 1056)
    v0, v1 = wt.rows(1056, 1120), wt.rows(1120, 1184)
    return jnp.concatenate(wt.rows(0, 384) + z32 + z32 + wt.rows(384, 416) + z32 + wt.rows(416, 928)
                           + k0 * 4 + k1 * 4 + v0 * 4 + v1 * 4 + wt.rows(1184, 2208), axis=0)


def _fold_w_in0(d):
    def fold(blk):
        b = blk.reshape(8, 64, blk.shape[1])
        return jnp.concatenate([b[0] + b[1] + b[2] + b[3], b[4] + b[5] + b[6] + b[7]], axis=0)
    return _RowSeq([d[0:384], d[448:480], d[512:1024], fold(d[1024:1536]), fold(d[1536:2048]), d[2048:3072]])


def _prep_w_q(wt):
    return jnp.pad(wt.reshape(N_MLA, 96, Q_RANK), ((0, 0), (0, 32), (0, 0))).reshape(1024, Q_RANK)


def _fold_w_q(d):
    return d.reshape(N_MLA, 128, Q_RANK)[:, :96].reshape(768, Q_RANK)


def _prep_w_kv(w):
    w3 = w.reshape(KV_RANK, N_MLA, 128)
    kk = jnp.pad(w3[:, :, :64], ((0, 0), (0, 0), (0, 64))).reshape(KV_RANK, 1024)
    return jnp.concatenate([kk, w3[:, :, 64:].reshape(KV_RANK, 512)], axis=1)


def _fold_w_kv(d):
    kk = d[:, :1024].reshape(KV_RANK, N_MLA, 128)[:, :, :64]
    vv = d[:, 1024:].reshape(KV_RANK, N_MLA, 64)
    return jnp.concatenate([kk, vv], axis=2).reshape(KV_RANK, 1024)


W_IN1_SHARD = 1028
W_IN1_STEP = W_IN1_SHARD % 16


class _ShiftedShards:
    def __init__(self, blocks):
        self.blocks = list(blocks)


def _shifted_shard(a, chip, rows):
    out = jnp.zeros((rows, a.shape[1]), a.dtype)
    for k in range(4):
        out = jnp.where(chip == k, jnp.pad(a, ((W_IN1_STEP * k, rows - W_IN1_STEP * k - a.shape[0]), (0, 0))), out)
    return out


def _prep_w_in1(wt):
    if not isinstance(wt, _ShiftedShards):
        return jnp.concatenate([wt[0:3072], wt[3088:4112], wt[3072:3088], jnp.zeros((112, wt.shape[1]), wt.dtype)], axis=0)
    b = wt.blocks
    row = lax.broadcasted_iota(jnp.int32, (16, 1), 0)

    def seam(k, first, second):
        return jnp.where(row < W_IN1_STEP * (k + 1), first, second)

    return jnp.concatenate([
        b[0][0:1024], seam(0, b[0][1024:1040], b[1][0:16]), b[1][16:1024], seam(1, b[1][1024:1040], b[2][0:16]),
        b[2][16:1024], b[3][16:1040], seam(2, b[2][1024:1040], b[3][0:16]), jnp.zeros((112, 1024), b[0].dtype)], axis=0)


def _fold_w_in1(d):
    return _RowSeq([d[0:3072], d[4096:4112], d[3072:4096]])


class _Alone:
    def __init__(self, w_out0, o_g_in, w_in1, w_out1):
        self.layer1 = (w_out0, o_g_in, w_in1, w_out1)

    def gather_plan(self):
        return None

    def layer1_weights(self, rode):
        return self.layer1

    def swap_plan(self, grads1):
        return None

    def exchange_plan(self, rode):
        return None

    def halves_plan(self, rode):
        return None

    def finish(self, rode):
        pass


def _local_step(x, pos, target, e_g_in, w_in0, e_g_q, w_q, e_g_kv, w_kv, sinks, b_f, g_final, layer1):
    S = x.shape[0]
    w_in0p, w_qp, w_kvp = _prep_w_in0(w_in0), _prep_w_q(w_q), _prep_w_kv(w_kv)
    slopes = jnp.asarray(2.0 ** (-8.0 * (np.arange(N_SWA, dtype=np.float32) + 1.0) / N_SWA), jnp.float32)
    sinks1 = sinks.reshape(N_SWA)
    b_col = b_f.reshape(N_FOX, 1)

    (h0, cq, ckv, qm, km, vm, qs, kd, vd, gate0, cos, sin) = _layer0_in(
        x, pos, e_g_in, w_in0p, e_g_q, w_qp, e_g_kv, w_kvp)
    o_m, lse_m, rode = _attn_fwd_t(qm, km, vm, (NOPE + ROPE) ** -0.5, split=True, name="mla_fwd", plan=layer1.gather_plan())
    w_out0, o_g_in, w_in1, w_out1 = layer1.layer1_weights(rode)
    w_in1p = _prep_w_in1(w_in1)
    o_s, lse_s = _swa_fwd(qs, kd, vd, sinks1, slopes)
    x1, h1, q1, k1, v1, gate1, f_slab = _layer0_out_layer1_in(x, o_m, o_s, gate0, w_out0, o_g_in, w_in1p)
    f_row = f_slab[:, :N_FOX].T
    lc_row = _forget_fwd(f_row, b_col)
    lcc = lc_row.T
    o1, lse1, _ = _attn_fwd_t(q1, k1, v1, HEAD ** -0.5, split=False, name="fox_fwd", lcc=lcc)
    loss8, dg_final, dw_out1, dx2, do1, dgate1 = _head(x1, o1, gate1, w_out1, g_final, target)

    dq1, dk1, dv1, dlc, _ = _attn_bwd_t(q1, k1, v1, do1, o1, lse1, HEAD ** -0.5, split=False, name="fox_bwd", lcc=lcc)
    df_row, db_f = _forget_bwd(dlc.reshape(N_FOX, S), f_row, b_col)
    df_slab = jnp.pad(df_row.T, ((0, 0), (0, LANES - N_FOX))).astype(MXU)
    dz1, dx1, dg_o_in, dw_out0, do_m, do_s, dgate0 = _layer1_in_bwd(
        dq1, dk1, dv1, dgate1, df_slab, x1, dx2, o_g_in, w_in1p, gate0, o_m, o_s, w_out0)
    grads1 = dict(o_g_in=dg_o_in, o_w_in=_fold_w_in1(_wgrad(dz1, h1, "wgrad_in1")), o_w_out=dw_out1, e_w_out=dw_out0)
    dqs, dkd, dvd, dsink, rode = _swa_bwd(qs, kd, vd, do_s, o_s, lse_s, sinks1, slopes, plan=layer1.swap_plan(grads1))
    dqm, dkm, dvm, rode = _attn_bwd_t(qm, km, vm, do_m, o_m, lse_m, (NOPE + ROPE) ** -0.5, split=True, name="mla_bwd",
                                      plan=layer1.exchange_plan(rode))
    dx, dz0, dg_in, dg_q, dg_kv, dw_q, dw_kv, rode = _layer0_in_bwd(
        dqm, dkm, dvm, dqs, dkd, dvd, dgate0, cos, sin, cq, ckv, x, dx1, e_g_in, w_in0p, e_g_q, w_qp, e_g_kv, w_kvp,
        plan=layer1.halves_plan(rode))
    layer1.finish(rode)

    grads = dict(
        e_g_in=dg_in,
        e_w_in=_fold_w_in0(_wgrad(dz0, h0, "wgrad_in0")),
        e_g_q_a=dg_q,
        e_w_q_up=_fold_w_q(dw_q),
        e_g_kv_a=dg_kv,
        e_w_kv_up=_fold_w_kv(dw_kv),
        e_sinks=dsink[:, 0:4, 0].reshape(1, N_SWA),
        o_b_f=db_f.reshape(1, N_FOX),
        g_final=dg_final,
        **grads1,
    )
    return loss8[0, 0], dx, grads


SHARDED = ("e_w_in", "e_w_q_up", "e_w_kv_up", "e_w_out", "o_g_in", "o_w_in", "o_w_out")
TRANSPOSED = ("e_w_in", "e_w_q_up", "o_w_in")
COL_SHARDED = ("e_w_kv_up", "o_g_in")
REPLICATED = ("e_g_in", "e_g_q_a", "e_g_kv_a", "e_sinks", "o_b_f", "g_final")
FULL_SHAPES = dict(e_w_in=(2208, 1024), e_w_q_up=(768, 256), e_w_kv_up=(128, 1024), e_w_out=(1024, 1024),
                   o_g_in=(1, 1024), o_w_in=(4112, 1024), o_w_out=(1024, 1024))
GROUPS = dict(
    layer0=dict(rows=768, windows=dict(e_w_in=(0, 0), e_w_q_up=(560, 0), e_w_kv_up=(560, 256))),
    layer1=dict(rows=1568, windows=dict(o_w_in=(0, 0), o_w_out=(1040, 0), e_w_out=(1296, 0), o_g_in=(1552, 0))),
)


def _shard_shape(name):
    r, c = FULL_SHAPES[name]
    return (r, c // 4) if name in COL_SHARDED else (r // 4, c)


def _as_handled(name, a):
    a = a[0] if a.ndim == 3 else a
    return a.T if name in TRANSPOSED else a


def _as_given(name, a, shape):
    return (a.T if name in TRANSPOSED else a).reshape(shape)


def _pack_block(p, group, shifted_for=None):
    def rows(a, n):
        return jnp.pad(a, ((0, n - a.shape[0]), (0, 0)))

    if group == "layer0":
        band = jnp.concatenate([p["e_w_q_up"], rows(p["e_w_kv_up"], 192), jnp.zeros((192, 512), p["e_w_in"].dtype)], axis=1)
        return jnp.concatenate([rows(p["e_w_in"], 560), rows(band, 208)], axis=0)
    g = p["o_g_in"]
    band = jnp.pad(g, ((0, 16 - g.shape[0]), (0, PACK_COLS - g.shape[1])))
    w_in = rows(p["o_w_in"], 1040) if shifted_for is None else _shifted_shard(p["o_w_in"], shifted_for, 1040)
    return jnp.concatenate([w_in, p["o_w_out"], p["e_w_out"], band], axis=0)


def _window(block, group, name, width=None):
    r0, c0 = GROUPS[group]["windows"][name]
    r, c = _shard_shape(name)
    return block[..., r0:r0 + r, c0:c0 + (c if width is None else width)]


def _chip_slice(name, full, k):
    r, c = _shard_shape(name)
    if isinstance(full, _RowSeq):
        return jnp.concatenate(full.rows(r * k, r * (k + 1)), axis=0)
    return full[:, c * k:c * (k + 1)] if name in COL_SHARDED else full[r * k:r * (k + 1), :]


def _packed_weights(w, group):
    parts = {}
    for n in GROUPS[group]["windows"]:
        a = _as_handled(n, w[n])
        parts[n] = lax.bitcast_convert_type(a, jnp.bfloat16).reshape(1, -1) if n == "o_g_in" else a.astype(jnp.bfloat16)
    x, y, _ = _place()
    halves = _pack_block(parts, group, shifted_for=2 * x + y).reshape(2, GROUPS[group]["rows"] // 2, PACK_COLS)
    return lax.dynamic_index_in_dim(halves, lax.axis_index("c"), 0, keepdims=False)


def _unpacked_weights(gathered, half, group):
    blocks = _fill_own_slot(gathered, half).reshape(4, GROUPS[group]["rows"], PACK_COLS)
    full = {}
    for n in GROUPS[group]["windows"]:
        if n == "o_g_in":
            halves = _window(blocks, group, n, width=512).reshape(4, 1, 256, 2)
            full[n] = jnp.concatenate(list(lax.bitcast_convert_type(halves, jnp.float32)), axis=1)
        elif n == "o_w_in":
            full[n] = _ShiftedShards(blocks[k, 0:1040].astype(MXU) for k in range(4))
        else:
            pieces = [_window(blocks[k], group, n).astype(MXU) for k in range(4)]
            if n == "e_w_in":
                full[n] = _RowSeq(pieces)
            else:
                full[n] = jnp.concatenate(pieces, axis=1 if n in COL_SHARDED else 0)
    return full


class _GroupReduce:
    def __init__(self, group):
        self.group = group
        self.c = lax.axis_index("c")
        self.chip = 2 * lax.axis_index("x") + lax.axis_index("y")

    def swap_plan(self, grads):
        names = GROUPS[self.group]["windows"]
        per_chip = jnp.concatenate([_pack_block({n: _chip_slice(n, grads[n], k) for n in names}, self.group)
                                    for k in range(4)], axis=0)
        self.g4 = per_chip.reshape(4, 2, GROUPS[self.group]["rows"] // 2, PACK_COLS)
        return _pair_swap_plan(self.g4)

    def exchange_plan(self, rode):
        self.theirs = rode[0]
        return _chip_exchange_plan(_add_halves(self.g4, self.c, self.theirs, "pair_add_" + self.group, jnp.bfloat16))

    def halves_plan(self, rode):
        self.my_half = _total_sum(self.g4, self.theirs, self.chip, self.c, rode[0], "chip_sum_" + self.group)
        return _pair_exchange_plan(self.my_half)

    def finish(self, rode):
        my_half, other_half = self.my_half, rode[0]
        total = jnp.concatenate([jnp.where(self.c == 0, my_half, other_half), jnp.where(self.c == 0, other_half, my_half)], axis=0)
        self.sums = {n: _window(total, self.group, n) for n in GROUPS[self.group]["windows"]}

    def run(self, grads, beside):
        swap = self.swap_plan(grads)
        outs = _run_plan(_both_plans(swap, beside), "pair_swap_" + self.group)
        rode, others = outs[:len(swap.out_shape)], outs[len(swap.out_shape):]
        halves = self.halves_plan(_run_plan(self.exchange_plan(rode), "chip_exchange_" + self.group))
        self.finish(_run_plan(halves, "pair_exchange_" + self.group))
        return self.sums, others


class _Layer1Exchange(_GroupReduce):
    def __init__(self, w):
        super().__init__("layer1")
        self.half = _packed_weights(w, "layer1")

    def gather_plan(self):
        return _gather8_plan(self.half)

    def layer1_weights(self, rode):
        full = _unpacked_weights(rode[0], self.half, "layer1")
        return full["e_w_out"], full["o_g_in"], full["o_w_in"], full["o_w_out"]


def kernel(x, positions, e_g_in, e_w_in, e_g_q_a, e_w_q_up, e_g_kv_a, e_w_kv_up, e_sinks, e_w_out, o_g_in, o_w_in, o_b_f, o_w_out, g_final, loss_target, m_e_g_in, m_e_w_in, m_e_g_q_a, m_e_w_q_up, m_e_g_kv_a, m_e_w_kv_up, m_e_sinks, m_e_w_out, m_o_g_in, m_o_w_in, m_o_b_f, m_o_w_out, m_g_final, v_e_g_in, v_e_w_in, v_e_g_q_a, v_e_w_q_up, v_e_g_kv_a, v_e_w_kv_up, v_e_sinks, v_e_w_out, v_o_g_in, v_o_w_in, v_o_b_f, v_o_w_out, v_g_final):
    w = dict(e_g_in=e_g_in, e_w_in=e_w_in, e_g_q_a=e_g_q_a, e_w_q_up=e_w_q_up, e_g_kv_a=e_g_kv_a, e_w_kv_up=e_w_kv_up,
             e_sinks=e_sinks, e_w_out=e_w_out, o_g_in=o_g_in, o_w_in=o_w_in, o_b_f=o_b_f, o_w_out=o_w_out, g_final=g_final)
    m = dict(e_g_in=m_e_g_in, e_w_in=m_e_w_in, e_g_q_a=m_e_g_q_a, e_w_q_up=m_e_w_q_up, e_g_kv_a=m_e_g_kv_a,
             e_w_kv_up=m_e_w_kv_up, e_sinks=m_e_sinks, e_w_out=m_e_w_out, o_g_in=m_o_g_in, o_w_in=m_o_w_in, o_b_f=m_o_b_f,
             o_w_out=m_o_w_out, g_final=m_g_final)
    v = dict(e_g_in=v_e_g_in, e_w_in=v_e_w_in, e_g_q_a=v_e_g_q_a, e_w_q_up=v_e_w_q_up, e_g_kv_a=v_e_g_kv_a,
             e_w_kv_up=v_e_w_kv_up, e_sinks=v_e_sinks, e_w_out=v_e_w_out, o_g_in=v_o_g_in, o_w_in=v_o_w_in, o_b_f=v_o_b_f,
             o_w_out=v_o_w_out, g_final=v_g_final)
    order = ("e_g_in", "e_w_in", "e_g_q_a", "e_w_q_up", "e_g_kv_a", "e_w_kv_up", "e_sinks", "e_w_out", "o_g_in", "o_w_in",
             "o_b_f", "o_w_out", "g_final")
    half0 = _packed_weights(w, "layer0")
    full = _unpacked_weights(_run_plan(_gather8_plan(half0), "gather_weights_layer0")[0], half0, "layer0")
    layer1 = _Layer1Exchange(w)

    loss_part, dx, grads = _local_step(
        x[0], positions.reshape(-1, 1), loss_target[0], e_g_in, full["e_w_in"], e_g_q_a, full["e_w_q_up"], e_g_kv_a,
        full["e_w_kv_up"], e_sinks, o_b_f, g_final.reshape(1, D), layer1)

    small = jnp.concatenate([jnp.pad(loss_part.reshape(1), (0, LANES - 1))]
                            + [jnp.pad(grads[n].reshape(-1), (0, (-grads[n].size) % LANES)) for n in REPLICATED])
    rows = small.shape[0] // LANES
    small = jnp.pad(small.reshape(rows, LANES), ((0, (-rows) % 8), (0, 0)))
    sums0, (gathered_small,) = _GroupReduce("layer0").run(grads, _gather8_plan(small))
    gsum = {**layer1.sums, **sums0}
    ssum = _sum_leading(_fill_own_slot(gathered_small, small), "small_grad_sum").reshape(-1)
    loss = ssum[0]
    off = LANES
    for n in REPLICATED:
        cnt = w[n].size
        gsum[n] = ssum[off:off + cnt].reshape(w[n].shape)
        off += cnt + (-cnt) % LANES

    grad, delta, new_m, new_v = {}, {}, {}, {}
    for n in order:
        if n == "o_w_in":
            def tiles(a):
                return jnp.transpose(a, (2, 0, 1)).reshape(-1, LANES)

            def given(a):
                return jnp.transpose(a.reshape(-1, 8, LANES), (1, 2, 0)).reshape(w[n].shape)

            g_t = gsum[n].reshape(-1, LANES)
            outs = _adamw(tiles(w[n]), g_t, tiles(m[n]), tiles(v[n]), "adamw_" + n)
            grad[n], delta[n], new_m[n], new_v[n] = (given(a) for a in (g_t,) + outs)
        elif n in SHARDED:
            outs = _adamw(_as_handled(n, w[n]), gsum[n], _as_handled(n, m[n]), _as_handled(n, v[n]), "adamw_" + n)
            grad[n], delta[n], new_m[n], new_v[n] = (_as_given(n, a, w[n].shape) for a in (gsum[n],) + outs)
        else:
            grad[n] = gsum[n]
            delta[n], new_m[n], new_v[n] = _adamw(w[n], gsum[n], m[n], v[n], "adamw_" + n)
    return (loss, dx[None], *[grad[n] for n in order], *[delta[n] for n in order], *[new_m[n] for n in order],
            *[new_v[n] for n in order])
```

```python
import math

import numpy as np
import jax
import jax.numpy as jnp
from jax import lax
from jax.experimental import pallas as pl
from jax.experimental.pallas import tpu as pltpu

D = 1024
EPS = 1e-6
ROPE_THETA = 10000.0
N_MLA = 8
Q_RANK = 256
KV_RANK = 128
NOPE = 64
ROPE = 32
N_SWA = 8
WINDOW = 128
N_FOX = 16
HEAD = 64
LR, B1, B2, AEPS, WD, STEP = 0.001, 0.9, 0.999, 1e-08, 0.01, 10

LANES = 128
HALF = 64
VMEM_LIMIT = 56 * 1024 * 1024
MXU = jnp.bfloat16
TOK = 256
HEAD_TOK = 512
WG_TOK = 2048
WG_ROWS = 1536
ATT = 256
FWD_CHUNK = 2
BWD_CHUNK = 2
SWA_GROUP = 8
NEG = float("-inf")

PACK_COLS = 1024
SUM_ROWS = 256
ADAM_TILE_BYTES = 2 << 20
MESH_ID = pl.DeviceIdType.MESH


def _pcall(body, *, name, vmem=VMEM_LIMIT, semantics=None, **kw):
    params = dict(vmem_limit_bytes=vmem)
    if semantics is not None:
        params["dimension_semantics"] = semantics
    return pl.pallas_call(body, name=name, compiler_params=pltpu.CompilerParams(**params), **kw)


def _mm(a, b):
    return jnp.dot(a.astype(MXU), b.astype(MXU), preferred_element_type=jnp.float32)


def _mm_nt(a, b):
    return lax.dot_general(a.astype(MXU), b.astype(MXU), (((1,), (1,)), ((), ())),
                           preferred_element_type=jnp.float32)


def _mm_tn(a, b):
    return lax.dot_general(a.astype(MXU), b.astype(MXU), (((0,), (0,)), ((), ())),
                           preferred_element_type=jnp.float32)


def _full(shape):
    n = len(shape)
    return pl.BlockSpec(shape, lambda *_: (0,) * n)


def _rows(tm, n):
    return pl.BlockSpec((tm, n), lambda i: (i, 0))


def _sds(shape, dtype):
    return jax.ShapeDtypeStruct(shape, dtype)


def _rms(x, g):
    r = lax.rsqrt(jnp.mean(x * x, axis=-1, keepdims=True) + EPS)
    return x * r * g


def _rms_bwd(x, g, dy):
    r = lax.rsqrt(jnp.mean(x * x, axis=-1, keepdims=True) + EPS)
    xh = x * r
    dxh = dy * g
    dx = r * (dxh - xh * jnp.mean(dxh * xh, axis=-1, keepdims=True))
    return dx, dy * xh


def _sigmoid(x):
    return 1.0 / (1.0 + jnp.exp(-x))


def _lane_masks():
    lane = lax.broadcasted_iota(jnp.int32, (1, LANES), 1)
    return lane < HALF


def _split_heads(a, lo):
    z = jnp.zeros_like(a)
    return [jnp.where(lo, a, z), jnp.where(lo, z, a)]


def _rope_consts():
    inv = np.zeros((8, LANES), np.float32)
    j = np.arange(ROPE // 2, dtype=np.float32)
    f = (1.0 / (ROPE_THETA ** (np.arange(0, ROPE, 2, dtype=np.float32) / ROPE))).astype(np.float32)
    inv[0, HALF:HALF + 16] = f
    inv[0, HALF + 16:HALF + 32] = f
    inv[1, HALF:HALF + 16] = -1.0
    inv[1, HALF + 16:HALF + 32] = 1.0
    del j
    return jnp.asarray(inv)


def _rope_tables(pos_f, consts):
    ang = pos_f * consts[0:1, :]
    sign = consts[1:2, :]
    c = jnp.where(sign != 0.0, jnp.cos(ang), 1.0)
    s = jnp.sin(ang) * sign
    return c, s


def _swap_halves(v, sign):
    lo = pltpu.roll(v, LANES - 16, axis=1)
    hi = pltpu.roll(v, 16, axis=1)
    return jnp.where(sign < 0.0, lo, jnp.where(sign > 0.0, hi, 0.0))


def _rope(x, c, s, sign):
    return x * c + _swap_halves(x, sign) * s


def _rope_t(dy, c, s, sign):
    return dy * c + _swap_halves(dy * s, sign)


def _layer0_in(x, pos, g_in, w_in, g_q, w_q, g_kv, w_kv):
    S = x.shape[0]
    T = math.gcd(HEAD_TOK, S)
    consts = _rope_consts()

    def body(x_ref, pos_ref, c_ref, g_ref, w_ref, gq_ref, wq_ref, gkv_ref, wkv_ref,
             h_ref, cq_ref, ckv_ref, qm_ref, km_ref, vm_ref,
             qs_ref, kd_ref, vd_ref, gate_ref, cos_ref, sin_ref):
        h = _rms(x_ref[...], g_ref[...])
        h_ref[...] = h.astype(h_ref.dtype)
        z = _mm_nt(h, w_ref[...])
        cq = z[:, 0:256]
        ckv = z[:, 256:384]
        kpe = z[:, 384:512]
        cq_ref[...] = cq
        ckv_ref[...] = ckv
        qs_ref[...] = z[:, 512:1024].astype(qs_ref.dtype)
        kd_ref[...] = z[:, 1024:1536].astype(kd_ref.dtype)
        vd_ref[...] = z[:, 1536:2048].astype(vd_ref.dtype)
        gate_ref[...] = z[:, 2048:3072]
        cqn = _rms(cq, gq_ref[...])
        ckvn = _rms(ckv, gkv_ref[...])
        q = _mm_nt(cqn, wq_ref[...])
        kv = _mm(ckvn, wkv_ref[...])
        vm_ref[...] = kv[:, 1024:1536].astype(vm_ref.dtype)
        consts_v = c_ref[...]
        sign = consts_v[1:2, :]
        c, s = _rope_tables(pos_ref[...].astype(jnp.float32), consts_v)
        cos_ref[...] = c
        sin_ref[...] = s
        kpe_r = _rope(kpe, c, s, sign)
        for hd in range(N_MLA):
            sl = slice(LANES * hd, LANES * (hd + 1))
            qm_ref[:, sl] = _rope(q[:, sl], c, s, sign).astype(qm_ref.dtype)
            km_ref[:, sl] = (kv[:, sl] + kpe_r).astype(km_ref.dtype)

    outs = [
        ((S, D), MXU), ((S, 256), jnp.float32), ((S, 128), jnp.float32),
        ((S, 1024), MXU), ((S, 1024), MXU), ((S, 512), MXU), ((S, 512), MXU), ((S, 512), MXU), ((S, 512), MXU),
        ((S, 1024), jnp.float32), ((S, 128), jnp.float32), ((S, 128), jnp.float32),
    ]
    return _pcall(
        body, name="layer0_in", grid=(S // T,), semantics=("arbitrary",),
        in_specs=[_rows(T, D), _rows(T, 1), _full((8, LANES)), _full((1, D)), _full(w_in.shape), _full((1, 256)),
                  _full(w_q.shape), _full((1, 128)), _full(w_kv.shape)],
        out_specs=[_rows(T, s[1]) for s, _ in outs],
        out_shape=[_sds(s, d) for s, d in outs],
    )(x, pos, consts, g_in, w_in, g_q, w_q, g_kv, w_kv)


AUG = (HALF, 0)
ONE = (HALF + 8, 8)


def _data_lanes(idx, h):
    return (idx < HALF) if h == 0 else (idx >= HALF)


def _three_terms(x):
    hi = x.astype(MXU).astype(jnp.float32)
    mid = (x - hi).astype(MXU).astype(jnp.float32)
    lo = (x - hi - mid).astype(MXU).astype(jnp.float32)
    return hi, mid, lo


def _q_aug(qblk, lc, h, scale, lane):
    a = AUG[h]
    hi, mid, lo = _three_terms(lc)
    ones = ((lane >= a + 3) & (lane <= a + 5)).astype(jnp.float32)
    aug = jnp.where(lane == a, hi, jnp.where(lane == a + 1, mid, jnp.where(lane == a + 2, lo, ones)))
    return jnp.where(_data_lanes(lane, h), qblk * jnp.asarray(scale, qblk.dtype), aug.astype(qblk.dtype))


def _k_aug(kblk, lc, h, lane):
    a = AUG[h]
    hi, mid, lo = _three_terms(-lc)
    ones = ((lane >= a) & (lane <= a + 2)).astype(jnp.float32)
    aug = jnp.where(lane == a + 3, hi, jnp.where(lane == a + 4, mid, jnp.where(lane == a + 5, lo, ones)))
    return jnp.where(_data_lanes(lane, h), kblk, aug.astype(kblk.dtype))


def _lc_col(lc_ref, r0, rows, h):
    head = lax.broadcasted_iota(jnp.int32, (1, lc_ref.shape[1]), 1)
    return jnp.sum(jnp.where(head == 2 * pl.program_id(0) + h, lc_ref[pl.ds(r0, rows), :], 0.0), axis=1, keepdims=True)


def _attn_fwd_t(q, k, v, scale, *, split, name, lcc=None, plan=None):
    S = q.shape[0]
    npair = v.shape[1] // LANES
    W = 2 * LANES if split else LANES
    T = ATT
    CH = FWD_CHUNK * T
    assert S % CH == 0
    nq = S // T

    def body(*refs):
        if split:
            q_ref, k_ref, v_ref, o_ref, lse_ref, vt, acc, m_sc = refs
        else:
            q_ref, k_ref, v_ref, lcc_ref, o_ref, lse_ref, kaug, vt, acc, m_sc = refs
        lane = lax.broadcasted_iota(jnp.int32, (1, LANES), 1)
        sub = lax.broadcasted_iota(jnp.int32, (LANES, 1), 0)
        key_minus_qry = lax.broadcasted_iota(jnp.int32, (CH, T), 0) - lax.broadcasted_iota(jnp.int32, (CH, T), 1)

        def prep(i, c):
            r0 = pl.multiple_of(i * T, T)
            vblk = v_ref[pl.ds(r0, T), :].astype(jnp.float32)
            for h in (0, 1):
                vh = jnp.where(_data_lanes(lane, h), vblk, (lane == ONE[h]).astype(jnp.float32))
                vt[h, :, pl.ds(r0, T)] = vh.T.astype(vt.dtype)
                if not split:
                    kaug[h, pl.ds(r0, T), :] = _k_aug(k_ref[pl.ds(r0, T), :], _lc_col(lcc_ref, r0, T, h), h, lane)
            return c

        lax.fori_loop(0, nq, prep, 0)

        def queries(qi):
            q0 = pl.multiple_of(qi * T, T)
            qblk = q_ref[pl.ds(q0, T), :]
            if split:
                return (qblk[:, :LANES], qblk[:, LANES:])
            return tuple(_q_aug(qblk, _lc_col(lcc_ref, q0, T, h), h, scale, lane) for h in (0, 1))

        def scores(qs, c):
            k0 = pl.multiple_of(c * CH, CH)
            out = []
            for h in (0, 1):
                if split:
                    out.append(_mm_nt(k_ref[pl.ds(k0, CH), LANES * h:LANES * (h + 1)], qs[h]) * scale)
                else:
                    out.append(_mm_nt(kaug[h, pl.ds(k0, CH), :], qs[h]))
            return tuple(out)

        def q_block(qi, carry):
            qs, first_scores = carry[:2], carry[2:]
            q0 = pl.multiple_of(qi * T, T)
            acc[...] = jnp.zeros_like(acc)
            m_sc[...] = jnp.full(m_sc.shape, NEG, jnp.float32)

            def absorb(c, sts, masked):
                k0 = pl.multiple_of(c * CH, CH)
                for h in (0, 1):
                    st = sts[h]
                    if masked:
                        st = jnp.where(key_minus_qry <= q0 - k0, st, NEG)
                    m_old = m_sc[h:h + 1, :]
                    m_new = jnp.maximum(m_old, jnp.max(st, axis=0, keepdims=True))
                    alpha = jnp.exp(m_old - m_new)
                    pt = jnp.exp(st - m_new)
                    acc[h] = alpha * acc[h] + _mm(vt[h, :, pl.ds(k0, CH)], pt)
                    m_sc[h:h + 1, :] = m_new

            last = qi // FWD_CHUNK

            def pipelined(c, sts):
                nxt = scores(qs, c + 1)
                absorb(c, sts, False)
                return nxt

            sts = lax.fori_loop(0, last, pipelined, first_scores)
            qs_next = queries(jnp.minimum(qi + 1, nq - 1))
            nxt = qs_next + scores(qs_next, 0)
            absorb(last, sts, True)
            ot = None
            for h in (0, 1):
                a = acc[h]
                l = a[ONE[h]:ONE[h] + 1, :]
                oh = jnp.where(_data_lanes(sub, h), a * (1.0 / l), 0.0)
                ot = oh if ot is None else ot + oh
                lse_ref[0, h:h + 1, pl.ds(q0, T)] = m_sc[h:h + 1, :] + jnp.log(l)
            o_ref[pl.ds(q0, T), :] = ot.T
            return nxt

        qs0 = queries(0)
        lax.fori_loop(0, nq, q_block, qs0 + scores(qs0, 0))

    wide = pl.BlockSpec((S, W), lambda j: (0, j))
    slab = pl.BlockSpec((S, LANES), lambda j: (0, j))
    rows = pl.BlockSpec((1, 2, S), lambda j: (j, 0, 0))
    in_specs = [wide, wide, slab]
    args = [q, k, v]
    scratch = []
    if not split:
        in_specs.append(_full(lcc.shape))
        args.append(lcc)
        scratch.append(pltpu.VMEM((2, S, LANES), MXU))
    scratch += [pltpu.VMEM((2, LANES, S), MXU), pltpu.VMEM((2, LANES, T), jnp.float32), pltpu.VMEM((8, T), jnp.float32)]
    (o, lse), rode = _pcall_riding(
        body, plan, args, name=name, grid=(npair,), in_specs=in_specs, out_specs=[slab, rows],
        out_shape=[_sds((S, npair * LANES), jnp.float32), _sds((npair, 2, S), jnp.float32)], scratch_shapes=scratch)
    return o, lse, rode


def _attn_bwd_t(q, k, v, do, o, lse, scale, *, split, name, lcc=None, plan=None):
    S = q.shape[0]
    npair = v.shape[1] // LANES
    W = 2 * LANES if split else LANES
    T = ATT
    CH = BWD_CHUNK * T
    assert S % CH == 0
    nq = S // T

    def body(*refs):
        if split:
            (q_ref, k_ref, v_ref, do_ref, o_ref, lse_ref, dq_ref, dk_ref, dv_ref, dqt, delta, dk_acc, dv_acc) = refs
        else:
            (q_ref, k_ref, v_ref, do_ref, o_ref, lse_ref, lcc_ref, dq_ref, dk_ref, dv_ref, dlc_ref,
             dqt, delta, dk_acc, dv_acc, qaug, csum) = refs
        lane = lax.broadcasted_iota(jnp.int32, (1, LANES), 1)
        sub = lax.broadcasted_iota(jnp.int32, (LANES, 1), 0)
        key_minus_qry = lax.broadcasted_iota(jnp.int32, (T, CH), 0) - lax.broadcasted_iota(jnp.int32, (T, CH), 1)

        def prep(i, c):
            r0 = pl.multiple_of(i * T, T)
            prod_t = (do_ref[pl.ds(r0, T), :].astype(jnp.float32) * o_ref[pl.ds(r0, T), :]).T
            for h in (0, 1):
                delta[h:h + 1, pl.ds(r0, T)] = jnp.sum(jnp.where(_data_lanes(sub, h), prod_t, 0.0), axis=0, keepdims=True)
                dqt[h, :, pl.ds(r0, T)] = jnp.zeros((LANES, T), jnp.float32)
                if not split:
                    qaug[h, pl.ds(r0, T), :] = _q_aug(q_ref[pl.ds(r0, T), :], _lc_col(lcc_ref, r0, T, h), h, scale, lane)
            return c

        lax.fori_loop(0, nq, prep, 0)

        def keys(ki):
            k0 = pl.multiple_of(ki * T, T)
            kblk = k_ref[pl.ds(k0, T), :]
            if split:
                return (kblk[:, :LANES], kblk[:, LANES:])
            return tuple(_k_aug(kblk, _lc_col(lcc_ref, k0, T, h), h, lane) for h in (0, 1))

        def q_of(c, h):
            q0 = pl.multiple_of(c * CH, CH)
            if split:
                return q_ref[pl.ds(q0, CH), LANES * h:LANES * (h + 1)]
            return qaug[h, pl.ds(q0, CH), :]

        def scores(khs, c):
            out = []
            for h in (0, 1):
                st = _mm_nt(khs[h], q_of(c, h))
                out.append(st * scale if split else st)
            return tuple(out)

        def k_block(ki, carry):
            khs, first_scores = carry[:2], carry[2:]
            k0 = pl.multiple_of(ki * T, T)
            khts = [kh.astype(jnp.float32).T.astype(kh.dtype) for kh in khs]
            vhs = _split_heads(v_ref[pl.ds(k0, T), :], lane < HALF)
            dk_acc[...] = jnp.zeros_like(dk_acc)
            dv_acc[...] = jnp.zeros_like(dv_acc)

            def absorb(c, vals):
                q0 = pl.multiple_of(c * CH, CH)
                dos = _split_heads(do_ref[pl.ds(q0, CH), :], lane < HALF)
                visible = key_minus_qry <= q0 - k0
                for h in (0, 1):
                    dpt = _mm_nt(vhs[h], dos[h])
                    st = jnp.where(visible, vals[h], NEG)
                    pt = jnp.exp(st - lse_ref[0, h:h + 1, pl.ds(q0, CH)])
                    dv_acc[...] += _mm(pt, dos[h])
                    dst = pt * (dpt - delta[h:h + 1, pl.ds(q0, CH)])
                    dk_acc[h] += _mm(dst, q_of(c, h))
                    dqt[h, :, pl.ds(q0, CH)] += _mm(khts[h], dst)

            first = ki // BWD_CHUNK

            def pipelined(c, vals):
                nxt = scores(khs, c + 1)
                absorb(c, vals)
                return nxt

            vals = lax.fori_loop(first, S // CH - 1, pipelined, first_scores)
            kn = jnp.minimum(ki + 1, nq - 1)
            khs_next = keys(kn)
            nxt = khs_next + scores(khs_next, kn // BWD_CHUNK)
            absorb(S // CH - 1, vals)
            if split:
                dk_ref[pl.ds(k0, T), :LANES] = (dk_acc[0] * scale).astype(dk_ref.dtype)
                dk_ref[pl.ds(k0, T), LANES:] = (dk_acc[1] * scale).astype(dk_ref.dtype)
            else:
                dk_ref[pl.ds(k0, T), :] = jnp.where(lane < HALF, dk_acc[0], dk_acc[1]).astype(dk_ref.dtype)
                for h in (0, 1):
                    csum[h:h + 1, pl.ds(k0, T)] = dk_acc[h].T[AUG[h] + 3:AUG[h] + 4, :]
            dv_ref[pl.ds(k0, T), :] = dv_acc[...].astype(dv_ref.dtype)
            return nxt

        khs0 = keys(0)
        lax.fori_loop(0, nq, k_block, khs0 + scores(khs0, 0))

        def finish(i, c):
            r0 = pl.multiple_of(i * T, T)
            if split:
                for h in (0, 1):
                    dq_ref[pl.ds(r0, T), LANES * h:LANES * (h + 1)] = (dqt[h, :, pl.ds(r0, T)].T * scale).astype(dq_ref.dtype)
            else:
                d = jnp.where(sub < HALF, dqt[0, :, pl.ds(r0, T)], dqt[1, :, pl.ds(r0, T)])
                dq_ref[pl.ds(r0, T), :] = (d.T * scale).astype(dq_ref.dtype)
                for h in (0, 1):
                    dlc_ref[0, h:h + 1, pl.ds(r0, T)] = dqt[h, AUG[h]:AUG[h] + 1, pl.ds(r0, T)] - csum[h:h + 1, pl.ds(r0, T)]
            return c

        lax.fori_loop(0, nq, finish, 0)

    wide = pl.BlockSpec((S, W), lambda j: (0, j))
    slab = pl.BlockSpec((S, LANES), lambda j: (0, j))
    rows = pl.BlockSpec((1, 2, S), lambda j: (j, 0, 0))
    in_specs = [wide, wide, slab, slab, slab, rows]
    args = [q, k, v, do, o, lse]
    out_specs = [wide, wide, slab]
    out_shape = [_sds(q.shape, jnp.float32 if split else do.dtype), _sds(k.shape, jnp.float32 if split else do.dtype),
                 _sds(v.shape, do.dtype)]
    scratch = [pltpu.VMEM((2, LANES, S), jnp.float32), pltpu.VMEM((8, S), jnp.float32),
               pltpu.VMEM((2, T, LANES), jnp.float32), pltpu.VMEM((T, LANES), jnp.float32)]
    if not split:
        in_specs.append(_full(lcc.shape))
        args.append(lcc)
        out_specs.append(rows)
        out_shape.append(_sds((npair, 2, S), jnp.float32))
        scratch += [pltpu.VMEM((2, S, LANES), MXU), pltpu.VMEM((8, S), jnp.float32)]
    outs, rode = _pcall_riding(body, plan, args, name=name, grid=(npair,), in_specs=in_specs, out_specs=out_specs,
                               out_shape=out_shape, scratch_shapes=scratch)
    return (*outs, rode)


def _swa_bias(slope, shift):
    a = lax.broadcasted_iota(jnp.int32, (WINDOW, 2 * WINDOW), 0)
    c = lax.broadcasted_iota(jnp.int32, (WINDOW, 2 * WINDOW), 1)
    dist = a - c + shift
    return jnp.where((dist >= 0) & (dist < WINDOW), -slope * dist.astype(jnp.float32), NEG)


def _swa_scores(qh, kblk, bias):
    return _mm_nt(qh, kblk) * (HEAD ** -0.5) + bias


def _swa_stack(blk, lo):
    return jnp.concatenate(_split_heads(blk[:, :LANES], lo) + _split_heads(blk[:, LANES:], lo), axis=0)


def _swa_unstack(x, lo):
    r = x.shape[0] // 4
    return jnp.concatenate([jnp.where(lo, x[0:r], x[r:2 * r]), jnp.where(lo, x[2 * r:3 * r], x[3 * r:])], axis=1)


def _swa_per_head(ref, j, rows):
    quarter = lax.broadcasted_iota(jnp.int32, (4 * rows, 1), 0) // rows
    return jnp.where(quarter == 0, ref[4 * j], jnp.where(quarter == 1, ref[4 * j + 1],
                                                         jnp.where(quarter == 2, ref[4 * j + 2], ref[4 * j + 3])))


def _swa_fwd(q, kd, vd, sinks, slopes):
    S = q.shape[0]
    nkv = q.shape[1] // (2 * LANES)
    nb = S // WINDOW
    group = math.gcd(SWA_GROUP, nb)

    def body(sink_ref, slope_ref, q_ref, k_ref, v_ref, o_ref, lse_ref):
        j = pl.program_id(0)
        lo = _lane_masks()
        sink = _swa_per_head(sink_ref, j, WINDOW)
        biases = [jnp.concatenate([_swa_bias(slope_ref[4 * j + h], shift) for h in range(4)], axis=0)
                  for shift in (0, WINDOW)]

        def q_block(qi, c):
            q0 = pl.multiple_of(qi * WINDOW, WINDOW)
            k0 = pl.multiple_of(jnp.maximum(qi - 1, 0) * WINDOW, WINDOW)
            s = _swa_scores(_swa_stack(q_ref[pl.ds(q0, WINDOW), :], lo), k_ref[pl.ds(k0, 2 * WINDOW), :],
                            jnp.where(qi == 0, *biases))
            m = jnp.maximum(jnp.max(s, axis=1, keepdims=True), sink)
            p = jnp.exp(s - m)
            den = jnp.sum(p, axis=1, keepdims=True) + jnp.exp(sink - m)
            o_ref[pl.ds(q0, WINDOW), :] = _swa_unstack(_mm(p / den, v_ref[pl.ds(k0, 2 * WINDOW), :]), lo)
            lse = m + jnp.log(den)
            for h in range(4):
                lse_ref[h, pl.ds(q0, WINDOW), :] = lse[h * WINDOW:(h + 1) * WINDOW]
            return c

        def q_group(gi, c):
            for g in range(group):
                q_block(gi * group + g, c)
            return c

        lax.fori_loop(0, nb // group, q_group, 0)

    smem = pl.BlockSpec(memory_space=pltpu.SMEM)
    two = pl.BlockSpec((S, 2 * LANES), lambda j: (0, j))
    kv = pl.BlockSpec((S, LANES), lambda j: (0, 2 * j))
    return _pcall(
        body, name="swa_fwd", grid=(nkv,), semantics=("arbitrary",),
        in_specs=[smem, smem, two, kv, kv],
        out_specs=[two, pl.BlockSpec((4, S, 1), lambda j: (j, 0, 0))],
        out_shape=[_sds(q.shape, jnp.float32), _sds((4 * nkv, S, 1), jnp.float32)],
    )(sinks, slopes, q, kd, vd)


def _swa_bwd(q, kd, vd, do, o, lse, sinks, slopes, plan=None):
    S = q.shape[0]
    nkv = q.shape[1] // (2 * LANES)
    nb = S // WINDOW
    group = math.gcd(SWA_GROUP, nb)

    def body(sink_ref, slope_ref, q_ref, k_ref, v_ref, do_ref, o_ref, lse_ref,
             dq_ref, dk_ref, dv_ref, dsink_ref, dk_acc, dv_acc):
        j = pl.program_id(0)
        lo = _lane_masks()
        dk_acc[...] = jnp.zeros_like(dk_acc)
        dv_acc[...] = jnp.zeros_like(dv_acc)
        sink = _swa_per_head(sink_ref, j, WINDOW)
        biases = [jnp.concatenate([_swa_bias(slope_ref[4 * j + h], shift) for h in range(4)], axis=0)
                  for shift in (0, WINDOW)]

        def q_block(qi, carry):
            q0 = pl.multiple_of(qi * WINDOW, WINDOW)
            k0 = pl.multiple_of(jnp.maximum(qi - 1, 0) * WINDOW, WINDOW)
            q4 = _swa_stack(q_ref[pl.ds(q0, WINDOW), :], lo)
            do4 = _swa_stack(do_ref[pl.ds(q0, WINDOW), :], lo)
            oblk = o_ref[pl.ds(q0, WINDOW), :]
            o4 = jnp.concatenate([oblk[:, :LANES], oblk[:, :LANES], oblk[:, LANES:], oblk[:, LANES:]], axis=0)
            kblk = k_ref[pl.ds(k0, 2 * WINDOW), :]
            vblk = v_ref[pl.ds(k0, 2 * WINDOW), :]
            lse = jnp.concatenate([lse_ref[h, pl.ds(q0, WINDOW), :] for h in range(4)], axis=0)
            p = jnp.exp(_swa_scores(q4, kblk, jnp.where(qi == 0, *biases)) - lse)
            delta = jnp.sum(do4.astype(jnp.float32) * o4, axis=1, keepdims=True)
            dv_acc[pl.ds(k0, 2 * WINDOW), :] += _mm_tn(p, do4)
            ds = p * (_mm_nt(do4, vblk) - delta)
            dq_ref[pl.ds(q0, WINDOW), :] = _swa_unstack(_mm(ds, kblk) * (HEAD ** -0.5), lo).astype(dq_ref.dtype)
            dk_acc[pl.ds(k0, 2 * WINDOW), :] += _mm_tn(ds, q4) * (HEAD ** -0.5)
            dsk = -jnp.exp(sink - lse) * delta
            return tuple(carry[h] + jnp.sum(dsk[h * WINDOW:(h + 1) * WINDOW], axis=0, keepdims=True)
                         for h in range(4))

        def q_group(gi, carry):
            for g in range(group):
                carry = q_block(gi * group + g, carry)
            return carry

        zero = jnp.zeros((1, 1), jnp.float32)
        dsinks = lax.fori_loop(0, nb // group, q_group, (zero,) * 4)
        dk_ref[:, :LANES] = dk_acc[...].astype(dk_ref.dtype)
        dk_ref[:, LANES:] = jnp.zeros((S, LANES), dk_ref.dtype)
        dv_ref[:, :LANES] = dv_acc[...].astype(dv_ref.dtype)
        dv_ref[:, LANES:] = jnp.zeros((S, LANES), dv_ref.dtype)
        r = lax.broadcasted_iota(jnp.int32, (8, LANES), 0)
        dsink_ref[0] = jnp.where(r == 0, dsinks[0], jnp.where(r == 1, dsinks[1], jnp.where(r == 2, dsinks[2],
                                 jnp.where(r == 3, dsinks[3], 0.0))))

    smem = pl.BlockSpec(memory_space=pltpu.SMEM)
    two = pl.BlockSpec((S, 2 * LANES), lambda j: (0, j))
    kv = pl.BlockSpec((S, LANES), lambda j: (0, 2 * j))
    outs, rode = _pcall_riding(
        body, plan, [sinks, slopes, q, kd, vd, do, o, lse], name="swa_bwd", grid=(nkv,),
        in_specs=[smem, smem, two, kv, kv, two, two, pl.BlockSpec((4, S, 1), lambda j: (j, 0, 0))],
        out_specs=[two, two, two, pl.BlockSpec((1, 8, LANES), lambda j: (j, 0, 0))],
        out_shape=[_sds(q.shape, do.dtype), _sds(kd.shape, do.dtype), _sds(vd.shape, do.dtype),
                   _sds((nkv, 8, LANES), jnp.float32)],
        scratch_shapes=[pltpu.VMEM((S, LANES), jnp.float32), pltpu.VMEM((S, LANES), jnp.float32)])
    return (*outs, rode)


def _log_steps(S):
    k, out = 1, []
    while k < S:
        out.append(k)
        k *= 2
    return out


def _forget_fwd(f_row, b_col):
    S = f_row.shape[1]

    def body(f_ref, b_ref, lc_ref):
        x = f_ref[...] + b_ref[...]
        lc = jnp.minimum(x, 0.0) - jnp.log(1.0 + jnp.exp(-jnp.abs(x)))
        idx = lax.broadcasted_iota(jnp.int32, lc.shape, 1)
        for k in _log_steps(S):
            lc = lc + jnp.where(idx >= k, pltpu.roll(lc, k, axis=1), 0.0)
        lc_ref[...] = lc

    return _pcall(body, name="forget_fwd", out_shape=_sds(f_row.shape, jnp.float32))(f_row, b_col)


def _forget_bwd(dlc_row, f_row, b_col):
    S = f_row.shape[1]

    def body(d_ref, f_ref, b_ref, df_ref, db_ref):
        g = d_ref[...]
        idx = lax.broadcasted_iota(jnp.int32, g.shape, 1)
        for k in _log_steps(S):
            g = g + jnp.where(idx < S - k, pltpu.roll(g, S - k, axis=1), 0.0)
        x = f_ref[...] + b_ref[...]
        df = g * _sigmoid(-x)
        df_ref[...] = df
        db_ref[...] = jnp.sum(df, axis=1, keepdims=True)

    return _pcall(body, name="forget_bwd",
                  out_shape=[_sds(f_row.shape, jnp.float32), _sds((f_row.shape[0], 1), jnp.float32)])(dlc_row, f_row, b_col)


def _layer0_out_layer1_in(x, o_m, o_s, gate, w_out, g1, w_in1):
    S = x.shape[0]

    def body(x_ref, om_ref, os_ref, gate_ref, wo_ref, g_ref, w_ref,
             x1_ref, h_ref, q_ref, k_ref, v_ref, g1_ref, f_ref):
        gt = gate_ref[...]
        sg = gt * _sigmoid(gt)
        um = om_ref[...] * sg[:, :512]
        us = os_ref[...] * sg[:, 512:]
        x1 = x_ref[...] + _mm(um, wo_ref[0:512, :]) + _mm(us, wo_ref[512:1024, :])
        x1_ref[...] = x1
        h = _rms(x1, g_ref[...])
        h_ref[...] = h.astype(h_ref.dtype)
        z = _mm_nt(h, w_ref[...])
        q_ref[...] = z[:, 0:1024].astype(q_ref.dtype)
        k_ref[...] = z[:, 1024:2048].astype(k_ref.dtype)
        v_ref[...] = z[:, 2048:3072].astype(v_ref.dtype)
        g1_ref[...] = z[:, 3072:4096]
        f_ref[...] = z[:, 4096:4224]

    outs = [((S, D), jnp.float32), ((S, D), MXU), ((S, D), MXU), ((S, D), MXU), ((S, D), MXU),
            ((S, D), jnp.float32), ((S, LANES), jnp.float32)]
    return _pcall(
        body, name="layer0_out_layer1_in", grid=(S // TOK,), semantics=("arbitrary",),
        in_specs=[_rows(TOK, D), _rows(TOK, 512), _rows(TOK, 512), _rows(TOK, D), _full((D, D)), _full((1, D)),
                  _full(w_in1.shape)],
        out_specs=[_rows(TOK, s[1]) for s, _ in outs],
        out_shape=[_sds(s, d) for s, d in outs],
    )(x, o_m, o_s, gate, w_out, g1, w_in1)


def _head(x1, o1, gate1, w_out1, g_f, target):
    S = x1.shape[0]
    T = math.gcd(HEAD_TOK, S)

    def body(x1_ref, o_ref, gate_ref, wo_ref, g_ref, t_ref,
             loss_ref, dgf_ref, dwo_ref, dx2_ref, do_ref, dgate_ref):
        i = pl.program_id(0)
        gt = gate_ref[...]
        sig = _sigmoid(gt)
        sg = gt * sig
        o = o_ref[...]
        u = o * sg
        x2 = x1_ref[...] + _mm(u, wo_ref[...])
        g = g_ref[...]
        y = _rms(x2, g)
        err = y - t_ref[...]
        part = 0.5 * jnp.sum(jnp.mean(err * err, axis=-1, keepdims=True), axis=0, keepdims=True)
        dy = err * (1.0 / D)
        dx2, dg_rows = _rms_bwd(x2, g, dy)
        dx2_ref[...] = dx2
        du = _mm_nt(dx2, wo_ref[...])
        do_ref[...] = (du * sg).astype(do_ref.dtype)
        dgate_ref[...] = (du * o * (sig * (1.0 + gt * (1.0 - sig)))).astype(dgate_ref.dtype)

        @pl.when(i == 0)
        def _():
            loss_ref[...] = jnp.zeros_like(loss_ref)
            dgf_ref[...] = jnp.zeros_like(dgf_ref)
            dwo_ref[...] = jnp.zeros_like(dwo_ref)

        loss_ref[...] += jnp.broadcast_to(part, loss_ref.shape)
        dgf_ref[...] += jnp.sum(dg_rows, axis=0, keepdims=True)
        dwo_ref[...] += _mm_tn(u, dx2)

    outs = [((S, D), jnp.float32), ((S, D), MXU), ((S, D), MXU)]
    return _pcall(
        body, name="head", grid=(S // T,), semantics=("arbitrary",),
        in_specs=[_rows(T, D), _rows(T, D), _rows(T, D), _full((D, D)), _full((1, D)), _rows(T, D)],
        out_specs=[_full((8, LANES)), _full((1, D)), _full((D, D))] + [_rows(T, D) for _ in outs],
        out_shape=[_sds((8, LANES), jnp.float32), _sds((1, D), jnp.float32), _sds((D, D), jnp.float32)]
        + [_sds(s, d) for s, d in outs],
    )(x1, o1, gate1, w_out1, g_f, target)


def _layer1_in_bwd(dq, dk, dv, dgate1, df, x1, dx2, g1, w_in1, gate0, o_m, o_s, w_out0):
    S = x1.shape[0]

    def body(dq_ref, dk_ref, dv_ref, dg1_ref, df_ref, x1_ref, dx2_ref, g_ref, w_ref, gate_ref, om_ref, os_ref,
             wo_ref, dz_ref, dx1_ref, dgn_ref, dwo_ref, dom_ref, dos_ref, dgate_ref):
        i = pl.program_id(0)
        dz_ref[:, 0:1024] = dq_ref[...]
        dz_ref[:, 1024:2048] = dk_ref[...]
        dz_ref[:, 2048:3072] = dv_ref[...]
        dz_ref[:, 3072:4096] = dg1_ref[...]
        dz_ref[:, 4096:4224] = df_ref[...]
        dh = _mm(dz_ref[...], w_ref[...])
        g = g_ref[...]
        dxn, dg_rows = _rms_bwd(x1_ref[...], g, dh)
        dx1 = dx2_ref[...] + dxn
        dx1_ref[...] = dx1
        du = _mm_nt(dx1, wo_ref[...])
        gt = gate_ref[...]
        sig = _sigmoid(gt)
        sg = gt * sig
        dsg = sig * (1.0 + gt * (1.0 - sig))
        dom_ref[...] = (du[:, :512] * sg[:, :512]).astype(dom_ref.dtype)
        dos_ref[...] = (du[:, 512:] * sg[:, 512:]).astype(dos_ref.dtype)
        dgate_ref[:, :512] = (du[:, :512] * om_ref[...] * dsg[:, :512]).astype(dgate_ref.dtype)
        dgate_ref[:, 512:] = (du[:, 512:] * os_ref[...] * dsg[:, 512:]).astype(dgate_ref.dtype)

        @pl.when(i == 0)
        def _():
            dgn_ref[...] = jnp.zeros_like(dgn_ref)
            dwo_ref[...] = jnp.zeros_like(dwo_ref)

        dgn_ref[...] += jnp.sum(dg_rows, axis=0, keepdims=True)
        dwo_ref[0:512, :] += _mm_tn(om_ref[...] * sg[:, :512], dx1)
        dwo_ref[512:1024, :] += _mm_tn(os_ref[...] * sg[:, 512:], dx1)

    return _pcall(
        body, name="layer1_in_bwd", grid=(S // TOK,), semantics=("arbitrary",),
        in_specs=[_rows(TOK, D), _rows(TOK, D), _rows(TOK, D), _rows(TOK, D), _rows(TOK, LANES), _rows(TOK, D),
                  _rows(TOK, D), _full((1, D)), _full(w_in1.shape), _rows(TOK, D), _rows(TOK, 512), _rows(TOK, 512),
                  _full((D, D))],
        out_specs=[_rows(TOK, 4224), _rows(TOK, D), _full((1, D)), _full((D, D)), _rows(TOK, 512), _rows(TOK, 512),
                   _rows(TOK, D)],
        out_shape=[_sds((S, 4224), MXU), _sds((S, D), jnp.float32), _sds((1, D), jnp.float32), _sds((D, D), jnp.float32),
                   _sds((S, 512), MXU), _sds((S, 512), MXU), _sds((S, D), MXU)],
    )(dq, dk, dv, dgate1, df, x1, dx2, g1, w_in1, gate0, o_m, o_s, w_out0)


def _layer0_in_bwd(dqm, dkm, dvm, dqs, dkd, dvd, dgate0, cos, sin, cq, ckv, x, dx1, g_in, w_in, g_q, w_q, g_kv, w_kv):
    S = x.shape[0]
    consts = _rope_consts()

    def body(dqm_ref, dkm_ref, dvm_ref, dqs_ref, dkd_ref, dvd_ref, dgate_ref, cos_ref, sin_ref, c_ref, cq_ref, ckv_ref,
             x_ref, dx1_ref, g_ref, w_ref, gq_ref, wq_ref, gkv_ref, wkv_ref,
             dx_ref, dz_ref, dgin_ref, dgq_ref, dgkv_ref, dwq_ref, dwkv_ref, dqu_ref, dkvu_ref):
        i = pl.program_id(0)
        lo = _lane_masks()
        sign = c_ref[...][1:2, :]
        c = cos_ref[...]
        s = sin_ref[...]
        dkpe = None
        for hd in range(N_MLA):
            sl = slice(LANES * hd, LANES * (hd + 1))
            dqu_ref[:, sl] = _rope_t(dqm_ref[:, sl], c, s, sign).astype(dqu_ref.dtype)
            dkh = dkm_ref[:, sl]
            dkvu_ref[:, sl] = jnp.where(lo, dkh, 0.0).astype(dkvu_ref.dtype)
            dkpe = dkh if dkpe is None else dkpe + dkh
        dkvu_ref[:, 1024:1536] = dvm_ref[...]
        dkpe = _rope_t(jnp.where(lo, 0.0, dkpe), c, s, sign)
        dcqn = _mm(dqu_ref[...], wq_ref[...])
        dckvn = _mm_nt(dkvu_ref[...], wkv_ref[...])
        gq = gq_ref[...]
        gkv = gkv_ref[...]
        dcq, dgq_rows = _rms_bwd(cq_ref[...], gq, dcqn)
        dckv, dgkv_rows = _rms_bwd(ckv_ref[...], gkv, dckvn)
        dz_ref[:, 0:256] = dcq.astype(dz_ref.dtype)
        dz_ref[:, 256:384] = dckv.astype(dz_ref.dtype)
        dz_ref[:, 384:512] = dkpe.astype(dz_ref.dtype)
        dz_ref[:, 512:1024] = dqs_ref[...]
        dz_ref[:, 1024:1536] = dkd_ref[...]
        dz_ref[:, 1536:2048] = dvd_ref[...]
        dz_ref[:, 2048:3072] = dgate_ref[...]
        dh = _mm(dz_ref[...], w_ref[...])
        g = g_ref[...]
        dxn, dg_rows = _rms_bwd(x_ref[...], g, dh)
        dx_ref[...] = dx1_ref[...] + dxn

        @pl.when(i == 0)
        def _():
            dgin_ref[...] = jnp.zeros_like(dgin_ref)
            dgq_ref[...] = jnp.zeros_like(dgq_ref)
            dgkv_ref[...] = jnp.zeros_like(dgkv_ref)
            dwq_ref[...] = jnp.zeros_like(dwq_ref)
            dwkv_ref[...] = jnp.zeros_like(dwkv_ref)

        dgin_ref[...] += jnp.sum(dg_rows, axis=0, keepdims=True)
        dgq_ref[...] += jnp.sum(dgq_rows, axis=0, keepdims=True)
        dgkv_ref[...] += jnp.sum(dgkv_rows, axis=0, keepdims=True)
        dwq_ref[...] += _mm_tn(dqu_ref[...], _rms(cq_ref[...], gq))
        dwkv_ref[...] += _mm_tn(_rms(ckv_ref[...], gkv), dkvu_ref[...])

    return _pcall(
        body, name="layer0_in_bwd", grid=(S // TOK,), semantics=("arbitrary",),
        in_specs=[_rows(TOK, 1024), _rows(TOK, 1024), _rows(TOK, 512), _rows(TOK, 512), _rows(TOK, 512), _rows(TOK, 512),
                  _rows(TOK, D), _rows(TOK, LANES), _rows(TOK, LANES), _full((8, LANES)), _rows(TOK, 256), _rows(TOK, 128),
                  _rows(TOK, D), _rows(TOK, D), _full((1, D)), _full(w_in.shape), _full((1, 256)), _full(w_q.shape),
                  _full((1, 128)), _full(w_kv.shape)],
        out_specs=[_rows(TOK, D), _rows(TOK, 3072), _full((1, D)), _full((1, 256)), _full((1, 128)), _full(w_q.shape),
                   _full(w_kv.shape)],
        out_shape=[_sds((S, D), jnp.float32), _sds((S, 3072), MXU), _sds((1, D), jnp.float32), _sds((1, 256), jnp.float32),
                   _sds((1, 128), jnp.float32), _sds(w_q.shape, jnp.float32), _sds(w_kv.shape, jnp.float32)],
        scratch_shapes=[pltpu.VMEM((TOK, 1024), MXU), pltpu.VMEM((TOK, 1536), MXU)],
    )(dqm, dkm, dvm, dqs, dkd, dvd, dgate0, cos, sin, consts, cq, ckv, x, dx1, g_in, w_in, g_q, w_q, g_kv, w_kv)


def _wgrad(a, b, name, plan=None):
    S, M = a.shape
    N = b.shape[1]
    tm = next(t for t in range(WG_ROWS, 0, -LANES) if M % t == 0)
    tn = N if N <= 1024 else 512
    tk = min(WG_TOK, S)

    def body(a_ref, b_ref, o_ref):
        @pl.when(pl.program_id(2) == 0)
        def _():
            o_ref[...] = jnp.zeros_like(o_ref)

        o_ref[...] += _mm_tn(a_ref[...], b_ref[...])

    (dw,), rode = _pcall_riding(
        body, plan, [a, b], name=name, grid=(M // tm, N // tn, S // tk),
        in_specs=[pl.BlockSpec((tk, tm), lambda m, n, k: (k, m)), pl.BlockSpec((tk, tn), lambda m, n, k: (k, n))],
        out_specs=[pl.BlockSpec((tm, tn), lambda m, n, k: (m, n))],
        out_shape=[_sds((M, N), jnp.float32)], scratch_shapes=[])
    return dw, rode


def _adamw(w, g, m, v, name):
    shape = w.shape
    R, C = (int(np.prod(shape[:-1])), shape[-1])
    w2, g2, m2, v2 = (t.reshape(R, C) for t in (w, g, m, v))
    fits = [t for t in range(8, ADAM_TILE_BYTES // (4 * C) + 1, 8) if R % t == 0]
    tr = max(fits) if fits else R
    tc = C if (tr * C * 4 <= ADAM_TILE_BYTES or C % 256) else 256

    def body(w_ref, g_ref, m_ref, v_ref, d_ref, nm_ref, nv_ref):
        gg = g_ref[...]
        nm = B1 * m_ref[...] + (1.0 - B1) * gg
        nv = B2 * v_ref[...] + (1.0 - B2) * (gg * gg)
        m_hat = nm / (1.0 - B1 ** STEP)
        v_hat = nv / (1.0 - B2 ** STEP)
        d_ref[...] = -LR * (m_hat / (jnp.sqrt(v_hat) + AEPS) + WD * w_ref[...])
        nm_ref[...] = nm
        nv_ref[...] = nv

    spec = pl.BlockSpec((tr, tc), lambda i, j: (i, j))
    d, nm, nv = _pcall(
        body, name=name, grid=(R // tr, C // tc), semantics=("parallel", "parallel"),
        in_specs=[spec] * 4, out_specs=[spec] * 3, out_shape=[_sds((R, C), jnp.float32)] * 3,
    )(w2, g2, m2, v2)
    return d.reshape(shape), nm.reshape(shape), nv.reshape(shape)


def _sum_leading(a, name):
    n, R, C = a.shape
    tr = SUM_ROWS if R % SUM_ROWS == 0 else R

    def body(a_ref, o_ref):
        acc = a_ref[0]
        for i in range(1, n):
            acc = acc + a_ref[i]
        o_ref[...] = acc

    return _pcall(
        body, name=name, grid=(R // tr,), semantics=("parallel",),
        in_specs=[pl.BlockSpec((n, tr, C), lambda i: (0, i, 0))], out_specs=_rows(tr, C),
        out_shape=_sds((R, C), a.dtype),
    )(a)


def _add_halves(g, c, b, name, out_dtype):
    n, _, R, C = g.shape
    tr = SUM_ROWS if R % SUM_ROWS == 0 else R

    def body(c_ref, a_ref, b_ref, o_ref):
        o_ref[...] = (a_ref[0] + b_ref[...]).astype(o_ref.dtype)

    spec = pl.BlockSpec((1, tr, C), lambda k, i, c_ref: (k, i, 0))
    grid_spec = pltpu.PrefetchScalarGridSpec(
        num_scalar_prefetch=1, grid=(n, R // tr),
        in_specs=[pl.BlockSpec((1, 1, tr, C), lambda k, i, c_ref: (k, c_ref[0], i, 0)), spec], out_specs=spec)
    return _pcall(body, name=name, semantics=("parallel", "parallel"), grid_spec=grid_spec,
                  out_shape=_sds(b.shape, out_dtype))(c.reshape(1).astype(jnp.int32), g, b)


def _total_sum(g, theirs, chip, c, recv, name):
    _, _, R, C = g.shape
    n = recv.shape[0]
    tr = SUM_ROWS if R % SUM_ROWS == 0 else R

    def body(at_ref, a_ref, b_ref, r_ref, o_ref):
        acc = a_ref[0, 0] + b_ref[0]
        for i in range(n):
            acc = acc + r_ref[i].astype(jnp.float32)
        o_ref[...] = acc

    grid_spec = pltpu.PrefetchScalarGridSpec(
        num_scalar_prefetch=1, grid=(R // tr,),
        in_specs=[pl.BlockSpec((1, 1, tr, C), lambda i, at_ref: (at_ref[0], at_ref[1], i, 0)),
                  pl.BlockSpec((1, tr, C), lambda i, at_ref: (at_ref[0], i, 0)),
                  pl.BlockSpec((n, tr, C), lambda i, at_ref: (0, i, 0))],
        out_specs=pl.BlockSpec((tr, C), lambda i, at_ref: (i, 0)))
    return _pcall(body, name=name, semantics=("parallel",), grid_spec=grid_spec,
                  out_shape=_sds((R, C), jnp.float32))(jnp.stack([chip, c]).astype(jnp.int32), g, theirs, recv)


def _place():
    return lax.axis_index("x"), lax.axis_index("y"), lax.axis_index("c")


class _Plan:
    def __init__(self, arrays, out_shape, scratch, start, finish, middle=None):
        self.arrays, self.out_shape, self.scratch = list(arrays), list(out_shape), list(scratch)
        self.start, self.finish, self.middle = start, finish, middle


def _gather8_plan(block):
    R, C = block.shape

    def parts(ins, outs, sems):
        (x_ref,), (out_ref,), (send_sems, recv_sems) = ins, outs, sems
        x, y, c = _place()
        me, sibling = (x, y, c), (x, y, 1 - c)
        chips = [(1 - x, y), (x, 1 - y), (1 - x, 1 - y)]

        def copy(k, blk, to, src=None):
            slot = out_ref.at[4 * blk[0] + 2 * blk[1] + blk[2]]
            return pltpu.make_async_remote_copy(
                src_ref=slot if src is None else src, dst_ref=slot,
                send_sem=send_sems.at[k], recv_sem=recv_sems.at[k], device_id=to, device_id_type=MESH_ID)

        def first():
            return [copy(0, me, sibling, src=x_ref)] + [copy(1 + j, me, (*chip, c), src=x_ref) for j, chip in enumerate(chips)]

        def passed():
            return [copy(4 + j, (*chip, c), sibling) for j, chip in enumerate(chips)]

        def arrivals():
            return [copy(1 + j, (*chip, c), me) for j, chip in enumerate(chips)]

        def late():
            return [copy(0, sibling, me)] + [copy(4 + j, (*chip, 1 - c), me) for j, chip in enumerate(chips)]

        return first, passed, arrivals, late

    def start(ins, outs, sems):
        for cp in parts(ins, outs, sems)[0]():
            cp.start()

    def middle(ins, outs, sems):
        _, passed, arrivals, _ = parts(ins, outs, sems)
        for arrived, forward in zip(arrivals(), passed()):
            arrived.wait_recv()
            forward.start()

    def finish(ins, outs, sems):
        first, passed, _, late = parts(ins, outs, sems)
        for cp in late():
            cp.wait_recv()
        for cp in first() + passed():
            cp.wait_send()

    return _Plan([block], [_sds((8, R, C), block.dtype)], [pltpu.SemaphoreType.DMA((7,)), pltpu.SemaphoreType.DMA((7,))],
                 start, finish, middle)


def _fill_own_slot(gathered, block):
    x, y, c = _place()
    return lax.dynamic_update_index_in_dim(gathered, block, 4 * x + 2 * y + c, 0)


def _started_and_waited(arrays, out_shape, n, copies):
    def start(ins, outs, sems):
        for cp in copies(ins, outs, sems):
            cp.start()

    def finish(ins, outs, sems):
        for cp in copies(ins, outs, sems):
            cp.wait()

    return _Plan(arrays, out_shape, [pltpu.SemaphoreType.DMA((n,)), pltpu.SemaphoreType.DMA((n,))], start, finish)


def _pair_swap_plan(g):
    n = g.shape[0]

    def copies(ins, outs, sems):
        (g_ref,), (out_ref,), (send_sems, recv_sems) = ins, outs, sems
        x, y, c = _place()
        return [pltpu.make_async_remote_copy(src_ref=g_ref.at[k, 1 - c], dst_ref=out_ref.at[k], send_sem=send_sems.at[k],
                                             recv_sem=recv_sems.at[k], device_id=(x, y, 1 - c), device_id_type=MESH_ID)
                for k in range(n)]

    return _started_and_waited([g], [_sds((n,) + g.shape[2:], g.dtype)], n, copies)


def _chip_exchange_plan(p):
    def copies(ins, outs, sems):
        (p_ref,), (out_ref,), (send_sems, recv_sems) = ins, outs, sems
        x, y, c = _place()
        chips = [(1 - x, y), (x, 1 - y), (1 - x, 1 - y)]
        return [pltpu.make_async_remote_copy(
            src_ref=p_ref.at[2 * cx + cy], dst_ref=out_ref.at[j], send_sem=send_sems.at[j],
            recv_sem=recv_sems.at[j], device_id=(cx, cy, c), device_id_type=MESH_ID)
            for j, (cx, cy) in enumerate(chips)]

    return _started_and_waited([p], [_sds((3,) + p.shape[1:], p.dtype)], 3, copies)


def _pair_exchange_plan(t):
    def copies(ins, outs, sems):
        (t_ref,), (out_ref,), (send_sems, recv_sems) = ins, outs, sems
        x, y, c = _place()
        return [pltpu.make_async_remote_copy(src_ref=t_ref, dst_ref=out_ref, send_sem=send_sems.at[0], recv_sem=recv_sems.at[0],
                                             device_id=(x, y, 1 - c), device_id_type=MESH_ID)]

    return _started_and_waited([t], [_sds(t.shape, t.dtype)], 1, copies)


def _both_plans(a, b):
    na, ma, sa = len(a.arrays), len(a.out_shape), len(a.scratch)

    def phase(name):
        fa, fb = getattr(a, name), getattr(b, name)
        if fa is None and fb is None:
            return None

        def run(ins, outs, sems):
            if fa is not None:
                fa(ins[:na], outs[:ma], sems[:sa])
            if fb is not None:
                fb(ins[na:], outs[ma:], sems[sa:])
        return run

    return _Plan(a.arrays + b.arrays, a.out_shape + b.out_shape, a.scratch + b.scratch,
                 phase("start"), phase("finish"), phase("middle"))


ANY_SPEC = pl.BlockSpec(memory_space=pl.ANY)


def _run_plan(plan, name):
    n_in, n_out = len(plan.arrays), len(plan.out_shape)

    def body(*refs):
        ins, outs, sems = refs[:n_in], refs[n_in:n_in + n_out], refs[n_in + n_out:]
        plan.start(ins, outs, sems)
        if plan.middle is not None:
            plan.middle(ins, outs, sems)
        plan.finish(ins, outs, sems)

    return _pcall(body, name=name, in_specs=[ANY_SPEC] * n_in, out_specs=[ANY_SPEC] * n_out, out_shape=plan.out_shape,
                  scratch_shapes=plan.scratch)(*plan.arrays)


def _pcall_riding(body, plan, args, *, name, grid, in_specs, out_specs, out_shape, scratch_shapes):
    order = ("arbitrary",) * len(grid)
    if plan is None:
        outs = _pcall(body, name=name, grid=grid, semantics=order, in_specs=in_specs, out_specs=out_specs,
                      out_shape=out_shape, scratch_shapes=scratch_shapes)(*args)
        return list(outs), None
    n_in, n_out, n_s = len(args), len(out_shape), len(scratch_shapes)
    p_in, p_out = len(plan.arrays), len(plan.out_shape)
    steps = math.prod(grid)

    def riding(*refs):
        ins, pins = refs[:n_in], refs[n_in:n_in + p_in]
        o0 = n_in + p_in
        outs, pouts = refs[o0:o0 + n_out], refs[o0 + n_out:o0 + n_out + p_out]
        s0 = o0 + n_out + p_out
        scr, sems = refs[s0:s0 + n_s], refs[s0 + n_s:]
        j = pl.program_id(0)
        for axis in range(1, len(grid)):
            j = j * grid[axis] + pl.program_id(axis)

        @pl.when(j == 0)
        def _():
            plan.start(pins, pouts, sems)

        if plan.middle is not None:
            @pl.when(j == steps // 2)
            def _():
                plan.middle(pins, pouts, sems)

        body(*ins, *outs, *scr)

        @pl.when(j == steps - 1)
        def _():
            plan.finish(pins, pouts, sems)

    res = _pcall(riding, name=name, grid=grid, semantics=order, in_specs=list(in_specs) + [ANY_SPEC] * p_in,
                 out_specs=list(out_specs) + [ANY_SPEC] * p_out, out_shape=list(out_shape) + plan.out_shape,
                 scratch_shapes=list(scratch_shapes) + plan.scratch)(*args, *plan.arrays)
    return list(res[:n_out]), list(res[n_out:])


class _RowSeq:
    def __init__(self, pieces):
        self.pieces = list(pieces)

    def rows(self, a, b):
        out, off = [], 0
        for p in self.pieces:
            lo, hi = max(a, off), min(b, off + p.shape[0])
            if lo < hi:
                out.append(p[lo - off:hi - off])
            off += p.shape[0]
        return out

    def array(self):
        return jnp.concatenate(self.pieces, axis=0)


def _row_seq(w):
    return w if isinstance(w, _RowSeq) else _RowSeq([w])


def _prep_w_in0(wt):
    wt = _row_seq(wt)
    one = wt.pieces[0]
    z32 = [jnp.zeros((32, one.shape[1]), one.dtype)]
    k0, k1 = wt.rows(928, 992), wt.rows(992, 1056)
    v0, v1 = wt.rows(1056, 1120), wt.rows(1120, 1184)
    return jnp.concatenate(wt.rows(0, 384) + z32 + z32 + wt.rows(384, 416) + z32 + wt.rows(416, 928)
                           + k0 * 4 + k1 * 4 + v0 * 4 + v1 * 4 + wt.rows(1184, 2208), axis=0)


def _fold_w_in0(d):
    def fold(blk):
        b = blk.reshape(8, 64, blk.shape[1])
        return jnp.concatenate([b[0] + b[1] + b[2] + b[3], b[4] + b[5] + b[6] + b[7]], axis=0)
    return _RowSeq([d[0:384], d[448:480], d[512:1024], fold(d[1024:1536]), fold(d[1536:2048]), d[2048:3072]])


def _prep_w_q(wt):
    return jnp.pad(wt.reshape(N_MLA, 96, Q_RANK), ((0, 0), (0, 32), (0, 0))).reshape(1024, Q_RANK)


def _fold_w_q(d):
    return d.reshape(N_MLA, 128, Q_RANK)[:, :96].reshape(768, Q_RANK)


def _prep_w_kv(w):
    w3 = w.reshape(KV_RANK, N_MLA, 128)
    kk = jnp.pad(w3[:, :, :64], ((0, 0), (0, 0), (0, 64))).reshape(KV_RANK, 1024)
    return jnp.concatenate([kk, w3[:, :, 64:].reshape(KV_RANK, 512)], axis=1)


def _fold_w_kv(d):
    kk = d[:, :1024].reshape(KV_RANK, N_MLA, 128)[:, :, :64]
    vv = d[:, 1024:].reshape(KV_RANK, N_MLA, 64)
    return jnp.concatenate([kk, vv], axis=2).reshape(KV_RANK, 1024)


W_IN1_SHARD = 1028
W_IN1_STEP = W_IN1_SHARD % 16


class _ShiftedShards:
    def __init__(self, blocks):
        self.blocks = list(blocks)


def _shifted_shard(a, chip, rows):
    out = jnp.zeros((rows, a.shape[1]), a.dtype)
    for k in range(4):
        out = jnp.where(chip == k, jnp.pad(a, ((W_IN1_STEP * k, rows - W_IN1_STEP * k - a.shape[0]), (0, 0))), out)
    return out


def _prep_w_in1(wt):
    if not isinstance(wt, _ShiftedShards):
        return jnp.concatenate([wt[0:3072], wt[3088:4112], wt[3072:3088], jnp.zeros((112, wt.shape[1]), wt.dtype)], axis=0)
    b = wt.blocks
    row = lax.broadcasted_iota(jnp.int32, (16, 1), 0)

    def seam(k, first, second):
        return jnp.where(row < W_IN1_STEP * (k + 1), first, second)

    return jnp.concatenate([
        b[0][0:1024], seam(0, b[0][1024:1040], b[1][0:16]), b[1][16:1024], seam(1, b[1][1024:1040], b[2][0:16]),
        b[2][16:1024], b[3][16:1040], seam(2, b[2][1024:1040], b[3][0:16]), jnp.zeros((112, 1024), b[0].dtype)], axis=0)


def _fold_w_in1(d):
    return _RowSeq([d[0:3072], d[4096:4112], d[3072:4096]])


class _Alone:
    def __init__(self, w_out0, o_g_in, w_in1, w_out1):
        self.layer1 = (w_out0, o_g_in, w_in1, w_out1)

    def gather_plan(self):
        return None

    def layer1_weights(self, rode):
        return self.layer1

    def swap_plan(self, grads1):
        return None

    def exchange_plan(self, rode):
        return None

    def halves_plan(self, rode):
        return None

    def finish(self, rode):
        pass


def _local_step(x, pos, target, e_g_in, w_in0, e_g_q, w_q, e_g_kv, w_kv, sinks, b_f, g_final, layer1):
    S = x.shape[0]
    w_in0p, w_qp, w_kvp = _prep_w_in0(w_in0), _prep_w_q(w_q), _prep_w_kv(w_kv)
    slopes = jnp.asarray(2.0 ** (-8.0 * (np.arange(N_SWA, dtype=np.float32) + 1.0) / N_SWA), jnp.float32)
    sinks1 = sinks.reshape(N_SWA)
    b_col = b_f.reshape(N_FOX, 1)

    (h0, cq, ckv, qm, km, vm, qs, kd, vd, gate0, cos, sin) = _layer0_in(
        x, pos, e_g_in, w_in0p, e_g_q, w_qp, e_g_kv, w_kvp)
    o_m, lse_m, rode = _attn_fwd_t(qm, km, vm, (NOPE + ROPE) ** -0.5, split=True, name="mla_fwd", plan=layer1.gather_plan())
    w_out0, o_g_in, w_in1, w_out1 = layer1.layer1_weights(rode)
    w_in1p = _prep_w_in1(w_in1)
    o_s, lse_s = _swa_fwd(qs, kd, vd, sinks1, slopes)
    x1, h1, q1, k1, v1, gate1, f_slab = _layer0_out_layer1_in(x, o_m, o_s, gate0, w_out0, o_g_in, w_in1p)
    f_row = f_slab[:, :N_FOX].T
    lc_row = _forget_fwd(f_row, b_col)
    lcc = lc_row.T
    o1, lse1, _ = _attn_fwd_t(q1, k1, v1, HEAD ** -0.5, split=False, name="fox_fwd", lcc=lcc)
    loss8, dg_final, dw_out1, dx2, do1, dgate1 = _head(x1, o1, gate1, w_out1, g_final, target)

    dq1, dk1, dv1, dlc, _ = _attn_bwd_t(q1, k1, v1, do1, o1, lse1, HEAD ** -0.5, split=False, name="fox_bwd", lcc=lcc)
    df_row, db_f = _forget_bwd(dlc.reshape(N_FOX, S), f_row, b_col)
    df_slab = jnp.pad(df_row.T, ((0, 0), (0, LANES - N_FOX))).astype(MXU)
    dz1, dx1, dg_o_in, dw_out0, do_m, do_s, dgate0 = _layer1_in_bwd(
        dq1, dk1, dv1, dgate1, df_slab, x1, dx2, o_g_in, w_in1p, gate0, o_m, o_s, w_out0)
    grads1 = dict(o_g_in=dg_o_in, o_w_in=_fold_w_in1(_wgrad(dz1, h1, "wgrad_in1")[0]), o_w_out=dw_out1,
                  e_w_out=dw_out0)
    dqs, dkd, dvd, dsink, rode = _swa_bwd(qs, kd, vd, do_s, o_s, lse_s, sinks1, slopes, plan=layer1.swap_plan(grads1))
    dqm, dkm, dvm, rode = _attn_bwd_t(qm, km, vm, do_m, o_m, lse_m, (NOPE + ROPE) ** -0.5, split=True, name="mla_bwd",
                                      plan=layer1.exchange_plan(rode))
    halves_plan = layer1.halves_plan(rode)
    dx, dz0, dg_in, dg_q, dg_kv, dw_q, dw_kv = _layer0_in_bwd(
        dqm, dkm, dvm, dqs, dkd, dvd, dgate0, cos, sin, cq, ckv, x, dx1, e_g_in, w_in0p, e_g_q, w_qp, e_g_kv, w_kvp)
    dw_in0, rode = _wgrad(dz0, h0, "wgrad_in0", plan=halves_plan)
    layer1.finish(rode)

    grads = dict(
        e_g_in=dg_in,
        e_w_in=_fold_w_in0(dw_in0),
        e_g_q_a=dg_q,
        e_w_q_up=_fold_w_q(dw_q),
        e_g_kv_a=dg_kv,
        e_w_kv_up=_fold_w_kv(dw_kv),
        e_sinks=dsink[:, 0:4, 0].reshape(1, N_SWA),
        o_b_f=db_f.reshape(1, N_FOX),
        g_final=dg_final,
        **grads1,
    )
    return loss8[0, 0], dx, grads


SHARDED = ("e_w_in", "e_w_q_up", "e_w_kv_up", "e_w_out", "o_g_in", "o_w_in", "o_w_out")
TRANSPOSED = ("e_w_in", "e_w_q_up", "o_w_in")
COL_SHARDED = ("e_w_kv_up", "o_g_in")
REPLICATED = ("e_g_in", "e_g_q_a", "e_g_kv_a", "e_sinks", "o_b_f", "g_final")
FULL_SHAPES = dict(e_w_in=(2208, 1024), e_w_q_up=(768, 256), e_w_kv_up=(128, 1024), e_w_out=(1024, 1024),
                   o_g_in=(1, 1024), o_w_in=(4112, 1024), o_w_out=(1024, 1024))
GROUPS = dict(
    layer0=dict(rows=768, windows=dict(e_w_in=(0, 0), e_w_q_up=(560, 0), e_w_kv_up=(560, 256))),
    layer1=dict(rows=1568, windows=dict(o_w_in=(0, 0), o_w_out=(1040, 0), e_w_out=(1296, 0), o_g_in=(1552, 0))),
)


def _shard_shape(name):
    r, c = FULL_SHAPES[name]
    return (r, c // 4) if name in COL_SHARDED else (r // 4, c)


def _as_handled(name, a):
    a = a[0] if a.ndim == 3 else a
    return a.T if name in TRANSPOSED else a


def _as_given(name, a, shape):
    return (a.T if name in TRANSPOSED else a).reshape(shape)


def _pack_block(p, group, shifted_for=None):
    def rows(a, n):
        return jnp.pad(a, ((0, n - a.shape[0]), (0, 0)))

    if group == "layer0":
        band = jnp.concatenate([p["e_w_q_up"], rows(p["e_w_kv_up"], 192), jnp.zeros((192, 512), p["e_w_in"].dtype)], axis=1)
        return jnp.concatenate([rows(p["e_w_in"], 560), rows(band, 208)], axis=0)
    g = p["o_g_in"]
    band = jnp.pad(g, ((0, 16 - g.shape[0]), (0, PACK_COLS - g.shape[1])))
    w_in = rows(p["o_w_in"], 1040) if shifted_for is None else _shifted_shard(p["o_w_in"], shifted_for, 1040)
    return jnp.concatenate([w_in, p["o_w_out"], p["e_w_out"], band], axis=0)


def _window(block, group, name, width=None):
    r0, c0 = GROUPS[group]["windows"][name]
    r, c = _shard_shape(name)
    return block[..., r0:r0 + r, c0:c0 + (c if width is None else width)]


def _chip_slice(name, full, k):
    r, c = _shard_shape(name)
    if isinstance(full, _RowSeq):
        return jnp.concatenate(full.rows(r * k, r * (k + 1)), axis=0)
    return full[:, c * k:c * (k + 1)] if name in COL_SHARDED else full[r * k:r * (k + 1), :]


def _packed_weights(w, group):
    parts = {}
    for n in GROUPS[group]["windows"]:
        a = _as_handled(n, w[n])
        parts[n] = lax.bitcast_convert_type(a, jnp.bfloat16).reshape(1, -1) if n == "o_g_in" else a.astype(jnp.bfloat16)
    x, y, _ = _place()
    halves = _pack_block(parts, group, shifted_for=2 * x + y).reshape(2, GROUPS[group]["rows"] // 2, PACK_COLS)
    return lax.dynamic_index_in_dim(halves, lax.axis_index("c"), 0, keepdims=False)


def _unpacked_weights(gathered, half, group):
    blocks = _fill_own_slot(gathered, half).reshape(4, GROUPS[group]["rows"], PACK_COLS)
    full = {}
    for n in GROUPS[group]["windows"]:
        if n == "o_g_in":
            halves = _window(blocks, group, n, width=512).reshape(4, 1, 256, 2)
            full[n] = jnp.concatenate(list(lax.bitcast_convert_type(halves, jnp.float32)), axis=1)
        elif n == "o_w_in":
            full[n] = _ShiftedShards(blocks[k, 0:1040].astype(MXU) for k in range(4))
        else:
            pieces = [_window(blocks[k], group, n).astype(MXU) for k in range(4)]
            if n == "e_w_in":
                full[n] = _RowSeq(pieces)
            else:
                full[n] = jnp.concatenate(pieces, axis=1 if n in COL_SHARDED else 0)
    return full


class _GroupReduce:
    def __init__(self, group):
        self.group = group
        self.c = lax.axis_index("c")
        self.chip = 2 * lax.axis_index("x") + lax.axis_index("y")

    def swap_plan(self, grads):
        names = GROUPS[self.group]["windows"]
        per_chip = jnp.concatenate([_pack_block({n: _chip_slice(n, grads[n], k) for n in names}, self.group)
                                    for k in range(4)], axis=0)
        self.g4 = per_chip.reshape(4, 2, GROUPS[self.group]["rows"] // 2, PACK_COLS)
        return _pair_swap_plan(self.g4)

    def exchange_plan(self, rode):
        self.theirs = rode[0]
        return _chip_exchange_plan(_add_halves(self.g4, self.c, self.theirs, "pair_add_" + self.group, jnp.bfloat16))

    def halves_plan(self, rode):
        self.my_half = _total_sum(self.g4, self.theirs, self.chip, self.c, rode[0], "chip_sum_" + self.group)
        return _pair_exchange_plan(self.my_half)

    def finish(self, rode):
        my_half, other_half = self.my_half, rode[0]
        total = jnp.concatenate([jnp.where(self.c == 0, my_half, other_half), jnp.where(self.c == 0, other_half, my_half)], axis=0)
        self.sums = {n: _window(total, self.group, n) for n in GROUPS[self.group]["windows"]}

    def run(self, grads, beside):
        swap = self.swap_plan(grads)
        outs = _run_plan(_both_plans(swap, beside), "pair_swap_" + self.group)
        rode, others = outs[:len(swap.out_shape)], outs[len(swap.out_shape):]
        halves = self.halves_plan(_run_plan(self.exchange_plan(rode), "chip_exchange_" + self.group))
        self.finish(_run_plan(halves, "pair_exchange_" + self.group))
        return self.sums, others


class _Layer1Exchange(_GroupReduce):
    def __init__(self, w):
        super().__init__("layer1")
        self.half = _packed_weights(w, "layer1")

    def gather_plan(self):
        return _gather8_plan(self.half)

    def layer1_weights(self, rode):
        full = _unpacked_weights(rode[0], self.half, "layer1")
        return full["e_w_out"], full["o_g_in"], full["o_w_in"], full["o_w_out"]


def kernel(x, positions, e_g_in, e_w_in, e_g_q_a, e_w_q_up, e_g_kv_a, e_w_kv_up, e_sinks, e_w_out, o_g_in, o_w_in, o_b_f, o_w_out, g_final, loss_target, m_e_g_in, m_e_w_in, m_e_g_q_a, m_e_w_q_up, m_e_g_kv_a, m_e_w_kv_up, m_e_sinks, m_e_w_out, m_o_g_in, m_o_w_in, m_o_b_f, m_o_w_out, m_g_final, v_e_g_in, v_e_w_in, v_e_g_q_a, v_e_w_q_up, v_e_g_kv_a, v_e_w_kv_up, v_e_sinks, v_e_w_out, v_o_g_in, v_o_w_in, v_o_b_f, v_o_w_out, v_g_final):
    w = dict(e_g_in=e_g_in, e_w_in=e_w_in, e_g_q_a=e_g_q_a, e_w_q_up=e_w_q_up, e_g_kv_a=e_g_kv_a, e_w_kv_up=e_w_kv_up,
             e_sinks=e_sinks, e_w_out=e_w_out, o_g_in=o_g_in, o_w_in=o_w_in, o_b_f=o_b_f, o_w_out=o_w_out, g_final=g_final)
    m = dict(e_g_in=m_e_g_in, e_w_in=m_e_w_in, e_g_q_a=m_e_g_q_a, e_w_q_up=m_e_w_q_up, e_g_kv_a=m_e_g_kv_a,
             e_w_kv_up=m_e_w_kv_up, e_sinks=m_e_sinks, e_w_out=m_e_w_out, o_g_in=m_o_g_in, o_w_in=m_o_w_in, o_b_f=m_o_b_f,
             o_w_out=m_o_w_out, g_final=m_g_final)
    v = dict(e_g_in=v_e_g_in, e_w_in=v_e_w_in, e_g_q_a=v_e_g_q_a, e_w_q_up=v_e_w_q_up, e_g_kv_a=v_e_g_kv_a,
             e_w_kv_up=v_e_w_kv_up, e_sinks=v_e_sinks, e_w_out=v_e_w_out, o_g_in=v_o_g_in, o_w_in=v_o_w_in, o_b_f=v_o_b_f,
             o_w_out=v_o_w_out, g_final=v_g_final)
    order = ("e_g_in", "e_w_in", "e_g_q_a", "e_w_q_up", "e_g_kv_a", "e_w_kv_up", "e_sinks", "e_w_out", "o_g_in", "o_w_in",
             "o_b_f", "o_w_out", "g_final")
    half0 = _packed_weights(w, "layer0")
    full = _unpacked_weights(_run_plan(_gather8_plan(half0), "gather_weights_layer0")[0], half0, "layer0")
    layer1 = _Layer1Exchange(w)

    loss_part, dx, grads = _local_step(
        x[0], positions.reshape(-1, 1), loss_target[0], e_g_in, full["e_w_in"], e_g_q_a, full["e_w_q_up"], e_g_kv_a,
        full["e_w_kv_up"], e_sinks, o_b_f, g_final.reshape(1, D), layer1)

    small = jnp.concatenate([jnp.pad(loss_part.reshape(1), (0, LANES - 1))]
                            + [jnp.pad(grads[n].reshape(-1), (0, (-grads[n].size) % LANES)) for n in REPLICATED])
    rows = small.shape[0] // LANES
    small = jnp.pad(small.reshape(rows, LANES), ((0, (-rows) % 8), (0, 0)))
    sums0, (gathered_small,) = _GroupReduce("layer0").run(grads, _gather8_plan(small))
    gsum = {**layer1.sums, **sums0}
    ssum = _sum_leading(_fill_own_slot(gathered_small, small), "small_grad_sum").reshape(-1)
    loss = ssum[0]
    off = LANES
    for n in REPLICATED:
        cnt = w[n].size
        gsum[n] = ssum[off:off + cnt].reshape(w[n].shape)
        off += cnt + (-cnt) % LANES

    grad, delta, new_m, new_v = {}, {}, {}, {}
    for n in order:
        if n == "o_w_in":
            def tiles(a):
                return jnp.transpose(a, (2, 0, 1)).reshape(-1, LANES)

            def given(a):
                return jnp.transpose(a.reshape(-1, 8, LANES), (1, 2, 0)).reshape(w[n].shape)

            g_t = gsum[n].reshape(-1, LANES)
            outs = _adamw(tiles(w[n]), g_t, tiles(m[n]), tiles(v[n]), "adamw_" + n)
            grad[n], delta[n], new_m[n], new_v[n] = (given(a) for a in (g_t,) + outs)
        elif n in SHARDED:
            outs = _adamw(_as_handled(n, w[n]), gsum[n], _as_handled(n, m[n]), _as_handled(n, v[n]), "adamw_" + n)
            grad[n], delta[n], new_m[n], new_v[n] = (_as_given(n, a, w[n].shape) for a in (gsum[n],) + outs)
        else:
            grad[n] = gsum[n]
            delta[n], new_m[n], new_v[n] = _adamw(w[n], gsum[n], m[n], v[n], "adamw_" + n)
    return (loss, dx[None], *[grad[n] for n in order], *[delta[n] for n in order], *[new_m[n] for n in order],
            *[new_v[n] for n in order])
```

```python
import math

import numpy as np
import jax
import jax.numpy as jnp
from jax import lax
from jax.experimental import pallas as pl
from jax.experimental.pallas import tpu as pltpu

D = 1024
EPS = 1e-6
ROPE_THETA = 10000.0
N_MLA = 8
Q_RANK = 256
KV_RANK = 128
NOPE = 64
ROPE = 32
N_SWA = 8
WINDOW = 128
N_FOX = 16
HEAD = 64
LR, B1, B2, AEPS, WD, STEP = 0.001, 0.9, 0.999, 1e-08, 0.01, 10

LANES = 128
HALF = 64
VMEM_LIMIT = 56 * 1024 * 1024
MXU = jnp.bfloat16
TOK = 256
HEAD_TOK = 512
WG_TOK = 2048
WG_ROWS = 1536
ATT = 256
FWD_CHUNK = 2
BWD_CHUNK = 2
SWA_GROUP = 8
NEG = float("-inf")

PACK_COLS = 1024
SUM_ROWS = 256
ADAM_TILE_BYTES = 2 << 20
MESH_ID = pl.DeviceIdType.MESH


def _pcall(body, *, name, vmem=VMEM_LIMIT, semantics=None, **kw):
    params = dict(vmem_limit_bytes=vmem)
    if semantics is not None:
        params["dimension_semantics"] = semantics
    return pl.pallas_call(body, name=name, compiler_params=pltpu.CompilerParams(**params), **kw)


def _mm(a, b):
    return jnp.dot(a.astype(MXU), b.astype(MXU), preferred_element_type=jnp.float32)


def _mm_nt(a, b):
    return lax.dot_general(a.astype(MXU), b.astype(MXU), (((1,), (1,)), ((), ())),
                           preferred_element_type=jnp.float32)


def _mm_tn(a, b):
    return lax.dot_general(a.astype(MXU), b.astype(MXU), (((0,), (0,)), ((), ())),
                           preferred_element_type=jnp.float32)


def _full(shape):
    n = len(shape)
    return pl.BlockSpec(shape, lambda *_: (0,) * n)


def _rows(tm, n):
    return pl.BlockSpec((tm, n), lambda i: (i, 0))


def _sds(shape, dtype):
    return jax.ShapeDtypeStruct(shape, dtype)


def _rms(x, g):
    r = lax.rsqrt(jnp.mean(x * x, axis=-1, keepdims=True) + EPS)
    return x * r * g


def _rms_bwd(x, g, dy):
    r = lax.rsqrt(jnp.mean(x * x, axis=-1, keepdims=True) + EPS)
    xh = x * r
    dxh = dy * g
    dx = r * (dxh - xh * jnp.mean(dxh * xh, axis=-1, keepdims=True))
    return dx, dy * xh


def _sigmoid(x):
    return 1.0 / (1.0 + jnp.exp(-x))


def _lane_masks():
    lane = lax.broadcasted_iota(jnp.int32, (1, LANES), 1)
    return lane < HALF


def _split_heads(a, lo):
    z = jnp.zeros_like(a)
    return [jnp.where(lo, a, z), jnp.where(lo, z, a)]


def _rope_consts():
    inv = np.zeros((8, LANES), np.float32)
    j = np.arange(ROPE // 2, dtype=np.float32)
    f = (1.0 / (ROPE_THETA ** (np.arange(0, ROPE, 2, dtype=np.float32) / ROPE))).astype(np.float32)
    inv[0, HALF:HALF + 16] = f
    inv[0, HALF + 16:HALF + 32] = f
    inv[1, HALF:HALF + 16] = -1.0
    inv[1, HALF + 16:HALF + 32] = 1.0
    del j
    return jnp.asarray(inv)


def _rope_tables(pos_f, consts):
    ang = pos_f * consts[0:1, :]
    sign = consts[1:2, :]
    c = jnp.where(sign != 0.0, jnp.cos(ang), 1.0)
    s = jnp.sin(ang) * sign
    return c, s


def _swap_halves(v, sign):
    lo = pltpu.roll(v, LANES - 16, axis=1)
    hi = pltpu.roll(v, 16, axis=1)
    return jnp.where(sign < 0.0, lo, jnp.where(sign > 0.0, hi, 0.0))


def _rope(x, c, s, sign):
    return x * c + _swap_halves(x, sign) * s


def _rope_t(dy, c, s, sign):
    return dy * c + _swap_halves(dy * s, sign)


def _layer0_in(x, pos, g_in, w_in, g_q, w_q, g_kv, w_kv):
    S = x.shape[0]
    T = math.gcd(HEAD_TOK, S)
    consts = _rope_consts()

    def body(x_ref, pos_ref, c_ref, g_ref, w_ref, gq_ref, wq_ref, gkv_ref, wkv_ref,
             h_ref, cq_ref, ckv_ref, qm_ref, km_ref, vm_ref,
             qs_ref, kd_ref, vd_ref, gate_ref, cos_ref, sin_ref):
        h = _rms(x_ref[...], g_ref[...])
        h_ref[...] = h.astype(h_ref.dtype)
        z = _mm_nt(h, w_ref[...])
        cq = z[:, 0:256]
        ckv = z[:, 256:384]
        kpe = z[:, 384:512]
        cq_ref[...] = cq
        ckv_ref[...] = ckv
        qs_ref[...] = z[:, 512:1024].astype(qs_ref.dtype)
        kd_ref[...] = z[:, 1024:1536].astype(kd_ref.dtype)
        vd_ref[...] = z[:, 1536:2048].astype(vd_ref.dtype)
        gate_ref[...] = z[:, 2048:3072]
        cqn = _rms(cq, gq_ref[...])
        ckvn = _rms(ckv, gkv_ref[...])
        q = _mm_nt(cqn, wq_ref[...])
        kv = _mm(ckvn, wkv_ref[...])
        vm_ref[...] = kv[:, 1024:1536].astype(vm_ref.dtype)
        consts_v = c_ref[...]
        sign = consts_v[1:2, :]
        c, s = _rope_tables(pos_ref[...].astype(jnp.float32), consts_v)
        cos_ref[...] = c
        sin_ref[...] = s
        kpe_r = _rope(kpe, c, s, sign)
        for hd in range(N_MLA):
            sl = slice(LANES * hd, LANES * (hd + 1))
            qm_ref[:, sl] = _rope(q[:, sl], c, s, sign).astype(qm_ref.dtype)
            km_ref[:, sl] = (kv[:, sl] + kpe_r).astype(km_ref.dtype)

    outs = [
        ((S, D), MXU), ((S, 256), jnp.float32), ((S, 128), jnp.float32),
        ((S, 1024), MXU), ((S, 1024), MXU), ((S, 512), MXU), ((S, 512), MXU), ((S, 512), MXU), ((S, 512), MXU),
        ((S, 1024), jnp.float32), ((S, 128), jnp.float32), ((S, 128), jnp.float32),
    ]
    return _pcall(
        body, name="layer0_in", grid=(S // T,), semantics=("arbitrary",),
        in_specs=[_rows(T, D), _rows(T, 1), _full((8, LANES)), _full((1, D)), _full(w_in.shape), _full((1, 256)),
                  _full(w_q.shape), _full((1, 128)), _full(w_kv.shape)],
        out_specs=[_rows(T, s[1]) for s, _ in outs],
        out_shape=[_sds(s, d) for s, d in outs],
    )(x, pos, consts, g_in, w_in, g_q, w_q, g_kv, w_kv)


AUG = (HALF, 0)
ONE = (HALF + 8, 8)


def _data_lanes(idx, h):
    return (idx < HALF) if h == 0 else (idx >= HALF)


def _three_terms(x):
    hi = x.astype(MXU).astype(jnp.float32)
    mid = (x - hi).astype(MXU).astype(jnp.float32)
    lo = (x - hi - mid).astype(MXU).astype(jnp.float32)
    return hi, mid, lo


def _q_aug(qblk, lc, h, scale, lane):
    a = AUG[h]
    hi, mid, lo = _three_terms(lc)
    ones = ((lane >= a + 3) & (lane <= a + 5)).astype(jnp.float32)
    aug = jnp.where(lane == a, hi, jnp.where(lane == a + 1, mid, jnp.where(lane == a + 2, lo, ones)))
    return jnp.where(_data_lanes(lane, h), qblk * jnp.asarray(scale, qblk.dtype), aug.astype(qblk.dtype))


def _k_aug(kblk, lc, h, lane):
    a = AUG[h]
    hi, mid, lo = _three_terms(-lc)
    ones = ((lane >= a) & (lane <= a + 2)).astype(jnp.float32)
    aug = jnp.where(lane == a + 3, hi, jnp.where(lane == a + 4, mid, jnp.where(lane == a + 5, lo, ones)))
    return jnp.where(_data_lanes(lane, h), kblk, aug.astype(kblk.dtype))


def _lc_col(lc_ref, r0, rows, h):
    head = lax.broadcasted_iota(jnp.int32, (1, lc_ref.shape[1]), 1)
    return jnp.sum(jnp.where(head == 2 * pl.program_id(0) + h, lc_ref[pl.ds(r0, rows), :], 0.0), axis=1, keepdims=True)


def _attn_fwd_t(q, k, v, scale, *, split, name, lcc=None, plan=None):
    S = q.shape[0]
    npair = v.shape[1] // LANES
    W = 2 * LANES if split else LANES
    T = ATT
    CH = FWD_CHUNK * T
    assert S % CH == 0
    nq = S // T

    def body(*refs):
        if split:
            q_ref, k_ref, v_ref, o_ref, lse_ref, vt, acc, m_sc = refs
        else:
            q_ref, k_ref, v_ref, lcc_ref, o_ref, lse_ref, kaug, vt, acc, m_sc = refs
        lane = lax.broadcasted_iota(jnp.int32, (1, LANES), 1)
        sub = lax.broadcasted_iota(jnp.int32, (LANES, 1), 0)
        key_minus_qry = lax.broadcasted_iota(jnp.int32, (CH, T), 0) - lax.broadcasted_iota(jnp.int32, (CH, T), 1)

        def prep(i, c):
            r0 = pl.multiple_of(i * T, T)
            vblk = v_ref[pl.ds(r0, T), :].astype(jnp.float32)
            for h in (0, 1):
                vh = jnp.where(_data_lanes(lane, h), vblk, (lane == ONE[h]).astype(jnp.float32))
                vt[h, :, pl.ds(r0, T)] = vh.T.astype(vt.dtype)
                if not split:
                    kaug[h, pl.ds(r0, T), :] = _k_aug(k_ref[pl.ds(r0, T), :], _lc_col(lcc_ref, r0, T, h), h, lane)
            return c

        lax.fori_loop(0, nq, prep, 0)

        def queries(qi):
            q0 = pl.multiple_of(qi * T, T)
            qblk = q_ref[pl.ds(q0, T), :]
            if split:
                return (qblk[:, :LANES], qblk[:, LANES:])
            return tuple(_q_aug(qblk, _lc_col(lcc_ref, q0, T, h), h, scale, lane) for h in (0, 1))

        def scores(qs, c):
            k0 = pl.multiple_of(c * CH, CH)
            out = []
            for h in (0, 1):
                if split:
                    out.append(_mm_nt(k_ref[pl.ds(k0, CH), LANES * h:LANES * (h + 1)], qs[h]) * scale)
                else:
                    out.append(_mm_nt(kaug[h, pl.ds(k0, CH), :], qs[h]))
            return tuple(out)

        def q_block(qi, carry):
            qs, first_scores = carry[:2], carry[2:]
            q0 = pl.multiple_of(qi * T, T)
            acc[...] = jnp.zeros_like(acc)
            m_sc[...] = jnp.full(m_sc.shape, NEG, jnp.float32)

            def absorb(c, sts, masked):
                k0 = pl.multiple_of(c * CH, CH)
                for h in (0, 1):
                    st = sts[h]
                    if masked:
                        st = jnp.where(key_minus_qry <= q0 - k0, st, NEG)
                    m_old = m_sc[h:h + 1, :]
                    m_new = jnp.maximum(m_old, jnp.max(st, axis=0, keepdims=True))
                    alpha = jnp.exp(m_old - m_new)
                    pt = jnp.exp(st - m_new)
                    acc[h] = alpha * acc[h] + _mm(vt[h, :, pl.ds(k0, CH)], pt)
                    m_sc[h:h + 1, :] = m_new

            last = qi // FWD_CHUNK

            def pipelined(c, sts):
                nxt = scores(qs, c + 1)
                absorb(c, sts, False)
                return nxt

            sts = lax.fori_loop(0, last, pipelined, first_scores)
            qs_next = queries(jnp.minimum(qi + 1, nq - 1))
            nxt = qs_next + scores(qs_next, 0)
            absorb(last, sts, True)
            ot = None
            for h in (0, 1):
                a = acc[h]
                l = a[ONE[h]:ONE[h] + 1, :]
                oh = jnp.where(_data_lanes(sub, h), a * (1.0 / l), 0.0)
                ot = oh if ot is None else ot + oh
                lse_ref[0, h:h + 1, pl.ds(q0, T)] = m_sc[h:h + 1, :] + jnp.log(l)
            o_ref[pl.ds(q0, T), :] = ot.T
            return nxt

        qs0 = queries(0)
        lax.fori_loop(0, nq, q_block, qs0 + scores(qs0, 0))

    wide = pl.BlockSpec((S, W), lambda j: (0, j))
    slab = pl.BlockSpec((S, LANES), lambda j: (0, j))
    rows = pl.BlockSpec((1, 2, S), lambda j: (j, 0, 0))
    in_specs = [wide, wide, slab]
    args = [q, k, v]
    scratch = []
    if not split:
        in_specs.append(_full(lcc.shape))
        args.append(lcc)
        scratch.append(pltpu.VMEM((2, S, LANES), MXU))
    scratch += [pltpu.VMEM((2, LANES, S), MXU), pltpu.VMEM((2, LANES, T), jnp.float32), pltpu.VMEM((8, T), jnp.float32)]
    (o, lse), rode = _pcall_riding(
        body, plan, args, name=name, grid=(npair,), in_specs=in_specs, out_specs=[slab, rows],
        out_shape=[_sds((S, npair * LANES), jnp.float32), _sds((npair, 2, S), jnp.float32)], scratch_shapes=scratch)
    return o, lse, rode


def _attn_bwd_t(q, k, v, do, o, lse, scale, *, split, name, lcc=None, plan=None):
    S = q.shape[0]
    npair = v.shape[1] // LANES
    W = 2 * LANES if split else LANES
    T = ATT
    CH = BWD_CHUNK * T
    assert S % CH == 0
    nq = S // T

    def body(*refs):
        if split:
            (q_ref, k_ref, v_ref, do_ref, o_ref, lse_ref, dq_ref, dk_ref, dv_ref, dqt, delta, dk_acc, dv_acc) = refs
        else:
            (q_ref, k_ref, v_ref, do_ref, o_ref, lse_ref, lcc_ref, dq_ref, dk_ref, dv_ref, dlc_ref,
             dqt, delta, dk_acc, dv_acc, qaug, csum) = refs
        lane = lax.broadcasted_iota(jnp.int32, (1, LANES), 1)
        sub = lax.broadcasted_iota(jnp.int32, (LANES, 1), 0)
        key_minus_qry = lax.broadcasted_iota(jnp.int32, (T, CH), 0) - lax.broadcasted_iota(jnp.int32, (T, CH), 1)

        def prep(i, c):
            r0 = pl.multiple_of(i * T, T)
            prod_t = (do_ref[pl.ds(r0, T), :].astype(jnp.float32) * o_ref[pl.ds(r0, T), :]).T
            for h in (0, 1):
                delta[h:h + 1, pl.ds(r0, T)] = jnp.sum(jnp.where(_data_lanes(sub, h), prod_t, 0.0), axis=0, keepdims=True)
                dqt[h, :, pl.ds(r0, T)] = jnp.zeros((LANES, T), jnp.float32)
                if not split:
                    qaug[h, pl.ds(r0, T), :] = _q_aug(q_ref[pl.ds(r0, T), :], _lc_col(lcc_ref, r0, T, h), h, scale, lane)
            return c

        lax.fori_loop(0, nq, prep, 0)

        def keys(ki):
            k0 = pl.multiple_of(ki * T, T)
            kblk = k_ref[pl.ds(k0, T), :]
            if split:
                return (kblk[:, :LANES], kblk[:, LANES:])
            return tuple(_k_aug(kblk, _lc_col(lcc_ref, k0, T, h), h, lane) for h in (0, 1))

        def q_of(c, h):
            q0 = pl.multiple_of(c * CH, CH)
            if split:
                return q_ref[pl.ds(q0, CH), LANES * h:LANES * (h + 1)]
            return qaug[h, pl.ds(q0, CH), :]

        def scores(khs, c):
            out = []
            for h in (0, 1):
                st = _mm_nt(khs[h], q_of(c, h))
                out.append(st * scale if split else st)
            return tuple(out)

        def k_block(ki, carry):
            khs, first_scores = carry[:2], carry[2:]
            k0 = pl.multiple_of(ki * T, T)
            khts = [kh.astype(jnp.float32).T.astype(kh.dtype) for kh in khs]
            vhs = _split_heads(v_ref[pl.ds(k0, T), :], lane < HALF)
            dk_acc[...] = jnp.zeros_like(dk_acc)
            dv_acc[...] = jnp.zeros_like(dv_acc)

            def absorb(c, vals):
                q0 = pl.multiple_of(c * CH, CH)
                dos = _split_heads(do_ref[pl.ds(q0, CH), :], lane < HALF)
                visible = key_minus_qry <= q0 - k0
                for h in (0, 1):
                    dpt = _mm_nt(vhs[h], dos[h])
                    st = jnp.where(visible, vals[h], NEG)
                    pt = jnp.exp(st - lse_ref[0, h:h + 1, pl.ds(q0, CH)])
                    dv_acc[...] += _mm(pt, dos[h])
                    dst = pt * (dpt - delta[h:h + 1, pl.ds(q0, CH)])
                    dk_acc[h] += _mm(dst, q_of(c, h))
                    dqt[h, :, pl.ds(q0, CH)] += _mm(khts[h], dst)

            first = ki // BWD_CHUNK

            def pipelined(c, vals):
                nxt = scores(khs, c + 1)
                absorb(c, vals)
                return nxt

            vals = lax.fori_loop(first, S // CH - 1, pipelined, first_scores)
            kn = jnp.minimum(ki + 1, nq - 1)
            khs_next = keys(kn)
            nxt = khs_next + scores(khs_next, kn // BWD_CHUNK)
            absorb(S // CH - 1, vals)
            if split:
                dk_ref[pl.ds(k0, T), :LANES] = (dk_acc[0] * scale).astype(dk_ref.dtype)
                dk_ref[pl.ds(k0, T), LANES:] = (dk_acc[1] * scale).astype(dk_ref.dtype)
            else:
                dk_ref[pl.ds(k0, T), :] = jnp.where(lane < HALF, dk_acc[0], dk_acc[1]).astype(dk_ref.dtype)
                for h in (0, 1):
                    csum[h:h + 1, pl.ds(k0, T)] = dk_acc[h].T[AUG[h] + 3:AUG[h] + 4, :]
            dv_ref[pl.ds(k0, T), :] = dv_acc[...].astype(dv_ref.dtype)
            return nxt

        khs0 = keys(0)
        lax.fori_loop(0, nq, k_block, khs0 + scores(khs0, 0))

        def finish(i, c):
            r0 = pl.multiple_of(i * T, T)
            if split:
                for h in (0, 1):
                    dq_ref[pl.ds(r0, T), LANES * h:LANES * (h + 1)] = (dqt[h, :, pl.ds(r0, T)].T * scale).astype(dq_ref.dtype)
            else:
                d = jnp.where(sub < HALF, dqt[0, :, pl.ds(r0, T)], dqt[1, :, pl.ds(r0, T)])
                dq_ref[pl.ds(r0, T), :] = (d.T * scale).astype(dq_ref.dtype)
                for h in (0, 1):
                    dlc_ref[0, h:h + 1, pl.ds(r0, T)] = dqt[h, AUG[h]:AUG[h] + 1, pl.ds(r0, T)] - csum[h:h + 1, pl.ds(r0, T)]
            return c

        lax.fori_loop(0, nq, finish, 0)

    wide = pl.BlockSpec((S, W), lambda j: (0, j))
    slab = pl.BlockSpec((S, LANES), lambda j: (0, j))
    rows = pl.BlockSpec((1, 2, S), lambda j: (j, 0, 0))
    in_specs = [wide, wide, slab, slab, slab, rows]
    args = [q, k, v, do, o, lse]
    out_specs = [wide, wide, slab]
    out_shape = [_sds(q.shape, jnp.float32 if split else do.dtype), _sds(k.shape, jnp.float32 if split else do.dtype),
                 _sds(v.shape, do.dtype)]
    scratch = [pltpu.VMEM((2, LANES, S), jnp.float32), pltpu.VMEM((8, S), jnp.float32),
               pltpu.VMEM((2, T, LANES), jnp.float32), pltpu.VMEM((T, LANES), jnp.float32)]
    if not split:
        in_specs.append(_full(lcc.shape))
        args.append(lcc)
        out_specs.append(rows)
        out_shape.append(_sds((npair, 2, S), jnp.float32))
        scratch += [pltpu.VMEM((2, S, LANES), MXU), pltpu.VMEM((8, S), jnp.float32)]
    outs, rode = _pcall_riding(body, plan, args, name=name, grid=(npair,), in_specs=in_specs, out_specs=out_specs,
                               out_shape=out_shape, scratch_shapes=scratch)
    return (*outs, rode)


def _swa_bias(slope, shift):
    a = lax.broadcasted_iota(jnp.int32, (WINDOW, 2 * WINDOW), 0)
    c = lax.broadcasted_iota(jnp.int32, (WINDOW, 2 * WINDOW), 1)
    dist = a - c + shift
    return jnp.where((dist >= 0) & (dist < WINDOW), -slope * dist.astype(jnp.float32), NEG)


def _swa_scores(qh, kblk, bias):
    return _mm_nt(qh, kblk) * (HEAD ** -0.5) + bias


def _swa_stack(blk, lo):
    return jnp.concatenate(_split_heads(blk[:, :LANES], lo) + _split_heads(blk[:, LANES:], lo), axis=0)


def _swa_unstack(x, lo):
    r = x.shape[0] // 4
    return jnp.concatenate([jnp.where(lo, x[0:r], x[r:2 * r]), jnp.where(lo, x[2 * r:3 * r], x[3 * r:])], axis=1)


def _swa_per_head(ref, j, rows):
    quarter = lax.broadcasted_iota(jnp.int32, (4 * rows, 1), 0) // rows
    return jnp.where(quarter == 0, ref[4 * j], jnp.where(quarter == 1, ref[4 * j + 1],
                                                         jnp.where(quarter == 2, ref[4 * j + 2], ref[4 * j + 3])))


def _swa_fwd(q, kd, vd, sinks, slopes):
    S = q.shape[0]
    nkv = q.shape[1] // (2 * LANES)
    nb = S // WINDOW
    group = math.gcd(SWA_GROUP, nb)

    def body(sink_ref, slope_ref, q_ref, k_ref, v_ref, o_ref, lse_ref):
        j = pl.program_id(0)
        lo = _lane_masks()
        sink = _swa_per_head(sink_ref, j, WINDOW)
        biases = [jnp.concatenate([_swa_bias(slope_ref[4 * j + h], shift) for h in range(4)], axis=0)
                  for shift in (0, WINDOW)]

        def q_block(qi, c):
            q0 = pl.multiple_of(qi * WINDOW, WINDOW)
            k0 = pl.multiple_of(jnp.maximum(qi - 1, 0) * WINDOW, WINDOW)
            s = _swa_scores(_swa_stack(q_ref[pl.ds(q0, WINDOW), :], lo), k_ref[pl.ds(k0, 2 * WINDOW), :],
                            jnp.where(qi == 0, *biases))
            m = jnp.maximum(jnp.max(s, axis=1, keepdims=True), sink)
            p = jnp.exp(s - m)
            den = jnp.sum(p, axis=1, keepdims=True) + jnp.exp(sink - m)
            o_ref[pl.ds(q0, WINDOW), :] = _swa_unstack(_mm(p / den, v_ref[pl.ds(k0, 2 * WINDOW), :]), lo)
            lse = m + jnp.log(den)
            for h in range(4):
                lse_ref[h, pl.ds(q0, WINDOW), :] = lse[h * WINDOW:(h + 1) * WINDOW]
            return c

        def q_group(gi, c):
            for g in range(group):
                q_block(gi * group + g, c)
            return c

        lax.fori_loop(0, nb // group, q_group, 0)

    smem = pl.BlockSpec(memory_space=pltpu.SMEM)
    two = pl.BlockSpec((S, 2 * LANES), lambda j: (0, j))
    kv = pl.BlockSpec((S, LANES), lambda j: (0, 2 * j))
    return _pcall(
        body, name="swa_fwd", grid=(nkv,), semantics=("arbitrary",),
        in_specs=[smem, smem, two, kv, kv],
        out_specs=[two, pl.BlockSpec((4, S, 1), lambda j: (j, 0, 0))],
        out_shape=[_sds(q.shape, jnp.float32), _sds((4 * nkv, S, 1), jnp.float32)],
    )(sinks, slopes, q, kd, vd)


def _swa_bwd(q, kd, vd, do, o, lse, sinks, slopes, plan=None):
    S = q.shape[0]
    nkv = q.shape[1] // (2 * LANES)
    nb = S // WINDOW
    group = math.gcd(SWA_GROUP, nb)

    def body(sink_ref, slope_ref, q_ref, k_ref, v_ref, do_ref, o_ref, lse_ref,
             dq_ref, dk_ref, dv_ref, dsink_ref, dk_acc, dv_acc):
        j = pl.program_id(0)
        lo = _lane_masks()
        dk_acc[...] = jnp.zeros_like(dk_acc)
        dv_acc[...] = jnp.zeros_like(dv_acc)
        sink = _swa_per_head(sink_ref, j, WINDOW)
        biases = [jnp.concatenate([_swa_bias(slope_ref[4 * j + h], shift) for h in range(4)], axis=0)
                  for shift in (0, WINDOW)]

        def q_block(qi, carry):
            q0 = pl.multiple_of(qi * WINDOW, WINDOW)
            k0 = pl.multiple_of(jnp.maximum(qi - 1, 0) * WINDOW, WINDOW)
            q4 = _swa_stack(q_ref[pl.ds(q0, WINDOW), :], lo)
            do4 = _swa_stack(do_ref[pl.ds(q0, WINDOW), :], lo)
            oblk = o_ref[pl.ds(q0, WINDOW), :]
            o4 = jnp.concatenate([oblk[:, :LANES], oblk[:, :LANES], oblk[:, LANES:], oblk[:, LANES:]], axis=0)
            kblk = k_ref[pl.ds(k0, 2 * WINDOW), :]
            vblk = v_ref[pl.ds(k0, 2 * WINDOW), :]
            lse = jnp.concatenate([lse_ref[h, pl.ds(q0, WINDOW), :] for h in range(4)], axis=0)
            p = jnp.exp(_swa_scores(q4, kblk, jnp.where(qi == 0, *biases)) - lse)
            delta = jnp.sum(do4.astype(jnp.float32) * o4, axis=1, keepdims=True)
            dv_acc[pl.ds(k0, 2 * WINDOW), :] += _mm_tn(p, do4)
            ds = p * (_mm_nt(do4, vblk) - delta)
            dq_ref[pl.ds(q0, WINDOW), :] = _swa_unstack(_mm(ds, kblk) * (HEAD ** -0.5), lo).astype(dq_ref.dtype)
            dk_acc[pl.ds(k0, 2 * WINDOW), :] += _mm_tn(ds, q4) * (HEAD ** -0.5)
            dsk = -jnp.exp(sink - lse) * delta
            return tuple(carry[h] + jnp.sum(dsk[h * WINDOW:(h + 1) * WINDOW], axis=0, keepdims=True)
                         for h in range(4))

        def q_group(gi, carry):
            for g in range(group):
                carry = q_block(gi * group + g, carry)
            return carry

        zero = jnp.zeros((1, 1), jnp.float32)
        dsinks = lax.fori_loop(0, nb // group, q_group, (zero,) * 4)
        dk_ref[:, :LANES] = dk_acc[...].astype(dk_ref.dtype)
        dk_ref[:, LANES:] = jnp.zeros((S, LANES), dk_ref.dtype)
        dv_ref[:, :LANES] = dv_acc[...].astype(dv_ref.dtype)
        dv_ref[:, LANES:] = jnp.zeros((S, LANES), dv_ref.dtype)
        r = lax.broadcasted_iota(jnp.int32, (8, LANES), 0)
        dsink_ref[0] = jnp.where(r == 0, dsinks[0], jnp.where(r == 1, dsinks[1], jnp.where(r == 2, dsinks[2],
                                 jnp.where(r == 3, dsinks[3], 0.0))))

    smem = pl.BlockSpec(memory_space=pltpu.SMEM)
    two = pl.BlockSpec((S, 2 * LANES), lambda j: (0, j))
    kv = pl.BlockSpec((S, LANES), lambda j: (0, 2 * j))
    outs, rode = _pcall_riding(
        body, plan, [sinks, slopes, q, kd, vd, do, o, lse], name="swa_bwd", grid=(nkv,),
        in_specs=[smem, smem, two, kv, kv, two, two, pl.BlockSpec((4, S, 1), lambda j: (j, 0, 0))],
        out_specs=[two, two, two, pl.BlockSpec((1, 8, LANES), lambda j: (j, 0, 0))],
        out_shape=[_sds(q.shape, do.dtype), _sds(kd.shape, do.dtype), _sds(vd.shape, do.dtype),
                   _sds((nkv, 8, LANES), jnp.float32)],
        scratch_shapes=[pltpu.VMEM((S, LANES), jnp.float32), pltpu.VMEM((S, LANES), jnp.float32)])
    return (*outs, rode)


def _log_steps(S):
    k, out = 1, []
    while k < S:
        out.append(k)
        k *= 2
    return out


def _forget_fwd(f_row, b_col):
    S = f_row.shape[1]

    def body(f_ref, b_ref, lc_ref):
        x = f_ref[...] + b_ref[...]
        lc = jnp.minimum(x, 0.0) - jnp.log(1.0 + jnp.exp(-jnp.abs(x)))
        idx = lax.broadcasted_iota(jnp.int32, lc.shape, 1)
        for k in _log_steps(S):
            lc = lc + jnp.where(idx >= k, pltpu.roll(lc, k, axis=1), 0.0)
        lc_ref[...] = lc

    return _pcall(body, name="forget_fwd", out_shape=_sds(f_row.shape, jnp.float32))(f_row, b_col)


def _forget_bwd(dlc_row, f_row, b_col):
    S = f_row.shape[1]

    def body(d_ref, f_ref, b_ref, df_ref, db_ref):
        g = d_ref[...]
        idx = lax.broadcasted_iota(jnp.int32, g.shape, 1)
        for k in _log_steps(S):
            g = g + jnp.where(idx < S - k, pltpu.roll(g, S - k, axis=1), 0.0)
        x = f_ref[...] + b_ref[...]
        df = g * _sigmoid(-x)
        df_ref[...] = df
        db_ref[...] = jnp.sum(df, axis=1, keepdims=True)

    return _pcall(body, name="forget_bwd",
                  out_shape=[_sds(f_row.shape, jnp.float32), _sds((f_row.shape[0], 1), jnp.float32)])(dlc_row, f_row, b_col)


def _layer0_out_layer1_in(x, o_m, o_s, gate, w_out, g1, w_in1):
    S = x.shape[0]

    def body(x_ref, om_ref, os_ref, gate_ref, wo_ref, g_ref, w_ref,
             x1_ref, h_ref, q_ref, k_ref, v_ref, g1_ref, f_ref):
        gt = gate_ref[...]
        sg = gt * _sigmoid(gt)
        um = om_ref[...] * sg[:, :512]
        us = os_ref[...] * sg[:, 512:]
        x1 = x_ref[...] + _mm(um, wo_ref[0:512, :]) + _mm(us, wo_ref[512:1024, :])
        x1_ref[...] = x1
        h = _rms(x1, g_ref[...])
        h_ref[...] = h.astype(h_ref.dtype)
        z = _mm_nt(h, w_ref[...])
        q_ref[...] = z[:, 0:1024].astype(q_ref.dtype)
        k_ref[...] = z[:, 1024:2048].astype(k_ref.dtype)
        v_ref[...] = z[:, 2048:3072].astype(v_ref.dtype)
        g1_ref[...] = z[:, 3072:4096]
        f_ref[...] = z[:, 4096:4224]

    outs = [((S, D), jnp.float32), ((S, D), MXU), ((S, D), MXU), ((S, D), MXU), ((S, D), MXU),
            ((S, D), jnp.float32), ((S, LANES), jnp.float32)]
    return _pcall(
        body, name="layer0_out_layer1_in", grid=(S // TOK,), semantics=("arbitrary",),
        in_specs=[_rows(TOK, D), _rows(TOK, 512), _rows(TOK, 512), _rows(TOK, D), _full((D, D)), _full((1, D)),
                  _full(w_in1.shape)],
        out_specs=[_rows(TOK, s[1]) for s, _ in outs],
        out_shape=[_sds(s, d) for s, d in outs],
    )(x, o_m, o_s, gate, w_out, g1, w_in1)


def _head(x1, o1, gate1, w_out1, g_f, target):
    S = x1.shape[0]
    T = math.gcd(HEAD_TOK, S)

    def body(x1_ref, o_ref, gate_ref, wo_ref, g_ref, t_ref,
             loss_ref, dgf_ref, dwo_ref, dx2_ref, do_ref, dgate_ref):
        i = pl.program_id(0)
        gt = gate_ref[...]
        sig = _sigmoid(gt)
        sg = gt * sig
        o = o_ref[...]
        u = o * sg
        x2 = x1_ref[...] + _mm(u, wo_ref[...])
        g = g_ref[...]
        y = _rms(x2, g)
        err = y - t_ref[...]
        part = 0.5 * jnp.sum(jnp.mean(err * err, axis=-1, keepdims=True), axis=0, keepdims=True)
        dy = err * (1.0 / D)
        dx2, dg_rows = _rms_bwd(x2, g, dy)
        dx2_ref[...] = dx2
        du = _mm_nt(dx2, wo_ref[...])
        do_ref[...] = (du * sg).astype(do_ref.dtype)
        dgate_ref[...] = (du * o * (sig * (1.0 + gt * (1.0 - sig)))).astype(dgate_ref.dtype)

        @pl.when(i == 0)
        def _():
            loss_ref[...] = jnp.zeros_like(loss_ref)
            dgf_ref[...] = jnp.zeros_like(dgf_ref)
            dwo_ref[...] = jnp.zeros_like(dwo_ref)

        loss_ref[...] += jnp.broadcast_to(part, loss_ref.shape)
        dgf_ref[...] += jnp.sum(dg_rows, axis=0, keepdims=True)
        dwo_ref[...] += _mm_tn(u, dx2)

    outs = [((S, D), jnp.float32), ((S, D), MXU), ((S, D), MXU)]
    return _pcall(
        body, name="head", grid=(S // T,), semantics=("arbitrary",),
        in_specs=[_rows(T, D), _rows(T, D), _rows(T, D), _full((D, D)), _full((1, D)), _rows(T, D)],
        out_specs=[_full((8, LANES)), _full((1, D)), _full((D, D))] + [_rows(T, D) for _ in outs],
        out_shape=[_sds((8, LANES), jnp.float32), _sds((1, D), jnp.float32), _sds((D, D), jnp.float32)]
        + [_sds(s, d) for s, d in outs],
    )(x1, o1, gate1, w_out1, g_f, target)


def _layer1_in_bwd(dq, dk, dv, dgate1, df, x1, dx2, g1, w_in1, gate0, o_m, o_s, w_out0):
    S = x1.shape[0]

    def body(dq_ref, dk_ref, dv_ref, dg1_ref, df_ref, x1_ref, dx2_ref, g_ref, w_ref, gate_ref, om_ref, os_ref,
             wo_ref, dz_ref, dx1_ref, dgn_ref, dwo_ref, dom_ref, dos_ref, dgate_ref):
        i = pl.program_id(0)
        dz_ref[:, 0:1024] = dq_ref[...]
        dz_ref[:, 1024:2048] = dk_ref[...]
        dz_ref[:, 2048:3072] = dv_ref[...]
        dz_ref[:, 3072:4096] = dg1_ref[...]
        dz_ref[:, 4096:4224] = df_ref[...]
        dh = _mm(dz_ref[...], w_ref[...])
        g = g_ref[...]
        dxn, dg_rows = _rms_bwd(x1_ref[...], g, dh)
        dx1 = dx2_ref[...] + dxn
        dx1_ref[...] = dx1
        du = _mm_nt(dx1, wo_ref[...])
        gt = gate_ref[...]
        sig = _sigmoid(gt)
        sg = gt * sig
        dsg = sig * (1.0 + gt * (1.0 - sig))
        dom_ref[...] = (du[:, :512] * sg[:, :512]).astype(dom_ref.dtype)
        dos_ref[...] = (du[:, 512:] * sg[:, 512:]).astype(dos_ref.dtype)
        dgate_ref[:, :512] = (du[:, :512] * om_ref[...] * dsg[:, :512]).astype(dgate_ref.dtype)
        dgate_ref[:, 512:] = (du[:, 512:] * os_ref[...] * dsg[:, 512:]).astype(dgate_ref.dtype)

        @pl.when(i == 0)
        def _():
            dgn_ref[...] = jnp.zeros_like(dgn_ref)
            dwo_ref[...] = jnp.zeros_like(dwo_ref)

        dgn_ref[...] += jnp.sum(dg_rows, axis=0, keepdims=True)
        dwo_ref[0:512, :] += _mm_tn(om_ref[...] * sg[:, :512], dx1)
        dwo_ref[512:1024, :] += _mm_tn(os_ref[...] * sg[:, 512:], dx1)

    return _pcall(
        body, name="layer1_in_bwd", grid=(S // TOK,), semantics=("arbitrary",),
        in_specs=[_rows(TOK, D), _rows(TOK, D), _rows(TOK, D), _rows(TOK, D), _rows(TOK, LANES), _rows(TOK, D),
                  _rows(TOK, D), _full((1, D)), _full(w_in1.shape), _rows(TOK, D), _rows(TOK, 512), _rows(TOK, 512),
                  _full((D, D))],
        out_specs=[_rows(TOK, 4224), _rows(TOK, D), _full((1, D)), _full((D, D)), _rows(TOK, 512), _rows(TOK, 512),
                   _rows(TOK, D)],
        out_shape=[_sds((S, 4224), MXU), _sds((S, D), jnp.float32), _sds((1, D), jnp.float32), _sds((D, D), jnp.float32),
                   _sds((S, 512), MXU), _sds((S, 512), MXU), _sds((S, D), MXU)],
    )(dq, dk, dv, dgate1, df, x1, dx2, g1, w_in1, gate0, o_m, o_s, w_out0)


def _layer0_in_bwd(dqm, dkm, dvm, dqs, dkd, dvd, dgate0, cos, sin, cq, ckv, x, dx1, g_in, w_in, g_q, w_q, g_kv, w_kv):
    S = x.shape[0]
    T = math.gcd(HEAD_TOK, S)
    consts = _rope_consts()

    def body(dqm_ref, dkm_ref, dvm_ref, dqs_ref, dkd_ref, dvd_ref, dgate_ref, cos_ref, sin_ref, c_ref, cq_ref, ckv_ref,
             x_ref, dx1_ref, g_ref, w_ref, gq_ref, wq_ref, gkv_ref, wkv_ref,
             dx_ref, dz_ref, dgin_ref, dgq_ref, dgkv_ref, dwq_ref, dwkv_ref, dqu_ref, dkvu_ref):
        i = pl.program_id(0)
        lo = _lane_masks()
        sign = c_ref[...][1:2, :]
        c = cos_ref[...]
        s = sin_ref[...]
        dkpe = None
        for hd in range(N_MLA):
            sl = slice(LANES * hd, LANES * (hd + 1))
            dqu_ref[:, sl] = _rope_t(dqm_ref[:, sl], c, s, sign).astype(dqu_ref.dtype)
            dkh = dkm_ref[:, sl]
            dkvu_ref[:, sl] = jnp.where(lo, dkh, 0.0).astype(dkvu_ref.dtype)
            dkpe = dkh if dkpe is None else dkpe + dkh
        dkvu_ref[:, 1024:1536] = dvm_ref[...]
        dkpe = _rope_t(jnp.where(lo, 0.0, dkpe), c, s, sign)
        dcqn = _mm(dqu_ref[...], wq_ref[...])
        dckvn = _mm_nt(dkvu_ref[...], wkv_ref[...])
        gq = gq_ref[...]
        gkv = gkv_ref[...]
        dcq, dgq_rows = _rms_bwd(cq_ref[...], gq, dcqn)
        dckv, dgkv_rows = _rms_bwd(ckv_ref[...], gkv, dckvn)
        dz_ref[:, 0:256] = dcq.astype(dz_ref.dtype)
        dz_ref[:, 256:384] = dckv.astype(dz_ref.dtype)
        dz_ref[:, 384:512] = dkpe.astype(dz_ref.dtype)
        dz_ref[:, 512:1024] = dqs_ref[...]
        dz_ref[:, 1024:1536] = dkd_ref[...]
        dz_ref[:, 1536:2048] = dvd_ref[...]
        dz_ref[:, 2048:3072] = dgate_ref[...]
        dh = _mm(dz_ref[...], w_ref[...])
        g = g_ref[...]
        dxn, dg_rows = _rms_bwd(x_ref[...], g, dh)
        dx_ref[...] = dx1_ref[...] + dxn

        @pl.when(i == 0)
        def _():
            dgin_ref[...] = jnp.zeros_like(dgin_ref)
            dgq_ref[...] = jnp.zeros_like(dgq_ref)
            dgkv_ref[...] = jnp.zeros_like(dgkv_ref)
            dwq_ref[...] = jnp.zeros_like(dwq_ref)
            dwkv_ref[...] = jnp.zeros_like(dwkv_ref)

        dgin_ref[...] += jnp.sum(dg_rows, axis=0, keepdims=True)
        dgq_ref[...] += jnp.sum(dgq_rows, axis=0, keepdims=True)
        dgkv_ref[...] += jnp.sum(dgkv_rows, axis=0, keepdims=True)
        dwq_ref[...] += _mm_tn(dqu_ref[...], _rms(cq_ref[...], gq))
        dwkv_ref[...] += _mm_tn(_rms(ckv_ref[...], gkv), dkvu_ref[...])

    return _pcall(
        body, name="layer0_in_bwd", grid=(S // T,), semantics=("arbitrary",),
        in_specs=[_rows(T, 1024), _rows(T, 1024), _rows(T, 512), _rows(T, 512), _rows(T, 512), _rows(T, 512),
                  _rows(T, D), _rows(T, LANES), _rows(T, LANES), _full((8, LANES)), _rows(T, 256), _rows(T, 128),
                  _rows(T, D), _rows(T, D), _full((1, D)), _full(w_in.shape), _full((1, 256)), _full(w_q.shape),
                  _full((1, 128)), _full(w_kv.shape)],
        out_specs=[_rows(T, D), _rows(T, 3072), _full((1, D)), _full((1, 256)), _full((1, 128)), _full(w_q.shape),
                   _full(w_kv.shape)],
        out_shape=[_sds((S, D), jnp.float32), _sds((S, 3072), MXU), _sds((1, D), jnp.float32), _sds((1, 256), jnp.float32),
                   _sds((1, 128), jnp.float32), _sds(w_q.shape, jnp.float32), _sds(w_kv.shape, jnp.float32)],
        scratch_shapes=[pltpu.VMEM((T, 1024), MXU), pltpu.VMEM((T, 1536), MXU)],
    )(dqm, dkm, dvm, dqs, dkd, dvd, dgate0, cos, sin, consts, cq, ckv, x, dx1, g_in, w_in, g_q, w_q, g_kv, w_kv)


def _wgrad(a, b, name, plan=None):
    S, M = a.shape
    N = b.shape[1]
    tm = next(t for t in range(WG_ROWS, 0, -LANES) if M % t == 0)
    tn = N if N <= 1024 else 512
    tk = min(WG_TOK, S)

    def body(a_ref, b_ref, o_ref):
        @pl.when(pl.program_id(2) == 0)
        def _():
            o_ref[...] = jnp.zeros_like(o_ref)

        o_ref[...] += _mm_tn(a_ref[...], b_ref[...])

    (dw,), rode = _pcall_riding(
        body, plan, [a, b], name=name, grid=(M // tm, N // tn, S // tk),
        in_specs=[pl.BlockSpec((tk, tm), lambda m, n, k: (k, m)), pl.BlockSpec((tk, tn), lambda m, n, k: (k, n))],
        out_specs=[pl.BlockSpec((tm, tn), lambda m, n, k: (m, n))],
        out_shape=[_sds((M, N), jnp.float32)], scratch_shapes=[])
    return dw, rode


def _adamw(w, g, m, v, name):
    shape = w.shape
    R, C = (int(np.prod(shape[:-1])), shape[-1])
    w2, g2, m2, v2 = (t.reshape(R, C) for t in (w, g, m, v))
    fits = [t for t in range(8, ADAM_TILE_BYTES // (4 * C) + 1, 8) if R % t == 0]
    tr = max(fits) if fits else R
    tc = C if (tr * C * 4 <= ADAM_TILE_BYTES or C % 256) else 256

    def body(w_ref, g_ref, m_ref, v_ref, d_ref, nm_ref, nv_ref):
        gg = g_ref[...]
        nm = B1 * m_ref[...] + (1.0 - B1) * gg
        nv = B2 * v_ref[...] + (1.0 - B2) * (gg * gg)
        m_hat = nm / (1.0 - B1 ** STEP)
        v_hat = nv / (1.0 - B2 ** STEP)
        d_ref[...] = -LR * (m_hat / (jnp.sqrt(v_hat) + AEPS) + WD * w_ref[...])
        nm_ref[...] = nm
        nv_ref[...] = nv

    spec = pl.BlockSpec((tr, tc), lambda i, j: (i, j))
    d, nm, nv = _pcall(
        body, name=name, grid=(R // tr, C // tc), semantics=("parallel", "parallel"),
        in_specs=[spec] * 4, out_specs=[spec] * 3, out_shape=[_sds((R, C), jnp.float32)] * 3,
    )(w2, g2, m2, v2)
    return d.reshape(shape), nm.reshape(shape), nv.reshape(shape)


def _sum_leading(a, name):
    n, R, C = a.shape
    tr = SUM_ROWS if R % SUM_ROWS == 0 else R

    def body(a_ref, o_ref):
        acc = a_ref[0]
        for i in range(1, n):
            acc = acc + a_ref[i]
        o_ref[...] = acc

    return _pcall(
        body, name=name, grid=(R // tr,), semantics=("parallel",),
        in_specs=[pl.BlockSpec((n, tr, C), lambda i: (0, i, 0))], out_specs=_rows(tr, C),
        out_shape=_sds((R, C), a.dtype),
    )(a)


def _add_halves(g, c, b, name, out_dtype):
    n, _, R, C = g.shape
    tr = SUM_ROWS if R % SUM_ROWS == 0 else R

    def body(c_ref, a_ref, b_ref, o_ref):
        o_ref[...] = (a_ref[0] + b_ref[...]).astype(o_ref.dtype)

    spec = pl.BlockSpec((1, tr, C), lambda k, i, c_ref: (k, i, 0))
    grid_spec = pltpu.PrefetchScalarGridSpec(
        num_scalar_prefetch=1, grid=(n, R // tr),
        in_specs=[pl.BlockSpec((1, 1, tr, C), lambda k, i, c_ref: (k, c_ref[0], i, 0)), spec], out_specs=spec)
    return _pcall(body, name=name, semantics=("parallel", "parallel"), grid_spec=grid_spec,
                  out_shape=_sds(b.shape, out_dtype))(c.reshape(1).astype(jnp.int32), g, b)


def _total_sum(g, theirs, chip, c, recv, name):
    _, _, R, C = g.shape
    n = recv.shape[0]
    tr = SUM_ROWS if R % SUM_ROWS == 0 else R

    def body(at_ref, a_ref, b_ref, r_ref, o_ref):
        acc = a_ref[0, 0] + b_ref[0]
        for i in range(n):
            acc = acc + r_ref[i].astype(jnp.float32)
        o_ref[...] = acc

    grid_spec = pltpu.PrefetchScalarGridSpec(
        num_scalar_prefetch=1, grid=(R // tr,),
        in_specs=[pl.BlockSpec((1, 1, tr, C), lambda i, at_ref: (at_ref[0], at_ref[1], i, 0)),
                  pl.BlockSpec((1, tr, C), lambda i, at_ref: (at_ref[0], i, 0)),
                  pl.BlockSpec((n, tr, C), lambda i, at_ref: (0, i, 0))],
        out_specs=pl.BlockSpec((tr, C), lambda i, at_ref: (i, 0)))
    return _pcall(body, name=name, semantics=("parallel",), grid_spec=grid_spec,
                  out_shape=_sds((R, C), jnp.float32))(jnp.stack([chip, c]).astype(jnp.int32), g, theirs, recv)


def _place():
    return lax.axis_index("x"), lax.axis_index("y"), lax.axis_index("c")


class _Plan:
    def __init__(self, arrays, out_shape, scratch, start, finish, middle=None):
        self.arrays, self.out_shape, self.scratch = list(arrays), list(out_shape), list(scratch)
        self.start, self.finish, self.middle = start, finish, middle


def _gather8_plan(block):
    R, C = block.shape

    def parts(ins, outs, sems):
        (x_ref,), (out_ref,), (send_sems, recv_sems) = ins, outs, sems
        x, y, c = _place()
        me, sibling = (x, y, c), (x, y, 1 - c)
        chips = [(1 - x, y), (x, 1 - y), (1 - x, 1 - y)]

        def copy(k, blk, to, src=None):
            slot = out_ref.at[4 * blk[0] + 2 * blk[1] + blk[2]]
            return pltpu.make_async_remote_copy(
                src_ref=slot if src is None else src, dst_ref=slot,
                send_sem=send_sems.at[k], recv_sem=recv_sems.at[k], device_id=to, device_id_type=MESH_ID)

        def first():
            return [copy(0, me, sibling, src=x_ref)] + [copy(1 + j, me, (*chip, c), src=x_ref) for j, chip in enumerate(chips)]

        def passed():
            return [copy(4 + j, (*chip, c), sibling) for j, chip in enumerate(chips)]

        def arrivals():
            return [copy(1 + j, (*chip, c), me) for j, chip in enumerate(chips)]

        def late():
            return [copy(0, sibling, me)] + [copy(4 + j, (*chip, 1 - c), me) for j, chip in enumerate(chips)]

        return first, passed, arrivals, late

    def start(ins, outs, sems):
        for cp in parts(ins, outs, sems)[0]():
            cp.start()

    def middle(ins, outs, sems):
        _, passed, arrivals, _ = parts(ins, outs, sems)
        for arrived, forward in zip(arrivals(), passed()):
            arrived.wait_recv()
            forward.start()

    def finish(ins, outs, sems):
        first, passed, _, late = parts(ins, outs, sems)
        for cp in late():
            cp.wait_recv()
        for cp in first() + passed():
            cp.wait_send()

    return _Plan([block], [_sds((8, R, C), block.dtype)], [pltpu.SemaphoreType.DMA((7,)), pltpu.SemaphoreType.DMA((7,))],
                 start, finish, middle)


def _fill_own_slot(gathered, block):
    x, y, c = _place()
    return lax.dynamic_update_index_in_dim(gathered, block, 4 * x + 2 * y + c, 0)


def _started_and_waited(arrays, out_shape, n, copies):
    def start(ins, outs, sems):
        for cp in copies(ins, outs, sems):
            cp.start()

    def finish(ins, outs, sems):
        for cp in copies(ins, outs, sems):
            cp.wait()

    return _Plan(arrays, out_shape, [pltpu.SemaphoreType.DMA((n,)), pltpu.SemaphoreType.DMA((n,))], start, finish)


def _pair_swap_plan(g):
    n = g.shape[0]

    def copies(ins, outs, sems):
        (g_ref,), (out_ref,), (send_sems, recv_sems) = ins, outs, sems
        x, y, c = _place()
        return [pltpu.make_async_remote_copy(src_ref=g_ref.at[k, 1 - c], dst_ref=out_ref.at[k], send_sem=send_sems.at[k],
                                             recv_sem=recv_sems.at[k], device_id=(x, y, 1 - c), device_id_type=MESH_ID)
                for k in range(n)]

    return _started_and_waited([g], [_sds((n,) + g.shape[2:], g.dtype)], n, copies)


def _chip_exchange_plan(p):
    def copies(ins, outs, sems):
        (p_ref,), (out_ref,), (send_sems, recv_sems) = ins, outs, sems
        x, y, c = _place()
        chips = [(1 - x, y), (x, 1 - y), (1 - x, 1 - y)]
        return [pltpu.make_async_remote_copy(
            src_ref=p_ref.at[2 * cx + cy], dst_ref=out_ref.at[j], send_sem=send_sems.at[j],
            recv_sem=recv_sems.at[j], device_id=(cx, cy, c), device_id_type=MESH_ID)
            for j, (cx, cy) in enumerate(chips)]

    return _started_and_waited([p], [_sds((3,) + p.shape[1:], p.dtype)], 3, copies)


def _pair_exchange_plan(t):
    def copies(ins, outs, sems):
        (t_ref,), (out_ref,), (send_sems, recv_sems) = ins, outs, sems
        x, y, c = _place()
        return [pltpu.make_async_remote_copy(src_ref=t_ref, dst_ref=out_ref, send_sem=send_sems.at[0], recv_sem=recv_sems.at[0],
                                             device_id=(x, y, 1 - c), device_id_type=MESH_ID)]

    return _started_and_waited([t], [_sds(t.shape, t.dtype)], 1, copies)


def _both_plans(a, b):
    na, ma, sa = len(a.arrays), len(a.out_shape), len(a.scratch)

    def phase(name):
        fa, fb = getattr(a, name), getattr(b, name)
        if fa is None and fb is None:
            return None

        def run(ins, outs, sems):
            if fa is not None:
                fa(ins[:na], outs[:ma], sems[:sa])
            if fb is not None:
                fb(ins[na:], outs[ma:], sems[sa:])
        return run

    return _Plan(a.arrays + b.arrays, a.out_shape + b.out_shape, a.scratch + b.scratch,
                 phase("start"), phase("finish"), phase("middle"))


ANY_SPEC = pl.BlockSpec(memory_space=pl.ANY)


def _run_plan(plan, name):
    n_in, n_out = len(plan.arrays), len(plan.out_shape)

    def body(*refs):
        ins, outs, sems = refs[:n_in], refs[n_in:n_in + n_out], refs[n_in + n_out:]
        plan.start(ins, outs, sems)
        if plan.middle is not None:
            plan.middle(ins, outs, sems)
        plan.finish(ins, outs, sems)

    return _pcall(body, name=name, in_specs=[ANY_SPEC] * n_in, out_specs=[ANY_SPEC] * n_out, out_shape=plan.out_shape,
                  scratch_shapes=plan.scratch)(*plan.arrays)


def _pcall_riding(body, plan, args, *, name, grid, in_specs, out_specs, out_shape, scratch_shapes):
    order = ("arbitrary",) * len(grid)
    if plan is None:
        outs = _pcall(body, name=name, grid=grid, semantics=order, in_specs=in_specs, out_specs=out_specs,
                      out_shape=out_shape, scratch_shapes=scratch_shapes)(*args)
        return list(outs), None
    n_in, n_out, n_s = len(args), len(out_shape), len(scratch_shapes)
    p_in, p_out = len(plan.arrays), len(plan.out_shape)
    steps = math.prod(grid)

    def riding(*refs):
        ins, pins = refs[:n_in], refs[n_in:n_in + p_in]
        o0 = n_in + p_in
        outs, pouts = refs[o0:o0 + n_out], refs[o0 + n_out:o0 + n_out + p_out]
        s0 = o0 + n_out + p_out
        scr, sems = refs[s0:s0 + n_s], refs[s0 + n_s:]
        j = pl.program_id(0)
        for axis in range(1, len(grid)):
            j = j * grid[axis] + pl.program_id(axis)

        @pl.when(j == 0)
        def _():
            plan.start(pins, pouts, sems)

        if plan.middle is not None:
            @pl.when(j == steps // 2)
            def _():
                plan.middle(pins, pouts, sems)

        body(*ins, *outs, *scr)

        @pl.when(j == steps - 1)
        def _():
            plan.finish(pins, pouts, sems)

    res = _pcall(riding, name=name, grid=grid, semantics=order, in_specs=list(in_specs) + [ANY_SPEC] * p_in,
                 out_specs=list(out_specs) + [ANY_SPEC] * p_out, out_shape=list(out_shape) + plan.out_shape,
                 scratch_shapes=list(scratch_shapes) + plan.scratch)(*args, *plan.arrays)
    return list(res[:n_out]), list(res[n_out:])


class _RowSeq:
    def __init__(self, pieces):
        self.pieces = list(pieces)

    def rows(self, a, b):
        out, off = [], 0
        for p in self.pieces:
            lo, hi = max(a, off), min(b, off + p.shape[0])
            if lo < hi:
                out.append(p[lo - off:hi - off])
            off += p.shape[0]
        return out

    def array(self):
        return jnp.concatenate(self.pieces, axis=0)


def _row_seq(w):
    return w if isinstance(w, _RowSeq) else _RowSeq([w])


def _prep_w_in0(wt):
    wt = _row_seq(wt)
    one = wt.pieces[0]
    z32 = [jnp.zeros((32, one.shape[1]), one.dtype)]
    k0, k1 = wt.rows(928, 992), wt.rows(992, 1056)
    v0, v1 = wt.rows(1056, 1120), wt.rows(1120, 1184)
    return jnp.concatenate(wt.rows(0, 384) + z32 + z32 + wt.rows(384, 416) + z32 + wt.rows(416, 928)
                           + k0 * 4 + k1 * 4 + v0 * 4 + v1 * 4 + wt.rows(1184, 2208), axis=0)


def _fold_w_in0(d):
    def fold(blk):
        b = blk.reshape(8, 64, blk.shape[1])
        return jnp.concatenate([b[0] + b[1] + b[2] + b[3], b[4] + b[5] + b[6] + b[7]], axis=0)
    return _RowSeq([d[0:384], d[448:480], d[512:1024], fold(d[1024:1536]), fold(d[1536:2048]), d[2048:3072]])


def _prep_w_q(wt):
    return jnp.pad(wt.reshape(N_MLA, 96, Q_RANK), ((0, 0), (0, 32), (0, 0))).reshape(1024, Q_RANK)


def _fold_w_q(d):
    return d.reshape(N_MLA, 128, Q_RANK)[:, :96].reshape(768, Q_RANK)


def _prep_w_kv(w):
    w3 = w.reshape(KV_RANK, N_MLA, 128)
    kk = jnp.pad(w3[:, :, :64], ((0, 0), (0, 0), (0, 64))).reshape(KV_RANK, 1024)
    return jnp.concatenate([kk, w3[:, :, 64:].reshape(KV_RANK, 512)], axis=1)


def _fold_w_kv(d):
    kk = d[:, :1024].reshape(KV_RANK, N_MLA, 128)[:, :, :64]
    vv = d[:, 1024:].reshape(KV_RANK, N_MLA, 64)
    return jnp.concatenate([kk, vv], axis=2).reshape(KV_RANK, 1024)


W_IN1_SHARD = 1028
W_IN1_STEP = W_IN1_SHARD % 16


class _ShiftedShards:
    def __init__(self, blocks):
        self.blocks = list(blocks)


def _shifted_shard(a, chip, rows):
    out = jnp.zeros((rows, a.shape[1]), a.dtype)
    for k in range(4):
        out = jnp.where(chip == k, jnp.pad(a, ((W_IN1_STEP * k, rows - W_IN1_STEP * k - a.shape[0]), (0, 0))), out)
    return out


def _prep_w_in1(wt):
    if not isinstance(wt, _ShiftedShards):
        return jnp.concatenate([wt[0:3072], wt[3088:4112], wt[3072:3088], jnp.zeros((112, wt.shape[1]), wt.dtype)], axis=0)
    b = wt.blocks
    row = lax.broadcasted_iota(jnp.int32, (16, 1), 0)

    def seam(k, first, second):
        return jnp.where(row < W_IN1_STEP * (k + 1), first, second)

    return jnp.concatenate([
        b[0][0:1024], seam(0, b[0][1024:1040], b[1][0:16]), b[1][16:1024], seam(1, b[1][1024:1040], b[2][0:16]),
        b[2][16:1024], b[3][16:1040], seam(2, b[2][1024:1040], b[3][0:16]), jnp.zeros((112, 1024), b[0].dtype)], axis=0)


def _fold_w_in1(d):
    return _RowSeq([d[0:3072], d[4096:4112], d[3072:4096]])


class _Alone:
    def __init__(self, w_out0, o_g_in, w_in1, w_out1):
        self.layer1 = (w_out0, o_g_in, w_in1, w_out1)

    def gather_plan(self):
        return None

    def layer1_weights(self, rode):
        return self.layer1

    def swap_plan(self, grads1):
        return None

    def exchange_plan(self, rode):
        return None

    def halves_plan(self, rode):
        return None

    def finish(self, rode):
        pass


def _local_step(x, pos, target, e_g_in, w_in0, e_g_q, w_q, e_g_kv, w_kv, sinks, b_f, g_final, layer1):
    S = x.shape[0]
    w_in0p, w_qp, w_kvp = _prep_w_in0(w_in0), _prep_w_q(w_q), _prep_w_kv(w_kv)
    slopes = jnp.asarray(2.0 ** (-8.0 * (np.arange(N_SWA, dtype=np.float32) + 1.0) / N_SWA), jnp.float32)
    sinks1 = sinks.reshape(N_SWA)
    b_col = b_f.reshape(N_FOX, 1)

    (h0, cq, ckv, qm, km, vm, qs, kd, vd, gate0, cos, sin) = _layer0_in(
        x, pos, e_g_in, w_in0p, e_g_q, w_qp, e_g_kv, w_kvp)
    o_m, lse_m, rode = _attn_fwd_t(qm, km, vm, (NOPE + ROPE) ** -0.5, split=True, name="mla_fwd", plan=layer1.gather_plan())
    w_out0, o_g_in, w_in1, w_out1 = layer1.layer1_weights(rode)
    w_in1p = _prep_w_in1(w_in1)
    o_s, lse_s = _swa_fwd(qs, kd, vd, sinks1, slopes)
    x1, h1, q1, k1, v1, gate1, f_slab = _layer0_out_layer1_in(x, o_m, o_s, gate0, w_out0, o_g_in, w_in1p)
    f_row = f_slab[:, :N_FOX].T
    lc_row = _forget_fwd(f_row, b_col)
    lcc = lc_row.T
    o1, lse1, _ = _attn_fwd_t(q1, k1, v1, HEAD ** -0.5, split=False, name="fox_fwd", lcc=lcc)
    loss8, dg_final, dw_out1, dx2, do1, dgate1 = _head(x1, o1, gate1, w_out1, g_final, target)

    dq1, dk1, dv1, dlc, _ = _attn_bwd_t(q1, k1, v1, do1, o1, lse1, HEAD ** -0.5, split=False, name="fox_bwd", lcc=lcc)
    df_row, db_f = _forget_bwd(dlc.reshape(N_FOX, S), f_row, b_col)
    df_slab = jnp.pad(df_row.T, ((0, 0), (0, LANES - N_FOX))).astype(MXU)
    dz1, dx1, dg_o_in, dw_out0, do_m, do_s, dgate0 = _layer1_in_bwd(
        dq1, dk1, dv1, dgate1, df_slab, x1, dx2, o_g_in, w_in1p, gate0, o_m, o_s, w_out0)
    grads1 = dict(o_g_in=dg_o_in, o_w_in=_fold_w_in1(_wgrad(dz1, h1, "wgrad_in1")[0]), o_w_out=dw_out1,
                  e_w_out=dw_out0)
    dqs, dkd, dvd, dsink, rode = _swa_bwd(qs, kd, vd, do_s, o_s, lse_s, sinks1, slopes, plan=layer1.swap_plan(grads1))
    dqm, dkm, dvm, rode = _attn_bwd_t(qm, km, vm, do_m, o_m, lse_m, (NOPE + ROPE) ** -0.5, split=True, name="mla_bwd",
                                      plan=layer1.exchange_plan(rode))
    halves_plan = layer1.halves_plan(rode)
    dx, dz0, dg_in, dg_q, dg_kv, dw_q, dw_kv = _layer0_in_bwd(
        dqm, dkm, dvm, dqs, dkd, dvd, dgate0, cos, sin, cq, ckv, x, dx1, e_g_in, w_in0p, e_g_q, w_qp, e_g_kv, w_kvp)
    dw_in0, rode = _wgrad(dz0, h0, "wgrad_in0", plan=halves_plan)
    layer1.finish(rode)

    grads = dict(
        e_g_in=dg_in,
        e_w_in=_fold_w_in0(dw_in0),
        e_g_q_a=dg_q,
        e_w_q_up=_fold_w_q(dw_q),
        e_g_kv_a=dg_kv,
        e_w_kv_up=_fold_w_kv(dw_kv),
        e_sinks=dsink[:, 0:4, 0].reshape(1, N_SWA),
        o_b_f=db_f.reshape(1, N_FOX),
        g_final=dg_final,
        **grads1,
    )
    return loss8[0, 0], dx, grads


SHARDED = ("e_w_in", "e_w_q_up", "e_w_kv_up", "e_w_out", "o_g_in", "o_w_in", "o_w_out")
TRANSPOSED = ("e_w_in", "e_w_q_up", "o_w_in")
COL_SHARDED = ("e_w_kv_up", "o_g_in")
REPLICATED = ("e_g_in", "e_g_q_a", "e_g_kv_a", "e_sinks", "o_b_f", "g_final")
FULL_SHAPES = dict(e_w_in=(2208, 1024), e_w_q_up=(768, 256), e_w_kv_up=(128, 1024), e_w_out=(1024, 1024),
                   o_g_in=(1, 1024), o_w_in=(4112, 1024), o_w_out=(1024, 1024))
GROUPS = dict(
    layer0=dict(rows=768, windows=dict(e_w_in=(0, 0), e_w_q_up=(560, 0), e_w_kv_up=(560, 256))),
    layer1=dict(rows=1568, windows=dict(o_w_in=(0, 0), o_w_out=(1040, 0), e_w_out=(1296, 0), o_g_in=(1552, 0))),
)


def _shard_shape(name):
    r, c = FULL_SHAPES[name]
    return (r, c // 4) if name in COL_SHARDED else (r // 4, c)


def _as_handled(name, a):
    a = a[0] if a.ndim == 3 else a
    return a.T if name in TRANSPOSED else a


def _as_given(name, a, shape):
    return (a.T if name in TRANSPOSED else a).reshape(shape)


def _pack_block(p, group, shifted_for=None):
    def rows(a, n):
        return jnp.pad(a, ((0, n - a.shape[0]), (0, 0)))

    if group == "layer0":
        band = jnp.concatenate([p["e_w_q_up"], rows(p["e_w_kv_up"], 192), jnp.zeros((192, 512), p["e_w_in"].dtype)], axis=1)
        return jnp.concatenate([rows(p["e_w_in"], 560), rows(band, 208)], axis=0)
    g = p["o_g_in"]
    band = jnp.pad(g, ((0, 16 - g.shape[0]), (0, PACK_COLS - g.shape[1])))
    w_in = rows(p["o_w_in"], 1040) if shifted_for is None else _shifted_shard(p["o_w_in"], shifted_for, 1040)
    return jnp.concatenate([w_in, p["o_w_out"], p["e_w_out"], band], axis=0)


def _window(block, group, name, width=None):
    r0, c0 = GROUPS[group]["windows"][name]
    r, c = _shard_shape(name)
    return block[..., r0:r0 + r, c0:c0 + (c if width is None else width)]


def _chip_slice(name, full, k):
    r, c = _shard_shape(name)
    if isinstance(full, _RowSeq):
        return jnp.concatenate(full.rows(r * k, r * (k + 1)), axis=0)
    return full[:, c * k:c * (k + 1)] if name in COL_SHARDED else full[r * k:r * (k + 1), :]


def _packed_weights(w, group):
    parts = {}
    for n in GROUPS[group]["windows"]:
        a = _as_handled(n, w[n])
        parts[n] = lax.bitcast_convert_type(a, jnp.bfloat16).reshape(1, -1) if n == "o_g_in" else a.astype(jnp.bfloat16)
    x, y, _ = _place()
    halves = _pack_block(parts, group, shifted_for=2 * x + y).reshape(2, GROUPS[group]["rows"] // 2, PACK_COLS)
    return lax.dynamic_index_in_dim(halves, lax.axis_index("c"), 0, keepdims=False)


def _unpacked_weights(gathered, half, group):
    blocks = _fill_own_slot(gathered, half).reshape(4, GROUPS[group]["rows"], PACK_COLS)
    full = {}
    for n in GROUPS[group]["windows"]:
        if n == "o_g_in":
            halves = _window(blocks, group, n, width=512).reshape(4, 1, 256, 2)
            full[n] = jnp.concatenate(list(lax.bitcast_convert_type(halves, jnp.float32)), axis=1)
        elif n == "o_w_in":
            full[n] = _ShiftedShards(blocks[k, 0:1040].astype(MXU) for k in range(4))
        else:
            pieces = [_window(blocks[k], group, n).astype(MXU) for k in range(4)]
            if n == "e_w_in":
                full[n] = _RowSeq(pieces)
            else:
                full[n] = jnp.concatenate(pieces, axis=1 if n in COL_SHARDED else 0)
    return full


class _GroupReduce:
    def __init__(self, group):
        self.group = group
        self.c = lax.axis_index("c")
        self.chip = 2 * lax.axis_index("x") + lax.axis_index("y")

    def swap_plan(self, grads):
        names = GROUPS[self.group]["windows"]
        per_chip = jnp.concatenate([_pack_block({n: _chip_slice(n, grads[n], k) for n in names}, self.group)
                                    for k in range(4)], axis=0)
        self.g4 = per_chip.reshape(4, 2, GROUPS[self.group]["rows"] // 2, PACK_COLS)
        return _pair_swap_plan(self.g4)

    def exchange_plan(self, rode):
        self.theirs = rode[0]
        return _chip_exchange_plan(_add_halves(self.g4, self.c, self.theirs, "pair_add_" + self.group, jnp.bfloat16))

    def halves_plan(self, rode):
        self.my_half = _total_sum(self.g4, self.theirs, self.chip, self.c, rode[0], "chip_sum_" + self.group)
        return _pair_exchange_plan(self.my_half)

    def finish(self, rode):
        my_half, other_half = self.my_half, rode[0]
        total = jnp.concatenate([jnp.where(self.c == 0, my_half, other_half), jnp.where(self.c == 0, other_half, my_half)], axis=0)
        self.sums = {n: _window(total, self.group, n) for n in GROUPS[self.group]["windows"]}

    def run(self, grads, beside):
        swap = self.swap_plan(grads)
        outs = _run_plan(_both_plans(swap, beside), "pair_swap_" + self.group)
        rode, others = outs[:len(swap.out_shape)], outs[len(swap.out_shape):]
        halves = self.halves_plan(_run_plan(self.exchange_plan(rode), "chip_exchange_" + self.group))
        self.finish(_run_plan(halves, "pair_exchange_" + self.group))
        return self.sums, others


class _Layer1Exchange(_GroupReduce):
    def __init__(self, w):
        super().__init__("layer1")
        self.half = _packed_weights(w, "layer1")

    def gather_plan(self):
        return _gather8_plan(self.half)

    def layer1_weights(self, rode):
        full = _unpacked_weights(rode[0], self.half, "layer1")
        return full["e_w_out"], full["o_g_in"], full["o_w_in"], full["o_w_out"]


def kernel(x, positions, e_g_in, e_w_in, e_g_q_a, e_w_q_up, e_g_kv_a, e_w_kv_up, e_sinks, e_w_out, o_g_in, o_w_in, o_b_f, o_w_out, g_final, loss_target, m_e_g_in, m_e_w_in, m_e_g_q_a, m_e_w_q_up, m_e_g_kv_a, m_e_w_kv_up, m_e_sinks, m_e_w_out, m_o_g_in, m_o_w_in, m_o_b_f, m_o_w_out, m_g_final, v_e_g_in, v_e_w_in, v_e_g_q_a, v_e_w_q_up, v_e_g_kv_a, v_e_w_kv_up, v_e_sinks, v_e_w_out, v_o_g_in, v_o_w_in, v_o_b_f, v_o_w_out, v_g_final):
    w = dict(e_g_in=e_g_in, e_w_in=e_w_in, e_g_q_a=e_g_q_a, e_w_q_up=e_w_q_up, e_g_kv_a=e_g_kv_a, e_w_kv_up=e_w_kv_up,
             e_sinks=e_sinks, e_w_out=e_w_out, o_g_in=o_g_in, o_w_in=o_w_in, o_b_f=o_b_f, o_w_out=o_w_out, g_final=g_final)
    m = dict(e_g_in=m_e_g_in, e_w_in=m_e_w_in, e_g_q_a=m_e_g_q_a, e_w_q_up=m_e_w_q_up, e_g_kv_a=m_e_g_kv_a,
             e_w_kv_up=m_e_w_kv_up, e_sinks=m_e_sinks, e_w_out=m_e_w_out, o_g_in=m_o_g_in, o_w_in=m_o_w_in, o_b_f=m_o_b_f,
             o_w_out=m_o_w_out, g_final=m_g_final)
    v = dict(e_g_in=v_e_g_in, e_w_in=v_e_w_in, e_g_q_a=v_e_g_q_a, e_w_q_up=v_e_w_q_up, e_g_kv_a=v_e_g_kv_a,
             e_w_kv_up=v_e_w_kv_up, e_sinks=v_e_sinks, e_w_out=v_e_w_out, o_g_in=v_o_g_in, o_w_in=v_o_w_in, o_b_f=v_o_b_f,
             o_w_out=v_o_w_out, g_final=v_g_final)
    order = ("e_g_in", "e_w_in", "e_g_q_a", "e_w_q_up", "e_g_kv_a", "e_w_kv_up", "e_sinks", "e_w_out", "o_g_in", "o_w_in",
             "o_b_f", "o_w_out", "g_final")
    half0 = _packed_weights(w, "layer0")
    full = _unpacked_weights(_run_plan(_gather8_plan(half0), "gather_weights_layer0")[0], half0, "layer0")
    layer1 = _Layer1Exchange(w)

    loss_part, dx, grads = _local_step(
        x[0], positions.reshape(-1, 1), loss_target[0], e_g_in, full["e_w_in"], e_g_q_a, full["e_w_q_up"], e_g_kv_a,
        full["e_w_kv_up"], e_sinks, o_b_f, g_final.reshape(1, D), layer1)

    small = jnp.concatenate([jnp.pad(loss_part.reshape(1), (0, LANES - 1))]
                            + [jnp.pad(grads[n].reshape(-1), (0, (-grads[n].size) % LANES)) for n in REPLICATED])
    rows = small.shape[0] // LANES
    small = jnp.pad(small.reshape(rows, LANES), ((0, (-rows) % 8), (0, 0)))
    sums0, (gathered_small,) = _GroupReduce("layer0").run(grads, _gather8_plan(small))
    gsum = {**layer1.sums, **sums0}
    ssum = _sum_leading(_fill_own_slot(gathered_small, small), "small_grad_sum").reshape(-1)
    loss = ssum[0]
    off = LANES
    for n in REPLICATED:
        cnt = w[n].size
        gsum[n] = ssum[off:off + cnt].reshape(w[n].shape)
        off += cnt + (-cnt) % LANES

    grad, delta, new_m, new_v = {}, {}, {}, {}
    for n in order:
        if n == "o_w_in":
            def tiles(a):
                return jnp.transpose(a, (2, 0, 1)).reshape(-1, LANES)

            def given(a):
                return jnp.transpose(a.reshape(-1, 8, LANES), (1, 2, 0)).reshape(w[n].shape)

            g_t = gsum[n].reshape(-1, LANES)
            outs = _adamw(tiles(w[n]), g_t, tiles(m[n]), tiles(v[n]), "adamw_" + n)
            grad[n], delta[n], new_m[n], new_v[n] = (given(a) for a in (g_t,) + outs)
        elif n in SHARDED:
            outs = _adamw(_as_handled(n, w[n]), gsum[n], _as_handled(n, m[n]), _as_handled(n, v[n]), "adamw_" + n)
            grad[n], delta[n], new_m[n], new_v[n] = (_as_given(n, a, w[n].shape) for a in (gsum[n],) + outs)
        else:
            grad[n] = gsum[n]
            delta[n], new_m[n], new_v[n] = _adamw(w[n], gsum[n], m[n], v[n], "adamw_" + n)
    return (loss, dx[None], *[grad[n] for n in order], *[delta[n] for n in order], *[new_m[n] for n in order],
            *[new_v[n] for n in order])
```

```python
import math

import numpy as np
import jax
import jax.numpy as jnp
from jax import lax
from jax.experimental import pallas as pl
from jax.experimental.pallas import tpu as pltpu

D = 1024
EPS = 1e-6
ROPE_THETA = 10000.0
N_MLA = 8
Q_RANK = 256
KV_RANK = 128
NOPE = 64
ROPE = 32
N_SWA = 8
WINDOW = 128
N_FOX = 16
HEAD = 64
LR, B1, B2, AEPS, WD, STEP = 0.001, 0.9, 0.999, 1e-08, 0.01, 10

LANES = 128
HALF = 64
VMEM_LIMIT = 56 * 1024 * 1024
MXU = jnp.bfloat16
TOK = 256
BIG_TOK = 512
WG_TOK = 2048
WG_ROWS = 1536
ATT = 256
FWD_CHUNK = 2
BWD_CHUNK = 2
SWA_GROUP = 8
NEG = float("-inf")

PACK_COLS = 1024
SUM_ROWS = 256
ADAM_TILE_BYTES = 2 << 20
MESH_ID = pl.DeviceIdType.MESH


def _pcall(body, *, name, vmem=VMEM_LIMIT, semantics=None, **kw):
    params = dict(vmem_limit_bytes=vmem)
    if semantics is not None:
        params["dimension_semantics"] = semantics
    return pl.pallas_call(body, name=name, compiler_params=pltpu.CompilerParams(**params), **kw)


def _mm(a, b):
    return jnp.dot(a.astype(MXU), b.astype(MXU), preferred_element_type=jnp.float32)


def _mm_nt(a, b):
    return lax.dot_general(a.astype(MXU), b.astype(MXU), (((1,), (1,)), ((), ())),
                           preferred_element_type=jnp.float32)


def _mm_tn(a, b):
    return lax.dot_general(a.astype(MXU), b.astype(MXU), (((0,), (0,)), ((), ())),
                           preferred_element_type=jnp.float32)


def _full(shape):
    n = len(shape)
    return pl.BlockSpec(shape, lambda *_: (0,) * n)


def _rows(tm, n):
    return pl.BlockSpec((tm, n), lambda i: (i, 0))


def _sds(shape, dtype):
    return jax.ShapeDtypeStruct(shape, dtype)


def _rms(x, g):
    r = lax.rsqrt(jnp.mean(x * x, axis=-1, keepdims=True) + EPS)
    return x * r * g


def _rms_bwd(x, g, dy):
    r = lax.rsqrt(jnp.mean(x * x, axis=-1, keepdims=True) + EPS)
    xh = x * r
    dxh = dy * g
    dx = r * (dxh - xh * jnp.mean(dxh * xh, axis=-1, keepdims=True))
    return dx, dy * xh


def _sigmoid(x):
    return 1.0 / (1.0 + jnp.exp(-x))


def _lane_masks():
    lane = lax.broadcasted_iota(jnp.int32, (1, LANES), 1)
    return lane < HALF


def _split_heads(a, lo):
    z = jnp.zeros_like(a)
    return [jnp.where(lo, a, z), jnp.where(lo, z, a)]


def _rope_consts():
    inv = np.zeros((8, LANES), np.float32)
    j = np.arange(ROPE // 2, dtype=np.float32)
    f = (1.0 / (ROPE_THETA ** (np.arange(0, ROPE, 2, dtype=np.float32) / ROPE))).astype(np.float32)
    inv[0, HALF:HALF + 16] = f
    inv[0, HALF + 16:HALF + 32] = f
    inv[1, HALF:HALF + 16] = -1.0
    inv[1, HALF + 16:HALF + 32] = 1.0
    del j
    return jnp.asarray(inv)


def _rope_tables(pos_f, consts):
    ang = pos_f * consts[0:1, :]
    sign = consts[1:2, :]
    c = jnp.where(sign != 0.0, jnp.cos(ang), 1.0)
    s = jnp.sin(ang) * sign
    return c, s


def _swap_halves(v, sign):
    lo = pltpu.roll(v, LANES - 16, axis=1)
    hi = pltpu.roll(v, 16, axis=1)
    return jnp.where(sign < 0.0, lo, jnp.where(sign > 0.0, hi, 0.0))


def _rope(x, c, s, sign):
    return x * c + _swap_halves(x, sign) * s


def _rope_t(dy, c, s, sign):
    return dy * c + _swap_halves(dy * s, sign)


def _layer0_in(x, pos, g_in, w_in, g_q, w_q, g_kv, w_kv):
    S = x.shape[0]
    T = math.gcd(BIG_TOK, S)
    consts = _rope_consts()

    def body(x_ref, pos_ref, c_ref, g_ref, w_ref, gq_ref, wq_ref, gkv_ref, wkv_ref,
             h_ref, cq_ref, ckv_ref, qm_ref, km_ref, vm_ref,
             qs_ref, kd_ref, vd_ref, gate_ref, cos_ref, sin_ref):
        h = _rms(x_ref[...], g_ref[...])
        h_ref[...] = h.astype(h_ref.dtype)
        z = _mm_nt(h, w_ref[...])
        cq = z[:, 0:256]
        ckv = z[:, 256:384]
        kpe = z[:, 384:512]
        cq_ref[...] = cq
        ckv_ref[...] = ckv
        qs_ref[...] = z[:, 512:1024].astype(qs_ref.dtype)
        kd_ref[...] = z[:, 1024:1536].astype(kd_ref.dtype)
        vd_ref[...] = z[:, 1536:2048].astype(vd_ref.dtype)
        gate_ref[...] = z[:, 2048:3072]
        cqn = _rms(cq, gq_ref[...])
        ckvn = _rms(ckv, gkv_ref[...])
        q = _mm_nt(cqn, wq_ref[...])
        kv = _mm(ckvn, wkv_ref[...])
        vm_ref[...] = kv[:, 1024:1536].astype(vm_ref.dtype)
        consts_v = c_ref[...]
        sign = consts_v[1:2, :]
        c, s = _rope_tables(pos_ref[...].astype(jnp.float32), consts_v)
        cos_ref[...] = c
        sin_ref[...] = s
        kpe_r = _rope(kpe, c, s, sign)
        for hd in range(N_MLA):
            sl = slice(LANES * hd, LANES * (hd + 1))
            qm_ref[:, sl] = _rope(q[:, sl], c, s, sign).astype(qm_ref.dtype)
            km_ref[:, sl] = (kv[:, sl] + kpe_r).astype(km_ref.dtype)

    outs = [
        ((S, D), MXU), ((S, 256), jnp.float32), ((S, 128), jnp.float32),
        ((S, 1024), MXU), ((S, 1024), MXU), ((S, 512), MXU), ((S, 512), MXU), ((S, 512), MXU), ((S, 512), MXU),
        ((S, 1024), jnp.float32), ((S, 128), jnp.float32), ((S, 128), jnp.float32),
    ]
    return _pcall(
        body, name="layer0_in", grid=(S // T,), semantics=("arbitrary",),
        in_specs=[_rows(T, D), _rows(T, 1), _full((8, LANES)), _full((1, D)), _full(w_in.shape), _full((1, 256)),
                  _full(w_q.shape), _full((1, 128)), _full(w_kv.shape)],
        out_specs=[_rows(T, s[1]) for s, _ in outs],
        out_shape=[_sds(s, d) for s, d in outs],
    )(x, pos, consts, g_in, w_in, g_q, w_q, g_kv, w_kv)


AUG = (HALF, 0)
ONE = (HALF + 8, 8)


def _data_lanes(idx, h):
    return (idx < HALF) if h == 0 else (idx >= HALF)


def _three_terms(x):
    hi = x.astype(MXU).astype(jnp.float32)
    mid = (x - hi).astype(MXU).astype(jnp.float32)
    lo = (x - hi - mid).astype(MXU).astype(jnp.float32)
    return hi, mid, lo


def _q_aug(qblk, lc, h, scale, lane):
    a = AUG[h]
    hi, mid, lo = _three_terms(lc)
    ones = ((lane >= a + 3) & (lane <= a + 5)).astype(jnp.float32)
    aug = jnp.where(lane == a, hi, jnp.where(lane == a + 1, mid, jnp.where(lane == a + 2, lo, ones)))
    return jnp.where(_data_lanes(lane, h), qblk * jnp.asarray(scale, qblk.dtype), aug.astype(qblk.dtype))


def _k_aug(kblk, lc, h, lane):
    a = AUG[h]
    hi, mid, lo = _three_terms(-lc)
    ones = ((lane >= a) & (lane <= a + 2)).astype(jnp.float32)
    aug = jnp.where(lane == a + 3, hi, jnp.where(lane == a + 4, mid, jnp.where(lane == a + 5, lo, ones)))
    return jnp.where(_data_lanes(lane, h), kblk, aug.astype(kblk.dtype))


def _lc_col(lc_ref, r0, rows, h):
    head = lax.broadcasted_iota(jnp.int32, (1, lc_ref.shape[1]), 1)
    return jnp.sum(jnp.where(head == 2 * pl.program_id(0) + h, lc_ref[pl.ds(r0, rows), :], 0.0), axis=1, keepdims=True)


def _attn_fwd_t(q, k, v, scale, *, split, name, lcc=None, plan=None):
    S = q.shape[0]
    npair = v.shape[1] // LANES
    W = 2 * LANES if split else LANES
    T = ATT
    CH = FWD_CHUNK * T
    assert S % CH == 0
    nq = S // T

    def body(*refs):
        if split:
            q_ref, k_ref, v_ref, o_ref, lse_ref, vt, acc, m_sc = refs
        else:
            q_ref, k_ref, v_ref, lcc_ref, o_ref, lse_ref, kaug, vt, acc, m_sc = refs
        lane = lax.broadcasted_iota(jnp.int32, (1, LANES), 1)
        sub = lax.broadcasted_iota(jnp.int32, (LANES, 1), 0)
        key_minus_qry = lax.broadcasted_iota(jnp.int32, (CH, T), 0) - lax.broadcasted_iota(jnp.int32, (CH, T), 1)

        def prep(i, c):
            r0 = pl.multiple_of(i * T, T)
            vblk = v_ref[pl.ds(r0, T), :].astype(jnp.float32)
            for h in (0, 1):
                vh = jnp.where(_data_lanes(lane, h), vblk, (lane == ONE[h]).astype(jnp.float32))
                vt[h, :, pl.ds(r0, T)] = vh.T.astype(vt.dtype)
                if not split:
                    kaug[h, pl.ds(r0, T), :] = _k_aug(k_ref[pl.ds(r0, T), :], _lc_col(lcc_ref, r0, T, h), h, lane)
            return c

        lax.fori_loop(0, nq, prep, 0)

        def queries(qi):
            q0 = pl.multiple_of(qi * T, T)
            qblk = q_ref[pl.ds(q0, T), :]
            if split:
                return (qblk[:, :LANES], qblk[:, LANES:])
            return tuple(_q_aug(qblk, _lc_col(lcc_ref, q0, T, h), h, scale, lane) for h in (0, 1))

        def scores(qs, c):
            k0 = pl.multiple_of(c * CH, CH)
            out = []
            for h in (0, 1):
                if split:
                    out.append(_mm_nt(k_ref[pl.ds(k0, CH), LANES * h:LANES * (h + 1)], qs[h]) * scale)
                else:
                    out.append(_mm_nt(kaug[h, pl.ds(k0, CH), :], qs[h]))
            return tuple(out)

        def q_block(qi, carry):
            qs, first_scores = carry[:2], carry[2:]
            q0 = pl.multiple_of(qi * T, T)
            acc[...] = jnp.zeros_like(acc)
            m_sc[...] = jnp.full(m_sc.shape, NEG, jnp.float32)

            def absorb(c, sts, masked):
                k0 = pl.multiple_of(c * CH, CH)
                for h in (0, 1):
                    st = sts[h]
                    if masked:
                        st = jnp.where(key_minus_qry <= q0 - k0, st, NEG)
                    m_old = m_sc[h:h + 1, :]
                    m_new = jnp.maximum(m_old, jnp.max(st, axis=0, keepdims=True))
                    alpha = jnp.exp(m_old - m_new)
                    pt = jnp.exp(st - m_new)
                    acc[h] = alpha * acc[h] + _mm(vt[h, :, pl.ds(k0, CH)], pt)
                    m_sc[h:h + 1, :] = m_new

            last = qi // FWD_CHUNK

            def pipelined(c, sts):
                nxt = scores(qs, c + 1)
                absorb(c, sts, False)
                return nxt

            sts = lax.fori_loop(0, last, pipelined, first_scores)
            qs_next = queries(jnp.minimum(qi + 1, nq - 1))
            nxt = qs_next + scores(qs_next, 0)
            absorb(last, sts, True)
            ot = None
            for h in (0, 1):
                a = acc[h]
                l = a[ONE[h]:ONE[h] + 1, :]
                oh = jnp.where(_data_lanes(sub, h), a * (1.0 / l), 0.0)
                ot = oh if ot is None else ot + oh
                lse_ref[0, h:h + 1, pl.ds(q0, T)] = m_sc[h:h + 1, :] + jnp.log(l)
            o_ref[pl.ds(q0, T), :] = ot.T
            return nxt

        qs0 = queries(0)
        lax.fori_loop(0, nq, q_block, qs0 + scores(qs0, 0))

    wide = pl.BlockSpec((S, W), lambda j: (0, j))
    slab = pl.BlockSpec((S, LANES), lambda j: (0, j))
    rows = pl.BlockSpec((1, 2, S), lambda j: (j, 0, 0))
    in_specs = [wide, wide, slab]
    args = [q, k, v]
    scratch = []
    if not split:
        in_specs.append(_full(lcc.shape))
        args.append(lcc)
        scratch.append(pltpu.VMEM((2, S, LANES), MXU))
    scratch += [pltpu.VMEM((2, LANES, S), MXU), pltpu.VMEM((2, LANES, T), jnp.float32), pltpu.VMEM((8, T), jnp.float32)]
    (o, lse), rode = _pcall_riding(
        body, plan, args, name=name, grid=(npair,), in_specs=in_specs, out_specs=[slab, rows],
        out_shape=[_sds((S, npair * LANES), jnp.float32), _sds((npair, 2, S), jnp.float32)], scratch_shapes=scratch)
    return o, lse, rode


def _attn_bwd_t(q, k, v, do, o, lse, scale, *, split, name, lcc=None, plan=None):
    S = q.shape[0]
    npair = v.shape[1] // LANES
    W = 2 * LANES if split else LANES
    T = ATT
    CH = BWD_CHUNK * T
    assert S % CH == 0
    nq = S // T

    def body(*refs):
        if split:
            (q_ref, k_ref, v_ref, do_ref, o_ref, lse_ref, dq_ref, dk_ref, dv_ref, dqt, delta, dk_acc, dv_acc) = refs
        else:
            (q_ref, k_ref, v_ref, do_ref, o_ref, lse_ref, lcc_ref, dq_ref, dk_ref, dv_ref, dlc_ref,
             dqt, delta, dk_acc, dv_acc, qaug, csum) = refs
        lane = lax.broadcasted_iota(jnp.int32, (1, LANES), 1)
        sub = lax.broadcasted_iota(jnp.int32, (LANES, 1), 0)
        key_minus_qry = lax.broadcasted_iota(jnp.int32, (T, CH), 0) - lax.broadcasted_iota(jnp.int32, (T, CH), 1)

        def prep(i, c):
            r0 = pl.multiple_of(i * T, T)
            prod_t = (do_ref[pl.ds(r0, T), :].astype(jnp.float32) * o_ref[pl.ds(r0, T), :]).T
            for h in (0, 1):
                delta[h:h + 1, pl.ds(r0, T)] = jnp.sum(jnp.where(_data_lanes(sub, h), prod_t, 0.0), axis=0, keepdims=True)
                dqt[h, :, pl.ds(r0, T)] = jnp.zeros((LANES, T), jnp.float32)
                if not split:
                    qaug[h, pl.ds(r0, T), :] = _q_aug(q_ref[pl.ds(r0, T), :], _lc_col(lcc_ref, r0, T, h), h, scale, lane)
            return c

        lax.fori_loop(0, nq, prep, 0)

        def keys(ki):
            k0 = pl.multiple_of(ki * T, T)
            kblk = k_ref[pl.ds(k0, T), :]
            if split:
                return (kblk[:, :LANES], kblk[:, LANES:])
            return tuple(_k_aug(kblk, _lc_col(lcc_ref, k0, T, h), h, lane) for h in (0, 1))

        def q_of(c, h):
            q0 = pl.multiple_of(c * CH, CH)
            if split:
                return q_ref[pl.ds(q0, CH), LANES * h:LANES * (h + 1)]
            return qaug[h, pl.ds(q0, CH), :]

        def scores(khs, c):
            out = []
            for h in (0, 1):
                st = _mm_nt(khs[h], q_of(c, h))
                out.append(st * scale if split else st)
            return tuple(out)

        def k_block(ki, carry):
            khs, first_scores = carry[:2], carry[2:]
            k0 = pl.multiple_of(ki * T, T)
            khts = [kh.astype(jnp.float32).T.astype(kh.dtype) for kh in khs]
            vhs = _split_heads(v_ref[pl.ds(k0, T), :], lane < HALF)
            dk_acc[...] = jnp.zeros_like(dk_acc)
            dv_acc[...] = jnp.zeros_like(dv_acc)

            def absorb(c, vals):
                q0 = pl.multiple_of(c * CH, CH)
                dos = _split_heads(do_ref[pl.ds(q0, CH), :], lane < HALF)
                visible = key_minus_qry <= q0 - k0
                for h in (0, 1):
                    dpt = _mm_nt(vhs[h], dos[h])
                    st = jnp.where(visible, vals[h], NEG)
                    pt = jnp.exp(st - lse_ref[0, h:h + 1, pl.ds(q0, CH)])
                    dv_acc[...] += _mm(pt, dos[h])
                    dst = pt * (dpt - delta[h:h + 1, pl.ds(q0, CH)])
                    dk_acc[h] += _mm(dst, q_of(c, h))
                    dqt[h, :, pl.ds(q0, CH)] += _mm(khts[h], dst)

            first = ki // BWD_CHUNK

            def pipelined(c, vals):
                nxt = scores(khs, c + 1)
                absorb(c, vals)
                return nxt

            vals = lax.fori_loop(first, S // CH - 1, pipelined, first_scores)
            kn = jnp.minimum(ki + 1, nq - 1)
            khs_next = keys(kn)
            nxt = khs_next + scores(khs_next, kn // BWD_CHUNK)
            absorb(S // CH - 1, vals)
            if split:
                dk_ref[pl.ds(k0, T), :LANES] = (dk_acc[0] * scale).astype(dk_ref.dtype)
                dk_ref[pl.ds(k0, T), LANES:] = (dk_acc[1] * scale).astype(dk_ref.dtype)
            else:
                dk_ref[pl.ds(k0, T), :] = jnp.where(lane < HALF, dk_acc[0], dk_acc[1]).astype(dk_ref.dtype)
                for h in (0, 1):
                    csum[h:h + 1, pl.ds(k0, T)] = dk_acc[h].T[AUG[h] + 3:AUG[h] + 4, :]
            dv_ref[pl.ds(k0, T), :] = dv_acc[...].astype(dv_ref.dtype)
            return nxt

        khs0 = keys(0)
        lax.fori_loop(0, nq, k_block, khs0 + scores(khs0, 0))

        def finish(i, c):
            r0 = pl.multiple_of(i * T, T)
            if split:
                for h in (0, 1):
                    dq_ref[pl.ds(r0, T), LANES * h:LANES * (h + 1)] = (dqt[h, :, pl.ds(r0, T)].T * scale).astype(dq_ref.dtype)
            else:
                d = jnp.where(sub < HALF, dqt[0, :, pl.ds(r0, T)], dqt[1, :, pl.ds(r0, T)])
                dq_ref[pl.ds(r0, T), :] = (d.T * scale).astype(dq_ref.dtype)
                for h in (0, 1):
                    dlc_ref[0, h:h + 1, pl.ds(r0, T)] = dqt[h, AUG[h]:AUG[h] + 1, pl.ds(r0, T)] - csum[h:h + 1, pl.ds(r0, T)]
            return c

        lax.fori_loop(0, nq, finish, 0)

    wide = pl.BlockSpec((S, W), lambda j: (0, j))
    slab = pl.BlockSpec((S, LANES), lambda j: (0, j))
    rows = pl.BlockSpec((1, 2, S), lambda j: (j, 0, 0))
    in_specs = [wide, wide, slab, slab, slab, rows]
    args = [q, k, v, do, o, lse]
    out_specs = [wide, wide, slab]
    out_shape = [_sds(q.shape, jnp.float32 if split else do.dtype), _sds(k.shape, jnp.float32 if split else do.dtype),
                 _sds(v.shape, do.dtype)]
    scratch = [pltpu.VMEM((2, LANES, S), jnp.float32), pltpu.VMEM((8, S), jnp.float32),
               pltpu.VMEM((2, T, LANES), jnp.float32), pltpu.VMEM((T, LANES), jnp.float32)]
    if not split:
        in_specs.append(_full(lcc.shape))
        args.append(lcc)
        out_specs.append(rows)
        out_shape.append(_sds((npair, 2, S), jnp.float32))
        scratch += [pltpu.VMEM((2, S, LANES), MXU), pltpu.VMEM((8, S), jnp.float32)]
    outs, rode = _pcall_riding(body, plan, args, name=name, grid=(npair,), in_specs=in_specs, out_specs=out_specs,
                               out_shape=out_shape, scratch_shapes=scratch)
    return (*outs, rode)


def _swa_bias(slope, shift):
    a = lax.broadcasted_iota(jnp.int32, (WINDOW, 2 * WINDOW), 0)
    c = lax.broadcasted_iota(jnp.int32, (WINDOW, 2 * WINDOW), 1)
    dist = a - c + shift
    return jnp.where((dist >= 0) & (dist < WINDOW), -slope * dist.astype(jnp.float32), NEG)


def _swa_scores(qh, kblk, bias):
    return _mm_nt(qh, kblk) * (HEAD ** -0.5) + bias


def _swa_stack(blk, lo):
    return jnp.concatenate(_split_heads(blk[:, :LANES], lo) + _split_heads(blk[:, LANES:], lo), axis=0)


def _swa_unstack(x, lo):
    r = x.shape[0] // 4
    return jnp.concatenate([jnp.where(lo, x[0:r], x[r:2 * r]), jnp.where(lo, x[2 * r:3 * r], x[3 * r:])], axis=1)


def _swa_per_head(ref, j, rows):
    quarter = lax.broadcasted_iota(jnp.int32, (4 * rows, 1), 0) // rows
    return jnp.where(quarter == 0, ref[4 * j], jnp.where(quarter == 1, ref[4 * j + 1],
                                                         jnp.where(quarter == 2, ref[4 * j + 2], ref[4 * j + 3])))


def _swa_fwd(q, kd, vd, sinks, slopes):
    S = q.shape[0]
    nkv = q.shape[1] // (2 * LANES)
    nb = S // WINDOW
    group = math.gcd(SWA_GROUP, nb)

    def body(sink_ref, slope_ref, q_ref, k_ref, v_ref, o_ref, lse_ref):
        j = pl.program_id(0)
        lo = _lane_masks()
        sink = _swa_per_head(sink_ref, j, WINDOW)
        biases = [jnp.concatenate([_swa_bias(slope_ref[4 * j + h], shift) for h in range(4)], axis=0)
                  for shift in (0, WINDOW)]

        def q_block(qi, c):
            q0 = pl.multiple_of(qi * WINDOW, WINDOW)
            k0 = pl.multiple_of(jnp.maximum(qi - 1, 0) * WINDOW, WINDOW)
            s = _swa_scores(_swa_stack(q_ref[pl.ds(q0, WINDOW), :], lo), k_ref[pl.ds(k0, 2 * WINDOW), :],
                            jnp.where(qi == 0, *biases))
            m = jnp.maximum(jnp.max(s, axis=1, keepdims=True), sink)
            p = jnp.exp(s - m)
            den = jnp.sum(p, axis=1, keepdims=True) + jnp.exp(sink - m)
            o_ref[pl.ds(q0, WINDOW), :] = _swa_unstack(_mm(p / den, v_ref[pl.ds(k0, 2 * WINDOW), :]), lo)
            lse = m + jnp.log(den)
            for h in range(4):
                lse_ref[h, pl.ds(q0, WINDOW), :] = lse[h * WINDOW:(h + 1) * WINDOW]
            return c

        def q_group(gi, c):
            for g in range(group):
                q_block(gi * group + g, c)
            return c

        lax.fori_loop(0, nb // group, q_group, 0)

    smem = pl.BlockSpec(memory_space=pltpu.SMEM)
    two = pl.BlockSpec((S, 2 * LANES), lambda j: (0, j))
    kv = pl.BlockSpec((S, LANES), lambda j: (0, 2 * j))
    return _pcall(
        body, name="swa_fwd", grid=(nkv,), semantics=("arbitrary",),
        in_specs=[smem, smem, two, kv, kv],
        out_specs=[two, pl.BlockSpec((4, S, 1), lambda j: (j, 0, 0))],
        out_shape=[_sds(q.shape, jnp.float32), _sds((4 * nkv, S, 1), jnp.float32)],
    )(sinks, slopes, q, kd, vd)


def _swa_bwd(q, kd, vd, do, o, lse, sinks, slopes, plan=None):
    S = q.shape[0]
    nkv = q.shape[1] // (2 * LANES)
    nb = S // WINDOW
    group = math.gcd(SWA_GROUP, nb)

    def body(sink_ref, slope_ref, q_ref, k_ref, v_ref, do_ref, o_ref, lse_ref,
             dq_ref, dk_ref, dv_ref, dsink_ref, dk_acc, dv_acc):
        j = pl.program_id(0)
        lo = _lane_masks()
        dk_acc[...] = jnp.zeros_like(dk_acc)
        dv_acc[...] = jnp.zeros_like(dv_acc)
        sink = _swa_per_head(sink_ref, j, WINDOW)
        biases = [jnp.concatenate([_swa_bias(slope_ref[4 * j + h], shift) for h in range(4)], axis=0)
                  for shift in (0, WINDOW)]

        def q_block(qi, carry):
            q0 = pl.multiple_of(qi * WINDOW, WINDOW)
            k0 = pl.multiple_of(jnp.maximum(qi - 1, 0) * WINDOW, WINDOW)
            q4 = _swa_stack(q_ref[pl.ds(q0, WINDOW), :], lo)
            do4 = _swa_stack(do_ref[pl.ds(q0, WINDOW), :], lo)
            oblk = o_ref[pl.ds(q0, WINDOW), :]
            o4 = jnp.concatenate([oblk[:, :LANES], oblk[:, :LANES], oblk[:, LANES:], oblk[:, LANES:]], axis=0)
            kblk = k_ref[pl.ds(k0, 2 * WINDOW), :]
            vblk = v_ref[pl.ds(k0, 2 * WINDOW), :]
            lse = jnp.concatenate([lse_ref[h, pl.ds(q0, WINDOW), :] for h in range(4)], axis=0)
            p = jnp.exp(_swa_scores(q4, kblk, jnp.where(qi == 0, *biases)) - lse)
            delta = jnp.sum(do4.astype(jnp.float32) * o4, axis=1, keepdims=True)
            dv_acc[pl.ds(k0, 2 * WINDOW), :] += _mm_tn(p, do4)
            ds = p * (_mm_nt(do4, vblk) - delta)
            dq_ref[pl.ds(q0, WINDOW), :] = _swa_unstack(_mm(ds, kblk) * (HEAD ** -0.5), lo).astype(dq_ref.dtype)
            dk_acc[pl.ds(k0, 2 * WINDOW), :] += _mm_tn(ds, q4) * (HEAD ** -0.5)
            dsk = -jnp.exp(sink - lse) * delta
            return tuple(carry[h] + jnp.sum(dsk[h * WINDOW:(h + 1) * WINDOW], axis=0, keepdims=True)
                         for h in range(4))

        def q_group(gi, carry):
            for g in range(group):
                carry = q_block(gi * group + g, carry)
            return carry

        zero = jnp.zeros((1, 1), jnp.float32)
        dsinks = lax.fori_loop(0, nb // group, q_group, (zero,) * 4)
        dk_ref[:, :LANES] = dk_acc[...].astype(dk_ref.dtype)
        dk_ref[:, LANES:] = jnp.zeros((S, LANES), dk_ref.dtype)
        dv_ref[:, :LANES] = dv_acc[...].astype(dv_ref.dtype)
        dv_ref[:, LANES:] = jnp.zeros((S, LANES), dv_ref.dtype)
        r = lax.broadcasted_iota(jnp.int32, (8, LANES), 0)
        dsink_ref[0] = jnp.where(r == 0, dsinks[0], jnp.where(r == 1, dsinks[1], jnp.where(r == 2, dsinks[2],
                                 jnp.where(r == 3, dsinks[3], 0.0))))

    smem = pl.BlockSpec(memory_space=pltpu.SMEM)
    two = pl.BlockSpec((S, 2 * LANES), lambda j: (0, j))
    kv = pl.BlockSpec((S, LANES), lambda j: (0, 2 * j))
    outs, rode = _pcall_riding(
        body, plan, [sinks, slopes, q, kd, vd, do, o, lse], name="swa_bwd", grid=(nkv,),
        in_specs=[smem, smem, two, kv, kv, two, two, pl.BlockSpec((4, S, 1), lambda j: (j, 0, 0))],
        out_specs=[two, two, two, pl.BlockSpec((1, 8, LANES), lambda j: (j, 0, 0))],
        out_shape=[_sds(q.shape, do.dtype), _sds(kd.shape, do.dtype), _sds(vd.shape, do.dtype),
                   _sds((nkv, 8, LANES), jnp.float32)],
        scratch_shapes=[pltpu.VMEM((S, LANES), jnp.float32), pltpu.VMEM((S, LANES), jnp.float32)])
    return (*outs, rode)


def _log_steps(S):
    k, out = 1, []
    while k < S:
        out.append(k)
        k *= 2
    return out


def _forget_fwd(f_row, b_col):
    S = f_row.shape[1]

    def body(f_ref, b_ref, lc_ref):
        x = f_ref[...] + b_ref[...]
        lc = jnp.minimum(x, 0.0) - jnp.log(1.0 + jnp.exp(-jnp.abs(x)))
        idx = lax.broadcasted_iota(jnp.int32, lc.shape, 1)
        for k in _log_steps(S):
            lc = lc + jnp.where(idx >= k, pltpu.roll(lc, k, axis=1), 0.0)
        lc_ref[...] = lc

    return _pcall(body, name="forget_fwd", out_shape=_sds(f_row.shape, jnp.float32))(f_row, b_col)


def _forget_bwd(dlc_row, f_row, b_col):
    S = f_row.shape[1]

    def body(d_ref, f_ref, b_ref, df_ref, db_ref):
        g = d_ref[...]
        idx = lax.broadcasted_iota(jnp.int32, g.shape, 1)
        for k in _log_steps(S):
            g = g + jnp.where(idx < S - k, pltpu.roll(g, S - k, axis=1), 0.0)
        x = f_ref[...] + b_ref[...]
        df = g * _sigmoid(-x)
        df_ref[...] = df
        db_ref[...] = jnp.sum(df, axis=1, keepdims=True)

    return _pcall(body, name="forget_bwd",
                  out_shape=[_sds(f_row.shape, jnp.float32), _sds((f_row.shape[0], 1), jnp.float32)])(dlc_row, f_row, b_col)


def _layer0_out_layer1_in(x, o_m, o_s, gate, w_out, g1, w_in1):
    S = x.shape[0]
    T = math.gcd(BIG_TOK, S)

    def body(x_ref, om_ref, os_ref, gate_ref, wo_ref, g_ref, w_ref,
             x1_ref, h_ref, q_ref, k_ref, v_ref, g1_ref, f_ref):
        gt = gate_ref[...]
        sg = gt * _sigmoid(gt)
        um = om_ref[...] * sg[:, :512]
        us = os_ref[...] * sg[:, 512:]
        x1 = x_ref[...] + _mm(um, wo_ref[0:512, :]) + _mm(us, wo_ref[512:1024, :])
        x1_ref[...] = x1
        h = _rms(x1, g_ref[...])
        h_ref[...] = h.astype(h_ref.dtype)
        z = _mm_nt(h, w_ref[...])
        q_ref[...] = z[:, 0:1024].astype(q_ref.dtype)
        k_ref[...] = z[:, 1024:2048].astype(k_ref.dtype)
        v_ref[...] = z[:, 2048:3072].astype(v_ref.dtype)
        g1_ref[...] = z[:, 3072:4096]
        f_ref[...] = z[:, 4096:4224]

    outs = [((S, D), jnp.float32), ((S, D), MXU), ((S, D), MXU), ((S, D), MXU), ((S, D), MXU),
            ((S, D), jnp.float32), ((S, LANES), jnp.float32)]
    return _pcall(
        body, name="layer0_out_layer1_in", grid=(S // T,), semantics=("arbitrary",),
        in_specs=[_rows(T, D), _rows(T, 512), _rows(T, 512), _rows(T, D), _full((D, D)), _full((1, D)),
                  _full(w_in1.shape)],
        out_specs=[_rows(T, s[1]) for s, _ in outs],
        out_shape=[_sds(s, d) for s, d in outs],
    )(x, o_m, o_s, gate, w_out, g1, w_in1)


def _head(x1, o1, gate1, w_out1, g_f, target):
    S = x1.shape[0]
    T = math.gcd(BIG_TOK, S)

    def body(x1_ref, o_ref, gate_ref, wo_ref, g_ref, t_ref,
             loss_ref, dgf_ref, dwo_ref, dx2_ref, do_ref, dgate_ref):
        i = pl.program_id(0)
        gt = gate_ref[...]
        sig = _sigmoid(gt)
        sg = gt * sig
        o = o_ref[...]
        u = o * sg
        x2 = x1_ref[...] + _mm(u, wo_ref[...])
        g = g_ref[...]
        y = _rms(x2, g)
        err = y - t_ref[...]
        part = 0.5 * jnp.sum(jnp.mean(err * err, axis=-1, keepdims=True), axis=0, keepdims=True)
        dy = err * (1.0 / D)
        dx2, dg_rows = _rms_bwd(x2, g, dy)
        dx2_ref[...] = dx2
        du = _mm_nt(dx2, wo_ref[...])
        do_ref[...] = (du * sg).astype(do_ref.dtype)
        dgate_ref[...] = (du * o * (sig * (1.0 + gt * (1.0 - sig)))).astype(dgate_ref.dtype)

        @pl.when(i == 0)
        def _():
            loss_ref[...] = jnp.zeros_like(loss_ref)
            dgf_ref[...] = jnp.zeros_like(dgf_ref)
            dwo_ref[...] = jnp.zeros_like(dwo_ref)

        loss_ref[...] += jnp.broadcast_to(part, loss_ref.shape)
        dgf_ref[...] += jnp.sum(dg_rows, axis=0, keepdims=True)
        dwo_ref[...] += _mm_tn(u, dx2)

    outs = [((S, D), jnp.float32), ((S, D), MXU), ((S, D), MXU)]
    return _pcall(
        body, name="head", grid=(S // T,), semantics=("arbitrary",),
        in_specs=[_rows(T, D), _rows(T, D), _rows(T, D), _full((D, D)), _full((1, D)), _rows(T, D)],
        out_specs=[_full((8, LANES)), _full((1, D)), _full((D, D))] + [_rows(T, D) for _ in outs],
        out_shape=[_sds((8, LANES), jnp.float32), _sds((1, D), jnp.float32), _sds((D, D), jnp.float32)]
        + [_sds(s, d) for s, d in outs],
    )(x1, o1, gate1, w_out1, g_f, target)


def _layer1_in_bwd(dq, dk, dv, dgate1, df, x1, dx2, g1, w_in1, gate0, o_m, o_s, w_out0):
    S = x1.shape[0]

    def body(dq_ref, dk_ref, dv_ref, dg1_ref, df_ref, x1_ref, dx2_ref, g_ref, w_ref, gate_ref, om_ref, os_ref,
             wo_ref, dz_ref, dx1_ref, dgn_ref, dwo_ref, dom_ref, dos_ref, dgate_ref):
        i = pl.program_id(0)
        dz_ref[:, 0:1024] = dq_ref[...]
        dz_ref[:, 1024:2048] = dk_ref[...]
        dz_ref[:, 2048:3072] = dv_ref[...]
        dz_ref[:, 3072:4096] = dg1_ref[...]
        dz_ref[:, 4096:4224] = df_ref[...]
        dh = _mm(dz_ref[...], w_ref[...])
        g = g_ref[...]
        dxn, dg_rows = _rms_bwd(x1_ref[...], g, dh)
        dx1 = dx2_ref[...] + dxn
        dx1_ref[...] = dx1
        du = _mm_nt(dx1, wo_ref[...])
        gt = gate_ref[...]
        sig = _sigmoid(gt)
        sg = gt * sig
        dsg = sig * (1.0 + gt * (1.0 - sig))
        dom_ref[...] = (du[:, :512] * sg[:, :512]).astype(dom_ref.dtype)
        dos_ref[...] = (du[:, 512:] * sg[:, 512:]).astype(dos_ref.dtype)
        dgate_ref[:, :512] = (du[:, :512] * om_ref[...] * dsg[:, :512]).astype(dgate_ref.dtype)
        dgate_ref[:, 512:] = (du[:, 512:] * os_ref[...] * dsg[:, 512:]).astype(dgate_ref.dtype)

        @pl.when(i == 0)
        def _():
            dgn_ref[...] = jnp.zeros_like(dgn_ref)
            dwo_ref[...] = jnp.zeros_like(dwo_ref)

        dgn_ref[...] += jnp.sum(dg_rows, axis=0, keepdims=True)
        dwo_ref[0:512, :] += _mm_tn(om_ref[...] * sg[:, :512], dx1)
        dwo_ref[512:1024, :] += _mm_tn(os_ref[...] * sg[:, 512:], dx1)

    return _pcall(
        body, name="layer1_in_bwd", grid=(S // TOK,), semantics=("arbitrary",),
        in_specs=[_rows(TOK, D), _rows(TOK, D), _rows(TOK, D), _rows(TOK, D), _rows(TOK, LANES), _rows(TOK, D),
                  _rows(TOK, D), _full((1, D)), _full(w_in1.shape), _rows(TOK, D), _rows(TOK, 512), _rows(TOK, 512),
                  _full((D, D))],
        out_specs=[_rows(TOK, 4224), _rows(TOK, D), _full((1, D)), _full((D, D)), _rows(TOK, 512), _rows(TOK, 512),
                   _rows(TOK, D)],
        out_shape=[_sds((S, 4224), MXU), _sds((S, D), jnp.float32), _sds((1, D), jnp.float32), _sds((D, D), jnp.float32),
                   _sds((S, 512), MXU), _sds((S, 512), MXU), _sds((S, D), MXU)],
    )(dq, dk, dv, dgate1, df, x1, dx2, g1, w_in1, gate0, o_m, o_s, w_out0)


def _layer0_in_bwd(dqm, dkm, dvm, dqs, dkd, dvd, dgate0, cos, sin, cq, ckv, x, dx1, g_in, w_in, g_q, w_q, g_kv, w_kv):
    S = x.shape[0]
    T = math.gcd(BIG_TOK, S)
    consts = _rope_consts()

    def body(dqm_ref, dkm_ref, dvm_ref, dqs_ref, dkd_ref, dvd_ref, dgate_ref, cos_ref, sin_ref, c_ref, cq_ref, ckv_ref,
             x_ref, dx1_ref, g_ref, w_ref, gq_ref, wq_ref, gkv_ref, wkv_ref,
             dx_ref, dz_ref, dgin_ref, dgq_ref, dgkv_ref, dwq_ref, dwkv_ref, dqu_ref, dkvu_ref):
        i = pl.program_id(0)
        lo = _lane_masks()
        sign = c_ref[...][1:2, :]
        c = cos_ref[...]
        s = sin_ref[...]
        dkpe = None
        for hd in range(N_MLA):
            sl = slice(LANES * hd, LANES * (hd + 1))
            dqu_ref[:, sl] = _rope_t(dqm_ref[:, sl], c, s, sign).astype(dqu_ref.dtype)
            dkh = dkm_ref[:, sl]
            dkvu_ref[:, sl] = jnp.where(lo, dkh, 0.0).astype(dkvu_ref.dtype)
            dkpe = dkh if dkpe is None else dkpe + dkh
        dkvu_ref[:, 1024:1536] = dvm_ref[...]
        dkpe = _rope_t(jnp.where(lo, 0.0, dkpe), c, s, sign)
        dcqn = _mm(dqu_ref[...], wq_ref[...])
        dckvn = _mm_nt(dkvu_ref[...], wkv_ref[...])
        gq = gq_ref[...]
        gkv = gkv_ref[...]
        dcq, dgq_rows = _rms_bwd(cq_ref[...], gq, dcqn)
        dckv, dgkv_rows = _rms_bwd(ckv_ref[...], gkv, dckvn)
        dz_ref[:, 0:256] = dcq.astype(dz_ref.dtype)
        dz_ref[:, 256:384] = dckv.astype(dz_ref.dtype)
        dz_ref[:, 384:512] = dkpe.astype(dz_ref.dtype)
        dz_ref[:, 512:1024] = dqs_ref[...]
        dz_ref[:, 1024:1536] = dkd_ref[...]
        dz_ref[:, 1536:2048] = dvd_ref[...]
        dz_ref[:, 2048:3072] = dgate_ref[...]
        dh = _mm(dz_ref[...], w_ref[...])
        g = g_ref[...]
        dxn, dg_rows = _rms_bwd(x_ref[...], g, dh)
        dx_ref[...] = dx1_ref[...] + dxn

        @pl.when(i == 0)
        def _():
            dgin_ref[...] = jnp.zeros_like(dgin_ref)
            dgq_ref[...] = jnp.zeros_like(dgq_ref)
            dgkv_ref[...] = jnp.zeros_like(dgkv_ref)
            dwq_ref[...] = jnp.zeros_like(dwq_ref)
            dwkv_ref[...] = jnp.zeros_like(dwkv_ref)

        dgin_ref[...] += jnp.sum(dg_rows, axis=0, keepdims=True)
        dgq_ref[...] += jnp.sum(dgq_rows, axis=0, keepdims=True)
        dgkv_ref[...] += jnp.sum(dgkv_rows, axis=0, keepdims=True)
        dwq_ref[...] += _mm_tn(dqu_ref[...], _rms(cq_ref[...], gq))
        dwkv_ref[...] += _mm_tn(_rms(ckv_ref[...], gkv), dkvu_ref[...])

    return _pcall(
        body, name="layer0_in_bwd", grid=(S // T,), semantics=("arbitrary",),
        in_specs=[_rows(T, 1024), _rows(T, 1024), _rows(T, 512), _rows(T, 512), _rows(T, 512), _rows(T, 512),
                  _rows(T, D), _rows(T, LANES), _rows(T, LANES), _full((8, LANES)), _rows(T, 256), _rows(T, 128),
                  _rows(T, D), _rows(T, D), _full((1, D)), _full(w_in.shape), _full((1, 256)), _full(w_q.shape),
                  _full((1, 128)), _full(w_kv.shape)],
        out_specs=[_rows(T, D), _rows(T, 3072), _full((1, D)), _full((1, 256)), _full((1, 128)), _full(w_q.shape),
                   _full(w_kv.shape)],
        out_shape=[_sds((S, D), jnp.float32), _sds((S, 3072), MXU), _sds((1, D), jnp.float32), _sds((1, 256), jnp.float32),
                   _sds((1, 128), jnp.float32), _sds(w_q.shape, jnp.float32), _sds(w_kv.shape, jnp.float32)],
        scratch_shapes=[pltpu.VMEM((T, 1024), MXU), pltpu.VMEM((T, 1536), MXU)],
    )(dqm, dkm, dvm, dqs, dkd, dvd, dgate0, cos, sin, consts, cq, ckv, x, dx1, g_in, w_in, g_q, w_q, g_kv, w_kv)


def _wgrad(a, b, name, plan=None):
    S, M = a.shape
    N = b.shape[1]
    tm = next(t for t in range(WG_ROWS, 0, -LANES) if M % t == 0)
    tn = N if N <= 1024 else 512
    tk = min(WG_TOK, S)

    def body(a_ref, b_ref, o_ref):
        @pl.when(pl.program_id(2) == 0)
        def _():
            o_ref[...] = jnp.zeros_like(o_ref)

        o_ref[...] += _mm_tn(a_ref[...], b_ref[...])

    (dw,), rode = _pcall_riding(
        body, plan, [a, b], name=name, grid=(M // tm, N // tn, S // tk),
        in_specs=[pl.BlockSpec((tk, tm), lambda m, n, k: (k, m)), pl.BlockSpec((tk, tn), lambda m, n, k: (k, n))],
        out_specs=[pl.BlockSpec((tm, tn), lambda m, n, k: (m, n))],
        out_shape=[_sds((M, N), jnp.float32)], scratch_shapes=[])
    return dw, rode


def _adamw(w, g, m, v, name):
    shape = w.shape
    R, C = (int(np.prod(shape[:-1])), shape[-1])
    w2, g2, m2, v2 = (t.reshape(R, C) for t in (w, g, m, v))
    fits = [t for t in range(8, ADAM_TILE_BYTES // (4 * C) + 1, 8) if R % t == 0]
    tr = max(fits) if fits else R
    tc = C if (tr * C * 4 <= ADAM_TILE_BYTES or C % 256) else 256

    def body(w_ref, g_ref, m_ref, v_ref, d_ref, nm_ref, nv_ref):
        gg = g_ref[...]
        nm = B1 * m_ref[...] + (1.0 - B1) * gg
        nv = B2 * v_ref[...] + (1.0 - B2) * (gg * gg)
        m_hat = nm / (1.0 - B1 ** STEP)
        v_hat = nv / (1.0 - B2 ** STEP)
        d_ref[...] = -LR * (m_hat / (jnp.sqrt(v_hat) + AEPS) + WD * w_ref[...])
        nm_ref[...] = nm
        nv_ref[...] = nv

    spec = pl.BlockSpec((tr, tc), lambda i, j: (i, j))
    d, nm, nv = _pcall(
        body, name=name, grid=(R // tr, C // tc), semantics=("parallel", "parallel"),
        in_specs=[spec] * 4, out_specs=[spec] * 3, out_shape=[_sds((R, C), jnp.float32)] * 3,
    )(w2, g2, m2, v2)
    return d.reshape(shape), nm.reshape(shape), nv.reshape(shape)


def _sum_leading(a, name):
    n, R, C = a.shape
    tr = SUM_ROWS if R % SUM_ROWS == 0 else R

    def body(a_ref, o_ref):
        acc = a_ref[0]
        for i in range(1, n):
            acc = acc + a_ref[i]
        o_ref[...] = acc

    return _pcall(
        body, name=name, grid=(R // tr,), semantics=("parallel",),
        in_specs=[pl.BlockSpec((n, tr, C), lambda i: (0, i, 0))], out_specs=_rows(tr, C),
        out_shape=_sds((R, C), a.dtype),
    )(a)


def _add_halves(g, c, b, name, out_dtype):
    n, _, R, C = g.shape
    tr = SUM_ROWS if R % SUM_ROWS == 0 else R

    def body(c_ref, a_ref, b_ref, o_ref):
        o_ref[...] = (a_ref[0] + b_ref[...]).astype(o_ref.dtype)

    spec = pl.BlockSpec((1, tr, C), lambda k, i, c_ref: (k, i, 0))
    grid_spec = pltpu.PrefetchScalarGridSpec(
        num_scalar_prefetch=1, grid=(n, R // tr),
        in_specs=[pl.BlockSpec((1, 1, tr, C), lambda k, i, c_ref: (k, c_ref[0], i, 0)), spec], out_specs=spec)
    return _pcall(body, name=name, semantics=("parallel", "parallel"), grid_spec=grid_spec,
                  out_shape=_sds(b.shape, out_dtype))(c.reshape(1).astype(jnp.int32), g, b)


def _total_sum(g, theirs, chip, c, recv, name):
    _, _, R, C = g.shape
    n = recv.shape[0]
    tr = SUM_ROWS if R % SUM_ROWS == 0 else R

    def body(at_ref, a_ref, b_ref, r_ref, o_ref):
        acc = a_ref[0, 0] + b_ref[0]
        for i in range(n):
            acc = acc + r_ref[i].astype(jnp.float32)
        o_ref[...] = acc

    grid_spec = pltpu.PrefetchScalarGridSpec(
        num_scalar_prefetch=1, grid=(R // tr,),
        in_specs=[pl.BlockSpec((1, 1, tr, C), lambda i, at_ref: (at_ref[0], at_ref[1], i, 0)),
                  pl.BlockSpec((1, tr, C), lambda i, at_ref: (at_ref[0], i, 0)),
                  pl.BlockSpec((n, tr, C), lambda i, at_ref: (0, i, 0))],
        out_specs=pl.BlockSpec((tr, C), lambda i, at_ref: (i, 0)))
    return _pcall(body, name=name, semantics=("parallel",), grid_spec=grid_spec,
                  out_shape=_sds((R, C), jnp.float32))(jnp.stack([chip, c]).astype(jnp.int32), g, theirs, recv)


def _place():
    return lax.axis_index("x"), lax.axis_index("y"), lax.axis_index("c")


class _Plan:
    def __init__(self, arrays, out_shape, scratch, start, finish, middle=None):
        self.arrays, self.out_shape, self.scratch = list(arrays), list(out_shape), list(scratch)
        self.start, self.finish, self.middle = start, finish, middle


def _gather8_plan(block):
    R, C = block.shape

    def parts(ins, outs, sems):
        (x_ref,), (out_ref,), (send_sems, recv_sems) = ins, outs, sems
        x, y, c = _place()
        me, sibling = (x, y, c), (x, y, 1 - c)
        chips = [(1 - x, y), (x, 1 - y), (1 - x, 1 - y)]

        def copy(k, blk, to, src=None):
            slot = out_ref.at[4 * blk[0] + 2 * blk[1] + blk[2]]
            return pltpu.make_async_remote_copy(
                src_ref=slot if src is None else src, dst_ref=slot,
                send_sem=send_sems.at[k], recv_sem=recv_sems.at[k], device_id=to, device_id_type=MESH_ID)

        def first():
            return [copy(0, me, sibling, src=x_ref)] + [copy(1 + j, me, (*chip, c), src=x_ref) for j, chip in enumerate(chips)]

        def passed():
            return [copy(4 + j, (*chip, c), sibling) for j, chip in enumerate(chips)]

        def arrivals():
            return [copy(1 + j, (*chip, c), me) for j, chip in enumerate(chips)]

        def late():
            return [copy(0, sibling, me)] + [copy(4 + j, (*chip, 1 - c), me) for j, chip in enumerate(chips)]

        return first, passed, arrivals, late

    def start(ins, outs, sems):
        for cp in parts(ins, outs, sems)[0]():
            cp.start()

    def middle(ins, outs, sems):
        _, passed, arrivals, _ = parts(ins, outs, sems)
        for arrived, forward in zip(arrivals(), passed()):
            arrived.wait_recv()
            forward.start()

    def finish(ins, outs, sems):
        first, passed, _, late = parts(ins, outs, sems)
        for cp in late():
            cp.wait_recv()
        for cp in first() + passed():
            cp.wait_send()

    return _Plan([block], [_sds((8, R, C), block.dtype)], [pltpu.SemaphoreType.DMA((7,)), pltpu.SemaphoreType.DMA((7,))],
                 start, finish, middle)


def _fill_own_slot(gathered, block):
    x, y, c = _place()
    return lax.dynamic_update_index_in_dim(gathered, block, 4 * x + 2 * y + c, 0)


def _started_and_waited(arrays, out_shape, n, copies):
    def start(ins, outs, sems):
        for cp in copies(ins, outs, sems):
            cp.start()

    def finish(ins, outs, sems):
        for cp in copies(ins, outs, sems):
            cp.wait()

    return _Plan(arrays, out_shape, [pltpu.SemaphoreType.DMA((n,)), pltpu.SemaphoreType.DMA((n,))], start, finish)


def _pair_swap_plan(g):
    n = g.shape[0]

    def copies(ins, outs, sems):
        (g_ref,), (out_ref,), (send_sems, recv_sems) = ins, outs, sems
        x, y, c = _place()
        return [pltpu.make_async_remote_copy(src_ref=g_ref.at[k, 1 - c], dst_ref=out_ref.at[k], send_sem=send_sems.at[k],
                                             recv_sem=recv_sems.at[k], device_id=(x, y, 1 - c), device_id_type=MESH_ID)
                for k in range(n)]

    return _started_and_waited([g], [_sds((n,) + g.shape[2:], g.dtype)], n, copies)


def _chip_exchange_plan(p):
    def copies(ins, outs, sems):
        (p_ref,), (out_ref,), (send_sems, recv_sems) = ins, outs, sems
        x, y, c = _place()
        chips = [(1 - x, y), (x, 1 - y), (1 - x, 1 - y)]
        return [pltpu.make_async_remote_copy(
            src_ref=p_ref.at[2 * cx + cy], dst_ref=out_ref.at[j], send_sem=send_sems.at[j],
            recv_sem=recv_sems.at[j], device_id=(cx, cy, c), device_id_type=MESH_ID)
            for j, (cx, cy) in enumerate(chips)]

    return _started_and_waited([p], [_sds((3,) + p.shape[1:], p.dtype)], 3, copies)


def _pair_exchange_plan(t):
    def copies(ins, outs, sems):
        (t_ref,), (out_ref,), (send_sems, recv_sems) = ins, outs, sems
        x, y, c = _place()
        return [pltpu.make_async_remote_copy(src_ref=t_ref, dst_ref=out_ref, send_sem=send_sems.at[0], recv_sem=recv_sems.at[0],
                                             device_id=(x, y, 1 - c), device_id_type=MESH_ID)]

    return _started_and_waited([t], [_sds(t.shape, t.dtype)], 1, copies)


def _both_plans(a, b):
    na, ma, sa = len(a.arrays), len(a.out_shape), len(a.scratch)

    def phase(name):
        fa, fb = getattr(a, name), getattr(b, name)
        if fa is None and fb is None:
            return None

        def run(ins, outs, sems):
            if fa is not None:
                fa(ins[:na], outs[:ma], sems[:sa])
            if fb is not None:
                fb(ins[na:], outs[ma:], sems[sa:])
        return run

    return _Plan(a.arrays + b.arrays, a.out_shape + b.out_shape, a.scratch + b.scratch,
                 phase("start"), phase("finish"), phase("middle"))


ANY_SPEC = pl.BlockSpec(memory_space=pl.ANY)


def _run_plan(plan, name):
    n_in, n_out = len(plan.arrays), len(plan.out_shape)

    def body(*refs):
        ins, outs, sems = refs[:n_in], refs[n_in:n_in + n_out], refs[n_in + n_out:]
        plan.start(ins, outs, sems)
        if plan.middle is not None:
            plan.middle(ins, outs, sems)
        plan.finish(ins, outs, sems)

    return _pcall(body, name=name, in_specs=[ANY_SPEC] * n_in, out_specs=[ANY_SPEC] * n_out, out_shape=plan.out_shape,
                  scratch_shapes=plan.scratch)(*plan.arrays)


def _pcall_riding(body, plan, args, *, name, grid, in_specs, out_specs, out_shape, scratch_shapes):
    order = ("arbitrary",) * len(grid)
    if plan is None:
        outs = _pcall(body, name=name, grid=grid, semantics=order, in_specs=in_specs, out_specs=out_specs,
                      out_shape=out_shape, scratch_shapes=scratch_shapes)(*args)
        return list(outs), None
    n_in, n_out, n_s = len(args), len(out_shape), len(scratch_shapes)
    p_in, p_out = len(plan.arrays), len(plan.out_shape)
    steps = math.prod(grid)

    def riding(*refs):
        ins, pins = refs[:n_in], refs[n_in:n_in + p_in]
        o0 = n_in + p_in
        outs, pouts = refs[o0:o0 + n_out], refs[o0 + n_out:o0 + n_out + p_out]
        s0 = o0 + n_out + p_out
        scr, sems = refs[s0:s0 + n_s], refs[s0 + n_s:]
        j = pl.program_id(0)
        for axis in range(1, len(grid)):
            j = j * grid[axis] + pl.program_id(axis)

        @pl.when(j == 0)
        def _():
            plan.start(pins, pouts, sems)

        if plan.middle is not None:
            @pl.when(j == steps // 2)
            def _():
                plan.middle(pins, pouts, sems)

        body(*ins, *outs, *scr)

        @pl.when(j == steps - 1)
        def _():
            plan.finish(pins, pouts, sems)

    res = _pcall(riding, name=name, grid=grid, semantics=order, in_specs=list(in_specs) + [ANY_SPEC] * p_in,
                 out_specs=list(out_specs) + [ANY_SPEC] * p_out, out_shape=list(out_shape) + plan.out_shape,
                 scratch_shapes=list(scratch_shapes) + plan.scratch)(*args, *plan.arrays)
    return list(res[:n_out]), list(res[n_out:])


class _RowSeq:
    def __init__(self, pieces):
        self.pieces = list(pieces)

    def rows(self, a, b):
        out, off = [], 0
        for p in self.pieces:
            lo, hi = max(a, off), min(b, off + p.shape[0])
            if lo < hi:
                out.append(p[lo - off:hi - off])
            off += p.shape[0]
        return out

    def array(self):
        return jnp.concatenate(self.pieces, axis=0)


def _row_seq(w):
    return w if isinstance(w, _RowSeq) else _RowSeq([w])


def _prep_w_in0(wt):
    wt = _row_seq(wt)
    one = wt.pieces[0]
    z32 = [jnp.zeros((32, one.shape[1]), one.dtype)]
    k0, k1 = wt.rows(928, 992), wt.rows(992, 1056)
    v0, v1 = wt.rows(1056, 1120), wt.rows(1120, 1184)
    return jnp.concatenate(wt.rows(0, 384) + z32 + z32 + wt.rows(384, 416) + z32 + wt.rows(416, 928)
                           + k0 * 4 + k1 * 4 + v0 * 4 + v1 * 4 + wt.rows(1184, 2208), axis=0)


def _fold_w_in0(d):
    def fold(blk):
        b = blk.reshape(8, 64, blk.shape[1])
        return jnp.concatenate([b[0] + b[1] + b[2] + b[3], b[4] + b[5] + b[6] + b[7]], axis=0)
    return _RowSeq([d[0:384], d[448:480], d[512:1024], fold(d[1024:1536]), fold(d[1536:2048]), d[2048:3072]])


def _prep_w_q(wt):
    return jnp.pad(wt.reshape(N_MLA, 96, Q_RANK), ((0, 0), (0, 32), (0, 0))).reshape(1024, Q_RANK)


def _fold_w_q(d):
    return d.reshape(N_MLA, 128, Q_RANK)[:, :96].reshape(768, Q_RANK)


def _prep_w_kv(w):
    w3 = w.reshape(KV_RANK, N_MLA, 128)
    kk = jnp.pad(w3[:, :, :64], ((0, 0), (0, 0), (0, 64))).reshape(KV_RANK, 1024)
    return jnp.concatenate([kk, w3[:, :, 64:].reshape(KV_RANK, 512)], axis=1)


def _fold_w_kv(d):
    kk = d[:, :1024].reshape(KV_RANK, N_MLA, 128)[:, :, :64]
    vv = d[:, 1024:].reshape(KV_RANK, N_MLA, 64)
    return jnp.concatenate([kk, vv], axis=2).reshape(KV_RANK, 1024)


W_IN1_SHARD = 1028
W_IN1_STEP = W_IN1_SHARD % 16


class _ShiftedShards:
    def __init__(self, blocks):
        self.blocks = list(blocks)


def _shifted_shard(a, chip, rows):
    out = jnp.zeros((rows, a.shape[1]), a.dtype)
    for k in range(4):
        out = jnp.where(chip == k, jnp.pad(a, ((W_IN1_STEP * k, rows - W_IN1_STEP * k - a.shape[0]), (0, 0))), out)
    return out


def _prep_w_in1(wt):
    if not isinstance(wt, _ShiftedShards):
        return jnp.concatenate([wt[0:3072], wt[3088:4112], wt[3072:3088], jnp.zeros((112, wt.shape[1]), wt.dtype)], axis=0)
    b = wt.blocks
    row = lax.broadcasted_iota(jnp.int32, (16, 1), 0)

    def seam(k, first, second):
        return jnp.where(row < W_IN1_STEP * (k + 1), first, second)

    return jnp.concatenate([
        b[0][0:1024], seam(0, b[0][1024:1040], b[1][0:16]), b[1][16:1024], seam(1, b[1][1024:1040], b[2][0:16]),
        b[2][16:1024], b[3][16:1040], seam(2, b[2][1024:1040], b[3][0:16]), jnp.zeros((112, 1024), b[0].dtype)], axis=0)


def _fold_w_in1(d):
    return _RowSeq([d[0:3072], d[4096:4112], d[3072:4096]])


class _Alone:
    def __init__(self, w_out0, o_g_in, w_in1, w_out1):
        self.layer1 = (w_out0, o_g_in, w_in1, w_out1)

    def gather_plan(self):
        return None

    def layer1_weights(self, rode):
        return self.layer1

    def swap_plan(self, grads1):
        return None

    def exchange_plan(self, rode):
        return None

    def halves_plan(self, rode):
        return None

    def finish(self, rode):
        pass


def _local_step(x, pos, target, e_g_in, w_in0, e_g_q, w_q, e_g_kv, w_kv, sinks, b_f, g_final, layer1):
    S = x.shape[0]
    w_in0p, w_qp, w_kvp = _prep_w_in0(w_in0), _prep_w_q(w_q), _prep_w_kv(w_kv)
    slopes = jnp.asarray(2.0 ** (-8.0 * (np.arange(N_SWA, dtype=np.float32) + 1.0) / N_SWA), jnp.float32)
    sinks1 = sinks.reshape(N_SWA)
    b_col = b_f.reshape(N_FOX, 1)

    (h0, cq, ckv, qm, km, vm, qs, kd, vd, gate0, cos, sin) = _layer0_in(
        x, pos, e_g_in, w_in0p, e_g_q, w_qp, e_g_kv, w_kvp)
    o_m, lse_m, rode = _attn_fwd_t(qm, km, vm, (NOPE + ROPE) ** -0.5, split=True, name="mla_fwd", plan=layer1.gather_plan())
    w_out0, o_g_in, w_in1, w_out1 = layer1.layer1_weights(rode)
    w_in1p = _prep_w_in1(w_in1)
    o_s, lse_s = _swa_fwd(qs, kd, vd, sinks1, slopes)
    x1, h1, q1, k1, v1, gate1, f_slab = _layer0_out_layer1_in(x, o_m, o_s, gate0, w_out0, o_g_in, w_in1p)
    f_row = f_slab[:, :N_FOX].T
    lc_row = _forget_fwd(f_row, b_col)
    lcc = lc_row.T
    o1, lse1, _ = _attn_fwd_t(q1, k1, v1, HEAD ** -0.5, split=False, name="fox_fwd", lcc=lcc)
    loss8, dg_final, dw_out1, dx2, do1, dgate1 = _head(x1, o1, gate1, w_out1, g_final, target)

    dq1, dk1, dv1, dlc, _ = _attn_bwd_t(q1, k1, v1, do1, o1, lse1, HEAD ** -0.5, split=False, name="fox_bwd", lcc=lcc)
    df_row, db_f = _forget_bwd(dlc.reshape(N_FOX, S), f_row, b_col)
    df_slab = jnp.pad(df_row.T, ((0, 0), (0, LANES - N_FOX))).astype(MXU)
    dz1, dx1, dg_o_in, dw_out0, do_m, do_s, dgate0 = _layer1_in_bwd(
        dq1, dk1, dv1, dgate1, df_slab, x1, dx2, o_g_in, w_in1p, gate0, o_m, o_s, w_out0)
    grads1 = dict(o_g_in=dg_o_in, o_w_in=_fold_w_in1(_wgrad(dz1, h1, "wgrad_in1")[0]), o_w_out=dw_out1,
                  e_w_out=dw_out0)
    dqs, dkd, dvd, dsink, rode = _swa_bwd(qs, kd, vd, do_s, o_s, lse_s, sinks1, slopes, plan=layer1.swap_plan(grads1))
    dqm, dkm, dvm, rode = _attn_bwd_t(qm, km, vm, do_m, o_m, lse_m, (NOPE + ROPE) ** -0.5, split=True, name="mla_bwd",
                                      plan=layer1.exchange_plan(rode))
    halves_plan = layer1.halves_plan(rode)
    dx, dz0, dg_in, dg_q, dg_kv, dw_q, dw_kv = _layer0_in_bwd(
        dqm, dkm, dvm, dqs, dkd, dvd, dgate0, cos, sin, cq, ckv, x, dx1, e_g_in, w_in0p, e_g_q, w_qp, e_g_kv, w_kvp)
    dw_in0, rode = _wgrad(dz0, h0, "wgrad_in0", plan=halves_plan)
    layer1.finish(rode)

    grads = dict(
        e_g_in=dg_in,
        e_w_in=_fold_w_in0(dw_in0),
        e_g_q_a=dg_q,
        e_w_q_up=_fold_w_q(dw_q),
        e_g_kv_a=dg_kv,
        e_w_kv_up=_fold_w_kv(dw_kv),
        e_sinks=dsink[:, 0:4, 0].reshape(1, N_SWA),
        o_b_f=db_f.reshape(1, N_FOX),
        g_final=dg_final,
        **grads1,
    )
    return loss8[0, 0], dx, grads


SHARDED = ("e_w_in", "e_w_q_up", "e_w_kv_up", "e_w_out", "o_g_in", "o_w_in", "o_w_out")
TRANSPOSED = ("e_w_in", "e_w_q_up", "o_w_in")
COL_SHARDED = ("e_w_kv_up", "o_g_in")
REPLICATED = ("e_g_in", "e_g_q_a", "e_g_kv_a", "e_sinks", "o_b_f", "g_final")
FULL_SHAPES = dict(e_w_in=(2208, 1024), e_w_q_up=(768, 256), e_w_kv_up=(128, 1024), e_w_out=(1024, 1024),
                   o_g_in=(1, 1024), o_w_in=(4112, 1024), o_w_out=(1024, 1024))
GROUPS = dict(
    layer0=dict(rows=768, windows=dict(e_w_in=(0, 0), e_w_q_up=(560, 0), e_w_kv_up=(560, 256))),
    layer1=dict(rows=1568, windows=dict(o_w_in=(0, 0), o_w_out=(1040, 0), e_w_out=(1296, 0), o_g_in=(1552, 0))),
)


def _shard_shape(name):
    r, c = FULL_SHAPES[name]
    return (r, c // 4) if name in COL_SHARDED else (r // 4, c)


def _as_handled(name, a):
    a = a[0] if a.ndim == 3 else a
    return a.T if name in TRANSPOSED else a


def _as_given(name, a, shape):
    return (a.T if name in TRANSPOSED else a).reshape(shape)


def _pack_block(p, group, shifted_for=None):
    def rows(a, n):
        return jnp.pad(a, ((0, n - a.shape[0]), (0, 0)))

    if group == "layer0":
        band = jnp.concatenate([p["e_w_q_up"], rows(p["e_w_kv_up"], 192), jnp.zeros((192, 512), p["e_w_in"].dtype)], axis=1)
        return jnp.concatenate([rows(p["e_w_in"], 560), rows(band, 208)], axis=0)
    g = p["o_g_in"]
    band = jnp.pad(g, ((0, 16 - g.shape[0]), (0, PACK_COLS - g.shape[1])))
    w_in = rows(p["o_w_in"], 1040) if shifted_for is None else _shifted_shard(p["o_w_in"], shifted_for, 1040)
    return jnp.concatenate([w_in, p["o_w_out"], p["e_w_out"], band], axis=0)


def _window(block, group, name, width=None):
    r0, c0 = GROUPS[group]["windows"][name]
    r, c = _shard_shape(name)
    return block[..., r0:r0 + r, c0:c0 + (c if width is None else width)]


def _chip_slice(name, full, k):
    r, c = _shard_shape(name)
    if isinstance(full, _RowSeq):
        return jnp.concatenate(full.rows(r * k, r * (k + 1)), axis=0)
    return full[:, c * k:c * (k + 1)] if name in COL_SHARDED else full[r * k:r * (k + 1), :]


def _packed_weights(w, group):
    parts = {}
    for n in GROUPS[group]["windows"]:
        a = _as_handled(n, w[n])
        parts[n] = lax.bitcast_convert_type(a, jnp.bfloat16).reshape(1, -1) if n == "o_g_in" else a.astype(jnp.bfloat16)
    x, y, _ = _place()
    halves = _pack_block(parts, group, shifted_for=2 * x + y).reshape(2, GROUPS[group]["rows"] // 2, PACK_COLS)
    return lax.dynamic_index_in_dim(halves, lax.axis_index("c"), 0, keepdims=False)


def _unpacked_weights(gathered, half, group):
    blocks = _fill_own_slot(gathered, half).reshape(4, GROUPS[group]["rows"], PACK_COLS)
    full = {}
    for n in GROUPS[group]["windows"]:
        if n == "o_g_in":
            halves = _window(blocks, group, n, width=512).reshape(4, 1, 256, 2)
            full[n] = jnp.concatenate(list(lax.bitcast_convert_type(halves, jnp.float32)), axis=1)
        elif n == "o_w_in":
            full[n] = _ShiftedShards(blocks[k, 0:1040].astype(MXU) for k in range(4))
        else:
            pieces = [_window(blocks[k], group, n).astype(MXU) for k in range(4)]
            if n == "e_w_in":
                full[n] = _RowSeq(pieces)
            else:
                full[n] = jnp.concatenate(pieces, axis=1 if n in COL_SHARDED else 0)
    return full


class _GroupReduce:
    def __init__(self, group):
        self.group = group
        self.c = lax.axis_index("c")
        self.chip = 2 * lax.axis_index("x") + lax.axis_index("y")

    def swap_plan(self, grads):
        names = GROUPS[self.group]["windows"]
        per_chip = jnp.concatenate([_pack_block({n: _chip_slice(n, grads[n], k) for n in names}, self.group)
                                    for k in range(4)], axis=0)
        self.g4 = per_chip.reshape(4, 2, GROUPS[self.group]["rows"] // 2, PACK_COLS)
        return _pair_swap_plan(self.g4)

    def exchange_plan(self, rode):
        self.theirs = rode[0]
        return _chip_exchange_plan(_add_halves(self.g4, self.c, self.theirs, "pair_add_" + self.group, jnp.bfloat16))

    def halves_plan(self, rode):
        self.my_half = _total_sum(self.g4, self.theirs, self.chip, self.c, rode[0], "chip_sum_" + self.group)
        return _pair_exchange_plan(self.my_half)

    def finish(self, rode):
        my_half, other_half = self.my_half, rode[0]
        total = jnp.concatenate([jnp.where(self.c == 0, my_half, other_half), jnp.where(self.c == 0, other_half, my_half)], axis=0)
        self.sums = {n: _window(total, self.group, n) for n in GROUPS[self.group]["windows"]}

    def run(self, grads, beside):
        swap = self.swap_plan(grads)
        outs = _run_plan(_both_plans(swap, beside), "pair_swap_" + self.group)
        rode, others = outs[:len(swap.out_shape)], outs[len(swap.out_shape):]
        halves = self.halves_plan(_run_plan(self.exchange_plan(rode), "chip_exchange_" + self.group))
        self.finish(_run_plan(halves, "pair_exchange_" + self.group))
        return self.sums, others


class _Layer1Exchange(_GroupReduce):
    def __init__(self, w):
        super().__init__("layer1")
        self.half = _packed_weights(w, "layer1")

    def gather_plan(self):
        return _gather8_plan(self.half)

    def layer1_weights(self, rode):
        full = _unpacked_weights(rode[0], self.half, "layer1")
        return full["e_w_out"], full["o_g_in"], full["o_w_in"], full["o_w_out"]


def kernel(x, positions, e_g_in, e_w_in, e_g_q_a, e_w_q_up, e_g_kv_a, e_w_kv_up, e_sinks, e_w_out, o_g_in, o_w_in, o_b_f, o_w_out, g_final, loss_target, m_e_g_in, m_e_w_in, m_e_g_q_a, m_e_w_q_up, m_e_g_kv_a, m_e_w_kv_up, m_e_sinks, m_e_w_out, m_o_g_in, m_o_w_in, m_o_b_f, m_o_w_out, m_g_final, v_e_g_in, v_e_w_in, v_e_g_q_a, v_e_w_q_up, v_e_g_kv_a, v_e_w_kv_up, v_e_sinks, v_e_w_out, v_o_g_in, v_o_w_in, v_o_b_f, v_o_w_out, v_g_final):
    w = dict(e_g_in=e_g_in, e_w_in=e_w_in, e_g_q_a=e_g_q_a, e_w_q_up=e_w_q_up, e_g_kv_a=e_g_kv_a, e_w_kv_up=e_w_kv_up,
             e_sinks=e_sinks, e_w_out=e_w_out, o_g_in=o_g_in, o_w_in=o_w_in, o_b_f=o_b_f, o_w_out=o_w_out, g_final=g_final)
    m = dict(e_g_in=m_e_g_in, e_w_in=m_e_w_in, e_g_q_a=m_e_g_q_a, e_w_q_up=m_e_w_q_up, e_g_kv_a=m_e_g_kv_a,
             e_w_kv_up=m_e_w_kv_up, e_sinks=m_e_sinks, e_w_out=m_e_w_out, o_g_in=m_o_g_in, o_w_in=m_o_w_in, o_b_f=m_o_b_f,
             o_w_out=m_o_w_out, g_final=m_g_final)
    v = dict(e_g_in=v_e_g_in, e_w_in=v_e_w_in, e_g_q_a=v_e_g_q_a, e_w_q_up=v_e_w_q_up, e_g_kv_a=v_e_g_kv_a,
             e_w_kv_up=v_e_w_kv_up, e_sinks=v_e_sinks, e_w_out=v_e_w_out, o_g_in=v_o_g_in, o_w_in=v_o_w_in, o_b_f=v_o_b_f,
             o_w_out=v_o_w_out, g_final=v_g_final)
    order = ("e_g_in", "e_w_in", "e_g_q_a", "e_w_q_up", "e_g_kv_a", "e_w_kv_up", "e_sinks", "e_w_out", "o_g_in", "o_w_in",
             "o_b_f", "o_w_out", "g_final")
    half0 = _packed_weights(w, "layer0")
    full = _unpacked_weights(_run_plan(_gather8_plan(half0), "gather_weights_layer0")[0], half0, "layer0")
    layer1 = _Layer1Exchange(w)

    loss_part, dx, grads = _local_step(
        x[0], positions.reshape(-1, 1), loss_target[0], e_g_in, full["e_w_in"], e_g_q_a, full["e_w_q_up"], e_g_kv_a,
        full["e_w_kv_up"], e_sinks, o_b_f, g_final.reshape(1, D), layer1)

    small = jnp.concatenate([jnp.pad(loss_part.reshape(1), (0, LANES - 1))]
                            + [jnp.pad(grads[n].reshape(-1), (0, (-grads[n].size) % LANES)) for n in REPLICATED])
    rows = small.shape[0] // LANES
    small = jnp.pad(small.reshape(rows, LANES), ((0, (-rows) % 8), (0, 0)))
    sums0, (gathered_small,) = _GroupReduce("layer0").run(grads, _gather8_plan(small))
    gsum = {**layer1.sums, **sums0}
    ssum = _sum_leading(_fill_own_slot(gathered_small, small), "small_grad_sum").reshape(-1)
    loss = ssum[0]
    off = LANES
    for n in REPLICATED:
        cnt = w[n].size
        gsum[n] = ssum[off:off + cnt].reshape(w[n].shape)
        off += cnt + (-cnt) % LANES

    grad, delta, new_m, new_v = {}, {}, {}, {}
    for n in order:
        if n == "o_w_in":
            def tiles(a):
                return jnp.transpose(a, (2, 0, 1)).reshape(-1, LANES)

            def given(a):
                return jnp.transpose(a.reshape(-1, 8, LANES), (1, 2, 0)).reshape(w[n].shape)

            g_t = gsum[n].reshape(-1, LANES)
            outs = _adamw(tiles(w[n]), g_t, tiles(m[n]), tiles(v[n]), "adamw_" + n)
            grad[n], delta[n], new_m[n], new_v[n] = (given(a) for a in (g_t,) + outs)
        elif n in SHARDED:
            outs = _adamw(_as_handled(n, w[n]), gsum[n], _as_handled(n, m[n]), _as_handled(n, v[n]), "adamw_" + n)
            grad[n], delta[n], new_m[n], new_v[n] = (_as_given(n, a, w[n].shape) for a in (gsum[n],) + outs)
        else:
            grad[n] = gsum[n]
            delta[n], new_m[n], new_v[n] = _adamw(w[n], gsum[n], m[n], v[n], "adamw_" + n)
    return (loss, dx[None], *[grad[n] for n in order], *[delta[n] for n in order], *[new_m[n] for n in order],
            *[new_v[n] for n in order])
```
